```python
import math
import jax, jax.numpy as jnp
from jax import lax
import numpy as np

D_MODEL = 2048
BATCH = 8
SEQ = 2048
DEPTH = 1

D_MIX = D_MODEL
HEAD_DIM = 64
D_ATTN = D_MIX // 2
N_Q_HEADS = D_ATTN // HEAD_DIM
N_KV_HEADS = 4
Q_PER_KV = N_Q_HEADS // N_KV_HEADS
D_KV = N_KV_HEADS * HEAD_DIM
WINDOW = 128
BLOCK = 128
ROPE_THETA = 10000.0
D_SSM = D_MIX - D_ATTN
SSM_GROUP = 16
N_SSM_GROUPS = D_SSM // SSM_GROUP
SSM_STATE = 64
D_IN = D_ATTN + 2 * D_KV + D_SSM
D_FF = ((8 * D_MODEL // 3 + 255) // 256) * 256
RMS_EPS = 1e-6

kernel_name = 'hymba_swa_sink_s5_sandwich_block'


def rms_norm(x, g):
    xf = x.astype(jnp.float32)
    y = xf * lax.rsqrt(jnp.mean(xf * xf, axis=-1, keepdims=True) + RMS_EPS)
    return (y * g.astype(jnp.float32)).astype(x.dtype)


def rotary(t, positions):
    half = HEAD_DIM // 2
    inv_freq = ROPE_THETA ** (-jnp.arange(half, dtype=jnp.float32) / half)
    ang = positions.astype(jnp.float32)[:, :, None] * inv_freq
    cos = jnp.cos(ang)[:, :, None, :]
    sin = jnp.sin(ang)[:, :, None, :]
    tf = t.astype(jnp.float32)
    t1, t2 = tf[..., :half], tf[..., half:]
    return jnp.concatenate([t1 * cos - t2 * sin, t2 * cos + t1 * sin], axis=-1).astype(t.dtype)


def sliding_window_attention(q, k, v, sinks):
    B, L = q.shape[0], q.shape[1]
    nb = L // BLOCK
    qb = q.reshape(B, nb, BLOCK, N_KV_HEADS, Q_PER_KV, HEAD_DIM)
    kb = k.reshape(B, nb, BLOCK, N_KV_HEADS, HEAD_DIM)
    vb = v.reshape(B, nb, BLOCK, N_KV_HEADS, HEAD_DIM)
    pad = ((0, 0), (1, 0), (0, 0), (0, 0), (0, 0))
    kk = jnp.concatenate([jnp.pad(kb, pad)[:, :-1], kb], axis=2)
    vv = jnp.concatenate([jnp.pad(vb, pad)[:, :-1], vb], axis=2)
    scale = 1.0 / math.sqrt(HEAD_DIM)
    scores = jnp.einsum('bnqkgd,bnskd->bnkgqs', qb, kk).astype(jnp.float32) * scale
    blk = jnp.arange(nb, dtype=jnp.int32)[:, None] * BLOCK
    q_pos = blk + jnp.arange(BLOCK, dtype=jnp.int32)[None, :]
    k_pos = blk - BLOCK + jnp.arange(2 * BLOCK, dtype=jnp.int32)[None, :]
    diff = q_pos[:, :, None] - k_pos[:, None, :]
    mask = (diff >= 0) & (diff < WINDOW) & (k_pos[:, None, :] >= 0)
    scores = jnp.where(mask[None, :, None, None], scores, -jnp.inf)
    sink = sinks.astype(jnp.float32).reshape(N_KV_HEADS, Q_PER_KV)[None, None, :, :, None, None]
    m = jnp.maximum(jnp.max(scores, axis=-1, keepdims=True), sink)
    p = jnp.exp(scores - m)
    probs = p / (jnp.sum(p, axis=-1, keepdims=True) + jnp.exp(sink - m))
    out = jnp.einsum('bnkgqs,bnskd->bnqkgd', probs.astype(v.dtype), vv)
    return out.reshape(B, L, N_Q_HEADS * HEAD_DIM)


def s5_ssm(u, a_re, a_im, log_dt, b_re, b_im, c_re, c_im, d_skip):
    L = u.shape[1]
    uf = u.astype(jnp.float32)
    dt = jnp.exp(log_dt.astype(jnp.float32))[:, None]
    ar = a_re.astype(jnp.float32)
    ai = a_im.astype(jnp.float32)
    mag = jnp.exp(ar * dt)
    lam_re = mag * jnp.cos(ai * dt)
    lam_im = mag * jnp.sin(ai * dt)
    den = ar * ar + ai * ai
    nr = lam_re - 1.0
    ni = lam_im
    f_re = (nr * ar + ni * ai) / den
    f_im = (ni * ar - nr * ai) / den
    br = b_re.astype(jnp.float32)
    bi = b_im.astype(jnp.float32)
    bbar_re = f_re[..., None] * br - f_im[..., None] * bi
    bbar_im = f_re[..., None] * bi + f_im[..., None] * br
    bu_re = jnp.einsum('blgp,gnp->blgn', uf, bbar_re)
    bu_im = jnp.einsum('blgp,gnp->blgn', uf, bbar_im)
    shp = (1, L) + lam_re.shape
    a_seq_re = jnp.broadcast_to(lam_re[None, None], shp)
    a_seq_im = jnp.broadcast_to(lam_im[None, None], shp)

    def combine(earlier, later):
        a1r, a1i, b1r, b1i = earlier
        a2r, a2i, b2r, b2i = later
        return (a2r * a1r - a2i * a1i,
                a2r * a1i + a2i * a1r,
                a2r * b1r - a2i * b1i + b2r,
                a2r * b1i + a2i * b1r + b2i)

    _, _, s_re, s_im = lax.associative_scan(combine, (a_seq_re, a_seq_im, bu_re, bu_im), axis=1)
    y = (jnp.einsum('blgn,gpn->blgp', s_re, c_re.astype(jnp.float32))
         - jnp.einsum('blgn,gpn->blgp', s_im, c_im.astype(jnp.float32))
         + d_skip.astype(jnp.float32) * uf)
    return y.astype(u.dtype)


def hybrid_mixer(xn, positions, w_in, sinks, a_re, a_im, log_dt, b_re, b_im, c_re, c_im,
                 d_skip, w_glu, b_glu, g_attn_out, g_ssm_out, w_o):
    B, L = xn.shape[0], xn.shape[1]
    proj = jnp.einsum('bld,de->ble', xn, w_in)
    q, k, v, u = jnp.split(proj, [D_ATTN, D_ATTN + D_KV, D_ATTN + 2 * D_KV], axis=-1)
    q = rotary(q.reshape(B, L, N_Q_HEADS, HEAD_DIM), positions)
    k = rotary(k.reshape(B, L, N_KV_HEADS, HEAD_DIM), positions)
    v = v.reshape(B, L, N_KV_HEADS, HEAD_DIM)
    attn = sliding_window_attention(q, k, v, sinks)
    y = s5_ssm(u.reshape(B, L, N_SSM_GROUPS, SSM_GROUP), a_re, a_im, log_dt,
               b_re, b_im, c_re, c_im, d_skip).reshape(B, L, D_SSM)
    z = jax.nn.gelu(y)
    ssm = z * jax.nn.sigmoid(jnp.einsum('blc,ce->ble', z, w_glu) + b_glu)
    mixed = jnp.concatenate([rms_norm(attn, g_attn_out), rms_norm(ssm, g_ssm_out)], axis=-1)
    return jnp.einsum('blc,cd->bld', mixed, w_o)


def swiglu(xn, w_gate, w_up, w_down):
    hid = jax.nn.silu(jnp.einsum('bld,df->blf', xn, w_gate)) * jnp.einsum('bld,df->blf', xn, w_up)
    return jnp.einsum('blf,fd->bld', hid, w_down)


def _fwd_setup_inputs(seed: int = 0) -> dict:
    key = jax.random.key(seed)
    ks = jax.random.split(key, 24)
    f32 = jnp.float32

    def nrm(k, shape, scale):
        return jax.random.normal(k, shape, f32) * scale

    def gain(k, width):
        return 1.0 + nrm(k, (DEPTH, width), 0.05)

    G, N, P = N_SSM_GROUPS, SSM_STATE, SSM_GROUP
    x = nrm(ks[0], (BATCH, SEQ, D_MODEL), 1.0)
    positions = jnp.tile(jnp.arange(SEQ, dtype=jnp.int32)[None, :], (BATCH, 1))
    return {
        'x': x,
        'positions': positions,
        'g_pre_mix': gain(ks[1], D_MODEL),
        'w_in': nrm(ks[2], (DEPTH, D_MODEL, D_IN), D_MODEL ** -0.5),
        'sinks': nrm(ks[3], (DEPTH, N_Q_HEADS), 1.0),
        'a_re': -0.5 + nrm(ks[4], (DEPTH, G, N), 0.01),
        'a_im': math.pi * jnp.arange(N, dtype=f32)[None, None, :] + nrm(ks[5], (DEPTH, G, N), 0.01),
        'log_dt': jax.random.uniform(ks[6], (DEPTH, G), f32, math.log(1e-3), math.log(1e-1)),
        'b_re': nrm(ks[7], (DEPTH, G, N, P), (2 * P) ** -0.5),
        'b_im': nrm(ks[8], (DEPTH, G, N, P), (2 * P) ** -0.5),
        'c_re': nrm(ks[9], (DEPTH, G, P, N), (2 * N) ** -0.5),
        'c_im': nrm(ks[10], (DEPTH, G, P, N), (2 * N) ** -0.5),
        'd_skip': nrm(ks[11], (DEPTH, G, P), 1.0),
        'w_glu': nrm(ks[12], (DEPTH, D_SSM, D_SSM), D_SSM ** -0.5),
        'b_glu': nrm(ks[13], (DEPTH, D_SSM), 0.01),
        'g_attn_out': gain(ks[14], D_ATTN),
        'g_ssm_out': gain(ks[15], D_SSM),
        'w_o': nrm(ks[16], (DEPTH, D_MIX, D_MODEL), D_MIX ** -0.5),
        'g_post_mix': gain(ks[17], D_MODEL),
        'g_pre_ffn': gain(ks[18], D_MODEL),
        'w_gate': nrm(ks[19], (DEPTH, D_MODEL, D_FF), D_MODEL ** -0.5),
        'w_up': nrm(ks[20], (DEPTH, D_MODEL, D_FF), D_MODEL ** -0.5),
        'w_down': nrm(ks[21], (DEPTH, D_FF, D_MODEL), D_FF ** -0.5),
        'g_post_ffn': gain(ks[22], D_MODEL),
    }


def _fwd_reference(x, positions, g_pre_mix, w_in, sinks, a_re, a_im, log_dt, b_re, b_im, c_re, c_im,
              d_skip, w_glu, b_glu, g_attn_out, g_ssm_out, w_o, g_post_mix, g_pre_ffn,
              w_gate, w_up, w_down, g_post_ffn):
    h = x
    for i in range(DEPTH):
        mix = hybrid_mixer(rms_norm(h, g_pre_mix[i]), positions, w_in[i], sinks[i], a_re[i], a_im[i],
                           log_dt[i], b_re[i], b_im[i], c_re[i], c_im[i], d_skip[i], w_glu[i],
                           b_glu[i], g_attn_out[i], g_ssm_out[i], w_o[i])
        h = h + rms_norm(mix, g_post_mix[i])
        ff = swiglu(rms_norm(h, g_pre_ffn[i]), w_gate[i], w_up[i], w_down[i])
        h = h + rms_norm(ff, g_post_ffn[i])
    return h


import jax as _jax
import jax.numpy as _jnp

TWIN_FORMAT = 'train_step'
FWD_PARAMS = ['x', 'positions', 'g_pre_mix', 'w_in', 'sinks', 'a_re', 'a_im', 'log_dt', 'b_re', 'b_im', 'c_re', 'c_im', 'd_skip', 'w_glu', 'b_glu', 'g_attn_out', 'g_ssm_out', 'w_o', 'g_post_mix', 'g_pre_ffn', 'w_gate', 'w_up', 'w_down', 'g_post_ffn']
TWIN_WEIGHTS = ['g_pre_mix', 'w_in', 'sinks', 'a_re', 'a_im', 'log_dt', 'b_re', 'b_im', 'c_re', 'c_im', 'd_skip', 'w_glu', 'b_glu', 'g_attn_out', 'g_ssm_out', 'w_o', 'g_post_mix', 'g_pre_ffn', 'w_gate', 'w_up', 'w_down', 'g_post_ffn']
TWIN_DIFF_INPUT = 'x'
TWIN_INPUTS = ['x', 'positions', 'g_pre_mix', 'w_in', 'sinks', 'a_re', 'a_im', 'log_dt', 'b_re', 'b_im', 'c_re', 'c_im', 'd_skip', 'w_glu', 'b_glu', 'g_attn_out', 'g_ssm_out', 'w_o', 'g_post_mix', 'g_pre_ffn', 'w_gate', 'w_up', 'w_down', 'g_post_ffn', 'loss_target', 'm_g_pre_mix', 'm_w_in', 'm_sinks', 'm_a_re', 'm_a_im', 'm_log_dt', 'm_b_re', 'm_b_im', 'm_c_re', 'm_c_im', 'm_d_skip', 'm_w_glu', 'm_b_glu', 'm_g_attn_out', 'm_g_ssm_out', 'm_w_o', 'm_g_post_mix', 'm_g_pre_ffn', 'm_w_gate', 'm_w_up', 'm_w_down', 'm_g_post_ffn', 'v_g_pre_mix', 'v_w_in', 'v_sinks', 'v_a_re', 'v_a_im', 'v_log_dt', 'v_b_re', 'v_b_im', 'v_c_re', 'v_c_im', 'v_d_skip', 'v_w_glu', 'v_b_glu', 'v_g_attn_out', 'v_g_ssm_out', 'v_w_o', 'v_g_post_mix', 'v_g_pre_ffn', 'v_w_gate', 'v_w_up', 'v_w_down', 'v_g_post_ffn']
TWIN_OUTPUTS = ['loss', 'grad_x', 'grad_g_pre_mix', 'grad_w_in', 'grad_sinks', 'grad_a_re', 'grad_a_im', 'grad_log_dt', 'grad_b_re', 'grad_b_im', 'grad_c_re', 'grad_c_im', 'grad_d_skip', 'grad_w_glu', 'grad_b_glu', 'grad_g_attn_out', 'grad_g_ssm_out', 'grad_w_o', 'grad_g_post_mix', 'grad_g_pre_ffn', 'grad_w_gate', 'grad_w_up', 'grad_w_down', 'grad_g_post_ffn', 'delta_g_pre_mix', 'delta_w_in', 'delta_sinks', 'delta_a_re', 'delta_a_im', 'delta_log_dt', 'delta_b_re', 'delta_b_im', 'delta_c_re', 'delta_c_im', 'delta_d_skip', 'delta_w_glu', 'delta_b_glu', 'delta_g_attn_out', 'delta_g_ssm_out', 'delta_w_o', 'delta_g_post_mix', 'delta_g_pre_ffn', 'delta_w_gate', 'delta_w_up', 'delta_w_down', 'delta_g_post_ffn', 'new_m_g_pre_mix', 'new_m_w_in', 'new_m_sinks', 'new_m_a_re', 'new_m_a_im', 'new_m_log_dt', 'new_m_b_re', 'new_m_b_im', 'new_m_c_re', 'new_m_c_im', 'new_m_d_skip', 'new_m_w_glu', 'new_m_b_glu', 'new_m_g_attn_out', 'new_m_g_ssm_out', 'new_m_w_o', 'new_m_g_post_mix', 'new_m_g_pre_ffn', 'new_m_w_gate', 'new_m_w_up', 'new_m_w_down', 'new_m_g_post_ffn', 'new_v_g_pre_mix', 'new_v_w_in', 'new_v_sinks', 'new_v_a_re', 'new_v_a_im', 'new_v_log_dt', 'new_v_b_re', 'new_v_b_im', 'new_v_c_re', 'new_v_c_im', 'new_v_d_skip', 'new_v_w_glu', 'new_v_b_glu', 'new_v_g_attn_out', 'new_v_g_ssm_out', 'new_v_w_o', 'new_v_g_post_mix', 'new_v_g_pre_ffn', 'new_v_w_gate', 'new_v_w_up', 'new_v_w_down', 'new_v_g_post_ffn']
TWIN_LEAF_KINDS = {'loss': 'loss', 'grad_x': 'grad_x', 'grad_g_pre_mix': 'grad_w', 'grad_w_in': 'grad_w', 'grad_sinks': 'grad_w', 'grad_a_re': 'grad_w', 'grad_a_im': 'grad_w', 'grad_log_dt': 'grad_w', 'grad_b_re': 'grad_w', 'grad_b_im': 'grad_w', 'grad_c_re': 'grad_w', 'grad_c_im': 'grad_w', 'grad_d_skip': 'grad_w', 'grad_w_glu': 'grad_w', 'grad_b_glu': 'grad_w', 'grad_g_attn_out': 'grad_w', 'grad_g_ssm_out': 'grad_w', 'grad_w_o': 'grad_w', 'grad_g_post_mix': 'grad_w', 'grad_g_pre_ffn': 'grad_w', 'grad_w_gate': 'grad_w', 'grad_w_up': 'grad_w', 'grad_w_down': 'grad_w', 'grad_g_post_ffn': 'grad_w', 'delta_g_pre_mix': 'delta_w', 'delta_w_in': 'delta_w', 'delta_sinks': 'delta_w', 'delta_a_re': 'delta_w', 'delta_a_im': 'delta_w', 'delta_log_dt': 'delta_w', 'delta_b_re': 'delta_w', 'delta_b_im': 'delta_w', 'delta_c_re': 'delta_w', 'delta_c_im': 'delta_w', 'delta_d_skip': 'delta_w', 'delta_w_glu': 'delta_w', 'delta_b_glu': 'delta_w', 'delta_g_attn_out': 'delta_w', 'delta_g_ssm_out': 'delta_w', 'delta_w_o': 'delta_w', 'delta_g_post_mix': 'delta_w', 'delta_g_pre_ffn': 'delta_w', 'delta_w_gate': 'delta_w', 'delta_w_up': 'delta_w', 'delta_w_down': 'delta_w', 'delta_g_post_ffn': 'delta_w', 'new_m_g_pre_mix': 'new_m', 'new_m_w_in': 'new_m', 'new_m_sinks': 'new_m', 'new_m_a_re': 'new_m', 'new_m_a_im': 'new_m', 'new_m_log_dt': 'new_m', 'new_m_b_re': 'new_m', 'new_m_b_im': 'new_m', 'new_m_c_re': 'new_m', 'new_m_c_im': 'new_m', 'new_m_d_skip': 'new_m', 'new_m_w_glu': 'new_m', 'new_m_b_glu': 'new_m', 'new_m_g_attn_out': 'new_m', 'new_m_g_ssm_out': 'new_m', 'new_m_w_o': 'new_m', 'new_m_g_post_mix': 'new_m', 'new_m_g_pre_ffn': 'new_m', 'new_m_w_gate': 'new_m', 'new_m_w_up': 'new_m', 'new_m_w_down': 'new_m', 'new_m_g_post_ffn': 'new_m', 'new_v_g_pre_mix': 'new_v', 'new_v_w_in': 'new_v', 'new_v_sinks': 'new_v', 'new_v_a_re': 'new_v', 'new_v_a_im': 'new_v', 'new_v_log_dt': 'new_v', 'new_v_b_re': 'new_v', 'new_v_b_im': 'new_v', 'new_v_c_re': 'new_v', 'new_v_c_im': 'new_v', 'new_v_d_skip': 'new_v', 'new_v_w_glu': 'new_v', 'new_v_b_glu': 'new_v', 'new_v_g_attn_out': 'new_v', 'new_v_g_ssm_out': 'new_v', 'new_v_w_o': 'new_v', 'new_v_g_post_mix': 'new_v', 'new_v_g_pre_ffn': 'new_v', 'new_v_w_gate': 'new_v', 'new_v_w_up': 'new_v', 'new_v_w_down': 'new_v', 'new_v_g_post_ffn': 'new_v'}


def _forward(args):
    return _fwd_reference(*[args[k] for k in FWD_PARAMS])


def _output_shape():
    out = _jax.eval_shape(lambda: _forward(_fwd_setup_inputs(0)))
    return out.shape, out.dtype

N_MICROBATCH = 1
ADAM_LR = 0.001
ADAM_B1 = 0.9
ADAM_B2 = 0.999
ADAM_EPS = 1e-08
ADAM_WD = 0.01
ADAM_STEP = 10
PER_EXAMPLE_BATCH_AXIS = {'x': 0, 'positions': 0, 'loss_target': 0}
SHARED_INPUTS = []
_WEIGHT_DTYPES = {'g_pre_mix': _jnp.float32, 'w_in': _jnp.float32, 'sinks': _jnp.float32, 'a_re': _jnp.float32, 'a_im': _jnp.float32, 'log_dt': _jnp.float32, 'b_re': _jnp.float32, 'b_im': _jnp.float32, 'c_re': _jnp.float32, 'c_im': _jnp.float32, 'd_skip': _jnp.float32, 'w_glu': _jnp.float32, 'b_glu': _jnp.float32, 'g_attn_out': _jnp.float32, 'g_ssm_out': _jnp.float32, 'w_o': _jnp.float32, 'g_post_mix': _jnp.float32, 'g_pre_ffn': _jnp.float32, 'w_gate': _jnp.float32, 'w_up': _jnp.float32, 'w_down': _jnp.float32, 'g_post_ffn': _jnp.float32}
MOMENT_SCALE = {'g_pre_mix': 2.570649e-01, 'w_in': 2.213892e-01, 'sinks': 6.664091e-02, 'a_re': 7.498875e-03, 'a_im': 8.400150e-03, 'log_dt': 5.261537e+00, 'b_re': 5.555744e-03, 'b_im': 5.551940e-03, 'c_re': 1.113541e-02, 'c_im': 1.103888e-02, 'd_skip': 5.792262e-01, 'w_glu': 9.407759e-02, 'b_glu': 2.933977e-01, 'g_attn_out': 2.028388e-01, 'g_ssm_out': 6.830880e-01, 'w_o': 3.678367e-01, 'g_post_mix': 8.076889e+00, 'g_pre_ffn': 3.011886e-01, 'w_gate': 1.009429e-01, 'w_up': 1.530751e-01, 'w_down': 2.539191e-01, 'g_post_ffn': 8.007931e+00}


def _to_microbatches(a, axis):
    t = _jnp.moveaxis(a, axis, 0)
    t = t.reshape((N_MICROBATCH, t.shape[0] // N_MICROBATCH) + t.shape[1:])
    return _jnp.moveaxis(t, 1, axis + 1)


def setup_inputs(seed: int = 0) -> dict:
    inp = _fwd_setup_inputs(seed)
    key = _jax.random.fold_in(_jax.random.key(seed), 7919)
    shape, _ = _output_shape()
    out = dict(inp)
    out["loss_target"] = _jax.random.normal(_jax.random.fold_in(key, 0), shape, _jnp.float32)
    for i, name in enumerate(TWIN_WEIGHTS):
        w = inp[name].astype(_jnp.float32)
        if MOMENT_SCALE is None:
            s = _jnp.sqrt(_jnp.mean(_jnp.square(w)) + 1e-30)
        else:
            s = MOMENT_SCALE[name]
        km, kv = _jax.random.split(_jax.random.fold_in(key, i + 1))
        out[name] = w
        out["m_" + name] = s * _jax.random.normal(km, w.shape, _jnp.float32)
        out["v_" + name] = (s * s) * _jax.random.uniform(kv, w.shape, _jnp.float32, 0.5, 1.5)
    if N_MICROBATCH > 1:
        for name, axis in PER_EXAMPLE_BATCH_AXIS.items():
            out[name] = _to_microbatches(out[name], axis)
    return {'x': out['x'], 'positions': out['positions'], 'g_pre_mix': out['g_pre_mix'], 'w_in': out['w_in'], 'sinks': out['sinks'], 'a_re': out['a_re'], 'a_im': out['a_im'], 'log_dt': out['log_dt'], 'b_re': out['b_re'], 'b_im': out['b_im'], 'c_re': out['c_re'], 'c_im': out['c_im'], 'd_skip': out['d_skip'], 'w_glu': out['w_glu'], 'b_glu': out['b_glu'], 'g_attn_out': out['g_attn_out'], 'g_ssm_out': out['g_ssm_out'], 'w_o': out['w_o'], 'g_post_mix': out['g_post_mix'], 'g_pre_ffn': out['g_pre_ffn'], 'w_gate': out['w_gate'], 'w_up': out['w_up'], 'w_down': out['w_down'], 'g_post_ffn': out['g_post_ffn'], 'loss_target': out['loss_target'], 'm_g_pre_mix': out['m_g_pre_mix'], 'm_w_in': out['m_w_in'], 'm_sinks': out['m_sinks'], 'm_a_re': out['m_a_re'], 'm_a_im': out['m_a_im'], 'm_log_dt': out['m_log_dt'], 'm_b_re': out['m_b_re'], 'm_b_im': out['m_b_im'], 'm_c_re': out['m_c_re'], 'm_c_im': out['m_c_im'], 'm_d_skip': out['m_d_skip'], 'm_w_glu': out['m_w_glu'], 'm_b_glu': out['m_b_glu'], 'm_g_attn_out': out['m_g_attn_out'], 'm_g_ssm_out': out['m_g_ssm_out'], 'm_w_o': out['m_w_o'], 'm_g_post_mix': out['m_g_post_mix'], 'm_g_pre_ffn': out['m_g_pre_ffn'], 'm_w_gate': out['m_w_gate'], 'm_w_up': out['m_w_up'], 'm_w_down': out['m_w_down'], 'm_g_post_ffn': out['m_g_post_ffn'], 'v_g_pre_mix': out['v_g_pre_mix'], 'v_w_in': out['v_w_in'], 'v_sinks': out['v_sinks'], 'v_a_re': out['v_a_re'], 'v_a_im': out['v_a_im'], 'v_log_dt': out['v_log_dt'], 'v_b_re': out['v_b_re'], 'v_b_im': out['v_b_im'], 'v_c_re': out['v_c_re'], 'v_c_im': out['v_c_im'], 'v_d_skip': out['v_d_skip'], 'v_w_glu': out['v_w_glu'], 'v_b_glu': out['v_b_glu'], 'v_g_attn_out': out['v_g_attn_out'], 'v_g_ssm_out': out['v_g_ssm_out'], 'v_w_o': out['v_w_o'], 'v_g_post_mix': out['v_g_post_mix'], 'v_g_pre_ffn': out['v_g_pre_ffn'], 'v_w_gate': out['v_w_gate'], 'v_w_up': out['v_w_up'], 'v_w_down': out['v_w_down'], 'v_g_post_ffn': out['v_g_post_ffn']}


def _loss(weights, diff, rest, loss_target):
    with _jax.named_scope("forward"):
        args = {**rest, TWIN_DIFF_INPUT: diff, **{k: w.astype(_WEIGHT_DTYPES[k]) for k, w in weights.items()}}
        y = _forward(args)
    with _jax.named_scope("loss_head"):
        err = _jnp.square(y.astype(_jnp.float32) - loss_target)
        return 0.5 * _jnp.sum(_jnp.mean(err, axis=-1)) if err.ndim else 0.5 * err


def _adamw(w, g, m, v):
    m = ADAM_B1 * m + (1.0 - ADAM_B1) * g
    v = ADAM_B2 * v + (1.0 - ADAM_B2) * _jnp.square(g)
    m_hat = m / (1.0 - ADAM_B1 ** ADAM_STEP)
    v_hat = v / (1.0 - ADAM_B2 ** ADAM_STEP)
    delta = -ADAM_LR * (m_hat / (_jnp.sqrt(v_hat) + ADAM_EPS) + ADAM_WD * w)
    return delta, m, v


def reference(x, positions, g_pre_mix, w_in, sinks, a_re, a_im, log_dt, b_re, b_im, c_re, c_im, d_skip, w_glu, b_glu, g_attn_out, g_ssm_out, w_o, g_post_mix, g_pre_ffn, w_gate, w_up, w_down, g_post_ffn, loss_target, m_g_pre_mix, m_w_in, m_sinks, m_a_re, m_a_im, m_log_dt, m_b_re, m_b_im, m_c_re, m_c_im, m_d_skip, m_w_glu, m_b_glu, m_g_attn_out, m_g_ssm_out, m_w_o, m_g_post_mix, m_g_pre_ffn, m_w_gate, m_w_up, m_w_down, m_g_post_ffn, v_g_pre_mix, v_w_in, v_sinks, v_a_re, v_a_im, v_log_dt, v_b_re, v_b_im, v_c_re, v_c_im, v_d_skip, v_w_glu, v_b_glu, v_g_attn_out, v_g_ssm_out, v_w_o, v_g_post_mix, v_g_pre_ffn, v_w_gate, v_w_up, v_w_down, v_g_post_ffn):
    given = dict(x=x, positions=positions, g_pre_mix=g_pre_mix, w_in=w_in, sinks=sinks, a_re=a_re, a_im=a_im, log_dt=log_dt, b_re=b_re, b_im=b_im, c_re=c_re, c_im=c_im, d_skip=d_skip, w_glu=w_glu, b_glu=b_glu, g_attn_out=g_attn_out, g_ssm_out=g_ssm_out, w_o=w_o, g_post_mix=g_post_mix, g_pre_ffn=g_pre_ffn, w_gate=w_gate, w_up=w_up, w_down=w_down, g_post_ffn=g_post_ffn, loss_target=loss_target, m_g_pre_mix=m_g_pre_mix, m_w_in=m_w_in, m_sinks=m_sinks, m_a_re=m_a_re, m_a_im=m_a_im, m_log_dt=m_log_dt, m_b_re=m_b_re, m_b_im=m_b_im, m_c_re=m_c_re, m_c_im=m_c_im, m_d_skip=m_d_skip, m_w_glu=m_w_glu, m_b_glu=m_b_glu, m_g_attn_out=m_g_attn_out, m_g_ssm_out=m_g_ssm_out, m_w_o=m_w_o, m_g_post_mix=m_g_post_mix, m_g_pre_ffn=m_g_pre_ffn, m_w_gate=m_w_gate, m_w_up=m_w_up, m_w_down=m_w_down, m_g_post_ffn=m_g_post_ffn, v_g_pre_mix=v_g_pre_mix, v_w_in=v_w_in, v_sinks=v_sinks, v_a_re=v_a_re, v_a_im=v_a_im, v_log_dt=v_log_dt, v_b_re=v_b_re, v_b_im=v_b_im, v_c_re=v_c_re, v_c_im=v_c_im, v_d_skip=v_d_skip, v_w_glu=v_w_glu, v_b_glu=v_b_glu, v_g_attn_out=v_g_attn_out, v_g_ssm_out=v_g_ssm_out, v_w_o=v_w_o, v_g_post_mix=v_g_post_mix, v_g_pre_ffn=v_g_pre_ffn, v_w_gate=v_w_gate, v_w_up=v_w_up, v_w_down=v_w_down, v_g_post_ffn=v_g_post_ffn)
    weights = {n: given[n] for n in TWIN_WEIGHTS}
    shared = {n: given[n] for n in SHARED_INPUTS}
    per_example = {n: given[n] for n in ['x', 'positions']}
    grad_fn = _jax.value_and_grad(_loss, argnums=(0, 1))

    def one_microbatch(ex, loss_target):
        ex = dict(ex)
        diff = ex.pop(TWIN_DIFF_INPUT)
        return grad_fn(weights, diff, {**shared, **ex}, loss_target)

    if N_MICROBATCH == 1:
        loss, (grad_w, grad_x) = one_microbatch(per_example, given["loss_target"])
    else:
        def body(carry, xs):
            loss_sum, grad_sum = carry
            l_k, (gw_k, gx_k) = one_microbatch(xs[0], xs[1])
            with _jax.named_scope("update"):
                return (loss_sum + l_k, _jax.tree.map(_jnp.add, grad_sum, gw_k)), gx_k

        init = (_jnp.zeros((), _jnp.float32), _jax.tree.map(_jnp.zeros_like, weights))
        (loss, grad_w), grad_x = _jax.lax.scan(body, init, (per_example, given["loss_target"]))
    with _jax.named_scope("update"):
        delta_w, new_m, new_v = {}, {}, {}
        for n in TWIN_WEIGHTS:
            delta_w[n], new_m[n], new_v[n] = _adamw(weights[n], grad_w[n], given["m_" + n], given["v_" + n])
    return (loss, grad_x, *[grad_w[n] for n in TWIN_WEIGHTS], *[delta_w[n] for n in TWIN_WEIGHTS],
            *[new_m[n] for n in TWIN_WEIGHTS], *[new_v[n] for n in TWIN_WEIGHTS])
```

```python
import functools
import math

import jax
import jax.numpy as jnp
import numpy as np
from jax import lax
from jax.experimental import pallas as pl
from jax.experimental.pallas import tpu as pltpu

f32 = jnp.float32
bf16 = jnp.bfloat16
HIGHEST = lax.Precision.HIGHEST
MESH = pl.DeviceIdType.MESH

HEAD_DIM = 64
N_KV_HEADS = 4
ATTN_BLOCK = 128
ROPE_THETA = 10000.0
SSM_GROUP = 16
SSM_STATE = 64
RMS_EPS = 1e-6
LANES = 128
SUBLANES = 8
VMEM_LIMIT = 52 * 1024 * 1024
N_CHIPS = 4
N_DEV = 8
NEG = -1e30

ADAM_LR, ADAM_B1, ADAM_B2, ADAM_EPS, ADAM_WD, ADAM_STEP = 0.001, 0.9, 0.999, 1e-08, 0.01, 10

NN = (((1,), (0,)), ((), ()))
NT = (((1,), (1,)), ((), ()))
TN = (((0,), (0,)), ((), ()))


def _params(*sem):
    return pltpu.CompilerParams(dimension_semantics=sem or None, vmem_limit_bytes=VMEM_LIMIT)


def _dot(a, b, dims=NN):
    return lax.dot_general(a, b, dims, preferred_element_type=f32)


def _pick(dim, pref):
    t = min(dim, pref)
    while dim % t:
        t -= LANES
    assert t > 0, (dim, pref)
    return t


def _mm_call(name, grid, in_specs, out_spec, out_shape, acc_shape, dims, operands):
    nk = grid[2]

    def body(a_ref, b_ref, o_ref, acc_ref):
        k = pl.program_id(2)

        @pl.when(k == 0)
        def _():
            acc_ref[...] = jnp.zeros_like(acc_ref)

        acc_ref[...] += _dot(a_ref[...], b_ref[...], dims)

        @pl.when(k == nk - 1)
        def _():
            o_ref[...] = acc_ref[...].astype(o_ref.dtype)

    return pl.pallas_call(
        body, out_shape=out_shape, grid=grid, in_specs=in_specs, out_specs=out_spec,
        scratch_shapes=[pltpu.VMEM(acc_shape, f32)], name=name,
        compiler_params=_params("parallel", "parallel", "arbitrary"))(*operands)


def mm_nn(name, a, b, out_dtype=f32, tm=512, tn=1024, tk=512):
    M, K = a.shape
    tm, tk = _pick(M, tm), _pick(K, tk)
    if b.ndim == 3:
        S, _, n = b.shape
        tn = _pick(n, 2048)
        per = n // tn
        b_spec = pl.BlockSpec((None, tk, tn), lambda i, j, k: (j // per, k, j % per))
        N = S * n
    else:
        N = b.shape[1]
        tn = _pick(N, tn)
        b_spec = pl.BlockSpec((tk, tn), lambda i, j, k: (k, j))
    grid = (M // tm, N // tn, K // tk)
    return _mm_call(name, grid, [pl.BlockSpec((tm, tk), lambda i, j, k: (i, k)), b_spec],
                    pl.BlockSpec((tm, tn), lambda i, j, k: (i, j)), jax.ShapeDtypeStruct((M, N), out_dtype),
                    (tm, tn), NN, (a, b))


def mm_nt(name, a, b, out_dtype=f32, tm=512, tn=1024, tk=512):
    M, N = a.shape
    tm = _pick(M, tm)
    if b.ndim == 3:
        S, K, n = b.shape
        tr = _pick(n, 2048)
        per = n // tr
        tko = _pick(K, tk)
        b_spec = pl.BlockSpec((None, tko, tr), lambda i, j, k: (k // per, j, k % per))
    else:
        K = b.shape[0]
        tr = _pick(N, tn)
        tko = _pick(K, tk)
        b_spec = pl.BlockSpec((tko, tr), lambda i, j, k: (j, k))
    grid = (M // tm, K // tko, N // tr)
    return _mm_call(name, grid, [pl.BlockSpec((tm, tr), lambda i, j, k: (i, k)), b_spec],
                    pl.BlockSpec((tm, tko), lambda i, j, k: (i, j)), jax.ShapeDtypeStruct((M, K), out_dtype),
                    (tm, tko), NT, (a, b))


def mm_tn(name, a, b, shards=None, out_dtype=f32, tm=512, tn=1024, tl=512):
    L, K = a.shape
    N = b.shape[1]
    tl, tko = _pick(L, tl), _pick(K, tm)
    if shards:
        n = N // shards
        tn = _pick(n, 2048)
        per = n // tn
        o_spec = pl.BlockSpec((None, tko, tn), lambda i, j, k: (j // per, i, j % per))
        o_shape = jax.ShapeDtypeStruct((shards, K, n), out_dtype)
    else:
        tn = _pick(N, tn)
        o_spec = pl.BlockSpec((tko, tn), lambda i, j, k: (i, j))
        o_shape = jax.ShapeDtypeStruct((K, N), out_dtype)
    grid = (K // tko, N // tn, L // tl)
    return _mm_call(name, grid, [pl.BlockSpec((tl, tko), lambda i, j, k: (k, i)),
                                 pl.BlockSpec((tl, tn), lambda i, j, k: (k, j))],
                    o_spec, o_shape, (tko, tn), TN, (a, b))


def rowwise(name, fn, L, rows, bcast, outs, reds=(), tr=128):
    tr = min(tr, L)
    nt = L // tr
    n_rows, n_b, n_o = len(rows), len(bcast), len(outs)

    def body(*refs):
        i = pl.program_id(0)
        ins = [r[...] for r in refs[:n_rows + n_b]]
        res = fn(i, nt, *ins)
        o_refs = refs[n_rows + n_b:]
        for k in range(n_o):
            o_refs[k][...] = res[k].astype(o_refs[k].dtype)
        if reds:
            @pl.when(i == 0)
            def _():
                for k in range(len(reds)):
                    o_refs[n_o + k][...] = jnp.zeros_like(o_refs[n_o + k])
            for k in range(len(reds)):
                o_refs[n_o + k][...] += res[n_o + k]

    def row_spec(width, cb, shift):
        if shift:
            return pl.BlockSpec((tr, width), lambda i: (jnp.minimum(i + shift, nt - 1), cb))
        return pl.BlockSpec((tr, width), lambda i: (i, cb))

    in_specs = [row_spec(w, cb, sh) for (_, w, cb, sh) in rows]
    in_specs += [pl.BlockSpec(b.shape, lambda i: (0, 0)) for b in bcast]
    out_specs = [pl.BlockSpec((tr, w), lambda i: (i, 0)) for (w, _) in outs]
    out_specs += [pl.BlockSpec((1, w), lambda i: (0, 0)) for w in reds]
    out_shape = [jax.ShapeDtypeStruct((L, w), dt) for (w, dt) in outs]
    out_shape += [jax.ShapeDtypeStruct((1, w), f32) for w in reds]
    return pl.pallas_call(
        body, out_shape=out_shape, grid=(nt,), in_specs=in_specs, out_specs=out_specs, name=name,
        compiler_params=_params("arbitrary"))(*[r[0] for r in rows], *bcast)


def full(a):
    return (a, a.shape[1], 0, 0)


def colsum(v):
    return jnp.sum(v, axis=0, keepdims=True)


def rms_fwd(x, g):
    r = lax.rsqrt(jnp.mean(x * x, axis=-1, keepdims=True) + RMS_EPS)
    return x * r * g


def rms_bwd(x, g, dy):
    r = lax.rsqrt(jnp.mean(x * x, axis=-1, keepdims=True) + RMS_EPS)
    xh = x * r
    dyg = dy * g
    dx = r * (dyg - xh * jnp.mean(dyg * xh, axis=-1, keepdims=True))
    return dx, colsum(dy * xh)


GELU_C = math.sqrt(2.0 / math.pi)


def gelu(y):
    return y * (0.5 * (1.0 + jnp.tanh(GELU_C * (y + 0.044715 * (y * y * y)))))


def gelu_grad(y):
    t = jnp.tanh(GELU_C * (y + 0.044715 * (y * y * y)))
    return 0.5 * (1.0 + t) + 0.5 * y * (1.0 - t * t) * (GELU_C * (1.0 + 3 * 0.044715 * (y * y)))


def sigmoid(v):
    return 1.0 / (1.0 + jnp.exp(-v))


def _lane(shape):
    return lax.broadcasted_iota(jnp.int32, shape, 1)


def _rot_chunk(t, cos, sin_signed):
    first = (_lane(t.shape) % HEAD_DIM) < (HEAD_DIM // 2)
    partner = jnp.where(first, pltpu.roll(t, LANES - HEAD_DIM // 2, 1), pltpu.roll(t, HEAD_DIM // 2, 1))
    return t * cos + partner * sin_signed


def _cos_sin(pos, inv_freq, inverse):
    ang = pos * inv_freq
    cos, sin = jnp.cos(ang), jnp.sin(ang)
    first = (_lane(ang.shape) % HEAD_DIM) < (HEAD_DIM // 2)
    sign = jnp.where(first, -1.0, 1.0) * (-1.0 if inverse else 1.0)
    return cos, sin * sign


def _dup_head(chunk, odd):
    low = _lane(chunk.shape) < HEAD_DIM
    x = jnp.where(low != odd, chunk, 0.0)
    return x + pltpu.roll(x, HEAD_DIM, 1)


def _chunks(v):
    return [v[:, LANES * c:LANES * (c + 1)] for c in range(v.shape[1] // LANES)]


def qkv_prep(proj, pos, inv_freq, d_attn, d_kv):
    L = proj.shape[0]
    d_ssm = proj.shape[1] - d_attn - 2 * d_kv
    half = d_ssm // 2
    scale = 1.0 / math.sqrt(HEAD_DIM)

    def fn(i, nt, q, k, v, u0, u1, p, invf):
        cos, sin = _cos_sin(p, invf, False)
        qr = jnp.concatenate([_rot_chunk(c, cos, sin) for c in _chunks(q)], axis=1) * scale
        kr = [_rot_chunk(c, cos, sin) for c in _chunks(k)]
        kk = jnp.concatenate([_dup_head(c, odd) for c in kr for odd in (False, True)], axis=1)
        vv = jnp.concatenate([_dup_head(c, odd) for c in _chunks(v) for odd in (False, True)], axis=1)
        return qr, kk, vv, jnp.concatenate([u0, u1], axis=1)

    u_cb = (d_attn + 2 * d_kv) // half
    return rowwise("qkv_prep", fn, L,
                   [(proj, d_attn, 0, 0), (proj, d_kv, d_attn // d_kv, 0), (proj, d_kv, d_attn // d_kv + 1, 0),
                    (proj, half, u_cb, 0), (proj, half, u_cb + 1, 0), full(pos)],
                   [inv_freq], [(d_attn, bf16), (2 * d_kv, bf16), (2 * d_kv, bf16), (d_ssm, bf16)])


def qkv_grad(dq, dkk_c, dkk_p, dvv_c, dvv_p, du, pos, inv_freq):
    L, d_attn = dq.shape
    d_kv = dkk_c.shape[1] // 2
    scale = 1.0 / math.sqrt(HEAD_DIM)

    def fold(cur, prev, i, nt):
        t = cur + jnp.where(i < nt - 1, prev, 0.0)
        out = []
        for c in range(t.shape[1] // (2 * LANES)):
            even, odd = t[:, 2 * c * LANES:(2 * c + 1) * LANES], t[:, (2 * c + 1) * LANES:(2 * c + 2) * LANES]
            even, odd = even + pltpu.roll(even, HEAD_DIM, 1), odd + pltpu.roll(odd, HEAD_DIM, 1)
            out.append(jnp.where(_lane(even.shape) < HEAD_DIM, even, odd))
        return out

    def fn(i, nt, dq_t, kc, kp, vc, vp, du_t, p, invf):
        cos, sin = _cos_sin(p, invf, True)
        dq_o = jnp.concatenate([_rot_chunk(c, cos, sin) for c in _chunks(dq_t)], axis=1) * scale
        dk_o = jnp.concatenate([_rot_chunk(c, cos, sin) for c in fold(kc, kp, i, nt)], axis=1)
        dv_o = jnp.concatenate(fold(vc, vp, i, nt), axis=1)
        return (jnp.concatenate([dq_o, dk_o, dv_o, du_t], axis=1),)

    return rowwise("qkv_grad", fn, L,
                   [full(dq), full(dkk_c), (dkk_p, 2 * d_kv, 0, 1), full(dvv_c), (dvv_p, 2 * d_kv, 0, 1), full(du), full(pos)],
                   [inv_freq], [(d_attn + 2 * d_kv + du.shape[1], bf16)], tr=ATTN_BLOCK)[0]


def _attn_specs(L):
    nb = L // ATTN_BLOCK
    B = ATTN_BLOCK
    q_spec = pl.BlockSpec((B, 2 * LANES), lambda h, n: (n, h))
    cur = pl.BlockSpec((B, LANES), lambda h, n: (n, h))
    prev = pl.BlockSpec((B, LANES), lambda h, n: (jnp.maximum(n - 1, 0), h))
    return nb, q_spec, cur, prev


def _attn_probs(qm, kc, kp, sink, n):
    B = ATTN_BLOCK
    row = lax.broadcasted_iota(jnp.int32, (B, B), 0)
    col = lax.broadcasted_iota(jnp.int32, (B, B), 1)
    sc = jnp.where(row >= col, _dot(qm, kc, NT), NEG)
    sp = jnp.where((col > row) & (n > 0), _dot(qm, kp, NT), NEG)
    m = jnp.maximum(jnp.maximum(jnp.max(sc, axis=1, keepdims=True), jnp.max(sp, axis=1, keepdims=True)), sink)
    pc, pp, ps = jnp.exp(sc - m), jnp.exp(sp - m), jnp.exp(sink - m)
    inv = 1.0 / (jnp.sum(pc, axis=1, keepdims=True) + jnp.sum(pp, axis=1, keepdims=True) + ps)
    return pc, pp, inv, ps


def attn_fwd(qr, kk, vv, sink_b):
    L, d_attn = qr.shape
    nb, q_spec, cur, prev = _attn_specs(L)
    n_kv = kk.shape[1] // LANES
    q_per_kv = d_attn // HEAD_DIM // n_kv

    def body(q_ref, kc_ref, kp_ref, vc_ref, vp_ref, s_ref, o_ref):
        h, n = pl.program_id(0), pl.program_id(1)
        low = _lane((ATTN_BLOCK, LANES)) < HEAD_DIM
        kc, kp, vc, vp = kc_ref[...], kp_ref[...], vc_ref[...], vp_ref[...]
        for pr in range(q_per_kv // 2):
            q2 = q_ref[:, LANES * pr:LANES * (pr + 1)]
            o2 = jnp.zeros((ATTN_BLOCK, LANES), f32)
            for odd in (False, True):
                mine = low != odd
                qm = jnp.where(mine, q2, jnp.zeros_like(q2))
                sink = jnp.max(s_ref[pl.ds(h * q_per_kv + 2 * pr + int(odd), 1), :], axis=1, keepdims=True)
                pc, pp, inv, _ = _attn_probs(qm, kc, kp, sink, n)
                o = _dot(pc.astype(bf16), vc) + _dot(pp.astype(bf16), vp)
                o2 = o2 + jnp.where(mine, o * inv, 0.0)
            o_ref[:, LANES * pr:LANES * (pr + 1)] = o2

    return pl.pallas_call(
        body, out_shape=jax.ShapeDtypeStruct((L, d_attn), f32), grid=(n_kv, nb),
        in_specs=[q_spec, cur, prev, cur, prev, pl.BlockSpec(sink_b.shape, lambda h, n: (0, 0))],
        out_specs=q_spec, name="attn_fwd", compiler_params=_params("parallel", "arbitrary"))(qr, kk, kk, vv, vv, sink_b)


def attn_bwd(qr, kk, vv, sink_b, attn, d_attn_out):
    L, d_attn = qr.shape
    nb, q_spec, cur, prev = _attn_specs(L)
    n_kv = kk.shape[1] // LANES
    q_per_kv = d_attn // HEAD_DIM // n_kv

    def body(q_ref, kc_ref, kp_ref, vc_ref, vp_ref, s_ref, o_ref, do_ref, dq_ref, dkc_ref, dkp_ref, dvc_ref, dvp_ref, ds_ref):
        h, n = pl.program_id(0), pl.program_id(1)
        B = ATTN_BLOCK
        low = _lane((B, LANES)) < HEAD_DIM
        kc, kp, vc, vp = kc_ref[...], kp_ref[...], vc_ref[...], vp_ref[...]
        dkc = jnp.zeros((B, LANES), f32)
        dkp, dvc, dvp = dkc, dkc, dkc
        srow = lax.broadcasted_iota(jnp.int32, (SUBLANES, LANES), 0)
        dsink = jnp.zeros((SUBLANES, LANES), f32)
        for pr in range(q_per_kv // 2):
            q2 = q_ref[:, LANES * pr:LANES * (pr + 1)]
            do2 = do_ref[:, LANES * pr:LANES * (pr + 1)]
            prod = do2 * o_ref[:, LANES * pr:LANES * (pr + 1)]
            dq2 = jnp.zeros((B, LANES), f32)
            for odd in (False, True):
                mine = low != odd
                qm = jnp.where(mine, q2, jnp.zeros_like(q2))
                sink = jnp.max(s_ref[pl.ds(h * q_per_kv + 2 * pr + int(odd), 1), :], axis=1, keepdims=True)
                pc, pp, inv, ps = _attn_probs(qm, kc, kp, sink, n)
                pc, pp = pc * inv, pp * inv
                delta = jnp.sum(jnp.where(mine, prod, 0.0), axis=1, keepdims=True)
                dob = jnp.where(mine, do2, 0.0).astype(bf16)
                dsc = (pc * (_dot(dob, vc, NT) - delta)).astype(bf16)
                dsp = (pp * (_dot(dob, vp, NT) - delta)).astype(bf16)
                dq2 = dq2 + jnp.where(mine, _dot(dsc, kc) + _dot(dsp, kp), 0.0)
                dkc = dkc + _dot(dsc, qm, TN)
                dkp = dkp + _dot(dsp, qm, TN)
                dvc = dvc + _dot(pc.astype(bf16), dob, TN)
                dvp = dvp + _dot(pp.astype(bf16), dob, TN)
                dsink = dsink + jnp.where(srow == 2 * pr + int(odd), -jnp.sum(ps * inv * delta), 0.0)
            dq_ref[:, LANES * pr:LANES * (pr + 1)] = dq2
        dkc_ref[...] = dkc
        dkp_ref[...] = dkp
        dvc_ref[...] = dvc
        dvp_ref[...] = dvp

        @pl.when(n == 0)
        def _():
            ds_ref[...] = jnp.zeros_like(ds_ref)

        ds_ref[...] += dsink

    kv_shape = jax.ShapeDtypeStruct(kk.shape, f32)
    return pl.pallas_call(
        body,
        out_shape=[jax.ShapeDtypeStruct((L, d_attn), f32), kv_shape, kv_shape, kv_shape, kv_shape,
                   jax.ShapeDtypeStruct((n_kv, SUBLANES, LANES), f32)],
        grid=(n_kv, nb),
        in_specs=[q_spec, cur, prev, cur, prev, pl.BlockSpec(sink_b.shape, lambda h, n: (0, 0)), q_spec, q_spec],
        out_specs=[q_spec, cur, cur, cur, cur, pl.BlockSpec((None, SUBLANES, LANES), lambda h, n: (h, 0, 0))],
        name="attn_bwd", compiler_params=_params("parallel", "arbitrary"))(qr, kk, kk, vv, vv, sink_b, attn, d_attn_out)


SSM_T = 128
NQ = SUBLANES * SSM_STATE // LANES
NJ = SUBLANES


def _strided_put(ref, j, val):
    for q in range(NQ):
        ref.at[q][pl.ds(j, SSM_T, stride=NJ), :] = val[:, LANES * q:LANES * (q + 1)]


def _strided_get(ref, j):
    return jnp.concatenate([ref.at[q][pl.ds(j, SSM_T, stride=NJ), :] for q in range(NQ)], axis=1)


def _ssm_specs(L, rev):
    nt = L // SSM_T
    idx = (lambda i: nt - 1 - i) if rev else (lambda i: i)
    row = lambda w, cb=0: pl.BlockSpec((SSM_T, w), lambda i: (idx(i), cb))
    state = pl.BlockSpec((NQ, SSM_T * NJ, LANES), lambda i: (0, idx(i), 0))
    whole = lambda a: pl.BlockSpec(a.shape, lambda i: (0,) * a.ndim)
    return nt, row, state, whole


def ssm_fwd(u_bf, proj, u_cb, bd_re, bd_im, cd_re, cd_im, lam_re, lam_im, d_skip):
    L, d_ssm = u_bf.shape
    nt, row, state, whole = _ssm_specs(L, False)
    half = d_ssm // 2
    gw = d_ssm // NJ

    def body(u_ref, u0_ref, u1_ref, bdr, bdi, cdr, cdi, lr_ref, li_ref, d_ref, y_ref, sr_ref, si_ref, carry):
        i = pl.program_id(0)

        @pl.when(i == 0)
        def _():
            carry[...] = jnp.zeros_like(carry)

        for j in range(NJ):
            uj = u_ref[:, gw * j:gw * (j + 1)]
            _strided_put(sr_ref, j, _dot(uj, bdr[j]))
            _strided_put(si_ref, j, _dot(uj, bdi[j]))
        lr = [lr_ref[q] for q in range(NQ)]
        li = [li_ref[q] for q in range(NQ)]

        def step(t, s):
            sr, si = s
            rows = pl.ds(pl.multiple_of(t * NJ, NJ), NJ)
            nr = tuple(lr[q] * sr[q] - li[q] * si[q] + sr_ref[q, rows, :] for q in range(NQ))
            ni = tuple(lr[q] * si[q] + li[q] * sr[q] + si_ref[q, rows, :] for q in range(NQ))
            for q in range(NQ):
                sr_ref[q, rows, :] = nr[q]
                si_ref[q, rows, :] = ni[q]
            return nr, ni

        init = (tuple(carry[0, q] for q in range(NQ)), tuple(carry[1, q] for q in range(NQ)))
        sr, si = lax.fori_loop(0, SSM_T, step, init, unroll=8)
        for q in range(NQ):
            carry[0, q] = sr[q]
            carry[1, q] = si[q]
        uf = jnp.concatenate([u0_ref[...], u1_ref[...]], axis=1)
        for j in range(NJ):
            cols = slice(gw * j, gw * (j + 1))
            yj = _dot(_strided_get(sr_ref, j).astype(bf16), cdr[j]) - _dot(_strided_get(si_ref, j).astype(bf16), cdi[j])
            y_ref[:, cols] = yj + d_ref[:, cols] * uf[:, cols]

    s_shape = jax.ShapeDtypeStruct((NQ, L * NJ, LANES), f32)
    consts = (bd_re, bd_im, cd_re, cd_im, lam_re, lam_im, d_skip)
    return pl.pallas_call(
        body, out_shape=[jax.ShapeDtypeStruct((L, d_ssm), f32), s_shape, s_shape], grid=(nt,),
        in_specs=[row(d_ssm), row(half, u_cb), row(half, u_cb + 1)] + [whole(a) for a in consts],
        out_specs=[row(d_ssm), state, state],
        scratch_shapes=[pltpu.VMEM((2, NQ, NJ, LANES), f32)], name="ssm_fwd",
        compiler_params=_params("arbitrary"))(u_bf, proj, proj, *consts)


def ssm_bwd(dy, u_bf, proj, u_cb, s_re, s_im, bd_re, bd_im, cd_re, cd_im, lam_re, lam_im, d_skip):
    L, d_ssm = dy.shape
    nt, row, state, whole = _ssm_specs(L, True)
    half = d_ssm // 2
    gw = d_ssm // NJ

    def body(dy_ref, u_ref, u0_ref, u1_ref, sr_ref, si_ref, bdr, bdi, cdr, cdi, lr_ref, li_ref, d_ref,
             du_ref, dbdr, dbdi, dcdr, dcdi, dlr, dli, dd_ref, gr_ref, gi_ref, carry):
        i = pl.program_id(0)

        @pl.when(i == 0)
        def _():
            carry[...] = jnp.zeros_like(carry)
            for r in (dbdr, dbdi, dcdr, dcdi, dlr, dli, dd_ref):
                r[...] = jnp.zeros_like(r)

        dyf = dy_ref[...]
        dyb = dyf.astype(bf16)
        for j in range(NJ):
            dyj = dyb[:, gw * j:gw * (j + 1)]
            _strided_put(gr_ref, j, _dot(dyj, cdr[j], NT))
            _strided_put(gi_ref, j, -_dot(dyj, cdi[j], NT))
            dcdr[j] += _dot(_strided_get(sr_ref, j).astype(bf16), dyj, TN)
            dcdi[j] -= _dot(_strided_get(si_ref, j).astype(bf16), dyj, TN)
        lr = [lr_ref[q] for q in range(NQ)]
        li = [li_ref[q] for q in range(NQ)]

        def step(k, c):
            gr, gi, ar, ai = c
            rows = pl.ds(pl.multiple_of((SSM_T - 1 - k) * NJ, NJ), NJ)
            s_r = [sr_ref[q, rows, :] for q in range(NQ)]
            s_i = [si_ref[q, rows, :] for q in range(NQ)]
            ar = tuple(ar[q] + gr[q] * s_r[q] + gi[q] * s_i[q] for q in range(NQ))
            ai = tuple(ai[q] + gi[q] * s_r[q] - gr[q] * s_i[q] for q in range(NQ))
            nr = tuple(gr_ref[q, rows, :] + lr[q] * gr[q] + li[q] * gi[q] for q in range(NQ))
            ni = tuple(gi_ref[q, rows, :] + lr[q] * gi[q] - li[q] * gr[q] for q in range(NQ))
            for q in range(NQ):
                gr_ref[q, rows, :] = nr[q]
                gi_ref[q, rows, :] = ni[q]
            return nr, ni, ar, ai

        zero = tuple(jnp.zeros((NJ, LANES), f32) for _ in range(NQ))
        init = (tuple(carry[0, q] for q in range(NQ)), tuple(carry[1, q] for q in range(NQ)), zero, zero)
        gr, gi, ar, ai = lax.fori_loop(0, SSM_T, step, init, unroll=8)
        for q in range(NQ):
            carry[0, q] = gr[q]
            carry[1, q] = gi[q]
            dlr[q] += ar[q]
            dli[q] += ai[q]
        uf = jnp.concatenate([u0_ref[...], u1_ref[...]], axis=1)
        dd_ref[...] += colsum(dyf * uf)
        for j in range(NJ):
            cols = slice(gw * j, gw * (j + 1))
            gjr, gji = _strided_get(gr_ref, j).astype(bf16), _strided_get(gi_ref, j).astype(bf16)
            du_ref[:, cols] = _dot(gjr, bdr[j], NT) + _dot(gji, bdi[j], NT) + d_ref[:, cols] * dyf[:, cols]
            uj = u_ref[:, cols]
            dbdr[j] += _dot(uj, gjr, TN)
            dbdi[j] += _dot(uj, gji, TN)

    consts = (bd_re, bd_im, cd_re, cd_im, lam_re, lam_im, d_skip)
    acc = lambda a: jax.ShapeDtypeStruct(a.shape, f32)
    outs = [jax.ShapeDtypeStruct((L, d_ssm), f32), acc(bd_re), acc(bd_im), acc(cd_re), acc(cd_im), acc(lam_re), acc(lam_im), acc(d_skip)]
    return pl.pallas_call(
        body, out_shape=outs, grid=(nt,),
        in_specs=[row(d_ssm), row(d_ssm), row(half, u_cb), row(half, u_cb + 1), state, state] + [whole(a) for a in consts],
        out_specs=[row(d_ssm)] + [whole(a) for a in consts],
        scratch_shapes=[pltpu.VMEM((NQ, SSM_T * NJ, LANES), f32), pltpu.VMEM((NQ, SSM_T * NJ, LANES), f32),
                        pltpu.VMEM((2, NQ, NJ, LANES), f32)],
        name="ssm_bwd", compiler_params=_params("arbitrary"))(dy, u_bf, proj, proj, s_re, s_im, *consts)


def _cmul(ar, ai, br, bi):
    return ar * br - ai * bi, ar * bi + ai * br


def _disc(ar, ai, logdt):
    dt = jnp.exp(logdt)
    mag = jnp.exp(ar * dt)
    lr, li = mag * jnp.cos(ai * dt), mag * jnp.sin(ai * dt)
    den = ar * ar + ai * ai
    nr, ni = lr - 1.0, li
    fr, fi = (nr * ar + ni * ai) / den, (ni * ar - nr * ai) / den
    return dt, lr, li, den, fr, fi


def ssm_params(a_re, a_im, logdt_b, bt_re, bt_im, spread):
    def body(ar_ref, ai_ref, ld_ref, br_ref, bi_ref, sp_ref, lr_ref, li_ref, or_ref, oi_ref):
        _, lr, li, _, fr, fi = _disc(ar_ref[...], ai_ref[...], ld_ref[...])
        lr_ref[...] = lr
        li_ref[...] = li
        fre = jnp.dot(sp_ref[...], fr, precision=HIGHEST, preferred_element_type=f32)
        fie = jnp.dot(sp_ref[...], fi, precision=HIGHEST, preferred_element_type=f32)
        o_r, o_i = _cmul(fre, fie, br_ref[...], bi_ref[...])
        or_ref[...] = o_r
        oi_ref[...] = o_i

    g = jax.ShapeDtypeStruct(a_re.shape, f32)
    b = jax.ShapeDtypeStruct(bt_re.shape, f32)
    return pl.pallas_call(body, out_shape=[g, g, b, b], name="ssm_params",
                          compiler_params=_params())(a_re, a_im, logdt_b, bt_re, bt_im, spread)


def ssm_params_grad(a_re, a_im, logdt_b, bt_re, bt_im, spread, gather, dlam_re, dlam_im, dbt_re, dbt_im):
    def body(ar_ref, ai_ref, ld_ref, br_ref, bi_ref, sp_ref, ga_ref, glr_ref, gli_ref, gbr_ref, gbi_ref,
             dar_ref, dai_ref, dld_ref, dbr_ref, dbi_ref):
        ar, ai = ar_ref[...], ai_ref[...]
        dt, lr, li, den, fr, fi = _disc(ar, ai, ld_ref[...])
        hdot = functools.partial(jnp.dot, precision=HIGHEST, preferred_element_type=f32)
        fre, fie = hdot(sp_ref[...], fr), hdot(sp_ref[...], fi)
        gbr, gbi, br, bi = gbr_ref[...], gbi_ref[...], br_ref[...], bi_ref[...]
        dbr_ref[...], dbi_ref[...] = _cmul(fre, -fie, gbr, gbi)
        t_r, t_i = _cmul(br, -bi, gbr, gbi)
        gfr, gfi = hdot(ga_ref[...], t_r), hdot(ga_ref[...], t_i)
        iwr, iwi = ar / den, -ai / den
        x_r, x_i = _cmul(iwr, -iwi, gfr, gfi)
        glr, gli = glr_ref[...] + x_r, gli_ref[...] + x_i
        q_r, q_i = _cmul(fr, fi, iwr, iwi)
        gwr, gwi = _cmul(-q_r, q_i, gfr, gfi)
        y_r, y_i = _cmul(dt * lr, -dt * li, glr, gli)
        dar_ref[...] = gwr + y_r
        dai_ref[...] = gwi + y_i
        wl_r, wl_i = _cmul(ar, ai, lr, li)
        z_r, _ = _cmul(wl_r, -wl_i, glr, gli)
        dld_ref[...] = jnp.sum(z_r * dt, axis=1, keepdims=True)

    g = jax.ShapeDtypeStruct(a_re.shape, f32)
    b = jax.ShapeDtypeStruct(bt_re.shape, f32)
    return pl.pallas_call(body, out_shape=[g, g, jax.ShapeDtypeStruct((a_re.shape[0], 1), f32), b, b], name="ssm_params_grad",
                          compiler_params=_params())(a_re, a_im, logdt_b, bt_re, bt_im, spread, gather, dlam_re, dlam_im, dbt_re, dbt_im)


def _block_diag(t, rows, cols):
    G = t.shape[0]
    t = t.reshape(G // NJ, NJ, rows, cols)
    eye = jnp.eye(NJ, dtype=t.dtype)
    return jnp.einsum('jgrc,gh->jgrhc', t, eye).reshape(G // NJ, NJ * rows, NJ * cols)


def _block_diag_take(m, rows, cols):
    J = m.shape[0]
    m = m.reshape(J, NJ, rows, NJ, cols)
    idx = jnp.arange(NJ)
    return m[:, idx, :, idx, :].transpose(1, 0, 2, 3).reshape(J * NJ, rows, cols)


def _state_layout(t):
    return t.reshape(NJ, NQ, LANES).transpose(1, 0, 2)


def _state_layout_inv(t, G, N):
    return t.transpose(1, 0, 2).reshape(G, N)


def _tiles2d(shape, budget_rows=128):
    rows, cols = shape
    tr = rows
    if rows > budget_rows:
        tr = budget_rows
        while rows % tr:
            tr -= SUBLANES
    return tr, cols


def adamw(name, w, g, m, v):
    tr, cols = _tiles2d(w.shape, 128 if w.shape[1] > 1024 else 256)
    c1 = 1.0 - ADAM_B1 ** ADAM_STEP
    c2 = 1.0 - ADAM_B2 ** ADAM_STEP

    def body(w_ref, g_ref, m_ref, v_ref, d_ref, nm_ref, nv_ref):
        gg = g_ref[...]
        nm = ADAM_B1 * m_ref[...] + (1.0 - ADAM_B1) * gg
        nv = ADAM_B2 * v_ref[...] + (1.0 - ADAM_B2) * (gg * gg)
        d_ref[...] = -ADAM_LR * ((nm / c1) / (jnp.sqrt(nv / c2) + ADAM_EPS) + ADAM_WD * w_ref[...])
        nm_ref[...] = nm
        nv_ref[...] = nv

    spec = pl.BlockSpec((tr, cols), lambda i: (i, 0))
    o = jax.ShapeDtypeStruct(w.shape, f32)
    return pl.pallas_call(body, out_shape=[o, o, o], grid=(w.shape[0] // tr,), in_specs=[spec] * 4, out_specs=[spec] * 3,
                          name=name, compiler_params=_params("parallel"))(w, g, m, v)


def add_n(name, parts, out_dtype):
    tr, cols = _tiles2d(parts[0].shape, 256)
    n = len(parts)

    def body(*refs):
        acc = refs[0][...].astype(f32)
        for r in refs[1:n]:
            acc = acc + r[...].astype(f32)
        refs[n][...] = acc.astype(out_dtype)

    spec = pl.BlockSpec((tr, cols), lambda i: (i, 0))
    return pl.pallas_call(body, out_shape=jax.ShapeDtypeStruct(parts[0].shape, out_dtype), grid=(parts[0].shape[0] // tr,),
                          in_specs=[spec] * n, out_specs=spec, name=name, compiler_params=_params("parallel"))(*parts)


def sum_slabs(name, t):
    S, rows, cols = t.shape
    tr, _ = _tiles2d((rows, cols), 256)

    def body(t_ref, o_ref):
        acc = t_ref[0].astype(f32)
        for s in range(1, S):
            acc = acc + t_ref[s].astype(f32)
        o_ref[...] = acc

    return pl.pallas_call(body, out_shape=jax.ShapeDtypeStruct((rows, cols), f32), grid=(rows // tr,),
                          in_specs=[pl.BlockSpec((S, tr, cols), lambda i: (0, i, 0))], out_specs=pl.BlockSpec((tr, cols), lambda i: (i, 0)),
                          name=name, compiler_params=_params("parallel"))(t)


def cast_bf16(name, w):
    tr, cols = _tiles2d(w.shape, 256)

    def body(w_ref, o_ref):
        o_ref[...] = w_ref[...].astype(bf16)

    spec = pl.BlockSpec((tr, cols), lambda i: (i, 0))
    return pl.pallas_call(body, out_shape=jax.ShapeDtypeStruct(w.shape, bf16), grid=(w.shape[0] // tr,),
                          in_specs=[spec], out_specs=spec, name=name, compiler_params=_params("parallel"))(w)


ANY = pl.BlockSpec(memory_space=pl.ANY)


def _place():
    x, y, c = lax.axis_index("x"), lax.axis_index("y"), lax.axis_index("c")
    return x, y, c


def _other_chips(x, y):
    return [(1 - x, y, 2 * (1 - x) + y), (x, 1 - y, 2 * x + 1 - y), (1 - x, 1 - y, 2 * (1 - x) + 1 - y)]


def gather_weights(shards):
    nw = len(shards)

    def body(*refs):
        src, dst = refs[:nw], refs[nw:2 * nw]
        send1, recv1, send2, recv2, loc = refs[2 * nw:]
        x, y, c = _place()
        mine = 2 * x + y
        chips = _other_chips(x, y)
        sib = (x, y, 1 - c)
        first, passed, local = [], [], []
        for w in range(nw):
            h = src[w].shape[0] // 2
            cp = pltpu.make_async_copy(src[w], dst[w].at[mine], loc.at[w])
            cp.start()
            local.append(cp)
            for k, (px, py, _) in enumerate(chips):
                half = dst[w].at[mine, pl.ds(c * h, h), :]
                cp = pltpu.make_async_remote_copy(src_ref=src[w].at[pl.ds(c * h, h), :], dst_ref=half,
                                                  send_sem=send1.at[w, k], recv_sem=recv1.at[w, k],
                                                  device_id=(px, py, c), device_id_type=MESH)
                cp.start()
                first.append(cp)
        for w in range(nw):
            h = src[w].shape[0] // 2
            for k, (px, py, s) in enumerate(chips):
                landed = dst[w].at[s, pl.ds(c * h, h), :]
                pltpu.make_async_remote_copy(src_ref=landed, dst_ref=landed, send_sem=send1.at[w, k], recv_sem=recv1.at[w, k],
                                             device_id=(px, py, c), device_id_type=MESH).wait_recv()
                cp = pltpu.make_async_remote_copy(src_ref=landed, dst_ref=landed, send_sem=send2.at[w, k], recv_sem=recv2.at[w, k],
                                                  device_id=sib, device_id_type=MESH)
                cp.start()
                passed.append(cp)
        for w in range(nw):
            h = src[w].shape[0] // 2
            for k, (px, py, s) in enumerate(chips):
                other = dst[w].at[s, pl.ds((1 - c) * h, h), :]
                pltpu.make_async_remote_copy(src_ref=other, dst_ref=other, send_sem=send2.at[w, k], recv_sem=recv2.at[w, k],
                                             device_id=sib, device_id_type=MESH).wait_recv()
        for cp in first + passed:
            cp.wait_send()
        for cp in local:
            cp.wait()

    sem = pltpu.SemaphoreType.DMA((nw, 3))
    return pl.pallas_call(
        body, out_shape=[jax.ShapeDtypeStruct((N_CHIPS,) + s.shape, s.dtype) for s in shards],
        in_specs=[ANY] * nw, out_specs=[ANY] * nw,
        scratch_shapes=[sem, sem, sem, sem, pltpu.SemaphoreType.DMA((nw,))], name="gather_weights",
        compiler_params=pltpu.CompilerParams(has_side_effects=True))(*shards)


def swap_halves(grads):
    nw = len(grads)

    def body(*refs):
        src, kept, got = refs[:nw], refs[nw:2 * nw], refs[2 * nw:3 * nw]
        send, recv, loc = refs[3 * nw:]
        x, y, c = _place()
        cps = []
        for w in range(nw):
            h = src[w].shape[1] // 2
            cp = pltpu.make_async_copy(src[w].at[:, pl.ds(c * h, h), :], kept[w], loc.at[w])
            cp.start()
            cps.append(cp)
            cp = pltpu.make_async_remote_copy(src_ref=src[w].at[:, pl.ds((1 - c) * h, h), :], dst_ref=got[w],
                                              send_sem=send.at[w], recv_sem=recv.at[w],
                                              device_id=(x, y, 1 - c), device_id_type=MESH)
            cp.start()
            cps.append(cp)
        for cp in cps:
            cp.wait()

    half = [jax.ShapeDtypeStruct((g.shape[0], g.shape[1] // 2, g.shape[2]), g.dtype) for g in grads]
    sem = pltpu.SemaphoreType.DMA((nw,))
    outs = pl.pallas_call(
        body, out_shape=half + half, in_specs=[ANY] * nw, out_specs=[ANY] * (2 * nw),
        scratch_shapes=[sem, sem, sem], name="swap_halves",
        compiler_params=pltpu.CompilerParams(has_side_effects=True))(*grads)
    return outs[:nw], outs[nw:]


def scatter_to_owners(parts):
    nw = len(parts)

    def body(*refs):
        src, dst = refs[:nw], refs[nw:2 * nw]
        send, recv, loc = refs[2 * nw:]
        x, y, c = _place()
        mine = 2 * x + y
        cps = []
        for w in range(nw):
            cp = pltpu.make_async_copy(src[w].at[mine], dst[w].at[mine], loc.at[w])
            cp.start()
            cps.append(cp)
            for k, (px, py, s) in enumerate(_other_chips(x, y)):
                cp = pltpu.make_async_remote_copy(src_ref=src[w].at[s], dst_ref=dst[w].at[mine],
                                                  send_sem=send.at[w, k], recv_sem=recv.at[w, k],
                                                  device_id=(px, py, c), device_id_type=MESH)
                cp.start()
                cps.append(cp)
        for cp in cps:
            cp.wait()

    sem = pltpu.SemaphoreType.DMA((nw, 3))
    return pl.pallas_call(
        body, out_shape=[jax.ShapeDtypeStruct(p.shape, p.dtype) for p in parts], in_specs=[ANY] * nw, out_specs=[ANY] * nw,
        scratch_shapes=[sem, sem, pltpu.SemaphoreType.DMA((nw,))], name="scatter_to_owners",
        compiler_params=pltpu.CompilerParams(has_side_effects=True))(*parts)


def join_halves(halves):
    nw = len(halves)

    def body(*refs):
        src, dst = refs[:nw], refs[nw:2 * nw]
        send, recv, loc = refs[2 * nw:]
        x, y, c = _place()
        cps = []
        for w in range(nw):
            cp = pltpu.make_async_copy(src[w], dst[w].at[c], loc.at[w])
            cp.start()
            cps.append(cp)
            cp = pltpu.make_async_remote_copy(src_ref=src[w], dst_ref=dst[w].at[c], send_sem=send.at[w], recv_sem=recv.at[w],
                                              device_id=(x, y, 1 - c), device_id_type=MESH)
            cp.start()
            cps.append(cp)
        for cp in cps:
            cp.wait()

    sem = pltpu.SemaphoreType.DMA((nw,))
    return pl.pallas_call(
        body, out_shape=[jax.ShapeDtypeStruct((2,) + p.shape, p.dtype) for p in halves], in_specs=[ANY] * nw, out_specs=[ANY] * nw,
        scratch_shapes=[sem, sem, sem], name="join_halves",
        compiler_params=pltpu.CompilerParams(has_side_effects=True))(*halves)


def all_reduce_small(v):
    R, n = v.shape

    def body(v_ref, o_ref, all_ref, send_sems, recv_sems, local_sem):
        x, y, c = _place()
        me, sib = (x, y, c), (x, y, 1 - c)
        chips = [(1 - x, y), (x, 1 - y), (1 - x, 1 - y)]

        def rows(px, py, pc):
            return all_ref.at[pl.ds((4 * px + 2 * py + pc) * R, R), :]

        def copy(k, block, to, src=None):
            return pltpu.make_async_remote_copy(src_ref=rows(*block) if src is None else src, dst_ref=rows(*block),
                                                send_sem=send_sems.at[k], recv_sem=recv_sems.at[k],
                                                device_id=to, device_id_type=MESH)

        own = pltpu.make_async_copy(v_ref, rows(*me), local_sem)
        own.start()
        first = [copy(0, me, sib, src=v_ref)]
        first += [copy(1 + j, me, (*chip, c), src=v_ref) for j, chip in enumerate(chips)]
        for cp in first:
            cp.start()
        passed = [copy(4 + j, (*chip, c), sib) for j, chip in enumerate(chips)]
        for j, chip in enumerate(chips):
            copy(1 + j, (*chip, c), me).wait_recv()
            passed[j].start()
        copy(0, sib, me).wait_recv()
        for j, chip in enumerate(chips):
            copy(4 + j, (*chip, 1 - c), me).wait_recv()
        for cp in first + passed:
            cp.wait_send()
        own.wait()
        acc = all_ref[pl.ds(0, R), :]
        for d in range(1, N_DEV):
            acc = acc + all_ref[pl.ds(d * R, R), :]
        o_ref[...] = acc

    vm = pl.BlockSpec(memory_space=pltpu.VMEM)
    return pl.pallas_call(
        body, out_shape=jax.ShapeDtypeStruct((R, n), f32), in_specs=[vm], out_specs=vm,
        scratch_shapes=[pltpu.VMEM((N_DEV * R, n), f32), pltpu.SemaphoreType.DMA((7,)), pltpu.SemaphoreType.DMA((7,)),
                        pltpu.SemaphoreType.DMA],
        name="all_reduce_small", compiler_params=pltpu.CompilerParams(vmem_limit_bytes=VMEM_LIMIT, has_side_effects=True))(v)


def local_step(x, pos, tgt, small, big):
    L, D = x.shape
    d_kv = N_KV_HEADS * HEAD_DIM
    d_ssm = small["d_skip"].shape[1]
    d_attn = big["w_in"].shape[0] * big["w_in"].shape[2] - 2 * d_kv - d_ssm
    G = d_ssm // SSM_GROUP
    N, P = SSM_STATE, SSM_GROUP
    gbf = bf16

    half_dim = HEAD_DIM // 2
    inv_freq = ROPE_THETA ** (-jnp.arange(half_dim, dtype=f32) / half_dim)
    inv_freq = jnp.tile(inv_freq, LANES // half_dim).reshape(1, LANES)
    sink_b = jnp.broadcast_to(small["sinks"].reshape(-1, 1), (small["sinks"].size, LANES))

    spread = jnp.repeat(jnp.eye(G, dtype=f32), P, axis=0)
    logdt_b = jnp.broadcast_to(small["log_dt"].reshape(G, 1), (G, N))
    bt_re = small["b_re"].reshape(G, N, P).transpose(0, 2, 1).reshape(G * P, N)
    bt_im = small["b_im"].reshape(G, N, P).transpose(0, 2, 1).reshape(G * P, N)
    a_re, a_im = small["a_re"].reshape(G, N), small["a_im"].reshape(G, N)
    lam_re, lam_im, bbt_re, bbt_im = ssm_params(a_re, a_im, logdt_b, bt_re, bt_im, spread)
    bd_re = _block_diag(bbt_re.reshape(G, P, N), P, N).astype(bf16)
    bd_im = _block_diag(bbt_im.reshape(G, P, N), P, N).astype(bf16)
    c_re = small["c_re"].reshape(G, P, N).transpose(0, 2, 1)
    c_im = small["c_im"].reshape(G, P, N).transpose(0, 2, 1)
    cd_re = _block_diag(c_re, N, P).astype(bf16)
    cd_im = _block_diag(c_im, N, P).astype(bf16)
    lam_re_l, lam_im_l = _state_layout(lam_re), _state_layout(lam_im)

    def k1(i, nt, xt, g):
        return (rms_fwd(xt, g),)
    xn = rowwise("pre_mix_norm", k1, L, [full(x)], [small["g_pre_mix"]], [(D, bf16)])[0]
    proj = mm_nn("proj_in", xn, big["w_in"])
    qr, kk, vv, u_bf = qkv_prep(proj, pos, inv_freq, d_attn, d_kv)
    attn = attn_fwd(qr, kk, vv, sink_b)
    u_cb = (d_attn + 2 * d_kv) // (d_ssm // 2)
    y, s_re, s_im = ssm_fwd(u_bf, proj, u_cb, bd_re, bd_im, cd_re, cd_im, lam_re_l, lam_im_l, small["d_skip"])

    def k5(i, nt, yt):
        return (gelu(yt),)
    z_bf = rowwise("ssm_gelu", k5, L, [full(y)], [], [(d_ssm, bf16)])[0]
    gl = mm_nn("glu_proj", z_bf, big["w_glu"])

    def k6(i, nt, at, yt, glt, bg, ga, gs):
        ssm = gelu(yt) * sigmoid(glt + bg)
        return (jnp.concatenate([rms_fwd(at, ga), rms_fwd(ssm, gs)], axis=1),)
    mixed = rowwise("mix_norms", k6, L, [full(attn), full(y), full(gl)],
                    [small["b_glu"], small["g_attn_out"], small["g_ssm_out"]], [(d_attn + d_ssm, bf16)])[0]
    mix = mm_nn("proj_out", mixed, big["w_o"])

    def k7(i, nt, xt, mt, gpm, gpf):
        h = xt + rms_fwd(mt, gpm)
        return h, rms_fwd(h, gpf)
    h, hn = rowwise("post_mix", k7, L, [full(x), full(mix)], [small["g_post_mix"], small["g_pre_ffn"]], [(D, f32), (D, bf16)])
    gt = mm_nn("ffn_gate", hn, big["w_gate"])
    up = mm_nn("ffn_up", hn, big["w_up"])
    d_ff_dim = gt.shape[1]

    def k8(i, nt, g_t, u_t):
        return (g_t * sigmoid(g_t) * u_t,)
    hid = rowwise("ffn_act", k8, L, [full(gt), full(up)], [], [(d_ff_dim, bf16)], tr=64)[0]
    ff = mm_nn("ffn_down", hid, big["w_down"])

    def k9(i, nt, ht, fft, tt, g):
        out = ht + rms_fwd(fft, g)
        err = out - tt
        per_row = jnp.mean(err * err, axis=-1, keepdims=True)
        loss = 0.5 * jnp.sum(per_row) * jnp.where(_lane((1, LANES)) == 0, 1.0, 0.0)
        d_out = err * (1.0 / D)
        d_ff, dg = rms_bwd(fft, g, d_out)
        return d_out, d_ff, dg, loss
    d_out, d_ff, dg_post_ffn, loss = rowwise("loss_head", k9, L, [full(h), full(ff), full(tgt)], [small["g_post_ffn"]],
                                             [(D, f32), (D, bf16)], reds=[D, LANES])

    d_hid = mm_nt("d_ffn_hidden", d_ff, big["w_down"])
    dw_down = mm_tn("dw_down", hid, d_ff, out_dtype=gbf)

    def k10(i, nt, g_t, u_t, dh_t):
        sg = sigmoid(g_t)
        return dh_t * u_t * (sg * (1.0 + g_t * (1.0 - sg))), dh_t * (g_t * sg)
    d_gt, d_up = rowwise("ffn_act_grad", k10, L, [full(gt), full(up), full(d_hid)], [], [(d_ff_dim, bf16), (d_ff_dim, bf16)], tr=64)
    d_hn_a = mm_nt("d_hn_gate", d_gt, big["w_gate"])
    d_hn_b = mm_nt("d_hn_up", d_up, big["w_up"])
    dw_gate = mm_tn("dw_gate", hn, d_gt, shards=N_CHIPS, out_dtype=gbf)
    dw_up = mm_tn("dw_up", hn, d_up, shards=N_CHIPS, out_dtype=gbf)

    def k11(i, nt, ht, da, db, do, mt, gpf, gpm):
        dh_n, dg_pf = rms_bwd(ht, gpf, da + db)
        dh = do + dh_n
        d_mix, dg_pm = rms_bwd(mt, gpm, dh)
        return dh, d_mix, dg_pf, dg_pm
    dh, d_mix, dg_pre_ffn, dg_post_mix = rowwise("post_mix_grad", k11, L, [full(h), full(d_hn_a), full(d_hn_b), full(d_out), full(mix)],
                                                 [small["g_pre_ffn"], small["g_post_mix"]], [(D, f32), (D, bf16)], reds=[D, D])
    d_mixed = mm_nt("d_mixed", d_mix, big["w_o"])
    dw_o = mm_tn("dw_o", mixed, d_mix, out_dtype=gbf)

    def k12(i, nt, at, yt, glt, da_n, ds_n, bg, ga, gs):
        z = gelu(yt)
        sg = sigmoid(glt + bg)
        ssm = z * sg
        d_at, dga = rms_bwd(at, ga, da_n)
        d_ssm_t, dgs = rms_bwd(ssm, gs, ds_n)
        d_gl = d_ssm_t * z * sg * (1.0 - sg)
        return d_at, d_ssm_t * sg, d_gl, dga, dgs, colsum(d_gl)
    d_attn_o, dz1, d_gl, dg_attn, dg_ssm, db_glu = rowwise(
        "mix_norms_grad", k12, L, [full(attn), full(y), full(gl), (d_mixed, d_attn, 0, 0), (d_mixed, d_ssm, d_attn // d_ssm, 0)],
        [small["b_glu"], small["g_attn_out"], small["g_ssm_out"]], [(d_attn, f32), (d_ssm, f32), (d_ssm, bf16)],
        reds=[d_attn, d_ssm, d_ssm])
    dz2 = mm_nt("d_glu_in", d_gl, big["w_glu"])
    dw_glu = mm_tn("dw_glu", z_bf, d_gl, out_dtype=gbf)

    def k13(i, nt, yt, a, b):
        return ((a + b) * gelu_grad(yt),)
    dy = rowwise("ssm_gelu_grad", k13, L, [full(y), full(dz1), full(dz2)], [], [(d_ssm, f32)])[0]
    du, dbd_re, dbd_im, dcd_re, dcd_im, dlam_re_l, dlam_im_l, dd_skip = ssm_bwd(
        dy, u_bf, proj, u_cb, s_re, s_im, bd_re, bd_im, cd_re, cd_im, lam_re_l, lam_im_l, small["d_skip"])
    dq, dkk_c, dkk_p, dvv_c, dvv_p, dsink = attn_bwd(qr, kk, vv, sink_b, attn, d_attn_o)
    d_proj = qkv_grad(dq, dkk_c, dkk_p, dvv_c, dvv_p, du, pos, inv_freq)
    d_xn = mm_nt("d_xn", d_proj, big["w_in"])
    dw_in = mm_tn("dw_in", xn, d_proj, shards=N_CHIPS, out_dtype=gbf)

    def k17(i, nt, xt, dxn, dht, g):
        dx, dg = rms_bwd(xt, g, dxn)
        return dht + dx, dg
    grad_x, dg_pre_mix = rowwise("pre_mix_grad", k17, L, [full(x), full(d_xn), full(dh)], [small["g_pre_mix"]], [(D, f32)], reds=[D])

    gather = spread.T
    dbbt_re = _block_diag_take(dbd_re, P, N).reshape(G * P, N)
    dbbt_im = _block_diag_take(dbd_im, P, N).reshape(G * P, N)
    d_a_re, d_a_im, d_logdt, dbt_re, dbt_im = ssm_params_grad(
        a_re, a_im, logdt_b, bt_re, bt_im, spread, gather,
        _state_layout_inv(dlam_re_l, G, N), _state_layout_inv(dlam_im_l, G, N), dbbt_re, dbbt_im)
    q_per_kv = d_attn // HEAD_DIM // N_KV_HEADS
    small_grads = {
        "g_pre_mix": dg_pre_mix, "sinks": dsink[:, :q_per_kv, 0].reshape(1, -1),
        "a_re": d_a_re, "a_im": d_a_im, "log_dt": d_logdt.reshape(1, G),
        "b_re": dbt_re.reshape(G, P, N).transpose(0, 2, 1), "b_im": dbt_im.reshape(G, P, N).transpose(0, 2, 1),
        "c_re": _block_diag_take(dcd_re, N, P).transpose(0, 2, 1), "c_im": _block_diag_take(dcd_im, N, P).transpose(0, 2, 1),
        "d_skip": dd_skip, "b_glu": db_glu, "g_attn_out": dg_attn, "g_ssm_out": dg_ssm,
        "g_post_mix": dg_post_mix, "g_pre_ffn": dg_pre_ffn, "g_post_ffn": dg_post_ffn,
    }
    big_grads = {"w_in": dw_in, "w_glu": dw_glu, "w_o": dw_o, "w_gate": dw_gate, "w_up": dw_up, "w_down": dw_down}
    return loss, grad_x, small_grads, big_grads


WEIGHTS = ['g_pre_mix', 'w_in', 'sinks', 'a_re', 'a_im', 'log_dt', 'b_re', 'b_im', 'c_re', 'c_im', 'd_skip', 'w_glu', 'b_glu',
           'g_attn_out', 'g_ssm_out', 'w_o', 'g_post_mix', 'g_pre_ffn', 'w_gate', 'w_up', 'w_down', 'g_post_ffn']
BIG = ['w_in', 'w_glu', 'w_o', 'w_gate', 'w_up', 'w_down']
COL_SHARDED = ['w_in', 'w_gate', 'w_up']
SMALL = [n for n in WEIGHTS if n not in BIG]


PACK_ROWS = 256


def _pack(parts):
    flat = jnp.concatenate([p.reshape(-1) for p in parts])
    pad = (-flat.size) % (PACK_ROWS * LANES)
    return jnp.pad(flat, (0, pad)).reshape(-1, LANES)


def _unpack(packed, shapes):
    flat = packed.reshape(-1)
    out, off = [], 0
    for s in shapes:
        n = int(np.prod(s))
        out.append(flat[off:off + n].reshape(s))
        off += n
    return out


def kernel(x, positions, g_pre_mix, w_in, sinks, a_re, a_im, log_dt, b_re, b_im, c_re, c_im, d_skip, w_glu, b_glu, g_attn_out, g_ssm_out, w_o, g_post_mix, g_pre_ffn, w_gate, w_up, w_down, g_post_ffn, loss_target, m_g_pre_mix, m_w_in, m_sinks, m_a_re, m_a_im, m_log_dt, m_b_re, m_b_im, m_c_re, m_c_im, m_d_skip, m_w_glu, m_b_glu, m_g_attn_out, m_g_ssm_out, m_w_o, m_g_post_mix, m_g_pre_ffn, m_w_gate, m_w_up, m_w_down, m_g_post_ffn, v_g_pre_mix, v_w_in, v_sinks, v_a_re, v_a_im, v_log_dt, v_b_re, v_b_im, v_c_re, v_c_im, v_d_skip, v_w_glu, v_b_glu, v_g_attn_out, v_g_ssm_out, v_w_o, v_g_post_mix, v_g_pre_ffn, v_w_gate, v_w_up, v_w_down, v_g_post_ffn):
    args = dict(locals())
    w = {n: args[n] for n in WEIGHTS}
    m = {n: args["m_" + n] for n in WEIGHTS}
    v = {n: args["v_" + n] for n in WEIGHTS}
    L, D = x.shape[1], x.shape[2]

    shards = [cast_bf16("cast_" + n, w[n][0]) for n in BIG]
    gathered = gather_weights(shards)
    big = {}
    for n, g in zip(BIG, gathered):
        big[n] = g if n in COL_SHARDED else g.reshape(g.shape[0] * g.shape[1], g.shape[2])

    small = {n: w[n].reshape(1, -1) for n in SMALL}
    pos = positions.reshape(L, 1).astype(f32)
    loss, grad_x, small_grads, big_grads = local_step(x[0], pos, loss_target[0], small, big)
    loss = lax.psum(jnp.sum(loss), ("x", "y", "c"))

    g3 = []
    for n in BIG:
        g = big_grads[n]
        g3.append(g if n in COL_SHARDED else g.reshape(N_CHIPS, g.shape[0] // N_CHIPS, g.shape[1]))
    kept, got = swap_halves(g3)
    pair = [add_n("pair_sum_" + n, [k.reshape(-1, k.shape[2]), o.reshape(-1, o.shape[2])], bf16).reshape(k.shape)
            for n, k, o in zip(BIG, kept, got)]
    landed = scatter_to_owners(pair)
    halves = [sum_slabs("chip_sum_" + n, t) for n, t in zip(BIG, landed)]
    joined = join_halves(halves)
    big_g = {n: j.reshape(j.shape[0] * j.shape[1], j.shape[2]) for n, j in zip(BIG, joined)}

    shapes = [w[n].shape for n in SMALL]
    small_sum = all_reduce_small(_pack([small_grads[n] for n in SMALL]))
    small_g = dict(zip(SMALL, _unpack(small_sum, shapes)))

    grads, delta, new_m, new_v = {}, {}, {}, {}
    for n in BIG:
        grads[n] = big_g[n][None]
        d_, m_, v_ = adamw("adamw_" + n, w[n][0], big_g[n], m[n][0], v[n][0])
        delta[n], new_m[n], new_v[n] = d_[None], m_[None], v_[None]
    pw, pm, pv = (_pack([t[n] for n in SMALL]) for t in (w, m, v))
    d_, m_, v_ = adamw("adamw_small", pw, small_sum, pm, pv)
    for t, packed in ((delta, d_), (new_m, m_), (new_v, v_)):
        t.update(zip(SMALL, _unpack(packed, shapes)))
    grads.update(small_g)

    return (loss, grad_x[None], *[grads[n] for n in WEIGHTS], *[delta[n] for n in WEIGHTS],
            *[new_m[n] for n in WEIGHTS], *[new_v[n] for n in WEIGHTS])
```

```python
import functools
import math

import jax
import jax.numpy as jnp
import numpy as np
from jax import lax
from jax.experimental import pallas as pl
from jax.experimental.pallas import tpu as pltpu

f32 = jnp.float32
bf16 = jnp.bfloat16
HIGHEST = lax.Precision.HIGHEST
MESH = pl.DeviceIdType.MESH

HEAD_DIM = 64
N_KV_HEADS = 4
ATTN_BLOCK = 128
ROPE_THETA = 10000.0
SSM_GROUP = 16
SSM_STATE = 64
RMS_EPS = 1e-6
LANES = 128
SUBLANES = 8
VMEM_LIMIT = 52 * 1024 * 1024
N_CHIPS = 4
N_DEV = 8
NEG = -1e30

ADAM_LR, ADAM_B1, ADAM_B2, ADAM_EPS, ADAM_WD, ADAM_STEP = 0.001, 0.9, 0.999, 1e-08, 0.01, 10

NN = (((1,), (0,)), ((), ()))
NT = (((1,), (1,)), ((), ()))
TN = (((0,), (0,)), ((), ()))


def _params(*sem):
    return pltpu.CompilerParams(dimension_semantics=sem or None, vmem_limit_bytes=VMEM_LIMIT)


def _dot(a, b, dims=NN):
    return lax.dot_general(a, b, dims, preferred_element_type=f32)


def _pick(dim, pref):
    t = min(dim, pref)
    while dim % t:
        t -= LANES
    assert t > 0, (dim, pref)
    return t


def _mm_call(name, grid, in_specs, out_spec, out_shape, acc_shape, dims, operands):
    nk = grid[2]

    def body(a_ref, b_ref, o_ref, acc_ref):
        k = pl.program_id(2)

        @pl.when(k == 0)
        def _():
            acc_ref[...] = jnp.zeros_like(acc_ref)

        acc_ref[...] += _dot(a_ref[...], b_ref[...], dims)

        @pl.when(k == nk - 1)
        def _():
            o_ref[...] = acc_ref[...].astype(o_ref.dtype)

    return pl.pallas_call(
        body, out_shape=out_shape, grid=grid, in_specs=in_specs, out_specs=out_spec,
        scratch_shapes=[pltpu.VMEM(acc_shape, f32)], name=name,
        compiler_params=_params("parallel", "parallel", "arbitrary"))(*operands)


def mm_nn(name, a, b, out_dtype=f32, tm=1024, tn=1024, tk=1024):
    M, K = a.shape
    tm, tk = _pick(M, tm), _pick(K, tk)
    if b.ndim == 3:
        S, _, n = b.shape
        tn = _pick(n, 2048)
        per = n // tn
        b_spec = pl.BlockSpec((None, tk, tn), lambda i, j, k: (j // per, k, j % per))
        N = S * n
    else:
        N = b.shape[1]
        tn = _pick(N, tn)
        b_spec = pl.BlockSpec((tk, tn), lambda i, j, k: (k, j))
    grid = (M // tm, N // tn, K // tk)
    return _mm_call(name, grid, [pl.BlockSpec((tm, tk), lambda i, j, k: (i, k)), b_spec],
                    pl.BlockSpec((tm, tn), lambda i, j, k: (i, j)), jax.ShapeDtypeStruct((M, N), out_dtype),
                    (tm, tn), NN, (a, b))


def mm_nt(name, a, b, out_dtype=f32, tm=1024, tn=1024, tk=1024):
    M, N = a.shape
    tm = _pick(M, tm)
    if b.ndim == 3:
        S, K, n = b.shape
        tr = _pick(n, 2048)
        per = n // tr
        tko = _pick(K, tk)
        b_spec = pl.BlockSpec((None, tko, tr), lambda i, j, k: (k // per, j, k % per))
    else:
        K = b.shape[0]
        tr = _pick(N, tn)
        tko = _pick(K, tk)
        b_spec = pl.BlockSpec((tko, tr), lambda i, j, k: (j, k))
    grid = (M // tm, K // tko, N // tr)
    return _mm_call(name, grid, [pl.BlockSpec((tm, tr), lambda i, j, k: (i, k)), b_spec],
                    pl.BlockSpec((tm, tko), lambda i, j, k: (i, j)), jax.ShapeDtypeStruct((M, K), out_dtype),
                    (tm, tko), NT, (a, b))


def mm_tn(name, a, b, shards=None, out_dtype=f32, tm=1024, tn=1024, tl=1024):
    L, K = a.shape
    N = b.shape[1]
    tl, tko = _pick(L, tl), _pick(K, tm)
    if shards:
        n = N // shards
        tn = _pick(n, 2048)
        per = n // tn
        o_spec = pl.BlockSpec((None, tko, tn), lambda i, j, k: (j // per, i, j % per))
        o_shape = jax.ShapeDtypeStruct((shards, K, n), out_dtype)
    else:
        tn = _pick(N, tn)
        o_spec = pl.BlockSpec((tko, tn), lambda i, j, k: (i, j))
        o_shape = jax.ShapeDtypeStruct((K, N), out_dtype)
    grid = (K // tko, N // tn, L // tl)
    return _mm_call(name, grid, [pl.BlockSpec((tl, tko), lambda i, j, k: (k, i)),
                                 pl.BlockSpec((tl, tn), lambda i, j, k: (k, j))],
                    o_spec, o_shape, (tko, tn), TN, (a, b))


def rowwise(name, fn, L, rows, bcast, outs, reds=(), tr=128):
    tr = min(tr, L)
    nt = L // tr
    n_rows, n_b, n_o = len(rows), len(bcast), len(outs)

    def body(*refs):
        i = pl.program_id(0)
        ins = [r[...] for r in refs[:n_rows + n_b]]
        res = fn(i, nt, *ins)
        o_refs = refs[n_rows + n_b:]
        for k in range(n_o):
            o_refs[k][...] = res[k].astype(o_refs[k].dtype)
        if reds:
            @pl.when(i == 0)
            def _():
                for k in range(len(reds)):
                    o_refs[n_o + k][...] = jnp.zeros_like(o_refs[n_o + k])
            for k in range(len(reds)):
                o_refs[n_o + k][...] += res[n_o + k]

    def row_spec(width, cb, shift):
        if shift:
            return pl.BlockSpec((tr, width), lambda i: (jnp.minimum(i + shift, nt - 1), cb))
        return pl.BlockSpec((tr, width), lambda i: (i, cb))

    in_specs = [row_spec(w, cb, sh) for (_, w, cb, sh) in rows]
    in_specs += [pl.BlockSpec(b.shape, lambda i: (0, 0)) for b in bcast]
    out_specs = [pl.BlockSpec((tr, w), lambda i: (i, 0)) for (w, _) in outs]
    out_specs += [pl.BlockSpec((1, w), lambda i: (0, 0)) for w in reds]
    out_shape = [jax.ShapeDtypeStruct((L, w), dt) for (w, dt) in outs]
    out_shape += [jax.ShapeDtypeStruct((1, w), f32) for w in reds]
    return pl.pallas_call(
        body, out_shape=out_shape, grid=(nt,), in_specs=in_specs, out_specs=out_specs, name=name,
        compiler_params=_params("arbitrary"))(*[r[0] for r in rows], *bcast)


def full(a):
    return (a, a.shape[1], 0, 0)


def colsum(v):
    return jnp.sum(v, axis=0, keepdims=True)


def rms_fwd(x, g):
    r = lax.rsqrt(jnp.mean(x * x, axis=-1, keepdims=True) + RMS_EPS)
    return x * r * g


def rms_bwd(x, g, dy):
    r = lax.rsqrt(jnp.mean(x * x, axis=-1, keepdims=True) + RMS_EPS)
    xh = x * r
    dyg = dy * g
    dx = r * (dyg - xh * jnp.mean(dyg * xh, axis=-1, keepdims=True))
    return dx, colsum(dy * xh)


GELU_C = math.sqrt(2.0 / math.pi)


def gelu(y):
    return y * (0.5 * (1.0 + jnp.tanh(GELU_C * (y + 0.044715 * (y * y * y)))))


def gelu_grad(y):
    t = jnp.tanh(GELU_C * (y + 0.044715 * (y * y * y)))
    return 0.5 * (1.0 + t) + 0.5 * y * (1.0 - t * t) * (GELU_C * (1.0 + 3 * 0.044715 * (y * y)))


def sigmoid(v):
    return 1.0 / (1.0 + jnp.exp(-v))


def _lane(shape):
    return lax.broadcasted_iota(jnp.int32, shape, 1)


def _rot_chunk(t, cos, sin_signed):
    first = (_lane(t.shape) % HEAD_DIM) < (HEAD_DIM // 2)
    partner = jnp.where(first, pltpu.roll(t, LANES - HEAD_DIM // 2, 1), pltpu.roll(t, HEAD_DIM // 2, 1))
    return t * cos + partner * sin_signed


def _cos_sin(pos, inv_freq, inverse):
    ang = pos * inv_freq
    cos, sin = jnp.cos(ang), jnp.sin(ang)
    first = (_lane(ang.shape) % HEAD_DIM) < (HEAD_DIM // 2)
    sign = jnp.where(first, -1.0, 1.0) * (-1.0 if inverse else 1.0)
    return cos, sin * sign


def _dup_head(chunk, odd):
    low = _lane(chunk.shape) < HEAD_DIM
    x = jnp.where(low != odd, chunk, 0.0)
    return x + pltpu.roll(x, HEAD_DIM, 1)


def _chunks(v):
    return [v[:, LANES * c:LANES * (c + 1)] for c in range(v.shape[1] // LANES)]


def qkv_prep(proj, pos, inv_freq, d_attn, d_kv):
    L = proj.shape[0]
    d_ssm = proj.shape[1] - d_attn - 2 * d_kv
    half = d_ssm // 2
    scale = 1.0 / math.sqrt(HEAD_DIM)

    def fn(i, nt, q, k, v, u0, u1, p, invf):
        cos, sin = _cos_sin(p, invf, False)
        qr = jnp.concatenate([_rot_chunk(c, cos, sin) for c in _chunks(q)], axis=1) * scale
        kr = [_rot_chunk(c, cos, sin) for c in _chunks(k)]
        kk = jnp.concatenate([_dup_head(c, odd) for c in kr for odd in (False, True)], axis=1)
        vv = jnp.concatenate([_dup_head(c, odd) for c in _chunks(v) for odd in (False, True)], axis=1)
        return qr, kk, vv, jnp.concatenate([u0, u1], axis=1)

    u_cb = (d_attn + 2 * d_kv) // half
    return rowwise("qkv_prep", fn, L,
                   [(proj, d_attn, 0, 0), (proj, d_kv, d_attn // d_kv, 0), (proj, d_kv, d_attn // d_kv + 1, 0),
                    (proj, half, u_cb, 0), (proj, half, u_cb + 1, 0), full(pos)],
                   [inv_freq], [(d_attn, bf16), (2 * d_kv, bf16), (2 * d_kv, bf16), (d_ssm, bf16)])


def qkv_grad(dq, dkk_c, dkk_p, dvv_c, dvv_p, du, pos, inv_freq):
    L, d_attn = dq.shape
    d_kv = dkk_c.shape[1] // 2
    scale = 1.0 / math.sqrt(HEAD_DIM)

    def fold(cur, prev, i, nt):
        t = cur + jnp.where(i < nt - 1, prev, 0.0)
        out = []
        for c in range(t.shape[1] // (2 * LANES)):
            even, odd = t[:, 2 * c * LANES:(2 * c + 1) * LANES], t[:, (2 * c + 1) * LANES:(2 * c + 2) * LANES]
            even, odd = even + pltpu.roll(even, HEAD_DIM, 1), odd + pltpu.roll(odd, HEAD_DIM, 1)
            out.append(jnp.where(_lane(even.shape) < HEAD_DIM, even, odd))
        return out

    def fn(i, nt, dq_t, kc, kp, vc, vp, du_t, p, invf):
        cos, sin = _cos_sin(p, invf, True)
        dq_o = jnp.concatenate([_rot_chunk(c, cos, sin) for c in _chunks(dq_t)], axis=1) * scale
        dk_o = jnp.concatenate([_rot_chunk(c, cos, sin) for c in fold(kc, kp, i, nt)], axis=1)
        dv_o = jnp.concatenate(fold(vc, vp, i, nt), axis=1)
        return (jnp.concatenate([dq_o, dk_o, dv_o, du_t], axis=1),)

    return rowwise("qkv_grad", fn, L,
                   [full(dq), full(dkk_c), (dkk_p, 2 * d_kv, 0, 1), full(dvv_c), (dvv_p, 2 * d_kv, 0, 1), full(du), full(pos)],
                   [inv_freq], [(d_attn + 2 * d_kv + du.shape[1], bf16)], tr=ATTN_BLOCK)[0]


def _attn_specs(L):
    nb = L // ATTN_BLOCK
    B = ATTN_BLOCK
    q_spec = pl.BlockSpec((B, 2 * LANES), lambda h, n: (n, h))
    cur = pl.BlockSpec((B, LANES), lambda h, n: (n, h))
    prev = pl.BlockSpec((B, LANES), lambda h, n: (jnp.maximum(n - 1, 0), h))
    return nb, q_spec, cur, prev


def _attn_probs(qm, kc, kp, sink, n):
    B = ATTN_BLOCK
    row = lax.broadcasted_iota(jnp.int32, (B, B), 0)
    col = lax.broadcasted_iota(jnp.int32, (B, B), 1)
    sc = jnp.where(row >= col, _dot(qm, kc, NT), NEG)
    sp = jnp.where((col > row) & (n > 0), _dot(qm, kp, NT), NEG)
    m = jnp.maximum(jnp.maximum(jnp.max(sc, axis=1, keepdims=True), jnp.max(sp, axis=1, keepdims=True)), sink)
    pc, pp, ps = jnp.exp(sc - m), jnp.exp(sp - m), jnp.exp(sink - m)
    inv = 1.0 / (jnp.sum(pc, axis=1, keepdims=True) + jnp.sum(pp, axis=1, keepdims=True) + ps)
    return pc, pp, inv, ps


def attn_fwd(qr, kk, vv, sink_b):
    L, d_attn = qr.shape
    nb, q_spec, cur, prev = _attn_specs(L)
    n_kv = kk.shape[1] // LANES
    q_per_kv = d_attn // HEAD_DIM // n_kv

    def body(q_ref, kc_ref, kp_ref, vc_ref, vp_ref, s_ref, o_ref):
        h, n = pl.program_id(0), pl.program_id(1)
        low = _lane((ATTN_BLOCK, LANES)) < HEAD_DIM
        kc, kp, vc, vp = kc_ref[...], kp_ref[...], vc_ref[...], vp_ref[...]
        for pr in range(q_per_kv // 2):
            q2 = q_ref[:, LANES * pr:LANES * (pr + 1)]
            o2 = jnp.zeros((ATTN_BLOCK, LANES), f32)
            for odd in (False, True):
                mine = low != odd
                qm = jnp.where(mine, q2, jnp.zeros_like(q2))
                sink = jnp.max(s_ref[pl.ds(h * q_per_kv + 2 * pr + int(odd), 1), :], axis=1, keepdims=True)
                pc, pp, inv, _ = _attn_probs(qm, kc, kp, sink, n)
                o = _dot(pc.astype(bf16), vc) + _dot(pp.astype(bf16), vp)
                o2 = o2 + jnp.where(mine, o * inv, 0.0)
            o_ref[:, LANES * pr:LANES * (pr + 1)] = o2

    return pl.pallas_call(
        body, out_shape=jax.ShapeDtypeStruct((L, d_attn), f32), grid=(n_kv, nb),
        in_specs=[q_spec, cur, prev, cur, prev, pl.BlockSpec(sink_b.shape, lambda h, n: (0, 0))],
        out_specs=q_spec, name="attn_fwd", compiler_params=_params("parallel", "arbitrary"))(qr, kk, kk, vv, vv, sink_b)


def attn_bwd(qr, kk, vv, sink_b, attn, d_attn_out):
    L, d_attn = qr.shape
    nb, q_spec, cur, prev = _attn_specs(L)
    n_kv = kk.shape[1] // LANES
    q_per_kv = d_attn // HEAD_DIM // n_kv

    def body(q_ref, kc_ref, kp_ref, vc_ref, vp_ref, s_ref, o_ref, do_ref, dq_ref, dkc_ref, dkp_ref, dvc_ref, dvp_ref, ds_ref):
        h, n = pl.program_id(0), pl.program_id(1)
        B = ATTN_BLOCK
        low = _lane((B, LANES)) < HEAD_DIM
        kc, kp, vc, vp = kc_ref[...], kp_ref[...], vc_ref[...], vp_ref[...]
        dkc = jnp.zeros((B, LANES), f32)
        dkp, dvc, dvp = dkc, dkc, dkc
        srow = lax.broadcasted_iota(jnp.int32, (SUBLANES, LANES), 0)
        dsink = jnp.zeros((SUBLANES, LANES), f32)
        for pr in range(q_per_kv // 2):
            q2 = q_ref[:, LANES * pr:LANES * (pr + 1)]
            do2 = do_ref[:, LANES * pr:LANES * (pr + 1)]
            prod = do2 * o_ref[:, LANES * pr:LANES * (pr + 1)]
            dq2 = jnp.zeros((B, LANES), f32)
            for odd in (False, True):
                mine = low != odd
                qm = jnp.where(mine, q2, jnp.zeros_like(q2))
                sink = jnp.max(s_ref[pl.ds(h * q_per_kv + 2 * pr + int(odd), 1), :], axis=1, keepdims=True)
                pc, pp, inv, ps = _attn_probs(qm, kc, kp, sink, n)
                pc, pp = pc * inv, pp * inv
                delta = jnp.sum(jnp.where(mine, prod, 0.0), axis=1, keepdims=True)
                dob = jnp.where(mine, do2, 0.0).astype(bf16)
                dsc = (pc * (_dot(dob, vc, NT) - delta)).astype(bf16)
                dsp = (pp * (_dot(dob, vp, NT) - delta)).astype(bf16)
                dq2 = dq2 + jnp.where(mine, _dot(dsc, kc) + _dot(dsp, kp), 0.0)
                dkc = dkc + _dot(dsc, qm, TN)
                dkp = dkp + _dot(dsp, qm, TN)
                dvc = dvc + _dot(pc.astype(bf16), dob, TN)
                dvp = dvp + _dot(pp.astype(bf16), dob, TN)
                dsink = dsink + jnp.where(srow == 2 * pr + int(odd), -jnp.sum(ps * inv * delta), 0.0)
            dq_ref[:, LANES * pr:LANES * (pr + 1)] = dq2
        dkc_ref[...] = dkc
        dkp_ref[...] = dkp
        dvc_ref[...] = dvc
        dvp_ref[...] = dvp

        @pl.when(n == 0)
        def _():
            ds_ref[...] = jnp.zeros_like(ds_ref)

        ds_ref[...] += dsink

    kv_shape = jax.ShapeDtypeStruct(kk.shape, f32)
    return pl.pallas_call(
        body,
        out_shape=[jax.ShapeDtypeStruct((L, d_attn), f32), kv_shape, kv_shape, kv_shape, kv_shape,
                   jax.ShapeDtypeStruct((n_kv, SUBLANES, LANES), f32)],
        grid=(n_kv, nb),
        in_specs=[q_spec, cur, prev, cur, prev, pl.BlockSpec(sink_b.shape, lambda h, n: (0, 0)), q_spec, q_spec],
        out_specs=[q_spec, cur, cur, cur, cur, pl.BlockSpec((None, SUBLANES, LANES), lambda h, n: (h, 0, 0))],
        name="attn_bwd", compiler_params=_params("parallel", "arbitrary"))(qr, kk, kk, vv, vv, sink_b, attn, d_attn_out)


SSM_T = 128
NQ = SUBLANES * SSM_STATE // LANES
NJ = SUBLANES


def _strided_put(ref, j, val):
    for q in range(NQ):
        ref.at[q][pl.ds(j, SSM_T, stride=NJ), :] = val[:, LANES * q:LANES * (q + 1)]


def _strided_get(ref, j):
    return jnp.concatenate([ref.at[q][pl.ds(j, SSM_T, stride=NJ), :] for q in range(NQ)], axis=1)


def _ssm_specs(L, rev):
    nt = L // SSM_T
    idx = (lambda i: nt - 1 - i) if rev else (lambda i: i)
    row = lambda w, cb=0: pl.BlockSpec((SSM_T, w), lambda i: (idx(i), cb))
    state = pl.BlockSpec((NQ, SSM_T * NJ, LANES), lambda i: (0, idx(i), 0))
    whole = lambda a: pl.BlockSpec(a.shape, lambda i: (0,) * a.ndim)
    return nt, row, state, whole


def ssm_fwd(u_bf, proj, u_cb, bd_re, bd_im, cd_re, cd_im, lam_re, lam_im, d_skip):
    L, d_ssm = u_bf.shape
    nt, row, state, whole = _ssm_specs(L, False)
    half = d_ssm // 2
    gw = d_ssm // NJ

    def body(u_ref, u0_ref, u1_ref, bdr, bdi, cdr, cdi, lr_ref, li_ref, d_ref, y_ref, sr_ref, si_ref, carry):
        i = pl.program_id(0)

        @pl.when(i == 0)
        def _():
            carry[...] = jnp.zeros_like(carry)

        for j in range(NJ):
            uj = u_ref[:, gw * j:gw * (j + 1)]
            _strided_put(sr_ref, j, _dot(uj, bdr[j]))
            _strided_put(si_ref, j, _dot(uj, bdi[j]))
        lr = [lr_ref[q] for q in range(NQ)]
        li = [li_ref[q] for q in range(NQ)]

        def step(t, s):
            sr, si = s
            rows = pl.ds(pl.multiple_of(t * NJ, NJ), NJ)
            nr = tuple(lr[q] * sr[q] - li[q] * si[q] + sr_ref[q, rows, :] for q in range(NQ))
            ni = tuple(lr[q] * si[q] + li[q] * sr[q] + si_ref[q, rows, :] for q in range(NQ))
            for q in range(NQ):
                sr_ref[q, rows, :] = nr[q]
                si_ref[q, rows, :] = ni[q]
            return nr, ni

        init = (tuple(carry[0, q] for q in range(NQ)), tuple(carry[1, q] for q in range(NQ)))
        sr, si = lax.fori_loop(0, SSM_T, step, init, unroll=8)
        for q in range(NQ):
            carry[0, q] = sr[q]
            carry[1, q] = si[q]
        uf = jnp.concatenate([u0_ref[...], u1_ref[...]], axis=1)
        for j in range(NJ):
            cols = slice(gw * j, gw * (j + 1))
            yj = _dot(_strided_get(sr_ref, j).astype(bf16), cdr[j]) - _dot(_strided_get(si_ref, j).astype(bf16), cdi[j])
            y_ref[:, cols] = yj + d_ref[:, cols] * uf[:, cols]

    s_shape = jax.ShapeDtypeStruct((NQ, L * NJ, LANES), f32)
    consts = (bd_re, bd_im, cd_re, cd_im, lam_re, lam_im, d_skip)
    return pl.pallas_call(
        body, out_shape=[jax.ShapeDtypeStruct((L, d_ssm), f32), s_shape, s_shape], grid=(nt,),
        in_specs=[row(d_ssm), row(half, u_cb), row(half, u_cb + 1)] + [whole(a) for a in consts],
        out_specs=[row(d_ssm), state, state],
        scratch_shapes=[pltpu.VMEM((2, NQ, NJ, LANES), f32)], name="ssm_fwd",
        compiler_params=_params("arbitrary"))(u_bf, proj, proj, *consts)


def ssm_bwd(dy, u_bf, proj, u_cb, s_re, s_im, bd_re, bd_im, cd_re, cd_im, lam_re, lam_im, d_skip):
    L, d_ssm = dy.shape
    nt, row, state, whole = _ssm_specs(L, True)
    half = d_ssm // 2
    gw = d_ssm // NJ

    def body(dy_ref, u_ref, u0_ref, u1_ref, sr_ref, si_ref, bdr, bdi, cdr, cdi, lr_ref, li_ref, d_ref,
             du_ref, dbdr, dbdi, dcdr, dcdi, dlr, dli, dd_ref, gr_ref, gi_ref, carry):
        i = pl.program_id(0)

        @pl.when(i == 0)
        def _():
            carry[...] = jnp.zeros_like(carry)
            for r in (dbdr, dbdi, dcdr, dcdi, dlr, dli, dd_ref):
                r[...] = jnp.zeros_like(r)

        dyf = dy_ref[...]
        dyb = dyf.astype(bf16)
        for j in range(NJ):
            dyj = dyb[:, gw * j:gw * (j + 1)]
            _strided_put(gr_ref, j, _dot(dyj, cdr[j], NT))
            _strided_put(gi_ref, j, -_dot(dyj, cdi[j], NT))
            dcdr[j] += _dot(_strided_get(sr_ref, j).astype(bf16), dyj, TN)
            dcdi[j] -= _dot(_strided_get(si_ref, j).astype(bf16), dyj, TN)
        lr = [lr_ref[q] for q in range(NQ)]
        li = [li_ref[q] for q in range(NQ)]

        def step(k, c):
            gr, gi, ar, ai = c
            rows = pl.ds(pl.multiple_of((SSM_T - 1 - k) * NJ, NJ), NJ)
            s_r = [sr_ref[q, rows, :] for q in range(NQ)]
            s_i = [si_ref[q, rows, :] for q in range(NQ)]
            ar = tuple(ar[q] + gr[q] * s_r[q] + gi[q] * s_i[q] for q in range(NQ))
            ai = tuple(ai[q] + gi[q] * s_r[q] - gr[q] * s_i[q] for q in range(NQ))
            nr = tuple(gr_ref[q, rows, :] + lr[q] * gr[q] + li[q] * gi[q] for q in range(NQ))
            ni = tuple(gi_ref[q, rows, :] + lr[q] * gi[q] - li[q] * gr[q] for q in range(NQ))
            for q in range(NQ):
                gr_ref[q, rows, :] = nr[q]
                gi_ref[q, rows, :] = ni[q]
            return nr, ni, ar, ai

        zero = tuple(jnp.zeros((NJ, LANES), f32) for _ in range(NQ))
        init = (tuple(carry[0, q] for q in range(NQ)), tuple(carry[1, q] for q in range(NQ)), zero, zero)
        gr, gi, ar, ai = lax.fori_loop(0, SSM_T, step, init, unroll=8)
        for q in range(NQ):
            carry[0, q] = gr[q]
            carry[1, q] = gi[q]
            dlr[q] += ar[q]
            dli[q] += ai[q]
        uf = jnp.concatenate([u0_ref[...], u1_ref[...]], axis=1)
        dd_ref[...] += colsum(dyf * uf)
        for j in range(NJ):
            cols = slice(gw * j, gw * (j + 1))
            gjr, gji = _strided_get(gr_ref, j).astype(bf16), _strided_get(gi_ref, j).astype(bf16)
            du_ref[:, cols] = _dot(gjr, bdr[j], NT) + _dot(gji, bdi[j], NT) + d_ref[:, cols] * dyf[:, cols]
            uj = u_ref[:, cols]
            dbdr[j] += _dot(uj, gjr, TN)
            dbdi[j] += _dot(uj, gji, TN)

    consts = (bd_re, bd_im, cd_re, cd_im, lam_re, lam_im, d_skip)
    acc = lambda a: jax.ShapeDtypeStruct(a.shape, f32)
    outs = [jax.ShapeDtypeStruct((L, d_ssm), f32), acc(bd_re), acc(bd_im), acc(cd_re), acc(cd_im), acc(lam_re), acc(lam_im), acc(d_skip)]
    return pl.pallas_call(
        body, out_shape=outs, grid=(nt,),
        in_specs=[row(d_ssm), row(d_ssm), row(half, u_cb), row(half, u_cb + 1), state, state] + [whole(a) for a in consts],
        out_specs=[row(d_ssm)] + [whole(a) for a in consts],
        scratch_shapes=[pltpu.VMEM((NQ, SSM_T * NJ, LANES), f32), pltpu.VMEM((NQ, SSM_T * NJ, LANES), f32),
                        pltpu.VMEM((2, NQ, NJ, LANES), f32)],
        name="ssm_bwd", compiler_params=_params("arbitrary"))(dy, u_bf, proj, proj, s_re, s_im, *consts)


def _cmul(ar, ai, br, bi):
    return ar * br - ai * bi, ar * bi + ai * br


def _disc(ar, ai, logdt):
    dt = jnp.exp(logdt)
    mag = jnp.exp(ar * dt)
    lr, li = mag * jnp.cos(ai * dt), mag * jnp.sin(ai * dt)
    den = ar * ar + ai * ai
    nr, ni = lr - 1.0, li
    fr, fi = (nr * ar + ni * ai) / den, (ni * ar - nr * ai) / den
    return dt, lr, li, den, fr, fi


def ssm_params(a_re, a_im, logdt_b, bt_re, bt_im, spread):
    def body(ar_ref, ai_ref, ld_ref, br_ref, bi_ref, sp_ref, lr_ref, li_ref, or_ref, oi_ref):
        _, lr, li, _, fr, fi = _disc(ar_ref[...], ai_ref[...], ld_ref[...])
        lr_ref[...] = lr
        li_ref[...] = li
        fre = jnp.dot(sp_ref[...], fr, precision=HIGHEST, preferred_element_type=f32)
        fie = jnp.dot(sp_ref[...], fi, precision=HIGHEST, preferred_element_type=f32)
        o_r, o_i = _cmul(fre, fie, br_ref[...], bi_ref[...])
        or_ref[...] = o_r
        oi_ref[...] = o_i

    g = jax.ShapeDtypeStruct(a_re.shape, f32)
    b = jax.ShapeDtypeStruct(bt_re.shape, f32)
    return pl.pallas_call(body, out_shape=[g, g, b, b], name="ssm_params",
                          compiler_params=_params())(a_re, a_im, logdt_b, bt_re, bt_im, spread)


def ssm_params_grad(a_re, a_im, logdt_b, bt_re, bt_im, spread, gather, dlam_re, dlam_im, dbt_re, dbt_im):
    def body(ar_ref, ai_ref, ld_ref, br_ref, bi_ref, sp_ref, ga_ref, glr_ref, gli_ref, gbr_ref, gbi_ref,
             dar_ref, dai_ref, dld_ref, dbr_ref, dbi_ref):
        ar, ai = ar_ref[...], ai_ref[...]
        dt, lr, li, den, fr, fi = _disc(ar, ai, ld_ref[...])
        hdot = functools.partial(jnp.dot, precision=HIGHEST, preferred_element_type=f32)
        fre, fie = hdot(sp_ref[...], fr), hdot(sp_ref[...], fi)
        gbr, gbi, br, bi = gbr_ref[...], gbi_ref[...], br_ref[...], bi_ref[...]
        dbr_ref[...], dbi_ref[...] = _cmul(fre, -fie, gbr, gbi)
        t_r, t_i = _cmul(br, -bi, gbr, gbi)
        gfr, gfi = hdot(ga_ref[...], t_r), hdot(ga_ref[...], t_i)
        iwr, iwi = ar / den, -ai / den
        x_r, x_i = _cmul(iwr, -iwi, gfr, gfi)
        glr, gli = glr_ref[...] + x_r, gli_ref[...] + x_i
        q_r, q_i = _cmul(fr, fi, iwr, iwi)
        gwr, gwi = _cmul(-q_r, q_i, gfr, gfi)
        y_r, y_i = _cmul(dt * lr, -dt * li, glr, gli)
        dar_ref[...] = gwr + y_r
        dai_ref[...] = gwi + y_i
        wl_r, wl_i = _cmul(ar, ai, lr, li)
        z_r, _ = _cmul(wl_r, -wl_i, glr, gli)
        dld_ref[...] = jnp.sum(z_r * dt, axis=1, keepdims=True)

    g = jax.ShapeDtypeStruct(a_re.shape, f32)
    b = jax.ShapeDtypeStruct(bt_re.shape, f32)
    return pl.pallas_call(body, out_shape=[g, g, jax.ShapeDtypeStruct((a_re.shape[0], 1), f32), b, b], name="ssm_params_grad",
                          compiler_params=_params())(a_re, a_im, logdt_b, bt_re, bt_im, spread, gather, dlam_re, dlam_im, dbt_re, dbt_im)


def _block_diag(t, rows, cols):
    G = t.shape[0]
    t = t.reshape(G // NJ, NJ, rows, cols)
    eye = jnp.eye(NJ, dtype=t.dtype)
    return jnp.einsum('jgrc,gh->jgrhc', t, eye).reshape(G // NJ, NJ * rows, NJ * cols)


def _block_diag_take(m, rows, cols):
    J = m.shape[0]
    m = m.reshape(J, NJ, rows, NJ, cols)
    idx = jnp.arange(NJ)
    return m[:, idx, :, idx, :].transpose(1, 0, 2, 3).reshape(J * NJ, rows, cols)


def _state_layout(t):
    return t.reshape(NJ, NQ, LANES).transpose(1, 0, 2)


def _state_layout_inv(t, G, N):
    return t.transpose(1, 0, 2).reshape(G, N)


def _tiles2d(shape, budget_rows=128):
    rows, cols = shape
    tr = rows
    if rows > budget_rows:
        tr = budget_rows
        while rows % tr:
            tr -= SUBLANES
    return tr, cols


def _adam_update(w, g, m, v):
    c1 = 1.0 - ADAM_B1 ** ADAM_STEP
    c2 = 1.0 - ADAM_B2 ** ADAM_STEP
    nm = ADAM_B1 * m + (1.0 - ADAM_B1) * g
    nv = ADAM_B2 * v + (1.0 - ADAM_B2) * (g * g)
    delta = -ADAM_LR * ((nm / c1) / (jnp.sqrt(nv / c2) + ADAM_EPS) + ADAM_WD * w)
    return delta, nm, nv


def adamw(name, w, g, m, v):
    tr, cols = _tiles2d(w.shape, 128 if w.shape[1] > 1024 else 256)

    def body(w_ref, g_ref, m_ref, v_ref, d_ref, nm_ref, nv_ref):
        d_ref[...], nm_ref[...], nv_ref[...] = _adam_update(w_ref[...], g_ref[...], m_ref[...], v_ref[...])

    spec = pl.BlockSpec((tr, cols), lambda i: (i, 0))
    o = jax.ShapeDtypeStruct(w.shape, f32)
    return pl.pallas_call(body, out_shape=[o, o, o], grid=(w.shape[0] // tr,), in_specs=[spec] * 4, out_specs=[spec] * 3,
                          name=name, compiler_params=_params("parallel"))(w, g, m, v)


def adamw_halves(name, w, own, got, m, v):
    h, cols = own.shape
    tr, _ = _tiles2d((h, cols), 128 if cols > 1024 else 256)
    nh = h // tr

    def body(w_ref, own_ref, got_ref, m_ref, v_ref, g_ref, d_ref, nm_ref, nv_ref):
        mine = (pl.program_id(0) // nh) == lax.axis_index("c")
        g = jnp.where(mine, own_ref[...], got_ref[...])
        g_ref[...] = g
        d_ref[...], nm_ref[...], nv_ref[...] = _adam_update(w_ref[...], g, m_ref[...], v_ref[...])

    spec = pl.BlockSpec((tr, cols), lambda i: (i, 0))
    half = pl.BlockSpec((tr, cols), lambda i: (i % nh, 0))
    o = jax.ShapeDtypeStruct(w.shape, f32)
    return pl.pallas_call(body, out_shape=[o, o, o, o], grid=(2 * nh,), in_specs=[spec, half, half, spec, spec], out_specs=[spec] * 4,
                          name=name, compiler_params=_params("parallel"))(w, own, got, m, v)


def pair_sum(name, g, got, c_arr):
    S, h, cols = got.shape
    tr, _ = _tiles2d((h, cols), 256)
    nh = h // tr

    def body(c_ref, g_ref, o_ref, out_ref):
        out_ref[...] = (g_ref[...].astype(f32) + o_ref[...].astype(f32)).astype(out_ref.dtype)

    spec = pl.BlockSpec((None, tr, cols), lambda s, i, c: (s, i, 0))
    grid_spec = pltpu.PrefetchScalarGridSpec(
        num_scalar_prefetch=1, grid=(S, nh),
        in_specs=[pl.BlockSpec((None, tr, cols), lambda s, i, c: (s, c[0] * nh + i, 0)), spec], out_specs=spec)
    return pl.pallas_call(body, out_shape=jax.ShapeDtypeStruct(got.shape, g.dtype), grid_spec=grid_spec, name=name,
                          compiler_params=_params("parallel", "parallel"))(c_arr, g, got)


def chip_sum(name, pair, landed, mine_arr):
    n_in, h, cols = landed.shape
    tr, _ = _tiles2d((h, cols), 256)

    def body(s_ref, p_ref, l_ref, o_ref):
        acc = p_ref[...].astype(f32)
        for k in range(n_in):
            acc = acc + l_ref[k].astype(f32)
        o_ref[...] = acc

    grid_spec = pltpu.PrefetchScalarGridSpec(
        num_scalar_prefetch=1, grid=(h // tr,),
        in_specs=[pl.BlockSpec((None, tr, cols), lambda i, s: (s[0], i, 0)), pl.BlockSpec((n_in, tr, cols), lambda i, s: (0, i, 0))],
        out_specs=pl.BlockSpec((tr, cols), lambda i, s: (i, 0)))
    return pl.pallas_call(body, out_shape=jax.ShapeDtypeStruct((h, cols), f32), grid_spec=grid_spec, name=name,
                          compiler_params=_params("parallel"))(mine_arr, pair, landed)


def cast_into_slot(name, w, mine_arr):
    tr, cols = _tiles2d(w.shape, 256)

    def body(s_ref, w_ref, o_ref):
        o_ref[...] = w_ref[...].astype(bf16)

    grid_spec = pltpu.PrefetchScalarGridSpec(
        num_scalar_prefetch=1, grid=(w.shape[0] // tr,),
        in_specs=[pl.BlockSpec((tr, cols), lambda i, s: (i, 0))],
        out_specs=pl.BlockSpec((None, tr, cols), lambda i, s: (s[0], i, 0)))
    return pl.pallas_call(body, out_shape=jax.ShapeDtypeStruct((N_CHIPS,) + w.shape, bf16), grid_spec=grid_spec, name=name,
                          compiler_params=_params("parallel"))(mine_arr, w)


ANY = pl.BlockSpec(memory_space=pl.ANY)


def _place():
    x, y, c = lax.axis_index("x"), lax.axis_index("y"), lax.axis_index("c")
    return x, y, c


def _other_chips(x, y):
    return [(1 - x, y, 2 * (1 - x) + y), (x, 1 - y, 2 * x + 1 - y), (1 - x, 1 - y, 2 * (1 - x) + 1 - y)]


def gather_weights(bufs):
    nw = len(bufs)

    def body(*refs):
        dst = refs[nw:2 * nw]
        send1, recv1, send2, recv2 = refs[2 * nw:]
        x, y, c = _place()
        mine = 2 * x + y
        chips = _other_chips(x, y)
        sib = (x, y, 1 - c)
        first, passed = [], []
        for w in range(nw):
            h = dst[w].shape[1] // 2
            for k, (px, py, _) in enumerate(chips):
                half = dst[w].at[mine, pl.ds(c * h, h), :]
                cp = pltpu.make_async_remote_copy(src_ref=half, dst_ref=half,
                                                  send_sem=send1.at[w, k], recv_sem=recv1.at[w, k],
                                                  device_id=(px, py, c), device_id_type=MESH)
                cp.start()
                first.append(cp)
        for w in range(nw):
            h = dst[w].shape[1] // 2
            for k, (px, py, s) in enumerate(chips):
                landed = dst[w].at[s, pl.ds(c * h, h), :]
                pltpu.make_async_remote_copy(src_ref=landed, dst_ref=landed, send_sem=send1.at[w, k], recv_sem=recv1.at[w, k],
                                             device_id=(px, py, c), device_id_type=MESH).wait_recv()
                cp = pltpu.make_async_remote_copy(src_ref=landed, dst_ref=landed, send_sem=send2.at[w, k], recv_sem=recv2.at[w, k],
                                                  device_id=sib, device_id_type=MESH)
                cp.start()
                passed.append(cp)
        for w in range(nw):
            h = dst[w].shape[1] // 2
            for k, (px, py, s) in enumerate(chips):
                other = dst[w].at[s, pl.ds((1 - c) * h, h), :]
                pltpu.make_async_remote_copy(src_ref=other, dst_ref=other, send_sem=send2.at[w, k], recv_sem=recv2.at[w, k],
                                             device_id=sib, device_id_type=MESH).wait_recv()
        for cp in first + passed:
            cp.wait_send()

    sem = pltpu.SemaphoreType.DMA((nw, 3))
    return pl.pallas_call(
        body, out_shape=[jax.ShapeDtypeStruct(b.shape, b.dtype) for b in bufs],
        in_specs=[ANY] * nw, out_specs=[ANY] * nw, input_output_aliases={w: w for w in range(nw)},
        scratch_shapes=[sem, sem, sem, sem], name="gather_weights",
        compiler_params=pltpu.CompilerParams(has_side_effects=True))(*bufs)


def swap_halves(grads):
    nw = len(grads)

    def body(*refs):
        src, got = refs[:nw], refs[nw:2 * nw]
        send, recv = refs[2 * nw:]
        x, y, c = _place()
        cps = []
        for w in range(nw):
            h = src[w].shape[1] // 2
            cp = pltpu.make_async_remote_copy(src_ref=src[w].at[:, pl.ds((1 - c) * h, h), :], dst_ref=got[w],
                                              send_sem=send.at[w], recv_sem=recv.at[w],
                                              device_id=(x, y, 1 - c), device_id_type=MESH)
            cp.start()
            cps.append(cp)
        for cp in cps:
            cp.wait()

    half = [jax.ShapeDtypeStruct((g.shape[0], g.shape[1] // 2, g.shape[2]), g.dtype) for g in grads]
    sem = pltpu.SemaphoreType.DMA((nw,))
    return pl.pallas_call(
        body, out_shape=half, in_specs=[ANY] * nw, out_specs=[ANY] * nw,
        scratch_shapes=[sem, sem], name="swap_halves",
        compiler_params=pltpu.CompilerParams(has_side_effects=True))(*grads)


def scatter_to_owners(parts):
    nw = len(parts)

    def body(*refs):
        src, dst = refs[:nw], refs[nw:2 * nw]
        send, recv = refs[2 * nw:]
        x, y, c = _place()
        cps = []
        for w in range(nw):
            for k, (px, py, s) in enumerate(_other_chips(x, y)):
                cp = pltpu.make_async_remote_copy(src_ref=src[w].at[s], dst_ref=dst[w].at[k],
                                                  send_sem=send.at[w, k], recv_sem=recv.at[w, k],
                                                  device_id=(px, py, c), device_id_type=MESH)
                cp.start()
                cps.append(cp)
        for cp in cps:
            cp.wait()

    sem = pltpu.SemaphoreType.DMA((nw, 3))
    return pl.pallas_call(
        body, out_shape=[jax.ShapeDtypeStruct((N_CHIPS - 1,) + p.shape[1:], p.dtype) for p in parts],
        in_specs=[ANY] * nw, out_specs=[ANY] * nw,
        scratch_shapes=[sem, sem], name="scatter_to_owners",
        compiler_params=pltpu.CompilerParams(has_side_effects=True))(*parts)


def join_halves(halves):
    nw = len(halves)

    def body(*refs):
        src, dst = refs[:nw], refs[nw:2 * nw]
        send, recv = refs[2 * nw:]
        x, y, c = _place()
        cps = []
        for w in range(nw):
            cp = pltpu.make_async_remote_copy(src_ref=src[w], dst_ref=dst[w], send_sem=send.at[w], recv_sem=recv.at[w],
                                              device_id=(x, y, 1 - c), device_id_type=MESH)
            cp.start()
            cps.append(cp)
        for cp in cps:
            cp.wait()

    sem = pltpu.SemaphoreType.DMA((nw,))
    return pl.pallas_call(
        body, out_shape=[jax.ShapeDtypeStruct(p.shape, p.dtype) for p in halves], in_specs=[ANY] * nw, out_specs=[ANY] * nw,
        scratch_shapes=[sem, sem], name="join_halves",
        compiler_params=pltpu.CompilerParams(has_side_effects=True))(*halves)


def all_reduce_small(v):
    R, n = v.shape

    def body(v_ref, o_ref, all_ref, send_sems, recv_sems, local_sem):
        x, y, c = _place()
        me, sib = (x, y, c), (x, y, 1 - c)
        chips = [(1 - x, y), (x, 1 - y), (1 - x, 1 - y)]

        def rows(px, py, pc):
            return all_ref.at[pl.ds((4 * px + 2 * py + pc) * R, R), :]

        def copy(k, block, to, src=None):
            return pltpu.make_async_remote_copy(src_ref=rows(*block) if src is None else src, dst_ref=rows(*block),
                                                send_sem=send_sems.at[k], recv_sem=recv_sems.at[k],
                                                device_id=to, device_id_type=MESH)

        own = pltpu.make_async_copy(v_ref, rows(*me), local_sem)
        own.start()
        first = [copy(0, me, sib, src=v_ref)]
        first += [copy(1 + j, me, (*chip, c), src=v_ref) for j, chip in enumerate(chips)]
        for cp in first:
            cp.start()
        passed = [copy(4 + j, (*chip, c), sib) for j, chip in enumerate(chips)]
        for j, chip in enumerate(chips):
            copy(1 + j, (*chip, c), me).wait_recv()
            passed[j].start()
        copy(0, sib, me).wait_recv()
        for j, chip in enumerate(chips):
            copy(4 + j, (*chip, 1 - c), me).wait_recv()
        for cp in first + passed:
            cp.wait_send()
        own.wait()
        acc = all_ref[pl.ds(0, R), :]
        for d in range(1, N_DEV):
            acc = acc + all_ref[pl.ds(d * R, R), :]
        o_ref[...] = acc

    vm = pl.BlockSpec(memory_space=pltpu.VMEM)
    return pl.pallas_call(
        body, out_shape=jax.ShapeDtypeStruct((R, n), f32), in_specs=[vm], out_specs=vm,
        scratch_shapes=[pltpu.VMEM((N_DEV * R, n), f32), pltpu.SemaphoreType.DMA((7,)), pltpu.SemaphoreType.DMA((7,)),
                        pltpu.SemaphoreType.DMA],
        name="all_reduce_small", compiler_params=pltpu.CompilerParams(vmem_limit_bytes=VMEM_LIMIT, has_side_effects=True))(v)


def local_step(x, pos, tgt, small, big):
    L, D = x.shape
    d_kv = N_KV_HEADS * HEAD_DIM
    d_ssm = small["d_skip"].shape[1]
    d_attn = big["w_in"].shape[0] * big["w_in"].shape[2] - 2 * d_kv - d_ssm
    G = d_ssm // SSM_GROUP
    N, P = SSM_STATE, SSM_GROUP
    gbf = bf16

    half_dim = HEAD_DIM // 2
    inv_freq = ROPE_THETA ** (-jnp.arange(half_dim, dtype=f32) / half_dim)
    inv_freq = jnp.tile(inv_freq, LANES // half_dim).reshape(1, LANES)
    sink_b = jnp.broadcast_to(small["sinks"].reshape(-1, 1), (small["sinks"].size, LANES))

    spread = jnp.repeat(jnp.eye(G, dtype=f32), P, axis=0)
    logdt_b = jnp.broadcast_to(small["log_dt"].reshape(G, 1), (G, N))
    bt_re = small["b_re"].reshape(G, N, P).transpose(0, 2, 1).reshape(G * P, N)
    bt_im = small["b_im"].reshape(G, N, P).transpose(0, 2, 1).reshape(G * P, N)
    a_re, a_im = small["a_re"].reshape(G, N), small["a_im"].reshape(G, N)
    lam_re, lam_im, bbt_re, bbt_im = ssm_params(a_re, a_im, logdt_b, bt_re, bt_im, spread)
    bd_re = _block_diag(bbt_re.reshape(G, P, N), P, N).astype(bf16)
    bd_im = _block_diag(bbt_im.reshape(G, P, N), P, N).astype(bf16)
    c_re = small["c_re"].reshape(G, P, N).transpose(0, 2, 1)
    c_im = small["c_im"].reshape(G, P, N).transpose(0, 2, 1)
    cd_re = _block_diag(c_re, N, P).astype(bf16)
    cd_im = _block_diag(c_im, N, P).astype(bf16)
    lam_re_l, lam_im_l = _state_layout(lam_re), _state_layout(lam_im)

    def k1(i, nt, xt, g):
        return (rms_fwd(xt, g),)
    xn = rowwise("pre_mix_norm", k1, L, [full(x)], [small["g_pre_mix"]], [(D, bf16)])[0]
    proj = mm_nn("proj_in", xn, big["w_in"])
    qr, kk, vv, u_bf = qkv_prep(proj, pos, inv_freq, d_attn, d_kv)
    attn = attn_fwd(qr, kk, vv, sink_b)
    u_cb = (d_attn + 2 * d_kv) // (d_ssm // 2)
    y, s_re, s_im = ssm_fwd(u_bf, proj, u_cb, bd_re, bd_im, cd_re, cd_im, lam_re_l, lam_im_l, small["d_skip"])

    def k5(i, nt, yt):
        return (gelu(yt),)
    z_bf = rowwise("ssm_gelu", k5, L, [full(y)], [], [(d_ssm, bf16)])[0]
    gl = mm_nn("glu_proj", z_bf, big["w_glu"])

    def k6(i, nt, at, yt, glt, bg, ga, gs):
        ssm = gelu(yt) * sigmoid(glt + bg)
        return (jnp.concatenate([rms_fwd(at, ga), rms_fwd(ssm, gs)], axis=1),)
    mixed = rowwise("mix_norms", k6, L, [full(attn), full(y), full(gl)],
                    [small["b_glu"], small["g_attn_out"], small["g_ssm_out"]], [(d_attn + d_ssm, bf16)])[0]
    mix = mm_nn("proj_out", mixed, big["w_o"])

    def k7(i, nt, xt, mt, gpm, gpf):
        h = xt + rms_fwd(mt, gpm)
        return h, rms_fwd(h, gpf)
    h, hn = rowwise("post_mix", k7, L, [full(x), full(mix)], [small["g_post_mix"], small["g_pre_ffn"]], [(D, f32), (D, bf16)])
    gt = mm_nn("ffn_gate", hn, big["w_gate"])
    up = mm_nn("ffn_up", hn, big["w_up"])
    d_ff_dim = gt.shape[1]

    def k8(i, nt, g_t, u_t):
        return (g_t * sigmoid(g_t) * u_t,)
    hid = rowwise("ffn_act", k8, L, [full(gt), full(up)], [], [(d_ff_dim, bf16)], tr=64)[0]
    ff = mm_nn("ffn_down", hid, big["w_down"])

    def k9(i, nt, ht, fft, tt, g):
        out = ht + rms_fwd(fft, g)
        err = out - tt
        per_row = jnp.mean(err * err, axis=-1, keepdims=True)
        loss = 0.5 * jnp.sum(per_row) * jnp.where(_lane((1, LANES)) == 0, 1.0, 0.0)
        d_out = err * (1.0 / D)
        d_ff, dg = rms_bwd(fft, g, d_out)
        return d_out, d_ff, dg, loss
    d_out, d_ff, dg_post_ffn, loss = rowwise("loss_head", k9, L, [full(h), full(ff), full(tgt)], [small["g_post_ffn"]],
                                             [(D, f32), (D, bf16)], reds=[D, LANES])

    d_hid = mm_nt("d_ffn_hidden", d_ff, big["w_down"])
    dw_down = mm_tn("dw_down", hid, d_ff, out_dtype=gbf)

    def k10(i, nt, g_t, u_t, dh_t):
        sg = sigmoid(g_t)
        return dh_t * u_t * (sg * (1.0 + g_t * (1.0 - sg))), dh_t * (g_t * sg)
    d_gt, d_up = rowwise("ffn_act_grad", k10, L, [full(gt), full(up), full(d_hid)], [], [(d_ff_dim, bf16), (d_ff_dim, bf16)], tr=64)
    d_hn_a = mm_nt("d_hn_gate", d_gt, big["w_gate"])
    d_hn_b = mm_nt("d_hn_up", d_up, big["w_up"])
    dw_gate = mm_tn("dw_gate", hn, d_gt, shards=N_CHIPS, out_dtype=gbf)
    dw_up = mm_tn("dw_up", hn, d_up, shards=N_CHIPS, out_dtype=gbf)

    def k11(i, nt, ht, da, db, do, mt, gpf, gpm):
        dh_n, dg_pf = rms_bwd(ht, gpf, da + db)
        dh = do + dh_n
        d_mix, dg_pm = rms_bwd(mt, gpm, dh)
        return dh, d_mix, dg_pf, dg_pm
    dh, d_mix, dg_pre_ffn, dg_post_mix = rowwise("post_mix_grad", k11, L, [full(h), full(d_hn_a), full(d_hn_b), full(d_out), full(mix)],
                                                 [small["g_pre_ffn"], small["g_post_mix"]], [(D, f32), (D, bf16)], reds=[D, D])
    d_mixed = mm_nt("d_mixed", d_mix, big["w_o"])
    dw_o = mm_tn("dw_o", mixed, d_mix, out_dtype=gbf)

    def k12(i, nt, at, yt, glt, da_n, ds_n, bg, ga, gs):
        z = gelu(yt)
        sg = sigmoid(glt + bg)
        ssm = z * sg
        d_at, dga = rms_bwd(at, ga, da_n)
        d_ssm_t, dgs = rms_bwd(ssm, gs, ds_n)
        d_gl = d_ssm_t * z * sg * (1.0 - sg)
        return d_at, d_ssm_t * sg, d_gl, dga, dgs, colsum(d_gl)
    d_attn_o, dz1, d_gl, dg_attn, dg_ssm, db_glu = rowwise(
        "mix_norms_grad", k12, L, [full(attn), full(y), full(gl), (d_mixed, d_attn, 0, 0), (d_mixed, d_ssm, d_attn // d_ssm, 0)],
        [small["b_glu"], small["g_attn_out"], small["g_ssm_out"]], [(d_attn, f32), (d_ssm, f32), (d_ssm, bf16)],
        reds=[d_attn, d_ssm, d_ssm])
    dz2 = mm_nt("d_glu_in", d_gl, big["w_glu"])
    dw_glu = mm_tn("dw_glu", z_bf, d_gl, out_dtype=gbf)

    def k13(i, nt, yt, a, b):
        return ((a + b) * gelu_grad(yt),)
    dy = rowwise("ssm_gelu_grad", k13, L, [full(y), full(dz1), full(dz2)], [], [(d_ssm, f32)])[0]
    du, dbd_re, dbd_im, dcd_re, dcd_im, dlam_re_l, dlam_im_l, dd_skip = ssm_bwd(
        dy, u_bf, proj, u_cb, s_re, s_im, bd_re, bd_im, cd_re, cd_im, lam_re_l, lam_im_l, small["d_skip"])
    dq, dkk_c, dkk_p, dvv_c, dvv_p, dsink = attn_bwd(qr, kk, vv, sink_b, attn, d_attn_o)
    d_proj = qkv_grad(dq, dkk_c, dkk_p, dvv_c, dvv_p, du, pos, inv_freq)
    d_xn = mm_nt("d_xn", d_proj, big["w_in"])
    dw_in = mm_tn("dw_in", xn, d_proj, shards=N_CHIPS, out_dtype=gbf)

    def k17(i, nt, xt, dxn, dht, g):
        dx, dg = rms_bwd(xt, g, dxn)
        return dht + dx, dg
    grad_x, dg_pre_mix = rowwise("pre_mix_grad", k17, L, [full(x), full(d_xn), full(dh)], [small["g_pre_mix"]], [(D, f32)], reds=[D])

    gather = spread.T
    dbbt_re = _block_diag_take(dbd_re, P, N).reshape(G * P, N)
    dbbt_im = _block_diag_take(dbd_im, P, N).reshape(G * P, N)
    d_a_re, d_a_im, d_logdt, dbt_re, dbt_im = ssm_params_grad(
        a_re, a_im, logdt_b, bt_re, bt_im, spread, gather,
        _state_layout_inv(dlam_re_l, G, N), _state_layout_inv(dlam_im_l, G, N), dbbt_re, dbbt_im)
    q_per_kv = d_attn // HEAD_DIM // N_KV_HEADS
    small_grads = {
        "g_pre_mix": dg_pre_mix, "sinks": dsink[:, :q_per_kv, 0].reshape(1, -1),
        "a_re": d_a_re, "a_im": d_a_im, "log_dt": d_logdt.reshape(1, G),
        "b_re": dbt_re.reshape(G, P, N).transpose(0, 2, 1), "b_im": dbt_im.reshape(G, P, N).transpose(0, 2, 1),
        "c_re": _block_diag_take(dcd_re, N, P).transpose(0, 2, 1), "c_im": _block_diag_take(dcd_im, N, P).transpose(0, 2, 1),
        "d_skip": dd_skip, "b_glu": db_glu, "g_attn_out": dg_attn, "g_ssm_out": dg_ssm,
        "g_post_mix": dg_post_mix, "g_pre_ffn": dg_pre_ffn, "g_post_ffn": dg_post_ffn,
    }
    big_grads = {"w_in": dw_in, "w_glu": dw_glu, "w_o": dw_o, "w_gate": dw_gate, "w_up": dw_up, "w_down": dw_down}
    return loss, grad_x, small_grads, big_grads


WEIGHTS = ['g_pre_mix', 'w_in', 'sinks', 'a_re', 'a_im', 'log_dt', 'b_re', 'b_im', 'c_re', 'c_im', 'd_skip', 'w_glu', 'b_glu',
           'g_attn_out', 'g_ssm_out', 'w_o', 'g_post_mix', 'g_pre_ffn', 'w_gate', 'w_up', 'w_down', 'g_post_ffn']
BIG = ['w_in', 'w_glu', 'w_o', 'w_gate', 'w_up', 'w_down']
COL_SHARDED = ['w_in', 'w_gate', 'w_up']
SMALL = [n for n in WEIGHTS if n not in BIG]


PACK_ROWS = 256


def _pack(parts):
    flat = jnp.concatenate([p.reshape(-1) for p in parts])
    pad = (-flat.size) % (PACK_ROWS * LANES)
    return jnp.pad(flat, (0, pad)).reshape(-1, LANES)


def _unpack(packed, shapes):
    flat = packed.reshape(-1)
    out, off = [], 0
    for s in shapes:
        n = int(np.prod(s))
        out.append(flat[off:off + n].reshape(s))
        off += n
    return out


def kernel(x, positions, g_pre_mix, w_in, sinks, a_re, a_im, log_dt, b_re, b_im, c_re, c_im, d_skip, w_glu, b_glu, g_attn_out, g_ssm_out, w_o, g_post_mix, g_pre_ffn, w_gate, w_up, w_down, g_post_ffn, loss_target, m_g_pre_mix, m_w_in, m_sinks, m_a_re, m_a_im, m_log_dt, m_b_re, m_b_im, m_c_re, m_c_im, m_d_skip, m_w_glu, m_b_glu, m_g_attn_out, m_g_ssm_out, m_w_o, m_g_post_mix, m_g_pre_ffn, m_w_gate, m_w_up, m_w_down, m_g_post_ffn, v_g_pre_mix, v_w_in, v_sinks, v_a_re, v_a_im, v_log_dt, v_b_re, v_b_im, v_c_re, v_c_im, v_d_skip, v_w_glu, v_b_glu, v_g_attn_out, v_g_ssm_out, v_w_o, v_g_post_mix, v_g_pre_ffn, v_w_gate, v_w_up, v_w_down, v_g_post_ffn):
    args = dict(locals())
    w = {n: args[n] for n in WEIGHTS}
    m = {n: args["m_" + n] for n in WEIGHTS}
    v = {n: args["v_" + n] for n in WEIGHTS}
    L, D = x.shape[1], x.shape[2]

    ax, ay, ac = _place()
    mine_arr = (2 * ax + ay).astype(jnp.int32).reshape(1)
    c_arr = ac.astype(jnp.int32).reshape(1)

    gathered = gather_weights([cast_into_slot("cast_" + n, w[n][0], mine_arr) for n in BIG])
    big = {}
    for n, g in zip(BIG, gathered):
        big[n] = g if n in COL_SHARDED else g.reshape(g.shape[0] * g.shape[1], g.shape[2])

    small = {n: w[n].reshape(1, -1) for n in SMALL}
    pos = positions.reshape(L, 1).astype(f32)
    loss, grad_x, small_grads, big_grads = local_step(x[0], pos, loss_target[0], small, big)
    loss = lax.psum(jnp.sum(loss), ("x", "y", "c"))

    g3 = []
    for n in BIG:
        g = big_grads[n]
        g3.append(g if n in COL_SHARDED else g.reshape(N_CHIPS, g.shape[0] // N_CHIPS, g.shape[1]))
    got = swap_halves(g3)
    pair = [pair_sum("pair_sum_" + n, g, o, c_arr) for n, g, o in zip(BIG, g3, got)]
    landed = scatter_to_owners(pair)
    halves = [chip_sum("chip_sum_" + n, p, t, mine_arr) for n, p, t in zip(BIG, pair, landed)]
    sib_halves = dict(zip(BIG, join_halves(halves)))
    own_halves = dict(zip(BIG, halves))

    shapes = [w[n].shape for n in SMALL]
    small_sum = all_reduce_small(_pack([small_grads[n] for n in SMALL]))
    small_g = dict(zip(SMALL, _unpack(small_sum, shapes)))

    grads, delta, new_m, new_v = {}, {}, {}, {}
    for n in BIG:
        g_, d_, m_, v_ = adamw_halves("adamw_" + n, w[n][0], own_halves[n], sib_halves[n], m[n][0], v[n][0])
        grads[n], delta[n], new_m[n], new_v[n] = g_[None], d_[None], m_[None], v_[None]
    pw, pm, pv = (_pack([t[n] for n in SMALL]) for t in (w, m, v))
    d_, m_, v_ = adamw("adamw_small", pw, small_sum, pm, pv)
    for t, packed in ((delta, d_), (new_m, m_), (new_v, v_)):
        t.update(zip(SMALL, _unpack(packed, shapes)))
    grads.update(small_g)

    return (loss, grad_x[None], *[grads[n] for n in WEIGHTS], *[delta[n] for n in WEIGHTS],
            *[new_m[n] for n in WEIGHTS], *[new_v[n] for n in WEIGHTS])
```

```python
import functools
import math

import jax
import jax.numpy as jnp
import numpy as np
from jax import lax
from jax.experimental import pallas as pl
from jax.experimental.pallas import tpu as pltpu

f32 = jnp.float32
bf16 = jnp.bfloat16
HIGHEST = lax.Precision.HIGHEST
MESH = pl.DeviceIdType.MESH

HEAD_DIM = 64
N_KV_HEADS = 4
ATTN_BLOCK = 128
ROPE_THETA = 10000.0
SSM_GROUP = 16
SSM_STATE = 64
RMS_EPS = 1e-6
LANES = 128
SUBLANES = 8
VMEM_LIMIT = 52 * 1024 * 1024
N_CHIPS = 4
N_DEV = 8
NEG = -1e30

ADAM_LR, ADAM_B1, ADAM_B2, ADAM_EPS, ADAM_WD, ADAM_STEP = 0.001, 0.9, 0.999, 1e-08, 0.01, 10

NN = (((1,), (0,)), ((), ()))
NT = (((1,), (1,)), ((), ()))
TN = (((0,), (0,)), ((), ()))


def _params(*sem):
    return pltpu.CompilerParams(dimension_semantics=sem or None, vmem_limit_bytes=VMEM_LIMIT)


def _dot(a, b, dims=NN):
    return lax.dot_general(a, b, dims, preferred_element_type=f32)


def _pick(dim, pref):
    t = min(dim, pref)
    while dim % t:
        t -= LANES
    assert t > 0, (dim, pref)
    return t


def _mm_call(name, grid, in_specs, out_spec, out_shape, acc_shape, dims, operands):
    nk = grid[2]

    def body(a_ref, b_ref, o_ref, acc_ref):
        k = pl.program_id(2)

        @pl.when(k == 0)
        def _():
            acc_ref[...] = jnp.zeros_like(acc_ref)

        acc_ref[...] += _dot(a_ref[...], b_ref[...], dims)

        @pl.when(k == nk - 1)
        def _():
            o_ref[...] = acc_ref[...].astype(o_ref.dtype)

    return pl.pallas_call(
        body, out_shape=out_shape, grid=grid, in_specs=in_specs, out_specs=out_spec,
        scratch_shapes=[pltpu.VMEM(acc_shape, f32)], name=name,
        compiler_params=_params("parallel", "parallel", "arbitrary"))(*operands)


def mm_nn(name, a, b, out_dtype=f32, tm=1024, tn=1024, tk=1024):
    M, K = a.shape
    tm, tk = _pick(M, tm), _pick(K, tk)
    if b.ndim == 3:
        S, _, n = b.shape
        tn = _pick(n, 2048)
        per = n // tn
        b_spec = pl.BlockSpec((None, tk, tn), lambda i, j, k: (j // per, k, j % per))
        N = S * n
    else:
        N = b.shape[1]
        tn = _pick(N, tn)
        b_spec = pl.BlockSpec((tk, tn), lambda i, j, k: (k, j))
    grid = (M // tm, N // tn, K // tk)
    return _mm_call(name, grid, [pl.BlockSpec((tm, tk), lambda i, j, k: (i, k)), b_spec],
                    pl.BlockSpec((tm, tn), lambda i, j, k: (i, j)), jax.ShapeDtypeStruct((M, N), out_dtype),
                    (tm, tn), NN, (a, b))


def mm_nt(name, a, b, out_dtype=f32, tm=1024, tn=1024, tk=1024):
    M, N = a.shape
    tm = _pick(M, tm)
    if b.ndim == 3:
        S, K, n = b.shape
        tr = _pick(n, 2048)
        per = n // tr
        tko = _pick(K, tk)
        b_spec = pl.BlockSpec((None, tko, tr), lambda i, j, k: (k // per, j, k % per))
    else:
        K = b.shape[0]
        tr = _pick(N, tn)
        tko = _pick(K, tk)
        b_spec = pl.BlockSpec((tko, tr), lambda i, j, k: (j, k))
    grid = (M // tm, K // tko, N // tr)
    return _mm_call(name, grid, [pl.BlockSpec((tm, tr), lambda i, j, k: (i, k)), b_spec],
                    pl.BlockSpec((tm, tko), lambda i, j, k: (i, j)), jax.ShapeDtypeStruct((M, K), out_dtype),
                    (tm, tko), NT, (a, b))


def mm_tn(name, a, b, shards=None, out_dtype=f32, tm=1024, tn=1024, tl=1024):
    L, K = a.shape
    N = b.shape[1]
    tl, tko = _pick(L, tl), _pick(K, tm)
    if shards:
        n = N // shards
        tn = _pick(n, 2048)
        per = n // tn
        o_spec = pl.BlockSpec((None, tko, tn), lambda i, j, k: (j // per, i, j % per))
        o_shape = jax.ShapeDtypeStruct((shards, K, n), out_dtype)
    else:
        tn = _pick(N, tn)
        o_spec = pl.BlockSpec((tko, tn), lambda i, j, k: (i, j))
        o_shape = jax.ShapeDtypeStruct((K, N), out_dtype)
    grid = (K // tko, N // tn, L // tl)
    return _mm_call(name, grid, [pl.BlockSpec((tl, tko), lambda i, j, k: (k, i)),
                                 pl.BlockSpec((tl, tn), lambda i, j, k: (k, j))],
                    o_spec, o_shape, (tko, tn), TN, (a, b))


def rowwise(name, fn, L, rows, bcast, outs, reds=(), tr=128):
    tr = min(tr, L)
    nt = L // tr
    n_rows, n_b, n_o = len(rows), len(bcast), len(outs)

    def body(*refs):
        i = pl.program_id(0)
        ins = [r[...] for r in refs[:n_rows + n_b]]
        res = fn(i, nt, *ins)
        o_refs = refs[n_rows + n_b:]
        for k in range(n_o):
            o_refs[k][...] = res[k].astype(o_refs[k].dtype)
        if reds:
            @pl.when(i == 0)
            def _():
                for k in range(len(reds)):
                    o_refs[n_o + k][...] = jnp.zeros_like(o_refs[n_o + k])
            for k in range(len(reds)):
                o_refs[n_o + k][...] += res[n_o + k]

    def row_spec(width, cb, shift):
        if shift:
            return pl.BlockSpec((tr, width), lambda i: (jnp.minimum(i + shift, nt - 1), cb))
        return pl.BlockSpec((tr, width), lambda i: (i, cb))

    in_specs = [row_spec(w, cb, sh) for (_, w, cb, sh) in rows]
    in_specs += [pl.BlockSpec(b.shape, lambda i: (0, 0)) for b in bcast]
    out_specs = [pl.BlockSpec((tr, w), lambda i: (i, 0)) for (w, _) in outs]
    out_specs += [pl.BlockSpec((1, w), lambda i: (0, 0)) for w in reds]
    out_shape = [jax.ShapeDtypeStruct((L, w), dt) for (w, dt) in outs]
    out_shape += [jax.ShapeDtypeStruct((1, w), f32) for w in reds]
    return pl.pallas_call(
        body, out_shape=out_shape, grid=(nt,), in_specs=in_specs, out_specs=out_specs, name=name,
        compiler_params=_params("arbitrary"))(*[r[0] for r in rows], *bcast)


def full(a):
    return (a, a.shape[1], 0, 0)


def colsum(v):
    return jnp.sum(v, axis=0, keepdims=True)


def rms_fwd(x, g):
    r = lax.rsqrt(jnp.mean(x * x, axis=-1, keepdims=True) + RMS_EPS)
    return x * r * g


def rms_bwd(x, g, dy):
    r = lax.rsqrt(jnp.mean(x * x, axis=-1, keepdims=True) + RMS_EPS)
    xh = x * r
    dyg = dy * g
    dx = r * (dyg - xh * jnp.mean(dyg * xh, axis=-1, keepdims=True))
    return dx, colsum(dy * xh)


GELU_C = math.sqrt(2.0 / math.pi)


def gelu(y):
    return y * (0.5 * (1.0 + jnp.tanh(GELU_C * (y + 0.044715 * (y * y * y)))))


def gelu_grad(y):
    t = jnp.tanh(GELU_C * (y + 0.044715 * (y * y * y)))
    return 0.5 * (1.0 + t) + 0.5 * y * (1.0 - t * t) * (GELU_C * (1.0 + 3 * 0.044715 * (y * y)))


def sigmoid(v):
    return 1.0 / (1.0 + jnp.exp(-v))


def _lane(shape):
    return lax.broadcasted_iota(jnp.int32, shape, 1)


def _rot_chunk(t, cos, sin_signed):
    first = (_lane(t.shape) % HEAD_DIM) < (HEAD_DIM // 2)
    partner = jnp.where(first, pltpu.roll(t, LANES - HEAD_DIM // 2, 1), pltpu.roll(t, HEAD_DIM // 2, 1))
    return t * cos + partner * sin_signed


def _cos_sin(pos, inv_freq, inverse):
    ang = pos * inv_freq
    cos, sin = jnp.cos(ang), jnp.sin(ang)
    first = (_lane(ang.shape) % HEAD_DIM) < (HEAD_DIM // 2)
    sign = jnp.where(first, -1.0, 1.0) * (-1.0 if inverse else 1.0)
    return cos, sin * sign


def _dup_head(chunk, odd):
    low = _lane(chunk.shape) < HEAD_DIM
    x = jnp.where(low != odd, chunk, 0.0)
    return x + pltpu.roll(x, HEAD_DIM, 1)


def _chunks(v):
    return [v[:, LANES * c:LANES * (c + 1)] for c in range(v.shape[1] // LANES)]


def qkv_prep(proj, pos, inv_freq, d_attn, d_kv):
    L = proj.shape[0]
    d_ssm = proj.shape[1] - d_attn - 2 * d_kv
    half = d_ssm // 2
    scale = 1.0 / math.sqrt(HEAD_DIM)

    def fn(i, nt, q, k, v, u0, u1, p, invf):
        cos, sin = _cos_sin(p, invf, False)
        qr = jnp.concatenate([_rot_chunk(c, cos, sin) for c in _chunks(q)], axis=1) * scale
        kr = [_rot_chunk(c, cos, sin) for c in _chunks(k)]
        kk = jnp.concatenate([_dup_head(c, odd) for c in kr for odd in (False, True)], axis=1)
        vv = jnp.concatenate([_dup_head(c, odd) for c in _chunks(v) for odd in (False, True)], axis=1)
        return qr, kk, vv, jnp.concatenate([u0, u1], axis=1)

    u_cb = (d_attn + 2 * d_kv) // half
    return rowwise("qkv_prep", fn, L,
                   [(proj, d_attn, 0, 0), (proj, d_kv, d_attn // d_kv, 0), (proj, d_kv, d_attn // d_kv + 1, 0),
                    (proj, half, u_cb, 0), (proj, half, u_cb + 1, 0), full(pos)],
                   [inv_freq], [(d_attn, bf16), (2 * d_kv, bf16), (2 * d_kv, bf16), (d_ssm, bf16)])


def qkv_grad(dq, dkk_c, dkk_p, dvv_c, dvv_p, du, pos, inv_freq):
    L, d_attn = dq.shape
    d_kv = dkk_c.shape[1] // 2
    scale = 1.0 / math.sqrt(HEAD_DIM)

    def fold(cur, prev, i, nt):
        t = cur + jnp.where(i < nt - 1, prev, 0.0)
        out = []
        for c in range(t.shape[1] // (2 * LANES)):
            even, odd = t[:, 2 * c * LANES:(2 * c + 1) * LANES], t[:, (2 * c + 1) * LANES:(2 * c + 2) * LANES]
            even, odd = even + pltpu.roll(even, HEAD_DIM, 1), odd + pltpu.roll(odd, HEAD_DIM, 1)
            out.append(jnp.where(_lane(even.shape) < HEAD_DIM, even, odd))
        return out

    def fn(i, nt, dq_t, kc, kp, vc, vp, du_t, p, invf):
        cos, sin = _cos_sin(p, invf, True)
        dq_o = jnp.concatenate([_rot_chunk(c, cos, sin) for c in _chunks(dq_t)], axis=1) * scale
        dk_o = jnp.concatenate([_rot_chunk(c, cos, sin) for c in fold(kc, kp, i, nt)], axis=1)
        dv_o = jnp.concatenate(fold(vc, vp, i, nt), axis=1)
        return (jnp.concatenate([dq_o, dk_o, dv_o, du_t], axis=1),)

    return rowwise("qkv_grad", fn, L,
                   [full(dq), full(dkk_c), (dkk_p, 2 * d_kv, 0, 1), full(dvv_c), (dvv_p, 2 * d_kv, 0, 1), full(du), full(pos)],
                   [inv_freq], [(d_attn + 2 * d_kv + du.shape[1], bf16)], tr=ATTN_BLOCK)[0]


def _attn_specs(L):
    nb = L // ATTN_BLOCK
    B = ATTN_BLOCK
    q_spec = pl.BlockSpec((B, 2 * LANES), lambda h, n: (n, h))
    cur = pl.BlockSpec((B, LANES), lambda h, n: (n, h))
    prev = pl.BlockSpec((B, LANES), lambda h, n: (jnp.maximum(n - 1, 0), h))
    return nb, q_spec, cur, prev


def _attn_probs(qm, kc, kp, sink, n):
    B = ATTN_BLOCK
    row = lax.broadcasted_iota(jnp.int32, (B, B), 0)
    col = lax.broadcasted_iota(jnp.int32, (B, B), 1)
    sc = jnp.where(row >= col, _dot(qm, kc, NT), NEG)
    sp = jnp.where((col > row) & (n > 0), _dot(qm, kp, NT), NEG)
    m = jnp.maximum(jnp.maximum(jnp.max(sc, axis=1, keepdims=True), jnp.max(sp, axis=1, keepdims=True)), sink)
    pc, pp, ps = jnp.exp(sc - m), jnp.exp(sp - m), jnp.exp(sink - m)
    inv = 1.0 / (jnp.sum(pc, axis=1, keepdims=True) + jnp.sum(pp, axis=1, keepdims=True) + ps)
    return pc, pp, inv, ps


def attn_fwd(qr, kk, vv, sink_b):
    L, d_attn = qr.shape
    nb, q_spec, cur, prev = _attn_specs(L)
    n_kv = kk.shape[1] // LANES
    q_per_kv = d_attn // HEAD_DIM // n_kv

    def body(q_ref, kc_ref, kp_ref, vc_ref, vp_ref, s_ref, o_ref):
        h, n = pl.program_id(0), pl.program_id(1)
        low = _lane((ATTN_BLOCK, LANES)) < HEAD_DIM
        kc, kp, vc, vp = kc_ref[...], kp_ref[...], vc_ref[...], vp_ref[...]
        for pr in range(q_per_kv // 2):
            q2 = q_ref[:, LANES * pr:LANES * (pr + 1)]
            o2 = jnp.zeros((ATTN_BLOCK, LANES), f32)
            for odd in (False, True):
                mine = low != odd
                qm = jnp.where(mine, q2, jnp.zeros_like(q2))
                sink = jnp.max(s_ref[pl.ds(h * q_per_kv + 2 * pr + int(odd), 1), :], axis=1, keepdims=True)
                pc, pp, inv, _ = _attn_probs(qm, kc, kp, sink, n)
                o = _dot(pc.astype(bf16), vc) + _dot(pp.astype(bf16), vp)
                o2 = o2 + jnp.where(mine, o * inv, 0.0)
            o_ref[:, LANES * pr:LANES * (pr + 1)] = o2

    return pl.pallas_call(
        body, out_shape=jax.ShapeDtypeStruct((L, d_attn), f32), grid=(n_kv, nb),
        in_specs=[q_spec, cur, prev, cur, prev, pl.BlockSpec(sink_b.shape, lambda h, n: (0, 0))],
        out_specs=q_spec, name="attn_fwd", compiler_params=_params("parallel", "arbitrary"))(qr, kk, kk, vv, vv, sink_b)


def attn_bwd(qr, kk, vv, sink_b, attn, d_attn_out):
    L, d_attn = qr.shape
    nb, q_spec, cur, prev = _attn_specs(L)
    n_kv = kk.shape[1] // LANES
    q_per_kv = d_attn // HEAD_DIM // n_kv

    def body(q_ref, kc_ref, kp_ref, vc_ref, vp_ref, s_ref, o_ref, do_ref, dq_ref, dkc_ref, dkp_ref, dvc_ref, dvp_ref, ds_ref):
        h, n = pl.program_id(0), pl.program_id(1)
        B = ATTN_BLOCK
        low = _lane((B, LANES)) < HEAD_DIM
        kc, kp, vc, vp = kc_ref[...], kp_ref[...], vc_ref[...], vp_ref[...]
        dkc = jnp.zeros((B, LANES), f32)
        dkp, dvc, dvp = dkc, dkc, dkc
        srow = lax.broadcasted_iota(jnp.int32, (SUBLANES, LANES), 0)
        dsink = jnp.zeros((SUBLANES, LANES), f32)
        for pr in range(q_per_kv // 2):
            q2 = q_ref[:, LANES * pr:LANES * (pr + 1)]
            do2 = do_ref[:, LANES * pr:LANES * (pr + 1)]
            prod = do2 * o_ref[:, LANES * pr:LANES * (pr + 1)]
            dq2 = jnp.zeros((B, LANES), f32)
            for odd in (False, True):
                mine = low != odd
                qm = jnp.where(mine, q2, jnp.zeros_like(q2))
                sink = jnp.max(s_ref[pl.ds(h * q_per_kv + 2 * pr + int(odd), 1), :], axis=1, keepdims=True)
                pc, pp, inv, ps = _attn_probs(qm, kc, kp, sink, n)
                pc, pp = pc * inv, pp * inv
                delta = jnp.sum(jnp.where(mine, prod, 0.0), axis=1, keepdims=True)
                dob = jnp.where(mine, do2, 0.0).astype(bf16)
                dsc = (pc * (_dot(dob, vc, NT) - delta)).astype(bf16)
                dsp = (pp * (_dot(dob, vp, NT) - delta)).astype(bf16)
                dq2 = dq2 + jnp.where(mine, _dot(dsc, kc) + _dot(dsp, kp), 0.0)
                dkc = dkc + _dot(dsc, qm, TN)
                dkp = dkp + _dot(dsp, qm, TN)
                dvc = dvc + _dot(pc.astype(bf16), dob, TN)
                dvp = dvp + _dot(pp.astype(bf16), dob, TN)
                dsink = dsink + jnp.where(srow == 2 * pr + int(odd), -jnp.sum(ps * inv * delta), 0.0)
            dq_ref[:, LANES * pr:LANES * (pr + 1)] = dq2
        dkc_ref[...] = dkc
        dkp_ref[...] = dkp
        dvc_ref[...] = dvc
        dvp_ref[...] = dvp

        @pl.when(n == 0)
        def _():
            ds_ref[...] = jnp.zeros_like(ds_ref)

        ds_ref[...] += dsink

    kv_shape = jax.ShapeDtypeStruct(kk.shape, f32)
    return pl.pallas_call(
        body,
        out_shape=[jax.ShapeDtypeStruct((L, d_attn), f32), kv_shape, kv_shape, kv_shape, kv_shape,
                   jax.ShapeDtypeStruct((n_kv, SUBLANES, LANES), f32)],
        grid=(n_kv, nb),
        in_specs=[q_spec, cur, prev, cur, prev, pl.BlockSpec(sink_b.shape, lambda h, n: (0, 0)), q_spec, q_spec],
        out_specs=[q_spec, cur, cur, cur, cur, pl.BlockSpec((None, SUBLANES, LANES), lambda h, n: (h, 0, 0))],
        name="attn_bwd", compiler_params=_params("parallel", "arbitrary"))(qr, kk, kk, vv, vv, sink_b, attn, d_attn_out)


SSM_T = 128
NQ = SUBLANES * SSM_STATE // LANES
NJ = SUBLANES


def _strided_put(ref, j, val):
    for q in range(NQ):
        ref.at[q][pl.ds(j, SSM_T, stride=NJ), :] = val[:, LANES * q:LANES * (q + 1)]


def _strided_get(ref, j):
    return jnp.concatenate([ref.at[q][pl.ds(j, SSM_T, stride=NJ), :] for q in range(NQ)], axis=1)


def _ssm_specs(L, rev):
    nt = L // SSM_T
    idx = (lambda i: nt - 1 - i) if rev else (lambda i: i)
    row = lambda w, cb=0: pl.BlockSpec((SSM_T, w), lambda i: (idx(i), cb))
    state = pl.BlockSpec((NQ, SSM_T * NJ, LANES), lambda i: (0, idx(i), 0))
    whole = lambda a: pl.BlockSpec(a.shape, lambda i: (0,) * a.ndim)
    return nt, row, state, whole


def ssm_fwd(u_bf, proj, u_cb, bd_re, bd_im, cd_re, cd_im, lam_re, lam_im, d_skip):
    L, d_ssm = u_bf.shape
    nt, row, state, whole = _ssm_specs(L, False)
    half = d_ssm // 2
    gw = d_ssm // NJ

    def body(u_ref, u0_ref, u1_ref, bdr, bdi, cdr, cdi, lr_ref, li_ref, d_ref, y_ref, sr_ref, si_ref, carry):
        i = pl.program_id(0)

        @pl.when(i == 0)
        def _():
            carry[...] = jnp.zeros_like(carry)

        for j in range(NJ):
            uj = u_ref[:, gw * j:gw * (j + 1)]
            _strided_put(sr_ref, j, _dot(uj, bdr[j]))
            _strided_put(si_ref, j, _dot(uj, bdi[j]))
        lr = [lr_ref[q] for q in range(NQ)]
        li = [li_ref[q] for q in range(NQ)]

        def step(t, s):
            sr, si = s
            rows = pl.ds(pl.multiple_of(t * NJ, NJ), NJ)
            nr = tuple(lr[q] * sr[q] - li[q] * si[q] + sr_ref[q, rows, :] for q in range(NQ))
            ni = tuple(lr[q] * si[q] + li[q] * sr[q] + si_ref[q, rows, :] for q in range(NQ))
            for q in range(NQ):
                sr_ref[q, rows, :] = nr[q]
                si_ref[q, rows, :] = ni[q]
            return nr, ni

        init = (tuple(carry[0, q] for q in range(NQ)), tuple(carry[1, q] for q in range(NQ)))
        sr, si = lax.fori_loop(0, SSM_T, step, init, unroll=8)
        for q in range(NQ):
            carry[0, q] = sr[q]
            carry[1, q] = si[q]
        uf = jnp.concatenate([u0_ref[...], u1_ref[...]], axis=1)
        for j in range(NJ):
            cols = slice(gw * j, gw * (j + 1))
            yj = _dot(_strided_get(sr_ref, j).astype(bf16), cdr[j]) - _dot(_strided_get(si_ref, j).astype(bf16), cdi[j])
            y_ref[:, cols] = yj + d_ref[:, cols] * uf[:, cols]

    s_shape = jax.ShapeDtypeStruct((NQ, L * NJ, LANES), f32)
    consts = (bd_re, bd_im, cd_re, cd_im, lam_re, lam_im, d_skip)
    return pl.pallas_call(
        body, out_shape=[jax.ShapeDtypeStruct((L, d_ssm), f32), s_shape, s_shape], grid=(nt,),
        in_specs=[row(d_ssm), row(half, u_cb), row(half, u_cb + 1)] + [whole(a) for a in consts],
        out_specs=[row(d_ssm), state, state],
        scratch_shapes=[pltpu.VMEM((2, NQ, NJ, LANES), f32)], name="ssm_fwd",
        compiler_params=_params("arbitrary"))(u_bf, proj, proj, *consts)


def ssm_bwd(dy, u_bf, proj, u_cb, s_re, s_im, bd_re, bd_im, cd_re, cd_im, lam_re, lam_im, d_skip):
    L, d_ssm = dy.shape
    nt, row, state, whole = _ssm_specs(L, True)
    half = d_ssm // 2
    gw = d_ssm // NJ

    def body(dy_ref, u_ref, u0_ref, u1_ref, sr_ref, si_ref, bdr, bdi, cdr, cdi, lr_ref, li_ref, d_ref,
             du_ref, dbdr, dbdi, dcdr, dcdi, dlr, dli, dd_ref, gr_ref, gi_ref, carry):
        i = pl.program_id(0)

        @pl.when(i == 0)
        def _():
            carry[...] = jnp.zeros_like(carry)
            for r in (dbdr, dbdi, dcdr, dcdi, dlr, dli, dd_ref):
                r[...] = jnp.zeros_like(r)

        dyf = dy_ref[...]
        dyb = dyf.astype(bf16)
        for j in range(NJ):
            dyj = dyb[:, gw * j:gw * (j + 1)]
            _strided_put(gr_ref, j, _dot(dyj, cdr[j], NT))
            _strided_put(gi_ref, j, -_dot(dyj, cdi[j], NT))
            dcdr[j] += _dot(_strided_get(sr_ref, j).astype(bf16), dyj, TN)
            dcdi[j] -= _dot(_strided_get(si_ref, j).astype(bf16), dyj, TN)
        lr = [lr_ref[q] for q in range(NQ)]
        li = [li_ref[q] for q in range(NQ)]

        def step(k, c):
            gr, gi, ar, ai = c
            rows = pl.ds(pl.multiple_of((SSM_T - 1 - k) * NJ, NJ), NJ)
            s_r = [sr_ref[q, rows, :] for q in range(NQ)]
            s_i = [si_ref[q, rows, :] for q in range(NQ)]
            ar = tuple(ar[q] + gr[q] * s_r[q] + gi[q] * s_i[q] for q in range(NQ))
            ai = tuple(ai[q] + gi[q] * s_r[q] - gr[q] * s_i[q] for q in range(NQ))
            nr = tuple(gr_ref[q, rows, :] + lr[q] * gr[q] + li[q] * gi[q] for q in range(NQ))
            ni = tuple(gi_ref[q, rows, :] + lr[q] * gi[q] - li[q] * gr[q] for q in range(NQ))
            for q in range(NQ):
                gr_ref[q, rows, :] = nr[q]
                gi_ref[q, rows, :] = ni[q]
            return nr, ni, ar, ai

        zero = tuple(jnp.zeros((NJ, LANES), f32) for _ in range(NQ))
        init = (tuple(carry[0, q] for q in range(NQ)), tuple(carry[1, q] for q in range(NQ)), zero, zero)
        gr, gi, ar, ai = lax.fori_loop(0, SSM_T, step, init, unroll=8)
        for q in range(NQ):
            carry[0, q] = gr[q]
            carry[1, q] = gi[q]
            dlr[q] += ar[q]
            dli[q] += ai[q]
        uf = jnp.concatenate([u0_ref[...], u1_ref[...]], axis=1)
        dd_ref[...] += colsum(dyf * uf)
        for j in range(NJ):
            cols = slice(gw * j, gw * (j + 1))
            gjr, gji = _strided_get(gr_ref, j).astype(bf16), _strided_get(gi_ref, j).astype(bf16)
            du_ref[:, cols] = _dot(gjr, bdr[j], NT) + _dot(gji, bdi[j], NT) + d_ref[:, cols] * dyf[:, cols]
            uj = u_ref[:, cols]
            dbdr[j] += _dot(uj, gjr, TN)
            dbdi[j] += _dot(uj, gji, TN)

    consts = (bd_re, bd_im, cd_re, cd_im, lam_re, lam_im, d_skip)
    acc = lambda a: jax.ShapeDtypeStruct(a.shape, f32)
    outs = [jax.ShapeDtypeStruct((L, d_ssm), f32), acc(bd_re), acc(bd_im), acc(cd_re), acc(cd_im), acc(lam_re), acc(lam_im), acc(d_skip)]
    return pl.pallas_call(
        body, out_shape=outs, grid=(nt,),
        in_specs=[row(d_ssm), row(d_ssm), row(half, u_cb), row(half, u_cb + 1), state, state] + [whole(a) for a in consts],
        out_specs=[row(d_ssm)] + [whole(a) for a in consts],
        scratch_shapes=[pltpu.VMEM((NQ, SSM_T * NJ, LANES), f32), pltpu.VMEM((NQ, SSM_T * NJ, LANES), f32),
                        pltpu.VMEM((2, NQ, NJ, LANES), f32)],
        name="ssm_bwd", compiler_params=_params("arbitrary"))(dy, u_bf, proj, proj, s_re, s_im, *consts)


def _cmul(ar, ai, br, bi):
    return ar * br - ai * bi, ar * bi + ai * br


def _disc(ar, ai, logdt):
    dt = jnp.exp(logdt)
    mag = jnp.exp(ar * dt)
    lr, li = mag * jnp.cos(ai * dt), mag * jnp.sin(ai * dt)
    den = ar * ar + ai * ai
    nr, ni = lr - 1.0, li
    fr, fi = (nr * ar + ni * ai) / den, (ni * ar - nr * ai) / den
    return dt, lr, li, den, fr, fi


def ssm_params(a_re, a_im, logdt_b, bt_re, bt_im, spread):
    def body(ar_ref, ai_ref, ld_ref, br_ref, bi_ref, sp_ref, lr_ref, li_ref, or_ref, oi_ref):
        _, lr, li, _, fr, fi = _disc(ar_ref[...], ai_ref[...], ld_ref[...])
        lr_ref[...] = lr
        li_ref[...] = li
        fre = jnp.dot(sp_ref[...], fr, precision=HIGHEST, preferred_element_type=f32)
        fie = jnp.dot(sp_ref[...], fi, precision=HIGHEST, preferred_element_type=f32)
        o_r, o_i = _cmul(fre, fie, br_ref[...], bi_ref[...])
        or_ref[...] = o_r
        oi_ref[...] = o_i

    g = jax.ShapeDtypeStruct(a_re.shape, f32)
    b = jax.ShapeDtypeStruct(bt_re.shape, f32)
    return pl.pallas_call(body, out_shape=[g, g, b, b], name="ssm_params",
                          compiler_params=_params())(a_re, a_im, logdt_b, bt_re, bt_im, spread)


def ssm_params_grad(a_re, a_im, logdt_b, bt_re, bt_im, spread, gather, dlam_re, dlam_im, dbt_re, dbt_im):
    def body(ar_ref, ai_ref, ld_ref, br_ref, bi_ref, sp_ref, ga_ref, glr_ref, gli_ref, gbr_ref, gbi_ref,
             dar_ref, dai_ref, dld_ref, dbr_ref, dbi_ref):
        ar, ai = ar_ref[...], ai_ref[...]
        dt, lr, li, den, fr, fi = _disc(ar, ai, ld_ref[...])
        hdot = functools.partial(jnp.dot, precision=HIGHEST, preferred_element_type=f32)
        fre, fie = hdot(sp_ref[...], fr), hdot(sp_ref[...], fi)
        gbr, gbi, br, bi = gbr_ref[...], gbi_ref[...], br_ref[...], bi_ref[...]
        dbr_ref[...], dbi_ref[...] = _cmul(fre, -fie, gbr, gbi)
        t_r, t_i = _cmul(br, -bi, gbr, gbi)
        gfr, gfi = hdot(ga_ref[...], t_r), hdot(ga_ref[...], t_i)
        iwr, iwi = ar / den, -ai / den
        x_r, x_i = _cmul(iwr, -iwi, gfr, gfi)
        glr, gli = glr_ref[...] + x_r, gli_ref[...] + x_i
        q_r, q_i = _cmul(fr, fi, iwr, iwi)
        gwr, gwi = _cmul(-q_r, q_i, gfr, gfi)
        y_r, y_i = _cmul(dt * lr, -dt * li, glr, gli)
        dar_ref[...] = gwr + y_r
        dai_ref[...] = gwi + y_i
        wl_r, wl_i = _cmul(ar, ai, lr, li)
        z_r, _ = _cmul(wl_r, -wl_i, glr, gli)
        dld_ref[...] = jnp.sum(z_r * dt, axis=1, keepdims=True)

    g = jax.ShapeDtypeStruct(a_re.shape, f32)
    b = jax.ShapeDtypeStruct(bt_re.shape, f32)
    return pl.pallas_call(body, out_shape=[g, g, jax.ShapeDtypeStruct((a_re.shape[0], 1), f32), b, b], name="ssm_params_grad",
                          compiler_params=_params())(a_re, a_im, logdt_b, bt_re, bt_im, spread, gather, dlam_re, dlam_im, dbt_re, dbt_im)


def _block_diag(t, rows, cols):
    G = t.shape[0]
    t = t.reshape(G // NJ, NJ, rows, cols)
    eye = jnp.eye(NJ, dtype=t.dtype)
    return jnp.einsum('jgrc,gh->jgrhc', t, eye).reshape(G // NJ, NJ * rows, NJ * cols)


def _block_diag_take(m, rows, cols):
    J = m.shape[0]
    m = m.reshape(J, NJ, rows, NJ, cols)
    idx = jnp.arange(NJ)
    return m[:, idx, :, idx, :].transpose(1, 0, 2, 3).reshape(J * NJ, rows, cols)


def _state_layout(t):
    return t.reshape(NJ, NQ, LANES).transpose(1, 0, 2)


def _state_layout_inv(t, G, N):
    return t.transpose(1, 0, 2).reshape(G, N)


def _tiles2d(shape, budget_rows=128):
    rows, cols = shape
    tr = rows
    if rows > budget_rows:
        tr = budget_rows
        while rows % tr:
            tr -= SUBLANES
    return tr, cols


def _adam_update(w, g, m, v):
    c1 = 1.0 - ADAM_B1 ** ADAM_STEP
    c2 = 1.0 - ADAM_B2 ** ADAM_STEP
    nm = ADAM_B1 * m + (1.0 - ADAM_B1) * g
    nv = ADAM_B2 * v + (1.0 - ADAM_B2) * (g * g)
    delta = -ADAM_LR * ((nm / c1) / (jnp.sqrt(nv / c2) + ADAM_EPS) + ADAM_WD * w)
    return delta, nm, nv


def adamw(name, w, g, m, v):
    tr, cols = _tiles2d(w.shape, 128 if w.shape[1] > 1024 else 256)

    def body(w_ref, g_ref, m_ref, v_ref, d_ref, nm_ref, nv_ref):
        d_ref[...], nm_ref[...], nv_ref[...] = _adam_update(w_ref[...], g_ref[...], m_ref[...], v_ref[...])

    spec = pl.BlockSpec((tr, cols), lambda i: (i, 0))
    o = jax.ShapeDtypeStruct(w.shape, f32)
    return pl.pallas_call(body, out_shape=[o, o, o], grid=(w.shape[0] // tr,), in_specs=[spec] * 4, out_specs=[spec] * 3,
                          name=name, compiler_params=_params("parallel"))(w, g, m, v)


def adamw_halves(name, w, own, got, m, v):
    h, cols = own.shape
    tr, _ = _tiles2d((h, cols), 128 if cols > 1024 else 256)
    nh = h // tr

    def body(w_ref, own_ref, got_ref, m_ref, v_ref, g_ref, d_ref, nm_ref, nv_ref):
        mine = (pl.program_id(0) // nh) == lax.axis_index("c")
        g = jnp.where(mine, own_ref[...], got_ref[...])
        g_ref[...] = g
        d_ref[...], nm_ref[...], nv_ref[...] = _adam_update(w_ref[...], g, m_ref[...], v_ref[...])

    spec = pl.BlockSpec((tr, cols), lambda i: (i, 0))
    half = pl.BlockSpec((tr, cols), lambda i: (i % nh, 0))
    o = jax.ShapeDtypeStruct(w.shape, f32)
    return pl.pallas_call(body, out_shape=[o, o, o, o], grid=(2 * nh,), in_specs=[spec, half, half, spec, spec], out_specs=[spec] * 4,
                          name=name, compiler_params=_params("parallel"))(w, own, got, m, v)


def pair_sum(name, g, got, c_arr):
    S, h, cols = got.shape
    tr, _ = _tiles2d((h, cols), 256)
    nh = h // tr

    def body(c_ref, g_ref, o_ref, out_ref):
        out_ref[...] = (g_ref[...].astype(f32) + o_ref[...].astype(f32)).astype(out_ref.dtype)

    spec = pl.BlockSpec((None, tr, cols), lambda s, i, c: (s, i, 0))
    grid_spec = pltpu.PrefetchScalarGridSpec(
        num_scalar_prefetch=1, grid=(S, nh),
        in_specs=[pl.BlockSpec((None, tr, cols), lambda s, i, c: (s, c[0] * nh + i, 0)), spec], out_specs=spec)
    return pl.pallas_call(body, out_shape=jax.ShapeDtypeStruct(got.shape, g.dtype), grid_spec=grid_spec, name=name,
                          compiler_params=_params("parallel", "parallel"))(c_arr, g, got)


def chip_sum(name, pair, landed, mine_arr):
    n_in, h, cols = landed.shape
    tr, _ = _tiles2d((h, cols), 256)

    def body(s_ref, p_ref, l_ref, o_ref):
        acc = p_ref[...].astype(f32)
        for k in range(n_in):
            acc = acc + l_ref[k].astype(f32)
        o_ref[...] = acc

    grid_spec = pltpu.PrefetchScalarGridSpec(
        num_scalar_prefetch=1, grid=(h // tr,),
        in_specs=[pl.BlockSpec((None, tr, cols), lambda i, s: (s[0], i, 0)), pl.BlockSpec((n_in, tr, cols), lambda i, s: (0, i, 0))],
        out_specs=pl.BlockSpec((tr, cols), lambda i, s: (i, 0)))
    return pl.pallas_call(body, out_shape=jax.ShapeDtypeStruct((h, cols), f32), grid_spec=grid_spec, name=name,
                          compiler_params=_params("parallel"))(mine_arr, pair, landed)


def cast_into_slot(name, w, mine_arr):
    tr, cols = _tiles2d(w.shape, 256)

    def body(s_ref, w_ref, o_ref):
        o_ref[...] = w_ref[...].astype(bf16)

    grid_spec = pltpu.PrefetchScalarGridSpec(
        num_scalar_prefetch=1, grid=(w.shape[0] // tr,),
        in_specs=[pl.BlockSpec((tr, cols), lambda i, s: (i, 0))],
        out_specs=pl.BlockSpec((None, tr, cols), lambda i, s: (s[0], i, 0)))
    return pl.pallas_call(body, out_shape=jax.ShapeDtypeStruct((N_CHIPS,) + w.shape, bf16), grid_spec=grid_spec, name=name,
                          compiler_params=_params("parallel"))(mine_arr, w)


ANY = pl.BlockSpec(memory_space=pl.ANY)


def _place():
    x, y, c = lax.axis_index("x"), lax.axis_index("y"), lax.axis_index("c")
    return x, y, c


def _other_chips(x, y):
    return [(1 - x, y, 2 * (1 - x) + y), (x, 1 - y, 2 * x + 1 - y), (1 - x, 1 - y, 2 * (1 - x) + 1 - y)]


def gather_weights(bufs):
    nw = len(bufs)

    def body(*refs):
        dst = refs[nw:2 * nw]
        send1, recv1, send2, recv2 = refs[2 * nw:]
        x, y, c = _place()
        mine = 2 * x + y
        chips = _other_chips(x, y)
        sib = (x, y, 1 - c)
        first, passed = [], []
        for w in range(nw):
            h = dst[w].shape[1] // 2
            for k, (px, py, _) in enumerate(chips):
                half = dst[w].at[mine, pl.ds(c * h, h), :]
                cp = pltpu.make_async_remote_copy(src_ref=half, dst_ref=half,
                                                  send_sem=send1.at[w, k], recv_sem=recv1.at[w, k],
                                                  device_id=(px, py, c), device_id_type=MESH)
                cp.start()
                first.append(cp)
        for w in range(nw):
            h = dst[w].shape[1] // 2
            for k, (px, py, s) in enumerate(chips):
                landed = dst[w].at[s, pl.ds(c * h, h), :]
                pltpu.make_async_remote_copy(src_ref=landed, dst_ref=landed, send_sem=send1.at[w, k], recv_sem=recv1.at[w, k],
                                             device_id=(px, py, c), device_id_type=MESH).wait_recv()
                cp = pltpu.make_async_remote_copy(src_ref=landed, dst_ref=landed, send_sem=send2.at[w, k], recv_sem=recv2.at[w, k],
                                                  device_id=sib, device_id_type=MESH)
                cp.start()
                passed.append(cp)
        for w in range(nw):
            h = dst[w].shape[1] // 2
            for k, (px, py, s) in enumerate(chips):
                other = dst[w].at[s, pl.ds((1 - c) * h, h), :]
                pltpu.make_async_remote_copy(src_ref=other, dst_ref=other, send_sem=send2.at[w, k], recv_sem=recv2.at[w, k],
                                             device_id=sib, device_id_type=MESH).wait_recv()
        for cp in first + passed:
            cp.wait_send()

    sem = pltpu.SemaphoreType.DMA((nw, 3))
    return pl.pallas_call(
        body, out_shape=[jax.ShapeDtypeStruct(b.shape, b.dtype) for b in bufs],
        in_specs=[ANY] * nw, out_specs=[ANY] * nw, input_output_aliases={w: w for w in range(nw)},
        scratch_shapes=[sem, sem, sem, sem], name="gather_weights",
        compiler_params=pltpu.CompilerParams(has_side_effects=True))(*bufs)


def swap_halves(name, grads):
    nw = len(grads)

    def body(*refs):
        src, got = refs[:nw], refs[nw:2 * nw]
        send, recv = refs[2 * nw:]
        x, y, c = _place()
        cps = []
        for w in range(nw):
            h = src[w].shape[1] // 2
            cp = pltpu.make_async_remote_copy(src_ref=src[w].at[:, pl.ds((1 - c) * h, h), :], dst_ref=got[w],
                                              send_sem=send.at[w], recv_sem=recv.at[w],
                                              device_id=(x, y, 1 - c), device_id_type=MESH)
            cp.start()
            cps.append(cp)
        for cp in cps:
            cp.wait()

    half = [jax.ShapeDtypeStruct((g.shape[0], g.shape[1] // 2, g.shape[2]), g.dtype) for g in grads]
    sem = pltpu.SemaphoreType.DMA((nw,))
    return pl.pallas_call(
        body, out_shape=half, in_specs=[ANY] * nw, out_specs=[ANY] * nw,
        scratch_shapes=[sem, sem], name=name,
        compiler_params=pltpu.CompilerParams(has_side_effects=True))(*grads)


def scatter_to_owners(parts):
    nw = len(parts)

    def body(*refs):
        src, dst = refs[:nw], refs[nw:2 * nw]
        send, recv = refs[2 * nw:]
        x, y, c = _place()
        cps = []
        for w in range(nw):
            for k, (px, py, s) in enumerate(_other_chips(x, y)):
                cp = pltpu.make_async_remote_copy(src_ref=src[w].at[s], dst_ref=dst[w].at[k],
                                                  send_sem=send.at[w, k], recv_sem=recv.at[w, k],
                                                  device_id=(px, py, c), device_id_type=MESH)
                cp.start()
                cps.append(cp)
        for cp in cps:
            cp.wait()

    sem = pltpu.SemaphoreType.DMA((nw, 3))
    return pl.pallas_call(
        body, out_shape=[jax.ShapeDtypeStruct((N_CHIPS - 1,) + p.shape[1:], p.dtype) for p in parts],
        in_specs=[ANY] * nw, out_specs=[ANY] * nw,
        scratch_shapes=[sem, sem], name="scatter_to_owners",
        compiler_params=pltpu.CompilerParams(has_side_effects=True))(*parts)


def join_halves(name, halves):
    nw = len(halves)

    def body(*refs):
        src, dst = refs[:nw], refs[nw:2 * nw]
        send, recv = refs[2 * nw:]
        x, y, c = _place()
        cps = []
        for w in range(nw):
            cp = pltpu.make_async_remote_copy(src_ref=src[w], dst_ref=dst[w], send_sem=send.at[w], recv_sem=recv.at[w],
                                              device_id=(x, y, 1 - c), device_id_type=MESH)
            cp.start()
            cps.append(cp)
        for cp in cps:
            cp.wait()

    sem = pltpu.SemaphoreType.DMA((nw,))
    return pl.pallas_call(
        body, out_shape=[jax.ShapeDtypeStruct(p.shape, p.dtype) for p in halves], in_specs=[ANY] * nw, out_specs=[ANY] * nw,
        scratch_shapes=[sem, sem], name=name,
        compiler_params=pltpu.CompilerParams(has_side_effects=True))(*halves)


SEM = pl.BlockSpec(memory_space=pltpu.SEMAPHORE)
VM = pl.BlockSpec(memory_space=pltpu.VMEM)
DATAFLOW = pltpu.SideEffectType.DATAFLOW_SIDE_EFFECTING
TOKEN = jax.ShapeDtypeStruct((SUBLANES, LANES), f32)


def _gather_copy(buf, w, k, chip, c, mine, send, recv):
    px, py, _ = chip
    h = buf.shape[1] // 2
    half = buf.at[mine, pl.ds(c * h, h), :]
    return pltpu.make_async_remote_copy(src_ref=half, dst_ref=half, send_sem=send.at[3 * w + k], recv_sem=recv.at[3 * w + k],
                                        device_id=(px, py, c), device_id_type=MESH)


def _gather_landing(buf, w, k, chip, c, send, recv):
    px, py, s = chip
    h = buf.shape[1] // 2
    landed = buf.at[s, pl.ds(c * h, h), :]
    return pltpu.make_async_remote_copy(src_ref=landed, dst_ref=landed, send_sem=send.at[3 * w + k], recv_sem=recv.at[3 * w + k],
                                        device_id=(px, py, c), device_id_type=MESH)


def gather_start(bufs, groups, after):
    nw, ng = len(bufs), len(groups)

    def body(*refs):
        outs = refs[nw + 1:]
        sems, dst = outs[:2 * ng], outs[2 * ng:2 * ng + nw]
        token = outs[2 * ng + nw]
        x, y, c = _place()
        mine = 2 * x + y
        for g, members in enumerate(groups):
            for i, w in enumerate(members):
                for k, chip in enumerate(_other_chips(x, y)):
                    _gather_copy(dst[w], i, k, chip, c, mine, sems[2 * g], sems[2 * g + 1]).start()
        token[...] = jnp.zeros_like(token)

    sem_shapes = []
    for members in groups:
        sem_shapes += [pltpu.SemaphoreType.DMA((3 * len(members),))] * 2
    outs = pl.pallas_call(
        body, out_shape=sem_shapes + [jax.ShapeDtypeStruct(b.shape, b.dtype) for b in bufs] + [TOKEN],
        in_specs=[ANY] * (nw + 1), out_specs=[SEM] * (2 * ng) + [ANY] * nw + [VM],
        input_output_aliases={w: 2 * ng + w for w in range(nw)}, name="gather_start",
        compiler_params=pltpu.CompilerParams(has_side_effects=DATAFLOW))(*bufs, after)
    return [(outs[2 * g], outs[2 * g + 1]) for g in range(ng)], list(outs[2 * ng:2 * ng + nw]), outs[2 * ng + nw]


def gather_wait(name, bufs, send, recv, after):
    nw = len(bufs)

    def body(*refs):
        src = refs[:nw]
        send_ref, recv_ref = refs[nw], refs[nw + 1]
        x, y, c = _place()
        mine = 2 * x + y
        for w in range(nw):
            for k, chip in enumerate(_other_chips(x, y)):
                _gather_copy(src[w], w, k, chip, c, mine, send_ref, recv_ref).wait_send()
                _gather_landing(src[w], w, k, chip, c, send_ref, recv_ref).wait_recv()

    return pl.pallas_call(
        body, out_shape=[jax.ShapeDtypeStruct(b.shape, b.dtype) for b in bufs],
        in_specs=[ANY] * nw + [SEM, SEM, ANY], out_specs=[ANY] * nw,
        input_output_aliases={w: w for w in range(nw)}, name=name,
        compiler_params=pltpu.CompilerParams(has_side_effects=DATAFLOW))(*bufs, send, recv, after)


def gather_forward(name, bufs):
    nw = len(bufs)

    def body(*refs):
        dst = refs[nw:2 * nw]
        send, recv = refs[2 * nw:]
        x, y, c = _place()
        sib = (x, y, 1 - c)
        cps = []
        for w in range(nw):
            h = dst[w].shape[1] // 2
            for k, (_, _, s) in enumerate(_other_chips(x, y)):
                landed = dst[w].at[s, pl.ds(c * h, h), :]
                cp = pltpu.make_async_remote_copy(src_ref=landed, dst_ref=landed, send_sem=send.at[w, k], recv_sem=recv.at[w, k],
                                                  device_id=sib, device_id_type=MESH)
                cp.start()
                cps.append(cp)
        for w in range(nw):
            h = dst[w].shape[1] // 2
            for k, (_, _, s) in enumerate(_other_chips(x, y)):
                other = dst[w].at[s, pl.ds((1 - c) * h, h), :]
                pltpu.make_async_remote_copy(src_ref=other, dst_ref=other, send_sem=send.at[w, k], recv_sem=recv.at[w, k],
                                             device_id=sib, device_id_type=MESH).wait_recv()
        for cp in cps:
            cp.wait_send()

    sem = pltpu.SemaphoreType.DMA((nw, 3))
    return pl.pallas_call(
        body, out_shape=[jax.ShapeDtypeStruct(b.shape, b.dtype) for b in bufs],
        in_specs=[ANY] * nw, out_specs=[ANY] * nw, input_output_aliases={w: w for w in range(nw)},
        scratch_shapes=[sem, sem], name=name, compiler_params=pltpu.CompilerParams(has_side_effects=True))(*bufs)


def _scatter_copy(src, dst, w, k, chip, c, send, recv):
    px, py, s = chip
    return pltpu.make_async_remote_copy(src_ref=src.at[s], dst_ref=dst.at[k], send_sem=send.at[3 * w + k], recv_sem=recv.at[3 * w + k],
                                        device_id=(px, py, c), device_id_type=MESH)


def scatter_start(name, parts):
    nw = len(parts)
    lands = [pltpu.with_memory_space_constraint(lax.empty((N_CHIPS - 1,) + p.shape[1:], p.dtype), pltpu.HBM) for p in parts]

    def body(*refs):
        outs = refs[2 * nw:]
        send, recv = outs[0], outs[1]
        src, dst, token = outs[2:2 + nw], outs[2 + nw:2 + 2 * nw], outs[2 + 2 * nw]
        x, y, c = _place()
        for w in range(nw):
            for k, chip in enumerate(_other_chips(x, y)):
                _scatter_copy(src[w], dst[w], w, k, chip, c, send, recv).start()
        token[...] = jnp.zeros_like(token)

    sem = pltpu.SemaphoreType.DMA((3 * nw,))
    outs = pl.pallas_call(
        body, out_shape=[sem, sem] + [jax.ShapeDtypeStruct(p.shape, p.dtype) for p in parts]
        + [jax.ShapeDtypeStruct(l.shape, l.dtype) for l in lands] + [TOKEN],
        in_specs=[ANY] * (2 * nw), out_specs=[SEM, SEM] + [ANY] * (2 * nw) + [VM],
        input_output_aliases={i: 2 + i for i in range(2 * nw)}, name=name,
        compiler_params=pltpu.CompilerParams(has_side_effects=DATAFLOW))(*parts, *lands)
    return outs[0], outs[1], list(outs[2:2 + nw]), list(outs[2 + nw:2 + 2 * nw]), outs[2 + 2 * nw]


def scatter_wait(name, parts, lands, send, recv, after):
    nw = len(parts)

    def body(*refs):
        src, dst = refs[:nw], refs[nw:2 * nw]
        send_ref, recv_ref = refs[2 * nw], refs[2 * nw + 1]
        x, y, c = _place()
        for w in range(nw):
            for k, chip in enumerate(_other_chips(x, y)):
                cp = _scatter_copy(src[w], dst[w], w, k, chip, c, send_ref, recv_ref)
                cp.wait_send()
                cp.wait_recv()

    outs = pl.pallas_call(
        body, out_shape=[jax.ShapeDtypeStruct(a.shape, a.dtype) for a in list(parts) + list(lands)],
        in_specs=[ANY] * (2 * nw) + [SEM, SEM, ANY], out_specs=[ANY] * (2 * nw),
        input_output_aliases={i: i for i in range(2 * nw)}, name=name,
        compiler_params=pltpu.CompilerParams(has_side_effects=DATAFLOW))(*parts, *lands, send, recv, after)
    return list(outs[:nw]), list(outs[nw:])


def all_reduce_small(v):
    R, n = v.shape

    def body(v_ref, o_ref, all_ref, send_sems, recv_sems, local_sem):
        x, y, c = _place()
        me, sib = (x, y, c), (x, y, 1 - c)
        chips = [(1 - x, y), (x, 1 - y), (1 - x, 1 - y)]

        def rows(px, py, pc):
            return all_ref.at[pl.ds((4 * px + 2 * py + pc) * R, R), :]

        def copy(k, block, to, src=None):
            return pltpu.make_async_remote_copy(src_ref=rows(*block) if src is None else src, dst_ref=rows(*block),
                                                send_sem=send_sems.at[k], recv_sem=recv_sems.at[k],
                                                device_id=to, device_id_type=MESH)

        own = pltpu.make_async_copy(v_ref, rows(*me), local_sem)
        own.start()
        first = [copy(0, me, sib, src=v_ref)]
        first += [copy(1 + j, me, (*chip, c), src=v_ref) for j, chip in enumerate(chips)]
        for cp in first:
            cp.start()
        passed = [copy(4 + j, (*chip, c), sib) for j, chip in enumerate(chips)]
        for j, chip in enumerate(chips):
            copy(1 + j, (*chip, c), me).wait_recv()
            passed[j].start()
        copy(0, sib, me).wait_recv()
        for j, chip in enumerate(chips):
            copy(4 + j, (*chip, 1 - c), me).wait_recv()
        for cp in first + passed:
            cp.wait_send()
        own.wait()
        acc = all_ref[pl.ds(0, R), :]
        for d in range(1, N_DEV):
            acc = acc + all_ref[pl.ds(d * R, R), :]
        o_ref[...] = acc

    vm = pl.BlockSpec(memory_space=pltpu.VMEM)
    return pl.pallas_call(
        body, out_shape=jax.ShapeDtypeStruct((R, n), f32), in_specs=[vm], out_specs=vm,
        scratch_shapes=[pltpu.VMEM((N_DEV * R, n), f32), pltpu.SemaphoreType.DMA((7,)), pltpu.SemaphoreType.DMA((7,)),
                        pltpu.SemaphoreType.DMA],
        name="all_reduce_small", compiler_params=pltpu.CompilerParams(vmem_limit_bytes=VMEM_LIMIT, has_side_effects=True))(v)


def local_step(x, pos, tgt, small, get_w, put_g):
    L, D = x.shape
    d_kv = N_KV_HEADS * HEAD_DIM
    d_ssm = small["d_skip"].shape[1]
    big = {"w_in": get_w("w_in", x)}
    d_attn = big["w_in"].shape[0] * big["w_in"].shape[2] - 2 * d_kv - d_ssm

    def behind(token, operand):
        return operand if token is None else operand + token
    G = d_ssm // SSM_GROUP
    N, P = SSM_STATE, SSM_GROUP
    gbf = bf16

    half_dim = HEAD_DIM // 2
    inv_freq = ROPE_THETA ** (-jnp.arange(half_dim, dtype=f32) / half_dim)
    inv_freq = jnp.tile(inv_freq, LANES // half_dim).reshape(1, LANES)
    sink_b = jnp.broadcast_to(small["sinks"].reshape(-1, 1), (small["sinks"].size, LANES))

    spread = jnp.repeat(jnp.eye(G, dtype=f32), P, axis=0)
    logdt_b = jnp.broadcast_to(small["log_dt"].reshape(G, 1), (G, N))
    bt_re = small["b_re"].reshape(G, N, P).transpose(0, 2, 1).reshape(G * P, N)
    bt_im = small["b_im"].reshape(G, N, P).transpose(0, 2, 1).reshape(G * P, N)
    a_re, a_im = small["a_re"].reshape(G, N), small["a_im"].reshape(G, N)
    lam_re, lam_im, bbt_re, bbt_im = ssm_params(a_re, a_im, logdt_b, bt_re, bt_im, spread)
    bd_re = _block_diag(bbt_re.reshape(G, P, N), P, N).astype(bf16)
    bd_im = _block_diag(bbt_im.reshape(G, P, N), P, N).astype(bf16)
    c_re = small["c_re"].reshape(G, P, N).transpose(0, 2, 1)
    c_im = small["c_im"].reshape(G, P, N).transpose(0, 2, 1)
    cd_re = _block_diag(c_re, N, P).astype(bf16)
    cd_im = _block_diag(c_im, N, P).astype(bf16)
    lam_re_l, lam_im_l = _state_layout(lam_re), _state_layout(lam_im)

    def k1(i, nt, xt, g):
        return (rms_fwd(xt, g),)
    xn = rowwise("pre_mix_norm", k1, L, [full(x)], [small["g_pre_mix"]], [(D, bf16)])[0]
    proj = mm_nn("proj_in", xn, big["w_in"])
    qr, kk, vv, u_bf = qkv_prep(proj, pos, inv_freq, d_attn, d_kv)
    attn = attn_fwd(qr, kk, vv, sink_b)
    u_cb = (d_attn + 2 * d_kv) // (d_ssm // 2)
    y, s_re, s_im = ssm_fwd(u_bf, proj, u_cb, bd_re, bd_im, cd_re, cd_im, lam_re_l, lam_im_l, small["d_skip"])

    def k5(i, nt, yt):
        return (gelu(yt),)
    z_bf = rowwise("ssm_gelu", k5, L, [full(y)], [], [(d_ssm, bf16)])[0]
    big["w_glu"] = get_w("w_glu", z_bf)
    gl = mm_nn("glu_proj", z_bf, big["w_glu"])

    def k6(i, nt, at, yt, glt, bg, ga, gs):
        ssm = gelu(yt) * sigmoid(glt + bg)
        return (jnp.concatenate([rms_fwd(at, ga), rms_fwd(ssm, gs)], axis=1),)
    mixed = rowwise("mix_norms", k6, L, [full(attn), full(y), full(gl)],
                    [small["b_glu"], small["g_attn_out"], small["g_ssm_out"]], [(d_attn + d_ssm, bf16)])[0]
    big["w_o"] = get_w("w_o", mixed)
    mix = mm_nn("proj_out", mixed, big["w_o"])

    def k7(i, nt, xt, mt, gpm, gpf):
        h = xt + rms_fwd(mt, gpm)
        return h, rms_fwd(h, gpf)
    h, hn = rowwise("post_mix", k7, L, [full(x), full(mix)], [small["g_post_mix"], small["g_pre_ffn"]], [(D, f32), (D, bf16)])
    big["w_gate"] = get_w("w_gate", hn)
    gt = mm_nn("ffn_gate", hn, big["w_gate"])
    big["w_up"] = get_w("w_up", gt)
    up = mm_nn("ffn_up", hn, big["w_up"])
    d_ff_dim = gt.shape[1]

    def k8(i, nt, g_t, u_t):
        return (g_t * sigmoid(g_t) * u_t,)
    hid = rowwise("ffn_act", k8, L, [full(gt), full(up)], [], [(d_ff_dim, bf16)], tr=64)[0]
    big["w_down"] = get_w("w_down", hid)
    ff = mm_nn("ffn_down", hid, big["w_down"])

    def k9(i, nt, ht, fft, tt, g):
        out = ht + rms_fwd(fft, g)
        err = out - tt
        per_row = jnp.mean(err * err, axis=-1, keepdims=True)
        loss = 0.5 * jnp.sum(per_row) * jnp.where(_lane((1, LANES)) == 0, 1.0, 0.0)
        d_out = err * (1.0 / D)
        d_ff, dg = rms_bwd(fft, g, d_out)
        return d_out, d_ff, dg, loss
    d_out, d_ff, dg_post_ffn, loss = rowwise("loss_head", k9, L, [full(h), full(ff), full(tgt)], [small["g_post_ffn"]],
                                             [(D, f32), (D, bf16)], reds=[D, LANES])

    d_hid = mm_nt("d_ffn_hidden", d_ff, big["w_down"])
    put_g("w_down", mm_tn("dw_down", hid, d_ff, out_dtype=gbf))

    def k10(i, nt, g_t, u_t, dh_t):
        sg = sigmoid(g_t)
        return dh_t * u_t * (sg * (1.0 + g_t * (1.0 - sg))), dh_t * (g_t * sg)
    d_gt, d_up = rowwise("ffn_act_grad", k10, L, [full(gt), full(up), full(d_hid)], [], [(d_ff_dim, bf16), (d_ff_dim, bf16)], tr=64)
    d_hn_a = mm_nt("d_hn_gate", d_gt, big["w_gate"])
    d_hn_b = mm_nt("d_hn_up", d_up, big["w_up"])
    put_g("w_gate", mm_tn("dw_gate", hn, d_gt, shards=N_CHIPS, out_dtype=gbf))
    token = put_g("w_up", mm_tn("dw_up", hn, d_up, shards=N_CHIPS, out_dtype=gbf))

    def k11(i, nt, ht, da, db, do, mt, gpf, gpm):
        dh_n, dg_pf = rms_bwd(ht, gpf, da + db)
        dh = do + dh_n
        d_mix, dg_pm = rms_bwd(mt, gpm, dh)
        return dh, d_mix, dg_pf, dg_pm
    dh, d_mix, dg_pre_ffn, dg_post_mix = rowwise("post_mix_grad", k11, L, [full(h), full(d_hn_a), full(d_hn_b), full(d_out), full(mix)],
                                                 [behind(token, small["g_pre_ffn"]), small["g_post_mix"]], [(D, f32), (D, bf16)], reds=[D, D])
    d_mixed = mm_nt("d_mixed", d_mix, big["w_o"])
    put_g("w_o", mm_tn("dw_o", mixed, d_mix, out_dtype=gbf))

    def k12(i, nt, at, yt, glt, da_n, ds_n, bg, ga, gs):
        z = gelu(yt)
        sg = sigmoid(glt + bg)
        ssm = z * sg
        d_at, dga = rms_bwd(at, ga, da_n)
        d_ssm_t, dgs = rms_bwd(ssm, gs, ds_n)
        d_gl = d_ssm_t * z * sg * (1.0 - sg)
        return d_at, d_ssm_t * sg, d_gl, dga, dgs, colsum(d_gl)
    d_attn_o, dz1, d_gl, dg_attn, dg_ssm, db_glu = rowwise(
        "mix_norms_grad", k12, L, [full(attn), full(y), full(gl), (d_mixed, d_attn, 0, 0), (d_mixed, d_ssm, d_attn // d_ssm, 0)],
        [small["b_glu"], small["g_attn_out"], small["g_ssm_out"]], [(d_attn, f32), (d_ssm, f32), (d_ssm, bf16)],
        reds=[d_attn, d_ssm, d_ssm])
    dz2 = mm_nt("d_glu_in", d_gl, big["w_glu"])
    put_g("w_glu", mm_tn("dw_glu", z_bf, d_gl, out_dtype=gbf))

    def k13(i, nt, yt, a, b):
        return ((a + b) * gelu_grad(yt),)
    dy = rowwise("ssm_gelu_grad", k13, L, [full(y), full(dz1), full(dz2)], [], [(d_ssm, f32)])[0]
    du, dbd_re, dbd_im, dcd_re, dcd_im, dlam_re_l, dlam_im_l, dd_skip = ssm_bwd(
        dy, u_bf, proj, u_cb, s_re, s_im, bd_re, bd_im, cd_re, cd_im, lam_re_l, lam_im_l, small["d_skip"])
    dq, dkk_c, dkk_p, dvv_c, dvv_p, dsink = attn_bwd(qr, kk, vv, sink_b, attn, d_attn_o)
    d_proj = qkv_grad(dq, dkk_c, dkk_p, dvv_c, dvv_p, du, pos, inv_freq)
    d_xn = mm_nt("d_xn", d_proj, big["w_in"])
    token = put_g("w_in", mm_tn("dw_in", xn, d_proj, shards=N_CHIPS, out_dtype=gbf))

    def k17(i, nt, xt, dxn, dht, g):
        dx, dg = rms_bwd(xt, g, dxn)
        return dht + dx, dg
    grad_x, dg_pre_mix = rowwise("pre_mix_grad", k17, L, [full(x), full(d_xn), full(dh)], [behind(token, small["g_pre_mix"])],
                                 [(D, f32)], reds=[D])

    gather = spread.T
    dbbt_re = _block_diag_take(dbd_re, P, N).reshape(G * P, N)
    dbbt_im = _block_diag_take(dbd_im, P, N).reshape(G * P, N)
    d_a_re, d_a_im, d_logdt, dbt_re, dbt_im = ssm_params_grad(
        a_re, a_im, logdt_b, bt_re, bt_im, spread, gather,
        _state_layout_inv(dlam_re_l, G, N), _state_layout_inv(dlam_im_l, G, N), dbbt_re, dbbt_im)
    q_per_kv = d_attn // HEAD_DIM // N_KV_HEADS
    small_grads = {
        "g_pre_mix": dg_pre_mix, "sinks": dsink[:, :q_per_kv, 0].reshape(1, -1),
        "a_re": d_a_re, "a_im": d_a_im, "log_dt": d_logdt.reshape(1, G),
        "b_re": dbt_re.reshape(G, P, N).transpose(0, 2, 1), "b_im": dbt_im.reshape(G, P, N).transpose(0, 2, 1),
        "c_re": _block_diag_take(dcd_re, N, P).transpose(0, 2, 1), "c_im": _block_diag_take(dcd_im, N, P).transpose(0, 2, 1),
        "d_skip": dd_skip, "b_glu": db_glu, "g_attn_out": dg_attn, "g_ssm_out": dg_ssm,
        "g_post_mix": dg_post_mix, "g_pre_ffn": dg_pre_ffn, "g_post_ffn": dg_post_ffn,
    }
    return loss, grad_x, small_grads


WEIGHTS = ['g_pre_mix', 'w_in', 'sinks', 'a_re', 'a_im', 'log_dt', 'b_re', 'b_im', 'c_re', 'c_im', 'd_skip', 'w_glu', 'b_glu',
           'g_attn_out', 'g_ssm_out', 'w_o', 'g_post_mix', 'g_pre_ffn', 'w_gate', 'w_up', 'w_down', 'g_post_ffn']
BIG = ['w_in', 'w_glu', 'w_o', 'w_gate', 'w_up', 'w_down']
COL_SHARDED = ['w_in', 'w_gate', 'w_up']
SMALL = [n for n in WEIGHTS if n not in BIG]
GATHER_GROUPS = [["w_glu", "w_o", "w_gate"], ["w_up", "w_down"]]
REDUCE_GROUPS = [["w_down", "w_gate", "w_up"], ["w_o", "w_glu", "w_in"]]


PACK_ROWS = 256


def _pack(parts):
    flat = jnp.concatenate([p.reshape(-1) for p in parts])
    pad = (-flat.size) % (PACK_ROWS * LANES)
    return jnp.pad(flat, (0, pad)).reshape(-1, LANES)


def _unpack(packed, shapes):
    flat = packed.reshape(-1)
    out, off = [], 0
    for s in shapes:
        n = int(np.prod(s))
        out.append(flat[off:off + n].reshape(s))
        off += n
    return out


def kernel(x, positions, g_pre_mix, w_in, sinks, a_re, a_im, log_dt, b_re, b_im, c_re, c_im, d_skip, w_glu, b_glu, g_attn_out, g_ssm_out, w_o, g_post_mix, g_pre_ffn, w_gate, w_up, w_down, g_post_ffn, loss_target, m_g_pre_mix, m_w_in, m_sinks, m_a_re, m_a_im, m_log_dt, m_b_re, m_b_im, m_c_re, m_c_im, m_d_skip, m_w_glu, m_b_glu, m_g_attn_out, m_g_ssm_out, m_w_o, m_g_post_mix, m_g_pre_ffn, m_w_gate, m_w_up, m_w_down, m_g_post_ffn, v_g_pre_mix, v_w_in, v_sinks, v_a_re, v_a_im, v_log_dt, v_b_re, v_b_im, v_c_re, v_c_im, v_d_skip, v_w_glu, v_b_glu, v_g_attn_out, v_g_ssm_out, v_w_o, v_g_post_mix, v_g_pre_ffn, v_w_gate, v_w_up, v_w_down, v_g_post_ffn):
    args = dict(locals())
    w = {n: args[n] for n in WEIGHTS}
    m = {n: args["m_" + n] for n in WEIGHTS}
    v = {n: args["v_" + n] for n in WEIGHTS}
    L, D = x.shape[1], x.shape[2]

    ax, ay, ac = _place()
    mine_arr = (2 * ax + ay).astype(jnp.int32).reshape(1)
    c_arr = ac.astype(jnp.int32).reshape(1)

    bufs = {n: cast_into_slot("cast_" + n, w[n][0], mine_arr) for n in BIG}
    bufs["w_in"] = gather_weights([bufs["w_in"]])[0]
    later = [n for n in BIG if n != "w_in"]
    sems, started, token = gather_start([bufs[n] for n in later], [[later.index(n) for n in g] for g in GATHER_GROUPS], bufs["w_in"])
    bufs.update(zip(later, started))
    ready = {"w_in"}

    def get_w(n, after):
        if n not in ready:
            gi = [n in g for g in GATHER_GROUPS].index(True)
            members = GATHER_GROUPS[gi]
            landed = gather_wait("gather_wait_%d" % gi, [bufs[k] for k in members], *sems[gi], after)
            bufs.update(zip(members, gather_forward("gather_forward_%d" % gi, landed)))
            ready.update(members)
        g = bufs[n]
        return g if n in COL_SHARDED else g.reshape(g.shape[0] * g.shape[1], g.shape[2])

    pending, inflight = {}, []

    def put_g(n, g):
        pending[n] = g if n in COL_SHARDED else g.reshape(N_CHIPS, g.shape[0] // N_CHIPS, g.shape[1])
        for gi, members in enumerate(REDUCE_GROUPS):
            if n == members[-1]:
                g3 = [pending[k] for k in members]
                got = swap_halves("swap_halves_%d" % gi, g3)
                pair = [pair_sum("pair_sum_" + k, a, o, c_arr) for k, a, o in zip(members, g3, got)]
                send, recv, parts, lands, tok = scatter_start("scatter_start_%d" % gi, pair)
                inflight.append((members, send, recv, parts, lands))
                return tok[0, 0]
        return None

    small = {n: w[n].reshape(1, -1) for n in SMALL}
    small["g_pre_mix"] = small["g_pre_mix"] + token[0, 0]
    pos = positions.reshape(L, 1).astype(f32)
    loss, grad_x, small_grads = local_step(x[0], pos, loss_target[0], small, get_w, put_g)
    loss = lax.psum(jnp.sum(loss), ("x", "y", "c"))

    shapes = [w[n].shape for n in SMALL]
    small_sum = all_reduce_small(_pack([small_grads[n] for n in SMALL]))
    small_g = dict(zip(SMALL, _unpack(small_sum, shapes)))

    grads, delta, new_m, new_v = {}, {}, {}, {}
    after = small_sum
    for gi, (members, send, recv, parts, lands) in enumerate(inflight):
        parts, landed = scatter_wait("scatter_wait_%d" % gi, parts, lands, send, recv, after)
        halves = [chip_sum("chip_sum_" + k, p, t, mine_arr) for k, p, t in zip(members, parts, landed)]
        sib_halves = join_halves("join_halves_%d" % gi, halves)
        for n, own, sib in zip(members, halves, sib_halves):
            g_, d_, m_, v_ = adamw_halves("adamw_" + n, w[n][0], own, sib, m[n][0], v[n][0])
            grads[n], delta[n], new_m[n], new_v[n] = g_[None], d_[None], m_[None], v_[None]
            after = v_
    pw, pm, pv = (_pack([t[n] for n in SMALL]) for t in (w, m, v))
    d_, m_, v_ = adamw("adamw_small", pw, small_sum, pm, pv)
    for t, packed in ((delta, d_), (new_m, m_), (new_v, v_)):
        t.update(zip(SMALL, _unpack(packed, shapes)))
    grads.update(small_g)

    return (loss, grad_x[None], *[grads[n] for n in WEIGHTS], *[delta[n] for n in WEIGHTS],
            *[new_m[n] for n in WEIGHTS], *[new_v[n] for n in WEIGHTS])
```

```python
import functools
import math

import jax
import jax.numpy as jnp
import numpy as np
from jax import lax
from jax.experimental import pallas as pl
from jax.experimental.pallas import tpu as pltpu

f32 = jnp.float32
bf16 = jnp.bfloat16
HIGHEST = lax.Precision.HIGHEST
MESH = pl.DeviceIdType.MESH

HEAD_DIM = 64
N_KV_HEADS = 4
ATTN_BLOCK = 128
ROPE_THETA = 10000.0
SSM_GROUP = 16
SSM_STATE = 64
RMS_EPS = 1e-6
LANES = 128
SUBLANES = 8
VMEM_LIMIT = 52 * 1024 * 1024
N_CHIPS = 4
N_DEV = 8
NEG = -1e30

ADAM_LR, ADAM_B1, ADAM_B2, ADAM_EPS, ADAM_WD, ADAM_STEP = 0.001, 0.9, 0.999, 1e-08, 0.01, 10

NN = (((1,), (0,)), ((), ()))
NT = (((1,), (1,)), ((), ()))
TN = (((0,), (0,)), ((), ()))


def _params(*sem):
    return pltpu.CompilerParams(dimension_semantics=sem or None, vmem_limit_bytes=VMEM_LIMIT)


def _dot(a, b, dims=NN):
    return lax.dot_general(a, b, dims, preferred_element_type=f32)


def _pick(dim, pref):
    t = min(dim, pref)
    while dim % t:
        t -= LANES
    assert t > 0, (dim, pref)
    return t


def _mm_call(name, grid, in_specs, out_spec, out_shape, acc_shape, dims, operands):
    nk = grid[2]

    def body_one(a_ref, b_ref, o_ref):
        o_ref[...] = _dot(a_ref[...], b_ref[...], dims).astype(o_ref.dtype)

    def body(a_ref, b_ref, o_ref, acc_ref):
        k = pl.program_id(2)

        @pl.when(k == 0)
        def _():
            acc_ref[...] = _dot(a_ref[...], b_ref[...], dims)

        @pl.when((k > 0) & (k < nk - 1))
        def _():
            acc_ref[...] += _dot(a_ref[...], b_ref[...], dims)

        @pl.when(k == nk - 1)
        def _():
            o_ref[...] = (acc_ref[...] + _dot(a_ref[...], b_ref[...], dims)).astype(o_ref.dtype)

    return pl.pallas_call(
        body_one if nk == 1 else body, out_shape=out_shape, grid=grid, in_specs=in_specs, out_specs=out_spec,
        scratch_shapes=[] if nk == 1 else [pltpu.VMEM(acc_shape, f32)], name=name,
        compiler_params=_params("parallel", "parallel", "arbitrary"))(*operands)


def mm_nt_pair(name, a1, b1, a2, b2, tm=1024, tk=1024):
    M = a1.shape[0]
    S, K, n = b1.shape
    tm, tko = _pick(M, tm), _pick(K, tk)
    nk = 2 * S

    def body(a1_ref, b1_ref, a2_ref, b2_ref, o_ref, acc_ref):
        k = pl.program_id(2)

        @pl.when(k == 0)
        def _():
            acc_ref[...] = _dot(a1_ref[...], b1_ref[...], NT)

        @pl.when((k > 0) & (k < S))
        def _():
            acc_ref[...] += _dot(a1_ref[...], b1_ref[...], NT)

        @pl.when((k >= S) & (k < nk - 1))
        def _():
            acc_ref[...] += _dot(a2_ref[...], b2_ref[...], NT)

        @pl.when(k == nk - 1)
        def _():
            o_ref[...] = acc_ref[...] + _dot(a2_ref[...], b2_ref[...], NT)

    first = lambda k: jnp.minimum(k, S - 1)
    second = lambda k: jnp.maximum(k - S, 0)
    in_specs = [pl.BlockSpec((tm, n), lambda i, j, k: (i, first(k))), pl.BlockSpec((None, tko, n), lambda i, j, k: (first(k), j, 0)),
                pl.BlockSpec((tm, n), lambda i, j, k: (i, second(k))), pl.BlockSpec((None, tko, n), lambda i, j, k: (second(k), j, 0))]
    return pl.pallas_call(
        body, out_shape=jax.ShapeDtypeStruct((M, K), f32), grid=(M // tm, K // tko, nk), in_specs=in_specs,
        out_specs=pl.BlockSpec((tm, tko), lambda i, j, k: (i, j)), scratch_shapes=[pltpu.VMEM((tm, tko), f32)], name=name,
        compiler_params=_params("parallel", "parallel", "arbitrary"))(a1, b1, a2, b2)


def mm_nn(name, a, b, out_dtype=f32, tm=1024, tn=1024, tk=2048):
    M, K = a.shape
    tm, tk = _pick(M, tm), _pick(K, tk)
    if b.ndim == 3:
        S, _, n = b.shape
        tn = _pick(n, 2048)
        per = n // tn
        b_spec = pl.BlockSpec((None, tk, tn), lambda i, j, k: (j // per, k, j % per))
        N = S * n
    else:
        N = b.shape[1]
        tn = _pick(N, tn)
        b_spec = pl.BlockSpec((tk, tn), lambda i, j, k: (k, j))
    grid = (M // tm, N // tn, K // tk)
    return _mm_call(name, grid, [pl.BlockSpec((tm, tk), lambda i, j, k: (i, k)), b_spec],
                    pl.BlockSpec((tm, tn), lambda i, j, k: (i, j)), jax.ShapeDtypeStruct((M, N), out_dtype),
                    (tm, tn), NN, (a, b))


def mm_nt(name, a, b, out_dtype=f32, tm=1024, tn=2048, tk=1024):
    M, N = a.shape
    tm = _pick(M, tm)
    if b.ndim == 3:
        S, K, n = b.shape
        tr = _pick(n, 2048)
        per = n // tr
        tko = _pick(K, tk)
        b_spec = pl.BlockSpec((None, tko, tr), lambda i, j, k: (k // per, j, k % per))
    else:
        K = b.shape[0]
        tr = _pick(N, tn)
        tko = _pick(K, tk)
        b_spec = pl.BlockSpec((tko, tr), lambda i, j, k: (j, k))
    grid = (M // tm, K // tko, N // tr)
    return _mm_call(name, grid, [pl.BlockSpec((tm, tr), lambda i, j, k: (i, k)), b_spec],
                    pl.BlockSpec((tm, tko), lambda i, j, k: (i, j)), jax.ShapeDtypeStruct((M, K), out_dtype),
                    (tm, tko), NT, (a, b))


def mm_tn(name, a, b, shards=None, out_dtype=f32, tm=1024, tn=1024, tl=2048):
    L, K = a.shape
    N = b.shape[1]
    tl, tko = _pick(L, tl), _pick(K, tm)
    if shards:
        n = N // shards
        tn = _pick(n, 2048)
        per = n // tn
        o_spec = pl.BlockSpec((None, tko, tn), lambda i, j, k: (j // per, i, j % per))
        o_shape = jax.ShapeDtypeStruct((shards, K, n), out_dtype)
    else:
        tn = _pick(N, tn)
        o_spec = pl.BlockSpec((tko, tn), lambda i, j, k: (i, j))
        o_shape = jax.ShapeDtypeStruct((K, N), out_dtype)
    grid = (K // tko, N // tn, L // tl)
    return _mm_call(name, grid, [pl.BlockSpec((tl, tko), lambda i, j, k: (k, i)),
                                 pl.BlockSpec((tl, tn), lambda i, j, k: (k, j))],
                    o_spec, o_shape, (tko, tn), TN, (a, b))


def ffn_hidden(hn, w_gate, w_up, tm=512):
    M, K = hn.shape
    S, _, n = w_gate.shape
    tm = _pick(M, tm)

    def body(a_ref, g_ref, u_ref, gt_ref, up_ref, hid_ref):
        a = a_ref[...]
        g = _dot(a, g_ref[...])
        u = _dot(a, u_ref[...])
        gt_ref[...] = g.astype(bf16)
        up_ref[...] = u.astype(bf16)
        hid_ref[...] = (g * sigmoid(g) * u).astype(bf16)

    w_spec = pl.BlockSpec((None, K, n), lambda s, i: (s, 0, 0))
    o_spec = pl.BlockSpec((tm, n), lambda s, i: (i, s))
    o = jax.ShapeDtypeStruct((M, S * n), bf16)
    return pl.pallas_call(
        body, out_shape=[o, o, o], grid=(S, M // tm), in_specs=[pl.BlockSpec((tm, K), lambda s, i: (i, 0)), w_spec, w_spec],
        out_specs=[o_spec, o_spec, o_spec], name="ffn_hidden", compiler_params=_params("parallel", "parallel"))(hn, w_gate, w_up)


def ffn_hidden_grad(d_ff, w_down, gt, up, tm=512):
    M, D = d_ff.shape
    F = w_down.shape[0]
    n = _pick(F // N_CHIPS, 2048)
    tm = _pick(M, tm)

    def body(a_ref, b_ref, gt_ref, up_ref, dg_ref, du_ref):
        dh = _dot(a_ref[...], b_ref[...], NT)
        g = gt_ref[...].astype(f32)
        sg = sigmoid(g)
        dg_ref[...] = (dh * up_ref[...].astype(f32) * (sg * (1.0 + g * (1.0 - sg)))).astype(bf16)
        du_ref[...] = (dh * (g * sg)).astype(bf16)

    t_spec = pl.BlockSpec((tm, n), lambda j, i: (i, j))
    o = jax.ShapeDtypeStruct((M, F), bf16)
    return pl.pallas_call(
        body, out_shape=[o, o], grid=(F // n, M // tm),
        in_specs=[pl.BlockSpec((tm, D), lambda j, i: (i, 0)), pl.BlockSpec((n, D), lambda j, i: (j, 0)), t_spec, t_spec],
        out_specs=[t_spec, t_spec], name="ffn_hidden_grad", compiler_params=_params("parallel", "parallel"))(d_ff, w_down, gt, up)


def rowwise(name, fn, L, rows, bcast, outs, reds=(), tr=128):
    tr = min(tr, L)
    nt = L // tr
    n_rows, n_b, n_o = len(rows), len(bcast), len(outs)

    def body(*refs):
        i = pl.program_id(0)
        ins = [r[...] for r in refs[:n_rows + n_b]]
        res = fn(i, nt, *ins)
        o_refs = refs[n_rows + n_b:]
        for k in range(n_o):
            o_refs[k][...] = res[k].astype(o_refs[k].dtype)
        if reds:
            @pl.when(i == 0)
            def _():
                for k in range(len(reds)):
                    o_refs[n_o + k][...] = jnp.zeros_like(o_refs[n_o + k])
            for k in range(len(reds)):
                o_refs[n_o + k][...] += res[n_o + k]

    def row_spec(width, cb, shift):
        if shift:
            return pl.BlockSpec((tr, width), lambda i: (jnp.minimum(i + shift, nt - 1), cb))
        return pl.BlockSpec((tr, width), lambda i: (i, cb))

    in_specs = [row_spec(w, cb, sh) for (_, w, cb, sh) in rows]
    in_specs += [pl.BlockSpec(b.shape, lambda i: (0, 0)) for b in bcast]
    out_specs = [pl.BlockSpec((tr, w), lambda i: (i, 0)) for (w, _) in outs]
    out_specs += [pl.BlockSpec((1, w), lambda i: (0, 0)) for w in reds]
    out_shape = [jax.ShapeDtypeStruct((L, w), dt) for (w, dt) in outs]
    out_shape += [jax.ShapeDtypeStruct((1, w), f32) for w in reds]
    return pl.pallas_call(
        body, out_shape=out_shape, grid=(nt,), in_specs=in_specs, out_specs=out_specs, name=name,
        compiler_params=_params("arbitrary"))(*[r[0] for r in rows], *bcast)


def full(a):
    return (a, a.shape[1], 0, 0)


def colsum(v):
    return jnp.sum(v, axis=0, keepdims=True)


def rms_fwd(x, g):
    r = lax.rsqrt(jnp.mean(x * x, axis=-1, keepdims=True) + RMS_EPS)
    return x * r * g


def rms_bwd(x, g, dy):
    r = lax.rsqrt(jnp.mean(x * x, axis=-1, keepdims=True) + RMS_EPS)
    xh = x * r
    dyg = dy * g
    dx = r * (dyg - xh * jnp.mean(dyg * xh, axis=-1, keepdims=True))
    return dx, colsum(dy * xh)


GELU_C = math.sqrt(2.0 / math.pi)


def gelu(y):
    return y * (0.5 * (1.0 + jnp.tanh(GELU_C * (y + 0.044715 * (y * y * y)))))


def gelu_grad(y):
    t = jnp.tanh(GELU_C * (y + 0.044715 * (y * y * y)))
    return 0.5 * (1.0 + t) + 0.5 * y * (1.0 - t * t) * (GELU_C * (1.0 + 3 * 0.044715 * (y * y)))


def sigmoid(v):
    return 1.0 / (1.0 + jnp.exp(-v))


def _lane(shape):
    return lax.broadcasted_iota(jnp.int32, shape, 1)


def _rot_chunk(t, cos, sin_signed):
    first = (_lane(t.shape) % HEAD_DIM) < (HEAD_DIM // 2)
    partner = jnp.where(first, pltpu.roll(t, LANES - HEAD_DIM // 2, 1), pltpu.roll(t, HEAD_DIM // 2, 1))
    return t * cos + partner * sin_signed


def _cos_sin(pos, inv_freq, inverse):
    ang = pos * inv_freq
    cos, sin = jnp.cos(ang), jnp.sin(ang)
    first = (_lane(ang.shape) % HEAD_DIM) < (HEAD_DIM // 2)
    sign = jnp.where(first, -1.0, 1.0) * (-1.0 if inverse else 1.0)
    return cos, sin * sign


def _dup_head(chunk, odd):
    low = _lane(chunk.shape) < HEAD_DIM
    x = jnp.where(low != odd, chunk, 0.0)
    return x + pltpu.roll(x, HEAD_DIM, 1)


def _chunks(v):
    return [v[:, LANES * c:LANES * (c + 1)] for c in range(v.shape[1] // LANES)]


def qkv_prep(proj, pos, inv_freq, d_attn, d_kv):
    L = proj.shape[0]
    d_ssm = proj.shape[1] - d_attn - 2 * d_kv
    half = d_ssm // 2
    scale = 1.0 / math.sqrt(HEAD_DIM)

    def fn(i, nt, q, k, v, u0, u1, p, invf):
        cos, sin = _cos_sin(p, invf, False)
        qr = jnp.concatenate([_rot_chunk(c, cos, sin) for c in _chunks(q)], axis=1) * scale
        kr = [_rot_chunk(c, cos, sin) for c in _chunks(k)]
        kk = jnp.concatenate([_dup_head(c, odd) for c in kr for odd in (False, True)], axis=1)
        vv = jnp.concatenate([_dup_head(c, odd) for c in _chunks(v) for odd in (False, True)], axis=1)
        return qr, kk, vv, jnp.concatenate([u0, u1], axis=1)

    u_cb = (d_attn + 2 * d_kv) // half
    return rowwise("qkv_prep", fn, L,
                   [(proj, d_attn, 0, 0), (proj, d_kv, d_attn // d_kv, 0), (proj, d_kv, d_attn // d_kv + 1, 0),
                    (proj, half, u_cb, 0), (proj, half, u_cb + 1, 0), full(pos)],
                   [inv_freq], [(d_attn, bf16), (2 * d_kv, bf16), (2 * d_kv, bf16), (d_ssm, bf16)])


def qkv_grad(dq, dkk_c, dkk_p, dvv_c, dvv_p, du, pos, inv_freq):
    L, d_attn = dq.shape
    d_kv = dkk_c.shape[1] // 2
    scale = 1.0 / math.sqrt(HEAD_DIM)

    def fold(cur, prev, i, nt):
        t = cur + jnp.where(i < nt - 1, prev, 0.0)
        out = []
        for c in range(t.shape[1] // (2 * LANES)):
            even, odd = t[:, 2 * c * LANES:(2 * c + 1) * LANES], t[:, (2 * c + 1) * LANES:(2 * c + 2) * LANES]
            even, odd = even + pltpu.roll(even, HEAD_DIM, 1), odd + pltpu.roll(odd, HEAD_DIM, 1)
            out.append(jnp.where(_lane(even.shape) < HEAD_DIM, even, odd))
        return out

    def fn(i, nt, dq_t, kc, kp, vc, vp, du_t, p, invf):
        cos, sin = _cos_sin(p, invf, True)
        dq_o = jnp.concatenate([_rot_chunk(c, cos, sin) for c in _chunks(dq_t)], axis=1) * scale
        dk_o = jnp.concatenate([_rot_chunk(c, cos, sin) for c in fold(kc, kp, i, nt)], axis=1)
        dv_o = jnp.concatenate(fold(vc, vp, i, nt), axis=1)
        return (jnp.concatenate([dq_o, dk_o, dv_o, du_t], axis=1),)

    return rowwise("qkv_grad", fn, L,
                   [full(dq), full(dkk_c), (dkk_p, 2 * d_kv, 0, 1), full(dvv_c), (dvv_p, 2 * d_kv, 0, 1), full(du), full(pos)],
                   [inv_freq], [(d_attn + 2 * d_kv + du.shape[1], bf16)], tr=ATTN_BLOCK)[0]


def _attn_specs(L):
    nb = L // ATTN_BLOCK
    B = ATTN_BLOCK
    q_spec = pl.BlockSpec((B, 2 * LANES), lambda h, n: (n, h))
    cur = pl.BlockSpec((B, LANES), lambda h, n: (n, h))
    prev = pl.BlockSpec((B, LANES), lambda h, n: (jnp.maximum(n - 1, 0), h))
    return nb, q_spec, cur, prev


def _attn_probs(qm, kc, kp, sink, n):
    B = ATTN_BLOCK
    row = lax.broadcasted_iota(jnp.int32, (B, B), 0)
    col = lax.broadcasted_iota(jnp.int32, (B, B), 1)
    sc = jnp.where(row >= col, _dot(qm, kc, NT), NEG)
    sp = jnp.where((col > row) & (n > 0), _dot(qm, kp, NT), NEG)
    m = jnp.maximum(jnp.maximum(jnp.max(sc, axis=1, keepdims=True), jnp.max(sp, axis=1, keepdims=True)), sink)
    pc, pp, ps = jnp.exp(sc - m), jnp.exp(sp - m), jnp.exp(sink - m)
    inv = 1.0 / (jnp.sum(pc, axis=1, keepdims=True) + jnp.sum(pp, axis=1, keepdims=True) + ps)
    return pc, pp, inv, ps


def attn_fwd(qr, kk, vv, sink_b):
    L, d_attn = qr.shape
    nb, q_spec, cur, prev = _attn_specs(L)
    n_kv = kk.shape[1] // LANES
    q_per_kv = d_attn // HEAD_DIM // n_kv

    def body(q_ref, kc_ref, kp_ref, vc_ref, vp_ref, s_ref, o_ref):
        h, n = pl.program_id(0), pl.program_id(1)
        low = _lane((ATTN_BLOCK, LANES)) < HEAD_DIM
        kc, kp, vc, vp = kc_ref[...], kp_ref[...], vc_ref[...], vp_ref[...]
        for pr in range(q_per_kv // 2):
            q2 = q_ref[:, LANES * pr:LANES * (pr + 1)]
            o2 = jnp.zeros((ATTN_BLOCK, LANES), f32)
            for odd in (False, True):
                mine = low != odd
                qm = jnp.where(mine, q2, jnp.zeros_like(q2))
                sink = jnp.max(s_ref[pl.ds(h * q_per_kv + 2 * pr + int(odd), 1), :], axis=1, keepdims=True)
                pc, pp, inv, _ = _attn_probs(qm, kc, kp, sink, n)
                o = _dot(pc.astype(bf16), vc) + _dot(pp.astype(bf16), vp)
                o2 = o2 + jnp.where(mine, o * inv, 0.0)
            o_ref[:, LANES * pr:LANES * (pr + 1)] = o2

    return pl.pallas_call(
        body, out_shape=jax.ShapeDtypeStruct((L, d_attn), f32), grid=(n_kv, nb),
        in_specs=[q_spec, cur, prev, cur, prev, pl.BlockSpec(sink_b.shape, lambda h, n: (0, 0))],
        out_specs=q_spec, name="attn_fwd", compiler_params=_params("parallel", "arbitrary"))(qr, kk, kk, vv, vv, sink_b)


def attn_bwd(qr, kk, vv, sink_b, attn, d_attn_out):
    L, d_attn = qr.shape
    nb, q_spec, cur, prev = _attn_specs(L)
    n_kv = kk.shape[1] // LANES
    q_per_kv = d_attn // HEAD_DIM // n_kv

    def body(q_ref, kc_ref, kp_ref, vc_ref, vp_ref, s_ref, o_ref, do_ref, dq_ref, dkc_ref, dkp_ref, dvc_ref, dvp_ref, ds_ref):
        h, n = pl.program_id(0), pl.program_id(1)
        B = ATTN_BLOCK
        low = _lane((B, LANES)) < HEAD_DIM
        kc, kp, vc, vp = kc_ref[...], kp_ref[...], vc_ref[...], vp_ref[...]
        dkc = jnp.zeros((B, LANES), f32)
        dkp, dvc, dvp = dkc, dkc, dkc
        srow = lax.broadcasted_iota(jnp.int32, (SUBLANES, LANES), 0)
        dsink = jnp.zeros((SUBLANES, LANES), f32)
        for pr in range(q_per_kv // 2):
            q2 = q_ref[:, LANES * pr:LANES * (pr + 1)]
            do2 = do_ref[:, LANES * pr:LANES * (pr + 1)]
            prod = do2 * o_ref[:, LANES * pr:LANES * (pr + 1)]
            dq2 = jnp.zeros((B, LANES), f32)
            for odd in (False, True):
                mine = low != odd
                qm = jnp.where(mine, q2, jnp.zeros_like(q2))
                sink = jnp.max(s_ref[pl.ds(h * q_per_kv + 2 * pr + int(odd), 1), :], axis=1, keepdims=True)
                pc, pp, inv, ps = _attn_probs(qm, kc, kp, sink, n)
                pc, pp = pc * inv, pp * inv
                delta = jnp.sum(jnp.where(mine, prod, 0.0), axis=1, keepdims=True)
                dob = jnp.where(mine, do2, 0.0).astype(bf16)
                dsc = (pc * (_dot(dob, vc, NT) - delta)).astype(bf16)
                dsp = (pp * (_dot(dob, vp, NT) - delta)).astype(bf16)
                dq2 = dq2 + jnp.where(mine, _dot(dsc, kc) + _dot(dsp, kp), 0.0)
                dkc = dkc + _dot(dsc, qm, TN)
                dkp = dkp + _dot(dsp, qm, TN)
                dvc = dvc + _dot(pc.astype(bf16), dob, TN)
                dvp = dvp + _dot(pp.astype(bf16), dob, TN)
                dsink = dsink + jnp.where(srow == 2 * pr + int(odd), -jnp.sum(ps * inv * delta), 0.0)
            dq_ref[:, LANES * pr:LANES * (pr + 1)] = dq2
        dkc_ref[...] = dkc
        dkp_ref[...] = dkp
        dvc_ref[...] = dvc
        dvp_ref[...] = dvp

        @pl.when(n == 0)
        def _():
            ds_ref[...] = jnp.zeros_like(ds_ref)

        ds_ref[...] += dsink

    kv_shape = jax.ShapeDtypeStruct(kk.shape, f32)
    return pl.pallas_call(
        body,
        out_shape=[jax.ShapeDtypeStruct((L, d_attn), f32), kv_shape, kv_shape, kv_shape, kv_shape,
                   jax.ShapeDtypeStruct((n_kv, SUBLANES, LANES), f32)],
        grid=(n_kv, nb),
        in_specs=[q_spec, cur, prev, cur, prev, pl.BlockSpec(sink_b.shape, lambda h, n: (0, 0)), q_spec, q_spec],
        out_specs=[q_spec, cur, cur, cur, cur, pl.BlockSpec((None, SUBLANES, LANES), lambda h, n: (h, 0, 0))],
        name="attn_bwd", compiler_params=_params("parallel", "arbitrary"))(qr, kk, kk, vv, vv, sink_b, attn, d_attn_out)


SSM_T = 128
NQ = SUBLANES * SSM_STATE // LANES
NJ = SUBLANES


def _strided_put(ref, j, val):
    for q in range(NQ):
        ref.at[q][pl.ds(j, SSM_T, stride=NJ), :] = val[:, LANES * q:LANES * (q + 1)]


def _strided_get(ref, j):
    return jnp.concatenate([ref.at[q][pl.ds(j, SSM_T, stride=NJ), :] for q in range(NQ)], axis=1)


def _ssm_specs(L, rev):
    nt = L // SSM_T
    idx = (lambda i: nt - 1 - i) if rev else (lambda i: i)
    row = lambda w, cb=0: pl.BlockSpec((SSM_T, w), lambda i: (idx(i), cb))
    state = pl.BlockSpec((NQ, SSM_T * NJ, LANES), lambda i: (0, idx(i), 0))
    whole = lambda a: pl.BlockSpec(a.shape, lambda i: (0,) * a.ndim)
    return nt, row, state, whole


def ssm_fwd(u_bf, proj, u_cb, bd_re, bd_im, cd_re, cd_im, lam_re, lam_im, d_skip):
    L, d_ssm = u_bf.shape
    nt, row, state, whole = _ssm_specs(L, False)
    half = d_ssm // 2
    gw = d_ssm // NJ

    def body(u_ref, u0_ref, u1_ref, bdr, bdi, cdr, cdi, lr_ref, li_ref, d_ref, y_ref, sr_ref, si_ref, carry):
        i = pl.program_id(0)

        @pl.when(i == 0)
        def _():
            carry[...] = jnp.zeros_like(carry)

        for j in range(NJ):
            uj = u_ref[:, gw * j:gw * (j + 1)]
            _strided_put(sr_ref, j, _dot(uj, bdr[j]))
            _strided_put(si_ref, j, _dot(uj, bdi[j]))
        lr = [lr_ref[q] for q in range(NQ)]
        li = [li_ref[q] for q in range(NQ)]

        def step(t, s):
            sr, si = s
            rows = pl.ds(pl.multiple_of(t * NJ, NJ), NJ)
            nr = tuple(lr[q] * sr[q] - li[q] * si[q] + sr_ref[q, rows, :] for q in range(NQ))
            ni = tuple(lr[q] * si[q] + li[q] * sr[q] + si_ref[q, rows, :] for q in range(NQ))
            for q in range(NQ):
                sr_ref[q, rows, :] = nr[q]
                si_ref[q, rows, :] = ni[q]
            return nr, ni

        init = (tuple(carry[0, q] for q in range(NQ)), tuple(carry[1, q] for q in range(NQ)))
        sr, si = lax.fori_loop(0, SSM_T, step, init, unroll=8)
        for q in range(NQ):
            carry[0, q] = sr[q]
            carry[1, q] = si[q]
        uf = jnp.concatenate([u0_ref[...], u1_ref[...]], axis=1)
        for j in range(NJ):
            cols = slice(gw * j, gw * (j + 1))
            yj = _dot(_strided_get(sr_ref, j).astype(bf16), cdr[j]) - _dot(_strided_get(si_ref, j).astype(bf16), cdi[j])
            y_ref[:, cols] = yj + d_ref[:, cols] * uf[:, cols]

    s_shape = jax.ShapeDtypeStruct((NQ, L * NJ, LANES), f32)
    consts = (bd_re, bd_im, cd_re, cd_im, lam_re, lam_im, d_skip)
    return pl.pallas_call(
        body, out_shape=[jax.ShapeDtypeStruct((L, d_ssm), f32), s_shape, s_shape], grid=(nt,),
        in_specs=[row(d_ssm), row(half, u_cb), row(half, u_cb + 1)] + [whole(a) for a in consts],
        out_specs=[row(d_ssm), state, state],
        scratch_shapes=[pltpu.VMEM((2, NQ, NJ, LANES), f32)], name="ssm_fwd",
        compiler_params=_params("arbitrary"))(u_bf, proj, proj, *consts)


def ssm_bwd(dy, u_bf, proj, u_cb, s_re, s_im, bd_re, bd_im, cd_re, cd_im, lam_re, lam_im, d_skip):
    L, d_ssm = dy.shape
    nt, row, state, whole = _ssm_specs(L, True)
    half = d_ssm // 2
    gw = d_ssm // NJ

    def body(dy_ref, u_ref, u0_ref, u1_ref, sr_ref, si_ref, bdr, bdi, cdr, cdi, lr_ref, li_ref, d_ref,
             du_ref, dbdr, dbdi, dcdr, dcdi, dlr, dli, dd_ref, gr_ref, gi_ref, carry):
        i = pl.program_id(0)

        @pl.when(i == 0)
        def _():
            carry[...] = jnp.zeros_like(carry)
            for r in (dbdr, dbdi, dcdr, dcdi, dlr, dli, dd_ref):
                r[...] = jnp.zeros_like(r)

        dyf = dy_ref[...]
        dyb = dyf.astype(bf16)
        for j in range(NJ):
            dyj = dyb[:, gw * j:gw * (j + 1)]
            _strided_put(gr_ref, j, _dot(dyj, cdr[j], NT))
            _strided_put(gi_ref, j, -_dot(dyj, cdi[j], NT))
            dcdr[j] += _dot(_strided_get(sr_ref, j).astype(bf16), dyj, TN)
            dcdi[j] -= _dot(_strided_get(si_ref, j).astype(bf16), dyj, TN)
        lr = [lr_ref[q] for q in range(NQ)]
        li = [li_ref[q] for q in range(NQ)]

        def step(k, c):
            gr, gi, ar, ai = c
            rows = pl.ds(pl.multiple_of((SSM_T - 1 - k) * NJ, NJ), NJ)
            s_r = [sr_ref[q, rows, :] for q in range(NQ)]
            s_i = [si_ref[q, rows, :] for q in range(NQ)]
            ar = tuple(ar[q] + gr[q] * s_r[q] + gi[q] * s_i[q] for q in range(NQ))
            ai = tuple(ai[q] + gi[q] * s_r[q] - gr[q] * s_i[q] for q in range(NQ))
            nr = tuple(gr_ref[q, rows, :] + lr[q] * gr[q] + li[q] * gi[q] for q in range(NQ))
            ni = tuple(gi_ref[q, rows, :] + lr[q] * gi[q] - li[q] * gr[q] for q in range(NQ))
            for q in range(NQ):
                gr_ref[q, rows, :] = nr[q]
                gi_ref[q, rows, :] = ni[q]
            return nr, ni, ar, ai

        zero = tuple(jnp.zeros((NJ, LANES), f32) for _ in range(NQ))
        init = (tuple(carry[0, q] for q in range(NQ)), tuple(carry[1, q] for q in range(NQ)), zero, zero)
        gr, gi, ar, ai = lax.fori_loop(0, SSM_T, step, init, unroll=8)
        for q in range(NQ):
            carry[0, q] = gr[q]
            carry[1, q] = gi[q]
            dlr[q] += ar[q]
            dli[q] += ai[q]
        uf = jnp.concatenate([u0_ref[...], u1_ref[...]], axis=1)
        dd_ref[...] += colsum(dyf * uf)
        for j in range(NJ):
            cols = slice(gw * j, gw * (j + 1))
            gjr, gji = _strided_get(gr_ref, j).astype(bf16), _strided_get(gi_ref, j).astype(bf16)
            du_ref[:, cols] = _dot(gjr, bdr[j], NT) + _dot(gji, bdi[j], NT) + d_ref[:, cols] * dyf[:, cols]
            uj = u_ref[:, cols]
            dbdr[j] += _dot(uj, gjr, TN)
            dbdi[j] += _dot(uj, gji, TN)

    consts = (bd_re, bd_im, cd_re, cd_im, lam_re, lam_im, d_skip)
    acc = lambda a: jax.ShapeDtypeStruct(a.shape, f32)
    outs = [jax.ShapeDtypeStruct((L, d_ssm), f32), acc(bd_re), acc(bd_im), acc(cd_re), acc(cd_im), acc(lam_re), acc(lam_im), acc(d_skip)]
    return pl.pallas_call(
        body, out_shape=outs, grid=(nt,),
        in_specs=[row(d_ssm), row(d_ssm), row(half, u_cb), row(half, u_cb + 1), state, state] + [whole(a) for a in consts],
        out_specs=[row(d_ssm)] + [whole(a) for a in consts],
        scratch_shapes=[pltpu.VMEM((NQ, SSM_T * NJ, LANES), f32), pltpu.VMEM((NQ, SSM_T * NJ, LANES), f32),
                        pltpu.VMEM((2, NQ, NJ, LANES), f32)],
        name="ssm_bwd", compiler_params=_params("arbitrary"))(dy, u_bf, proj, proj, s_re, s_im, *consts)


def _cmul(ar, ai, br, bi):
    return ar * br - ai * bi, ar * bi + ai * br


def _disc(ar, ai, logdt):
    dt = jnp.exp(logdt)
    mag = jnp.exp(ar * dt)
    lr, li = mag * jnp.cos(ai * dt), mag * jnp.sin(ai * dt)
    den = ar * ar + ai * ai
    nr, ni = lr - 1.0, li
    fr, fi = (nr * ar + ni * ai) / den, (ni * ar - nr * ai) / den
    return dt, lr, li, den, fr, fi


def ssm_params(a_re, a_im, logdt_b, bt_re, bt_im, spread):
    def body(ar_ref, ai_ref, ld_ref, br_ref, bi_ref, sp_ref, lr_ref, li_ref, or_ref, oi_ref):
        _, lr, li, _, fr, fi = _disc(ar_ref[...], ai_ref[...], ld_ref[...])
        lr_ref[...] = lr
        li_ref[...] = li
        fre = jnp.dot(sp_ref[...], fr, precision=HIGHEST, preferred_element_type=f32)
        fie = jnp.dot(sp_ref[...], fi, precision=HIGHEST, preferred_element_type=f32)
        o_r, o_i = _cmul(fre, fie, br_ref[...], bi_ref[...])
        or_ref[...] = o_r
        oi_ref[...] = o_i

    g = jax.ShapeDtypeStruct(a_re.shape, f32)
    b = jax.ShapeDtypeStruct(bt_re.shape, f32)
    return pl.pallas_call(body, out_shape=[g, g, b, b], name="ssm_params",
                          compiler_params=_params())(a_re, a_im, logdt_b, bt_re, bt_im, spread)


def ssm_params_grad(a_re, a_im, logdt_b, bt_re, bt_im, spread, gather, dlam_re, dlam_im, dbt_re, dbt_im):
    def body(ar_ref, ai_ref, ld_ref, br_ref, bi_ref, sp_ref, ga_ref, glr_ref, gli_ref, gbr_ref, gbi_ref,
             dar_ref, dai_ref, dld_ref, dbr_ref, dbi_ref):
        ar, ai = ar_ref[...], ai_ref[...]
        dt, lr, li, den, fr, fi = _disc(ar, ai, ld_ref[...])
        hdot = functools.partial(jnp.dot, precision=HIGHEST, preferred_element_type=f32)
        fre, fie = hdot(sp_ref[...], fr), hdot(sp_ref[...], fi)
        gbr, gbi, br, bi = gbr_ref[...], gbi_ref[...], br_ref[...], bi_ref[...]
        dbr_ref[...], dbi_ref[...] = _cmul(fre, -fie, gbr, gbi)
        t_r, t_i = _cmul(br, -bi, gbr, gbi)
        gfr, gfi = hdot(ga_ref[...], t_r), hdot(ga_ref[...], t_i)
        iwr, iwi = ar / den, -ai / den
        x_r, x_i = _cmul(iwr, -iwi, gfr, gfi)
        glr, gli = glr_ref[...] + x_r, gli_ref[...] + x_i
        q_r, q_i = _cmul(fr, fi, iwr, iwi)
        gwr, gwi = _cmul(-q_r, q_i, gfr, gfi)
        y_r, y_i = _cmul(dt * lr, -dt * li, glr, gli)
        dar_ref[...] = gwr + y_r
        dai_ref[...] = gwi + y_i
        wl_r, wl_i = _cmul(ar, ai, lr, li)
        z_r, _ = _cmul(wl_r, -wl_i, glr, gli)
        dld_ref[...] = jnp.sum(z_r * dt, axis=1, keepdims=True)

    g = jax.ShapeDtypeStruct(a_re.shape, f32)
    b = jax.ShapeDtypeStruct(bt_re.shape, f32)
    return pl.pallas_call(body, out_shape=[g, g, jax.ShapeDtypeStruct((a_re.shape[0], 1), f32), b, b], name="ssm_params_grad",
                          compiler_params=_params())(a_re, a_im, logdt_b, bt_re, bt_im, spread, gather, dlam_re, dlam_im, dbt_re, dbt_im)


def _block_diag(t, rows, cols):
    G = t.shape[0]
    t = t.reshape(G // NJ, NJ, rows, cols)
    eye = jnp.eye(NJ, dtype=t.dtype)
    return jnp.einsum('jgrc,gh->jgrhc', t, eye).reshape(G // NJ, NJ * rows, NJ * cols)


def _block_diag_take(m, rows, cols):
    J = m.shape[0]
    m = m.reshape(J, NJ, rows, NJ, cols)
    idx = jnp.arange(NJ)
    return m[:, idx, :, idx, :].transpose(1, 0, 2, 3).reshape(J * NJ, rows, cols)


def _state_layout(t):
    return t.reshape(NJ, NQ, LANES).transpose(1, 0, 2)


def _state_layout_inv(t, G, N):
    return t.transpose(1, 0, 2).reshape(G, N)


def _tiles2d(shape, budget_rows=128):
    rows, cols = shape
    tr = rows
    if rows > budget_rows:
        tr = budget_rows
        while rows % tr:
            tr -= SUBLANES
    return tr, cols


def _adam_update(w, g, m, v):
    c1 = 1.0 - ADAM_B1 ** ADAM_STEP
    c2 = 1.0 - ADAM_B2 ** ADAM_STEP
    nm = ADAM_B1 * m + (1.0 - ADAM_B1) * g
    nv = ADAM_B2 * v + (1.0 - ADAM_B2) * (g * g)
    delta = -ADAM_LR * ((nm / c1) / (jnp.sqrt(nv / c2) + ADAM_EPS) + ADAM_WD * w)
    return delta, nm, nv


def adamw(name, w, g, m, v):
    tr, cols = _tiles2d(w.shape, 128 if w.shape[1] > 1024 else 256)

    def body(w_ref, g_ref, m_ref, v_ref, d_ref, nm_ref, nv_ref):
        d_ref[...], nm_ref[...], nv_ref[...] = _adam_update(w_ref[...], g_ref[...], m_ref[...], v_ref[...])

    spec = pl.BlockSpec((tr, cols), lambda i: (i, 0))
    o = jax.ShapeDtypeStruct(w.shape, f32)
    return pl.pallas_call(body, out_shape=[o, o, o], grid=(w.shape[0] // tr,), in_specs=[spec] * 4, out_specs=[spec] * 3,
                          name=name, compiler_params=_params("parallel"))(w, g, m, v)


def adamw_halves(name, w, own, got, m, v):
    h, cols = own.shape
    tr, _ = _tiles2d((h, cols), 128 if cols > 1024 else 256)
    nh = h // tr

    def body(w_ref, own_ref, got_ref, m_ref, v_ref, g_ref, d_ref, nm_ref, nv_ref):
        mine = (pl.program_id(0) // nh) == lax.axis_index("c")
        g = jnp.where(mine, own_ref[...], got_ref[...])
        g_ref[...] = g
        d_ref[...], nm_ref[...], nv_ref[...] = _adam_update(w_ref[...], g, m_ref[...], v_ref[...])

    spec = pl.BlockSpec((tr, cols), lambda i: (i, 0))
    half = pl.BlockSpec((tr, cols), lambda i: (i % nh, 0))
    o = jax.ShapeDtypeStruct(w.shape, f32)
    return pl.pallas_call(body, out_shape=[o, o, o, o], grid=(2 * nh,), in_specs=[spec, half, half, spec, spec], out_specs=[spec] * 4,
                          name=name, compiler_params=_params("parallel"))(w, own, got, m, v)


def pair_sum(name, g, got, c_arr):
    S, h, cols = got.shape
    tr, _ = _tiles2d((h, cols), 256)
    nh = h // tr

    def body(c_ref, g_ref, o_ref, out_ref):
        out_ref[...] = (g_ref[...].astype(f32) + o_ref[...].astype(f32)).astype(out_ref.dtype)

    spec = pl.BlockSpec((None, tr, cols), lambda s, i, c: (s, i, 0))
    grid_spec = pltpu.PrefetchScalarGridSpec(
        num_scalar_prefetch=1, grid=(S, nh),
        in_specs=[pl.BlockSpec((None, tr, cols), lambda s, i, c: (s, c[0] * nh + i, 0)), spec], out_specs=spec)
    return pl.pallas_call(body, out_shape=jax.ShapeDtypeStruct(got.shape, g.dtype), grid_spec=grid_spec, name=name,
                          compiler_params=_params("parallel", "parallel"))(c_arr, g, got)


def chip_sum(name, pair, landed, mine_arr):
    n_in, h, cols = landed.shape
    tr, _ = _tiles2d((h, cols), 256)

    def body(s_ref, p_ref, l_ref, o_ref):
        acc = p_ref[...].astype(f32)
        for k in range(n_in):
            acc = acc + l_ref[k].astype(f32)
        o_ref[...] = acc

    grid_spec = pltpu.PrefetchScalarGridSpec(
        num_scalar_prefetch=1, grid=(h // tr,),
        in_specs=[pl.BlockSpec((None, tr, cols), lambda i, s: (s[0], i, 0)), pl.BlockSpec((n_in, tr, cols), lambda i, s: (0, i, 0))],
        out_specs=pl.BlockSpec((tr, cols), lambda i, s: (i, 0)))
    return pl.pallas_call(body, out_shape=jax.ShapeDtypeStruct((h, cols), f32), grid_spec=grid_spec, name=name,
                          compiler_params=_params("parallel"))(mine_arr, pair, landed)


def cast_into_slot(name, w, mine_arr):
    tr, cols = _tiles2d(w.shape, 256)

    def body(s_ref, w_ref, o_ref):
        o_ref[...] = w_ref[...].astype(bf16)

    grid_spec = pltpu.PrefetchScalarGridSpec(
        num_scalar_prefetch=1, grid=(w.shape[0] // tr,),
        in_specs=[pl.BlockSpec((tr, cols), lambda i, s: (i, 0))],
        out_specs=pl.BlockSpec((None, tr, cols), lambda i, s: (s[0], i, 0)))
    return pl.pallas_call(body, out_shape=jax.ShapeDtypeStruct((N_CHIPS,) + w.shape, bf16), grid_spec=grid_spec, name=name,
                          compiler_params=_params("parallel"))(mine_arr, w)


ANY = pl.BlockSpec(memory_space=pl.ANY)


def _place():
    x, y, c = lax.axis_index("x"), lax.axis_index("y"), lax.axis_index("c")
    return x, y, c


def _other_chips(x, y):
    return [(1 - x, y, 2 * (1 - x) + y), (x, 1 - y, 2 * x + 1 - y), (1 - x, 1 - y, 2 * (1 - x) + 1 - y)]


def gather_weights(bufs):
    nw = len(bufs)

    def body(*refs):
        dst = refs[nw:2 * nw]
        send1, recv1, send2, recv2 = refs[2 * nw:]
        x, y, c = _place()
        mine = 2 * x + y
        chips = _other_chips(x, y)
        sib = (x, y, 1 - c)
        first, passed = [], []
        for w in range(nw):
            h = dst[w].shape[1] // 2
            for k, (px, py, _) in enumerate(chips):
                half = dst[w].at[mine, pl.ds(c * h, h), :]
                cp = pltpu.make_async_remote_copy(src_ref=half, dst_ref=half,
                                                  send_sem=send1.at[w, k], recv_sem=recv1.at[w, k],
                                                  device_id=(px, py, c), device_id_type=MESH)
                cp.start()
                first.append(cp)
        for w in range(nw):
            h = dst[w].shape[1] // 2
            for k, (px, py, s) in enumerate(chips):
                landed = dst[w].at[s, pl.ds(c * h, h), :]
                pltpu.make_async_remote_copy(src_ref=landed, dst_ref=landed, send_sem=send1.at[w, k], recv_sem=recv1.at[w, k],
                                             device_id=(px, py, c), device_id_type=MESH).wait_recv()
                cp = pltpu.make_async_remote_copy(src_ref=landed, dst_ref=landed, send_sem=send2.at[w, k], recv_sem=recv2.at[w, k],
                                                  device_id=sib, device_id_type=MESH)
                cp.start()
                passed.append(cp)
        for w in range(nw):
            h = dst[w].shape[1] // 2
            for k, (px, py, s) in enumerate(chips):
                other = dst[w].at[s, pl.ds((1 - c) * h, h), :]
                pltpu.make_async_remote_copy(src_ref=other, dst_ref=other, send_sem=send2.at[w, k], recv_sem=recv2.at[w, k],
                                             device_id=sib, device_id_type=MESH).wait_recv()
        for cp in first + passed:
            cp.wait_send()

    sem = pltpu.SemaphoreType.DMA((nw, 3))
    return pl.pallas_call(
        body, out_shape=[jax.ShapeDtypeStruct(b.shape, b.dtype) for b in bufs],
        in_specs=[ANY] * nw, out_specs=[ANY] * nw, input_output_aliases={w: w for w in range(nw)},
        scratch_shapes=[sem, sem, sem, sem], name="gather_weights",
        compiler_params=pltpu.CompilerParams(has_side_effects=True))(*bufs)


def swap_halves(name, grads):
    nw = len(grads)

    def body(*refs):
        src, got = refs[:nw], refs[nw:2 * nw]
        send, recv = refs[2 * nw:]
        x, y, c = _place()
        cps = []
        for w in range(nw):
            h = src[w].shape[1] // 2
            cp = pltpu.make_async_remote_copy(src_ref=src[w].at[:, pl.ds((1 - c) * h, h), :], dst_ref=got[w],
                                              send_sem=send.at[w], recv_sem=recv.at[w],
                                              device_id=(x, y, 1 - c), device_id_type=MESH)
            cp.start()
            cps.append(cp)
        for cp in cps:
            cp.wait()

    half = [jax.ShapeDtypeStruct((g.shape[0], g.shape[1] // 2, g.shape[2]), g.dtype) for g in grads]
    sem = pltpu.SemaphoreType.DMA((nw,))
    return pl.pallas_call(
        body, out_shape=half, in_specs=[ANY] * nw, out_specs=[ANY] * nw,
        scratch_shapes=[sem, sem], name=name,
        compiler_params=pltpu.CompilerParams(has_side_effects=True))(*grads)


def scatter_to_owners(parts):
    nw = len(parts)

    def body(*refs):
        src, dst = refs[:nw], refs[nw:2 * nw]
        send, recv = refs[2 * nw:]
        x, y, c = _place()
        cps = []
        for w in range(nw):
            for k, (px, py, s) in enumerate(_other_chips(x, y)):
                cp = pltpu.make_async_remote_copy(src_ref=src[w].at[s], dst_ref=dst[w].at[k],
                                                  send_sem=send.at[w, k], recv_sem=recv.at[w, k],
                                                  device_id=(px, py, c), device_id_type=MESH)
                cp.start()
                cps.append(cp)
        for cp in cps:
            cp.wait()

    sem = pltpu.SemaphoreType.DMA((nw, 3))
    return pl.pallas_call(
        body, out_shape=[jax.ShapeDtypeStruct((N_CHIPS - 1,) + p.shape[1:], p.dtype) for p in parts],
        in_specs=[ANY] * nw, out_specs=[ANY] * nw,
        scratch_shapes=[sem, sem], name="scatter_to_owners",
        compiler_params=pltpu.CompilerParams(has_side_effects=True))(*parts)


def join_halves(name, halves):
    nw = len(halves)

    def body(*refs):
        src, dst = refs[:nw], refs[nw:2 * nw]
        send, recv = refs[2 * nw:]
        x, y, c = _place()
        cps = []
        for w in range(nw):
            cp = pltpu.make_async_remote_copy(src_ref=src[w], dst_ref=dst[w], send_sem=send.at[w], recv_sem=recv.at[w],
                                              device_id=(x, y, 1 - c), device_id_type=MESH)
            cp.start()
            cps.append(cp)
        for cp in cps:
            cp.wait()

    sem = pltpu.SemaphoreType.DMA((nw,))
    return pl.pallas_call(
        body, out_shape=[jax.ShapeDtypeStruct(p.shape, p.dtype) for p in halves], in_specs=[ANY] * nw, out_specs=[ANY] * nw,
        scratch_shapes=[sem, sem], name=name,
        compiler_params=pltpu.CompilerParams(has_side_effects=True))(*halves)


SEM = pl.BlockSpec(memory_space=pltpu.SEMAPHORE)
VM = pl.BlockSpec(memory_space=pltpu.VMEM)
DATAFLOW = pltpu.SideEffectType.DATAFLOW_SIDE_EFFECTING
TOKEN = jax.ShapeDtypeStruct((SUBLANES, LANES), f32)


def _gather_copy(buf, w, k, chip, c, mine, send, recv):
    px, py, _ = chip
    h = buf.shape[1] // 2
    half = buf.at[mine, pl.ds(c * h, h), :]
    return pltpu.make_async_remote_copy(src_ref=half, dst_ref=half, send_sem=send.at[3 * w + k], recv_sem=recv.at[3 * w + k],
                                        device_id=(px, py, c), device_id_type=MESH)


def _gather_landing(buf, w, k, chip, c, send, recv):
    px, py, s = chip
    h = buf.shape[1] // 2
    landed = buf.at[s, pl.ds(c * h, h), :]
    return pltpu.make_async_remote_copy(src_ref=landed, dst_ref=landed, send_sem=send.at[3 * w + k], recv_sem=recv.at[3 * w + k],
                                        device_id=(px, py, c), device_id_type=MESH)


def gather_start(bufs, groups, after):
    nw, ng = len(bufs), len(groups)

    def body(*refs):
        outs = refs[nw + 1:]
        sems, dst = outs[:2 * ng], outs[2 * ng:2 * ng + nw]
        token = outs[2 * ng + nw]
        x, y, c = _place()
        mine = 2 * x + y
        for g, members in enumerate(groups):
            for i, w in enumerate(members):
                for k, chip in enumerate(_other_chips(x, y)):
                    _gather_copy(dst[w], i, k, chip, c, mine, sems[2 * g], sems[2 * g + 1]).start()
        token[...] = jnp.zeros_like(token)

    sem_shapes = []
    for members in groups:
        sem_shapes += [pltpu.SemaphoreType.DMA((3 * len(members),))] * 2
    outs = pl.pallas_call(
        body, out_shape=sem_shapes + [jax.ShapeDtypeStruct(b.shape, b.dtype) for b in bufs] + [TOKEN],
        in_specs=[ANY] * (nw + 1), out_specs=[SEM] * (2 * ng) + [ANY] * nw + [VM],
        input_output_aliases={w: 2 * ng + w for w in range(nw)}, name="gather_start",
        compiler_params=pltpu.CompilerParams(has_side_effects=DATAFLOW))(*bufs, after)
    return [(outs[2 * g], outs[2 * g + 1]) for g in range(ng)], list(outs[2 * ng:2 * ng + nw]), outs[2 * ng + nw]


def gather_wait(name, bufs, send, recv, after):
    nw = len(bufs)

    def body(*refs):
        src = refs[:nw]
        send_ref, recv_ref = refs[nw], refs[nw + 1]
        x, y, c = _place()
        mine = 2 * x + y
        for w in range(nw):
            for k, chip in enumerate(_other_chips(x, y)):
                _gather_copy(src[w], w, k, chip, c, mine, send_ref, recv_ref).wait_send()
                _gather_landing(src[w], w, k, chip, c, send_ref, recv_ref).wait_recv()

    return pl.pallas_call(
        body, out_shape=[jax.ShapeDtypeStruct(b.shape, b.dtype) for b in bufs],
        in_specs=[ANY] * nw + [SEM, SEM, ANY], out_specs=[ANY] * nw,
        input_output_aliases={w: w for w in range(nw)}, name=name,
        compiler_params=pltpu.CompilerParams(has_side_effects=DATAFLOW))(*bufs, send, recv, after)


def gather_forward(name, bufs):
    nw = len(bufs)

    def body(*refs):
        dst = refs[nw:2 * nw]
        send, recv = refs[2 * nw:]
        x, y, c = _place()
        sib = (x, y, 1 - c)
        cps = []
        for w in range(nw):
            h = dst[w].shape[1] // 2
            for k, (_, _, s) in enumerate(_other_chips(x, y)):
                landed = dst[w].at[s, pl.ds(c * h, h), :]
                cp = pltpu.make_async_remote_copy(src_ref=landed, dst_ref=landed, send_sem=send.at[w, k], recv_sem=recv.at[w, k],
                                                  device_id=sib, device_id_type=MESH)
                cp.start()
                cps.append(cp)
        for w in range(nw):
            h = dst[w].shape[1] // 2
            for k, (_, _, s) in enumerate(_other_chips(x, y)):
                other = dst[w].at[s, pl.ds((1 - c) * h, h), :]
                pltpu.make_async_remote_copy(src_ref=other, dst_ref=other, send_sem=send.at[w, k], recv_sem=recv.at[w, k],
                                             device_id=sib, device_id_type=MESH).wait_recv()
        for cp in cps:
            cp.wait_send()

    sem = pltpu.SemaphoreType.DMA((nw, 3))
    return pl.pallas_call(
        body, out_shape=[jax.ShapeDtypeStruct(b.shape, b.dtype) for b in bufs],
        in_specs=[ANY] * nw, out_specs=[ANY] * nw, input_output_aliases={w: w for w in range(nw)},
        scratch_shapes=[sem, sem], name=name, compiler_params=pltpu.CompilerParams(has_side_effects=True))(*bufs)


def _scatter_copy(src, dst, w, k, chip, c, send, recv):
    px, py, s = chip
    return pltpu.make_async_remote_copy(src_ref=src.at[s], dst_ref=dst.at[k], send_sem=send.at[3 * w + k], recv_sem=recv.at[3 * w + k],
                                        device_id=(px, py, c), device_id_type=MESH)


def scatter_start(name, parts):
    nw = len(parts)
    lands = [pltpu.with_memory_space_constraint(lax.empty((N_CHIPS - 1,) + p.shape[1:], p.dtype), pltpu.HBM) for p in parts]

    def body(*refs):
        outs = refs[2 * nw:]
        send, recv = outs[0], outs[1]
        src, dst, token = outs[2:2 + nw], outs[2 + nw:2 + 2 * nw], outs[2 + 2 * nw]
        x, y, c = _place()
        for w in range(nw):
            for k, chip in enumerate(_other_chips(x, y)):
                _scatter_copy(src[w], dst[w], w, k, chip, c, send, recv).start()
        token[...] = jnp.zeros_like(token)

    sem = pltpu.SemaphoreType.DMA((3 * nw,))
    outs = pl.pallas_call(
        body, out_shape=[sem, sem] + [jax.ShapeDtypeStruct(p.shape, p.dtype) for p in parts]
        + [jax.ShapeDtypeStruct(l.shape, l.dtype) for l in lands] + [TOKEN],
        in_specs=[ANY] * (2 * nw), out_specs=[SEM, SEM] + [ANY] * (2 * nw) + [VM],
        input_output_aliases={i: 2 + i for i in range(2 * nw)}, name=name,
        compiler_params=pltpu.CompilerParams(has_side_effects=DATAFLOW))(*parts, *lands)
    return outs[0], outs[1], list(outs[2:2 + nw]), list(outs[2 + nw:2 + 2 * nw]), outs[2 + 2 * nw]


def scatter_wait(name, parts, lands, send, recv, after):
    nw = len(parts)

    def body(*refs):
        src, dst = refs[:nw], refs[nw:2 * nw]
        send_ref, recv_ref = refs[2 * nw], refs[2 * nw + 1]
        x, y, c = _place()
        for w in range(nw):
            for k, chip in enumerate(_other_chips(x, y)):
                cp = _scatter_copy(src[w], dst[w], w, k, chip, c, send_ref, recv_ref)
                cp.wait_send()
                cp.wait_recv()

    outs = pl.pallas_call(
        body, out_shape=[jax.ShapeDtypeStruct(a.shape, a.dtype) for a in list(parts) + list(lands)],
        in_specs=[ANY] * (2 * nw) + [SEM, SEM, ANY], out_specs=[ANY] * (2 * nw),
        input_output_aliases={i: i for i in range(2 * nw)}, name=name,
        compiler_params=pltpu.CompilerParams(has_side_effects=DATAFLOW))(*parts, *lands, send, recv, after)
    return list(outs[:nw]), list(outs[nw:])


def all_reduce_small(v):
    R, n = v.shape

    def body(v_ref, o_ref, all_ref, send_sems, recv_sems, local_sem):
        x, y, c = _place()
        me, sib = (x, y, c), (x, y, 1 - c)
        chips = [(1 - x, y), (x, 1 - y), (1 - x, 1 - y)]

        def rows(px, py, pc):
            return all_ref.at[pl.ds((4 * px + 2 * py + pc) * R, R), :]

        def copy(k, block, to, src=None):
            return pltpu.make_async_remote_copy(src_ref=rows(*block) if src is None else src, dst_ref=rows(*block),
                                                send_sem=send_sems.at[k], recv_sem=recv_sems.at[k],
                                                device_id=to, device_id_type=MESH)

        own = pltpu.make_async_copy(v_ref, rows(*me), local_sem)
        own.start()
        first = [copy(0, me, sib, src=v_ref)]
        first += [copy(1 + j, me, (*chip, c), src=v_ref) for j, chip in enumerate(chips)]
        for cp in first:
            cp.start()
        passed = [copy(4 + j, (*chip, c), sib) for j, chip in enumerate(chips)]
        for j, chip in enumerate(chips):
            copy(1 + j, (*chip, c), me).wait_recv()
            passed[j].start()
        copy(0, sib, me).wait_recv()
        for j, chip in enumerate(chips):
            copy(4 + j, (*chip, 1 - c), me).wait_recv()
        for cp in first + passed:
            cp.wait_send()
        own.wait()
        acc = all_ref[pl.ds(0, R), :]
        for d in range(1, N_DEV):
            acc = acc + all_ref[pl.ds(d * R, R), :]
        o_ref[...] = acc

    vm = pl.BlockSpec(memory_space=pltpu.VMEM)
    return pl.pallas_call(
        body, out_shape=jax.ShapeDtypeStruct((R, n), f32), in_specs=[vm], out_specs=vm,
        scratch_shapes=[pltpu.VMEM((N_DEV * R, n), f32), pltpu.SemaphoreType.DMA((7,)), pltpu.SemaphoreType.DMA((7,)),
                        pltpu.SemaphoreType.DMA],
        name="all_reduce_small", compiler_params=pltpu.CompilerParams(vmem_limit_bytes=VMEM_LIMIT, has_side_effects=True))(v)


def local_step(x, pos, tgt, small, get_w, put_g):
    L, D = x.shape
    d_kv = N_KV_HEADS * HEAD_DIM
    d_ssm = small["d_skip"].shape[1]
    big = {"w_in": get_w("w_in", x)}
    d_attn = big["w_in"].shape[0] * big["w_in"].shape[2] - 2 * d_kv - d_ssm

    def behind(token, operand):
        return operand if token is None else operand + token
    G = d_ssm // SSM_GROUP
    N, P = SSM_STATE, SSM_GROUP
    gbf = bf16

    half_dim = HEAD_DIM // 2
    inv_freq = ROPE_THETA ** (-jnp.arange(half_dim, dtype=f32) / half_dim)
    inv_freq = jnp.tile(inv_freq, LANES // half_dim).reshape(1, LANES)
    sink_b = jnp.broadcast_to(small["sinks"].reshape(-1, 1), (small["sinks"].size, LANES))

    spread = jnp.repeat(jnp.eye(G, dtype=f32), P, axis=0)
    logdt_b = jnp.broadcast_to(small["log_dt"].reshape(G, 1), (G, N))
    bt_re = small["b_re"].reshape(G, N, P).transpose(0, 2, 1).reshape(G * P, N)
    bt_im = small["b_im"].reshape(G, N, P).transpose(0, 2, 1).reshape(G * P, N)
    a_re, a_im = small["a_re"].reshape(G, N), small["a_im"].reshape(G, N)
    lam_re, lam_im, bbt_re, bbt_im = ssm_params(a_re, a_im, logdt_b, bt_re, bt_im, spread)
    bd_re = _block_diag(bbt_re.reshape(G, P, N), P, N).astype(bf16)
    bd_im = _block_diag(bbt_im.reshape(G, P, N), P, N).astype(bf16)
    c_re = small["c_re"].reshape(G, P, N).transpose(0, 2, 1)
    c_im = small["c_im"].reshape(G, P, N).transpose(0, 2, 1)
    cd_re = _block_diag(c_re, N, P).astype(bf16)
    cd_im = _block_diag(c_im, N, P).astype(bf16)
    lam_re_l, lam_im_l = _state_layout(lam_re), _state_layout(lam_im)

    def k1(i, nt, xt, g):
        return (rms_fwd(xt, g),)
    xn = rowwise("pre_mix_norm", k1, L, [full(x)], [small["g_pre_mix"]], [(D, bf16)])[0]
    proj = mm_nn("proj_in", xn, big["w_in"])
    qr, kk, vv, u_bf = qkv_prep(proj, pos, inv_freq, d_attn, d_kv)
    attn = attn_fwd(qr, kk, vv, sink_b)
    u_cb = (d_attn + 2 * d_kv) // (d_ssm // 2)
    y, s_re, s_im = ssm_fwd(u_bf, proj, u_cb, bd_re, bd_im, cd_re, cd_im, lam_re_l, lam_im_l, small["d_skip"])

    def k5(i, nt, yt):
        return (gelu(yt),)
    z_bf = rowwise("ssm_gelu", k5, L, [full(y)], [], [(d_ssm, bf16)])[0]
    big["w_glu"] = get_w("w_glu", z_bf)
    gl = mm_nn("glu_proj", z_bf, big["w_glu"])

    def k6(i, nt, at, yt, glt, bg, ga, gs):
        ssm = gelu(yt) * sigmoid(glt + bg)
        return (jnp.concatenate([rms_fwd(at, ga), rms_fwd(ssm, gs)], axis=1),)
    mixed = rowwise("mix_norms", k6, L, [full(attn), full(y), full(gl)],
                    [small["b_glu"], small["g_attn_out"], small["g_ssm_out"]], [(d_attn + d_ssm, bf16)])[0]
    big["w_o"] = get_w("w_o", mixed)
    mix = mm_nn("proj_out", mixed, big["w_o"])

    def k7(i, nt, xt, mt, gpm, gpf):
        h = xt + rms_fwd(mt, gpm)
        return h, rms_fwd(h, gpf)
    h, hn = rowwise("post_mix", k7, L, [full(x), full(mix)], [small["g_post_mix"], small["g_pre_ffn"]], [(D, f32), (D, bf16)])
    big["w_gate"] = get_w("w_gate", hn)
    big["w_up"] = get_w("w_up", hn)
    gt, up, hid = ffn_hidden(hn, big["w_gate"], big["w_up"])
    d_ff_dim = gt.shape[1]
    big["w_down"] = get_w("w_down", hid)
    ff = mm_nn("ffn_down", hid, big["w_down"], tk=d_ff_dim // 2)

    def k9(i, nt, ht, fft, tt, g):
        out = ht + rms_fwd(fft, g)
        err = out - tt
        per_row = jnp.mean(err * err, axis=-1, keepdims=True)
        loss = 0.5 * jnp.sum(per_row) * jnp.where(_lane((1, LANES)) == 0, 1.0, 0.0)
        d_out = err * (1.0 / D)
        d_ff, dg = rms_bwd(fft, g, d_out)
        return d_out, d_ff, dg, loss
    d_out, d_ff, dg_post_ffn, loss = rowwise("loss_head", k9, L, [full(h), full(ff), full(tgt)], [small["g_post_ffn"]],
                                             [(D, f32), (D, bf16)], reds=[D, LANES])

    d_gt, d_up = ffn_hidden_grad(d_ff, big["w_down"], gt, up)
    put_g("w_down", mm_tn("dw_down", hid, d_ff, out_dtype=gbf, tm=d_ff_dim // N_CHIPS))
    d_hn = mm_nt_pair("d_hn", d_gt, big["w_gate"], d_up, big["w_up"])
    put_g("w_gate", mm_tn("dw_gate", hn, d_gt, shards=N_CHIPS, out_dtype=gbf))
    token = put_g("w_up", mm_tn("dw_up", hn, d_up, shards=N_CHIPS, out_dtype=gbf))

    def k11(i, nt, ht, da, do, mt, gpf, gpm):
        dh_n, dg_pf = rms_bwd(ht, gpf, da)
        dh = do + dh_n
        d_mix, dg_pm = rms_bwd(mt, gpm, dh)
        return dh, d_mix, dg_pf, dg_pm
    dh, d_mix, dg_pre_ffn, dg_post_mix = rowwise("post_mix_grad", k11, L, [full(h), full(d_hn), full(d_out), full(mix)],
                                                 [behind(token, small["g_pre_ffn"]), small["g_post_mix"]], [(D, f32), (D, bf16)], reds=[D, D])
    d_mixed = mm_nt("d_mixed", d_mix, big["w_o"])
    put_g("w_o", mm_tn("dw_o", mixed, d_mix, out_dtype=gbf))

    def k12(i, nt, at, yt, glt, da_n, ds_n, bg, ga, gs):
        z = gelu(yt)
        sg = sigmoid(glt + bg)
        ssm = z * sg
        d_at, dga = rms_bwd(at, ga, da_n)
        d_ssm_t, dgs = rms_bwd(ssm, gs, ds_n)
        d_gl = d_ssm_t * z * sg * (1.0 - sg)
        return d_at, d_ssm_t * sg, d_gl, dga, dgs, colsum(d_gl)
    d_attn_o, dz1, d_gl, dg_attn, dg_ssm, db_glu = rowwise(
        "mix_norms_grad", k12, L, [full(attn), full(y), full(gl), (d_mixed, d_attn, 0, 0), (d_mixed, d_ssm, d_attn // d_ssm, 0)],
        [small["b_glu"], small["g_attn_out"], small["g_ssm_out"]], [(d_attn, f32), (d_ssm, f32), (d_ssm, bf16)],
        reds=[d_attn, d_ssm, d_ssm])
    dz2 = mm_nt("d_glu_in", d_gl, big["w_glu"])
    put_g("w_glu", mm_tn("dw_glu", z_bf, d_gl, out_dtype=gbf))

    def k13(i, nt, yt, a, b):
        return ((a + b) * gelu_grad(yt),)
    dy = rowwise("ssm_gelu_grad", k13, L, [full(y), full(dz1), full(dz2)], [], [(d_ssm, f32)])[0]
    du, dbd_re, dbd_im, dcd_re, dcd_im, dlam_re_l, dlam_im_l, dd_skip = ssm_bwd(
        dy, u_bf, proj, u_cb, s_re, s_im, bd_re, bd_im, cd_re, cd_im, lam_re_l, lam_im_l, small["d_skip"])
    dq, dkk_c, dkk_p, dvv_c, dvv_p, dsink = attn_bwd(qr, kk, vv, sink_b, attn, d_attn_o)
    d_proj = qkv_grad(dq, dkk_c, dkk_p, dvv_c, dvv_p, du, pos, inv_freq)
    d_xn = mm_nt("d_xn", d_proj, big["w_in"])
    token = put_g("w_in", mm_tn("dw_in", xn, d_proj, shards=N_CHIPS, out_dtype=gbf))

    def k17(i, nt, xt, dxn, dht, g):
        dx, dg = rms_bwd(xt, g, dxn)
        return dht + dx, dg
    grad_x, dg_pre_mix = rowwise("pre_mix_grad", k17, L, [full(x), full(d_xn), full(dh)], [behind(token, small["g_pre_mix"])],
                                 [(D, f32)], reds=[D])

    gather = spread.T
    dbbt_re = _block_diag_take(dbd_re, P, N).reshape(G * P, N)
    dbbt_im = _block_diag_take(dbd_im, P, N).reshape(G * P, N)
    d_a_re, d_a_im, d_logdt, dbt_re, dbt_im = ssm_params_grad(
        a_re, a_im, logdt_b, bt_re, bt_im, spread, gather,
        _state_layout_inv(dlam_re_l, G, N), _state_layout_inv(dlam_im_l, G, N), dbbt_re, dbbt_im)
    q_per_kv = d_attn // HEAD_DIM // N_KV_HEADS
    small_grads = {
        "g_pre_mix": dg_pre_mix, "sinks": dsink[:, :q_per_kv, 0].reshape(1, -1),
        "a_re": d_a_re, "a_im": d_a_im, "log_dt": d_logdt.reshape(1, G),
        "b_re": dbt_re.reshape(G, P, N).transpose(0, 2, 1), "b_im": dbt_im.reshape(G, P, N).transpose(0, 2, 1),
        "c_re": _block_diag_take(dcd_re, N, P).transpose(0, 2, 1), "c_im": _block_diag_take(dcd_im, N, P).transpose(0, 2, 1),
        "d_skip": dd_skip, "b_glu": db_glu, "g_attn_out": dg_attn, "g_ssm_out": dg_ssm,
        "g_post_mix": dg_post_mix, "g_pre_ffn": dg_pre_ffn, "g_post_ffn": dg_post_ffn,
    }
    return loss, grad_x, small_grads


WEIGHTS = ['g_pre_mix', 'w_in', 'sinks', 'a_re', 'a_im', 'log_dt', 'b_re', 'b_im', 'c_re', 'c_im', 'd_skip', 'w_glu', 'b_glu',
           'g_attn_out', 'g_ssm_out', 'w_o', 'g_post_mix', 'g_pre_ffn', 'w_gate', 'w_up', 'w_down', 'g_post_ffn']
BIG = ['w_in', 'w_glu', 'w_o', 'w_gate', 'w_up', 'w_down']
COL_SHARDED = ['w_in', 'w_gate', 'w_up']
SMALL = [n for n in WEIGHTS if n not in BIG]
GATHER_GROUPS = [["w_glu", "w_o"], ["w_gate", "w_up"], ["w_down"]]
REDUCE_GROUPS = [["w_down", "w_gate", "w_up"], ["w_o", "w_glu", "w_in"]]


PACK_ROWS = 256


def _pack(parts):
    flat = jnp.concatenate([p.reshape(-1) for p in parts])
    pad = (-flat.size) % (PACK_ROWS * LANES)
    return jnp.pad(flat, (0, pad)).reshape(-1, LANES)


def _unpack(packed, shapes):
    flat = packed.reshape(-1)
    out, off = [], 0
    for s in shapes:
        n = int(np.prod(s))
        out.append(flat[off:off + n].reshape(s))
        off += n
    return out


def kernel(x, positions, g_pre_mix, w_in, sinks, a_re, a_im, log_dt, b_re, b_im, c_re, c_im, d_skip, w_glu, b_glu, g_attn_out, g_ssm_out, w_o, g_post_mix, g_pre_ffn, w_gate, w_up, w_down, g_post_ffn, loss_target, m_g_pre_mix, m_w_in, m_sinks, m_a_re, m_a_im, m_log_dt, m_b_re, m_b_im, m_c_re, m_c_im, m_d_skip, m_w_glu, m_b_glu, m_g_attn_out, m_g_ssm_out, m_w_o, m_g_post_mix, m_g_pre_ffn, m_w_gate, m_w_up, m_w_down, m_g_post_ffn, v_g_pre_mix, v_w_in, v_sinks, v_a_re, v_a_im, v_log_dt, v_b_re, v_b_im, v_c_re, v_c_im, v_d_skip, v_w_glu, v_b_glu, v_g_attn_out, v_g_ssm_out, v_w_o, v_g_post_mix, v_g_pre_ffn, v_w_gate, v_w_up, v_w_down, v_g_post_ffn):
    args = dict(locals())
    w = {n: args[n] for n in WEIGHTS}
    m = {n: args["m_" + n] for n in WEIGHTS}
    v = {n: args["v_" + n] for n in WEIGHTS}
    L, D = x.shape[1], x.shape[2]

    ax, ay, ac = _place()
    mine_arr = (2 * ax + ay).astype(jnp.int32).reshape(1)
    c_arr = ac.astype(jnp.int32).reshape(1)

    bufs = {n: cast_into_slot("cast_" + n, w[n][0], mine_arr) for n in BIG}
    bufs["w_in"] = gather_weights([bufs["w_in"]])[0]
    later = [n for n in BIG if n != "w_in"]
    sems, started, token = gather_start([bufs[n] for n in later], [[later.index(n) for n in g] for g in GATHER_GROUPS], bufs["w_in"])
    bufs.update(zip(later, started))
    ready = {"w_in"}

    def get_w(n, after):
        if n not in ready:
            gi = [n in g for g in GATHER_GROUPS].index(True)
            members = GATHER_GROUPS[gi]
            landed = gather_wait("gather_wait_%d" % gi, [bufs[k] for k in members], *sems[gi], after)
            bufs.update(zip(members, gather_forward("gather_forward_%d" % gi, landed)))
            ready.update(members)
        g = bufs[n]
        return g if n in COL_SHARDED else g.reshape(g.shape[0] * g.shape[1], g.shape[2])

    pending, inflight = {}, []

    def put_g(n, g):
        pending[n] = g if n in COL_SHARDED else g.reshape(N_CHIPS, g.shape[0] // N_CHIPS, g.shape[1])
        for gi, members in enumerate(REDUCE_GROUPS):
            if n == members[-1]:
                g3 = [pending[k] for k in members]
                got = swap_halves("swap_halves_%d" % gi, g3)
                pair = [pair_sum("pair_sum_" + k, a, o, c_arr) for k, a, o in zip(members, g3, got)]
                send, recv, parts, lands, tok = scatter_start("scatter_start_%d" % gi, pair)
                inflight.append((members, send, recv, parts, lands))
                return tok[0, 0]
        return None

    small = {n: w[n].reshape(1, -1) for n in SMALL}
    small["g_pre_mix"] = small["g_pre_mix"] + token[0, 0]
    pos = positions.reshape(L, 1).astype(f32)
    loss, grad_x, small_grads = local_step(x[0], pos, loss_target[0], small, get_w, put_g)
    loss = lax.psum(jnp.sum(loss), ("x", "y", "c"))

    shapes = [w[n].shape for n in SMALL]
    small_sum = all_reduce_small(_pack([small_grads[n] for n in SMALL]))
    small_g = dict(zip(SMALL, _unpack(small_sum, shapes)))

    grads, delta, new_m, new_v = {}, {}, {}, {}
    after = small_sum
    for gi, (members, send, recv, parts, lands) in enumerate(inflight):
        parts, landed = scatter_wait("scatter_wait_%d" % gi, parts, lands, send, recv, after)
        halves = [chip_sum("chip_sum_" + k, p, t, mine_arr) for k, p, t in zip(members, parts, landed)]
        sib_halves = join_halves("join_halves_%d" % gi, halves)
        for n, own, sib in zip(members, halves, sib_halves):
            g_, d_, m_, v_ = adamw_halves("adamw_" + n, w[n][0], own, sib, m[n][0], v[n][0])
            grads[n], delta[n], new_m[n], new_v[n] = g_[None], d_[None], m_[None], v_[None]
            after = v_
    pw, pm, pv = (_pack([t[n] for n in SMALL]) for t in (w, m, v))
    d_, m_, v_ = adamw("adamw_small", pw, small_sum, pm, pv)
    for t, packed in ((delta, d_), (new_m, m_), (new_v, v_)):
        t.update(zip(SMALL, _unpack(packed, shapes)))
    grads.update(small_g)

    return (loss, grad_x[None], *[grads[n] for n in WEIGHTS], *[delta[n] for n in WEIGHTS],
            *[new_m[n] for n in WEIGHTS], *[new_v[n] for n in WEIGHTS])
```

```python
import functools
import math

import jax
import jax.numpy as jnp
import numpy as np
from jax import lax
from jax.experimental import pallas as pl
from jax.experimental.pallas import tpu as pltpu

f32 = jnp.float32
bf16 = jnp.bfloat16
HIGHEST = lax.Precision.HIGHEST
MESH = pl.DeviceIdType.MESH

HEAD_DIM = 64
N_KV_HEADS = 4
ATTN_BLOCK = 128
ROPE_THETA = 10000.0
SSM_GROUP = 16
SSM_STATE = 64
RMS_EPS = 1e-6
LANES = 128
SUBLANES = 8
VMEM_LIMIT = 52 * 1024 * 1024
N_CHIPS = 4
N_DEV = 8
NEG = -1e30

ADAM_LR, ADAM_B1, ADAM_B2, ADAM_EPS, ADAM_WD, ADAM_STEP = 0.001, 0.9, 0.999, 1e-08, 0.01, 10

NN = (((1,), (0,)), ((), ()))
NT = (((1,), (1,)), ((), ()))
TN = (((0,), (0,)), ((), ()))


def _params(*sem):
    return pltpu.CompilerParams(dimension_semantics=sem or None, vmem_limit_bytes=VMEM_LIMIT)


def _dot(a, b, dims=NN):
    return lax.dot_general(a, b, dims, preferred_element_type=f32)


def _pick(dim, pref):
    t = min(dim, pref)
    while dim % t:
        t -= LANES
    assert t > 0, (dim, pref)
    return t


ANY = pl.BlockSpec(memory_space=pl.ANY)


def _with_dep(in_specs, operands, dep):
    if dep is None:
        return list(in_specs), list(operands), 0
    return list(in_specs) + [ANY], list(operands) + [dep], 1


def _mm_call(name, grid, in_specs, out_spec, out_shape, acc_shape, dims, operands, dep=None):
    nk = grid[2]
    in_specs, operands, n_dep = _with_dep(in_specs, operands, dep)

    def body_one(a_ref, b_ref, *rest):
        o_ref = rest[n_dep]
        o_ref[...] = _dot(a_ref[...], b_ref[...], dims).astype(o_ref.dtype)

    def body(a_ref, b_ref, *rest):
        o_ref, acc_ref = rest[n_dep], rest[n_dep + 1]
        k = pl.program_id(2)

        @pl.when(k == 0)
        def _():
            acc_ref[...] = _dot(a_ref[...], b_ref[...], dims)

        @pl.when((k > 0) & (k < nk - 1))
        def _():
            acc_ref[...] += _dot(a_ref[...], b_ref[...], dims)

        @pl.when(k == nk - 1)
        def _():
            o_ref[...] = (acc_ref[...] + _dot(a_ref[...], b_ref[...], dims)).astype(o_ref.dtype)

    return pl.pallas_call(
        body_one if nk == 1 else body, out_shape=out_shape, grid=grid, in_specs=in_specs, out_specs=out_spec,
        scratch_shapes=[] if nk == 1 else [pltpu.VMEM(acc_shape, f32)], name=name,
        compiler_params=_params("parallel", "parallel", "arbitrary"))(*operands)


def mm_nt_pair(name, a1, b1, a2, b2, tm=1024, tk=1024, dep=None):
    M = a1.shape[0]
    S, K, n = b1.shape
    tm, tko = _pick(M, tm), _pick(K, tk)
    nk = 2 * S

    def body(a1_ref, b1_ref, a2_ref, b2_ref, *rest):
        o_ref, acc_ref = rest[-2], rest[-1]
        k = pl.program_id(2)

        @pl.when(k == 0)
        def _():
            acc_ref[...] = _dot(a1_ref[...], b1_ref[...], NT)

        @pl.when((k > 0) & (k < S))
        def _():
            acc_ref[...] += _dot(a1_ref[...], b1_ref[...], NT)

        @pl.when((k >= S) & (k < nk - 1))
        def _():
            acc_ref[...] += _dot(a2_ref[...], b2_ref[...], NT)

        @pl.when(k == nk - 1)
        def _():
            o_ref[...] = acc_ref[...] + _dot(a2_ref[...], b2_ref[...], NT)

    first = lambda k: jnp.minimum(k, S - 1)
    second = lambda k: jnp.maximum(k - S, 0)
    in_specs = [pl.BlockSpec((tm, n), lambda i, j, k: (i, first(k))), pl.BlockSpec((None, tko, n), lambda i, j, k: (first(k), j, 0)),
                pl.BlockSpec((tm, n), lambda i, j, k: (i, second(k))), pl.BlockSpec((None, tko, n), lambda i, j, k: (second(k), j, 0))]
    in_specs, operands, _ = _with_dep(in_specs, (a1, b1, a2, b2), dep)
    return pl.pallas_call(
        body, out_shape=jax.ShapeDtypeStruct((M, K), f32), grid=(M // tm, K // tko, nk), in_specs=in_specs,
        out_specs=pl.BlockSpec((tm, tko), lambda i, j, k: (i, j)), scratch_shapes=[pltpu.VMEM((tm, tko), f32)], name=name,
        compiler_params=_params("parallel", "parallel", "arbitrary"))(*operands)


def mm_nn(name, a, b, out_dtype=f32, tm=1024, tn=1024, tk=2048, dep=None):
    M, K = a.shape
    tm, tk = _pick(M, tm), _pick(K, tk)
    if b.ndim == 3:
        S, _, n = b.shape
        tn = _pick(n, 2048)
        per = n // tn
        b_spec = pl.BlockSpec((None, tk, tn), lambda i, j, k: (j // per, k, j % per))
        N = S * n
    else:
        N = b.shape[1]
        tn = _pick(N, tn)
        b_spec = pl.BlockSpec((tk, tn), lambda i, j, k: (k, j))
    grid = (M // tm, N // tn, K // tk)
    return _mm_call(name, grid, [pl.BlockSpec((tm, tk), lambda i, j, k: (i, k)), b_spec],
                    pl.BlockSpec((tm, tn), lambda i, j, k: (i, j)), jax.ShapeDtypeStruct((M, N), out_dtype),
                    (tm, tn), NN, (a, b), dep)


def mm_nt(name, a, b, out_dtype=f32, tm=1024, tn=2048, tk=1024, dep=None):
    M, N = a.shape
    tm = _pick(M, tm)
    if b.ndim == 3:
        S, K, n = b.shape
        tr = _pick(n, 2048)
        per = n // tr
        tko = _pick(K, tk)
        b_spec = pl.BlockSpec((None, tko, tr), lambda i, j, k: (k // per, j, k % per))
    else:
        K = b.shape[0]
        tr = _pick(N, tn)
        tko = _pick(K, tk)
        b_spec = pl.BlockSpec((tko, tr), lambda i, j, k: (j, k))
    grid = (M // tm, K // tko, N // tr)
    return _mm_call(name, grid, [pl.BlockSpec((tm, tr), lambda i, j, k: (i, k)), b_spec],
                    pl.BlockSpec((tm, tko), lambda i, j, k: (i, j)), jax.ShapeDtypeStruct((M, K), out_dtype),
                    (tm, tko), NT, (a, b), dep)


def mm_tn(name, a, b, shards=None, out_dtype=f32, tm=1024, tn=1024, tl=2048, dep=None):
    L, K = a.shape
    N = b.shape[1]
    tl, tko = _pick(L, tl), _pick(K, tm)
    if shards:
        n = N // shards
        tn = _pick(n, 2048)
        per = n // tn
        o_spec = pl.BlockSpec((None, tko, tn), lambda i, j, k: (j // per, i, j % per))
        o_shape = jax.ShapeDtypeStruct((shards, K, n), out_dtype)
    else:
        tn = _pick(N, tn)
        o_spec = pl.BlockSpec((tko, tn), lambda i, j, k: (i, j))
        o_shape = jax.ShapeDtypeStruct((K, N), out_dtype)
    grid = (K // tko, N // tn, L // tl)
    return _mm_call(name, grid, [pl.BlockSpec((tl, tko), lambda i, j, k: (k, i)),
                                 pl.BlockSpec((tl, tn), lambda i, j, k: (k, j))],
                    o_spec, o_shape, (tko, tn), TN, (a, b), dep)


def ffn_hidden(hn, w_gate, w_up, tm=512):
    M, K = hn.shape
    S, _, n = w_gate.shape
    tm = _pick(M, tm)

    def body(a_ref, g_ref, u_ref, gt_ref, up_ref, hid_ref):
        a = a_ref[...]
        g = _dot(a, g_ref[...])
        u = _dot(a, u_ref[...])
        gt_ref[...] = g.astype(bf16)
        up_ref[...] = u.astype(bf16)
        hid_ref[...] = (g * sigmoid(g) * u).astype(bf16)

    w_spec = pl.BlockSpec((None, K, n), lambda s, i: (s, 0, 0))
    o_spec = pl.BlockSpec((tm, n), lambda s, i: (i, s))
    o = jax.ShapeDtypeStruct((M, S * n), bf16)
    return pl.pallas_call(
        body, out_shape=[o, o, o], grid=(S, M // tm), in_specs=[pl.BlockSpec((tm, K), lambda s, i: (i, 0)), w_spec, w_spec],
        out_specs=[o_spec, o_spec, o_spec], name="ffn_hidden", compiler_params=_params("parallel", "parallel"))(hn, w_gate, w_up)


def ffn_hidden_grad(d_ff, w_down, gt, up, tm=512):
    M, D = d_ff.shape
    F = w_down.shape[0]
    n = _pick(F // N_CHIPS, 2048)
    tm = _pick(M, tm)

    def body(a_ref, b_ref, gt_ref, up_ref, dg_ref, du_ref):
        dh = _dot(a_ref[...], b_ref[...], NT)
        g = gt_ref[...].astype(f32)
        sg = sigmoid(g)
        dg_ref[...] = (dh * up_ref[...].astype(f32) * (sg * (1.0 + g * (1.0 - sg)))).astype(bf16)
        du_ref[...] = (dh * (g * sg)).astype(bf16)

    t_spec = pl.BlockSpec((tm, n), lambda j, i: (i, j))
    o = jax.ShapeDtypeStruct((M, F), bf16)
    return pl.pallas_call(
        body, out_shape=[o, o], grid=(F // n, M // tm),
        in_specs=[pl.BlockSpec((tm, D), lambda j, i: (i, 0)), pl.BlockSpec((n, D), lambda j, i: (j, 0)), t_spec, t_spec],
        out_specs=[t_spec, t_spec], name="ffn_hidden_grad", compiler_params=_params("parallel", "parallel"))(d_ff, w_down, gt, up)


def rowwise(name, fn, L, rows, bcast, outs, reds=(), tr=128, dep=None):
    tr = min(tr, L)
    nt = L // tr
    n_rows, n_b, n_o = len(rows), len(bcast), len(outs)
    n_dep = 0 if dep is None else 1

    def body(*refs):
        i = pl.program_id(0)
        ins = [r[...] for r in refs[:n_rows + n_b]]
        res = fn(i, nt, *ins)
        o_refs = refs[n_rows + n_b + n_dep:]
        for k in range(n_o):
            o_refs[k][...] = res[k].astype(o_refs[k].dtype)
        if reds:
            @pl.when(i == 0)
            def _():
                for k in range(len(reds)):
                    o_refs[n_o + k][...] = jnp.zeros_like(o_refs[n_o + k])
            for k in range(len(reds)):
                o_refs[n_o + k][...] += res[n_o + k]

    def row_spec(width, cb, shift):
        if shift:
            return pl.BlockSpec((tr, width), lambda i: (jnp.minimum(i + shift, nt - 1), cb))
        return pl.BlockSpec((tr, width), lambda i: (i, cb))

    in_specs = [row_spec(w, cb, sh) for (_, w, cb, sh) in rows]
    in_specs += [pl.BlockSpec(b.shape, lambda i: (0, 0)) for b in bcast]
    out_specs = [pl.BlockSpec((tr, w), lambda i: (i, 0)) for (w, _) in outs]
    out_specs += [pl.BlockSpec((1, w), lambda i: (0, 0)) for w in reds]
    out_shape = [jax.ShapeDtypeStruct((L, w), dt) for (w, dt) in outs]
    out_shape += [jax.ShapeDtypeStruct((1, w), f32) for w in reds]
    in_specs, operands, _ = _with_dep(in_specs, [r[0] for r in rows] + list(bcast), dep)
    return pl.pallas_call(
        body, out_shape=out_shape, grid=(nt,), in_specs=in_specs, out_specs=out_specs, name=name,
        compiler_params=_params("arbitrary"))(*operands)


def full(a):
    return (a, a.shape[1], 0, 0)


def colsum(v):
    return jnp.sum(v, axis=0, keepdims=True)


def rms_fwd(x, g):
    r = lax.rsqrt(jnp.mean(x * x, axis=-1, keepdims=True) + RMS_EPS)
    return x * r * g


def rms_bwd(x, g, dy):
    r = lax.rsqrt(jnp.mean(x * x, axis=-1, keepdims=True) + RMS_EPS)
    xh = x * r
    dyg = dy * g
    dx = r * (dyg - xh * jnp.mean(dyg * xh, axis=-1, keepdims=True))
    return dx, colsum(dy * xh)


GELU_C = math.sqrt(2.0 / math.pi)


def gelu(y):
    return y * (0.5 * (1.0 + jnp.tanh(GELU_C * (y + 0.044715 * (y * y * y)))))


def gelu_grad(y):
    t = jnp.tanh(GELU_C * (y + 0.044715 * (y * y * y)))
    return 0.5 * (1.0 + t) + 0.5 * y * (1.0 - t * t) * (GELU_C * (1.0 + 3 * 0.044715 * (y * y)))


def sigmoid(v):
    return 1.0 / (1.0 + jnp.exp(-v))


def _lane(shape):
    return lax.broadcasted_iota(jnp.int32, shape, 1)


def _rot_chunk(t, cos, sin_signed):
    first = (_lane(t.shape) % HEAD_DIM) < (HEAD_DIM // 2)
    partner = jnp.where(first, pltpu.roll(t, LANES - HEAD_DIM // 2, 1), pltpu.roll(t, HEAD_DIM // 2, 1))
    return t * cos + partner * sin_signed


def _cos_sin(pos, inv_freq, inverse):
    ang = pos * inv_freq
    cos, sin = jnp.cos(ang), jnp.sin(ang)
    first = (_lane(ang.shape) % HEAD_DIM) < (HEAD_DIM // 2)
    sign = jnp.where(first, -1.0, 1.0) * (-1.0 if inverse else 1.0)
    return cos, sin * sign


def _dup_head(chunk, odd):
    low = _lane(chunk.shape) < HEAD_DIM
    x = jnp.where(low != odd, chunk, 0.0)
    return x + pltpu.roll(x, HEAD_DIM, 1)


def _chunks(v):
    return [v[:, LANES * c:LANES * (c + 1)] for c in range(v.shape[1] // LANES)]


def qkv_prep(proj, pos, inv_freq, d_attn, d_kv):
    L = proj.shape[0]
    d_ssm = proj.shape[1] - d_attn - 2 * d_kv
    half = d_ssm // 2
    scale = 1.0 / math.sqrt(HEAD_DIM)

    def fn(i, nt, q, k, v, u0, u1, p, invf):
        cos, sin = _cos_sin(p, invf, False)
        qr = jnp.concatenate([_rot_chunk(c, cos, sin) for c in _chunks(q)], axis=1) * scale
        kr = [_rot_chunk(c, cos, sin) for c in _chunks(k)]
        kk = jnp.concatenate([_dup_head(c, odd) for c in kr for odd in (False, True)], axis=1)
        vv = jnp.concatenate([_dup_head(c, odd) for c in _chunks(v) for odd in (False, True)], axis=1)
        return qr, kk, vv, jnp.concatenate([u0, u1], axis=1)

    u_cb = (d_attn + 2 * d_kv) // half
    return rowwise("qkv_prep", fn, L,
                   [(proj, d_attn, 0, 0), (proj, d_kv, d_attn // d_kv, 0), (proj, d_kv, d_attn // d_kv + 1, 0),
                    (proj, half, u_cb, 0), (proj, half, u_cb + 1, 0), full(pos)],
                   [inv_freq], [(d_attn, bf16), (2 * d_kv, bf16), (2 * d_kv, bf16), (d_ssm, bf16)])


def qkv_grad(dq, dkk_c, dkk_p, dvv_c, dvv_p, du, pos, inv_freq):
    L, d_attn = dq.shape
    d_kv = dkk_c.shape[1] // 2
    scale = 1.0 / math.sqrt(HEAD_DIM)

    def fold(cur, prev, i, nt):
        t = cur + jnp.where(i < nt - 1, prev, 0.0)
        out = []
        for c in range(t.shape[1] // (2 * LANES)):
            even, odd = t[:, 2 * c * LANES:(2 * c + 1) * LANES], t[:, (2 * c + 1) * LANES:(2 * c + 2) * LANES]
            even, odd = even + pltpu.roll(even, HEAD_DIM, 1), odd + pltpu.roll(odd, HEAD_DIM, 1)
            out.append(jnp.where(_lane(even.shape) < HEAD_DIM, even, odd))
        return out

    def fn(i, nt, dq_t, kc, kp, vc, vp, du_t, p, invf):
        cos, sin = _cos_sin(p, invf, True)
        dq_o = jnp.concatenate([_rot_chunk(c, cos, sin) for c in _chunks(dq_t)], axis=1) * scale
        dk_o = jnp.concatenate([_rot_chunk(c, cos, sin) for c in fold(kc, kp, i, nt)], axis=1)
        dv_o = jnp.concatenate(fold(vc, vp, i, nt), axis=1)
        return (jnp.concatenate([dq_o, dk_o, dv_o, du_t], axis=1),)

    return rowwise("qkv_grad", fn, L,
                   [full(dq), full(dkk_c), (dkk_p, 2 * d_kv, 0, 1), full(dvv_c), (dvv_p, 2 * d_kv, 0, 1), full(du), full(pos)],
                   [inv_freq], [(d_attn + 2 * d_kv + du.shape[1], bf16)], tr=ATTN_BLOCK)[0]


def _attn_specs(L):
    nb = L // ATTN_BLOCK
    B = ATTN_BLOCK
    q_spec = pl.BlockSpec((B, 2 * LANES), lambda h, n: (n, h))
    cur = pl.BlockSpec((B, LANES), lambda h, n: (n, h))
    prev = pl.BlockSpec((B, LANES), lambda h, n: (jnp.maximum(n - 1, 0), h))
    return nb, q_spec, cur, prev


def _attn_probs(qm, kc, kp, sink, n):
    B = ATTN_BLOCK
    row = lax.broadcasted_iota(jnp.int32, (B, B), 0)
    col = lax.broadcasted_iota(jnp.int32, (B, B), 1)
    sc = jnp.where(row >= col, _dot(qm, kc, NT), NEG)
    sp = jnp.where((col > row) & (n > 0), _dot(qm, kp, NT), NEG)
    m = jnp.maximum(jnp.maximum(jnp.max(sc, axis=1, keepdims=True), jnp.max(sp, axis=1, keepdims=True)), sink)
    pc, pp, ps = jnp.exp(sc - m), jnp.exp(sp - m), jnp.exp(sink - m)
    inv = 1.0 / (jnp.sum(pc, axis=1, keepdims=True) + jnp.sum(pp, axis=1, keepdims=True) + ps)
    return pc, pp, inv, ps


def attn_fwd(qr, kk, vv, sink_b):
    L, d_attn = qr.shape
    nb, q_spec, cur, prev = _attn_specs(L)
    n_kv = kk.shape[1] // LANES
    q_per_kv = d_attn // HEAD_DIM // n_kv

    def body(q_ref, kc_ref, kp_ref, vc_ref, vp_ref, s_ref, o_ref):
        h, n = pl.program_id(0), pl.program_id(1)
        low = _lane((ATTN_BLOCK, LANES)) < HEAD_DIM
        kc, kp, vc, vp = kc_ref[...], kp_ref[...], vc_ref[...], vp_ref[...]
        for pr in range(q_per_kv // 2):
            q2 = q_ref[:, LANES * pr:LANES * (pr + 1)]
            o2 = jnp.zeros((ATTN_BLOCK, LANES), f32)
            for odd in (False, True):
                mine = low != odd
                qm = jnp.where(mine, q2, jnp.zeros_like(q2))
                sink = jnp.max(s_ref[pl.ds(h * q_per_kv + 2 * pr + int(odd), 1), :], axis=1, keepdims=True)
                pc, pp, inv, _ = _attn_probs(qm, kc, kp, sink, n)
                o = _dot(pc.astype(bf16), vc) + _dot(pp.astype(bf16), vp)
                o2 = o2 + jnp.where(mine, o * inv, 0.0)
            o_ref[:, LANES * pr:LANES * (pr + 1)] = o2

    return pl.pallas_call(
        body, out_shape=jax.ShapeDtypeStruct((L, d_attn), f32), grid=(n_kv, nb),
        in_specs=[q_spec, cur, prev, cur, prev, pl.BlockSpec(sink_b.shape, lambda h, n: (0, 0))],
        out_specs=q_spec, name="attn_fwd", compiler_params=_params("parallel", "arbitrary"))(qr, kk, kk, vv, vv, sink_b)


def attn_bwd(qr, kk, vv, sink_b, attn, d_attn_out):
    L, d_attn = qr.shape
    nb, q_spec, cur, prev = _attn_specs(L)
    n_kv = kk.shape[1] // LANES
    q_per_kv = d_attn // HEAD_DIM // n_kv

    def body(q_ref, kc_ref, kp_ref, vc_ref, vp_ref, s_ref, o_ref, do_ref, dq_ref, dkc_ref, dkp_ref, dvc_ref, dvp_ref, ds_ref):
        h, n = pl.program_id(0), pl.program_id(1)
        B = ATTN_BLOCK
        low = _lane((B, LANES)) < HEAD_DIM
        kc, kp, vc, vp = kc_ref[...], kp_ref[...], vc_ref[...], vp_ref[...]
        dkc = jnp.zeros((B, LANES), f32)
        dkp, dvc, dvp = dkc, dkc, dkc
        srow = lax.broadcasted_iota(jnp.int32, (SUBLANES, LANES), 0)
        dsink = jnp.zeros((SUBLANES, LANES), f32)
        for pr in range(q_per_kv // 2):
            q2 = q_ref[:, LANES * pr:LANES * (pr + 1)]
            do2 = do_ref[:, LANES * pr:LANES * (pr + 1)]
            prod = do2 * o_ref[:, LANES * pr:LANES * (pr + 1)]
            dq2 = jnp.zeros((B, LANES), f32)
            for odd in (False, True):
                mine = low != odd
                qm = jnp.where(mine, q2, jnp.zeros_like(q2))
                sink = jnp.max(s_ref[pl.ds(h * q_per_kv + 2 * pr + int(odd), 1), :], axis=1, keepdims=True)
                pc, pp, inv, ps = _attn_probs(qm, kc, kp, sink, n)
                pc, pp = pc * inv, pp * inv
                delta = jnp.sum(jnp.where(mine, prod, 0.0), axis=1, keepdims=True)
                dob = jnp.where(mine, do2, 0.0).astype(bf16)
                dsc = (pc * (_dot(dob, vc, NT) - delta)).astype(bf16)
                dsp = (pp * (_dot(dob, vp, NT) - delta)).astype(bf16)
                dq2 = dq2 + jnp.where(mine, _dot(dsc, kc) + _dot(dsp, kp), 0.0)
                dkc = dkc + _dot(dsc, qm, TN)
                dkp = dkp + _dot(dsp, qm, TN)
                dvc = dvc + _dot(pc.astype(bf16), dob, TN)
                dvp = dvp + _dot(pp.astype(bf16), dob, TN)
                dsink = dsink + jnp.where(srow == 2 * pr + int(odd), -jnp.sum(ps * inv * delta), 0.0)
            dq_ref[:, LANES * pr:LANES * (pr + 1)] = dq2
        dkc_ref[...] = dkc
        dkp_ref[...] = dkp
        dvc_ref[...] = dvc
        dvp_ref[...] = dvp

        @pl.when(n == 0)
        def _():
            ds_ref[...] = jnp.zeros_like(ds_ref)

        ds_ref[...] += dsink

    kv_shape = jax.ShapeDtypeStruct(kk.shape, f32)
    return pl.pallas_call(
        body,
        out_shape=[jax.ShapeDtypeStruct((L, d_attn), f32), kv_shape, kv_shape, kv_shape, kv_shape,
                   jax.ShapeDtypeStruct((n_kv, SUBLANES, LANES), f32)],
        grid=(n_kv, nb),
        in_specs=[q_spec, cur, prev, cur, prev, pl.BlockSpec(sink_b.shape, lambda h, n: (0, 0)), q_spec, q_spec],
        out_specs=[q_spec, cur, cur, cur, cur, pl.BlockSpec((None, SUBLANES, LANES), lambda h, n: (h, 0, 0))],
        name="attn_bwd", compiler_params=_params("parallel", "arbitrary"))(qr, kk, kk, vv, vv, sink_b, attn, d_attn_out)


SSM_T = 128
NQ = SUBLANES * SSM_STATE // LANES
NJ = SUBLANES


def _strided_put(ref, j, val):
    for q in range(NQ):
        ref.at[q][pl.ds(j, SSM_T, stride=NJ), :] = val[:, LANES * q:LANES * (q + 1)]


def _strided_get(ref, j):
    return jnp.concatenate([ref.at[q][pl.ds(j, SSM_T, stride=NJ), :] for q in range(NQ)], axis=1)


def _ssm_specs(L, rev):
    nt = L // SSM_T
    idx = (lambda i: nt - 1 - i) if rev else (lambda i: i)
    row = lambda w, cb=0: pl.BlockSpec((SSM_T, w), lambda i: (idx(i), cb))
    state = pl.BlockSpec((NQ, SSM_T * NJ, LANES), lambda i: (0, idx(i), 0))
    whole = lambda a: pl.BlockSpec(a.shape, lambda i: (0,) * a.ndim)
    return nt, row, state, whole


def ssm_fwd(u_bf, proj, u_cb, bd_re, bd_im, cd_re, cd_im, lam_re, lam_im, d_skip):
    L, d_ssm = u_bf.shape
    nt, row, state, whole = _ssm_specs(L, False)
    half = d_ssm // 2
    gw = d_ssm // NJ

    def body(u_ref, u0_ref, u1_ref, bdr, bdi, cdr, cdi, lr_ref, li_ref, d_ref, y_ref, sr_ref, si_ref, carry):
        i = pl.program_id(0)

        @pl.when(i == 0)
        def _():
            carry[...] = jnp.zeros_like(carry)

        for j in range(NJ):
            uj = u_ref[:, gw * j:gw * (j + 1)]
            _strided_put(sr_ref, j, _dot(uj, bdr[j]))
            _strided_put(si_ref, j, _dot(uj, bdi[j]))
        lr = [lr_ref[q] for q in range(NQ)]
        li = [li_ref[q] for q in range(NQ)]

        def step(t, s):
            sr, si = s
            rows = pl.ds(pl.multiple_of(t * NJ, NJ), NJ)
            nr = tuple(lr[q] * sr[q] - li[q] * si[q] + sr_ref[q, rows, :] for q in range(NQ))
            ni = tuple(lr[q] * si[q] + li[q] * sr[q] + si_ref[q, rows, :] for q in range(NQ))
            for q in range(NQ):
                sr_ref[q, rows, :] = nr[q]
                si_ref[q, rows, :] = ni[q]
            return nr, ni

        init = (tuple(carry[0, q] for q in range(NQ)), tuple(carry[1, q] for q in range(NQ)))
        sr, si = lax.fori_loop(0, SSM_T, step, init, unroll=8)
        for q in range(NQ):
            carry[0, q] = sr[q]
            carry[1, q] = si[q]
        uf = jnp.concatenate([u0_ref[...], u1_ref[...]], axis=1)
        for j in range(NJ):
            cols = slice(gw * j, gw * (j + 1))
            yj = _dot(_strided_get(sr_ref, j).astype(bf16), cdr[j]) - _dot(_strided_get(si_ref, j).astype(bf16), cdi[j])
            y_ref[:, cols] = yj + d_ref[:, cols] * uf[:, cols]

    s_shape = jax.ShapeDtypeStruct((NQ, L * NJ, LANES), f32)
    consts = (bd_re, bd_im, cd_re, cd_im, lam_re, lam_im, d_skip)
    return pl.pallas_call(
        body, out_shape=[jax.ShapeDtypeStruct((L, d_ssm), f32), s_shape, s_shape], grid=(nt,),
        in_specs=[row(d_ssm), row(half, u_cb), row(half, u_cb + 1)] + [whole(a) for a in consts],
        out_specs=[row(d_ssm), state, state],
        scratch_shapes=[pltpu.VMEM((2, NQ, NJ, LANES), f32)], name="ssm_fwd",
        compiler_params=_params("arbitrary"))(u_bf, proj, proj, *consts)


def ssm_bwd(dy, u_bf, proj, u_cb, s_re, s_im, bd_re, bd_im, cd_re, cd_im, lam_re, lam_im, d_skip):
    L, d_ssm = dy.shape
    nt, row, state, whole = _ssm_specs(L, True)
    half = d_ssm // 2
    gw = d_ssm // NJ

    def body(dy_ref, u_ref, u0_ref, u1_ref, sr_ref, si_ref, bdr, bdi, cdr, cdi, lr_ref, li_ref, d_ref,
             du_ref, dbdr, dbdi, dcdr, dcdi, dlr, dli, dd_ref, gr_ref, gi_ref, carry):
        i = pl.program_id(0)

        @pl.when(i == 0)
        def _():
            carry[...] = jnp.zeros_like(carry)
            for r in (dbdr, dbdi, dcdr, dcdi, dlr, dli, dd_ref):
                r[...] = jnp.zeros_like(r)

        dyf = dy_ref[...]
        dyb = dyf.astype(bf16)
        for j in range(NJ):
            dyj = dyb[:, gw * j:gw * (j + 1)]
            _strided_put(gr_ref, j, _dot(dyj, cdr[j], NT))
            _strided_put(gi_ref, j, -_dot(dyj, cdi[j], NT))
            dcdr[j] += _dot(_strided_get(sr_ref, j).astype(bf16), dyj, TN)
            dcdi[j] -= _dot(_strided_get(si_ref, j).astype(bf16), dyj, TN)
        lr = [lr_ref[q] for q in range(NQ)]
        li = [li_ref[q] for q in range(NQ)]

        def step(k, c):
            gr, gi, ar, ai = c
            rows = pl.ds(pl.multiple_of((SSM_T - 1 - k) * NJ, NJ), NJ)
            s_r = [sr_ref[q, rows, :] for q in range(NQ)]
            s_i = [si_ref[q, rows, :] for q in range(NQ)]
            ar = tuple(ar[q] + gr[q] * s_r[q] + gi[q] * s_i[q] for q in range(NQ))
            ai = tuple(ai[q] + gi[q] * s_r[q] - gr[q] * s_i[q] for q in range(NQ))
            nr = tuple(gr_ref[q, rows, :] + lr[q] * gr[q] + li[q] * gi[q] for q in range(NQ))
            ni = tuple(gi_ref[q, rows, :] + lr[q] * gi[q] - li[q] * gr[q] for q in range(NQ))
            for q in range(NQ):
                gr_ref[q, rows, :] = nr[q]
                gi_ref[q, rows, :] = ni[q]
            return nr, ni, ar, ai

        zero = tuple(jnp.zeros((NJ, LANES), f32) for _ in range(NQ))
        init = (tuple(carry[0, q] for q in range(NQ)), tuple(carry[1, q] for q in range(NQ)), zero, zero)
        gr, gi, ar, ai = lax.fori_loop(0, SSM_T, step, init, unroll=8)
        for q in range(NQ):
            carry[0, q] = gr[q]
            carry[1, q] = gi[q]
            dlr[q] += ar[q]
            dli[q] += ai[q]
        uf = jnp.concatenate([u0_ref[...], u1_ref[...]], axis=1)
        dd_ref[...] += colsum(dyf * uf)
        for j in range(NJ):
            cols = slice(gw * j, gw * (j + 1))
            gjr, gji = _strided_get(gr_ref, j).astype(bf16), _strided_get(gi_ref, j).astype(bf16)
            du_ref[:, cols] = _dot(gjr, bdr[j], NT) + _dot(gji, bdi[j], NT) + d_ref[:, cols] * dyf[:, cols]
            uj = u_ref[:, cols]
            dbdr[j] += _dot(uj, gjr, TN)
            dbdi[j] += _dot(uj, gji, TN)

    consts = (bd_re, bd_im, cd_re, cd_im, lam_re, lam_im, d_skip)
    acc = lambda a: jax.ShapeDtypeStruct(a.shape, f32)
    outs = [jax.ShapeDtypeStruct((L, d_ssm), f32), acc(bd_re), acc(bd_im), acc(cd_re), acc(cd_im), acc(lam_re), acc(lam_im), acc(d_skip)]
    return pl.pallas_call(
        body, out_shape=outs, grid=(nt,),
        in_specs=[row(d_ssm), row(d_ssm), row(half, u_cb), row(half, u_cb + 1), state, state] + [whole(a) for a in consts],
        out_specs=[row(d_ssm)] + [whole(a) for a in consts],
        scratch_shapes=[pltpu.VMEM((NQ, SSM_T * NJ, LANES), f32), pltpu.VMEM((NQ, SSM_T * NJ, LANES), f32),
                        pltpu.VMEM((2, NQ, NJ, LANES), f32)],
        name="ssm_bwd", compiler_params=_params("arbitrary"))(dy, u_bf, proj, proj, s_re, s_im, *consts)


def _cmul(ar, ai, br, bi):
    return ar * br - ai * bi, ar * bi + ai * br


def _disc(ar, ai, logdt):
    dt = jnp.exp(logdt)
    mag = jnp.exp(ar * dt)
    lr, li = mag * jnp.cos(ai * dt), mag * jnp.sin(ai * dt)
    den = ar * ar + ai * ai
    nr, ni = lr - 1.0, li
    fr, fi = (nr * ar + ni * ai) / den, (ni * ar - nr * ai) / den
    return dt, lr, li, den, fr, fi


def ssm_params(a_re, a_im, logdt_b, bt_re, bt_im, spread):
    def body(ar_ref, ai_ref, ld_ref, br_ref, bi_ref, sp_ref, lr_ref, li_ref, or_ref, oi_ref):
        _, lr, li, _, fr, fi = _disc(ar_ref[...], ai_ref[...], ld_ref[...])
        lr_ref[...] = lr
        li_ref[...] = li
        fre = jnp.dot(sp_ref[...], fr, precision=HIGHEST, preferred_element_type=f32)
        fie = jnp.dot(sp_ref[...], fi, precision=HIGHEST, preferred_element_type=f32)
        o_r, o_i = _cmul(fre, fie, br_ref[...], bi_ref[...])
        or_ref[...] = o_r
        oi_ref[...] = o_i

    g = jax.ShapeDtypeStruct(a_re.shape, f32)
    b = jax.ShapeDtypeStruct(bt_re.shape, f32)
    return pl.pallas_call(body, out_shape=[g, g, b, b], name="ssm_params",
                          compiler_params=_params())(a_re, a_im, logdt_b, bt_re, bt_im, spread)


def ssm_params_grad(a_re, a_im, logdt_b, bt_re, bt_im, spread, gather, dlam_re, dlam_im, dbt_re, dbt_im):
    def body(ar_ref, ai_ref, ld_ref, br_ref, bi_ref, sp_ref, ga_ref, glr_ref, gli_ref, gbr_ref, gbi_ref,
             dar_ref, dai_ref, dld_ref, dbr_ref, dbi_ref):
        ar, ai = ar_ref[...], ai_ref[...]
        dt, lr, li, den, fr, fi = _disc(ar, ai, ld_ref[...])
        hdot = functools.partial(jnp.dot, precision=HIGHEST, preferred_element_type=f32)
        fre, fie = hdot(sp_ref[...], fr), hdot(sp_ref[...], fi)
        gbr, gbi, br, bi = gbr_ref[...], gbi_ref[...], br_ref[...], bi_ref[...]
        dbr_ref[...], dbi_ref[...] = _cmul(fre, -fie, gbr, gbi)
        t_r, t_i = _cmul(br, -bi, gbr, gbi)
        gfr, gfi = hdot(ga_ref[...], t_r), hdot(ga_ref[...], t_i)
        iwr, iwi = ar / den, -ai / den
        x_r, x_i = _cmul(iwr, -iwi, gfr, gfi)
        glr, gli = glr_ref[...] + x_r, gli_ref[...] + x_i
        q_r, q_i = _cmul(fr, fi, iwr, iwi)
        gwr, gwi = _cmul(-q_r, q_i, gfr, gfi)
        y_r, y_i = _cmul(dt * lr, -dt * li, glr, gli)
        dar_ref[...] = gwr + y_r
        dai_ref[...] = gwi + y_i
        wl_r, wl_i = _cmul(ar, ai, lr, li)
        z_r, _ = _cmul(wl_r, -wl_i, glr, gli)
        dld_ref[...] = jnp.sum(z_r * dt, axis=1, keepdims=True)

    g = jax.ShapeDtypeStruct(a_re.shape, f32)
    b = jax.ShapeDtypeStruct(bt_re.shape, f32)
    return pl.pallas_call(body, out_shape=[g, g, jax.ShapeDtypeStruct((a_re.shape[0], 1), f32), b, b], name="ssm_params_grad",
                          compiler_params=_params())(a_re, a_im, logdt_b, bt_re, bt_im, spread, gather, dlam_re, dlam_im, dbt_re, dbt_im)


def _block_diag(t, rows, cols):
    G = t.shape[0]
    t = t.reshape(G // NJ, NJ, rows, cols)
    eye = jnp.eye(NJ, dtype=t.dtype)
    return jnp.einsum('jgrc,gh->jgrhc', t, eye).reshape(G // NJ, NJ * rows, NJ * cols)


def _block_diag_take(m, rows, cols):
    J = m.shape[0]
    m = m.reshape(J, NJ, rows, NJ, cols)
    idx = jnp.arange(NJ)
    return m[:, idx, :, idx, :].transpose(1, 0, 2, 3).reshape(J * NJ, rows, cols)


def _state_layout(t):
    return t.reshape(NJ, NQ, LANES).transpose(1, 0, 2)


def _state_layout_inv(t, G, N):
    return t.transpose(1, 0, 2).reshape(G, N)


def _tiles2d(shape, budget_rows=128):
    rows, cols = shape
    tr = rows
    if rows > budget_rows:
        tr = budget_rows
        while rows % tr:
            tr -= SUBLANES
    return tr, cols


def _adam_update(w, g, m, v):
    c1 = 1.0 - ADAM_B1 ** ADAM_STEP
    c2 = 1.0 - ADAM_B2 ** ADAM_STEP
    nm = ADAM_B1 * m + (1.0 - ADAM_B1) * g
    nv = ADAM_B2 * v + (1.0 - ADAM_B2) * (g * g)
    delta = -ADAM_LR * ((nm / c1) / (jnp.sqrt(nv / c2) + ADAM_EPS) + ADAM_WD * w)
    return delta, nm, nv


def adamw(name, w, g, m, v):
    tr, cols = _tiles2d(w.shape, 128 if w.shape[1] > 1024 else 256)

    def body(w_ref, g_ref, m_ref, v_ref, d_ref, nm_ref, nv_ref):
        d_ref[...], nm_ref[...], nv_ref[...] = _adam_update(w_ref[...], g_ref[...], m_ref[...], v_ref[...])

    spec = pl.BlockSpec((tr, cols), lambda i: (i, 0))
    o = jax.ShapeDtypeStruct(w.shape, f32)
    return pl.pallas_call(body, out_shape=[o, o, o], grid=(w.shape[0] // tr,), in_specs=[spec] * 4, out_specs=[spec] * 3,
                          name=name, compiler_params=_params("parallel"))(w, g, m, v)


def adamw_halves(name, w, own, got, m, v):
    h, cols = own.shape
    tr, _ = _tiles2d((h, cols), 128 if cols > 1024 else 256)
    nh = h // tr

    def body(w_ref, own_ref, got_ref, m_ref, v_ref, g_ref, d_ref, nm_ref, nv_ref):
        mine = (pl.program_id(0) // nh) == lax.axis_index("c")
        g = jnp.where(mine, own_ref[...], got_ref[...])
        g_ref[...] = g
        d_ref[...], nm_ref[...], nv_ref[...] = _adam_update(w_ref[...], g, m_ref[...], v_ref[...])

    spec = pl.BlockSpec((tr, cols), lambda i: (i, 0))
    half = pl.BlockSpec((tr, cols), lambda i: (i % nh, 0))
    o = jax.ShapeDtypeStruct(w.shape, f32)
    return pl.pallas_call(body, out_shape=[o, o, o, o], grid=(2 * nh,), in_specs=[spec, half, half, spec, spec], out_specs=[spec] * 4,
                          name=name, compiler_params=_params("parallel"))(w, own, got, m, v)


def pair_sum(name, g, got, c_arr):
    S, h, cols = got.shape
    tr, _ = _tiles2d((h, cols), 256)
    nh = h // tr

    def body(c_ref, g_ref, o_ref, out_ref):
        out_ref[...] = (g_ref[...].astype(f32) + o_ref[...].astype(f32)).astype(out_ref.dtype)

    spec = pl.BlockSpec((None, tr, cols), lambda s, i, c: (s, i, 0))
    grid_spec = pltpu.PrefetchScalarGridSpec(
        num_scalar_prefetch=1, grid=(S, nh),
        in_specs=[pl.BlockSpec((None, tr, cols), lambda s, i, c: (s, c[0] * nh + i, 0)), spec], out_specs=spec)
    return pl.pallas_call(body, out_shape=jax.ShapeDtypeStruct(got.shape, g.dtype), grid_spec=grid_spec, name=name,
                          compiler_params=_params("parallel", "parallel"))(c_arr, g, got)


def chip_sum(name, pair, landed, mine_arr, dep=None):
    n_in, h, cols = landed.shape
    tr, _ = _tiles2d((h, cols), 256)

    def body(s_ref, p_ref, l_ref, *rest):
        acc = p_ref[...].astype(f32)
        for k in range(n_in):
            acc = acc + l_ref[k].astype(f32)
        rest[-1][...] = acc

    in_specs, operands, _ = _with_dep(
        [pl.BlockSpec((None, tr, cols), lambda i, s: (s[0], i, 0)), pl.BlockSpec((n_in, tr, cols), lambda i, s: (0, i, 0))],
        [pair, landed], dep)
    grid_spec = pltpu.PrefetchScalarGridSpec(num_scalar_prefetch=1, grid=(h // tr,), in_specs=in_specs,
                                             out_specs=pl.BlockSpec((tr, cols), lambda i, s: (i, 0)))
    return pl.pallas_call(body, out_shape=jax.ShapeDtypeStruct((h, cols), f32), grid_spec=grid_spec, name=name,
                          compiler_params=_params("parallel"))(mine_arr, *operands)


def into_slot(name, w, slot_arr, n_slots, dtype, dep=None):
    tr, cols = _tiles2d(w.shape, 256)

    def body(s_ref, w_ref, *rest):
        rest[-1][...] = w_ref[...].astype(dtype)

    in_specs, operands, _ = _with_dep([pl.BlockSpec((tr, cols), lambda i, s: (i, 0))], [w], dep)
    grid_spec = pltpu.PrefetchScalarGridSpec(num_scalar_prefetch=1, grid=(w.shape[0] // tr,), in_specs=in_specs,
                                             out_specs=pl.BlockSpec((None, tr, cols), lambda i, s: (s[0], i, 0)))
    return pl.pallas_call(body, out_shape=jax.ShapeDtypeStruct((n_slots,) + w.shape, dtype), grid_spec=grid_spec, name=name,
                          compiler_params=_params("parallel"))(slot_arr, *operands)


def sum_slots(name, t):
    S, rows, cols = t.shape
    tr, _ = _tiles2d((rows, cols), 256)

    def body(t_ref, o_ref):
        acc = t_ref[0]
        for s in range(1, S):
            acc = acc + t_ref[s]
        o_ref[...] = acc

    return pl.pallas_call(body, out_shape=jax.ShapeDtypeStruct((rows, cols), f32), grid=(rows // tr,),
                          in_specs=[pl.BlockSpec((S, tr, cols), lambda i: (0, i, 0))], out_specs=pl.BlockSpec((tr, cols), lambda i: (i, 0)),
                          name=name, compiler_params=_params("parallel"))(t)


ANY = pl.BlockSpec(memory_space=pl.ANY)


def _place():
    x, y, c = lax.axis_index("x"), lax.axis_index("y"), lax.axis_index("c")
    return x, y, c


def _other_chips(x, y):
    return [(1 - x, y, 2 * (1 - x) + y), (x, 1 - y, 2 * x + 1 - y), (1 - x, 1 - y, 2 * (1 - x) + 1 - y)]


def gather_weights(bufs):
    nw = len(bufs)

    def body(*refs):
        dst = refs[nw:2 * nw]
        send1, recv1, send2, recv2 = refs[2 * nw:]
        x, y, c = _place()
        mine = 2 * x + y
        chips = _other_chips(x, y)
        sib = (x, y, 1 - c)
        first, passed = [], []
        for w in range(nw):
            h = dst[w].shape[1] // 2
            for k, (px, py, _) in enumerate(chips):
                half = dst[w].at[mine, pl.ds(c * h, h), :]
                cp = pltpu.make_async_remote_copy(src_ref=half, dst_ref=half,
                                                  send_sem=send1.at[w, k], recv_sem=recv1.at[w, k],
                                                  device_id=(px, py, c), device_id_type=MESH)
                cp.start()
                first.append(cp)
        for w in range(nw):
            h = dst[w].shape[1] // 2
            for k, (px, py, s) in enumerate(chips):
                landed = dst[w].at[s, pl.ds(c * h, h), :]
                pltpu.make_async_remote_copy(src_ref=landed, dst_ref=landed, send_sem=send1.at[w, k], recv_sem=recv1.at[w, k],
                                             device_id=(px, py, c), device_id_type=MESH).wait_recv()
                cp = pltpu.make_async_remote_copy(src_ref=landed, dst_ref=landed, send_sem=send2.at[w, k], recv_sem=recv2.at[w, k],
                                                  device_id=sib, device_id_type=MESH)
                cp.start()
                passed.append(cp)
        for w in range(nw):
            h = dst[w].shape[1] // 2
            for k, (px, py, s) in enumerate(chips):
                other = dst[w].at[s, pl.ds((1 - c) * h, h), :]
                pltpu.make_async_remote_copy(src_ref=other, dst_ref=other, send_sem=send2.at[w, k], recv_sem=recv2.at[w, k],
                                             device_id=sib, device_id_type=MESH).wait_recv()
        for cp in first + passed:
            cp.wait_send()

    sem = pltpu.SemaphoreType.DMA((nw, 3))
    return pl.pallas_call(
        body, out_shape=[jax.ShapeDtypeStruct(b.shape, b.dtype) for b in bufs],
        in_specs=[ANY] * nw, out_specs=[ANY] * nw, input_output_aliases={w: w for w in range(nw)},
        scratch_shapes=[sem, sem, sem, sem], name="gather_weights",
        compiler_params=pltpu.CompilerParams(has_side_effects=True))(*bufs)


def swap_halves(name, grads):
    nw = len(grads)

    def body(*refs):
        src, got = refs[:nw], refs[nw:2 * nw]
        send, recv = refs[2 * nw:]
        x, y, c = _place()
        cps = []
        for w in range(nw):
            h = src[w].shape[1] // 2
            cp = pltpu.make_async_remote_copy(src_ref=src[w].at[:, pl.ds((1 - c) * h, h), :], dst_ref=got[w],
                                              send_sem=send.at[w], recv_sem=recv.at[w],
                                              device_id=(x, y, 1 - c), device_id_type=MESH)
            cp.start()
            cps.append(cp)
        for cp in cps:
            cp.wait()

    half = [jax.ShapeDtypeStruct((g.shape[0], g.shape[1] // 2, g.shape[2]), g.dtype) for g in grads]
    sem = pltpu.SemaphoreType.DMA((nw,))
    return pl.pallas_call(
        body, out_shape=half, in_specs=[ANY] * nw, out_specs=[ANY] * nw,
        scratch_shapes=[sem, sem], name=name,
        compiler_params=pltpu.CompilerParams(has_side_effects=True))(*grads)


def scatter_to_owners(parts):
    nw = len(parts)

    def body(*refs):
        src, dst = refs[:nw], refs[nw:2 * nw]
        send, recv = refs[2 * nw:]
        x, y, c = _place()
        cps = []
        for w in range(nw):
            for k, (px, py, s) in enumerate(_other_chips(x, y)):
                cp = pltpu.make_async_remote_copy(src_ref=src[w].at[s], dst_ref=dst[w].at[k],
                                                  send_sem=send.at[w, k], recv_sem=recv.at[w, k],
                                                  device_id=(px, py, c), device_id_type=MESH)
                cp.start()
                cps.append(cp)
        for cp in cps:
            cp.wait()

    sem = pltpu.SemaphoreType.DMA((nw, 3))
    return pl.pallas_call(
        body, out_shape=[jax.ShapeDtypeStruct((N_CHIPS - 1,) + p.shape[1:], p.dtype) for p in parts],
        in_specs=[ANY] * nw, out_specs=[ANY] * nw,
        scratch_shapes=[sem, sem], name="scatter_to_owners",
        compiler_params=pltpu.CompilerParams(has_side_effects=True))(*parts)


def join_halves(name, halves):
    nw = len(halves)

    def body(*refs):
        src, dst = refs[:nw], refs[nw:2 * nw]
        send, recv = refs[2 * nw:]
        x, y, c = _place()
        cps = []
        for w in range(nw):
            cp = pltpu.make_async_remote_copy(src_ref=src[w], dst_ref=dst[w], send_sem=send.at[w], recv_sem=recv.at[w],
                                              device_id=(x, y, 1 - c), device_id_type=MESH)
            cp.start()
            cps.append(cp)
        for cp in cps:
            cp.wait()

    sem = pltpu.SemaphoreType.DMA((nw,))
    return pl.pallas_call(
        body, out_shape=[jax.ShapeDtypeStruct(p.shape, p.dtype) for p in halves], in_specs=[ANY] * nw, out_specs=[ANY] * nw,
        scratch_shapes=[sem, sem], name=name,
        compiler_params=pltpu.CompilerParams(has_side_effects=True))(*halves)


SEM = pl.BlockSpec(memory_space=pltpu.SEMAPHORE)
VM = pl.BlockSpec(memory_space=pltpu.VMEM)
DATAFLOW = pltpu.SideEffectType.DATAFLOW_SIDE_EFFECTING
TOKEN = jax.ShapeDtypeStruct((SUBLANES, LANES), f32)


def _gather_copy(buf, w, k, chip, c, mine, send, recv):
    px, py, _ = chip
    h = buf.shape[1] // 2
    half = buf.at[mine, pl.ds(c * h, h), :]
    return pltpu.make_async_remote_copy(src_ref=half, dst_ref=half, send_sem=send.at[3 * w + k], recv_sem=recv.at[3 * w + k],
                                        device_id=(px, py, c), device_id_type=MESH)


def _gather_landing(buf, w, k, chip, c, send, recv):
    px, py, s = chip
    h = buf.shape[1] // 2
    landed = buf.at[s, pl.ds(c * h, h), :]
    return pltpu.make_async_remote_copy(src_ref=landed, dst_ref=landed, send_sem=send.at[3 * w + k], recv_sem=recv.at[3 * w + k],
                                        device_id=(px, py, c), device_id_type=MESH)


def gather_start(name, bufs, groups, after):
    nw, ng = len(bufs), len(groups)

    def body(*refs):
        outs = refs[nw + 1:]
        sems, dst = outs[:2 * ng], outs[2 * ng:2 * ng + nw]
        token = outs[2 * ng + nw]
        x, y, c = _place()
        mine = 2 * x + y
        for g, members in enumerate(groups):
            for i, w in enumerate(members):
                for k, chip in enumerate(_other_chips(x, y)):
                    _gather_copy(dst[w], i, k, chip, c, mine, sems[2 * g], sems[2 * g + 1]).start()
        token[...] = jnp.zeros_like(token)

    sem_shapes = []
    for members in groups:
        sem_shapes += [pltpu.SemaphoreType.DMA((3 * len(members),))] * 2
    outs = pl.pallas_call(
        body, out_shape=sem_shapes + [jax.ShapeDtypeStruct(b.shape, b.dtype) for b in bufs] + [TOKEN],
        in_specs=[ANY] * (nw + 1), out_specs=[SEM] * (2 * ng) + [ANY] * nw + [VM],
        input_output_aliases={w: 2 * ng + w for w in range(nw)}, name=name,
        compiler_params=pltpu.CompilerParams(has_side_effects=DATAFLOW))(*bufs, after)
    return [(outs[2 * g], outs[2 * g + 1]) for g in range(ng)], list(outs[2 * ng:2 * ng + nw]), outs[2 * ng + nw]


def gather_wait(name, bufs, send, recv, after):
    nw = len(bufs)

    def body(*refs):
        src = refs[:nw]
        send_ref, recv_ref = refs[nw], refs[nw + 1]
        x, y, c = _place()
        mine = 2 * x + y
        for w in range(nw):
            for k, chip in enumerate(_other_chips(x, y)):
                _gather_copy(src[w], w, k, chip, c, mine, send_ref, recv_ref).wait_send()
                _gather_landing(src[w], w, k, chip, c, send_ref, recv_ref).wait_recv()

    return pl.pallas_call(
        body, out_shape=[jax.ShapeDtypeStruct(b.shape, b.dtype) for b in bufs],
        in_specs=[ANY] * nw + [SEM, SEM, ANY], out_specs=[ANY] * nw,
        input_output_aliases={w: w for w in range(nw)}, name=name,
        compiler_params=pltpu.CompilerParams(has_side_effects=DATAFLOW))(*bufs, send, recv, after)


def gather_forward(name, bufs):
    nw = len(bufs)

    def body(*refs):
        dst = refs[nw:2 * nw]
        send, recv = refs[2 * nw:]
        x, y, c = _place()
        sib = (x, y, 1 - c)
        cps = []
        for w in range(nw):
            h = dst[w].shape[1] // 2
            for k, (_, _, s) in enumerate(_other_chips(x, y)):
                landed = dst[w].at[s, pl.ds(c * h, h), :]
                cp = pltpu.make_async_remote_copy(src_ref=landed, dst_ref=landed, send_sem=send.at[w, k], recv_sem=recv.at[w, k],
                                                  device_id=sib, device_id_type=MESH)
                cp.start()
                cps.append(cp)
        for w in range(nw):
            h = dst[w].shape[1] // 2
            for k, (_, _, s) in enumerate(_other_chips(x, y)):
                other = dst[w].at[s, pl.ds((1 - c) * h, h), :]
                pltpu.make_async_remote_copy(src_ref=other, dst_ref=other, send_sem=send.at[w, k], recv_sem=recv.at[w, k],
                                             device_id=sib, device_id_type=MESH).wait_recv()
        for cp in cps:
            cp.wait_send()

    sem = pltpu.SemaphoreType.DMA((nw, 3))
    return pl.pallas_call(
        body, out_shape=[jax.ShapeDtypeStruct(b.shape, b.dtype) for b in bufs],
        in_specs=[ANY] * nw, out_specs=[ANY] * nw, input_output_aliases={w: w for w in range(nw)},
        scratch_shapes=[sem, sem], name=name, compiler_params=pltpu.CompilerParams(has_side_effects=True))(*bufs)


def _scatter_copy(src, dst, w, k, chip, c, send, recv):
    px, py, s = chip
    return pltpu.make_async_remote_copy(src_ref=src.at[s], dst_ref=dst.at[k], send_sem=send.at[3 * w + k], recv_sem=recv.at[3 * w + k],
                                        device_id=(px, py, c), device_id_type=MESH)


def scatter_start(name, parts):
    nw = len(parts)
    lands = [pltpu.with_memory_space_constraint(lax.empty((N_CHIPS - 1,) + p.shape[1:], p.dtype), pltpu.HBM) for p in parts]

    def body(*refs):
        outs = refs[2 * nw:]
        send, recv = outs[0], outs[1]
        src, dst, token = outs[2:2 + nw], outs[2 + nw:2 + 2 * nw], outs[2 + 2 * nw]
        x, y, c = _place()
        for w in range(nw):
            for k, chip in enumerate(_other_chips(x, y)):
                _scatter_copy(src[w], dst[w], w, k, chip, c, send, recv).start()
        token[...] = jnp.zeros_like(token)

    sem = pltpu.SemaphoreType.DMA((3 * nw,))
    outs = pl.pallas_call(
        body, out_shape=[sem, sem] + [jax.ShapeDtypeStruct(p.shape, p.dtype) for p in parts]
        + [jax.ShapeDtypeStruct(l.shape, l.dtype) for l in lands] + [TOKEN],
        in_specs=[ANY] * (2 * nw), out_specs=[SEM, SEM] + [ANY] * (2 * nw) + [VM],
        input_output_aliases={i: 2 + i for i in range(2 * nw)}, name=name,
        compiler_params=pltpu.CompilerParams(has_side_effects=DATAFLOW))(*parts, *lands)
    return outs[0], outs[1], list(outs[2:2 + nw]), list(outs[2 + nw:2 + 2 * nw]), outs[2 + 2 * nw]


def scatter_wait(name, parts, lands, send, recv, after):
    nw = len(parts)

    def body(*refs):
        src, dst = refs[:nw], refs[nw:2 * nw]
        send_ref, recv_ref = refs[2 * nw], refs[2 * nw + 1]
        x, y, c = _place()
        for w in range(nw):
            for k, chip in enumerate(_other_chips(x, y)):
                cp = _scatter_copy(src[w], dst[w], w, k, chip, c, send_ref, recv_ref)
                cp.wait_send()
                cp.wait_recv()

    outs = pl.pallas_call(
        body, out_shape=[jax.ShapeDtypeStruct(a.shape, a.dtype) for a in list(parts) + list(lands)],
        in_specs=[ANY] * (2 * nw) + [SEM, SEM, ANY], out_specs=[ANY] * (2 * nw),
        input_output_aliases={i: i for i in range(2 * nw)}, name=name,
        compiler_params=pltpu.CompilerParams(has_side_effects=DATAFLOW))(*parts, *lands, send, recv, after)
    return list(outs[:nw]), list(outs[nw:])


def _sibling_copy(src, dst, w, c, half_rows, send, recv, sib):
    if half_rows:
        h = src.shape[1] // 2
        src = src.at[:, pl.ds((1 - c) * h, h), :]
    return pltpu.make_async_remote_copy(src_ref=src, dst_ref=dst, send_sem=send.at[w], recv_sem=recv.at[w],
                                        device_id=sib, device_id_type=MESH)


def _landing(shape, dtype):
    return pltpu.with_memory_space_constraint(lax.empty(shape, dtype), pltpu.HBM)


def sibling_start(name, srcs, half_rows):
    nw = len(srcs)
    lands = [_landing((s.shape[0], s.shape[1] // 2, s.shape[2]) if half_rows else s.shape, s.dtype) for s in srcs]

    def body(*refs):
        outs = refs[2 * nw:]
        send, recv = outs[0], outs[1]
        src, dst, token = outs[2:2 + nw], outs[2 + nw:2 + 2 * nw], outs[2 + 2 * nw]
        x, y, c = _place()
        for w in range(nw):
            _sibling_copy(src[w], dst[w], w, c, half_rows, send, recv, (x, y, 1 - c)).start()
        token[...] = jnp.zeros_like(token)

    sem = pltpu.SemaphoreType.DMA((nw,))
    outs = pl.pallas_call(
        body, out_shape=[sem, sem] + [jax.ShapeDtypeStruct(a.shape, a.dtype) for a in list(srcs) + lands] + [TOKEN],
        in_specs=[ANY] * (2 * nw), out_specs=[SEM, SEM] + [ANY] * (2 * nw) + [VM],
        input_output_aliases={i: 2 + i for i in range(2 * nw)}, name=name,
        compiler_params=pltpu.CompilerParams(has_side_effects=DATAFLOW))(*srcs, *lands)
    return outs[0], outs[1], list(outs[2:2 + nw]), list(outs[2 + nw:2 + 2 * nw]), outs[2 + 2 * nw]


def sibling_wait(name, srcs, lands, send, recv, half_rows, after):
    nw = len(srcs)

    def body(*refs):
        src, dst = refs[:nw], refs[nw:2 * nw]
        send_ref, recv_ref = refs[2 * nw], refs[2 * nw + 1]
        x, y, c = _place()
        for w in range(nw):
            cp = _sibling_copy(src[w], dst[w], w, c, half_rows, send_ref, recv_ref, (x, y, 1 - c))
            cp.wait_send()
            cp.wait_recv()

    outs = pl.pallas_call(
        body, out_shape=[jax.ShapeDtypeStruct(a.shape, a.dtype) for a in list(srcs) + list(lands)],
        in_specs=[ANY] * (2 * nw) + [SEM, SEM, ANY], out_specs=[ANY] * (2 * nw),
        input_output_aliases={i: i for i in range(2 * nw)}, name=name,
        compiler_params=pltpu.CompilerParams(has_side_effects=DATAFLOW))(*srcs, *lands, send, recv, after)
    return list(outs[:nw]), list(outs[nw:])


def _peer(x, y, c, r):
    return (1 - x if r & 4 else x, 1 - y if r & 2 else y, 1 - c if r & 1 else c)


def _everyone_copy(buf, r, x, y, c, send, recv, landing):
    px, py, pc = _peer(x, y, c, r)
    slot = buf.at[4 * px + 2 * py + pc] if landing else buf.at[4 * x + 2 * y + c]
    return pltpu.make_async_remote_copy(src_ref=slot, dst_ref=slot, send_sem=send.at[r - 1], recv_sem=recv.at[r - 1],
                                        device_id=(px, py, pc), device_id_type=MESH)


def everyone_start(name, buf):
    def body(buf_in, send, recv, buf_ref, token):
        x, y, c = _place()
        for r in range(1, N_DEV):
            _everyone_copy(buf_ref, r, x, y, c, send, recv, False).start()
        token[...] = jnp.zeros_like(token)

    sem = pltpu.SemaphoreType.DMA((N_DEV - 1,))
    return pl.pallas_call(
        body, out_shape=[sem, sem, jax.ShapeDtypeStruct(buf.shape, buf.dtype), TOKEN],
        in_specs=[ANY], out_specs=[SEM, SEM, ANY, VM], input_output_aliases={0: 2}, name=name,
        compiler_params=pltpu.CompilerParams(has_side_effects=DATAFLOW))(buf)


def everyone_wait(name, buf, send, recv, after):
    def body(buf_ref, send_ref, recv_ref, after_ref, out_ref):
        x, y, c = _place()
        for r in range(1, N_DEV):
            _everyone_copy(buf_ref, r, x, y, c, send_ref, recv_ref, False).wait_send()
            _everyone_copy(buf_ref, r, x, y, c, send_ref, recv_ref, True).wait_recv()

    return pl.pallas_call(
        body, out_shape=jax.ShapeDtypeStruct(buf.shape, buf.dtype), in_specs=[ANY, SEM, SEM, ANY], out_specs=ANY,
        input_output_aliases={0: 0}, name=name,
        compiler_params=pltpu.CompilerParams(has_side_effects=DATAFLOW))(buf, send, recv, after)


def all_reduce_small(v):
    R, n = v.shape

    def body(v_ref, o_ref, all_ref, send_sems, recv_sems, local_sem):
        x, y, c = _place()
        me, sib = (x, y, c), (x, y, 1 - c)
        chips = [(1 - x, y), (x, 1 - y), (1 - x, 1 - y)]

        def rows(px, py, pc):
            return all_ref.at[pl.ds((4 * px + 2 * py + pc) * R, R), :]

        def copy(k, block, to, src=None):
            return pltpu.make_async_remote_copy(src_ref=rows(*block) if src is None else src, dst_ref=rows(*block),
                                                send_sem=send_sems.at[k], recv_sem=recv_sems.at[k],
                                                device_id=to, device_id_type=MESH)

        own = pltpu.make_async_copy(v_ref, rows(*me), local_sem)
        own.start()
        first = [copy(0, me, sib, src=v_ref)]
        first += [copy(1 + j, me, (*chip, c), src=v_ref) for j, chip in enumerate(chips)]
        for cp in first:
            cp.start()
        passed = [copy(4 + j, (*chip, c), sib) for j, chip in enumerate(chips)]
        for j, chip in enumerate(chips):
            copy(1 + j, (*chip, c), me).wait_recv()
            passed[j].start()
        copy(0, sib, me).wait_recv()
        for j, chip in enumerate(chips):
            copy(4 + j, (*chip, 1 - c), me).wait_recv()
        for cp in first + passed:
            cp.wait_send()
        own.wait()
        acc = all_ref[pl.ds(0, R), :]
        for d in range(1, N_DEV):
            acc = acc + all_ref[pl.ds(d * R, R), :]
        o_ref[...] = acc

    vm = pl.BlockSpec(memory_space=pltpu.VMEM)
    return pl.pallas_call(
        body, out_shape=jax.ShapeDtypeStruct((R, n), f32), in_specs=[vm], out_specs=vm,
        scratch_shapes=[pltpu.VMEM((N_DEV * R, n), f32), pltpu.SemaphoreType.DMA((7,)), pltpu.SemaphoreType.DMA((7,)),
                        pltpu.SemaphoreType.DMA],
        name="all_reduce_small", compiler_params=pltpu.CompilerParams(vmem_limit_bytes=VMEM_LIMIT, has_side_effects=True))(v)


def local_step(x, pos, tgt, small, d_in, get_w, put_g, first_dep=None):
    L, D = x.shape
    d_kv = N_KV_HEADS * HEAD_DIM
    d_ssm = small["d_skip"].shape[1]
    d_attn = d_in - 2 * d_kv - d_ssm
    big = {}
    G = d_ssm // SSM_GROUP
    N, P = SSM_STATE, SSM_GROUP
    gbf = bf16

    half_dim = HEAD_DIM // 2
    inv_freq = ROPE_THETA ** (-jnp.arange(half_dim, dtype=f32) / half_dim)
    inv_freq = jnp.tile(inv_freq, LANES // half_dim).reshape(1, LANES)
    sink_b = jnp.broadcast_to(small["sinks"].reshape(-1, 1), (small["sinks"].size, LANES))

    spread = jnp.repeat(jnp.eye(G, dtype=f32), P, axis=0)
    logdt_b = jnp.broadcast_to(small["log_dt"].reshape(G, 1), (G, N))
    bt_re = small["b_re"].reshape(G, N, P).transpose(0, 2, 1).reshape(G * P, N)
    bt_im = small["b_im"].reshape(G, N, P).transpose(0, 2, 1).reshape(G * P, N)
    a_re, a_im = small["a_re"].reshape(G, N), small["a_im"].reshape(G, N)
    lam_re, lam_im, bbt_re, bbt_im = ssm_params(a_re, a_im, logdt_b, bt_re, bt_im, spread)
    bd_re = _block_diag(bbt_re.reshape(G, P, N), P, N).astype(bf16)
    bd_im = _block_diag(bbt_im.reshape(G, P, N), P, N).astype(bf16)
    c_re = small["c_re"].reshape(G, P, N).transpose(0, 2, 1)
    c_im = small["c_im"].reshape(G, P, N).transpose(0, 2, 1)
    cd_re = _block_diag(c_re, N, P).astype(bf16)
    cd_im = _block_diag(c_im, N, P).astype(bf16)
    lam_re_l, lam_im_l = _state_layout(lam_re), _state_layout(lam_im)

    def k1(i, nt, xt, g):
        return (rms_fwd(xt, g),)
    xn = rowwise("pre_mix_norm", k1, L, [full(x)], [small["g_pre_mix"]], [(D, bf16)], dep=first_dep)[0]
    big["w_in"] = get_w("w_in", xn)
    proj = mm_nn("proj_in", xn, big["w_in"])
    qr, kk, vv, u_bf = qkv_prep(proj, pos, inv_freq, d_attn, d_kv)
    attn = attn_fwd(qr, kk, vv, sink_b)
    u_cb = (d_attn + 2 * d_kv) // (d_ssm // 2)
    y, s_re, s_im = ssm_fwd(u_bf, proj, u_cb, bd_re, bd_im, cd_re, cd_im, lam_re_l, lam_im_l, small["d_skip"])

    def k5(i, nt, yt):
        return (gelu(yt),)
    z_bf = rowwise("ssm_gelu", k5, L, [full(y)], [], [(d_ssm, bf16)])[0]
    big["w_glu"] = get_w("w_glu", z_bf)
    gl = mm_nn("glu_proj", z_bf, big["w_glu"])

    def k6(i, nt, at, yt, glt, bg, ga, gs):
        ssm = gelu(yt) * sigmoid(glt + bg)
        return (jnp.concatenate([rms_fwd(at, ga), rms_fwd(ssm, gs)], axis=1),)
    mixed = rowwise("mix_norms", k6, L, [full(attn), full(y), full(gl)],
                    [small["b_glu"], small["g_attn_out"], small["g_ssm_out"]], [(d_attn + d_ssm, bf16)])[0]
    big["w_o"] = get_w("w_o", mixed)
    mix = mm_nn("proj_out", mixed, big["w_o"])

    def k7(i, nt, xt, mt, gpm, gpf):
        h = xt + rms_fwd(mt, gpm)
        return h, rms_fwd(h, gpf)
    h, hn = rowwise("post_mix", k7, L, [full(x), full(mix)], [small["g_post_mix"], small["g_pre_ffn"]], [(D, f32), (D, bf16)])
    big["w_gate"] = get_w("w_gate", hn)
    big["w_up"] = get_w("w_up", hn)
    gt, up, hid = ffn_hidden(hn, big["w_gate"], big["w_up"])
    d_ff_dim = gt.shape[1]
    big["w_down"] = get_w("w_down", hid)
    ff = mm_nn("ffn_down", hid, big["w_down"], tk=d_ff_dim // 2)

    def k9(i, nt, ht, fft, tt, g):
        out = ht + rms_fwd(fft, g)
        err = out - tt
        per_row = jnp.mean(err * err, axis=-1, keepdims=True)
        loss = 0.5 * jnp.sum(per_row) * jnp.where(_lane((1, LANES)) == 0, 1.0, 0.0)
        d_out = err * (1.0 / D)
        d_ff, dg = rms_bwd(fft, g, d_out)
        return d_out, d_ff, dg, loss
    d_out, d_ff, dg_post_ffn, loss = rowwise("loss_head", k9, L, [full(h), full(ff), full(tgt)], [small["g_post_ffn"]],
                                             [(D, f32), (D, bf16)], reds=[D, LANES])

    d_gt, d_up = ffn_hidden_grad(d_ff, big["w_down"], gt, up)
    token = put_g("w_down", mm_tn("dw_down", hid, d_ff, out_dtype=gbf, tm=d_ff_dim // N_CHIPS))
    d_hn = mm_nt_pair("d_hn", d_gt, big["w_gate"], d_up, big["w_up"], dep=token)
    token = put_g("w_gate", mm_tn("dw_gate", hn, d_gt, shards=N_CHIPS, out_dtype=gbf))
    token = put_g("w_up", mm_tn("dw_up", hn, d_up, shards=N_CHIPS, out_dtype=gbf, dep=token))

    def k11(i, nt, ht, da, do, mt, gpf, gpm):
        dh_n, dg_pf = rms_bwd(ht, gpf, da)
        dh = do + dh_n
        d_mix, dg_pm = rms_bwd(mt, gpm, dh)
        return dh, d_mix, dg_pf, dg_pm
    dh, d_mix, dg_pre_ffn, dg_post_mix = rowwise("post_mix_grad", k11, L, [full(h), full(d_hn), full(d_out), full(mix)],
                                                 [small["g_pre_ffn"], small["g_post_mix"]], [(D, f32), (D, bf16)], reds=[D, D], dep=token)
    d_mixed = mm_nt("d_mixed", d_mix, big["w_o"])
    token = put_g("w_o", mm_tn("dw_o", mixed, d_mix, out_dtype=gbf))

    def k12(i, nt, at, yt, glt, da_n, ds_n, bg, ga, gs):
        z = gelu(yt)
        sg = sigmoid(glt + bg)
        ssm = z * sg
        d_at, dga = rms_bwd(at, ga, da_n)
        d_ssm_t, dgs = rms_bwd(ssm, gs, ds_n)
        d_gl = d_ssm_t * z * sg * (1.0 - sg)
        return d_at, d_ssm_t * sg, d_gl, dga, dgs, colsum(d_gl)
    d_attn_o, dz1, d_gl, dg_attn, dg_ssm, db_glu = rowwise(
        "mix_norms_grad", k12, L, [full(attn), full(y), full(gl), (d_mixed, d_attn, 0, 0), (d_mixed, d_ssm, d_attn // d_ssm, 0)],
        [small["b_glu"], small["g_attn_out"], small["g_ssm_out"]], [(d_attn, f32), (d_ssm, f32), (d_ssm, bf16)],
        reds=[d_attn, d_ssm, d_ssm], dep=token)
    dz2 = mm_nt("d_glu_in", d_gl, big["w_glu"])
    token = put_g("w_glu", mm_tn("dw_glu", z_bf, d_gl, out_dtype=gbf))

    def k13(i, nt, yt, a, b):
        return ((a + b) * gelu_grad(yt),)
    dy = rowwise("ssm_gelu_grad", k13, L, [full(y), full(dz1), full(dz2)], [], [(d_ssm, f32)], dep=token)[0]
    du, dbd_re, dbd_im, dcd_re, dcd_im, dlam_re_l, dlam_im_l, dd_skip = ssm_bwd(
        dy, u_bf, proj, u_cb, s_re, s_im, bd_re, bd_im, cd_re, cd_im, lam_re_l, lam_im_l, small["d_skip"])
    dq, dkk_c, dkk_p, dvv_c, dvv_p, dsink = attn_bwd(qr, kk, vv, sink_b, attn, d_attn_o)
    d_proj = qkv_grad(dq, dkk_c, dkk_p, dvv_c, dvv_p, du, pos, inv_freq)
    d_xn = mm_nt("d_xn", d_proj, big["w_in"])
    token = put_g("w_in", mm_tn("dw_in", xn, d_proj, shards=N_CHIPS, out_dtype=gbf))

    def k17(i, nt, xt, dxn, dht, g):
        dx, dg = rms_bwd(xt, g, dxn)
        return dht + dx, dg
    grad_x, dg_pre_mix = rowwise("pre_mix_grad", k17, L, [full(x), full(d_xn), full(dh)], [small["g_pre_mix"]],
                                 [(D, f32)], reds=[D], dep=token)

    gather = spread.T
    dbbt_re = _block_diag_take(dbd_re, P, N).reshape(G * P, N)
    dbbt_im = _block_diag_take(dbd_im, P, N).reshape(G * P, N)
    d_a_re, d_a_im, d_logdt, dbt_re, dbt_im = ssm_params_grad(
        a_re, a_im, logdt_b, bt_re, bt_im, spread, gather,
        _state_layout_inv(dlam_re_l, G, N), _state_layout_inv(dlam_im_l, G, N), dbbt_re, dbbt_im)
    q_per_kv = d_attn // HEAD_DIM // N_KV_HEADS
    small_grads = {
        "g_pre_mix": dg_pre_mix, "sinks": dsink[:, :q_per_kv, 0].reshape(1, -1),
        "a_re": d_a_re, "a_im": d_a_im, "log_dt": d_logdt.reshape(1, G),
        "b_re": dbt_re.reshape(G, P, N).transpose(0, 2, 1), "b_im": dbt_im.reshape(G, P, N).transpose(0, 2, 1),
        "c_re": _block_diag_take(dcd_re, N, P).transpose(0, 2, 1), "c_im": _block_diag_take(dcd_im, N, P).transpose(0, 2, 1),
        "d_skip": dd_skip, "b_glu": db_glu, "g_attn_out": dg_attn, "g_ssm_out": dg_ssm,
        "g_post_mix": dg_post_mix, "g_pre_ffn": dg_pre_ffn, "g_post_ffn": dg_post_ffn,
    }
    return loss, grad_x, small_grads


WEIGHTS = ['g_pre_mix', 'w_in', 'sinks', 'a_re', 'a_im', 'log_dt', 'b_re', 'b_im', 'c_re', 'c_im', 'd_skip', 'w_glu', 'b_glu',
           'g_attn_out', 'g_ssm_out', 'w_o', 'g_post_mix', 'g_pre_ffn', 'w_gate', 'w_up', 'w_down', 'g_post_ffn']
BIG = ['w_in', 'w_glu', 'w_o', 'w_gate', 'w_up', 'w_down']
COL_SHARDED = ['w_in', 'w_gate', 'w_up']
SMALL = [n for n in WEIGHTS if n not in BIG]
GATHER_GROUPS = [["w_in"], ["w_glu", "w_o"], ["w_gate", "w_up"], ["w_down"]]
REDUCE_GROUPS = [["w_down", "w_gate", "w_up"], ["w_o", "w_glu", "w_in"]]


PACK_ROWS = 256


def _pack(parts):
    flat = jnp.concatenate([p.reshape(-1) for p in parts])
    pad = (-flat.size) % (PACK_ROWS * LANES)
    return jnp.pad(flat, (0, pad)).reshape(-1, LANES)


def _unpack(packed, shapes):
    flat = packed.reshape(-1)
    out, off = [], 0
    for s in shapes:
        n = int(np.prod(s))
        out.append(flat[off:off + n].reshape(s))
        off += n
    return out


def kernel(x, positions, g_pre_mix, w_in, sinks, a_re, a_im, log_dt, b_re, b_im, c_re, c_im, d_skip, w_glu, b_glu, g_attn_out, g_ssm_out, w_o, g_post_mix, g_pre_ffn, w_gate, w_up, w_down, g_post_ffn, loss_target, m_g_pre_mix, m_w_in, m_sinks, m_a_re, m_a_im, m_log_dt, m_b_re, m_b_im, m_c_re, m_c_im, m_d_skip, m_w_glu, m_b_glu, m_g_attn_out, m_g_ssm_out, m_w_o, m_g_post_mix, m_g_pre_ffn, m_w_gate, m_w_up, m_w_down, m_g_post_ffn, v_g_pre_mix, v_w_in, v_sinks, v_a_re, v_a_im, v_log_dt, v_b_re, v_b_im, v_c_re, v_c_im, v_d_skip, v_w_glu, v_b_glu, v_g_attn_out, v_g_ssm_out, v_w_o, v_g_post_mix, v_g_pre_ffn, v_w_gate, v_w_up, v_w_down, v_g_post_ffn):
    args = dict(locals())
    w = {n: args[n] for n in WEIGHTS}
    m = {n: args["m_" + n] for n in WEIGHTS}
    v = {n: args["v_" + n] for n in WEIGHTS}
    L, D = x.shape[1], x.shape[2]

    ax, ay, ac = _place()
    mine_arr = (2 * ax + ay).astype(jnp.int32).reshape(1)
    c_arr = ac.astype(jnp.int32).reshape(1)

    me_arr = (4 * ax + 2 * ay + ac).astype(jnp.int32).reshape(1)

    bufs = {"w_in": into_slot("cast_w_in", w["w_in"][0], mine_arr, N_CHIPS, bf16)}
    sems, (bufs["w_in"],), token = gather_start("gather_start_in", [bufs["w_in"]], [[0]], mine_arr)
    later = [n for n in BIG if n != "w_in"]
    for n in later:
        bufs[n] = into_slot("cast_" + n, w[n][0], mine_arr, N_CHIPS, bf16, dep=token)
    more, started, token = gather_start("gather_start_rest", [bufs[n] for n in later],
                                        [[later.index(n) for n in g] for g in GATHER_GROUPS[1:]], token)
    sems += more
    bufs.update(zip(later, started))
    ready = set()

    def get_w(n, after):
        if n not in ready:
            gi = [n in g for g in GATHER_GROUPS].index(True)
            members = GATHER_GROUPS[gi]
            landed = gather_wait("gather_wait_%d" % gi, [bufs[k] for k in members], *sems[gi], after)
            bufs.update(zip(members, gather_forward("gather_forward_%d" % gi, landed)))
            ready.update(members)
        g = bufs[n]
        return g if n in COL_SHARDED else g.reshape(g.shape[0] * g.shape[1], g.shape[2])

    swaps, inflight = {}, []

    def put_g(n, g):
        g3 = g if n in COL_SHARDED else g.reshape(N_CHIPS, g.shape[0] // N_CHIPS, g.shape[1])
        swaps[n] = sibling_start("swap_start_" + n, [g3], True)
        for gi, members in enumerate(REDUCE_GROUPS):
            if n == members[-1]:
                last = swaps[n][4]
                pair = []
                for k in members:
                    send, recv, srcs, lands, _ = swaps[k]
                    (src,), (got,) = sibling_wait("swap_wait_" + k, srcs, lands, send, recv, True, last)
                    pair.append(pair_sum("pair_sum_" + k, src, got, c_arr))
                send, recv, parts, lands, tok = scatter_start("scatter_start_%d" % gi, pair)
                inflight.append((members, send, recv, parts, lands))
                return tok
        return swaps[n][4]

    small = {n: w[n].reshape(1, -1) for n in SMALL}
    pos = positions.reshape(L, 1).astype(f32)
    d_in = N_CHIPS * w["w_in"].shape[2]
    loss, grad_x, small_grads = local_step(x[0], pos, loss_target[0], small, d_in, get_w, put_g, first_dep=token)
    loss = lax.psum(jnp.sum(loss), ("x", "y", "c"))

    shapes = [w[n].shape for n in SMALL]
    blocks = into_slot("small_block", _pack([small_grads[n] for n in SMALL]), me_arr, N_DEV, f32)
    small_send, small_recv, blocks, after = everyone_start("small_start", blocks)

    grads, delta, new_m, new_v = {}, {}, {}, {}
    for gi, (members, send, recv, parts, lands) in enumerate(inflight):
        parts, landed = scatter_wait("scatter_wait_%d" % gi, parts, lands, send, recv, after)
        joins, dep = [], None
        for k, p, t in zip(members, parts, landed):
            joins.append(sibling_start("join_start_" + k, [chip_sum("chip_sum_" + k, p, t, mine_arr, dep=dep)], False))
            dep = after = joins[-1][4]
        for n, (send, recv, srcs, lands, _) in zip(members, joins):
            (own,), (sib,) = sibling_wait("join_wait_" + n, srcs, lands, send, recv, False, after)
            g_, d_, m_, v_ = adamw_halves("adamw_" + n, w[n][0], own, sib, m[n][0], v[n][0])
            grads[n], delta[n], new_m[n], new_v[n] = g_[None], d_[None], m_[None], v_[None]
            after = v_
    blocks = everyone_wait("small_wait", blocks, small_send, small_recv, after)
    small_sum = sum_slots("small_sum", blocks)
    small_g = dict(zip(SMALL, _unpack(small_sum, shapes)))
    pw, pm, pv = (_pack([t[n] for n in SMALL]) for t in (w, m, v))
    d_, m_, v_ = adamw("adamw_small", pw, small_sum, pm, pv)
    for t, packed in ((delta, d_), (new_m, m_), (new_v, v_)):
        t.update(zip(SMALL, _unpack(packed, shapes)))
    grads.update(small_g)

    return (loss, grad_x[None], *[grads[n] for n in WEIGHTS], *[delta[n] for n in WEIGHTS],
            *[new_m[n] for n in WEIGHTS], *[new_v[n] for n in WEIGHTS])
```

```python
import functools
import math

import jax
import jax.numpy as jnp
import numpy as np
from jax import lax
from jax.experimental import pallas as pl
from jax.experimental.pallas import tpu as pltpu

f32 = jnp.float32
bf16 = jnp.bfloat16
HIGHEST = lax.Precision.HIGHEST
MESH = pl.DeviceIdType.MESH

HEAD_DIM = 64
N_KV_HEADS = 4
ATTN_BLOCK = 128
ROPE_THETA = 10000.0
SSM_GROUP = 16
SSM_STATE = 64
RMS_EPS = 1e-6
LANES = 128
SUBLANES = 8
VMEM_LIMIT = 52 * 1024 * 1024
N_CHIPS = 4
N_DEV = 8
NEG = -1e30

ADAM_LR, ADAM_B1, ADAM_B2, ADAM_EPS, ADAM_WD, ADAM_STEP = 0.001, 0.9, 0.999, 1e-08, 0.01, 10

NN = (((1,), (0,)), ((), ()))
NT = (((1,), (1,)), ((), ()))
TN = (((0,), (0,)), ((), ()))


def _params(*sem):
    return pltpu.CompilerParams(dimension_semantics=sem or None, vmem_limit_bytes=VMEM_LIMIT)


def _dot(a, b, dims=NN):
    return lax.dot_general(a, b, dims, preferred_element_type=f32)


def _pick(dim, pref):
    t = min(dim, pref)
    while dim % t:
        t -= LANES
    assert t > 0, (dim, pref)
    return t


ANY = pl.BlockSpec(memory_space=pl.ANY)


def _with_dep(in_specs, operands, dep):
    if dep is None:
        return list(in_specs), list(operands), 0
    return list(in_specs) + [ANY], list(operands) + [dep], 1


def _mm_call(name, grid, in_specs, out_spec, out_shape, acc_shape, dims, operands, dep=None):
    nk = grid[2]
    in_specs, operands, n_dep = _with_dep(in_specs, operands, dep)

    def body_one(a_ref, b_ref, *rest):
        o_ref = rest[n_dep]
        o_ref[...] = _dot(a_ref[...], b_ref[...], dims).astype(o_ref.dtype)

    def body(a_ref, b_ref, *rest):
        o_ref, acc_ref = rest[n_dep], rest[n_dep + 1]
        k = pl.program_id(2)

        @pl.when(k == 0)
        def _():
            acc_ref[...] = _dot(a_ref[...], b_ref[...], dims)

        @pl.when((k > 0) & (k < nk - 1))
        def _():
            acc_ref[...] += _dot(a_ref[...], b_ref[...], dims)

        @pl.when(k == nk - 1)
        def _():
            o_ref[...] = (acc_ref[...] + _dot(a_ref[...], b_ref[...], dims)).astype(o_ref.dtype)

    return pl.pallas_call(
        body_one if nk == 1 else body, out_shape=out_shape, grid=grid, in_specs=in_specs, out_specs=out_spec,
        scratch_shapes=[] if nk == 1 else [pltpu.VMEM(acc_shape, f32)], name=name,
        compiler_params=_params("parallel", "parallel", "arbitrary"))(*operands)


def mm_nt_pair(name, a1, b1, a2, b2, tm=1024, tk=1024, dep=None):
    M = a1.shape[0]
    S, K, n = b1.shape
    tm, tko = _pick(M, tm), _pick(K, tk)
    nk = 2 * S

    def body(a1_ref, b1_ref, a2_ref, b2_ref, *rest):
        o_ref, acc_ref = rest[-2], rest[-1]
        k = pl.program_id(2)

        @pl.when(k == 0)
        def _():
            acc_ref[...] = _dot(a1_ref[...], b1_ref[...], NT)

        @pl.when((k > 0) & (k < S))
        def _():
            acc_ref[...] += _dot(a1_ref[...], b1_ref[...], NT)

        @pl.when((k >= S) & (k < nk - 1))
        def _():
            acc_ref[...] += _dot(a2_ref[...], b2_ref[...], NT)

        @pl.when(k == nk - 1)
        def _():
            o_ref[...] = acc_ref[...] + _dot(a2_ref[...], b2_ref[...], NT)

    first = lambda k: jnp.minimum(k, S - 1)
    second = lambda k: jnp.maximum(k - S, 0)
    in_specs = [pl.BlockSpec((tm, n), lambda i, j, k: (i, first(k))), pl.BlockSpec((None, tko, n), lambda i, j, k: (first(k), j, 0)),
                pl.BlockSpec((tm, n), lambda i, j, k: (i, second(k))), pl.BlockSpec((None, tko, n), lambda i, j, k: (second(k), j, 0))]
    in_specs, operands, _ = _with_dep(in_specs, (a1, b1, a2, b2), dep)
    return pl.pallas_call(
        body, out_shape=jax.ShapeDtypeStruct((M, K), f32), grid=(M // tm, K // tko, nk), in_specs=in_specs,
        out_specs=pl.BlockSpec((tm, tko), lambda i, j, k: (i, j)), scratch_shapes=[pltpu.VMEM((tm, tko), f32)], name=name,
        compiler_params=_params("parallel", "parallel", "arbitrary"))(*operands)


def mm_nn(name, a, b, out_dtype=f32, tm=1024, tn=1024, tk=2048, dep=None):
    M, K = a.shape
    tm, tk = _pick(M, tm), _pick(K, tk)
    if b.ndim == 3:
        S, _, n = b.shape
        tn = _pick(n, 2048)
        per = n // tn
        b_spec = pl.BlockSpec((None, tk, tn), lambda i, j, k: (j // per, k, j % per))
        N = S * n
    else:
        N = b.shape[1]
        tn = _pick(N, tn)
        b_spec = pl.BlockSpec((tk, tn), lambda i, j, k: (k, j))
    grid = (M // tm, N // tn, K // tk)
    return _mm_call(name, grid, [pl.BlockSpec((tm, tk), lambda i, j, k: (i, k)), b_spec],
                    pl.BlockSpec((tm, tn), lambda i, j, k: (i, j)), jax.ShapeDtypeStruct((M, N), out_dtype),
                    (tm, tn), NN, (a, b), dep)


def mm_nt(name, a, b, out_dtype=f32, tm=1024, tn=2048, tk=1024, dep=None):
    M, N = a.shape
    tm = _pick(M, tm)
    if b.ndim == 3:
        S, K, n = b.shape
        tr = _pick(n, 2048)
        per = n // tr
        tko = _pick(K, tk)
        b_spec = pl.BlockSpec((None, tko, tr), lambda i, j, k: (k // per, j, k % per))
    else:
        K = b.shape[0]
        tr = _pick(N, tn)
        tko = _pick(K, tk)
        b_spec = pl.BlockSpec((tko, tr), lambda i, j, k: (j, k))
    grid = (M // tm, K // tko, N // tr)
    return _mm_call(name, grid, [pl.BlockSpec((tm, tr), lambda i, j, k: (i, k)), b_spec],
                    pl.BlockSpec((tm, tko), lambda i, j, k: (i, j)), jax.ShapeDtypeStruct((M, K), out_dtype),
                    (tm, tko), NT, (a, b), dep)


def mm_tn(name, a, b, shards=None, out_dtype=f32, tm=1024, tn=1024, tl=2048, dep=None):
    L, K = a.shape
    N = b.shape[1]
    tl, tko = _pick(L, tl), _pick(K, tm)
    if shards:
        n = N // shards
        tn = _pick(n, 2048)
        per = n // tn
        o_spec = pl.BlockSpec((None, tko, tn), lambda i, j, k: (j // per, i, j % per))
        o_shape = jax.ShapeDtypeStruct((shards, K, n), out_dtype)
    else:
        tn = _pick(N, tn)
        o_spec = pl.BlockSpec((tko, tn), lambda i, j, k: (i, j))
        o_shape = jax.ShapeDtypeStruct((K, N), out_dtype)
    grid = (K // tko, N // tn, L // tl)
    return _mm_call(name, grid, [pl.BlockSpec((tl, tko), lambda i, j, k: (k, i)),
                                 pl.BlockSpec((tl, tn), lambda i, j, k: (k, j))],
                    o_spec, o_shape, (tko, tn), TN, (a, b), dep)


def ffn_hidden(hn, w_gate, w_up, tm=512):
    M, K = hn.shape
    S, _, n = w_gate.shape
    tm = _pick(M, tm)

    def body(a_ref, g_ref, u_ref, gt_ref, up_ref, hid_ref):
        a = a_ref[...]
        g = _dot(a, g_ref[...])
        u = _dot(a, u_ref[...])
        gt_ref[...] = g.astype(bf16)
        up_ref[...] = u.astype(bf16)
        hid_ref[...] = (g * sigmoid(g) * u).astype(bf16)

    w_spec = pl.BlockSpec((None, K, n), lambda s, i: (s, 0, 0))
    o_spec = pl.BlockSpec((tm, n), lambda s, i: (i, s))
    o = jax.ShapeDtypeStruct((M, S * n), bf16)
    return pl.pallas_call(
        body, out_shape=[o, o, o], grid=(S, M // tm), in_specs=[pl.BlockSpec((tm, K), lambda s, i: (i, 0)), w_spec, w_spec],
        out_specs=[o_spec, o_spec, o_spec], name="ffn_hidden", compiler_params=_params("parallel", "parallel"))(hn, w_gate, w_up)


def ffn_hidden_grad(d_ff, w_down, gt, up, tm=512):
    M, D = d_ff.shape
    F = w_down.shape[0]
    n = _pick(F // N_CHIPS, 2048)
    tm = _pick(M, tm)

    def body(a_ref, b_ref, gt_ref, up_ref, dg_ref, du_ref):
        dh = _dot(a_ref[...], b_ref[...], NT)
        g = gt_ref[...].astype(f32)
        sg = sigmoid(g)
        dg_ref[...] = (dh * up_ref[...].astype(f32) * (sg * (1.0 + g * (1.0 - sg)))).astype(bf16)
        du_ref[...] = (dh * (g * sg)).astype(bf16)

    t_spec = pl.BlockSpec((tm, n), lambda j, i: (i, j))
    o = jax.ShapeDtypeStruct((M, F), bf16)
    return pl.pallas_call(
        body, out_shape=[o, o], grid=(F // n, M // tm),
        in_specs=[pl.BlockSpec((tm, D), lambda j, i: (i, 0)), pl.BlockSpec((n, D), lambda j, i: (j, 0)), t_spec, t_spec],
        out_specs=[t_spec, t_spec], name="ffn_hidden_grad", compiler_params=_params("parallel", "parallel"))(d_ff, w_down, gt, up)


def rowwise(name, fn, L, rows, bcast, outs, reds=(), tr=256, dep=None):
    tr = min(tr, L)
    nt = L // tr
    n_rows, n_b, n_o = len(rows), len(bcast), len(outs)
    n_dep = 0 if dep is None else 1

    def body(*refs):
        i = pl.program_id(0)
        ins = [r[...] for r in refs[:n_rows + n_b]]
        res = fn(i, nt, *ins)
        o_refs = refs[n_rows + n_b + n_dep:]
        for k in range(n_o):
            o_refs[k][...] = res[k].astype(o_refs[k].dtype)
        if reds:
            @pl.when(i == 0)
            def _():
                for k in range(len(reds)):
                    o_refs[n_o + k][...] = jnp.zeros_like(o_refs[n_o + k])
            for k in range(len(reds)):
                o_refs[n_o + k][...] += res[n_o + k]

    def row_spec(width, cb, shift):
        if shift:
            return pl.BlockSpec((tr, width), lambda i: (jnp.minimum(i + shift, nt - 1), cb))
        return pl.BlockSpec((tr, width), lambda i: (i, cb))

    in_specs = [row_spec(w, cb, sh) for (_, w, cb, sh) in rows]
    in_specs += [pl.BlockSpec(b.shape, lambda i: (0, 0)) for b in bcast]
    out_specs = [pl.BlockSpec((tr, w), lambda i: (i, 0)) for (w, _) in outs]
    out_specs += [pl.BlockSpec((1, w), lambda i: (0, 0)) for w in reds]
    out_shape = [jax.ShapeDtypeStruct((L, w), dt) for (w, dt) in outs]
    out_shape += [jax.ShapeDtypeStruct((1, w), f32) for w in reds]
    in_specs, operands, _ = _with_dep(in_specs, [r[0] for r in rows] + list(bcast), dep)
    return pl.pallas_call(
        body, out_shape=out_shape, grid=(nt,), in_specs=in_specs, out_specs=out_specs, name=name,
        compiler_params=_params("arbitrary"))(*operands)


def full(a):
    return (a, a.shape[1], 0, 0)


def colsum(v):
    return jnp.sum(v, axis=0, keepdims=True)


def rms_fwd(x, g):
    r = lax.rsqrt(jnp.mean(x * x, axis=-1, keepdims=True) + RMS_EPS)
    return x * r * g


def rms_bwd(x, g, dy):
    r = lax.rsqrt(jnp.mean(x * x, axis=-1, keepdims=True) + RMS_EPS)
    xh = x * r
    dyg = dy * g
    dx = r * (dyg - xh * jnp.mean(dyg * xh, axis=-1, keepdims=True))
    return dx, colsum(dy * xh)


GELU_C = math.sqrt(2.0 / math.pi)


def gelu(y):
    return y * (0.5 * (1.0 + jnp.tanh(GELU_C * (y + 0.044715 * (y * y * y)))))


def gelu_grad(y):
    t = jnp.tanh(GELU_C * (y + 0.044715 * (y * y * y)))
    return 0.5 * (1.0 + t) + 0.5 * y * (1.0 - t * t) * (GELU_C * (1.0 + 3 * 0.044715 * (y * y)))


def sigmoid(v):
    return 1.0 / (1.0 + jnp.exp(-v))


def _lane(shape):
    return lax.broadcasted_iota(jnp.int32, shape, 1)


def _rot_chunk(t, cos, sin_signed):
    first = (_lane(t.shape) % HEAD_DIM) < (HEAD_DIM // 2)
    partner = jnp.where(first, pltpu.roll(t, LANES - HEAD_DIM // 2, 1), pltpu.roll(t, HEAD_DIM // 2, 1))
    return t * cos + partner * sin_signed


def _cos_sin(pos, inv_freq, inverse):
    ang = pos * inv_freq
    cos, sin = jnp.cos(ang), jnp.sin(ang)
    first = (_lane(ang.shape) % HEAD_DIM) < (HEAD_DIM // 2)
    sign = jnp.where(first, -1.0, 1.0) * (-1.0 if inverse else 1.0)
    return cos, sin * sign


def _dup_head(chunk, odd):
    low = _lane(chunk.shape) < HEAD_DIM
    x = jnp.where(low != odd, chunk, 0.0)
    return x + pltpu.roll(x, HEAD_DIM, 1)


def _chunks(v):
    return [v[:, LANES * c:LANES * (c + 1)] for c in range(v.shape[1] // LANES)]


def qkv_prep(proj, pos, inv_freq, d_attn, d_kv):
    L = proj.shape[0]
    d_ssm = proj.shape[1] - d_attn - 2 * d_kv
    half = d_ssm // 2
    scale = 1.0 / math.sqrt(HEAD_DIM)

    def fn(i, nt, q, k, v, u0, u1, p, invf):
        cos, sin = _cos_sin(p, invf, False)
        qr = jnp.concatenate([_rot_chunk(c, cos, sin) for c in _chunks(q)], axis=1) * scale
        kr = [_rot_chunk(c, cos, sin) for c in _chunks(k)]
        kk = jnp.concatenate([_dup_head(c, odd) for c in kr for odd in (False, True)], axis=1)
        vv = jnp.concatenate([_dup_head(c, odd) for c in _chunks(v) for odd in (False, True)], axis=1)
        return qr, kk, vv, jnp.concatenate([u0, u1], axis=1)

    u_cb = (d_attn + 2 * d_kv) // half
    return rowwise("qkv_prep", fn, L,
                   [(proj, d_attn, 0, 0), (proj, d_kv, d_attn // d_kv, 0), (proj, d_kv, d_attn // d_kv + 1, 0),
                    (proj, half, u_cb, 0), (proj, half, u_cb + 1, 0), full(pos)],
                   [inv_freq], [(d_attn, bf16), (2 * d_kv, bf16), (2 * d_kv, bf16), (d_ssm, bf16)])


def qkv_grad(dq, dkk_c, dkk_p, dvv_c, dvv_p, du, pos, inv_freq):
    L, d_attn = dq.shape
    d_kv = dkk_c.shape[1] // 2
    scale = 1.0 / math.sqrt(HEAD_DIM)

    def fold(cur, prev, i, nt):
        t = cur + jnp.where(i < nt - 1, prev, 0.0)
        out = []
        for c in range(t.shape[1] // (2 * LANES)):
            even, odd = t[:, 2 * c * LANES:(2 * c + 1) * LANES], t[:, (2 * c + 1) * LANES:(2 * c + 2) * LANES]
            even, odd = even + pltpu.roll(even, HEAD_DIM, 1), odd + pltpu.roll(odd, HEAD_DIM, 1)
            out.append(jnp.where(_lane(even.shape) < HEAD_DIM, even, odd))
        return out

    def fn(i, nt, dq_t, kc, kp, vc, vp, du_t, p, invf):
        cos, sin = _cos_sin(p, invf, True)
        dq_o = jnp.concatenate([_rot_chunk(c, cos, sin) for c in _chunks(dq_t)], axis=1) * scale
        dk_o = jnp.concatenate([_rot_chunk(c, cos, sin) for c in fold(kc, kp, i, nt)], axis=1)
        dv_o = jnp.concatenate(fold(vc, vp, i, nt), axis=1)
        return (jnp.concatenate([dq_o, dk_o, dv_o, du_t], axis=1),)

    return rowwise("qkv_grad", fn, L,
                   [full(dq), full(dkk_c), (dkk_p, 2 * d_kv, 0, 1), full(dvv_c), (dvv_p, 2 * d_kv, 0, 1), full(du), full(pos)],
                   [inv_freq], [(d_attn + 2 * d_kv + du.shape[1], bf16)], tr=ATTN_BLOCK)[0]


def _attn_specs(L):
    nb = L // ATTN_BLOCK
    B = ATTN_BLOCK
    q_spec = pl.BlockSpec((B, 2 * LANES), lambda h, n: (n, h))
    cur = pl.BlockSpec((B, LANES), lambda h, n: (n, h))
    prev = pl.BlockSpec((B, LANES), lambda h, n: (jnp.maximum(n - 1, 0), h))
    return nb, q_spec, cur, prev


def _attn_mask(n):
    B = ATTN_BLOCK
    row = lax.broadcasted_iota(jnp.int32, (B, 2 * B), 0)
    col = lax.broadcasted_iota(jnp.int32, (B, 2 * B), 1)
    return ((col < B) & (col > row) & (n > 0)) | ((col >= B) & (row >= col - B))


def _attn_probs(qm, kcat, sink, mask):
    s = jnp.where(mask, _dot(qm, kcat, NT), NEG)
    m = jnp.maximum(jnp.max(s, axis=1, keepdims=True), sink)
    p, ps = jnp.exp(s - m), jnp.exp(sink - m)
    inv = 1.0 / (jnp.sum(p, axis=1, keepdims=True) + ps)
    return p, inv, ps


def _attn_heads(q_ref, s_ref, h, q_per_kv):
    low = _lane((ATTN_BLOCK, LANES)) < HEAD_DIM
    heads = []
    for pr in range(q_per_kv // 2):
        q2 = q_ref[:, LANES * pr:LANES * (pr + 1)]
        for odd in (False, True):
            mine = low != odd
            sink = jnp.max(s_ref[pl.ds(h * q_per_kv + 2 * pr + int(odd), 1), :], axis=1, keepdims=True)
            heads.append((pr, mine, jnp.where(mine, q2, jnp.zeros_like(q2)), sink))
    return low, heads


def attn_fwd(qr, kk, vv, sink_b):
    L, d_attn = qr.shape
    nb, q_spec, cur, prev = _attn_specs(L)
    n_kv = kk.shape[1] // LANES
    q_per_kv = d_attn // HEAD_DIM // n_kv

    def body(q_ref, kc_ref, kp_ref, vc_ref, vp_ref, s_ref, o_ref):
        h, n = pl.program_id(0), pl.program_id(1)
        kcat = jnp.concatenate([kp_ref[...], kc_ref[...]], axis=0)
        vcat = jnp.concatenate([vp_ref[...], vc_ref[...]], axis=0)
        mask = _attn_mask(n)
        low, heads = _attn_heads(q_ref, s_ref, h, q_per_kv)
        probs = [_attn_probs(qm, kcat, sink, mask) for (_, _, qm, sink) in heads]
        outs = [_dot(p.astype(bf16), vcat) * inv for (p, inv, _) in probs]
        for pr in range(q_per_kv // 2):
            o_ref[:, LANES * pr:LANES * (pr + 1)] = jnp.where(low, outs[2 * pr], outs[2 * pr + 1])

    return pl.pallas_call(
        body, out_shape=jax.ShapeDtypeStruct((L, d_attn), f32), grid=(n_kv, nb),
        in_specs=[q_spec, cur, prev, cur, prev, pl.BlockSpec(sink_b.shape, lambda h, n: (0, 0))],
        out_specs=q_spec, name="attn_fwd", compiler_params=_params("parallel", "arbitrary"))(qr, kk, kk, vv, vv, sink_b)


def attn_bwd(qr, kk, vv, sink_b, attn, d_attn_out):
    L, d_attn = qr.shape
    nb, q_spec, cur, prev = _attn_specs(L)
    n_kv = kk.shape[1] // LANES
    q_per_kv = d_attn // HEAD_DIM // n_kv

    def body(q_ref, kc_ref, kp_ref, vc_ref, vp_ref, s_ref, o_ref, do_ref, dq_ref, dkc_ref, dkp_ref, dvc_ref, dvp_ref, ds_ref):
        h, n = pl.program_id(0), pl.program_id(1)
        B = ATTN_BLOCK
        kcat = jnp.concatenate([kp_ref[...], kc_ref[...]], axis=0)
        vcat = jnp.concatenate([vp_ref[...], vc_ref[...]], axis=0)
        mask = _attn_mask(n)
        low, heads = _attn_heads(q_ref, s_ref, h, q_per_kv)
        probs = [_attn_probs(qm, kcat, sink, mask) for (_, _, qm, sink) in heads]
        dk = jnp.zeros((2 * B, LANES), f32)
        dv = dk
        srow = lax.broadcasted_iota(jnp.int32, (SUBLANES, LANES), 0)
        dsink = jnp.zeros((SUBLANES, LANES), f32)
        dqs = []
        for i, ((pr, mine, qm, _), (p, inv, ps)) in enumerate(zip(heads, probs)):
            do2 = do_ref[:, LANES * pr:LANES * (pr + 1)]
            delta = jnp.sum(jnp.where(mine, do2 * o_ref[:, LANES * pr:LANES * (pr + 1)], 0.0), axis=1, keepdims=True)
            dob = jnp.where(mine, do2, 0.0).astype(bf16)
            p = p * inv
            ds = (p * (_dot(dob, vcat, NT) - delta)).astype(bf16)
            dqs.append(_dot(ds, kcat))
            dk = dk + _dot(ds, qm, TN)
            dv = dv + _dot(p.astype(bf16), dob, TN)
            dsink = dsink + jnp.where(srow == i, -jnp.sum(ps * inv * delta), 0.0)
        for pr in range(q_per_kv // 2):
            dq_ref[:, LANES * pr:LANES * (pr + 1)] = jnp.where(low, dqs[2 * pr], dqs[2 * pr + 1])
        dkp_ref[...] = dk[:B]
        dkc_ref[...] = dk[B:]
        dvp_ref[...] = dv[:B]
        dvc_ref[...] = dv[B:]

        @pl.when(n == 0)
        def _():
            ds_ref[...] = jnp.zeros_like(ds_ref)

        ds_ref[...] += dsink

    kv_shape = jax.ShapeDtypeStruct(kk.shape, f32)
    return pl.pallas_call(
        body,
        out_shape=[jax.ShapeDtypeStruct((L, d_attn), f32), kv_shape, kv_shape, kv_shape, kv_shape,
                   jax.ShapeDtypeStruct((n_kv, SUBLANES, LANES), f32)],
        grid=(n_kv, nb),
        in_specs=[q_spec, cur, prev, cur, prev, pl.BlockSpec(sink_b.shape, lambda h, n: (0, 0)), q_spec, q_spec],
        out_specs=[q_spec, cur, cur, cur, cur, pl.BlockSpec((None, SUBLANES, LANES), lambda h, n: (h, 0, 0))],
        name="attn_bwd", compiler_params=_params("parallel", "arbitrary"))(qr, kk, kk, vv, vv, sink_b, attn, d_attn_out)


SSM_T = 128
NQ = SUBLANES * SSM_STATE // LANES
NJ = SUBLANES


def _strided_put(ref, j, val):
    for q in range(NQ):
        ref.at[q][pl.ds(j, SSM_T, stride=NJ), :] = val[:, LANES * q:LANES * (q + 1)]


def _strided_get(ref, j):
    return jnp.concatenate([ref.at[q][pl.ds(j, SSM_T, stride=NJ), :] for q in range(NQ)], axis=1)


def _ssm_specs(L, rev):
    nt = L // SSM_T
    idx = (lambda i: nt - 1 - i) if rev else (lambda i: i)
    row = lambda w, cb=0: pl.BlockSpec((SSM_T, w), lambda i: (idx(i), cb))
    state = pl.BlockSpec((NQ, SSM_T * NJ, LANES), lambda i: (0, idx(i), 0))
    whole = lambda a: pl.BlockSpec(a.shape, lambda i: (0,) * a.ndim)
    return nt, row, state, whole


def ssm_fwd(u_bf, proj, u_cb, bd_re, bd_im, cd_re, cd_im, lam_re, lam_im, d_skip):
    L, d_ssm = u_bf.shape
    nt, row, state, whole = _ssm_specs(L, False)
    half = d_ssm // 2
    gw = d_ssm // NJ

    def body(u_ref, u0_ref, u1_ref, bdr, bdi, cdr, cdi, lr_ref, li_ref, d_ref, y_ref, sr_ref, si_ref, carry):
        i = pl.program_id(0)

        @pl.when(i == 0)
        def _():
            carry[...] = jnp.zeros_like(carry)

        for j in range(NJ):
            uj = u_ref[:, gw * j:gw * (j + 1)]
            _strided_put(sr_ref, j, _dot(uj, bdr[j]))
            _strided_put(si_ref, j, _dot(uj, bdi[j]))
        lr = [lr_ref[q] for q in range(NQ)]
        li = [li_ref[q] for q in range(NQ)]

        def step(t, s):
            sr, si = s
            rows = pl.ds(pl.multiple_of(t * NJ, NJ), NJ)
            nr = tuple(lr[q] * sr[q] - li[q] * si[q] + sr_ref[q, rows, :] for q in range(NQ))
            ni = tuple(lr[q] * si[q] + li[q] * sr[q] + si_ref[q, rows, :] for q in range(NQ))
            for q in range(NQ):
                sr_ref[q, rows, :] = nr[q]
                si_ref[q, rows, :] = ni[q]
            return nr, ni

        init = (tuple(carry[0, q] for q in range(NQ)), tuple(carry[1, q] for q in range(NQ)))
        sr, si = lax.fori_loop(0, SSM_T, step, init, unroll=8)
        for q in range(NQ):
            carry[0, q] = sr[q]
            carry[1, q] = si[q]
        uf = jnp.concatenate([u0_ref[...], u1_ref[...]], axis=1)
        for j in range(NJ):
            cols = slice(gw * j, gw * (j + 1))
            yj = _dot(_strided_get(sr_ref, j).astype(bf16), cdr[j]) - _dot(_strided_get(si_ref, j).astype(bf16), cdi[j])
            y_ref[:, cols] = yj + d_ref[:, cols] * uf[:, cols]

    s_shape = jax.ShapeDtypeStruct((NQ, L * NJ, LANES), f32)
    consts = (bd_re, bd_im, cd_re, cd_im, lam_re, lam_im, d_skip)
    return pl.pallas_call(
        body, out_shape=[jax.ShapeDtypeStruct((L, d_ssm), f32), s_shape, s_shape], grid=(nt,),
        in_specs=[row(d_ssm), row(half, u_cb), row(half, u_cb + 1)] + [whole(a) for a in consts],
        out_specs=[row(d_ssm), state, state],
        scratch_shapes=[pltpu.VMEM((2, NQ, NJ, LANES), f32)], name="ssm_fwd",
        compiler_params=_params("arbitrary"))(u_bf, proj, proj, *consts)


def ssm_bwd(dy, u_bf, proj, u_cb, s_re, s_im, bd_re, bd_im, cd_re, cd_im, lam_re, lam_im, d_skip):
    L, d_ssm = dy.shape
    nt, row, state, whole = _ssm_specs(L, True)
    half = d_ssm // 2
    gw = d_ssm // NJ

    def body(dy_ref, u_ref, u0_ref, u1_ref, sr_ref, si_ref, bdr, bdi, cdr, cdi, lr_ref, li_ref, d_ref,
             du_ref, dbdr, dbdi, dcdr, dcdi, dlr, dli, dd_ref, gr_ref, gi_ref, carry):
        i = pl.program_id(0)

        @pl.when(i == 0)
        def _():
            carry[...] = jnp.zeros_like(carry)
            for r in (dbdr, dbdi, dcdr, dcdi, dlr, dli, dd_ref):
                r[...] = jnp.zeros_like(r)

        dyf = dy_ref[...]
        dyb = dyf.astype(bf16)
        for j in range(NJ):
            dyj = dyb[:, gw * j:gw * (j + 1)]
            _strided_put(gr_ref, j, _dot(dyj, cdr[j], NT))
            _strided_put(gi_ref, j, -_dot(dyj, cdi[j], NT))
            dcdr[j] += _dot(_strided_get(sr_ref, j).astype(bf16), dyj, TN)
            dcdi[j] -= _dot(_strided_get(si_ref, j).astype(bf16), dyj, TN)
        lr = [lr_ref[q] for q in range(NQ)]
        li = [li_ref[q] for q in range(NQ)]

        def step(k, c):
            gr, gi, ar, ai = c
            rows = pl.ds(pl.multiple_of((SSM_T - 1 - k) * NJ, NJ), NJ)
            s_r = [sr_ref[q, rows, :] for q in range(NQ)]
            s_i = [si_ref[q, rows, :] for q in range(NQ)]
            ar = tuple(ar[q] + gr[q] * s_r[q] + gi[q] * s_i[q] for q in range(NQ))
            ai = tuple(ai[q] + gi[q] * s_r[q] - gr[q] * s_i[q] for q in range(NQ))
            nr = tuple(gr_ref[q, rows, :] + lr[q] * gr[q] + li[q] * gi[q] for q in range(NQ))
            ni = tuple(gi_ref[q, rows, :] + lr[q] * gi[q] - li[q] * gr[q] for q in range(NQ))
            for q in range(NQ):
                gr_ref[q, rows, :] = nr[q]
                gi_ref[q, rows, :] = ni[q]
            return nr, ni, ar, ai

        zero = tuple(jnp.zeros((NJ, LANES), f32) for _ in range(NQ))
        init = (tuple(carry[0, q] for q in range(NQ)), tuple(carry[1, q] for q in range(NQ)), zero, zero)
        gr, gi, ar, ai = lax.fori_loop(0, SSM_T, step, init, unroll=8)
        for q in range(NQ):
            carry[0, q] = gr[q]
            carry[1, q] = gi[q]
            dlr[q] += ar[q]
            dli[q] += ai[q]
        uf = jnp.concatenate([u0_ref[...], u1_ref[...]], axis=1)
        dd_ref[...] += colsum(dyf * uf)
        for j in range(NJ):
            cols = slice(gw * j, gw * (j + 1))
            gjr, gji = _strided_get(gr_ref, j).astype(bf16), _strided_get(gi_ref, j).astype(bf16)
            du_ref[:, cols] = _dot(gjr, bdr[j], NT) + _dot(gji, bdi[j], NT) + d_ref[:, cols] * dyf[:, cols]
            uj = u_ref[:, cols]
            dbdr[j] += _dot(uj, gjr, TN)
            dbdi[j] += _dot(uj, gji, TN)

    consts = (bd_re, bd_im, cd_re, cd_im, lam_re, lam_im, d_skip)
    acc = lambda a: jax.ShapeDtypeStruct(a.shape, f32)
    outs = [jax.ShapeDtypeStruct((L, d_ssm), f32), acc(bd_re), acc(bd_im), acc(cd_re), acc(cd_im), acc(lam_re), acc(lam_im), acc(d_skip)]
    return pl.pallas_call(
        body, out_shape=outs, grid=(nt,),
        in_specs=[row(d_ssm), row(d_ssm), row(half, u_cb), row(half, u_cb + 1), state, state] + [whole(a) for a in consts],
        out_specs=[row(d_ssm)] + [whole(a) for a in consts],
        scratch_shapes=[pltpu.VMEM((NQ, SSM_T * NJ, LANES), f32), pltpu.VMEM((NQ, SSM_T * NJ, LANES), f32),
                        pltpu.VMEM((2, NQ, NJ, LANES), f32)],
        name="ssm_bwd", compiler_params=_params("arbitrary"))(dy, u_bf, proj, proj, s_re, s_im, *consts)


def _cmul(ar, ai, br, bi):
    return ar * br - ai * bi, ar * bi + ai * br


def _disc(ar, ai, logdt):
    dt = jnp.exp(logdt)
    mag = jnp.exp(ar * dt)
    lr, li = mag * jnp.cos(ai * dt), mag * jnp.sin(ai * dt)
    den = ar * ar + ai * ai
    nr, ni = lr - 1.0, li
    fr, fi = (nr * ar + ni * ai) / den, (ni * ar - nr * ai) / den
    return dt, lr, li, den, fr, fi


def ssm_params(a_re, a_im, logdt_b, bt_re, bt_im, spread):
    def body(ar_ref, ai_ref, ld_ref, br_ref, bi_ref, sp_ref, lr_ref, li_ref, or_ref, oi_ref):
        _, lr, li, _, fr, fi = _disc(ar_ref[...], ai_ref[...], ld_ref[...])
        lr_ref[...] = lr
        li_ref[...] = li
        fre = jnp.dot(sp_ref[...], fr, precision=HIGHEST, preferred_element_type=f32)
        fie = jnp.dot(sp_ref[...], fi, precision=HIGHEST, preferred_element_type=f32)
        o_r, o_i = _cmul(fre, fie, br_ref[...], bi_ref[...])
        or_ref[...] = o_r
        oi_ref[...] = o_i

    g = jax.ShapeDtypeStruct(a_re.shape, f32)
    b = jax.ShapeDtypeStruct(bt_re.shape, f32)
    return pl.pallas_call(body, out_shape=[g, g, b, b], name="ssm_params",
                          compiler_params=_params())(a_re, a_im, logdt_b, bt_re, bt_im, spread)


def ssm_params_grad(a_re, a_im, logdt_b, bt_re, bt_im, spread, gather, dlam_re, dlam_im, dbt_re, dbt_im):
    def body(ar_ref, ai_ref, ld_ref, br_ref, bi_ref, sp_ref, ga_ref, glr_ref, gli_ref, gbr_ref, gbi_ref,
             dar_ref, dai_ref, dld_ref, dbr_ref, dbi_ref):
        ar, ai = ar_ref[...], ai_ref[...]
        dt, lr, li, den, fr, fi = _disc(ar, ai, ld_ref[...])
        hdot = functools.partial(jnp.dot, precision=HIGHEST, preferred_element_type=f32)
        fre, fie = hdot(sp_ref[...], fr), hdot(sp_ref[...], fi)
        gbr, gbi, br, bi = gbr_ref[...], gbi_ref[...], br_ref[...], bi_ref[...]
        dbr_ref[...], dbi_ref[...] = _cmul(fre, -fie, gbr, gbi)
        t_r, t_i = _cmul(br, -bi, gbr, gbi)
        gfr, gfi = hdot(ga_ref[...], t_r), hdot(ga_ref[...], t_i)
        iwr, iwi = ar / den, -ai / den
        x_r, x_i = _cmul(iwr, -iwi, gfr, gfi)
        glr, gli = glr_ref[...] + x_r, gli_ref[...] + x_i
        q_r, q_i = _cmul(fr, fi, iwr, iwi)
        gwr, gwi = _cmul(-q_r, q_i, gfr, gfi)
        y_r, y_i = _cmul(dt * lr, -dt * li, glr, gli)
        dar_ref[...] = gwr + y_r
        dai_ref[...] = gwi + y_i
        wl_r, wl_i = _cmul(ar, ai, lr, li)
        z_r, _ = _cmul(wl_r, -wl_i, glr, gli)
        dld_ref[...] = jnp.sum(z_r * dt, axis=1, keepdims=True)

    g = jax.ShapeDtypeStruct(a_re.shape, f32)
    b = jax.ShapeDtypeStruct(bt_re.shape, f32)
    return pl.pallas_call(body, out_shape=[g, g, jax.ShapeDtypeStruct((a_re.shape[0], 1), f32), b, b], name="ssm_params_grad",
                          compiler_params=_params())(a_re, a_im, logdt_b, bt_re, bt_im, spread, gather, dlam_re, dlam_im, dbt_re, dbt_im)


def _block_diag(t, rows, cols):
    G = t.shape[0]
    t = t.reshape(G // NJ, NJ, rows, cols)
    eye = jnp.eye(NJ, dtype=t.dtype)
    return jnp.einsum('jgrc,gh->jgrhc', t, eye).reshape(G // NJ, NJ * rows, NJ * cols)


def _block_diag_take(m, rows, cols):
    J = m.shape[0]
    m = m.reshape(J, NJ, rows, NJ, cols)
    idx = jnp.arange(NJ)
    return m[:, idx, :, idx, :].transpose(1, 0, 2, 3).reshape(J * NJ, rows, cols)


def _state_layout(t):
    return t.reshape(NJ, NQ, LANES).transpose(1, 0, 2)


def _state_layout_inv(t, G, N):
    return t.transpose(1, 0, 2).reshape(G, N)


def _tiles2d(shape, budget_rows=128):
    rows, cols = shape
    tr = rows
    if rows > budget_rows:
        tr = budget_rows
        while rows % tr:
            tr -= SUBLANES
    return tr, cols


def _adam_update(w, g, m, v):
    c1 = 1.0 - ADAM_B1 ** ADAM_STEP
    c2 = 1.0 - ADAM_B2 ** ADAM_STEP
    nm = ADAM_B1 * m + (1.0 - ADAM_B1) * g
    nv = ADAM_B2 * v + (1.0 - ADAM_B2) * (g * g)
    delta = -ADAM_LR * ((nm / c1) / (jnp.sqrt(nv / c2) + ADAM_EPS) + ADAM_WD * w)
    return delta, nm, nv


def adamw(name, w, g, m, v):
    tr, cols = _tiles2d(w.shape, 128 if w.shape[1] > 1024 else 256)

    def body(w_ref, g_ref, m_ref, v_ref, d_ref, nm_ref, nv_ref):
        d_ref[...], nm_ref[...], nv_ref[...] = _adam_update(w_ref[...], g_ref[...], m_ref[...], v_ref[...])

    spec = pl.BlockSpec((tr, cols), lambda i: (i, 0))
    o = jax.ShapeDtypeStruct(w.shape, f32)
    return pl.pallas_call(body, out_shape=[o, o, o], grid=(w.shape[0] // tr,), in_specs=[spec] * 4, out_specs=[spec] * 3,
                          name=name, compiler_params=_params("parallel"))(w, g, m, v)


def adamw_halves(name, w, own, got, m, v):
    h, cols = own.shape
    tr, _ = _tiles2d((h, cols), 128 if cols > 1024 else 256)
    nh = h // tr

    def body(w_ref, own_ref, got_ref, m_ref, v_ref, g_ref, d_ref, nm_ref, nv_ref):
        mine = (pl.program_id(0) // nh) == lax.axis_index("c")
        g = jnp.where(mine, own_ref[...], got_ref[...])
        g_ref[...] = g
        d_ref[...], nm_ref[...], nv_ref[...] = _adam_update(w_ref[...], g, m_ref[...], v_ref[...])

    spec = pl.BlockSpec((tr, cols), lambda i: (i, 0))
    half = pl.BlockSpec((tr, cols), lambda i: (i % nh, 0))
    o = jax.ShapeDtypeStruct(w.shape, f32)
    return pl.pallas_call(body, out_shape=[o, o, o, o], grid=(2 * nh,), in_specs=[spec, half, half, spec, spec], out_specs=[spec] * 4,
                          name=name, compiler_params=_params("parallel"))(w, own, got, m, v)


def pair_sum(name, g, got, c_arr):
    S, h, cols = got.shape
    tr, _ = _tiles2d((h, cols), 256)
    nh = h // tr

    def body(c_ref, g_ref, o_ref, out_ref):
        out_ref[...] = (g_ref[...].astype(f32) + o_ref[...].astype(f32)).astype(out_ref.dtype)

    spec = pl.BlockSpec((None, tr, cols), lambda s, i, c: (s, i, 0))
    grid_spec = pltpu.PrefetchScalarGridSpec(
        num_scalar_prefetch=1, grid=(S, nh),
        in_specs=[pl.BlockSpec((None, tr, cols), lambda s, i, c: (s, c[0] * nh + i, 0)), spec], out_specs=spec)
    return pl.pallas_call(body, out_shape=jax.ShapeDtypeStruct(got.shape, g.dtype), grid_spec=grid_spec, name=name,
                          compiler_params=_params("parallel", "parallel"))(c_arr, g, got)


def chip_sum(name, pair, landed, mine_arr, dep=None):
    n_in, h, cols = landed.shape
    tr, _ = _tiles2d((h, cols), 256)

    def body(s_ref, p_ref, l_ref, *rest):
        acc = p_ref[...].astype(f32)
        for k in range(n_in):
            acc = acc + l_ref[k].astype(f32)
        rest[-1][...] = acc

    in_specs, operands, _ = _with_dep(
        [pl.BlockSpec((None, tr, cols), lambda i, s: (s[0], i, 0)), pl.BlockSpec((n_in, tr, cols), lambda i, s: (0, i, 0))],
        [pair, landed], dep)
    grid_spec = pltpu.PrefetchScalarGridSpec(num_scalar_prefetch=1, grid=(h // tr,), in_specs=in_specs,
                                             out_specs=pl.BlockSpec((tr, cols), lambda i, s: (i, 0)))
    return pl.pallas_call(body, out_shape=jax.ShapeDtypeStruct((h, cols), f32), grid_spec=grid_spec, name=name,
                          compiler_params=_params("parallel"))(mine_arr, *operands)


def into_slot(name, w, slot_arr, n_slots, dtype, dep=None):
    tr, cols = _tiles2d(w.shape, 256)

    def body(s_ref, w_ref, *rest):
        rest[-1][...] = w_ref[...].astype(dtype)

    in_specs, operands, _ = _with_dep([pl.BlockSpec((tr, cols), lambda i, s: (i, 0))], [w], dep)
    grid_spec = pltpu.PrefetchScalarGridSpec(num_scalar_prefetch=1, grid=(w.shape[0] // tr,), in_specs=in_specs,
                                             out_specs=pl.BlockSpec((None, tr, cols), lambda i, s: (s[0], i, 0)))
    return pl.pallas_call(body, out_shape=jax.ShapeDtypeStruct((n_slots,) + w.shape, dtype), grid_spec=grid_spec, name=name,
                          compiler_params=_params("parallel"))(slot_arr, *operands)


def sum_slots(name, t):
    S, rows, cols = t.shape
    tr, _ = _tiles2d((rows, cols), 256)

    def body(t_ref, o_ref):
        acc = t_ref[0]
        for s in range(1, S):
            acc = acc + t_ref[s]
        o_ref[...] = acc

    return pl.pallas_call(body, out_shape=jax.ShapeDtypeStruct((rows, cols), f32), grid=(rows // tr,),
                          in_specs=[pl.BlockSpec((S, tr, cols), lambda i: (0, i, 0))], out_specs=pl.BlockSpec((tr, cols), lambda i: (i, 0)),
                          name=name, compiler_params=_params("parallel"))(t)


ANY = pl.BlockSpec(memory_space=pl.ANY)


def _place():
    x, y, c = lax.axis_index("x"), lax.axis_index("y"), lax.axis_index("c")
    return x, y, c


def _other_chips(x, y):
    return [(1 - x, y, 2 * (1 - x) + y), (x, 1 - y, 2 * x + 1 - y), (1 - x, 1 - y, 2 * (1 - x) + 1 - y)]


def gather_weights(bufs):
    nw = len(bufs)

    def body(*refs):
        dst = refs[nw:2 * nw]
        send1, recv1, send2, recv2 = refs[2 * nw:]
        x, y, c = _place()
        mine = 2 * x + y
        chips = _other_chips(x, y)
        sib = (x, y, 1 - c)
        first, passed = [], []
        for w in range(nw):
            h = dst[w].shape[1] // 2
            for k, (px, py, _) in enumerate(chips):
                half = dst[w].at[mine, pl.ds(c * h, h), :]
                cp = pltpu.make_async_remote_copy(src_ref=half, dst_ref=half,
                                                  send_sem=send1.at[w, k], recv_sem=recv1.at[w, k],
                                                  device_id=(px, py, c), device_id_type=MESH)
                cp.start()
                first.append(cp)
        for w in range(nw):
            h = dst[w].shape[1] // 2
            for k, (px, py, s) in enumerate(chips):
                landed = dst[w].at[s, pl.ds(c * h, h), :]
                pltpu.make_async_remote_copy(src_ref=landed, dst_ref=landed, send_sem=send1.at[w, k], recv_sem=recv1.at[w, k],
                                             device_id=(px, py, c), device_id_type=MESH).wait_recv()
                cp = pltpu.make_async_remote_copy(src_ref=landed, dst_ref=landed, send_sem=send2.at[w, k], recv_sem=recv2.at[w, k],
                                                  device_id=sib, device_id_type=MESH)
                cp.start()
                passed.append(cp)
        for w in range(nw):
            h = dst[w].shape[1] // 2
            for k, (px, py, s) in enumerate(chips):
                other = dst[w].at[s, pl.ds((1 - c) * h, h), :]
                pltpu.make_async_remote_copy(src_ref=other, dst_ref=other, send_sem=send2.at[w, k], recv_sem=recv2.at[w, k],
                                             device_id=sib, device_id_type=MESH).wait_recv()
        for cp in first + passed:
            cp.wait_send()

    sem = pltpu.SemaphoreType.DMA((nw, 3))
    return pl.pallas_call(
        body, out_shape=[jax.ShapeDtypeStruct(b.shape, b.dtype) for b in bufs],
        in_specs=[ANY] * nw, out_specs=[ANY] * nw, input_output_aliases={w: w for w in range(nw)},
        scratch_shapes=[sem, sem, sem, sem], name="gather_weights",
        compiler_params=pltpu.CompilerParams(has_side_effects=True))(*bufs)


def swap_halves(name, grads):
    nw = len(grads)

    def body(*refs):
        src, got = refs[:nw], refs[nw:2 * nw]
        send, recv = refs[2 * nw:]
        x, y, c = _place()
        cps = []
        for w in range(nw):
            h = src[w].shape[1] // 2
            cp = pltpu.make_async_remote_copy(src_ref=src[w].at[:, pl.ds((1 - c) * h, h), :], dst_ref=got[w],
                                              send_sem=send.at[w], recv_sem=recv.at[w],
                                              device_id=(x, y, 1 - c), device_id_type=MESH)
            cp.start()
            cps.append(cp)
        for cp in cps:
            cp.wait()

    half = [jax.ShapeDtypeStruct((g.shape[0], g.shape[1] // 2, g.shape[2]), g.dtype) for g in grads]
    sem = pltpu.SemaphoreType.DMA((nw,))
    return pl.pallas_call(
        body, out_shape=half, in_specs=[ANY] * nw, out_specs=[ANY] * nw,
        scratch_shapes=[sem, sem], name=name,
        compiler_params=pltpu.CompilerParams(has_side_effects=True))(*grads)


def scatter_to_owners(parts):
    nw = len(parts)

    def body(*refs):
        src, dst = refs[:nw], refs[nw:2 * nw]
        send, recv = refs[2 * nw:]
        x, y, c = _place()
        cps = []
        for w in range(nw):
            for k, (px, py, s) in enumerate(_other_chips(x, y)):
                cp = pltpu.make_async_remote_copy(src_ref=src[w].at[s], dst_ref=dst[w].at[k],
                                                  send_sem=send.at[w, k], recv_sem=recv.at[w, k],
                                                  device_id=(px, py, c), device_id_type=MESH)
                cp.start()
                cps.append(cp)
        for cp in cps:
            cp.wait()

    sem = pltpu.SemaphoreType.DMA((nw, 3))
    return pl.pallas_call(
        body, out_shape=[jax.ShapeDtypeStruct((N_CHIPS - 1,) + p.shape[1:], p.dtype) for p in parts],
        in_specs=[ANY] * nw, out_specs=[ANY] * nw,
        scratch_shapes=[sem, sem], name="scatter_to_owners",
        compiler_params=pltpu.CompilerParams(has_side_effects=True))(*parts)


def join_halves(name, halves):
    nw = len(halves)

    def body(*refs):
        src, dst = refs[:nw], refs[nw:2 * nw]
        send, recv = refs[2 * nw:]
        x, y, c = _place()
        cps = []
        for w in range(nw):
            cp = pltpu.make_async_remote_copy(src_ref=src[w], dst_ref=dst[w], send_sem=send.at[w], recv_sem=recv.at[w],
                                              device_id=(x, y, 1 - c), device_id_type=MESH)
            cp.start()
            cps.append(cp)
        for cp in cps:
            cp.wait()

    sem = pltpu.SemaphoreType.DMA((nw,))
    return pl.pallas_call(
        body, out_shape=[jax.ShapeDtypeStruct(p.shape, p.dtype) for p in halves], in_specs=[ANY] * nw, out_specs=[ANY] * nw,
        scratch_shapes=[sem, sem], name=name,
        compiler_params=pltpu.CompilerParams(has_side_effects=True))(*halves)


SEM = pl.BlockSpec(memory_space=pltpu.SEMAPHORE)
VM = pl.BlockSpec(memory_space=pltpu.VMEM)
DATAFLOW = pltpu.SideEffectType.DATAFLOW_SIDE_EFFECTING
TOKEN = jax.ShapeDtypeStruct((SUBLANES, LANES), f32)


def _gather_copy(buf, w, k, chip, c, mine, send, recv):
    px, py, _ = chip
    h = buf.shape[1] // 2
    half = buf.at[mine, pl.ds(c * h, h), :]
    return pltpu.make_async_remote_copy(src_ref=half, dst_ref=half, send_sem=send.at[3 * w + k], recv_sem=recv.at[3 * w + k],
                                        device_id=(px, py, c), device_id_type=MESH)


def _gather_landing(buf, w, k, chip, c, send, recv):
    px, py, s = chip
    h = buf.shape[1] // 2
    landed = buf.at[s, pl.ds(c * h, h), :]
    return pltpu.make_async_remote_copy(src_ref=landed, dst_ref=landed, send_sem=send.at[3 * w + k], recv_sem=recv.at[3 * w + k],
                                        device_id=(px, py, c), device_id_type=MESH)


def gather_start(name, bufs, groups, after):
    nw, ng = len(bufs), len(groups)

    def body(*refs):
        outs = refs[nw + 1:]
        sems, dst = outs[:2 * ng], outs[2 * ng:2 * ng + nw]
        token = outs[2 * ng + nw]
        x, y, c = _place()
        mine = 2 * x + y
        for g, members in enumerate(groups):
            for i, w in enumerate(members):
                for k, chip in enumerate(_other_chips(x, y)):
                    _gather_copy(dst[w], i, k, chip, c, mine, sems[2 * g], sems[2 * g + 1]).start()
        token[...] = jnp.zeros_like(token)

    sem_shapes = []
    for members in groups:
        sem_shapes += [pltpu.SemaphoreType.DMA((3 * len(members),))] * 2
    outs = pl.pallas_call(
        body, out_shape=sem_shapes + [jax.ShapeDtypeStruct(b.shape, b.dtype) for b in bufs] + [TOKEN],
        in_specs=[ANY] * (nw + 1), out_specs=[SEM] * (2 * ng) + [ANY] * nw + [VM],
        input_output_aliases={w: 2 * ng + w for w in range(nw)}, name=name,
        compiler_params=pltpu.CompilerParams(has_side_effects=DATAFLOW))(*bufs, after)
    return [(outs[2 * g], outs[2 * g + 1]) for g in range(ng)], list(outs[2 * ng:2 * ng + nw]), outs[2 * ng + nw]


def gather_wait(name, bufs, send, recv, after):
    nw = len(bufs)

    def body(*refs):
        src = refs[:nw]
        send_ref, recv_ref = refs[nw], refs[nw + 1]
        x, y, c = _place()
        mine = 2 * x + y
        for w in range(nw):
            for k, chip in enumerate(_other_chips(x, y)):
                _gather_copy(src[w], w, k, chip, c, mine, send_ref, recv_ref).wait_send()
                _gather_landing(src[w], w, k, chip, c, send_ref, recv_ref).wait_recv()

    return pl.pallas_call(
        body, out_shape=[jax.ShapeDtypeStruct(b.shape, b.dtype) for b in bufs],
        in_specs=[ANY] * nw + [SEM, SEM, ANY], out_specs=[ANY] * nw,
        input_output_aliases={w: w for w in range(nw)}, name=name,
        compiler_params=pltpu.CompilerParams(has_side_effects=DATAFLOW))(*bufs, send, recv, after)


def gather_forward(name, bufs):
    nw = len(bufs)

    def body(*refs):
        dst = refs[nw:2 * nw]
        send, recv = refs[2 * nw:]
        x, y, c = _place()
        sib = (x, y, 1 - c)
        cps = []
        for w in range(nw):
            h = dst[w].shape[1] // 2
            for k, (_, _, s) in enumerate(_other_chips(x, y)):
                landed = dst[w].at[s, pl.ds(c * h, h), :]
                cp = pltpu.make_async_remote_copy(src_ref=landed, dst_ref=landed, send_sem=send.at[w, k], recv_sem=recv.at[w, k],
                                                  device_id=sib, device_id_type=MESH)
                cp.start()
                cps.append(cp)
        for w in range(nw):
            h = dst[w].shape[1] // 2
            for k, (_, _, s) in enumerate(_other_chips(x, y)):
                other = dst[w].at[s, pl.ds((1 - c) * h, h), :]
                pltpu.make_async_remote_copy(src_ref=other, dst_ref=other, send_sem=send.at[w, k], recv_sem=recv.at[w, k],
                                             device_id=sib, device_id_type=MESH).wait_recv()
        for cp in cps:
            cp.wait_send()

    sem = pltpu.SemaphoreType.DMA((nw, 3))
    return pl.pallas_call(
        body, out_shape=[jax.ShapeDtypeStruct(b.shape, b.dtype) for b in bufs],
        in_specs=[ANY] * nw, out_specs=[ANY] * nw, input_output_aliases={w: w for w in range(nw)},
        scratch_shapes=[sem, sem], name=name, compiler_params=pltpu.CompilerParams(has_side_effects=True))(*bufs)


def _scatter_copy(src, dst, w, k, chip, c, send, recv):
    px, py, s = chip
    return pltpu.make_async_remote_copy(src_ref=src.at[s], dst_ref=dst.at[k], send_sem=send.at[3 * w + k], recv_sem=recv.at[3 * w + k],
                                        device_id=(px, py, c), device_id_type=MESH)


def scatter_start(name, parts):
    nw = len(parts)
    lands = [pltpu.with_memory_space_constraint(lax.empty((N_CHIPS - 1,) + p.shape[1:], p.dtype), pltpu.HBM) for p in parts]

    def body(*refs):
        outs = refs[2 * nw:]
        send, recv = outs[0], outs[1]
        src, dst, token = outs[2:2 + nw], outs[2 + nw:2 + 2 * nw], outs[2 + 2 * nw]
        x, y, c = _place()
        for w in range(nw):
            for k, chip in enumerate(_other_chips(x, y)):
                _scatter_copy(src[w], dst[w], w, k, chip, c, send, recv).start()
        token[...] = jnp.zeros_like(token)

    sem = pltpu.SemaphoreType.DMA((3 * nw,))
    outs = pl.pallas_call(
        body, out_shape=[sem, sem] + [jax.ShapeDtypeStruct(p.shape, p.dtype) for p in parts]
        + [jax.ShapeDtypeStruct(l.shape, l.dtype) for l in lands] + [TOKEN],
        in_specs=[ANY] * (2 * nw), out_specs=[SEM, SEM] + [ANY] * (2 * nw) + [VM],
        input_output_aliases={i: 2 + i for i in range(2 * nw)}, name=name,
        compiler_params=pltpu.CompilerParams(has_side_effects=DATAFLOW))(*parts, *lands)
    return outs[0], outs[1], list(outs[2:2 + nw]), list(outs[2 + nw:2 + 2 * nw]), outs[2 + 2 * nw]


def scatter_wait(name, parts, lands, send, recv, after):
    nw = len(parts)

    def body(*refs):
        src, dst = refs[:nw], refs[nw:2 * nw]
        send_ref, recv_ref = refs[2 * nw], refs[2 * nw + 1]
        x, y, c = _place()
        for w in range(nw):
            for k, chip in enumerate(_other_chips(x, y)):
                cp = _scatter_copy(src[w], dst[w], w, k, chip, c, send_ref, recv_ref)
                cp.wait_send()
                cp.wait_recv()

    outs = pl.pallas_call(
        body, out_shape=[jax.ShapeDtypeStruct(a.shape, a.dtype) for a in list(parts) + list(lands)],
        in_specs=[ANY] * (2 * nw) + [SEM, SEM, ANY], out_specs=[ANY] * (2 * nw),
        input_output_aliases={i: i for i in range(2 * nw)}, name=name,
        compiler_params=pltpu.CompilerParams(has_side_effects=DATAFLOW))(*parts, *lands, send, recv, after)
    return list(outs[:nw]), list(outs[nw:])


def _sibling_copy(src, dst, w, c, half_rows, send, recv, sib):
    if half_rows:
        h = src.shape[1] // 2
        src = src.at[:, pl.ds((1 - c) * h, h), :]
    return pltpu.make_async_remote_copy(src_ref=src, dst_ref=dst, send_sem=send.at[w], recv_sem=recv.at[w],
                                        device_id=sib, device_id_type=MESH)


def _landing(shape, dtype):
    return pltpu.with_memory_space_constraint(lax.empty(shape, dtype), pltpu.HBM)


def sibling_start(name, srcs, half_rows):
    nw = len(srcs)
    lands = [_landing((s.shape[0], s.shape[1] // 2, s.shape[2]) if half_rows else s.shape, s.dtype) for s in srcs]

    def body(*refs):
        outs = refs[2 * nw:]
        send, recv = outs[0], outs[1]
        src, dst, token = outs[2:2 + nw], outs[2 + nw:2 + 2 * nw], outs[2 + 2 * nw]
        x, y, c = _place()
        for w in range(nw):
            _sibling_copy(src[w], dst[w], w, c, half_rows, send, recv, (x, y, 1 - c)).start()
        token[...] = jnp.zeros_like(token)

    sem = pltpu.SemaphoreType.DMA((nw,))
    outs = pl.pallas_call(
        body, out_shape=[sem, sem] + [jax.ShapeDtypeStruct(a.shape, a.dtype) for a in list(srcs) + lands] + [TOKEN],
        in_specs=[ANY] * (2 * nw), out_specs=[SEM, SEM] + [ANY] * (2 * nw) + [VM],
        input_output_aliases={i: 2 + i for i in range(2 * nw)}, name=name,
        compiler_params=pltpu.CompilerParams(has_side_effects=DATAFLOW))(*srcs, *lands)
    return outs[0], outs[1], list(outs[2:2 + nw]), list(outs[2 + nw:2 + 2 * nw]), outs[2 + 2 * nw]


def sibling_wait(name, srcs, lands, send, recv, half_rows, after):
    nw = len(srcs)

    def body(*refs):
        src, dst = refs[:nw], refs[nw:2 * nw]
        send_ref, recv_ref = refs[2 * nw], refs[2 * nw + 1]
        x, y, c = _place()
        for w in range(nw):
            cp = _sibling_copy(src[w], dst[w], w, c, half_rows, send_ref, recv_ref, (x, y, 1 - c))
            cp.wait_send()
            cp.wait_recv()

    outs = pl.pallas_call(
        body, out_shape=[jax.ShapeDtypeStruct(a.shape, a.dtype) for a in list(srcs) + list(lands)],
        in_specs=[ANY] * (2 * nw) + [SEM, SEM, ANY], out_specs=[ANY] * (2 * nw),
        input_output_aliases={i: i for i in range(2 * nw)}, name=name,
        compiler_params=pltpu.CompilerParams(has_side_effects=DATAFLOW))(*srcs, *lands, send, recv, after)
    return list(outs[:nw]), list(outs[nw:])


def _peer(x, y, c, r):
    return (1 - x if r & 4 else x, 1 - y if r & 2 else y, 1 - c if r & 1 else c)


def _everyone_copy(buf, r, x, y, c, send, recv, landing):
    px, py, pc = _peer(x, y, c, r)
    slot = buf.at[4 * px + 2 * py + pc] if landing else buf.at[4 * x + 2 * y + c]
    return pltpu.make_async_remote_copy(src_ref=slot, dst_ref=slot, send_sem=send.at[r - 1], recv_sem=recv.at[r - 1],
                                        device_id=(px, py, pc), device_id_type=MESH)


def everyone_start(name, buf):
    def body(buf_in, send, recv, buf_ref, token):
        x, y, c = _place()
        for r in range(1, N_DEV):
            _everyone_copy(buf_ref, r, x, y, c, send, recv, False).start()
        token[...] = jnp.zeros_like(token)

    sem = pltpu.SemaphoreType.DMA((N_DEV - 1,))
    return pl.pallas_call(
        body, out_shape=[sem, sem, jax.ShapeDtypeStruct(buf.shape, buf.dtype), TOKEN],
        in_specs=[ANY], out_specs=[SEM, SEM, ANY, VM], input_output_aliases={0: 2}, name=name,
        compiler_params=pltpu.CompilerParams(has_side_effects=DATAFLOW))(buf)


def everyone_wait(name, buf, send, recv, after):
    def body(buf_ref, send_ref, recv_ref, after_ref, out_ref):
        x, y, c = _place()
        for r in range(1, N_DEV):
            _everyone_copy(buf_ref, r, x, y, c, send_ref, recv_ref, False).wait_send()
            _everyone_copy(buf_ref, r, x, y, c, send_ref, recv_ref, True).wait_recv()

    return pl.pallas_call(
        body, out_shape=jax.ShapeDtypeStruct(buf.shape, buf.dtype), in_specs=[ANY, SEM, SEM, ANY], out_specs=ANY,
        input_output_aliases={0: 0}, name=name,
        compiler_params=pltpu.CompilerParams(has_side_effects=DATAFLOW))(buf, send, recv, after)


def all_reduce_small(v):
    R, n = v.shape

    def body(v_ref, o_ref, all_ref, send_sems, recv_sems, local_sem):
        x, y, c = _place()
        me, sib = (x, y, c), (x, y, 1 - c)
        chips = [(1 - x, y), (x, 1 - y), (1 - x, 1 - y)]

        def rows(px, py, pc):
            return all_ref.at[pl.ds((4 * px + 2 * py + pc) * R, R), :]

        def copy(k, block, to, src=None):
            return pltpu.make_async_remote_copy(src_ref=rows(*block) if src is None else src, dst_ref=rows(*block),
                                                send_sem=send_sems.at[k], recv_sem=recv_sems.at[k],
                                                device_id=to, device_id_type=MESH)

        own = pltpu.make_async_copy(v_ref, rows(*me), local_sem)
        own.start()
        first = [copy(0, me, sib, src=v_ref)]
        first += [copy(1 + j, me, (*chip, c), src=v_ref) for j, chip in enumerate(chips)]
        for cp in first:
            cp.start()
        passed = [copy(4 + j, (*chip, c), sib) for j, chip in enumerate(chips)]
        for j, chip in enumerate(chips):
            copy(1 + j, (*chip, c), me).wait_recv()
            passed[j].start()
        copy(0, sib, me).wait_recv()
        for j, chip in enumerate(chips):
            copy(4 + j, (*chip, 1 - c), me).wait_recv()
        for cp in first + passed:
            cp.wait_send()
        own.wait()
        acc = all_ref[pl.ds(0, R), :]
        for d in range(1, N_DEV):
            acc = acc + all_ref[pl.ds(d * R, R), :]
        o_ref[...] = acc

    vm = pl.BlockSpec(memory_space=pltpu.VMEM)
    return pl.pallas_call(
        body, out_shape=jax.ShapeDtypeStruct((R, n), f32), in_specs=[vm], out_specs=vm,
        scratch_shapes=[pltpu.VMEM((N_DEV * R, n), f32), pltpu.SemaphoreType.DMA((7,)), pltpu.SemaphoreType.DMA((7,)),
                        pltpu.SemaphoreType.DMA],
        name="all_reduce_small", compiler_params=pltpu.CompilerParams(vmem_limit_bytes=VMEM_LIMIT, has_side_effects=True))(v)


def local_step(x, pos, tgt, small, d_in, get_w, put_g, first_dep=None):
    L, D = x.shape
    d_kv = N_KV_HEADS * HEAD_DIM
    d_ssm = small["d_skip"].shape[1]
    d_attn = d_in - 2 * d_kv - d_ssm
    big = {}
    G = d_ssm // SSM_GROUP
    N, P = SSM_STATE, SSM_GROUP
    gbf = bf16

    half_dim = HEAD_DIM // 2
    inv_freq = ROPE_THETA ** (-jnp.arange(half_dim, dtype=f32) / half_dim)
    inv_freq = jnp.tile(inv_freq, LANES // half_dim).reshape(1, LANES)
    sink_b = jnp.broadcast_to(small["sinks"].reshape(-1, 1), (small["sinks"].size, LANES))

    spread = jnp.repeat(jnp.eye(G, dtype=f32), P, axis=0)
    logdt_b = jnp.broadcast_to(small["log_dt"].reshape(G, 1), (G, N))
    bt_re = small["b_re"].reshape(G, N, P).transpose(0, 2, 1).reshape(G * P, N)
    bt_im = small["b_im"].reshape(G, N, P).transpose(0, 2, 1).reshape(G * P, N)
    a_re, a_im = small["a_re"].reshape(G, N), small["a_im"].reshape(G, N)
    lam_re, lam_im, bbt_re, bbt_im = ssm_params(a_re, a_im, logdt_b, bt_re, bt_im, spread)
    bd_re = _block_diag(bbt_re.reshape(G, P, N), P, N).astype(bf16)
    bd_im = _block_diag(bbt_im.reshape(G, P, N), P, N).astype(bf16)
    c_re = small["c_re"].reshape(G, P, N).transpose(0, 2, 1)
    c_im = small["c_im"].reshape(G, P, N).transpose(0, 2, 1)
    cd_re = _block_diag(c_re, N, P).astype(bf16)
    cd_im = _block_diag(c_im, N, P).astype(bf16)
    lam_re_l, lam_im_l = _state_layout(lam_re), _state_layout(lam_im)

    def k1(i, nt, xt, g):
        return (rms_fwd(xt, g),)
    xn = rowwise("pre_mix_norm", k1, L, [full(x)], [small["g_pre_mix"]], [(D, bf16)], dep=first_dep)[0]
    big["w_in"] = get_w("w_in", xn)
    proj = mm_nn("proj_in", xn, big["w_in"])
    qr, kk, vv, u_bf = qkv_prep(proj, pos, inv_freq, d_attn, d_kv)
    attn = attn_fwd(qr, kk, vv, sink_b)
    u_cb = (d_attn + 2 * d_kv) // (d_ssm // 2)
    y, s_re, s_im = ssm_fwd(u_bf, proj, u_cb, bd_re, bd_im, cd_re, cd_im, lam_re_l, lam_im_l, small["d_skip"])

    def k5(i, nt, yt):
        return (gelu(yt),)
    z_bf = rowwise("ssm_gelu", k5, L, [full(y)], [], [(d_ssm, bf16)])[0]
    big["w_glu"] = get_w("w_glu", z_bf)
    gl = mm_nn("glu_proj", z_bf, big["w_glu"])

    def k6(i, nt, at, yt, glt, bg, ga, gs):
        ssm = gelu(yt) * sigmoid(glt + bg)
        return (jnp.concatenate([rms_fwd(at, ga), rms_fwd(ssm, gs)], axis=1),)
    mixed = rowwise("mix_norms", k6, L, [full(attn), full(y), full(gl)],
                    [small["b_glu"], small["g_attn_out"], small["g_ssm_out"]], [(d_attn + d_ssm, bf16)])[0]
    big["w_o"] = get_w("w_o", mixed)
    mix = mm_nn("proj_out", mixed, big["w_o"])

    def k7(i, nt, xt, mt, gpm, gpf):
        h = xt + rms_fwd(mt, gpm)
        return h, rms_fwd(h, gpf)
    h, hn = rowwise("post_mix", k7, L, [full(x), full(mix)], [small["g_post_mix"], small["g_pre_ffn"]], [(D, f32), (D, bf16)])
    big["w_gate"] = get_w("w_gate", hn)
    big["w_up"] = get_w("w_up", hn)
    gt, up, hid = ffn_hidden(hn, big["w_gate"], big["w_up"])
    d_ff_dim = gt.shape[1]
    big["w_down"] = get_w("w_down", hid)
    ff = mm_nn("ffn_down", hid, big["w_down"], tk=d_ff_dim // 2)

    def k9(i, nt, ht, fft, tt, g):
        out = ht + rms_fwd(fft, g)
        err = out - tt
        per_row = jnp.mean(err * err, axis=-1, keepdims=True)
        loss = 0.5 * jnp.sum(per_row) * jnp.where(_lane((1, LANES)) == 0, 1.0, 0.0)
        d_out = err * (1.0 / D)
        d_ff, dg = rms_bwd(fft, g, d_out)
        return d_out, d_ff, dg, loss
    d_out, d_ff, dg_post_ffn, loss = rowwise("loss_head", k9, L, [full(h), full(ff), full(tgt)], [small["g_post_ffn"]],
                                             [(D, f32), (D, bf16)], reds=[D, LANES])

    d_gt, d_up = ffn_hidden_grad(d_ff, big["w_down"], gt, up)
    token = put_g("w_down", mm_tn("dw_down", hid, d_ff, out_dtype=gbf, tm=d_ff_dim // N_CHIPS))
    d_hn = mm_nt_pair("d_hn", d_gt, big["w_gate"], d_up, big["w_up"], dep=token)
    token = put_g("w_gate", mm_tn("dw_gate", hn, d_gt, shards=N_CHIPS, out_dtype=gbf))
    token = put_g("w_up", mm_tn("dw_up", hn, d_up, shards=N_CHIPS, out_dtype=gbf, dep=token))

    def k11(i, nt, ht, da, do, mt, gpf, gpm):
        dh_n, dg_pf = rms_bwd(ht, gpf, da)
        dh = do + dh_n
        d_mix, dg_pm = rms_bwd(mt, gpm, dh)
        return dh, d_mix, dg_pf, dg_pm
    dh, d_mix, dg_pre_ffn, dg_post_mix = rowwise("post_mix_grad", k11, L, [full(h), full(d_hn), full(d_out), full(mix)],
                                                 [small["g_pre_ffn"], small["g_post_mix"]], [(D, f32), (D, bf16)], reds=[D, D], dep=token)
    d_mixed = mm_nt("d_mixed", d_mix, big["w_o"])
    token = put_g("w_o", mm_tn("dw_o", mixed, d_mix, out_dtype=gbf))

    def k12(i, nt, at, yt, glt, da_n, ds_n, bg, ga, gs):
        z = gelu(yt)
        sg = sigmoid(glt + bg)
        ssm = z * sg
        d_at, dga = rms_bwd(at, ga, da_n)
        d_ssm_t, dgs = rms_bwd(ssm, gs, ds_n)
        d_gl = d_ssm_t * z * sg * (1.0 - sg)
        return d_at, d_ssm_t * sg, d_gl, dga, dgs, colsum(d_gl)
    d_attn_o, dz1, d_gl, dg_attn, dg_ssm, db_glu = rowwise(
        "mix_norms_grad", k12, L, [full(attn), full(y), full(gl), (d_mixed, d_attn, 0, 0), (d_mixed, d_ssm, d_attn // d_ssm, 0)],
        [small["b_glu"], small["g_attn_out"], small["g_ssm_out"]], [(d_attn, f32), (d_ssm, f32), (d_ssm, bf16)],
        reds=[d_attn, d_ssm, d_ssm], dep=token)
    dz2 = mm_nt("d_glu_in", d_gl, big["w_glu"])
    token = put_g("w_glu", mm_tn("dw_glu", z_bf, d_gl, out_dtype=gbf))

    def k13(i, nt, yt, a, b):
        return ((a + b) * gelu_grad(yt),)
    dy = rowwise("ssm_gelu_grad", k13, L, [full(y), full(dz1), full(dz2)], [], [(d_ssm, f32)], dep=token)[0]
    du, dbd_re, dbd_im, dcd_re, dcd_im, dlam_re_l, dlam_im_l, dd_skip = ssm_bwd(
        dy, u_bf, proj, u_cb, s_re, s_im, bd_re, bd_im, cd_re, cd_im, lam_re_l, lam_im_l, small["d_skip"])
    dq, dkk_c, dkk_p, dvv_c, dvv_p, dsink = attn_bwd(qr, kk, vv, sink_b, attn, d_attn_o)
    d_proj = qkv_grad(dq, dkk_c, dkk_p, dvv_c, dvv_p, du, pos, inv_freq)
    d_xn = mm_nt("d_xn", d_proj, big["w_in"])
    token = put_g("w_in", mm_tn("dw_in", xn, d_proj, shards=N_CHIPS, out_dtype=gbf))

    def k17(i, nt, xt, dxn, dht, g):
        dx, dg = rms_bwd(xt, g, dxn)
        return dht + dx, dg
    grad_x, dg_pre_mix = rowwise("pre_mix_grad", k17, L, [full(x), full(d_xn), full(dh)], [small["g_pre_mix"]],
                                 [(D, f32)], reds=[D], dep=token)

    gather = spread.T
    dbbt_re = _block_diag_take(dbd_re, P, N).reshape(G * P, N)
    dbbt_im = _block_diag_take(dbd_im, P, N).reshape(G * P, N)
    d_a_re, d_a_im, d_logdt, dbt_re, dbt_im = ssm_params_grad(
        a_re, a_im, logdt_b, bt_re, bt_im, spread, gather,
        _state_layout_inv(dlam_re_l, G, N), _state_layout_inv(dlam_im_l, G, N), dbbt_re, dbbt_im)
    q_per_kv = d_attn // HEAD_DIM // N_KV_HEADS
    small_grads = {
        "g_pre_mix": dg_pre_mix, "sinks": dsink[:, :q_per_kv, 0].reshape(1, -1),
        "a_re": d_a_re, "a_im": d_a_im, "log_dt": d_logdt.reshape(1, G),
        "b_re": dbt_re.reshape(G, P, N).transpose(0, 2, 1), "b_im": dbt_im.reshape(G, P, N).transpose(0, 2, 1),
        "c_re": _block_diag_take(dcd_re, N, P).transpose(0, 2, 1), "c_im": _block_diag_take(dcd_im, N, P).transpose(0, 2, 1),
        "d_skip": dd_skip, "b_glu": db_glu, "g_attn_out": dg_attn, "g_ssm_out": dg_ssm,
        "g_post_mix": dg_post_mix, "g_pre_ffn": dg_pre_ffn, "g_post_ffn": dg_post_ffn,
    }
    return loss, grad_x, small_grads


WEIGHTS = ['g_pre_mix', 'w_in', 'sinks', 'a_re', 'a_im', 'log_dt', 'b_re', 'b_im', 'c_re', 'c_im', 'd_skip', 'w_glu', 'b_glu',
           'g_attn_out', 'g_ssm_out', 'w_o', 'g_post_mix', 'g_pre_ffn', 'w_gate', 'w_up', 'w_down', 'g_post_ffn']
BIG = ['w_in', 'w_glu', 'w_o', 'w_gate', 'w_up', 'w_down']
COL_SHARDED = ['w_in', 'w_gate', 'w_up']
SMALL = [n for n in WEIGHTS if n not in BIG]
GATHER_GROUPS = [["w_in"], ["w_glu", "w_o"], ["w_gate", "w_up"], ["w_down"]]
REDUCE_GROUPS = [["w_down", "w_gate", "w_up"], ["w_o", "w_glu", "w_in"]]


PACK_ROWS = 256


def _pack(parts):
    flat = jnp.concatenate([p.reshape(-1) for p in parts])
    pad = (-flat.size) % (PACK_ROWS * LANES)
    return jnp.pad(flat, (0, pad)).reshape(-1, LANES)


def _unpack(packed, shapes):
    flat = packed.reshape(-1)
    out, off = [], 0
    for s in shapes:
        n = int(np.prod(s))
        out.append(flat[off:off + n].reshape(s))
        off += n
    return out


def kernel(x, positions, g_pre_mix, w_in, sinks, a_re, a_im, log_dt, b_re, b_im, c_re, c_im, d_skip, w_glu, b_glu, g_attn_out, g_ssm_out, w_o, g_post_mix, g_pre_ffn, w_gate, w_up, w_down, g_post_ffn, loss_target, m_g_pre_mix, m_w_in, m_sinks, m_a_re, m_a_im, m_log_dt, m_b_re, m_b_im, m_c_re, m_c_im, m_d_skip, m_w_glu, m_b_glu, m_g_attn_out, m_g_ssm_out, m_w_o, m_g_post_mix, m_g_pre_ffn, m_w_gate, m_w_up, m_w_down, m_g_post_ffn, v_g_pre_mix, v_w_in, v_sinks, v_a_re, v_a_im, v_log_dt, v_b_re, v_b_im, v_c_re, v_c_im, v_d_skip, v_w_glu, v_b_glu, v_g_attn_out, v_g_ssm_out, v_w_o, v_g_post_mix, v_g_pre_ffn, v_w_gate, v_w_up, v_w_down, v_g_post_ffn):
    args = dict(locals())
    w = {n: args[n] for n in WEIGHTS}
    m = {n: args["m_" + n] for n in WEIGHTS}
    v = {n: args["v_" + n] for n in WEIGHTS}
    L, D = x.shape[1], x.shape[2]

    ax, ay, ac = _place()
    mine_arr = (2 * ax + ay).astype(jnp.int32).reshape(1)
    c_arr = ac.astype(jnp.int32).reshape(1)

    me_arr = (4 * ax + 2 * ay + ac).astype(jnp.int32).reshape(1)

    bufs = {"w_in": into_slot("cast_w_in", w["w_in"][0], mine_arr, N_CHIPS, bf16)}
    sems, (bufs["w_in"],), token = gather_start("gather_start_in", [bufs["w_in"]], [[0]], mine_arr)
    later = [n for n in BIG if n != "w_in"]
    for n in later:
        bufs[n] = into_slot("cast_" + n, w[n][0], mine_arr, N_CHIPS, bf16, dep=token)
    more, started, token = gather_start("gather_start_rest", [bufs[n] for n in later],
                                        [[later.index(n) for n in g] for g in GATHER_GROUPS[1:]], token)
    sems += more
    bufs.update(zip(later, started))
    ready = set()

    def get_w(n, after):
        if n not in ready:
            gi = [n in g for g in GATHER_GROUPS].index(True)
            members = GATHER_GROUPS[gi]
            landed = gather_wait("gather_wait_%d" % gi, [bufs[k] for k in members], *sems[gi], after)
            bufs.update(zip(members, gather_forward("gather_forward_%d" % gi, landed)))
            ready.update(members)
        g = bufs[n]
        return g if n in COL_SHARDED else g.reshape(g.shape[0] * g.shape[1], g.shape[2])

    swaps, inflight = {}, []

    def put_g(n, g):
        g3 = g if n in COL_SHARDED else g.reshape(N_CHIPS, g.shape[0] // N_CHIPS, g.shape[1])
        swaps[n] = sibling_start("swap_start_" + n, [g3], True)
        for gi, members in enumerate(REDUCE_GROUPS):
            if n == members[-1]:
                last = swaps[n][4]
                pair = []
                for k in members:
                    send, recv, srcs, lands, _ = swaps[k]
                    (src,), (got,) = sibling_wait("swap_wait_" + k, srcs, lands, send, recv, True, last)
                    pair.append(pair_sum("pair_sum_" + k, src, got, c_arr))
                send, recv, parts, lands, tok = scatter_start("scatter_start_%d" % gi, pair)
                inflight.append((members, send, recv, parts, lands))
                return tok
        return swaps[n][4]

    small = {n: w[n].reshape(1, -1) for n in SMALL}
    pos = positions.reshape(L, 1).astype(f32)
    d_in = N_CHIPS * w["w_in"].shape[2]
    loss, grad_x, small_grads = local_step(x[0], pos, loss_target[0], small, d_in, get_w, put_g, first_dep=token)
    loss = lax.psum(jnp.sum(loss), ("x", "y", "c"))

    shapes = [w[n].shape for n in SMALL]
    blocks = into_slot("small_block", _pack([small_grads[n] for n in SMALL]), me_arr, N_DEV, f32)
    small_send, small_recv, blocks, after = everyone_start("small_start", blocks)

    grads, delta, new_m, new_v = {}, {}, {}, {}
    for gi, (members, send, recv, parts, lands) in enumerate(inflight):
        parts, landed = scatter_wait("scatter_wait_%d" % gi, parts, lands, send, recv, after)
        joins, dep = [], None
        for k, p, t in zip(members, parts, landed):
            joins.append(sibling_start("join_start_" + k, [chip_sum("chip_sum_" + k, p, t, mine_arr, dep=dep)], False))
            dep = after = joins[-1][4]
        for n, (send, recv, srcs, lands, _) in zip(members, joins):
            (own,), (sib,) = sibling_wait("join_wait_" + n, srcs, lands, send, recv, False, after)
            g_, d_, m_, v_ = adamw_halves("adamw_" + n, w[n][0], own, sib, m[n][0], v[n][0])
            grads[n], delta[n], new_m[n], new_v[n] = g_[None], d_[None], m_[None], v_[None]
            after = v_
    blocks = everyone_wait("small_wait", blocks, small_send, small_recv, after)
    small_sum = sum_slots("small_sum", blocks)
    small_g = dict(zip(SMALL, _unpack(small_sum, shapes)))
    pw, pm, pv = (_pack([t[n] for n in SMALL]) for t in (w, m, v))
    d_, m_, v_ = adamw("adamw_small", pw, small_sum, pm, pv)
    for t, packed in ((delta, d_), (new_m, m_), (new_v, v_)):
        t.update(zip(SMALL, _unpack(packed, shapes)))
    grads.update(small_g)

    return (loss, grad_x[None], *[grads[n] for n in WEIGHTS], *[delta[n] for n in WEIGHTS],
            *[new_m[n] for n in WEIGHTS], *[new_v[n] for n in WEIGHTS])
```

```python
import functools
import math

import jax
import jax.numpy as jnp
import numpy as np
from jax import lax
from jax.experimental import pallas as pl
from jax.experimental.pallas import tpu as pltpu

f32 = jnp.float32
bf16 = jnp.bfloat16
HIGHEST = lax.Precision.HIGHEST
MESH = pl.DeviceIdType.MESH

HEAD_DIM = 64
N_KV_HEADS = 4
ATTN_BLOCK = 128
ROPE_THETA = 10000.0
SSM_GROUP = 16
SSM_STATE = 64
RMS_EPS = 1e-6
LANES = 128
SUBLANES = 8
VMEM_LIMIT = 52 * 1024 * 1024
N_CHIPS = 4
N_DEV = 8
NEG = -1e30

ADAM_LR, ADAM_B1, ADAM_B2, ADAM_EPS, ADAM_WD, ADAM_STEP = 0.001, 0.9, 0.999, 1e-08, 0.01, 10

NN = (((1,), (0,)), ((), ()))
NT = (((1,), (1,)), ((), ()))
TN = (((0,), (0,)), ((), ()))


def _params(*sem):
    return pltpu.CompilerParams(dimension_semantics=sem or None, vmem_limit_bytes=VMEM_LIMIT)


def _dot(a, b, dims=NN):
    return lax.dot_general(a, b, dims, preferred_element_type=f32)


def _pick(dim, pref):
    t = min(dim, pref)
    while dim % t:
        t -= LANES
    assert t > 0, (dim, pref)
    return t


ANY = pl.BlockSpec(memory_space=pl.ANY)


def _with_dep(in_specs, operands, dep):
    if dep is None:
        return list(in_specs), list(operands), 0
    return list(in_specs) + [ANY], list(operands) + [dep], 1


def _mm_call(name, grid, in_specs, out_spec, out_shape, acc_shape, dims, operands, dep=None):
    nk = grid[2]
    in_specs, operands, n_dep = _with_dep(in_specs, operands, dep)

    def body_one(a_ref, b_ref, *rest):
        o_ref = rest[n_dep]
        o_ref[...] = _dot(a_ref[...], b_ref[...], dims).astype(o_ref.dtype)

    def body(a_ref, b_ref, *rest):
        o_ref, acc_ref = rest[n_dep], rest[n_dep + 1]
        k = pl.program_id(2)

        @pl.when(k == 0)
        def _():
            acc_ref[...] = _dot(a_ref[...], b_ref[...], dims)

        @pl.when((k > 0) & (k < nk - 1))
        def _():
            acc_ref[...] += _dot(a_ref[...], b_ref[...], dims)

        @pl.when(k == nk - 1)
        def _():
            o_ref[...] = (acc_ref[...] + _dot(a_ref[...], b_ref[...], dims)).astype(o_ref.dtype)

    return pl.pallas_call(
        body_one if nk == 1 else body, out_shape=out_shape, grid=grid, in_specs=in_specs, out_specs=out_spec,
        scratch_shapes=[] if nk == 1 else [pltpu.VMEM(acc_shape, f32)], name=name,
        compiler_params=_params("parallel", "parallel", "arbitrary"))(*operands)


def mm_nt_pair(name, a1, b1, a2, b2, tm=1024, tk=1024, dep=None):
    M = a1.shape[0]
    S, K, n = b1.shape
    tm, tko = _pick(M, tm), _pick(K, tk)
    nk = 2 * S

    def body(a1_ref, b1_ref, a2_ref, b2_ref, *rest):
        o_ref, acc_ref = rest[-2], rest[-1]
        k = pl.program_id(2)

        @pl.when(k == 0)
        def _():
            acc_ref[...] = _dot(a1_ref[...], b1_ref[...], NT)

        @pl.when((k > 0) & (k < S))
        def _():
            acc_ref[...] += _dot(a1_ref[...], b1_ref[...], NT)

        @pl.when((k >= S) & (k < nk - 1))
        def _():
            acc_ref[...] += _dot(a2_ref[...], b2_ref[...], NT)

        @pl.when(k == nk - 1)
        def _():
            o_ref[...] = acc_ref[...] + _dot(a2_ref[...], b2_ref[...], NT)

    first = lambda k: jnp.minimum(k, S - 1)
    second = lambda k: jnp.maximum(k - S, 0)
    in_specs = [pl.BlockSpec((tm, n), lambda i, j, k: (i, first(k))), pl.BlockSpec((None, tko, n), lambda i, j, k: (first(k), j, 0)),
                pl.BlockSpec((tm, n), lambda i, j, k: (i, second(k))), pl.BlockSpec((None, tko, n), lambda i, j, k: (second(k), j, 0))]
    in_specs, operands, _ = _with_dep(in_specs, (a1, b1, a2, b2), dep)
    return pl.pallas_call(
        body, out_shape=jax.ShapeDtypeStruct((M, K), f32), grid=(M // tm, K // tko, nk), in_specs=in_specs,
        out_specs=pl.BlockSpec((tm, tko), lambda i, j, k: (i, j)), scratch_shapes=[pltpu.VMEM((tm, tko), f32)], name=name,
        compiler_params=_params("parallel", "parallel", "arbitrary"))(*operands)


def mm_nn(name, a, b, out_dtype=f32, tm=1024, tn=1024, tk=2048, dep=None):
    M, K = a.shape
    tm, tk = _pick(M, tm), _pick(K, tk)
    if b.ndim == 3:
        S, _, n = b.shape
        tn = _pick(n, 2048)
        per = n // tn
        b_spec = pl.BlockSpec((None, tk, tn), lambda i, j, k: (j // per, k, j % per))
        N = S * n
    else:
        N = b.shape[1]
        tn = _pick(N, tn)
        b_spec = pl.BlockSpec((tk, tn), lambda i, j, k: (k, j))
    grid = (M // tm, N // tn, K // tk)
    return _mm_call(name, grid, [pl.BlockSpec((tm, tk), lambda i, j, k: (i, k)), b_spec],
                    pl.BlockSpec((tm, tn), lambda i, j, k: (i, j)), jax.ShapeDtypeStruct((M, N), out_dtype),
                    (tm, tn), NN, (a, b), dep)


def mm_nt(name, a, b, out_dtype=f32, tm=1024, tn=2048, tk=1024, dep=None):
    M, N = a.shape
    tm = _pick(M, tm)
    if b.ndim == 3:
        S, K, n = b.shape
        tr = _pick(n, 2048)
        per = n // tr
        tko = _pick(K, tk)
        b_spec = pl.BlockSpec((None, tko, tr), lambda i, j, k: (k // per, j, k % per))
    else:
        K = b.shape[0]
        tr = _pick(N, tn)
        tko = _pick(K, tk)
        b_spec = pl.BlockSpec((tko, tr), lambda i, j, k: (j, k))
    grid = (M // tm, K // tko, N // tr)
    return _mm_call(name, grid, [pl.BlockSpec((tm, tr), lambda i, j, k: (i, k)), b_spec],
                    pl.BlockSpec((tm, tko), lambda i, j, k: (i, j)), jax.ShapeDtypeStruct((M, K), out_dtype),
                    (tm, tko), NT, (a, b), dep)


def mm_tn(name, a, b, shards=None, out_dtype=f32, tm=1024, tn=1024, tl=2048, dep=None):
    L, K = a.shape
    N = b.shape[1]
    tl, tko = _pick(L, tl), _pick(K, tm)
    if shards:
        n = N // shards
        tn = _pick(n, 2048)
        per = n // tn
        o_spec = pl.BlockSpec((None, tko, tn), lambda i, j, k: (j // per, i, j % per))
        o_shape = jax.ShapeDtypeStruct((shards, K, n), out_dtype)
    else:
        tn = _pick(N, tn)
        o_spec = pl.BlockSpec((tko, tn), lambda i, j, k: (i, j))
        o_shape = jax.ShapeDtypeStruct((K, N), out_dtype)
    grid = (K // tko, N // tn, L // tl)
    return _mm_call(name, grid, [pl.BlockSpec((tl, tko), lambda i, j, k: (k, i)),
                                 pl.BlockSpec((tl, tn), lambda i, j, k: (k, j))],
                    o_spec, o_shape, (tko, tn), TN, (a, b), dep)


def ffn_hidden(hn, w_gate, w_up, tm=512):
    M, K = hn.shape
    S, _, n = w_gate.shape
    tm = _pick(M, tm)

    def body(a_ref, g_ref, u_ref, gt_ref, up_ref, hid_ref):
        a = a_ref[...]
        g = _dot(a, g_ref[...])
        u = _dot(a, u_ref[...])
        gt_ref[...] = g.astype(bf16)
        up_ref[...] = u.astype(bf16)
        hid_ref[...] = (g * sigmoid(g) * u).astype(bf16)

    w_spec = pl.BlockSpec((None, K, n), lambda s, i: (s, 0, 0))
    o_spec = pl.BlockSpec((tm, n), lambda s, i: (i, s))
    o = jax.ShapeDtypeStruct((M, S * n), bf16)
    return pl.pallas_call(
        body, out_shape=[o, o, o], grid=(S, M // tm), in_specs=[pl.BlockSpec((tm, K), lambda s, i: (i, 0)), w_spec, w_spec],
        out_specs=[o_spec, o_spec, o_spec], name="ffn_hidden", compiler_params=_params("parallel", "parallel"))(hn, w_gate, w_up)


def ffn_hidden_grad(d_ff, w_down, gt, up, tm=512):
    M, D = d_ff.shape
    F = w_down.shape[0]
    n = _pick(F // N_CHIPS, 2048)
    tm = _pick(M, tm)

    def body(a_ref, b_ref, gt_ref, up_ref, dg_ref, du_ref):
        dh = _dot(a_ref[...], b_ref[...], NT)
        g = gt_ref[...].astype(f32)
        sg = sigmoid(g)
        dg_ref[...] = (dh * up_ref[...].astype(f32) * (sg * (1.0 + g * (1.0 - sg)))).astype(bf16)
        du_ref[...] = (dh * (g * sg)).astype(bf16)

    t_spec = pl.BlockSpec((tm, n), lambda j, i: (i, j))
    o = jax.ShapeDtypeStruct((M, F), bf16)
    return pl.pallas_call(
        body, out_shape=[o, o], grid=(F // n, M // tm),
        in_specs=[pl.BlockSpec((tm, D), lambda j, i: (i, 0)), pl.BlockSpec((n, D), lambda j, i: (j, 0)), t_spec, t_spec],
        out_specs=[t_spec, t_spec], name="ffn_hidden_grad", compiler_params=_params("parallel", "parallel"))(d_ff, w_down, gt, up)


def rowwise(name, fn, L, rows, bcast, outs, reds=(), tr=256, dep=None):
    tr = min(tr, L)
    nt = L // tr
    n_rows, n_b, n_o = len(rows), len(bcast), len(outs)
    n_dep = 0 if dep is None else 1

    def body(*refs):
        i = pl.program_id(0)
        ins = [r[...] for r in refs[:n_rows + n_b]]
        res = fn(i, nt, *ins)
        o_refs = refs[n_rows + n_b + n_dep:]
        for k in range(n_o):
            o_refs[k][...] = res[k].astype(o_refs[k].dtype)
        if reds:
            @pl.when(i == 0)
            def _():
                for k in range(len(reds)):
                    o_refs[n_o + k][...] = jnp.zeros_like(o_refs[n_o + k])
            for k in range(len(reds)):
                o_refs[n_o + k][...] += res[n_o + k]

    def row_spec(width, cb, shift):
        if shift:
            return pl.BlockSpec((tr, width), lambda i: (jnp.minimum(i + shift, nt - 1), cb))
        return pl.BlockSpec((tr, width), lambda i: (i, cb))

    in_specs = [row_spec(w, cb, sh) for (_, w, cb, sh) in rows]
    in_specs += [pl.BlockSpec(b.shape, lambda i: (0, 0)) for b in bcast]
    out_specs = [pl.BlockSpec((tr, w), lambda i: (i, 0)) for (w, _) in outs]
    out_specs += [pl.BlockSpec((1, w), lambda i: (0, 0)) for w in reds]
    out_shape = [jax.ShapeDtypeStruct((L, w), dt) for (w, dt) in outs]
    out_shape += [jax.ShapeDtypeStruct((1, w), f32) for w in reds]
    in_specs, operands, _ = _with_dep(in_specs, [r[0] for r in rows] + list(bcast), dep)
    return pl.pallas_call(
        body, out_shape=out_shape, grid=(nt,), in_specs=in_specs, out_specs=out_specs, name=name,
        compiler_params=_params("arbitrary"))(*operands)


def full(a):
    return (a, a.shape[1], 0, 0)


def colsum(v):
    return jnp.sum(v, axis=0, keepdims=True)


def rms_fwd(x, g):
    r = lax.rsqrt(jnp.mean(x * x, axis=-1, keepdims=True) + RMS_EPS)
    return x * r * g


def rms_bwd(x, g, dy):
    r = lax.rsqrt(jnp.mean(x * x, axis=-1, keepdims=True) + RMS_EPS)
    xh = x * r
    dyg = dy * g
    dx = r * (dyg - xh * jnp.mean(dyg * xh, axis=-1, keepdims=True))
    return dx, colsum(dy * xh)


GELU_C = math.sqrt(2.0 / math.pi)


def gelu(y):
    return y * (0.5 * (1.0 + jnp.tanh(GELU_C * (y + 0.044715 * (y * y * y)))))


def gelu_grad(y):
    t = jnp.tanh(GELU_C * (y + 0.044715 * (y * y * y)))
    return 0.5 * (1.0 + t) + 0.5 * y * (1.0 - t * t) * (GELU_C * (1.0 + 3 * 0.044715 * (y * y)))


def sigmoid(v):
    return 1.0 / (1.0 + jnp.exp(-v))


def _lane(shape):
    return lax.broadcasted_iota(jnp.int32, shape, 1)


def _rot_chunk(t, cos, sin_signed):
    first = (_lane(t.shape) % HEAD_DIM) < (HEAD_DIM // 2)
    partner = jnp.where(first, pltpu.roll(t, LANES - HEAD_DIM // 2, 1), pltpu.roll(t, HEAD_DIM // 2, 1))
    return t * cos + partner * sin_signed


def _cos_sin(pos, inv_freq, inverse):
    ang = pos * inv_freq
    cos, sin = jnp.cos(ang), jnp.sin(ang)
    first = (_lane(ang.shape) % HEAD_DIM) < (HEAD_DIM // 2)
    sign = jnp.where(first, -1.0, 1.0) * (-1.0 if inverse else 1.0)
    return cos, sin * sign


def _dup_head(chunk, odd):
    low = _lane(chunk.shape) < HEAD_DIM
    x = jnp.where(low != odd, chunk, 0.0)
    return x + pltpu.roll(x, HEAD_DIM, 1)


def _chunks(v):
    return [v[:, LANES * c:LANES * (c + 1)] for c in range(v.shape[1] // LANES)]


def qkv_prep(proj, pos, inv_freq, d_attn, d_kv):
    L = proj.shape[0]
    d_ssm = proj.shape[1] - d_attn - 2 * d_kv
    half = d_ssm // 2
    scale = 1.0 / math.sqrt(HEAD_DIM)

    def fn(i, nt, q, k, v, u0, u1, p, invf):
        cos, sin = _cos_sin(p, invf, False)
        qr = jnp.concatenate([_rot_chunk(c, cos, sin) for c in _chunks(q)], axis=1) * scale
        kr = [_rot_chunk(c, cos, sin) for c in _chunks(k)]
        kk = jnp.concatenate([_dup_head(c, odd) for c in kr for odd in (False, True)], axis=1)
        vv = jnp.concatenate([_dup_head(c, odd) for c in _chunks(v) for odd in (False, True)], axis=1)
        return qr, kk, vv, jnp.concatenate([u0, u1], axis=1)

    u_cb = (d_attn + 2 * d_kv) // half
    return rowwise("qkv_prep", fn, L,
                   [(proj, d_attn, 0, 0), (proj, d_kv, d_attn // d_kv, 0), (proj, d_kv, d_attn // d_kv + 1, 0),
                    (proj, half, u_cb, 0), (proj, half, u_cb + 1, 0), full(pos)],
                   [inv_freq], [(d_attn, bf16), (2 * d_kv, bf16), (2 * d_kv, bf16), (d_ssm, bf16)])


def qkv_grad(dq, dkk_c, dkk_p, dvv_c, dvv_p, du, pos, inv_freq):
    L, d_attn = dq.shape
    d_kv = dkk_c.shape[1] // 2
    scale = 1.0 / math.sqrt(HEAD_DIM)

    def fold(cur, prev, i, nt):
        t = cur + jnp.where(i < nt - 1, prev, 0.0)
        out = []
        for c in range(t.shape[1] // (2 * LANES)):
            even, odd = t[:, 2 * c * LANES:(2 * c + 1) * LANES], t[:, (2 * c + 1) * LANES:(2 * c + 2) * LANES]
            even, odd = even + pltpu.roll(even, HEAD_DIM, 1), odd + pltpu.roll(odd, HEAD_DIM, 1)
            out.append(jnp.where(_lane(even.shape) < HEAD_DIM, even, odd))
        return out

    def fn(i, nt, dq_t, kc, kp, vc, vp, du_t, p, invf):
        cos, sin = _cos_sin(p, invf, True)
        dq_o = jnp.concatenate([_rot_chunk(c, cos, sin) for c in _chunks(dq_t)], axis=1) * scale
        dk_o = jnp.concatenate([_rot_chunk(c, cos, sin) for c in fold(kc, kp, i, nt)], axis=1)
        dv_o = jnp.concatenate(fold(vc, vp, i, nt), axis=1)
        return (jnp.concatenate([dq_o, dk_o, dv_o, du_t], axis=1),)

    return rowwise("qkv_grad", fn, L,
                   [full(dq), full(dkk_c), (dkk_p, 2 * d_kv, 0, 1), full(dvv_c), (dvv_p, 2 * d_kv, 0, 1), full(du), full(pos)],
                   [inv_freq], [(d_attn + 2 * d_kv + du.shape[1], bf16)], tr=ATTN_BLOCK)[0]


def _attn_specs(L):
    nb = L // ATTN_BLOCK
    B = ATTN_BLOCK
    q_spec = pl.BlockSpec((B, 2 * LANES), lambda h, n: (n, h))
    cur = pl.BlockSpec((B, LANES), lambda h, n: (n, h))
    prev = pl.BlockSpec((B, LANES), lambda h, n: (jnp.maximum(n - 1, 0), h))
    return nb, q_spec, cur, prev


def _attn_mask(n):
    B = ATTN_BLOCK
    row = lax.broadcasted_iota(jnp.int32, (B, 2 * B), 0)
    col = lax.broadcasted_iota(jnp.int32, (B, 2 * B), 1)
    return ((col < B) & (col > row) & (n > 0)) | ((col >= B) & (row >= col - B))


def _attn_probs(qm, kcat, sink, mask):
    s = jnp.where(mask, _dot(qm, kcat, NT), NEG)
    m = jnp.maximum(jnp.max(s, axis=1, keepdims=True), sink)
    p, ps = jnp.exp(s - m), jnp.exp(sink - m)
    inv = 1.0 / (jnp.sum(p, axis=1, keepdims=True) + ps)
    return p, inv, ps


def _attn_heads(q_ref, s_ref, h, q_per_kv):
    low = _lane((ATTN_BLOCK, LANES)) < HEAD_DIM
    heads = []
    for pr in range(q_per_kv // 2):
        q2 = q_ref[:, LANES * pr:LANES * (pr + 1)]
        for odd in (False, True):
            mine = low != odd
            sink = jnp.max(s_ref[pl.ds(h * q_per_kv + 2 * pr + int(odd), 1), :], axis=1, keepdims=True)
            heads.append((pr, mine, jnp.where(mine, q2, jnp.zeros_like(q2)), sink))
    return low, heads


def attn_fwd(qr, kk, vv, sink_b):
    L, d_attn = qr.shape
    nb, q_spec, cur, prev = _attn_specs(L)
    n_kv = kk.shape[1] // LANES
    q_per_kv = d_attn // HEAD_DIM // n_kv

    def body(q_ref, kc_ref, kp_ref, vc_ref, vp_ref, s_ref, o_ref):
        h, n = pl.program_id(0), pl.program_id(1)
        kcat = jnp.concatenate([kp_ref[...], kc_ref[...]], axis=0)
        vcat = jnp.concatenate([vp_ref[...], vc_ref[...]], axis=0)
        mask = _attn_mask(n)
        low, heads = _attn_heads(q_ref, s_ref, h, q_per_kv)
        probs = [_attn_probs(qm, kcat, sink, mask) for (_, _, qm, sink) in heads]
        outs = [_dot(p.astype(bf16), vcat) * inv for (p, inv, _) in probs]
        for pr in range(q_per_kv // 2):
            o_ref[:, LANES * pr:LANES * (pr + 1)] = jnp.where(low, outs[2 * pr], outs[2 * pr + 1])

    return pl.pallas_call(
        body, out_shape=jax.ShapeDtypeStruct((L, d_attn), f32), grid=(n_kv, nb),
        in_specs=[q_spec, cur, prev, cur, prev, pl.BlockSpec(sink_b.shape, lambda h, n: (0, 0))],
        out_specs=q_spec, name="attn_fwd", compiler_params=_params("parallel", "arbitrary"))(qr, kk, kk, vv, vv, sink_b)


def attn_bwd(qr, kk, vv, sink_b, attn, d_attn_out):
    L, d_attn = qr.shape
    nb, q_spec, cur, prev = _attn_specs(L)
    n_kv = kk.shape[1] // LANES
    q_per_kv = d_attn // HEAD_DIM // n_kv

    def body(q_ref, kc_ref, kp_ref, vc_ref, vp_ref, s_ref, o_ref, do_ref, dq_ref, dkc_ref, dkp_ref, dvc_ref, dvp_ref, ds_ref):
        h, n = pl.program_id(0), pl.program_id(1)
        B = ATTN_BLOCK
        kcat = jnp.concatenate([kp_ref[...], kc_ref[...]], axis=0)
        vcat = jnp.concatenate([vp_ref[...], vc_ref[...]], axis=0)
        mask = _attn_mask(n)
        low, heads = _attn_heads(q_ref, s_ref, h, q_per_kv)
        probs = [_attn_probs(qm, kcat, sink, mask) for (_, _, qm, sink) in heads]
        dk = jnp.zeros((2 * B, LANES), f32)
        dv = dk
        srow = lax.broadcasted_iota(jnp.int32, (SUBLANES, LANES), 0)
        dsink = jnp.zeros((SUBLANES, LANES), f32)
        dqs = []
        for i, ((pr, mine, qm, _), (p, inv, ps)) in enumerate(zip(heads, probs)):
            do2 = do_ref[:, LANES * pr:LANES * (pr + 1)]
            delta = jnp.sum(jnp.where(mine, do2 * o_ref[:, LANES * pr:LANES * (pr + 1)], 0.0), axis=1, keepdims=True)
            dob = jnp.where(mine, do2, 0.0).astype(bf16)
            p = p * inv
            ds = (p * (_dot(dob, vcat, NT) - delta)).astype(bf16)
            dqs.append(_dot(ds, kcat))
            dk = dk + _dot(ds, qm, TN)
            dv = dv + _dot(p.astype(bf16), dob, TN)
            dsink = dsink + jnp.where(srow == i, -jnp.sum(ps * inv * delta), 0.0)
        for pr in range(q_per_kv // 2):
            dq_ref[:, LANES * pr:LANES * (pr + 1)] = jnp.where(low, dqs[2 * pr], dqs[2 * pr + 1])
        dkp_ref[...] = dk[:B]
        dkc_ref[...] = dk[B:]
        dvp_ref[...] = dv[:B]
        dvc_ref[...] = dv[B:]

        @pl.when(n == 0)
        def _():
            ds_ref[...] = jnp.zeros_like(ds_ref)

        ds_ref[...] += dsink

    kv_shape = jax.ShapeDtypeStruct(kk.shape, f32)
    return pl.pallas_call(
        body,
        out_shape=[jax.ShapeDtypeStruct((L, d_attn), f32), kv_shape, kv_shape, kv_shape, kv_shape,
                   jax.ShapeDtypeStruct((n_kv, SUBLANES, LANES), f32)],
        grid=(n_kv, nb),
        in_specs=[q_spec, cur, prev, cur, prev, pl.BlockSpec(sink_b.shape, lambda h, n: (0, 0)), q_spec, q_spec],
        out_specs=[q_spec, cur, cur, cur, cur, pl.BlockSpec((None, SUBLANES, LANES), lambda h, n: (h, 0, 0))],
        name="attn_bwd", compiler_params=_params("parallel", "arbitrary"))(qr, kk, kk, vv, vv, sink_b, attn, d_attn_out)


SSM_T = 128
NQ = SUBLANES * SSM_STATE // LANES
NJ = SUBLANES


def _strided_put(ref, j, val):
    for q in range(NQ):
        ref.at[q][pl.ds(j, SSM_T, stride=NJ), :] = val[:, LANES * q:LANES * (q + 1)]


def _strided_get(ref, j):
    return jnp.concatenate([ref.at[q][pl.ds(j, SSM_T, stride=NJ), :] for q in range(NQ)], axis=1)


def _ssm_specs(L, rev):
    nt = L // SSM_T
    idx = (lambda i: nt - 1 - i) if rev else (lambda i: i)
    row = lambda w, cb=0: pl.BlockSpec((SSM_T, w), lambda i: (idx(i), cb))
    state = pl.BlockSpec((NQ, SSM_T * NJ, LANES), lambda i: (0, idx(i), 0))
    whole = lambda a: pl.BlockSpec(a.shape, lambda i: (0,) * a.ndim)
    return nt, row, state, whole


def ssm_fwd(u_bf, proj, u_cb, bd_re, bd_im, cd_re, cd_im, lam_re, lam_im, d_skip):
    L, d_ssm = u_bf.shape
    nt, row, state, whole = _ssm_specs(L, False)
    half = d_ssm // 2
    gw = d_ssm // NJ

    def body(u_ref, u0_ref, u1_ref, bdr, bdi, cdr, cdi, lr_ref, li_ref, d_ref, y_ref, z_ref, sr_ref, si_ref, carry):
        i = pl.program_id(0)

        @pl.when(i == 0)
        def _():
            carry[...] = jnp.zeros_like(carry)

        for j in range(NJ):
            uj = u_ref[:, gw * j:gw * (j + 1)]
            _strided_put(sr_ref, j, _dot(uj, bdr[j]))
            _strided_put(si_ref, j, _dot(uj, bdi[j]))
        lr = [lr_ref[q] for q in range(NQ)]
        li = [li_ref[q] for q in range(NQ)]

        def step(t, s):
            sr, si = s
            rows = pl.ds(pl.multiple_of(t * NJ, NJ), NJ)
            nr = tuple(lr[q] * sr[q] - li[q] * si[q] + sr_ref[q, rows, :] for q in range(NQ))
            ni = tuple(lr[q] * si[q] + li[q] * sr[q] + si_ref[q, rows, :] for q in range(NQ))
            for q in range(NQ):
                sr_ref[q, rows, :] = nr[q]
                si_ref[q, rows, :] = ni[q]
            return nr, ni

        init = (tuple(carry[0, q] for q in range(NQ)), tuple(carry[1, q] for q in range(NQ)))
        sr, si = lax.fori_loop(0, SSM_T, step, init, unroll=8)
        for q in range(NQ):
            carry[0, q] = sr[q]
            carry[1, q] = si[q]
        uf = jnp.concatenate([u0_ref[...], u1_ref[...]], axis=1)
        for j in range(NJ):
            cols = slice(gw * j, gw * (j + 1))
            yj = _dot(_strided_get(sr_ref, j).astype(bf16), cdr[j]) - _dot(_strided_get(si_ref, j).astype(bf16), cdi[j])
            yj = yj + d_ref[:, cols] * uf[:, cols]
            y_ref[:, cols] = yj
            z_ref[:, cols] = gelu(yj).astype(bf16)

    s_shape = jax.ShapeDtypeStruct((NQ, L * NJ, LANES), f32)
    consts = (bd_re, bd_im, cd_re, cd_im, lam_re, lam_im, d_skip)
    return pl.pallas_call(
        body, out_shape=[jax.ShapeDtypeStruct((L, d_ssm), f32), jax.ShapeDtypeStruct((L, d_ssm), bf16), s_shape, s_shape], grid=(nt,),
        in_specs=[row(d_ssm), row(half, u_cb), row(half, u_cb + 1)] + [whole(a) for a in consts],
        out_specs=[row(d_ssm), row(d_ssm), state, state],
        scratch_shapes=[pltpu.VMEM((2, NQ, NJ, LANES), f32)], name="ssm_fwd",
        compiler_params=_params("arbitrary"))(u_bf, proj, proj, *consts)


def ssm_bwd(y, dz1, dz2, u_bf, proj, u_cb, s_re, s_im, bd_re, bd_im, cd_re, cd_im, lam_re, lam_im, d_skip, dep=None):
    L, d_ssm = y.shape
    nt, row, state, whole = _ssm_specs(L, True)
    half = d_ssm // 2
    gw = d_ssm // NJ
    n_dep = 0 if dep is None else 1

    def body(y_ref, dz1_ref, dz2_ref, u_ref, u0_ref, u1_ref, sr_ref, si_ref, bdr, bdi, cdr, cdi, lr_ref, li_ref, d_ref, *rest):
        du_ref, dbdr, dbdi, dcdr, dcdi, dlr, dli, dd_ref, gr_ref, gi_ref, carry = rest[n_dep:]
        i = pl.program_id(0)

        @pl.when(i == 0)
        def _():
            carry[...] = jnp.zeros_like(carry)
            for r in (dbdr, dbdi, dcdr, dcdi, dlr, dli, dd_ref):
                r[...] = jnp.zeros_like(r)

        dyf = (dz1_ref[...] + dz2_ref[...]) * gelu_grad(y_ref[...])
        dyb = dyf.astype(bf16)
        for j in range(NJ):
            dyj = dyb[:, gw * j:gw * (j + 1)]
            _strided_put(gr_ref, j, _dot(dyj, cdr[j], NT))
            _strided_put(gi_ref, j, -_dot(dyj, cdi[j], NT))
            dcdr[j] += _dot(_strided_get(sr_ref, j).astype(bf16), dyj, TN)
            dcdi[j] -= _dot(_strided_get(si_ref, j).astype(bf16), dyj, TN)
        lr = [lr_ref[q] for q in range(NQ)]
        li = [li_ref[q] for q in range(NQ)]

        def step(k, c):
            gr, gi, ar, ai = c
            rows = pl.ds(pl.multiple_of((SSM_T - 1 - k) * NJ, NJ), NJ)
            s_r = [sr_ref[q, rows, :] for q in range(NQ)]
            s_i = [si_ref[q, rows, :] for q in range(NQ)]
            ar = tuple(ar[q] + gr[q] * s_r[q] + gi[q] * s_i[q] for q in range(NQ))
            ai = tuple(ai[q] + gi[q] * s_r[q] - gr[q] * s_i[q] for q in range(NQ))
            nr = tuple(gr_ref[q, rows, :] + lr[q] * gr[q] + li[q] * gi[q] for q in range(NQ))
            ni = tuple(gi_ref[q, rows, :] + lr[q] * gi[q] - li[q] * gr[q] for q in range(NQ))
            for q in range(NQ):
                gr_ref[q, rows, :] = nr[q]
                gi_ref[q, rows, :] = ni[q]
            return nr, ni, ar, ai

        zero = tuple(jnp.zeros((NJ, LANES), f32) for _ in range(NQ))
        init = (tuple(carry[0, q] for q in range(NQ)), tuple(carry[1, q] for q in range(NQ)), zero, zero)
        gr, gi, ar, ai = lax.fori_loop(0, SSM_T, step, init, unroll=8)
        for q in range(NQ):
            carry[0, q] = gr[q]
            carry[1, q] = gi[q]
            dlr[q] += ar[q]
            dli[q] += ai[q]
        uf = jnp.concatenate([u0_ref[...], u1_ref[...]], axis=1)
        dd_ref[...] += colsum(dyf * uf)
        for j in range(NJ):
            cols = slice(gw * j, gw * (j + 1))
            gjr, gji = _strided_get(gr_ref, j).astype(bf16), _strided_get(gi_ref, j).astype(bf16)
            du_ref[:, cols] = _dot(gjr, bdr[j], NT) + _dot(gji, bdi[j], NT) + d_ref[:, cols] * dyf[:, cols]
            uj = u_ref[:, cols]
            dbdr[j] += _dot(uj, gjr, TN)
            dbdi[j] += _dot(uj, gji, TN)

    consts = (bd_re, bd_im, cd_re, cd_im, lam_re, lam_im, d_skip)
    acc = lambda a: jax.ShapeDtypeStruct(a.shape, f32)
    outs = [jax.ShapeDtypeStruct((L, d_ssm), f32), acc(bd_re), acc(bd_im), acc(cd_re), acc(cd_im), acc(lam_re), acc(lam_im), acc(d_skip)]
    in_specs, operands, _ = _with_dep(
        [row(d_ssm)] * 4 + [row(half, u_cb), row(half, u_cb + 1), state, state] + [whole(a) for a in consts],
        [y, dz1, dz2, u_bf, proj, proj, s_re, s_im, *consts], dep)
    return pl.pallas_call(
        body, out_shape=outs, grid=(nt,),
        in_specs=in_specs, out_specs=[row(d_ssm)] + [whole(a) for a in consts],
        scratch_shapes=[pltpu.VMEM((NQ, SSM_T * NJ, LANES), f32), pltpu.VMEM((NQ, SSM_T * NJ, LANES), f32),
                        pltpu.VMEM((2, NQ, NJ, LANES), f32)],
        name="ssm_bwd", compiler_params=_params("arbitrary"))(*operands)


def _cmul(ar, ai, br, bi):
    return ar * br - ai * bi, ar * bi + ai * br


def _disc(ar, ai, logdt):
    dt = jnp.exp(logdt)
    mag = jnp.exp(ar * dt)
    lr, li = mag * jnp.cos(ai * dt), mag * jnp.sin(ai * dt)
    den = ar * ar + ai * ai
    nr, ni = lr - 1.0, li
    fr, fi = (nr * ar + ni * ai) / den, (ni * ar - nr * ai) / den
    return dt, lr, li, den, fr, fi


def ssm_params(a_re, a_im, logdt_b, bt_re, bt_im, spread):
    def body(ar_ref, ai_ref, ld_ref, br_ref, bi_ref, sp_ref, lr_ref, li_ref, or_ref, oi_ref):
        _, lr, li, _, fr, fi = _disc(ar_ref[...], ai_ref[...], ld_ref[...])
        lr_ref[...] = lr
        li_ref[...] = li
        fre = jnp.dot(sp_ref[...], fr, precision=HIGHEST, preferred_element_type=f32)
        fie = jnp.dot(sp_ref[...], fi, precision=HIGHEST, preferred_element_type=f32)
        o_r, o_i = _cmul(fre, fie, br_ref[...], bi_ref[...])
        or_ref[...] = o_r
        oi_ref[...] = o_i

    g = jax.ShapeDtypeStruct(a_re.shape, f32)
    b = jax.ShapeDtypeStruct(bt_re.shape, f32)
    return pl.pallas_call(body, out_shape=[g, g, b, b], name="ssm_params",
                          compiler_params=_params())(a_re, a_im, logdt_b, bt_re, bt_im, spread)


def ssm_params_grad(a_re, a_im, logdt_b, bt_re, bt_im, spread, gather, dlam_re, dlam_im, dbt_re, dbt_im):
    def body(ar_ref, ai_ref, ld_ref, br_ref, bi_ref, sp_ref, ga_ref, glr_ref, gli_ref, gbr_ref, gbi_ref,
             dar_ref, dai_ref, dld_ref, dbr_ref, dbi_ref):
        ar, ai = ar_ref[...], ai_ref[...]
        dt, lr, li, den, fr, fi = _disc(ar, ai, ld_ref[...])
        hdot = functools.partial(jnp.dot, precision=HIGHEST, preferred_element_type=f32)
        fre, fie = hdot(sp_ref[...], fr), hdot(sp_ref[...], fi)
        gbr, gbi, br, bi = gbr_ref[...], gbi_ref[...], br_ref[...], bi_ref[...]
        dbr_ref[...], dbi_ref[...] = _cmul(fre, -fie, gbr, gbi)
        t_r, t_i = _cmul(br, -bi, gbr, gbi)
        gfr, gfi = hdot(ga_ref[...], t_r), hdot(ga_ref[...], t_i)
        iwr, iwi = ar / den, -ai / den
        x_r, x_i = _cmul(iwr, -iwi, gfr, gfi)
        glr, gli = glr_ref[...] + x_r, gli_ref[...] + x_i
        q_r, q_i = _cmul(fr, fi, iwr, iwi)
        gwr, gwi = _cmul(-q_r, q_i, gfr, gfi)
        y_r, y_i = _cmul(dt * lr, -dt * li, glr, gli)
        dar_ref[...] = gwr + y_r
        dai_ref[...] = gwi + y_i
        wl_r, wl_i = _cmul(ar, ai, lr, li)
        z_r, _ = _cmul(wl_r, -wl_i, glr, gli)
        dld_ref[...] = jnp.sum(z_r * dt, axis=1, keepdims=True)

    g = jax.ShapeDtypeStruct(a_re.shape, f32)
    b = jax.ShapeDtypeStruct(bt_re.shape, f32)
    return pl.pallas_call(body, out_shape=[g, g, jax.ShapeDtypeStruct((a_re.shape[0], 1), f32), b, b], name="ssm_params_grad",
                          compiler_params=_params())(a_re, a_im, logdt_b, bt_re, bt_im, spread, gather, dlam_re, dlam_im, dbt_re, dbt_im)


def _block_diag(t, rows, cols):
    G = t.shape[0]
    t = t.reshape(G // NJ, NJ, rows, cols)
    eye = jnp.eye(NJ, dtype=t.dtype)
    return jnp.einsum('jgrc,gh->jgrhc', t, eye).reshape(G // NJ, NJ * rows, NJ * cols)


def _block_diag_take(m, rows, cols):
    J = m.shape[0]
    m = m.reshape(J, NJ, rows, NJ, cols)
    idx = jnp.arange(NJ)
    return m[:, idx, :, idx, :].transpose(1, 0, 2, 3).reshape(J * NJ, rows, cols)


def _state_layout(t):
    return t.reshape(NJ, NQ, LANES).transpose(1, 0, 2)


def _state_layout_inv(t, G, N):
    return t.transpose(1, 0, 2).reshape(G, N)


def _tiles2d(shape, budget_rows=128):
    rows, cols = shape
    tr = rows
    if rows > budget_rows:
        tr = budget_rows
        while rows % tr:
            tr -= SUBLANES
    return tr, cols


def _adam_update(w, g, m, v):
    c1 = 1.0 - ADAM_B1 ** ADAM_STEP
    c2 = 1.0 - ADAM_B2 ** ADAM_STEP
    nm = ADAM_B1 * m + (1.0 - ADAM_B1) * g
    nv = ADAM_B2 * v + (1.0 - ADAM_B2) * (g * g)
    delta = -ADAM_LR * ((nm / c1) / (jnp.sqrt(nv / c2) + ADAM_EPS) + ADAM_WD * w)
    return delta, nm, nv


def adamw(name, w, g, m, v):
    tr, cols = _tiles2d(w.shape, 128 if w.shape[1] > 1024 else 256)

    def body(w_ref, g_ref, m_ref, v_ref, d_ref, nm_ref, nv_ref):
        d_ref[...], nm_ref[...], nv_ref[...] = _adam_update(w_ref[...], g_ref[...], m_ref[...], v_ref[...])

    spec = pl.BlockSpec((tr, cols), lambda i: (i, 0))
    o = jax.ShapeDtypeStruct(w.shape, f32)
    return pl.pallas_call(body, out_shape=[o, o, o], grid=(w.shape[0] // tr,), in_specs=[spec] * 4, out_specs=[spec] * 3,
                          name=name, compiler_params=_params("parallel"))(w, g, m, v)


def adamw_halves(name, w, own, got, m, v, c_arr):
    h, cols = own.shape
    tr, _ = _tiles2d((h, cols), 128 if cols > 1024 else 256)
    nh = h // tr

    def body(c_ref, w_ref, own_ref, got_ref, m_ref, v_ref, g_ref, d_ref, nm_ref, nv_ref):
        mine = (pl.program_id(0) // nh) == c_ref[0]
        g = jnp.where(mine, own_ref[...], got_ref[...])
        g_ref[...] = g
        d_ref[...], nm_ref[...], nv_ref[...] = _adam_update(w_ref[...], g, m_ref[...], v_ref[...])

    spec = pl.BlockSpec((tr, cols), lambda i, c: (i, 0))
    own_spec = pl.BlockSpec((tr, cols), lambda i, c: (jnp.where(i // nh == c[0], i % nh, 0), 0))
    got_spec = pl.BlockSpec((tr, cols), lambda i, c: (jnp.where(i // nh == c[0], 0, i % nh), 0))
    o = jax.ShapeDtypeStruct(w.shape, f32)
    grid_spec = pltpu.PrefetchScalarGridSpec(num_scalar_prefetch=1, grid=(2 * nh,),
                                             in_specs=[spec, own_spec, got_spec, spec, spec], out_specs=[spec] * 4)
    return pl.pallas_call(body, out_shape=[o, o, o, o], grid_spec=grid_spec, name=name,
                          compiler_params=_params("arbitrary"))(c_arr, w, own, got, m, v)


def pair_sum(name, g, got, c_arr):
    S, h, cols = got.shape
    tr, _ = _tiles2d((h, cols), 1024)
    nh = h // tr

    def body(c_ref, g_ref, o_ref, out_ref):
        out_ref[...] = (g_ref[...].astype(f32) + o_ref[...].astype(f32)).astype(out_ref.dtype)

    spec = pl.BlockSpec((None, tr, cols), lambda s, i, c: (s, i, 0))
    grid_spec = pltpu.PrefetchScalarGridSpec(
        num_scalar_prefetch=1, grid=(S, nh),
        in_specs=[pl.BlockSpec((None, tr, cols), lambda s, i, c: (s, c[0] * nh + i, 0)), spec], out_specs=spec)
    return pl.pallas_call(body, out_shape=jax.ShapeDtypeStruct(got.shape, g.dtype), grid_spec=grid_spec, name=name,
                          compiler_params=_params("parallel", "parallel"))(c_arr, g, got)


def chip_sum(name, pair, landed, mine_arr, dep=None):
    n_in, h, cols = landed.shape
    tr, _ = _tiles2d((h, cols), 512)

    def body(s_ref, p_ref, l_ref, *rest):
        acc = p_ref[...].astype(f32)
        for k in range(n_in):
            acc = acc + l_ref[k].astype(f32)
        rest[-1][...] = acc

    in_specs, operands, _ = _with_dep(
        [pl.BlockSpec((None, tr, cols), lambda i, s: (s[0], i, 0)), pl.BlockSpec((n_in, tr, cols), lambda i, s: (0, i, 0))],
        [pair, landed], dep)
    grid_spec = pltpu.PrefetchScalarGridSpec(num_scalar_prefetch=1, grid=(h // tr,), in_specs=in_specs,
                                             out_specs=pl.BlockSpec((tr, cols), lambda i, s: (i, 0)))
    return pl.pallas_call(body, out_shape=jax.ShapeDtypeStruct((h, cols), f32), grid_spec=grid_spec, name=name,
                          compiler_params=_params("parallel"))(mine_arr, *operands)


def into_slot(name, w, slot_arr, n_slots, dtype, dep=None):
    tr, cols = _tiles2d(w.shape, 256)

    def body(s_ref, w_ref, *rest):
        rest[-1][...] = w_ref[...].astype(dtype)

    in_specs, operands, _ = _with_dep([pl.BlockSpec((tr, cols), lambda i, s: (i, 0))], [w], dep)
    grid_spec = pltpu.PrefetchScalarGridSpec(num_scalar_prefetch=1, grid=(w.shape[0] // tr,), in_specs=in_specs,
                                             out_specs=pl.BlockSpec((None, tr, cols), lambda i, s: (s[0], i, 0)))
    return pl.pallas_call(body, out_shape=jax.ShapeDtypeStruct((n_slots,) + w.shape, dtype), grid_spec=grid_spec, name=name,
                          compiler_params=_params("parallel"))(slot_arr, *operands)


def sum_slots(name, t):
    S, rows, cols = t.shape
    tr, _ = _tiles2d((rows, cols), 256)

    def body(t_ref, o_ref):
        acc = t_ref[0]
        for s in range(1, S):
            acc = acc + t_ref[s]
        o_ref[...] = acc

    return pl.pallas_call(body, out_shape=jax.ShapeDtypeStruct((rows, cols), f32), grid=(rows // tr,),
                          in_specs=[pl.BlockSpec((S, tr, cols), lambda i: (0, i, 0))], out_specs=pl.BlockSpec((tr, cols), lambda i: (i, 0)),
                          name=name, compiler_params=_params("parallel"))(t)


ANY = pl.BlockSpec(memory_space=pl.ANY)


def _place():
    x, y, c = lax.axis_index("x"), lax.axis_index("y"), lax.axis_index("c")
    return x, y, c


def _other_chips(x, y):
    return [(1 - x, y, 2 * (1 - x) + y), (x, 1 - y, 2 * x + 1 - y), (1 - x, 1 - y, 2 * (1 - x) + 1 - y)]


def gather_weights(bufs):
    nw = len(bufs)

    def body(*refs):
        dst = refs[nw:2 * nw]
        send1, recv1, send2, recv2 = refs[2 * nw:]
        x, y, c = _place()
        mine = 2 * x + y
        chips = _other_chips(x, y)
        sib = (x, y, 1 - c)
        first, passed = [], []
        for w in range(nw):
            h = dst[w].shape[1] // 2
            for k, (px, py, _) in enumerate(chips):
                half = dst[w].at[mine, pl.ds(c * h, h), :]
                cp = pltpu.make_async_remote_copy(src_ref=half, dst_ref=half,
                                                  send_sem=send1.at[w, k], recv_sem=recv1.at[w, k],
                                                  device_id=(px, py, c), device_id_type=MESH)
                cp.start()
                first.append(cp)
        for w in range(nw):
            h = dst[w].shape[1] // 2
            for k, (px, py, s) in enumerate(chips):
                landed = dst[w].at[s, pl.ds(c * h, h), :]
                pltpu.make_async_remote_copy(src_ref=landed, dst_ref=landed, send_sem=send1.at[w, k], recv_sem=recv1.at[w, k],
                                             device_id=(px, py, c), device_id_type=MESH).wait_recv()
                cp = pltpu.make_async_remote_copy(src_ref=landed, dst_ref=landed, send_sem=send2.at[w, k], recv_sem=recv2.at[w, k],
                                                  device_id=sib, device_id_type=MESH)
                cp.start()
                passed.append(cp)
        for w in range(nw):
            h = dst[w].shape[1] // 2
            for k, (px, py, s) in enumerate(chips):
                other = dst[w].at[s, pl.ds((1 - c) * h, h), :]
                pltpu.make_async_remote_copy(src_ref=other, dst_ref=other, send_sem=send2.at[w, k], recv_sem=recv2.at[w, k],
                                             device_id=sib, device_id_type=MESH).wait_recv()
        for cp in first + passed:
            cp.wait_send()

    sem = pltpu.SemaphoreType.DMA((nw, 3))
    return pl.pallas_call(
        body, out_shape=[jax.ShapeDtypeStruct(b.shape, b.dtype) for b in bufs],
        in_specs=[ANY] * nw, out_specs=[ANY] * nw, input_output_aliases={w: w for w in range(nw)},
        scratch_shapes=[sem, sem, sem, sem], name="gather_weights",
        compiler_params=pltpu.CompilerParams(has_side_effects=True))(*bufs)


def swap_halves(name, grads):
    nw = len(grads)

    def body(*refs):
        src, got = refs[:nw], refs[nw:2 * nw]
        send, recv = refs[2 * nw:]
        x, y, c = _place()
        cps = []
        for w in range(nw):
            h = src[w].shape[1] // 2
            cp = pltpu.make_async_remote_copy(src_ref=src[w].at[:, pl.ds((1 - c) * h, h), :], dst_ref=got[w],
                                              send_sem=send.at[w], recv_sem=recv.at[w],
                                              device_id=(x, y, 1 - c), device_id_type=MESH)
            cp.start()
            cps.append(cp)
        for cp in cps:
            cp.wait()

    half = [jax.ShapeDtypeStruct((g.shape[0], g.shape[1] // 2, g.shape[2]), g.dtype) for g in grads]
    sem = pltpu.SemaphoreType.DMA((nw,))
    return pl.pallas_call(
        body, out_shape=half, in_specs=[ANY] * nw, out_specs=[ANY] * nw,
        scratch_shapes=[sem, sem], name=name,
        compiler_params=pltpu.CompilerParams(has_side_effects=True))(*grads)


def scatter_to_owners(parts):
    nw = len(parts)

    def body(*refs):
        src, dst = refs[:nw], refs[nw:2 * nw]
        send, recv = refs[2 * nw:]
        x, y, c = _place()
        cps = []
        for w in range(nw):
            for k, (px, py, s) in enumerate(_other_chips(x, y)):
                cp = pltpu.make_async_remote_copy(src_ref=src[w].at[s], dst_ref=dst[w].at[k],
                                                  send_sem=send.at[w, k], recv_sem=recv.at[w, k],
                                                  device_id=(px, py, c), device_id_type=MESH)
                cp.start()
                cps.append(cp)
        for cp in cps:
            cp.wait()

    sem = pltpu.SemaphoreType.DMA((nw, 3))
    return pl.pallas_call(
        body, out_shape=[jax.ShapeDtypeStruct((N_CHIPS - 1,) + p.shape[1:], p.dtype) for p in parts],
        in_specs=[ANY] * nw, out_specs=[ANY] * nw,
        scratch_shapes=[sem, sem], name="scatter_to_owners",
        compiler_params=pltpu.CompilerParams(has_side_effects=True))(*parts)


def join_halves(name, halves):
    nw = len(halves)

    def body(*refs):
        src, dst = refs[:nw], refs[nw:2 * nw]
        send, recv = refs[2 * nw:]
        x, y, c = _place()
        cps = []
        for w in range(nw):
            cp = pltpu.make_async_remote_copy(src_ref=src[w], dst_ref=dst[w], send_sem=send.at[w], recv_sem=recv.at[w],
                                              device_id=(x, y, 1 - c), device_id_type=MESH)
            cp.start()
            cps.append(cp)
        for cp in cps:
            cp.wait()

    sem = pltpu.SemaphoreType.DMA((nw,))
    return pl.pallas_call(
        body, out_shape=[jax.ShapeDtypeStruct(p.shape, p.dtype) for p in halves], in_specs=[ANY] * nw, out_specs=[ANY] * nw,
        scratch_shapes=[sem, sem], name=name,
        compiler_params=pltpu.CompilerParams(has_side_effects=True))(*halves)


SEM = pl.BlockSpec(memory_space=pltpu.SEMAPHORE)
VM = pl.BlockSpec(memory_space=pltpu.VMEM)
DATAFLOW = pltpu.SideEffectType.DATAFLOW_SIDE_EFFECTING
TOKEN = jax.ShapeDtypeStruct((SUBLANES, LANES), f32)


def _gather_copy(buf, w, k, chip, c, mine, send, recv):
    px, py, _ = chip
    h = buf.shape[1] // 2
    half = buf.at[mine, pl.ds(c * h, h), :]
    return pltpu.make_async_remote_copy(src_ref=half, dst_ref=half, send_sem=send.at[3 * w + k], recv_sem=recv.at[3 * w + k],
                                        device_id=(px, py, c), device_id_type=MESH)


def _gather_landing(buf, w, k, chip, c, send, recv):
    px, py, s = chip
    h = buf.shape[1] // 2
    landed = buf.at[s, pl.ds(c * h, h), :]
    return pltpu.make_async_remote_copy(src_ref=landed, dst_ref=landed, send_sem=send.at[3 * w + k], recv_sem=recv.at[3 * w + k],
                                        device_id=(px, py, c), device_id_type=MESH)


def gather_start(name, bufs, groups, after):
    nw, ng = len(bufs), len(groups)

    def body(*refs):
        outs = refs[nw + 1:]
        sems, dst = outs[:2 * ng], outs[2 * ng:2 * ng + nw]
        token = outs[2 * ng + nw]
        x, y, c = _place()
        mine = 2 * x + y
        for g, members in enumerate(groups):
            for i, w in enumerate(members):
                for k, chip in enumerate(_other_chips(x, y)):
                    _gather_copy(dst[w], i, k, chip, c, mine, sems[2 * g], sems[2 * g + 1]).start()
        token[...] = jnp.zeros_like(token)

    sem_shapes = []
    for members in groups:
        sem_shapes += [pltpu.SemaphoreType.DMA((3 * len(members),))] * 2
    outs = pl.pallas_call(
        body, out_shape=sem_shapes + [jax.ShapeDtypeStruct(b.shape, b.dtype) for b in bufs] + [TOKEN],
        in_specs=[ANY] * (nw + 1), out_specs=[SEM] * (2 * ng) + [ANY] * nw + [VM],
        input_output_aliases={w: 2 * ng + w for w in range(nw)}, name=name,
        compiler_params=pltpu.CompilerParams(has_side_effects=DATAFLOW))(*bufs, after)
    return [(outs[2 * g], outs[2 * g + 1]) for g in range(ng)], list(outs[2 * ng:2 * ng + nw]), outs[2 * ng + nw]


def gather_wait(name, bufs, send, recv, after):
    nw = len(bufs)

    def body(*refs):
        src = refs[:nw]
        send_ref, recv_ref = refs[nw], refs[nw + 1]
        x, y, c = _place()
        mine = 2 * x + y
        for w in range(nw):
            for k, chip in enumerate(_other_chips(x, y)):
                _gather_copy(src[w], w, k, chip, c, mine, send_ref, recv_ref).wait_send()
                _gather_landing(src[w], w, k, chip, c, send_ref, recv_ref).wait_recv()

    return pl.pallas_call(
        body, out_shape=[jax.ShapeDtypeStruct(b.shape, b.dtype) for b in bufs],
        in_specs=[ANY] * nw + [SEM, SEM, ANY], out_specs=[ANY] * nw,
        input_output_aliases={w: w for w in range(nw)}, name=name,
        compiler_params=pltpu.CompilerParams(has_side_effects=DATAFLOW))(*bufs, send, recv, after)


def gather_forward(name, bufs):
    nw = len(bufs)

    def body(*refs):
        dst = refs[nw:2 * nw]
        send, recv = refs[2 * nw:]
        x, y, c = _place()
        sib = (x, y, 1 - c)
        cps = []
        for w in range(nw):
            h = dst[w].shape[1] // 2
            for k, (_, _, s) in enumerate(_other_chips(x, y)):
                landed = dst[w].at[s, pl.ds(c * h, h), :]
                cp = pltpu.make_async_remote_copy(src_ref=landed, dst_ref=landed, send_sem=send.at[w, k], recv_sem=recv.at[w, k],
                                                  device_id=sib, device_id_type=MESH)
                cp.start()
                cps.append(cp)
        for w in range(nw):
            h = dst[w].shape[1] // 2
            for k, (_, _, s) in enumerate(_other_chips(x, y)):
                other = dst[w].at[s, pl.ds((1 - c) * h, h), :]
                pltpu.make_async_remote_copy(src_ref=other, dst_ref=other, send_sem=send.at[w, k], recv_sem=recv.at[w, k],
                                             device_id=sib, device_id_type=MESH).wait_recv()
        for cp in cps:
            cp.wait_send()

    sem = pltpu.SemaphoreType.DMA((nw, 3))
    return pl.pallas_call(
        body, out_shape=[jax.ShapeDtypeStruct(b.shape, b.dtype) for b in bufs],
        in_specs=[ANY] * nw, out_specs=[ANY] * nw, input_output_aliases={w: w for w in range(nw)},
        scratch_shapes=[sem, sem], name=name, compiler_params=pltpu.CompilerParams(has_side_effects=True))(*bufs)


def _scatter_copy(src, dst, w, k, chip, c, send, recv):
    px, py, s = chip
    return pltpu.make_async_remote_copy(src_ref=src.at[s], dst_ref=dst.at[k], send_sem=send.at[3 * w + k], recv_sem=recv.at[3 * w + k],
                                        device_id=(px, py, c), device_id_type=MESH)


def scatter_start(name, parts):
    nw = len(parts)
    lands = [pltpu.with_memory_space_constraint(lax.empty((N_CHIPS - 1,) + p.shape[1:], p.dtype), pltpu.HBM) for p in parts]

    def body(*refs):
        outs = refs[2 * nw:]
        send, recv = outs[0], outs[1]
        src, dst, token = outs[2:2 + nw], outs[2 + nw:2 + 2 * nw], outs[2 + 2 * nw]
        x, y, c = _place()
        for w in range(nw):
            for k, chip in enumerate(_other_chips(x, y)):
                _scatter_copy(src[w], dst[w], w, k, chip, c, send, recv).start()
        token[...] = jnp.zeros_like(token)

    sem = pltpu.SemaphoreType.DMA((3 * nw,))
    outs = pl.pallas_call(
        body, out_shape=[sem, sem] + [jax.ShapeDtypeStruct(p.shape, p.dtype) for p in parts]
        + [jax.ShapeDtypeStruct(l.shape, l.dtype) for l in lands] + [TOKEN],
        in_specs=[ANY] * (2 * nw), out_specs=[SEM, SEM] + [ANY] * (2 * nw) + [VM],
        input_output_aliases={i: 2 + i for i in range(2 * nw)}, name=name,
        compiler_params=pltpu.CompilerParams(has_side_effects=DATAFLOW))(*parts, *lands)
    return outs[0], outs[1], list(outs[2:2 + nw]), list(outs[2 + nw:2 + 2 * nw]), outs[2 + 2 * nw]


def scatter_wait(name, parts, lands, send, recv, after):
    nw = len(parts)

    def body(*refs):
        src, dst = refs[:nw], refs[nw:2 * nw]
        send_ref, recv_ref = refs[2 * nw], refs[2 * nw + 1]
        x, y, c = _place()
        for w in range(nw):
            for k, chip in enumerate(_other_chips(x, y)):
                cp = _scatter_copy(src[w], dst[w], w, k, chip, c, send_ref, recv_ref)
                cp.wait_send()
                cp.wait_recv()

    outs = pl.pallas_call(
        body, out_shape=[jax.ShapeDtypeStruct(a.shape, a.dtype) for a in list(parts) + list(lands)],
        in_specs=[ANY] * (2 * nw) + [SEM, SEM, ANY], out_specs=[ANY] * (2 * nw),
        input_output_aliases={i: i for i in range(2 * nw)}, name=name,
        compiler_params=pltpu.CompilerParams(has_side_effects=DATAFLOW))(*parts, *lands, send, recv, after)
    return list(outs[:nw]), list(outs[nw:])


def _sibling_copy(src, dst, w, c, half_rows, send, recv, sib):
    if half_rows:
        h = src.shape[1] // 2
        src = src.at[:, pl.ds((1 - c) * h, h), :]
    return pltpu.make_async_remote_copy(src_ref=src, dst_ref=dst, send_sem=send.at[w], recv_sem=recv.at[w],
                                        device_id=sib, device_id_type=MESH)


def _landing(shape, dtype):
    return pltpu.with_memory_space_constraint(lax.empty(shape, dtype), pltpu.HBM)


def sibling_start(name, srcs, half_rows):
    nw = len(srcs)
    lands = [_landing((s.shape[0], s.shape[1] // 2, s.shape[2]) if half_rows else s.shape, s.dtype) for s in srcs]

    def body(*refs):
        outs = refs[2 * nw:]
        send, recv = outs[0], outs[1]
        src, dst, token = outs[2:2 + nw], outs[2 + nw:2 + 2 * nw], outs[2 + 2 * nw]
        x, y, c = _place()
        for w in range(nw):
            _sibling_copy(src[w], dst[w], w, c, half_rows, send, recv, (x, y, 1 - c)).start()
        token[...] = jnp.zeros_like(token)

    sem = pltpu.SemaphoreType.DMA((nw,))
    outs = pl.pallas_call(
        body, out_shape=[sem, sem] + [jax.ShapeDtypeStruct(a.shape, a.dtype) for a in list(srcs) + lands] + [TOKEN],
        in_specs=[ANY] * (2 * nw), out_specs=[SEM, SEM] + [ANY] * (2 * nw) + [VM],
        input_output_aliases={i: 2 + i for i in range(2 * nw)}, name=name,
        compiler_params=pltpu.CompilerParams(has_side_effects=DATAFLOW))(*srcs, *lands)
    return outs[0], outs[1], list(outs[2:2 + nw]), list(outs[2 + nw:2 + 2 * nw]), outs[2 + 2 * nw]


def sibling_wait(name, srcs, lands, send, recv, half_rows, after):
    nw = len(srcs)

    def body(*refs):
        src, dst = refs[:nw], refs[nw:2 * nw]
        send_ref, recv_ref = refs[2 * nw], refs[2 * nw + 1]
        x, y, c = _place()
        for w in range(nw):
            cp = _sibling_copy(src[w], dst[w], w, c, half_rows, send_ref, recv_ref, (x, y, 1 - c))
            cp.wait_send()
            cp.wait_recv()

    outs = pl.pallas_call(
        body, out_shape=[jax.ShapeDtypeStruct(a.shape, a.dtype) for a in list(srcs) + list(lands)],
        in_specs=[ANY] * (2 * nw) + [SEM, SEM, ANY], out_specs=[ANY] * (2 * nw),
        input_output_aliases={i: i for i in range(2 * nw)}, name=name,
        compiler_params=pltpu.CompilerParams(has_side_effects=DATAFLOW))(*srcs, *lands, send, recv, after)
    return list(outs[:nw]), list(outs[nw:])


def _peer(x, y, c, r):
    return (1 - x if r & 4 else x, 1 - y if r & 2 else y, 1 - c if r & 1 else c)


def _everyone_copy(buf, r, x, y, c, send, recv, landing):
    px, py, pc = _peer(x, y, c, r)
    slot = buf.at[4 * px + 2 * py + pc] if landing else buf.at[4 * x + 2 * y + c]
    return pltpu.make_async_remote_copy(src_ref=slot, dst_ref=slot, send_sem=send.at[r - 1], recv_sem=recv.at[r - 1],
                                        device_id=(px, py, pc), device_id_type=MESH)


def everyone_start(name, buf):
    def body(buf_in, send, recv, buf_ref, token):
        x, y, c = _place()
        for r in range(1, N_DEV):
            _everyone_copy(buf_ref, r, x, y, c, send, recv, False).start()
        token[...] = jnp.zeros_like(token)

    sem = pltpu.SemaphoreType.DMA((N_DEV - 1,))
    return pl.pallas_call(
        body, out_shape=[sem, sem, jax.ShapeDtypeStruct(buf.shape, buf.dtype), TOKEN],
        in_specs=[ANY], out_specs=[SEM, SEM, ANY, VM], input_output_aliases={0: 2}, name=name,
        compiler_params=pltpu.CompilerParams(has_side_effects=DATAFLOW))(buf)


def everyone_wait(name, buf, send, recv, after):
    def body(buf_ref, send_ref, recv_ref, after_ref, out_ref):
        x, y, c = _place()
        for r in range(1, N_DEV):
            _everyone_copy(buf_ref, r, x, y, c, send_ref, recv_ref, False).wait_send()
            _everyone_copy(buf_ref, r, x, y, c, send_ref, recv_ref, True).wait_recv()

    return pl.pallas_call(
        body, out_shape=jax.ShapeDtypeStruct(buf.shape, buf.dtype), in_specs=[ANY, SEM, SEM, ANY], out_specs=ANY,
        input_output_aliases={0: 0}, name=name,
        compiler_params=pltpu.CompilerParams(has_side_effects=DATAFLOW))(buf, send, recv, after)


def all_reduce_small(v):
    R, n = v.shape

    def body(v_ref, o_ref, all_ref, send_sems, recv_sems, local_sem):
        x, y, c = _place()
        me, sib = (x, y, c), (x, y, 1 - c)
        chips = [(1 - x, y), (x, 1 - y), (1 - x, 1 - y)]

        def rows(px, py, pc):
            return all_ref.at[pl.ds((4 * px + 2 * py + pc) * R, R), :]

        def copy(k, block, to, src=None):
            return pltpu.make_async_remote_copy(src_ref=rows(*block) if src is None else src, dst_ref=rows(*block),
                                                send_sem=send_sems.at[k], recv_sem=recv_sems.at[k],
                                                device_id=to, device_id_type=MESH)

        own = pltpu.make_async_copy(v_ref, rows(*me), local_sem)
        own.start()
        first = [copy(0, me, sib, src=v_ref)]
        first += [copy(1 + j, me, (*chip, c), src=v_ref) for j, chip in enumerate(chips)]
        for cp in first:
            cp.start()
        passed = [copy(4 + j, (*chip, c), sib) for j, chip in enumerate(chips)]
        for j, chip in enumerate(chips):
            copy(1 + j, (*chip, c), me).wait_recv()
            passed[j].start()
        copy(0, sib, me).wait_recv()
        for j, chip in enumerate(chips):
            copy(4 + j, (*chip, 1 - c), me).wait_recv()
        for cp in first + passed:
            cp.wait_send()
        own.wait()
        acc = all_ref[pl.ds(0, R), :]
        for d in range(1, N_DEV):
            acc = acc + all_ref[pl.ds(d * R, R), :]
        o_ref[...] = acc

    vm = pl.BlockSpec(memory_space=pltpu.VMEM)
    return pl.pallas_call(
        body, out_shape=jax.ShapeDtypeStruct((R, n), f32), in_specs=[vm], out_specs=vm,
        scratch_shapes=[pltpu.VMEM((N_DEV * R, n), f32), pltpu.SemaphoreType.DMA((7,)), pltpu.SemaphoreType.DMA((7,)),
                        pltpu.SemaphoreType.DMA],
        name="all_reduce_small", compiler_params=pltpu.CompilerParams(vmem_limit_bytes=VMEM_LIMIT, has_side_effects=True))(v)


def local_step(x, pos, tgt, small, d_in, get_w, put_g, first_dep=None):
    L, D = x.shape
    d_kv = N_KV_HEADS * HEAD_DIM
    d_ssm = small["d_skip"].shape[1]
    d_attn = d_in - 2 * d_kv - d_ssm
    big = {}
    G = d_ssm // SSM_GROUP
    N, P = SSM_STATE, SSM_GROUP
    gbf = bf16

    half_dim = HEAD_DIM // 2
    inv_freq = ROPE_THETA ** (-jnp.arange(half_dim, dtype=f32) / half_dim)
    inv_freq = jnp.tile(inv_freq, LANES // half_dim).reshape(1, LANES)
    sink_b = jnp.broadcast_to(small["sinks"].reshape(-1, 1), (small["sinks"].size, LANES))

    spread = jnp.repeat(jnp.eye(G, dtype=f32), P, axis=0)
    logdt_b = jnp.broadcast_to(small["log_dt"].reshape(G, 1), (G, N))
    bt_re = small["b_re"].reshape(G, N, P).transpose(0, 2, 1).reshape(G * P, N)
    bt_im = small["b_im"].reshape(G, N, P).transpose(0, 2, 1).reshape(G * P, N)
    a_re, a_im = small["a_re"].reshape(G, N), small["a_im"].reshape(G, N)
    lam_re, lam_im, bbt_re, bbt_im = ssm_params(a_re, a_im, logdt_b, bt_re, bt_im, spread)
    bd_re = _block_diag(bbt_re.reshape(G, P, N), P, N).astype(bf16)
    bd_im = _block_diag(bbt_im.reshape(G, P, N), P, N).astype(bf16)
    c_re = small["c_re"].reshape(G, P, N).transpose(0, 2, 1)
    c_im = small["c_im"].reshape(G, P, N).transpose(0, 2, 1)
    cd_re = _block_diag(c_re, N, P).astype(bf16)
    cd_im = _block_diag(c_im, N, P).astype(bf16)
    lam_re_l, lam_im_l = _state_layout(lam_re), _state_layout(lam_im)

    def k1(i, nt, xt, g):
        return (rms_fwd(xt, g),)
    xn = rowwise("pre_mix_norm", k1, L, [full(x)], [small["g_pre_mix"]], [(D, bf16)], dep=first_dep)[0]
    big["w_in"] = get_w("w_in", xn)
    proj = mm_nn("proj_in", xn, big["w_in"])
    qr, kk, vv, u_bf = qkv_prep(proj, pos, inv_freq, d_attn, d_kv)
    attn = attn_fwd(qr, kk, vv, sink_b)
    u_cb = (d_attn + 2 * d_kv) // (d_ssm // 2)
    y, z_bf, s_re, s_im = ssm_fwd(u_bf, proj, u_cb, bd_re, bd_im, cd_re, cd_im, lam_re_l, lam_im_l, small["d_skip"])
    big["w_glu"] = get_w("w_glu", z_bf)
    gl = mm_nn("glu_proj", z_bf, big["w_glu"])

    def k6(i, nt, at, yt, glt, bg, ga, gs):
        ssm = gelu(yt) * sigmoid(glt + bg)
        return (jnp.concatenate([rms_fwd(at, ga), rms_fwd(ssm, gs)], axis=1),)
    mixed = rowwise("mix_norms", k6, L, [full(attn), full(y), full(gl)],
                    [small["b_glu"], small["g_attn_out"], small["g_ssm_out"]], [(d_attn + d_ssm, bf16)])[0]
    big["w_o"] = get_w("w_o", mixed)
    mix = mm_nn("proj_out", mixed, big["w_o"])

    def k7(i, nt, xt, mt, gpm, gpf):
        h = xt + rms_fwd(mt, gpm)
        return h, rms_fwd(h, gpf)
    h, hn = rowwise("post_mix", k7, L, [full(x), full(mix)], [small["g_post_mix"], small["g_pre_ffn"]], [(D, f32), (D, bf16)])
    big["w_gate"] = get_w("w_gate", hn)
    big["w_up"] = get_w("w_up", hn)
    gt, up, hid = ffn_hidden(hn, big["w_gate"], big["w_up"])
    d_ff_dim = gt.shape[1]
    big["w_down"] = get_w("w_down", hid)
    ff = mm_nn("ffn_down", hid, big["w_down"], tk=d_ff_dim // 2)

    def k9(i, nt, ht, fft, tt, g):
        out = ht + rms_fwd(fft, g)
        err = out - tt
        per_row = jnp.mean(err * err, axis=-1, keepdims=True)
        loss = 0.5 * jnp.sum(per_row) * jnp.where(_lane((1, LANES)) == 0, 1.0, 0.0)
        d_out = err * (1.0 / D)
        d_ff, dg = rms_bwd(fft, g, d_out)
        return d_out, d_ff, dg, loss
    d_out, d_ff, dg_post_ffn, loss = rowwise("loss_head", k9, L, [full(h), full(ff), full(tgt)], [small["g_post_ffn"]],
                                             [(D, f32), (D, bf16)], reds=[D, LANES])

    d_gt, d_up = ffn_hidden_grad(d_ff, big["w_down"], gt, up)
    token = put_g("w_down", mm_tn("dw_down", hid, d_ff, out_dtype=gbf, tm=d_ff_dim // N_CHIPS))
    d_hn = mm_nt_pair("d_hn", d_gt, big["w_gate"], d_up, big["w_up"], dep=token)
    token = put_g("w_gate", mm_tn("dw_gate", hn, d_gt, shards=N_CHIPS, out_dtype=gbf))
    token = put_g("w_up", mm_tn("dw_up", hn, d_up, shards=N_CHIPS, out_dtype=gbf, dep=token))

    def k11(i, nt, ht, da, do, mt, gpf, gpm):
        dh_n, dg_pf = rms_bwd(ht, gpf, da)
        dh = do + dh_n
        d_mix, dg_pm = rms_bwd(mt, gpm, dh)
        return dh, d_mix, dg_pf, dg_pm
    dh, d_mix, dg_pre_ffn, dg_post_mix = rowwise("post_mix_grad", k11, L, [full(h), full(d_hn), full(d_out), full(mix)],
                                                 [small["g_pre_ffn"], small["g_post_mix"]], [(D, f32), (D, bf16)], reds=[D, D], dep=token)
    d_mixed = mm_nt("d_mixed", d_mix, big["w_o"])
    token = put_g("w_o", mm_tn("dw_o", mixed, d_mix, out_dtype=gbf))

    def k12(i, nt, at, yt, glt, da_n, ds_n, bg, ga, gs):
        z = gelu(yt)
        sg = sigmoid(glt + bg)
        ssm = z * sg
        d_at, dga = rms_bwd(at, ga, da_n)
        d_ssm_t, dgs = rms_bwd(ssm, gs, ds_n)
        d_gl = d_ssm_t * z * sg * (1.0 - sg)
        return d_at, d_ssm_t * sg, d_gl, dga, dgs, colsum(d_gl)
    d_attn_o, dz1, d_gl, dg_attn, dg_ssm, db_glu = rowwise(
        "mix_norms_grad", k12, L, [full(attn), full(y), full(gl), (d_mixed, d_attn, 0, 0), (d_mixed, d_ssm, d_attn // d_ssm, 0)],
        [small["b_glu"], small["g_attn_out"], small["g_ssm_out"]], [(d_attn, f32), (d_ssm, f32), (d_ssm, bf16)],
        reds=[d_attn, d_ssm, d_ssm], dep=token)
    dz2 = mm_nt("d_glu_in", d_gl, big["w_glu"])
    token = put_g("w_glu", mm_tn("dw_glu", z_bf, d_gl, out_dtype=gbf))

    du, dbd_re, dbd_im, dcd_re, dcd_im, dlam_re_l, dlam_im_l, dd_skip = ssm_bwd(
        y, dz1, dz2, u_bf, proj, u_cb, s_re, s_im, bd_re, bd_im, cd_re, cd_im, lam_re_l, lam_im_l, small["d_skip"], dep=token)
    dq, dkk_c, dkk_p, dvv_c, dvv_p, dsink = attn_bwd(qr, kk, vv, sink_b, attn, d_attn_o)
    d_proj = qkv_grad(dq, dkk_c, dkk_p, dvv_c, dvv_p, du, pos, inv_freq)
    d_xn = mm_nt("d_xn", d_proj, big["w_in"])
    token = put_g("w_in", mm_tn("dw_in", xn, d_proj, shards=N_CHIPS, out_dtype=gbf))

    def k17(i, nt, xt, dxn, dht, g):
        dx, dg = rms_bwd(xt, g, dxn)
        return dht + dx, dg
    grad_x, dg_pre_mix = rowwise("pre_mix_grad", k17, L, [full(x), full(d_xn), full(dh)], [small["g_pre_mix"]],
                                 [(D, f32)], reds=[D], dep=token)

    gather = spread.T
    dbbt_re = _block_diag_take(dbd_re, P, N).reshape(G * P, N)
    dbbt_im = _block_diag_take(dbd_im, P, N).reshape(G * P, N)
    d_a_re, d_a_im, d_logdt, dbt_re, dbt_im = ssm_params_grad(
        a_re, a_im, logdt_b, bt_re, bt_im, spread, gather,
        _state_layout_inv(dlam_re_l, G, N), _state_layout_inv(dlam_im_l, G, N), dbbt_re, dbbt_im)
    q_per_kv = d_attn // HEAD_DIM // N_KV_HEADS
    small_grads = {
        "g_pre_mix": dg_pre_mix, "sinks": dsink[:, :q_per_kv, 0].reshape(1, -1),
        "a_re": d_a_re, "a_im": d_a_im, "log_dt": d_logdt.reshape(1, G),
        "b_re": dbt_re.reshape(G, P, N).transpose(0, 2, 1), "b_im": dbt_im.reshape(G, P, N).transpose(0, 2, 1),
        "c_re": _block_diag_take(dcd_re, N, P).transpose(0, 2, 1), "c_im": _block_diag_take(dcd_im, N, P).transpose(0, 2, 1),
        "d_skip": dd_skip, "b_glu": db_glu, "g_attn_out": dg_attn, "g_ssm_out": dg_ssm,
        "g_post_mix": dg_post_mix, "g_pre_ffn": dg_pre_ffn, "g_post_ffn": dg_post_ffn,
    }
    return loss, grad_x, small_grads


WEIGHTS = ['g_pre_mix', 'w_in', 'sinks', 'a_re', 'a_im', 'log_dt', 'b_re', 'b_im', 'c_re', 'c_im', 'd_skip', 'w_glu', 'b_glu',
           'g_attn_out', 'g_ssm_out', 'w_o', 'g_post_mix', 'g_pre_ffn', 'w_gate', 'w_up', 'w_down', 'g_post_ffn']
BIG = ['w_in', 'w_glu', 'w_o', 'w_gate', 'w_up', 'w_down']
COL_SHARDED = ['w_in', 'w_gate', 'w_up']
SMALL = [n for n in WEIGHTS if n not in BIG]
GATHER_GROUPS = [["w_in"], ["w_glu", "w_o"], ["w_gate", "w_up"], ["w_down"]]
REDUCE_GROUPS = [["w_down", "w_gate", "w_up"], ["w_o", "w_glu", "w_in"]]


PACK_ROWS = 256


def _pack(parts):
    flat = jnp.concatenate([p.reshape(-1) for p in parts])
    pad = (-flat.size) % (PACK_ROWS * LANES)
    return jnp.pad(flat, (0, pad)).reshape(-1, LANES)


def _unpack(packed, shapes):
    flat = packed.reshape(-1)
    out, off = [], 0
    for s in shapes:
        n = int(np.prod(s))
        out.append(flat[off:off + n].reshape(s))
        off += n
    return out


def kernel(x, positions, g_pre_mix, w_in, sinks, a_re, a_im, log_dt, b_re, b_im, c_re, c_im, d_skip, w_glu, b_glu, g_attn_out, g_ssm_out, w_o, g_post_mix, g_pre_ffn, w_gate, w_up, w_down, g_post_ffn, loss_target, m_g_pre_mix, m_w_in, m_sinks, m_a_re, m_a_im, m_log_dt, m_b_re, m_b_im, m_c_re, m_c_im, m_d_skip, m_w_glu, m_b_glu, m_g_attn_out, m_g_ssm_out, m_w_o, m_g_post_mix, m_g_pre_ffn, m_w_gate, m_w_up, m_w_down, m_g_post_ffn, v_g_pre_mix, v_w_in, v_sinks, v_a_re, v_a_im, v_log_dt, v_b_re, v_b_im, v_c_re, v_c_im, v_d_skip, v_w_glu, v_b_glu, v_g_attn_out, v_g_ssm_out, v_w_o, v_g_post_mix, v_g_pre_ffn, v_w_gate, v_w_up, v_w_down, v_g_post_ffn):
    args = dict(locals())
    w = {n: args[n] for n in WEIGHTS}
    m = {n: args["m_" + n] for n in WEIGHTS}
    v = {n: args["v_" + n] for n in WEIGHTS}
    L, D = x.shape[1], x.shape[2]

    ax, ay, ac = _place()
    mine_arr = (2 * ax + ay).astype(jnp.int32).reshape(1)
    c_arr = ac.astype(jnp.int32).reshape(1)

    me_arr = (4 * ax + 2 * ay + ac).astype(jnp.int32).reshape(1)

    bufs = {"w_in": into_slot("cast_w_in", w["w_in"][0], mine_arr, N_CHIPS, bf16)}
    sems, (bufs["w_in"],), token = gather_start("gather_start_in", [bufs["w_in"]], [[0]], mine_arr)
    later = [n for n in BIG if n != "w_in"]
    for n in later:
        bufs[n] = into_slot("cast_" + n, w[n][0], mine_arr, N_CHIPS, bf16, dep=token)
    more, started, token = gather_start("gather_start_rest", [bufs[n] for n in later],
                                        [[later.index(n) for n in g] for g in GATHER_GROUPS[1:]], token)
    sems += more
    bufs.update(zip(later, started))
    ready = set()

    def get_w(n, after):
        if n not in ready:
            gi = [n in g for g in GATHER_GROUPS].index(True)
            members = GATHER_GROUPS[gi]
            landed = gather_wait("gather_wait_%d" % gi, [bufs[k] for k in members], *sems[gi], after)
            bufs.update(zip(members, gather_forward("gather_forward_%d" % gi, landed)))
            ready.update(members)
        g = bufs[n]
        return g if n in COL_SHARDED else g.reshape(g.shape[0] * g.shape[1], g.shape[2])

    swaps, inflight = {}, []

    def put_g(n, g):
        g3 = g if n in COL_SHARDED else g.reshape(N_CHIPS, g.shape[0] // N_CHIPS, g.shape[1])
        swaps[n] = sibling_start("swap_start_" + n, [g3], True)
        for gi, members in enumerate(REDUCE_GROUPS):
            if n == members[-1]:
                last = swaps[n][4]
                pair = []
                for k in members:
                    send, recv, srcs, lands, _ = swaps[k]
                    (src,), (got,) = sibling_wait("swap_wait_" + k, srcs, lands, send, recv, True, last)
                    pair.append(pair_sum("pair_sum_" + k, src, got, c_arr))
                send, recv, parts, lands, tok = scatter_start("scatter_start_%d" % gi, pair)
                inflight.append((members, send, recv, parts, lands))
                return tok
        return swaps[n][4]

    small = {n: w[n].reshape(1, -1) for n in SMALL}
    pos = positions.reshape(L, 1).astype(f32)
    d_in = N_CHIPS * w["w_in"].shape[2]
    loss, grad_x, small_grads = local_step(x[0], pos, loss_target[0], small, d_in, get_w, put_g, first_dep=token)
    loss = lax.psum(jnp.sum(loss), ("x", "y", "c"))

    shapes = [w[n].shape for n in SMALL]
    blocks = into_slot("small_block", _pack([small_grads[n] for n in SMALL]), me_arr, N_DEV, f32)
    small_send, small_recv, blocks, after = everyone_start("small_start", blocks)

    grads, delta, new_m, new_v = {}, {}, {}, {}
    for gi, (members, send, recv, parts, lands) in enumerate(inflight):
        parts, landed = scatter_wait("scatter_wait_%d" % gi, parts, lands, send, recv, after)
        joins, dep = [], None
        for k, p, t in zip(members, parts, landed):
            joins.append(sibling_start("join_start_" + k, [chip_sum("chip_sum_" + k, p, t, mine_arr, dep=dep)], False))
            dep = after = joins[-1][4]
        for n, (send, recv, srcs, lands, _) in zip(members, joins):
            (own,), (sib,) = sibling_wait("join_wait_" + n, srcs, lands, send, recv, False, after)
            g_, d_, m_, v_ = adamw_halves("adamw_" + n, w[n][0], own, sib, m[n][0], v[n][0], c_arr)
            grads[n], delta[n], new_m[n], new_v[n] = g_[None], d_[None], m_[None], v_[None]
            after = v_
    blocks = everyone_wait("small_wait", blocks, small_send, small_recv, after)
    small_sum = sum_slots("small_sum", blocks)
    small_g = dict(zip(SMALL, _unpack(small_sum, shapes)))
    pw, pm, pv = (_pack([t[n] for n in SMALL]) for t in (w, m, v))
    d_, m_, v_ = adamw("adamw_small", pw, small_sum, pm, pv)
    for t, packed in ((delta, d_), (new_m, m_), (new_v, v_)):
        t.update(zip(SMALL, _unpack(packed, shapes)))
    grads.update(small_g)

    return (loss, grad_x[None], *[grads[n] for n in WEIGHTS], *[delta[n] for n in WEIGHTS],
            *[new_m[n] for n in WEIGHTS], *[new_v[n] for n in WEIGHTS])
```

```python
import functools
import math

import jax
import jax.numpy as jnp
import numpy as np
from jax import lax
from jax.experimental import pallas as pl
from jax.experimental.pallas import tpu as pltpu

f32 = jnp.float32
bf16 = jnp.bfloat16
HIGHEST = lax.Precision.HIGHEST
MESH = pl.DeviceIdType.MESH

HEAD_DIM = 64
N_KV_HEADS = 4
ATTN_BLOCK = 128
ROPE_THETA = 10000.0
SSM_GROUP = 16
SSM_STATE = 64
RMS_EPS = 1e-6
LANES = 128
SUBLANES = 8
VMEM_LIMIT = 52 * 1024 * 1024
N_CHIPS = 4
N_DEV = 8
NEG = -1e30

ADAM_LR, ADAM_B1, ADAM_B2, ADAM_EPS, ADAM_WD, ADAM_STEP = 0.001, 0.9, 0.999, 1e-08, 0.01, 10

NN = (((1,), (0,)), ((), ()))
NT = (((1,), (1,)), ((), ()))
TN = (((0,), (0,)), ((), ()))


def _params(*sem):
    return pltpu.CompilerParams(dimension_semantics=sem or None, vmem_limit_bytes=VMEM_LIMIT)


def _dot(a, b, dims=NN):
    return lax.dot_general(a, b, dims, preferred_element_type=f32)


def _pick(dim, pref):
    t = min(dim, pref)
    while dim % t:
        t -= LANES
    assert t > 0, (dim, pref)
    return t


ANY = pl.BlockSpec(memory_space=pl.ANY)


def _with_dep(in_specs, operands, dep):
    if dep is None:
        return list(in_specs), list(operands), 0
    return list(in_specs) + [ANY], list(operands) + [dep], 1


def _mm_call(name, grid, in_specs, out_spec, out_shape, acc_shape, dims, operands, dep=None):
    nk = grid[2]
    in_specs, operands, n_dep = _with_dep(in_specs, operands, dep)

    def body_one(a_ref, b_ref, *rest):
        o_ref = rest[n_dep]
        o_ref[...] = _dot(a_ref[...], b_ref[...], dims).astype(o_ref.dtype)

    def body(a_ref, b_ref, *rest):
        o_ref, acc_ref = rest[n_dep], rest[n_dep + 1]
        k = pl.program_id(2)

        @pl.when(k == 0)
        def _():
            acc_ref[...] = _dot(a_ref[...], b_ref[...], dims)

        @pl.when((k > 0) & (k < nk - 1))
        def _():
            acc_ref[...] += _dot(a_ref[...], b_ref[...], dims)

        @pl.when(k == nk - 1)
        def _():
            o_ref[...] = (acc_ref[...] + _dot(a_ref[...], b_ref[...], dims)).astype(o_ref.dtype)

    return pl.pallas_call(
        body_one if nk == 1 else body, out_shape=out_shape, grid=grid, in_specs=in_specs, out_specs=out_spec,
        scratch_shapes=[] if nk == 1 else [pltpu.VMEM(acc_shape, f32)], name=name,
        compiler_params=_params("parallel", "parallel", "arbitrary"))(*operands)


def mm_nt_pair(name, a1, b1, a2, b2, tm=1024, tk=1024, dep=None):
    M = a1.shape[0]
    S, K, n = b1.shape
    tm, tko = _pick(M, tm), _pick(K, tk)
    nk = 2 * S

    def body(a1_ref, b1_ref, a2_ref, b2_ref, *rest):
        o_ref, acc_ref = rest[-2], rest[-1]
        k = pl.program_id(2)

        @pl.when(k == 0)
        def _():
            acc_ref[...] = _dot(a1_ref[...], b1_ref[...], NT)

        @pl.when((k > 0) & (k < S))
        def _():
            acc_ref[...] += _dot(a1_ref[...], b1_ref[...], NT)

        @pl.when((k >= S) & (k < nk - 1))
        def _():
            acc_ref[...] += _dot(a2_ref[...], b2_ref[...], NT)

        @pl.when(k == nk - 1)
        def _():
            o_ref[...] = acc_ref[...] + _dot(a2_ref[...], b2_ref[...], NT)

    first = lambda k: jnp.minimum(k, S - 1)
    second = lambda k: jnp.maximum(k - S, 0)
    in_specs = [pl.BlockSpec((tm, n), lambda i, j, k: (i, first(k))), pl.BlockSpec((None, tko, n), lambda i, j, k: (first(k), j, 0)),
                pl.BlockSpec((tm, n), lambda i, j, k: (i, second(k))), pl.BlockSpec((None, tko, n), lambda i, j, k: (second(k), j, 0))]
    in_specs, operands, _ = _with_dep(in_specs, (a1, b1, a2, b2), dep)
    return pl.pallas_call(
        body, out_shape=jax.ShapeDtypeStruct((M, K), f32), grid=(M // tm, K // tko, nk), in_specs=in_specs,
        out_specs=pl.BlockSpec((tm, tko), lambda i, j, k: (i, j)), scratch_shapes=[pltpu.VMEM((tm, tko), f32)], name=name,
        compiler_params=_params("parallel", "parallel", "arbitrary"))(*operands)


def mm_nn(name, a, b, out_dtype=f32, tm=1024, tn=1024, tk=2048, dep=None):
    M, K = a.shape
    tm, tk = _pick(M, tm), _pick(K, tk)
    if b.ndim == 3:
        S, _, n = b.shape
        tn = _pick(n, 2048)
        per = n // tn
        b_spec = pl.BlockSpec((None, tk, tn), lambda i, j, k: (j // per, k, j % per))
        N = S * n
    else:
        N = b.shape[1]
        tn = _pick(N, tn)
        b_spec = pl.BlockSpec((tk, tn), lambda i, j, k: (k, j))
    grid = (M // tm, N // tn, K // tk)
    return _mm_call(name, grid, [pl.BlockSpec((tm, tk), lambda i, j, k: (i, k)), b_spec],
                    pl.BlockSpec((tm, tn), lambda i, j, k: (i, j)), jax.ShapeDtypeStruct((M, N), out_dtype),
                    (tm, tn), NN, (a, b), dep)


def mm_nt(name, a, b, out_dtype=f32, tm=1024, tn=2048, tk=1024, dep=None):
    M, N = a.shape
    tm = _pick(M, tm)
    if b.ndim == 3:
        S, K, n = b.shape
        tr = _pick(n, 2048)
        per = n // tr
        tko = _pick(K, tk)
        b_spec = pl.BlockSpec((None, tko, tr), lambda i, j, k: (k // per, j, k % per))
    else:
        K = b.shape[0]
        tr = _pick(N, tn)
        tko = _pick(K, tk)
        b_spec = pl.BlockSpec((tko, tr), lambda i, j, k: (j, k))
    grid = (M // tm, K // tko, N // tr)
    return _mm_call(name, grid, [pl.BlockSpec((tm, tr), lambda i, j, k: (i, k)), b_spec],
                    pl.BlockSpec((tm, tko), lambda i, j, k: (i, j)), jax.ShapeDtypeStruct((M, K), out_dtype),
                    (tm, tko), NT, (a, b), dep)


def mm_tn(name, a, b, shards=None, out_dtype=f32, tm=1024, tn=1024, tl=2048, dep=None):
    L, K = a.shape
    N = b.shape[1]
    tl, tko = _pick(L, tl), _pick(K, tm)
    if shards:
        n = N // shards
        tn = _pick(n, 2048)
        per = n // tn
        o_spec = pl.BlockSpec((None, tko, tn), lambda i, j, k: (j // per, i, j % per))
        o_shape = jax.ShapeDtypeStruct((shards, K, n), out_dtype)
    else:
        tn = _pick(N, tn)
        o_spec = pl.BlockSpec((tko, tn), lambda i, j, k: (i, j))
        o_shape = jax.ShapeDtypeStruct((K, N), out_dtype)
    grid = (K // tko, N // tn, L // tl)
    return _mm_call(name, grid, [pl.BlockSpec((tl, tko), lambda i, j, k: (k, i)),
                                 pl.BlockSpec((tl, tn), lambda i, j, k: (k, j))],
                    o_spec, o_shape, (tko, tn), TN, (a, b), dep)


def ffn_hidden(hn, w_gate, w_up, tm=512):
    M, K = hn.shape
    S, _, n = w_gate.shape
    tm = _pick(M, tm)

    def body(a_ref, g_ref, u_ref, gt_ref, up_ref, hid_ref):
        a = a_ref[...]
        g = _dot(a, g_ref[...])
        u = _dot(a, u_ref[...])
        gt_ref[...] = g.astype(bf16)
        up_ref[...] = u.astype(bf16)
        hid_ref[...] = (g * sigmoid(g) * u).astype(bf16)

    w_spec = pl.BlockSpec((None, K, n), lambda s, i: (s, 0, 0))
    o_spec = pl.BlockSpec((tm, n), lambda s, i: (i, s))
    o = jax.ShapeDtypeStruct((M, S * n), bf16)
    return pl.pallas_call(
        body, out_shape=[o, o, o], grid=(S, M // tm), in_specs=[pl.BlockSpec((tm, K), lambda s, i: (i, 0)), w_spec, w_spec],
        out_specs=[o_spec, o_spec, o_spec], name="ffn_hidden", compiler_params=_params("parallel", "parallel"))(hn, w_gate, w_up)


def ffn_hidden_grad(d_ff, w_down, gt, up, tm=512):
    M, D = d_ff.shape
    F = w_down.shape[0]
    n = _pick(F // N_CHIPS, 2048)
    tm = _pick(M, tm)

    def body(a_ref, b_ref, gt_ref, up_ref, dg_ref, du_ref):
        dh = _dot(a_ref[...], b_ref[...], NT)
        g = gt_ref[...].astype(f32)
        sg = sigmoid(g)
        dg_ref[...] = (dh * up_ref[...].astype(f32) * (sg * (1.0 + g * (1.0 - sg)))).astype(bf16)
        du_ref[...] = (dh * (g * sg)).astype(bf16)

    t_spec = pl.BlockSpec((tm, n), lambda j, i: (i, j))
    o = jax.ShapeDtypeStruct((M, F), bf16)
    return pl.pallas_call(
        body, out_shape=[o, o], grid=(F // n, M // tm),
        in_specs=[pl.BlockSpec((tm, D), lambda j, i: (i, 0)), pl.BlockSpec((n, D), lambda j, i: (j, 0)), t_spec, t_spec],
        out_specs=[t_spec, t_spec], name="ffn_hidden_grad", compiler_params=_params("parallel", "parallel"))(d_ff, w_down, gt, up)


def rowwise(name, fn, L, rows, bcast, outs, reds=(), tr=256, dep=None):
    tr = min(tr, L)
    nt = L // tr
    n_rows, n_b, n_o = len(rows), len(bcast), len(outs)
    n_dep = 0 if dep is None else 1

    def body(*refs):
        i = pl.program_id(0)
        ins = [r[...] for r in refs[:n_rows + n_b]]
        res = fn(i, nt, *ins)
        o_refs = refs[n_rows + n_b + n_dep:]
        for k in range(n_o):
            o_refs[k][...] = res[k].astype(o_refs[k].dtype)
        if reds:
            @pl.when(i == 0)
            def _():
                for k in range(len(reds)):
                    o_refs[n_o + k][...] = jnp.zeros_like(o_refs[n_o + k])
            for k in range(len(reds)):
                o_refs[n_o + k][...] += res[n_o + k]

    def row_spec(width, cb, shift):
        if shift:
            return pl.BlockSpec((tr, width), lambda i: (jnp.minimum(i + shift, nt - 1), cb))
        return pl.BlockSpec((tr, width), lambda i: (i, cb))

    in_specs = [row_spec(w, cb, sh) for (_, w, cb, sh) in rows]
    in_specs += [pl.BlockSpec(b.shape, lambda i: (0, 0)) for b in bcast]
    out_specs = [pl.BlockSpec((tr, w), lambda i: (i, 0)) for (w, _) in outs]
    out_specs += [pl.BlockSpec((1, w), lambda i: (0, 0)) for w in reds]
    out_shape = [jax.ShapeDtypeStruct((L, w), dt) for (w, dt) in outs]
    out_shape += [jax.ShapeDtypeStruct((1, w), f32) for w in reds]
    in_specs, operands, _ = _with_dep(in_specs, [r[0] for r in rows] + list(bcast), dep)
    return pl.pallas_call(
        body, out_shape=out_shape, grid=(nt,), in_specs=in_specs, out_specs=out_specs, name=name,
        compiler_params=_params("arbitrary"))(*operands)


def full(a):
    return (a, a.shape[1], 0, 0)


def colsum(v):
    return jnp.sum(v, axis=0, keepdims=True)


def rms_fwd(x, g):
    r = lax.rsqrt(jnp.mean(x * x, axis=-1, keepdims=True) + RMS_EPS)
    return x * r * g


def rms_bwd(x, g, dy):
    r = lax.rsqrt(jnp.mean(x * x, axis=-1, keepdims=True) + RMS_EPS)
    xh = x * r
    dyg = dy * g
    dx = r * (dyg - xh * jnp.mean(dyg * xh, axis=-1, keepdims=True))
    return dx, colsum(dy * xh)


GELU_C = math.sqrt(2.0 / math.pi)


def gelu(y):
    return y * (0.5 * (1.0 + jnp.tanh(GELU_C * (y + 0.044715 * (y * y * y)))))


def gelu_grad(y):
    t = jnp.tanh(GELU_C * (y + 0.044715 * (y * y * y)))
    return 0.5 * (1.0 + t) + 0.5 * y * (1.0 - t * t) * (GELU_C * (1.0 + 3 * 0.044715 * (y * y)))


def sigmoid(v):
    return 1.0 / (1.0 + jnp.exp(-v))


def _lane(shape):
    return lax.broadcasted_iota(jnp.int32, shape, 1)


def _rot_chunk(t, cos, sin_signed):
    first = (_lane(t.shape) % HEAD_DIM) < (HEAD_DIM // 2)
    partner = jnp.where(first, pltpu.roll(t, LANES - HEAD_DIM // 2, 1), pltpu.roll(t, HEAD_DIM // 2, 1))
    return t * cos + partner * sin_signed


def _cos_sin(pos, inv_freq, inverse):
    ang = pos * inv_freq
    cos, sin = jnp.cos(ang), jnp.sin(ang)
    first = (_lane(ang.shape) % HEAD_DIM) < (HEAD_DIM // 2)
    sign = jnp.where(first, -1.0, 1.0) * (-1.0 if inverse else 1.0)
    return cos, sin * sign


def _dup_head(chunk, odd):
    low = _lane(chunk.shape) < HEAD_DIM
    x = jnp.where(low != odd, chunk, 0.0)
    return x + pltpu.roll(x, HEAD_DIM, 1)


def _chunks(v):
    return [v[:, LANES * c:LANES * (c + 1)] for c in range(v.shape[1] // LANES)]


def qkv_prep(proj, pos, inv_freq, d_attn, d_kv):
    L = proj.shape[0]
    d_ssm = proj.shape[1] - d_attn - 2 * d_kv
    half = d_ssm // 2
    scale = 1.0 / math.sqrt(HEAD_DIM)

    def fn(i, nt, q, k, v, u0, u1, p, invf):
        cos, sin = _cos_sin(p, invf, False)
        qr = jnp.concatenate([_rot_chunk(c, cos, sin) for c in _chunks(q)], axis=1) * scale
        kr = [_rot_chunk(c, cos, sin) for c in _chunks(k)]
        kk = jnp.concatenate([_dup_head(c, odd) for c in kr for odd in (False, True)], axis=1)
        vv = jnp.concatenate([_dup_head(c, odd) for c in _chunks(v) for odd in (False, True)], axis=1)
        return qr, kk, vv, jnp.concatenate([u0, u1], axis=1)

    u_cb = (d_attn + 2 * d_kv) // half
    return rowwise("qkv_prep", fn, L,
                   [(proj, d_attn, 0, 0), (proj, d_kv, d_attn // d_kv, 0), (proj, d_kv, d_attn // d_kv + 1, 0),
                    (proj, half, u_cb, 0), (proj, half, u_cb + 1, 0), full(pos)],
                   [inv_freq], [(d_attn, bf16), (2 * d_kv, bf16), (2 * d_kv, bf16), (d_ssm, bf16)])


def qkv_grad(dq, dkk_c, dkk_p, dvv_c, dvv_p, du, pos, inv_freq):
    L, d_attn = dq.shape
    d_kv = dkk_c.shape[1] // 2
    scale = 1.0 / math.sqrt(HEAD_DIM)

    def fold(cur, prev, i, nt):
        t = cur + jnp.where(i < nt - 1, prev, 0.0)
        out = []
        for c in range(t.shape[1] // (2 * LANES)):
            even, odd = t[:, 2 * c * LANES:(2 * c + 1) * LANES], t[:, (2 * c + 1) * LANES:(2 * c + 2) * LANES]
            even, odd = even + pltpu.roll(even, HEAD_DIM, 1), odd + pltpu.roll(odd, HEAD_DIM, 1)
            out.append(jnp.where(_lane(even.shape) < HEAD_DIM, even, odd))
        return out

    def fn(i, nt, dq_t, kc, kp, vc, vp, du_t, p, invf):
        cos, sin = _cos_sin(p, invf, True)
        dq_o = jnp.concatenate([_rot_chunk(c, cos, sin) for c in _chunks(dq_t)], axis=1) * scale
        dk_o = jnp.concatenate([_rot_chunk(c, cos, sin) for c in fold(kc, kp, i, nt)], axis=1)
        dv_o = jnp.concatenate(fold(vc, vp, i, nt), axis=1)
        return (jnp.concatenate([dq_o, dk_o, dv_o, du_t], axis=1),)

    return rowwise("qkv_grad", fn, L,
                   [full(dq), full(dkk_c), (dkk_p, 2 * d_kv, 0, 1), full(dvv_c), (dvv_p, 2 * d_kv, 0, 1), full(du), full(pos)],
                   [inv_freq], [(d_attn + 2 * d_kv + du.shape[1], bf16)], tr=ATTN_BLOCK)[0]


def _attn_specs(L):
    nb = L // ATTN_BLOCK
    B = ATTN_BLOCK
    q_spec = pl.BlockSpec((B, 2 * LANES), lambda h, n: (n, h))
    cur = pl.BlockSpec((B, LANES), lambda h, n: (n, h))
    prev = pl.BlockSpec((B, LANES), lambda h, n: (jnp.maximum(n - 1, 0), h))
    return nb, q_spec, cur, prev


def _attn_mask(n):
    B = ATTN_BLOCK
    row = lax.broadcasted_iota(jnp.int32, (B, 2 * B), 0)
    col = lax.broadcasted_iota(jnp.int32, (B, 2 * B), 1)
    return ((col < B) & (col > row) & (n > 0)) | ((col >= B) & (row >= col - B))


def _attn_probs(qm, kcat, sink, mask):
    s = jnp.where(mask, _dot(qm, kcat, NT), NEG)
    m = jnp.maximum(jnp.max(s, axis=1, keepdims=True), sink)
    p, ps = jnp.exp(s - m), jnp.exp(sink - m)
    inv = 1.0 / (jnp.sum(p, axis=1, keepdims=True) + ps)
    return p, inv, ps


def _attn_heads(q_ref, s_ref, h, q_per_kv):
    low = _lane((ATTN_BLOCK, LANES)) < HEAD_DIM
    heads = []
    for pr in range(q_per_kv // 2):
        q2 = q_ref[:, LANES * pr:LANES * (pr + 1)]
        for odd in (False, True):
            mine = low != odd
            sink = jnp.max(s_ref[pl.ds(h * q_per_kv + 2 * pr + int(odd), 1), :], axis=1, keepdims=True)
            heads.append((pr, mine, jnp.where(mine, q2, jnp.zeros_like(q2)), sink))
    return low, heads


def attn_fwd(qr, kk, vv, sink_b):
    L, d_attn = qr.shape
    nb, q_spec, cur, prev = _attn_specs(L)
    n_kv = kk.shape[1] // LANES
    q_per_kv = d_attn // HEAD_DIM // n_kv

    def body(q_ref, kc_ref, kp_ref, vc_ref, vp_ref, s_ref, o_ref):
        h, n = pl.program_id(0), pl.program_id(1)
        kcat = jnp.concatenate([kp_ref[...], kc_ref[...]], axis=0)
        vcat = jnp.concatenate([vp_ref[...], vc_ref[...]], axis=0)
        mask = _attn_mask(n)
        low, heads = _attn_heads(q_ref, s_ref, h, q_per_kv)
        probs = [_attn_probs(qm, kcat, sink, mask) for (_, _, qm, sink) in heads]
        outs = [_dot(p.astype(bf16), vcat) * inv for (p, inv, _) in probs]
        for pr in range(q_per_kv // 2):
            o_ref[:, LANES * pr:LANES * (pr + 1)] = jnp.where(low, outs[2 * pr], outs[2 * pr + 1])

    return pl.pallas_call(
        body, out_shape=jax.ShapeDtypeStruct((L, d_attn), f32), grid=(n_kv, nb),
        in_specs=[q_spec, cur, prev, cur, prev, pl.BlockSpec(sink_b.shape, lambda h, n: (0, 0))],
        out_specs=q_spec, name="attn_fwd", compiler_params=_params("parallel", "arbitrary"))(qr, kk, kk, vv, vv, sink_b)


def attn_bwd(qr, kk, vv, sink_b, attn, d_attn_out):
    L, d_attn = qr.shape
    nb, q_spec, cur, prev = _attn_specs(L)
    n_kv = kk.shape[1] // LANES
    q_per_kv = d_attn // HEAD_DIM // n_kv

    def body(q_ref, kc_ref, kp_ref, vc_ref, vp_ref, s_ref, o_ref, do_ref, dq_ref, dkc_ref, dkp_ref, dvc_ref, dvp_ref, ds_ref):
        h, n = pl.program_id(0), pl.program_id(1)
        B = ATTN_BLOCK
        kcat = jnp.concatenate([kp_ref[...], kc_ref[...]], axis=0)
        vcat = jnp.concatenate([vp_ref[...], vc_ref[...]], axis=0)
        mask = _attn_mask(n)
        low, heads = _attn_heads(q_ref, s_ref, h, q_per_kv)
        probs = [_attn_probs(qm, kcat, sink, mask) for (_, _, qm, sink) in heads]
        dk = jnp.zeros((2 * B, LANES), f32)
        dv = dk
        srow = lax.broadcasted_iota(jnp.int32, (SUBLANES, LANES), 0)
        dsink = jnp.zeros((SUBLANES, LANES), f32)
        dqs = []
        for i, ((pr, mine, qm, _), (p, inv, ps)) in enumerate(zip(heads, probs)):
            do2 = do_ref[:, LANES * pr:LANES * (pr + 1)]
            delta = jnp.sum(jnp.where(mine, do2 * o_ref[:, LANES * pr:LANES * (pr + 1)], 0.0), axis=1, keepdims=True)
            dob = jnp.where(mine, do2, 0.0).astype(bf16)
            p = p * inv
            ds = (p * (_dot(dob, vcat, NT) - delta)).astype(bf16)
            dqs.append(_dot(ds, kcat))
            dk = dk + _dot(ds, qm, TN)
            dv = dv + _dot(p.astype(bf16), dob, TN)
            dsink = dsink + jnp.where(srow == i, -jnp.sum(ps * inv * delta), 0.0)
        for pr in range(q_per_kv // 2):
            dq_ref[:, LANES * pr:LANES * (pr + 1)] = jnp.where(low, dqs[2 * pr], dqs[2 * pr + 1])
        dkp_ref[...] = dk[:B]
        dkc_ref[...] = dk[B:]
        dvp_ref[...] = dv[:B]
        dvc_ref[...] = dv[B:]

        @pl.when(n == 0)
        def _():
            ds_ref[...] = jnp.zeros_like(ds_ref)

        ds_ref[...] += dsink

    kv_shape = jax.ShapeDtypeStruct(kk.shape, f32)
    return pl.pallas_call(
        body,
        out_shape=[jax.ShapeDtypeStruct((L, d_attn), f32), kv_shape, kv_shape, kv_shape, kv_shape,
                   jax.ShapeDtypeStruct((n_kv, SUBLANES, LANES), f32)],
        grid=(n_kv, nb),
        in_specs=[q_spec, cur, prev, cur, prev, pl.BlockSpec(sink_b.shape, lambda h, n: (0, 0)), q_spec, q_spec],
        out_specs=[q_spec, cur, cur, cur, cur, pl.BlockSpec((None, SUBLANES, LANES), lambda h, n: (h, 0, 0))],
        name="attn_bwd", compiler_params=_params("parallel", "arbitrary"))(qr, kk, kk, vv, vv, sink_b, attn, d_attn_out)


SSM_T = 128
NQ = SUBLANES * SSM_STATE // LANES
NJ = SUBLANES


def _strided_put(ref, j, val):
    for q in range(NQ):
        ref.at[q][pl.ds(j, SSM_T, stride=NJ), :] = val[:, LANES * q:LANES * (q + 1)]


def _strided_get(ref, j):
    return jnp.concatenate([ref.at[q][pl.ds(j, SSM_T, stride=NJ), :] for q in range(NQ)], axis=1)


def _ssm_specs(L, rev):
    nt = L // SSM_T
    idx = (lambda i: nt - 1 - i) if rev else (lambda i: i)
    row = lambda w, cb=0: pl.BlockSpec((SSM_T, w), lambda i: (idx(i), cb))
    state = pl.BlockSpec((NQ, SSM_T * NJ, LANES), lambda i: (0, idx(i), 0))
    whole = lambda a: pl.BlockSpec(a.shape, lambda i: (0,) * a.ndim)
    return nt, row, state, whole


def ssm_fwd(u_bf, proj, u_cb, bd_re, bd_im, cd_re, cd_im, lam_re, lam_im, d_skip):
    L, d_ssm = u_bf.shape
    nt, row, state, whole = _ssm_specs(L, False)
    half = d_ssm // 2
    gw = d_ssm // NJ

    def body(u_ref, u0_ref, u1_ref, bdr, bdi, cdr, cdi, lr_ref, li_ref, d_ref, y_ref, z_ref, sr_ref, si_ref, carry):
        i = pl.program_id(0)

        @pl.when(i == 0)
        def _():
            carry[...] = jnp.zeros_like(carry)

        for j in range(NJ):
            uj = u_ref[:, gw * j:gw * (j + 1)]
            _strided_put(sr_ref, j, _dot(uj, bdr[j]))
            _strided_put(si_ref, j, _dot(uj, bdi[j]))
        lr = [lr_ref[q] for q in range(NQ)]
        li = [li_ref[q] for q in range(NQ)]

        def step(t, s):
            sr, si = s
            rows = pl.ds(pl.multiple_of(t * NJ, NJ), NJ)
            nr = tuple(lr[q] * sr[q] - li[q] * si[q] + sr_ref[q, rows, :] for q in range(NQ))
            ni = tuple(lr[q] * si[q] + li[q] * sr[q] + si_ref[q, rows, :] for q in range(NQ))
            for q in range(NQ):
                sr_ref[q, rows, :] = nr[q]
                si_ref[q, rows, :] = ni[q]
            return nr, ni

        init = (tuple(carry[0, q] for q in range(NQ)), tuple(carry[1, q] for q in range(NQ)))
        sr, si = lax.fori_loop(0, SSM_T, step, init, unroll=8)
        for q in range(NQ):
            carry[0, q] = sr[q]
            carry[1, q] = si[q]
        uf = jnp.concatenate([u0_ref[...], u1_ref[...]], axis=1)
        for j in range(NJ):
            cols = slice(gw * j, gw * (j + 1))
            yj = _dot(_strided_get(sr_ref, j).astype(bf16), cdr[j]) - _dot(_strided_get(si_ref, j).astype(bf16), cdi[j])
            yj = yj + d_ref[:, cols] * uf[:, cols]
            y_ref[:, cols] = yj
            z_ref[:, cols] = gelu(yj).astype(bf16)

    s_shape = jax.ShapeDtypeStruct((NQ, L * NJ, LANES), f32)
    consts = (bd_re, bd_im, cd_re, cd_im, lam_re, lam_im, d_skip)
    return pl.pallas_call(
        body, out_shape=[jax.ShapeDtypeStruct((L, d_ssm), f32), jax.ShapeDtypeStruct((L, d_ssm), bf16), s_shape, s_shape], grid=(nt,),
        in_specs=[row(d_ssm), row(half, u_cb), row(half, u_cb + 1)] + [whole(a) for a in consts],
        out_specs=[row(d_ssm), row(d_ssm), state, state],
        scratch_shapes=[pltpu.VMEM((2, NQ, NJ, LANES), f32)], name="ssm_fwd",
        compiler_params=_params("arbitrary"))(u_bf, proj, proj, *consts)


def ssm_bwd(y, dz1, dz2, u_bf, proj, u_cb, s_re, s_im, bd_re, bd_im, cd_re, cd_im, lam_re, lam_im, d_skip, dep=None):
    L, d_ssm = y.shape
    nt, row, state, whole = _ssm_specs(L, True)
    half = d_ssm // 2
    gw = d_ssm // NJ
    n_dep = 0 if dep is None else 1

    def body(y_ref, dz1_ref, dz2_ref, u_ref, u0_ref, u1_ref, sr_ref, si_ref, bdr, bdi, cdr, cdi, lr_ref, li_ref, d_ref, *rest):
        du_ref, dbdr, dbdi, dcdr, dcdi, dlr, dli, dd_ref, gr_ref, gi_ref, carry = rest[n_dep:]
        i = pl.program_id(0)

        @pl.when(i == 0)
        def _():
            carry[...] = jnp.zeros_like(carry)
            for r in (dbdr, dbdi, dcdr, dcdi, dlr, dli, dd_ref):
                r[...] = jnp.zeros_like(r)

        dyf = (dz1_ref[...] + dz2_ref[...]) * gelu_grad(y_ref[...])
        dyb = dyf.astype(bf16)
        for j in range(NJ):
            dyj = dyb[:, gw * j:gw * (j + 1)]
            _strided_put(gr_ref, j, _dot(dyj, cdr[j], NT))
            _strided_put(gi_ref, j, -_dot(dyj, cdi[j], NT))
            dcdr[j] += _dot(_strided_get(sr_ref, j).astype(bf16), dyj, TN)
            dcdi[j] -= _dot(_strided_get(si_ref, j).astype(bf16), dyj, TN)
        lr = [lr_ref[q] for q in range(NQ)]
        li = [li_ref[q] for q in range(NQ)]

        def step(k, c):
            gr, gi, ar, ai = c
            rows = pl.ds(pl.multiple_of((SSM_T - 1 - k) * NJ, NJ), NJ)
            s_r = [sr_ref[q, rows, :] for q in range(NQ)]
            s_i = [si_ref[q, rows, :] for q in range(NQ)]
            ar = tuple(ar[q] + gr[q] * s_r[q] + gi[q] * s_i[q] for q in range(NQ))
            ai = tuple(ai[q] + gi[q] * s_r[q] - gr[q] * s_i[q] for q in range(NQ))
            nr = tuple(gr_ref[q, rows, :] + lr[q] * gr[q] + li[q] * gi[q] for q in range(NQ))
            ni = tuple(gi_ref[q, rows, :] + lr[q] * gi[q] - li[q] * gr[q] for q in range(NQ))
            for q in range(NQ):
                gr_ref[q, rows, :] = nr[q]
                gi_ref[q, rows, :] = ni[q]
            return nr, ni, ar, ai

        zero = tuple(jnp.zeros((NJ, LANES), f32) for _ in range(NQ))
        init = (tuple(carry[0, q] for q in range(NQ)), tuple(carry[1, q] for q in range(NQ)), zero, zero)
        gr, gi, ar, ai = lax.fori_loop(0, SSM_T, step, init, unroll=8)
        for q in range(NQ):
            carry[0, q] = gr[q]
            carry[1, q] = gi[q]
            dlr[q] += ar[q]
            dli[q] += ai[q]
        uf = jnp.concatenate([u0_ref[...], u1_ref[...]], axis=1)
        dd_ref[...] += colsum(dyf * uf)
        for j in range(NJ):
            cols = slice(gw * j, gw * (j + 1))
            gjr, gji = _strided_get(gr_ref, j).astype(bf16), _strided_get(gi_ref, j).astype(bf16)
            du_ref[:, cols] = _dot(gjr, bdr[j], NT) + _dot(gji, bdi[j], NT) + d_ref[:, cols] * dyf[:, cols]
            uj = u_ref[:, cols]
            dbdr[j] += _dot(uj, gjr, TN)
            dbdi[j] += _dot(uj, gji, TN)

    consts = (bd_re, bd_im, cd_re, cd_im, lam_re, lam_im, d_skip)
    acc = lambda a: jax.ShapeDtypeStruct(a.shape, f32)
    outs = [jax.ShapeDtypeStruct((L, d_ssm), f32), acc(bd_re), acc(bd_im), acc(cd_re), acc(cd_im), acc(lam_re), acc(lam_im), acc(d_skip)]
    in_specs, operands, _ = _with_dep(
        [row(d_ssm)] * 4 + [row(half, u_cb), row(half, u_cb + 1), state, state] + [whole(a) for a in consts],
        [y, dz1, dz2, u_bf, proj, proj, s_re, s_im, *consts], dep)
    return pl.pallas_call(
        body, out_shape=outs, grid=(nt,),
        in_specs=in_specs, out_specs=[row(d_ssm)] + [whole(a) for a in consts],
        scratch_shapes=[pltpu.VMEM((NQ, SSM_T * NJ, LANES), f32), pltpu.VMEM((NQ, SSM_T * NJ, LANES), f32),
                        pltpu.VMEM((2, NQ, NJ, LANES), f32)],
        name="ssm_bwd", compiler_params=_params("arbitrary"))(*operands)


def _cmul(ar, ai, br, bi):
    return ar * br - ai * bi, ar * bi + ai * br


def _disc(ar, ai, logdt):
    dt = jnp.exp(logdt)
    mag = jnp.exp(ar * dt)
    lr, li = mag * jnp.cos(ai * dt), mag * jnp.sin(ai * dt)
    den = ar * ar + ai * ai
    nr, ni = lr - 1.0, li
    fr, fi = (nr * ar + ni * ai) / den, (ni * ar - nr * ai) / den
    return dt, lr, li, den, fr, fi


def ssm_params(a_re, a_im, logdt_b, bt_re, bt_im, spread):
    def body(ar_ref, ai_ref, ld_ref, br_ref, bi_ref, sp_ref, lr_ref, li_ref, or_ref, oi_ref):
        _, lr, li, _, fr, fi = _disc(ar_ref[...], ai_ref[...], ld_ref[...])
        lr_ref[...] = lr
        li_ref[...] = li
        fre = jnp.dot(sp_ref[...], fr, precision=HIGHEST, preferred_element_type=f32)
        fie = jnp.dot(sp_ref[...], fi, precision=HIGHEST, preferred_element_type=f32)
        o_r, o_i = _cmul(fre, fie, br_ref[...], bi_ref[...])
        or_ref[...] = o_r
        oi_ref[...] = o_i

    g = jax.ShapeDtypeStruct(a_re.shape, f32)
    b = jax.ShapeDtypeStruct(bt_re.shape, f32)
    return pl.pallas_call(body, out_shape=[g, g, b, b], name="ssm_params",
                          compiler_params=_params())(a_re, a_im, logdt_b, bt_re, bt_im, spread)


def ssm_params_grad(a_re, a_im, logdt_b, bt_re, bt_im, spread, gather, dlam_re, dlam_im, dbt_re, dbt_im):
    def body(ar_ref, ai_ref, ld_ref, br_ref, bi_ref, sp_ref, ga_ref, glr_ref, gli_ref, gbr_ref, gbi_ref,
             dar_ref, dai_ref, dld_ref, dbr_ref, dbi_ref):
        ar, ai = ar_ref[...], ai_ref[...]
        dt, lr, li, den, fr, fi = _disc(ar, ai, ld_ref[...])
        hdot = functools.partial(jnp.dot, precision=HIGHEST, preferred_element_type=f32)
        fre, fie = hdot(sp_ref[...], fr), hdot(sp_ref[...], fi)
        gbr, gbi, br, bi = gbr_ref[...], gbi_ref[...], br_ref[...], bi_ref[...]
        dbr_ref[...], dbi_ref[...] = _cmul(fre, -fie, gbr, gbi)
        t_r, t_i = _cmul(br, -bi, gbr, gbi)
        gfr, gfi = hdot(ga_ref[...], t_r), hdot(ga_ref[...], t_i)
        iwr, iwi = ar / den, -ai / den
        x_r, x_i = _cmul(iwr, -iwi, gfr, gfi)
        glr, gli = glr_ref[...] + x_r, gli_ref[...] + x_i
        q_r, q_i = _cmul(fr, fi, iwr, iwi)
        gwr, gwi = _cmul(-q_r, q_i, gfr, gfi)
        y_r, y_i = _cmul(dt * lr, -dt * li, glr, gli)
        dar_ref[...] = gwr + y_r
        dai_ref[...] = gwi + y_i
        wl_r, wl_i = _cmul(ar, ai, lr, li)
        z_r, _ = _cmul(wl_r, -wl_i, glr, gli)
        dld_ref[...] = jnp.sum(z_r * dt, axis=1, keepdims=True)

    g = jax.ShapeDtypeStruct(a_re.shape, f32)
    b = jax.ShapeDtypeStruct(bt_re.shape, f32)
    return pl.pallas_call(body, out_shape=[g, g, jax.ShapeDtypeStruct((a_re.shape[0], 1), f32), b, b], name="ssm_params_grad",
                          compiler_params=_params())(a_re, a_im, logdt_b, bt_re, bt_im, spread, gather, dlam_re, dlam_im, dbt_re, dbt_im)


def _block_diag(t, rows, cols):
    G = t.shape[0]
    t = t.reshape(G // NJ, NJ, rows, cols)
    eye = jnp.eye(NJ, dtype=t.dtype)
    return jnp.einsum('jgrc,gh->jgrhc', t, eye).reshape(G // NJ, NJ * rows, NJ * cols)


def _block_diag_take(m, rows, cols):
    J = m.shape[0]
    m = m.reshape(J, NJ, rows, NJ, cols)
    idx = jnp.arange(NJ)
    return m[:, idx, :, idx, :].transpose(1, 0, 2, 3).reshape(J * NJ, rows, cols)


def _state_layout(t):
    return t.reshape(NJ, NQ, LANES).transpose(1, 0, 2)


def _state_layout_inv(t, G, N):
    return t.transpose(1, 0, 2).reshape(G, N)


def _tiles2d(shape, budget_rows=128):
    rows, cols = shape
    tr = rows
    if rows > budget_rows:
        tr = budget_rows
        while rows % tr:
            tr -= SUBLANES
    return tr, cols


def _adam_update(w, g, m, v):
    c1 = 1.0 - ADAM_B1 ** ADAM_STEP
    c2 = 1.0 - ADAM_B2 ** ADAM_STEP
    nm = ADAM_B1 * m + (1.0 - ADAM_B1) * g
    nv = ADAM_B2 * v + (1.0 - ADAM_B2) * (g * g)
    delta = -ADAM_LR * ((nm / c1) / (jnp.sqrt(nv / c2) + ADAM_EPS) + ADAM_WD * w)
    return delta, nm, nv


def adamw_many(name, ws, gs, ms, vs):
    n = len(ws)

    def body(*refs):
        w, g, m, v = (refs[k * n:(k + 1) * n] for k in range(4))
        d, nm, nv = (refs[(4 + k) * n:(5 + k) * n] for k in range(3))
        for i in range(n):
            d[i][...], nm[i][...], nv[i][...] = _adam_update(w[i][...], g[i][...], m[i][...], v[i][...])

    o = [jax.ShapeDtypeStruct(a.shape, f32) for a in ws]
    outs = pl.pallas_call(body, out_shape=o * 3, name=name, compiler_params=_params())(*ws, *gs, *ms, *vs)
    return outs[:n], outs[n:2 * n], outs[2 * n:]


def adamw_halves(name, w, own, got, m, v, c_arr):
    h, cols = own.shape
    tr, _ = _tiles2d((h, cols), 128 if cols > 1024 else 256)
    nh = h // tr

    def body(c_ref, w_ref, own_ref, got_ref, m_ref, v_ref, g_ref, d_ref, nm_ref, nv_ref):
        mine = (pl.program_id(0) // nh) == c_ref[0]
        g = jnp.where(mine, own_ref[...], got_ref[...])
        g_ref[...] = g
        d_ref[...], nm_ref[...], nv_ref[...] = _adam_update(w_ref[...], g, m_ref[...], v_ref[...])

    spec = pl.BlockSpec((tr, cols), lambda i, c: (i, 0))
    own_spec = pl.BlockSpec((tr, cols), lambda i, c: (jnp.where(i // nh == c[0], i % nh, 0), 0))
    got_spec = pl.BlockSpec((tr, cols), lambda i, c: (jnp.where(i // nh == c[0], 0, i % nh), 0))
    o = jax.ShapeDtypeStruct(w.shape, f32)
    grid_spec = pltpu.PrefetchScalarGridSpec(num_scalar_prefetch=1, grid=(2 * nh,),
                                             in_specs=[spec, own_spec, got_spec, spec, spec], out_specs=[spec] * 4)
    return pl.pallas_call(body, out_shape=[o, o, o, o], grid_spec=grid_spec, name=name,
                          compiler_params=_params("arbitrary"))(c_arr, w, own, got, m, v)


def pair_sum(name, g, got, c_arr):
    S, h, cols = got.shape
    tr, _ = _tiles2d((h, cols), 1024)
    nh = h // tr

    def body(c_ref, g_ref, o_ref, out_ref):
        out_ref[...] = (g_ref[...].astype(f32) + o_ref[...].astype(f32)).astype(out_ref.dtype)

    spec = pl.BlockSpec((None, tr, cols), lambda s, i, c: (s, i, 0))
    grid_spec = pltpu.PrefetchScalarGridSpec(
        num_scalar_prefetch=1, grid=(S, nh),
        in_specs=[pl.BlockSpec((None, tr, cols), lambda s, i, c: (s, c[0] * nh + i, 0)), spec], out_specs=spec)
    return pl.pallas_call(body, out_shape=jax.ShapeDtypeStruct(got.shape, g.dtype), grid_spec=grid_spec, name=name,
                          compiler_params=_params("parallel", "parallel"))(c_arr, g, got)


def chip_sum(name, pair, landed, mine_arr, dep=None):
    n_in, h, cols = landed.shape
    tr, _ = _tiles2d((h, cols), 512)

    def body(s_ref, p_ref, l_ref, *rest):
        acc = p_ref[...].astype(f32)
        for k in range(n_in):
            acc = acc + l_ref[k].astype(f32)
        rest[-1][...] = acc

    in_specs, operands, _ = _with_dep(
        [pl.BlockSpec((None, tr, cols), lambda i, s: (s[0], i, 0)), pl.BlockSpec((n_in, tr, cols), lambda i, s: (0, i, 0))],
        [pair, landed], dep)
    grid_spec = pltpu.PrefetchScalarGridSpec(num_scalar_prefetch=1, grid=(h // tr,), in_specs=in_specs,
                                             out_specs=pl.BlockSpec((tr, cols), lambda i, s: (i, 0)))
    return pl.pallas_call(body, out_shape=jax.ShapeDtypeStruct((h, cols), f32), grid_spec=grid_spec, name=name,
                          compiler_params=_params("parallel"))(mine_arr, *operands)


def into_slot(name, w, slot_arr, n_slots, dtype, dep=None):
    tr, cols = _tiles2d(w.shape, 256)

    def body(s_ref, w_ref, *rest):
        rest[-1][...] = w_ref[...].astype(dtype)

    in_specs, operands, _ = _with_dep([pl.BlockSpec((tr, cols), lambda i, s: (i, 0))], [w], dep)
    grid_spec = pltpu.PrefetchScalarGridSpec(num_scalar_prefetch=1, grid=(w.shape[0] // tr,), in_specs=in_specs,
                                             out_specs=pl.BlockSpec((None, tr, cols), lambda i, s: (s[0], i, 0)))
    return pl.pallas_call(body, out_shape=jax.ShapeDtypeStruct((n_slots,) + w.shape, dtype), grid_spec=grid_spec, name=name,
                          compiler_params=_params("parallel"))(slot_arr, *operands)


def sum_slots(name, t):
    S, rows, cols = t.shape
    tr, _ = _tiles2d((rows, cols), 256)

    def body(t_ref, o_ref):
        acc = t_ref[0]
        for s in range(1, S):
            acc = acc + t_ref[s]
        o_ref[...] = acc

    return pl.pallas_call(body, out_shape=jax.ShapeDtypeStruct((rows, cols), f32), grid=(rows // tr,),
                          in_specs=[pl.BlockSpec((S, tr, cols), lambda i: (0, i, 0))], out_specs=pl.BlockSpec((tr, cols), lambda i: (i, 0)),
                          name=name, compiler_params=_params("parallel"))(t)


ANY = pl.BlockSpec(memory_space=pl.ANY)


def _place():
    x, y, c = lax.axis_index("x"), lax.axis_index("y"), lax.axis_index("c")
    return x, y, c


def _other_chips(x, y):
    return [(1 - x, y, 2 * (1 - x) + y), (x, 1 - y, 2 * x + 1 - y), (1 - x, 1 - y, 2 * (1 - x) + 1 - y)]


def gather_weights(bufs):
    nw = len(bufs)

    def body(*refs):
        dst = refs[nw:2 * nw]
        send1, recv1, send2, recv2 = refs[2 * nw:]
        x, y, c = _place()
        mine = 2 * x + y
        chips = _other_chips(x, y)
        sib = (x, y, 1 - c)
        first, passed = [], []
        for w in range(nw):
            h = dst[w].shape[1] // 2
            for k, (px, py, _) in enumerate(chips):
                half = dst[w].at[mine, pl.ds(c * h, h), :]
                cp = pltpu.make_async_remote_copy(src_ref=half, dst_ref=half,
                                                  send_sem=send1.at[w, k], recv_sem=recv1.at[w, k],
                                                  device_id=(px, py, c), device_id_type=MESH)
                cp.start()
                first.append(cp)
        for w in range(nw):
            h = dst[w].shape[1] // 2
            for k, (px, py, s) in enumerate(chips):
                landed = dst[w].at[s, pl.ds(c * h, h), :]
                pltpu.make_async_remote_copy(src_ref=landed, dst_ref=landed, send_sem=send1.at[w, k], recv_sem=recv1.at[w, k],
                                             device_id=(px, py, c), device_id_type=MESH).wait_recv()
                cp = pltpu.make_async_remote_copy(src_ref=landed, dst_ref=landed, send_sem=send2.at[w, k], recv_sem=recv2.at[w, k],
                                                  device_id=sib, device_id_type=MESH)
                cp.start()
                passed.append(cp)
        for w in range(nw):
            h = dst[w].shape[1] // 2
            for k, (px, py, s) in enumerate(chips):
                other = dst[w].at[s, pl.ds((1 - c) * h, h), :]
                pltpu.make_async_remote_copy(src_ref=other, dst_ref=other, send_sem=send2.at[w, k], recv_sem=recv2.at[w, k],
                                             device_id=sib, device_id_type=MESH).wait_recv()
        for cp in first + passed:
            cp.wait_send()

    sem = pltpu.SemaphoreType.DMA((nw, 3))
    return pl.pallas_call(
        body, out_shape=[jax.ShapeDtypeStruct(b.shape, b.dtype) for b in bufs],
        in_specs=[ANY] * nw, out_specs=[ANY] * nw, input_output_aliases={w: w for w in range(nw)},
        scratch_shapes=[sem, sem, sem, sem], name="gather_weights",
        compiler_params=pltpu.CompilerParams(has_side_effects=True))(*bufs)


def swap_halves(name, grads):
    nw = len(grads)

    def body(*refs):
        src, got = refs[:nw], refs[nw:2 * nw]
        send, recv = refs[2 * nw:]
        x, y, c = _place()
        cps = []
        for w in range(nw):
            h = src[w].shape[1] // 2
            cp = pltpu.make_async_remote_copy(src_ref=src[w].at[:, pl.ds((1 - c) * h, h), :], dst_ref=got[w],
                                              send_sem=send.at[w], recv_sem=recv.at[w],
                                              device_id=(x, y, 1 - c), device_id_type=MESH)
            cp.start()
            cps.append(cp)
        for cp in cps:
            cp.wait()

    half = [jax.ShapeDtypeStruct((g.shape[0], g.shape[1] // 2, g.shape[2]), g.dtype) for g in grads]
    sem = pltpu.SemaphoreType.DMA((nw,))
    return pl.pallas_call(
        body, out_shape=half, in_specs=[ANY] * nw, out_specs=[ANY] * nw,
        scratch_shapes=[sem, sem], name=name,
        compiler_params=pltpu.CompilerParams(has_side_effects=True))(*grads)


def scatter_to_owners(parts):
    nw = len(parts)

    def body(*refs):
        src, dst = refs[:nw], refs[nw:2 * nw]
        send, recv = refs[2 * nw:]
        x, y, c = _place()
        cps = []
        for w in range(nw):
            for k, (px, py, s) in enumerate(_other_chips(x, y)):
                cp = pltpu.make_async_remote_copy(src_ref=src[w].at[s], dst_ref=dst[w].at[k],
                                                  send_sem=send.at[w, k], recv_sem=recv.at[w, k],
                                                  device_id=(px, py, c), device_id_type=MESH)
                cp.start()
                cps.append(cp)
        for cp in cps:
            cp.wait()

    sem = pltpu.SemaphoreType.DMA((nw, 3))
    return pl.pallas_call(
        body, out_shape=[jax.ShapeDtypeStruct((N_CHIPS - 1,) + p.shape[1:], p.dtype) for p in parts],
        in_specs=[ANY] * nw, out_specs=[ANY] * nw,
        scratch_shapes=[sem, sem], name="scatter_to_owners",
        compiler_params=pltpu.CompilerParams(has_side_effects=True))(*parts)


def join_halves(name, halves):
    nw = len(halves)

    def body(*refs):
        src, dst = refs[:nw], refs[nw:2 * nw]
        send, recv = refs[2 * nw:]
        x, y, c = _place()
        cps = []
        for w in range(nw):
            cp = pltpu.make_async_remote_copy(src_ref=src[w], dst_ref=dst[w], send_sem=send.at[w], recv_sem=recv.at[w],
                                              device_id=(x, y, 1 - c), device_id_type=MESH)
            cp.start()
            cps.append(cp)
        for cp in cps:
            cp.wait()

    sem = pltpu.SemaphoreType.DMA((nw,))
    return pl.pallas_call(
        body, out_shape=[jax.ShapeDtypeStruct(p.shape, p.dtype) for p in halves], in_specs=[ANY] * nw, out_specs=[ANY] * nw,
        scratch_shapes=[sem, sem], name=name,
        compiler_params=pltpu.CompilerParams(has_side_effects=True))(*halves)


SEM = pl.BlockSpec(memory_space=pltpu.SEMAPHORE)
VM = pl.BlockSpec(memory_space=pltpu.VMEM)
DATAFLOW = pltpu.SideEffectType.DATAFLOW_SIDE_EFFECTING
TOKEN = jax.ShapeDtypeStruct((SUBLANES, LANES), f32)


def _gather_copy(buf, w, k, chip, c, mine, send, recv):
    px, py, _ = chip
    h = buf.shape[1] // 2
    half = buf.at[mine, pl.ds(c * h, h), :]
    return pltpu.make_async_remote_copy(src_ref=half, dst_ref=half, send_sem=send.at[3 * w + k], recv_sem=recv.at[3 * w + k],
                                        device_id=(px, py, c), device_id_type=MESH)


def _gather_landing(buf, w, k, chip, c, send, recv):
    px, py, s = chip
    h = buf.shape[1] // 2
    landed = buf.at[s, pl.ds(c * h, h), :]
    return pltpu.make_async_remote_copy(src_ref=landed, dst_ref=landed, send_sem=send.at[3 * w + k], recv_sem=recv.at[3 * w + k],
                                        device_id=(px, py, c), device_id_type=MESH)


def gather_start(name, bufs, groups, after):
    nw, ng = len(bufs), len(groups)

    def body(*refs):
        outs = refs[nw + 1:]
        sems, dst = outs[:2 * ng], outs[2 * ng:2 * ng + nw]
        token = outs[2 * ng + nw]
        x, y, c = _place()
        mine = 2 * x + y
        for g, members in enumerate(groups):
            for i, w in enumerate(members):
                for k, chip in enumerate(_other_chips(x, y)):
                    _gather_copy(dst[w], i, k, chip, c, mine, sems[2 * g], sems[2 * g + 1]).start()
        token[...] = jnp.zeros_like(token)

    sem_shapes = []
    for members in groups:
        sem_shapes += [pltpu.SemaphoreType.DMA((3 * len(members),))] * 2
    outs = pl.pallas_call(
        body, out_shape=sem_shapes + [jax.ShapeDtypeStruct(b.shape, b.dtype) for b in bufs] + [TOKEN],
        in_specs=[ANY] * (nw + 1), out_specs=[SEM] * (2 * ng) + [ANY] * nw + [VM],
        input_output_aliases={w: 2 * ng + w for w in range(nw)}, name=name,
        compiler_params=pltpu.CompilerParams(has_side_effects=DATAFLOW))(*bufs, after)
    return [(outs[2 * g], outs[2 * g + 1]) for g in range(ng)], list(outs[2 * ng:2 * ng + nw]), outs[2 * ng + nw]


def gather_wait(name, bufs, send, recv, after):
    nw = len(bufs)

    def body(*refs):
        src = refs[:nw]
        send_ref, recv_ref = refs[nw], refs[nw + 1]
        x, y, c = _place()
        mine = 2 * x + y
        for w in range(nw):
            for k, chip in enumerate(_other_chips(x, y)):
                _gather_copy(src[w], w, k, chip, c, mine, send_ref, recv_ref).wait_send()
                _gather_landing(src[w], w, k, chip, c, send_ref, recv_ref).wait_recv()

    return pl.pallas_call(
        body, out_shape=[jax.ShapeDtypeStruct(b.shape, b.dtype) for b in bufs],
        in_specs=[ANY] * nw + [SEM, SEM, ANY], out_specs=[ANY] * nw,
        input_output_aliases={w: w for w in range(nw)}, name=name,
        compiler_params=pltpu.CompilerParams(has_side_effects=DATAFLOW))(*bufs, send, recv, after)


def gather_forward(name, bufs):
    nw = len(bufs)

    def body(*refs):
        dst = refs[nw:2 * nw]
        send, recv = refs[2 * nw:]
        x, y, c = _place()
        sib = (x, y, 1 - c)
        cps = []
        for w in range(nw):
            h = dst[w].shape[1] // 2
            for k, (_, _, s) in enumerate(_other_chips(x, y)):
                landed = dst[w].at[s, pl.ds(c * h, h), :]
                cp = pltpu.make_async_remote_copy(src_ref=landed, dst_ref=landed, send_sem=send.at[w, k], recv_sem=recv.at[w, k],
                                                  device_id=sib, device_id_type=MESH)
                cp.start()
                cps.append(cp)
        for w in range(nw):
            h = dst[w].shape[1] // 2
            for k, (_, _, s) in enumerate(_other_chips(x, y)):
                other = dst[w].at[s, pl.ds((1 - c) * h, h), :]
                pltpu.make_async_remote_copy(src_ref=other, dst_ref=other, send_sem=send.at[w, k], recv_sem=recv.at[w, k],
                                             device_id=sib, device_id_type=MESH).wait_recv()
        for cp in cps:
            cp.wait_send()

    sem = pltpu.SemaphoreType.DMA((nw, 3))
    return pl.pallas_call(
        body, out_shape=[jax.ShapeDtypeStruct(b.shape, b.dtype) for b in bufs],
        in_specs=[ANY] * nw, out_specs=[ANY] * nw, input_output_aliases={w: w for w in range(nw)},
        scratch_shapes=[sem, sem], name=name, compiler_params=pltpu.CompilerParams(has_side_effects=True))(*bufs)


def _scatter_copy(src, dst, w, k, chip, c, send, recv):
    px, py, s = chip
    return pltpu.make_async_remote_copy(src_ref=src.at[s], dst_ref=dst.at[k], send_sem=send.at[3 * w + k], recv_sem=recv.at[3 * w + k],
                                        device_id=(px, py, c), device_id_type=MESH)


def scatter_start(name, parts):
    nw = len(parts)
    lands = [pltpu.with_memory_space_constraint(lax.empty((N_CHIPS - 1,) + p.shape[1:], p.dtype), pltpu.HBM) for p in parts]

    def body(*refs):
        outs = refs[2 * nw:]
        send, recv = outs[0], outs[1]
        src, dst, token = outs[2:2 + nw], outs[2 + nw:2 + 2 * nw], outs[2 + 2 * nw]
        x, y, c = _place()
        for w in range(nw):
            for k, chip in enumerate(_other_chips(x, y)):
                _scatter_copy(src[w], dst[w], w, k, chip, c, send, recv).start()
        token[...] = jnp.zeros_like(token)

    sem = pltpu.SemaphoreType.DMA((3 * nw,))
    outs = pl.pallas_call(
        body, out_shape=[sem, sem] + [jax.ShapeDtypeStruct(p.shape, p.dtype) for p in parts]
        + [jax.ShapeDtypeStruct(l.shape, l.dtype) for l in lands] + [TOKEN],
        in_specs=[ANY] * (2 * nw), out_specs=[SEM, SEM] + [ANY] * (2 * nw) + [VM],
        input_output_aliases={i: 2 + i for i in range(2 * nw)}, name=name,
        compiler_params=pltpu.CompilerParams(has_side_effects=DATAFLOW))(*parts, *lands)
    return outs[0], outs[1], list(outs[2:2 + nw]), list(outs[2 + nw:2 + 2 * nw]), outs[2 + 2 * nw]


def scatter_wait(name, parts, lands, send, recv, after):
    nw = len(parts)

    def body(*refs):
        src, dst = refs[:nw], refs[nw:2 * nw]
        send_ref, recv_ref = refs[2 * nw], refs[2 * nw + 1]
        x, y, c = _place()
        for w in range(nw):
            for k, chip in enumerate(_other_chips(x, y)):
                cp = _scatter_copy(src[w], dst[w], w, k, chip, c, send_ref, recv_ref)
                cp.wait_send()
                cp.wait_recv()

    outs = pl.pallas_call(
        body, out_shape=[jax.ShapeDtypeStruct(a.shape, a.dtype) for a in list(parts) + list(lands)],
        in_specs=[ANY] * (2 * nw) + [SEM, SEM, ANY], out_specs=[ANY] * (2 * nw),
        input_output_aliases={i: i for i in range(2 * nw)}, name=name,
        compiler_params=pltpu.CompilerParams(has_side_effects=DATAFLOW))(*parts, *lands, send, recv, after)
    return list(outs[:nw]), list(outs[nw:])


def _sibling_copy(src, dst, w, c, half_rows, send, recv, sib):
    if half_rows:
        h = src.shape[1] // 2
        src = src.at[:, pl.ds((1 - c) * h, h), :]
    return pltpu.make_async_remote_copy(src_ref=src, dst_ref=dst, send_sem=send.at[w], recv_sem=recv.at[w],
                                        device_id=sib, device_id_type=MESH)


def _landing(shape, dtype):
    return pltpu.with_memory_space_constraint(lax.empty(shape, dtype), pltpu.HBM)


def sibling_start(name, srcs, half_rows):
    nw = len(srcs)
    lands = [_landing((s.shape[0], s.shape[1] // 2, s.shape[2]) if half_rows else s.shape, s.dtype) for s in srcs]

    def body(*refs):
        outs = refs[2 * nw:]
        send, recv = outs[0], outs[1]
        src, dst, token = outs[2:2 + nw], outs[2 + nw:2 + 2 * nw], outs[2 + 2 * nw]
        x, y, c = _place()
        for w in range(nw):
            _sibling_copy(src[w], dst[w], w, c, half_rows, send, recv, (x, y, 1 - c)).start()
        token[...] = jnp.zeros_like(token)

    sem = pltpu.SemaphoreType.DMA((nw,))
    outs = pl.pallas_call(
        body, out_shape=[sem, sem] + [jax.ShapeDtypeStruct(a.shape, a.dtype) for a in list(srcs) + lands] + [TOKEN],
        in_specs=[ANY] * (2 * nw), out_specs=[SEM, SEM] + [ANY] * (2 * nw) + [VM],
        input_output_aliases={i: 2 + i for i in range(2 * nw)}, name=name,
        compiler_params=pltpu.CompilerParams(has_side_effects=DATAFLOW))(*srcs, *lands)
    return outs[0], outs[1], list(outs[2:2 + nw]), list(outs[2 + nw:2 + 2 * nw]), outs[2 + 2 * nw]


def sibling_wait(name, srcs, lands, send, recv, half_rows, after):
    nw = len(srcs)

    def body(*refs):
        src, dst = refs[:nw], refs[nw:2 * nw]
        send_ref, recv_ref = refs[2 * nw], refs[2 * nw + 1]
        x, y, c = _place()
        for w in range(nw):
            cp = _sibling_copy(src[w], dst[w], w, c, half_rows, send_ref, recv_ref, (x, y, 1 - c))
            cp.wait_send()
            cp.wait_recv()

    outs = pl.pallas_call(
        body, out_shape=[jax.ShapeDtypeStruct(a.shape, a.dtype) for a in list(srcs) + list(lands)],
        in_specs=[ANY] * (2 * nw) + [SEM, SEM, ANY], out_specs=[ANY] * (2 * nw),
        input_output_aliases={i: i for i in range(2 * nw)}, name=name,
        compiler_params=pltpu.CompilerParams(has_side_effects=DATAFLOW))(*srcs, *lands, send, recv, after)
    return list(outs[:nw]), list(outs[nw:])


def _peer(x, y, c, r):
    return (1 - x if r & 4 else x, 1 - y if r & 2 else y, 1 - c if r & 1 else c)


def _everyone_copy(buf, r, x, y, c, send, recv, landing):
    px, py, pc = _peer(x, y, c, r)
    slot = buf.at[4 * px + 2 * py + pc] if landing else buf.at[4 * x + 2 * y + c]
    return pltpu.make_async_remote_copy(src_ref=slot, dst_ref=slot, send_sem=send.at[r - 1], recv_sem=recv.at[r - 1],
                                        device_id=(px, py, pc), device_id_type=MESH)


def everyone_start(name, buf):
    def body(buf_in, send, recv, buf_ref, token):
        x, y, c = _place()
        for r in range(1, N_DEV):
            _everyone_copy(buf_ref, r, x, y, c, send, recv, False).start()
        token[...] = jnp.zeros_like(token)

    sem = pltpu.SemaphoreType.DMA((N_DEV - 1,))
    return pl.pallas_call(
        body, out_shape=[sem, sem, jax.ShapeDtypeStruct(buf.shape, buf.dtype), TOKEN],
        in_specs=[ANY], out_specs=[SEM, SEM, ANY, VM], input_output_aliases={0: 2}, name=name,
        compiler_params=pltpu.CompilerParams(has_side_effects=DATAFLOW))(buf)


def everyone_wait(name, buf, send, recv, after):
    def body(buf_ref, send_ref, recv_ref, after_ref, out_ref):
        x, y, c = _place()
        for r in range(1, N_DEV):
            _everyone_copy(buf_ref, r, x, y, c, send_ref, recv_ref, False).wait_send()
            _everyone_copy(buf_ref, r, x, y, c, send_ref, recv_ref, True).wait_recv()

    return pl.pallas_call(
        body, out_shape=jax.ShapeDtypeStruct(buf.shape, buf.dtype), in_specs=[ANY, SEM, SEM, ANY], out_specs=ANY,
        input_output_aliases={0: 0}, name=name,
        compiler_params=pltpu.CompilerParams(has_side_effects=DATAFLOW))(buf, send, recv, after)


def all_reduce_small(v):
    R, n = v.shape

    def body(v_ref, o_ref, all_ref, send_sems, recv_sems, local_sem):
        x, y, c = _place()
        me, sib = (x, y, c), (x, y, 1 - c)
        chips = [(1 - x, y), (x, 1 - y), (1 - x, 1 - y)]

        def rows(px, py, pc):
            return all_ref.at[pl.ds((4 * px + 2 * py + pc) * R, R), :]

        def copy(k, block, to, src=None):
            return pltpu.make_async_remote_copy(src_ref=rows(*block) if src is None else src, dst_ref=rows(*block),
                                                send_sem=send_sems.at[k], recv_sem=recv_sems.at[k],
                                                device_id=to, device_id_type=MESH)

        own = pltpu.make_async_copy(v_ref, rows(*me), local_sem)
        own.start()
        first = [copy(0, me, sib, src=v_ref)]
        first += [copy(1 + j, me, (*chip, c), src=v_ref) for j, chip in enumerate(chips)]
        for cp in first:
            cp.start()
        passed = [copy(4 + j, (*chip, c), sib) for j, chip in enumerate(chips)]
        for j, chip in enumerate(chips):
            copy(1 + j, (*chip, c), me).wait_recv()
            passed[j].start()
        copy(0, sib, me).wait_recv()
        for j, chip in enumerate(chips):
            copy(4 + j, (*chip, 1 - c), me).wait_recv()
        for cp in first + passed:
            cp.wait_send()
        own.wait()
        acc = all_ref[pl.ds(0, R), :]
        for d in range(1, N_DEV):
            acc = acc + all_ref[pl.ds(d * R, R), :]
        o_ref[...] = acc

    vm = pl.BlockSpec(memory_space=pltpu.VMEM)
    return pl.pallas_call(
        body, out_shape=jax.ShapeDtypeStruct((R, n), f32), in_specs=[vm], out_specs=vm,
        scratch_shapes=[pltpu.VMEM((N_DEV * R, n), f32), pltpu.SemaphoreType.DMA((7,)), pltpu.SemaphoreType.DMA((7,)),
                        pltpu.SemaphoreType.DMA],
        name="all_reduce_small", compiler_params=pltpu.CompilerParams(vmem_limit_bytes=VMEM_LIMIT, has_side_effects=True))(v)


def local_step(x, pos, tgt, small, d_in, get_w, put_g, first_dep=None):
    L, D = x.shape
    d_kv = N_KV_HEADS * HEAD_DIM
    d_ssm = small["d_skip"].shape[1]
    d_attn = d_in - 2 * d_kv - d_ssm
    big = {}
    G = d_ssm // SSM_GROUP
    N, P = SSM_STATE, SSM_GROUP
    gbf = bf16

    half_dim = HEAD_DIM // 2
    inv_freq = ROPE_THETA ** (-jnp.arange(half_dim, dtype=f32) / half_dim)
    inv_freq = jnp.tile(inv_freq, LANES // half_dim).reshape(1, LANES)
    sink_b = jnp.broadcast_to(small["sinks"].reshape(-1, 1), (small["sinks"].size, LANES))

    spread = jnp.repeat(jnp.eye(G, dtype=f32), P, axis=0)
    logdt_b = jnp.broadcast_to(small["log_dt"].reshape(G, 1), (G, N))
    bt_re = small["b_re"].reshape(G, N, P).transpose(0, 2, 1).reshape(G * P, N)
    bt_im = small["b_im"].reshape(G, N, P).transpose(0, 2, 1).reshape(G * P, N)
    a_re, a_im = small["a_re"].reshape(G, N), small["a_im"].reshape(G, N)
    lam_re, lam_im, bbt_re, bbt_im = ssm_params(a_re, a_im, logdt_b, bt_re, bt_im, spread)
    bd_re = _block_diag(bbt_re.reshape(G, P, N), P, N).astype(bf16)
    bd_im = _block_diag(bbt_im.reshape(G, P, N), P, N).astype(bf16)
    c_re = small["c_re"].reshape(G, P, N).transpose(0, 2, 1)
    c_im = small["c_im"].reshape(G, P, N).transpose(0, 2, 1)
    cd_re = _block_diag(c_re, N, P).astype(bf16)
    cd_im = _block_diag(c_im, N, P).astype(bf16)
    lam_re_l, lam_im_l = _state_layout(lam_re), _state_layout(lam_im)

    def k1(i, nt, xt, g):
        return (rms_fwd(xt, g),)
    xn = rowwise("pre_mix_norm", k1, L, [full(x)], [small["g_pre_mix"]], [(D, bf16)], dep=first_dep)[0]
    big["w_in"] = get_w("w_in", xn)
    proj = mm_nn("proj_in", xn, big["w_in"])
    qr, kk, vv, u_bf = qkv_prep(proj, pos, inv_freq, d_attn, d_kv)
    attn = attn_fwd(qr, kk, vv, sink_b)
    u_cb = (d_attn + 2 * d_kv) // (d_ssm // 2)
    y, z_bf, s_re, s_im = ssm_fwd(u_bf, proj, u_cb, bd_re, bd_im, cd_re, cd_im, lam_re_l, lam_im_l, small["d_skip"])
    big["w_glu"] = get_w("w_glu", z_bf)
    gl = mm_nn("glu_proj", z_bf, big["w_glu"])

    def k6(i, nt, at, yt, glt, bg, ga, gs):
        ssm = gelu(yt) * sigmoid(glt + bg)
        return (jnp.concatenate([rms_fwd(at, ga), rms_fwd(ssm, gs)], axis=1),)
    mixed = rowwise("mix_norms", k6, L, [full(attn), full(y), full(gl)],
                    [small["b_glu"], small["g_attn_out"], small["g_ssm_out"]], [(d_attn + d_ssm, bf16)])[0]
    big["w_o"] = get_w("w_o", mixed)
    mix = mm_nn("proj_out", mixed, big["w_o"])

    def k7(i, nt, xt, mt, gpm, gpf):
        h = xt + rms_fwd(mt, gpm)
        return h, rms_fwd(h, gpf)
    h, hn = rowwise("post_mix", k7, L, [full(x), full(mix)], [small["g_post_mix"], small["g_pre_ffn"]], [(D, f32), (D, bf16)])
    big["w_gate"] = get_w("w_gate", hn)
    big["w_up"] = get_w("w_up", hn)
    gt, up, hid = ffn_hidden(hn, big["w_gate"], big["w_up"])
    d_ff_dim = gt.shape[1]
    big["w_down"] = get_w("w_down", hid)
    ff = mm_nn("ffn_down", hid, big["w_down"], tk=d_ff_dim // 2)

    def k9(i, nt, ht, fft, tt, g):
        out = ht + rms_fwd(fft, g)
        err = out - tt
        per_row = jnp.mean(err * err, axis=-1, keepdims=True)
        loss = 0.5 * jnp.sum(per_row) * jnp.where(_lane((1, LANES)) == 0, 1.0, 0.0)
        d_out = err * (1.0 / D)
        d_ff, dg = rms_bwd(fft, g, d_out)
        return d_out, d_ff, dg, loss
    d_out, d_ff, dg_post_ffn, loss = rowwise("loss_head", k9, L, [full(h), full(ff), full(tgt)], [small["g_post_ffn"]],
                                             [(D, f32), (D, bf16)], reds=[D, LANES])

    d_gt, d_up = ffn_hidden_grad(d_ff, big["w_down"], gt, up)
    token = put_g("w_down", mm_tn("dw_down", hid, d_ff, out_dtype=gbf, tm=d_ff_dim // N_CHIPS))
    d_hn = mm_nt_pair("d_hn", d_gt, big["w_gate"], d_up, big["w_up"], dep=token)
    token = put_g("w_gate", mm_tn("dw_gate", hn, d_gt, shards=N_CHIPS, out_dtype=gbf))
    token = put_g("w_up", mm_tn("dw_up", hn, d_up, shards=N_CHIPS, out_dtype=gbf, dep=token))

    def k11(i, nt, ht, da, do, mt, gpf, gpm):
        dh_n, dg_pf = rms_bwd(ht, gpf, da)
        dh = do + dh_n
        d_mix, dg_pm = rms_bwd(mt, gpm, dh)
        return dh, d_mix, dg_pf, dg_pm
    dh, d_mix, dg_pre_ffn, dg_post_mix = rowwise("post_mix_grad", k11, L, [full(h), full(d_hn), full(d_out), full(mix)],
                                                 [small["g_pre_ffn"], small["g_post_mix"]], [(D, f32), (D, bf16)], reds=[D, D], dep=token)
    d_mixed = mm_nt("d_mixed", d_mix, big["w_o"])
    token = put_g("w_o", mm_tn("dw_o", mixed, d_mix, out_dtype=gbf))

    def k12(i, nt, at, yt, glt, da_n, ds_n, bg, ga, gs):
        z = gelu(yt)
        sg = sigmoid(glt + bg)
        ssm = z * sg
        d_at, dga = rms_bwd(at, ga, da_n)
        d_ssm_t, dgs = rms_bwd(ssm, gs, ds_n)
        d_gl = d_ssm_t * z * sg * (1.0 - sg)
        return d_at, d_ssm_t * sg, d_gl, dga, dgs, colsum(d_gl)
    d_attn_o, dz1, d_gl, dg_attn, dg_ssm, db_glu = rowwise(
        "mix_norms_grad", k12, L, [full(attn), full(y), full(gl), (d_mixed, d_attn, 0, 0), (d_mixed, d_ssm, d_attn // d_ssm, 0)],
        [small["b_glu"], small["g_attn_out"], small["g_ssm_out"]], [(d_attn, f32), (d_ssm, f32), (d_ssm, bf16)],
        reds=[d_attn, d_ssm, d_ssm], dep=token)
    dz2 = mm_nt("d_glu_in", d_gl, big["w_glu"])
    token = put_g("w_glu", mm_tn("dw_glu", z_bf, d_gl, out_dtype=gbf))

    du, dbd_re, dbd_im, dcd_re, dcd_im, dlam_re_l, dlam_im_l, dd_skip = ssm_bwd(
        y, dz1, dz2, u_bf, proj, u_cb, s_re, s_im, bd_re, bd_im, cd_re, cd_im, lam_re_l, lam_im_l, small["d_skip"], dep=token)
    dq, dkk_c, dkk_p, dvv_c, dvv_p, dsink = attn_bwd(qr, kk, vv, sink_b, attn, d_attn_o)
    d_proj = qkv_grad(dq, dkk_c, dkk_p, dvv_c, dvv_p, du, pos, inv_freq)
    d_xn = mm_nt("d_xn", d_proj, big["w_in"])
    token = put_g("w_in", mm_tn("dw_in", xn, d_proj, shards=N_CHIPS, out_dtype=gbf))

    def k17(i, nt, xt, dxn, dht, g):
        dx, dg = rms_bwd(xt, g, dxn)
        return dht + dx, dg
    grad_x, dg_pre_mix = rowwise("pre_mix_grad", k17, L, [full(x), full(d_xn), full(dh)], [small["g_pre_mix"]],
                                 [(D, f32)], reds=[D], dep=token)

    gather = spread.T
    dbbt_re = _block_diag_take(dbd_re, P, N).reshape(G * P, N)
    dbbt_im = _block_diag_take(dbd_im, P, N).reshape(G * P, N)
    d_a_re, d_a_im, d_logdt, dbt_re, dbt_im = ssm_params_grad(
        a_re, a_im, logdt_b, bt_re, bt_im, spread, gather,
        _state_layout_inv(dlam_re_l, G, N), _state_layout_inv(dlam_im_l, G, N), dbbt_re, dbbt_im)
    q_per_kv = d_attn // HEAD_DIM // N_KV_HEADS
    small_grads = {
        "g_pre_mix": dg_pre_mix, "sinks": dsink[:, :q_per_kv, 0].reshape(1, -1),
        "a_re": d_a_re, "a_im": d_a_im, "log_dt": d_logdt.reshape(1, G),
        "b_re": dbt_re.reshape(G, P, N).transpose(0, 2, 1), "b_im": dbt_im.reshape(G, P, N).transpose(0, 2, 1),
        "c_re": _block_diag_take(dcd_re, N, P).transpose(0, 2, 1), "c_im": _block_diag_take(dcd_im, N, P).transpose(0, 2, 1),
        "d_skip": dd_skip, "b_glu": db_glu, "g_attn_out": dg_attn, "g_ssm_out": dg_ssm,
        "g_post_mix": dg_post_mix, "g_pre_ffn": dg_pre_ffn, "g_post_ffn": dg_post_ffn,
    }
    return loss, grad_x, small_grads


WEIGHTS = ['g_pre_mix', 'w_in', 'sinks', 'a_re', 'a_im', 'log_dt', 'b_re', 'b_im', 'c_re', 'c_im', 'd_skip', 'w_glu', 'b_glu',
           'g_attn_out', 'g_ssm_out', 'w_o', 'g_post_mix', 'g_pre_ffn', 'w_gate', 'w_up', 'w_down', 'g_post_ffn']
BIG = ['w_in', 'w_glu', 'w_o', 'w_gate', 'w_up', 'w_down']
COL_SHARDED = ['w_in', 'w_gate', 'w_up']
SMALL = [n for n in WEIGHTS if n not in BIG]
GATHER_GROUPS = [["w_in"], ["w_glu", "w_o"], ["w_gate", "w_up"], ["w_down"]]
REDUCE_GROUPS = [["w_down", "w_gate", "w_up"], ["w_o", "w_glu", "w_in"]]


PACK_ROWS = 256


def _pack(parts):
    flat = jnp.concatenate([p.reshape(-1) for p in parts])
    pad = (-flat.size) % (PACK_ROWS * LANES)
    return jnp.pad(flat, (0, pad)).reshape(-1, LANES)


def _unpack(packed, shapes):
    flat = packed.reshape(-1)
    out, off = [], 0
    for s in shapes:
        n = int(np.prod(s))
        out.append(flat[off:off + n].reshape(s))
        off += n
    return out


def kernel(x, positions, g_pre_mix, w_in, sinks, a_re, a_im, log_dt, b_re, b_im, c_re, c_im, d_skip, w_glu, b_glu, g_attn_out, g_ssm_out, w_o, g_post_mix, g_pre_ffn, w_gate, w_up, w_down, g_post_ffn, loss_target, m_g_pre_mix, m_w_in, m_sinks, m_a_re, m_a_im, m_log_dt, m_b_re, m_b_im, m_c_re, m_c_im, m_d_skip, m_w_glu, m_b_glu, m_g_attn_out, m_g_ssm_out, m_w_o, m_g_post_mix, m_g_pre_ffn, m_w_gate, m_w_up, m_w_down, m_g_post_ffn, v_g_pre_mix, v_w_in, v_sinks, v_a_re, v_a_im, v_log_dt, v_b_re, v_b_im, v_c_re, v_c_im, v_d_skip, v_w_glu, v_b_glu, v_g_attn_out, v_g_ssm_out, v_w_o, v_g_post_mix, v_g_pre_ffn, v_w_gate, v_w_up, v_w_down, v_g_post_ffn):
    args = dict(locals())
    w = {n: args[n] for n in WEIGHTS}
    m = {n: args["m_" + n] for n in WEIGHTS}
    v = {n: args["v_" + n] for n in WEIGHTS}
    L, D = x.shape[1], x.shape[2]

    ax, ay, ac = _place()
    mine_arr = (2 * ax + ay).astype(jnp.int32).reshape(1)
    c_arr = ac.astype(jnp.int32).reshape(1)

    me_arr = (4 * ax + 2 * ay + ac).astype(jnp.int32).reshape(1)

    bufs = {"w_in": into_slot("cast_w_in", w["w_in"][0], mine_arr, N_CHIPS, bf16)}
    sems, (bufs["w_in"],), token = gather_start("gather_start_in", [bufs["w_in"]], [[0]], mine_arr)
    later = [n for n in BIG if n != "w_in"]
    for n in later:
        bufs[n] = into_slot("cast_" + n, w[n][0], mine_arr, N_CHIPS, bf16, dep=token)
    more, started, token = gather_start("gather_start_rest", [bufs[n] for n in later],
                                        [[later.index(n) for n in g] for g in GATHER_GROUPS[1:]], token)
    sems += more
    bufs.update(zip(later, started))
    ready = set()

    def get_w(n, after):
        if n not in ready:
            gi = [n in g for g in GATHER_GROUPS].index(True)
            members = GATHER_GROUPS[gi]
            landed = gather_wait("gather_wait_%d" % gi, [bufs[k] for k in members], *sems[gi], after)
            bufs.update(zip(members, gather_forward("gather_forward_%d" % gi, landed)))
            ready.update(members)
        g = bufs[n]
        return g if n in COL_SHARDED else g.reshape(g.shape[0] * g.shape[1], g.shape[2])

    swaps, inflight = {}, []

    def put_g(n, g):
        g3 = g if n in COL_SHARDED else g.reshape(N_CHIPS, g.shape[0] // N_CHIPS, g.shape[1])
        swaps[n] = sibling_start("swap_start_" + n, [g3], True)
        for gi, members in enumerate(REDUCE_GROUPS):
            if n == members[-1]:
                last = swaps[n][4]
                pair = []
                for k in members:
                    send, recv, srcs, lands, _ = swaps[k]
                    (src,), (got,) = sibling_wait("swap_wait_" + k, srcs, lands, send, recv, True, last)
                    pair.append(pair_sum("pair_sum_" + k, src, got, c_arr))
                send, recv, parts, lands, tok = scatter_start("scatter_start_%d" % gi, pair)
                inflight.append((members, send, recv, parts, lands))
                return tok
        return swaps[n][4]

    small = {n: w[n].reshape(1, -1) for n in SMALL}
    pos = positions.reshape(L, 1).astype(f32)
    d_in = N_CHIPS * w["w_in"].shape[2]
    loss, grad_x, small_grads = local_step(x[0], pos, loss_target[0], small, d_in, get_w, put_g, first_dep=token)

    shapes = [w[n].shape for n in SMALL]
    blocks = into_slot("small_block", _pack([small_grads[n] for n in SMALL] + [loss]), me_arr, N_DEV, f32)
    small_send, small_recv, blocks, after = everyone_start("small_start", blocks)

    grads, delta, new_m, new_v = {}, {}, {}, {}
    for gi, (members, send, recv, parts, lands) in enumerate(inflight):
        parts, landed = scatter_wait("scatter_wait_%d" % gi, parts, lands, send, recv, after)
        joins, dep = [], None
        for k, p, t in zip(members, parts, landed):
            joins.append(sibling_start("join_start_" + k, [chip_sum("chip_sum_" + k, p, t, mine_arr, dep=dep)], False))
            dep = after = joins[-1][4]
        for n, (send, recv, srcs, lands, _) in zip(members, joins):
            (own,), (sib,) = sibling_wait("join_wait_" + n, srcs, lands, send, recv, False, after)
            g_, d_, m_, v_ = adamw_halves("adamw_" + n, w[n][0], own, sib, m[n][0], v[n][0], c_arr)
            grads[n], delta[n], new_m[n], new_v[n] = g_[None], d_[None], m_[None], v_[None]
            after = v_
    blocks = everyone_wait("small_wait", blocks, small_send, small_recv, after)
    small_sum = sum_slots("small_sum", blocks)
    *small_g, loss = _unpack(small_sum, shapes + [loss.shape])
    loss = loss[0, 0]
    grads.update(zip(SMALL, small_g))
    view = lambda a: a.reshape(-1, a.shape[-1])
    outs = adamw_many("adamw_small", *[[view(t[n]) for n in SMALL] for t in (w, grads, m, v)])
    for t, parts in zip((delta, new_m, new_v), outs):
        t.update({n: p.reshape(w[n].shape) for n, p in zip(SMALL, parts)})

    return (loss, grad_x[None], *[grads[n] for n in WEIGHTS], *[delta[n] for n in WEIGHTS],
            *[new_m[n] for n in WEIGHTS], *[new_v[n] for n in WEIGHTS])
```

```python
import functools
import math

import jax
import jax.numpy as jnp
import numpy as np
from jax import lax
from jax.experimental import pallas as pl
from jax.experimental.pallas import tpu as pltpu

f32 = jnp.float32
bf16 = jnp.bfloat16
HIGHEST = lax.Precision.HIGHEST
MESH = pl.DeviceIdType.MESH

HEAD_DIM = 64
N_KV_HEADS = 4
ATTN_BLOCK = 128
ROPE_THETA = 10000.0
SSM_GROUP = 16
SSM_STATE = 64
RMS_EPS = 1e-6
LANES = 128
SUBLANES = 8
VMEM_LIMIT = 52 * 1024 * 1024
N_CHIPS = 4
N_DEV = 8
NEG = -1e30

ADAM_LR, ADAM_B1, ADAM_B2, ADAM_EPS, ADAM_WD, ADAM_STEP = 0.001, 0.9, 0.999, 1e-08, 0.01, 10

NN = (((1,), (0,)), ((), ()))
NT = (((1,), (1,)), ((), ()))
TN = (((0,), (0,)), ((), ()))


def _params(*sem):
    return pltpu.CompilerParams(dimension_semantics=sem or None, vmem_limit_bytes=VMEM_LIMIT)


def _dot(a, b, dims=NN):
    return lax.dot_general(a, b, dims, preferred_element_type=f32)


def _pick(dim, pref):
    t = min(dim, pref)
    while dim % t:
        t -= LANES
    assert t > 0, (dim, pref)
    return t


ANY = pl.BlockSpec(memory_space=pl.ANY)


def _with_dep(in_specs, operands, dep):
    if dep is None:
        return list(in_specs), list(operands), 0
    return list(in_specs) + [ANY], list(operands) + [dep], 1


def _mm_call(name, grid, in_specs, out_spec, out_shape, acc_shape, dims, operands, dep=None):
    nk = grid[2]
    in_specs, operands, n_dep = _with_dep(in_specs, operands, dep)

    def body_one(a_ref, b_ref, *rest):
        o_ref = rest[n_dep]
        o_ref[...] = _dot(a_ref[...], b_ref[...], dims).astype(o_ref.dtype)

    def body(a_ref, b_ref, *rest):
        o_ref, acc_ref = rest[n_dep], rest[n_dep + 1]
        k = pl.program_id(2)

        @pl.when(k == 0)
        def _():
            acc_ref[...] = _dot(a_ref[...], b_ref[...], dims)

        @pl.when((k > 0) & (k < nk - 1))
        def _():
            acc_ref[...] += _dot(a_ref[...], b_ref[...], dims)

        @pl.when(k == nk - 1)
        def _():
            o_ref[...] = (acc_ref[...] + _dot(a_ref[...], b_ref[...], dims)).astype(o_ref.dtype)

    return pl.pallas_call(
        body_one if nk == 1 else body, out_shape=out_shape, grid=grid, in_specs=in_specs, out_specs=out_spec,
        scratch_shapes=[] if nk == 1 else [pltpu.VMEM(acc_shape, f32)], name=name,
        compiler_params=_params("parallel", "parallel", "arbitrary"))(*operands)


def mm_nt_pair(name, a1, b1, a2, b2, tm=1024, tk=1024, dep=None):
    M = a1.shape[0]
    S, K, n = b1.shape
    tm, tko = _pick(M, tm), _pick(K, tk)
    nk = 2 * S

    def body(a1_ref, b1_ref, a2_ref, b2_ref, *rest):
        o_ref, acc_ref = rest[-2], rest[-1]
        k = pl.program_id(2)

        @pl.when(k == 0)
        def _():
            acc_ref[...] = _dot(a1_ref[...], b1_ref[...], NT)

        @pl.when((k > 0) & (k < S))
        def _():
            acc_ref[...] += _dot(a1_ref[...], b1_ref[...], NT)

        @pl.when((k >= S) & (k < nk - 1))
        def _():
            acc_ref[...] += _dot(a2_ref[...], b2_ref[...], NT)

        @pl.when(k == nk - 1)
        def _():
            o_ref[...] = acc_ref[...] + _dot(a2_ref[...], b2_ref[...], NT)

    first = lambda k: jnp.minimum(k, S - 1)
    second = lambda k: jnp.maximum(k - S, 0)
    in_specs = [pl.BlockSpec((tm, n), lambda i, j, k: (i, first(k))), pl.BlockSpec((None, tko, n), lambda i, j, k: (first(k), j, 0)),
                pl.BlockSpec((tm, n), lambda i, j, k: (i, second(k))), pl.BlockSpec((None, tko, n), lambda i, j, k: (second(k), j, 0))]
    in_specs, operands, _ = _with_dep(in_specs, (a1, b1, a2, b2), dep)
    return pl.pallas_call(
        body, out_shape=jax.ShapeDtypeStruct((M, K), f32), grid=(M // tm, K // tko, nk), in_specs=in_specs,
        out_specs=pl.BlockSpec((tm, tko), lambda i, j, k: (i, j)), scratch_shapes=[pltpu.VMEM((tm, tko), f32)], name=name,
        compiler_params=_params("parallel", "parallel", "arbitrary"))(*operands)


def mm_nn(name, a, b, out_dtype=f32, tm=1024, tn=1024, tk=2048, dep=None):
    M, K = a.shape
    tm, tk = _pick(M, tm), _pick(K, tk)
    if b.ndim == 3:
        S, _, n = b.shape
        tn = _pick(n, 2048)
        per = n // tn
        b_spec = pl.BlockSpec((None, tk, tn), lambda i, j, k: (j // per, k, j % per))
        N = S * n
    else:
        N = b.shape[1]
        tn = _pick(N, tn)
        b_spec = pl.BlockSpec((tk, tn), lambda i, j, k: (k, j))
    grid = (M // tm, N // tn, K // tk)
    return _mm_call(name, grid, [pl.BlockSpec((tm, tk), lambda i, j, k: (i, k)), b_spec],
                    pl.BlockSpec((tm, tn), lambda i, j, k: (i, j)), jax.ShapeDtypeStruct((M, N), out_dtype),
                    (tm, tn), NN, (a, b), dep)


def mm_nt(name, a, b, out_dtype=f32, tm=1024, tn=2048, tk=1024, dep=None):
    M, N = a.shape
    tm = _pick(M, tm)
    if b.ndim == 3:
        S, K, n = b.shape
        tr = _pick(n, 2048)
        per = n // tr
        tko = _pick(K, tk)
        b_spec = pl.BlockSpec((None, tko, tr), lambda i, j, k: (k // per, j, k % per))
    else:
        K = b.shape[0]
        tr = _pick(N, tn)
        tko = _pick(K, tk)
        b_spec = pl.BlockSpec((tko, tr), lambda i, j, k: (j, k))
    grid = (M // tm, K // tko, N // tr)
    return _mm_call(name, grid, [pl.BlockSpec((tm, tr), lambda i, j, k: (i, k)), b_spec],
                    pl.BlockSpec((tm, tko), lambda i, j, k: (i, j)), jax.ShapeDtypeStruct((M, K), out_dtype),
                    (tm, tko), NT, (a, b), dep)


def mm_tn(name, a, b, shards=None, out_dtype=f32, tm=1024, tn=1024, tl=2048, dep=None):
    L, K = a.shape
    N = b.shape[1]
    tl, tko = _pick(L, tl), _pick(K, tm)
    if shards:
        n = N // shards
        tn = _pick(n, 2048)
        per = n // tn
        o_spec = pl.BlockSpec((None, tko, tn), lambda i, j, k: (j // per, i, j % per))
        o_shape = jax.ShapeDtypeStruct((shards, K, n), out_dtype)
    else:
        tn = _pick(N, tn)
        o_spec = pl.BlockSpec((tko, tn), lambda i, j, k: (i, j))
        o_shape = jax.ShapeDtypeStruct((K, N), out_dtype)
    grid = (K // tko, N // tn, L // tl)
    return _mm_call(name, grid, [pl.BlockSpec((tl, tko), lambda i, j, k: (k, i)),
                                 pl.BlockSpec((tl, tn), lambda i, j, k: (k, j))],
                    o_spec, o_shape, (tko, tn), TN, (a, b), dep)


def ffn_hidden(hn, w_gate, w_up, tm=512):
    M, K = hn.shape
    S, _, n = w_gate.shape
    tm = _pick(M, tm)

    def body(a_ref, g_ref, u_ref, gt_ref, up_ref, hid_ref):
        a = a_ref[...]
        g = _dot(a, g_ref[...])
        u = _dot(a, u_ref[...])
        gt_ref[...] = g.astype(bf16)
        up_ref[...] = u.astype(bf16)
        hid_ref[...] = (g * sigmoid(g) * u).astype(bf16)

    w_spec = pl.BlockSpec((None, K, n), lambda s, i: (s, 0, 0))
    o_spec = pl.BlockSpec((tm, n), lambda s, i: (i, s))
    o = jax.ShapeDtypeStruct((M, S * n), bf16)
    return pl.pallas_call(
        body, out_shape=[o, o, o], grid=(S, M // tm), in_specs=[pl.BlockSpec((tm, K), lambda s, i: (i, 0)), w_spec, w_spec],
        out_specs=[o_spec, o_spec, o_spec], name="ffn_hidden", compiler_params=_params("parallel", "parallel"))(hn, w_gate, w_up)


def ffn_hidden_grad(d_ff, w_down, gt, up, tm=512):
    M, D = d_ff.shape
    F = w_down.shape[0]
    n = _pick(F // N_CHIPS, 2048)
    tm = _pick(M, tm)

    def body(a_ref, b_ref, gt_ref, up_ref, dg_ref, du_ref):
        dh = _dot(a_ref[...], b_ref[...], NT)
        g = gt_ref[...].astype(f32)
        sg = sigmoid(g)
        dg_ref[...] = (dh * up_ref[...].astype(f32) * (sg * (1.0 + g * (1.0 - sg)))).astype(bf16)
        du_ref[...] = (dh * (g * sg)).astype(bf16)

    t_spec = pl.BlockSpec((tm, n), lambda j, i: (i, j))
    o = jax.ShapeDtypeStruct((M, F), bf16)
    return pl.pallas_call(
        body, out_shape=[o, o], grid=(F // n, M // tm),
        in_specs=[pl.BlockSpec((tm, D), lambda j, i: (i, 0)), pl.BlockSpec((n, D), lambda j, i: (j, 0)), t_spec, t_spec],
        out_specs=[t_spec, t_spec], name="ffn_hidden_grad", compiler_params=_params("parallel", "parallel"))(d_ff, w_down, gt, up)


def rowwise(name, fn, L, rows, bcast, outs, reds=(), tr=256, dep=None):
    tr = min(tr, L)
    nt = L // tr
    n_rows, n_b, n_o = len(rows), len(bcast), len(outs)
    n_dep = 0 if dep is None else 1

    def body(*refs):
        i = pl.program_id(0)
        ins = [r[...] for r in refs[:n_rows + n_b]]
        res = fn(i, nt, *ins)
        o_refs = refs[n_rows + n_b + n_dep:]
        for k in range(n_o):
            o_refs[k][...] = res[k].astype(o_refs[k].dtype)
        if reds:
            @pl.when(i == 0)
            def _():
                for k in range(len(reds)):
                    o_refs[n_o + k][...] = jnp.zeros_like(o_refs[n_o + k])
            for k in range(len(reds)):
                o_refs[n_o + k][...] += res[n_o + k]

    def row_spec(width, cb, shift):
        if shift:
            return pl.BlockSpec((tr, width), lambda i: (jnp.minimum(i + shift, nt - 1), cb))
        return pl.BlockSpec((tr, width), lambda i: (i, cb))

    in_specs = [row_spec(w, cb, sh) for (_, w, cb, sh) in rows]
    in_specs += [pl.BlockSpec(b.shape, lambda i: (0, 0)) for b in bcast]
    out_specs = [pl.BlockSpec((tr, w), lambda i: (i, 0)) for (w, _) in outs]
    out_specs += [pl.BlockSpec((1, w), lambda i: (0, 0)) for w in reds]
    out_shape = [jax.ShapeDtypeStruct((L, w), dt) for (w, dt) in outs]
    out_shape += [jax.ShapeDtypeStruct((1, w), f32) for w in reds]
    in_specs, operands, _ = _with_dep(in_specs, [r[0] for r in rows] + list(bcast), dep)
    return pl.pallas_call(
        body, out_shape=out_shape, grid=(nt,), in_specs=in_specs, out_specs=out_specs, name=name,
        compiler_params=_params("arbitrary"))(*operands)


def full(a):
    return (a, a.shape[1], 0, 0)


def colsum(v):
    return jnp.sum(v, axis=0, keepdims=True)


def rms_fwd(x, g):
    r = lax.rsqrt(jnp.mean(x * x, axis=-1, keepdims=True) + RMS_EPS)
    return x * r * g


def rms_bwd(x, g, dy):
    r = lax.rsqrt(jnp.mean(x * x, axis=-1, keepdims=True) + RMS_EPS)
    xh = x * r
    dyg = dy * g
    dx = r * (dyg - xh * jnp.mean(dyg * xh, axis=-1, keepdims=True))
    return dx, colsum(dy * xh)


GELU_C = math.sqrt(2.0 / math.pi)


def gelu(y):
    return y * (0.5 * (1.0 + jnp.tanh(GELU_C * (y + 0.044715 * (y * y * y)))))


def gelu_grad(y):
    t = jnp.tanh(GELU_C * (y + 0.044715 * (y * y * y)))
    return 0.5 * (1.0 + t) + 0.5 * y * (1.0 - t * t) * (GELU_C * (1.0 + 3 * 0.044715 * (y * y)))


def sigmoid(v):
    return 1.0 / (1.0 + jnp.exp(-v))


def _lane(shape):
    return lax.broadcasted_iota(jnp.int32, shape, 1)


def _rot_chunk(t, cos, sin_signed):
    first = (_lane(t.shape) % HEAD_DIM) < (HEAD_DIM // 2)
    partner = jnp.where(first, pltpu.roll(t, LANES - HEAD_DIM // 2, 1), pltpu.roll(t, HEAD_DIM // 2, 1))
    return t * cos + partner * sin_signed


def _cos_sin(pos, inv_freq, inverse):
    ang = pos * inv_freq
    cos, sin = jnp.cos(ang), jnp.sin(ang)
    first = (_lane(ang.shape) % HEAD_DIM) < (HEAD_DIM // 2)
    sign = jnp.where(first, -1.0, 1.0) * (-1.0 if inverse else 1.0)
    return cos, sin * sign


def _dup_head(chunk, odd):
    low = _lane(chunk.shape) < HEAD_DIM
    x = jnp.where(low != odd, chunk, 0.0)
    return x + pltpu.roll(x, HEAD_DIM, 1)


def _chunks(v):
    return [v[:, LANES * c:LANES * (c + 1)] for c in range(v.shape[1] // LANES)]


def qkv_prep(proj, pos, inv_freq, d_attn, d_kv):
    L = proj.shape[0]
    d_ssm = proj.shape[1] - d_attn - 2 * d_kv
    half = d_ssm // 2
    scale = 1.0 / math.sqrt(HEAD_DIM)

    def fn(i, nt, q, k, v, u0, u1, p, invf):
        cos, sin = _cos_sin(p, invf, False)
        qr = jnp.concatenate([_rot_chunk(c, cos, sin) for c in _chunks(q)], axis=1) * scale
        kr = [_rot_chunk(c, cos, sin) for c in _chunks(k)]
        kk = jnp.concatenate([_dup_head(c, odd) for c in kr for odd in (False, True)], axis=1)
        vv = jnp.concatenate([_dup_head(c, odd) for c in _chunks(v) for odd in (False, True)], axis=1)
        return qr, kk, vv, jnp.concatenate([u0, u1], axis=1)

    u_cb = (d_attn + 2 * d_kv) // half
    return rowwise("qkv_prep", fn, L,
                   [(proj, d_attn, 0, 0), (proj, d_kv, d_attn // d_kv, 0), (proj, d_kv, d_attn // d_kv + 1, 0),
                    (proj, half, u_cb, 0), (proj, half, u_cb + 1, 0), full(pos)],
                   [inv_freq], [(d_attn, bf16), (2 * d_kv, bf16), (2 * d_kv, bf16), (d_ssm, bf16)])


def qkv_grad(dq, dkk_c, dkk_p, dvv_c, dvv_p, du, pos, inv_freq):
    L, d_attn = dq.shape
    d_kv = dkk_c.shape[1] // 2
    scale = 1.0 / math.sqrt(HEAD_DIM)

    def fold(cur, prev, i, nt):
        t = cur + jnp.where(i < nt - 1, prev, 0.0)
        out = []
        for c in range(t.shape[1] // (2 * LANES)):
            even, odd = t[:, 2 * c * LANES:(2 * c + 1) * LANES], t[:, (2 * c + 1) * LANES:(2 * c + 2) * LANES]
            even, odd = even + pltpu.roll(even, HEAD_DIM, 1), odd + pltpu.roll(odd, HEAD_DIM, 1)
            out.append(jnp.where(_lane(even.shape) < HEAD_DIM, even, odd))
        return out

    def fn(i, nt, dq_t, kc, kp, vc, vp, du_t, p, invf):
        cos, sin = _cos_sin(p, invf, True)
        dq_o = jnp.concatenate([_rot_chunk(c, cos, sin) for c in _chunks(dq_t)], axis=1) * scale
        dk_o = jnp.concatenate([_rot_chunk(c, cos, sin) for c in fold(kc, kp, i, nt)], axis=1)
        dv_o = jnp.concatenate(fold(vc, vp, i, nt), axis=1)
        return (jnp.concatenate([dq_o, dk_o, dv_o, du_t], axis=1),)

    return rowwise("qkv_grad", fn, L,
                   [full(dq), full(dkk_c), (dkk_p, 2 * d_kv, 0, 1), full(dvv_c), (dvv_p, 2 * d_kv, 0, 1), full(du), full(pos)],
                   [inv_freq], [(d_attn + 2 * d_kv + du.shape[1], bf16)], tr=ATTN_BLOCK)[0]


def _attn_specs(L):
    nb = L // ATTN_BLOCK
    B = ATTN_BLOCK
    q_spec = pl.BlockSpec((B, 2 * LANES), lambda h, n: (n, h))
    cur = pl.BlockSpec((B, LANES), lambda h, n: (n, h))
    prev = pl.BlockSpec((B, LANES), lambda h, n: (jnp.maximum(n - 1, 0), h))
    return nb, q_spec, cur, prev


def _attn_mask(n):
    B = ATTN_BLOCK
    row = lax.broadcasted_iota(jnp.int32, (B, 2 * B), 0)
    col = lax.broadcasted_iota(jnp.int32, (B, 2 * B), 1)
    return ((col < B) & (col > row) & (n > 0)) | ((col >= B) & (row >= col - B))


def _attn_probs(qm, kcat, sink, mask):
    s = jnp.where(mask, _dot(qm, kcat, NT), NEG)
    m = jnp.maximum(jnp.max(s, axis=1, keepdims=True), sink)
    p, ps = jnp.exp(s - m), jnp.exp(sink - m)
    inv = 1.0 / (jnp.sum(p, axis=1, keepdims=True) + ps)
    return p, inv, ps


def _attn_heads(q_ref, s_ref, h, q_per_kv):
    low = _lane((ATTN_BLOCK, LANES)) < HEAD_DIM
    heads = []
    for pr in range(q_per_kv // 2):
        q2 = q_ref[:, LANES * pr:LANES * (pr + 1)]
        for odd in (False, True):
            mine = low != odd
            sink = jnp.max(s_ref[pl.ds(h * q_per_kv + 2 * pr + int(odd), 1), :], axis=1, keepdims=True)
            heads.append((pr, mine, jnp.where(mine, q2, jnp.zeros_like(q2)), sink))
    return low, heads


def attn_fwd(qr, kk, vv, sink_b):
    L, d_attn = qr.shape
    nb, q_spec, cur, prev = _attn_specs(L)
    n_kv = kk.shape[1] // LANES
    q_per_kv = d_attn // HEAD_DIM // n_kv

    def body(q_ref, kc_ref, kp_ref, vc_ref, vp_ref, s_ref, o_ref):
        h, n = pl.program_id(0), pl.program_id(1)
        kcat = jnp.concatenate([kp_ref[...], kc_ref[...]], axis=0)
        vcat = jnp.concatenate([vp_ref[...], vc_ref[...]], axis=0)
        mask = _attn_mask(n)
        low, heads = _attn_heads(q_ref, s_ref, h, q_per_kv)
        probs = [_attn_probs(qm, kcat, sink, mask) for (_, _, qm, sink) in heads]
        outs = [_dot(p.astype(bf16), vcat) * inv for (p, inv, _) in probs]
        for pr in range(q_per_kv // 2):
            o_ref[:, LANES * pr:LANES * (pr + 1)] = jnp.where(low, outs[2 * pr], outs[2 * pr + 1])

    return pl.pallas_call(
        body, out_shape=jax.ShapeDtypeStruct((L, d_attn), f32), grid=(n_kv, nb),
        in_specs=[q_spec, cur, prev, cur, prev, pl.BlockSpec(sink_b.shape, lambda h, n: (0, 0))],
        out_specs=q_spec, name="attn_fwd", compiler_params=_params("parallel", "arbitrary"))(qr, kk, kk, vv, vv, sink_b)


def attn_bwd(qr, kk, vv, sink_b, attn, d_attn_out):
    L, d_attn = qr.shape
    nb, q_spec, cur, prev = _attn_specs(L)
    n_kv = kk.shape[1] // LANES
    q_per_kv = d_attn // HEAD_DIM // n_kv

    def body(q_ref, kc_ref, kp_ref, vc_ref, vp_ref, s_ref, o_ref, do_ref, dq_ref, dkc_ref, dkp_ref, dvc_ref, dvp_ref, ds_ref):
        h, n = pl.program_id(0), pl.program_id(1)
        B = ATTN_BLOCK
        kcat = jnp.concatenate([kp_ref[...], kc_ref[...]], axis=0)
        vcat = jnp.concatenate([vp_ref[...], vc_ref[...]], axis=0)
        mask = _attn_mask(n)
        low, heads = _attn_heads(q_ref, s_ref, h, q_per_kv)
        probs = [_attn_probs(qm, kcat, sink, mask) for (_, _, qm, sink) in heads]
        dk = jnp.zeros((2 * B, LANES), f32)
        dv = dk
        srow = lax.broadcasted_iota(jnp.int32, (SUBLANES, LANES), 0)
        dsink = jnp.zeros((SUBLANES, LANES), f32)
        dqs = []
        for i, ((pr, mine, qm, _), (p, inv, ps)) in enumerate(zip(heads, probs)):
            do2 = do_ref[:, LANES * pr:LANES * (pr + 1)]
            delta = jnp.sum(jnp.where(mine, do2 * o_ref[:, LANES * pr:LANES * (pr + 1)], 0.0), axis=1, keepdims=True)
            dob = jnp.where(mine, do2, 0.0).astype(bf16)
            p = p * inv
            ds = (p * (_dot(dob, vcat, NT) - delta)).astype(bf16)
            dqs.append(_dot(ds, kcat))
            dk = dk + _dot(ds, qm, TN)
            dv = dv + _dot(p.astype(bf16), dob, TN)
            dsink = dsink + jnp.where(srow == i, -jnp.sum(ps * inv * delta), 0.0)
        for pr in range(q_per_kv // 2):
            dq_ref[:, LANES * pr:LANES * (pr + 1)] = jnp.where(low, dqs[2 * pr], dqs[2 * pr + 1])
        dkp_ref[...] = dk[:B]
        dkc_ref[...] = dk[B:]
        dvp_ref[...] = dv[:B]
        dvc_ref[...] = dv[B:]

        @pl.when(n == 0)
        def _():
            ds_ref[...] = jnp.zeros_like(ds_ref)

        ds_ref[...] += dsink

    kv_shape = jax.ShapeDtypeStruct(kk.shape, f32)
    return pl.pallas_call(
        body,
        out_shape=[jax.ShapeDtypeStruct((L, d_attn), f32), kv_shape, kv_shape, kv_shape, kv_shape,
                   jax.ShapeDtypeStruct((n_kv, SUBLANES, LANES), f32)],
        grid=(n_kv, nb),
        in_specs=[q_spec, cur, prev, cur, prev, pl.BlockSpec(sink_b.shape, lambda h, n: (0, 0)), q_spec, q_spec],
        out_specs=[q_spec, cur, cur, cur, cur, pl.BlockSpec((None, SUBLANES, LANES), lambda h, n: (h, 0, 0))],
        name="attn_bwd", compiler_params=_params("parallel", "arbitrary"))(qr, kk, kk, vv, vv, sink_b, attn, d_attn_out)


SSM_T = 128
NQ = SUBLANES * SSM_STATE // LANES
NJ = SUBLANES


def _strided_put(ref, j, val):
    for q in range(NQ):
        ref.at[q][pl.ds(j, SSM_T, stride=NJ), :] = val[:, LANES * q:LANES * (q + 1)]


def _strided_get(ref, j):
    return jnp.concatenate([ref.at[q][pl.ds(j, SSM_T, stride=NJ), :] for q in range(NQ)], axis=1)


def _ssm_specs(L, rev):
    nt = L // SSM_T
    idx = (lambda i: nt - 1 - i) if rev else (lambda i: i)
    row = lambda w, cb=0: pl.BlockSpec((SSM_T, w), lambda i: (idx(i), cb))
    state = pl.BlockSpec((NQ, SSM_T * NJ, LANES), lambda i: (0, idx(i), 0))
    whole = lambda a: pl.BlockSpec(a.shape, lambda i: (0,) * a.ndim)
    return nt, row, state, whole


def ssm_fwd(u_bf, proj, u_cb, bd_re, bd_im, cd_re, cd_im, lam_re, lam_im, d_skip):
    L, d_ssm = u_bf.shape
    nt, row, state, whole = _ssm_specs(L, False)
    half = d_ssm // 2
    gw = d_ssm // NJ

    def body(u_ref, u0_ref, u1_ref, bdr, bdi, cdr, cdi, lr_ref, li_ref, d_ref, y_ref, z_ref, sr_ref, si_ref, carry):
        i = pl.program_id(0)

        @pl.when(i == 0)
        def _():
            carry[...] = jnp.zeros_like(carry)

        for j in range(NJ):
            uj = u_ref[:, gw * j:gw * (j + 1)]
            _strided_put(sr_ref, j, _dot(uj, bdr[j]))
            _strided_put(si_ref, j, _dot(uj, bdi[j]))
        lr = [lr_ref[q] for q in range(NQ)]
        li = [li_ref[q] for q in range(NQ)]

        def step(t, s):
            sr, si = s
            rows = pl.ds(pl.multiple_of(t * NJ, NJ), NJ)
            nr = tuple(lr[q] * sr[q] - li[q] * si[q] + sr_ref[q, rows, :] for q in range(NQ))
            ni = tuple(lr[q] * si[q] + li[q] * sr[q] + si_ref[q, rows, :] for q in range(NQ))
            for q in range(NQ):
                sr_ref[q, rows, :] = nr[q]
                si_ref[q, rows, :] = ni[q]
            return nr, ni

        init = (tuple(carry[0, q] for q in range(NQ)), tuple(carry[1, q] for q in range(NQ)))
        sr, si = lax.fori_loop(0, SSM_T, step, init, unroll=8)
        for q in range(NQ):
            carry[0, q] = sr[q]
            carry[1, q] = si[q]
        uf = jnp.concatenate([u0_ref[...], u1_ref[...]], axis=1)
        for j in range(NJ):
            cols = slice(gw * j, gw * (j + 1))
            yj = _dot(_strided_get(sr_ref, j).astype(bf16), cdr[j]) - _dot(_strided_get(si_ref, j).astype(bf16), cdi[j])
            yj = yj + d_ref[:, cols] * uf[:, cols]
            y_ref[:, cols] = yj
            z_ref[:, cols] = gelu(yj).astype(bf16)

    s_shape = jax.ShapeDtypeStruct((NQ, L * NJ, LANES), f32)
    consts = (bd_re, bd_im, cd_re, cd_im, lam_re, lam_im, d_skip)
    return pl.pallas_call(
        body, out_shape=[jax.ShapeDtypeStruct((L, d_ssm), f32), jax.ShapeDtypeStruct((L, d_ssm), bf16), s_shape, s_shape], grid=(nt,),
        in_specs=[row(d_ssm), row(half, u_cb), row(half, u_cb + 1)] + [whole(a) for a in consts],
        out_specs=[row(d_ssm), row(d_ssm), state, state],
        scratch_shapes=[pltpu.VMEM((2, NQ, NJ, LANES), f32)], name="ssm_fwd",
        compiler_params=_params("arbitrary"))(u_bf, proj, proj, *consts)


def ssm_bwd(y, dz1, dz2, u_bf, proj, u_cb, s_re, s_im, bd_re, bd_im, cd_re, cd_im, lam_re, lam_im, d_skip, dep=None):
    L, d_ssm = y.shape
    nt, row, state, whole = _ssm_specs(L, True)
    half = d_ssm // 2
    gw = d_ssm // NJ
    n_dep = 0 if dep is None else 1

    def body(y_ref, dz1_ref, dz2_ref, u_ref, u0_ref, u1_ref, sr_ref, si_ref, bdr, bdi, cdr, cdi, lr_ref, li_ref, d_ref, *rest):
        du_ref, dbdr, dbdi, dcdr, dcdi, dlr, dli, dd_ref, gr_ref, gi_ref, carry = rest[n_dep:]
        i = pl.program_id(0)

        @pl.when(i == 0)
        def _():
            carry[...] = jnp.zeros_like(carry)
            for r in (dbdr, dbdi, dcdr, dcdi, dlr, dli, dd_ref):
                r[...] = jnp.zeros_like(r)

        dyf = (dz1_ref[...] + dz2_ref[...]) * gelu_grad(y_ref[...])
        dyb = dyf.astype(bf16)
        for j in range(NJ):
            dyj = dyb[:, gw * j:gw * (j + 1)]
            _strided_put(gr_ref, j, _dot(dyj, cdr[j], NT))
            _strided_put(gi_ref, j, -_dot(dyj, cdi[j], NT))
            dcdr[j] += _dot(_strided_get(sr_ref, j).astype(bf16), dyj, TN)
            dcdi[j] -= _dot(_strided_get(si_ref, j).astype(bf16), dyj, TN)
        lr = [lr_ref[q] for q in range(NQ)]
        li = [li_ref[q] for q in range(NQ)]

        def step(k, c):
            gr, gi, ar, ai = c
            rows = pl.ds(pl.multiple_of((SSM_T - 1 - k) * NJ, NJ), NJ)
            s_r = [sr_ref[q, rows, :] for q in range(NQ)]
            s_i = [si_ref[q, rows, :] for q in range(NQ)]
            ar = tuple(ar[q] + gr[q] * s_r[q] + gi[q] * s_i[q] for q in range(NQ))
            ai = tuple(ai[q] + gi[q] * s_r[q] - gr[q] * s_i[q] for q in range(NQ))
            nr = tuple(gr_ref[q, rows, :] + lr[q] * gr[q] + li[q] * gi[q] for q in range(NQ))
            ni = tuple(gi_ref[q, rows, :] + lr[q] * gi[q] - li[q] * gr[q] for q in range(NQ))
            for q in range(NQ):
                gr_ref[q, rows, :] = nr[q]
                gi_ref[q, rows, :] = ni[q]
            return nr, ni, ar, ai

        zero = tuple(jnp.zeros((NJ, LANES), f32) for _ in range(NQ))
        init = (tuple(carry[0, q] for q in range(NQ)), tuple(carry[1, q] for q in range(NQ)), zero, zero)
        gr, gi, ar, ai = lax.fori_loop(0, SSM_T, step, init, unroll=8)
        for q in range(NQ):
            carry[0, q] = gr[q]
            carry[1, q] = gi[q]
            dlr[q] += ar[q]
            dli[q] += ai[q]
        uf = jnp.concatenate([u0_ref[...], u1_ref[...]], axis=1)
        dd_ref[...] += colsum(dyf * uf)
        for j in range(NJ):
            cols = slice(gw * j, gw * (j + 1))
            gjr, gji = _strided_get(gr_ref, j).astype(bf16), _strided_get(gi_ref, j).astype(bf16)
            du_ref[:, cols] = _dot(gjr, bdr[j], NT) + _dot(gji, bdi[j], NT) + d_ref[:, cols] * dyf[:, cols]
            uj = u_ref[:, cols]
            dbdr[j] += _dot(uj, gjr, TN)
            dbdi[j] += _dot(uj, gji, TN)

    consts = (bd_re, bd_im, cd_re, cd_im, lam_re, lam_im, d_skip)
    acc = lambda a: jax.ShapeDtypeStruct(a.shape, f32)
    outs = [jax.ShapeDtypeStruct((L, d_ssm), f32), acc(bd_re), acc(bd_im), acc(cd_re), acc(cd_im), acc(lam_re), acc(lam_im), acc(d_skip)]
    in_specs, operands, _ = _with_dep(
        [row(d_ssm)] * 4 + [row(half, u_cb), row(half, u_cb + 1), state, state] + [whole(a) for a in consts],
        [y, dz1, dz2, u_bf, proj, proj, s_re, s_im, *consts], dep)
    return pl.pallas_call(
        body, out_shape=outs, grid=(nt,),
        in_specs=in_specs, out_specs=[row(d_ssm)] + [whole(a) for a in consts],
        scratch_shapes=[pltpu.VMEM((NQ, SSM_T * NJ, LANES), f32), pltpu.VMEM((NQ, SSM_T * NJ, LANES), f32),
                        pltpu.VMEM((2, NQ, NJ, LANES), f32)],
        name="ssm_bwd", compiler_params=_params("arbitrary"))(*operands)


def _cmul(ar, ai, br, bi):
    return ar * br - ai * bi, ar * bi + ai * br


def _disc(ar, ai, logdt):
    dt = jnp.exp(logdt)
    mag = jnp.exp(ar * dt)
    lr, li = mag * jnp.cos(ai * dt), mag * jnp.sin(ai * dt)
    den = ar * ar + ai * ai
    nr, ni = lr - 1.0, li
    fr, fi = (nr * ar + ni * ai) / den, (ni * ar - nr * ai) / den
    return dt, lr, li, den, fr, fi


def ssm_params(a_re, a_im, logdt_b, bt_re, bt_im, spread):
    def body(ar_ref, ai_ref, ld_ref, br_ref, bi_ref, sp_ref, lr_ref, li_ref, or_ref, oi_ref):
        _, lr, li, _, fr, fi = _disc(ar_ref[...], ai_ref[...], ld_ref[...])
        lr_ref[...] = lr
        li_ref[...] = li
        fre = jnp.dot(sp_ref[...], fr, precision=HIGHEST, preferred_element_type=f32)
        fie = jnp.dot(sp_ref[...], fi, precision=HIGHEST, preferred_element_type=f32)
        o_r, o_i = _cmul(fre, fie, br_ref[...], bi_ref[...])
        or_ref[...] = o_r
        oi_ref[...] = o_i

    g = jax.ShapeDtypeStruct(a_re.shape, f32)
    b = jax.ShapeDtypeStruct(bt_re.shape, f32)
    return pl.pallas_call(body, out_shape=[g, g, b, b], name="ssm_params",
                          compiler_params=_params())(a_re, a_im, logdt_b, bt_re, bt_im, spread)


def ssm_params_grad(a_re, a_im, logdt_b, bt_re, bt_im, spread, gather, dlam_re, dlam_im, dbt_re, dbt_im):
    def body(ar_ref, ai_ref, ld_ref, br_ref, bi_ref, sp_ref, ga_ref, glr_ref, gli_ref, gbr_ref, gbi_ref,
             dar_ref, dai_ref, dld_ref, dbr_ref, dbi_ref):
        ar, ai = ar_ref[...], ai_ref[...]
        dt, lr, li, den, fr, fi = _disc(ar, ai, ld_ref[...])
        hdot = functools.partial(jnp.dot, precision=HIGHEST, preferred_element_type=f32)
        fre, fie = hdot(sp_ref[...], fr), hdot(sp_ref[...], fi)
        gbr, gbi, br, bi = gbr_ref[...], gbi_ref[...], br_ref[...], bi_ref[...]
        dbr_ref[...], dbi_ref[...] = _cmul(fre, -fie, gbr, gbi)
        t_r, t_i = _cmul(br, -bi, gbr, gbi)
        gfr, gfi = hdot(ga_ref[...], t_r), hdot(ga_ref[...], t_i)
        iwr, iwi = ar / den, -ai / den
        x_r, x_i = _cmul(iwr, -iwi, gfr, gfi)
        glr, gli = glr_ref[...] + x_r, gli_ref[...] + x_i
        q_r, q_i = _cmul(fr, fi, iwr, iwi)
        gwr, gwi = _cmul(-q_r, q_i, gfr, gfi)
        y_r, y_i = _cmul(dt * lr, -dt * li, glr, gli)
        dar_ref[...] = gwr + y_r
        dai_ref[...] = gwi + y_i
        wl_r, wl_i = _cmul(ar, ai, lr, li)
        z_r, _ = _cmul(wl_r, -wl_i, glr, gli)
        dld_ref[...] = jnp.sum(z_r * dt, axis=1, keepdims=True)

    g = jax.ShapeDtypeStruct(a_re.shape, f32)
    b = jax.ShapeDtypeStruct(bt_re.shape, f32)
    return pl.pallas_call(body, out_shape=[g, g, jax.ShapeDtypeStruct((a_re.shape[0], 1), f32), b, b], name="ssm_params_grad",
                          compiler_params=_params())(a_re, a_im, logdt_b, bt_re, bt_im, spread, gather, dlam_re, dlam_im, dbt_re, dbt_im)


def _block_diag(t, rows, cols):
    G = t.shape[0]
    t = t.reshape(G // NJ, NJ, rows, cols)
    eye = jnp.eye(NJ, dtype=t.dtype)
    return jnp.einsum('jgrc,gh->jgrhc', t, eye).reshape(G // NJ, NJ * rows, NJ * cols)


def _block_diag_take(m, rows, cols):
    J = m.shape[0]
    m = m.reshape(J, NJ, rows, NJ, cols)
    idx = jnp.arange(NJ)
    return m[:, idx, :, idx, :].transpose(1, 0, 2, 3).reshape(J * NJ, rows, cols)


def _state_layout(t):
    return t.reshape(NJ, NQ, LANES).transpose(1, 0, 2)


def _state_layout_inv(t, G, N):
    return t.transpose(1, 0, 2).reshape(G, N)


def _tiles2d(shape, budget_rows=128):
    rows, cols = shape
    tr = rows
    if rows > budget_rows:
        tr = budget_rows
        while rows % tr:
            tr -= SUBLANES
    return tr, cols


def _adam_update(w, g, m, v):
    c1 = 1.0 - ADAM_B1 ** ADAM_STEP
    c2 = 1.0 - ADAM_B2 ** ADAM_STEP
    nm = ADAM_B1 * m + (1.0 - ADAM_B1) * g
    nv = ADAM_B2 * v + (1.0 - ADAM_B2) * (g * g)
    delta = -ADAM_LR * ((nm / c1) / (jnp.sqrt(nv / c2) + ADAM_EPS) + ADAM_WD * w)
    return delta, nm, nv


def adamw_many(name, ws, gs, ms, vs):
    n = len(ws)

    def body(*refs):
        w, g, m, v = (refs[k * n:(k + 1) * n] for k in range(4))
        d, nm, nv = (refs[(4 + k) * n:(5 + k) * n] for k in range(3))
        for i in range(n):
            d[i][...], nm[i][...], nv[i][...] = _adam_update(w[i][...], g[i][...], m[i][...], v[i][...])

    o = [jax.ShapeDtypeStruct(a.shape, f32) for a in ws]
    outs = pl.pallas_call(body, out_shape=o * 3, name=name, compiler_params=_params())(*ws, *gs, *ms, *vs)
    return outs[:n], outs[n:2 * n], outs[2 * n:]


def adamw_halves(name, w, own, got, m, v, c_arr):
    h, cols = own.shape
    tr, _ = _tiles2d((h, cols), 128 if cols > 1024 else 256)
    nh = h // tr

    def body(c_ref, w_ref, own_ref, got_ref, m_ref, v_ref, g_ref, d_ref, nm_ref, nv_ref):
        mine = (pl.program_id(0) // nh) == c_ref[0]
        g = jnp.where(mine, own_ref[...], got_ref[...])
        g_ref[...] = g
        d_ref[...], nm_ref[...], nv_ref[...] = _adam_update(w_ref[...], g, m_ref[...], v_ref[...])

    spec = pl.BlockSpec((tr, cols), lambda i, c: (i, 0))
    own_spec = pl.BlockSpec((tr, cols), lambda i, c: (jnp.where(i // nh == c[0], i % nh, 0), 0))
    got_spec = pl.BlockSpec((tr, cols), lambda i, c: (jnp.where(i // nh == c[0], 0, i % nh), 0))
    o = jax.ShapeDtypeStruct(w.shape, f32)
    grid_spec = pltpu.PrefetchScalarGridSpec(num_scalar_prefetch=1, grid=(2 * nh,),
                                             in_specs=[spec, own_spec, got_spec, spec, spec], out_specs=[spec] * 4)
    return pl.pallas_call(body, out_shape=[o, o, o, o], grid_spec=grid_spec, name=name,
                          compiler_params=_params("arbitrary"))(c_arr, w, own, got, m, v)


def pair_sum(name, g, got, c_arr):
    S, h, cols = got.shape
    tr, _ = _tiles2d((h, cols), 1024)
    nh = h // tr

    def body(c_ref, g_ref, o_ref, out_ref):
        out_ref[...] = (g_ref[...].astype(f32) + o_ref[...].astype(f32)).astype(out_ref.dtype)

    spec = pl.BlockSpec((None, tr, cols), lambda s, i, c: (s, i, 0))
    grid_spec = pltpu.PrefetchScalarGridSpec(
        num_scalar_prefetch=1, grid=(S, nh),
        in_specs=[pl.BlockSpec((None, tr, cols), lambda s, i, c: (s, c[0] * nh + i, 0)), spec], out_specs=spec)
    return pl.pallas_call(body, out_shape=jax.ShapeDtypeStruct(got.shape, g.dtype), grid_spec=grid_spec, name=name,
                          compiler_params=_params("parallel", "parallel"))(c_arr, g, got)


def chip_sum(name, pair, landed, mine_arr, dep=None):
    n_in, h, cols = landed.shape
    tr, _ = _tiles2d((h, cols), 512)

    def body(s_ref, p_ref, l_ref, *rest):
        acc = p_ref[...].astype(f32)
        for k in range(n_in):
            acc = acc + l_ref[k].astype(f32)
        rest[-1][...] = acc

    in_specs, operands, _ = _with_dep(
        [pl.BlockSpec((None, tr, cols), lambda i, s: (s[0], i, 0)), pl.BlockSpec((n_in, tr, cols), lambda i, s: (0, i, 0))],
        [pair, landed], dep)
    grid_spec = pltpu.PrefetchScalarGridSpec(num_scalar_prefetch=1, grid=(h // tr,), in_specs=in_specs,
                                             out_specs=pl.BlockSpec((tr, cols), lambda i, s: (i, 0)))
    return pl.pallas_call(body, out_shape=jax.ShapeDtypeStruct((h, cols), f32), grid_spec=grid_spec, name=name,
                          compiler_params=_params("parallel"))(mine_arr, *operands)


def into_slot(name, w, slot_arr, n_slots, dtype, dep=None):
    tr, cols = _tiles2d(w.shape, 256)

    def body(s_ref, w_ref, *rest):
        rest[-1][...] = w_ref[...].astype(dtype)

    in_specs, operands, _ = _with_dep([pl.BlockSpec((tr, cols), lambda i, s: (i, 0))], [w], dep)
    grid_spec = pltpu.PrefetchScalarGridSpec(num_scalar_prefetch=1, grid=(w.shape[0] // tr,), in_specs=in_specs,
                                             out_specs=pl.BlockSpec((None, tr, cols), lambda i, s: (s[0], i, 0)))
    return pl.pallas_call(body, out_shape=jax.ShapeDtypeStruct((n_slots,) + w.shape, dtype), grid_spec=grid_spec, name=name,
                          compiler_params=_params("parallel"))(slot_arr, *operands)


def sum_slots(name, t):
    S, rows, cols = t.shape
    tr, _ = _tiles2d((rows, cols), 256)

    def body(t_ref, o_ref):
        acc = t_ref[0]
        for s in range(1, S):
            acc = acc + t_ref[s]
        o_ref[...] = acc

    return pl.pallas_call(body, out_shape=jax.ShapeDtypeStruct((rows, cols), f32), grid=(rows // tr,),
                          in_specs=[pl.BlockSpec((S, tr, cols), lambda i: (0, i, 0))], out_specs=pl.BlockSpec((tr, cols), lambda i: (i, 0)),
                          name=name, compiler_params=_params("parallel"))(t)


ANY = pl.BlockSpec(memory_space=pl.ANY)


def _place():
    x, y, c = lax.axis_index("x"), lax.axis_index("y"), lax.axis_index("c")
    return x, y, c


def _other_chips(x, y):
    return [(1 - x, y, 2 * (1 - x) + y), (x, 1 - y, 2 * x + 1 - y), (1 - x, 1 - y, 2 * (1 - x) + 1 - y)]


def gather_weights(bufs):
    nw = len(bufs)

    def body(*refs):
        dst = refs[nw:2 * nw]
        send1, recv1, send2, recv2 = refs[2 * nw:]
        x, y, c = _place()
        mine = 2 * x + y
        chips = _other_chips(x, y)
        sib = (x, y, 1 - c)
        first, passed = [], []
        for w in range(nw):
            h = dst[w].shape[1] // 2
            for k, (px, py, _) in enumerate(chips):
                half = dst[w].at[mine, pl.ds(c * h, h), :]
                cp = pltpu.make_async_remote_copy(src_ref=half, dst_ref=half,
                                                  send_sem=send1.at[w, k], recv_sem=recv1.at[w, k],
                                                  device_id=(px, py, c), device_id_type=MESH)
                cp.start()
                first.append(cp)
        for w in range(nw):
            h = dst[w].shape[1] // 2
            for k, (px, py, s) in enumerate(chips):
                landed = dst[w].at[s, pl.ds(c * h, h), :]
                pltpu.make_async_remote_copy(src_ref=landed, dst_ref=landed, send_sem=send1.at[w, k], recv_sem=recv1.at[w, k],
                                             device_id=(px, py, c), device_id_type=MESH).wait_recv()
                cp = pltpu.make_async_remote_copy(src_ref=landed, dst_ref=landed, send_sem=send2.at[w, k], recv_sem=recv2.at[w, k],
                                                  device_id=sib, device_id_type=MESH)
                cp.start()
                passed.append(cp)
        for w in range(nw):
            h = dst[w].shape[1] // 2
            for k, (px, py, s) in enumerate(chips):
                other = dst[w].at[s, pl.ds((1 - c) * h, h), :]
                pltpu.make_async_remote_copy(src_ref=other, dst_ref=other, send_sem=send2.at[w, k], recv_sem=recv2.at[w, k],
                                             device_id=sib, device_id_type=MESH).wait_recv()
        for cp in first + passed:
            cp.wait_send()

    sem = pltpu.SemaphoreType.DMA((nw, 3))
    return pl.pallas_call(
        body, out_shape=[jax.ShapeDtypeStruct(b.shape, b.dtype) for b in bufs],
        in_specs=[ANY] * nw, out_specs=[ANY] * nw, input_output_aliases={w: w for w in range(nw)},
        scratch_shapes=[sem, sem, sem, sem], name="gather_weights",
        compiler_params=pltpu.CompilerParams(has_side_effects=True))(*bufs)


def swap_halves(name, grads):
    nw = len(grads)

    def body(*refs):
        src, got = refs[:nw], refs[nw:2 * nw]
        send, recv = refs[2 * nw:]
        x, y, c = _place()
        cps = []
        for w in range(nw):
            h = src[w].shape[1] // 2
            cp = pltpu.make_async_remote_copy(src_ref=src[w].at[:, pl.ds((1 - c) * h, h), :], dst_ref=got[w],
                                              send_sem=send.at[w], recv_sem=recv.at[w],
                                              device_id=(x, y, 1 - c), device_id_type=MESH)
            cp.start()
            cps.append(cp)
        for cp in cps:
            cp.wait()

    half = [jax.ShapeDtypeStruct((g.shape[0], g.shape[1] // 2, g.shape[2]), g.dtype) for g in grads]
    sem = pltpu.SemaphoreType.DMA((nw,))
    return pl.pallas_call(
        body, out_shape=half, in_specs=[ANY] * nw, out_specs=[ANY] * nw,
        scratch_shapes=[sem, sem], name=name,
        compiler_params=pltpu.CompilerParams(has_side_effects=True))(*grads)


def scatter_to_owners(parts):
    nw = len(parts)

    def body(*refs):
        src, dst = refs[:nw], refs[nw:2 * nw]
        send, recv = refs[2 * nw:]
        x, y, c = _place()
        cps = []
        for w in range(nw):
            for k, (px, py, s) in enumerate(_other_chips(x, y)):
                cp = pltpu.make_async_remote_copy(src_ref=src[w].at[s], dst_ref=dst[w].at[k],
                                                  send_sem=send.at[w, k], recv_sem=recv.at[w, k],
                                                  device_id=(px, py, c), device_id_type=MESH)
                cp.start()
                cps.append(cp)
        for cp in cps:
            cp.wait()

    sem = pltpu.SemaphoreType.DMA((nw, 3))
    return pl.pallas_call(
        body, out_shape=[jax.ShapeDtypeStruct((N_CHIPS - 1,) + p.shape[1:], p.dtype) for p in parts],
        in_specs=[ANY] * nw, out_specs=[ANY] * nw,
        scratch_shapes=[sem, sem], name="scatter_to_owners",
        compiler_params=pltpu.CompilerParams(has_side_effects=True))(*parts)


def join_halves(name, halves):
    nw = len(halves)

    def body(*refs):
        src, dst = refs[:nw], refs[nw:2 * nw]
        send, recv = refs[2 * nw:]
        x, y, c = _place()
        cps = []
        for w in range(nw):
            cp = pltpu.make_async_remote_copy(src_ref=src[w], dst_ref=dst[w], send_sem=send.at[w], recv_sem=recv.at[w],
                                              device_id=(x, y, 1 - c), device_id_type=MESH)
            cp.start()
            cps.append(cp)
        for cp in cps:
            cp.wait()

    sem = pltpu.SemaphoreType.DMA((nw,))
    return pl.pallas_call(
        body, out_shape=[jax.ShapeDtypeStruct(p.shape, p.dtype) for p in halves], in_specs=[ANY] * nw, out_specs=[ANY] * nw,
        scratch_shapes=[sem, sem], name=name,
        compiler_params=pltpu.CompilerParams(has_side_effects=True))(*halves)


SEM = pl.BlockSpec(memory_space=pltpu.SEMAPHORE)
VM = pl.BlockSpec(memory_space=pltpu.VMEM)
DATAFLOW = pltpu.SideEffectType.DATAFLOW_SIDE_EFFECTING
TOKEN = jax.ShapeDtypeStruct((SUBLANES, LANES), f32)


def _gather_copy(buf, w, k, chip, c, mine, send, recv):
    px, py, _ = chip
    h = buf.shape[1] // 2
    half = buf.at[mine, pl.ds(c * h, h), :]
    return pltpu.make_async_remote_copy(src_ref=half, dst_ref=half, send_sem=send.at[3 * w + k], recv_sem=recv.at[3 * w + k],
                                        device_id=(px, py, c), device_id_type=MESH)


def _gather_landing(buf, w, k, chip, c, send, recv):
    px, py, s = chip
    h = buf.shape[1] // 2
    landed = buf.at[s, pl.ds(c * h, h), :]
    return pltpu.make_async_remote_copy(src_ref=landed, dst_ref=landed, send_sem=send.at[3 * w + k], recv_sem=recv.at[3 * w + k],
                                        device_id=(px, py, c), device_id_type=MESH)


def gather_start(name, bufs, groups, after):
    nw, ng = len(bufs), len(groups)

    def body(*refs):
        outs = refs[nw + 1:]
        sems, dst = outs[:2 * ng], outs[2 * ng:2 * ng + nw]
        token = outs[2 * ng + nw]
        x, y, c = _place()
        mine = 2 * x + y
        for g, members in enumerate(groups):
            for i, w in enumerate(members):
                for k, chip in enumerate(_other_chips(x, y)):
                    _gather_copy(dst[w], i, k, chip, c, mine, sems[2 * g], sems[2 * g + 1]).start()
        token[...] = jnp.zeros_like(token)

    sem_shapes = []
    for members in groups:
        sem_shapes += [pltpu.SemaphoreType.DMA((3 * len(members),))] * 2
    outs = pl.pallas_call(
        body, out_shape=sem_shapes + [jax.ShapeDtypeStruct(b.shape, b.dtype) for b in bufs] + [TOKEN],
        in_specs=[ANY] * (nw + 1), out_specs=[SEM] * (2 * ng) + [ANY] * nw + [VM],
        input_output_aliases={w: 2 * ng + w for w in range(nw)}, name=name,
        compiler_params=pltpu.CompilerParams(has_side_effects=DATAFLOW))(*bufs, after)
    return [(outs[2 * g], outs[2 * g + 1]) for g in range(ng)], list(outs[2 * ng:2 * ng + nw]), outs[2 * ng + nw]


def gather_wait(name, bufs, send, recv, after):
    nw = len(bufs)

    def body(*refs):
        src = refs[:nw]
        send_ref, recv_ref = refs[nw], refs[nw + 1]
        x, y, c = _place()
        mine = 2 * x + y
        for w in range(nw):
            for k, chip in enumerate(_other_chips(x, y)):
                _gather_copy(src[w], w, k, chip, c, mine, send_ref, recv_ref).wait_send()
                _gather_landing(src[w], w, k, chip, c, send_ref, recv_ref).wait_recv()

    return pl.pallas_call(
        body, out_shape=[jax.ShapeDtypeStruct(b.shape, b.dtype) for b in bufs],
        in_specs=[ANY] * nw + [SEM, SEM, ANY], out_specs=[ANY] * nw,
        input_output_aliases={w: w for w in range(nw)}, name=name,
        compiler_params=pltpu.CompilerParams(has_side_effects=DATAFLOW))(*bufs, send, recv, after)


def gather_forward(name, bufs):
    nw = len(bufs)

    def body(*refs):
        dst = refs[nw:2 * nw]
        send, recv = refs[2 * nw:]
        x, y, c = _place()
        sib = (x, y, 1 - c)
        cps = []
        for w in range(nw):
            h = dst[w].shape[1] // 2
            for k, (_, _, s) in enumerate(_other_chips(x, y)):
                landed = dst[w].at[s, pl.ds(c * h, h), :]
                cp = pltpu.make_async_remote_copy(src_ref=landed, dst_ref=landed, send_sem=send.at[w, k], recv_sem=recv.at[w, k],
                                                  device_id=sib, device_id_type=MESH)
                cp.start()
                cps.append(cp)
        for w in range(nw):
            h = dst[w].shape[1] // 2
            for k, (_, _, s) in enumerate(_other_chips(x, y)):
                other = dst[w].at[s, pl.ds((1 - c) * h, h), :]
                pltpu.make_async_remote_copy(src_ref=other, dst_ref=other, send_sem=send.at[w, k], recv_sem=recv.at[w, k],
                                             device_id=sib, device_id_type=MESH).wait_recv()
        for cp in cps:
            cp.wait_send()

    sem = pltpu.SemaphoreType.DMA((nw, 3))
    return pl.pallas_call(
        body, out_shape=[jax.ShapeDtypeStruct(b.shape, b.dtype) for b in bufs],
        in_specs=[ANY] * nw, out_specs=[ANY] * nw, input_output_aliases={w: w for w in range(nw)},
        scratch_shapes=[sem, sem], name=name, compiler_params=pltpu.CompilerParams(has_side_effects=True))(*bufs)


def _scatter_copy(src, dst, w, k, chip, c, send, recv):
    px, py, s = chip
    return pltpu.make_async_remote_copy(src_ref=src.at[s], dst_ref=dst.at[k], send_sem=send.at[3 * w + k], recv_sem=recv.at[3 * w + k],
                                        device_id=(px, py, c), device_id_type=MESH)


def scatter_start(name, parts):
    nw = len(parts)
    lands = [pltpu.with_memory_space_constraint(lax.empty((N_CHIPS - 1,) + p.shape[1:], p.dtype), pltpu.HBM) for p in parts]

    def body(*refs):
        outs = refs[2 * nw:]
        send, recv = outs[0], outs[1]
        src, dst, token = outs[2:2 + nw], outs[2 + nw:2 + 2 * nw], outs[2 + 2 * nw]
        x, y, c = _place()
        for w in range(nw):
            for k, chip in enumerate(_other_chips(x, y)):
                _scatter_copy(src[w], dst[w], w, k, chip, c, send, recv).start()
        token[...] = jnp.zeros_like(token)

    sem = pltpu.SemaphoreType.DMA((3 * nw,))
    outs = pl.pallas_call(
        body, out_shape=[sem, sem] + [jax.ShapeDtypeStruct(p.shape, p.dtype) for p in parts]
        + [jax.ShapeDtypeStruct(l.shape, l.dtype) for l in lands] + [TOKEN],
        in_specs=[ANY] * (2 * nw), out_specs=[SEM, SEM] + [ANY] * (2 * nw) + [VM],
        input_output_aliases={i: 2 + i for i in range(2 * nw)}, name=name,
        compiler_params=pltpu.CompilerParams(has_side_effects=DATAFLOW))(*parts, *lands)
    return outs[0], outs[1], list(outs[2:2 + nw]), list(outs[2 + nw:2 + 2 * nw]), outs[2 + 2 * nw]


def scatter_wait(name, parts, lands, send, recv, after):
    nw = len(parts)

    def body(*refs):
        src, dst = refs[:nw], refs[nw:2 * nw]
        send_ref, recv_ref = refs[2 * nw], refs[2 * nw + 1]
        x, y, c = _place()
        for w in range(nw):
            for k, chip in enumerate(_other_chips(x, y)):
                cp = _scatter_copy(src[w], dst[w], w, k, chip, c, send_ref, recv_ref)
                cp.wait_send()
                cp.wait_recv()

    outs = pl.pallas_call(
        body, out_shape=[jax.ShapeDtypeStruct(a.shape, a.dtype) for a in list(parts) + list(lands)],
        in_specs=[ANY] * (2 * nw) + [SEM, SEM, ANY], out_specs=[ANY] * (2 * nw),
        input_output_aliases={i: i for i in range(2 * nw)}, name=name,
        compiler_params=pltpu.CompilerParams(has_side_effects=DATAFLOW))(*parts, *lands, send, recv, after)
    return list(outs[:nw]), list(outs[nw:])


def _sibling_copy(src, dst, w, c, half_rows, send, recv, sib):
    if half_rows:
        h = src.shape[1] // 2
        src = src.at[:, pl.ds((1 - c) * h, h), :]
    return pltpu.make_async_remote_copy(src_ref=src, dst_ref=dst, send_sem=send.at[w], recv_sem=recv.at[w],
                                        device_id=sib, device_id_type=MESH)


def _landing(shape, dtype):
    return pltpu.with_memory_space_constraint(lax.empty(shape, dtype), pltpu.HBM)


def sibling_start(name, srcs, half_rows):
    nw = len(srcs)
    lands = [_landing((s.shape[0], s.shape[1] // 2, s.shape[2]) if half_rows else s.shape, s.dtype) for s in srcs]

    def body(*refs):
        outs = refs[2 * nw:]
        send, recv = outs[0], outs[1]
        src, dst, token = outs[2:2 + nw], outs[2 + nw:2 + 2 * nw], outs[2 + 2 * nw]
        x, y, c = _place()
        for w in range(nw):
            _sibling_copy(src[w], dst[w], w, c, half_rows, send, recv, (x, y, 1 - c)).start()
        token[...] = jnp.zeros_like(token)

    sem = pltpu.SemaphoreType.DMA((nw,))
    outs = pl.pallas_call(
        body, out_shape=[sem, sem] + [jax.ShapeDtypeStruct(a.shape, a.dtype) for a in list(srcs) + lands] + [TOKEN],
        in_specs=[ANY] * (2 * nw), out_specs=[SEM, SEM] + [ANY] * (2 * nw) + [VM],
        input_output_aliases={i: 2 + i for i in range(2 * nw)}, name=name,
        compiler_params=pltpu.CompilerParams(has_side_effects=DATAFLOW))(*srcs, *lands)
    return outs[0], outs[1], list(outs[2:2 + nw]), list(outs[2 + nw:2 + 2 * nw]), outs[2 + 2 * nw]


def sibling_wait(name, srcs, lands, send, recv, half_rows, after):
    nw = len(srcs)

    def body(*refs):
        src, dst = refs[:nw], refs[nw:2 * nw]
        send_ref, recv_ref = refs[2 * nw], refs[2 * nw + 1]
        x, y, c = _place()
        for w in range(nw):
            cp = _sibling_copy(src[w], dst[w], w, c, half_rows, send_ref, recv_ref, (x, y, 1 - c))
            cp.wait_send()
            cp.wait_recv()

    outs = pl.pallas_call(
        body, out_shape=[jax.ShapeDtypeStruct(a.shape, a.dtype) for a in list(srcs) + list(lands)],
        in_specs=[ANY] * (2 * nw) + [SEM, SEM, ANY], out_specs=[ANY] * (2 * nw),
        input_output_aliases={i: i for i in range(2 * nw)}, name=name,
        compiler_params=pltpu.CompilerParams(has_side_effects=DATAFLOW))(*srcs, *lands, send, recv, after)
    return list(outs[:nw]), list(outs[nw:])


def _peer(x, y, c, r):
    return (1 - x if r & 4 else x, 1 - y if r & 2 else y, 1 - c if r & 1 else c)


def _everyone_copy(buf, r, x, y, c, send, recv, landing):
    px, py, pc = _peer(x, y, c, r)
    slot = buf.at[4 * px + 2 * py + pc] if landing else buf.at[4 * x + 2 * y + c]
    return pltpu.make_async_remote_copy(src_ref=slot, dst_ref=slot, send_sem=send.at[r - 1], recv_sem=recv.at[r - 1],
                                        device_id=(px, py, pc), device_id_type=MESH)


def everyone_start(name, buf):
    def body(buf_in, send, recv, buf_ref, token):
        x, y, c = _place()
        for r in range(1, N_DEV):
            _everyone_copy(buf_ref, r, x, y, c, send, recv, False).start()
        token[...] = jnp.zeros_like(token)

    sem = pltpu.SemaphoreType.DMA((N_DEV - 1,))
    return pl.pallas_call(
        body, out_shape=[sem, sem, jax.ShapeDtypeStruct(buf.shape, buf.dtype), TOKEN],
        in_specs=[ANY], out_specs=[SEM, SEM, ANY, VM], input_output_aliases={0: 2}, name=name,
        compiler_params=pltpu.CompilerParams(has_side_effects=DATAFLOW))(buf)


def everyone_wait(name, buf, send, recv, after):
    def body(buf_ref, send_ref, recv_ref, after_ref, out_ref):
        x, y, c = _place()
        for r in range(1, N_DEV):
            _everyone_copy(buf_ref, r, x, y, c, send_ref, recv_ref, False).wait_send()
            _everyone_copy(buf_ref, r, x, y, c, send_ref, recv_ref, True).wait_recv()

    return pl.pallas_call(
        body, out_shape=jax.ShapeDtypeStruct(buf.shape, buf.dtype), in_specs=[ANY, SEM, SEM, ANY], out_specs=ANY,
        input_output_aliases={0: 0}, name=name,
        compiler_params=pltpu.CompilerParams(has_side_effects=DATAFLOW))(buf, send, recv, after)


def all_reduce_small(v):
    R, n = v.shape

    def body(v_ref, o_ref, all_ref, send_sems, recv_sems, local_sem):
        x, y, c = _place()
        me, sib = (x, y, c), (x, y, 1 - c)
        chips = [(1 - x, y), (x, 1 - y), (1 - x, 1 - y)]

        def rows(px, py, pc):
            return all_ref.at[pl.ds((4 * px + 2 * py + pc) * R, R), :]

        def copy(k, block, to, src=None):
            return pltpu.make_async_remote_copy(src_ref=rows(*block) if src is None else src, dst_ref=rows(*block),
                                                send_sem=send_sems.at[k], recv_sem=recv_sems.at[k],
                                                device_id=to, device_id_type=MESH)

        own = pltpu.make_async_copy(v_ref, rows(*me), local_sem)
        own.start()
        first = [copy(0, me, sib, src=v_ref)]
        first += [copy(1 + j, me, (*chip, c), src=v_ref) for j, chip in enumerate(chips)]
        for cp in first:
            cp.start()
        passed = [copy(4 + j, (*chip, c), sib) for j, chip in enumerate(chips)]
        for j, chip in enumerate(chips):
            copy(1 + j, (*chip, c), me).wait_recv()
            passed[j].start()
        copy(0, sib, me).wait_recv()
        for j, chip in enumerate(chips):
            copy(4 + j, (*chip, 1 - c), me).wait_recv()
        for cp in first + passed:
            cp.wait_send()
        own.wait()
        acc = all_ref[pl.ds(0, R), :]
        for d in range(1, N_DEV):
            acc = acc + all_ref[pl.ds(d * R, R), :]
        o_ref[...] = acc

    vm = pl.BlockSpec(memory_space=pltpu.VMEM)
    return pl.pallas_call(
        body, out_shape=jax.ShapeDtypeStruct((R, n), f32), in_specs=[vm], out_specs=vm,
        scratch_shapes=[pltpu.VMEM((N_DEV * R, n), f32), pltpu.SemaphoreType.DMA((7,)), pltpu.SemaphoreType.DMA((7,)),
                        pltpu.SemaphoreType.DMA],
        name="all_reduce_small", compiler_params=pltpu.CompilerParams(vmem_limit_bytes=VMEM_LIMIT, has_side_effects=True))(v)


def local_step(x, pos, tgt, small, d_in, get_w, put_g, first_dep=None):
    L, D = x.shape
    d_kv = N_KV_HEADS * HEAD_DIM
    d_ssm = small["d_skip"].shape[1]
    d_attn = d_in - 2 * d_kv - d_ssm
    big = {}
    G = d_ssm // SSM_GROUP
    N, P = SSM_STATE, SSM_GROUP
    gbf = bf16

    half_dim = HEAD_DIM // 2
    inv_freq = ROPE_THETA ** (-jnp.arange(half_dim, dtype=f32) / half_dim)
    inv_freq = jnp.tile(inv_freq, LANES // half_dim).reshape(1, LANES)
    sink_b = jnp.broadcast_to(small["sinks"].reshape(-1, 1), (small["sinks"].size, LANES))

    spread = jnp.repeat(jnp.eye(G, dtype=f32), P, axis=0)
    logdt_b = jnp.broadcast_to(small["log_dt"].reshape(G, 1), (G, N))
    bt_re = small["b_re"].reshape(G, N, P).transpose(0, 2, 1).reshape(G * P, N)
    bt_im = small["b_im"].reshape(G, N, P).transpose(0, 2, 1).reshape(G * P, N)
    a_re, a_im = small["a_re"].reshape(G, N), small["a_im"].reshape(G, N)
    lam_re, lam_im, bbt_re, bbt_im = ssm_params(a_re, a_im, logdt_b, bt_re, bt_im, spread)
    bd_re = _block_diag(bbt_re.reshape(G, P, N), P, N).astype(bf16)
    bd_im = _block_diag(bbt_im.reshape(G, P, N), P, N).astype(bf16)
    c_re = small["c_re"].reshape(G, P, N).transpose(0, 2, 1)
    c_im = small["c_im"].reshape(G, P, N).transpose(0, 2, 1)
    cd_re = _block_diag(c_re, N, P).astype(bf16)
    cd_im = _block_diag(c_im, N, P).astype(bf16)
    lam_re_l, lam_im_l = _state_layout(lam_re), _state_layout(lam_im)

    def k1(i, nt, xt, g):
        return (rms_fwd(xt, g),)
    xn = rowwise("pre_mix_norm", k1, L, [full(x)], [small["g_pre_mix"]], [(D, bf16)], dep=first_dep)[0]
    big["w_in"] = get_w("w_in", xn)
    proj = mm_nn("proj_in", xn, big["w_in"])
    qr, kk, vv, u_bf = qkv_prep(proj, pos, inv_freq, d_attn, d_kv)
    attn = attn_fwd(qr, kk, vv, sink_b)
    u_cb = (d_attn + 2 * d_kv) // (d_ssm // 2)
    y, z_bf, s_re, s_im = ssm_fwd(u_bf, proj, u_cb, bd_re, bd_im, cd_re, cd_im, lam_re_l, lam_im_l, small["d_skip"])
    big["w_glu"] = get_w("w_glu", z_bf)
    gl = mm_nn("glu_proj", z_bf, big["w_glu"])

    def k6(i, nt, at, yt, glt, bg, ga, gs):
        ssm = gelu(yt) * sigmoid(glt + bg)
        return (jnp.concatenate([rms_fwd(at, ga), rms_fwd(ssm, gs)], axis=1),)
    mixed = rowwise("mix_norms", k6, L, [full(attn), full(y), full(gl)],
                    [small["b_glu"], small["g_attn_out"], small["g_ssm_out"]], [(d_attn + d_ssm, bf16)])[0]
    big["w_o"] = get_w("w_o", mixed)
    mix = mm_nn("proj_out", mixed, big["w_o"])

    def k7(i, nt, xt, mt, gpm, gpf):
        h = xt + rms_fwd(mt, gpm)
        return h, rms_fwd(h, gpf)
    h, hn = rowwise("post_mix", k7, L, [full(x), full(mix)], [small["g_post_mix"], small["g_pre_ffn"]], [(D, f32), (D, bf16)])
    big["w_gate"] = get_w("w_gate", hn)
    big["w_up"] = get_w("w_up", hn)
    gt, up, hid = ffn_hidden(hn, big["w_gate"], big["w_up"])
    d_ff_dim = gt.shape[1]
    big["w_down"] = get_w("w_down", hid)
    ff = mm_nn("ffn_down", hid, big["w_down"], tk=d_ff_dim // 2)

    def k9(i, nt, ht, fft, tt, g):
        out = ht + rms_fwd(fft, g)
        err = out - tt
        per_row = jnp.mean(err * err, axis=-1, keepdims=True)
        loss = 0.5 * jnp.sum(per_row) * jnp.where(_lane((1, LANES)) == 0, 1.0, 0.0)
        d_out = err * (1.0 / D)
        d_ff, dg = rms_bwd(fft, g, d_out)
        return d_out, d_ff, dg, loss
    d_out, d_ff, dg_post_ffn, loss = rowwise("loss_head", k9, L, [full(h), full(ff), full(tgt)], [small["g_post_ffn"]],
                                             [(D, f32), (D, bf16)], reds=[D, LANES])

    d_gt, d_up = ffn_hidden_grad(d_ff, big["w_down"], gt, up)
    token = put_g("w_down", mm_tn("dw_down", hid, d_ff, out_dtype=gbf, tm=d_ff_dim // N_CHIPS))
    d_hn = mm_nt_pair("d_hn", d_gt, big["w_gate"], d_up, big["w_up"], dep=token)
    token = put_g("w_gate", mm_tn("dw_gate", hn, d_gt, shards=N_CHIPS, out_dtype=gbf))
    token = put_g("w_up", mm_tn("dw_up", hn, d_up, shards=N_CHIPS, out_dtype=gbf, dep=token))

    def k11(i, nt, ht, da, do, mt, gpf, gpm):
        dh_n, dg_pf = rms_bwd(ht, gpf, da)
        dh = do + dh_n
        d_mix, dg_pm = rms_bwd(mt, gpm, dh)
        return dh, d_mix, dg_pf, dg_pm
    dh, d_mix, dg_pre_ffn, dg_post_mix = rowwise("post_mix_grad", k11, L, [full(h), full(d_hn), full(d_out), full(mix)],
                                                 [small["g_pre_ffn"], small["g_post_mix"]], [(D, f32), (D, bf16)], reds=[D, D], dep=token)
    d_mixed = mm_nt("d_mixed", d_mix, big["w_o"])
    token = put_g("w_o", mm_tn("dw_o", mixed, d_mix, out_dtype=gbf))

    def k12(i, nt, at, yt, glt, da_n, ds_n, bg, ga, gs):
        z = gelu(yt)
        sg = sigmoid(glt + bg)
        ssm = z * sg
        d_at, dga = rms_bwd(at, ga, da_n)
        d_ssm_t, dgs = rms_bwd(ssm, gs, ds_n)
        d_gl = d_ssm_t * z * sg * (1.0 - sg)
        return d_at, d_ssm_t * sg, d_gl, dga, dgs, colsum(d_gl)
    d_attn_o, dz1, d_gl, dg_attn, dg_ssm, db_glu = rowwise(
        "mix_norms_grad", k12, L, [full(attn), full(y), full(gl), (d_mixed, d_attn, 0, 0), (d_mixed, d_ssm, d_attn // d_ssm, 0)],
        [small["b_glu"], small["g_attn_out"], small["g_ssm_out"]], [(d_attn, f32), (d_ssm, f32), (d_ssm, bf16)],
        reds=[d_attn, d_ssm, d_ssm], dep=token)
    dz2 = mm_nt("d_glu_in", d_gl, big["w_glu"])
    token = put_g("w_glu", mm_tn("dw_glu", z_bf, d_gl, out_dtype=gbf))

    du, dbd_re, dbd_im, dcd_re, dcd_im, dlam_re_l, dlam_im_l, dd_skip = ssm_bwd(
        y, dz1, dz2, u_bf, proj, u_cb, s_re, s_im, bd_re, bd_im, cd_re, cd_im, lam_re_l, lam_im_l, small["d_skip"], dep=token)
    dq, dkk_c, dkk_p, dvv_c, dvv_p, dsink = attn_bwd(qr, kk, vv, sink_b, attn, d_attn_o)
    d_proj = qkv_grad(dq, dkk_c, dkk_p, dvv_c, dvv_p, du, pos, inv_freq)
    d_xn = mm_nt("d_xn", d_proj, big["w_in"])
    token = put_g("w_in", mm_tn("dw_in", xn, d_proj, shards=N_CHIPS, out_dtype=gbf))

    def k17(i, nt, xt, dxn, dht, g):
        dx, dg = rms_bwd(xt, g, dxn)
        return dht + dx, dg
    grad_x, dg_pre_mix = rowwise("pre_mix_grad", k17, L, [full(x), full(d_xn), full(dh)], [small["g_pre_mix"]],
                                 [(D, f32)], reds=[D], dep=token)

    gather = spread.T
    dbbt_re = _block_diag_take(dbd_re, P, N).reshape(G * P, N)
    dbbt_im = _block_diag_take(dbd_im, P, N).reshape(G * P, N)
    d_a_re, d_a_im, d_logdt, dbt_re, dbt_im = ssm_params_grad(
        a_re, a_im, logdt_b, bt_re, bt_im, spread, gather,
        _state_layout_inv(dlam_re_l, G, N), _state_layout_inv(dlam_im_l, G, N), dbbt_re, dbbt_im)
    q_per_kv = d_attn // HEAD_DIM // N_KV_HEADS
    small_grads = {
        "g_pre_mix": dg_pre_mix, "sinks": dsink[:, :q_per_kv, 0].reshape(1, -1),
        "a_re": d_a_re, "a_im": d_a_im, "log_dt": d_logdt.reshape(1, G),
        "b_re": dbt_re, "b_im": dbt_im,
        "c_re": _block_diag_take(dcd_re, N, P).transpose(0, 2, 1), "c_im": _block_diag_take(dcd_im, N, P).transpose(0, 2, 1),
        "d_skip": dd_skip, "b_glu": db_glu, "g_attn_out": dg_attn, "g_ssm_out": dg_ssm,
        "g_post_mix": dg_post_mix, "g_pre_ffn": dg_pre_ffn, "g_post_ffn": dg_post_ffn,
    }
    return loss, grad_x, small_grads


WEIGHTS = ['g_pre_mix', 'w_in', 'sinks', 'a_re', 'a_im', 'log_dt', 'b_re', 'b_im', 'c_re', 'c_im', 'd_skip', 'w_glu', 'b_glu',
           'g_attn_out', 'g_ssm_out', 'w_o', 'g_post_mix', 'g_pre_ffn', 'w_gate', 'w_up', 'w_down', 'g_post_ffn']
BIG = ['w_in', 'w_glu', 'w_o', 'w_gate', 'w_up', 'w_down']
COL_SHARDED = ['w_in', 'w_gate', 'w_up']
SMALL = [n for n in WEIGHTS if n not in BIG]
GATHER_GROUPS = [["w_in"], ["w_glu", "w_o"], ["w_gate", "w_up"], ["w_down"]]
REDUCE_GROUPS = [["w_down", "w_gate", "w_up"], ["w_o", "w_glu", "w_in"]]


PACK_ROWS = 256


def _pack(parts):
    flat = jnp.concatenate([p.reshape(-1) for p in parts])
    pad = (-flat.size) % (PACK_ROWS * LANES)
    return jnp.pad(flat, (0, pad)).reshape(-1, LANES)


TRANSPOSED_VIEW = ("b_re", "b_im")


def small_view(name, a):
    if name in TRANSPOSED_VIEW:
        a = a.transpose(0, 1, 3, 2)
    return a.reshape(-1, a.shape[-1])


def small_unview(name, p, shape):
    if name in TRANSPOSED_VIEW:
        return p.reshape(shape[0], shape[1], shape[3], shape[2]).transpose(0, 1, 3, 2)
    return p.reshape(shape)


def _unpack(packed, shapes):
    flat = packed.reshape(-1)
    out, off = [], 0
    for s in shapes:
        n = int(np.prod(s))
        out.append(flat[off:off + n].reshape(s))
        off += n
    return out


def kernel(x, positions, g_pre_mix, w_in, sinks, a_re, a_im, log_dt, b_re, b_im, c_re, c_im, d_skip, w_glu, b_glu, g_attn_out, g_ssm_out, w_o, g_post_mix, g_pre_ffn, w_gate, w_up, w_down, g_post_ffn, loss_target, m_g_pre_mix, m_w_in, m_sinks, m_a_re, m_a_im, m_log_dt, m_b_re, m_b_im, m_c_re, m_c_im, m_d_skip, m_w_glu, m_b_glu, m_g_attn_out, m_g_ssm_out, m_w_o, m_g_post_mix, m_g_pre_ffn, m_w_gate, m_w_up, m_w_down, m_g_post_ffn, v_g_pre_mix, v_w_in, v_sinks, v_a_re, v_a_im, v_log_dt, v_b_re, v_b_im, v_c_re, v_c_im, v_d_skip, v_w_glu, v_b_glu, v_g_attn_out, v_g_ssm_out, v_w_o, v_g_post_mix, v_g_pre_ffn, v_w_gate, v_w_up, v_w_down, v_g_post_ffn):
    args = dict(locals())
    w = {n: args[n] for n in WEIGHTS}
    m = {n: args["m_" + n] for n in WEIGHTS}
    v = {n: args["v_" + n] for n in WEIGHTS}
    L, D = x.shape[1], x.shape[2]

    ax, ay, ac = _place()
    mine_arr = (2 * ax + ay).astype(jnp.int32).reshape(1)
    c_arr = ac.astype(jnp.int32).reshape(1)

    me_arr = (4 * ax + 2 * ay + ac).astype(jnp.int32).reshape(1)

    bufs = {"w_in": into_slot("cast_w_in", w["w_in"][0], mine_arr, N_CHIPS, bf16)}
    sems, (bufs["w_in"],), token = gather_start("gather_start_in", [bufs["w_in"]], [[0]], mine_arr)
    later = [n for n in BIG if n != "w_in"]
    for n in later:
        bufs[n] = into_slot("cast_" + n, w[n][0], mine_arr, N_CHIPS, bf16, dep=token)
    more, started, token = gather_start("gather_start_rest", [bufs[n] for n in later],
                                        [[later.index(n) for n in g] for g in GATHER_GROUPS[1:]], token)
    sems += more
    bufs.update(zip(later, started))
    ready = set()

    def get_w(n, after):
        if n not in ready:
            gi = [n in g for g in GATHER_GROUPS].index(True)
            members = GATHER_GROUPS[gi]
            landed = gather_wait("gather_wait_%d" % gi, [bufs[k] for k in members], *sems[gi], after)
            bufs.update(zip(members, gather_forward("gather_forward_%d" % gi, landed)))
            ready.update(members)
        g = bufs[n]
        return g if n in COL_SHARDED else g.reshape(g.shape[0] * g.shape[1], g.shape[2])

    swaps, inflight = {}, []

    def put_g(n, g):
        g3 = g if n in COL_SHARDED else g.reshape(N_CHIPS, g.shape[0] // N_CHIPS, g.shape[1])
        swaps[n] = sibling_start("swap_start_" + n, [g3], True)
        for gi, members in enumerate(REDUCE_GROUPS):
            if n == members[-1]:
                last = swaps[n][4]
                pair = []
                for k in members:
                    send, recv, srcs, lands, _ = swaps[k]
                    (src,), (got,) = sibling_wait("swap_wait_" + k, srcs, lands, send, recv, True, last)
                    pair.append(pair_sum("pair_sum_" + k, src, got, c_arr))
                send, recv, parts, lands, tok = scatter_start("scatter_start_%d" % gi, pair)
                inflight.append((members, send, recv, parts, lands))
                return tok
        return swaps[n][4]

    small = {n: w[n].reshape(1, -1) for n in SMALL}
    pos = positions.reshape(L, 1).astype(f32)
    d_in = N_CHIPS * w["w_in"].shape[2]
    loss, grad_x, small_grads = local_step(x[0], pos, loss_target[0], small, d_in, get_w, put_g, first_dep=token)

    shapes = [w[n].shape for n in SMALL]
    blocks = into_slot("small_block", _pack([small_grads[n] for n in SMALL] + [loss]), me_arr, N_DEV, f32)
    small_send, small_recv, blocks, after = everyone_start("small_start", blocks)

    grads, delta, new_m, new_v = {}, {}, {}, {}
    for gi, (members, send, recv, parts, lands) in enumerate(inflight):
        parts, landed = scatter_wait("scatter_wait_%d" % gi, parts, lands, send, recv, after)
        joins, dep = [], None
        for k, p, t in zip(members, parts, landed):
            joins.append(sibling_start("join_start_" + k, [chip_sum("chip_sum_" + k, p, t, mine_arr, dep=dep)], False))
            dep = after = joins[-1][4]
        for n, (send, recv, srcs, lands, _) in zip(members, joins):
            (own,), (sib,) = sibling_wait("join_wait_" + n, srcs, lands, send, recv, False, after)
            g_, d_, m_, v_ = adamw_halves("adamw_" + n, w[n][0], own, sib, m[n][0], v[n][0], c_arr)
            grads[n], delta[n], new_m[n], new_v[n] = g_[None], d_[None], m_[None], v_[None]
            after = v_
    blocks = everyone_wait("small_wait", blocks, small_send, small_recv, after)
    small_sum = sum_slots("small_sum", blocks)
    *small_g, loss = _unpack(small_sum, [small_view(n, w[n]).shape for n in SMALL] + [loss.shape])
    loss = loss[0, 0]
    outs = adamw_many("adamw_small", [small_view(n, w[n]) for n in SMALL], small_g,
                      [small_view(n, m[n]) for n in SMALL], [small_view(n, v[n]) for n in SMALL])
    for t, parts in zip((grads, delta, new_m, new_v), (small_g,) + tuple(outs)):
        t.update({n: small_unview(n, p, w[n].shape) for n, p in zip(SMALL, parts)})

    return (loss, grad_x[None], *[grads[n] for n in WEIGHTS], *[delta[n] for n in WEIGHTS],
            *[new_m[n] for n in WEIGHTS], *[new_v[n] for n in WEIGHTS])
```

```python
import functools
import math

import jax
import jax.numpy as jnp
import numpy as np
from jax import lax
from jax.experimental import pallas as pl
from jax.experimental.pallas import tpu as pltpu

f32 = jnp.float32
bf16 = jnp.bfloat16
HIGHEST = lax.Precision.HIGHEST
MESH = pl.DeviceIdType.MESH

HEAD_DIM = 64
N_KV_HEADS = 4
ATTN_BLOCK = 128
ROPE_THETA = 10000.0
SSM_GROUP = 16
SSM_STATE = 64
RMS_EPS = 1e-6
LANES = 128
SUBLANES = 8
VMEM_LIMIT = 52 * 1024 * 1024
N_CHIPS = 4
N_DEV = 8
NEG = -1e30

ADAM_LR, ADAM_B1, ADAM_B2, ADAM_EPS, ADAM_WD, ADAM_STEP = 0.001, 0.9, 0.999, 1e-08, 0.01, 10

NN = (((1,), (0,)), ((), ()))
NT = (((1,), (1,)), ((), ()))
TN = (((0,), (0,)), ((), ()))


def _params(*sem):
    return pltpu.CompilerParams(dimension_semantics=sem or None, vmem_limit_bytes=VMEM_LIMIT)


def _dot(a, b, dims=NN):
    return lax.dot_general(a, b, dims, preferred_element_type=f32)


def _pick(dim, pref):
    t = min(dim, pref)
    while dim % t:
        t -= LANES
    assert t > 0, (dim, pref)
    return t


ANY = pl.BlockSpec(memory_space=pl.ANY)


def _with_dep(in_specs, operands, dep):
    if dep is None:
        return list(in_specs), list(operands), 0
    return list(in_specs) + [ANY], list(operands) + [dep], 1


def _mm_call(name, grid, in_specs, out_spec, out_shape, acc_shape, dims, operands, dep=None):
    nk = grid[2]
    in_specs, operands, n_dep = _with_dep(in_specs, operands, dep)

    def body_one(a_ref, b_ref, *rest):
        o_ref = rest[n_dep]
        o_ref[...] = _dot(a_ref[...], b_ref[...], dims).astype(o_ref.dtype)

    def body(a_ref, b_ref, *rest):
        o_ref, acc_ref = rest[n_dep], rest[n_dep + 1]
        k = pl.program_id(2)

        @pl.when(k == 0)
        def _():
            acc_ref[...] = _dot(a_ref[...], b_ref[...], dims)

        @pl.when((k > 0) & (k < nk - 1))
        def _():
            acc_ref[...] += _dot(a_ref[...], b_ref[...], dims)

        @pl.when(k == nk - 1)
        def _():
            o_ref[...] = (acc_ref[...] + _dot(a_ref[...], b_ref[...], dims)).astype(o_ref.dtype)

    return pl.pallas_call(
        body_one if nk == 1 else body, out_shape=out_shape, grid=grid, in_specs=in_specs, out_specs=out_spec,
        scratch_shapes=[] if nk == 1 else [pltpu.VMEM(acc_shape, f32)], name=name,
        compiler_params=_params("parallel", "parallel", "arbitrary"))(*operands)


def mm_nt_pair(name, a1, b1, a2, b2, tm=1024, tk=1024, dep=None):
    M = a1.shape[0]
    S, K, n = b1.shape
    tm, tko = _pick(M, tm), _pick(K, tk)
    nk = 2 * S

    def body(a1_ref, b1_ref, a2_ref, b2_ref, *rest):
        o_ref, acc_ref = rest[-2], rest[-1]
        k = pl.program_id(2)

        @pl.when(k == 0)
        def _():
            acc_ref[...] = _dot(a1_ref[...], b1_ref[...], NT)

        @pl.when((k > 0) & (k < S))
        def _():
            acc_ref[...] += _dot(a1_ref[...], b1_ref[...], NT)

        @pl.when((k >= S) & (k < nk - 1))
        def _():
            acc_ref[...] += _dot(a2_ref[...], b2_ref[...], NT)

        @pl.when(k == nk - 1)
        def _():
            o_ref[...] = acc_ref[...] + _dot(a2_ref[...], b2_ref[...], NT)

    first = lambda k: jnp.minimum(k, S - 1)
    second = lambda k: jnp.maximum(k - S, 0)
    in_specs = [pl.BlockSpec((tm, n), lambda i, j, k: (i, first(k))), pl.BlockSpec((None, tko, n), lambda i, j, k: (first(k), j, 0)),
                pl.BlockSpec((tm, n), lambda i, j, k: (i, second(k))), pl.BlockSpec((None, tko, n), lambda i, j, k: (second(k), j, 0))]
    in_specs, operands, _ = _with_dep(in_specs, (a1, b1, a2, b2), dep)
    return pl.pallas_call(
        body, out_shape=jax.ShapeDtypeStruct((M, K), f32), grid=(M // tm, K // tko, nk), in_specs=in_specs,
        out_specs=pl.BlockSpec((tm, tko), lambda i, j, k: (i, j)), scratch_shapes=[pltpu.VMEM((tm, tko), f32)], name=name,
        compiler_params=_params("parallel", "parallel", "arbitrary"))(*operands)


def mm_nn(name, a, b, out_dtype=f32, tm=1024, tn=1024, tk=2048, dep=None):
    M, K = a.shape
    tm, tk = _pick(M, tm), _pick(K, tk)
    if b.ndim == 3:
        S, _, n = b.shape
        tn = _pick(n, 2048)
        per = n // tn
        b_spec = pl.BlockSpec((None, tk, tn), lambda i, j, k: (j // per, k, j % per))
        N = S * n
    else:
        N = b.shape[1]
        tn = _pick(N, tn)
        b_spec = pl.BlockSpec((tk, tn), lambda i, j, k: (k, j))
    grid = (M // tm, N // tn, K // tk)
    return _mm_call(name, grid, [pl.BlockSpec((tm, tk), lambda i, j, k: (i, k)), b_spec],
                    pl.BlockSpec((tm, tn), lambda i, j, k: (i, j)), jax.ShapeDtypeStruct((M, N), out_dtype),
                    (tm, tn), NN, (a, b), dep)


def mm_nt(name, a, b, out_dtype=f32, tm=1024, tn=2048, tk=1024, dep=None):
    M, N = a.shape
    tm = _pick(M, tm)
    if b.ndim == 3:
        S, K, n = b.shape
        tr = _pick(n, 2048)
        per = n // tr
        tko = _pick(K, tk)
        b_spec = pl.BlockSpec((None, tko, tr), lambda i, j, k: (k // per, j, k % per))
    else:
        K = b.shape[0]
        tr = _pick(N, tn)
        tko = _pick(K, tk)
        b_spec = pl.BlockSpec((tko, tr), lambda i, j, k: (j, k))
    grid = (M // tm, K // tko, N // tr)
    return _mm_call(name, grid, [pl.BlockSpec((tm, tr), lambda i, j, k: (i, k)), b_spec],
                    pl.BlockSpec((tm, tko), lambda i, j, k: (i, j)), jax.ShapeDtypeStruct((M, K), out_dtype),
                    (tm, tko), NT, (a, b), dep)


def mm_tn(name, a, b, shards=None, out_dtype=f32, tm=1024, tn=1024, tl=2048, dep=None):
    L, K = a.shape
    N = b.shape[1]
    tl, tko = _pick(L, tl), _pick(K, tm)
    if shards:
        n = N // shards
        tn = _pick(n, 2048)
        per = n // tn
        o_spec = pl.BlockSpec((None, tko, tn), lambda i, j, k: (j // per, i, j % per))
        o_shape = jax.ShapeDtypeStruct((shards, K, n), out_dtype)
    else:
        tn = _pick(N, tn)
        o_spec = pl.BlockSpec((tko, tn), lambda i, j, k: (i, j))
        o_shape = jax.ShapeDtypeStruct((K, N), out_dtype)
    grid = (K // tko, N // tn, L // tl)
    return _mm_call(name, grid, [pl.BlockSpec((tl, tko), lambda i, j, k: (k, i)),
                                 pl.BlockSpec((tl, tn), lambda i, j, k: (k, j))],
                    o_spec, o_shape, (tko, tn), TN, (a, b), dep)


def ffn_hidden(hn, w_gate, w_up, tm=512):
    M, K = hn.shape
    S, _, n = w_gate.shape
    tm = _pick(M, tm)

    def body(a_ref, g_ref, u_ref, gt_ref, up_ref, hid_ref):
        a = a_ref[...]
        g = _dot(a, g_ref[...])
        u = _dot(a, u_ref[...])
        gt_ref[...] = g.astype(bf16)
        up_ref[...] = u.astype(bf16)
        hid_ref[...] = (g * sigmoid(g) * u).astype(bf16)

    w_spec = pl.BlockSpec((None, K, n), lambda s, i: (s, 0, 0))
    o_spec = pl.BlockSpec((tm, n), lambda s, i: (i, s))
    o = jax.ShapeDtypeStruct((M, S * n), bf16)
    return pl.pallas_call(
        body, out_shape=[o, o, o], grid=(S, M // tm), in_specs=[pl.BlockSpec((tm, K), lambda s, i: (i, 0)), w_spec, w_spec],
        out_specs=[o_spec, o_spec, o_spec], name="ffn_hidden", compiler_params=_params("parallel", "parallel"))(hn, w_gate, w_up)


def ffn_hidden_grad(d_ff, w_down, gt, up, tm=512):
    M, D = d_ff.shape
    F = w_down.shape[0]
    n = _pick(F // N_CHIPS, 2048)
    tm = _pick(M, tm)

    def body(a_ref, b_ref, gt_ref, up_ref, dg_ref, du_ref):
        dh = _dot(a_ref[...], b_ref[...], NT)
        g = gt_ref[...].astype(f32)
        sg = sigmoid(g)
        dg_ref[...] = (dh * up_ref[...].astype(f32) * (sg * (1.0 + g * (1.0 - sg)))).astype(bf16)
        du_ref[...] = (dh * (g * sg)).astype(bf16)

    t_spec = pl.BlockSpec((tm, n), lambda j, i: (i, j))
    o = jax.ShapeDtypeStruct((M, F), bf16)
    return pl.pallas_call(
        body, out_shape=[o, o], grid=(F // n, M // tm),
        in_specs=[pl.BlockSpec((tm, D), lambda j, i: (i, 0)), pl.BlockSpec((n, D), lambda j, i: (j, 0)), t_spec, t_spec],
        out_specs=[t_spec, t_spec], name="ffn_hidden_grad", compiler_params=_params("parallel", "parallel"))(d_ff, w_down, gt, up)


def rowwise(name, fn, L, rows, bcast, outs, reds=(), tr=256, dep=None):
    tr = min(tr, L)
    nt = L // tr
    n_rows, n_b, n_o = len(rows), len(bcast), len(outs)
    n_dep = 0 if dep is None else 1

    def body(*refs):
        i = pl.program_id(0)
        ins = [r[...] for r in refs[:n_rows + n_b]]
        res = fn(i, nt, *ins)
        o_refs = refs[n_rows + n_b + n_dep:]
        for k in range(n_o):
            o_refs[k][...] = res[k].astype(o_refs[k].dtype)
        if reds:
            @pl.when(i == 0)
            def _():
                for k in range(len(reds)):
                    o_refs[n_o + k][...] = jnp.zeros_like(o_refs[n_o + k])
            for k in range(len(reds)):
                o_refs[n_o + k][...] += res[n_o + k]

    def row_spec(width, cb, shift):
        if shift:
            return pl.BlockSpec((tr, width), lambda i: (jnp.minimum(i + shift, nt - 1), cb))
        return pl.BlockSpec((tr, width), lambda i: (i, cb))

    in_specs = [row_spec(w, cb, sh) for (_, w, cb, sh) in rows]
    in_specs += [pl.BlockSpec(b.shape, lambda i: (0, 0)) for b in bcast]
    out_specs = [pl.BlockSpec((tr, w), lambda i: (i, 0)) for (w, _) in outs]
    out_specs += [pl.BlockSpec((1, w), lambda i: (0, 0)) for w in reds]
    out_shape = [jax.ShapeDtypeStruct((L, w), dt) for (w, dt) in outs]
    out_shape += [jax.ShapeDtypeStruct((1, w), f32) for w in reds]
    in_specs, operands, _ = _with_dep(in_specs, [r[0] for r in rows] + list(bcast), dep)
    return pl.pallas_call(
        body, out_shape=out_shape, grid=(nt,), in_specs=in_specs, out_specs=out_specs, name=name,
        compiler_params=_params("arbitrary"))(*operands)


def full(a):
    return (a, a.shape[1], 0, 0)


def colsum(v):
    return jnp.sum(v, axis=0, keepdims=True)


def rms_fwd(x, g):
    r = lax.rsqrt(jnp.mean(x * x, axis=-1, keepdims=True) + RMS_EPS)
    return x * r * g


def rms_bwd(x, g, dy):
    r = lax.rsqrt(jnp.mean(x * x, axis=-1, keepdims=True) + RMS_EPS)
    xh = x * r
    dyg = dy * g
    dx = r * (dyg - xh * jnp.mean(dyg * xh, axis=-1, keepdims=True))
    return dx, colsum(dy * xh)


GELU_C = math.sqrt(2.0 / math.pi)


def gelu(y):
    return y * (0.5 * (1.0 + jnp.tanh(GELU_C * (y + 0.044715 * (y * y * y)))))


def gelu_grad(y):
    t = jnp.tanh(GELU_C * (y + 0.044715 * (y * y * y)))
    return 0.5 * (1.0 + t) + 0.5 * y * (1.0 - t * t) * (GELU_C * (1.0 + 3 * 0.044715 * (y * y)))


def sigmoid(v):
    return 1.0 / (1.0 + jnp.exp(-v))


def _lane(shape):
    return lax.broadcasted_iota(jnp.int32, shape, 1)


def _rot_chunk(t, cos, sin_signed):
    first = (_lane(t.shape) % HEAD_DIM) < (HEAD_DIM // 2)
    partner = jnp.where(first, pltpu.roll(t, LANES - HEAD_DIM // 2, 1), pltpu.roll(t, HEAD_DIM // 2, 1))
    return t * cos + partner * sin_signed


def _cos_sin(pos, inv_freq, inverse):
    ang = pos * inv_freq
    cos, sin = jnp.cos(ang), jnp.sin(ang)
    first = (_lane(ang.shape) % HEAD_DIM) < (HEAD_DIM // 2)
    sign = jnp.where(first, -1.0, 1.0) * (-1.0 if inverse else 1.0)
    return cos, sin * sign


def _dup_head(chunk, odd):
    low = _lane(chunk.shape) < HEAD_DIM
    x = jnp.where(low != odd, chunk, 0.0)
    return x + pltpu.roll(x, HEAD_DIM, 1)


def _chunks(v):
    return [v[:, LANES * c:LANES * (c + 1)] for c in range(v.shape[1] // LANES)]


def qkv_prep(proj, pos, inv_freq, d_attn, d_kv):
    L = proj.shape[0]
    d_ssm = proj.shape[1] - d_attn - 2 * d_kv
    half = d_ssm // 2
    scale = 1.0 / math.sqrt(HEAD_DIM)

    def fn(i, nt, q, k, v, u0, u1, p, invf):
        cos, sin = _cos_sin(p, invf, False)
        qr = jnp.concatenate([_rot_chunk(c, cos, sin) for c in _chunks(q)], axis=1) * scale
        kr = [_rot_chunk(c, cos, sin) for c in _chunks(k)]
        kk = jnp.concatenate([_dup_head(c, odd) for c in kr for odd in (False, True)], axis=1)
        vv = jnp.concatenate([_dup_head(c, odd) for c in _chunks(v) for odd in (False, True)], axis=1)
        return qr, kk, vv, jnp.concatenate([u0, u1], axis=1)

    u_cb = (d_attn + 2 * d_kv) // half
    return rowwise("qkv_prep", fn, L,
                   [(proj, d_attn, 0, 0), (proj, d_kv, d_attn // d_kv, 0), (proj, d_kv, d_attn // d_kv + 1, 0),
                    (proj, half, u_cb, 0), (proj, half, u_cb + 1, 0), full(pos)],
                   [inv_freq], [(d_attn, bf16), (2 * d_kv, bf16), (2 * d_kv, bf16), (d_ssm, bf16)])


def qkv_grad(dq, dkk_c, dkk_p, dvv_c, dvv_p, du, pos, inv_freq):
    L, d_attn = dq.shape
    d_kv = dkk_c.shape[1] // 2
    scale = 1.0 / math.sqrt(HEAD_DIM)

    def fold(cur, prev, i, nt):
        t = cur + jnp.where(i < nt - 1, prev, 0.0)
        out = []
        for c in range(t.shape[1] // (2 * LANES)):
            even, odd = t[:, 2 * c * LANES:(2 * c + 1) * LANES], t[:, (2 * c + 1) * LANES:(2 * c + 2) * LANES]
            even, odd = even + pltpu.roll(even, HEAD_DIM, 1), odd + pltpu.roll(odd, HEAD_DIM, 1)
            out.append(jnp.where(_lane(even.shape) < HEAD_DIM, even, odd))
        return out

    def fn(i, nt, dq_t, kc, kp, vc, vp, du_t, p, invf):
        cos, sin = _cos_sin(p, invf, True)
        dq_o = jnp.concatenate([_rot_chunk(c, cos, sin) for c in _chunks(dq_t)], axis=1) * scale
        dk_o = jnp.concatenate([_rot_chunk(c, cos, sin) for c in fold(kc, kp, i, nt)], axis=1)
        dv_o = jnp.concatenate(fold(vc, vp, i, nt), axis=1)
        return (jnp.concatenate([dq_o, dk_o, dv_o, du_t], axis=1),)

    return rowwise("qkv_grad", fn, L,
                   [full(dq), full(dkk_c), (dkk_p, 2 * d_kv, 0, 1), full(dvv_c), (dvv_p, 2 * d_kv, 0, 1), full(du), full(pos)],
                   [inv_freq], [(d_attn + 2 * d_kv + du.shape[1], bf16)], tr=ATTN_BLOCK)[0]


def _attn_specs(L):
    nb = L // ATTN_BLOCK
    B = ATTN_BLOCK
    q_spec = lambda width: pl.BlockSpec((B, width), lambda n: (n, 0))
    prev = lambda width: pl.BlockSpec((B, width), lambda n: (jnp.maximum(n - 1, 0), 0))
    return nb, q_spec, prev


def _attn_mask(n):
    B = ATTN_BLOCK
    row = lax.broadcasted_iota(jnp.int32, (B, 2 * B), 0)
    col = lax.broadcasted_iota(jnp.int32, (B, 2 * B), 1)
    return ((col < B) & (col > row) & (n > 0)) | ((col >= B) & (row >= col - B))


def _attn_probs(qm, kcat, sink, mask):
    s = jnp.where(mask, _dot(qm, kcat, NT), NEG)
    m = jnp.maximum(jnp.max(s, axis=1, keepdims=True), sink)
    p, ps = jnp.exp(s - m), jnp.exp(sink - m)
    inv = 1.0 / (jnp.sum(p, axis=1, keepdims=True) + ps)
    return p, inv, ps


def _attn_heads(q_ref, s_ref, h, q_per_kv):
    low = _lane((ATTN_BLOCK, LANES)) < HEAD_DIM
    heads = []
    for pr in range(h * q_per_kv // 2, (h + 1) * q_per_kv // 2):
        q2 = q_ref[:, LANES * pr:LANES * (pr + 1)]
        for odd in (False, True):
            mine = low != odd
            sink = jnp.max(s_ref[2 * pr + int(odd):2 * pr + int(odd) + 1, :], axis=1, keepdims=True)
            heads.append((pr, mine, jnp.where(mine, q2, jnp.zeros_like(q2)), sink))
    return low, heads


def _kv_block(prev_ref, cur_ref, h):
    return jnp.concatenate([prev_ref[:, LANES * h:LANES * (h + 1)], cur_ref[:, LANES * h:LANES * (h + 1)]], axis=0)


def attn_fwd(qr, kk, vv, sink_b):
    L, d_attn = qr.shape
    nb, q_spec, prev = _attn_specs(L)
    d_kk = kk.shape[1]
    n_kv = d_kk // LANES
    q_per_kv = d_attn // HEAD_DIM // n_kv

    def body(q_ref, kc_ref, kp_ref, vc_ref, vp_ref, s_ref, o_ref):
        mask = _attn_mask(pl.program_id(0))
        for h in range(n_kv):
            kcat, vcat = _kv_block(kp_ref, kc_ref, h), _kv_block(vp_ref, vc_ref, h)
            low, heads = _attn_heads(q_ref, s_ref, h, q_per_kv)
            probs = [_attn_probs(qm, kcat, sink, mask) for (_, _, qm, sink) in heads]
            outs = [_dot(p.astype(bf16), vcat) * inv for (p, inv, _) in probs]
            for i in range(0, len(heads), 2):
                pr = heads[i][0]
                o_ref[:, LANES * pr:LANES * (pr + 1)] = jnp.where(low, outs[i], outs[i + 1])

    return pl.pallas_call(
        body, out_shape=jax.ShapeDtypeStruct((L, d_attn), f32), grid=(nb,),
        in_specs=[q_spec(d_attn), q_spec(d_kk), prev(d_kk), q_spec(d_kk), prev(d_kk), pl.BlockSpec(sink_b.shape, lambda n: (0, 0))],
        out_specs=q_spec(d_attn), name="attn_fwd", compiler_params=_params("arbitrary"))(qr, kk, kk, vv, vv, sink_b)


def attn_bwd(qr, kk, vv, sink_b, attn, d_attn_out):
    L, d_attn = qr.shape
    nb, q_spec, prev = _attn_specs(L)
    d_kk = kk.shape[1]
    n_kv = d_kk // LANES
    q_per_kv = d_attn // HEAD_DIM // n_kv

    def body(q_ref, kc_ref, kp_ref, vc_ref, vp_ref, s_ref, o_ref, do_ref, dq_ref, dkc_ref, dkp_ref, dvc_ref, dvp_ref, ds_ref):
        n = pl.program_id(0)
        B = ATTN_BLOCK
        mask = _attn_mask(n)
        srow = lax.broadcasted_iota(jnp.int32, (SUBLANES, LANES), 0)

        @pl.when(n == 0)
        def _():
            ds_ref[...] = jnp.zeros_like(ds_ref)

        for h in range(n_kv):
            kcat, vcat = _kv_block(kp_ref, kc_ref, h), _kv_block(vp_ref, vc_ref, h)
            low, heads = _attn_heads(q_ref, s_ref, h, q_per_kv)
            probs = [_attn_probs(qm, kcat, sink, mask) for (_, _, qm, sink) in heads]
            dk = jnp.zeros((2 * B, LANES), f32)
            dv = dk
            dsink = jnp.zeros((SUBLANES, LANES), f32)
            dqs = []
            for i, ((pr, mine, qm, _), (p, inv, ps)) in enumerate(zip(heads, probs)):
                do2 = do_ref[:, LANES * pr:LANES * (pr + 1)]
                delta = jnp.sum(jnp.where(mine, do2 * o_ref[:, LANES * pr:LANES * (pr + 1)], 0.0), axis=1, keepdims=True)
                dob = jnp.where(mine, do2, 0.0).astype(bf16)
                p = p * inv
                ds = (p * (_dot(dob, vcat, NT) - delta)).astype(bf16)
                dqs.append(_dot(ds, kcat))
                dk = dk + _dot(ds, qm, TN)
                dv = dv + _dot(p.astype(bf16), dob, TN)
                dsink = dsink + jnp.where(srow == i, -jnp.sum(ps * inv * delta), 0.0)
            for i in range(0, len(heads), 2):
                pr = heads[i][0]
                dq_ref[:, LANES * pr:LANES * (pr + 1)] = jnp.where(low, dqs[i], dqs[i + 1])
            cols = slice(LANES * h, LANES * (h + 1))
            dkp_ref[:, cols] = dk[:B]
            dkc_ref[:, cols] = dk[B:]
            dvp_ref[:, cols] = dv[:B]
            dvc_ref[:, cols] = dv[B:]
            ds_ref[h] += dsink

    kv_shape = jax.ShapeDtypeStruct(kk.shape, f32)
    ds_shape = (n_kv, SUBLANES, LANES)
    return pl.pallas_call(
        body,
        out_shape=[jax.ShapeDtypeStruct((L, d_attn), f32), kv_shape, kv_shape, kv_shape, kv_shape, jax.ShapeDtypeStruct(ds_shape, f32)],
        grid=(nb,),
        in_specs=[q_spec(d_attn), q_spec(d_kk), prev(d_kk), q_spec(d_kk), prev(d_kk), pl.BlockSpec(sink_b.shape, lambda n: (0, 0)),
                  q_spec(d_attn), q_spec(d_attn)],
        out_specs=[q_spec(d_attn)] + [q_spec(d_kk)] * 4 + [pl.BlockSpec(ds_shape, lambda n: (0, 0, 0))],
        name="attn_bwd", compiler_params=_params("arbitrary"))(qr, kk, kk, vv, vv, sink_b, attn, d_attn_out)


SSM_T = 256
NQ = SUBLANES * SSM_STATE // LANES
NJ = SUBLANES


def _strided_put(ref, j, val):
    for q in range(NQ):
        ref.at[q][pl.ds(j, SSM_T, stride=NJ), :] = val[:, LANES * q:LANES * (q + 1)]


def _strided_get(ref, j):
    return jnp.concatenate([ref.at[q][pl.ds(j, SSM_T, stride=NJ), :] for q in range(NQ)], axis=1)


def _ssm_specs(L, rev):
    nt = L // SSM_T
    idx = (lambda i: nt - 1 - i) if rev else (lambda i: i)
    row = lambda w, cb=0: pl.BlockSpec((SSM_T, w), lambda i: (idx(i), cb))
    state = pl.BlockSpec((NQ, SSM_T * NJ, LANES), lambda i: (0, idx(i), 0))
    whole = lambda a: pl.BlockSpec(a.shape, lambda i: (0,) * a.ndim)
    return nt, row, state, whole


def ssm_fwd(u_bf, proj, u_cb, bd_re, bd_im, cd_re, cd_im, lam_re, lam_im, d_skip):
    L, d_ssm = u_bf.shape
    nt, row, state, whole = _ssm_specs(L, False)
    half = d_ssm // 2
    gw = d_ssm // NJ

    def body(u_ref, u0_ref, u1_ref, bdr, bdi, cdr, cdi, lr_ref, li_ref, d_ref, y_ref, z_ref, sr_ref, si_ref, carry):
        i = pl.program_id(0)

        @pl.when(i == 0)
        def _():
            carry[...] = jnp.zeros_like(carry)

        for j in range(NJ):
            uj = u_ref[:, gw * j:gw * (j + 1)]
            _strided_put(sr_ref, j, _dot(uj, bdr[j]))
            _strided_put(si_ref, j, _dot(uj, bdi[j]))
        lr = [lr_ref[q] for q in range(NQ)]
        li = [li_ref[q] for q in range(NQ)]

        def step(t, s):
            sr, si = s
            rows = pl.ds(pl.multiple_of(t * NJ, NJ), NJ)
            nr = tuple(lr[q] * sr[q] - li[q] * si[q] + sr_ref[q, rows, :] for q in range(NQ))
            ni = tuple(lr[q] * si[q] + li[q] * sr[q] + si_ref[q, rows, :] for q in range(NQ))
            for q in range(NQ):
                sr_ref[q, rows, :] = nr[q]
                si_ref[q, rows, :] = ni[q]
            return nr, ni

        init = (tuple(carry[0, q] for q in range(NQ)), tuple(carry[1, q] for q in range(NQ)))
        sr, si = lax.fori_loop(0, SSM_T, step, init, unroll=8)
        for q in range(NQ):
            carry[0, q] = sr[q]
            carry[1, q] = si[q]
        uf = jnp.concatenate([u0_ref[...], u1_ref[...]], axis=1)
        for j in range(NJ):
            cols = slice(gw * j, gw * (j + 1))
            yj = _dot(_strided_get(sr_ref, j).astype(bf16), cdr[j]) - _dot(_strided_get(si_ref, j).astype(bf16), cdi[j])
            yj = yj + d_ref[:, cols] * uf[:, cols]
            y_ref[:, cols] = yj
            z_ref[:, cols] = gelu(yj).astype(bf16)

    s_shape = jax.ShapeDtypeStruct((NQ, L * NJ, LANES), f32)
    consts = (bd_re, bd_im, cd_re, cd_im, lam_re, lam_im, d_skip)
    return pl.pallas_call(
        body, out_shape=[jax.ShapeDtypeStruct((L, d_ssm), f32), jax.ShapeDtypeStruct((L, d_ssm), bf16), s_shape, s_shape], grid=(nt,),
        in_specs=[row(d_ssm), row(half, u_cb), row(half, u_cb + 1)] + [whole(a) for a in consts],
        out_specs=[row(d_ssm), row(d_ssm), state, state],
        scratch_shapes=[pltpu.VMEM((2, NQ, NJ, LANES), f32)], name="ssm_fwd",
        compiler_params=_params("arbitrary"))(u_bf, proj, proj, *consts)


def ssm_bwd(y, dz1, dz2, u_bf, proj, u_cb, s_re, s_im, bd_re, bd_im, cd_re, cd_im, lam_re, lam_im, d_skip, dep=None):
    L, d_ssm = y.shape
    nt, row, state, whole = _ssm_specs(L, True)
    half = d_ssm // 2
    gw = d_ssm // NJ
    n_dep = 0 if dep is None else 1

    def body(y_ref, dz1_ref, dz2_ref, u_ref, u0_ref, u1_ref, sr_ref, si_ref, bdr, bdi, cdr, cdi, lr_ref, li_ref, d_ref, *rest):
        du_ref, dbdr, dbdi, dcdr, dcdi, dlr, dli, dd_ref, gr_ref, gi_ref, carry = rest[n_dep:]
        i = pl.program_id(0)

        @pl.when(i == 0)
        def _():
            carry[...] = jnp.zeros_like(carry)
            for r in (dbdr, dbdi, dcdr, dcdi, dlr, dli, dd_ref):
                r[...] = jnp.zeros_like(r)

        dyf = (dz1_ref[...] + dz2_ref[...]) * gelu_grad(y_ref[...])
        dyb = dyf.astype(bf16)
        for j in range(NJ):
            dyj = dyb[:, gw * j:gw * (j + 1)]
            _strided_put(gr_ref, j, _dot(dyj, cdr[j], NT))
            _strided_put(gi_ref, j, -_dot(dyj, cdi[j], NT))
            dcdr[j] += _dot(_strided_get(sr_ref, j).astype(bf16), dyj, TN)
            dcdi[j] -= _dot(_strided_get(si_ref, j).astype(bf16), dyj, TN)
        lr = [lr_ref[q] for q in range(NQ)]
        li = [li_ref[q] for q in range(NQ)]

        def step(k, c):
            gr, gi, ar, ai = c
            rows = pl.ds(pl.multiple_of((SSM_T - 1 - k) * NJ, NJ), NJ)
            s_r = [sr_ref[q, rows, :] for q in range(NQ)]
            s_i = [si_ref[q, rows, :] for q in range(NQ)]
            ar = tuple(ar[q] + gr[q] * s_r[q] + gi[q] * s_i[q] for q in range(NQ))
            ai = tuple(ai[q] + gi[q] * s_r[q] - gr[q] * s_i[q] for q in range(NQ))
            nr = tuple(gr_ref[q, rows, :] + lr[q] * gr[q] + li[q] * gi[q] for q in range(NQ))
            ni = tuple(gi_ref[q, rows, :] + lr[q] * gi[q] - li[q] * gr[q] for q in range(NQ))
            for q in range(NQ):
                gr_ref[q, rows, :] = nr[q]
                gi_ref[q, rows, :] = ni[q]
            return nr, ni, ar, ai

        zero = tuple(jnp.zeros((NJ, LANES), f32) for _ in range(NQ))
        init = (tuple(carry[0, q] for q in range(NQ)), tuple(carry[1, q] for q in range(NQ)), zero, zero)
        gr, gi, ar, ai = lax.fori_loop(0, SSM_T, step, init, unroll=8)
        for q in range(NQ):
            carry[0, q] = gr[q]
            carry[1, q] = gi[q]
            dlr[q] += ar[q]
            dli[q] += ai[q]
        uf = jnp.concatenate([u0_ref[...], u1_ref[...]], axis=1)
        dd_ref[...] += colsum(dyf * uf)
        for j in range(NJ):
            cols = slice(gw * j, gw * (j + 1))
            gjr, gji = _strided_get(gr_ref, j).astype(bf16), _strided_get(gi_ref, j).astype(bf16)
            du_ref[:, cols] = _dot(gjr, bdr[j], NT) + _dot(gji, bdi[j], NT) + d_ref[:, cols] * dyf[:, cols]
            uj = u_ref[:, cols]
            dbdr[j] += _dot(uj, gjr, TN)
            dbdi[j] += _dot(uj, gji, TN)

    consts = (bd_re, bd_im, cd_re, cd_im, lam_re, lam_im, d_skip)
    acc = lambda a: jax.ShapeDtypeStruct(a.shape, f32)
    outs = [jax.ShapeDtypeStruct((L, d_ssm), f32), acc(bd_re), acc(bd_im), acc(cd_re), acc(cd_im), acc(lam_re), acc(lam_im), acc(d_skip)]
    in_specs, operands, _ = _with_dep(
        [row(d_ssm)] * 4 + [row(half, u_cb), row(half, u_cb + 1), state, state] + [whole(a) for a in consts],
        [y, dz1, dz2, u_bf, proj, proj, s_re, s_im, *consts], dep)
    return pl.pallas_call(
        body, out_shape=outs, grid=(nt,),
        in_specs=in_specs, out_specs=[row(d_ssm)] + [whole(a) for a in consts],
        scratch_shapes=[pltpu.VMEM((NQ, SSM_T * NJ, LANES), f32), pltpu.VMEM((NQ, SSM_T * NJ, LANES), f32),
                        pltpu.VMEM((2, NQ, NJ, LANES), f32)],
        name="ssm_bwd", compiler_params=_params("arbitrary"))(*operands)


def _cmul(ar, ai, br, bi):
    return ar * br - ai * bi, ar * bi + ai * br


def _disc(ar, ai, logdt):
    dt = jnp.exp(logdt)
    mag = jnp.exp(ar * dt)
    lr, li = mag * jnp.cos(ai * dt), mag * jnp.sin(ai * dt)
    den = ar * ar + ai * ai
    nr, ni = lr - 1.0, li
    fr, fi = (nr * ar + ni * ai) / den, (ni * ar - nr * ai) / den
    return dt, lr, li, den, fr, fi


def ssm_params(a_re, a_im, logdt_b, bt_re, bt_im, spread):
    def body(ar_ref, ai_ref, ld_ref, br_ref, bi_ref, sp_ref, lr_ref, li_ref, or_ref, oi_ref):
        _, lr, li, _, fr, fi = _disc(ar_ref[...], ai_ref[...], ld_ref[...])
        lr_ref[...] = lr
        li_ref[...] = li
        fre = jnp.dot(sp_ref[...], fr, precision=HIGHEST, preferred_element_type=f32)
        fie = jnp.dot(sp_ref[...], fi, precision=HIGHEST, preferred_element_type=f32)
        o_r, o_i = _cmul(fre, fie, br_ref[...], bi_ref[...])
        or_ref[...] = o_r
        oi_ref[...] = o_i

    g = jax.ShapeDtypeStruct(a_re.shape, f32)
    b = jax.ShapeDtypeStruct(bt_re.shape, f32)
    return pl.pallas_call(body, out_shape=[g, g, b, b], name="ssm_params",
                          compiler_params=_params())(a_re, a_im, logdt_b, bt_re, bt_im, spread)


def ssm_params_grad(a_re, a_im, logdt_b, bt_re, bt_im, spread, gather, dlam_re, dlam_im, dbt_re, dbt_im):
    def body(ar_ref, ai_ref, ld_ref, br_ref, bi_ref, sp_ref, ga_ref, glr_ref, gli_ref, gbr_ref, gbi_ref,
             dar_ref, dai_ref, dld_ref, dbr_ref, dbi_ref):
        ar, ai = ar_ref[...], ai_ref[...]
        dt, lr, li, den, fr, fi = _disc(ar, ai, ld_ref[...])
        hdot = functools.partial(jnp.dot, precision=HIGHEST, preferred_element_type=f32)
        fre, fie = hdot(sp_ref[...], fr), hdot(sp_ref[...], fi)
        gbr, gbi, br, bi = gbr_ref[...], gbi_ref[...], br_ref[...], bi_ref[...]
        dbr_ref[...], dbi_ref[...] = _cmul(fre, -fie, gbr, gbi)
        t_r, t_i = _cmul(br, -bi, gbr, gbi)
        gfr, gfi = hdot(ga_ref[...], t_r), hdot(ga_ref[...], t_i)
        iwr, iwi = ar / den, -ai / den
        x_r, x_i = _cmul(iwr, -iwi, gfr, gfi)
        glr, gli = glr_ref[...] + x_r, gli_ref[...] + x_i
        q_r, q_i = _cmul(fr, fi, iwr, iwi)
        gwr, gwi = _cmul(-q_r, q_i, gfr, gfi)
        y_r, y_i = _cmul(dt * lr, -dt * li, glr, gli)
        dar_ref[...] = gwr + y_r
        dai_ref[...] = gwi + y_i
        wl_r, wl_i = _cmul(ar, ai, lr, li)
        z_r, _ = _cmul(wl_r, -wl_i, glr, gli)
        dld_ref[...] = jnp.sum(z_r * dt, axis=1, keepdims=True)

    g = jax.ShapeDtypeStruct(a_re.shape, f32)
    b = jax.ShapeDtypeStruct(bt_re.shape, f32)
    return pl.pallas_call(body, out_shape=[g, g, jax.ShapeDtypeStruct((a_re.shape[0], 1), f32), b, b], name="ssm_params_grad",
                          compiler_params=_params())(a_re, a_im, logdt_b, bt_re, bt_im, spread, gather, dlam_re, dlam_im, dbt_re, dbt_im)


def _block_diag(t, rows, cols):
    G = t.shape[0]
    t = t.reshape(G // NJ, NJ, rows, cols)
    eye = jnp.eye(NJ, dtype=t.dtype)
    return jnp.einsum('jgrc,gh->jgrhc', t, eye).reshape(G // NJ, NJ * rows, NJ * cols)


def _block_diag_take(m, rows, cols):
    J = m.shape[0]
    m = m.reshape(J, NJ, rows, NJ, cols)
    idx = jnp.arange(NJ)
    return m[:, idx, :, idx, :].transpose(1, 0, 2, 3).reshape(J * NJ, rows, cols)


def _state_layout(t):
    return t.reshape(NJ, NQ, LANES).transpose(1, 0, 2)


def _state_layout_inv(t, G, N):
    return t.transpose(1, 0, 2).reshape(G, N)


def _tiles2d(shape, budget_rows=128):
    rows, cols = shape
    tr = rows
    if rows > budget_rows:
        tr = budget_rows
        while rows % tr:
            tr -= SUBLANES
    return tr, cols


def _adam_update(w, g, m, v):
    c1 = 1.0 - ADAM_B1 ** ADAM_STEP
    c2 = 1.0 - ADAM_B2 ** ADAM_STEP
    nm = ADAM_B1 * m + (1.0 - ADAM_B1) * g
    nv = ADAM_B2 * v + (1.0 - ADAM_B2) * (g * g)
    delta = -ADAM_LR * ((nm / c1) / (jnp.sqrt(nv / c2) + ADAM_EPS) + ADAM_WD * w)
    return delta, nm, nv


def adamw_many(name, ws, gs, ms, vs):
    n = len(ws)

    def body(*refs):
        w, g, m, v = (refs[k * n:(k + 1) * n] for k in range(4))
        d, nm, nv = (refs[(4 + k) * n:(5 + k) * n] for k in range(3))
        for i in range(n):
            d[i][...], nm[i][...], nv[i][...] = _adam_update(w[i][...], g[i][...], m[i][...], v[i][...])

    o = [jax.ShapeDtypeStruct(a.shape, f32) for a in ws]
    outs = pl.pallas_call(body, out_shape=o * 3, name=name, compiler_params=_params())(*ws, *gs, *ms, *vs)
    return outs[:n], outs[n:2 * n], outs[2 * n:]


def adamw_halves(name, w, own, got, m, v, c_arr):
    h, cols = own.shape
    tr, _ = _tiles2d((h, cols), 128 if cols > 1024 else 256)
    nh = h // tr

    def body(c_ref, w_ref, own_ref, got_ref, m_ref, v_ref, g_ref, d_ref, nm_ref, nv_ref):
        mine = (pl.program_id(0) // nh) == c_ref[0]
        g = jnp.where(mine, own_ref[...], got_ref[...])
        g_ref[...] = g
        d_ref[...], nm_ref[...], nv_ref[...] = _adam_update(w_ref[...], g, m_ref[...], v_ref[...])

    spec = pl.BlockSpec((tr, cols), lambda i, c: (i, 0))
    own_spec = pl.BlockSpec((tr, cols), lambda i, c: (jnp.where(i // nh == c[0], i % nh, 0), 0))
    got_spec = pl.BlockSpec((tr, cols), lambda i, c: (jnp.where(i // nh == c[0], 0, i % nh), 0))
    o = jax.ShapeDtypeStruct(w.shape, f32)
    grid_spec = pltpu.PrefetchScalarGridSpec(num_scalar_prefetch=1, grid=(2 * nh,),
                                             in_specs=[spec, own_spec, got_spec, spec, spec], out_specs=[spec] * 4)
    return pl.pallas_call(body, out_shape=[o, o, o, o], grid_spec=grid_spec, name=name,
                          compiler_params=_params("arbitrary"))(c_arr, w, own, got, m, v)


def pair_sum(name, g, got, c_arr):
    S, h, cols = got.shape
    tr, _ = _tiles2d((h, cols), 1024)
    nh = h // tr

    def body(c_ref, g_ref, o_ref, out_ref):
        out_ref[...] = (g_ref[...].astype(f32) + o_ref[...].astype(f32)).astype(out_ref.dtype)

    spec = pl.BlockSpec((None, tr, cols), lambda s, i, c: (s, i, 0))
    grid_spec = pltpu.PrefetchScalarGridSpec(
        num_scalar_prefetch=1, grid=(S, nh),
        in_specs=[pl.BlockSpec((None, tr, cols), lambda s, i, c: (s, c[0] * nh + i, 0)), spec], out_specs=spec)
    return pl.pallas_call(body, out_shape=jax.ShapeDtypeStruct(got.shape, g.dtype), grid_spec=grid_spec, name=name,
                          compiler_params=_params("parallel", "parallel"))(c_arr, g, got)


def chip_sum(name, pair, landed, mine_arr, dep=None):
    n_in, h, cols = landed.shape
    tr, _ = _tiles2d((h, cols), 512)

    def body(s_ref, p_ref, l_ref, *rest):
        acc = p_ref[...].astype(f32)
        for k in range(n_in):
            acc = acc + l_ref[k].astype(f32)
        rest[-1][...] = acc

    in_specs, operands, _ = _with_dep(
        [pl.BlockSpec((None, tr, cols), lambda i, s: (s[0], i, 0)), pl.BlockSpec((n_in, tr, cols), lambda i, s: (0, i, 0))],
        [pair, landed], dep)
    grid_spec = pltpu.PrefetchScalarGridSpec(num_scalar_prefetch=1, grid=(h // tr,), in_specs=in_specs,
                                             out_specs=pl.BlockSpec((tr, cols), lambda i, s: (i, 0)))
    return pl.pallas_call(body, out_shape=jax.ShapeDtypeStruct((h, cols), f32), grid_spec=grid_spec, name=name,
                          compiler_params=_params("parallel"))(mine_arr, *operands)


def into_slot(name, w, slot_arr, n_slots, dtype, dep=None):
    tr, cols = _tiles2d(w.shape, 256)

    def body(s_ref, w_ref, *rest):
        rest[-1][...] = w_ref[...].astype(dtype)

    in_specs, operands, _ = _with_dep([pl.BlockSpec((tr, cols), lambda i, s: (i, 0))], [w], dep)
    grid_spec = pltpu.PrefetchScalarGridSpec(num_scalar_prefetch=1, grid=(w.shape[0] // tr,), in_specs=in_specs,
                                             out_specs=pl.BlockSpec((None, tr, cols), lambda i, s: (s[0], i, 0)))
    return pl.pallas_call(body, out_shape=jax.ShapeDtypeStruct((n_slots,) + w.shape, dtype), grid_spec=grid_spec, name=name,
                          compiler_params=_params("parallel"))(slot_arr, *operands)


def sum_slots(name, t):
    S, rows, cols = t.shape
    tr, _ = _tiles2d((rows, cols), 256)

    def body(t_ref, o_ref):
        acc = t_ref[0]
        for s in range(1, S):
            acc = acc + t_ref[s]
        o_ref[...] = acc

    return pl.pallas_call(body, out_shape=jax.ShapeDtypeStruct((rows, cols), f32), grid=(rows // tr,),
                          in_specs=[pl.BlockSpec((S, tr, cols), lambda i: (0, i, 0))], out_specs=pl.BlockSpec((tr, cols), lambda i: (i, 0)),
                          name=name, compiler_params=_params("parallel"))(t)


ANY = pl.BlockSpec(memory_space=pl.ANY)


def _place():
    x, y, c = lax.axis_index("x"), lax.axis_index("y"), lax.axis_index("c")
    return x, y, c


def _other_chips(x, y):
    return [(1 - x, y, 2 * (1 - x) + y), (x, 1 - y, 2 * x + 1 - y), (1 - x, 1 - y, 2 * (1 - x) + 1 - y)]


def gather_weights(bufs):
    nw = len(bufs)

    def body(*refs):
        dst = refs[nw:2 * nw]
        send1, recv1, send2, recv2 = refs[2 * nw:]
        x, y, c = _place()
        mine = 2 * x + y
        chips = _other_chips(x, y)
        sib = (x, y, 1 - c)
        first, passed = [], []
        for w in range(nw):
            h = dst[w].shape[1] // 2
            for k, (px, py, _) in enumerate(chips):
                half = dst[w].at[mine, pl.ds(c * h, h), :]
                cp = pltpu.make_async_remote_copy(src_ref=half, dst_ref=half,
                                                  send_sem=send1.at[w, k], recv_sem=recv1.at[w, k],
                                                  device_id=(px, py, c), device_id_type=MESH)
                cp.start()
                first.append(cp)
        for w in range(nw):
            h = dst[w].shape[1] // 2
            for k, (px, py, s) in enumerate(chips):
                landed = dst[w].at[s, pl.ds(c * h, h), :]
                pltpu.make_async_remote_copy(src_ref=landed, dst_ref=landed, send_sem=send1.at[w, k], recv_sem=recv1.at[w, k],
                                             device_id=(px, py, c), device_id_type=MESH).wait_recv()
                cp = pltpu.make_async_remote_copy(src_ref=landed, dst_ref=landed, send_sem=send2.at[w, k], recv_sem=recv2.at[w, k],
                                                  device_id=sib, device_id_type=MESH)
                cp.start()
                passed.append(cp)
        for w in range(nw):
            h = dst[w].shape[1] // 2
            for k, (px, py, s) in enumerate(chips):
                other = dst[w].at[s, pl.ds((1 - c) * h, h), :]
                pltpu.make_async_remote_copy(src_ref=other, dst_ref=other, send_sem=send2.at[w, k], recv_sem=recv2.at[w, k],
                                             device_id=sib, device_id_type=MESH).wait_recv()
        for cp in first + passed:
            cp.wait_send()

    sem = pltpu.SemaphoreType.DMA((nw, 3))
    return pl.pallas_call(
        body, out_shape=[jax.ShapeDtypeStruct(b.shape, b.dtype) for b in bufs],
        in_specs=[ANY] * nw, out_specs=[ANY] * nw, input_output_aliases={w: w for w in range(nw)},
        scratch_shapes=[sem, sem, sem, sem], name="gather_weights",
        compiler_params=pltpu.CompilerParams(has_side_effects=True))(*bufs)


def swap_halves(name, grads):
    nw = len(grads)

    def body(*refs):
        src, got = refs[:nw], refs[nw:2 * nw]
        send, recv = refs[2 * nw:]
        x, y, c = _place()
        cps = []
        for w in range(nw):
            h = src[w].shape[1] // 2
            cp = pltpu.make_async_remote_copy(src_ref=src[w].at[:, pl.ds((1 - c) * h, h), :], dst_ref=got[w],
                                              send_sem=send.at[w], recv_sem=recv.at[w],
                                              device_id=(x, y, 1 - c), device_id_type=MESH)
            cp.start()
            cps.append(cp)
        for cp in cps:
            cp.wait()

    half = [jax.ShapeDtypeStruct((g.shape[0], g.shape[1] // 2, g.shape[2]), g.dtype) for g in grads]
    sem = pltpu.SemaphoreType.DMA((nw,))
    return pl.pallas_call(
        body, out_shape=half, in_specs=[ANY] * nw, out_specs=[ANY] * nw,
        scratch_shapes=[sem, sem], name=name,
        compiler_params=pltpu.CompilerParams(has_side_effects=True))(*grads)


def scatter_to_owners(parts):
    nw = len(parts)

    def body(*refs):
        src, dst = refs[:nw], refs[nw:2 * nw]
        send, recv = refs[2 * nw:]
        x, y, c = _place()
        cps = []
        for w in range(nw):
            for k, (px, py, s) in enumerate(_other_chips(x, y)):
                cp = pltpu.make_async_remote_copy(src_ref=src[w].at[s], dst_ref=dst[w].at[k],
                                                  send_sem=send.at[w, k], recv_sem=recv.at[w, k],
                                                  device_id=(px, py, c), device_id_type=MESH)
                cp.start()
                cps.append(cp)
        for cp in cps:
            cp.wait()

    sem = pltpu.SemaphoreType.DMA((nw, 3))
    return pl.pallas_call(
        body, out_shape=[jax.ShapeDtypeStruct((N_CHIPS - 1,) + p.shape[1:], p.dtype) for p in parts],
        in_specs=[ANY] * nw, out_specs=[ANY] * nw,
        scratch_shapes=[sem, sem], name="scatter_to_owners",
        compiler_params=pltpu.CompilerParams(has_side_effects=True))(*parts)


def join_halves(name, halves):
    nw = len(halves)

    def body(*refs):
        src, dst = refs[:nw], refs[nw:2 * nw]
        send, recv = refs[2 * nw:]
        x, y, c = _place()
        cps = []
        for w in range(nw):
            cp = pltpu.make_async_remote_copy(src_ref=src[w], dst_ref=dst[w], send_sem=send.at[w], recv_sem=recv.at[w],
                                              device_id=(x, y, 1 - c), device_id_type=MESH)
            cp.start()
            cps.append(cp)
        for cp in cps:
            cp.wait()

    sem = pltpu.SemaphoreType.DMA((nw,))
    return pl.pallas_call(
        body, out_shape=[jax.ShapeDtypeStruct(p.shape, p.dtype) for p in halves], in_specs=[ANY] * nw, out_specs=[ANY] * nw,
        scratch_shapes=[sem, sem], name=name,
        compiler_params=pltpu.CompilerParams(has_side_effects=True))(*halves)


SEM = pl.BlockSpec(memory_space=pltpu.SEMAPHORE)
VM = pl.BlockSpec(memory_space=pltpu.VMEM)
DATAFLOW = pltpu.SideEffectType.DATAFLOW_SIDE_EFFECTING
TOKEN = jax.ShapeDtypeStruct((SUBLANES, LANES), f32)


def _gather_copy(buf, w, k, chip, c, mine, send, recv):
    px, py, _ = chip
    h = buf.shape[1] // 2
    half = buf.at[mine, pl.ds(c * h, h), :]
    return pltpu.make_async_remote_copy(src_ref=half, dst_ref=half, send_sem=send.at[3 * w + k], recv_sem=recv.at[3 * w + k],
                                        device_id=(px, py, c), device_id_type=MESH)


def _gather_landing(buf, w, k, chip, c, send, recv):
    px, py, s = chip
    h = buf.shape[1] // 2
    landed = buf.at[s, pl.ds(c * h, h), :]
    return pltpu.make_async_remote_copy(src_ref=landed, dst_ref=landed, send_sem=send.at[3 * w + k], recv_sem=recv.at[3 * w + k],
                                        device_id=(px, py, c), device_id_type=MESH)


def gather_start(name, bufs, groups, after):
    nw, ng = len(bufs), len(groups)

    def body(*refs):
        outs = refs[nw + 1:]
        sems, dst = outs[:2 * ng], outs[2 * ng:2 * ng + nw]
        token = outs[2 * ng + nw]
        x, y, c = _place()
        mine = 2 * x + y
        for g, members in enumerate(groups):
            for i, w in enumerate(members):
                for k, chip in enumerate(_other_chips(x, y)):
                    _gather_copy(dst[w], i, k, chip, c, mine, sems[2 * g], sems[2 * g + 1]).start()
        token[...] = jnp.zeros_like(token)

    sem_shapes = []
    for members in groups:
        sem_shapes += [pltpu.SemaphoreType.DMA((3 * len(members),))] * 2
    outs = pl.pallas_call(
        body, out_shape=sem_shapes + [jax.ShapeDtypeStruct(b.shape, b.dtype) for b in bufs] + [TOKEN],
        in_specs=[ANY] * (nw + 1), out_specs=[SEM] * (2 * ng) + [ANY] * nw + [VM],
        input_output_aliases={w: 2 * ng + w for w in range(nw)}, name=name,
        compiler_params=pltpu.CompilerParams(has_side_effects=DATAFLOW))(*bufs, after)
    return [(outs[2 * g], outs[2 * g + 1]) for g in range(ng)], list(outs[2 * ng:2 * ng + nw]), outs[2 * ng + nw]


def gather_wait(name, bufs, send, recv, after):
    nw = len(bufs)

    def body(*refs):
        src = refs[:nw]
        send_ref, recv_ref = refs[nw], refs[nw + 1]
        x, y, c = _place()
        mine = 2 * x + y
        for w in range(nw):
            for k, chip in enumerate(_other_chips(x, y)):
                _gather_copy(src[w], w, k, chip, c, mine, send_ref, recv_ref).wait_send()
                _gather_landing(src[w], w, k, chip, c, send_ref, recv_ref).wait_recv()

    return pl.pallas_call(
        body, out_shape=[jax.ShapeDtypeStruct(b.shape, b.dtype) for b in bufs],
        in_specs=[ANY] * nw + [SEM, SEM, ANY], out_specs=[ANY] * nw,
        input_output_aliases={w: w for w in range(nw)}, name=name,
        compiler_params=pltpu.CompilerParams(has_side_effects=DATAFLOW))(*bufs, send, recv, after)


def gather_forward(name, bufs):
    nw = len(bufs)

    def body(*refs):
        dst = refs[nw:2 * nw]
        send, recv = refs[2 * nw:]
        x, y, c = _place()
        sib = (x, y, 1 - c)
        cps = []
        for w in range(nw):
            h = dst[w].shape[1] // 2
            for k, (_, _, s) in enumerate(_other_chips(x, y)):
                landed = dst[w].at[s, pl.ds(c * h, h), :]
                cp = pltpu.make_async_remote_copy(src_ref=landed, dst_ref=landed, send_sem=send.at[w, k], recv_sem=recv.at[w, k],
                                                  device_id=sib, device_id_type=MESH)
                cp.start()
                cps.append(cp)
        for w in range(nw):
            h = dst[w].shape[1] // 2
            for k, (_, _, s) in enumerate(_other_chips(x, y)):
                other = dst[w].at[s, pl.ds((1 - c) * h, h), :]
                pltpu.make_async_remote_copy(src_ref=other, dst_ref=other, send_sem=send.at[w, k], recv_sem=recv.at[w, k],
                                             device_id=sib, device_id_type=MESH).wait_recv()
        for cp in cps:
            cp.wait_send()

    sem = pltpu.SemaphoreType.DMA((nw, 3))
    return pl.pallas_call(
        body, out_shape=[jax.ShapeDtypeStruct(b.shape, b.dtype) for b in bufs],
        in_specs=[ANY] * nw, out_specs=[ANY] * nw, input_output_aliases={w: w for w in range(nw)},
        scratch_shapes=[sem, sem], name=name, compiler_params=pltpu.CompilerParams(has_side_effects=True))(*bufs)


def _scatter_copy(src, dst, w, k, chip, c, send, recv):
    px, py, s = chip
    return pltpu.make_async_remote_copy(src_ref=src.at[s], dst_ref=dst.at[k], send_sem=send.at[3 * w + k], recv_sem=recv.at[3 * w + k],
                                        device_id=(px, py, c), device_id_type=MESH)


def scatter_start(name, parts):
    nw = len(parts)
    lands = [pltpu.with_memory_space_constraint(lax.empty((N_CHIPS - 1,) + p.shape[1:], p.dtype), pltpu.HBM) for p in parts]

    def body(*refs):
        outs = refs[2 * nw:]
        send, recv = outs[0], outs[1]
        src, dst, token = outs[2:2 + nw], outs[2 + nw:2 + 2 * nw], outs[2 + 2 * nw]
        x, y, c = _place()
        for w in range(nw):
            for k, chip in enumerate(_other_chips(x, y)):
                _scatter_copy(src[w], dst[w], w, k, chip, c, send, recv).start()
        token[...] = jnp.zeros_like(token)

    sem = pltpu.SemaphoreType.DMA((3 * nw,))
    outs = pl.pallas_call(
        body, out_shape=[sem, sem] + [jax.ShapeDtypeStruct(p.shape, p.dtype) for p in parts]
        + [jax.ShapeDtypeStruct(l.shape, l.dtype) for l in lands] + [TOKEN],
        in_specs=[ANY] * (2 * nw), out_specs=[SEM, SEM] + [ANY] * (2 * nw) + [VM],
        input_output_aliases={i: 2 + i for i in range(2 * nw)}, name=name,
        compiler_params=pltpu.CompilerParams(has_side_effects=DATAFLOW))(*parts, *lands)
    return outs[0], outs[1], list(outs[2:2 + nw]), list(outs[2 + nw:2 + 2 * nw]), outs[2 + 2 * nw]


def scatter_wait(name, parts, lands, send, recv, after):
    nw = len(parts)

    def body(*refs):
        src, dst = refs[:nw], refs[nw:2 * nw]
        send_ref, recv_ref = refs[2 * nw], refs[2 * nw + 1]
        x, y, c = _place()
        for w in range(nw):
            for k, chip in enumerate(_other_chips(x, y)):
                cp = _scatter_copy(src[w], dst[w], w, k, chip, c, send_ref, recv_ref)
                cp.wait_send()
                cp.wait_recv()

    outs = pl.pallas_call(
        body, out_shape=[jax.ShapeDtypeStruct(a.shape, a.dtype) for a in list(parts) + list(lands)],
        in_specs=[ANY] * (2 * nw) + [SEM, SEM, ANY], out_specs=[ANY] * (2 * nw),
        input_output_aliases={i: i for i in range(2 * nw)}, name=name,
        compiler_params=pltpu.CompilerParams(has_side_effects=DATAFLOW))(*parts, *lands, send, recv, after)
    return list(outs[:nw]), list(outs[nw:])


def _sibling_copy(src, dst, w, c, half_rows, send, recv, sib):
    if half_rows:
        h = src.shape[1] // 2
        src = src.at[:, pl.ds((1 - c) * h, h), :]
    return pltpu.make_async_remote_copy(src_ref=src, dst_ref=dst, send_sem=send.at[w], recv_sem=recv.at[w],
                                        device_id=sib, device_id_type=MESH)


def _landing(shape, dtype):
    return pltpu.with_memory_space_constraint(lax.empty(shape, dtype), pltpu.HBM)


def sibling_start(name, srcs, half_rows):
    nw = len(srcs)
    lands = [_landing((s.shape[0], s.shape[1] // 2, s.shape[2]) if half_rows else s.shape, s.dtype) for s in srcs]

    def body(*refs):
        outs = refs[2 * nw:]
        send, recv = outs[0], outs[1]
        src, dst, token = outs[2:2 + nw], outs[2 + nw:2 + 2 * nw], outs[2 + 2 * nw]
        x, y, c = _place()
        for w in range(nw):
            _sibling_copy(src[w], dst[w], w, c, half_rows, send, recv, (x, y, 1 - c)).start()
        token[...] = jnp.zeros_like(token)

    sem = pltpu.SemaphoreType.DMA((nw,))
    outs = pl.pallas_call(
        body, out_shape=[sem, sem] + [jax.ShapeDtypeStruct(a.shape, a.dtype) for a in list(srcs) + lands] + [TOKEN],
        in_specs=[ANY] * (2 * nw), out_specs=[SEM, SEM] + [ANY] * (2 * nw) + [VM],
        input_output_aliases={i: 2 + i for i in range(2 * nw)}, name=name,
        compiler_params=pltpu.CompilerParams(has_side_effects=DATAFLOW))(*srcs, *lands)
    return outs[0], outs[1], list(outs[2:2 + nw]), list(outs[2 + nw:2 + 2 * nw]), outs[2 + 2 * nw]


def sibling_wait(name, srcs, lands, send, recv, half_rows, after):
    nw = len(srcs)

    def body(*refs):
        src, dst = refs[:nw], refs[nw:2 * nw]
        send_ref, recv_ref = refs[2 * nw], refs[2 * nw + 1]
        x, y, c = _place()
        for w in range(nw):
            cp = _sibling_copy(src[w], dst[w], w, c, half_rows, send_ref, recv_ref, (x, y, 1 - c))
            cp.wait_send()
            cp.wait_recv()

    outs = pl.pallas_call(
        body, out_shape=[jax.ShapeDtypeStruct(a.shape, a.dtype) for a in list(srcs) + list(lands)],
        in_specs=[ANY] * (2 * nw) + [SEM, SEM, ANY], out_specs=[ANY] * (2 * nw),
        input_output_aliases={i: i for i in range(2 * nw)}, name=name,
        compiler_params=pltpu.CompilerParams(has_side_effects=DATAFLOW))(*srcs, *lands, send, recv, after)
    return list(outs[:nw]), list(outs[nw:])


def _peer(x, y, c, r):
    return (1 - x if r & 4 else x, 1 - y if r & 2 else y, 1 - c if r & 1 else c)


def _everyone_copy(buf, r, x, y, c, send, recv, landing):
    px, py, pc = _peer(x, y, c, r)
    slot = buf.at[4 * px + 2 * py + pc] if landing else buf.at[4 * x + 2 * y + c]
    return pltpu.make_async_remote_copy(src_ref=slot, dst_ref=slot, send_sem=send.at[r - 1], recv_sem=recv.at[r - 1],
                                        device_id=(px, py, pc), device_id_type=MESH)


def everyone_start(name, buf):
    def body(buf_in, send, recv, buf_ref, token):
        x, y, c = _place()
        for r in range(1, N_DEV):
            _everyone_copy(buf_ref, r, x, y, c, send, recv, False).start()
        token[...] = jnp.zeros_like(token)

    sem = pltpu.SemaphoreType.DMA((N_DEV - 1,))
    return pl.pallas_call(
        body, out_shape=[sem, sem, jax.ShapeDtypeStruct(buf.shape, buf.dtype), TOKEN],
        in_specs=[ANY], out_specs=[SEM, SEM, ANY, VM], input_output_aliases={0: 2}, name=name,
        compiler_params=pltpu.CompilerParams(has_side_effects=DATAFLOW))(buf)


def everyone_wait(name, buf, send, recv, after):
    def body(buf_ref, send_ref, recv_ref, after_ref, out_ref):
        x, y, c = _place()
        for r in range(1, N_DEV):
            _everyone_copy(buf_ref, r, x, y, c, send_ref, recv_ref, False).wait_send()
            _everyone_copy(buf_ref, r, x, y, c, send_ref, recv_ref, True).wait_recv()

    return pl.pallas_call(
        body, out_shape=jax.ShapeDtypeStruct(buf.shape, buf.dtype), in_specs=[ANY, SEM, SEM, ANY], out_specs=ANY,
        input_output_aliases={0: 0}, name=name,
        compiler_params=pltpu.CompilerParams(has_side_effects=DATAFLOW))(buf, send, recv, after)


def all_reduce_small(v):
    R, n = v.shape

    def body(v_ref, o_ref, all_ref, send_sems, recv_sems, local_sem):
        x, y, c = _place()
        me, sib = (x, y, c), (x, y, 1 - c)
        chips = [(1 - x, y), (x, 1 - y), (1 - x, 1 - y)]

        def rows(px, py, pc):
            return all_ref.at[pl.ds((4 * px + 2 * py + pc) * R, R), :]

        def copy(k, block, to, src=None):
            return pltpu.make_async_remote_copy(src_ref=rows(*block) if src is None else src, dst_ref=rows(*block),
                                                send_sem=send_sems.at[k], recv_sem=recv_sems.at[k],
                                                device_id=to, device_id_type=MESH)

        own = pltpu.make_async_copy(v_ref, rows(*me), local_sem)
        own.start()
        first = [copy(0, me, sib, src=v_ref)]
        first += [copy(1 + j, me, (*chip, c), src=v_ref) for j, chip in enumerate(chips)]
        for cp in first:
            cp.start()
        passed = [copy(4 + j, (*chip, c), sib) for j, chip in enumerate(chips)]
        for j, chip in enumerate(chips):
            copy(1 + j, (*chip, c), me).wait_recv()
            passed[j].start()
        copy(0, sib, me).wait_recv()
        for j, chip in enumerate(chips):
            copy(4 + j, (*chip, 1 - c), me).wait_recv()
        for cp in first + passed:
            cp.wait_send()
        own.wait()
        acc = all_ref[pl.ds(0, R), :]
        for d in range(1, N_DEV):
            acc = acc + all_ref[pl.ds(d * R, R), :]
        o_ref[...] = acc

    vm = pl.BlockSpec(memory_space=pltpu.VMEM)
    return pl.pallas_call(
        body, out_shape=jax.ShapeDtypeStruct((R, n), f32), in_specs=[vm], out_specs=vm,
        scratch_shapes=[pltpu.VMEM((N_DEV * R, n), f32), pltpu.SemaphoreType.DMA((7,)), pltpu.SemaphoreType.DMA((7,)),
                        pltpu.SemaphoreType.DMA],
        name="all_reduce_small", compiler_params=pltpu.CompilerParams(vmem_limit_bytes=VMEM_LIMIT, has_side_effects=True))(v)


def local_step(x, pos, tgt, small, d_in, get_w, put_g, first_dep=None):
    L, D = x.shape
    d_kv = N_KV_HEADS * HEAD_DIM
    d_ssm = small["d_skip"].shape[1]
    d_attn = d_in - 2 * d_kv - d_ssm
    big = {}
    G = d_ssm // SSM_GROUP
    N, P = SSM_STATE, SSM_GROUP
    gbf = bf16

    half_dim = HEAD_DIM // 2
    inv_freq = ROPE_THETA ** (-jnp.arange(half_dim, dtype=f32) / half_dim)
    inv_freq = jnp.tile(inv_freq, LANES // half_dim).reshape(1, LANES)
    sink_b = jnp.broadcast_to(small["sinks"].reshape(-1, 1), (small["sinks"].size, LANES))

    spread = jnp.repeat(jnp.eye(G, dtype=f32), P, axis=0)
    logdt_b = jnp.broadcast_to(small["log_dt"].reshape(G, 1), (G, N))
    bt_re = small["b_re"].reshape(G, N, P).transpose(0, 2, 1).reshape(G * P, N)
    bt_im = small["b_im"].reshape(G, N, P).transpose(0, 2, 1).reshape(G * P, N)
    a_re, a_im = small["a_re"].reshape(G, N), small["a_im"].reshape(G, N)
    lam_re, lam_im, bbt_re, bbt_im = ssm_params(a_re, a_im, logdt_b, bt_re, bt_im, spread)
    bd_re = _block_diag(bbt_re.reshape(G, P, N), P, N).astype(bf16)
    bd_im = _block_diag(bbt_im.reshape(G, P, N), P, N).astype(bf16)
    c_re = small["c_re"].reshape(G, P, N).transpose(0, 2, 1)
    c_im = small["c_im"].reshape(G, P, N).transpose(0, 2, 1)
    cd_re = _block_diag(c_re, N, P).astype(bf16)
    cd_im = _block_diag(c_im, N, P).astype(bf16)
    lam_re_l, lam_im_l = _state_layout(lam_re), _state_layout(lam_im)

    def k1(i, nt, xt, g):
        return (rms_fwd(xt, g),)
    xn = rowwise("pre_mix_norm", k1, L, [full(x)], [small["g_pre_mix"]], [(D, bf16)], dep=first_dep)[0]
    big["w_in"] = get_w("w_in", xn)
    proj = mm_nn("proj_in", xn, big["w_in"])
    qr, kk, vv, u_bf = qkv_prep(proj, pos, inv_freq, d_attn, d_kv)
    attn = attn_fwd(qr, kk, vv, sink_b)
    u_cb = (d_attn + 2 * d_kv) // (d_ssm // 2)
    y, z_bf, s_re, s_im = ssm_fwd(u_bf, proj, u_cb, bd_re, bd_im, cd_re, cd_im, lam_re_l, lam_im_l, small["d_skip"])
    big["w_glu"] = get_w("w_glu", z_bf)
    gl = mm_nn("glu_proj", z_bf, big["w_glu"])

    def k6(i, nt, at, yt, glt, bg, ga, gs):
        ssm = gelu(yt) * sigmoid(glt + bg)
        return (jnp.concatenate([rms_fwd(at, ga), rms_fwd(ssm, gs)], axis=1),)
    mixed = rowwise("mix_norms", k6, L, [full(attn), full(y), full(gl)],
                    [small["b_glu"], small["g_attn_out"], small["g_ssm_out"]], [(d_attn + d_ssm, bf16)])[0]
    big["w_o"] = get_w("w_o", mixed)
    mix = mm_nn("proj_out", mixed, big["w_o"])

    def k7(i, nt, xt, mt, gpm, gpf):
        h = xt + rms_fwd(mt, gpm)
        return h, rms_fwd(h, gpf)
    h, hn = rowwise("post_mix", k7, L, [full(x), full(mix)], [small["g_post_mix"], small["g_pre_ffn"]], [(D, f32), (D, bf16)])
    big["w_gate"] = get_w("w_gate", hn)
    big["w_up"] = get_w("w_up", hn)
    gt, up, hid = ffn_hidden(hn, big["w_gate"], big["w_up"])
    d_ff_dim = gt.shape[1]
    big["w_down"] = get_w("w_down", hid)
    ff = mm_nn("ffn_down", hid, big["w_down"], tk=d_ff_dim // 2)

    def k9(i, nt, ht, fft, tt, g):
        out = ht + rms_fwd(fft, g)
        err = out - tt
        per_row = jnp.mean(err * err, axis=-1, keepdims=True)
        loss = 0.5 * jnp.sum(per_row) * jnp.where(_lane((1, LANES)) == 0, 1.0, 0.0)
        d_out = err * (1.0 / D)
        d_ff, dg = rms_bwd(fft, g, d_out)
        return d_out, d_ff, dg, loss
    d_out, d_ff, dg_post_ffn, loss = rowwise("loss_head", k9, L, [full(h), full(ff), full(tgt)], [small["g_post_ffn"]],
                                             [(D, f32), (D, bf16)], reds=[D, LANES])

    d_gt, d_up = ffn_hidden_grad(d_ff, big["w_down"], gt, up)
    token = put_g("w_down", mm_tn("dw_down", hid, d_ff, out_dtype=gbf, tm=d_ff_dim // N_CHIPS))
    d_hn = mm_nt_pair("d_hn", d_gt, big["w_gate"], d_up, big["w_up"], dep=token)
    token = put_g("w_gate", mm_tn("dw_gate", hn, d_gt, shards=N_CHIPS, out_dtype=gbf))
    token = put_g("w_up", mm_tn("dw_up", hn, d_up, shards=N_CHIPS, out_dtype=gbf, dep=token))

    def k11(i, nt, ht, da, do, mt, gpf, gpm):
        dh_n, dg_pf = rms_bwd(ht, gpf, da)
        dh = do + dh_n
        d_mix, dg_pm = rms_bwd(mt, gpm, dh)
        return dh, d_mix, dg_pf, dg_pm
    dh, d_mix, dg_pre_ffn, dg_post_mix = rowwise("post_mix_grad", k11, L, [full(h), full(d_hn), full(d_out), full(mix)],
                                                 [small["g_pre_ffn"], small["g_post_mix"]], [(D, f32), (D, bf16)], reds=[D, D], dep=token)
    d_mixed = mm_nt("d_mixed", d_mix, big["w_o"])
    token = put_g("w_o", mm_tn("dw_o", mixed, d_mix, out_dtype=gbf))

    def k12(i, nt, at, yt, glt, da_n, ds_n, bg, ga, gs):
        z = gelu(yt)
        sg = sigmoid(glt + bg)
        ssm = z * sg
        d_at, dga = rms_bwd(at, ga, da_n)
        d_ssm_t, dgs = rms_bwd(ssm, gs, ds_n)
        d_gl = d_ssm_t * z * sg * (1.0 - sg)
        return d_at, d_ssm_t * sg, d_gl, dga, dgs, colsum(d_gl)
    d_attn_o, dz1, d_gl, dg_attn, dg_ssm, db_glu = rowwise(
        "mix_norms_grad", k12, L, [full(attn), full(y), full(gl), (d_mixed, d_attn, 0, 0), (d_mixed, d_ssm, d_attn // d_ssm, 0)],
        [small["b_glu"], small["g_attn_out"], small["g_ssm_out"]], [(d_attn, f32), (d_ssm, f32), (d_ssm, bf16)],
        reds=[d_attn, d_ssm, d_ssm], dep=token)
    dz2 = mm_nt("d_glu_in", d_gl, big["w_glu"])
    token = put_g("w_glu", mm_tn("dw_glu", z_bf, d_gl, out_dtype=gbf))

    du, dbd_re, dbd_im, dcd_re, dcd_im, dlam_re_l, dlam_im_l, dd_skip = ssm_bwd(
        y, dz1, dz2, u_bf, proj, u_cb, s_re, s_im, bd_re, bd_im, cd_re, cd_im, lam_re_l, lam_im_l, small["d_skip"], dep=token)
    dq, dkk_c, dkk_p, dvv_c, dvv_p, dsink = attn_bwd(qr, kk, vv, sink_b, attn, d_attn_o)
    d_proj = qkv_grad(dq, dkk_c, dkk_p, dvv_c, dvv_p, du, pos, inv_freq)
    d_xn = mm_nt("d_xn", d_proj, big["w_in"])
    token = put_g("w_in", mm_tn("dw_in", xn, d_proj, shards=N_CHIPS, out_dtype=gbf))

    def k17(i, nt, xt, dxn, dht, g):
        dx, dg = rms_bwd(xt, g, dxn)
        return dht + dx, dg
    grad_x, dg_pre_mix = rowwise("pre_mix_grad", k17, L, [full(x), full(d_xn), full(dh)], [small["g_pre_mix"]],
                                 [(D, f32)], reds=[D], dep=token)

    gather = spread.T
    dbbt_re = _block_diag_take(dbd_re, P, N).reshape(G * P, N)
    dbbt_im = _block_diag_take(dbd_im, P, N).reshape(G * P, N)
    d_a_re, d_a_im, d_logdt, dbt_re, dbt_im = ssm_params_grad(
        a_re, a_im, logdt_b, bt_re, bt_im, spread, gather,
        _state_layout_inv(dlam_re_l, G, N), _state_layout_inv(dlam_im_l, G, N), dbbt_re, dbbt_im)
    q_per_kv = d_attn // HEAD_DIM // N_KV_HEADS
    small_grads = {
        "g_pre_mix": dg_pre_mix, "sinks": dsink[:, :q_per_kv, 0].reshape(1, -1),
        "a_re": d_a_re, "a_im": d_a_im, "log_dt": d_logdt.reshape(1, G),
        "b_re": dbt_re, "b_im": dbt_im,
        "c_re": _block_diag_take(dcd_re, N, P).transpose(0, 2, 1), "c_im": _block_diag_take(dcd_im, N, P).transpose(0, 2, 1),
        "d_skip": dd_skip, "b_glu": db_glu, "g_attn_out": dg_attn, "g_ssm_out": dg_ssm,
        "g_post_mix": dg_post_mix, "g_pre_ffn": dg_pre_ffn, "g_post_ffn": dg_post_ffn,
    }
    return loss, grad_x, small_grads


WEIGHTS = ['g_pre_mix', 'w_in', 'sinks', 'a_re', 'a_im', 'log_dt', 'b_re', 'b_im', 'c_re', 'c_im', 'd_skip', 'w_glu', 'b_glu',
           'g_attn_out', 'g_ssm_out', 'w_o', 'g_post_mix', 'g_pre_ffn', 'w_gate', 'w_up', 'w_down', 'g_post_ffn']
BIG = ['w_in', 'w_glu', 'w_o', 'w_gate', 'w_up', 'w_down']
COL_SHARDED = ['w_in', 'w_gate', 'w_up']
SMALL = [n for n in WEIGHTS if n not in BIG]
GATHER_GROUPS = [["w_in"], ["w_glu", "w_o"], ["w_gate", "w_up"], ["w_down"]]
REDUCE_GROUPS = [["w_down", "w_gate", "w_up"], ["w_o", "w_glu", "w_in"]]


PACK_ROWS = 256


def _pack(parts):
    flat = jnp.concatenate([p.reshape(-1) for p in parts])
    pad = (-flat.size) % (PACK_ROWS * LANES)
    return jnp.pad(flat, (0, pad)).reshape(-1, LANES)


TRANSPOSED_VIEW = ("b_re", "b_im")


def small_view(name, a):
    if name in TRANSPOSED_VIEW:
        a = a.transpose(0, 1, 3, 2)
    return a.reshape(-1, a.shape[-1])


def small_unview(name, p, shape):
    if name in TRANSPOSED_VIEW:
        return p.reshape(shape[0], shape[1], shape[3], shape[2]).transpose(0, 1, 3, 2)
    return p.reshape(shape)


def _unpack(packed, shapes):
    flat = packed.reshape(-1)
    out, off = [], 0
    for s in shapes:
        n = int(np.prod(s))
        out.append(flat[off:off + n].reshape(s))
        off += n
    return out


def kernel(x, positions, g_pre_mix, w_in, sinks, a_re, a_im, log_dt, b_re, b_im, c_re, c_im, d_skip, w_glu, b_glu, g_attn_out, g_ssm_out, w_o, g_post_mix, g_pre_ffn, w_gate, w_up, w_down, g_post_ffn, loss_target, m_g_pre_mix, m_w_in, m_sinks, m_a_re, m_a_im, m_log_dt, m_b_re, m_b_im, m_c_re, m_c_im, m_d_skip, m_w_glu, m_b_glu, m_g_attn_out, m_g_ssm_out, m_w_o, m_g_post_mix, m_g_pre_ffn, m_w_gate, m_w_up, m_w_down, m_g_post_ffn, v_g_pre_mix, v_w_in, v_sinks, v_a_re, v_a_im, v_log_dt, v_b_re, v_b_im, v_c_re, v_c_im, v_d_skip, v_w_glu, v_b_glu, v_g_attn_out, v_g_ssm_out, v_w_o, v_g_post_mix, v_g_pre_ffn, v_w_gate, v_w_up, v_w_down, v_g_post_ffn):
    args = dict(locals())
    w = {n: args[n] for n in WEIGHTS}
    m = {n: args["m_" + n] for n in WEIGHTS}
    v = {n: args["v_" + n] for n in WEIGHTS}
    L, D = x.shape[1], x.shape[2]

    ax, ay, ac = _place()
    mine_arr = (2 * ax + ay).astype(jnp.int32).reshape(1)
    c_arr = ac.astype(jnp.int32).reshape(1)

    me_arr = (4 * ax + 2 * ay + ac).astype(jnp.int32).reshape(1)

    bufs = {"w_in": into_slot("cast_w_in", w["w_in"][0], mine_arr, N_CHIPS, bf16)}
    sems, (bufs["w_in"],), token = gather_start("gather_start_in", [bufs["w_in"]], [[0]], mine_arr)
    later = [n for n in BIG if n != "w_in"]
    for n in later:
        bufs[n] = into_slot("cast_" + n, w[n][0], mine_arr, N_CHIPS, bf16, dep=token)
    more, started, token = gather_start("gather_start_rest", [bufs[n] for n in later],
                                        [[later.index(n) for n in g] for g in GATHER_GROUPS[1:]], token)
    sems += more
    bufs.update(zip(later, started))
    ready = set()

    def get_w(n, after):
        if n not in ready:
            gi = [n in g for g in GATHER_GROUPS].index(True)
            members = GATHER_GROUPS[gi]
            landed = gather_wait("gather_wait_%d" % gi, [bufs[k] for k in members], *sems[gi], after)
            bufs.update(zip(members, gather_forward("gather_forward_%d" % gi, landed)))
            ready.update(members)
        g = bufs[n]
        return g if n in COL_SHARDED else g.reshape(g.shape[0] * g.shape[1], g.shape[2])

    swaps, inflight = {}, []

    def put_g(n, g):
        g3 = g if n in COL_SHARDED else g.reshape(N_CHIPS, g.shape[0] // N_CHIPS, g.shape[1])
        swaps[n] = sibling_start("swap_start_" + n, [g3], True)
        for gi, members in enumerate(REDUCE_GROUPS):
            if n == members[-1]:
                last = swaps[n][4]
                pair = []
                for k in members:
                    send, recv, srcs, lands, _ = swaps[k]
                    (src,), (got,) = sibling_wait("swap_wait_" + k, srcs, lands, send, recv, True, last)
                    pair.append(pair_sum("pair_sum_" + k, src, got, c_arr))
                send, recv, parts, lands, tok = scatter_start("scatter_start_%d" % gi, pair)
                inflight.append((members, send, recv, parts, lands))
                return tok
        return swaps[n][4]

    small = {n: w[n].reshape(1, -1) for n in SMALL}
    pos = positions.reshape(L, 1).astype(f32)
    d_in = N_CHIPS * w["w_in"].shape[2]
    loss, grad_x, small_grads = local_step(x[0], pos, loss_target[0], small, d_in, get_w, put_g, first_dep=token)

    shapes = [w[n].shape for n in SMALL]
    blocks = into_slot("small_block", _pack([small_grads[n] for n in SMALL] + [loss]), me_arr, N_DEV, f32)
    small_send, small_recv, blocks, after = everyone_start("small_start", blocks)

    grads, delta, new_m, new_v = {}, {}, {}, {}
    for gi, (members, send, recv, parts, lands) in enumerate(inflight):
        parts, landed = scatter_wait("scatter_wait_%d" % gi, parts, lands, send, recv, after)
        joins, dep = [], None
        for k, p, t in zip(members, parts, landed):
            joins.append(sibling_start("join_start_" + k, [chip_sum("chip_sum_" + k, p, t, mine_arr, dep=dep)], False))
            dep = after = joins[-1][4]
        for n, (send, recv, srcs, lands, _) in zip(members, joins):
            (own,), (sib,) = sibling_wait("join_wait_" + n, srcs, lands, send, recv, False, after)
            g_, d_, m_, v_ = adamw_halves("adamw_" + n, w[n][0], own, sib, m[n][0], v[n][0], c_arr)
            grads[n], delta[n], new_m[n], new_v[n] = g_[None], d_[None], m_[None], v_[None]
            after = v_
    blocks = everyone_wait("small_wait", blocks, small_send, small_recv, after)
    small_sum = sum_slots("small_sum", blocks)
    *small_g, loss = _unpack(small_sum, [small_view(n, w[n]).shape for n in SMALL] + [loss.shape])
    loss = loss[0, 0]
    outs = adamw_many("adamw_small", [small_view(n, w[n]) for n in SMALL], small_g,
                      [small_view(n, m[n]) for n in SMALL], [small_view(n, v[n]) for n in SMALL])
    for t, parts in zip((grads, delta, new_m, new_v), (small_g,) + tuple(outs)):
        t.update({n: small_unview(n, p, w[n].shape) for n, p in zip(SMALL, parts)})

    return (loss, grad_x[None], *[grads[n] for n in WEIGHTS], *[delta[n] for n in WEIGHTS],
            *[new_m[n] for n in WEIGHTS], *[new_v[n] for n in WEIGHTS])
```

```python
import functools
import math

import jax
import jax.numpy as jnp
import numpy as np
from jax import lax
from jax.experimental import pallas as pl
from jax.experimental.pallas import tpu as pltpu

f32 = jnp.float32
bf16 = jnp.bfloat16
HIGHEST = lax.Precision.HIGHEST
MESH = pl.DeviceIdType.MESH

HEAD_DIM = 64
N_KV_HEADS = 4
ATTN_BLOCK = 128
ROPE_THETA = 10000.0
SSM_GROUP = 16
SSM_STATE = 64
RMS_EPS = 1e-6
LANES = 128
SUBLANES = 8
VMEM_LIMIT = 52 * 1024 * 1024
N_CHIPS = 4
N_DEV = 8
NEG = -1e30

ADAM_LR, ADAM_B1, ADAM_B2, ADAM_EPS, ADAM_WD, ADAM_STEP = 0.001, 0.9, 0.999, 1e-08, 0.01, 10

NN = (((1,), (0,)), ((), ()))
NT = (((1,), (1,)), ((), ()))
TN = (((0,), (0,)), ((), ()))


def _params(*sem):
    return pltpu.CompilerParams(dimension_semantics=sem or None, vmem_limit_bytes=VMEM_LIMIT)


def _dot(a, b, dims=NN):
    return lax.dot_general(a, b, dims, preferred_element_type=f32)


def _pick(dim, pref):
    t = min(dim, pref)
    while dim % t:
        t -= LANES
    assert t > 0, (dim, pref)
    return t


ANY = pl.BlockSpec(memory_space=pl.ANY)


def _with_dep(in_specs, operands, dep):
    if dep is None:
        return list(in_specs), list(operands), 0
    return list(in_specs) + [ANY], list(operands) + [dep], 1


def _mm_call(name, grid, in_specs, out_spec, out_shape, acc_shape, dims, operands, dep=None):
    nk = grid[2]
    in_specs, operands, n_dep = _with_dep(in_specs, operands, dep)

    def body_one(a_ref, b_ref, *rest):
        o_ref = rest[n_dep]
        o_ref[...] = _dot(a_ref[...], b_ref[...], dims).astype(o_ref.dtype)

    def body(a_ref, b_ref, *rest):
        o_ref, acc_ref = rest[n_dep], rest[n_dep + 1]
        k = pl.program_id(2)

        @pl.when(k == 0)
        def _():
            acc_ref[...] = _dot(a_ref[...], b_ref[...], dims)

        @pl.when((k > 0) & (k < nk - 1))
        def _():
            acc_ref[...] += _dot(a_ref[...], b_ref[...], dims)

        @pl.when(k == nk - 1)
        def _():
            o_ref[...] = (acc_ref[...] + _dot(a_ref[...], b_ref[...], dims)).astype(o_ref.dtype)

    return pl.pallas_call(
        body_one if nk == 1 else body, out_shape=out_shape, grid=grid, in_specs=in_specs, out_specs=out_spec,
        scratch_shapes=[] if nk == 1 else [pltpu.VMEM(acc_shape, f32)], name=name,
        compiler_params=_params("parallel", "parallel", "arbitrary"))(*operands)


def mm_nt_pair(name, a1, b1, a2, b2, tm=1024, tk=1024, dep=None):
    M = a1.shape[0]
    S, K, n = b1.shape
    tm, tko = _pick(M, tm), _pick(K, tk)
    nk = 2 * S

    def body(a1_ref, b1_ref, a2_ref, b2_ref, *rest):
        o_ref, acc_ref = rest[-2], rest[-1]
        k = pl.program_id(2)

        @pl.when(k == 0)
        def _():
            acc_ref[...] = _dot(a1_ref[...], b1_ref[...], NT)

        @pl.when((k > 0) & (k < S))
        def _():
            acc_ref[...] += _dot(a1_ref[...], b1_ref[...], NT)

        @pl.when((k >= S) & (k < nk - 1))
        def _():
            acc_ref[...] += _dot(a2_ref[...], b2_ref[...], NT)

        @pl.when(k == nk - 1)
        def _():
            o_ref[...] = acc_ref[...] + _dot(a2_ref[...], b2_ref[...], NT)

    first = lambda k: jnp.minimum(k, S - 1)
    second = lambda k: jnp.maximum(k - S, 0)
    in_specs = [pl.BlockSpec((tm, n), lambda i, j, k: (i, first(k))), pl.BlockSpec((None, tko, n), lambda i, j, k: (first(k), j, 0)),
                pl.BlockSpec((tm, n), lambda i, j, k: (i, second(k))), pl.BlockSpec((None, tko, n), lambda i, j, k: (second(k), j, 0))]
    in_specs, operands, _ = _with_dep(in_specs, (a1, b1, a2, b2), dep)
    return pl.pallas_call(
        body, out_shape=jax.ShapeDtypeStruct((M, K), f32), grid=(M // tm, K // tko, nk), in_specs=in_specs,
        out_specs=pl.BlockSpec((tm, tko), lambda i, j, k: (i, j)), scratch_shapes=[pltpu.VMEM((tm, tko), f32)], name=name,
        compiler_params=_params("parallel", "parallel", "arbitrary"))(*operands)


def mm_nn(name, a, b, out_dtype=f32, tm=1024, tn=1024, tk=2048, dep=None):
    M, K = a.shape
    tm, tk = _pick(M, tm), _pick(K, tk)
    if b.ndim == 3:
        S, _, n = b.shape
        tn = _pick(n, 2048)
        per = n // tn
        b_spec = pl.BlockSpec((None, tk, tn), lambda i, j, k: (j // per, k, j % per))
        N = S * n
    else:
        N = b.shape[1]
        tn = _pick(N, tn)
        b_spec = pl.BlockSpec((tk, tn), lambda i, j, k: (k, j))
    grid = (M // tm, N // tn, K // tk)
    return _mm_call(name, grid, [pl.BlockSpec((tm, tk), lambda i, j, k: (i, k)), b_spec],
                    pl.BlockSpec((tm, tn), lambda i, j, k: (i, j)), jax.ShapeDtypeStruct((M, N), out_dtype),
                    (tm, tn), NN, (a, b), dep)


def mm_nt(name, a, b, out_dtype=f32, tm=1024, tn=2048, tk=1024, dep=None):
    M, N = a.shape
    tm = _pick(M, tm)
    if b.ndim == 3:
        S, K, n = b.shape
        tr = _pick(n, 2048)
        per = n // tr
        tko = _pick(K, tk)
        b_spec = pl.BlockSpec((None, tko, tr), lambda i, j, k: (k // per, j, k % per))
    else:
        K = b.shape[0]
        tr = _pick(N, tn)
        tko = _pick(K, tk)
        b_spec = pl.BlockSpec((tko, tr), lambda i, j, k: (j, k))
    grid = (M // tm, K // tko, N // tr)
    return _mm_call(name, grid, [pl.BlockSpec((tm, tr), lambda i, j, k: (i, k)), b_spec],
                    pl.BlockSpec((tm, tko), lambda i, j, k: (i, j)), jax.ShapeDtypeStruct((M, K), out_dtype),
                    (tm, tko), NT, (a, b), dep)


def mm_tn(name, a, b, shards=None, out_dtype=f32, tm=1024, tn=1024, tl=2048, dep=None):
    L, K = a.shape
    N = b.shape[1]
    tl, tko = _pick(L, tl), _pick(K, tm)
    if shards:
        n = N // shards
        tn = _pick(n, 2048)
        per = n // tn
        o_spec = pl.BlockSpec((None, tko, tn), lambda i, j, k: (j // per, i, j % per))
        o_shape = jax.ShapeDtypeStruct((shards, K, n), out_dtype)
    else:
        tn = _pick(N, tn)
        o_spec = pl.BlockSpec((tko, tn), lambda i, j, k: (i, j))
        o_shape = jax.ShapeDtypeStruct((K, N), out_dtype)
    grid = (K // tko, N // tn, L // tl)
    return _mm_call(name, grid, [pl.BlockSpec((tl, tko), lambda i, j, k: (k, i)),
                                 pl.BlockSpec((tl, tn), lambda i, j, k: (k, j))],
                    o_spec, o_shape, (tko, tn), TN, (a, b), dep)


def ffn_hidden(hn, w_gate, w_up, tm=512):
    M, K = hn.shape
    S, _, n = w_gate.shape
    tm = _pick(M, tm)

    def body(a_ref, g_ref, u_ref, gt_ref, up_ref, hid_ref):
        a = a_ref[...]
        g = _dot(a, g_ref[...])
        u = _dot(a, u_ref[...])
        gt_ref[...] = g.astype(bf16)
        up_ref[...] = u.astype(bf16)
        hid_ref[...] = (g * sigmoid(g) * u).astype(bf16)

    w_spec = pl.BlockSpec((None, K, n), lambda s, i: (s, 0, 0))
    o_spec = pl.BlockSpec((tm, n), lambda s, i: (i, s))
    o = jax.ShapeDtypeStruct((M, S * n), bf16)
    return pl.pallas_call(
        body, out_shape=[o, o, o], grid=(S, M // tm), in_specs=[pl.BlockSpec((tm, K), lambda s, i: (i, 0)), w_spec, w_spec],
        out_specs=[o_spec, o_spec, o_spec], name="ffn_hidden", compiler_params=_params("parallel", "parallel"))(hn, w_gate, w_up)


def ffn_hidden_grad(d_ff, w_down, gt, up, tm=512):
    M, D = d_ff.shape
    F = w_down.shape[0]
    n = _pick(F // N_CHIPS, 2048)
    tm = _pick(M, tm)

    def body(a_ref, b_ref, gt_ref, up_ref, dg_ref, du_ref):
        dh = _dot(a_ref[...], b_ref[...], NT)
        g = gt_ref[...].astype(f32)
        sg = sigmoid(g)
        dg_ref[...] = (dh * up_ref[...].astype(f32) * (sg * (1.0 + g * (1.0 - sg)))).astype(bf16)
        du_ref[...] = (dh * (g * sg)).astype(bf16)

    t_spec = pl.BlockSpec((tm, n), lambda j, i: (i, j))
    o = jax.ShapeDtypeStruct((M, F), bf16)
    return pl.pallas_call(
        body, out_shape=[o, o], grid=(F // n, M // tm),
        in_specs=[pl.BlockSpec((tm, D), lambda j, i: (i, 0)), pl.BlockSpec((n, D), lambda j, i: (j, 0)), t_spec, t_spec],
        out_specs=[t_spec, t_spec], name="ffn_hidden_grad", compiler_params=_params("parallel", "parallel"))(d_ff, w_down, gt, up)


def rowwise(name, fn, L, rows, bcast, outs, reds=(), tr=256, dep=None):
    tr = min(tr, L)
    nt = L // tr
    n_rows, n_b, n_o = len(rows), len(bcast), len(outs)
    n_dep = 0 if dep is None else 1

    def body(*refs):
        i = pl.program_id(0)
        ins = [r[...] for r in refs[:n_rows + n_b]]
        res = fn(i, nt, *ins)
        o_refs = refs[n_rows + n_b + n_dep:]
        for k in range(n_o):
            o_refs[k][...] = res[k].astype(o_refs[k].dtype)
        if reds:
            @pl.when(i == 0)
            def _():
                for k in range(len(reds)):
                    o_refs[n_o + k][...] = jnp.zeros_like(o_refs[n_o + k])
            for k in range(len(reds)):
                o_refs[n_o + k][...] += res[n_o + k]

    def row_spec(width, cb, shift):
        if shift:
            return pl.BlockSpec((tr, width), lambda i: (jnp.minimum(i + shift, nt - 1), cb))
        return pl.BlockSpec((tr, width), lambda i: (i, cb))

    in_specs = [row_spec(w, cb, sh) for (_, w, cb, sh) in rows]
    in_specs += [pl.BlockSpec(b.shape, lambda i: (0, 0)) for b in bcast]
    out_specs = [pl.BlockSpec((tr, w), lambda i: (i, 0)) for (w, _) in outs]
    out_specs += [pl.BlockSpec((1, w), lambda i: (0, 0)) for w in reds]
    out_shape = [jax.ShapeDtypeStruct((L, w), dt) for (w, dt) in outs]
    out_shape += [jax.ShapeDtypeStruct((1, w), f32) for w in reds]
    in_specs, operands, _ = _with_dep(in_specs, [r[0] for r in rows] + list(bcast), dep)
    return pl.pallas_call(
        body, out_shape=out_shape, grid=(nt,), in_specs=in_specs, out_specs=out_specs, name=name,
        compiler_params=_params("arbitrary"))(*operands)


def full(a):
    return (a, a.shape[1], 0, 0)


def colsum(v):
    return jnp.sum(v, axis=0, keepdims=True)


def rms_fwd(x, g):
    r = lax.rsqrt(jnp.mean(x * x, axis=-1, keepdims=True) + RMS_EPS)
    return x * r * g


def rms_bwd(x, g, dy):
    r = lax.rsqrt(jnp.mean(x * x, axis=-1, keepdims=True) + RMS_EPS)
    xh = x * r
    dyg = dy * g
    dx = r * (dyg - xh * jnp.mean(dyg * xh, axis=-1, keepdims=True))
    return dx, colsum(dy * xh)


GELU_C = math.sqrt(2.0 / math.pi)


def gelu(y):
    return y * (0.5 * (1.0 + jnp.tanh(GELU_C * (y + 0.044715 * (y * y * y)))))


def gelu_grad(y):
    t = jnp.tanh(GELU_C * (y + 0.044715 * (y * y * y)))
    return 0.5 * (1.0 + t) + 0.5 * y * (1.0 - t * t) * (GELU_C * (1.0 + 3 * 0.044715 * (y * y)))


def sigmoid(v):
    return 1.0 / (1.0 + jnp.exp(-v))


def _lane(shape):
    return lax.broadcasted_iota(jnp.int32, shape, 1)


def _rot_chunk(t, cos, sin_signed):
    first = (_lane(t.shape) % HEAD_DIM) < (HEAD_DIM // 2)
    partner = jnp.where(first, pltpu.roll(t, LANES - HEAD_DIM // 2, 1), pltpu.roll(t, HEAD_DIM // 2, 1))
    return t * cos + partner * sin_signed


def _cos_sin(pos, inv_freq, inverse):
    ang = pos * inv_freq
    cos, sin = jnp.cos(ang), jnp.sin(ang)
    first = (_lane(ang.shape) % HEAD_DIM) < (HEAD_DIM // 2)
    sign = jnp.where(first, -1.0, 1.0) * (-1.0 if inverse else 1.0)
    return cos, sin * sign


def _dup_head(chunk, odd):
    low = _lane(chunk.shape) < HEAD_DIM
    x = jnp.where(low != odd, chunk, 0.0)
    return x + pltpu.roll(x, HEAD_DIM, 1)


def _chunks(v):
    return [v[:, LANES * c:LANES * (c + 1)] for c in range(v.shape[1] // LANES)]


def qkv_prep(proj, pos, inv_freq, d_attn, d_kv):
    L = proj.shape[0]
    d_ssm = proj.shape[1] - d_attn - 2 * d_kv
    half = d_ssm // 2
    scale = 1.0 / math.sqrt(HEAD_DIM)

    def fn(i, nt, q, k, v, u0, u1, p, invf):
        cos, sin = _cos_sin(p, invf, False)
        qr = jnp.concatenate([_rot_chunk(c, cos, sin) for c in _chunks(q)], axis=1) * scale
        kr = [_rot_chunk(c, cos, sin) for c in _chunks(k)]
        kk = jnp.concatenate([_dup_head(c, odd) for c in kr for odd in (False, True)], axis=1)
        vv = jnp.concatenate([_dup_head(c, odd) for c in _chunks(v) for odd in (False, True)], axis=1)
        return qr, kk, vv, jnp.concatenate([u0, u1], axis=1)

    u_cb = (d_attn + 2 * d_kv) // half
    return rowwise("qkv_prep", fn, L,
                   [(proj, d_attn, 0, 0), (proj, d_kv, d_attn // d_kv, 0), (proj, d_kv, d_attn // d_kv + 1, 0),
                    (proj, half, u_cb, 0), (proj, half, u_cb + 1, 0), full(pos)],
                   [inv_freq], [(d_attn, bf16), (2 * d_kv, bf16), (2 * d_kv, bf16), (d_ssm, bf16)])


def qkv_grad(dq, dkk_c, dkk_p, dvv_c, dvv_p, du, pos, inv_freq):
    L, d_attn = dq.shape
    d_kv = dkk_c.shape[1] // 2
    scale = 1.0 / math.sqrt(HEAD_DIM)

    def fold(cur, prev, i, nt):
        t = cur + jnp.where(i < nt - 1, prev, 0.0)
        out = []
        for c in range(t.shape[1] // (2 * LANES)):
            even, odd = t[:, 2 * c * LANES:(2 * c + 1) * LANES], t[:, (2 * c + 1) * LANES:(2 * c + 2) * LANES]
            even, odd = even + pltpu.roll(even, HEAD_DIM, 1), odd + pltpu.roll(odd, HEAD_DIM, 1)
            out.append(jnp.where(_lane(even.shape) < HEAD_DIM, even, odd))
        return out

    def fn(i, nt, dq_t, kc, kp, vc, vp, du_t, p, invf):
        cos, sin = _cos_sin(p, invf, True)
        dq_o = jnp.concatenate([_rot_chunk(c, cos, sin) for c in _chunks(dq_t)], axis=1) * scale
        dk_o = jnp.concatenate([_rot_chunk(c, cos, sin) for c in fold(kc, kp, i, nt)], axis=1)
        dv_o = jnp.concatenate(fold(vc, vp, i, nt), axis=1)
        return (jnp.concatenate([dq_o, dk_o, dv_o, du_t], axis=1),)

    return rowwise("qkv_grad", fn, L,
                   [full(dq), full(dkk_c), (dkk_p, 2 * d_kv, 0, 1), full(dvv_c), (dvv_p, 2 * d_kv, 0, 1), full(du), full(pos)],
                   [inv_freq], [(d_attn + 2 * d_kv + du.shape[1], bf16)], tr=ATTN_BLOCK)[0]


def _attn_specs(L):
    nb = L // ATTN_BLOCK
    B = ATTN_BLOCK
    q_spec = lambda width: pl.BlockSpec((B, width), lambda n: (n, 0))
    prev = lambda width: pl.BlockSpec((B, width), lambda n: (jnp.maximum(n - 1, 0), 0))
    return nb, q_spec, prev


def _attn_mask(n):
    B = ATTN_BLOCK
    row = lax.broadcasted_iota(jnp.int32, (B, 2 * B), 0)
    col = lax.broadcasted_iota(jnp.int32, (B, 2 * B), 1)
    return ((col < B) & (col > row) & (n > 0)) | ((col >= B) & (row >= col - B))


def _attn_probs(qm, kcat, sink, mask):
    s = jnp.where(mask, _dot(qm, kcat, NT), NEG)
    m = jnp.maximum(jnp.max(s, axis=1, keepdims=True), sink)
    p, ps = jnp.exp(s - m), jnp.exp(sink - m)
    inv = 1.0 / (jnp.sum(p, axis=1, keepdims=True) + ps)
    return p, inv, ps


def _attn_heads(q_ref, s_ref, h, q_per_kv):
    low = _lane((ATTN_BLOCK, LANES)) < HEAD_DIM
    heads = []
    for pr in range(h * q_per_kv // 2, (h + 1) * q_per_kv // 2):
        q2 = q_ref[:, LANES * pr:LANES * (pr + 1)]
        for odd in (False, True):
            mine = low != odd
            sink = jnp.max(s_ref[2 * pr + int(odd):2 * pr + int(odd) + 1, :], axis=1, keepdims=True)
            heads.append((pr, mine, jnp.where(mine, q2, jnp.zeros_like(q2)), sink))
    return low, heads


def _kv_block(prev_ref, cur_ref, h):
    return jnp.concatenate([prev_ref[:, LANES * h:LANES * (h + 1)], cur_ref[:, LANES * h:LANES * (h + 1)]], axis=0)


def attn_fwd(qr, kk, vv, sink_b):
    L, d_attn = qr.shape
    nb, q_spec, prev = _attn_specs(L)
    d_kk = kk.shape[1]
    n_kv = d_kk // LANES
    q_per_kv = d_attn // HEAD_DIM // n_kv

    def body(q_ref, kc_ref, kp_ref, vc_ref, vp_ref, s_ref, o_ref):
        mask = _attn_mask(pl.program_id(0))
        for h in range(n_kv):
            kcat, vcat = _kv_block(kp_ref, kc_ref, h), _kv_block(vp_ref, vc_ref, h)
            low, heads = _attn_heads(q_ref, s_ref, h, q_per_kv)
            probs = [_attn_probs(qm, kcat, sink, mask) for (_, _, qm, sink) in heads]
            outs = [_dot(p.astype(bf16), vcat) * inv for (p, inv, _) in probs]
            for i in range(0, len(heads), 2):
                pr = heads[i][0]
                o_ref[:, LANES * pr:LANES * (pr + 1)] = jnp.where(low, outs[i], outs[i + 1])

    return pl.pallas_call(
        body, out_shape=jax.ShapeDtypeStruct((L, d_attn), f32), grid=(nb,),
        in_specs=[q_spec(d_attn), q_spec(d_kk), prev(d_kk), q_spec(d_kk), prev(d_kk), pl.BlockSpec(sink_b.shape, lambda n: (0, 0))],
        out_specs=q_spec(d_attn), name="attn_fwd", compiler_params=_params("arbitrary"))(qr, kk, kk, vv, vv, sink_b)


def attn_bwd(qr, kk, vv, sink_b, attn, d_attn_out):
    L, d_attn = qr.shape
    nb, q_spec, prev = _attn_specs(L)
    d_kk = kk.shape[1]
    n_kv = d_kk // LANES
    q_per_kv = d_attn // HEAD_DIM // n_kv

    def body(q_ref, kc_ref, kp_ref, vc_ref, vp_ref, s_ref, o_ref, do_ref, dq_ref, dkc_ref, dkp_ref, dvc_ref, dvp_ref, ds_ref):
        n = pl.program_id(0)
        B = ATTN_BLOCK
        mask = _attn_mask(n)
        srow = lax.broadcasted_iota(jnp.int32, (SUBLANES, LANES), 0)

        @pl.when(n == 0)
        def _():
            ds_ref[...] = jnp.zeros_like(ds_ref)

        for h in range(n_kv):
            kcat, vcat = _kv_block(kp_ref, kc_ref, h), _kv_block(vp_ref, vc_ref, h)
            low, heads = _attn_heads(q_ref, s_ref, h, q_per_kv)
            probs = [_attn_probs(qm, kcat, sink, mask) for (_, _, qm, sink) in heads]
            dk = jnp.zeros((2 * B, LANES), f32)
            dv = dk
            dsink = jnp.zeros((SUBLANES, LANES), f32)
            dqs = []
            for i, ((pr, mine, qm, _), (p, inv, ps)) in enumerate(zip(heads, probs)):
                do2 = do_ref[:, LANES * pr:LANES * (pr + 1)]
                delta = jnp.sum(jnp.where(mine, do2 * o_ref[:, LANES * pr:LANES * (pr + 1)], 0.0), axis=1, keepdims=True)
                dob = jnp.where(mine, do2, 0.0).astype(bf16)
                p = p * inv
                ds = (p * (_dot(dob, vcat, NT) - delta)).astype(bf16)
                dqs.append(_dot(ds, kcat))
                dk = dk + _dot(ds, qm, TN)
                dv = dv + _dot(p.astype(bf16), dob, TN)
                dsink = dsink + jnp.where(srow == i, -jnp.sum(ps * inv * delta), 0.0)
            for i in range(0, len(heads), 2):
                pr = heads[i][0]
                dq_ref[:, LANES * pr:LANES * (pr + 1)] = jnp.where(low, dqs[i], dqs[i + 1])
            cols = slice(LANES * h, LANES * (h + 1))
            dkp_ref[:, cols] = dk[:B]
            dkc_ref[:, cols] = dk[B:]
            dvp_ref[:, cols] = dv[:B]
            dvc_ref[:, cols] = dv[B:]
            ds_ref[h] += dsink

    kv_shape = jax.ShapeDtypeStruct(kk.shape, f32)
    ds_shape = (n_kv, SUBLANES, LANES)
    return pl.pallas_call(
        body,
        out_shape=[jax.ShapeDtypeStruct((L, d_attn), f32), kv_shape, kv_shape, kv_shape, kv_shape, jax.ShapeDtypeStruct(ds_shape, f32)],
        grid=(nb,),
        in_specs=[q_spec(d_attn), q_spec(d_kk), prev(d_kk), q_spec(d_kk), prev(d_kk), pl.BlockSpec(sink_b.shape, lambda n: (0, 0)),
                  q_spec(d_attn), q_spec(d_attn)],
        out_specs=[q_spec(d_attn)] + [q_spec(d_kk)] * 4 + [pl.BlockSpec(ds_shape, lambda n: (0, 0, 0))],
        name="attn_bwd", compiler_params=_params("arbitrary"))(qr, kk, kk, vv, vv, sink_b, attn, d_attn_out)


SSM_T = 256
NQ = SUBLANES * SSM_STATE // LANES
NJ = SUBLANES


def _strided_put(ref, j, val):
    for q in range(NQ):
        ref.at[q][pl.ds(j, SSM_T, stride=NJ), :] = val[:, LANES * q:LANES * (q + 1)]


def _strided_get(ref, j):
    return jnp.concatenate([ref.at[q][pl.ds(j, SSM_T, stride=NJ), :] for q in range(NQ)], axis=1)


def _ssm_specs(L, rev):
    nt = L // SSM_T
    idx = (lambda i: nt - 1 - i) if rev else (lambda i: i)
    row = lambda w, cb=0: pl.BlockSpec((SSM_T, w), lambda i: (idx(i), cb))
    state = pl.BlockSpec((NQ, SSM_T * NJ, LANES), lambda i: (0, idx(i), 0))
    whole = lambda a: pl.BlockSpec(a.shape, lambda i: (0,) * a.ndim)
    return nt, row, state, whole


def ssm_fwd(u_bf, proj, u_cb, bd_re, bd_im, cd_re, cd_im, lam_re, lam_im, d_skip, dep=None):
    L, d_ssm = u_bf.shape
    nt, row, state, whole = _ssm_specs(L, False)
    half = d_ssm // 2
    gw = d_ssm // NJ
    n_dep = 0 if dep is None else 1

    def body(u_ref, u0_ref, u1_ref, bdr, bdi, cdr, cdi, lr_ref, li_ref, d_ref, *rest):
        y_ref, z_ref, sr_ref, si_ref, carry = rest[n_dep:]
        i = pl.program_id(0)

        @pl.when(i == 0)
        def _():
            carry[...] = jnp.zeros_like(carry)

        for j in range(NJ):
            uj = u_ref[:, gw * j:gw * (j + 1)]
            _strided_put(sr_ref, j, _dot(uj, bdr[j]))
            _strided_put(si_ref, j, _dot(uj, bdi[j]))
        lr = [lr_ref[q] for q in range(NQ)]
        li = [li_ref[q] for q in range(NQ)]

        def step(t, s):
            sr, si = s
            rows = pl.ds(pl.multiple_of(t * NJ, NJ), NJ)
            nr = tuple(lr[q] * sr[q] - li[q] * si[q] + sr_ref[q, rows, :] for q in range(NQ))
            ni = tuple(lr[q] * si[q] + li[q] * sr[q] + si_ref[q, rows, :] for q in range(NQ))
            for q in range(NQ):
                sr_ref[q, rows, :] = nr[q]
                si_ref[q, rows, :] = ni[q]
            return nr, ni

        init = (tuple(carry[0, q] for q in range(NQ)), tuple(carry[1, q] for q in range(NQ)))
        sr, si = lax.fori_loop(0, SSM_T, step, init, unroll=8)
        for q in range(NQ):
            carry[0, q] = sr[q]
            carry[1, q] = si[q]
        uf = jnp.concatenate([u0_ref[...], u1_ref[...]], axis=1)
        for j in range(NJ):
            cols = slice(gw * j, gw * (j + 1))
            yj = _dot(_strided_get(sr_ref, j).astype(bf16), cdr[j]) - _dot(_strided_get(si_ref, j).astype(bf16), cdi[j])
            yj = yj + d_ref[:, cols] * uf[:, cols]
            y_ref[:, cols] = yj
            z_ref[:, cols] = gelu(yj).astype(bf16)

    s_shape = jax.ShapeDtypeStruct((NQ, L * NJ, LANES), f32)
    consts = (bd_re, bd_im, cd_re, cd_im, lam_re, lam_im, d_skip)
    in_specs, operands, _ = _with_dep([row(d_ssm), row(half, u_cb), row(half, u_cb + 1)] + [whole(a) for a in consts],
                                      [u_bf, proj, proj, *consts], dep)
    return pl.pallas_call(
        body, out_shape=[jax.ShapeDtypeStruct((L, d_ssm), f32), jax.ShapeDtypeStruct((L, d_ssm), bf16), s_shape, s_shape], grid=(nt,),
        in_specs=in_specs, out_specs=[row(d_ssm), row(d_ssm), state, state],
        scratch_shapes=[pltpu.VMEM((2, NQ, NJ, LANES), f32)], name="ssm_fwd",
        compiler_params=_params("arbitrary"))(*operands)


def ssm_bwd(y, dz1, dz2, u_bf, proj, u_cb, s_re, s_im, bd_re, bd_im, cd_re, cd_im, lam_re, lam_im, d_skip, dep=None):
    L, d_ssm = y.shape
    nt, row, state, whole = _ssm_specs(L, True)
    half = d_ssm // 2
    gw = d_ssm // NJ
    n_dep = 0 if dep is None else 1

    def body(y_ref, dz1_ref, dz2_ref, u_ref, u0_ref, u1_ref, sr_ref, si_ref, bdr, bdi, cdr, cdi, lr_ref, li_ref, d_ref, *rest):
        du_ref, dbdr, dbdi, dcdr, dcdi, dlr, dli, dd_ref, gr_ref, gi_ref, carry = rest[n_dep:]
        i = pl.program_id(0)

        @pl.when(i == 0)
        def _():
            carry[...] = jnp.zeros_like(carry)
            for r in (dbdr, dbdi, dcdr, dcdi, dlr, dli, dd_ref):
                r[...] = jnp.zeros_like(r)

        dyf = (dz1_ref[...] + dz2_ref[...]) * gelu_grad(y_ref[...])
        dyb = dyf.astype(bf16)
        for j in range(NJ):
            dyj = dyb[:, gw * j:gw * (j + 1)]
            _strided_put(gr_ref, j, _dot(dyj, cdr[j], NT))
            _strided_put(gi_ref, j, -_dot(dyj, cdi[j], NT))
            dcdr[j] += _dot(_strided_get(sr_ref, j).astype(bf16), dyj, TN)
            dcdi[j] -= _dot(_strided_get(si_ref, j).astype(bf16), dyj, TN)
        lr = [lr_ref[q] for q in range(NQ)]
        li = [li_ref[q] for q in range(NQ)]

        def step(k, c):
            gr, gi, ar, ai = c
            rows = pl.ds(pl.multiple_of((SSM_T - 1 - k) * NJ, NJ), NJ)
            s_r = [sr_ref[q, rows, :] for q in range(NQ)]
            s_i = [si_ref[q, rows, :] for q in range(NQ)]
            ar = tuple(ar[q] + gr[q] * s_r[q] + gi[q] * s_i[q] for q in range(NQ))
            ai = tuple(ai[q] + gi[q] * s_r[q] - gr[q] * s_i[q] for q in range(NQ))
            nr = tuple(gr_ref[q, rows, :] + lr[q] * gr[q] + li[q] * gi[q] for q in range(NQ))
            ni = tuple(gi_ref[q, rows, :] + lr[q] * gi[q] - li[q] * gr[q] for q in range(NQ))
            for q in range(NQ):
                gr_ref[q, rows, :] = nr[q]
                gi_ref[q, rows, :] = ni[q]
            return nr, ni, ar, ai

        zero = tuple(jnp.zeros((NJ, LANES), f32) for _ in range(NQ))
        init = (tuple(carry[0, q] for q in range(NQ)), tuple(carry[1, q] for q in range(NQ)), zero, zero)
        gr, gi, ar, ai = lax.fori_loop(0, SSM_T, step, init, unroll=8)
        for q in range(NQ):
            carry[0, q] = gr[q]
            carry[1, q] = gi[q]
            dlr[q] += ar[q]
            dli[q] += ai[q]
        uf = jnp.concatenate([u0_ref[...], u1_ref[...]], axis=1)
        dd_ref[...] += colsum(dyf * uf)
        for j in range(NJ):
            cols = slice(gw * j, gw * (j + 1))
            gjr, gji = _strided_get(gr_ref, j).astype(bf16), _strided_get(gi_ref, j).astype(bf16)
            du_ref[:, cols] = _dot(gjr, bdr[j], NT) + _dot(gji, bdi[j], NT) + d_ref[:, cols] * dyf[:, cols]
            uj = u_ref[:, cols]
            dbdr[j] += _dot(uj, gjr, TN)
            dbdi[j] += _dot(uj, gji, TN)

    consts = (bd_re, bd_im, cd_re, cd_im, lam_re, lam_im, d_skip)
    acc = lambda a: jax.ShapeDtypeStruct(a.shape, f32)
    outs = [jax.ShapeDtypeStruct((L, d_ssm), f32), acc(bd_re), acc(bd_im), acc(cd_re), acc(cd_im), acc(lam_re), acc(lam_im), acc(d_skip)]
    in_specs, operands, _ = _with_dep(
        [row(d_ssm)] * 4 + [row(half, u_cb), row(half, u_cb + 1), state, state] + [whole(a) for a in consts],
        [y, dz1, dz2, u_bf, proj, proj, s_re, s_im, *consts], dep)
    return pl.pallas_call(
        body, out_shape=outs, grid=(nt,),
        in_specs=in_specs, out_specs=[row(d_ssm)] + [whole(a) for a in consts],
        scratch_shapes=[pltpu.VMEM((NQ, SSM_T * NJ, LANES), f32), pltpu.VMEM((NQ, SSM_T * NJ, LANES), f32),
                        pltpu.VMEM((2, NQ, NJ, LANES), f32)],
        name="ssm_bwd", compiler_params=_params("arbitrary"))(*operands)


def _cmul(ar, ai, br, bi):
    return ar * br - ai * bi, ar * bi + ai * br


def _disc(ar, ai, logdt):
    dt = jnp.exp(logdt)
    mag = jnp.exp(ar * dt)
    lr, li = mag * jnp.cos(ai * dt), mag * jnp.sin(ai * dt)
    den = ar * ar + ai * ai
    nr, ni = lr - 1.0, li
    fr, fi = (nr * ar + ni * ai) / den, (ni * ar - nr * ai) / den
    return dt, lr, li, den, fr, fi


def ssm_params(a_re, a_im, logdt_b, bt_re, bt_im, spread):
    def body(ar_ref, ai_ref, ld_ref, br_ref, bi_ref, sp_ref, lr_ref, li_ref, or_ref, oi_ref):
        _, lr, li, _, fr, fi = _disc(ar_ref[...], ai_ref[...], ld_ref[...])
        lr_ref[...] = lr
        li_ref[...] = li
        fre = jnp.dot(sp_ref[...], fr, precision=HIGHEST, preferred_element_type=f32)
        fie = jnp.dot(sp_ref[...], fi, precision=HIGHEST, preferred_element_type=f32)
        o_r, o_i = _cmul(fre, fie, br_ref[...], bi_ref[...])
        or_ref[...] = o_r
        oi_ref[...] = o_i

    g = jax.ShapeDtypeStruct(a_re.shape, f32)
    b = jax.ShapeDtypeStruct(bt_re.shape, f32)
    return pl.pallas_call(body, out_shape=[g, g, b, b], name="ssm_params",
                          compiler_params=_params())(a_re, a_im, logdt_b, bt_re, bt_im, spread)


def ssm_params_grad(a_re, a_im, logdt_b, bt_re, bt_im, spread, gather, dlam_re, dlam_im, dbt_re, dbt_im):
    def body(ar_ref, ai_ref, ld_ref, br_ref, bi_ref, sp_ref, ga_ref, glr_ref, gli_ref, gbr_ref, gbi_ref,
             dar_ref, dai_ref, dld_ref, dbr_ref, dbi_ref):
        ar, ai = ar_ref[...], ai_ref[...]
        dt, lr, li, den, fr, fi = _disc(ar, ai, ld_ref[...])
        hdot = functools.partial(jnp.dot, precision=HIGHEST, preferred_element_type=f32)
        fre, fie = hdot(sp_ref[...], fr), hdot(sp_ref[...], fi)
        gbr, gbi, br, bi = gbr_ref[...], gbi_ref[...], br_ref[...], bi_ref[...]
        dbr_ref[...], dbi_ref[...] = _cmul(fre, -fie, gbr, gbi)
        t_r, t_i = _cmul(br, -bi, gbr, gbi)
        gfr, gfi = hdot(ga_ref[...], t_r), hdot(ga_ref[...], t_i)
        iwr, iwi = ar / den, -ai / den
        x_r, x_i = _cmul(iwr, -iwi, gfr, gfi)
        glr, gli = glr_ref[...] + x_r, gli_ref[...] + x_i
        q_r, q_i = _cmul(fr, fi, iwr, iwi)
        gwr, gwi = _cmul(-q_r, q_i, gfr, gfi)
        y_r, y_i = _cmul(dt * lr, -dt * li, glr, gli)
        dar_ref[...] = gwr + y_r
        dai_ref[...] = gwi + y_i
        wl_r, wl_i = _cmul(ar, ai, lr, li)
        z_r, _ = _cmul(wl_r, -wl_i, glr, gli)
        dld_ref[...] = jnp.sum(z_r * dt, axis=1, keepdims=True)

    g = jax.ShapeDtypeStruct(a_re.shape, f32)
    b = jax.ShapeDtypeStruct(bt_re.shape, f32)
    return pl.pallas_call(body, out_shape=[g, g, jax.ShapeDtypeStruct((a_re.shape[0], 1), f32), b, b], name="ssm_params_grad",
                          compiler_params=_params())(a_re, a_im, logdt_b, bt_re, bt_im, spread, gather, dlam_re, dlam_im, dbt_re, dbt_im)


def _block_diag(t, rows, cols):
    G = t.shape[0]
    t = t.reshape(G // NJ, NJ, rows, cols)
    eye = jnp.eye(NJ, dtype=t.dtype)
    return jnp.einsum('jgrc,gh->jgrhc', t, eye).reshape(G // NJ, NJ * rows, NJ * cols)


def _block_diag_take(m, rows, cols):
    J = m.shape[0]
    m = m.reshape(J, NJ, rows, NJ, cols)
    idx = jnp.arange(NJ)
    return m[:, idx, :, idx, :].transpose(1, 0, 2, 3).reshape(J * NJ, rows, cols)


def _state_layout(t):
    return t.reshape(NJ, NQ, LANES).transpose(1, 0, 2)


def _state_layout_inv(t, G, N):
    return t.transpose(1, 0, 2).reshape(G, N)


def _tiles2d(shape, budget_rows=128):
    rows, cols = shape
    tr = rows
    if rows > budget_rows:
        tr = budget_rows
        while rows % tr:
            tr -= SUBLANES
    return tr, cols


def _adam_update(w, g, m, v):
    c1 = 1.0 - ADAM_B1 ** ADAM_STEP
    c2 = 1.0 - ADAM_B2 ** ADAM_STEP
    nm = ADAM_B1 * m + (1.0 - ADAM_B1) * g
    nv = ADAM_B2 * v + (1.0 - ADAM_B2) * (g * g)
    delta = -ADAM_LR * ((nm / c1) / (jnp.sqrt(nv / c2) + ADAM_EPS) + ADAM_WD * w)
    return delta, nm, nv


def adamw_many(name, ws, gs, ms, vs):
    n = len(ws)

    def body(*refs):
        w, g, m, v = (refs[k * n:(k + 1) * n] for k in range(4))
        d, nm, nv = (refs[(4 + k) * n:(5 + k) * n] for k in range(3))
        for i in range(n):
            d[i][...], nm[i][...], nv[i][...] = _adam_update(w[i][...], g[i][...], m[i][...], v[i][...])

    o = [jax.ShapeDtypeStruct(a.shape, f32) for a in ws]
    outs = pl.pallas_call(body, out_shape=o * 3, name=name, compiler_params=_params())(*ws, *gs, *ms, *vs)
    return outs[:n], outs[n:2 * n], outs[2 * n:]


def adamw_halves(name, w, own, got, m, v, c_arr):
    h, cols = own.shape
    tr, _ = _tiles2d((h, cols), 128 if cols > 1024 else 256)
    nh = h // tr

    def body(c_ref, w_ref, own_ref, got_ref, m_ref, v_ref, g_ref, d_ref, nm_ref, nv_ref):
        mine = (pl.program_id(0) // nh) == c_ref[0]
        g = jnp.where(mine, own_ref[...], got_ref[...])
        g_ref[...] = g
        d_ref[...], nm_ref[...], nv_ref[...] = _adam_update(w_ref[...], g, m_ref[...], v_ref[...])

    spec = pl.BlockSpec((tr, cols), lambda i, c: (i, 0))
    own_spec = pl.BlockSpec((tr, cols), lambda i, c: (jnp.where(i // nh == c[0], i % nh, 0), 0))
    got_spec = pl.BlockSpec((tr, cols), lambda i, c: (jnp.where(i // nh == c[0], 0, i % nh), 0))
    o = jax.ShapeDtypeStruct(w.shape, f32)
    grid_spec = pltpu.PrefetchScalarGridSpec(num_scalar_prefetch=1, grid=(2 * nh,),
                                             in_specs=[spec, own_spec, got_spec, spec, spec], out_specs=[spec] * 4)
    return pl.pallas_call(body, out_shape=[o, o, o, o], grid_spec=grid_spec, name=name,
                          compiler_params=_params("arbitrary"))(c_arr, w, own, got, m, v)


def pair_sum(name, g, got, c_arr):
    S, h, cols = got.shape
    tr, _ = _tiles2d((h, cols), 1024)
    nh = h // tr

    def body(c_ref, g_ref, o_ref, out_ref):
        out_ref[...] = (g_ref[...].astype(f32) + o_ref[...].astype(f32)).astype(out_ref.dtype)

    spec = pl.BlockSpec((None, tr, cols), lambda s, i, c: (s, i, 0))
    grid_spec = pltpu.PrefetchScalarGridSpec(
        num_scalar_prefetch=1, grid=(S, nh),
        in_specs=[pl.BlockSpec((None, tr, cols), lambda s, i, c: (s, c[0] * nh + i, 0)), spec], out_specs=spec)
    return pl.pallas_call(body, out_shape=jax.ShapeDtypeStruct(got.shape, g.dtype), grid_spec=grid_spec, name=name,
                          compiler_params=_params("parallel", "parallel"))(c_arr, g, got)


def chip_sum(name, pair, landed, mine_arr, dep=None):
    n_in, h, cols = landed.shape
    tr, _ = _tiles2d((h, cols), 512)

    def body(s_ref, p_ref, l_ref, *rest):
        acc = p_ref[...].astype(f32)
        for k in range(n_in):
            acc = acc + l_ref[k].astype(f32)
        rest[-1][...] = acc

    in_specs, operands, _ = _with_dep(
        [pl.BlockSpec((None, tr, cols), lambda i, s: (s[0], i, 0)), pl.BlockSpec((n_in, tr, cols), lambda i, s: (0, i, 0))],
        [pair, landed], dep)
    grid_spec = pltpu.PrefetchScalarGridSpec(num_scalar_prefetch=1, grid=(h // tr,), in_specs=in_specs,
                                             out_specs=pl.BlockSpec((tr, cols), lambda i, s: (i, 0)))
    return pl.pallas_call(body, out_shape=jax.ShapeDtypeStruct((h, cols), f32), grid_spec=grid_spec, name=name,
                          compiler_params=_params("parallel"))(mine_arr, *operands)


def into_slot(name, w, slot_arr, n_slots, dtype, dep=None):
    tr, cols = _tiles2d(w.shape, 256)

    def body(s_ref, w_ref, *rest):
        rest[-1][...] = w_ref[...].astype(dtype)

    in_specs, operands, _ = _with_dep([pl.BlockSpec((tr, cols), lambda i, s: (i, 0))], [w], dep)
    grid_spec = pltpu.PrefetchScalarGridSpec(num_scalar_prefetch=1, grid=(w.shape[0] // tr,), in_specs=in_specs,
                                             out_specs=pl.BlockSpec((None, tr, cols), lambda i, s: (s[0], i, 0)))
    return pl.pallas_call(body, out_shape=jax.ShapeDtypeStruct((n_slots,) + w.shape, dtype), grid_spec=grid_spec, name=name,
                          compiler_params=_params("parallel"))(slot_arr, *operands)


def sum_slots(name, t):
    S, rows, cols = t.shape
    tr, _ = _tiles2d((rows, cols), 256)

    def body(t_ref, o_ref):
        acc = t_ref[0]
        for s in range(1, S):
            acc = acc + t_ref[s]
        o_ref[...] = acc

    return pl.pallas_call(body, out_shape=jax.ShapeDtypeStruct((rows, cols), f32), grid=(rows // tr,),
                          in_specs=[pl.BlockSpec((S, tr, cols), lambda i: (0, i, 0))], out_specs=pl.BlockSpec((tr, cols), lambda i: (i, 0)),
                          name=name, compiler_params=_params("parallel"))(t)


ANY = pl.BlockSpec(memory_space=pl.ANY)


def _place():
    x, y, c = lax.axis_index("x"), lax.axis_index("y"), lax.axis_index("c")
    return x, y, c


def _other_chips(x, y):
    return [(1 - x, y, 2 * (1 - x) + y), (x, 1 - y, 2 * x + 1 - y), (1 - x, 1 - y, 2 * (1 - x) + 1 - y)]


def gather_weights(bufs):
    nw = len(bufs)

    def body(*refs):
        dst = refs[nw:2 * nw]
        send1, recv1, send2, recv2 = refs[2 * nw:]
        x, y, c = _place()
        mine = 2 * x + y
        chips = _other_chips(x, y)
        sib = (x, y, 1 - c)
        first, passed = [], []
        for w in range(nw):
            h = dst[w].shape[1] // 2
            for k, (px, py, _) in enumerate(chips):
                half = dst[w].at[mine, pl.ds(c * h, h), :]
                cp = pltpu.make_async_remote_copy(src_ref=half, dst_ref=half,
                                                  send_sem=send1.at[w, k], recv_sem=recv1.at[w, k],
                                                  device_id=(px, py, c), device_id_type=MESH)
                cp.start()
                first.append(cp)
        for w in range(nw):
            h = dst[w].shape[1] // 2
            for k, (px, py, s) in enumerate(chips):
                landed = dst[w].at[s, pl.ds(c * h, h), :]
                pltpu.make_async_remote_copy(src_ref=landed, dst_ref=landed, send_sem=send1.at[w, k], recv_sem=recv1.at[w, k],
                                             device_id=(px, py, c), device_id_type=MESH).wait_recv()
                cp = pltpu.make_async_remote_copy(src_ref=landed, dst_ref=landed, send_sem=send2.at[w, k], recv_sem=recv2.at[w, k],
                                                  device_id=sib, device_id_type=MESH)
                cp.start()
                passed.append(cp)
        for w in range(nw):
            h = dst[w].shape[1] // 2
            for k, (px, py, s) in enumerate(chips):
                other = dst[w].at[s, pl.ds((1 - c) * h, h), :]
                pltpu.make_async_remote_copy(src_ref=other, dst_ref=other, send_sem=send2.at[w, k], recv_sem=recv2.at[w, k],
                                             device_id=sib, device_id_type=MESH).wait_recv()
        for cp in first + passed:
            cp.wait_send()

    sem = pltpu.SemaphoreType.DMA((nw, 3))
    return pl.pallas_call(
        body, out_shape=[jax.ShapeDtypeStruct(b.shape, b.dtype) for b in bufs],
        in_specs=[ANY] * nw, out_specs=[ANY] * nw, input_output_aliases={w: w for w in range(nw)},
        scratch_shapes=[sem, sem, sem, sem], name="gather_weights",
        compiler_params=pltpu.CompilerParams(has_side_effects=True))(*bufs)


def swap_halves(name, grads):
    nw = len(grads)

    def body(*refs):
        src, got = refs[:nw], refs[nw:2 * nw]
        send, recv = refs[2 * nw:]
        x, y, c = _place()
        cps = []
        for w in range(nw):
            h = src[w].shape[1] // 2
            cp = pltpu.make_async_remote_copy(src_ref=src[w].at[:, pl.ds((1 - c) * h, h), :], dst_ref=got[w],
                                              send_sem=send.at[w], recv_sem=recv.at[w],
                                              device_id=(x, y, 1 - c), device_id_type=MESH)
            cp.start()
            cps.append(cp)
        for cp in cps:
            cp.wait()

    half = [jax.ShapeDtypeStruct((g.shape[0], g.shape[1] // 2, g.shape[2]), g.dtype) for g in grads]
    sem = pltpu.SemaphoreType.DMA((nw,))
    return pl.pallas_call(
        body, out_shape=half, in_specs=[ANY] * nw, out_specs=[ANY] * nw,
        scratch_shapes=[sem, sem], name=name,
        compiler_params=pltpu.CompilerParams(has_side_effects=True))(*grads)


def scatter_to_owners(parts):
    nw = len(parts)

    def body(*refs):
        src, dst = refs[:nw], refs[nw:2 * nw]
        send, recv = refs[2 * nw:]
        x, y, c = _place()
        cps = []
        for w in range(nw):
            for k, (px, py, s) in enumerate(_other_chips(x, y)):
                cp = pltpu.make_async_remote_copy(src_ref=src[w].at[s], dst_ref=dst[w].at[k],
                                                  send_sem=send.at[w, k], recv_sem=recv.at[w, k],
                                                  device_id=(px, py, c), device_id_type=MESH)
                cp.start()
                cps.append(cp)
        for cp in cps:
            cp.wait()

    sem = pltpu.SemaphoreType.DMA((nw, 3))
    return pl.pallas_call(
        body, out_shape=[jax.ShapeDtypeStruct((N_CHIPS - 1,) + p.shape[1:], p.dtype) for p in parts],
        in_specs=[ANY] * nw, out_specs=[ANY] * nw,
        scratch_shapes=[sem, sem], name="scatter_to_owners",
        compiler_params=pltpu.CompilerParams(has_side_effects=True))(*parts)


def join_halves(name, halves):
    nw = len(halves)

    def body(*refs):
        src, dst = refs[:nw], refs[nw:2 * nw]
        send, recv = refs[2 * nw:]
        x, y, c = _place()
        cps = []
        for w in range(nw):
            cp = pltpu.make_async_remote_copy(src_ref=src[w], dst_ref=dst[w], send_sem=send.at[w], recv_sem=recv.at[w],
                                              device_id=(x, y, 1 - c), device_id_type=MESH)
            cp.start()
            cps.append(cp)
        for cp in cps:
            cp.wait()

    sem = pltpu.SemaphoreType.DMA((nw,))
    return pl.pallas_call(
        body, out_shape=[jax.ShapeDtypeStruct(p.shape, p.dtype) for p in halves], in_specs=[ANY] * nw, out_specs=[ANY] * nw,
        scratch_shapes=[sem, sem], name=name,
        compiler_params=pltpu.CompilerParams(has_side_effects=True))(*halves)


SEM = pl.BlockSpec(memory_space=pltpu.SEMAPHORE)
VM = pl.BlockSpec(memory_space=pltpu.VMEM)
DATAFLOW = pltpu.SideEffectType.DATAFLOW_SIDE_EFFECTING
TOKEN = jax.ShapeDtypeStruct((SUBLANES, LANES), f32)


def _gather_copy(buf, w, k, chip, c, mine, send, recv):
    px, py, _ = chip
    h = buf.shape[1] // 2
    half = buf.at[mine, pl.ds(c * h, h), :]
    return pltpu.make_async_remote_copy(src_ref=half, dst_ref=half, send_sem=send.at[3 * w + k], recv_sem=recv.at[3 * w + k],
                                        device_id=(px, py, c), device_id_type=MESH)


def _gather_landing(buf, w, k, chip, c, send, recv):
    px, py, s = chip
    h = buf.shape[1] // 2
    landed = buf.at[s, pl.ds(c * h, h), :]
    return pltpu.make_async_remote_copy(src_ref=landed, dst_ref=landed, send_sem=send.at[3 * w + k], recv_sem=recv.at[3 * w + k],
                                        device_id=(px, py, c), device_id_type=MESH)


def gather_start(name, bufs, groups, after, neighbours_only=()):
    nw, ng = len(bufs), len(groups)

    def body(*refs):
        outs = refs[nw + 1:]
        sems, dst = outs[:2 * ng], outs[2 * ng:2 * ng + nw]
        token = outs[2 * ng + nw]
        x, y, c = _place()
        mine = 2 * x + y
        for g, members in enumerate(groups):
            for i, w in enumerate(members):
                for k, chip in enumerate(_other_chips(x, y)[:2 if w in neighbours_only else 3]):
                    _gather_copy(dst[w], i, k, chip, c, mine, sems[2 * g], sems[2 * g + 1]).start()
        token[...] = jnp.zeros_like(token)

    sem_shapes = []
    for members in groups:
        sem_shapes += [pltpu.SemaphoreType.DMA((3 * len(members),))] * 2
    outs = pl.pallas_call(
        body, out_shape=sem_shapes + [jax.ShapeDtypeStruct(b.shape, b.dtype) for b in bufs] + [TOKEN],
        in_specs=[ANY] * (nw + 1), out_specs=[SEM] * (2 * ng) + [ANY] * nw + [VM],
        input_output_aliases={w: 2 * ng + w for w in range(nw)}, name=name,
        compiler_params=pltpu.CompilerParams(has_side_effects=DATAFLOW))(*bufs, after)
    return [(outs[2 * g], outs[2 * g + 1]) for g in range(ng)], list(outs[2 * ng:2 * ng + nw]), outs[2 * ng + nw]


def gather_wait(name, bufs, send, recv, after):
    nw = len(bufs)

    def body(*refs):
        src = refs[:nw]
        send_ref, recv_ref = refs[nw], refs[nw + 1]
        x, y, c = _place()
        mine = 2 * x + y
        for w in range(nw):
            for k, chip in enumerate(_other_chips(x, y)):
                _gather_copy(src[w], w, k, chip, c, mine, send_ref, recv_ref).wait_send()
                _gather_landing(src[w], w, k, chip, c, send_ref, recv_ref).wait_recv()

    return pl.pallas_call(
        body, out_shape=[jax.ShapeDtypeStruct(b.shape, b.dtype) for b in bufs],
        in_specs=[ANY] * nw + [SEM, SEM, ANY], out_specs=[ANY] * nw,
        input_output_aliases={w: w for w in range(nw)}, name=name,
        compiler_params=pltpu.CompilerParams(has_side_effects=DATAFLOW))(*bufs, send, recv, after)


def _relay_copy(buf, w, j, x, y, c, send, recv, landing):
    chips = _other_chips(x, y)
    px, py, _ = chips[j]
    h = buf.shape[1] // 2
    q = h // 2
    s = chips[2][2] if landing else chips[1 - j][2]
    part = buf.at[s, pl.ds(c * h + j * q, q), :]
    return pltpu.make_async_remote_copy(src_ref=part, dst_ref=part, send_sem=send.at[2 * w + j], recv_sem=recv.at[2 * w + j],
                                        device_id=(px, py, c), device_id_type=MESH)


def gather_relay(name, bufs, sems, more, after):
    nw, nm = len(bufs), len(more)
    ns = 4 if nm else 2

    def body(*refs):
        ins, outs = refs[:nw + nm + 2 * nw + 1], refs[nw + nm + 2 * nw + 1:]
        src, d_sems = ins[:nw], ins[nw + nm:nw + nm + 2 * nw]
        r_send, r_recv = outs[:2]
        m_send, m_recv = outs[2:ns] if nm else (None, None)
        dst, mdst, token = outs[ns:ns + nw], outs[ns + nw:ns + nw + nm], outs[ns + nw + nm]
        x, y, c = _place()
        mine = 2 * x + y
        chips = _other_chips(x, y)
        for w in range(nw):
            for k in range(2):
                _gather_copy(src[w], 0, k, chips[k], c, mine, d_sems[2 * w], d_sems[2 * w + 1]).wait_send()
                _gather_landing(src[w], 0, k, chips[k], c, d_sems[2 * w], d_sems[2 * w + 1]).wait_recv()
            for j in range(2):
                _relay_copy(dst[w], w, j, x, y, c, r_send, r_recv, False).start()
        for w in range(nm):
            for k, chip in enumerate(chips):
                _gather_copy(mdst[w], w, k, chip, c, mine, m_send, m_recv).start()
        token[...] = jnp.zeros_like(token)

    sem_shapes = [pltpu.SemaphoreType.DMA((2 * nw,))] * 2 + [pltpu.SemaphoreType.DMA((3 * nm,))] * (ns - 2)
    flat_sems = [s for pair in sems for s in pair]
    outs = pl.pallas_call(
        body, out_shape=sem_shapes + [jax.ShapeDtypeStruct(b.shape, b.dtype) for b in list(bufs) + list(more)] + [TOKEN],
        in_specs=[ANY] * (nw + nm) + [SEM] * (2 * nw) + [ANY], out_specs=[SEM] * ns + [ANY] * (nw + nm) + [VM],
        input_output_aliases={i: ns + i for i in range(nw + nm)}, name=name,
        compiler_params=pltpu.CompilerParams(has_side_effects=DATAFLOW))(*bufs, *more, *flat_sems, after)
    m_sems = (outs[2], outs[3]) if nm else None
    return outs[0], outs[1], m_sems, list(outs[ns:ns + nw]), list(outs[ns + nw:ns + nw + nm]), outs[ns + nw + nm]


def gather_wait_relay(name, bufs, r_send, r_recv, after):
    nw = len(bufs)

    def body(*refs):
        src = refs[:nw]
        send_ref, recv_ref = refs[nw], refs[nw + 1]
        x, y, c = _place()
        for w in range(nw):
            for j in range(2):
                _relay_copy(src[w], w, j, x, y, c, send_ref, recv_ref, False).wait_send()
                _relay_copy(src[w], w, j, x, y, c, send_ref, recv_ref, True).wait_recv()

    return pl.pallas_call(
        body, out_shape=[jax.ShapeDtypeStruct(b.shape, b.dtype) for b in bufs],
        in_specs=[ANY] * nw + [SEM, SEM, ANY], out_specs=[ANY] * nw,
        input_output_aliases={w: w for w in range(nw)}, name=name,
        compiler_params=pltpu.CompilerParams(has_side_effects=DATAFLOW))(*bufs, r_send, r_recv, after)


def gather_forward(name, bufs):
    nw = len(bufs)

    def body(*refs):
        dst = refs[nw:2 * nw]
        send, recv = refs[2 * nw:]
        x, y, c = _place()
        sib = (x, y, 1 - c)
        cps = []
        for w in range(nw):
            h = dst[w].shape[1] // 2
            for k, (_, _, s) in enumerate(_other_chips(x, y)):
                landed = dst[w].at[s, pl.ds(c * h, h), :]
                cp = pltpu.make_async_remote_copy(src_ref=landed, dst_ref=landed, send_sem=send.at[w, k], recv_sem=recv.at[w, k],
                                                  device_id=sib, device_id_type=MESH)
                cp.start()
                cps.append(cp)
        for w in range(nw):
            h = dst[w].shape[1] // 2
            for k, (_, _, s) in enumerate(_other_chips(x, y)):
                other = dst[w].at[s, pl.ds((1 - c) * h, h), :]
                pltpu.make_async_remote_copy(src_ref=other, dst_ref=other, send_sem=send.at[w, k], recv_sem=recv.at[w, k],
                                             device_id=sib, device_id_type=MESH).wait_recv()
        for cp in cps:
            cp.wait_send()

    sem = pltpu.SemaphoreType.DMA((nw, 3))
    return pl.pallas_call(
        body, out_shape=[jax.ShapeDtypeStruct(b.shape, b.dtype) for b in bufs],
        in_specs=[ANY] * nw, out_specs=[ANY] * nw, input_output_aliases={w: w for w in range(nw)},
        scratch_shapes=[sem, sem], name=name, compiler_params=pltpu.CompilerParams(has_side_effects=True))(*bufs)


def _scatter_copy(src, dst, w, k, chip, c, send, recv):
    px, py, s = chip
    return pltpu.make_async_remote_copy(src_ref=src.at[s], dst_ref=dst.at[k], send_sem=send.at[3 * w + k], recv_sem=recv.at[3 * w + k],
                                        device_id=(px, py, c), device_id_type=MESH)


def scatter_start(name, parts):
    nw = len(parts)
    lands = [pltpu.with_memory_space_constraint(lax.empty((N_CHIPS - 1,) + p.shape[1:], p.dtype), pltpu.HBM) for p in parts]

    def body(*refs):
        outs = refs[2 * nw:]
        send, recv = outs[0], outs[1]
        src, dst, token = outs[2:2 + nw], outs[2 + nw:2 + 2 * nw], outs[2 + 2 * nw]
        x, y, c = _place()
        for w in range(nw):
            for k, chip in enumerate(_other_chips(x, y)):
                _scatter_copy(src[w], dst[w], w, k, chip, c, send, recv).start()
        token[...] = jnp.zeros_like(token)

    sem = pltpu.SemaphoreType.DMA((3 * nw,))
    outs = pl.pallas_call(
        body, out_shape=[sem, sem] + [jax.ShapeDtypeStruct(p.shape, p.dtype) for p in parts]
        + [jax.ShapeDtypeStruct(l.shape, l.dtype) for l in lands] + [TOKEN],
        in_specs=[ANY] * (2 * nw), out_specs=[SEM, SEM] + [ANY] * (2 * nw) + [VM],
        input_output_aliases={i: 2 + i for i in range(2 * nw)}, name=name,
        compiler_params=pltpu.CompilerParams(has_side_effects=DATAFLOW))(*parts, *lands)
    return outs[0], outs[1], list(outs[2:2 + nw]), list(outs[2 + nw:2 + 2 * nw]), outs[2 + 2 * nw]


def scatter_wait(name, parts, lands, send, recv, after):
    nw = len(parts)

    def body(*refs):
        src, dst = refs[:nw], refs[nw:2 * nw]
        send_ref, recv_ref = refs[2 * nw], refs[2 * nw + 1]
        x, y, c = _place()
        for w in range(nw):
            for k, chip in enumerate(_other_chips(x, y)):
                cp = _scatter_copy(src[w], dst[w], w, k, chip, c, send_ref, recv_ref)
                cp.wait_send()
                cp.wait_recv()

    outs = pl.pallas_call(
        body, out_shape=[jax.ShapeDtypeStruct(a.shape, a.dtype) for a in list(parts) + list(lands)],
        in_specs=[ANY] * (2 * nw) + [SEM, SEM, ANY], out_specs=[ANY] * (2 * nw),
        input_output_aliases={i: i for i in range(2 * nw)}, name=name,
        compiler_params=pltpu.CompilerParams(has_side_effects=DATAFLOW))(*parts, *lands, send, recv, after)
    return list(outs[:nw]), list(outs[nw:])


def _sibling_copy(src, dst, w, c, half_rows, send, recv, sib):
    if half_rows:
        h = src.shape[1] // 2
        src = src.at[:, pl.ds((1 - c) * h, h), :]
    return pltpu.make_async_remote_copy(src_ref=src, dst_ref=dst, send_sem=send.at[w], recv_sem=recv.at[w],
                                        device_id=sib, device_id_type=MESH)


def _landing(shape, dtype):
    return pltpu.with_memory_space_constraint(lax.empty(shape, dtype), pltpu.HBM)


def sibling_start(name, srcs, half_rows):
    nw = len(srcs)
    lands = [_landing((s.shape[0], s.shape[1] // 2, s.shape[2]) if half_rows else s.shape, s.dtype) for s in srcs]

    def body(*refs):
        outs = refs[2 * nw:]
        send, recv = outs[0], outs[1]
        src, dst, token = outs[2:2 + nw], outs[2 + nw:2 + 2 * nw], outs[2 + 2 * nw]
        x, y, c = _place()
        for w in range(nw):
            _sibling_copy(src[w], dst[w], w, c, half_rows, send, recv, (x, y, 1 - c)).start()
        token[...] = jnp.zeros_like(token)

    sem = pltpu.SemaphoreType.DMA((nw,))
    outs = pl.pallas_call(
        body, out_shape=[sem, sem] + [jax.ShapeDtypeStruct(a.shape, a.dtype) for a in list(srcs) + lands] + [TOKEN],
        in_specs=[ANY] * (2 * nw), out_specs=[SEM, SEM] + [ANY] * (2 * nw) + [VM],
        input_output_aliases={i: 2 + i for i in range(2 * nw)}, name=name,
        compiler_params=pltpu.CompilerParams(has_side_effects=DATAFLOW))(*srcs, *lands)
    return outs[0], outs[1], list(outs[2:2 + nw]), list(outs[2 + nw:2 + 2 * nw]), outs[2 + 2 * nw]


def sibling_wait(name, srcs, lands, send, recv, half_rows, after):
    nw = len(srcs)

    def body(*refs):
        src, dst = refs[:nw], refs[nw:2 * nw]
        send_ref, recv_ref = refs[2 * nw], refs[2 * nw + 1]
        x, y, c = _place()
        for w in range(nw):
            cp = _sibling_copy(src[w], dst[w], w, c, half_rows, send_ref, recv_ref, (x, y, 1 - c))
            cp.wait_send()
            cp.wait_recv()

    outs = pl.pallas_call(
        body, out_shape=[jax.ShapeDtypeStruct(a.shape, a.dtype) for a in list(srcs) + list(lands)],
        in_specs=[ANY] * (2 * nw) + [SEM, SEM, ANY], out_specs=[ANY] * (2 * nw),
        input_output_aliases={i: i for i in range(2 * nw)}, name=name,
        compiler_params=pltpu.CompilerParams(has_side_effects=DATAFLOW))(*srcs, *lands, send, recv, after)
    return list(outs[:nw]), list(outs[nw:])


def _peer(x, y, c, r):
    return (1 - x if r & 4 else x, 1 - y if r & 2 else y, 1 - c if r & 1 else c)


def _everyone_copy(buf, r, x, y, c, send, recv, landing):
    px, py, pc = _peer(x, y, c, r)
    slot = buf.at[4 * px + 2 * py + pc] if landing else buf.at[4 * x + 2 * y + c]
    return pltpu.make_async_remote_copy(src_ref=slot, dst_ref=slot, send_sem=send.at[r - 1], recv_sem=recv.at[r - 1],
                                        device_id=(px, py, pc), device_id_type=MESH)


def everyone_start(name, buf):
    def body(buf_in, send, recv, buf_ref, token):
        x, y, c = _place()
        for r in range(1, N_DEV):
            _everyone_copy(buf_ref, r, x, y, c, send, recv, False).start()
        token[...] = jnp.zeros_like(token)

    sem = pltpu.SemaphoreType.DMA((N_DEV - 1,))
    return pl.pallas_call(
        body, out_shape=[sem, sem, jax.ShapeDtypeStruct(buf.shape, buf.dtype), TOKEN],
        in_specs=[ANY], out_specs=[SEM, SEM, ANY, VM], input_output_aliases={0: 2}, name=name,
        compiler_params=pltpu.CompilerParams(has_side_effects=DATAFLOW))(buf)


def everyone_wait(name, buf, send, recv, after):
    def body(buf_ref, send_ref, recv_ref, after_ref, out_ref):
        x, y, c = _place()
        for r in range(1, N_DEV):
            _everyone_copy(buf_ref, r, x, y, c, send_ref, recv_ref, False).wait_send()
            _everyone_copy(buf_ref, r, x, y, c, send_ref, recv_ref, True).wait_recv()

    return pl.pallas_call(
        body, out_shape=jax.ShapeDtypeStruct(buf.shape, buf.dtype), in_specs=[ANY, SEM, SEM, ANY], out_specs=ANY,
        input_output_aliases={0: 0}, name=name,
        compiler_params=pltpu.CompilerParams(has_side_effects=DATAFLOW))(buf, send, recv, after)


def all_reduce_small(v):
    R, n = v.shape

    def body(v_ref, o_ref, all_ref, send_sems, recv_sems, local_sem):
        x, y, c = _place()
        me, sib = (x, y, c), (x, y, 1 - c)
        chips = [(1 - x, y), (x, 1 - y), (1 - x, 1 - y)]

        def rows(px, py, pc):
            return all_ref.at[pl.ds((4 * px + 2 * py + pc) * R, R), :]

        def copy(k, block, to, src=None):
            return pltpu.make_async_remote_copy(src_ref=rows(*block) if src is None else src, dst_ref=rows(*block),
                                                send_sem=send_sems.at[k], recv_sem=recv_sems.at[k],
                                                device_id=to, device_id_type=MESH)

        own = pltpu.make_async_copy(v_ref, rows(*me), local_sem)
        own.start()
        first = [copy(0, me, sib, src=v_ref)]
        first += [copy(1 + j, me, (*chip, c), src=v_ref) for j, chip in enumerate(chips)]
        for cp in first:
            cp.start()
        passed = [copy(4 + j, (*chip, c), sib) for j, chip in enumerate(chips)]
        for j, chip in enumerate(chips):
            copy(1 + j, (*chip, c), me).wait_recv()
            passed[j].start()
        copy(0, sib, me).wait_recv()
        for j, chip in enumerate(chips):
            copy(4 + j, (*chip, 1 - c), me).wait_recv()
        for cp in first + passed:
            cp.wait_send()
        own.wait()
        acc = all_ref[pl.ds(0, R), :]
        for d in range(1, N_DEV):
            acc = acc + all_ref[pl.ds(d * R, R), :]
        o_ref[...] = acc

    vm = pl.BlockSpec(memory_space=pltpu.VMEM)
    return pl.pallas_call(
        body, out_shape=jax.ShapeDtypeStruct((R, n), f32), in_specs=[vm], out_specs=vm,
        scratch_shapes=[pltpu.VMEM((N_DEV * R, n), f32), pltpu.SemaphoreType.DMA((7,)), pltpu.SemaphoreType.DMA((7,)),
                        pltpu.SemaphoreType.DMA],
        name="all_reduce_small", compiler_params=pltpu.CompilerParams(vmem_limit_bytes=VMEM_LIMIT, has_side_effects=True))(v)


def local_step(x, pos, tgt, small, d_in, get_w, put_g, first_dep=None, tick=lambda name, after: None):
    L, D = x.shape
    d_kv = N_KV_HEADS * HEAD_DIM
    d_ssm = small["d_skip"].shape[1]
    d_attn = d_in - 2 * d_kv - d_ssm
    big = {}
    G = d_ssm // SSM_GROUP
    N, P = SSM_STATE, SSM_GROUP
    gbf = bf16

    half_dim = HEAD_DIM // 2
    inv_freq = ROPE_THETA ** (-jnp.arange(half_dim, dtype=f32) / half_dim)
    inv_freq = jnp.tile(inv_freq, LANES // half_dim).reshape(1, LANES)
    sink_b = jnp.broadcast_to(small["sinks"].reshape(-1, 1), (small["sinks"].size, LANES))

    spread = jnp.repeat(jnp.eye(G, dtype=f32), P, axis=0)
    logdt_b = jnp.broadcast_to(small["log_dt"].reshape(G, 1), (G, N))
    bt_re = small["b_re"].reshape(G, N, P).transpose(0, 2, 1).reshape(G * P, N)
    bt_im = small["b_im"].reshape(G, N, P).transpose(0, 2, 1).reshape(G * P, N)
    a_re, a_im = small["a_re"].reshape(G, N), small["a_im"].reshape(G, N)
    lam_re, lam_im, bbt_re, bbt_im = ssm_params(a_re, a_im, logdt_b, bt_re, bt_im, spread)
    bd_re = _block_diag(bbt_re.reshape(G, P, N), P, N).astype(bf16)
    bd_im = _block_diag(bbt_im.reshape(G, P, N), P, N).astype(bf16)
    c_re = small["c_re"].reshape(G, P, N).transpose(0, 2, 1)
    c_im = small["c_im"].reshape(G, P, N).transpose(0, 2, 1)
    cd_re = _block_diag(c_re, N, P).astype(bf16)
    cd_im = _block_diag(c_im, N, P).astype(bf16)
    lam_re_l, lam_im_l = _state_layout(lam_re), _state_layout(lam_im)

    def k1(i, nt, xt, g):
        return (rms_fwd(xt, g),)
    xn = rowwise("pre_mix_norm", k1, L, [full(x)], [small["g_pre_mix"]], [(D, bf16)], dep=first_dep)[0]
    big["w_in"] = get_w("w_in", xn)
    proj = mm_nn("proj_in", xn, big["w_in"])
    qr, kk, vv, u_bf = qkv_prep(proj, pos, inv_freq, d_attn, d_kv)
    attn = attn_fwd(qr, kk, vv, sink_b)
    u_cb = (d_attn + 2 * d_kv) // (d_ssm // 2)
    token = tick("attn", attn)
    y, z_bf, s_re, s_im = ssm_fwd(u_bf, proj, u_cb, bd_re, bd_im, cd_re, cd_im, lam_re_l, lam_im_l, small["d_skip"], dep=token)
    token = tick("ssm", z_bf)
    big["w_glu"] = get_w("w_glu", z_bf)
    gl = mm_nn("glu_proj", z_bf, big["w_glu"], dep=token)

    def k6(i, nt, at, yt, glt, bg, ga, gs):
        ssm = gelu(yt) * sigmoid(glt + bg)
        return (jnp.concatenate([rms_fwd(at, ga), rms_fwd(ssm, gs)], axis=1),)
    mixed = rowwise("mix_norms", k6, L, [full(attn), full(y), full(gl)],
                    [small["b_glu"], small["g_attn_out"], small["g_ssm_out"]], [(d_attn + d_ssm, bf16)])[0]
    big["w_o"] = get_w("w_o", mixed)
    mix = mm_nn("proj_out", mixed, big["w_o"])

    def k7(i, nt, xt, mt, gpm, gpf):
        h = xt + rms_fwd(mt, gpm)
        return h, rms_fwd(h, gpf)
    h, hn = rowwise("post_mix", k7, L, [full(x), full(mix)], [small["g_post_mix"], small["g_pre_ffn"]], [(D, f32), (D, bf16)])
    big["w_gate"] = get_w("w_gate", hn)
    big["w_up"] = get_w("w_up", hn)
    gt, up, hid = ffn_hidden(hn, big["w_gate"], big["w_up"])
    d_ff_dim = gt.shape[1]
    big["w_down"] = get_w("w_down", hid)
    ff = mm_nn("ffn_down", hid, big["w_down"], tk=d_ff_dim // 2)

    def k9(i, nt, ht, fft, tt, g):
        out = ht + rms_fwd(fft, g)
        err = out - tt
        per_row = jnp.mean(err * err, axis=-1, keepdims=True)
        loss = 0.5 * jnp.sum(per_row) * jnp.where(_lane((1, LANES)) == 0, 1.0, 0.0)
        d_out = err * (1.0 / D)
        d_ff, dg = rms_bwd(fft, g, d_out)
        return d_out, d_ff, dg, loss
    d_out, d_ff, dg_post_ffn, loss = rowwise("loss_head", k9, L, [full(h), full(ff), full(tgt)], [small["g_post_ffn"]],
                                             [(D, f32), (D, bf16)], reds=[D, LANES])

    d_gt, d_up = ffn_hidden_grad(d_ff, big["w_down"], gt, up)
    token = put_g("w_down", mm_tn("dw_down", hid, d_ff, out_dtype=gbf, tm=d_ff_dim // N_CHIPS))
    d_hn = mm_nt_pair("d_hn", d_gt, big["w_gate"], d_up, big["w_up"], dep=token)
    token = put_g("w_gate", mm_tn("dw_gate", hn, d_gt, shards=N_CHIPS, out_dtype=gbf))
    token = put_g("w_up", mm_tn("dw_up", hn, d_up, shards=N_CHIPS, out_dtype=gbf, dep=token))

    def k11(i, nt, ht, da, do, mt, gpf, gpm):
        dh_n, dg_pf = rms_bwd(ht, gpf, da)
        dh = do + dh_n
        d_mix, dg_pm = rms_bwd(mt, gpm, dh)
        return dh, d_mix, dg_pf, dg_pm
    dh, d_mix, dg_pre_ffn, dg_post_mix = rowwise("post_mix_grad", k11, L, [full(h), full(d_hn), full(d_out), full(mix)],
                                                 [small["g_pre_ffn"], small["g_post_mix"]], [(D, f32), (D, bf16)], reds=[D, D], dep=token)
    d_mixed = mm_nt("d_mixed", d_mix, big["w_o"])
    token = put_g("w_o", mm_tn("dw_o", mixed, d_mix, out_dtype=gbf))

    def k12(i, nt, at, yt, glt, da_n, ds_n, bg, ga, gs):
        z = gelu(yt)
        sg = sigmoid(glt + bg)
        ssm = z * sg
        d_at, dga = rms_bwd(at, ga, da_n)
        d_ssm_t, dgs = rms_bwd(ssm, gs, ds_n)
        d_gl = d_ssm_t * z * sg * (1.0 - sg)
        return d_at, d_ssm_t * sg, d_gl, dga, dgs, colsum(d_gl)
    d_attn_o, dz1, d_gl, dg_attn, dg_ssm, db_glu = rowwise(
        "mix_norms_grad", k12, L, [full(attn), full(y), full(gl), (d_mixed, d_attn, 0, 0), (d_mixed, d_ssm, d_attn // d_ssm, 0)],
        [small["b_glu"], small["g_attn_out"], small["g_ssm_out"]], [(d_attn, f32), (d_ssm, f32), (d_ssm, bf16)],
        reds=[d_attn, d_ssm, d_ssm], dep=token)
    dz2 = mm_nt("d_glu_in", d_gl, big["w_glu"])
    token = put_g("w_glu", mm_tn("dw_glu", z_bf, d_gl, out_dtype=gbf))

    du, dbd_re, dbd_im, dcd_re, dcd_im, dlam_re_l, dlam_im_l, dd_skip = ssm_bwd(
        y, dz1, dz2, u_bf, proj, u_cb, s_re, s_im, bd_re, bd_im, cd_re, cd_im, lam_re_l, lam_im_l, small["d_skip"], dep=token)
    dq, dkk_c, dkk_p, dvv_c, dvv_p, dsink = attn_bwd(qr, kk, vv, sink_b, attn, d_attn_o)
    d_proj = qkv_grad(dq, dkk_c, dkk_p, dvv_c, dvv_p, du, pos, inv_freq)
    d_xn = mm_nt("d_xn", d_proj, big["w_in"])
    token = put_g("w_in", mm_tn("dw_in", xn, d_proj, shards=N_CHIPS, out_dtype=gbf))

    def k17(i, nt, xt, dxn, dht, g):
        dx, dg = rms_bwd(xt, g, dxn)
        return dht + dx, dg
    grad_x, dg_pre_mix = rowwise("pre_mix_grad", k17, L, [full(x), full(d_xn), full(dh)], [small["g_pre_mix"]],
                                 [(D, f32)], reds=[D], dep=token)

    gather = spread.T
    dbbt_re = _block_diag_take(dbd_re, P, N).reshape(G * P, N)
    dbbt_im = _block_diag_take(dbd_im, P, N).reshape(G * P, N)
    d_a_re, d_a_im, d_logdt, dbt_re, dbt_im = ssm_params_grad(
        a_re, a_im, logdt_b, bt_re, bt_im, spread, gather,
        _state_layout_inv(dlam_re_l, G, N), _state_layout_inv(dlam_im_l, G, N), dbbt_re, dbbt_im)
    q_per_kv = d_attn // HEAD_DIM // N_KV_HEADS
    small_grads = {
        "g_pre_mix": dg_pre_mix, "sinks": dsink[:, :q_per_kv, 0].reshape(1, -1),
        "a_re": d_a_re, "a_im": d_a_im, "log_dt": d_logdt.reshape(1, G),
        "b_re": dbt_re, "b_im": dbt_im,
        "c_re": _block_diag_take(dcd_re, N, P).transpose(0, 2, 1), "c_im": _block_diag_take(dcd_im, N, P).transpose(0, 2, 1),
        "d_skip": dd_skip, "b_glu": db_glu, "g_attn_out": dg_attn, "g_ssm_out": dg_ssm,
        "g_post_mix": dg_post_mix, "g_pre_ffn": dg_pre_ffn, "g_post_ffn": dg_post_ffn,
    }
    return loss, grad_x, small_grads


WEIGHTS = ['g_pre_mix', 'w_in', 'sinks', 'a_re', 'a_im', 'log_dt', 'b_re', 'b_im', 'c_re', 'c_im', 'd_skip', 'w_glu', 'b_glu',
           'g_attn_out', 'g_ssm_out', 'w_o', 'g_post_mix', 'g_pre_ffn', 'w_gate', 'w_up', 'w_down', 'g_post_ffn']
BIG = ['w_in', 'w_glu', 'w_o', 'w_gate', 'w_up', 'w_down']
COL_SHARDED = ['w_in', 'w_gate', 'w_up']
SMALL = [n for n in WEIGHTS if n not in BIG]
GATHER_GROUPS = [["w_in"], ["w_glu", "w_o"], ["w_gate", "w_up"], ["w_down"]]
REDUCE_GROUPS = [["w_down", "w_gate", "w_up"], ["w_o", "w_glu", "w_in"]]


PACK_ROWS = 256


def _pack(parts):
    flat = jnp.concatenate([p.reshape(-1) for p in parts])
    pad = (-flat.size) % (PACK_ROWS * LANES)
    return jnp.pad(flat, (0, pad)).reshape(-1, LANES)


TRANSPOSED_VIEW = ("b_re", "b_im")


def small_view(name, a):
    if name in TRANSPOSED_VIEW:
        a = a.transpose(0, 1, 3, 2)
    return a.reshape(-1, a.shape[-1])


def small_unview(name, p, shape):
    if name in TRANSPOSED_VIEW:
        return p.reshape(shape[0], shape[1], shape[3], shape[2]).transpose(0, 1, 3, 2)
    return p.reshape(shape)


def _unpack(packed, shapes):
    flat = packed.reshape(-1)
    out, off = [], 0
    for s in shapes:
        n = int(np.prod(s))
        out.append(flat[off:off + n].reshape(s))
        off += n
    return out


def kernel(x, positions, g_pre_mix, w_in, sinks, a_re, a_im, log_dt, b_re, b_im, c_re, c_im, d_skip, w_glu, b_glu, g_attn_out, g_ssm_out, w_o, g_post_mix, g_pre_ffn, w_gate, w_up, w_down, g_post_ffn, loss_target, m_g_pre_mix, m_w_in, m_sinks, m_a_re, m_a_im, m_log_dt, m_b_re, m_b_im, m_c_re, m_c_im, m_d_skip, m_w_glu, m_b_glu, m_g_attn_out, m_g_ssm_out, m_w_o, m_g_post_mix, m_g_pre_ffn, m_w_gate, m_w_up, m_w_down, m_g_post_ffn, v_g_pre_mix, v_w_in, v_sinks, v_a_re, v_a_im, v_log_dt, v_b_re, v_b_im, v_c_re, v_c_im, v_d_skip, v_w_glu, v_b_glu, v_g_attn_out, v_g_ssm_out, v_w_o, v_g_post_mix, v_g_pre_ffn, v_w_gate, v_w_up, v_w_down, v_g_post_ffn):
    args = dict(locals())
    w = {n: args[n] for n in WEIGHTS}
    m = {n: args["m_" + n] for n in WEIGHTS}
    v = {n: args["v_" + n] for n in WEIGHTS}
    L, D = x.shape[1], x.shape[2]

    ax, ay, ac = _place()
    mine_arr = (2 * ax + ay).astype(jnp.int32).reshape(1)
    c_arr = ac.astype(jnp.int32).reshape(1)

    me_arr = (4 * ax + 2 * ay + ac).astype(jnp.int32).reshape(1)

    bufs = {"w_in": into_slot("cast_w_in", w["w_in"][0], mine_arr, N_CHIPS, bf16)}
    (first_sems,), (bufs["w_in"],), token = gather_start("gather_start_in", [bufs["w_in"]], [[0]], mine_arr)
    sems = {"w_in": first_sems}
    for n in BIG[1:]:
        bufs[n] = into_slot("cast_" + n, w[n][0], mine_arr, N_CHIPS, bf16, dep=token)
    first = ["w_glu", "w_o", "w_gate", "w_up"]
    (sems["w_glu"], sems["w_gate"], sems["w_up"]), started, token = gather_start(
        "gather_start_rest", [bufs[n] for n in first], [[0, 1], [2], [3]], token, neighbours_only=(2, 3))
    bufs.update(zip(first, started))
    relays, ready = {}, set()

    def tick(name, after):
        n, more = ("w_gate", []) if name == "attn" else ("w_up", ["w_down"])
        r_send, r_recv, sems["w_down"], (bufs[n],), down, tok = gather_relay(
            "gather_relay_" + n, [bufs[n]], [sems[n]], [bufs[k] for k in more], after)
        bufs.update(zip(more, down))
        relays[n] = (r_send, r_recv)
        return tok

    def get_w(n, after):
        if n not in ready:
            members = [g for g in GATHER_GROUPS if n in g][0]
            if members[0] in relays:
                landed = [gather_wait_relay("gather_wait_" + k, [bufs[k]], *relays[k], after)[0] for k in members]
            else:
                landed = gather_wait("gather_wait_" + members[0], [bufs[k] for k in members], *sems[members[0]], after)
            bufs.update(zip(members, gather_forward("gather_forward_" + members[0], landed)))
            ready.update(members)
        g = bufs[n]
        return g if n in COL_SHARDED else g.reshape(g.shape[0] * g.shape[1], g.shape[2])

    swaps, inflight = {}, []

    def put_g(n, g):
        g3 = g if n in COL_SHARDED else g.reshape(N_CHIPS, g.shape[0] // N_CHIPS, g.shape[1])
        swaps[n] = sibling_start("swap_start_" + n, [g3], True)
        for gi, members in enumerate(REDUCE_GROUPS):
            if n == members[-1]:
                last = swaps[n][4]
                pair = []
                for k in members:
                    send, recv, srcs, lands, _ = swaps[k]
                    (src,), (got,) = sibling_wait("swap_wait_" + k, srcs, lands, send, recv, True, last)
                    pair.append(pair_sum("pair_sum_" + k, src, got, c_arr))
                send, recv, parts, lands, tok = scatter_start("scatter_start_%d" % gi, pair)
                inflight.append((members, send, recv, parts, lands))
                return tok
        return swaps[n][4]

    small = {n: w[n].reshape(1, -1) for n in SMALL}
    pos = positions.reshape(L, 1).astype(f32)
    d_in = N_CHIPS * w["w_in"].shape[2]
    loss, grad_x, small_grads = local_step(x[0], pos, loss_target[0], small, d_in, get_w, put_g, first_dep=token, tick=tick)

    shapes = [w[n].shape for n in SMALL]
    blocks = into_slot("small_block", _pack([small_grads[n] for n in SMALL] + [loss]), me_arr, N_DEV, f32)
    small_send, small_recv, blocks, after = everyone_start("small_start", blocks)

    grads, delta, new_m, new_v = {}, {}, {}, {}
    for gi, (members, send, recv, parts, lands) in enumerate(inflight):
        parts, landed = scatter_wait("scatter_wait_%d" % gi, parts, lands, send, recv, after)
        joins, dep = [], None
        for k, p, t in zip(members, parts, landed):
            joins.append(sibling_start("join_start_" + k, [chip_sum("chip_sum_" + k, p, t, mine_arr, dep=dep)], False))
            dep = after = joins[-1][4]
        for n, (send, recv, srcs, lands, _) in zip(members, joins):
            (own,), (sib,) = sibling_wait("join_wait_" + n, srcs, lands, send, recv, False, after)
            g_, d_, m_, v_ = adamw_halves("adamw_" + n, w[n][0], own, sib, m[n][0], v[n][0], c_arr)
            grads[n], delta[n], new_m[n], new_v[n] = g_[None], d_[None], m_[None], v_[None]
            after = v_
    blocks = everyone_wait("small_wait", blocks, small_send, small_recv, after)
    small_sum = sum_slots("small_sum", blocks)
    *small_g, loss = _unpack(small_sum, [small_view(n, w[n]).shape for n in SMALL] + [loss.shape])
    loss = loss[0, 0]
    outs = adamw_many("adamw_small", [small_view(n, w[n]) for n in SMALL], small_g,
                      [small_view(n, m[n]) for n in SMALL], [small_view(n, v[n]) for n in SMALL])
    for t, parts in zip((grads, delta, new_m, new_v), (small_g,) + tuple(outs)):
        t.update({n: small_unview(n, p, w[n].shape) for n, p in zip(SMALL, parts)})

    return (loss, grad_x[None], *[grads[n] for n in WEIGHTS], *[delta[n] for n in WEIGHTS],
            *[new_m[n] for n in WEIGHTS], *[new_v[n] for n in WEIGHTS])
```

```python
import functools
import math

import jax
import jax.numpy as jnp
import numpy as np
from jax import lax
from jax.experimental import pallas as pl
from jax.experimental.pallas import tpu as pltpu

f32 = jnp.float32
bf16 = jnp.bfloat16
HIGHEST = lax.Precision.HIGHEST
MESH = pl.DeviceIdType.MESH

HEAD_DIM = 64
N_KV_HEADS = 4
ATTN_BLOCK = 128
ROPE_THETA = 10000.0
SSM_GROUP = 16
SSM_STATE = 64
RMS_EPS = 1e-6
LANES = 128
SUBLANES = 8
VMEM_LIMIT = 52 * 1024 * 1024
N_CHIPS = 4
N_DEV = 8
NEG = -1e30

ADAM_LR, ADAM_B1, ADAM_B2, ADAM_EPS, ADAM_WD, ADAM_STEP = 0.001, 0.9, 0.999, 1e-08, 0.01, 10

NN = (((1,), (0,)), ((), ()))
NT = (((1,), (1,)), ((), ()))
TN = (((0,), (0,)), ((), ()))


def _params(*sem):
    return pltpu.CompilerParams(dimension_semantics=sem or None, vmem_limit_bytes=VMEM_LIMIT)


def _dot(a, b, dims=NN):
    return lax.dot_general(a, b, dims, preferred_element_type=f32)


def _pick(dim, pref):
    t = min(dim, pref)
    while dim % t:
        t -= LANES
    assert t > 0, (dim, pref)
    return t


ANY = pl.BlockSpec(memory_space=pl.ANY)


def _with_dep(in_specs, operands, dep):
    if dep is None:
        return list(in_specs), list(operands), 0
    return list(in_specs) + [ANY], list(operands) + [dep], 1


def _mm_call(name, grid, in_specs, out_spec, out_shape, acc_shape, dims, operands, dep=None):
    nk = grid[2]
    in_specs, operands, n_dep = _with_dep(in_specs, operands, dep)

    def body_one(a_ref, b_ref, *rest):
        o_ref = rest[n_dep]
        o_ref[...] = _dot(a_ref[...], b_ref[...], dims).astype(o_ref.dtype)

    def body(a_ref, b_ref, *rest):
        o_ref, acc_ref = rest[n_dep], rest[n_dep + 1]
        k = pl.program_id(2)

        @pl.when(k == 0)
        def _():
            acc_ref[...] = _dot(a_ref[...], b_ref[...], dims)

        @pl.when((k > 0) & (k < nk - 1))
        def _():
            acc_ref[...] += _dot(a_ref[...], b_ref[...], dims)

        @pl.when(k == nk - 1)
        def _():
            o_ref[...] = (acc_ref[...] + _dot(a_ref[...], b_ref[...], dims)).astype(o_ref.dtype)

    return pl.pallas_call(
        body_one if nk == 1 else body, out_shape=out_shape, grid=grid, in_specs=in_specs, out_specs=out_spec,
        scratch_shapes=[] if nk == 1 else [pltpu.VMEM(acc_shape, f32)], name=name,
        compiler_params=_params("parallel", "parallel", "arbitrary"))(*operands)


def mm_nt_pair(name, a1, b1, a2, b2, tm=1024, tk=1024, dep=None):
    M = a1.shape[0]
    S, K, n = b1.shape
    tm, tko = _pick(M, tm), _pick(K, tk)
    nk = 2 * S

    def body(a1_ref, b1_ref, a2_ref, b2_ref, *rest):
        o_ref, acc_ref = rest[-2], rest[-1]
        k = pl.program_id(2)

        @pl.when(k == 0)
        def _():
            acc_ref[...] = _dot(a1_ref[...], b1_ref[...], NT)

        @pl.when((k > 0) & (k < S))
        def _():
            acc_ref[...] += _dot(a1_ref[...], b1_ref[...], NT)

        @pl.when((k >= S) & (k < nk - 1))
        def _():
            acc_ref[...] += _dot(a2_ref[...], b2_ref[...], NT)

        @pl.when(k == nk - 1)
        def _():
            o_ref[...] = acc_ref[...] + _dot(a2_ref[...], b2_ref[...], NT)

    first = lambda k: jnp.minimum(k, S - 1)
    second = lambda k: jnp.maximum(k - S, 0)
    in_specs = [pl.BlockSpec((tm, n), lambda i, j, k: (i, first(k))), pl.BlockSpec((None, tko, n), lambda i, j, k: (first(k), j, 0)),
                pl.BlockSpec((tm, n), lambda i, j, k: (i, second(k))), pl.BlockSpec((None, tko, n), lambda i, j, k: (second(k), j, 0))]
    in_specs, operands, _ = _with_dep(in_specs, (a1, b1, a2, b2), dep)
    return pl.pallas_call(
        body, out_shape=jax.ShapeDtypeStruct((M, K), f32), grid=(M // tm, K // tko, nk), in_specs=in_specs,
        out_specs=pl.BlockSpec((tm, tko), lambda i, j, k: (i, j)), scratch_shapes=[pltpu.VMEM((tm, tko), f32)], name=name,
        compiler_params=_params("parallel", "parallel", "arbitrary"))(*operands)


def mm_nn(name, a, b, out_dtype=f32, tm=1024, tn=1024, tk=2048, dep=None):
    M, K = a.shape
    tm, tk = _pick(M, tm), _pick(K, tk)
    if b.ndim == 3:
        S, _, n = b.shape
        tn = _pick(n, 2048)
        per = n // tn
        b_spec = pl.BlockSpec((None, tk, tn), lambda i, j, k: (j // per, k, j % per))
        N = S * n
    else:
        N = b.shape[1]
        tn = _pick(N, tn)
        b_spec = pl.BlockSpec((tk, tn), lambda i, j, k: (k, j))
    grid = (M // tm, N // tn, K // tk)
    return _mm_call(name, grid, [pl.BlockSpec((tm, tk), lambda i, j, k: (i, k)), b_spec],
                    pl.BlockSpec((tm, tn), lambda i, j, k: (i, j)), jax.ShapeDtypeStruct((M, N), out_dtype),
                    (tm, tn), NN, (a, b), dep)


def mm_nt(name, a, b, out_dtype=f32, tm=1024, tn=2048, tk=1024, dep=None):
    M, N = a.shape
    tm = _pick(M, tm)
    if b.ndim == 3:
        S, K, n = b.shape
        tr = _pick(n, 2048)
        per = n // tr
        tko = _pick(K, tk)
        b_spec = pl.BlockSpec((None, tko, tr), lambda i, j, k: (k // per, j, k % per))
    else:
        K = b.shape[0]
        tr = _pick(N, tn)
        tko = _pick(K, tk)
        b_spec = pl.BlockSpec((tko, tr), lambda i, j, k: (j, k))
    grid = (M // tm, K // tko, N // tr)
    return _mm_call(name, grid, [pl.BlockSpec((tm, tr), lambda i, j, k: (i, k)), b_spec],
                    pl.BlockSpec((tm, tko), lambda i, j, k: (i, j)), jax.ShapeDtypeStruct((M, K), out_dtype),
                    (tm, tko), NT, (a, b), dep)


def mm_tn(name, a, b, shards=None, out_dtype=f32, tm=1024, tn=1024, tl=2048, dep=None):
    L, K = a.shape
    N = b.shape[1]
    tl, tko = _pick(L, tl), _pick(K, tm)
    if shards:
        n = N // shards
        tn = _pick(n, 2048)
        per = n // tn
        o_spec = pl.BlockSpec((None, tko, tn), lambda i, j, k: (j // per, i, j % per))
        o_shape = jax.ShapeDtypeStruct((shards, K, n), out_dtype)
    else:
        tn = _pick(N, tn)
        o_spec = pl.BlockSpec((tko, tn), lambda i, j, k: (i, j))
        o_shape = jax.ShapeDtypeStruct((K, N), out_dtype)
    grid = (K // tko, N // tn, L // tl)
    return _mm_call(name, grid, [pl.BlockSpec((tl, tko), lambda i, j, k: (k, i)),
                                 pl.BlockSpec((tl, tn), lambda i, j, k: (k, j))],
                    o_spec, o_shape, (tko, tn), TN, (a, b), dep)


def ffn_hidden(hn, w_gate, w_up, tm=512):
    M, K = hn.shape
    S, _, n = w_gate.shape
    tm = _pick(M, tm)

    def body(a_ref, g_ref, u_ref, gt_ref, up_ref, hid_ref):
        a = a_ref[...]
        g = _dot(a, g_ref[...])
        u = _dot(a, u_ref[...])
        gt_ref[...] = g.astype(bf16)
        up_ref[...] = u.astype(bf16)
        hid_ref[...] = (g * sigmoid(g) * u).astype(bf16)

    w_spec = pl.BlockSpec((None, K, n), lambda s, i: (s, 0, 0))
    o_spec = pl.BlockSpec((tm, n), lambda s, i: (i, s))
    o = jax.ShapeDtypeStruct((M, S * n), bf16)
    return pl.pallas_call(
        body, out_shape=[o, o, o], grid=(S, M // tm), in_specs=[pl.BlockSpec((tm, K), lambda s, i: (i, 0)), w_spec, w_spec],
        out_specs=[o_spec, o_spec, o_spec], name="ffn_hidden", compiler_params=_params("parallel", "parallel"))(hn, w_gate, w_up)


def ffn_hidden_grad(d_ff, w_down, gt, up, tm=512):
    M, D = d_ff.shape
    F = w_down.shape[0]
    n = _pick(F // N_CHIPS, 2048)
    tm = _pick(M, tm)

    def body(a_ref, b_ref, gt_ref, up_ref, dg_ref, du_ref):
        dh = _dot(a_ref[...], b_ref[...], NT)
        g = gt_ref[...].astype(f32)
        sg = sigmoid(g)
        dg_ref[...] = (dh * up_ref[...].astype(f32) * (sg * (1.0 + g * (1.0 - sg)))).astype(bf16)
        du_ref[...] = (dh * (g * sg)).astype(bf16)

    t_spec = pl.BlockSpec((tm, n), lambda j, i: (i, j))
    o = jax.ShapeDtypeStruct((M, F), bf16)
    return pl.pallas_call(
        body, out_shape=[o, o], grid=(F // n, M // tm),
        in_specs=[pl.BlockSpec((tm, D), lambda j, i: (i, 0)), pl.BlockSpec((n, D), lambda j, i: (j, 0)), t_spec, t_spec],
        out_specs=[t_spec, t_spec], name="ffn_hidden_grad", compiler_params=_params("parallel", "parallel"))(d_ff, w_down, gt, up)


def rowwise(name, fn, L, rows, bcast, outs, reds=(), tr=256, dep=None):
    tr = min(tr, L)
    nt = L // tr
    n_rows, n_b, n_o = len(rows), len(bcast), len(outs)
    n_dep = 0 if dep is None else 1

    def body(*refs):
        i = pl.program_id(0)
        ins = [r[...] for r in refs[:n_rows + n_b]]
        res = fn(i, nt, *ins)
        o_refs = refs[n_rows + n_b + n_dep:]
        for k in range(n_o):
            o_refs[k][...] = res[k].astype(o_refs[k].dtype)
        if reds:
            @pl.when(i == 0)
            def _():
                for k in range(len(reds)):
                    o_refs[n_o + k][...] = jnp.zeros_like(o_refs[n_o + k])
            for k in range(len(reds)):
                o_refs[n_o + k][...] += res[n_o + k]

    def row_spec(width, cb, shift):
        if shift:
            return pl.BlockSpec((tr, width), lambda i: (jnp.minimum(i + shift, nt - 1), cb))
        return pl.BlockSpec((tr, width), lambda i: (i, cb))

    in_specs = [row_spec(w, cb, sh) for (_, w, cb, sh) in rows]
    in_specs += [pl.BlockSpec(b.shape, lambda i: (0, 0)) for b in bcast]
    out_specs = [pl.BlockSpec((tr, w), lambda i: (i, 0)) for (w, _) in outs]
    out_specs += [pl.BlockSpec((1, w), lambda i: (0, 0)) for w in reds]
    out_shape = [jax.ShapeDtypeStruct((L, w), dt) for (w, dt) in outs]
    out_shape += [jax.ShapeDtypeStruct((1, w), f32) for w in reds]
    in_specs, operands, _ = _with_dep(in_specs, [r[0] for r in rows] + list(bcast), dep)
    return pl.pallas_call(
        body, out_shape=out_shape, grid=(nt,), in_specs=in_specs, out_specs=out_specs, name=name,
        compiler_params=_params("arbitrary"))(*operands)


def full(a):
    return (a, a.shape[1], 0, 0)


def colsum(v):
    return jnp.sum(v, axis=0, keepdims=True)


def rms_fwd(x, g):
    r = lax.rsqrt(jnp.mean(x * x, axis=-1, keepdims=True) + RMS_EPS)
    return x * r * g


def rms_bwd(x, g, dy):
    r = lax.rsqrt(jnp.mean(x * x, axis=-1, keepdims=True) + RMS_EPS)
    xh = x * r
    dyg = dy * g
    dx = r * (dyg - xh * jnp.mean(dyg * xh, axis=-1, keepdims=True))
    return dx, colsum(dy * xh)


GELU_C = math.sqrt(2.0 / math.pi)


def gelu(y):
    return y * (0.5 * (1.0 + jnp.tanh(GELU_C * (y + 0.044715 * (y * y * y)))))


def gelu_grad(y):
    t = jnp.tanh(GELU_C * (y + 0.044715 * (y * y * y)))
    return 0.5 * (1.0 + t) + 0.5 * y * (1.0 - t * t) * (GELU_C * (1.0 + 3 * 0.044715 * (y * y)))


def sigmoid(v):
    return 1.0 / (1.0 + jnp.exp(-v))


def _lane(shape):
    return lax.broadcasted_iota(jnp.int32, shape, 1)


def _rot_chunk(t, cos, sin_signed):
    first = (_lane(t.shape) % HEAD_DIM) < (HEAD_DIM // 2)
    partner = jnp.where(first, pltpu.roll(t, LANES - HEAD_DIM // 2, 1), pltpu.roll(t, HEAD_DIM // 2, 1))
    return t * cos + partner * sin_signed


def _cos_sin(pos, inv_freq, inverse):
    ang = pos * inv_freq
    cos, sin = jnp.cos(ang), jnp.sin(ang)
    first = (_lane(ang.shape) % HEAD_DIM) < (HEAD_DIM // 2)
    sign = jnp.where(first, -1.0, 1.0) * (-1.0 if inverse else 1.0)
    return cos, sin * sign


def _dup_head(chunk, odd):
    low = _lane(chunk.shape) < HEAD_DIM
    x = jnp.where(low != odd, chunk, 0.0)
    return x + pltpu.roll(x, HEAD_DIM, 1)


def _chunks(v):
    return [v[:, LANES * c:LANES * (c + 1)] for c in range(v.shape[1] // LANES)]


def qkv_prep(proj, pos, inv_freq, d_attn, d_kv):
    L = proj.shape[0]
    d_ssm = proj.shape[1] - d_attn - 2 * d_kv
    half = d_ssm // 2
    scale = 1.0 / math.sqrt(HEAD_DIM)

    def fn(i, nt, q, k, v, u0, u1, p, invf):
        cos, sin = _cos_sin(p, invf, False)
        qr = jnp.concatenate([_rot_chunk(c, cos, sin) for c in _chunks(q)], axis=1) * scale
        kr = [_rot_chunk(c, cos, sin) for c in _chunks(k)]
        kk = jnp.concatenate([_dup_head(c, odd) for c in kr for odd in (False, True)], axis=1)
        vv = jnp.concatenate([_dup_head(c, odd) for c in _chunks(v) for odd in (False, True)], axis=1)
        return qr, kk, vv, jnp.concatenate([u0, u1], axis=1)

    u_cb = (d_attn + 2 * d_kv) // half
    return rowwise("qkv_prep", fn, L,
                   [(proj, d_attn, 0, 0), (proj, d_kv, d_attn // d_kv, 0), (proj, d_kv, d_attn // d_kv + 1, 0),
                    (proj, half, u_cb, 0), (proj, half, u_cb + 1, 0), full(pos)],
                   [inv_freq], [(d_attn, bf16), (2 * d_kv, bf16), (2 * d_kv, bf16), (d_ssm, bf16)])


def qkv_grad(dq, dkk_c, dkk_p, dvv_c, dvv_p, du, pos, inv_freq):
    L, d_attn = dq.shape
    d_kv = dkk_c.shape[1] // 2
    scale = 1.0 / math.sqrt(HEAD_DIM)

    def fold(cur, prev, i, nt):
        t = cur + jnp.where(i < nt - 1, prev, 0.0)
        out = []
        for c in range(t.shape[1] // (2 * LANES)):
            even, odd = t[:, 2 * c * LANES:(2 * c + 1) * LANES], t[:, (2 * c + 1) * LANES:(2 * c + 2) * LANES]
            even, odd = even + pltpu.roll(even, HEAD_DIM, 1), odd + pltpu.roll(odd, HEAD_DIM, 1)
            out.append(jnp.where(_lane(even.shape) < HEAD_DIM, even, odd))
        return out

    def fn(i, nt, dq_t, kc, kp, vc, vp, du_t, p, invf):
        cos, sin = _cos_sin(p, invf, True)
        dq_o = jnp.concatenate([_rot_chunk(c, cos, sin) for c in _chunks(dq_t)], axis=1) * scale
        dk_o = jnp.concatenate([_rot_chunk(c, cos, sin) for c in fold(kc, kp, i, nt)], axis=1)
        dv_o = jnp.concatenate(fold(vc, vp, i, nt), axis=1)
        return (jnp.concatenate([dq_o, dk_o, dv_o, du_t], axis=1),)

    return rowwise("qkv_grad", fn, L,
                   [full(dq), full(dkk_c), (dkk_p, 2 * d_kv, 0, 1), full(dvv_c), (dvv_p, 2 * d_kv, 0, 1), full(du), full(pos)],
                   [inv_freq], [(d_attn + 2 * d_kv + du.shape[1], bf16)], tr=ATTN_BLOCK)[0]


def _attn_specs(L):
    nb = L // ATTN_BLOCK
    B = ATTN_BLOCK
    q_spec = lambda width: pl.BlockSpec((B, width), lambda n: (n, 0))
    prev = lambda width: pl.BlockSpec((B, width), lambda n: (jnp.maximum(n - 1, 0), 0))
    return nb, q_spec, prev


def _attn_mask(n):
    B = ATTN_BLOCK
    row = lax.broadcasted_iota(jnp.int32, (B, 2 * B), 0)
    col = lax.broadcasted_iota(jnp.int32, (B, 2 * B), 1)
    return ((col < B) & (col > row) & (n > 0)) | ((col >= B) & (row >= col - B))


def _attn_probs(qm, kcat, sink, mask):
    s = jnp.where(mask, _dot(qm, kcat, NT), NEG)
    m = jnp.maximum(jnp.max(s, axis=1, keepdims=True), sink)
    p, ps = jnp.exp(s - m), jnp.exp(sink - m)
    inv = 1.0 / (jnp.sum(p, axis=1, keepdims=True) + ps)
    return p, inv, ps


def _attn_heads(q_ref, s_ref, h, q_per_kv):
    low = _lane((ATTN_BLOCK, LANES)) < HEAD_DIM
    heads = []
    for pr in range(h * q_per_kv // 2, (h + 1) * q_per_kv // 2):
        q2 = q_ref[:, LANES * pr:LANES * (pr + 1)]
        for odd in (False, True):
            mine = low != odd
            sink = jnp.max(s_ref[2 * pr + int(odd):2 * pr + int(odd) + 1, :], axis=1, keepdims=True)
            heads.append((pr, mine, jnp.where(mine, q2, jnp.zeros_like(q2)), sink))
    return low, heads


def _kv_block(prev_ref, cur_ref, h):
    return jnp.concatenate([prev_ref[:, LANES * h:LANES * (h + 1)], cur_ref[:, LANES * h:LANES * (h + 1)]], axis=0)


def attn_fwd(qr, kk, vv, sink_b):
    L, d_attn = qr.shape
    nb, q_spec, prev = _attn_specs(L)
    d_kk = kk.shape[1]
    n_kv = d_kk // LANES
    q_per_kv = d_attn // HEAD_DIM // n_kv

    def body(q_ref, kc_ref, kp_ref, vc_ref, vp_ref, s_ref, o_ref):
        mask = _attn_mask(pl.program_id(0))
        for h in range(n_kv):
            kcat, vcat = _kv_block(kp_ref, kc_ref, h), _kv_block(vp_ref, vc_ref, h)
            low, heads = _attn_heads(q_ref, s_ref, h, q_per_kv)
            probs = [_attn_probs(qm, kcat, sink, mask) for (_, _, qm, sink) in heads]
            outs = [_dot(p.astype(bf16), vcat) * inv for (p, inv, _) in probs]
            for i in range(0, len(heads), 2):
                pr = heads[i][0]
                o_ref[:, LANES * pr:LANES * (pr + 1)] = jnp.where(low, outs[i], outs[i + 1])

    return pl.pallas_call(
        body, out_shape=jax.ShapeDtypeStruct((L, d_attn), f32), grid=(nb,),
        in_specs=[q_spec(d_attn), q_spec(d_kk), prev(d_kk), q_spec(d_kk), prev(d_kk), pl.BlockSpec(sink_b.shape, lambda n: (0, 0))],
        out_specs=q_spec(d_attn), name="attn_fwd", compiler_params=_params("arbitrary"))(qr, kk, kk, vv, vv, sink_b)


def attn_bwd(qr, kk, vv, sink_b, attn, d_attn_out):
    L, d_attn = qr.shape
    nb, q_spec, prev = _attn_specs(L)
    d_kk = kk.shape[1]
    n_kv = d_kk // LANES
    q_per_kv = d_attn // HEAD_DIM // n_kv

    def body(q_ref, kc_ref, kp_ref, vc_ref, vp_ref, s_ref, o_ref, do_ref, dq_ref, dkc_ref, dkp_ref, dvc_ref, dvp_ref, ds_ref):
        n = pl.program_id(0)
        B = ATTN_BLOCK
        mask = _attn_mask(n)
        srow = lax.broadcasted_iota(jnp.int32, (SUBLANES, LANES), 0)

        @pl.when(n == 0)
        def _():
            ds_ref[...] = jnp.zeros_like(ds_ref)

        for h in range(n_kv):
            kcat, vcat = _kv_block(kp_ref, kc_ref, h), _kv_block(vp_ref, vc_ref, h)
            low, heads = _attn_heads(q_ref, s_ref, h, q_per_kv)
            probs = [_attn_probs(qm, kcat, sink, mask) for (_, _, qm, sink) in heads]
            dk = jnp.zeros((2 * B, LANES), f32)
            dv = dk
            dsink = jnp.zeros((SUBLANES, LANES), f32)
            dqs = []
            for i, ((pr, mine, qm, _), (p, inv, ps)) in enumerate(zip(heads, probs)):
                do2 = do_ref[:, LANES * pr:LANES * (pr + 1)]
                delta = jnp.sum(jnp.where(mine, do2 * o_ref[:, LANES * pr:LANES * (pr + 1)], 0.0), axis=1, keepdims=True)
                dob = jnp.where(mine, do2, 0.0).astype(bf16)
                p = p * inv
                ds = (p * (_dot(dob, vcat, NT) - delta)).astype(bf16)
                dqs.append(_dot(ds, kcat))
                dk = dk + _dot(ds, qm, TN)
                dv = dv + _dot(p.astype(bf16), dob, TN)
                dsink = dsink + jnp.where(srow == i, -jnp.sum(ps * inv * delta), 0.0)
            for i in range(0, len(heads), 2):
                pr = heads[i][0]
                dq_ref[:, LANES * pr:LANES * (pr + 1)] = jnp.where(low, dqs[i], dqs[i + 1])
            cols = slice(LANES * h, LANES * (h + 1))
            dkp_ref[:, cols] = dk[:B]
            dkc_ref[:, cols] = dk[B:]
            dvp_ref[:, cols] = dv[:B]
            dvc_ref[:, cols] = dv[B:]
            ds_ref[h] += dsink

    kv_shape = jax.ShapeDtypeStruct(kk.shape, f32)
    ds_shape = (n_kv, SUBLANES, LANES)
    return pl.pallas_call(
        body,
        out_shape=[jax.ShapeDtypeStruct((L, d_attn), f32), kv_shape, kv_shape, kv_shape, kv_shape, jax.ShapeDtypeStruct(ds_shape, f32)],
        grid=(nb,),
        in_specs=[q_spec(d_attn), q_spec(d_kk), prev(d_kk), q_spec(d_kk), prev(d_kk), pl.BlockSpec(sink_b.shape, lambda n: (0, 0)),
                  q_spec(d_attn), q_spec(d_attn)],
        out_specs=[q_spec(d_attn)] + [q_spec(d_kk)] * 4 + [pl.BlockSpec(ds_shape, lambda n: (0, 0, 0))],
        name="attn_bwd", compiler_params=_params("arbitrary"))(qr, kk, kk, vv, vv, sink_b, attn, d_attn_out)


SSM_T = 256
NQ = SUBLANES * SSM_STATE // LANES
NJ = SUBLANES


def _strided_put(ref, j, val):
    for q in range(NQ):
        ref.at[q][pl.ds(j, SSM_T, stride=NJ), :] = val[:, LANES * q:LANES * (q + 1)]


def _strided_get(ref, j):
    return jnp.concatenate([ref.at[q][pl.ds(j, SSM_T, stride=NJ), :] for q in range(NQ)], axis=1)


def _ssm_specs(L, rev):
    nt = L // SSM_T
    idx = (lambda i: nt - 1 - i) if rev else (lambda i: i)
    row = lambda w, cb=0: pl.BlockSpec((SSM_T, w), lambda i: (idx(i), cb))
    state = pl.BlockSpec((NQ, SSM_T * NJ, LANES), lambda i: (0, idx(i), 0))
    whole = lambda a: pl.BlockSpec(a.shape, lambda i: (0,) * a.ndim)
    return nt, row, state, whole


def ssm_fwd(u_bf, proj, u_cb, bd_re, bd_im, cd_re, cd_im, lam_re, lam_im, d_skip, dep=None):
    L, d_ssm = u_bf.shape
    nt, row, state, whole = _ssm_specs(L, False)
    half = d_ssm // 2
    gw = d_ssm // NJ
    n_dep = 0 if dep is None else 1

    def body(u_ref, u0_ref, u1_ref, bdr, bdi, cdr, cdi, lr_ref, li_ref, d_ref, *rest):
        y_ref, z_ref, sr_ref, si_ref, carry = rest[n_dep:]
        i = pl.program_id(0)

        @pl.when(i == 0)
        def _():
            carry[...] = jnp.zeros_like(carry)

        for j in range(NJ):
            uj = u_ref[:, gw * j:gw * (j + 1)]
            _strided_put(sr_ref, j, _dot(uj, bdr[j]))
            _strided_put(si_ref, j, _dot(uj, bdi[j]))
        lr = [lr_ref[q] for q in range(NQ)]
        li = [li_ref[q] for q in range(NQ)]

        def step(t, s):
            sr, si = s
            rows = pl.ds(pl.multiple_of(t * NJ, NJ), NJ)
            nr = tuple(lr[q] * sr[q] - li[q] * si[q] + sr_ref[q, rows, :] for q in range(NQ))
            ni = tuple(lr[q] * si[q] + li[q] * sr[q] + si_ref[q, rows, :] for q in range(NQ))
            for q in range(NQ):
                sr_ref[q, rows, :] = nr[q]
                si_ref[q, rows, :] = ni[q]
            return nr, ni

        init = (tuple(carry[0, q] for q in range(NQ)), tuple(carry[1, q] for q in range(NQ)))
        sr, si = lax.fori_loop(0, SSM_T, step, init, unroll=8)
        for q in range(NQ):
            carry[0, q] = sr[q]
            carry[1, q] = si[q]
        uf = jnp.concatenate([u0_ref[...], u1_ref[...]], axis=1)
        for j in range(NJ):
            cols = slice(gw * j, gw * (j + 1))
            yj = _dot(_strided_get(sr_ref, j).astype(bf16), cdr[j]) - _dot(_strided_get(si_ref, j).astype(bf16), cdi[j])
            yj = yj + d_ref[:, cols] * uf[:, cols]
            y_ref[:, cols] = yj
            z_ref[:, cols] = gelu(yj).astype(bf16)

    s_shape = jax.ShapeDtypeStruct((NQ, L * NJ, LANES), f32)
    consts = (bd_re, bd_im, cd_re, cd_im, lam_re, lam_im, d_skip)
    in_specs, operands, _ = _with_dep([row(d_ssm), row(half, u_cb), row(half, u_cb + 1)] + [whole(a) for a in consts],
                                      [u_bf, proj, proj, *consts], dep)
    return pl.pallas_call(
        body, out_shape=[jax.ShapeDtypeStruct((L, d_ssm), f32), jax.ShapeDtypeStruct((L, d_ssm), bf16), s_shape, s_shape], grid=(nt,),
        in_specs=in_specs, out_specs=[row(d_ssm), row(d_ssm), state, state],
        scratch_shapes=[pltpu.VMEM((2, NQ, NJ, LANES), f32)], name="ssm_fwd",
        compiler_params=_params("arbitrary"))(*operands)


def ssm_bwd(y, dz1, dz2, u_bf, proj, u_cb, s_re, s_im, bd_re, bd_im, cd_re, cd_im, lam_re, lam_im, d_skip, dep=None):
    L, d_ssm = y.shape
    nt, row, state, whole = _ssm_specs(L, True)
    half = d_ssm // 2
    gw = d_ssm // NJ
    n_dep = 0 if dep is None else 1

    def body(y_ref, dz1_ref, dz2_ref, u_ref, u0_ref, u1_ref, sr_ref, si_ref, bdr, bdi, cdr, cdi, lr_ref, li_ref, d_ref, *rest):
        du_ref, dbdr, dbdi, dcdr, dcdi, dlr, dli, dd_ref, gr_ref, gi_ref, carry = rest[n_dep:]
        i = pl.program_id(0)

        @pl.when(i == 0)
        def _():
            carry[...] = jnp.zeros_like(carry)
            for r in (dbdr, dbdi, dcdr, dcdi, dlr, dli, dd_ref):
                r[...] = jnp.zeros_like(r)

        dyf = (dz1_ref[...] + dz2_ref[...]) * gelu_grad(y_ref[...])
        dyb = dyf.astype(bf16)
        for j in range(NJ):
            dyj = dyb[:, gw * j:gw * (j + 1)]
            _strided_put(gr_ref, j, _dot(dyj, cdr[j], NT))
            _strided_put(gi_ref, j, -_dot(dyj, cdi[j], NT))
            dcdr[j] += _dot(_strided_get(sr_ref, j).astype(bf16), dyj, TN)
            dcdi[j] -= _dot(_strided_get(si_ref, j).astype(bf16), dyj, TN)
        lr = [lr_ref[q] for q in range(NQ)]
        li = [li_ref[q] for q in range(NQ)]

        def step(k, c):
            gr, gi, ar, ai = c
            rows = pl.ds(pl.multiple_of((SSM_T - 1 - k) * NJ, NJ), NJ)
            s_r = [sr_ref[q, rows, :] for q in range(NQ)]
            s_i = [si_ref[q, rows, :] for q in range(NQ)]
            ar = tuple(ar[q] + gr[q] * s_r[q] + gi[q] * s_i[q] for q in range(NQ))
            ai = tuple(ai[q] + gi[q] * s_r[q] - gr[q] * s_i[q] for q in range(NQ))
            nr = tuple(gr_ref[q, rows, :] + lr[q] * gr[q] + li[q] * gi[q] for q in range(NQ))
            ni = tuple(gi_ref[q, rows, :] + lr[q] * gi[q] - li[q] * gr[q] for q in range(NQ))
            for q in range(NQ):
                gr_ref[q, rows, :] = nr[q]
                gi_ref[q, rows, :] = ni[q]
            return nr, ni, ar, ai

        zero = tuple(jnp.zeros((NJ, LANES), f32) for _ in range(NQ))
        init = (tuple(carry[0, q] for q in range(NQ)), tuple(carry[1, q] for q in range(NQ)), zero, zero)
        gr, gi, ar, ai = lax.fori_loop(0, SSM_T, step, init, unroll=8)
        for q in range(NQ):
            carry[0, q] = gr[q]
            carry[1, q] = gi[q]
            dlr[q] += ar[q]
            dli[q] += ai[q]
        uf = jnp.concatenate([u0_ref[...], u1_ref[...]], axis=1)
        dd_ref[...] += colsum(dyf * uf)
        for j in range(NJ):
            cols = slice(gw * j, gw * (j + 1))
            gjr, gji = _strided_get(gr_ref, j).astype(bf16), _strided_get(gi_ref, j).astype(bf16)
            du_ref[:, cols] = _dot(gjr, bdr[j], NT) + _dot(gji, bdi[j], NT) + d_ref[:, cols] * dyf[:, cols]
            uj = u_ref[:, cols]
            dbdr[j] += _dot(uj, gjr, TN)
            dbdi[j] += _dot(uj, gji, TN)

    consts = (bd_re, bd_im, cd_re, cd_im, lam_re, lam_im, d_skip)
    acc = lambda a: jax.ShapeDtypeStruct(a.shape, f32)
    outs = [jax.ShapeDtypeStruct((L, d_ssm), f32), acc(bd_re), acc(bd_im), acc(cd_re), acc(cd_im), acc(lam_re), acc(lam_im), acc(d_skip)]
    in_specs, operands, _ = _with_dep(
        [row(d_ssm)] * 4 + [row(half, u_cb), row(half, u_cb + 1), state, state] + [whole(a) for a in consts],
        [y, dz1, dz2, u_bf, proj, proj, s_re, s_im, *consts], dep)
    return pl.pallas_call(
        body, out_shape=outs, grid=(nt,),
        in_specs=in_specs, out_specs=[row(d_ssm)] + [whole(a) for a in consts],
        scratch_shapes=[pltpu.VMEM((NQ, SSM_T * NJ, LANES), f32), pltpu.VMEM((NQ, SSM_T * NJ, LANES), f32),
                        pltpu.VMEM((2, NQ, NJ, LANES), f32)],
        name="ssm_bwd", compiler_params=_params("arbitrary"))(*operands)


def _cmul(ar, ai, br, bi):
    return ar * br - ai * bi, ar * bi + ai * br


def _disc(ar, ai, logdt):
    dt = jnp.exp(logdt)
    mag = jnp.exp(ar * dt)
    lr, li = mag * jnp.cos(ai * dt), mag * jnp.sin(ai * dt)
    den = ar * ar + ai * ai
    nr, ni = lr - 1.0, li
    fr, fi = (nr * ar + ni * ai) / den, (ni * ar - nr * ai) / den
    return dt, lr, li, den, fr, fi


def ssm_params(a_re, a_im, logdt_b, bt_re, bt_im, spread):
    def body(ar_ref, ai_ref, ld_ref, br_ref, bi_ref, sp_ref, lr_ref, li_ref, or_ref, oi_ref):
        _, lr, li, _, fr, fi = _disc(ar_ref[...], ai_ref[...], ld_ref[...])
        lr_ref[...] = lr
        li_ref[...] = li
        fre = jnp.dot(sp_ref[...], fr, precision=HIGHEST, preferred_element_type=f32)
        fie = jnp.dot(sp_ref[...], fi, precision=HIGHEST, preferred_element_type=f32)
        o_r, o_i = _cmul(fre, fie, br_ref[...], bi_ref[...])
        or_ref[...] = o_r
        oi_ref[...] = o_i

    g = jax.ShapeDtypeStruct(a_re.shape, f32)
    b = jax.ShapeDtypeStruct(bt_re.shape, f32)
    return pl.pallas_call(body, out_shape=[g, g, b, b], name="ssm_params",
                          compiler_params=_params())(a_re, a_im, logdt_b, bt_re, bt_im, spread)


def ssm_params_grad(a_re, a_im, logdt_b, bt_re, bt_im, spread, gather, dlam_re, dlam_im, dbt_re, dbt_im):
    def body(ar_ref, ai_ref, ld_ref, br_ref, bi_ref, sp_ref, ga_ref, glr_ref, gli_ref, gbr_ref, gbi_ref,
             dar_ref, dai_ref, dld_ref, dbr_ref, dbi_ref):
        ar, ai = ar_ref[...], ai_ref[...]
        dt, lr, li, den, fr, fi = _disc(ar, ai, ld_ref[...])
        hdot = functools.partial(jnp.dot, precision=HIGHEST, preferred_element_type=f32)
        fre, fie = hdot(sp_ref[...], fr), hdot(sp_ref[...], fi)
        gbr, gbi, br, bi = gbr_ref[...], gbi_ref[...], br_ref[...], bi_ref[...]
        dbr_ref[...], dbi_ref[...] = _cmul(fre, -fie, gbr, gbi)
        t_r, t_i = _cmul(br, -bi, gbr, gbi)
        gfr, gfi = hdot(ga_ref[...], t_r), hdot(ga_ref[...], t_i)
        iwr, iwi = ar / den, -ai / den
        x_r, x_i = _cmul(iwr, -iwi, gfr, gfi)
        glr, gli = glr_ref[...] + x_r, gli_ref[...] + x_i
        q_r, q_i = _cmul(fr, fi, iwr, iwi)
        gwr, gwi = _cmul(-q_r, q_i, gfr, gfi)
        y_r, y_i = _cmul(dt * lr, -dt * li, glr, gli)
        dar_ref[...] = gwr + y_r
        dai_ref[...] = gwi + y_i
        wl_r, wl_i = _cmul(ar, ai, lr, li)
        z_r, _ = _cmul(wl_r, -wl_i, glr, gli)
        dld_ref[...] = jnp.sum(z_r * dt, axis=1, keepdims=True)

    g = jax.ShapeDtypeStruct(a_re.shape, f32)
    b = jax.ShapeDtypeStruct(bt_re.shape, f32)
    return pl.pallas_call(body, out_shape=[g, g, jax.ShapeDtypeStruct((a_re.shape[0], 1), f32), b, b], name="ssm_params_grad",
                          compiler_params=_params())(a_re, a_im, logdt_b, bt_re, bt_im, spread, gather, dlam_re, dlam_im, dbt_re, dbt_im)


def _block_diag(t, rows, cols):
    G = t.shape[0]
    t = t.reshape(G // NJ, NJ, rows, cols)
    eye = jnp.eye(NJ, dtype=t.dtype)
    return jnp.einsum('jgrc,gh->jgrhc', t, eye).reshape(G // NJ, NJ * rows, NJ * cols)


def _block_diag_take(m, rows, cols):
    J = m.shape[0]
    m = m.reshape(J, NJ, rows, NJ, cols)
    idx = jnp.arange(NJ)
    return m[:, idx, :, idx, :].transpose(1, 0, 2, 3).reshape(J * NJ, rows, cols)


def _state_layout(t):
    return t.reshape(NJ, NQ, LANES).transpose(1, 0, 2)


def _state_layout_inv(t, G, N):
    return t.transpose(1, 0, 2).reshape(G, N)


def _tiles2d(shape, budget_rows=128):
    rows, cols = shape
    tr = rows
    if rows > budget_rows:
        tr = budget_rows
        while rows % tr:
            tr -= SUBLANES
    return tr, cols


def _adam_update(w, g, m, v):
    c1 = 1.0 - ADAM_B1 ** ADAM_STEP
    c2 = 1.0 - ADAM_B2 ** ADAM_STEP
    nm = ADAM_B1 * m + (1.0 - ADAM_B1) * g
    nv = ADAM_B2 * v + (1.0 - ADAM_B2) * (g * g)
    delta = -ADAM_LR * ((nm / c1) / (jnp.sqrt(nv / c2) + ADAM_EPS) + ADAM_WD * w)
    return delta, nm, nv


def adamw_many(name, ws, gs, ms, vs):
    n = len(ws)

    def body(*refs):
        w, g, m, v = (refs[k * n:(k + 1) * n] for k in range(4))
        d, nm, nv = (refs[(4 + k) * n:(5 + k) * n] for k in range(3))
        for i in range(n):
            d[i][...], nm[i][...], nv[i][...] = _adam_update(w[i][...], g[i][...], m[i][...], v[i][...])

    o = [jax.ShapeDtypeStruct(a.shape, f32) for a in ws]
    outs = pl.pallas_call(body, out_shape=o * 3, name=name, compiler_params=_params())(*ws, *gs, *ms, *vs)
    return outs[:n], outs[n:2 * n], outs[2 * n:]


def adamw_halves(name, w, own, got, m, v, c_arr):
    h, cols = own.shape
    tr, _ = _tiles2d((h, cols), 128 if cols > 1024 else 256)
    nh = h // tr

    def body(c_ref, w_ref, own_ref, got_ref, m_ref, v_ref, g_ref, d_ref, nm_ref, nv_ref):
        mine = (pl.program_id(0) // nh) == c_ref[0]
        g = jnp.where(mine, own_ref[...], got_ref[...])
        g_ref[...] = g
        d_ref[...], nm_ref[...], nv_ref[...] = _adam_update(w_ref[...], g, m_ref[...], v_ref[...])

    spec = pl.BlockSpec((tr, cols), lambda i, c: (i, 0))
    own_spec = pl.BlockSpec((tr, cols), lambda i, c: (jnp.where(i // nh == c[0], i % nh, 0), 0))
    got_spec = pl.BlockSpec((tr, cols), lambda i, c: (jnp.where(i // nh == c[0], 0, i % nh), 0))
    o = jax.ShapeDtypeStruct(w.shape, f32)
    grid_spec = pltpu.PrefetchScalarGridSpec(num_scalar_prefetch=1, grid=(2 * nh,),
                                             in_specs=[spec, own_spec, got_spec, spec, spec], out_specs=[spec] * 4)
    return pl.pallas_call(body, out_shape=[o, o, o, o], grid_spec=grid_spec, name=name,
                          compiler_params=_params("arbitrary"))(c_arr, w, own, got, m, v)


def pair_sum(name, g, got, c_arr):
    S, h, cols = got.shape
    tr, _ = _tiles2d((h, cols), 1024)
    nh = h // tr

    def body(c_ref, g_ref, o_ref, out_ref):
        out_ref[...] = (g_ref[...].astype(f32) + o_ref[...].astype(f32)).astype(out_ref.dtype)

    spec = pl.BlockSpec((None, tr, cols), lambda s, i, c: (s, i, 0))
    grid_spec = pltpu.PrefetchScalarGridSpec(
        num_scalar_prefetch=1, grid=(S, nh),
        in_specs=[pl.BlockSpec((None, tr, cols), lambda s, i, c: (s, c[0] * nh + i, 0)), spec], out_specs=spec)
    return pl.pallas_call(body, out_shape=jax.ShapeDtypeStruct(got.shape, g.dtype), grid_spec=grid_spec, name=name,
                          compiler_params=_params("parallel", "parallel"))(c_arr, g, got)


def chip_sum(name, pair, landed, mine_arr, dep=None):
    n_in, h, cols = landed.shape
    tr, _ = _tiles2d((h, cols), 512)

    def body(s_ref, p_ref, l_ref, *rest):
        acc = p_ref[...].astype(f32)
        for k in range(n_in):
            acc = acc + l_ref[k].astype(f32)
        rest[-1][...] = acc

    in_specs, operands, _ = _with_dep(
        [pl.BlockSpec((None, tr, cols), lambda i, s: (s[0], i, 0)), pl.BlockSpec((n_in, tr, cols), lambda i, s: (0, i, 0))],
        [pair, landed], dep)
    grid_spec = pltpu.PrefetchScalarGridSpec(num_scalar_prefetch=1, grid=(h // tr,), in_specs=in_specs,
                                             out_specs=pl.BlockSpec((tr, cols), lambda i, s: (i, 0)))
    return pl.pallas_call(body, out_shape=jax.ShapeDtypeStruct((h, cols), f32), grid_spec=grid_spec, name=name,
                          compiler_params=_params("parallel"))(mine_arr, *operands)


def into_slot(name, w, slot_arr, n_slots, dtype, dep=None):
    tr, cols = _tiles2d(w.shape, 256)

    def body(s_ref, w_ref, *rest):
        rest[-1][...] = w_ref[...].astype(dtype)

    in_specs, operands, _ = _with_dep([pl.BlockSpec((tr, cols), lambda i, s: (i, 0))], [w], dep)
    grid_spec = pltpu.PrefetchScalarGridSpec(num_scalar_prefetch=1, grid=(w.shape[0] // tr,), in_specs=in_specs,
                                             out_specs=pl.BlockSpec((None, tr, cols), lambda i, s: (s[0], i, 0)))
    return pl.pallas_call(body, out_shape=jax.ShapeDtypeStruct((n_slots,) + w.shape, dtype), grid_spec=grid_spec, name=name,
                          compiler_params=_params("parallel"))(slot_arr, *operands)


def sum_slots(name, t):
    S, rows, cols = t.shape
    tr, _ = _tiles2d((rows, cols), 256)

    def body(t_ref, o_ref):
        acc = t_ref[0]
        for s in range(1, S):
            acc = acc + t_ref[s]
        o_ref[...] = acc

    return pl.pallas_call(body, out_shape=jax.ShapeDtypeStruct((rows, cols), f32), grid=(rows // tr,),
                          in_specs=[pl.BlockSpec((S, tr, cols), lambda i: (0, i, 0))], out_specs=pl.BlockSpec((tr, cols), lambda i: (i, 0)),
                          name=name, compiler_params=_params("parallel"))(t)


def _place():
    x, y, c = lax.axis_index("x"), lax.axis_index("y"), lax.axis_index("c")
    return x, y, c


def _other_chips(x, y):
    return [(1 - x, y, 2 * (1 - x) + y), (x, 1 - y, 2 * x + 1 - y), (1 - x, 1 - y, 2 * (1 - x) + 1 - y)]


SEM = pl.BlockSpec(memory_space=pltpu.SEMAPHORE)
VM = pl.BlockSpec(memory_space=pltpu.VMEM)
DATAFLOW = pltpu.SideEffectType.DATAFLOW_SIDE_EFFECTING
TOKEN = jax.ShapeDtypeStruct((SUBLANES, LANES), f32)


def _gather_copy(buf, w, k, chip, c, mine, send, recv):
    px, py, _ = chip
    h = buf.shape[1] // 2
    half = buf.at[mine, pl.ds(c * h, h), :]
    return pltpu.make_async_remote_copy(src_ref=half, dst_ref=half, send_sem=send.at[3 * w + k], recv_sem=recv.at[3 * w + k],
                                        device_id=(px, py, c), device_id_type=MESH)


def _gather_landing(buf, w, k, chip, c, send, recv):
    px, py, s = chip
    h = buf.shape[1] // 2
    landed = buf.at[s, pl.ds(c * h, h), :]
    return pltpu.make_async_remote_copy(src_ref=landed, dst_ref=landed, send_sem=send.at[3 * w + k], recv_sem=recv.at[3 * w + k],
                                        device_id=(px, py, c), device_id_type=MESH)


def gather_start(name, bufs, groups, after, neighbours_only=()):
    nw, ng = len(bufs), len(groups)

    def body(*refs):
        outs = refs[nw + 1:]
        sems, dst = outs[:2 * ng], outs[2 * ng:2 * ng + nw]
        token = outs[2 * ng + nw]
        x, y, c = _place()
        mine = 2 * x + y
        for g, members in enumerate(groups):
            for i, w in enumerate(members):
                for k, chip in enumerate(_other_chips(x, y)[:2 if w in neighbours_only else 3]):
                    _gather_copy(dst[w], i, k, chip, c, mine, sems[2 * g], sems[2 * g + 1]).start()
        token[...] = jnp.zeros_like(token)

    sem_shapes = []
    for members in groups:
        sem_shapes += [pltpu.SemaphoreType.DMA((3 * len(members),))] * 2
    outs = pl.pallas_call(
        body, out_shape=sem_shapes + [jax.ShapeDtypeStruct(b.shape, b.dtype) for b in bufs] + [TOKEN],
        in_specs=[ANY] * (nw + 1), out_specs=[SEM] * (2 * ng) + [ANY] * nw + [VM],
        input_output_aliases={w: 2 * ng + w for w in range(nw)}, name=name,
        compiler_params=pltpu.CompilerParams(has_side_effects=DATAFLOW))(*bufs, after)
    return [(outs[2 * g], outs[2 * g + 1]) for g in range(ng)], list(outs[2 * ng:2 * ng + nw]), outs[2 * ng + nw]


def gather_wait(name, bufs, send, recv, after):
    nw = len(bufs)

    def body(*refs):
        src = refs[:nw]
        send_ref, recv_ref = refs[nw], refs[nw + 1]
        x, y, c = _place()
        mine = 2 * x + y
        for w in range(nw):
            for k, chip in enumerate(_other_chips(x, y)):
                _gather_copy(src[w], w, k, chip, c, mine, send_ref, recv_ref).wait_send()
                _gather_landing(src[w], w, k, chip, c, send_ref, recv_ref).wait_recv()

    return pl.pallas_call(
        body, out_shape=[jax.ShapeDtypeStruct(b.shape, b.dtype) for b in bufs],
        in_specs=[ANY] * nw + [SEM, SEM, ANY], out_specs=[ANY] * nw,
        input_output_aliases={w: w for w in range(nw)}, name=name,
        compiler_params=pltpu.CompilerParams(has_side_effects=DATAFLOW))(*bufs, send, recv, after)


def _relay_copy(buf, w, j, x, y, c, send, recv, landing):
    chips = _other_chips(x, y)
    px, py, _ = chips[j]
    h = buf.shape[1] // 2
    q = h // 2
    s = chips[2][2] if landing else chips[1 - j][2]
    part = buf.at[s, pl.ds(c * h + j * q, q), :]
    return pltpu.make_async_remote_copy(src_ref=part, dst_ref=part, send_sem=send.at[2 * w + j], recv_sem=recv.at[2 * w + j],
                                        device_id=(px, py, c), device_id_type=MESH)


def _early_pass(buf, nw, w, k, x, y, c, send, recv, landing):
    s = _other_chips(x, y)[k][2]
    h = buf.shape[1] // 2
    part = buf.at[s, pl.ds(((1 - c) if landing else c) * h, h), :]
    i = 2 * nw + 2 * w + k
    return pltpu.make_async_remote_copy(src_ref=part, dst_ref=part, send_sem=send.at[i], recv_sem=recv.at[i],
                                        device_id=(x, y, 1 - c), device_id_type=MESH)


def gather_relay(name, bufs, sems, more, after):
    nw, nm = len(bufs), len(more)
    ns = 4 if nm else 2

    def body(*refs):
        ins, outs = refs[:nw + nm + 2 * nw + 1], refs[nw + nm + 2 * nw + 1:]
        src, d_sems = ins[:nw], ins[nw + nm:nw + nm + 2 * nw]
        r_send, r_recv = outs[:2]
        m_send, m_recv = outs[2:ns] if nm else (None, None)
        dst, mdst, token = outs[ns:ns + nw], outs[ns + nw:ns + nw + nm], outs[ns + nw + nm]
        x, y, c = _place()
        mine = 2 * x + y
        chips = _other_chips(x, y)
        for w in range(nw):
            for k in range(2):
                _gather_copy(src[w], 0, k, chips[k], c, mine, d_sems[2 * w], d_sems[2 * w + 1]).wait_send()
                _gather_landing(src[w], 0, k, chips[k], c, d_sems[2 * w], d_sems[2 * w + 1]).wait_recv()
            for j in range(2):
                _relay_copy(dst[w], w, j, x, y, c, r_send, r_recv, False).start()
            for k in range(2):
                _early_pass(dst[w], nw, w, k, x, y, c, r_send, r_recv, False).start()
        for w in range(nm):
            for k, chip in enumerate(chips):
                _gather_copy(mdst[w], w, k, chip, c, mine, m_send, m_recv).start()
        token[...] = jnp.zeros_like(token)

    sem_shapes = [pltpu.SemaphoreType.DMA((4 * nw,))] * 2 + [pltpu.SemaphoreType.DMA((3 * nm,))] * (ns - 2)
    flat_sems = [s for pair in sems for s in pair]
    outs = pl.pallas_call(
        body, out_shape=sem_shapes + [jax.ShapeDtypeStruct(b.shape, b.dtype) for b in list(bufs) + list(more)] + [TOKEN],
        in_specs=[ANY] * (nw + nm) + [SEM] * (2 * nw) + [ANY], out_specs=[SEM] * ns + [ANY] * (nw + nm) + [VM],
        input_output_aliases={i: ns + i for i in range(nw + nm)}, name=name,
        compiler_params=pltpu.CompilerParams(has_side_effects=DATAFLOW))(*bufs, *more, *flat_sems, after)
    m_sems = (outs[2], outs[3]) if nm else None
    return outs[0], outs[1], m_sems, list(outs[ns:ns + nw]), list(outs[ns + nw:ns + nw + nm]), outs[ns + nw + nm]


def gather_wait_relay(name, bufs, r_send, r_recv, after):
    nw = len(bufs)

    def body(*refs):
        src = refs[:nw]
        send_ref, recv_ref = refs[nw], refs[nw + 1]
        x, y, c = _place()
        for w in range(nw):
            for j in range(2):
                _relay_copy(src[w], w, j, x, y, c, send_ref, recv_ref, False).wait_send()
                _relay_copy(src[w], w, j, x, y, c, send_ref, recv_ref, True).wait_recv()
                _early_pass(src[w], nw, w, j, x, y, c, send_ref, recv_ref, False).wait_send()
                _early_pass(src[w], nw, w, j, x, y, c, send_ref, recv_ref, True).wait_recv()

    return pl.pallas_call(
        body, out_shape=[jax.ShapeDtypeStruct(b.shape, b.dtype) for b in bufs],
        in_specs=[ANY] * nw + [SEM, SEM, ANY], out_specs=[ANY] * nw,
        input_output_aliases={w: w for w in range(nw)}, name=name,
        compiler_params=pltpu.CompilerParams(has_side_effects=DATAFLOW))(*bufs, r_send, r_recv, after)


def gather_forward(name, bufs, which=(0, 1, 2)):
    nw = len(bufs)

    def body(*refs):
        dst = refs[nw:2 * nw]
        send, recv = refs[2 * nw:]
        x, y, c = _place()
        sib = (x, y, 1 - c)
        cps = []
        for w in range(nw):
            h = dst[w].shape[1] // 2
            for k in which:
                s = _other_chips(x, y)[k][2]
                landed = dst[w].at[s, pl.ds(c * h, h), :]
                cp = pltpu.make_async_remote_copy(src_ref=landed, dst_ref=landed, send_sem=send.at[w, k], recv_sem=recv.at[w, k],
                                                  device_id=sib, device_id_type=MESH)
                cp.start()
                cps.append(cp)
        for w in range(nw):
            h = dst[w].shape[1] // 2
            for k in which:
                s = _other_chips(x, y)[k][2]
                other = dst[w].at[s, pl.ds((1 - c) * h, h), :]
                pltpu.make_async_remote_copy(src_ref=other, dst_ref=other, send_sem=send.at[w, k], recv_sem=recv.at[w, k],
                                             device_id=sib, device_id_type=MESH).wait_recv()
        for cp in cps:
            cp.wait_send()

    sem = pltpu.SemaphoreType.DMA((nw, 3))
    return pl.pallas_call(
        body, out_shape=[jax.ShapeDtypeStruct(b.shape, b.dtype) for b in bufs],
        in_specs=[ANY] * nw, out_specs=[ANY] * nw, input_output_aliases={w: w for w in range(nw)},
        scratch_shapes=[sem, sem], name=name, compiler_params=pltpu.CompilerParams(has_side_effects=True))(*bufs)


def _scatter_copy(src, dst, w, k, chip, c, send, recv):
    px, py, s = chip
    return pltpu.make_async_remote_copy(src_ref=src.at[s], dst_ref=dst.at[k], send_sem=send.at[3 * w + k], recv_sem=recv.at[3 * w + k],
                                        device_id=(px, py, c), device_id_type=MESH)


def scatter_start(name, parts):
    nw = len(parts)
    lands = [pltpu.with_memory_space_constraint(lax.empty((N_CHIPS - 1,) + p.shape[1:], p.dtype), pltpu.HBM) for p in parts]

    def body(*refs):
        outs = refs[2 * nw:]
        send, recv = outs[0], outs[1]
        src, dst, token = outs[2:2 + nw], outs[2 + nw:2 + 2 * nw], outs[2 + 2 * nw]
        x, y, c = _place()
        for w in range(nw):
            for k, chip in enumerate(_other_chips(x, y)):
                _scatter_copy(src[w], dst[w], w, k, chip, c, send, recv).start()
        token[...] = jnp.zeros_like(token)

    sem = pltpu.SemaphoreType.DMA((3 * nw,))
    outs = pl.pallas_call(
        body, out_shape=[sem, sem] + [jax.ShapeDtypeStruct(p.shape, p.dtype) for p in parts]
        + [jax.ShapeDtypeStruct(l.shape, l.dtype) for l in lands] + [TOKEN],
        in_specs=[ANY] * (2 * nw), out_specs=[SEM, SEM] + [ANY] * (2 * nw) + [VM],
        input_output_aliases={i: 2 + i for i in range(2 * nw)}, name=name,
        compiler_params=pltpu.CompilerParams(has_side_effects=DATAFLOW))(*parts, *lands)
    return outs[0], outs[1], list(outs[2:2 + nw]), list(outs[2 + nw:2 + 2 * nw]), outs[2 + 2 * nw]


def scatter_wait(name, parts, lands, send, recv, after):
    nw = len(parts)

    def body(*refs):
        src, dst = refs[:nw], refs[nw:2 * nw]
        send_ref, recv_ref = refs[2 * nw], refs[2 * nw + 1]
        x, y, c = _place()
        for w in range(nw):
            for k, chip in enumerate(_other_chips(x, y)):
                cp = _scatter_copy(src[w], dst[w], w, k, chip, c, send_ref, recv_ref)
                cp.wait_send()
                cp.wait_recv()

    outs = pl.pallas_call(
        body, out_shape=[jax.ShapeDtypeStruct(a.shape, a.dtype) for a in list(parts) + list(lands)],
        in_specs=[ANY] * (2 * nw) + [SEM, SEM, ANY], out_specs=[ANY] * (2 * nw),
        input_output_aliases={i: i for i in range(2 * nw)}, name=name,
        compiler_params=pltpu.CompilerParams(has_side_effects=DATAFLOW))(*parts, *lands, send, recv, after)
    return list(outs[:nw]), list(outs[nw:])


def _sibling_copy(src, dst, w, c, half_rows, send, recv, sib):
    if half_rows:
        h = src.shape[1] // 2
        src = src.at[:, pl.ds((1 - c) * h, h), :]
    return pltpu.make_async_remote_copy(src_ref=src, dst_ref=dst, send_sem=send.at[w], recv_sem=recv.at[w],
                                        device_id=sib, device_id_type=MESH)


def _landing(shape, dtype):
    return pltpu.with_memory_space_constraint(lax.empty(shape, dtype), pltpu.HBM)


def sibling_start(name, srcs, half_rows):
    nw = len(srcs)
    lands = [_landing((s.shape[0], s.shape[1] // 2, s.shape[2]) if half_rows else s.shape, s.dtype) for s in srcs]

    def body(*refs):
        outs = refs[2 * nw:]
        send, recv = outs[0], outs[1]
        src, dst, token = outs[2:2 + nw], outs[2 + nw:2 + 2 * nw], outs[2 + 2 * nw]
        x, y, c = _place()
        for w in range(nw):
            _sibling_copy(src[w], dst[w], w, c, half_rows, send, recv, (x, y, 1 - c)).start()
        token[...] = jnp.zeros_like(token)

    sem = pltpu.SemaphoreType.DMA((nw,))
    outs = pl.pallas_call(
        body, out_shape=[sem, sem] + [jax.ShapeDtypeStruct(a.shape, a.dtype) for a in list(srcs) + lands] + [TOKEN],
        in_specs=[ANY] * (2 * nw), out_specs=[SEM, SEM] + [ANY] * (2 * nw) + [VM],
        input_output_aliases={i: 2 + i for i in range(2 * nw)}, name=name,
        compiler_params=pltpu.CompilerParams(has_side_effects=DATAFLOW))(*srcs, *lands)
    return outs[0], outs[1], list(outs[2:2 + nw]), list(outs[2 + nw:2 + 2 * nw]), outs[2 + 2 * nw]


def sibling_wait(name, srcs, lands, send, recv, half_rows, after):
    nw = len(srcs)

    def body(*refs):
        src, dst = refs[:nw], refs[nw:2 * nw]
        send_ref, recv_ref = refs[2 * nw], refs[2 * nw + 1]
        x, y, c = _place()
        for w in range(nw):
            cp = _sibling_copy(src[w], dst[w], w, c, half_rows, send_ref, recv_ref, (x, y, 1 - c))
            cp.wait_send()
            cp.wait_recv()

    outs = pl.pallas_call(
        body, out_shape=[jax.ShapeDtypeStruct(a.shape, a.dtype) for a in list(srcs) + list(lands)],
        in_specs=[ANY] * (2 * nw) + [SEM, SEM, ANY], out_specs=[ANY] * (2 * nw),
        input_output_aliases={i: i for i in range(2 * nw)}, name=name,
        compiler_params=pltpu.CompilerParams(has_side_effects=DATAFLOW))(*srcs, *lands, send, recv, after)
    return list(outs[:nw]), list(outs[nw:])


def _peer(x, y, c, r):
    return (1 - x if r & 4 else x, 1 - y if r & 2 else y, 1 - c if r & 1 else c)


def _everyone_copy(buf, r, x, y, c, send, recv, landing):
    px, py, pc = _peer(x, y, c, r)
    slot = buf.at[4 * px + 2 * py + pc] if landing else buf.at[4 * x + 2 * y + c]
    return pltpu.make_async_remote_copy(src_ref=slot, dst_ref=slot, send_sem=send.at[r - 1], recv_sem=recv.at[r - 1],
                                        device_id=(px, py, pc), device_id_type=MESH)


def everyone_start(name, buf):
    def body(buf_in, send, recv, buf_ref, token):
        x, y, c = _place()
        for r in range(1, N_DEV):
            _everyone_copy(buf_ref, r, x, y, c, send, recv, False).start()
        token[...] = jnp.zeros_like(token)

    sem = pltpu.SemaphoreType.DMA((N_DEV - 1,))
    return pl.pallas_call(
        body, out_shape=[sem, sem, jax.ShapeDtypeStruct(buf.shape, buf.dtype), TOKEN],
        in_specs=[ANY], out_specs=[SEM, SEM, ANY, VM], input_output_aliases={0: 2}, name=name,
        compiler_params=pltpu.CompilerParams(has_side_effects=DATAFLOW))(buf)


def everyone_wait(name, buf, send, recv, after):
    def body(buf_ref, send_ref, recv_ref, after_ref, out_ref):
        x, y, c = _place()
        for r in range(1, N_DEV):
            _everyone_copy(buf_ref, r, x, y, c, send_ref, recv_ref, False).wait_send()
            _everyone_copy(buf_ref, r, x, y, c, send_ref, recv_ref, True).wait_recv()

    return pl.pallas_call(
        body, out_shape=jax.ShapeDtypeStruct(buf.shape, buf.dtype), in_specs=[ANY, SEM, SEM, ANY], out_specs=ANY,
        input_output_aliases={0: 0}, name=name,
        compiler_params=pltpu.CompilerParams(has_side_effects=DATAFLOW))(buf, send, recv, after)


def local_step(x, pos, tgt, small, d_in, get_w, put_g, first_dep=None, tick=lambda name, after: None):
    L, D = x.shape
    d_kv = N_KV_HEADS * HEAD_DIM
    d_ssm = small["d_skip"].shape[1]
    d_attn = d_in - 2 * d_kv - d_ssm
    big = {}
    G = d_ssm // SSM_GROUP
    N, P = SSM_STATE, SSM_GROUP
    gbf = bf16

    half_dim = HEAD_DIM // 2
    inv_freq = ROPE_THETA ** (-jnp.arange(half_dim, dtype=f32) / half_dim)
    inv_freq = jnp.tile(inv_freq, LANES // half_dim).reshape(1, LANES)
    sink_b = jnp.broadcast_to(small["sinks"].reshape(-1, 1), (small["sinks"].size, LANES))

    spread = jnp.repeat(jnp.eye(G, dtype=f32), P, axis=0)
    logdt_b = jnp.broadcast_to(small["log_dt"].reshape(G, 1), (G, N))
    bt_re = small["b_re"].reshape(G, N, P).transpose(0, 2, 1).reshape(G * P, N)
    bt_im = small["b_im"].reshape(G, N, P).transpose(0, 2, 1).reshape(G * P, N)
    a_re, a_im = small["a_re"].reshape(G, N), small["a_im"].reshape(G, N)
    lam_re, lam_im, bbt_re, bbt_im = ssm_params(a_re, a_im, logdt_b, bt_re, bt_im, spread)
    bd_re = _block_diag(bbt_re.reshape(G, P, N), P, N).astype(bf16)
    bd_im = _block_diag(bbt_im.reshape(G, P, N), P, N).astype(bf16)
    c_re = small["c_re"].reshape(G, P, N).transpose(0, 2, 1)
    c_im = small["c_im"].reshape(G, P, N).transpose(0, 2, 1)
    cd_re = _block_diag(c_re, N, P).astype(bf16)
    cd_im = _block_diag(c_im, N, P).astype(bf16)
    lam_re_l, lam_im_l = _state_layout(lam_re), _state_layout(lam_im)

    def k1(i, nt, xt, g):
        return (rms_fwd(xt, g),)
    xn = rowwise("pre_mix_norm", k1, L, [full(x)], [small["g_pre_mix"]], [(D, bf16)], dep=first_dep)[0]
    big["w_in"] = get_w("w_in", xn)
    proj = mm_nn("proj_in", xn, big["w_in"])
    qr, kk, vv, u_bf = qkv_prep(proj, pos, inv_freq, d_attn, d_kv)
    attn = attn_fwd(qr, kk, vv, sink_b)
    u_cb = (d_attn + 2 * d_kv) // (d_ssm // 2)
    token = tick("attn", attn)
    y, z_bf, s_re, s_im = ssm_fwd(u_bf, proj, u_cb, bd_re, bd_im, cd_re, cd_im, lam_re_l, lam_im_l, small["d_skip"], dep=token)
    token = tick("ssm", z_bf)
    big["w_glu"] = get_w("w_glu", z_bf)
    gl = mm_nn("glu_proj", z_bf, big["w_glu"], dep=token)

    def k6(i, nt, at, yt, glt, bg, ga, gs):
        ssm = gelu(yt) * sigmoid(glt + bg)
        return (jnp.concatenate([rms_fwd(at, ga), rms_fwd(ssm, gs)], axis=1),)
    mixed = rowwise("mix_norms", k6, L, [full(attn), full(y), full(gl)],
                    [small["b_glu"], small["g_attn_out"], small["g_ssm_out"]], [(d_attn + d_ssm, bf16)])[0]
    big["w_o"] = get_w("w_o", mixed)
    mix = mm_nn("proj_out", mixed, big["w_o"])

    def k7(i, nt, xt, mt, gpm, gpf):
        h = xt + rms_fwd(mt, gpm)
        return h, rms_fwd(h, gpf)
    h, hn = rowwise("post_mix", k7, L, [full(x), full(mix)], [small["g_post_mix"], small["g_pre_ffn"]], [(D, f32), (D, bf16)])
    big["w_gate"] = get_w("w_gate", hn)
    big["w_up"] = get_w("w_up", hn)
    gt, up, hid = ffn_hidden(hn, big["w_gate"], big["w_up"])
    d_ff_dim = gt.shape[1]
    big["w_down"] = get_w("w_down", hid)
    ff = mm_nn("ffn_down", hid, big["w_down"], tk=d_ff_dim // 2)

    def k9(i, nt, ht, fft, tt, g):
        out = ht + rms_fwd(fft, g)
        err = out - tt
        per_row = jnp.mean(err * err, axis=-1, keepdims=True)
        loss = 0.5 * jnp.sum(per_row) * jnp.where(_lane((1, LANES)) == 0, 1.0, 0.0)
        d_out = err * (1.0 / D)
        d_ff, dg = rms_bwd(fft, g, d_out)
        return d_out, d_ff, dg, loss
    d_out, d_ff, dg_post_ffn, loss = rowwise("loss_head", k9, L, [full(h), full(ff), full(tgt)], [small["g_post_ffn"]],
                                             [(D, f32), (D, bf16)], reds=[D, LANES])

    d_gt, d_up = ffn_hidden_grad(d_ff, big["w_down"], gt, up)
    token = put_g("w_down", mm_tn("dw_down", hid, d_ff, out_dtype=gbf, tm=d_ff_dim // N_CHIPS))
    d_hn = mm_nt_pair("d_hn", d_gt, big["w_gate"], d_up, big["w_up"], dep=token)
    token = put_g("w_gate", mm_tn("dw_gate", hn, d_gt, shards=N_CHIPS, out_dtype=gbf))
    token = put_g("w_up", mm_tn("dw_up", hn, d_up, shards=N_CHIPS, out_dtype=gbf, dep=token))

    def k11(i, nt, ht, da, do, mt, gpf, gpm):
        dh_n, dg_pf = rms_bwd(ht, gpf, da)
        dh = do + dh_n
        d_mix, dg_pm = rms_bwd(mt, gpm, dh)
        return dh, d_mix, dg_pf, dg_pm
    dh, d_mix, dg_pre_ffn, dg_post_mix = rowwise("post_mix_grad", k11, L, [full(h), full(d_hn), full(d_out), full(mix)],
                                                 [small["g_pre_ffn"], small["g_post_mix"]], [(D, f32), (D, bf16)], reds=[D, D], dep=token)
    d_mixed = mm_nt("d_mixed", d_mix, big["w_o"])
    token = put_g("w_o", mm_tn("dw_o", mixed, d_mix, out_dtype=gbf))

    def k12(i, nt, at, yt, glt, da_n, ds_n, bg, ga, gs):
        z = gelu(yt)
        sg = sigmoid(glt + bg)
        ssm = z * sg
        d_at, dga = rms_bwd(at, ga, da_n)
        d_ssm_t, dgs = rms_bwd(ssm, gs, ds_n)
        d_gl = d_ssm_t * z * sg * (1.0 - sg)
        return d_at, d_ssm_t * sg, d_gl, dga, dgs, colsum(d_gl)
    d_attn_o, dz1, d_gl, dg_attn, dg_ssm, db_glu = rowwise(
        "mix_norms_grad", k12, L, [full(attn), full(y), full(gl), (d_mixed, d_attn, 0, 0), (d_mixed, d_ssm, d_attn // d_ssm, 0)],
        [small["b_glu"], small["g_attn_out"], small["g_ssm_out"]], [(d_attn, f32), (d_ssm, f32), (d_ssm, bf16)],
        reds=[d_attn, d_ssm, d_ssm], dep=token)
    dz2 = mm_nt("d_glu_in", d_gl, big["w_glu"])
    token = put_g("w_glu", mm_tn("dw_glu", z_bf, d_gl, out_dtype=gbf))

    du, dbd_re, dbd_im, dcd_re, dcd_im, dlam_re_l, dlam_im_l, dd_skip = ssm_bwd(
        y, dz1, dz2, u_bf, proj, u_cb, s_re, s_im, bd_re, bd_im, cd_re, cd_im, lam_re_l, lam_im_l, small["d_skip"], dep=token)
    dq, dkk_c, dkk_p, dvv_c, dvv_p, dsink = attn_bwd(qr, kk, vv, sink_b, attn, d_attn_o)
    d_proj = qkv_grad(dq, dkk_c, dkk_p, dvv_c, dvv_p, du, pos, inv_freq)
    d_xn = mm_nt("d_xn", d_proj, big["w_in"])
    token = put_g("w_in", mm_tn("dw_in", xn, d_proj, shards=N_CHIPS, out_dtype=gbf))

    def k17(i, nt, xt, dxn, dht, g):
        dx, dg = rms_bwd(xt, g, dxn)
        return dht + dx, dg
    grad_x, dg_pre_mix = rowwise("pre_mix_grad", k17, L, [full(x), full(d_xn), full(dh)], [small["g_pre_mix"]],
                                 [(D, f32)], reds=[D], dep=token)

    gather = spread.T
    dbbt_re = _block_diag_take(dbd_re, P, N).reshape(G * P, N)
    dbbt_im = _block_diag_take(dbd_im, P, N).reshape(G * P, N)
    d_a_re, d_a_im, d_logdt, dbt_re, dbt_im = ssm_params_grad(
        a_re, a_im, logdt_b, bt_re, bt_im, spread, gather,
        _state_layout_inv(dlam_re_l, G, N), _state_layout_inv(dlam_im_l, G, N), dbbt_re, dbbt_im)
    q_per_kv = d_attn // HEAD_DIM // N_KV_HEADS
    small_grads = {
        "g_pre_mix": dg_pre_mix, "sinks": dsink[:, :q_per_kv, 0].reshape(1, -1),
        "a_re": d_a_re, "a_im": d_a_im, "log_dt": d_logdt.reshape(1, G),
        "b_re": dbt_re, "b_im": dbt_im,
        "c_re": _block_diag_take(dcd_re, N, P).transpose(0, 2, 1), "c_im": _block_diag_take(dcd_im, N, P).transpose(0, 2, 1),
        "d_skip": dd_skip, "b_glu": db_glu, "g_attn_out": dg_attn, "g_ssm_out": dg_ssm,
        "g_post_mix": dg_post_mix, "g_pre_ffn": dg_pre_ffn, "g_post_ffn": dg_post_ffn,
    }
    return loss, grad_x, small_grads


WEIGHTS = ['g_pre_mix', 'w_in', 'sinks', 'a_re', 'a_im', 'log_dt', 'b_re', 'b_im', 'c_re', 'c_im', 'd_skip', 'w_glu', 'b_glu',
           'g_attn_out', 'g_ssm_out', 'w_o', 'g_post_mix', 'g_pre_ffn', 'w_gate', 'w_up', 'w_down', 'g_post_ffn']
BIG = ['w_in', 'w_glu', 'w_o', 'w_gate', 'w_up', 'w_down']
COL_SHARDED = ['w_in', 'w_gate', 'w_up']
SMALL = [n for n in WEIGHTS if n not in BIG]
GATHER_GROUPS = [["w_in"], ["w_glu", "w_o"], ["w_gate", "w_up"], ["w_down"]]
REDUCE_GROUPS = [["w_down", "w_gate", "w_up"], ["w_o", "w_glu", "w_in"]]


PACK_ROWS = 256


def _pack(parts):
    flat = jnp.concatenate([p.reshape(-1) for p in parts])
    pad = (-flat.size) % (PACK_ROWS * LANES)
    return jnp.pad(flat, (0, pad)).reshape(-1, LANES)


TRANSPOSED_VIEW = ("b_re", "b_im")


def small_view(name, a):
    if name in TRANSPOSED_VIEW:
        a = a.transpose(0, 1, 3, 2)
    return a.reshape(-1, a.shape[-1])


def small_unview(name, p, shape):
    if name in TRANSPOSED_VIEW:
        return p.reshape(shape[0], shape[1], shape[3], shape[2]).transpose(0, 1, 3, 2)
    return p.reshape(shape)


def _unpack(packed, shapes):
    flat = packed.reshape(-1)
    out, off = [], 0
    for s in shapes:
        n = int(np.prod(s))
        out.append(flat[off:off + n].reshape(s))
        off += n
    return out


def kernel(x, positions, g_pre_mix, w_in, sinks, a_re, a_im, log_dt, b_re, b_im, c_re, c_im, d_skip, w_glu, b_glu, g_attn_out, g_ssm_out, w_o, g_post_mix, g_pre_ffn, w_gate, w_up, w_down, g_post_ffn, loss_target, m_g_pre_mix, m_w_in, m_sinks, m_a_re, m_a_im, m_log_dt, m_b_re, m_b_im, m_c_re, m_c_im, m_d_skip, m_w_glu, m_b_glu, m_g_attn_out, m_g_ssm_out, m_w_o, m_g_post_mix, m_g_pre_ffn, m_w_gate, m_w_up, m_w_down, m_g_post_ffn, v_g_pre_mix, v_w_in, v_sinks, v_a_re, v_a_im, v_log_dt, v_b_re, v_b_im, v_c_re, v_c_im, v_d_skip, v_w_glu, v_b_glu, v_g_attn_out, v_g_ssm_out, v_w_o, v_g_post_mix, v_g_pre_ffn, v_w_gate, v_w_up, v_w_down, v_g_post_ffn):
    args = dict(locals())
    w = {n: args[n] for n in WEIGHTS}
    m = {n: args["m_" + n] for n in WEIGHTS}
    v = {n: args["v_" + n] for n in WEIGHTS}
    L, D = x.shape[1], x.shape[2]

    ax, ay, ac = _place()
    mine_arr = (2 * ax + ay).astype(jnp.int32).reshape(1)
    c_arr = ac.astype(jnp.int32).reshape(1)

    me_arr = (4 * ax + 2 * ay + ac).astype(jnp.int32).reshape(1)

    bufs = {"w_in": into_slot("cast_w_in", w["w_in"][0], mine_arr, N_CHIPS, bf16)}
    (first_sems,), (bufs["w_in"],), token = gather_start("gather_start_in", [bufs["w_in"]], [[0]], mine_arr)
    sems = {"w_in": first_sems}
    for n in BIG[1:]:
        bufs[n] = into_slot("cast_" + n, w[n][0], mine_arr, N_CHIPS, bf16, dep=token)
    first = ["w_glu", "w_o", "w_gate", "w_up"]
    (sems["w_glu"], sems["w_gate"], sems["w_up"]), started, token = gather_start(
        "gather_start_rest", [bufs[n] for n in first], [[0, 1], [2], [3]], token, neighbours_only=(2, 3))
    bufs.update(zip(first, started))
    relays, ready = {}, set()

    def tick(name, after):
        n, more = ("w_gate", []) if name == "attn" else ("w_up", ["w_down"])
        r_send, r_recv, sems["w_down"], (bufs[n],), down, tok = gather_relay(
            "gather_relay_" + n, [bufs[n]], [sems[n]], [bufs[k] for k in more], after)
        bufs.update(zip(more, down))
        relays[n] = (r_send, r_recv)
        return tok

    def get_w(n, after):
        if n not in ready:
            members = [g for g in GATHER_GROUPS if n in g][0]
            if members[0] in relays:
                landed = [gather_wait_relay("gather_wait_" + k, [bufs[k]], *relays[k], after)[0] for k in members]
                which = (2,)
            else:
                landed = gather_wait("gather_wait_" + members[0], [bufs[k] for k in members], *sems[members[0]], after)
                which = (0, 1, 2)
            bufs.update(zip(members, gather_forward("gather_forward_" + members[0], landed, which)))
            ready.update(members)
        g = bufs[n]
        return g if n in COL_SHARDED else g.reshape(g.shape[0] * g.shape[1], g.shape[2])

    swaps, inflight = {}, []

    def put_g(n, g):
        g3 = g if n in COL_SHARDED else g.reshape(N_CHIPS, g.shape[0] // N_CHIPS, g.shape[1])
        swaps[n] = sibling_start("swap_start_" + n, [g3], True)
        for gi, members in enumerate(REDUCE_GROUPS):
            if n == members[-1]:
                last = swaps[n][4]
                pair = []
                for k in members:
                    send, recv, srcs, lands, _ = swaps[k]
                    (src,), (got,) = sibling_wait("swap_wait_" + k, srcs, lands, send, recv, True, last)
                    pair.append(pair_sum("pair_sum_" + k, src, got, c_arr))
                send, recv, parts, lands, tok = scatter_start("scatter_start_%d" % gi, pair)
                inflight.append((members, send, recv, parts, lands))
                return tok
        return swaps[n][4]

    small = {n: w[n].reshape(1, -1) for n in SMALL}
    pos = positions.reshape(L, 1).astype(f32)
    d_in = N_CHIPS * w["w_in"].shape[2]
    loss, grad_x, small_grads = local_step(x[0], pos, loss_target[0], small, d_in, get_w, put_g, first_dep=token, tick=tick)

    shapes = [w[n].shape for n in SMALL]
    blocks = into_slot("small_block", _pack([small_grads[n] for n in SMALL] + [loss]), me_arr, N_DEV, f32)
    small_send, small_recv, blocks, after = everyone_start("small_start", blocks)

    grads, delta, new_m, new_v = {}, {}, {}, {}
    for gi, (members, send, recv, parts, lands) in enumerate(inflight):
        parts, landed = scatter_wait("scatter_wait_%d" % gi, parts, lands, send, recv, after)
        joins, dep = [], None
        for k, p, t in zip(members, parts, landed):
            joins.append(sibling_start("join_start_" + k, [chip_sum("chip_sum_" + k, p, t, mine_arr, dep=dep)], False))
            dep = after = joins[-1][4]
        for n, (send, recv, srcs, lands, _) in zip(members, joins):
            (own,), (sib,) = sibling_wait("join_wait_" + n, srcs, lands, send, recv, False, after)
            g_, d_, m_, v_ = adamw_halves("adamw_" + n, w[n][0], own, sib, m[n][0], v[n][0], c_arr)
            grads[n], delta[n], new_m[n], new_v[n] = g_[None], d_[None], m_[None], v_[None]
            after = v_
    blocks = everyone_wait("small_wait", blocks, small_send, small_recv, after)
    small_sum = sum_slots("small_sum", blocks)
    *small_g, loss = _unpack(small_sum, [small_view(n, w[n]).shape for n in SMALL] + [loss.shape])
    loss = loss[0, 0]
    outs = adamw_many("adamw_small", [small_view(n, w[n]) for n in SMALL], small_g,
                      [small_view(n, m[n]) for n in SMALL], [small_view(n, v[n]) for n in SMALL])
    for t, parts in zip((grads, delta, new_m, new_v), (small_g,) + tuple(outs)):
        t.update({n: small_unview(n, p, w[n].shape) for n, p in zip(SMALL, parts)})

    return (loss, grad_x[None], *[grads[n] for n in WEIGHTS], *[delta[n] for n in WEIGHTS],
            *[new_m[n] for n in WEIGHTS], *[new_v[n] for n in WEIGHTS])
```

```python
import functools
import math

import jax
import jax.numpy as jnp
import numpy as np
from jax import lax
from jax.experimental import pallas as pl
from jax.experimental.pallas import tpu as pltpu

f32 = jnp.float32
bf16 = jnp.bfloat16
HIGHEST = lax.Precision.HIGHEST
MESH = pl.DeviceIdType.MESH

HEAD_DIM = 64
N_KV_HEADS = 4
ATTN_BLOCK = 128
ROPE_THETA = 10000.0
SSM_GROUP = 16
SSM_STATE = 64
RMS_EPS = 1e-6
LANES = 128
SUBLANES = 8
VMEM_LIMIT = 52 * 1024 * 1024
N_CHIPS = 4
N_DEV = 8
NEG = -1e30

ADAM_LR, ADAM_B1, ADAM_B2, ADAM_EPS, ADAM_WD, ADAM_STEP = 0.001, 0.9, 0.999, 1e-08, 0.01, 10

NN = (((1,), (0,)), ((), ()))
NT = (((1,), (1,)), ((), ()))
TN = (((0,), (0,)), ((), ()))


def _params(*sem):
    return pltpu.CompilerParams(dimension_semantics=sem or None, vmem_limit_bytes=VMEM_LIMIT)


def _dot(a, b, dims=NN):
    return lax.dot_general(a, b, dims, preferred_element_type=f32)


def _pick(dim, pref):
    t = min(dim, pref)
    while dim % t:
        t -= LANES
    assert t > 0, (dim, pref)
    return t


ANY = pl.BlockSpec(memory_space=pl.ANY)


def _with_dep(in_specs, operands, dep):
    if dep is None:
        return list(in_specs), list(operands), 0
    return list(in_specs) + [ANY], list(operands) + [dep], 1


def _mm_call(name, grid, in_specs, out_spec, out_shape, acc_shape, dims, operands, dep=None):
    nk = grid[2]
    in_specs, operands, n_dep = _with_dep(in_specs, operands, dep)

    def body_one(a_ref, b_ref, *rest):
        o_ref = rest[n_dep]
        o_ref[...] = _dot(a_ref[...], b_ref[...], dims).astype(o_ref.dtype)

    def body(a_ref, b_ref, *rest):
        o_ref, acc_ref = rest[n_dep], rest[n_dep + 1]
        k = pl.program_id(2)

        @pl.when(k == 0)
        def _():
            acc_ref[...] = _dot(a_ref[...], b_ref[...], dims)

        @pl.when((k > 0) & (k < nk - 1))
        def _():
            acc_ref[...] += _dot(a_ref[...], b_ref[...], dims)

        @pl.when(k == nk - 1)
        def _():
            o_ref[...] = (acc_ref[...] + _dot(a_ref[...], b_ref[...], dims)).astype(o_ref.dtype)

    return pl.pallas_call(
        body_one if nk == 1 else body, out_shape=out_shape, grid=grid, in_specs=in_specs, out_specs=out_spec,
        scratch_shapes=[] if nk == 1 else [pltpu.VMEM(acc_shape, f32)], name=name,
        compiler_params=_params("parallel", "parallel", "arbitrary"))(*operands)


def mm_nt_pair(name, a1, b1, a2, b2, tm=1024, tk=1024, dep=None):
    M = a1.shape[0]
    S, K, n = b1.shape
    tm, tko = _pick(M, tm), _pick(K, tk)
    nk = 2 * S

    def body(a1_ref, b1_ref, a2_ref, b2_ref, *rest):
        o_ref, acc_ref = rest[-2], rest[-1]
        k = pl.program_id(2)

        @pl.when(k == 0)
        def _():
            acc_ref[...] = _dot(a1_ref[...], b1_ref[...], NT)

        @pl.when((k > 0) & (k < S))
        def _():
            acc_ref[...] += _dot(a1_ref[...], b1_ref[...], NT)

        @pl.when((k >= S) & (k < nk - 1))
        def _():
            acc_ref[...] += _dot(a2_ref[...], b2_ref[...], NT)

        @pl.when(k == nk - 1)
        def _():
            o_ref[...] = acc_ref[...] + _dot(a2_ref[...], b2_ref[...], NT)

    first = lambda k: jnp.minimum(k, S - 1)
    second = lambda k: jnp.maximum(k - S, 0)
    in_specs = [pl.BlockSpec((tm, n), lambda i, j, k: (i, first(k))), pl.BlockSpec((None, tko, n), lambda i, j, k: (first(k), j, 0)),
                pl.BlockSpec((tm, n), lambda i, j, k: (i, second(k))), pl.BlockSpec((None, tko, n), lambda i, j, k: (second(k), j, 0))]
    in_specs, operands, _ = _with_dep(in_specs, (a1, b1, a2, b2), dep)
    return pl.pallas_call(
        body, out_shape=jax.ShapeDtypeStruct((M, K), f32), grid=(M // tm, K // tko, nk), in_specs=in_specs,
        out_specs=pl.BlockSpec((tm, tko), lambda i, j, k: (i, j)), scratch_shapes=[pltpu.VMEM((tm, tko), f32)], name=name,
        compiler_params=_params("parallel", "parallel", "arbitrary"))(*operands)


def mm_nn(name, a, b, out_dtype=f32, tm=1024, tn=1024, tk=2048, dep=None):
    M, K = a.shape
    tm, tk = _pick(M, tm), _pick(K, tk)
    if b.ndim == 3:
        S, _, n = b.shape
        tn = _pick(n, 2048)
        per = n // tn
        b_spec = pl.BlockSpec((None, tk, tn), lambda i, j, k: (j // per, k, j % per))
        N = S * n
    else:
        N = b.shape[1]
        tn = _pick(N, tn)
        b_spec = pl.BlockSpec((tk, tn), lambda i, j, k: (k, j))
    grid = (M // tm, N // tn, K // tk)
    return _mm_call(name, grid, [pl.BlockSpec((tm, tk), lambda i, j, k: (i, k)), b_spec],
                    pl.BlockSpec((tm, tn), lambda i, j, k: (i, j)), jax.ShapeDtypeStruct((M, N), out_dtype),
                    (tm, tn), NN, (a, b), dep)


def mm_nt(name, a, b, out_dtype=f32, tm=1024, tn=2048, tk=1024, dep=None):
    M, N = a.shape
    tm = _pick(M, tm)
    if b.ndim == 3:
        S, K, n = b.shape
        tr = _pick(n, 2048)
        per = n // tr
        tko = _pick(K, tk)
        b_spec = pl.BlockSpec((None, tko, tr), lambda i, j, k: (k // per, j, k % per))
    else:
        K = b.shape[0]
        tr = _pick(N, tn)
        tko = _pick(K, tk)
        b_spec = pl.BlockSpec((tko, tr), lambda i, j, k: (j, k))
    grid = (M // tm, K // tko, N // tr)
    return _mm_call(name, grid, [pl.BlockSpec((tm, tr), lambda i, j, k: (i, k)), b_spec],
                    pl.BlockSpec((tm, tko), lambda i, j, k: (i, j)), jax.ShapeDtypeStruct((M, K), out_dtype),
                    (tm, tko), NT, (a, b), dep)


def mm_tn(name, a, b, shards=None, out_dtype=f32, tm=1024, tn=1024, tl=2048, dep=None):
    L, K = a.shape
    N = b.shape[1]
    tl, tko = _pick(L, tl), _pick(K, tm)
    if shards:
        n = N // shards
        tn = _pick(n, 2048)
        per = n // tn
        o_spec = pl.BlockSpec((None, tko, tn), lambda i, j, k: (j // per, i, j % per))
        o_shape = jax.ShapeDtypeStruct((shards, K, n), out_dtype)
    else:
        tn = _pick(N, tn)
        o_spec = pl.BlockSpec((tko, tn), lambda i, j, k: (i, j))
        o_shape = jax.ShapeDtypeStruct((K, N), out_dtype)
    grid = (K // tko, N // tn, L // tl)
    return _mm_call(name, grid, [pl.BlockSpec((tl, tko), lambda i, j, k: (k, i)),
                                 pl.BlockSpec((tl, tn), lambda i, j, k: (k, j))],
                    o_spec, o_shape, (tko, tn), TN, (a, b), dep)


def ffn_hidden(hn, w_gate, w_up, tm=512):
    M, K = hn.shape
    S, _, n = w_gate.shape
    tm = _pick(M, tm)

    def body(a_ref, g_ref, u_ref, gt_ref, up_ref, hid_ref):
        a = a_ref[...]
        g = _dot(a, g_ref[...])
        u = _dot(a, u_ref[...])
        gt_ref[...] = g.astype(bf16)
        up_ref[...] = u.astype(bf16)
        hid_ref[...] = (g * sigmoid(g) * u).astype(bf16)

    w_spec = pl.BlockSpec((None, K, n), lambda s, i: (s, 0, 0))
    o_spec = pl.BlockSpec((tm, n), lambda s, i: (i, s))
    o = jax.ShapeDtypeStruct((M, S * n), bf16)
    return pl.pallas_call(
        body, out_shape=[o, o, o], grid=(S, M // tm), in_specs=[pl.BlockSpec((tm, K), lambda s, i: (i, 0)), w_spec, w_spec],
        out_specs=[o_spec, o_spec, o_spec], name="ffn_hidden", compiler_params=_params("parallel", "parallel"))(hn, w_gate, w_up)


def ffn_hidden_grad(d_ff, w_down, gt, up, tm=512):
    M, D = d_ff.shape
    F = w_down.shape[0]
    n = _pick(F // N_CHIPS, 2048)
    tm = _pick(M, tm)

    def body(a_ref, b_ref, gt_ref, up_ref, dg_ref, du_ref):
        dh = _dot(a_ref[...], b_ref[...], NT)
        g = gt_ref[...].astype(f32)
        sg = sigmoid(g)
        dg_ref[...] = (dh * up_ref[...].astype(f32) * (sg * (1.0 + g * (1.0 - sg)))).astype(bf16)
        du_ref[...] = (dh * (g * sg)).astype(bf16)

    t_spec = pl.BlockSpec((tm, n), lambda j, i: (i, j))
    o = jax.ShapeDtypeStruct((M, F), bf16)
    return pl.pallas_call(
        body, out_shape=[o, o], grid=(F // n, M // tm),
        in_specs=[pl.BlockSpec((tm, D), lambda j, i: (i, 0)), pl.BlockSpec((n, D), lambda j, i: (j, 0)), t_spec, t_spec],
        out_specs=[t_spec, t_spec], name="ffn_hidden_grad", compiler_params=_params("parallel", "parallel"))(d_ff, w_down, gt, up)


def rowwise(name, fn, L, rows, bcast, outs, reds=(), tr=256, dep=None):
    tr = min(tr, L)
    nt = L // tr
    n_rows, n_b, n_o = len(rows), len(bcast), len(outs)
    n_dep = 0 if dep is None else 1

    def body(*refs):
        i = pl.program_id(0)
        ins = [r[...] for r in refs[:n_rows + n_b]]
        res = fn(i, nt, *ins)
        o_refs = refs[n_rows + n_b + n_dep:]
        for k in range(n_o):
            o_refs[k][...] = res[k].astype(o_refs[k].dtype)
        if reds:
            @pl.when(i == 0)
            def _():
                for k in range(len(reds)):
                    o_refs[n_o + k][...] = jnp.zeros_like(o_refs[n_o + k])
            for k in range(len(reds)):
                o_refs[n_o + k][...] += res[n_o + k]

    def row_spec(width, cb, shift):
        if shift:
            return pl.BlockSpec((tr, width), lambda i: (jnp.minimum(i + shift, nt - 1), cb))
        return pl.BlockSpec((tr, width), lambda i: (i, cb))

    in_specs = [row_spec(w, cb, sh) for (_, w, cb, sh) in rows]
    in_specs += [pl.BlockSpec(b.shape, lambda i: (0, 0)) for b in bcast]
    out_specs = [pl.BlockSpec((tr, w), lambda i: (i, 0)) for (w, _) in outs]
    out_specs += [pl.BlockSpec((1, w), lambda i: (0, 0)) for w in reds]
    out_shape = [jax.ShapeDtypeStruct((L, w), dt) for (w, dt) in outs]
    out_shape += [jax.ShapeDtypeStruct((1, w), f32) for w in reds]
    in_specs, operands, _ = _with_dep(in_specs, [r[0] for r in rows] + list(bcast), dep)
    return pl.pallas_call(
        body, out_shape=out_shape, grid=(nt,), in_specs=in_specs, out_specs=out_specs, name=name,
        compiler_params=_params("arbitrary"))(*operands)


def full(a):
    return (a, a.shape[1], 0, 0)


def colsum(v):
    return jnp.sum(v, axis=0, keepdims=True)


def rms_fwd(x, g):
    r = lax.rsqrt(jnp.mean(x * x, axis=-1, keepdims=True) + RMS_EPS)
    return x * r * g


def rms_bwd(x, g, dy):
    r = lax.rsqrt(jnp.mean(x * x, axis=-1, keepdims=True) + RMS_EPS)
    xh = x * r
    dyg = dy * g
    dx = r * (dyg - xh * jnp.mean(dyg * xh, axis=-1, keepdims=True))
    return dx, colsum(dy * xh)


GELU_C = math.sqrt(2.0 / math.pi)


def gelu(y):
    return y * (0.5 * (1.0 + jnp.tanh(GELU_C * (y + 0.044715 * (y * y * y)))))


def gelu_grad(y):
    t = jnp.tanh(GELU_C * (y + 0.044715 * (y * y * y)))
    return 0.5 * (1.0 + t) + 0.5 * y * (1.0 - t * t) * (GELU_C * (1.0 + 3 * 0.044715 * (y * y)))


def sigmoid(v):
    return 1.0 / (1.0 + jnp.exp(-v))


def _lane(shape):
    return lax.broadcasted_iota(jnp.int32, shape, 1)


def _rot_chunk(t, cos, sin_signed):
    first = (_lane(t.shape) % HEAD_DIM) < (HEAD_DIM // 2)
    partner = jnp.where(first, pltpu.roll(t, LANES - HEAD_DIM // 2, 1), pltpu.roll(t, HEAD_DIM // 2, 1))
    return t * cos + partner * sin_signed


def _cos_sin(pos, inv_freq, inverse):
    ang = pos * inv_freq
    cos, sin = jnp.cos(ang), jnp.sin(ang)
    first = (_lane(ang.shape) % HEAD_DIM) < (HEAD_DIM // 2)
    sign = jnp.where(first, -1.0, 1.0) * (-1.0 if inverse else 1.0)
    return cos, sin * sign


def _dup_head(chunk, odd):
    low = _lane(chunk.shape) < HEAD_DIM
    x = jnp.where(low != odd, chunk, 0.0)
    return x + pltpu.roll(x, HEAD_DIM, 1)


def _chunks(v):
    return [v[:, LANES * c:LANES * (c + 1)] for c in range(v.shape[1] // LANES)]


def qkv_prep(proj, pos, inv_freq, d_attn, d_kv):
    L = proj.shape[0]
    d_ssm = proj.shape[1] - d_attn - 2 * d_kv
    half = d_ssm // 2
    scale = 1.0 / math.sqrt(HEAD_DIM)

    def fn(i, nt, q, k, v, u0, u1, p, invf):
        cos, sin = _cos_sin(p, invf, False)
        qr = jnp.concatenate([_rot_chunk(c, cos, sin) for c in _chunks(q)], axis=1) * scale
        kr = [_rot_chunk(c, cos, sin) for c in _chunks(k)]
        kk = jnp.concatenate([_dup_head(c, odd) for c in kr for odd in (False, True)], axis=1)
        vv = jnp.concatenate([_dup_head(c, odd) for c in _chunks(v) for odd in (False, True)], axis=1)
        return qr, kk, vv, jnp.concatenate([u0, u1], axis=1)

    u_cb = (d_attn + 2 * d_kv) // half
    return rowwise("qkv_prep", fn, L,
                   [(proj, d_attn, 0, 0), (proj, d_kv, d_attn // d_kv, 0), (proj, d_kv, d_attn // d_kv + 1, 0),
                    (proj, half, u_cb, 0), (proj, half, u_cb + 1, 0), full(pos)],
                   [inv_freq], [(d_attn, bf16), (2 * d_kv, bf16), (2 * d_kv, bf16), (d_ssm, bf16)])


def qkv_grad(dq, dkk_c, dkk_p, dvv_c, dvv_p, du, pos, inv_freq):
    L, d_attn = dq.shape
    d_kv = dkk_c.shape[1] // 2
    scale = 1.0 / math.sqrt(HEAD_DIM)

    def fold(cur, prev, i, nt):
        t = cur + jnp.where(i < nt - 1, prev, 0.0)
        out = []
        for c in range(t.shape[1] // (2 * LANES)):
            even, odd = t[:, 2 * c * LANES:(2 * c + 1) * LANES], t[:, (2 * c + 1) * LANES:(2 * c + 2) * LANES]
            even, odd = even + pltpu.roll(even, HEAD_DIM, 1), odd + pltpu.roll(odd, HEAD_DIM, 1)
            out.append(jnp.where(_lane(even.shape) < HEAD_DIM, even, odd))
        return out

    def fn(i, nt, dq_t, kc, kp, vc, vp, du_t, p, invf):
        cos, sin = _cos_sin(p, invf, True)
        dq_o = jnp.concatenate([_rot_chunk(c, cos, sin) for c in _chunks(dq_t)], axis=1) * scale
        dk_o = jnp.concatenate([_rot_chunk(c, cos, sin) for c in fold(kc, kp, i, nt)], axis=1)
        dv_o = jnp.concatenate(fold(vc, vp, i, nt), axis=1)
        return (jnp.concatenate([dq_o, dk_o, dv_o, du_t], axis=1),)

    return rowwise("qkv_grad", fn, L,
                   [full(dq), full(dkk_c), (dkk_p, 2 * d_kv, 0, 1), full(dvv_c), (dvv_p, 2 * d_kv, 0, 1), full(du), full(pos)],
                   [inv_freq], [(d_attn + 2 * d_kv + du.shape[1], bf16)], tr=ATTN_BLOCK)[0]


def _attn_specs(L):
    nb = L // ATTN_BLOCK
    B = ATTN_BLOCK
    q_spec = lambda width: pl.BlockSpec((B, width), lambda n: (n, 0))
    prev = lambda width: pl.BlockSpec((B, width), lambda n: (jnp.maximum(n - 1, 0), 0))
    return nb, q_spec, prev


def _attn_mask(n):
    B = ATTN_BLOCK
    row = lax.broadcasted_iota(jnp.int32, (B, 2 * B), 0)
    col = lax.broadcasted_iota(jnp.int32, (B, 2 * B), 1)
    return ((col < B) & (col > row) & (n > 0)) | ((col >= B) & (row >= col - B))


def _attn_probs(qm, kcat, sink, mask):
    s = jnp.where(mask, _dot(qm, kcat, NT), NEG)
    m = jnp.maximum(jnp.max(s, axis=1, keepdims=True), sink)
    p, ps = jnp.exp(s - m), jnp.exp(sink - m)
    inv = 1.0 / (jnp.sum(p, axis=1, keepdims=True) + ps)
    return p, inv, ps


def _attn_heads(q_ref, s_ref, h, q_per_kv):
    low = _lane((ATTN_BLOCK, LANES)) < HEAD_DIM
    heads = []
    for pr in range(h * q_per_kv // 2, (h + 1) * q_per_kv // 2):
        q2 = q_ref[:, LANES * pr:LANES * (pr + 1)]
        for odd in (False, True):
            mine = low != odd
            sink = jnp.max(s_ref[2 * pr + int(odd):2 * pr + int(odd) + 1, :], axis=1, keepdims=True)
            heads.append((pr, mine, jnp.where(mine, q2, jnp.zeros_like(q2)), sink))
    return low, heads


def _kv_block(prev_ref, cur_ref, h):
    return jnp.concatenate([prev_ref[:, LANES * h:LANES * (h + 1)], cur_ref[:, LANES * h:LANES * (h + 1)]], axis=0)


def attn_fwd(qr, kk, vv, sink_b):
    L, d_attn = qr.shape
    nb, q_spec, prev = _attn_specs(L)
    d_kk = kk.shape[1]
    n_kv = d_kk // LANES
    q_per_kv = d_attn // HEAD_DIM // n_kv

    def body(q_ref, kc_ref, kp_ref, vc_ref, vp_ref, s_ref, o_ref):
        mask = _attn_mask(pl.program_id(0))
        for h in range(n_kv):
            kcat, vcat = _kv_block(kp_ref, kc_ref, h), _kv_block(vp_ref, vc_ref, h)
            low, heads = _attn_heads(q_ref, s_ref, h, q_per_kv)
            probs = [_attn_probs(qm, kcat, sink, mask) for (_, _, qm, sink) in heads]
            outs = [_dot(p.astype(bf16), vcat) * inv for (p, inv, _) in probs]
            for i in range(0, len(heads), 2):
                pr = heads[i][0]
                o_ref[:, LANES * pr:LANES * (pr + 1)] = jnp.where(low, outs[i], outs[i + 1])

    return pl.pallas_call(
        body, out_shape=jax.ShapeDtypeStruct((L, d_attn), f32), grid=(nb,),
        in_specs=[q_spec(d_attn), q_spec(d_kk), prev(d_kk), q_spec(d_kk), prev(d_kk), pl.BlockSpec(sink_b.shape, lambda n: (0, 0))],
        out_specs=q_spec(d_attn), name="attn_fwd", compiler_params=_params("arbitrary"))(qr, kk, kk, vv, vv, sink_b)


def attn_bwd(qr, kk, vv, sink_b, attn, d_attn_out):
    L, d_attn = qr.shape
    nb, q_spec, prev = _attn_specs(L)
    d_kk = kk.shape[1]
    n_kv = d_kk // LANES
    q_per_kv = d_attn // HEAD_DIM // n_kv

    def body(q_ref, kc_ref, kp_ref, vc_ref, vp_ref, s_ref, o_ref, do_ref, dq_ref, dkc_ref, dkp_ref, dvc_ref, dvp_ref, ds_ref):
        n = pl.program_id(0)
        B = ATTN_BLOCK
        mask = _attn_mask(n)
        srow = lax.broadcasted_iota(jnp.int32, (SUBLANES, LANES), 0)

        @pl.when(n == 0)
        def _():
            ds_ref[...] = jnp.zeros_like(ds_ref)

        for h in range(n_kv):
            kcat, vcat = _kv_block(kp_ref, kc_ref, h), _kv_block(vp_ref, vc_ref, h)
            low, heads = _attn_heads(q_ref, s_ref, h, q_per_kv)
            probs = [_attn_probs(qm, kcat, sink, mask) for (_, _, qm, sink) in heads]
            dk = jnp.zeros((2 * B, LANES), f32)
            dv = dk
            dsink = jnp.zeros((SUBLANES, LANES), f32)
            dqs = []
            for i, ((pr, mine, qm, _), (p, inv, ps)) in enumerate(zip(heads, probs)):
                do2 = do_ref[:, LANES * pr:LANES * (pr + 1)]
                delta = jnp.sum(jnp.where(mine, do2 * o_ref[:, LANES * pr:LANES * (pr + 1)], 0.0), axis=1, keepdims=True)
                dob = jnp.where(mine, do2, 0.0).astype(bf16)
                p = p * inv
                ds = (p * (_dot(dob, vcat, NT) - delta)).astype(bf16)
                dqs.append(_dot(ds, kcat))
                dk = dk + _dot(ds, qm, TN)
                dv = dv + _dot(p.astype(bf16), dob, TN)
                dsink = dsink + jnp.where(srow == i, -jnp.sum(ps * inv * delta), 0.0)
            for i in range(0, len(heads), 2):
                pr = heads[i][0]
                dq_ref[:, LANES * pr:LANES * (pr + 1)] = jnp.where(low, dqs[i], dqs[i + 1])
            cols = slice(LANES * h, LANES * (h + 1))
            dkp_ref[:, cols] = dk[:B]
            dkc_ref[:, cols] = dk[B:]
            dvp_ref[:, cols] = dv[:B]
            dvc_ref[:, cols] = dv[B:]
            ds_ref[h] += dsink

    kv_shape = jax.ShapeDtypeStruct(kk.shape, f32)
    ds_shape = (n_kv, SUBLANES, LANES)
    return pl.pallas_call(
        body,
        out_shape=[jax.ShapeDtypeStruct((L, d_attn), f32), kv_shape, kv_shape, kv_shape, kv_shape, jax.ShapeDtypeStruct(ds_shape, f32)],
        grid=(nb,),
        in_specs=[q_spec(d_attn), q_spec(d_kk), prev(d_kk), q_spec(d_kk), prev(d_kk), pl.BlockSpec(sink_b.shape, lambda n: (0, 0)),
                  q_spec(d_attn), q_spec(d_attn)],
        out_specs=[q_spec(d_attn)] + [q_spec(d_kk)] * 4 + [pl.BlockSpec(ds_shape, lambda n: (0, 0, 0))],
        name="attn_bwd", compiler_params=_params("arbitrary"))(qr, kk, kk, vv, vv, sink_b, attn, d_attn_out)


SSM_T = 256
NQ = SUBLANES * SSM_STATE // LANES
NJ = SUBLANES


def _strided_put(ref, j, val):
    for q in range(NQ):
        ref.at[q][pl.ds(j, SSM_T, stride=NJ), :] = val[:, LANES * q:LANES * (q + 1)]


def _strided_get(ref, j):
    return jnp.concatenate([ref.at[q][pl.ds(j, SSM_T, stride=NJ), :] for q in range(NQ)], axis=1)


def _ssm_specs(L, rev):
    nt = L // SSM_T
    idx = (lambda i: nt - 1 - i) if rev else (lambda i: i)
    row = lambda w, cb=0: pl.BlockSpec((SSM_T, w), lambda i: (idx(i), cb))
    state = pl.BlockSpec((NQ, SSM_T * NJ, LANES), lambda i: (0, idx(i), 0))
    whole = lambda a: pl.BlockSpec(a.shape, lambda i: (0,) * a.ndim)
    return nt, row, state, whole


def ssm_fwd(u_bf, proj, u_cb, bd_re, bd_im, cd_re, cd_im, lam_re, lam_im, d_skip, dep=None):
    L, d_ssm = u_bf.shape
    nt, row, state, whole = _ssm_specs(L, False)
    half = d_ssm // 2
    gw = d_ssm // NJ
    n_dep = 0 if dep is None else 1

    def body(u_ref, u0_ref, u1_ref, bdr, bdi, cdr, cdi, lr_ref, li_ref, d_ref, *rest):
        y_ref, z_ref, sr_ref, si_ref, carry = rest[n_dep:]
        i = pl.program_id(0)

        @pl.when(i == 0)
        def _():
            carry[...] = jnp.zeros_like(carry)

        for j in range(NJ):
            uj = u_ref[:, gw * j:gw * (j + 1)]
            _strided_put(sr_ref, j, _dot(uj, bdr[j]))
            _strided_put(si_ref, j, _dot(uj, bdi[j]))
        lr = [lr_ref[q] for q in range(NQ)]
        li = [li_ref[q] for q in range(NQ)]

        def step(t, s):
            sr, si = s
            rows = pl.ds(pl.multiple_of(t * NJ, NJ), NJ)
            nr = tuple(lr[q] * sr[q] - li[q] * si[q] + sr_ref[q, rows, :] for q in range(NQ))
            ni = tuple(lr[q] * si[q] + li[q] * sr[q] + si_ref[q, rows, :] for q in range(NQ))
            for q in range(NQ):
                sr_ref[q, rows, :] = nr[q]
                si_ref[q, rows, :] = ni[q]
            return nr, ni

        init = (tuple(carry[0, q] for q in range(NQ)), tuple(carry[1, q] for q in range(NQ)))
        sr, si = lax.fori_loop(0, SSM_T, step, init, unroll=8)
        for q in range(NQ):
            carry[0, q] = sr[q]
            carry[1, q] = si[q]
        uf = jnp.concatenate([u0_ref[...], u1_ref[...]], axis=1)
        for j in range(NJ):
            cols = slice(gw * j, gw * (j + 1))
            yj = _dot(_strided_get(sr_ref, j).astype(bf16), cdr[j]) - _dot(_strided_get(si_ref, j).astype(bf16), cdi[j])
            yj = yj + d_ref[:, cols] * uf[:, cols]
            y_ref[:, cols] = yj
            z_ref[:, cols] = gelu(yj).astype(bf16)

    s_shape = jax.ShapeDtypeStruct((NQ, L * NJ, LANES), f32)
    consts = (bd_re, bd_im, cd_re, cd_im, lam_re, lam_im, d_skip)
    in_specs, operands, _ = _with_dep([row(d_ssm), row(half, u_cb), row(half, u_cb + 1)] + [whole(a) for a in consts],
                                      [u_bf, proj, proj, *consts], dep)
    return pl.pallas_call(
        body, out_shape=[jax.ShapeDtypeStruct((L, d_ssm), f32), jax.ShapeDtypeStruct((L, d_ssm), bf16), s_shape, s_shape], grid=(nt,),
        in_specs=in_specs, out_specs=[row(d_ssm), row(d_ssm), state, state],
        scratch_shapes=[pltpu.VMEM((2, NQ, NJ, LANES), f32)], name="ssm_fwd",
        compiler_params=_params("arbitrary"))(*operands)


def ssm_bwd(y, dz1, dz2, u_bf, proj, u_cb, s_re, s_im, bd_re, bd_im, cd_re, cd_im, lam_re, lam_im, d_skip, dep=None):
    L, d_ssm = y.shape
    nt, row, state, whole = _ssm_specs(L, True)
    half = d_ssm // 2
    gw = d_ssm // NJ
    n_dep = 0 if dep is None else 1

    def body(y_ref, dz1_ref, dz2_ref, u_ref, u0_ref, u1_ref, sr_ref, si_ref, bdr, bdi, cdr, cdi, lr_ref, li_ref, d_ref, *rest):
        du_ref, dbdr, dbdi, dcdr, dcdi, dlr, dli, dd_ref, gr_ref, gi_ref, carry = rest[n_dep:]
        i = pl.program_id(0)

        @pl.when(i == 0)
        def _():
            carry[...] = jnp.zeros_like(carry)
            for r in (dbdr, dbdi, dcdr, dcdi, dlr, dli, dd_ref):
                r[...] = jnp.zeros_like(r)

        dyf = (dz1_ref[...] + dz2_ref[...]) * gelu_grad(y_ref[...])
        dyb = dyf.astype(bf16)
        for j in range(NJ):
            dyj = dyb[:, gw * j:gw * (j + 1)]
            _strided_put(gr_ref, j, _dot(dyj, cdr[j], NT))
            _strided_put(gi_ref, j, -_dot(dyj, cdi[j], NT))
            dcdr[j] += _dot(_strided_get(sr_ref, j).astype(bf16), dyj, TN)
            dcdi[j] -= _dot(_strided_get(si_ref, j).astype(bf16), dyj, TN)
        lr = [lr_ref[q] for q in range(NQ)]
        li = [li_ref[q] for q in range(NQ)]

        def step(k, c):
            gr, gi, ar, ai = c
            rows = pl.ds(pl.multiple_of((SSM_T - 1 - k) * NJ, NJ), NJ)
            s_r = [sr_ref[q, rows, :] for q in range(NQ)]
            s_i = [si_ref[q, rows, :] for q in range(NQ)]
            ar = tuple(ar[q] + gr[q] * s_r[q] + gi[q] * s_i[q] for q in range(NQ))
            ai = tuple(ai[q] + gi[q] * s_r[q] - gr[q] * s_i[q] for q in range(NQ))
            nr = tuple(gr_ref[q, rows, :] + lr[q] * gr[q] + li[q] * gi[q] for q in range(NQ))
            ni = tuple(gi_ref[q, rows, :] + lr[q] * gi[q] - li[q] * gr[q] for q in range(NQ))
            for q in range(NQ):
                gr_ref[q, rows, :] = nr[q]
                gi_ref[q, rows, :] = ni[q]
            return nr, ni, ar, ai

        zero = tuple(jnp.zeros((NJ, LANES), f32) for _ in range(NQ))
        init = (tuple(carry[0, q] for q in range(NQ)), tuple(carry[1, q] for q in range(NQ)), zero, zero)
        gr, gi, ar, ai = lax.fori_loop(0, SSM_T, step, init, unroll=8)
        for q in range(NQ):
            carry[0, q] = gr[q]
            carry[1, q] = gi[q]
            dlr[q] += ar[q]
            dli[q] += ai[q]
        uf = jnp.concatenate([u0_ref[...], u1_ref[...]], axis=1)
        dd_ref[...] += colsum(dyf * uf)
        for j in range(NJ):
            cols = slice(gw * j, gw * (j + 1))
            gjr, gji = _strided_get(gr_ref, j).astype(bf16), _strided_get(gi_ref, j).astype(bf16)
            du_ref[:, cols] = _dot(gjr, bdr[j], NT) + _dot(gji, bdi[j], NT) + d_ref[:, cols] * dyf[:, cols]
            uj = u_ref[:, cols]
            dbdr[j] += _dot(uj, gjr, TN)
            dbdi[j] += _dot(uj, gji, TN)

    consts = (bd_re, bd_im, cd_re, cd_im, lam_re, lam_im, d_skip)
    acc = lambda a: jax.ShapeDtypeStruct(a.shape, f32)
    outs = [jax.ShapeDtypeStruct((L, d_ssm), f32), acc(bd_re), acc(bd_im), acc(cd_re), acc(cd_im), acc(lam_re), acc(lam_im), acc(d_skip)]
    in_specs, operands, _ = _with_dep(
        [row(d_ssm)] * 4 + [row(half, u_cb), row(half, u_cb + 1), state, state] + [whole(a) for a in consts],
        [y, dz1, dz2, u_bf, proj, proj, s_re, s_im, *consts], dep)
    return pl.pallas_call(
        body, out_shape=outs, grid=(nt,),
        in_specs=in_specs, out_specs=[row(d_ssm)] + [whole(a) for a in consts],
        scratch_shapes=[pltpu.VMEM((NQ, SSM_T * NJ, LANES), f32), pltpu.VMEM((NQ, SSM_T * NJ, LANES), f32),
                        pltpu.VMEM((2, NQ, NJ, LANES), f32)],
        name="ssm_bwd", compiler_params=_params("arbitrary"))(*operands)


def _cmul(ar, ai, br, bi):
    return ar * br - ai * bi, ar * bi + ai * br


def _disc(ar, ai, logdt):
    dt = jnp.exp(logdt)
    mag = jnp.exp(ar * dt)
    lr, li = mag * jnp.cos(ai * dt), mag * jnp.sin(ai * dt)
    den = ar * ar + ai * ai
    nr, ni = lr - 1.0, li
    fr, fi = (nr * ar + ni * ai) / den, (ni * ar - nr * ai) / den
    return dt, lr, li, den, fr, fi


def ssm_params(a_re, a_im, logdt_b, bt_re, bt_im, spread):
    def body(ar_ref, ai_ref, ld_ref, br_ref, bi_ref, sp_ref, lr_ref, li_ref, or_ref, oi_ref):
        _, lr, li, _, fr, fi = _disc(ar_ref[...], ai_ref[...], ld_ref[...])
        lr_ref[...] = lr
        li_ref[...] = li
        fre = jnp.dot(sp_ref[...], fr, precision=HIGHEST, preferred_element_type=f32)
        fie = jnp.dot(sp_ref[...], fi, precision=HIGHEST, preferred_element_type=f32)
        o_r, o_i = _cmul(fre, fie, br_ref[...], bi_ref[...])
        or_ref[...] = o_r
        oi_ref[...] = o_i

    g = jax.ShapeDtypeStruct(a_re.shape, f32)
    b = jax.ShapeDtypeStruct(bt_re.shape, f32)
    return pl.pallas_call(body, out_shape=[g, g, b, b], name="ssm_params",
                          compiler_params=_params())(a_re, a_im, logdt_b, bt_re, bt_im, spread)


def ssm_params_grad(a_re, a_im, logdt_b, bt_re, bt_im, spread, gather, dlam_re, dlam_im, dbt_re, dbt_im):
    def body(ar_ref, ai_ref, ld_ref, br_ref, bi_ref, sp_ref, ga_ref, glr_ref, gli_ref, gbr_ref, gbi_ref,
             dar_ref, dai_ref, dld_ref, dbr_ref, dbi_ref):
        ar, ai = ar_ref[...], ai_ref[...]
        dt, lr, li, den, fr, fi = _disc(ar, ai, ld_ref[...])
        hdot = functools.partial(jnp.dot, precision=HIGHEST, preferred_element_type=f32)
        fre, fie = hdot(sp_ref[...], fr), hdot(sp_ref[...], fi)
        gbr, gbi, br, bi = gbr_ref[...], gbi_ref[...], br_ref[...], bi_ref[...]
        dbr_ref[...], dbi_ref[...] = _cmul(fre, -fie, gbr, gbi)
        t_r, t_i = _cmul(br, -bi, gbr, gbi)
        gfr, gfi = hdot(ga_ref[...], t_r), hdot(ga_ref[...], t_i)
        iwr, iwi = ar / den, -ai / den
        x_r, x_i = _cmul(iwr, -iwi, gfr, gfi)
        glr, gli = glr_ref[...] + x_r, gli_ref[...] + x_i
        q_r, q_i = _cmul(fr, fi, iwr, iwi)
        gwr, gwi = _cmul(-q_r, q_i, gfr, gfi)
        y_r, y_i = _cmul(dt * lr, -dt * li, glr, gli)
        dar_ref[...] = gwr + y_r
        dai_ref[...] = gwi + y_i
        wl_r, wl_i = _cmul(ar, ai, lr, li)
        z_r, _ = _cmul(wl_r, -wl_i, glr, gli)
        dld_ref[...] = jnp.sum(z_r * dt, axis=1, keepdims=True)

    g = jax.ShapeDtypeStruct(a_re.shape, f32)
    b = jax.ShapeDtypeStruct(bt_re.shape, f32)
    return pl.pallas_call(body, out_shape=[g, g, jax.ShapeDtypeStruct((a_re.shape[0], 1), f32), b, b], name="ssm_params_grad",
                          compiler_params=_params())(a_re, a_im, logdt_b, bt_re, bt_im, spread, gather, dlam_re, dlam_im, dbt_re, dbt_im)


def _block_diag(t, rows, cols):
    G = t.shape[0]
    t = t.reshape(G // NJ, NJ, rows, cols)
    eye = jnp.eye(NJ, dtype=t.dtype)
    return jnp.einsum('jgrc,gh->jgrhc', t, eye).reshape(G // NJ, NJ * rows, NJ * cols)


def _block_diag_take(m, rows, cols):
    J = m.shape[0]
    m = m.reshape(J, NJ, rows, NJ, cols)
    idx = jnp.arange(NJ)
    return m[:, idx, :, idx, :].transpose(1, 0, 2, 3).reshape(J * NJ, rows, cols)


def _state_layout(t):
    return t.reshape(NJ, NQ, LANES).transpose(1, 0, 2)


def _state_layout_inv(t, G, N):
    return t.transpose(1, 0, 2).reshape(G, N)


def _tiles2d(shape, budget_rows=128):
    rows, cols = shape
    tr = rows
    if rows > budget_rows:
        tr = budget_rows
        while rows % tr:
            tr -= SUBLANES
    return tr, cols


def _adam_update(w, g, m, v):
    c1 = 1.0 - ADAM_B1 ** ADAM_STEP
    c2 = 1.0 - ADAM_B2 ** ADAM_STEP
    nm = ADAM_B1 * m + (1.0 - ADAM_B1) * g
    nv = ADAM_B2 * v + (1.0 - ADAM_B2) * (g * g)
    delta = -ADAM_LR * ((nm / c1) / (jnp.sqrt(nv / c2) + ADAM_EPS) + ADAM_WD * w)
    return delta, nm, nv


def adamw_many(name, ws, gs, ms, vs):
    n = len(ws)

    def body(*refs):
        w, g, m, v = (refs[k * n:(k + 1) * n] for k in range(4))
        d, nm, nv = (refs[(4 + k) * n:(5 + k) * n] for k in range(3))
        for i in range(n):
            d[i][...], nm[i][...], nv[i][...] = _adam_update(w[i][...], g[i][...], m[i][...], v[i][...])

    o = [jax.ShapeDtypeStruct(a.shape, f32) for a in ws]
    outs = pl.pallas_call(body, out_shape=o * 3, name=name, compiler_params=_params())(*ws, *gs, *ms, *vs)
    return outs[:n], outs[n:2 * n], outs[2 * n:]


def adamw_halves(name, w, own, got, m, v, c_arr):
    h, cols = own.shape
    tr, _ = _tiles2d((h, cols), 128 if cols > 1024 else 256)
    nh = h // tr

    def body(c_ref, w_ref, own_ref, got_ref, m_ref, v_ref, g_ref, d_ref, nm_ref, nv_ref):
        mine = (pl.program_id(0) // nh) == c_ref[0]
        g = jnp.where(mine, own_ref[...], got_ref[...])
        g_ref[...] = g
        d_ref[...], nm_ref[...], nv_ref[...] = _adam_update(w_ref[...], g, m_ref[...], v_ref[...])

    spec = pl.BlockSpec((tr, cols), lambda i, c: (i, 0))
    own_spec = pl.BlockSpec((tr, cols), lambda i, c: (jnp.where(i // nh == c[0], i % nh, 0), 0))
    got_spec = pl.BlockSpec((tr, cols), lambda i, c: (jnp.where(i // nh == c[0], 0, i % nh), 0))
    o = jax.ShapeDtypeStruct(w.shape, f32)
    grid_spec = pltpu.PrefetchScalarGridSpec(num_scalar_prefetch=1, grid=(2 * nh,),
                                             in_specs=[spec, own_spec, got_spec, spec, spec], out_specs=[spec] * 4)
    return pl.pallas_call(body, out_shape=[o, o, o, o], grid_spec=grid_spec, name=name,
                          compiler_params=_params("arbitrary"))(c_arr, w, own, got, m, v)


def pair_sum(name, g, got, c_arr):
    S, h, cols = got.shape
    tr, _ = _tiles2d((h, cols), 1024)
    nh = h // tr

    def body(c_ref, g_ref, o_ref, out_ref):
        out_ref[...] = (g_ref[...].astype(f32) + o_ref[...].astype(f32)).astype(out_ref.dtype)

    spec = pl.BlockSpec((None, tr, cols), lambda s, i, c: (s, i, 0))
    grid_spec = pltpu.PrefetchScalarGridSpec(
        num_scalar_prefetch=1, grid=(S, nh),
        in_specs=[pl.BlockSpec((None, tr, cols), lambda s, i, c: (s, c[0] * nh + i, 0)), spec], out_specs=spec)
    return pl.pallas_call(body, out_shape=jax.ShapeDtypeStruct(got.shape, g.dtype), grid_spec=grid_spec, name=name,
                          compiler_params=_params("parallel", "parallel"))(c_arr, g, got)


def chip_sum(name, pair, landed, mine_arr, dep=None):
    n_in, h, cols = landed.shape
    tr, _ = _tiles2d((h, cols), 512)

    def body(s_ref, p_ref, l_ref, *rest):
        acc = p_ref[...].astype(f32)
        for k in range(n_in):
            acc = acc + l_ref[k].astype(f32)
        rest[-1][...] = acc

    in_specs, operands, _ = _with_dep(
        [pl.BlockSpec((None, tr, cols), lambda i, s: (s[0], i, 0)), pl.BlockSpec((n_in, tr, cols), lambda i, s: (0, i, 0))],
        [pair, landed], dep)
    grid_spec = pltpu.PrefetchScalarGridSpec(num_scalar_prefetch=1, grid=(h // tr,), in_specs=in_specs,
                                             out_specs=pl.BlockSpec((tr, cols), lambda i, s: (i, 0)))
    return pl.pallas_call(body, out_shape=jax.ShapeDtypeStruct((h, cols), f32), grid_spec=grid_spec, name=name,
                          compiler_params=_params("parallel"))(mine_arr, *operands)


def into_slot(name, w, slot_arr, n_slots, dtype, dep=None):
    tr, cols = _tiles2d(w.shape, 256)

    def body(s_ref, w_ref, *rest):
        rest[-1][...] = w_ref[...].astype(dtype)

    in_specs, operands, _ = _with_dep([pl.BlockSpec((tr, cols), lambda i, s: (i, 0))], [w], dep)
    grid_spec = pltpu.PrefetchScalarGridSpec(num_scalar_prefetch=1, grid=(w.shape[0] // tr,), in_specs=in_specs,
                                             out_specs=pl.BlockSpec((None, tr, cols), lambda i, s: (s[0], i, 0)))
    return pl.pallas_call(body, out_shape=jax.ShapeDtypeStruct((n_slots,) + w.shape, dtype), grid_spec=grid_spec, name=name,
                          compiler_params=_params("parallel"))(slot_arr, *operands)


def sum_slots(name, t):
    S, rows, cols = t.shape
    tr, _ = _tiles2d((rows, cols), 256)

    def body(t_ref, o_ref):
        acc = t_ref[0]
        for s in range(1, S):
            acc = acc + t_ref[s]
        o_ref[...] = acc

    return pl.pallas_call(body, out_shape=jax.ShapeDtypeStruct((rows, cols), f32), grid=(rows // tr,),
                          in_specs=[pl.BlockSpec((S, tr, cols), lambda i: (0, i, 0))], out_specs=pl.BlockSpec((tr, cols), lambda i: (i, 0)),
                          name=name, compiler_params=_params("parallel"))(t)


def _place():
    x, y, c = lax.axis_index("x"), lax.axis_index("y"), lax.axis_index("c")
    return x, y, c


def _other_chips(x, y):
    return [(1 - x, y, 2 * (1 - x) + y), (x, 1 - y, 2 * x + 1 - y), (1 - x, 1 - y, 2 * (1 - x) + 1 - y)]


SEM = pl.BlockSpec(memory_space=pltpu.SEMAPHORE)
VM = pl.BlockSpec(memory_space=pltpu.VMEM)
DATAFLOW = pltpu.SideEffectType.DATAFLOW_SIDE_EFFECTING
TOKEN = jax.ShapeDtypeStruct((SUBLANES, LANES), f32)


def _gather_copy(buf, w, k, chip, c, mine, send, recv):
    px, py, _ = chip
    h = buf.shape[1] // 2
    half = buf.at[mine, pl.ds(c * h, h), :]
    return pltpu.make_async_remote_copy(src_ref=half, dst_ref=half, send_sem=send.at[3 * w + k], recv_sem=recv.at[3 * w + k],
                                        device_id=(px, py, c), device_id_type=MESH)


def _gather_landing(buf, w, k, chip, c, send, recv):
    px, py, s = chip
    h = buf.shape[1] // 2
    landed = buf.at[s, pl.ds(c * h, h), :]
    return pltpu.make_async_remote_copy(src_ref=landed, dst_ref=landed, send_sem=send.at[3 * w + k], recv_sem=recv.at[3 * w + k],
                                        device_id=(px, py, c), device_id_type=MESH)


def gather_start(name, bufs, groups, after, neighbours_only=()):
    nw, ng = len(bufs), len(groups)

    def body(*refs):
        outs = refs[nw + 1:]
        sems, dst = outs[:2 * ng], outs[2 * ng:2 * ng + nw]
        token = outs[2 * ng + nw]
        x, y, c = _place()
        mine = 2 * x + y
        for g, members in enumerate(groups):
            for i, w in enumerate(members):
                for k, chip in enumerate(_other_chips(x, y)[:2 if w in neighbours_only else 3]):
                    _gather_copy(dst[w], i, k, chip, c, mine, sems[2 * g], sems[2 * g + 1]).start()
        token[...] = jnp.zeros_like(token)

    sem_shapes = []
    for members in groups:
        sem_shapes += [pltpu.SemaphoreType.DMA((3 * len(members),))] * 2
    outs = pl.pallas_call(
        body, out_shape=sem_shapes + [jax.ShapeDtypeStruct(b.shape, b.dtype) for b in bufs] + [TOKEN],
        in_specs=[ANY] * (nw + 1), out_specs=[SEM] * (2 * ng) + [ANY] * nw + [VM],
        input_output_aliases={w: 2 * ng + w for w in range(nw)}, name=name,
        compiler_params=pltpu.CompilerParams(has_side_effects=DATAFLOW))(*bufs, after)
    return [(outs[2 * g], outs[2 * g + 1]) for g in range(ng)], list(outs[2 * ng:2 * ng + nw]), outs[2 * ng + nw]


def gather_wait(name, bufs, send, recv, after):
    nw = len(bufs)

    def body(*refs):
        src = refs[:nw]
        send_ref, recv_ref = refs[nw], refs[nw + 1]
        x, y, c = _place()
        mine = 2 * x + y
        for w in range(nw):
            for k, chip in enumerate(_other_chips(x, y)):
                _gather_copy(src[w], w, k, chip, c, mine, send_ref, recv_ref).wait_send()
                _gather_landing(src[w], w, k, chip, c, send_ref, recv_ref).wait_recv()

    return pl.pallas_call(
        body, out_shape=[jax.ShapeDtypeStruct(b.shape, b.dtype) for b in bufs],
        in_specs=[ANY] * nw + [SEM, SEM, ANY], out_specs=[ANY] * nw,
        input_output_aliases={w: w for w in range(nw)}, name=name,
        compiler_params=pltpu.CompilerParams(has_side_effects=DATAFLOW))(*bufs, send, recv, after)


def _relay_copy(buf, w, j, x, y, c, send, recv, landing):
    chips = _other_chips(x, y)
    px, py, _ = chips[j]
    h = buf.shape[1] // 2
    q = h // 2
    s = chips[2][2] if landing else chips[1 - j][2]
    part = buf.at[s, pl.ds(c * h + j * q, q), :]
    return pltpu.make_async_remote_copy(src_ref=part, dst_ref=part, send_sem=send.at[2 * w + j], recv_sem=recv.at[2 * w + j],
                                        device_id=(px, py, c), device_id_type=MESH)


def _early_pass(buf, nw, w, k, x, y, c, send, recv, landing):
    s = _other_chips(x, y)[k][2]
    h = buf.shape[1] // 2
    part = buf.at[s, pl.ds(((1 - c) if landing else c) * h, h), :]
    i = 2 * nw + 2 * w + k
    return pltpu.make_async_remote_copy(src_ref=part, dst_ref=part, send_sem=send.at[i], recv_sem=recv.at[i],
                                        device_id=(x, y, 1 - c), device_id_type=MESH)


def gather_relay(name, bufs, sems, more, after):
    nw, nm = len(bufs), len(more)
    ns = 4 if nm else 2

    def body(*refs):
        ins, outs = refs[:nw + nm + 2 * nw + 1], refs[nw + nm + 2 * nw + 1:]
        src, d_sems = ins[:nw], ins[nw + nm:nw + nm + 2 * nw]
        r_send, r_recv = outs[:2]
        m_send, m_recv = outs[2:ns] if nm else (None, None)
        dst, mdst, token = outs[ns:ns + nw], outs[ns + nw:ns + nw + nm], outs[ns + nw + nm]
        x, y, c = _place()
        mine = 2 * x + y
        chips = _other_chips(x, y)
        for w in range(nw):
            for k in range(2):
                _gather_copy(src[w], 0, k, chips[k], c, mine, d_sems[2 * w], d_sems[2 * w + 1]).wait_send()
                _gather_landing(src[w], 0, k, chips[k], c, d_sems[2 * w], d_sems[2 * w + 1]).wait_recv()
            for j in range(2):
                _relay_copy(dst[w], w, j, x, y, c, r_send, r_recv, False).start()
            for k in range(2):
                _early_pass(dst[w], nw, w, k, x, y, c, r_send, r_recv, False).start()
        for w in range(nm):
            for k, chip in enumerate(chips):
                _gather_copy(mdst[w], w, k, chip, c, mine, m_send, m_recv).start()
        token[...] = jnp.zeros_like(token)

    sem_shapes = [pltpu.SemaphoreType.DMA((4 * nw,))] * 2 + [pltpu.SemaphoreType.DMA((3 * nm,))] * (ns - 2)
    flat_sems = [s for pair in sems for s in pair]
    outs = pl.pallas_call(
        body, out_shape=sem_shapes + [jax.ShapeDtypeStruct(b.shape, b.dtype) for b in list(bufs) + list(more)] + [TOKEN],
        in_specs=[ANY] * (nw + nm) + [SEM] * (2 * nw) + [ANY], out_specs=[SEM] * ns + [ANY] * (nw + nm) + [VM],
        input_output_aliases={i: ns + i for i in range(nw + nm)}, name=name,
        compiler_params=pltpu.CompilerParams(has_side_effects=DATAFLOW))(*bufs, *more, *flat_sems, after)
    m_sems = (outs[2], outs[3]) if nm else None
    return outs[0], outs[1], m_sems, list(outs[ns:ns + nw]), list(outs[ns + nw:ns + nw + nm]), outs[ns + nw + nm]


def gather_wait_relay(name, bufs, r_send, r_recv, after):
    nw = len(bufs)

    def body(*refs):
        src = refs[:nw]
        send_ref, recv_ref = refs[nw], refs[nw + 1]
        x, y, c = _place()
        for w in range(nw):
            for j in range(2):
                _relay_copy(src[w], w, j, x, y, c, send_ref, recv_ref, False).wait_send()
                _relay_copy(src[w], w, j, x, y, c, send_ref, recv_ref, True).wait_recv()
                _early_pass(src[w], nw, w, j, x, y, c, send_ref, recv_ref, False).wait_send()
                _early_pass(src[w], nw, w, j, x, y, c, send_ref, recv_ref, True).wait_recv()

    return pl.pallas_call(
        body, out_shape=[jax.ShapeDtypeStruct(b.shape, b.dtype) for b in bufs],
        in_specs=[ANY] * nw + [SEM, SEM, ANY], out_specs=[ANY] * nw,
        input_output_aliases={w: w for w in range(nw)}, name=name,
        compiler_params=pltpu.CompilerParams(has_side_effects=DATAFLOW))(*bufs, r_send, r_recv, after)


def gather_forward(name, bufs, which=(0, 1, 2)):
    nw = len(bufs)

    def body(*refs):
        dst = refs[nw:2 * nw]
        send, recv = refs[2 * nw:]
        x, y, c = _place()
        sib = (x, y, 1 - c)
        cps = []
        for w in range(nw):
            h = dst[w].shape[1] // 2
            for k in which:
                s = _other_chips(x, y)[k][2]
                landed = dst[w].at[s, pl.ds(c * h, h), :]
                cp = pltpu.make_async_remote_copy(src_ref=landed, dst_ref=landed, send_sem=send.at[w, k], recv_sem=recv.at[w, k],
                                                  device_id=sib, device_id_type=MESH)
                cp.start()
                cps.append(cp)
        for w in range(nw):
            h = dst[w].shape[1] // 2
            for k in which:
                s = _other_chips(x, y)[k][2]
                other = dst[w].at[s, pl.ds((1 - c) * h, h), :]
                pltpu.make_async_remote_copy(src_ref=other, dst_ref=other, send_sem=send.at[w, k], recv_sem=recv.at[w, k],
                                             device_id=sib, device_id_type=MESH).wait_recv()
        for cp in cps:
            cp.wait_send()

    sem = pltpu.SemaphoreType.DMA((nw, 3))
    return pl.pallas_call(
        body, out_shape=[jax.ShapeDtypeStruct(b.shape, b.dtype) for b in bufs],
        in_specs=[ANY] * nw, out_specs=[ANY] * nw, input_output_aliases={w: w for w in range(nw)},
        scratch_shapes=[sem, sem], name=name, compiler_params=pltpu.CompilerParams(has_side_effects=True))(*bufs)


def _scatter_copy(src, dst, w, k, chip, c, send, recv):
    px, py, s = chip
    return pltpu.make_async_remote_copy(src_ref=src.at[s], dst_ref=dst.at[k], send_sem=send.at[3 * w + k], recv_sem=recv.at[3 * w + k],
                                        device_id=(px, py, c), device_id_type=MESH)


def scatter_start(name, parts):
    nw = len(parts)
    lands = [pltpu.with_memory_space_constraint(lax.empty((N_CHIPS - 1,) + p.shape[1:], p.dtype), pltpu.HBM) for p in parts]

    def body(*refs):
        outs = refs[2 * nw:]
        send, recv = outs[0], outs[1]
        src, dst, token = outs[2:2 + nw], outs[2 + nw:2 + 2 * nw], outs[2 + 2 * nw]
        x, y, c = _place()
        for w in range(nw):
            for k, chip in enumerate(_other_chips(x, y)):
                _scatter_copy(src[w], dst[w], w, k, chip, c, send, recv).start()
        token[...] = jnp.zeros_like(token)

    sem = pltpu.SemaphoreType.DMA((3 * nw,))
    outs = pl.pallas_call(
        body, out_shape=[sem, sem] + [jax.ShapeDtypeStruct(p.shape, p.dtype) for p in parts]
        + [jax.ShapeDtypeStruct(l.shape, l.dtype) for l in lands] + [TOKEN],
        in_specs=[ANY] * (2 * nw), out_specs=[SEM, SEM] + [ANY] * (2 * nw) + [VM],
        input_output_aliases={i: 2 + i for i in range(2 * nw)}, name=name,
        compiler_params=pltpu.CompilerParams(has_side_effects=DATAFLOW))(*parts, *lands)
    return outs[0], outs[1], list(outs[2:2 + nw]), list(outs[2 + nw:2 + 2 * nw]), outs[2 + 2 * nw]


def scatter_wait(name, parts, lands, send, recv, after):
    nw = len(parts)

    def body(*refs):
        src, dst = refs[:nw], refs[nw:2 * nw]
        send_ref, recv_ref = refs[2 * nw], refs[2 * nw + 1]
        x, y, c = _place()
        for w in range(nw):
            for k, chip in enumerate(_other_chips(x, y)):
                cp = _scatter_copy(src[w], dst[w], w, k, chip, c, send_ref, recv_ref)
                cp.wait_send()
                cp.wait_recv()

    outs = pl.pallas_call(
        body, out_shape=[jax.ShapeDtypeStruct(a.shape, a.dtype) for a in list(parts) + list(lands)],
        in_specs=[ANY] * (2 * nw) + [SEM, SEM, ANY], out_specs=[ANY] * (2 * nw),
        input_output_aliases={i: i for i in range(2 * nw)}, name=name,
        compiler_params=pltpu.CompilerParams(has_side_effects=DATAFLOW))(*parts, *lands, send, recv, after)
    return list(outs[:nw]), list(outs[nw:])


SIBLING_PAIR = 0


def _sibling_copy(src, dst, w, c, half_rows, send, recv, sib):
    if half_rows:
        h = src.shape[1] // 2
        src = src.at[:, pl.ds((1 - c) * h, h), :]
    return pltpu.make_async_remote_copy(src_ref=src, dst_ref=dst, send_sem=send.at[w], recv_sem=recv.at[w],
                                        device_id=sib, device_id_type=MESH)


def _landing(shape, dtype):
    return pltpu.with_memory_space_constraint(lax.empty(shape, dtype), pltpu.HBM)


def sibling_start(name, srcs, half_rows):
    nw = len(srcs)
    lands = [_landing((s.shape[0], s.shape[1] // 2, s.shape[2]) if half_rows else s.shape, s.dtype) for s in srcs]

    def body(*refs):
        outs = refs[2 * nw:]
        send, recv = outs[0], outs[1]
        src, dst, token = outs[2:2 + nw], outs[2 + nw:2 + 2 * nw], outs[2 + 2 * nw]
        x, y, c = _place()
        barrier = pltpu.get_barrier_semaphore()
        pl.semaphore_signal(barrier, inc=1, device_id=(x, y, 1 - c), device_id_type=MESH)
        pl.semaphore_wait(barrier, 1)
        for w in range(nw):
            _sibling_copy(src[w], dst[w], w, c, half_rows, send, recv, (x, y, 1 - c)).start()
        token[...] = jnp.zeros_like(token)

    sem = pltpu.SemaphoreType.DMA((nw,))
    outs = pl.pallas_call(
        body, out_shape=[sem, sem] + [jax.ShapeDtypeStruct(a.shape, a.dtype) for a in list(srcs) + lands] + [TOKEN],
        in_specs=[ANY] * (2 * nw), out_specs=[SEM, SEM] + [ANY] * (2 * nw) + [VM],
        input_output_aliases={i: 2 + i for i in range(2 * nw)}, name=name,
        compiler_params=pltpu.CompilerParams(has_side_effects=DATAFLOW, collective_id=SIBLING_PAIR))(*srcs, *lands)
    return outs[0], outs[1], list(outs[2:2 + nw]), list(outs[2 + nw:2 + 2 * nw]), outs[2 + 2 * nw]


def sibling_wait(name, srcs, lands, send, recv, half_rows, after):
    nw = len(srcs)

    def body(*refs):
        src, dst = refs[:nw], refs[nw:2 * nw]
        send_ref, recv_ref = refs[2 * nw], refs[2 * nw + 1]
        x, y, c = _place()
        for w in range(nw):
            cp = _sibling_copy(src[w], dst[w], w, c, half_rows, send_ref, recv_ref, (x, y, 1 - c))
            cp.wait_send()
            cp.wait_recv()

    outs = pl.pallas_call(
        body, out_shape=[jax.ShapeDtypeStruct(a.shape, a.dtype) for a in list(srcs) + list(lands)],
        in_specs=[ANY] * (2 * nw) + [SEM, SEM, ANY], out_specs=[ANY] * (2 * nw),
        input_output_aliases={i: i for i in range(2 * nw)}, name=name,
        compiler_params=pltpu.CompilerParams(has_side_effects=DATAFLOW))(*srcs, *lands, send, recv, after)
    return list(outs[:nw]), list(outs[nw:])


def _peer(x, y, c, r):
    return (1 - x if r & 4 else x, 1 - y if r & 2 else y, 1 - c if r & 1 else c)


def _everyone_copy(buf, r, x, y, c, send, recv, landing):
    px, py, pc = _peer(x, y, c, r)
    slot = buf.at[4 * px + 2 * py + pc] if landing else buf.at[4 * x + 2 * y + c]
    return pltpu.make_async_remote_copy(src_ref=slot, dst_ref=slot, send_sem=send.at[r - 1], recv_sem=recv.at[r - 1],
                                        device_id=(px, py, pc), device_id_type=MESH)


def everyone_start(name, buf):
    def body(buf_in, send, recv, buf_ref, token):
        x, y, c = _place()
        for r in range(1, N_DEV):
            _everyone_copy(buf_ref, r, x, y, c, send, recv, False).start()
        token[...] = jnp.zeros_like(token)

    sem = pltpu.SemaphoreType.DMA((N_DEV - 1,))
    return pl.pallas_call(
        body, out_shape=[sem, sem, jax.ShapeDtypeStruct(buf.shape, buf.dtype), TOKEN],
        in_specs=[ANY], out_specs=[SEM, SEM, ANY, VM], input_output_aliases={0: 2}, name=name,
        compiler_params=pltpu.CompilerParams(has_side_effects=DATAFLOW))(buf)


def everyone_wait(name, buf, send, recv, after):
    def body(buf_ref, send_ref, recv_ref, after_ref, out_ref):
        x, y, c = _place()
        for r in range(1, N_DEV):
            _everyone_copy(buf_ref, r, x, y, c, send_ref, recv_ref, False).wait_send()
            _everyone_copy(buf_ref, r, x, y, c, send_ref, recv_ref, True).wait_recv()

    return pl.pallas_call(
        body, out_shape=jax.ShapeDtypeStruct(buf.shape, buf.dtype), in_specs=[ANY, SEM, SEM, ANY], out_specs=ANY,
        input_output_aliases={0: 0}, name=name,
        compiler_params=pltpu.CompilerParams(has_side_effects=DATAFLOW))(buf, send, recv, after)


def local_step(x, pos, tgt, small, d_in, get_w, put_g, first_dep=None, tick=lambda name, after: None):
    L, D = x.shape
    d_kv = N_KV_HEADS * HEAD_DIM
    d_ssm = small["d_skip"].shape[1]
    d_attn = d_in - 2 * d_kv - d_ssm
    big = {}
    G = d_ssm // SSM_GROUP
    N, P = SSM_STATE, SSM_GROUP
    gbf = bf16

    half_dim = HEAD_DIM // 2
    inv_freq = ROPE_THETA ** (-jnp.arange(half_dim, dtype=f32) / half_dim)
    inv_freq = jnp.tile(inv_freq, LANES // half_dim).reshape(1, LANES)
    sink_b = jnp.broadcast_to(small["sinks"].reshape(-1, 1), (small["sinks"].size, LANES))

    spread = jnp.repeat(jnp.eye(G, dtype=f32), P, axis=0)
    logdt_b = jnp.broadcast_to(small["log_dt"].reshape(G, 1), (G, N))
    bt_re = small["b_re"].reshape(G, N, P).transpose(0, 2, 1).reshape(G * P, N)
    bt_im = small["b_im"].reshape(G, N, P).transpose(0, 2, 1).reshape(G * P, N)
    a_re, a_im = small["a_re"].reshape(G, N), small["a_im"].reshape(G, N)
    lam_re, lam_im, bbt_re, bbt_im = ssm_params(a_re, a_im, logdt_b, bt_re, bt_im, spread)
    bd_re = _block_diag(bbt_re.reshape(G, P, N), P, N).astype(bf16)
    bd_im = _block_diag(bbt_im.reshape(G, P, N), P, N).astype(bf16)
    c_re = small["c_re"].reshape(G, P, N).transpose(0, 2, 1)
    c_im = small["c_im"].reshape(G, P, N).transpose(0, 2, 1)
    cd_re = _block_diag(c_re, N, P).astype(bf16)
    cd_im = _block_diag(c_im, N, P).astype(bf16)
    lam_re_l, lam_im_l = _state_layout(lam_re), _state_layout(lam_im)

    def k1(i, nt, xt, g):
        return (rms_fwd(xt, g),)
    xn = rowwise("pre_mix_norm", k1, L, [full(x)], [small["g_pre_mix"]], [(D, bf16)], dep=first_dep)[0]
    big["w_in"] = get_w("w_in", xn)
    proj = mm_nn("proj_in", xn, big["w_in"])
    qr, kk, vv, u_bf = qkv_prep(proj, pos, inv_freq, d_attn, d_kv)
    attn = attn_fwd(qr, kk, vv, sink_b)
    u_cb = (d_attn + 2 * d_kv) // (d_ssm // 2)
    token = tick("attn", attn)
    y, z_bf, s_re, s_im = ssm_fwd(u_bf, proj, u_cb, bd_re, bd_im, cd_re, cd_im, lam_re_l, lam_im_l, small["d_skip"], dep=token)
    token = tick("ssm", z_bf)
    big["w_glu"] = get_w("w_glu", z_bf)
    gl = mm_nn("glu_proj", z_bf, big["w_glu"], dep=token)

    def k6(i, nt, at, yt, glt, bg, ga, gs):
        ssm = gelu(yt) * sigmoid(glt + bg)
        return (jnp.concatenate([rms_fwd(at, ga), rms_fwd(ssm, gs)], axis=1),)
    mixed = rowwise("mix_norms", k6, L, [full(attn), full(y), full(gl)],
                    [small["b_glu"], small["g_attn_out"], small["g_ssm_out"]], [(d_attn + d_ssm, bf16)])[0]
    big["w_o"] = get_w("w_o", mixed)
    mix = mm_nn("proj_out", mixed, big["w_o"])

    def k7(i, nt, xt, mt, gpm, gpf):
        h = xt + rms_fwd(mt, gpm)
        return h, rms_fwd(h, gpf)
    h, hn = rowwise("post_mix", k7, L, [full(x), full(mix)], [small["g_post_mix"], small["g_pre_ffn"]], [(D, f32), (D, bf16)])
    big["w_gate"] = get_w("w_gate", hn)
    big["w_up"] = get_w("w_up", hn)
    gt, up, hid = ffn_hidden(hn, big["w_gate"], big["w_up"])
    d_ff_dim = gt.shape[1]
    big["w_down"] = get_w("w_down", hid)
    ff = mm_nn("ffn_down", hid, big["w_down"], tk=d_ff_dim // 2)

    def k9(i, nt, ht, fft, tt, g):
        out = ht + rms_fwd(fft, g)
        err = out - tt
        per_row = jnp.mean(err * err, axis=-1, keepdims=True)
        loss = 0.5 * jnp.sum(per_row) * jnp.where(_lane((1, LANES)) == 0, 1.0, 0.0)
        d_out = err * (1.0 / D)
        d_ff, dg = rms_bwd(fft, g, d_out)
        return d_out, d_ff, dg, loss
    d_out, d_ff, dg_post_ffn, loss = rowwise("loss_head", k9, L, [full(h), full(ff), full(tgt)], [small["g_post_ffn"]],
                                             [(D, f32), (D, bf16)], reds=[D, LANES])

    d_gt, d_up = ffn_hidden_grad(d_ff, big["w_down"], gt, up)
    token = put_g("w_down", mm_tn("dw_down", hid, d_ff, out_dtype=gbf, tm=d_ff_dim // N_CHIPS))
    d_hn = mm_nt_pair("d_hn", d_gt, big["w_gate"], d_up, big["w_up"], dep=token)
    token = put_g("w_gate", mm_tn("dw_gate", hn, d_gt, shards=N_CHIPS, out_dtype=gbf))
    token = put_g("w_up", mm_tn("dw_up", hn, d_up, shards=N_CHIPS, out_dtype=gbf, dep=token))

    def k11(i, nt, ht, da, do, mt, gpf, gpm):
        dh_n, dg_pf = rms_bwd(ht, gpf, da)
        dh = do + dh_n
        d_mix, dg_pm = rms_bwd(mt, gpm, dh)
        return dh, d_mix, dg_pf, dg_pm
    dh, d_mix, dg_pre_ffn, dg_post_mix = rowwise("post_mix_grad", k11, L, [full(h), full(d_hn), full(d_out), full(mix)],
                                                 [small["g_pre_ffn"], small["g_post_mix"]], [(D, f32), (D, bf16)], reds=[D, D], dep=token)
    d_mixed = mm_nt("d_mixed", d_mix, big["w_o"])
    token = put_g("w_o", mm_tn("dw_o", mixed, d_mix, out_dtype=gbf))

    def k12(i, nt, at, yt, glt, da_n, ds_n, bg, ga, gs):
        z = gelu(yt)
        sg = sigmoid(glt + bg)
        ssm = z * sg
        d_at, dga = rms_bwd(at, ga, da_n)
        d_ssm_t, dgs = rms_bwd(ssm, gs, ds_n)
        d_gl = d_ssm_t * z * sg * (1.0 - sg)
        return d_at, d_ssm_t * sg, d_gl, dga, dgs, colsum(d_gl)
    d_attn_o, dz1, d_gl, dg_attn, dg_ssm, db_glu = rowwise(
        "mix_norms_grad", k12, L, [full(attn), full(y), full(gl), (d_mixed, d_attn, 0, 0), (d_mixed, d_ssm, d_attn // d_ssm, 0)],
        [small["b_glu"], small["g_attn_out"], small["g_ssm_out"]], [(d_attn, f32), (d_ssm, f32), (d_ssm, bf16)],
        reds=[d_attn, d_ssm, d_ssm], dep=token)
    dz2 = mm_nt("d_glu_in", d_gl, big["w_glu"])
    token = put_g("w_glu", mm_tn("dw_glu", z_bf, d_gl, out_dtype=gbf))

    du, dbd_re, dbd_im, dcd_re, dcd_im, dlam_re_l, dlam_im_l, dd_skip = ssm_bwd(
        y, dz1, dz2, u_bf, proj, u_cb, s_re, s_im, bd_re, bd_im, cd_re, cd_im, lam_re_l, lam_im_l, small["d_skip"], dep=token)
    dq, dkk_c, dkk_p, dvv_c, dvv_p, dsink = attn_bwd(qr, kk, vv, sink_b, attn, d_attn_o)
    d_proj = qkv_grad(dq, dkk_c, dkk_p, dvv_c, dvv_p, du, pos, inv_freq)
    d_xn = mm_nt("d_xn", d_proj, big["w_in"])
    token = put_g("w_in", mm_tn("dw_in", xn, d_proj, shards=N_CHIPS, out_dtype=gbf))

    def k17(i, nt, xt, dxn, dht, g):
        dx, dg = rms_bwd(xt, g, dxn)
        return dht + dx, dg
    grad_x, dg_pre_mix = rowwise("pre_mix_grad", k17, L, [full(x), full(d_xn), full(dh)], [small["g_pre_mix"]],
                                 [(D, f32)], reds=[D], dep=token)

    gather = spread.T
    dbbt_re = _block_diag_take(dbd_re, P, N).reshape(G * P, N)
    dbbt_im = _block_diag_take(dbd_im, P, N).reshape(G * P, N)
    d_a_re, d_a_im, d_logdt, dbt_re, dbt_im = ssm_params_grad(
        a_re, a_im, logdt_b, bt_re, bt_im, spread, gather,
        _state_layout_inv(dlam_re_l, G, N), _state_layout_inv(dlam_im_l, G, N), dbbt_re, dbbt_im)
    q_per_kv = d_attn // HEAD_DIM // N_KV_HEADS
    small_grads = {
        "g_pre_mix": dg_pre_mix, "sinks": dsink[:, :q_per_kv, 0].reshape(1, -1),
        "a_re": d_a_re, "a_im": d_a_im, "log_dt": d_logdt.reshape(1, G),
        "b_re": dbt_re, "b_im": dbt_im,
        "c_re": _block_diag_take(dcd_re, N, P).transpose(0, 2, 1), "c_im": _block_diag_take(dcd_im, N, P).transpose(0, 2, 1),
        "d_skip": dd_skip, "b_glu": db_glu, "g_attn_out": dg_attn, "g_ssm_out": dg_ssm,
        "g_post_mix": dg_post_mix, "g_pre_ffn": dg_pre_ffn, "g_post_ffn": dg_post_ffn,
    }
    return loss, grad_x, small_grads


WEIGHTS = ['g_pre_mix', 'w_in', 'sinks', 'a_re', 'a_im', 'log_dt', 'b_re', 'b_im', 'c_re', 'c_im', 'd_skip', 'w_glu', 'b_glu',
           'g_attn_out', 'g_ssm_out', 'w_o', 'g_post_mix', 'g_pre_ffn', 'w_gate', 'w_up', 'w_down', 'g_post_ffn']
BIG = ['w_in', 'w_glu', 'w_o', 'w_gate', 'w_up', 'w_down']
COL_SHARDED = ['w_in', 'w_gate', 'w_up']
SMALL = [n for n in WEIGHTS if n not in BIG]
GATHER_GROUPS = [["w_in"], ["w_glu", "w_o"], ["w_gate", "w_up"], ["w_down"]]
REDUCE_GROUPS = [["w_down", "w_gate", "w_up"], ["w_o", "w_glu", "w_in"]]


PACK_ROWS = 256


def _pack(parts):
    flat = jnp.concatenate([p.reshape(-1) for p in parts])
    pad = (-flat.size) % (PACK_ROWS * LANES)
    return jnp.pad(flat, (0, pad)).reshape(-1, LANES)


TRANSPOSED_VIEW = ("b_re", "b_im")


def small_view(name, a):
    if name in TRANSPOSED_VIEW:
        a = a.transpose(0, 1, 3, 2)
    return a.reshape(-1, a.shape[-1])


def small_unview(name, p, shape):
    if name in TRANSPOSED_VIEW:
        return p.reshape(shape[0], shape[1], shape[3], shape[2]).transpose(0, 1, 3, 2)
    return p.reshape(shape)


def _unpack(packed, shapes):
    flat = packed.reshape(-1)
    out, off = [], 0
    for s in shapes:
        n = int(np.prod(s))
        out.append(flat[off:off + n].reshape(s))
        off += n
    return out


def kernel(x, positions, g_pre_mix, w_in, sinks, a_re, a_im, log_dt, b_re, b_im, c_re, c_im, d_skip, w_glu, b_glu, g_attn_out, g_ssm_out, w_o, g_post_mix, g_pre_ffn, w_gate, w_up, w_down, g_post_ffn, loss_target, m_g_pre_mix, m_w_in, m_sinks, m_a_re, m_a_im, m_log_dt, m_b_re, m_b_im, m_c_re, m_c_im, m_d_skip, m_w_glu, m_b_glu, m_g_attn_out, m_g_ssm_out, m_w_o, m_g_post_mix, m_g_pre_ffn, m_w_gate, m_w_up, m_w_down, m_g_post_ffn, v_g_pre_mix, v_w_in, v_sinks, v_a_re, v_a_im, v_log_dt, v_b_re, v_b_im, v_c_re, v_c_im, v_d_skip, v_w_glu, v_b_glu, v_g_attn_out, v_g_ssm_out, v_w_o, v_g_post_mix, v_g_pre_ffn, v_w_gate, v_w_up, v_w_down, v_g_post_ffn):
    args = dict(locals())
    w = {n: args[n] for n in WEIGHTS}
    m = {n: args["m_" + n] for n in WEIGHTS}
    v = {n: args["v_" + n] for n in WEIGHTS}
    L, D = x.shape[1], x.shape[2]

    ax, ay, ac = _place()
    mine_arr = (2 * ax + ay).astype(jnp.int32).reshape(1)
    c_arr = ac.astype(jnp.int32).reshape(1)

    me_arr = (4 * ax + 2 * ay + ac).astype(jnp.int32).reshape(1)

    bufs = {"w_in": into_slot("cast_w_in", w["w_in"][0], mine_arr, N_CHIPS, bf16)}
    (first_sems,), (bufs["w_in"],), token = gather_start("gather_start_in", [bufs["w_in"]], [[0]], mine_arr)
    sems = {"w_in": first_sems}
    for n in BIG[1:]:
        bufs[n] = into_slot("cast_" + n, w[n][0], mine_arr, N_CHIPS, bf16, dep=token)
    first = ["w_glu", "w_o", "w_gate", "w_up"]
    (sems["w_glu"], sems["w_gate"], sems["w_up"]), started, token = gather_start(
        "gather_start_rest", [bufs[n] for n in first], [[0, 1], [2], [3]], token, neighbours_only=(2, 3))
    bufs.update(zip(first, started))
    relays, ready = {}, set()

    def tick(name, after):
        n, more = ("w_gate", []) if name == "attn" else ("w_up", ["w_down"])
        r_send, r_recv, sems["w_down"], (bufs[n],), down, tok = gather_relay(
            "gather_relay_" + n, [bufs[n]], [sems[n]], [bufs[k] for k in more], after)
        bufs.update(zip(more, down))
        relays[n] = (r_send, r_recv)
        return tok

    def get_w(n, after):
        if n not in ready:
            members = [g for g in GATHER_GROUPS if n in g][0]
            if members[0] in relays:
                landed = [gather_wait_relay("gather_wait_" + k, [bufs[k]], *relays[k], after)[0] for k in members]
                which = (2,)
            else:
                landed = gather_wait("gather_wait_" + members[0], [bufs[k] for k in members], *sems[members[0]], after)
                which = (0, 1, 2)
            bufs.update(zip(members, gather_forward("gather_forward_" + members[0], landed, which)))
            ready.update(members)
        g = bufs[n]
        return g if n in COL_SHARDED else g.reshape(g.shape[0] * g.shape[1], g.shape[2])

    swaps, inflight = {}, []

    def put_g(n, g):
        g3 = g if n in COL_SHARDED else g.reshape(N_CHIPS, g.shape[0] // N_CHIPS, g.shape[1])
        swaps[n] = sibling_start("swap_start_" + n, [g3], True)
        for gi, members in enumerate(REDUCE_GROUPS):
            if n == members[-1]:
                last = swaps[n][4]
                pair = []
                for k in members:
                    send, recv, srcs, lands, _ = swaps[k]
                    (src,), (got,) = sibling_wait("swap_wait_" + k, srcs, lands, send, recv, True, last)
                    pair.append(pair_sum("pair_sum_" + k, src, got, c_arr))
                send, recv, parts, lands, tok = scatter_start("scatter_start_%d" % gi, pair)
                inflight.append((members, send, recv, parts, lands))
                return tok
        return swaps[n][4]

    small = {n: w[n].reshape(1, -1) for n in SMALL}
    pos = positions.reshape(L, 1).astype(f32)
    d_in = N_CHIPS * w["w_in"].shape[2]
    loss, grad_x, small_grads = local_step(x[0], pos, loss_target[0], small, d_in, get_w, put_g, first_dep=token, tick=tick)

    shapes = [w[n].shape for n in SMALL]
    blocks = into_slot("small_block", _pack([small_grads[n] for n in SMALL] + [loss]), me_arr, N_DEV, f32)
    small_send, small_recv, blocks, after = everyone_start("small_start", blocks)

    grads, delta, new_m, new_v = {}, {}, {}, {}
    for gi, (members, send, recv, parts, lands) in enumerate(inflight):
        parts, landed = scatter_wait("scatter_wait_%d" % gi, parts, lands, send, recv, after)
        joins, dep = [], None
        for k, p, t in zip(members, parts, landed):
            joins.append(sibling_start("join_start_" + k, [chip_sum("chip_sum_" + k, p, t, mine_arr, dep=dep)], False))
            dep = after = joins[-1][4]
        for n, (send, recv, srcs, lands, _) in zip(members, joins):
            (own,), (sib,) = sibling_wait("join_wait_" + n, srcs, lands, send, recv, False, after)
            g_, d_, m_, v_ = adamw_halves("adamw_" + n, w[n][0], own, sib, m[n][0], v[n][0], c_arr)
            grads[n], delta[n], new_m[n], new_v[n] = g_[None], d_[None], m_[None], v_[None]
            after = v_
    blocks = everyone_wait("small_wait", blocks, small_send, small_recv, after)
    small_sum = sum_slots("small_sum", blocks)
    *small_g, loss = _unpack(small_sum, [small_view(n, w[n]).shape for n in SMALL] + [loss.shape])
    loss = loss[0, 0]
    outs = adamw_many("adamw_small", [small_view(n, w[n]) for n in SMALL], small_g,
                      [small_view(n, m[n]) for n in SMALL], [small_view(n, v[n]) for n in SMALL])
    for t, parts in zip((grads, delta, new_m, new_v), (small_g,) + tuple(outs)):
        t.update({n: small_unview(n, p, w[n].shape) for n, p in zip(SMALL, parts)})

    return (loss, grad_x[None], *[grads[n] for n in WEIGHTS], *[delta[n] for n in WEIGHTS],
            *[new_m[n] for n in WEIGHTS], *[new_v[n] for n in WEIGHTS])
```

```python
import functools
import math

import jax
import jax.numpy as jnp
import numpy as np
from jax import lax
from jax.experimental import pallas as pl
from jax.experimental.pallas import tpu as pltpu

f32 = jnp.float32
bf16 = jnp.bfloat16
HIGHEST = lax.Precision.HIGHEST
MESH = pl.DeviceIdType.MESH

HEAD_DIM = 64
N_KV_HEADS = 4
ATTN_BLOCK = 128
ROPE_THETA = 10000.0
SSM_GROUP = 16
SSM_STATE = 64
RMS_EPS = 1e-6
LANES = 128
SUBLANES = 8
VMEM_LIMIT = 52 * 1024 * 1024
N_CHIPS = 4
N_DEV = 8
NEG = -1e30

ADAM_LR, ADAM_B1, ADAM_B2, ADAM_EPS, ADAM_WD, ADAM_STEP = 0.001, 0.9, 0.999, 1e-08, 0.01, 10

NN = (((1,), (0,)), ((), ()))
NT = (((1,), (1,)), ((), ()))
TN = (((0,), (0,)), ((), ()))


def _params(*sem):
    return pltpu.CompilerParams(dimension_semantics=sem or None, vmem_limit_bytes=VMEM_LIMIT)


def _dot(a, b, dims=NN):
    return lax.dot_general(a, b, dims, preferred_element_type=f32)


def _pick(dim, pref):
    t = min(dim, pref)
    while dim % t:
        t -= LANES
    assert t > 0, (dim, pref)
    return t


ANY = pl.BlockSpec(memory_space=pl.ANY)


def _with_dep(in_specs, operands, dep):
    if dep is None:
        return list(in_specs), list(operands), 0
    return list(in_specs) + [ANY], list(operands) + [dep], 1


def _mm_call(name, grid, in_specs, out_spec, out_shape, acc_shape, dims, operands, dep=None):
    nk = grid[2]
    in_specs, operands, n_dep = _with_dep(in_specs, operands, dep)

    def body_one(a_ref, b_ref, *rest):
        o_ref = rest[n_dep]
        o_ref[...] = _dot(a_ref[...], b_ref[...], dims).astype(o_ref.dtype)

    def body(a_ref, b_ref, *rest):
        o_ref, acc_ref = rest[n_dep], rest[n_dep + 1]
        k = pl.program_id(2)

        @pl.when(k == 0)
        def _():
            acc_ref[...] = _dot(a_ref[...], b_ref[...], dims)

        @pl.when((k > 0) & (k < nk - 1))
        def _():
            acc_ref[...] += _dot(a_ref[...], b_ref[...], dims)

        @pl.when(k == nk - 1)
        def _():
            o_ref[...] = (acc_ref[...] + _dot(a_ref[...], b_ref[...], dims)).astype(o_ref.dtype)

    return pl.pallas_call(
        body_one if nk == 1 else body, out_shape=out_shape, grid=grid, in_specs=in_specs, out_specs=out_spec,
        scratch_shapes=[] if nk == 1 else [pltpu.VMEM(acc_shape, f32)], name=name,
        compiler_params=_params("parallel", "parallel", "arbitrary"))(*operands)


def mm_nt_pair(name, a1, b1, a2, b2, tm=1024, tk=1024, dep=None):
    M = a1.shape[0]
    S, K, n = b1.shape
    tm, tko = _pick(M, tm), _pick(K, tk)
    nk = 2 * S

    def body(a1_ref, b1_ref, a2_ref, b2_ref, *rest):
        o_ref, acc_ref = rest[-2], rest[-1]
        k = pl.program_id(2)

        @pl.when(k == 0)
        def _():
            acc_ref[...] = _dot(a1_ref[...], b1_ref[...], NT)

        @pl.when((k > 0) & (k < S))
        def _():
            acc_ref[...] += _dot(a1_ref[...], b1_ref[...], NT)

        @pl.when((k >= S) & (k < nk - 1))
        def _():
            acc_ref[...] += _dot(a2_ref[...], b2_ref[...], NT)

        @pl.when(k == nk - 1)
        def _():
            o_ref[...] = acc_ref[...] + _dot(a2_ref[...], b2_ref[...], NT)

    first = lambda k: jnp.minimum(k, S - 1)
    second = lambda k: jnp.maximum(k - S, 0)
    in_specs = [pl.BlockSpec((tm, n), lambda i, j, k: (i, first(k))), pl.BlockSpec((None, tko, n), lambda i, j, k: (first(k), j, 0)),
                pl.BlockSpec((tm, n), lambda i, j, k: (i, second(k))), pl.BlockSpec((None, tko, n), lambda i, j, k: (second(k), j, 0))]
    in_specs, operands, _ = _with_dep(in_specs, (a1, b1, a2, b2), dep)
    return pl.pallas_call(
        body, out_shape=jax.ShapeDtypeStruct((M, K), f32), grid=(M // tm, K // tko, nk), in_specs=in_specs,
        out_specs=pl.BlockSpec((tm, tko), lambda i, j, k: (i, j)), scratch_shapes=[pltpu.VMEM((tm, tko), f32)], name=name,
        compiler_params=_params("parallel", "parallel", "arbitrary"))(*operands)


def mm_nn(name, a, b, out_dtype=f32, tm=1024, tn=1024, tk=2048, dep=None):
    M, K = a.shape
    tm, tk = _pick(M, tm), _pick(K, tk)
    if b.ndim == 3:
        S, _, n = b.shape
        tn = _pick(n, 2048)
        per = n // tn
        b_spec = pl.BlockSpec((None, tk, tn), lambda i, j, k: (j // per, k, j % per))
        N = S * n
    else:
        N = b.shape[1]
        tn = _pick(N, tn)
        b_spec = pl.BlockSpec((tk, tn), lambda i, j, k: (k, j))
    grid = (M // tm, N // tn, K // tk)
    return _mm_call(name, grid, [pl.BlockSpec((tm, tk), lambda i, j, k: (i, k)), b_spec],
                    pl.BlockSpec((tm, tn), lambda i, j, k: (i, j)), jax.ShapeDtypeStruct((M, N), out_dtype),
                    (tm, tn), NN, (a, b), dep)


def mm_nt(name, a, b, out_dtype=f32, tm=1024, tn=2048, tk=1024, dep=None):
    M, N = a.shape
    tm = _pick(M, tm)
    if b.ndim == 3:
        S, K, n = b.shape
        tr = _pick(n, 2048)
        per = n // tr
        tko = _pick(K, tk)
        b_spec = pl.BlockSpec((None, tko, tr), lambda i, j, k: (k // per, j, k % per))
    else:
        K = b.shape[0]
        tr = _pick(N, tn)
        tko = _pick(K, tk)
        b_spec = pl.BlockSpec((tko, tr), lambda i, j, k: (j, k))
    grid = (M // tm, K // tko, N // tr)
    return _mm_call(name, grid, [pl.BlockSpec((tm, tr), lambda i, j, k: (i, k)), b_spec],
                    pl.BlockSpec((tm, tko), lambda i, j, k: (i, j)), jax.ShapeDtypeStruct((M, K), out_dtype),
                    (tm, tko), NT, (a, b), dep)


def mm_tn(name, a, b, shards=None, out_dtype=f32, tm=1024, tn=1024, tl=2048, dep=None):
    L, K = a.shape
    N = b.shape[1]
    tl, tko = _pick(L, tl), _pick(K, tm)
    if shards:
        n = N // shards
        tn = _pick(n, 2048)
        per = n // tn
        o_spec = pl.BlockSpec((None, tko, tn), lambda i, j, k: (j // per, i, j % per))
        o_shape = jax.ShapeDtypeStruct((shards, K, n), out_dtype)
    else:
        tn = _pick(N, tn)
        o_spec = pl.BlockSpec((tko, tn), lambda i, j, k: (i, j))
        o_shape = jax.ShapeDtypeStruct((K, N), out_dtype)
    grid = (K // tko, N // tn, L // tl)
    return _mm_call(name, grid, [pl.BlockSpec((tl, tko), lambda i, j, k: (k, i)),
                                 pl.BlockSpec((tl, tn), lambda i, j, k: (k, j))],
                    o_spec, o_shape, (tko, tn), TN, (a, b), dep)


def ffn_hidden(hn, w_gate, w_up, tm=512):
    M, K = hn.shape
    S, _, n = w_gate.shape
    tm = _pick(M, tm)

    def body(a_ref, g_ref, u_ref, gt_ref, up_ref, hid_ref):
        a = a_ref[...]
        g = _dot(a, g_ref[...])
        u = _dot(a, u_ref[...])
        gt_ref[...] = g.astype(bf16)
        up_ref[...] = u.astype(bf16)
        hid_ref[...] = (g * sigmoid(g) * u).astype(bf16)

    w_spec = pl.BlockSpec((None, K, n), lambda s, i: (s, 0, 0))
    o_spec = pl.BlockSpec((tm, n), lambda s, i: (i, s))
    o = jax.ShapeDtypeStruct((M, S * n), bf16)
    return pl.pallas_call(
        body, out_shape=[o, o, o], grid=(S, M // tm), in_specs=[pl.BlockSpec((tm, K), lambda s, i: (i, 0)), w_spec, w_spec],
        out_specs=[o_spec, o_spec, o_spec], name="ffn_hidden", compiler_params=_params("parallel", "parallel"))(hn, w_gate, w_up)


def ffn_hidden_grad(d_ff, w_down, gt, up, tm=512):
    M, D = d_ff.shape
    F = w_down.shape[0]
    n = _pick(F // N_CHIPS, 2048)
    tm = _pick(M, tm)

    def body(a_ref, b_ref, gt_ref, up_ref, dg_ref, du_ref):
        dh = _dot(a_ref[...], b_ref[...], NT)
        g = gt_ref[...].astype(f32)
        sg = sigmoid(g)
        dg_ref[...] = (dh * up_ref[...].astype(f32) * (sg * (1.0 + g * (1.0 - sg)))).astype(bf16)
        du_ref[...] = (dh * (g * sg)).astype(bf16)

    t_spec = pl.BlockSpec((tm, n), lambda j, i: (i, j))
    o = jax.ShapeDtypeStruct((M, F), bf16)
    return pl.pallas_call(
        body, out_shape=[o, o], grid=(F // n, M // tm),
        in_specs=[pl.BlockSpec((tm, D), lambda j, i: (i, 0)), pl.BlockSpec((n, D), lambda j, i: (j, 0)), t_spec, t_spec],
        out_specs=[t_spec, t_spec], name="ffn_hidden_grad", compiler_params=_params("parallel", "parallel"))(d_ff, w_down, gt, up)


def rowwise(name, fn, L, rows, bcast, outs, reds=(), tr=256, dep=None):
    tr = min(tr, L)
    nt = L // tr
    n_rows, n_b, n_o = len(rows), len(bcast), len(outs)
    n_dep = 0 if dep is None else 1

    def body(*refs):
        i = pl.program_id(0)
        ins = [r[...] for r in refs[:n_rows + n_b]]
        res = fn(i, nt, *ins)
        o_refs = refs[n_rows + n_b + n_dep:]
        for k in range(n_o):
            o_refs[k][...] = res[k].astype(o_refs[k].dtype)
        if reds:
            @pl.when(i == 0)
            def _():
                for k in range(len(reds)):
                    o_refs[n_o + k][...] = jnp.zeros_like(o_refs[n_o + k])
            for k in range(len(reds)):
                o_refs[n_o + k][...] += res[n_o + k]

    def row_spec(width, cb, shift):
        if shift:
            return pl.BlockSpec((tr, width), lambda i: (jnp.minimum(i + shift, nt - 1), cb))
        return pl.BlockSpec((tr, width), lambda i: (i, cb))

    in_specs = [row_spec(w, cb, sh) for (_, w, cb, sh) in rows]
    in_specs += [pl.BlockSpec(b.shape, lambda i: (0, 0)) for b in bcast]
    out_specs = [pl.BlockSpec((tr, w), lambda i: (i, 0)) for (w, _) in outs]
    out_specs += [pl.BlockSpec((1, w), lambda i: (0, 0)) for w in reds]
    out_shape = [jax.ShapeDtypeStruct((L, w), dt) for (w, dt) in outs]
    out_shape += [jax.ShapeDtypeStruct((1, w), f32) for w in reds]
    in_specs, operands, _ = _with_dep(in_specs, [r[0] for r in rows] + list(bcast), dep)
    return pl.pallas_call(
        body, out_shape=out_shape, grid=(nt,), in_specs=in_specs, out_specs=out_specs, name=name,
        compiler_params=_params("arbitrary"))(*operands)


def full(a):
    return (a, a.shape[1], 0, 0)


def colsum(v):
    return jnp.sum(v, axis=0, keepdims=True)


def rms_fwd(x, g):
    r = lax.rsqrt(jnp.mean(x * x, axis=-1, keepdims=True) + RMS_EPS)
    return x * r * g


def rms_bwd(x, g, dy):
    r = lax.rsqrt(jnp.mean(x * x, axis=-1, keepdims=True) + RMS_EPS)
    xh = x * r
    dyg = dy * g
    dx = r * (dyg - xh * jnp.mean(dyg * xh, axis=-1, keepdims=True))
    return dx, colsum(dy * xh)


GELU_C = math.sqrt(2.0 / math.pi)


def gelu(y):
    return y * (0.5 * (1.0 + jnp.tanh(GELU_C * (y + 0.044715 * (y * y * y)))))


def gelu_grad(y):
    t = jnp.tanh(GELU_C * (y + 0.044715 * (y * y * y)))
    return 0.5 * (1.0 + t) + 0.5 * y * (1.0 - t * t) * (GELU_C * (1.0 + 3 * 0.044715 * (y * y)))


def sigmoid(v):
    return 1.0 / (1.0 + jnp.exp(-v))


def _lane(shape):
    return lax.broadcasted_iota(jnp.int32, shape, 1)


def _rot_chunk(t, cos, sin_signed):
    first = (_lane(t.shape) % HEAD_DIM) < (HEAD_DIM // 2)
    partner = jnp.where(first, pltpu.roll(t, LANES - HEAD_DIM // 2, 1), pltpu.roll(t, HEAD_DIM // 2, 1))
    return t * cos + partner * sin_signed


def _cos_sin(pos, inv_freq, inverse):
    ang = pos * inv_freq
    cos, sin = jnp.cos(ang), jnp.sin(ang)
    first = (_lane(ang.shape) % HEAD_DIM) < (HEAD_DIM // 2)
    sign = jnp.where(first, -1.0, 1.0) * (-1.0 if inverse else 1.0)
    return cos, sin * sign


def _dup_head(chunk, odd):
    low = _lane(chunk.shape) < HEAD_DIM
    x = jnp.where(low != odd, chunk, 0.0)
    return x + pltpu.roll(x, HEAD_DIM, 1)


def _chunks(v):
    return [v[:, LANES * c:LANES * (c + 1)] for c in range(v.shape[1] // LANES)]


def qkv_prep(proj, pos, inv_freq, d_attn, d_kv):
    L = proj.shape[0]
    d_ssm = proj.shape[1] - d_attn - 2 * d_kv
    half = d_ssm // 2
    scale = 1.0 / math.sqrt(HEAD_DIM)

    def fn(i, nt, q, k, v, u0, u1, p, invf):
        cos, sin = _cos_sin(p, invf, False)
        qr = jnp.concatenate([_rot_chunk(c, cos, sin) for c in _chunks(q)], axis=1) * scale
        kr = [_rot_chunk(c, cos, sin) for c in _chunks(k)]
        kk = jnp.concatenate([_dup_head(c, odd) for c in kr for odd in (False, True)], axis=1)
        vv = jnp.concatenate([_dup_head(c, odd) for c in _chunks(v) for odd in (False, True)], axis=1)
        return qr, kk, vv, jnp.concatenate([u0, u1], axis=1)

    u_cb = (d_attn + 2 * d_kv) // half
    return rowwise("qkv_prep", fn, L,
                   [(proj, d_attn, 0, 0), (proj, d_kv, d_attn // d_kv, 0), (proj, d_kv, d_attn // d_kv + 1, 0),
                    (proj, half, u_cb, 0), (proj, half, u_cb + 1, 0), full(pos)],
                   [inv_freq], [(d_attn, bf16), (2 * d_kv, bf16), (2 * d_kv, bf16), (d_ssm, bf16)])


def qkv_grad(dq, dkk_c, dkk_p, dvv_c, dvv_p, du, pos, inv_freq):
    L, d_attn = dq.shape
    d_kv = dkk_c.shape[1] // 2
    scale = 1.0 / math.sqrt(HEAD_DIM)

    def fold(cur, prev, i, nt):
        t = cur + jnp.where(i < nt - 1, prev, 0.0)
        out = []
        for c in range(t.shape[1] // (2 * LANES)):
            even, odd = t[:, 2 * c * LANES:(2 * c + 1) * LANES], t[:, (2 * c + 1) * LANES:(2 * c + 2) * LANES]
            even, odd = even + pltpu.roll(even, HEAD_DIM, 1), odd + pltpu.roll(odd, HEAD_DIM, 1)
            out.append(jnp.where(_lane(even.shape) < HEAD_DIM, even, odd))
        return out

    def fn(i, nt, dq_t, kc, kp, vc, vp, du_t, p, invf):
        cos, sin = _cos_sin(p, invf, True)
        dq_o = jnp.concatenate([_rot_chunk(c, cos, sin) for c in _chunks(dq_t)], axis=1) * scale
        dk_o = jnp.concatenate([_rot_chunk(c, cos, sin) for c in fold(kc, kp, i, nt)], axis=1)
        dv_o = jnp.concatenate(fold(vc, vp, i, nt), axis=1)
        return (jnp.concatenate([dq_o, dk_o, dv_o, du_t], axis=1),)

    return rowwise("qkv_grad", fn, L,
                   [full(dq), full(dkk_c), (dkk_p, 2 * d_kv, 0, 1), full(dvv_c), (dvv_p, 2 * d_kv, 0, 1), full(du), full(pos)],
                   [inv_freq], [(d_attn + 2 * d_kv + du.shape[1], bf16)], tr=ATTN_BLOCK)[0]


def _attn_specs(L):
    nb = L // ATTN_BLOCK
    B = ATTN_BLOCK
    q_spec = lambda width: pl.BlockSpec((B, width), lambda n: (n, 0))
    prev = lambda width: pl.BlockSpec((B, width), lambda n: (jnp.maximum(n - 1, 0), 0))
    return nb, q_spec, prev


def _attn_mask(n):
    B = ATTN_BLOCK
    row = lax.broadcasted_iota(jnp.int32, (B, 2 * B), 0)
    col = lax.broadcasted_iota(jnp.int32, (B, 2 * B), 1)
    return ((col < B) & (col > row) & (n > 0)) | ((col >= B) & (row >= col - B))


def _attn_probs(qm, kcat, sink, mask):
    s = jnp.where(mask, _dot(qm, kcat, NT), NEG)
    m = jnp.maximum(jnp.max(s, axis=1, keepdims=True), sink)
    p, ps = jnp.exp(s - m), jnp.exp(sink - m)
    inv = 1.0 / (jnp.sum(p, axis=1, keepdims=True) + ps)
    return p, inv, ps


def _attn_heads(q_ref, s_ref, h, q_per_kv):
    low = _lane((ATTN_BLOCK, LANES)) < HEAD_DIM
    heads = []
    for pr in range(h * q_per_kv // 2, (h + 1) * q_per_kv // 2):
        q2 = q_ref[:, LANES * pr:LANES * (pr + 1)]
        for odd in (False, True):
            mine = low != odd
            sink = jnp.max(s_ref[2 * pr + int(odd):2 * pr + int(odd) + 1, :], axis=1, keepdims=True)
            heads.append((pr, mine, jnp.where(mine, q2, jnp.zeros_like(q2)), sink))
    return low, heads


def _kv_block(prev_ref, cur_ref, h):
    return jnp.concatenate([prev_ref[:, LANES * h:LANES * (h + 1)], cur_ref[:, LANES * h:LANES * (h + 1)]], axis=0)


def attn_fwd(qr, kk, vv, sink_b):
    L, d_attn = qr.shape
    nb, q_spec, prev = _attn_specs(L)
    d_kk = kk.shape[1]
    n_kv = d_kk // LANES
    q_per_kv = d_attn // HEAD_DIM // n_kv

    def body(q_ref, kc_ref, kp_ref, vc_ref, vp_ref, s_ref, o_ref):
        mask = _attn_mask(pl.program_id(0))
        for h in range(n_kv):
            kcat, vcat = _kv_block(kp_ref, kc_ref, h), _kv_block(vp_ref, vc_ref, h)
            low, heads = _attn_heads(q_ref, s_ref, h, q_per_kv)
            probs = [_attn_probs(qm, kcat, sink, mask) for (_, _, qm, sink) in heads]
            outs = [_dot(p.astype(bf16), vcat) * inv for (p, inv, _) in probs]
            for i in range(0, len(heads), 2):
                pr = heads[i][0]
                o_ref[:, LANES * pr:LANES * (pr + 1)] = jnp.where(low, outs[i], outs[i + 1])

    return pl.pallas_call(
        body, out_shape=jax.ShapeDtypeStruct((L, d_attn), f32), grid=(nb,),
        in_specs=[q_spec(d_attn), q_spec(d_kk), prev(d_kk), q_spec(d_kk), prev(d_kk), pl.BlockSpec(sink_b.shape, lambda n: (0, 0))],
        out_specs=q_spec(d_attn), name="attn_fwd", compiler_params=_params("arbitrary"))(qr, kk, kk, vv, vv, sink_b)


def attn_bwd(qr, kk, vv, sink_b, attn, d_attn_out):
    L, d_attn = qr.shape
    nb, q_spec, prev = _attn_specs(L)
    d_kk = kk.shape[1]
    n_kv = d_kk // LANES
    q_per_kv = d_attn // HEAD_DIM // n_kv

    def body(q_ref, kc_ref, kp_ref, vc_ref, vp_ref, s_ref, o_ref, do_ref, dq_ref, dkc_ref, dkp_ref, dvc_ref, dvp_ref, ds_ref):
        n = pl.program_id(0)
        B = ATTN_BLOCK
        mask = _attn_mask(n)
        srow = lax.broadcasted_iota(jnp.int32, (SUBLANES, LANES), 0)

        @pl.when(n == 0)
        def _():
            ds_ref[...] = jnp.zeros_like(ds_ref)

        for h in range(n_kv):
            kcat, vcat = _kv_block(kp_ref, kc_ref, h), _kv_block(vp_ref, vc_ref, h)
            low, heads = _attn_heads(q_ref, s_ref, h, q_per_kv)
            probs = [_attn_probs(qm, kcat, sink, mask) for (_, _, qm, sink) in heads]
            dk = jnp.zeros((2 * B, LANES), f32)
            dv = dk
            dsink = jnp.zeros((SUBLANES, LANES), f32)
            dqs = []
            for i, ((pr, mine, qm, _), (p, inv, ps)) in enumerate(zip(heads, probs)):
                do2 = do_ref[:, LANES * pr:LANES * (pr + 1)]
                delta = jnp.sum(jnp.where(mine, do2 * o_ref[:, LANES * pr:LANES * (pr + 1)], 0.0), axis=1, keepdims=True)
                dob = jnp.where(mine, do2, 0.0).astype(bf16)
                p = p * inv
                ds = (p * (_dot(dob, vcat, NT) - delta)).astype(bf16)
                dqs.append(_dot(ds, kcat))
                dk = dk + _dot(ds, qm, TN)
                dv = dv + _dot(p.astype(bf16), dob, TN)
                dsink = dsink + jnp.where(srow == i, -jnp.sum(ps * inv * delta), 0.0)
            for i in range(0, len(heads), 2):
                pr = heads[i][0]
                dq_ref[:, LANES * pr:LANES * (pr + 1)] = jnp.where(low, dqs[i], dqs[i + 1])
            cols = slice(LANES * h, LANES * (h + 1))
            dkp_ref[:, cols] = dk[:B]
            dkc_ref[:, cols] = dk[B:]
            dvp_ref[:, cols] = dv[:B]
            dvc_ref[:, cols] = dv[B:]
            ds_ref[h] += dsink

    kv_shape = jax.ShapeDtypeStruct(kk.shape, f32)
    ds_shape = (n_kv, SUBLANES, LANES)
    return pl.pallas_call(
        body,
        out_shape=[jax.ShapeDtypeStruct((L, d_attn), f32), kv_shape, kv_shape, kv_shape, kv_shape, jax.ShapeDtypeStruct(ds_shape, f32)],
        grid=(nb,),
        in_specs=[q_spec(d_attn), q_spec(d_kk), prev(d_kk), q_spec(d_kk), prev(d_kk), pl.BlockSpec(sink_b.shape, lambda n: (0, 0)),
                  q_spec(d_attn), q_spec(d_attn)],
        out_specs=[q_spec(d_attn)] + [q_spec(d_kk)] * 4 + [pl.BlockSpec(ds_shape, lambda n: (0, 0, 0))],
        name="attn_bwd", compiler_params=_params("arbitrary"))(qr, kk, kk, vv, vv, sink_b, attn, d_attn_out)


SSM_T = 256
NQ = SUBLANES * SSM_STATE // LANES
NJ = SUBLANES


def _strided_put(ref, j, val):
    for q in range(NQ):
        ref.at[q][pl.ds(j, SSM_T, stride=NJ), :] = val[:, LANES * q:LANES * (q + 1)]


def _strided_get(ref, j):
    return jnp.concatenate([ref.at[q][pl.ds(j, SSM_T, stride=NJ), :] for q in range(NQ)], axis=1)


def _ssm_specs(L, rev):
    nt = L // SSM_T
    idx = (lambda i: nt - 1 - i) if rev else (lambda i: i)
    row = lambda w, cb=0: pl.BlockSpec((SSM_T, w), lambda i: (idx(i), cb))
    state = pl.BlockSpec((NQ, SSM_T * NJ, LANES), lambda i: (0, idx(i), 0))
    whole = lambda a: pl.BlockSpec(a.shape, lambda i: (0,) * a.ndim)
    return nt, row, state, whole


def ssm_fwd(u_bf, proj, u_cb, bd_re, bd_im, cd_re, cd_im, lam_re, lam_im, d_skip, dep=None):
    L, d_ssm = u_bf.shape
    nt, row, state, whole = _ssm_specs(L, False)
    half = d_ssm // 2
    gw = d_ssm // NJ
    n_dep = 0 if dep is None else 1

    def body(u_ref, u0_ref, u1_ref, bdr, bdi, cdr, cdi, lr_ref, li_ref, d_ref, *rest):
        y_ref, z_ref, sr_ref, si_ref, carry = rest[n_dep:]
        i = pl.program_id(0)

        @pl.when(i == 0)
        def _():
            carry[...] = jnp.zeros_like(carry)

        for j in range(NJ):
            uj = u_ref[:, gw * j:gw * (j + 1)]
            _strided_put(sr_ref, j, _dot(uj, bdr[j]))
            _strided_put(si_ref, j, _dot(uj, bdi[j]))
        lr = [lr_ref[q] for q in range(NQ)]
        li = [li_ref[q] for q in range(NQ)]

        def step(t, s):
            sr, si = s
            rows = pl.ds(pl.multiple_of(t * NJ, NJ), NJ)
            nr = tuple(lr[q] * sr[q] - li[q] * si[q] + sr_ref[q, rows, :] for q in range(NQ))
            ni = tuple(lr[q] * si[q] + li[q] * sr[q] + si_ref[q, rows, :] for q in range(NQ))
            for q in range(NQ):
                sr_ref[q, rows, :] = nr[q]
                si_ref[q, rows, :] = ni[q]
            return nr, ni

        init = (tuple(carry[0, q] for q in range(NQ)), tuple(carry[1, q] for q in range(NQ)))
        sr, si = lax.fori_loop(0, SSM_T, step, init, unroll=8)
        for q in range(NQ):
            carry[0, q] = sr[q]
            carry[1, q] = si[q]
        uf = jnp.concatenate([u0_ref[...], u1_ref[...]], axis=1)
        for j in range(NJ):
            cols = slice(gw * j, gw * (j + 1))
            yj = _dot(_strided_get(sr_ref, j).astype(bf16), cdr[j]) - _dot(_strided_get(si_ref, j).astype(bf16), cdi[j])
            yj = yj + d_ref[:, cols] * uf[:, cols]
            y_ref[:, cols] = yj
            z_ref[:, cols] = gelu(yj).astype(bf16)

    s_shape = jax.ShapeDtypeStruct((NQ, L * NJ, LANES), f32)
    consts = (bd_re, bd_im, cd_re, cd_im, lam_re, lam_im, d_skip)
    in_specs, operands, _ = _with_dep([row(d_ssm), row(half, u_cb), row(half, u_cb + 1)] + [whole(a) for a in consts],
                                      [u_bf, proj, proj, *consts], dep)
    return pl.pallas_call(
        body, out_shape=[jax.ShapeDtypeStruct((L, d_ssm), f32), jax.ShapeDtypeStruct((L, d_ssm), bf16), s_shape, s_shape], grid=(nt,),
        in_specs=in_specs, out_specs=[row(d_ssm), row(d_ssm), state, state],
        scratch_shapes=[pltpu.VMEM((2, NQ, NJ, LANES), f32)], name="ssm_fwd",
        compiler_params=_params("arbitrary"))(*operands)


def ssm_bwd(y, dz1, dz2, u_bf, proj, u_cb, s_re, s_im, bd_re, bd_im, cd_re, cd_im, lam_re, lam_im, d_skip, dep=None):
    L, d_ssm = y.shape
    nt, row, state, whole = _ssm_specs(L, True)
    half = d_ssm // 2
    gw = d_ssm // NJ
    n_dep = 0 if dep is None else 1

    def body(y_ref, dz1_ref, dz2_ref, u_ref, u0_ref, u1_ref, sr_ref, si_ref, bdr, bdi, cdr, cdi, lr_ref, li_ref, d_ref, *rest):
        du_ref, dbdr, dbdi, dcdr, dcdi, dlr, dli, dd_ref, gr_ref, gi_ref, carry = rest[n_dep:]
        i = pl.program_id(0)

        @pl.when(i == 0)
        def _():
            carry[...] = jnp.zeros_like(carry)
            for r in (dbdr, dbdi, dcdr, dcdi, dlr, dli, dd_ref):
                r[...] = jnp.zeros_like(r)

        dyf = (dz1_ref[...] + dz2_ref[...]) * gelu_grad(y_ref[...])
        dyb = dyf.astype(bf16)
        for j in range(NJ):
            dyj = dyb[:, gw * j:gw * (j + 1)]
            _strided_put(gr_ref, j, _dot(dyj, cdr[j], NT))
            _strided_put(gi_ref, j, -_dot(dyj, cdi[j], NT))
            dcdr[j] += _dot(_strided_get(sr_ref, j).astype(bf16), dyj, TN)
            dcdi[j] -= _dot(_strided_get(si_ref, j).astype(bf16), dyj, TN)
        lr = [lr_ref[q] for q in range(NQ)]
        li = [li_ref[q] for q in range(NQ)]

        def step(k, c):
            gr, gi, ar, ai = c
            rows = pl.ds(pl.multiple_of((SSM_T - 1 - k) * NJ, NJ), NJ)
            s_r = [sr_ref[q, rows, :] for q in range(NQ)]
            s_i = [si_ref[q, rows, :] for q in range(NQ)]
            ar = tuple(ar[q] + gr[q] * s_r[q] + gi[q] * s_i[q] for q in range(NQ))
            ai = tuple(ai[q] + gi[q] * s_r[q] - gr[q] * s_i[q] for q in range(NQ))
            nr = tuple(gr_ref[q, rows, :] + lr[q] * gr[q] + li[q] * gi[q] for q in range(NQ))
            ni = tuple(gi_ref[q, rows, :] + lr[q] * gi[q] - li[q] * gr[q] for q in range(NQ))
            for q in range(NQ):
                gr_ref[q, rows, :] = nr[q]
                gi_ref[q, rows, :] = ni[q]
            return nr, ni, ar, ai

        zero = tuple(jnp.zeros((NJ, LANES), f32) for _ in range(NQ))
        init = (tuple(carry[0, q] for q in range(NQ)), tuple(carry[1, q] for q in range(NQ)), zero, zero)
        gr, gi, ar, ai = lax.fori_loop(0, SSM_T, step, init, unroll=8)
        for q in range(NQ):
            carry[0, q] = gr[q]
            carry[1, q] = gi[q]
            dlr[q] += ar[q]
            dli[q] += ai[q]
        uf = jnp.concatenate([u0_ref[...], u1_ref[...]], axis=1)
        dd_ref[...] += colsum(dyf * uf)
        for j in range(NJ):
            cols = slice(gw * j, gw * (j + 1))
            gjr, gji = _strided_get(gr_ref, j).astype(bf16), _strided_get(gi_ref, j).astype(bf16)
            du_ref[:, cols] = _dot(gjr, bdr[j], NT) + _dot(gji, bdi[j], NT) + d_ref[:, cols] * dyf[:, cols]
            uj = u_ref[:, cols]
            dbdr[j] += _dot(uj, gjr, TN)
            dbdi[j] += _dot(uj, gji, TN)

    consts = (bd_re, bd_im, cd_re, cd_im, lam_re, lam_im, d_skip)
    acc = lambda a: jax.ShapeDtypeStruct(a.shape, f32)
    outs = [jax.ShapeDtypeStruct((L, d_ssm), f32), acc(bd_re), acc(bd_im), acc(cd_re), acc(cd_im), acc(lam_re), acc(lam_im), acc(d_skip)]
    in_specs, operands, _ = _with_dep(
        [row(d_ssm)] * 4 + [row(half, u_cb), row(half, u_cb + 1), state, state] + [whole(a) for a in consts],
        [y, dz1, dz2, u_bf, proj, proj, s_re, s_im, *consts], dep)
    return pl.pallas_call(
        body, out_shape=outs, grid=(nt,),
        in_specs=in_specs, out_specs=[row(d_ssm)] + [whole(a) for a in consts],
        scratch_shapes=[pltpu.VMEM((NQ, SSM_T * NJ, LANES), f32), pltpu.VMEM((NQ, SSM_T * NJ, LANES), f32),
                        pltpu.VMEM((2, NQ, NJ, LANES), f32)],
        name="ssm_bwd", compiler_params=_params("arbitrary"))(*operands)


def _cmul(ar, ai, br, bi):
    return ar * br - ai * bi, ar * bi + ai * br


def _disc(ar, ai, logdt):
    dt = jnp.exp(logdt)
    mag = jnp.exp(ar * dt)
    lr, li = mag * jnp.cos(ai * dt), mag * jnp.sin(ai * dt)
    den = ar * ar + ai * ai
    nr, ni = lr - 1.0, li
    fr, fi = (nr * ar + ni * ai) / den, (ni * ar - nr * ai) / den
    return dt, lr, li, den, fr, fi


def ssm_params(a_re, a_im, logdt_b, bt_re, bt_im, spread):
    def body(ar_ref, ai_ref, ld_ref, br_ref, bi_ref, sp_ref, lr_ref, li_ref, or_ref, oi_ref):
        _, lr, li, _, fr, fi = _disc(ar_ref[...], ai_ref[...], ld_ref[...])
        lr_ref[...] = lr
        li_ref[...] = li
        fre = jnp.dot(sp_ref[...], fr, precision=HIGHEST, preferred_element_type=f32)
        fie = jnp.dot(sp_ref[...], fi, precision=HIGHEST, preferred_element_type=f32)
        o_r, o_i = _cmul(fre, fie, br_ref[...], bi_ref[...])
        or_ref[...] = o_r
        oi_ref[...] = o_i

    g = jax.ShapeDtypeStruct(a_re.shape, f32)
    b = jax.ShapeDtypeStruct(bt_re.shape, f32)
    return pl.pallas_call(body, out_shape=[g, g, b, b], name="ssm_params",
                          compiler_params=_params())(a_re, a_im, logdt_b, bt_re, bt_im, spread)


def ssm_params_grad(a_re, a_im, logdt_b, bt_re, bt_im, spread, gather, dlam_re, dlam_im, dbt_re, dbt_im):
    def body(ar_ref, ai_ref, ld_ref, br_ref, bi_ref, sp_ref, ga_ref, glr_ref, gli_ref, gbr_ref, gbi_ref,
             dar_ref, dai_ref, dld_ref, dbr_ref, dbi_ref):
        ar, ai = ar_ref[...], ai_ref[...]
        dt, lr, li, den, fr, fi = _disc(ar, ai, ld_ref[...])
        hdot = functools.partial(jnp.dot, precision=HIGHEST, preferred_element_type=f32)
        fre, fie = hdot(sp_ref[...], fr), hdot(sp_ref[...], fi)
        gbr, gbi, br, bi = gbr_ref[...], gbi_ref[...], br_ref[...], bi_ref[...]
        dbr_ref[...], dbi_ref[...] = _cmul(fre, -fie, gbr, gbi)
        t_r, t_i = _cmul(br, -bi, gbr, gbi)
        gfr, gfi = hdot(ga_ref[...], t_r), hdot(ga_ref[...], t_i)
        iwr, iwi = ar / den, -ai / den
        x_r, x_i = _cmul(iwr, -iwi, gfr, gfi)
        glr, gli = glr_ref[...] + x_r, gli_ref[...] + x_i
        q_r, q_i = _cmul(fr, fi, iwr, iwi)
        gwr, gwi = _cmul(-q_r, q_i, gfr, gfi)
        y_r, y_i = _cmul(dt * lr, -dt * li, glr, gli)
        dar_ref[...] = gwr + y_r
        dai_ref[...] = gwi + y_i
        wl_r, wl_i = _cmul(ar, ai, lr, li)
        z_r, _ = _cmul(wl_r, -wl_i, glr, gli)
        dld_ref[...] = jnp.sum(z_r * dt, axis=1, keepdims=True)

    g = jax.ShapeDtypeStruct(a_re.shape, f32)
    b = jax.ShapeDtypeStruct(bt_re.shape, f32)
    return pl.pallas_call(body, out_shape=[g, g, jax.ShapeDtypeStruct((a_re.shape[0], 1), f32), b, b], name="ssm_params_grad",
                          compiler_params=_params())(a_re, a_im, logdt_b, bt_re, bt_im, spread, gather, dlam_re, dlam_im, dbt_re, dbt_im)


def _block_diag(t, rows, cols):
    G = t.shape[0]
    t = t.reshape(G // NJ, NJ, rows, cols)
    eye = jnp.eye(NJ, dtype=t.dtype)
    return jnp.einsum('jgrc,gh->jgrhc', t, eye).reshape(G // NJ, NJ * rows, NJ * cols)


def _block_diag_take(m, rows, cols):
    J = m.shape[0]
    m = m.reshape(J, NJ, rows, NJ, cols)
    idx = jnp.arange(NJ)
    return m[:, idx, :, idx, :].transpose(1, 0, 2, 3).reshape(J * NJ, rows, cols)


def _state_layout(t):
    return t.reshape(NJ, NQ, LANES).transpose(1, 0, 2)


def _state_layout_inv(t, G, N):
    return t.transpose(1, 0, 2).reshape(G, N)


def _tiles2d(shape, budget_rows=128):
    rows, cols = shape
    tr = rows
    if rows > budget_rows:
        tr = budget_rows
        while rows % tr:
            tr -= SUBLANES
    return tr, cols


def _adam_update(w, g, m, v):
    c1 = 1.0 - ADAM_B1 ** ADAM_STEP
    c2 = 1.0 - ADAM_B2 ** ADAM_STEP
    nm = ADAM_B1 * m + (1.0 - ADAM_B1) * g
    nv = ADAM_B2 * v + (1.0 - ADAM_B2) * (g * g)
    delta = -ADAM_LR * ((nm / c1) / (jnp.sqrt(nv / c2) + ADAM_EPS) + ADAM_WD * w)
    return delta, nm, nv


def adamw_many(name, ws, gs, ms, vs):
    n = len(ws)

    def body(*refs):
        w, g, m, v = (refs[k * n:(k + 1) * n] for k in range(4))
        d, nm, nv = (refs[(4 + k) * n:(5 + k) * n] for k in range(3))
        for i in range(n):
            d[i][...], nm[i][...], nv[i][...] = _adam_update(w[i][...], g[i][...], m[i][...], v[i][...])

    o = [jax.ShapeDtypeStruct(a.shape, f32) for a in ws]
    outs = pl.pallas_call(body, out_shape=o * 3, name=name, compiler_params=_params())(*ws, *gs, *ms, *vs)
    return outs[:n], outs[n:2 * n], outs[2 * n:]


def adamw_halves(name, w, own, got, m, v, c_arr):
    h, cols = own.shape
    tr, _ = _tiles2d((h, cols), 128 if cols > 1024 else 256)
    nh = h // tr

    def body(c_ref, w_ref, own_ref, got_ref, m_ref, v_ref, g_ref, d_ref, nm_ref, nv_ref):
        mine = (pl.program_id(0) // nh) == c_ref[0]
        g = jnp.where(mine, own_ref[...], got_ref[...])
        g_ref[...] = g
        d_ref[...], nm_ref[...], nv_ref[...] = _adam_update(w_ref[...], g, m_ref[...], v_ref[...])

    spec = pl.BlockSpec((tr, cols), lambda i, c: (i, 0))
    own_spec = pl.BlockSpec((tr, cols), lambda i, c: (jnp.where(i // nh == c[0], i % nh, 0), 0))
    got_spec = pl.BlockSpec((tr, cols), lambda i, c: (jnp.where(i // nh == c[0], 0, i % nh), 0))
    o = jax.ShapeDtypeStruct(w.shape, f32)
    grid_spec = pltpu.PrefetchScalarGridSpec(num_scalar_prefetch=1, grid=(2 * nh,),
                                             in_specs=[spec, own_spec, got_spec, spec, spec], out_specs=[spec] * 4)
    return pl.pallas_call(body, out_shape=[o, o, o, o], grid_spec=grid_spec, name=name,
                          compiler_params=_params("arbitrary"))(c_arr, w, own, got, m, v)


def pair_sum(name, g, got, c_arr):
    S, h, cols = got.shape
    tr, _ = _tiles2d((h, cols), 1024)
    nh = h // tr

    def body(c_ref, g_ref, o_ref, out_ref):
        out_ref[...] = (g_ref[...].astype(f32) + o_ref[...].astype(f32)).astype(out_ref.dtype)

    spec = pl.BlockSpec((None, tr, cols), lambda s, i, c: (s, i, 0))
    grid_spec = pltpu.PrefetchScalarGridSpec(
        num_scalar_prefetch=1, grid=(S, nh),
        in_specs=[pl.BlockSpec((None, tr, cols), lambda s, i, c: (s, c[0] * nh + i, 0)), spec], out_specs=spec)
    return pl.pallas_call(body, out_shape=jax.ShapeDtypeStruct(got.shape, g.dtype), grid_spec=grid_spec, name=name,
                          compiler_params=_params("parallel", "parallel"))(c_arr, g, got)


def chip_sum(name, pair, landed, mine_arr, dep=None):
    n_in, h, cols = landed.shape
    tr, _ = _tiles2d((h, cols), 512)

    def body(s_ref, p_ref, l_ref, *rest):
        acc = p_ref[...].astype(f32)
        for k in range(n_in):
            acc = acc + l_ref[k].astype(f32)
        rest[-1][...] = acc

    in_specs, operands, _ = _with_dep(
        [pl.BlockSpec((None, tr, cols), lambda i, s: (s[0], i, 0)), pl.BlockSpec((n_in, tr, cols), lambda i, s: (0, i, 0))],
        [pair, landed], dep)
    grid_spec = pltpu.PrefetchScalarGridSpec(num_scalar_prefetch=1, grid=(h // tr,), in_specs=in_specs,
                                             out_specs=pl.BlockSpec((tr, cols), lambda i, s: (i, 0)))
    return pl.pallas_call(body, out_shape=jax.ShapeDtypeStruct((h, cols), f32), grid_spec=grid_spec, name=name,
                          compiler_params=_params("parallel"))(mine_arr, *operands)


def into_slot(name, w, slot_arr, n_slots, dtype, dep=None):
    tr, cols = _tiles2d(w.shape, 256)

    def body(s_ref, w_ref, *rest):
        rest[-1][...] = w_ref[...].astype(dtype)

    in_specs, operands, _ = _with_dep([pl.BlockSpec((tr, cols), lambda i, s: (i, 0))], [w], dep)
    grid_spec = pltpu.PrefetchScalarGridSpec(num_scalar_prefetch=1, grid=(w.shape[0] // tr,), in_specs=in_specs,
                                             out_specs=pl.BlockSpec((None, tr, cols), lambda i, s: (s[0], i, 0)))
    return pl.pallas_call(body, out_shape=jax.ShapeDtypeStruct((n_slots,) + w.shape, dtype), grid_spec=grid_spec, name=name,
                          compiler_params=_params("parallel"))(slot_arr, *operands)


def sum_slots(name, t):
    S, rows, cols = t.shape
    tr, _ = _tiles2d((rows, cols), 256)

    def body(t_ref, o_ref):
        acc = t_ref[0]
        for s in range(1, S):
            acc = acc + t_ref[s]
        o_ref[...] = acc

    return pl.pallas_call(body, out_shape=jax.ShapeDtypeStruct((rows, cols), f32), grid=(rows // tr,),
                          in_specs=[pl.BlockSpec((S, tr, cols), lambda i: (0, i, 0))], out_specs=pl.BlockSpec((tr, cols), lambda i: (i, 0)),
                          name=name, compiler_params=_params("parallel"))(t)


def _place():
    x, y, c = lax.axis_index("x"), lax.axis_index("y"), lax.axis_index("c")
    return x, y, c


def _other_chips(x, y):
    return [(1 - x, y, 2 * (1 - x) + y), (x, 1 - y, 2 * x + 1 - y), (1 - x, 1 - y, 2 * (1 - x) + 1 - y)]


SEM = pl.BlockSpec(memory_space=pltpu.SEMAPHORE)
VM = pl.BlockSpec(memory_space=pltpu.VMEM)
DATAFLOW = pltpu.SideEffectType.DATAFLOW_SIDE_EFFECTING
TOKEN = jax.ShapeDtypeStruct((SUBLANES, LANES), f32)


def _gather_copy(buf, w, k, chip, c, mine, send, recv):
    px, py, _ = chip
    h = buf.shape[1] // 2
    half = buf.at[mine, pl.ds(c * h, h), :]
    return pltpu.make_async_remote_copy(src_ref=half, dst_ref=half, send_sem=send.at[3 * w + k], recv_sem=recv.at[3 * w + k],
                                        device_id=(px, py, c), device_id_type=MESH)


def _gather_landing(buf, w, k, chip, c, send, recv):
    px, py, s = chip
    h = buf.shape[1] // 2
    landed = buf.at[s, pl.ds(c * h, h), :]
    return pltpu.make_async_remote_copy(src_ref=landed, dst_ref=landed, send_sem=send.at[3 * w + k], recv_sem=recv.at[3 * w + k],
                                        device_id=(px, py, c), device_id_type=MESH)


def gather_start(name, bufs, groups, after, neighbours_only=()):
    nw, ng = len(bufs), len(groups)

    def body(*refs):
        outs = refs[nw + 1:]
        sems, dst = outs[:2 * ng], outs[2 * ng:2 * ng + nw]
        token = outs[2 * ng + nw]
        x, y, c = _place()
        mine = 2 * x + y
        for g, members in enumerate(groups):
            for i, w in enumerate(members):
                for k, chip in enumerate(_other_chips(x, y)[:2 if w in neighbours_only else 3]):
                    _gather_copy(dst[w], i, k, chip, c, mine, sems[2 * g], sems[2 * g + 1]).start()
        token[...] = jnp.zeros_like(token)

    sem_shapes = []
    for members in groups:
        sem_shapes += [pltpu.SemaphoreType.DMA((3 * len(members),))] * 2
    outs = pl.pallas_call(
        body, out_shape=sem_shapes + [jax.ShapeDtypeStruct(b.shape, b.dtype) for b in bufs] + [TOKEN],
        in_specs=[ANY] * (nw + 1), out_specs=[SEM] * (2 * ng) + [ANY] * nw + [VM],
        input_output_aliases={w: 2 * ng + w for w in range(nw)}, name=name,
        compiler_params=pltpu.CompilerParams(has_side_effects=DATAFLOW))(*bufs, after)
    return [(outs[2 * g], outs[2 * g + 1]) for g in range(ng)], list(outs[2 * ng:2 * ng + nw]), outs[2 * ng + nw]


def gather_wait(name, bufs, send, recv, after):
    nw = len(bufs)

    def body(*refs):
        src = refs[:nw]
        send_ref, recv_ref = refs[nw], refs[nw + 1]
        x, y, c = _place()
        mine = 2 * x + y
        for w in range(nw):
            for k, chip in enumerate(_other_chips(x, y)):
                _gather_copy(src[w], w, k, chip, c, mine, send_ref, recv_ref).wait_send()
                _gather_landing(src[w], w, k, chip, c, send_ref, recv_ref).wait_recv()

    return pl.pallas_call(
        body, out_shape=[jax.ShapeDtypeStruct(b.shape, b.dtype) for b in bufs],
        in_specs=[ANY] * nw + [SEM, SEM, ANY], out_specs=[ANY] * nw,
        input_output_aliases={w: w for w in range(nw)}, name=name,
        compiler_params=pltpu.CompilerParams(has_side_effects=DATAFLOW))(*bufs, send, recv, after)


def _relay_copy(buf, w, j, x, y, c, send, recv, landing):
    chips = _other_chips(x, y)
    px, py, _ = chips[j]
    h = buf.shape[1] // 2
    q = h // 2
    s = chips[2][2] if landing else chips[1 - j][2]
    part = buf.at[s, pl.ds(c * h + j * q, q), :]
    return pltpu.make_async_remote_copy(src_ref=part, dst_ref=part, send_sem=send.at[2 * w + j], recv_sem=recv.at[2 * w + j],
                                        device_id=(px, py, c), device_id_type=MESH)


def _early_pass(buf, nw, w, k, x, y, c, send, recv, landing):
    s = _other_chips(x, y)[k][2]
    h = buf.shape[1] // 2
    part = buf.at[s, pl.ds(((1 - c) if landing else c) * h, h), :]
    i = 2 * nw + 2 * w + k
    return pltpu.make_async_remote_copy(src_ref=part, dst_ref=part, send_sem=send.at[i], recv_sem=recv.at[i],
                                        device_id=(x, y, 1 - c), device_id_type=MESH)


def gather_relay(name, bufs, sems, more, after):
    nw, nm = len(bufs), len(more)
    ns = 4 if nm else 2

    def body(*refs):
        ins, outs = refs[:nw + nm + 2 * nw + 1], refs[nw + nm + 2 * nw + 1:]
        src, d_sems = ins[:nw], ins[nw + nm:nw + nm + 2 * nw]
        r_send, r_recv = outs[:2]
        m_send, m_recv = outs[2:ns] if nm else (None, None)
        dst, mdst, token = outs[ns:ns + nw], outs[ns + nw:ns + nw + nm], outs[ns + nw + nm]
        x, y, c = _place()
        mine = 2 * x + y
        chips = _other_chips(x, y)
        for w in range(nw):
            for k in range(2):
                _gather_copy(src[w], 0, k, chips[k], c, mine, d_sems[2 * w], d_sems[2 * w + 1]).wait_send()
                _gather_landing(src[w], 0, k, chips[k], c, d_sems[2 * w], d_sems[2 * w + 1]).wait_recv()
            for j in range(2):
                _relay_copy(dst[w], w, j, x, y, c, r_send, r_recv, False).start()
            for k in range(2):
                _early_pass(dst[w], nw, w, k, x, y, c, r_send, r_recv, False).start()
        for w in range(nm):
            for k, chip in enumerate(chips):
                _gather_copy(mdst[w], w, k, chip, c, mine, m_send, m_recv).start()
        token[...] = jnp.zeros_like(token)

    sem_shapes = [pltpu.SemaphoreType.DMA((4 * nw,))] * 2 + [pltpu.SemaphoreType.DMA((3 * nm,))] * (ns - 2)
    flat_sems = [s for pair in sems for s in pair]
    outs = pl.pallas_call(
        body, out_shape=sem_shapes + [jax.ShapeDtypeStruct(b.shape, b.dtype) for b in list(bufs) + list(more)] + [TOKEN],
        in_specs=[ANY] * (nw + nm) + [SEM] * (2 * nw) + [ANY], out_specs=[SEM] * ns + [ANY] * (nw + nm) + [VM],
        input_output_aliases={i: ns + i for i in range(nw + nm)}, name=name,
        compiler_params=pltpu.CompilerParams(has_side_effects=DATAFLOW))(*bufs, *more, *flat_sems, after)
    m_sems = (outs[2], outs[3]) if nm else None
    return outs[0], outs[1], m_sems, list(outs[ns:ns + nw]), list(outs[ns + nw:ns + nw + nm]), outs[ns + nw + nm]


def gather_wait_relay(name, bufs, r_send, r_recv, after):
    nw = len(bufs)

    def body(*refs):
        src = refs[:nw]
        send_ref, recv_ref = refs[nw], refs[nw + 1]
        x, y, c = _place()
        for w in range(nw):
            for j in range(2):
                _relay_copy(src[w], w, j, x, y, c, send_ref, recv_ref, False).wait_send()
                _relay_copy(src[w], w, j, x, y, c, send_ref, recv_ref, True).wait_recv()
                _early_pass(src[w], nw, w, j, x, y, c, send_ref, recv_ref, False).wait_send()
                _early_pass(src[w], nw, w, j, x, y, c, send_ref, recv_ref, True).wait_recv()

    return pl.pallas_call(
        body, out_shape=[jax.ShapeDtypeStruct(b.shape, b.dtype) for b in bufs],
        in_specs=[ANY] * nw + [SEM, SEM, ANY], out_specs=[ANY] * nw,
        input_output_aliases={w: w for w in range(nw)}, name=name,
        compiler_params=pltpu.CompilerParams(has_side_effects=DATAFLOW))(*bufs, r_send, r_recv, after)


def gather_forward(name, bufs, which=(0, 1, 2)):
    nw = len(bufs)

    def body(*refs):
        dst = refs[nw:2 * nw]
        send, recv = refs[2 * nw:]
        x, y, c = _place()
        sib = (x, y, 1 - c)
        barrier = pltpu.get_barrier_semaphore()
        pl.semaphore_signal(barrier, inc=1, device_id=sib, device_id_type=MESH)
        pl.semaphore_wait(barrier, 1)
        cps = []
        for w in range(nw):
            h = dst[w].shape[1] // 2
            for k in which:
                s = _other_chips(x, y)[k][2]
                landed = dst[w].at[s, pl.ds(c * h, h), :]
                cp = pltpu.make_async_remote_copy(src_ref=landed, dst_ref=landed, send_sem=send.at[w, k], recv_sem=recv.at[w, k],
                                                  device_id=sib, device_id_type=MESH)
                cp.start()
                cps.append(cp)
        for w in range(nw):
            h = dst[w].shape[1] // 2
            for k in which:
                s = _other_chips(x, y)[k][2]
                other = dst[w].at[s, pl.ds((1 - c) * h, h), :]
                pltpu.make_async_remote_copy(src_ref=other, dst_ref=other, send_sem=send.at[w, k], recv_sem=recv.at[w, k],
                                             device_id=sib, device_id_type=MESH).wait_recv()
        for cp in cps:
            cp.wait_send()

    sem = pltpu.SemaphoreType.DMA((nw, 3))
    return pl.pallas_call(
        body, out_shape=[jax.ShapeDtypeStruct(b.shape, b.dtype) for b in bufs],
        in_specs=[ANY] * nw, out_specs=[ANY] * nw, input_output_aliases={w: w for w in range(nw)},
        scratch_shapes=[sem, sem], name=name,
        compiler_params=pltpu.CompilerParams(has_side_effects=True, collective_id=SIBLING_PAIR))(*bufs)


def _scatter_copy(src, dst, w, k, chip, c, send, recv):
    px, py, s = chip
    return pltpu.make_async_remote_copy(src_ref=src.at[s], dst_ref=dst.at[k], send_sem=send.at[3 * w + k], recv_sem=recv.at[3 * w + k],
                                        device_id=(px, py, c), device_id_type=MESH)


def scatter_start(name, parts):
    nw = len(parts)
    lands = [pltpu.with_memory_space_constraint(lax.empty((N_CHIPS - 1,) + p.shape[1:], p.dtype), pltpu.HBM) for p in parts]

    def body(*refs):
        outs = refs[2 * nw:]
        send, recv = outs[0], outs[1]
        src, dst, token = outs[2:2 + nw], outs[2 + nw:2 + 2 * nw], outs[2 + 2 * nw]
        x, y, c = _place()
        for w in range(nw):
            for k, chip in enumerate(_other_chips(x, y)):
                _scatter_copy(src[w], dst[w], w, k, chip, c, send, recv).start()
        token[...] = jnp.zeros_like(token)

    sem = pltpu.SemaphoreType.DMA((3 * nw,))
    outs = pl.pallas_call(
        body, out_shape=[sem, sem] + [jax.ShapeDtypeStruct(p.shape, p.dtype) for p in parts]
        + [jax.ShapeDtypeStruct(l.shape, l.dtype) for l in lands] + [TOKEN],
        in_specs=[ANY] * (2 * nw), out_specs=[SEM, SEM] + [ANY] * (2 * nw) + [VM],
        input_output_aliases={i: 2 + i for i in range(2 * nw)}, name=name,
        compiler_params=pltpu.CompilerParams(has_side_effects=DATAFLOW))(*parts, *lands)
    return outs[0], outs[1], list(outs[2:2 + nw]), list(outs[2 + nw:2 + 2 * nw]), outs[2 + 2 * nw]


def scatter_wait(name, parts, lands, send, recv, after):
    nw = len(parts)

    def body(*refs):
        src, dst = refs[:nw], refs[nw:2 * nw]
        send_ref, recv_ref = refs[2 * nw], refs[2 * nw + 1]
        x, y, c = _place()
        for w in range(nw):
            for k, chip in enumerate(_other_chips(x, y)):
                cp = _scatter_copy(src[w], dst[w], w, k, chip, c, send_ref, recv_ref)
                cp.wait_send()
                cp.wait_recv()

    outs = pl.pallas_call(
        body, out_shape=[jax.ShapeDtypeStruct(a.shape, a.dtype) for a in list(parts) + list(lands)],
        in_specs=[ANY] * (2 * nw) + [SEM, SEM, ANY], out_specs=[ANY] * (2 * nw),
        input_output_aliases={i: i for i in range(2 * nw)}, name=name,
        compiler_params=pltpu.CompilerParams(has_side_effects=DATAFLOW))(*parts, *lands, send, recv, after)
    return list(outs[:nw]), list(outs[nw:])


SIBLING_PAIR = 0


def _sibling_copy(src, dst, w, c, half_rows, send, recv, sib):
    if half_rows:
        h = src.shape[1] // 2
        src = src.at[:, pl.ds((1 - c) * h, h), :]
    return pltpu.make_async_remote_copy(src_ref=src, dst_ref=dst, send_sem=send.at[w], recv_sem=recv.at[w],
                                        device_id=sib, device_id_type=MESH)


def _landing(shape, dtype):
    return pltpu.with_memory_space_constraint(lax.empty(shape, dtype), pltpu.HBM)


def sibling_start(name, srcs, half_rows):
    nw = len(srcs)
    lands = [_landing((s.shape[0], s.shape[1] // 2, s.shape[2]) if half_rows else s.shape, s.dtype) for s in srcs]

    def body(*refs):
        outs = refs[2 * nw:]
        send, recv = outs[0], outs[1]
        src, dst, token = outs[2:2 + nw], outs[2 + nw:2 + 2 * nw], outs[2 + 2 * nw]
        x, y, c = _place()
        barrier = pltpu.get_barrier_semaphore()
        pl.semaphore_signal(barrier, inc=1, device_id=(x, y, 1 - c), device_id_type=MESH)
        pl.semaphore_wait(barrier, 1)
        for w in range(nw):
            _sibling_copy(src[w], dst[w], w, c, half_rows, send, recv, (x, y, 1 - c)).start()
        token[...] = jnp.zeros_like(token)

    sem = pltpu.SemaphoreType.DMA((nw,))
    outs = pl.pallas_call(
        body, out_shape=[sem, sem] + [jax.ShapeDtypeStruct(a.shape, a.dtype) for a in list(srcs) + lands] + [TOKEN],
        in_specs=[ANY] * (2 * nw), out_specs=[SEM, SEM] + [ANY] * (2 * nw) + [VM],
        input_output_aliases={i: 2 + i for i in range(2 * nw)}, name=name,
        compiler_params=pltpu.CompilerParams(has_side_effects=DATAFLOW, collective_id=SIBLING_PAIR))(*srcs, *lands)
    return outs[0], outs[1], list(outs[2:2 + nw]), list(outs[2 + nw:2 + 2 * nw]), outs[2 + 2 * nw]


def sibling_wait(name, srcs, lands, send, recv, half_rows, after):
    nw = len(srcs)

    def body(*refs):
        src, dst = refs[:nw], refs[nw:2 * nw]
        send_ref, recv_ref = refs[2 * nw], refs[2 * nw + 1]
        x, y, c = _place()
        for w in range(nw):
            cp = _sibling_copy(src[w], dst[w], w, c, half_rows, send_ref, recv_ref, (x, y, 1 - c))
            cp.wait_send()
            cp.wait_recv()

    outs = pl.pallas_call(
        body, out_shape=[jax.ShapeDtypeStruct(a.shape, a.dtype) for a in list(srcs) + list(lands)],
        in_specs=[ANY] * (2 * nw) + [SEM, SEM, ANY], out_specs=[ANY] * (2 * nw),
        input_output_aliases={i: i for i in range(2 * nw)}, name=name,
        compiler_params=pltpu.CompilerParams(has_side_effects=DATAFLOW))(*srcs, *lands, send, recv, after)
    return list(outs[:nw]), list(outs[nw:])


def _peer(x, y, c, r):
    return (1 - x if r & 4 else x, 1 - y if r & 2 else y, 1 - c if r & 1 else c)


def _everyone_copy(buf, r, x, y, c, send, recv, landing):
    px, py, pc = _peer(x, y, c, r)
    slot = buf.at[4 * px + 2 * py + pc] if landing else buf.at[4 * x + 2 * y + c]
    return pltpu.make_async_remote_copy(src_ref=slot, dst_ref=slot, send_sem=send.at[r - 1], recv_sem=recv.at[r - 1],
                                        device_id=(px, py, pc), device_id_type=MESH)


def everyone_start(name, buf):
    def body(buf_in, send, recv, buf_ref, token):
        x, y, c = _place()
        for r in range(1, N_DEV):
            _everyone_copy(buf_ref, r, x, y, c, send, recv, False).start()
        token[...] = jnp.zeros_like(token)

    sem = pltpu.SemaphoreType.DMA((N_DEV - 1,))
    return pl.pallas_call(
        body, out_shape=[sem, sem, jax.ShapeDtypeStruct(buf.shape, buf.dtype), TOKEN],
        in_specs=[ANY], out_specs=[SEM, SEM, ANY, VM], input_output_aliases={0: 2}, name=name,
        compiler_params=pltpu.CompilerParams(has_side_effects=DATAFLOW))(buf)


def everyone_wait(name, buf, send, recv, after):
    def body(buf_ref, send_ref, recv_ref, after_ref, out_ref):
        x, y, c = _place()
        for r in range(1, N_DEV):
            _everyone_copy(buf_ref, r, x, y, c, send_ref, recv_ref, False).wait_send()
            _everyone_copy(buf_ref, r, x, y, c, send_ref, recv_ref, True).wait_recv()

    return pl.pallas_call(
        body, out_shape=jax.ShapeDtypeStruct(buf.shape, buf.dtype), in_specs=[ANY, SEM, SEM, ANY], out_specs=ANY,
        input_output_aliases={0: 0}, name=name,
        compiler_params=pltpu.CompilerParams(has_side_effects=DATAFLOW))(buf, send, recv, after)


def local_step(x, pos, tgt, small, d_in, get_w, put_g, first_dep=None, tick=lambda name, after: None):
    L, D = x.shape
    d_kv = N_KV_HEADS * HEAD_DIM
    d_ssm = small["d_skip"].shape[1]
    d_attn = d_in - 2 * d_kv - d_ssm
    big = {}
    G = d_ssm // SSM_GROUP
    N, P = SSM_STATE, SSM_GROUP
    gbf = bf16

    half_dim = HEAD_DIM // 2
    inv_freq = ROPE_THETA ** (-jnp.arange(half_dim, dtype=f32) / half_dim)
    inv_freq = jnp.tile(inv_freq, LANES // half_dim).reshape(1, LANES)
    sink_b = jnp.broadcast_to(small["sinks"].reshape(-1, 1), (small["sinks"].size, LANES))

    spread = jnp.repeat(jnp.eye(G, dtype=f32), P, axis=0)
    logdt_b = jnp.broadcast_to(small["log_dt"].reshape(G, 1), (G, N))
    bt_re = small["b_re"].reshape(G, N, P).transpose(0, 2, 1).reshape(G * P, N)
    bt_im = small["b_im"].reshape(G, N, P).transpose(0, 2, 1).reshape(G * P, N)
    a_re, a_im = small["a_re"].reshape(G, N), small["a_im"].reshape(G, N)
    lam_re, lam_im, bbt_re, bbt_im = ssm_params(a_re, a_im, logdt_b, bt_re, bt_im, spread)
    bd_re = _block_diag(bbt_re.reshape(G, P, N), P, N).astype(bf16)
    bd_im = _block_diag(bbt_im.reshape(G, P, N), P, N).astype(bf16)
    c_re = small["c_re"].reshape(G, P, N).transpose(0, 2, 1)
    c_im = small["c_im"].reshape(G, P, N).transpose(0, 2, 1)
    cd_re = _block_diag(c_re, N, P).astype(bf16)
    cd_im = _block_diag(c_im, N, P).astype(bf16)
    lam_re_l, lam_im_l = _state_layout(lam_re), _state_layout(lam_im)

    def k1(i, nt, xt, g):
        return (rms_fwd(xt, g),)
    xn = rowwise("pre_mix_norm", k1, L, [full(x)], [small["g_pre_mix"]], [(D, bf16)], dep=first_dep)[0]
    big["w_in"] = get_w("w_in", xn)
    proj = mm_nn("proj_in", xn, big["w_in"])
    qr, kk, vv, u_bf = qkv_prep(proj, pos, inv_freq, d_attn, d_kv)
    attn = attn_fwd(qr, kk, vv, sink_b)
    u_cb = (d_attn + 2 * d_kv) // (d_ssm // 2)
    token = tick("attn", attn)
    y, z_bf, s_re, s_im = ssm_fwd(u_bf, proj, u_cb, bd_re, bd_im, cd_re, cd_im, lam_re_l, lam_im_l, small["d_skip"], dep=token)
    token = tick("ssm", z_bf)
    big["w_glu"] = get_w("w_glu", z_bf)
    gl = mm_nn("glu_proj", z_bf, big["w_glu"], dep=token)

    def k6(i, nt, at, yt, glt, bg, ga, gs):
        ssm = gelu(yt) * sigmoid(glt + bg)
        return (jnp.concatenate([rms_fwd(at, ga), rms_fwd(ssm, gs)], axis=1),)
    mixed = rowwise("mix_norms", k6, L, [full(attn), full(y), full(gl)],
                    [small["b_glu"], small["g_attn_out"], small["g_ssm_out"]], [(d_attn + d_ssm, bf16)])[0]
    big["w_o"] = get_w("w_o", mixed)
    mix = mm_nn("proj_out", mixed, big["w_o"])

    def k7(i, nt, xt, mt, gpm, gpf):
        h = xt + rms_fwd(mt, gpm)
        return h, rms_fwd(h, gpf)
    h, hn = rowwise("post_mix", k7, L, [full(x), full(mix)], [small["g_post_mix"], small["g_pre_ffn"]], [(D, f32), (D, bf16)])
    big["w_gate"] = get_w("w_gate", hn)
    big["w_up"] = get_w("w_up", hn)
    gt, up, hid = ffn_hidden(hn, big["w_gate"], big["w_up"])
    d_ff_dim = gt.shape[1]
    big["w_down"] = get_w("w_down", hid)
    ff = mm_nn("ffn_down", hid, big["w_down"], tk=d_ff_dim // 2)

    def k9(i, nt, ht, fft, tt, g):
        out = ht + rms_fwd(fft, g)
        err = out - tt
        per_row = jnp.mean(err * err, axis=-1, keepdims=True)
        loss = 0.5 * jnp.sum(per_row) * jnp.where(_lane((1, LANES)) == 0, 1.0, 0.0)
        d_out = err * (1.0 / D)
        d_ff, dg = rms_bwd(fft, g, d_out)
        return d_out, d_ff, dg, loss
    d_out, d_ff, dg_post_ffn, loss = rowwise("loss_head", k9, L, [full(h), full(ff), full(tgt)], [small["g_post_ffn"]],
                                             [(D, f32), (D, bf16)], reds=[D, LANES])

    d_gt, d_up = ffn_hidden_grad(d_ff, big["w_down"], gt, up)
    token = put_g("w_down", mm_tn("dw_down", hid, d_ff, out_dtype=gbf, tm=d_ff_dim // N_CHIPS))
    d_hn = mm_nt_pair("d_hn", d_gt, big["w_gate"], d_up, big["w_up"], dep=token)
    token = put_g("w_gate", mm_tn("dw_gate", hn, d_gt, shards=N_CHIPS, out_dtype=gbf))
    token = put_g("w_up", mm_tn("dw_up", hn, d_up, shards=N_CHIPS, out_dtype=gbf, dep=token))

    def k11(i, nt, ht, da, do, mt, gpf, gpm):
        dh_n, dg_pf = rms_bwd(ht, gpf, da)
        dh = do + dh_n
        d_mix, dg_pm = rms_bwd(mt, gpm, dh)
        return dh, d_mix, dg_pf, dg_pm
    dh, d_mix, dg_pre_ffn, dg_post_mix = rowwise("post_mix_grad", k11, L, [full(h), full(d_hn), full(d_out), full(mix)],
                                                 [small["g_pre_ffn"], small["g_post_mix"]], [(D, f32), (D, bf16)], reds=[D, D], dep=token)
    d_mixed = mm_nt("d_mixed", d_mix, big["w_o"])
    token = put_g("w_o", mm_tn("dw_o", mixed, d_mix, out_dtype=gbf))

    def k12(i, nt, at, yt, glt, da_n, ds_n, bg, ga, gs):
        z = gelu(yt)
        sg = sigmoid(glt + bg)
        ssm = z * sg
        d_at, dga = rms_bwd(at, ga, da_n)
        d_ssm_t, dgs = rms_bwd(ssm, gs, ds_n)
        d_gl = d_ssm_t * z * sg * (1.0 - sg)
        return d_at, d_ssm_t * sg, d_gl, dga, dgs, colsum(d_gl)
    d_attn_o, dz1, d_gl, dg_attn, dg_ssm, db_glu = rowwise(
        "mix_norms_grad", k12, L, [full(attn), full(y), full(gl), (d_mixed, d_attn, 0, 0), (d_mixed, d_ssm, d_attn // d_ssm, 0)],
        [small["b_glu"], small["g_attn_out"], small["g_ssm_out"]], [(d_attn, f32), (d_ssm, f32), (d_ssm, bf16)],
        reds=[d_attn, d_ssm, d_ssm], dep=token)
    dz2 = mm_nt("d_glu_in", d_gl, big["w_glu"])
    token = put_g("w_glu", mm_tn("dw_glu", z_bf, d_gl, out_dtype=gbf))

    du, dbd_re, dbd_im, dcd_re, dcd_im, dlam_re_l, dlam_im_l, dd_skip = ssm_bwd(
        y, dz1, dz2, u_bf, proj, u_cb, s_re, s_im, bd_re, bd_im, cd_re, cd_im, lam_re_l, lam_im_l, small["d_skip"], dep=token)
    dq, dkk_c, dkk_p, dvv_c, dvv_p, dsink = attn_bwd(qr, kk, vv, sink_b, attn, d_attn_o)
    d_proj = qkv_grad(dq, dkk_c, dkk_p, dvv_c, dvv_p, du, pos, inv_freq)
    d_xn = mm_nt("d_xn", d_proj, big["w_in"])
    token = put_g("w_in", mm_tn("dw_in", xn, d_proj, shards=N_CHIPS, out_dtype=gbf))

    def k17(i, nt, xt, dxn, dht, g):
        dx, dg = rms_bwd(xt, g, dxn)
        return dht + dx, dg
    grad_x, dg_pre_mix = rowwise("pre_mix_grad", k17, L, [full(x), full(d_xn), full(dh)], [small["g_pre_mix"]],
                                 [(D, f32)], reds=[D], dep=token)

    gather = spread.T
    dbbt_re = _block_diag_take(dbd_re, P, N).reshape(G * P, N)
    dbbt_im = _block_diag_take(dbd_im, P, N).reshape(G * P, N)
    d_a_re, d_a_im, d_logdt, dbt_re, dbt_im = ssm_params_grad(
        a_re, a_im, logdt_b, bt_re, bt_im, spread, gather,
        _state_layout_inv(dlam_re_l, G, N), _state_layout_inv(dlam_im_l, G, N), dbbt_re, dbbt_im)
    q_per_kv = d_attn // HEAD_DIM // N_KV_HEADS
    small_grads = {
        "g_pre_mix": dg_pre_mix, "sinks": dsink[:, :q_per_kv, 0].reshape(1, -1),
        "a_re": d_a_re, "a_im": d_a_im, "log_dt": d_logdt.reshape(1, G),
        "b_re": dbt_re, "b_im": dbt_im,
        "c_re": _block_diag_take(dcd_re, N, P).transpose(0, 2, 1), "c_im": _block_diag_take(dcd_im, N, P).transpose(0, 2, 1),
        "d_skip": dd_skip, "b_glu": db_glu, "g_attn_out": dg_attn, "g_ssm_out": dg_ssm,
        "g_post_mix": dg_post_mix, "g_pre_ffn": dg_pre_ffn, "g_post_ffn": dg_post_ffn,
    }
    return loss, grad_x, small_grads


WEIGHTS = ['g_pre_mix', 'w_in', 'sinks', 'a_re', 'a_im', 'log_dt', 'b_re', 'b_im', 'c_re', 'c_im', 'd_skip', 'w_glu', 'b_glu',
           'g_attn_out', 'g_ssm_out', 'w_o', 'g_post_mix', 'g_pre_ffn', 'w_gate', 'w_up', 'w_down', 'g_post_ffn']
BIG = ['w_in', 'w_glu', 'w_o', 'w_gate', 'w_up', 'w_down']
COL_SHARDED = ['w_in', 'w_gate', 'w_up']
SMALL = [n for n in WEIGHTS if n not in BIG]
GATHER_GROUPS = [["w_in"], ["w_glu", "w_o"], ["w_gate", "w_up"], ["w_down"]]
REDUCE_GROUPS = [["w_down", "w_gate", "w_up"], ["w_o", "w_glu", "w_in"]]


PACK_ROWS = 256


def _pack(parts):
    flat = jnp.concatenate([p.reshape(-1) for p in parts])
    pad = (-flat.size) % (PACK_ROWS * LANES)
    return jnp.pad(flat, (0, pad)).reshape(-1, LANES)


TRANSPOSED_VIEW = ("b_re", "b_im")


def small_view(name, a):
    if name in TRANSPOSED_VIEW:
        a = a.transpose(0, 1, 3, 2)
    return a.reshape(-1, a.shape[-1])


def small_unview(name, p, shape):
    if name in TRANSPOSED_VIEW:
        return p.reshape(shape[0], shape[1], shape[3], shape[2]).transpose(0, 1, 3, 2)
    return p.reshape(shape)


def _unpack(packed, shapes):
    flat = packed.reshape(-1)
    out, off = [], 0
    for s in shapes:
        n = int(np.prod(s))
        out.append(flat[off:off + n].reshape(s))
        off += n
    return out


def kernel(x, positions, g_pre_mix, w_in, sinks, a_re, a_im, log_dt, b_re, b_im, c_re, c_im, d_skip, w_glu, b_glu, g_attn_out, g_ssm_out, w_o, g_post_mix, g_pre_ffn, w_gate, w_up, w_down, g_post_ffn, loss_target, m_g_pre_mix, m_w_in, m_sinks, m_a_re, m_a_im, m_log_dt, m_b_re, m_b_im, m_c_re, m_c_im, m_d_skip, m_w_glu, m_b_glu, m_g_attn_out, m_g_ssm_out, m_w_o, m_g_post_mix, m_g_pre_ffn, m_w_gate, m_w_up, m_w_down, m_g_post_ffn, v_g_pre_mix, v_w_in, v_sinks, v_a_re, v_a_im, v_log_dt, v_b_re, v_b_im, v_c_re, v_c_im, v_d_skip, v_w_glu, v_b_glu, v_g_attn_out, v_g_ssm_out, v_w_o, v_g_post_mix, v_g_pre_ffn, v_w_gate, v_w_up, v_w_down, v_g_post_ffn):
    args = dict(locals())
    w = {n: args[n] for n in WEIGHTS}
    m = {n: args["m_" + n] for n in WEIGHTS}
    v = {n: args["v_" + n] for n in WEIGHTS}
    L, D = x.shape[1], x.shape[2]

    ax, ay, ac = _place()
    mine_arr = (2 * ax + ay).astype(jnp.int32).reshape(1)
    c_arr = ac.astype(jnp.int32).reshape(1)

    me_arr = (4 * ax + 2 * ay + ac).astype(jnp.int32).reshape(1)

    bufs = {"w_in": into_slot("cast_w_in", w["w_in"][0], mine_arr, N_CHIPS, bf16)}
    (first_sems,), (bufs["w_in"],), token = gather_start("gather_start_in", [bufs["w_in"]], [[0]], mine_arr)
    sems = {"w_in": first_sems}
    for n in BIG[1:]:
        bufs[n] = into_slot("cast_" + n, w[n][0], mine_arr, N_CHIPS, bf16, dep=token)
    first = ["w_glu", "w_o", "w_gate", "w_up"]
    (sems["w_glu"], sems["w_gate"], sems["w_up"]), started, token = gather_start(
        "gather_start_rest", [bufs[n] for n in first], [[0, 1], [2], [3]], token, neighbours_only=(2, 3))
    bufs.update(zip(first, started))
    relays, ready = {}, set()

    def tick(name, after):
        n, more = ("w_gate", []) if name == "attn" else ("w_up", ["w_down"])
        r_send, r_recv, sems["w_down"], (bufs[n],), down, tok = gather_relay(
            "gather_relay_" + n, [bufs[n]], [sems[n]], [bufs[k] for k in more], after)
        bufs.update(zip(more, down))
        relays[n] = (r_send, r_recv)
        return tok

    def get_w(n, after):
        if n not in ready:
            members = [g for g in GATHER_GROUPS if n in g][0]
            if members[0] in relays:
                landed = [gather_wait_relay("gather_wait_" + k, [bufs[k]], *relays[k], after)[0] for k in members]
                which = (2,)
            else:
                landed = gather_wait("gather_wait_" + members[0], [bufs[k] for k in members], *sems[members[0]], after)
                which = (0, 1, 2)
            bufs.update(zip(members, gather_forward("gather_forward_" + members[0], landed, which)))
            ready.update(members)
        g = bufs[n]
        return g if n in COL_SHARDED else g.reshape(g.shape[0] * g.shape[1], g.shape[2])

    swaps, inflight = {}, []

    def put_g(n, g):
        g3 = g if n in COL_SHARDED else g.reshape(N_CHIPS, g.shape[0] // N_CHIPS, g.shape[1])
        swaps[n] = sibling_start("swap_start_" + n, [g3], True)
        for gi, members in enumerate(REDUCE_GROUPS):
            if n == members[-1]:
                last = swaps[n][4]
                pair = []
                for k in members:
                    send, recv, srcs, lands, _ = swaps[k]
                    (src,), (got,) = sibling_wait("swap_wait_" + k, srcs, lands, send, recv, True, last)
                    pair.append(pair_sum("pair_sum_" + k, src, got, c_arr))
                send, recv, parts, lands, tok = scatter_start("scatter_start_%d" % gi, pair)
                inflight.append((members, send, recv, parts, lands))
                return tok
        return swaps[n][4]

    small = {n: w[n].reshape(1, -1) for n in SMALL}
    pos = positions.reshape(L, 1).astype(f32)
    d_in = N_CHIPS * w["w_in"].shape[2]
    loss, grad_x, small_grads = local_step(x[0], pos, loss_target[0], small, d_in, get_w, put_g, first_dep=token, tick=tick)

    shapes = [w[n].shape for n in SMALL]
    blocks = into_slot("small_block", _pack([small_grads[n] for n in SMALL] + [loss]), me_arr, N_DEV, f32)
    small_send, small_recv, blocks, after = everyone_start("small_start", blocks)

    grads, delta, new_m, new_v = {}, {}, {}, {}
    for gi, (members, send, recv, parts, lands) in enumerate(inflight):
        parts, landed = scatter_wait("scatter_wait_%d" % gi, parts, lands, send, recv, after)
        joins, dep = [], None
        for k, p, t in zip(members, parts, landed):
            joins.append(sibling_start("join_start_" + k, [chip_sum("chip_sum_" + k, p, t, mine_arr, dep=dep)], False))
            dep = after = joins[-1][4]
        for n, (send, recv, srcs, lands, _) in zip(members, joins):
            (own,), (sib,) = sibling_wait("join_wait_" + n, srcs, lands, send, recv, False, after)
            g_, d_, m_, v_ = adamw_halves("adamw_" + n, w[n][0], own, sib, m[n][0], v[n][0], c_arr)
            grads[n], delta[n], new_m[n], new_v[n] = g_[None], d_[None], m_[None], v_[None]
            after = v_
    blocks = everyone_wait("small_wait", blocks, small_send, small_recv, after)
    small_sum = sum_slots("small_sum", blocks)
    *small_g, loss = _unpack(small_sum, [small_view(n, w[n]).shape for n in SMALL] + [loss.shape])
    loss = loss[0, 0]
    outs = adamw_many("adamw_small", [small_view(n, w[n]) for n in SMALL], small_g,
                      [small_view(n, m[n]) for n in SMALL], [small_view(n, v[n]) for n in SMALL])
    for t, parts in zip((grads, delta, new_m, new_v), (small_g,) + tuple(outs)):
        t.update({n: small_unview(n, p, w[n].shape) for n, p in zip(SMALL, parts)})

    return (loss, grad_x[None], *[grads[n] for n in WEIGHTS], *[delta[n] for n in WEIGHTS],
            *[new_m[n] for n in WEIGHTS], *[new_v[n] for n in WEIGHTS])
```

```python
import functools
import math

import jax
import jax.numpy as jnp
import numpy as np
from jax import lax
from jax.experimental import pallas as pl
from jax.experimental.pallas import tpu as pltpu

f32 = jnp.float32
bf16 = jnp.bfloat16
HIGHEST = lax.Precision.HIGHEST
MESH = pl.DeviceIdType.MESH

HEAD_DIM = 64
N_KV_HEADS = 4
ATTN_BLOCK = 128
ROPE_THETA = 10000.0
SSM_GROUP = 16
SSM_STATE = 64
RMS_EPS = 1e-6
LANES = 128
SUBLANES = 8
VMEM_LIMIT = 52 * 1024 * 1024
N_CHIPS = 4
N_DEV = 8
NEG = -1e30

ADAM_LR, ADAM_B1, ADAM_B2, ADAM_EPS, ADAM_WD, ADAM_STEP = 0.001, 0.9, 0.999, 1e-08, 0.01, 10

NN = (((1,), (0,)), ((), ()))
NT = (((1,), (1,)), ((), ()))
TN = (((0,), (0,)), ((), ()))


def _params(*sem):
    return pltpu.CompilerParams(dimension_semantics=sem or None, vmem_limit_bytes=VMEM_LIMIT)


def _dot(a, b, dims=NN):
    return lax.dot_general(a, b, dims, preferred_element_type=f32)


def _pick(dim, pref):
    t = min(dim, pref)
    while dim % t:
        t -= LANES
    assert t > 0, (dim, pref)
    return t


ANY = pl.BlockSpec(memory_space=pl.ANY)


def _with_dep(in_specs, operands, dep):
    if dep is None:
        return list(in_specs), list(operands), 0
    return list(in_specs) + [ANY], list(operands) + [dep], 1


def _mm_call(name, grid, in_specs, out_spec, out_shape, acc_shape, dims, operands, dep=None):
    nk = grid[2]
    in_specs, operands, n_dep = _with_dep(in_specs, operands, dep)

    def body_one(a_ref, b_ref, *rest):
        o_ref = rest[n_dep]
        o_ref[...] = _dot(a_ref[...], b_ref[...], dims).astype(o_ref.dtype)

    def body(a_ref, b_ref, *rest):
        o_ref, acc_ref = rest[n_dep], rest[n_dep + 1]
        k = pl.program_id(2)

        @pl.when(k == 0)
        def _():
            acc_ref[...] = _dot(a_ref[...], b_ref[...], dims)

        @pl.when((k > 0) & (k < nk - 1))
        def _():
            acc_ref[...] += _dot(a_ref[...], b_ref[...], dims)

        @pl.when(k == nk - 1)
        def _():
            o_ref[...] = (acc_ref[...] + _dot(a_ref[...], b_ref[...], dims)).astype(o_ref.dtype)

    return pl.pallas_call(
        body_one if nk == 1 else body, out_shape=out_shape, grid=grid, in_specs=in_specs, out_specs=out_spec,
        scratch_shapes=[] if nk == 1 else [pltpu.VMEM(acc_shape, f32)], name=name,
        compiler_params=_params("parallel", "parallel", "arbitrary"))(*operands)


def mm_nt_pair(name, a1, b1, a2, b2, tm=1024, tk=1024, dep=None):
    M = a1.shape[0]
    S, K, n = b1.shape
    tm, tko = _pick(M, tm), _pick(K, tk)
    nk = 2 * S

    def body(a1_ref, b1_ref, a2_ref, b2_ref, *rest):
        o_ref, acc_ref = rest[-2], rest[-1]
        k = pl.program_id(2)

        @pl.when(k == 0)
        def _():
            acc_ref[...] = _dot(a1_ref[...], b1_ref[...], NT)

        @pl.when((k > 0) & (k < S))
        def _():
            acc_ref[...] += _dot(a1_ref[...], b1_ref[...], NT)

        @pl.when((k >= S) & (k < nk - 1))
        def _():
            acc_ref[...] += _dot(a2_ref[...], b2_ref[...], NT)

        @pl.when(k == nk - 1)
        def _():
            o_ref[...] = acc_ref[...] + _dot(a2_ref[...], b2_ref[...], NT)

    first = lambda k: jnp.minimum(k, S - 1)
    second = lambda k: jnp.maximum(k - S, 0)
    in_specs = [pl.BlockSpec((tm, n), lambda i, j, k: (i, first(k))), pl.BlockSpec((None, tko, n), lambda i, j, k: (first(k), j, 0)),
                pl.BlockSpec((tm, n), lambda i, j, k: (i, second(k))), pl.BlockSpec((None, tko, n), lambda i, j, k: (second(k), j, 0))]
    in_specs, operands, _ = _with_dep(in_specs, (a1, b1, a2, b2), dep)
    return pl.pallas_call(
        body, out_shape=jax.ShapeDtypeStruct((M, K), f32), grid=(M // tm, K // tko, nk), in_specs=in_specs,
        out_specs=pl.BlockSpec((tm, tko), lambda i, j, k: (i, j)), scratch_shapes=[pltpu.VMEM((tm, tko), f32)], name=name,
        compiler_params=_params("parallel", "parallel", "arbitrary"))(*operands)


def mm_nn(name, a, b, out_dtype=f32, tm=1024, tn=1024, tk=2048, dep=None):
    M, K = a.shape
    tm, tk = _pick(M, tm), _pick(K, tk)
    if b.ndim == 3:
        S, _, n = b.shape
        tn = _pick(n, 2048)
        per = n // tn
        b_spec = pl.BlockSpec((None, tk, tn), lambda i, j, k: (j // per, k, j % per))
        N = S * n
    else:
        N = b.shape[1]
        tn = _pick(N, tn)
        b_spec = pl.BlockSpec((tk, tn), lambda i, j, k: (k, j))
    grid = (M // tm, N // tn, K // tk)
    return _mm_call(name, grid, [pl.BlockSpec((tm, tk), lambda i, j, k: (i, k)), b_spec],
                    pl.BlockSpec((tm, tn), lambda i, j, k: (i, j)), jax.ShapeDtypeStruct((M, N), out_dtype),
                    (tm, tn), NN, (a, b), dep)


def mm_nt(name, a, b, out_dtype=f32, tm=1024, tn=2048, tk=1024, dep=None):
    M, N = a.shape
    tm = _pick(M, tm)
    if b.ndim == 3:
        S, K, n = b.shape
        tr = _pick(n, 2048)
        per = n // tr
        tko = _pick(K, tk)
        b_spec = pl.BlockSpec((None, tko, tr), lambda i, j, k: (k // per, j, k % per))
    else:
        K = b.shape[0]
        tr = _pick(N, tn)
        tko = _pick(K, tk)
        b_spec = pl.BlockSpec((tko, tr), lambda i, j, k: (j, k))
    grid = (M // tm, K // tko, N // tr)
    return _mm_call(name, grid, [pl.BlockSpec((tm, tr), lambda i, j, k: (i, k)), b_spec],
                    pl.BlockSpec((tm, tko), lambda i, j, k: (i, j)), jax.ShapeDtypeStruct((M, K), out_dtype),
                    (tm, tko), NT, (a, b), dep)


def mm_tn(name, a, b, shards=None, out_dtype=f32, tm=1024, tn=1024, tl=2048, dep=None):
    L, K = a.shape
    N = b.shape[1]
    tl, tko = _pick(L, tl), _pick(K, tm)
    if shards:
        n = N // shards
        tn = _pick(n, 2048)
        per = n // tn
        o_spec = pl.BlockSpec((None, tko, tn), lambda i, j, k: (j // per, i, j % per))
        o_shape = jax.ShapeDtypeStruct((shards, K, n), out_dtype)
    else:
        tn = _pick(N, tn)
        o_spec = pl.BlockSpec((tko, tn), lambda i, j, k: (i, j))
        o_shape = jax.ShapeDtypeStruct((K, N), out_dtype)
    grid = (K // tko, N // tn, L // tl)
    return _mm_call(name, grid, [pl.BlockSpec((tl, tko), lambda i, j, k: (k, i)),
                                 pl.BlockSpec((tl, tn), lambda i, j, k: (k, j))],
                    o_spec, o_shape, (tko, tn), TN, (a, b), dep)


def ffn_hidden(hn, w_gate, w_up, tm=512):
    M, K = hn.shape
    S, _, n = w_gate.shape
    tm = _pick(M, tm)

    def body(a_ref, g_ref, u_ref, gt_ref, up_ref, hid_ref):
        a = a_ref[...]
        g = _dot(a, g_ref[...])
        u = _dot(a, u_ref[...])
        gt_ref[...] = g.astype(bf16)
        up_ref[...] = u.astype(bf16)
        hid_ref[...] = (g * sigmoid(g) * u).astype(bf16)

    w_spec = pl.BlockSpec((None, K, n), lambda s, i: (s, 0, 0))
    o_spec = pl.BlockSpec((tm, n), lambda s, i: (i, s))
    o = jax.ShapeDtypeStruct((M, S * n), bf16)
    return pl.pallas_call(
        body, out_shape=[o, o, o], grid=(S, M // tm), in_specs=[pl.BlockSpec((tm, K), lambda s, i: (i, 0)), w_spec, w_spec],
        out_specs=[o_spec, o_spec, o_spec], name="ffn_hidden", compiler_params=_params("parallel", "parallel"))(hn, w_gate, w_up)


def ffn_hidden_grad(d_ff, w_down, gt, up, tm=512):
    M, D = d_ff.shape
    F = w_down.shape[0]
    n = _pick(F // N_CHIPS, 2048)
    tm = _pick(M, tm)

    def body(a_ref, b_ref, gt_ref, up_ref, dg_ref, du_ref):
        dh = _dot(a_ref[...], b_ref[...], NT)
        g = gt_ref[...].astype(f32)
        sg = sigmoid(g)
        dg_ref[...] = (dh * up_ref[...].astype(f32) * (sg * (1.0 + g * (1.0 - sg)))).astype(bf16)
        du_ref[...] = (dh * (g * sg)).astype(bf16)

    t_spec = pl.BlockSpec((tm, n), lambda j, i: (i, j))
    o = jax.ShapeDtypeStruct((M, F), bf16)
    return pl.pallas_call(
        body, out_shape=[o, o], grid=(F // n, M // tm),
        in_specs=[pl.BlockSpec((tm, D), lambda j, i: (i, 0)), pl.BlockSpec((n, D), lambda j, i: (j, 0)), t_spec, t_spec],
        out_specs=[t_spec, t_spec], name="ffn_hidden_grad", compiler_params=_params("parallel", "parallel"))(d_ff, w_down, gt, up)


def rowwise(name, fn, L, rows, bcast, outs, reds=(), tr=256, dep=None):
    tr = min(tr, L)
    nt = L // tr
    n_rows, n_b, n_o = len(rows), len(bcast), len(outs)
    n_dep = 0 if dep is None else 1

    def body(*refs):
        i = pl.program_id(0)
        ins = [r[...] for r in refs[:n_rows + n_b]]
        res = fn(i, nt, *ins)
        o_refs = refs[n_rows + n_b + n_dep:]
        for k in range(n_o):
            o_refs[k][...] = res[k].astype(o_refs[k].dtype)
        if reds:
            @pl.when(i == 0)
            def _():
                for k in range(len(reds)):
                    o_refs[n_o + k][...] = jnp.zeros_like(o_refs[n_o + k])
            for k in range(len(reds)):
                o_refs[n_o + k][...] += res[n_o + k]

    def row_spec(width, cb, shift):
        if shift:
            return pl.BlockSpec((tr, width), lambda i: (jnp.minimum(i + shift, nt - 1), cb))
        return pl.BlockSpec((tr, width), lambda i: (i, cb))

    in_specs = [row_spec(w, cb, sh) for (_, w, cb, sh) in rows]
    in_specs += [pl.BlockSpec(b.shape, lambda i: (0, 0)) for b in bcast]
    out_specs = [pl.BlockSpec((tr, w), lambda i: (i, 0)) for (w, _) in outs]
    out_specs += [pl.BlockSpec((1, w), lambda i: (0, 0)) for w in reds]
    out_shape = [jax.ShapeDtypeStruct((L, w), dt) for (w, dt) in outs]
    out_shape += [jax.ShapeDtypeStruct((1, w), f32) for w in reds]
    in_specs, operands, _ = _with_dep(in_specs, [r[0] for r in rows] + list(bcast), dep)
    return pl.pallas_call(
        body, out_shape=out_shape, grid=(nt,), in_specs=in_specs, out_specs=out_specs, name=name,
        compiler_params=_params("arbitrary"))(*operands)


def full(a):
    return (a, a.shape[1], 0, 0)


def colsum(v):
    return jnp.sum(v, axis=0, keepdims=True)


def rms_fwd(x, g):
    r = lax.rsqrt(jnp.mean(x * x, axis=-1, keepdims=True) + RMS_EPS)
    return x * r * g


def rms_bwd(x, g, dy):
    r = lax.rsqrt(jnp.mean(x * x, axis=-1, keepdims=True) + RMS_EPS)
    xh = x * r
    dyg = dy * g
    dx = r * (dyg - xh * jnp.mean(dyg * xh, axis=-1, keepdims=True))
    return dx, colsum(dy * xh)


GELU_C = math.sqrt(2.0 / math.pi)


def gelu(y):
    return y * (0.5 * (1.0 + jnp.tanh(GELU_C * (y + 0.044715 * (y * y * y)))))


def gelu_grad(y):
    t = jnp.tanh(GELU_C * (y + 0.044715 * (y * y * y)))
    return 0.5 * (1.0 + t) + 0.5 * y * (1.0 - t * t) * (GELU_C * (1.0 + 3 * 0.044715 * (y * y)))


def sigmoid(v):
    return 1.0 / (1.0 + jnp.exp(-v))


def _lane(shape):
    return lax.broadcasted_iota(jnp.int32, shape, 1)


def _rot_chunk(t, cos, sin_signed):
    first = (_lane(t.shape) % HEAD_DIM) < (HEAD_DIM // 2)
    partner = jnp.where(first, pltpu.roll(t, LANES - HEAD_DIM // 2, 1), pltpu.roll(t, HEAD_DIM // 2, 1))
    return t * cos + partner * sin_signed


def _cos_sin(pos, inv_freq, inverse):
    ang = pos * inv_freq
    cos, sin = jnp.cos(ang), jnp.sin(ang)
    first = (_lane(ang.shape) % HEAD_DIM) < (HEAD_DIM // 2)
    sign = jnp.where(first, -1.0, 1.0) * (-1.0 if inverse else 1.0)
    return cos, sin * sign


def _dup_head(chunk, odd):
    low = _lane(chunk.shape) < HEAD_DIM
    x = jnp.where(low != odd, chunk, 0.0)
    return x + pltpu.roll(x, HEAD_DIM, 1)


def _chunks(v):
    return [v[:, LANES * c:LANES * (c + 1)] for c in range(v.shape[1] // LANES)]


def qkv_prep(proj, pos, inv_freq, d_attn, d_kv):
    L = proj.shape[0]
    d_ssm = proj.shape[1] - d_attn - 2 * d_kv
    half = d_ssm // 2
    scale = 1.0 / math.sqrt(HEAD_DIM)

    def fn(i, nt, q, k, v, u0, u1, p, invf):
        cos, sin = _cos_sin(p, invf, False)
        qr = jnp.concatenate([_rot_chunk(c, cos, sin) for c in _chunks(q)], axis=1) * scale
        kr = [_rot_chunk(c, cos, sin) for c in _chunks(k)]
        kk = jnp.concatenate([_dup_head(c, odd) for c in kr for odd in (False, True)], axis=1)
        vv = jnp.concatenate([_dup_head(c, odd) for c in _chunks(v) for odd in (False, True)], axis=1)
        return qr, kk, vv, jnp.concatenate([u0, u1], axis=1)

    u_cb = (d_attn + 2 * d_kv) // half
    return rowwise("qkv_prep", fn, L,
                   [(proj, d_attn, 0, 0), (proj, d_kv, d_attn // d_kv, 0), (proj, d_kv, d_attn // d_kv + 1, 0),
                    (proj, half, u_cb, 0), (proj, half, u_cb + 1, 0), full(pos)],
                   [inv_freq], [(d_attn, bf16), (2 * d_kv, bf16), (2 * d_kv, bf16), (d_ssm, bf16)])


def qkv_grad(dq, dkk_c, dkk_p, dvv_c, dvv_p, du, pos, inv_freq):
    L, d_attn = dq.shape
    d_kv = dkk_c.shape[1] // 2
    scale = 1.0 / math.sqrt(HEAD_DIM)

    def fold(cur, prev, i, nt):
        t = cur + jnp.where(i < nt - 1, prev, 0.0)
        out = []
        for c in range(t.shape[1] // (2 * LANES)):
            even, odd = t[:, 2 * c * LANES:(2 * c + 1) * LANES], t[:, (2 * c + 1) * LANES:(2 * c + 2) * LANES]
            even, odd = even + pltpu.roll(even, HEAD_DIM, 1), odd + pltpu.roll(odd, HEAD_DIM, 1)
            out.append(jnp.where(_lane(even.shape) < HEAD_DIM, even, odd))
        return out

    def fn(i, nt, dq_t, kc, kp, vc, vp, du_t, p, invf):
        cos, sin = _cos_sin(p, invf, True)
        dq_o = jnp.concatenate([_rot_chunk(c, cos, sin) for c in _chunks(dq_t)], axis=1) * scale
        dk_o = jnp.concatenate([_rot_chunk(c, cos, sin) for c in fold(kc, kp, i, nt)], axis=1)
        dv_o = jnp.concatenate(fold(vc, vp, i, nt), axis=1)
        return (jnp.concatenate([dq_o, dk_o, dv_o, du_t], axis=1),)

    return rowwise("qkv_grad", fn, L,
                   [full(dq), full(dkk_c), (dkk_p, 2 * d_kv, 0, 1), full(dvv_c), (dvv_p, 2 * d_kv, 0, 1), full(du), full(pos)],
                   [inv_freq], [(d_attn + 2 * d_kv + du.shape[1], bf16)], tr=ATTN_BLOCK)[0]


def _attn_specs(L):
    nb = L // ATTN_BLOCK
    B = ATTN_BLOCK
    q_spec = lambda width: pl.BlockSpec((B, width), lambda n: (n, 0))
    prev = lambda width: pl.BlockSpec((B, width), lambda n: (jnp.maximum(n - 1, 0), 0))
    return nb, q_spec, prev


def _attn_mask(n):
    B = ATTN_BLOCK
    row = lax.broadcasted_iota(jnp.int32, (B, 2 * B), 0)
    col = lax.broadcasted_iota(jnp.int32, (B, 2 * B), 1)
    return ((col < B) & (col > row) & (n > 0)) | ((col >= B) & (row >= col - B))


def _attn_probs(qm, kcat, sink, mask):
    s = jnp.where(mask, _dot(qm, kcat, NT), NEG)
    m = jnp.maximum(jnp.max(s, axis=1, keepdims=True), sink)
    p, ps = jnp.exp(s - m), jnp.exp(sink - m)
    inv = 1.0 / (jnp.sum(p, axis=1, keepdims=True) + ps)
    return p, inv, ps


def _attn_heads(q_ref, s_ref, h, q_per_kv):
    low = _lane((ATTN_BLOCK, LANES)) < HEAD_DIM
    heads = []
    for pr in range(h * q_per_kv // 2, (h + 1) * q_per_kv // 2):
        q2 = q_ref[:, LANES * pr:LANES * (pr + 1)]
        for odd in (False, True):
            mine = low != odd
            sink = jnp.max(s_ref[2 * pr + int(odd):2 * pr + int(odd) + 1, :], axis=1, keepdims=True)
            heads.append((pr, mine, jnp.where(mine, q2, jnp.zeros_like(q2)), sink))
    return low, heads


def _kv_block(prev_ref, cur_ref, h):
    return jnp.concatenate([prev_ref[:, LANES * h:LANES * (h + 1)], cur_ref[:, LANES * h:LANES * (h + 1)]], axis=0)


def attn_fwd(qr, kk, vv, sink_b):
    L, d_attn = qr.shape
    nb, q_spec, prev = _attn_specs(L)
    d_kk = kk.shape[1]
    n_kv = d_kk // LANES
    q_per_kv = d_attn // HEAD_DIM // n_kv

    def body(q_ref, kc_ref, kp_ref, vc_ref, vp_ref, s_ref, o_ref):
        mask = _attn_mask(pl.program_id(0))
        for h in range(n_kv):
            kcat, vcat = _kv_block(kp_ref, kc_ref, h), _kv_block(vp_ref, vc_ref, h)
            low, heads = _attn_heads(q_ref, s_ref, h, q_per_kv)
            probs = [_attn_probs(qm, kcat, sink, mask) for (_, _, qm, sink) in heads]
            outs = [_dot(p.astype(bf16), vcat) * inv for (p, inv, _) in probs]
            for i in range(0, len(heads), 2):
                pr = heads[i][0]
                o_ref[:, LANES * pr:LANES * (pr + 1)] = jnp.where(low, outs[i], outs[i + 1])

    return pl.pallas_call(
        body, out_shape=jax.ShapeDtypeStruct((L, d_attn), f32), grid=(nb,),
        in_specs=[q_spec(d_attn), q_spec(d_kk), prev(d_kk), q_spec(d_kk), prev(d_kk), pl.BlockSpec(sink_b.shape, lambda n: (0, 0))],
        out_specs=q_spec(d_attn), name="attn_fwd", compiler_params=_params("arbitrary"))(qr, kk, kk, vv, vv, sink_b)


def attn_bwd(qr, kk, vv, sink_b, attn, d_attn_out):
    L, d_attn = qr.shape
    nb, q_spec, prev = _attn_specs(L)
    d_kk = kk.shape[1]
    n_kv = d_kk // LANES
    q_per_kv = d_attn // HEAD_DIM // n_kv

    def body(q_ref, kc_ref, kp_ref, vc_ref, vp_ref, s_ref, o_ref, do_ref, dq_ref, dkc_ref, dkp_ref, dvc_ref, dvp_ref, ds_ref):
        n = pl.program_id(0)
        B = ATTN_BLOCK
        mask = _attn_mask(n)
        srow = lax.broadcasted_iota(jnp.int32, (SUBLANES, LANES), 0)

        @pl.when(n == 0)
        def _():
            ds_ref[...] = jnp.zeros_like(ds_ref)

        for h in range(n_kv):
            kcat, vcat = _kv_block(kp_ref, kc_ref, h), _kv_block(vp_ref, vc_ref, h)
            low, heads = _attn_heads(q_ref, s_ref, h, q_per_kv)
            probs = [_attn_probs(qm, kcat, sink, mask) for (_, _, qm, sink) in heads]
            dk = jnp.zeros((2 * B, LANES), f32)
            dv = dk
            dsink = jnp.zeros((SUBLANES, LANES), f32)
            dqs = []
            for i, ((pr, mine, qm, _), (p, inv, ps)) in enumerate(zip(heads, probs)):
                do2 = do_ref[:, LANES * pr:LANES * (pr + 1)]
                delta = jnp.sum(jnp.where(mine, do2 * o_ref[:, LANES * pr:LANES * (pr + 1)], 0.0), axis=1, keepdims=True)
                dob = jnp.where(mine, do2, 0.0).astype(bf16)
                p = p * inv
                ds = (p * (_dot(dob, vcat, NT) - delta)).astype(bf16)
                dqs.append(_dot(ds, kcat))
                dk = dk + _dot(ds, qm, TN)
                dv = dv + _dot(p.astype(bf16), dob, TN)
                dsink = dsink + jnp.where(srow == i, -jnp.sum(ps * inv * delta), 0.0)
            for i in range(0, len(heads), 2):
                pr = heads[i][0]
                dq_ref[:, LANES * pr:LANES * (pr + 1)] = jnp.where(low, dqs[i], dqs[i + 1])
            cols = slice(LANES * h, LANES * (h + 1))
            dkp_ref[:, cols] = dk[:B]
            dkc_ref[:, cols] = dk[B:]
            dvp_ref[:, cols] = dv[:B]
            dvc_ref[:, cols] = dv[B:]
            ds_ref[h] += dsink

    kv_shape = jax.ShapeDtypeStruct(kk.shape, f32)
    ds_shape = (n_kv, SUBLANES, LANES)
    return pl.pallas_call(
        body,
        out_shape=[jax.ShapeDtypeStruct((L, d_attn), f32), kv_shape, kv_shape, kv_shape, kv_shape, jax.ShapeDtypeStruct(ds_shape, f32)],
        grid=(nb,),
        in_specs=[q_spec(d_attn), q_spec(d_kk), prev(d_kk), q_spec(d_kk), prev(d_kk), pl.BlockSpec(sink_b.shape, lambda n: (0, 0)),
                  q_spec(d_attn), q_spec(d_attn)],
        out_specs=[q_spec(d_attn)] + [q_spec(d_kk)] * 4 + [pl.BlockSpec(ds_shape, lambda n: (0, 0, 0))],
        name="attn_bwd", compiler_params=_params("arbitrary"))(qr, kk, kk, vv, vv, sink_b, attn, d_attn_out)


SSM_T = 128
NQ = SUBLANES * SSM_STATE // LANES
NJ = SUBLANES


def _strided_put(ref, j, val):
    for q in range(NQ):
        ref.at[q][pl.ds(j, SSM_T, stride=NJ), :] = val[:, LANES * q:LANES * (q + 1)]


def _strided_get(ref, j):
    return jnp.concatenate([ref.at[q][pl.ds(j, SSM_T, stride=NJ), :] for q in range(NQ)], axis=1)


def _ssm_specs(L, rev):
    nt = L // SSM_T
    idx = (lambda i: nt - 1 - i) if rev else (lambda i: i)
    row = lambda w, cb=0: pl.BlockSpec((SSM_T, w), lambda i: (idx(i), cb))
    state = pl.BlockSpec((NQ, SSM_T * NJ, LANES), lambda i: (0, idx(i), 0))
    whole = lambda a: pl.BlockSpec(a.shape, lambda i: (0,) * a.ndim)
    return nt, row, state, whole


def ssm_fwd(u_bf, proj, u_cb, bd_re, bd_im, cd_re, cd_im, lam_re, lam_im, d_skip, dep=None):
    L, d_ssm = u_bf.shape
    nt, row, state, whole = _ssm_specs(L, False)
    half = d_ssm // 2
    gw = d_ssm // NJ
    n_dep = 0 if dep is None else 1

    def body(u_ref, u0_ref, u1_ref, bdr, bdi, cdr, cdi, lr_ref, li_ref, d_ref, *rest):
        y_ref, z_ref, sr_ref, si_ref, carry = rest[n_dep:]
        i = pl.program_id(0)

        @pl.when(i == 0)
        def _():
            carry[...] = jnp.zeros_like(carry)

        for j in range(NJ):
            uj = u_ref[:, gw * j:gw * (j + 1)]
            _strided_put(sr_ref, j, _dot(uj, bdr[j]))
            _strided_put(si_ref, j, _dot(uj, bdi[j]))
        lr = [lr_ref[q] for q in range(NQ)]
        li = [li_ref[q] for q in range(NQ)]

        def step(t, s):
            sr, si = s
            rows = pl.ds(pl.multiple_of(t * NJ, NJ), NJ)
            nr = tuple(lr[q] * sr[q] - li[q] * si[q] + sr_ref[q, rows, :] for q in range(NQ))
            ni = tuple(lr[q] * si[q] + li[q] * sr[q] + si_ref[q, rows, :] for q in range(NQ))
            for q in range(NQ):
                sr_ref[q, rows, :] = nr[q]
                si_ref[q, rows, :] = ni[q]
            return nr, ni

        init = (tuple(carry[0, q] for q in range(NQ)), tuple(carry[1, q] for q in range(NQ)))
        sr, si = lax.fori_loop(0, SSM_T, step, init, unroll=8)
        for q in range(NQ):
            carry[0, q] = sr[q]
            carry[1, q] = si[q]
        uf = jnp.concatenate([u0_ref[...], u1_ref[...]], axis=1)
        for j in range(NJ):
            cols = slice(gw * j, gw * (j + 1))
            yj = _dot(_strided_get(sr_ref, j).astype(bf16), cdr[j]) - _dot(_strided_get(si_ref, j).astype(bf16), cdi[j])
            yj = yj + d_ref[:, cols] * uf[:, cols]
            y_ref[:, cols] = yj
            z_ref[:, cols] = gelu(yj).astype(bf16)

    s_shape = jax.ShapeDtypeStruct((NQ, L * NJ, LANES), f32)
    consts = (bd_re, bd_im, cd_re, cd_im, lam_re, lam_im, d_skip)
    in_specs, operands, _ = _with_dep([row(d_ssm), row(half, u_cb), row(half, u_cb + 1)] + [whole(a) for a in consts],
                                      [u_bf, proj, proj, *consts], dep)
    return pl.pallas_call(
        body, out_shape=[jax.ShapeDtypeStruct((L, d_ssm), f32), jax.ShapeDtypeStruct((L, d_ssm), bf16), s_shape, s_shape], grid=(nt,),
        in_specs=in_specs, out_specs=[row(d_ssm), row(d_ssm), state, state],
        scratch_shapes=[pltpu.VMEM((2, NQ, NJ, LANES), f32)], name="ssm_fwd",
        compiler_params=_params("arbitrary"))(*operands)


def ssm_bwd(y, dz1, dz2, u_bf, proj, u_cb, s_re, s_im, bd_re, bd_im, cd_re, cd_im, lam_re, lam_im, d_skip, dep=None):
    L, d_ssm = y.shape
    nt, row, state, whole = _ssm_specs(L, True)
    half = d_ssm // 2
    gw = d_ssm // NJ
    n_dep = 0 if dep is None else 1

    def body(y_ref, dz1_ref, dz2_ref, u_ref, u0_ref, u1_ref, sr_ref, si_ref, bdr, bdi, cdr, cdi, lr_ref, li_ref, d_ref, *rest):
        du_ref, fbr, fbi, fcr, fci, dlr, dli, dd_ref, gr_ref, gi_ref, carry, dbdr, dbdi, dcdr, dcdi = rest[n_dep:]
        i = pl.program_id(0)

        @pl.when(i == 0)
        def _():
            carry[...] = jnp.zeros_like(carry)
            for r in (dbdr, dbdi, dcdr, dcdi, dlr, dli, dd_ref):
                r[...] = jnp.zeros_like(r)

        dyf = (dz1_ref[...] + dz2_ref[...]) * gelu_grad(y_ref[...])
        dyb = dyf.astype(bf16)
        for j in range(NJ):
            dyj = dyb[:, gw * j:gw * (j + 1)]
            _strided_put(gr_ref, j, _dot(dyj, cdr[j], NT))
            _strided_put(gi_ref, j, -_dot(dyj, cdi[j], NT))
            dcdr[j] += _dot(_strided_get(sr_ref, j).astype(bf16), dyj, TN)
            dcdi[j] -= _dot(_strided_get(si_ref, j).astype(bf16), dyj, TN)
        lr = [lr_ref[q] for q in range(NQ)]
        li = [li_ref[q] for q in range(NQ)]

        def step(k, c):
            gr, gi, ar, ai = c
            rows = pl.ds(pl.multiple_of((SSM_T - 1 - k) * NJ, NJ), NJ)
            s_r = [sr_ref[q, rows, :] for q in range(NQ)]
            s_i = [si_ref[q, rows, :] for q in range(NQ)]
            ar = tuple(ar[q] + gr[q] * s_r[q] + gi[q] * s_i[q] for q in range(NQ))
            ai = tuple(ai[q] + gi[q] * s_r[q] - gr[q] * s_i[q] for q in range(NQ))
            nr = tuple(gr_ref[q, rows, :] + lr[q] * gr[q] + li[q] * gi[q] for q in range(NQ))
            ni = tuple(gi_ref[q, rows, :] + lr[q] * gi[q] - li[q] * gr[q] for q in range(NQ))
            for q in range(NQ):
                gr_ref[q, rows, :] = nr[q]
                gi_ref[q, rows, :] = ni[q]
            return nr, ni, ar, ai

        zero = tuple(jnp.zeros((NJ, LANES), f32) for _ in range(NQ))
        init = (tuple(carry[0, q] for q in range(NQ)), tuple(carry[1, q] for q in range(NQ)), zero, zero)
        gr, gi, ar, ai = lax.fori_loop(0, SSM_T, step, init, unroll=8)
        for q in range(NQ):
            carry[0, q] = gr[q]
            carry[1, q] = gi[q]
            dlr[q] += ar[q]
            dli[q] += ai[q]
        uf = jnp.concatenate([u0_ref[...], u1_ref[...]], axis=1)
        dd_ref[...] += colsum(dyf * uf)
        for j in range(NJ):
            cols = slice(gw * j, gw * (j + 1))
            gjr, gji = _strided_get(gr_ref, j).astype(bf16), _strided_get(gi_ref, j).astype(bf16)
            du_ref[:, cols] = _dot(gjr, bdr[j], NT) + _dot(gji, bdi[j], NT) + d_ref[:, cols] * dyf[:, cols]
            uj = u_ref[:, cols]
            dbdr[j] += _dot(uj, gjr, TN)
            dbdi[j] += _dot(uj, gji, TN)

        @pl.when(i == nt - 1)
        def _():
            nb = NJ * SSM_STATE
            b_diag = (lax.broadcasted_iota(jnp.int32, (gw, nb), 0) // SSM_GROUP) == (lax.broadcasted_iota(jnp.int32, (gw, nb), 1) // SSM_STATE)
            c_diag = (lax.broadcasted_iota(jnp.int32, (nb, gw), 0) // SSM_STATE) == (lax.broadcasted_iota(jnp.int32, (nb, gw), 1) // SSM_GROUP)
            for j in range(NJ):
                for acc_ref, out in ((dbdr, fbr), (dbdi, fbi)):
                    m = jnp.where(b_diag, acc_ref[j], 0.0)
                    f = m[:, :LANES]
                    for q in range(1, nb // LANES):
                        f = f + m[:, LANES * q:LANES * (q + 1)]
                    out[j] = f + pltpu.roll(f, SSM_STATE, 1)
                for acc_ref, out in ((dcdr, fcr), (dcdi, fci)):
                    f = jnp.where(c_diag, acc_ref[j], 0.0)
                    for shift in (64, 32, 16):
                        f = f + pltpu.roll(f, shift, 1)
                    out[j] = f

    consts = (bd_re, bd_im, cd_re, cd_im, lam_re, lam_im, d_skip)
    acc = lambda a: jax.ShapeDtypeStruct(a.shape, f32)
    fb = jax.ShapeDtypeStruct((NJ, gw, LANES), f32)
    fc = jax.ShapeDtypeStruct((NJ, NJ * SSM_STATE, LANES), f32)
    outs = [jax.ShapeDtypeStruct((L, d_ssm), f32), fb, fb, fc, fc, acc(lam_re), acc(lam_im), acc(d_skip)]
    in_specs, operands, _ = _with_dep(
        [row(d_ssm)] * 4 + [row(half, u_cb), row(half, u_cb + 1), state, state] + [whole(a) for a in consts],
        [y, dz1, dz2, u_bf, proj, proj, s_re, s_im, *consts], dep)
    return pl.pallas_call(
        body, out_shape=outs, grid=(nt,),
        in_specs=in_specs, out_specs=[row(d_ssm)] + [whole(a) for a in outs[1:]],
        scratch_shapes=[pltpu.VMEM((NQ, SSM_T * NJ, LANES), f32), pltpu.VMEM((NQ, SSM_T * NJ, LANES), f32),
                        pltpu.VMEM((2, NQ, NJ, LANES), f32)] + [pltpu.VMEM(a.shape, f32) for a in (bd_re, bd_im, cd_re, cd_im)],
        name="ssm_bwd", compiler_params=_params("arbitrary"))(*operands)


def _cmul(ar, ai, br, bi):
    return ar * br - ai * bi, ar * bi + ai * br


def _disc(ar, ai, logdt):
    dt = jnp.exp(logdt)
    mag = jnp.exp(ar * dt)
    lr, li = mag * jnp.cos(ai * dt), mag * jnp.sin(ai * dt)
    den = ar * ar + ai * ai
    nr, ni = lr - 1.0, li
    fr, fi = (nr * ar + ni * ai) / den, (ni * ar - nr * ai) / den
    return dt, lr, li, den, fr, fi


def ssm_params(a_re, a_im, logdt_b, bt_re, bt_im, spread):
    def body(ar_ref, ai_ref, ld_ref, br_ref, bi_ref, sp_ref, lr_ref, li_ref, or_ref, oi_ref):
        _, lr, li, _, fr, fi = _disc(ar_ref[...], ai_ref[...], ld_ref[...])
        lr_ref[...] = lr
        li_ref[...] = li
        fre = jnp.dot(sp_ref[...], fr, precision=HIGHEST, preferred_element_type=f32)
        fie = jnp.dot(sp_ref[...], fi, precision=HIGHEST, preferred_element_type=f32)
        o_r, o_i = _cmul(fre, fie, br_ref[...], bi_ref[...])
        or_ref[...] = o_r
        oi_ref[...] = o_i

    g = jax.ShapeDtypeStruct(a_re.shape, f32)
    b = jax.ShapeDtypeStruct(bt_re.shape, f32)
    return pl.pallas_call(body, out_shape=[g, g, b, b], name="ssm_params",
                          compiler_params=_params())(a_re, a_im, logdt_b, bt_re, bt_im, spread)


def ssm_params_grad(a_re, a_im, logdt_b, bt_re, bt_im, spread, gather, dlam_re, dlam_im, dbt_re, dbt_im):
    def body(ar_ref, ai_ref, ld_ref, br_ref, bi_ref, sp_ref, ga_ref, glr_ref, gli_ref, gbr_ref, gbi_ref,
             dar_ref, dai_ref, dld_ref, dbr_ref, dbi_ref):
        ar, ai = ar_ref[...], ai_ref[...]
        dt, lr, li, den, fr, fi = _disc(ar, ai, ld_ref[...])
        hdot = functools.partial(jnp.dot, precision=HIGHEST, preferred_element_type=f32)
        fre, fie = hdot(sp_ref[...], fr), hdot(sp_ref[...], fi)
        gbr, gbi, br, bi = gbr_ref[...], gbi_ref[...], br_ref[...], bi_ref[...]
        dbr_ref[...], dbi_ref[...] = _cmul(fre, -fie, gbr, gbi)
        t_r, t_i = _cmul(br, -bi, gbr, gbi)
        gfr, gfi = hdot(ga_ref[...], t_r), hdot(ga_ref[...], t_i)
        iwr, iwi = ar / den, -ai / den
        x_r, x_i = _cmul(iwr, -iwi, gfr, gfi)
        glr, gli = glr_ref[...] + x_r, gli_ref[...] + x_i
        q_r, q_i = _cmul(fr, fi, iwr, iwi)
        gwr, gwi = _cmul(-q_r, q_i, gfr, gfi)
        y_r, y_i = _cmul(dt * lr, -dt * li, glr, gli)
        dar_ref[...] = gwr + y_r
        dai_ref[...] = gwi + y_i
        wl_r, wl_i = _cmul(ar, ai, lr, li)
        z_r, _ = _cmul(wl_r, -wl_i, glr, gli)
        dld_ref[...] = jnp.sum(z_r * dt, axis=1, keepdims=True)

    g = jax.ShapeDtypeStruct(a_re.shape, f32)
    b = jax.ShapeDtypeStruct(bt_re.shape, f32)
    return pl.pallas_call(body, out_shape=[g, g, jax.ShapeDtypeStruct((a_re.shape[0], 1), f32), b, b], name="ssm_params_grad",
                          compiler_params=_params())(a_re, a_im, logdt_b, bt_re, bt_im, spread, gather, dlam_re, dlam_im, dbt_re, dbt_im)


def _block_diag(t, rows, cols):
    G = t.shape[0]
    t = t.reshape(G // NJ, NJ, rows, cols)
    eye = jnp.eye(NJ, dtype=t.dtype)
    return jnp.einsum('jgrc,gh->jgrhc', t, eye).reshape(G // NJ, NJ * rows, NJ * cols)


def _state_layout(t):
    return t.reshape(NJ, NQ, LANES).transpose(1, 0, 2)


def _state_layout_inv(t, G, N):
    return t.transpose(1, 0, 2).reshape(G, N)


def _tiles2d(shape, budget_rows=128):
    rows, cols = shape
    tr = rows
    if rows > budget_rows:
        tr = budget_rows
        while rows % tr:
            tr -= SUBLANES
    return tr, cols


def _adam_update(w, g, m, v):
    c1 = 1.0 - ADAM_B1 ** ADAM_STEP
    c2 = 1.0 - ADAM_B2 ** ADAM_STEP
    nm = ADAM_B1 * m + (1.0 - ADAM_B1) * g
    nv = ADAM_B2 * v + (1.0 - ADAM_B2) * (g * g)
    delta = -ADAM_LR * ((nm / c1) / (jnp.sqrt(nv / c2) + ADAM_EPS) + ADAM_WD * w)
    return delta, nm, nv


def adamw_many(name, ws, gs, ms, vs):
    n = len(ws)

    def body(*refs):
        w, g, m, v = (refs[k * n:(k + 1) * n] for k in range(4))
        d, nm, nv = (refs[(4 + k) * n:(5 + k) * n] for k in range(3))
        for i in range(n):
            d[i][...], nm[i][...], nv[i][...] = _adam_update(w[i][...], g[i][...], m[i][...], v[i][...])

    o = [jax.ShapeDtypeStruct(a.shape, f32) for a in ws]
    outs = pl.pallas_call(body, out_shape=o * 3, name=name, compiler_params=_params())(*ws, *gs, *ms, *vs)
    return outs[:n], outs[n:2 * n], outs[2 * n:]


def adamw_halves(name, w, own, got, m, v, c_arr):
    h, cols = own.shape
    tr, _ = _tiles2d((h, cols), 128 if cols > 1024 else 256)
    nh = h // tr

    def body(c_ref, w_ref, own_ref, got_ref, m_ref, v_ref, g_ref, d_ref, nm_ref, nv_ref):
        mine = (pl.program_id(0) // nh) == c_ref[0]
        g = jnp.where(mine, own_ref[...], got_ref[...])
        g_ref[...] = g
        d_ref[...], nm_ref[...], nv_ref[...] = _adam_update(w_ref[...], g, m_ref[...], v_ref[...])

    spec = pl.BlockSpec((tr, cols), lambda i, c: (i, 0))
    own_spec = pl.BlockSpec((tr, cols), lambda i, c: (jnp.where(i // nh == c[0], i % nh, 0), 0))
    got_spec = pl.BlockSpec((tr, cols), lambda i, c: (jnp.where(i // nh == c[0], 0, i % nh), 0))
    o = jax.ShapeDtypeStruct(w.shape, f32)
    grid_spec = pltpu.PrefetchScalarGridSpec(num_scalar_prefetch=1, grid=(2 * nh,),
                                             in_specs=[spec, own_spec, got_spec, spec, spec], out_specs=[spec] * 4)
    return pl.pallas_call(body, out_shape=[o, o, o, o], grid_spec=grid_spec, name=name,
                          compiler_params=_params("arbitrary"))(c_arr, w, own, got, m, v)


def pair_sum(name, g, got, c_arr):
    S, h, cols = got.shape
    tr, _ = _tiles2d((h, cols), 1024)
    nh = h // tr

    def body(c_ref, g_ref, o_ref, out_ref):
        out_ref[...] = (g_ref[...].astype(f32) + o_ref[...].astype(f32)).astype(out_ref.dtype)

    spec = pl.BlockSpec((None, tr, cols), lambda s, i, c: (s, i, 0))
    grid_spec = pltpu.PrefetchScalarGridSpec(
        num_scalar_prefetch=1, grid=(S, nh),
        in_specs=[pl.BlockSpec((None, tr, cols), lambda s, i, c: (s, c[0] * nh + i, 0)), spec], out_specs=spec)
    return pl.pallas_call(body, out_shape=jax.ShapeDtypeStruct(got.shape, g.dtype), grid_spec=grid_spec, name=name,
                          compiler_params=_params("parallel", "parallel"))(c_arr, g, got)


def chip_sum(name, pair, landed, mine_arr, dep=None):
    n_in, h, cols = landed.shape
    tr, _ = _tiles2d((h, cols), 512)

    def body(s_ref, p_ref, l_ref, *rest):
        acc = p_ref[...].astype(f32)
        for k in range(n_in):
            acc = acc + l_ref[k].astype(f32)
        rest[-1][...] = acc

    in_specs, operands, _ = _with_dep(
        [pl.BlockSpec((None, tr, cols), lambda i, s: (s[0], i, 0)), pl.BlockSpec((n_in, tr, cols), lambda i, s: (0, i, 0))],
        [pair, landed], dep)
    grid_spec = pltpu.PrefetchScalarGridSpec(num_scalar_prefetch=1, grid=(h // tr,), in_specs=in_specs,
                                             out_specs=pl.BlockSpec((tr, cols), lambda i, s: (i, 0)))
    return pl.pallas_call(body, out_shape=jax.ShapeDtypeStruct((h, cols), f32), grid_spec=grid_spec, name=name,
                          compiler_params=_params("parallel"))(mine_arr, *operands)


def into_slot(name, w, slot_arr, n_slots, dtype, dep=None):
    tr, cols = _tiles2d(w.shape, 256)

    def body(s_ref, w_ref, *rest):
        rest[-1][...] = w_ref[...].astype(dtype)

    in_specs, operands, _ = _with_dep([pl.BlockSpec((tr, cols), lambda i, s: (i, 0))], [w], dep)
    grid_spec = pltpu.PrefetchScalarGridSpec(num_scalar_prefetch=1, grid=(w.shape[0] // tr,), in_specs=in_specs,
                                             out_specs=pl.BlockSpec((None, tr, cols), lambda i, s: (s[0], i, 0)))
    return pl.pallas_call(body, out_shape=jax.ShapeDtypeStruct((n_slots,) + w.shape, dtype), grid_spec=grid_spec, name=name,
                          compiler_params=_params("parallel"))(slot_arr, *operands)


def sum_slots(name, t):
    S, rows, cols = t.shape
    tr, _ = _tiles2d((rows, cols), 256)

    def body(t_ref, o_ref):
        acc = t_ref[0]
        for s in range(1, S):
            acc = acc + t_ref[s]
        o_ref[...] = acc

    return pl.pallas_call(body, out_shape=jax.ShapeDtypeStruct((rows, cols), f32), grid=(rows // tr,),
                          in_specs=[pl.BlockSpec((S, tr, cols), lambda i: (0, i, 0))], out_specs=pl.BlockSpec((tr, cols), lambda i: (i, 0)),
                          name=name, compiler_params=_params("parallel"))(t)


def _place():
    x, y, c = lax.axis_index("x"), lax.axis_index("y"), lax.axis_index("c")
    return x, y, c


def _other_chips(x, y):
    return [(1 - x, y, 2 * (1 - x) + y), (x, 1 - y, 2 * x + 1 - y), (1 - x, 1 - y, 2 * (1 - x) + 1 - y)]


SEM = pl.BlockSpec(memory_space=pltpu.SEMAPHORE)
VM = pl.BlockSpec(memory_space=pltpu.VMEM)
DATAFLOW = pltpu.SideEffectType.DATAFLOW_SIDE_EFFECTING
TOKEN = jax.ShapeDtypeStruct((SUBLANES, LANES), f32)


def _gather_copy(buf, w, k, chip, c, mine, send, recv):
    px, py, _ = chip
    h = buf.shape[1] // 2
    half = buf.at[mine, pl.ds(c * h, h), :]
    return pltpu.make_async_remote_copy(src_ref=half, dst_ref=half, send_sem=send.at[3 * w + k], recv_sem=recv.at[3 * w + k],
                                        device_id=(px, py, c), device_id_type=MESH)


def _gather_landing(buf, w, k, chip, c, send, recv):
    px, py, s = chip
    h = buf.shape[1] // 2
    landed = buf.at[s, pl.ds(c * h, h), :]
    return pltpu.make_async_remote_copy(src_ref=landed, dst_ref=landed, send_sem=send.at[3 * w + k], recv_sem=recv.at[3 * w + k],
                                        device_id=(px, py, c), device_id_type=MESH)


def gather_start(name, bufs, groups, after, neighbours_only=()):
    nw, ng = len(bufs), len(groups)

    def body(*refs):
        outs = refs[nw + 1:]
        sems, dst = outs[:2 * ng], outs[2 * ng:2 * ng + nw]
        token = outs[2 * ng + nw]
        x, y, c = _place()
        mine = 2 * x + y
        for g, members in enumerate(groups):
            for i, w in enumerate(members):
                for k, chip in enumerate(_other_chips(x, y)[:2 if w in neighbours_only else 3]):
                    _gather_copy(dst[w], i, k, chip, c, mine, sems[2 * g], sems[2 * g + 1]).start()
        token[...] = jnp.zeros_like(token)

    sem_shapes = []
    for members in groups:
        sem_shapes += [pltpu.SemaphoreType.DMA((3 * len(members),))] * 2
    outs = pl.pallas_call(
        body, out_shape=sem_shapes + [jax.ShapeDtypeStruct(b.shape, b.dtype) for b in bufs] + [TOKEN],
        in_specs=[ANY] * (nw + 1), out_specs=[SEM] * (2 * ng) + [ANY] * nw + [VM],
        input_output_aliases={w: 2 * ng + w for w in range(nw)}, name=name,
        compiler_params=pltpu.CompilerParams(has_side_effects=DATAFLOW))(*bufs, after)
    return [(outs[2 * g], outs[2 * g + 1]) for g in range(ng)], list(outs[2 * ng:2 * ng + nw]), outs[2 * ng + nw]


def gather_wait(name, bufs, send, recv, after):
    nw = len(bufs)

    def body(*refs):
        src = refs[:nw]
        send_ref, recv_ref = refs[nw], refs[nw + 1]
        x, y, c = _place()
        mine = 2 * x + y
        for w in range(nw):
            for k, chip in enumerate(_other_chips(x, y)):
                _gather_copy(src[w], w, k, chip, c, mine, send_ref, recv_ref).wait_send()
                _gather_landing(src[w], w, k, chip, c, send_ref, recv_ref).wait_recv()

    return pl.pallas_call(
        body, out_shape=[jax.ShapeDtypeStruct(b.shape, b.dtype) for b in bufs],
        in_specs=[ANY] * nw + [SEM, SEM, ANY], out_specs=[ANY] * nw,
        input_output_aliases={w: w for w in range(nw)}, name=name,
        compiler_params=pltpu.CompilerParams(has_side_effects=DATAFLOW))(*bufs, send, recv, after)


def _relay_copy(buf, w, j, x, y, c, send, recv, landing):
    chips = _other_chips(x, y)
    px, py, _ = chips[j]
    h = buf.shape[1] // 2
    q = h // 2
    s = chips[2][2] if landing else chips[1 - j][2]
    part = buf.at[s, pl.ds(c * h + j * q, q), :]
    return pltpu.make_async_remote_copy(src_ref=part, dst_ref=part, send_sem=send.at[2 * w + j], recv_sem=recv.at[2 * w + j],
                                        device_id=(px, py, c), device_id_type=MESH)


def _early_pass(buf, nw, w, k, x, y, c, send, recv, landing):
    s = _other_chips(x, y)[k][2]
    h = buf.shape[1] // 2
    part = buf.at[s, pl.ds(((1 - c) if landing else c) * h, h), :]
    i = 2 * nw + 2 * w + k
    return pltpu.make_async_remote_copy(src_ref=part, dst_ref=part, send_sem=send.at[i], recv_sem=recv.at[i],
                                        device_id=(x, y, 1 - c), device_id_type=MESH)


def gather_relay(name, bufs, sems, more, after):
    nw, nm = len(bufs), len(more)
    ns = 4 if nm else 2

    def body(*refs):
        ins, outs = refs[:nw + nm + 2 * nw + 1], refs[nw + nm + 2 * nw + 1:]
        src, d_sems = ins[:nw], ins[nw + nm:nw + nm + 2 * nw]
        r_send, r_recv = outs[:2]
        m_send, m_recv = outs[2:ns] if nm else (None, None)
        dst, mdst, token = outs[ns:ns + nw], outs[ns + nw:ns + nw + nm], outs[ns + nw + nm]
        x, y, c = _place()
        mine = 2 * x + y
        chips = _other_chips(x, y)
        for w in range(nw):
            for k in range(2):
                _gather_copy(src[w], 0, k, chips[k], c, mine, d_sems[2 * w], d_sems[2 * w + 1]).wait_send()
                _gather_landing(src[w], 0, k, chips[k], c, d_sems[2 * w], d_sems[2 * w + 1]).wait_recv()
            for j in range(2):
                _relay_copy(dst[w], w, j, x, y, c, r_send, r_recv, False).start()
            for k in range(2):
                _early_pass(dst[w], nw, w, k, x, y, c, r_send, r_recv, False).start()
        for w in range(nm):
            for k, chip in enumerate(chips):
                _gather_copy(mdst[w], w, k, chip, c, mine, m_send, m_recv).start()
        token[...] = jnp.zeros_like(token)

    sem_shapes = [pltpu.SemaphoreType.DMA((4 * nw,))] * 2 + [pltpu.SemaphoreType.DMA((3 * nm,))] * (ns - 2)
    flat_sems = [s for pair in sems for s in pair]
    outs = pl.pallas_call(
        body, out_shape=sem_shapes + [jax.ShapeDtypeStruct(b.shape, b.dtype) for b in list(bufs) + list(more)] + [TOKEN],
        in_specs=[ANY] * (nw + nm) + [SEM] * (2 * nw) + [ANY], out_specs=[SEM] * ns + [ANY] * (nw + nm) + [VM],
        input_output_aliases={i: ns + i for i in range(nw + nm)}, name=name,
        compiler_params=pltpu.CompilerParams(has_side_effects=DATAFLOW))(*bufs, *more, *flat_sems, after)
    m_sems = (outs[2], outs[3]) if nm else None
    return outs[0], outs[1], m_sems, list(outs[ns:ns + nw]), list(outs[ns + nw:ns + nw + nm]), outs[ns + nw + nm]


def gather_wait_relay(name, bufs, r_send, r_recv, after):
    nw = len(bufs)

    def body(*refs):
        src = refs[:nw]
        send_ref, recv_ref = refs[nw], refs[nw + 1]
        x, y, c = _place()
        for w in range(nw):
            for j in range(2):
                _relay_copy(src[w], w, j, x, y, c, send_ref, recv_ref, False).wait_send()
                _relay_copy(src[w], w, j, x, y, c, send_ref, recv_ref, True).wait_recv()
                _early_pass(src[w], nw, w, j, x, y, c, send_ref, recv_ref, False).wait_send()
                _early_pass(src[w], nw, w, j, x, y, c, send_ref, recv_ref, True).wait_recv()

    return pl.pallas_call(
        body, out_shape=[jax.ShapeDtypeStruct(b.shape, b.dtype) for b in bufs],
        in_specs=[ANY] * nw + [SEM, SEM, ANY], out_specs=[ANY] * nw,
        input_output_aliases={w: w for w in range(nw)}, name=name,
        compiler_params=pltpu.CompilerParams(has_side_effects=DATAFLOW))(*bufs, r_send, r_recv, after)


def gather_forward(name, bufs, which=(0, 1, 2)):
    nw = len(bufs)

    def body(*refs):
        dst = refs[nw:2 * nw]
        send, recv = refs[2 * nw:]
        x, y, c = _place()
        sib = (x, y, 1 - c)
        barrier = pltpu.get_barrier_semaphore()
        pl.semaphore_signal(barrier, inc=1, device_id=sib, device_id_type=MESH)
        pl.semaphore_wait(barrier, 1)
        cps = []
        for w in range(nw):
            h = dst[w].shape[1] // 2
            for k in which:
                s = _other_chips(x, y)[k][2]
                landed = dst[w].at[s, pl.ds(c * h, h), :]
                cp = pltpu.make_async_remote_copy(src_ref=landed, dst_ref=landed, send_sem=send.at[w, k], recv_sem=recv.at[w, k],
                                                  device_id=sib, device_id_type=MESH)
                cp.start()
                cps.append(cp)
        for w in range(nw):
            h = dst[w].shape[1] // 2
            for k in which:
                s = _other_chips(x, y)[k][2]
                other = dst[w].at[s, pl.ds((1 - c) * h, h), :]
                pltpu.make_async_remote_copy(src_ref=other, dst_ref=other, send_sem=send.at[w, k], recv_sem=recv.at[w, k],
                                             device_id=sib, device_id_type=MESH).wait_recv()
        for cp in cps:
            cp.wait_send()

    sem = pltpu.SemaphoreType.DMA((nw, 3))
    return pl.pallas_call(
        body, out_shape=[jax.ShapeDtypeStruct(b.shape, b.dtype) for b in bufs],
        in_specs=[ANY] * nw, out_specs=[ANY] * nw, input_output_aliases={w: w for w in range(nw)},
        scratch_shapes=[sem, sem], name=name,
        compiler_params=pltpu.CompilerParams(has_side_effects=True, collective_id=SIBLING_PAIR))(*bufs)


def _scatter_copy(src, dst, w, k, chip, c, send, recv):
    px, py, s = chip
    return pltpu.make_async_remote_copy(src_ref=src.at[s], dst_ref=dst.at[k], send_sem=send.at[3 * w + k], recv_sem=recv.at[3 * w + k],
                                        device_id=(px, py, c), device_id_type=MESH)


def scatter_start(name, parts):
    nw = len(parts)
    lands = [pltpu.with_memory_space_constraint(lax.empty((N_CHIPS - 1,) + p.shape[1:], p.dtype), pltpu.HBM) for p in parts]

    def body(*refs):
        outs = refs[2 * nw:]
        send, recv = outs[0], outs[1]
        src, dst, token = outs[2:2 + nw], outs[2 + nw:2 + 2 * nw], outs[2 + 2 * nw]
        x, y, c = _place()
        for w in range(nw):
            for k, chip in enumerate(_other_chips(x, y)):
                _scatter_copy(src[w], dst[w], w, k, chip, c, send, recv).start()
        token[...] = jnp.zeros_like(token)

    sem = pltpu.SemaphoreType.DMA((3 * nw,))
    outs = pl.pallas_call(
        body, out_shape=[sem, sem] + [jax.ShapeDtypeStruct(p.shape, p.dtype) for p in parts]
        + [jax.ShapeDtypeStruct(l.shape, l.dtype) for l in lands] + [TOKEN],
        in_specs=[ANY] * (2 * nw), out_specs=[SEM, SEM] + [ANY] * (2 * nw) + [VM],
        input_output_aliases={i: 2 + i for i in range(2 * nw)}, name=name,
        compiler_params=pltpu.CompilerParams(has_side_effects=DATAFLOW))(*parts, *lands)
    return outs[0], outs[1], list(outs[2:2 + nw]), list(outs[2 + nw:2 + 2 * nw]), outs[2 + 2 * nw]


def scatter_wait(name, parts, lands, send, recv, after):
    nw = len(parts)

    def body(*refs):
        src, dst = refs[:nw], refs[nw:2 * nw]
        send_ref, recv_ref = refs[2 * nw], refs[2 * nw + 1]
        x, y, c = _place()
        for w in range(nw):
            for k, chip in enumerate(_other_chips(x, y)):
                cp = _scatter_copy(src[w], dst[w], w, k, chip, c, send_ref, recv_ref)
                cp.wait_send()
                cp.wait_recv()

    outs = pl.pallas_call(
        body, out_shape=[jax.ShapeDtypeStruct(a.shape, a.dtype) for a in list(parts) + list(lands)],
        in_specs=[ANY] * (2 * nw) + [SEM, SEM, ANY], out_specs=[ANY] * (2 * nw),
        input_output_aliases={i: i for i in range(2 * nw)}, name=name,
        compiler_params=pltpu.CompilerParams(has_side_effects=DATAFLOW))(*parts, *lands, send, recv, after)
    return list(outs[:nw]), list(outs[nw:])


SIBLING_PAIR = 0


def _sibling_copy(src, dst, w, c, half_rows, send, recv, sib):
    if half_rows:
        h = src.shape[1] // 2
        src = src.at[:, pl.ds((1 - c) * h, h), :]
    return pltpu.make_async_remote_copy(src_ref=src, dst_ref=dst, send_sem=send.at[w], recv_sem=recv.at[w],
                                        device_id=sib, device_id_type=MESH)


def _landing(shape, dtype):
    return pltpu.with_memory_space_constraint(lax.empty(shape, dtype), pltpu.HBM)


def sibling_start(name, srcs, half_rows):
    nw = len(srcs)
    lands = [_landing((s.shape[0], s.shape[1] // 2, s.shape[2]) if half_rows else s.shape, s.dtype) for s in srcs]

    def body(*refs):
        outs = refs[2 * nw:]
        send, recv = outs[0], outs[1]
        src, dst, token = outs[2:2 + nw], outs[2 + nw:2 + 2 * nw], outs[2 + 2 * nw]
        x, y, c = _place()
        barrier = pltpu.get_barrier_semaphore()
        pl.semaphore_signal(barrier, inc=1, device_id=(x, y, 1 - c), device_id_type=MESH)
        pl.semaphore_wait(barrier, 1)
        for w in range(nw):
            _sibling_copy(src[w], dst[w], w, c, half_rows, send, recv, (x, y, 1 - c)).start()
        token[...] = jnp.zeros_like(token)

    sem = pltpu.SemaphoreType.DMA((nw,))
    outs = pl.pallas_call(
        body, out_shape=[sem, sem] + [jax.ShapeDtypeStruct(a.shape, a.dtype) for a in list(srcs) + lands] + [TOKEN],
        in_specs=[ANY] * (2 * nw), out_specs=[SEM, SEM] + [ANY] * (2 * nw) + [VM],
        input_output_aliases={i: 2 + i for i in range(2 * nw)}, name=name,
        compiler_params=pltpu.CompilerParams(has_side_effects=DATAFLOW, collective_id=SIBLING_PAIR))(*srcs, *lands)
    return outs[0], outs[1], list(outs[2:2 + nw]), list(outs[2 + nw:2 + 2 * nw]), outs[2 + 2 * nw]


def sibling_wait(name, srcs, lands, send, recv, half_rows, after):
    nw = len(srcs)

    def body(*refs):
        src, dst = refs[:nw], refs[nw:2 * nw]
        send_ref, recv_ref = refs[2 * nw], refs[2 * nw + 1]
        x, y, c = _place()
        for w in range(nw):
            cp = _sibling_copy(src[w], dst[w], w, c, half_rows, send_ref, recv_ref, (x, y, 1 - c))
            cp.wait_send()
            cp.wait_recv()

    outs = pl.pallas_call(
        body, out_shape=[jax.ShapeDtypeStruct(a.shape, a.dtype) for a in list(srcs) + list(lands)],
        in_specs=[ANY] * (2 * nw) + [SEM, SEM, ANY], out_specs=[ANY] * (2 * nw),
        input_output_aliases={i: i for i in range(2 * nw)}, name=name,
        compiler_params=pltpu.CompilerParams(has_side_effects=DATAFLOW))(*srcs, *lands, send, recv, after)
    return list(outs[:nw]), list(outs[nw:])


def _peer(x, y, c, r):
    return (1 - x if r & 4 else x, 1 - y if r & 2 else y, 1 - c if r & 1 else c)


def _everyone_copy(buf, r, x, y, c, send, recv, landing):
    px, py, pc = _peer(x, y, c, r)
    slot = buf.at[4 * px + 2 * py + pc] if landing else buf.at[4 * x + 2 * y + c]
    return pltpu.make_async_remote_copy(src_ref=slot, dst_ref=slot, send_sem=send.at[r - 1], recv_sem=recv.at[r - 1],
                                        device_id=(px, py, pc), device_id_type=MESH)


def everyone_start(name, buf):
    def body(buf_in, send, recv, buf_ref, token):
        x, y, c = _place()
        for r in range(1, N_DEV):
            _everyone_copy(buf_ref, r, x, y, c, send, recv, False).start()
        token[...] = jnp.zeros_like(token)

    sem = pltpu.SemaphoreType.DMA((N_DEV - 1,))
    return pl.pallas_call(
        body, out_shape=[sem, sem, jax.ShapeDtypeStruct(buf.shape, buf.dtype), TOKEN],
        in_specs=[ANY], out_specs=[SEM, SEM, ANY, VM], input_output_aliases={0: 2}, name=name,
        compiler_params=pltpu.CompilerParams(has_side_effects=DATAFLOW))(buf)


def everyone_wait(name, buf, send, recv, after):
    def body(buf_ref, send_ref, recv_ref, after_ref, out_ref):
        x, y, c = _place()
        for r in range(1, N_DEV):
            _everyone_copy(buf_ref, r, x, y, c, send_ref, recv_ref, False).wait_send()
            _everyone_copy(buf_ref, r, x, y, c, send_ref, recv_ref, True).wait_recv()

    return pl.pallas_call(
        body, out_shape=jax.ShapeDtypeStruct(buf.shape, buf.dtype), in_specs=[ANY, SEM, SEM, ANY], out_specs=ANY,
        input_output_aliases={0: 0}, name=name,
        compiler_params=pltpu.CompilerParams(has_side_effects=DATAFLOW))(buf, send, recv, after)


def local_step(x, pos, tgt, small, d_in, get_w, put_g, first_dep=None, tick=lambda name, after: None):
    L, D = x.shape
    d_kv = N_KV_HEADS * HEAD_DIM
    d_ssm = small["d_skip"].shape[1]
    d_attn = d_in - 2 * d_kv - d_ssm
    big = {}
    G = d_ssm // SSM_GROUP
    N, P = SSM_STATE, SSM_GROUP
    gbf = bf16

    half_dim = HEAD_DIM // 2
    inv_freq = ROPE_THETA ** (-jnp.arange(half_dim, dtype=f32) / half_dim)
    inv_freq = jnp.tile(inv_freq, LANES // half_dim).reshape(1, LANES)
    sink_b = jnp.broadcast_to(small["sinks"].reshape(-1, 1), (small["sinks"].size, LANES))

    spread = jnp.repeat(jnp.eye(G, dtype=f32), P, axis=0)
    logdt_b = jnp.broadcast_to(small["log_dt"].reshape(G, 1), (G, N))
    bt_re = small["b_re"].reshape(G, N, P).transpose(0, 2, 1).reshape(G * P, N)
    bt_im = small["b_im"].reshape(G, N, P).transpose(0, 2, 1).reshape(G * P, N)
    a_re, a_im = small["a_re"].reshape(G, N), small["a_im"].reshape(G, N)
    lam_re, lam_im, bbt_re, bbt_im = ssm_params(a_re, a_im, logdt_b, bt_re, bt_im, spread)
    bd_re = _block_diag(bbt_re.reshape(G, P, N), P, N).astype(bf16)
    bd_im = _block_diag(bbt_im.reshape(G, P, N), P, N).astype(bf16)
    c_re = small["c_re"].reshape(G, P, N).transpose(0, 2, 1)
    c_im = small["c_im"].reshape(G, P, N).transpose(0, 2, 1)
    cd_re = _block_diag(c_re, N, P).astype(bf16)
    cd_im = _block_diag(c_im, N, P).astype(bf16)
    lam_re_l, lam_im_l = _state_layout(lam_re), _state_layout(lam_im)

    def k1(i, nt, xt, g):
        return (rms_fwd(xt, g),)
    xn = rowwise("pre_mix_norm", k1, L, [full(x)], [small["g_pre_mix"]], [(D, bf16)], dep=first_dep)[0]
    big["w_in"] = get_w("w_in", xn)
    proj = mm_nn("proj_in", xn, big["w_in"])
    qr, kk, vv, u_bf = qkv_prep(proj, pos, inv_freq, d_attn, d_kv)
    attn = attn_fwd(qr, kk, vv, sink_b)
    u_cb = (d_attn + 2 * d_kv) // (d_ssm // 2)
    token = tick("attn", attn)
    y, z_bf, s_re, s_im = ssm_fwd(u_bf, proj, u_cb, bd_re, bd_im, cd_re, cd_im, lam_re_l, lam_im_l, small["d_skip"], dep=token)
    token = tick("ssm", z_bf)
    big["w_glu"] = get_w("w_glu", z_bf)
    gl = mm_nn("glu_proj", z_bf, big["w_glu"], dep=token)

    def k6(i, nt, at, yt, glt, bg, ga, gs):
        ssm = gelu(yt) * sigmoid(glt + bg)
        return (jnp.concatenate([rms_fwd(at, ga), rms_fwd(ssm, gs)], axis=1),)
    mixed = rowwise("mix_norms", k6, L, [full(attn), full(y), full(gl)],
                    [small["b_glu"], small["g_attn_out"], small["g_ssm_out"]], [(d_attn + d_ssm, bf16)])[0]
    big["w_o"] = get_w("w_o", mixed)
    mix = mm_nn("proj_out", mixed, big["w_o"])

    def k7(i, nt, xt, mt, gpm, gpf):
        h = xt + rms_fwd(mt, gpm)
        return h, rms_fwd(h, gpf)
    h, hn = rowwise("post_mix", k7, L, [full(x), full(mix)], [small["g_post_mix"], small["g_pre_ffn"]], [(D, f32), (D, bf16)])
    big["w_gate"] = get_w("w_gate", hn)
    big["w_up"] = get_w("w_up", hn)
    gt, up, hid = ffn_hidden(hn, big["w_gate"], big["w_up"])
    d_ff_dim = gt.shape[1]
    big["w_down"] = get_w("w_down", hid)
    ff = mm_nn("ffn_down", hid, big["w_down"], tk=d_ff_dim // 2)

    def k9(i, nt, ht, fft, tt, g):
        out = ht + rms_fwd(fft, g)
        err = out - tt
        per_row = jnp.mean(err * err, axis=-1, keepdims=True)
        loss = 0.5 * jnp.sum(per_row) * jnp.where(_lane((1, LANES)) == 0, 1.0, 0.0)
        d_out = err * (1.0 / D)
        d_ff, dg = rms_bwd(fft, g, d_out)
        return d_out, d_ff, dg, loss
    d_out, d_ff, dg_post_ffn, loss = rowwise("loss_head", k9, L, [full(h), full(ff), full(tgt)], [small["g_post_ffn"]],
                                             [(D, f32), (D, bf16)], reds=[D, LANES])

    d_gt, d_up = ffn_hidden_grad(d_ff, big["w_down"], gt, up)
    token = put_g("w_down", mm_tn("dw_down", hid, d_ff, out_dtype=gbf, tm=d_ff_dim // N_CHIPS))
    d_hn = mm_nt_pair("d_hn", d_gt, big["w_gate"], d_up, big["w_up"], dep=token)
    token = put_g("w_gate", mm_tn("dw_gate", hn, d_gt, shards=N_CHIPS, out_dtype=gbf))
    token = put_g("w_up", mm_tn("dw_up", hn, d_up, shards=N_CHIPS, out_dtype=gbf, dep=token))

    def k11(i, nt, ht, da, do, mt, gpf, gpm):
        dh_n, dg_pf = rms_bwd(ht, gpf, da)
        dh = do + dh_n
        d_mix, dg_pm = rms_bwd(mt, gpm, dh)
        return dh, d_mix, dg_pf, dg_pm
    dh, d_mix, dg_pre_ffn, dg_post_mix = rowwise("post_mix_grad", k11, L, [full(h), full(d_hn), full(d_out), full(mix)],
                                                 [small["g_pre_ffn"], small["g_post_mix"]], [(D, f32), (D, bf16)], reds=[D, D], dep=token)
    d_mixed = mm_nt("d_mixed", d_mix, big["w_o"])
    token = put_g("w_o", mm_tn("dw_o", mixed, d_mix, out_dtype=gbf))

    def k12(i, nt, at, yt, glt, da_n, ds_n, bg, ga, gs):
        z = gelu(yt)
        sg = sigmoid(glt + bg)
        ssm = z * sg
        d_at, dga = rms_bwd(at, ga, da_n)
        d_ssm_t, dgs = rms_bwd(ssm, gs, ds_n)
        d_gl = d_ssm_t * z * sg * (1.0 - sg)
        return d_at, d_ssm_t * sg, d_gl, dga, dgs, colsum(d_gl)
    d_attn_o, dz1, d_gl, dg_attn, dg_ssm, db_glu = rowwise(
        "mix_norms_grad", k12, L, [full(attn), full(y), full(gl), (d_mixed, d_attn, 0, 0), (d_mixed, d_ssm, d_attn // d_ssm, 0)],
        [small["b_glu"], small["g_attn_out"], small["g_ssm_out"]], [(d_attn, f32), (d_ssm, f32), (d_ssm, bf16)],
        reds=[d_attn, d_ssm, d_ssm], dep=token)
    dz2 = mm_nt("d_glu_in", d_gl, big["w_glu"])
    token = put_g("w_glu", mm_tn("dw_glu", z_bf, d_gl, out_dtype=gbf))

    du, dbd_re, dbd_im, dcd_re, dcd_im, dlam_re_l, dlam_im_l, dd_skip = ssm_bwd(
        y, dz1, dz2, u_bf, proj, u_cb, s_re, s_im, bd_re, bd_im, cd_re, cd_im, lam_re_l, lam_im_l, small["d_skip"], dep=token)
    dq, dkk_c, dkk_p, dvv_c, dvv_p, dsink = attn_bwd(qr, kk, vv, sink_b, attn, d_attn_o)
    d_proj = qkv_grad(dq, dkk_c, dkk_p, dvv_c, dvv_p, du, pos, inv_freq)
    d_xn = mm_nt("d_xn", d_proj, big["w_in"])
    token = put_g("w_in", mm_tn("dw_in", xn, d_proj, shards=N_CHIPS, out_dtype=gbf))

    def k17(i, nt, xt, dxn, dht, g):
        dx, dg = rms_bwd(xt, g, dxn)
        return dht + dx, dg
    grad_x, dg_pre_mix = rowwise("pre_mix_grad", k17, L, [full(x), full(d_xn), full(dh)], [small["g_pre_mix"]],
                                 [(D, f32)], reds=[D], dep=token)

    gather = spread.T
    dbbt_re = dbd_re.reshape(G * P, LANES)[:, :N]
    dbbt_im = dbd_im.reshape(G * P, LANES)[:, :N]
    dc_re = dcd_re.reshape(G, N, LANES)[:, :, :P].transpose(0, 2, 1)
    dc_im = dcd_im.reshape(G, N, LANES)[:, :, :P].transpose(0, 2, 1)
    d_a_re, d_a_im, d_logdt, dbt_re, dbt_im = ssm_params_grad(
        a_re, a_im, logdt_b, bt_re, bt_im, spread, gather,
        _state_layout_inv(dlam_re_l, G, N), _state_layout_inv(dlam_im_l, G, N), dbbt_re, dbbt_im)
    q_per_kv = d_attn // HEAD_DIM // N_KV_HEADS
    small_grads = {
        "g_pre_mix": dg_pre_mix, "sinks": dsink[:, :q_per_kv, 0].reshape(1, -1),
        "a_re": d_a_re, "a_im": d_a_im, "log_dt": d_logdt.reshape(1, G),
        "b_re": dbt_re, "b_im": dbt_im,
        "c_re": dc_re, "c_im": dc_im,
        "d_skip": dd_skip, "b_glu": db_glu, "g_attn_out": dg_attn, "g_ssm_out": dg_ssm,
        "g_post_mix": dg_post_mix, "g_pre_ffn": dg_pre_ffn, "g_post_ffn": dg_post_ffn,
    }
    return loss, grad_x, small_grads


WEIGHTS = ['g_pre_mix', 'w_in', 'sinks', 'a_re', 'a_im', 'log_dt', 'b_re', 'b_im', 'c_re', 'c_im', 'd_skip', 'w_glu', 'b_glu',
           'g_attn_out', 'g_ssm_out', 'w_o', 'g_post_mix', 'g_pre_ffn', 'w_gate', 'w_up', 'w_down', 'g_post_ffn']
BIG = ['w_in', 'w_glu', 'w_o', 'w_gate', 'w_up', 'w_down']
COL_SHARDED = ['w_in', 'w_gate', 'w_up']
SMALL = [n for n in WEIGHTS if n not in BIG]
GATHER_GROUPS = [["w_in"], ["w_glu", "w_o"], ["w_gate", "w_up"], ["w_down"]]
REDUCE_GROUPS = [["w_down", "w_gate", "w_up"], ["w_o", "w_glu", "w_in"]]


PACK_ROWS = 256


def _pack(parts):
    flat = jnp.concatenate([p.reshape(-1) for p in parts])
    pad = (-flat.size) % (PACK_ROWS * LANES)
    return jnp.pad(flat, (0, pad)).reshape(-1, LANES)


TRANSPOSED_VIEW = ("b_re", "b_im")


def small_view(name, a):
    if name in TRANSPOSED_VIEW:
        a = a.transpose(0, 1, 3, 2)
    return a.reshape(-1, a.shape[-1])


def small_unview(name, p, shape):
    if name in TRANSPOSED_VIEW:
        return p.reshape(shape[0], shape[1], shape[3], shape[2]).transpose(0, 1, 3, 2)
    return p.reshape(shape)


def _unpack(packed, shapes):
    flat = packed.reshape(-1)
    out, off = [], 0
    for s in shapes:
        n = int(np.prod(s))
        out.append(flat[off:off + n].reshape(s))
        off += n
    return out


def kernel(x, positions, g_pre_mix, w_in, sinks, a_re, a_im, log_dt, b_re, b_im, c_re, c_im, d_skip, w_glu, b_glu, g_attn_out, g_ssm_out, w_o, g_post_mix, g_pre_ffn, w_gate, w_up, w_down, g_post_ffn, loss_target, m_g_pre_mix, m_w_in, m_sinks, m_a_re, m_a_im, m_log_dt, m_b_re, m_b_im, m_c_re, m_c_im, m_d_skip, m_w_glu, m_b_glu, m_g_attn_out, m_g_ssm_out, m_w_o, m_g_post_mix, m_g_pre_ffn, m_w_gate, m_w_up, m_w_down, m_g_post_ffn, v_g_pre_mix, v_w_in, v_sinks, v_a_re, v_a_im, v_log_dt, v_b_re, v_b_im, v_c_re, v_c_im, v_d_skip, v_w_glu, v_b_glu, v_g_attn_out, v_g_ssm_out, v_w_o, v_g_post_mix, v_g_pre_ffn, v_w_gate, v_w_up, v_w_down, v_g_post_ffn):
    args = dict(locals())
    w = {n: args[n] for n in WEIGHTS}
    m = {n: args["m_" + n] for n in WEIGHTS}
    v = {n: args["v_" + n] for n in WEIGHTS}
    L, D = x.shape[1], x.shape[2]

    ax, ay, ac = _place()
    mine_arr = (2 * ax + ay).astype(jnp.int32).reshape(1)
    c_arr = ac.astype(jnp.int32).reshape(1)

    me_arr = (4 * ax + 2 * ay + ac).astype(jnp.int32).reshape(1)

    bufs = {"w_in": into_slot("cast_w_in", w["w_in"][0], mine_arr, N_CHIPS, bf16)}
    (first_sems,), (bufs["w_in"],), token = gather_start("gather_start_in", [bufs["w_in"]], [[0]], mine_arr)
    sems = {"w_in": first_sems}
    for n in BIG[1:]:
        bufs[n] = into_slot("cast_" + n, w[n][0], mine_arr, N_CHIPS, bf16, dep=token)
    first = ["w_glu", "w_o", "w_gate", "w_up"]
    (sems["w_glu"], sems["w_gate"], sems["w_up"]), started, token = gather_start(
        "gather_start_rest", [bufs[n] for n in first], [[0, 1], [2], [3]], token, neighbours_only=(2, 3))
    bufs.update(zip(first, started))
    relays, ready = {}, set()

    def tick(name, after):
        n, more = ("w_gate", []) if name == "attn" else ("w_up", ["w_down"])
        r_send, r_recv, sems["w_down"], (bufs[n],), down, tok = gather_relay(
            "gather_relay_" + n, [bufs[n]], [sems[n]], [bufs[k] for k in more], after)
        bufs.update(zip(more, down))
        relays[n] = (r_send, r_recv)
        return tok

    def get_w(n, after):
        if n not in ready:
            members = [g for g in GATHER_GROUPS if n in g][0]
            if members[0] in relays:
                landed = [gather_wait_relay("gather_wait_" + k, [bufs[k]], *relays[k], after)[0] for k in members]
                which = (2,)
            else:
                landed = gather_wait("gather_wait_" + members[0], [bufs[k] for k in members], *sems[members[0]], after)
                which = (0, 1, 2)
            bufs.update(zip(members, gather_forward("gather_forward_" + members[0], landed, which)))
            ready.update(members)
        g = bufs[n]
        return g if n in COL_SHARDED else g.reshape(g.shape[0] * g.shape[1], g.shape[2])

    swaps, inflight = {}, []

    def put_g(n, g):
        g3 = g if n in COL_SHARDED else g.reshape(N_CHIPS, g.shape[0] // N_CHIPS, g.shape[1])
        swaps[n] = sibling_start("swap_start_" + n, [g3], True)
        for gi, members in enumerate(REDUCE_GROUPS):
            if n == members[-1]:
                last = swaps[n][4]
                pair = []
                for k in members:
                    send, recv, srcs, lands, _ = swaps[k]
                    (src,), (got,) = sibling_wait("swap_wait_" + k, srcs, lands, send, recv, True, last)
                    pair.append(pair_sum("pair_sum_" + k, src, got, c_arr))
                send, recv, parts, lands, tok = scatter_start("scatter_start_%d" % gi, pair)
                inflight.append((members, send, recv, parts, lands))
                return tok
        return swaps[n][4]

    small = {n: w[n].reshape(1, -1) for n in SMALL}
    pos = positions.reshape(L, 1).astype(f32)
    d_in = N_CHIPS * w["w_in"].shape[2]
    loss, grad_x, small_grads = local_step(x[0], pos, loss_target[0], small, d_in, get_w, put_g, first_dep=token, tick=tick)

    shapes = [w[n].shape for n in SMALL]
    blocks = into_slot("small_block", _pack([small_grads[n] for n in SMALL] + [loss]), me_arr, N_DEV, f32)
    small_send, small_recv, blocks, after = everyone_start("small_start", blocks)

    grads, delta, new_m, new_v = {}, {}, {}, {}
    for gi, (members, send, recv, parts, lands) in enumerate(inflight):
        parts, landed = scatter_wait("scatter_wait_%d" % gi, parts, lands, send, recv, after)
        joins, dep = [], None
        for k, p, t in zip(members, parts, landed):
            joins.append(sibling_start("join_start_" + k, [chip_sum("chip_sum_" + k, p, t, mine_arr, dep=dep)], False))
            dep = after = joins[-1][4]
        for n, (send, recv, srcs, lands, _) in zip(members, joins):
            (own,), (sib,) = sibling_wait("join_wait_" + n, srcs, lands, send, recv, False, after)
            g_, d_, m_, v_ = adamw_halves("adamw_" + n, w[n][0], own, sib, m[n][0], v[n][0], c_arr)
            grads[n], delta[n], new_m[n], new_v[n] = g_[None], d_[None], m_[None], v_[None]
            after = v_
    blocks = everyone_wait("small_wait", blocks, small_send, small_recv, after)
    small_sum = sum_slots("small_sum", blocks)
    *small_g, loss = _unpack(small_sum, [small_view(n, w[n]).shape for n in SMALL] + [loss.shape])
    loss = loss[0, 0]
    outs = adamw_many("adamw_small", [small_view(n, w[n]) for n in SMALL], small_g,
                      [small_view(n, m[n]) for n in SMALL], [small_view(n, v[n]) for n in SMALL])
    for t, parts in zip((grads, delta, new_m, new_v), (small_g,) + tuple(outs)):
        t.update({n: small_unview(n, p, w[n].shape) for n, p in zip(SMALL, parts)})

    return (loss, grad_x[None], *[grads[n] for n in WEIGHTS], *[delta[n] for n in WEIGHTS],
            *[new_m[n] for n in WEIGHTS], *[new_v[n] for n in WEIGHTS])
```

```python
import functools
import math

import jax
import jax.numpy as jnp
import numpy as np
from jax import lax
from jax.experimental import pallas as pl
from jax.experimental.pallas import tpu as pltpu

f32 = jnp.float32
bf16 = jnp.bfloat16
HIGHEST = lax.Precision.HIGHEST
MESH = pl.DeviceIdType.MESH

HEAD_DIM = 64
N_KV_HEADS = 4
ATTN_BLOCK = 128
ROPE_THETA = 10000.0
SSM_GROUP = 16
SSM_STATE = 64
RMS_EPS = 1e-6
LANES = 128
SUBLANES = 8
VMEM_LIMIT = 52 * 1024 * 1024
N_CHIPS = 4
N_DEV = 8
NEG = -1e30

ADAM_LR, ADAM_B1, ADAM_B2, ADAM_EPS, ADAM_WD, ADAM_STEP = 0.001, 0.9, 0.999, 1e-08, 0.01, 10

NN = (((1,), (0,)), ((), ()))
NT = (((1,), (1,)), ((), ()))
TN = (((0,), (0,)), ((), ()))


def _params(*sem):
    return pltpu.CompilerParams(dimension_semantics=sem or None, vmem_limit_bytes=VMEM_LIMIT)


def _dot(a, b, dims=NN):
    return lax.dot_general(a, b, dims, preferred_element_type=f32)


def _pick(dim, pref):
    t = min(dim, pref)
    while dim % t:
        t -= LANES
    assert t > 0, (dim, pref)
    return t


ANY = pl.BlockSpec(memory_space=pl.ANY)


def _with_dep(in_specs, operands, dep):
    if dep is None:
        return list(in_specs), list(operands), 0
    return list(in_specs) + [ANY], list(operands) + [dep], 1


def _mm_call(name, grid, in_specs, out_spec, out_shape, acc_shape, dims, operands, dep=None):
    nk = grid[2]
    in_specs, operands, n_dep = _with_dep(in_specs, operands, dep)

    def body_one(a_ref, b_ref, *rest):
        o_ref = rest[n_dep]
        o_ref[...] = _dot(a_ref[...], b_ref[...], dims).astype(o_ref.dtype)

    def body(a_ref, b_ref, *rest):
        o_ref, acc_ref = rest[n_dep], rest[n_dep + 1]
        k = pl.program_id(2)

        @pl.when(k == 0)
        def _():
            acc_ref[...] = _dot(a_ref[...], b_ref[...], dims)

        @pl.when((k > 0) & (k < nk - 1))
        def _():
            acc_ref[...] += _dot(a_ref[...], b_ref[...], dims)

        @pl.when(k == nk - 1)
        def _():
            o_ref[...] = (acc_ref[...] + _dot(a_ref[...], b_ref[...], dims)).astype(o_ref.dtype)

    return pl.pallas_call(
        body_one if nk == 1 else body, out_shape=out_shape, grid=grid, in_specs=in_specs, out_specs=out_spec,
        scratch_shapes=[] if nk == 1 else [pltpu.VMEM(acc_shape, f32)], name=name,
        compiler_params=_params("parallel", "parallel", "arbitrary"))(*operands)


def mm_nt_pair(name, a1, b1, a2, b2, tm=1024, tk=1024, dep=None):
    M = a1.shape[0]
    S, K, n = b1.shape
    tm, tko = _pick(M, tm), _pick(K, tk)
    nk = 2 * S

    def body(a1_ref, b1_ref, a2_ref, b2_ref, *rest):
        o_ref, acc_ref = rest[-2], rest[-1]
        k = pl.program_id(2)

        @pl.when(k == 0)
        def _():
            acc_ref[...] = _dot(a1_ref[...], b1_ref[...], NT)

        @pl.when((k > 0) & (k < S))
        def _():
            acc_ref[...] += _dot(a1_ref[...], b1_ref[...], NT)

        @pl.when((k >= S) & (k < nk - 1))
        def _():
            acc_ref[...] += _dot(a2_ref[...], b2_ref[...], NT)

        @pl.when(k == nk - 1)
        def _():
            o_ref[...] = acc_ref[...] + _dot(a2_ref[...], b2_ref[...], NT)

    first = lambda k: jnp.minimum(k, S - 1)
    second = lambda k: jnp.maximum(k - S, 0)
    in_specs = [pl.BlockSpec((tm, n), lambda i, j, k: (i, first(k))), pl.BlockSpec((None, tko, n), lambda i, j, k: (first(k), j, 0)),
                pl.BlockSpec((tm, n), lambda i, j, k: (i, second(k))), pl.BlockSpec((None, tko, n), lambda i, j, k: (second(k), j, 0))]
    in_specs, operands, _ = _with_dep(in_specs, (a1, b1, a2, b2), dep)
    return pl.pallas_call(
        body, out_shape=jax.ShapeDtypeStruct((M, K), f32), grid=(M // tm, K // tko, nk), in_specs=in_specs,
        out_specs=pl.BlockSpec((tm, tko), lambda i, j, k: (i, j)), scratch_shapes=[pltpu.VMEM((tm, tko), f32)], name=name,
        compiler_params=_params("parallel", "parallel", "arbitrary"))(*operands)


def mm_nn(name, a, b, out_dtype=f32, tm=1024, tn=1024, tk=2048, dep=None):
    M, K = a.shape
    tm, tk = _pick(M, tm), _pick(K, tk)
    if b.ndim == 3:
        S, _, n = b.shape
        tn = _pick(n, 2048)
        per = n // tn
        b_spec = pl.BlockSpec((None, tk, tn), lambda i, j, k: (j // per, k, j % per))
        N = S * n
    else:
        N = b.shape[1]
        tn = _pick(N, tn)
        b_spec = pl.BlockSpec((tk, tn), lambda i, j, k: (k, j))
    grid = (M // tm, N // tn, K // tk)
    return _mm_call(name, grid, [pl.BlockSpec((tm, tk), lambda i, j, k: (i, k)), b_spec],
                    pl.BlockSpec((tm, tn), lambda i, j, k: (i, j)), jax.ShapeDtypeStruct((M, N), out_dtype),
                    (tm, tn), NN, (a, b), dep)


def mm_nt(name, a, b, out_dtype=f32, tm=1024, tn=2048, tk=1024, dep=None):
    M, N = a.shape
    tm = _pick(M, tm)
    if b.ndim == 3:
        S, K, n = b.shape
        tr = _pick(n, 2048)
        per = n // tr
        tko = _pick(K, tk)
        b_spec = pl.BlockSpec((None, tko, tr), lambda i, j, k: (k // per, j, k % per))
    else:
        K = b.shape[0]
        tr = _pick(N, tn)
        tko = _pick(K, tk)
        b_spec = pl.BlockSpec((tko, tr), lambda i, j, k: (j, k))
    grid = (M // tm, K // tko, N // tr)
    return _mm_call(name, grid, [pl.BlockSpec((tm, tr), lambda i, j, k: (i, k)), b_spec],
                    pl.BlockSpec((tm, tko), lambda i, j, k: (i, j)), jax.ShapeDtypeStruct((M, K), out_dtype),
                    (tm, tko), NT, (a, b), dep)


def mm_tn(name, a, b, shards=None, out_dtype=f32, tm=1024, tn=1024, tl=2048, dep=None):
    L, K = a.shape
    N = b.shape[1]
    tl, tko = _pick(L, tl), _pick(K, tm)
    if shards:
        n = N // shards
        tn = _pick(n, 2048)
        per = n // tn
        o_spec = pl.BlockSpec((None, tko, tn), lambda i, j, k: (j // per, i, j % per))
        o_shape = jax.ShapeDtypeStruct((shards, K, n), out_dtype)
    else:
        tn = _pick(N, tn)
        o_spec = pl.BlockSpec((tko, tn), lambda i, j, k: (i, j))
        o_shape = jax.ShapeDtypeStruct((K, N), out_dtype)
    grid = (K // tko, N // tn, L // tl)
    return _mm_call(name, grid, [pl.BlockSpec((tl, tko), lambda i, j, k: (k, i)),
                                 pl.BlockSpec((tl, tn), lambda i, j, k: (k, j))],
                    o_spec, o_shape, (tko, tn), TN, (a, b), dep)


def ffn_hidden(hn, w_gate, w_up, tm=512):
    M, K = hn.shape
    S, _, n = w_gate.shape
    tm = _pick(M, tm)

    def body(a_ref, g_ref, u_ref, gt_ref, up_ref, hid_ref):
        a = a_ref[...]
        g = _dot(a, g_ref[...])
        u = _dot(a, u_ref[...])
        gt_ref[...] = g.astype(bf16)
        up_ref[...] = u.astype(bf16)
        hid_ref[...] = (g * sigmoid(g) * u).astype(bf16)

    w_spec = pl.BlockSpec((None, K, n), lambda s, i: (s, 0, 0))
    o_spec = pl.BlockSpec((tm, n), lambda s, i: (i, s))
    o = jax.ShapeDtypeStruct((M, S * n), bf16)
    return pl.pallas_call(
        body, out_shape=[o, o, o], grid=(S, M // tm), in_specs=[pl.BlockSpec((tm, K), lambda s, i: (i, 0)), w_spec, w_spec],
        out_specs=[o_spec, o_spec, o_spec], name="ffn_hidden", compiler_params=_params("parallel", "parallel"))(hn, w_gate, w_up)


def ffn_hidden_grad(d_ff, w_down, gt, up, tm=512):
    M, D = d_ff.shape
    F = w_down.shape[0]
    n = _pick(F // N_CHIPS, 2048)
    tm = _pick(M, tm)

    def body(a_ref, b_ref, gt_ref, up_ref, dg_ref, du_ref):
        dh = _dot(a_ref[...], b_ref[...], NT)
        g = gt_ref[...].astype(f32)
        sg = sigmoid(g)
        dg_ref[...] = (dh * up_ref[...].astype(f32) * (sg * (1.0 + g * (1.0 - sg)))).astype(bf16)
        du_ref[...] = (dh * (g * sg)).astype(bf16)

    t_spec = pl.BlockSpec((tm, n), lambda j, i: (i, j))
    o = jax.ShapeDtypeStruct((M, F), bf16)
    return pl.pallas_call(
        body, out_shape=[o, o], grid=(F // n, M // tm),
        in_specs=[pl.BlockSpec((tm, D), lambda j, i: (i, 0)), pl.BlockSpec((n, D), lambda j, i: (j, 0)), t_spec, t_spec],
        out_specs=[t_spec, t_spec], name="ffn_hidden_grad", compiler_params=_params("parallel", "parallel"))(d_ff, w_down, gt, up)


def rowwise(name, fn, L, rows, bcast, outs, reds=(), tr=256, dep=None):
    tr = min(tr, L)
    nt = L // tr
    n_rows, n_b, n_o = len(rows), len(bcast), len(outs)
    n_dep = 0 if dep is None else 1

    def body(*refs):
        i = pl.program_id(0)
        ins = [r[...] for r in refs[:n_rows + n_b]]
        res = fn(i, nt, *ins)
        o_refs = refs[n_rows + n_b + n_dep:]
        for k in range(n_o):
            o_refs[k][...] = res[k].astype(o_refs[k].dtype)
        if reds:
            @pl.when(i == 0)
            def _():
                for k in range(len(reds)):
                    o_refs[n_o + k][...] = jnp.zeros_like(o_refs[n_o + k])
            for k in range(len(reds)):
                o_refs[n_o + k][...] += res[n_o + k]

    def row_spec(width, cb, shift):
        if shift:
            return pl.BlockSpec((tr, width), lambda i: (jnp.minimum(i + shift, nt - 1), cb))
        return pl.BlockSpec((tr, width), lambda i: (i, cb))

    in_specs = [row_spec(w, cb, sh) for (_, w, cb, sh) in rows]
    in_specs += [pl.BlockSpec(b.shape, lambda i: (0, 0)) for b in bcast]
    out_specs = [pl.BlockSpec((tr, w), lambda i: (i, 0)) for (w, _) in outs]
    out_specs += [pl.BlockSpec((1, w), lambda i: (0, 0)) for w in reds]
    out_shape = [jax.ShapeDtypeStruct((L, w), dt) for (w, dt) in outs]
    out_shape += [jax.ShapeDtypeStruct((1, w), f32) for w in reds]
    in_specs, operands, _ = _with_dep(in_specs, [r[0] for r in rows] + list(bcast), dep)
    return pl.pallas_call(
        body, out_shape=out_shape, grid=(nt,), in_specs=in_specs, out_specs=out_specs, name=name,
        compiler_params=_params("arbitrary"))(*operands)


def full(a):
    return (a, a.shape[1], 0, 0)


def colsum(v):
    return jnp.sum(v, axis=0, keepdims=True)


def rms_fwd(x, g):
    r = lax.rsqrt(jnp.mean(x * x, axis=-1, keepdims=True) + RMS_EPS)
    return x * r * g


def rms_bwd(x, g, dy):
    r = lax.rsqrt(jnp.mean(x * x, axis=-1, keepdims=True) + RMS_EPS)
    xh = x * r
    dyg = dy * g
    dx = r * (dyg - xh * jnp.mean(dyg * xh, axis=-1, keepdims=True))
    return dx, colsum(dy * xh)


GELU_C = math.sqrt(2.0 / math.pi)


def gelu(y):
    return y * (0.5 * (1.0 + jnp.tanh(GELU_C * (y + 0.044715 * (y * y * y)))))


def gelu_grad(y):
    t = jnp.tanh(GELU_C * (y + 0.044715 * (y * y * y)))
    return 0.5 * (1.0 + t) + 0.5 * y * (1.0 - t * t) * (GELU_C * (1.0 + 3 * 0.044715 * (y * y)))


def sigmoid(v):
    return 1.0 / (1.0 + jnp.exp(-v))


def _lane(shape):
    return lax.broadcasted_iota(jnp.int32, shape, 1)


def _rot_chunk(t, cos, sin_signed):
    first = (_lane(t.shape) % HEAD_DIM) < (HEAD_DIM // 2)
    partner = jnp.where(first, pltpu.roll(t, LANES - HEAD_DIM // 2, 1), pltpu.roll(t, HEAD_DIM // 2, 1))
    return t * cos + partner * sin_signed


def _cos_sin(pos, inv_freq, inverse):
    ang = pos * inv_freq
    cos, sin = jnp.cos(ang), jnp.sin(ang)
    first = (_lane(ang.shape) % HEAD_DIM) < (HEAD_DIM // 2)
    sign = jnp.where(first, -1.0, 1.0) * (-1.0 if inverse else 1.0)
    return cos, sin * sign


def _dup_head(chunk, odd):
    low = _lane(chunk.shape) < HEAD_DIM
    x = jnp.where(low != odd, chunk, 0.0)
    return x + pltpu.roll(x, HEAD_DIM, 1)


def _chunks(v):
    return [v[:, LANES * c:LANES * (c + 1)] for c in range(v.shape[1] // LANES)]


def qkv_prep(proj, pos, inv_freq, d_attn, d_kv):
    L = proj.shape[0]
    d_ssm = proj.shape[1] - d_attn - 2 * d_kv
    half = d_ssm // 2
    scale = 1.0 / math.sqrt(HEAD_DIM)

    def fn(i, nt, q, k, v, u0, u1, p, invf):
        cos, sin = _cos_sin(p, invf, False)
        qr = jnp.concatenate([_rot_chunk(c, cos, sin) for c in _chunks(q)], axis=1) * scale
        kr = [_rot_chunk(c, cos, sin) for c in _chunks(k)]
        kk = jnp.concatenate([_dup_head(c, odd) for c in kr for odd in (False, True)], axis=1)
        vv = jnp.concatenate([_dup_head(c, odd) for c in _chunks(v) for odd in (False, True)], axis=1)
        return qr, kk, vv, jnp.concatenate([u0, u1], axis=1)

    u_cb = (d_attn + 2 * d_kv) // half
    return rowwise("qkv_prep", fn, L,
                   [(proj, d_attn, 0, 0), (proj, d_kv, d_attn // d_kv, 0), (proj, d_kv, d_attn // d_kv + 1, 0),
                    (proj, half, u_cb, 0), (proj, half, u_cb + 1, 0), full(pos)],
                   [inv_freq], [(d_attn, bf16), (2 * d_kv, bf16), (2 * d_kv, bf16), (d_ssm, bf16)])


def qkv_grad(dq, dkk_c, dkk_p, dvv_c, dvv_p, du, pos, inv_freq):
    L, d_attn = dq.shape
    d_kv = dkk_c.shape[1] // 2
    scale = 1.0 / math.sqrt(HEAD_DIM)

    def fold(cur, prev, i, nt):
        t = cur + jnp.where(i < nt - 1, prev, 0.0)
        out = []
        for c in range(t.shape[1] // (2 * LANES)):
            even, odd = t[:, 2 * c * LANES:(2 * c + 1) * LANES], t[:, (2 * c + 1) * LANES:(2 * c + 2) * LANES]
            even, odd = even + pltpu.roll(even, HEAD_DIM, 1), odd + pltpu.roll(odd, HEAD_DIM, 1)
            out.append(jnp.where(_lane(even.shape) < HEAD_DIM, even, odd))
        return out

    def fn(i, nt, dq_t, kc, kp, vc, vp, du_t, p, invf):
        cos, sin = _cos_sin(p, invf, True)
        dq_o = jnp.concatenate([_rot_chunk(c, cos, sin) for c in _chunks(dq_t)], axis=1) * scale
        dk_o = jnp.concatenate([_rot_chunk(c, cos, sin) for c in fold(kc, kp, i, nt)], axis=1)
        dv_o = jnp.concatenate(fold(vc, vp, i, nt), axis=1)
        return (jnp.concatenate([dq_o, dk_o, dv_o, du_t], axis=1),)

    return rowwise("qkv_grad", fn, L,
                   [full(dq), full(dkk_c), (dkk_p, 2 * d_kv, 0, 1), full(dvv_c), (dvv_p, 2 * d_kv, 0, 1), full(du), full(pos)],
                   [inv_freq], [(d_attn + 2 * d_kv + du.shape[1], bf16)], tr=ATTN_BLOCK)[0]


def _attn_specs(L):
    nb = L // ATTN_BLOCK
    B = ATTN_BLOCK
    q_spec = lambda width: pl.BlockSpec((B, width), lambda n: (n, 0))
    prev = lambda width: pl.BlockSpec((B, width), lambda n: (jnp.maximum(n - 1, 0), 0))
    return nb, q_spec, prev


def _attn_mask(n):
    B = ATTN_BLOCK
    row = lax.broadcasted_iota(jnp.int32, (B, 2 * B), 0)
    col = lax.broadcasted_iota(jnp.int32, (B, 2 * B), 1)
    return ((col < B) & (col > row) & (n > 0)) | ((col >= B) & (row >= col - B))


def _attn_probs(qm, kcat, sink, mask):
    s = jnp.where(mask, _dot(qm, kcat, NT), NEG)
    m = jnp.maximum(jnp.max(s, axis=1, keepdims=True), sink)
    p, ps = jnp.exp(s - m), jnp.exp(sink - m)
    inv = 1.0 / (jnp.sum(p, axis=1, keepdims=True) + ps)
    return p, inv, ps


def _attn_heads(q_ref, s_ref, h, q_per_kv):
    low = _lane((ATTN_BLOCK, LANES)) < HEAD_DIM
    heads = []
    for pr in range(h * q_per_kv // 2, (h + 1) * q_per_kv // 2):
        q2 = q_ref[:, LANES * pr:LANES * (pr + 1)]
        for odd in (False, True):
            mine = low != odd
            sink = jnp.max(s_ref[2 * pr + int(odd):2 * pr + int(odd) + 1, :], axis=1, keepdims=True)
            heads.append((pr, mine, jnp.where(mine, q2, jnp.zeros_like(q2)), sink))
    return low, heads


def _kv_block(prev_ref, cur_ref, h):
    return jnp.concatenate([prev_ref[:, LANES * h:LANES * (h + 1)], cur_ref[:, LANES * h:LANES * (h + 1)]], axis=0)


def attn_fwd(qr, kk, vv, sink_b):
    L, d_attn = qr.shape
    nb, q_spec, prev = _attn_specs(L)
    d_kk = kk.shape[1]
    n_kv = d_kk // LANES
    q_per_kv = d_attn // HEAD_DIM // n_kv

    def body(q_ref, kc_ref, kp_ref, vc_ref, vp_ref, s_ref, o_ref):
        mask = _attn_mask(pl.program_id(0))
        for h in range(n_kv):
            kcat, vcat = _kv_block(kp_ref, kc_ref, h), _kv_block(vp_ref, vc_ref, h)
            low, heads = _attn_heads(q_ref, s_ref, h, q_per_kv)
            probs = [_attn_probs(qm, kcat, sink, mask) for (_, _, qm, sink) in heads]
            outs = [_dot(p.astype(bf16), vcat) * inv for (p, inv, _) in probs]
            for i in range(0, len(heads), 2):
                pr = heads[i][0]
                o_ref[:, LANES * pr:LANES * (pr + 1)] = jnp.where(low, outs[i], outs[i + 1])

    return pl.pallas_call(
        body, out_shape=jax.ShapeDtypeStruct((L, d_attn), f32), grid=(nb,),
        in_specs=[q_spec(d_attn), q_spec(d_kk), prev(d_kk), q_spec(d_kk), prev(d_kk), pl.BlockSpec(sink_b.shape, lambda n: (0, 0))],
        out_specs=q_spec(d_attn), name="attn_fwd", compiler_params=_params("arbitrary"))(qr, kk, kk, vv, vv, sink_b)


def attn_bwd(qr, kk, vv, sink_b, attn, d_attn_out):
    L, d_attn = qr.shape
    nb, q_spec, prev = _attn_specs(L)
    d_kk = kk.shape[1]
    n_kv = d_kk // LANES
    q_per_kv = d_attn // HEAD_DIM // n_kv

    def body(q_ref, kc_ref, kp_ref, vc_ref, vp_ref, s_ref, o_ref, do_ref, dq_ref, dkc_ref, dkp_ref, dvc_ref, dvp_ref, ds_ref):
        n = pl.program_id(0)
        B = ATTN_BLOCK
        mask = _attn_mask(n)
        srow = lax.broadcasted_iota(jnp.int32, (SUBLANES, LANES), 0)

        @pl.when(n == 0)
        def _():
            ds_ref[...] = jnp.zeros_like(ds_ref)

        for h in range(n_kv):
            kcat, vcat = _kv_block(kp_ref, kc_ref, h), _kv_block(vp_ref, vc_ref, h)
            low, heads = _attn_heads(q_ref, s_ref, h, q_per_kv)
            probs = [_attn_probs(qm, kcat, sink, mask) for (_, _, qm, sink) in heads]
            dk = jnp.zeros((2 * B, LANES), f32)
            dv = dk
            dsink = jnp.zeros((SUBLANES, LANES), f32)
            dqs = []
            for i, ((pr, mine, qm, _), (p, inv, ps)) in enumerate(zip(heads, probs)):
                do2 = do_ref[:, LANES * pr:LANES * (pr + 1)]
                delta = jnp.sum(jnp.where(mine, do2 * o_ref[:, LANES * pr:LANES * (pr + 1)], 0.0), axis=1, keepdims=True)
                dob = jnp.where(mine, do2, 0.0).astype(bf16)
                p = p * inv
                ds = (p * (_dot(dob, vcat, NT) - delta)).astype(bf16)
                dqs.append(_dot(ds, kcat))
                dk = dk + _dot(ds, qm, TN)
                dv = dv + _dot(p.astype(bf16), dob, TN)
                dsink = dsink + jnp.where(srow == i, -jnp.sum(ps * inv * delta), 0.0)
            for i in range(0, len(heads), 2):
                pr = heads[i][0]
                dq_ref[:, LANES * pr:LANES * (pr + 1)] = jnp.where(low, dqs[i], dqs[i + 1])
            cols = slice(LANES * h, LANES * (h + 1))
            dkp_ref[:, cols] = dk[:B]
            dkc_ref[:, cols] = dk[B:]
            dvp_ref[:, cols] = dv[:B]
            dvc_ref[:, cols] = dv[B:]
            ds_ref[h] += dsink

    kv_shape = jax.ShapeDtypeStruct(kk.shape, f32)
    ds_shape = (n_kv, SUBLANES, LANES)
    return pl.pallas_call(
        body,
        out_shape=[jax.ShapeDtypeStruct((L, d_attn), f32), kv_shape, kv_shape, kv_shape, kv_shape, jax.ShapeDtypeStruct(ds_shape, f32)],
        grid=(nb,),
        in_specs=[q_spec(d_attn), q_spec(d_kk), prev(d_kk), q_spec(d_kk), prev(d_kk), pl.BlockSpec(sink_b.shape, lambda n: (0, 0)),
                  q_spec(d_attn), q_spec(d_attn)],
        out_specs=[q_spec(d_attn)] + [q_spec(d_kk)] * 4 + [pl.BlockSpec(ds_shape, lambda n: (0, 0, 0))],
        name="attn_bwd", compiler_params=_params("arbitrary"))(qr, kk, kk, vv, vv, sink_b, attn, d_attn_out)


SSM_T = 128
NQ = SUBLANES * SSM_STATE // LANES
NJ = SUBLANES


def _strided_put(ref, j, val):
    for q in range(NQ):
        ref.at[q][pl.ds(j, SSM_T, stride=NJ), :] = val[:, LANES * q:LANES * (q + 1)]


def _strided_get(ref, j):
    return jnp.concatenate([ref.at[q][pl.ds(j, SSM_T, stride=NJ), :] for q in range(NQ)], axis=1)


def _ssm_specs(L, rev):
    nt = L // SSM_T
    idx = (lambda i: nt - 1 - i) if rev else (lambda i: i)
    row = lambda w, cb=0: pl.BlockSpec((SSM_T, w), lambda i: (idx(i), cb))
    state = pl.BlockSpec((NQ, SSM_T * NJ, LANES), lambda i: (0, idx(i), 0))
    whole = lambda a: pl.BlockSpec(a.shape, lambda i: (0,) * a.ndim)
    return nt, row, state, whole


def ssm_fwd(u_bf, proj, u_cb, bd_re, bd_im, cd_re, cd_im, lam_re, lam_im, d_skip, dep=None):
    L, d_ssm = u_bf.shape
    nt, row, state, whole = _ssm_specs(L, False)
    half = d_ssm // 2
    gw = d_ssm // NJ
    n_dep = 0 if dep is None else 1

    def body(u_ref, u0_ref, u1_ref, bdr, bdi, cdr, cdi, lr_ref, li_ref, d_ref, *rest):
        y_ref, z_ref, sr_ref, si_ref, carry = rest[n_dep:]
        i = pl.program_id(0)

        @pl.when(i == 0)
        def _():
            carry[...] = jnp.zeros_like(carry)

        for j in range(NJ):
            uj = u_ref[:, gw * j:gw * (j + 1)]
            _strided_put(sr_ref, j, _dot(uj, bdr[j]))
            _strided_put(si_ref, j, _dot(uj, bdi[j]))
        lr = [lr_ref[q] for q in range(NQ)]
        li = [li_ref[q] for q in range(NQ)]

        def step(t, s):
            sr, si = s
            rows = pl.ds(pl.multiple_of(t * NJ, NJ), NJ)
            nr = tuple(lr[q] * sr[q] - li[q] * si[q] + sr_ref[q, rows, :] for q in range(NQ))
            ni = tuple(lr[q] * si[q] + li[q] * sr[q] + si_ref[q, rows, :] for q in range(NQ))
            for q in range(NQ):
                sr_ref[q, rows, :] = nr[q]
                si_ref[q, rows, :] = ni[q]
            return nr, ni

        init = (tuple(carry[0, q] for q in range(NQ)), tuple(carry[1, q] for q in range(NQ)))
        sr, si = lax.fori_loop(0, SSM_T, step, init, unroll=8)
        for q in range(NQ):
            carry[0, q] = sr[q]
            carry[1, q] = si[q]
        uf = jnp.concatenate([u0_ref[...], u1_ref[...]], axis=1)
        for j in range(NJ):
            cols = slice(gw * j, gw * (j + 1))
            yj = _dot(_strided_get(sr_ref, j).astype(bf16), cdr[j]) - _dot(_strided_get(si_ref, j).astype(bf16), cdi[j])
            yj = yj + d_ref[:, cols] * uf[:, cols]
            y_ref[:, cols] = yj
            z_ref[:, cols] = gelu(yj).astype(bf16)

    s_shape = jax.ShapeDtypeStruct((NQ, L * NJ, LANES), f32)
    consts = (bd_re, bd_im, cd_re, cd_im, lam_re, lam_im, d_skip)
    in_specs, operands, _ = _with_dep([row(d_ssm), row(half, u_cb), row(half, u_cb + 1)] + [whole(a) for a in consts],
                                      [u_bf, proj, proj, *consts], dep)
    return pl.pallas_call(
        body, out_shape=[jax.ShapeDtypeStruct((L, d_ssm), f32), jax.ShapeDtypeStruct((L, d_ssm), bf16), s_shape, s_shape], grid=(nt,),
        in_specs=in_specs, out_specs=[row(d_ssm), row(d_ssm), state, state],
        scratch_shapes=[pltpu.VMEM((2, NQ, NJ, LANES), f32)], name="ssm_fwd",
        compiler_params=_params("arbitrary"))(*operands)


def ssm_bwd(y, dz1, dz2, u_bf, proj, u_cb, s_re, s_im, bd_re, bd_im, cd_re, cd_im, lam_re, lam_im, d_skip, dep=None):
    L, d_ssm = y.shape
    nt, row, state, whole = _ssm_specs(L, True)
    half = d_ssm // 2
    gw = d_ssm // NJ
    n_dep = 0 if dep is None else 1

    def body(y_ref, dz1_ref, dz2_ref, u_ref, u0_ref, u1_ref, sr_ref, si_ref, bdr, bdi, cdr, cdi, lr_ref, li_ref, d_ref, *rest):
        du_ref, fbr, fbi, fcr, fci, dlr, dli, dd_ref, gr_ref, gi_ref, carry, dbdr, dbdi, dcdr, dcdi = rest[n_dep:]
        i = pl.program_id(0)

        @pl.when(i == 0)
        def _():
            carry[...] = jnp.zeros_like(carry)
            for r in (dbdr, dbdi, dcdr, dcdi, dlr, dli, dd_ref):
                r[...] = jnp.zeros_like(r)

        dyf = (dz1_ref[...] + dz2_ref[...]) * gelu_grad(y_ref[...])
        dyb = dyf.astype(bf16)
        for j in range(NJ):
            dyj = dyb[:, gw * j:gw * (j + 1)]
            _strided_put(gr_ref, j, _dot(dyj, cdr[j], NT))
            _strided_put(gi_ref, j, -_dot(dyj, cdi[j], NT))
            dcdr[j] += _dot(_strided_get(sr_ref, j).astype(bf16), dyj, TN)
            dcdi[j] -= _dot(_strided_get(si_ref, j).astype(bf16), dyj, TN)
        lr = [lr_ref[q] for q in range(NQ)]
        li = [li_ref[q] for q in range(NQ)]

        def step(k, c):
            gr, gi, ar, ai = c
            rows = pl.ds(pl.multiple_of((SSM_T - 1 - k) * NJ, NJ), NJ)
            s_r = [sr_ref[q, rows, :] for q in range(NQ)]
            s_i = [si_ref[q, rows, :] for q in range(NQ)]
            ar = tuple(ar[q] + gr[q] * s_r[q] + gi[q] * s_i[q] for q in range(NQ))
            ai = tuple(ai[q] + gi[q] * s_r[q] - gr[q] * s_i[q] for q in range(NQ))
            nr = tuple(gr_ref[q, rows, :] + lr[q] * gr[q] + li[q] * gi[q] for q in range(NQ))
            ni = tuple(gi_ref[q, rows, :] + lr[q] * gi[q] - li[q] * gr[q] for q in range(NQ))
            for q in range(NQ):
                gr_ref[q, rows, :] = nr[q]
                gi_ref[q, rows, :] = ni[q]
            return nr, ni, ar, ai

        zero = tuple(jnp.zeros((NJ, LANES), f32) for _ in range(NQ))
        init = (tuple(carry[0, q] for q in range(NQ)), tuple(carry[1, q] for q in range(NQ)), zero, zero)
        gr, gi, ar, ai = lax.fori_loop(0, SSM_T, step, init, unroll=8)
        for q in range(NQ):
            carry[0, q] = gr[q]
            carry[1, q] = gi[q]
            dlr[q] += ar[q]
            dli[q] += ai[q]
        uf = jnp.concatenate([u0_ref[...], u1_ref[...]], axis=1)
        dd_ref[...] += colsum(dyf * uf)
        for j in range(NJ):
            cols = slice(gw * j, gw * (j + 1))
            gjr, gji = _strided_get(gr_ref, j).astype(bf16), _strided_get(gi_ref, j).astype(bf16)
            du_ref[:, cols] = _dot(gjr, bdr[j], NT) + _dot(gji, bdi[j], NT) + d_ref[:, cols] * dyf[:, cols]
            uj = u_ref[:, cols]
            dbdr[j] += _dot(uj, gjr, TN)
            dbdi[j] += _dot(uj, gji, TN)

        @pl.when(i == nt - 1)
        def _():
            nb = NJ * SSM_STATE
            b_diag = (lax.broadcasted_iota(jnp.int32, (gw, nb), 0) // SSM_GROUP) == (lax.broadcasted_iota(jnp.int32, (gw, nb), 1) // SSM_STATE)
            c_diag = (lax.broadcasted_iota(jnp.int32, (nb, gw), 0) // SSM_STATE) == (lax.broadcasted_iota(jnp.int32, (nb, gw), 1) // SSM_GROUP)
            for j in range(NJ):
                for acc_ref, out in ((dbdr, fbr), (dbdi, fbi)):
                    m = jnp.where(b_diag, acc_ref[j], 0.0)
                    f = m[:, :LANES]
                    for q in range(1, nb // LANES):
                        f = f + m[:, LANES * q:LANES * (q + 1)]
                    out[j] = f + pltpu.roll(f, SSM_STATE, 1)
                for acc_ref, out in ((dcdr, fcr), (dcdi, fci)):
                    f = jnp.where(c_diag, acc_ref[j], 0.0)
                    for shift in (64, 32, 16):
                        f = f + pltpu.roll(f, shift, 1)
                    out[j] = f

    consts = (bd_re, bd_im, cd_re, cd_im, lam_re, lam_im, d_skip)
    acc = lambda a: jax.ShapeDtypeStruct(a.shape, f32)
    fb = jax.ShapeDtypeStruct((NJ, gw, LANES), f32)
    fc = jax.ShapeDtypeStruct((NJ, NJ * SSM_STATE, LANES), f32)
    outs = [jax.ShapeDtypeStruct((L, d_ssm), f32), fb, fb, fc, fc, acc(lam_re), acc(lam_im), acc(d_skip)]
    in_specs, operands, _ = _with_dep(
        [row(d_ssm)] * 4 + [row(half, u_cb), row(half, u_cb + 1), state, state] + [whole(a) for a in consts],
        [y, dz1, dz2, u_bf, proj, proj, s_re, s_im, *consts], dep)
    return pl.pallas_call(
        body, out_shape=outs, grid=(nt,),
        in_specs=in_specs, out_specs=[row(d_ssm)] + [whole(a) for a in outs[1:]],
        scratch_shapes=[pltpu.VMEM((NQ, SSM_T * NJ, LANES), f32), pltpu.VMEM((NQ, SSM_T * NJ, LANES), f32),
                        pltpu.VMEM((2, NQ, NJ, LANES), f32)] + [pltpu.VMEM(a.shape, f32) for a in (bd_re, bd_im, cd_re, cd_im)],
        name="ssm_bwd", compiler_params=_params("arbitrary"))(*operands)


def _cmul(ar, ai, br, bi):
    return ar * br - ai * bi, ar * bi + ai * br


def _disc(ar, ai, logdt):
    dt = jnp.exp(logdt)
    mag = jnp.exp(ar * dt)
    lr, li = mag * jnp.cos(ai * dt), mag * jnp.sin(ai * dt)
    den = ar * ar + ai * ai
    nr, ni = lr - 1.0, li
    fr, fi = (nr * ar + ni * ai) / den, (ni * ar - nr * ai) / den
    return dt, lr, li, den, fr, fi


def ssm_params(a_re, a_im, logdt_b, bt_re, bt_im, spread):
    def body(ar_ref, ai_ref, ld_ref, br_ref, bi_ref, sp_ref, lr_ref, li_ref, or_ref, oi_ref):
        _, lr, li, _, fr, fi = _disc(ar_ref[...], ai_ref[...], ld_ref[...])
        lr_ref[...] = lr
        li_ref[...] = li
        fre = jnp.dot(sp_ref[...], fr, precision=HIGHEST, preferred_element_type=f32)
        fie = jnp.dot(sp_ref[...], fi, precision=HIGHEST, preferred_element_type=f32)
        o_r, o_i = _cmul(fre, fie, br_ref[...], bi_ref[...])
        or_ref[...] = o_r
        oi_ref[...] = o_i

    g = jax.ShapeDtypeStruct(a_re.shape, f32)
    b = jax.ShapeDtypeStruct(bt_re.shape, f32)
    return pl.pallas_call(body, out_shape=[g, g, b, b], name="ssm_params",
                          compiler_params=_params())(a_re, a_im, logdt_b, bt_re, bt_im, spread)


def ssm_params_grad(a_re, a_im, logdt_b, bt_re, bt_im, spread, gather, dlam_re, dlam_im, dbt_re, dbt_im):
    def body(ar_ref, ai_ref, ld_ref, br_ref, bi_ref, sp_ref, ga_ref, glr_ref, gli_ref, gbr_ref, gbi_ref,
             dar_ref, dai_ref, dld_ref, dbr_ref, dbi_ref):
        ar, ai = ar_ref[...], ai_ref[...]
        dt, lr, li, den, fr, fi = _disc(ar, ai, ld_ref[...])
        hdot = functools.partial(jnp.dot, precision=HIGHEST, preferred_element_type=f32)
        fre, fie = hdot(sp_ref[...], fr), hdot(sp_ref[...], fi)
        gbr, gbi, br, bi = gbr_ref[...], gbi_ref[...], br_ref[...], bi_ref[...]
        dbr_ref[...], dbi_ref[...] = _cmul(fre, -fie, gbr, gbi)
        t_r, t_i = _cmul(br, -bi, gbr, gbi)
        gfr, gfi = hdot(ga_ref[...], t_r), hdot(ga_ref[...], t_i)
        iwr, iwi = ar / den, -ai / den
        x_r, x_i = _cmul(iwr, -iwi, gfr, gfi)
        glr, gli = glr_ref[...] + x_r, gli_ref[...] + x_i
        q_r, q_i = _cmul(fr, fi, iwr, iwi)
        gwr, gwi = _cmul(-q_r, q_i, gfr, gfi)
        y_r, y_i = _cmul(dt * lr, -dt * li, glr, gli)
        dar_ref[...] = gwr + y_r
        dai_ref[...] = gwi + y_i
        wl_r, wl_i = _cmul(ar, ai, lr, li)
        z_r, _ = _cmul(wl_r, -wl_i, glr, gli)
        dld_ref[...] = jnp.sum(z_r * dt, axis=1, keepdims=True)

    g = jax.ShapeDtypeStruct(a_re.shape, f32)
    b = jax.ShapeDtypeStruct(bt_re.shape, f32)
    return pl.pallas_call(body, out_shape=[g, g, jax.ShapeDtypeStruct((a_re.shape[0], 1), f32), b, b], name="ssm_params_grad",
                          compiler_params=_params())(a_re, a_im, logdt_b, bt_re, bt_im, spread, gather, dlam_re, dlam_im, dbt_re, dbt_im)


def _block_diag(t, rows, cols):
    G = t.shape[0]
    t = t.reshape(G // NJ, NJ, rows, cols)
    eye = jnp.eye(NJ, dtype=t.dtype)
    return jnp.einsum('jgrc,gh->jgrhc', t, eye).reshape(G // NJ, NJ * rows, NJ * cols)


def _state_layout(t):
    return t.reshape(NJ, NQ, LANES).transpose(1, 0, 2)


def _state_layout_inv(t, G, N):
    return t.transpose(1, 0, 2).reshape(G, N)


def _tiles2d(shape, budget_rows=128):
    rows, cols = shape
    tr = rows
    if rows > budget_rows:
        tr = budget_rows
        while rows % tr:
            tr -= SUBLANES
    return tr, cols


ADAM_TILE_BYTES = 3 << 19
ROW_ALIGN = 16


def _tile_rows(rows, row_bytes, target_bytes):
    tr = max(ROW_ALIGN, min(rows, target_bytes // row_bytes) // ROW_ALIGN * ROW_ALIGN)
    while rows % tr:
        tr -= ROW_ALIGN
    return tr


def _adam_update(w, g, m, v):
    c1 = 1.0 - ADAM_B1 ** ADAM_STEP
    c2 = 1.0 - ADAM_B2 ** ADAM_STEP
    nm = ADAM_B1 * m + (1.0 - ADAM_B1) * g
    nv = ADAM_B2 * v + (1.0 - ADAM_B2) * (g * g)
    delta = -ADAM_LR * ((nm / c1) / (jnp.sqrt(nv / c2) + ADAM_EPS) + ADAM_WD * w)
    return delta, nm, nv


def adamw_many(name, ws, gs, ms, vs):
    n = len(ws)

    def body(*refs):
        w, g, m, v = (refs[k * n:(k + 1) * n] for k in range(4))
        d, nm, nv = (refs[(4 + k) * n:(5 + k) * n] for k in range(3))
        for i in range(n):
            d[i][...], nm[i][...], nv[i][...] = _adam_update(w[i][...], g[i][...], m[i][...], v[i][...])

    o = [jax.ShapeDtypeStruct(a.shape, f32) for a in ws]
    outs = pl.pallas_call(body, out_shape=o * 3, name=name, compiler_params=_params())(*ws, *gs, *ms, *vs)
    return outs[:n], outs[n:2 * n], outs[2 * n:]


def adamw_halves(name, w, own, got, m, v, c_arr):
    h, cols = own.shape
    tr = _tile_rows(h, cols * 4, ADAM_TILE_BYTES)
    nh = h // tr

    def body(c_ref, w_ref, own_ref, got_ref, m_ref, v_ref, g_ref, d_ref, nm_ref, nv_ref):
        mine = (pl.program_id(0) // nh) == c_ref[0]
        g = jnp.where(mine, own_ref[...], got_ref[...])
        g_ref[...] = g
        d_ref[...], nm_ref[...], nv_ref[...] = _adam_update(w_ref[...], g, m_ref[...], v_ref[...])

    spec = pl.BlockSpec((tr, cols), lambda i, c: (i, 0))
    own_spec = pl.BlockSpec((tr, cols), lambda i, c: (jnp.where(i // nh == c[0], i % nh, 0), 0))
    got_spec = pl.BlockSpec((tr, cols), lambda i, c: (jnp.where(i // nh == c[0], 0, i % nh), 0))
    o = jax.ShapeDtypeStruct(w.shape, f32)
    grid_spec = pltpu.PrefetchScalarGridSpec(num_scalar_prefetch=1, grid=(2 * nh,),
                                             in_specs=[spec, own_spec, got_spec, spec, spec], out_specs=[spec] * 4)
    return pl.pallas_call(body, out_shape=[o, o, o, o], grid_spec=grid_spec, name=name,
                          compiler_params=_params("arbitrary"))(c_arr, w, own, got, m, v)


def pair_sum(name, g, got, c_arr):
    S, h, cols = got.shape
    tr, _ = _tiles2d((h, cols), 1024)
    nh = h // tr

    def body(c_ref, g_ref, o_ref, out_ref):
        out_ref[...] = (g_ref[...].astype(f32) + o_ref[...].astype(f32)).astype(out_ref.dtype)

    spec = pl.BlockSpec((None, tr, cols), lambda s, i, c: (s, i, 0))
    grid_spec = pltpu.PrefetchScalarGridSpec(
        num_scalar_prefetch=1, grid=(S, nh),
        in_specs=[pl.BlockSpec((None, tr, cols), lambda s, i, c: (s, c[0] * nh + i, 0)), spec], out_specs=spec)
    return pl.pallas_call(body, out_shape=jax.ShapeDtypeStruct(got.shape, g.dtype), grid_spec=grid_spec, name=name,
                          compiler_params=_params("parallel", "parallel"))(c_arr, g, got)


def chip_sum(name, pair, landed, mine_arr, dep=None):
    n_in, h, cols = landed.shape
    tr, _ = _tiles2d((h, cols), 256)

    def body(s_ref, p_ref, l_ref, *rest):
        acc = p_ref[...].astype(f32)
        for k in range(n_in):
            acc = acc + l_ref[k].astype(f32)
        rest[-1][...] = acc

    in_specs, operands, _ = _with_dep(
        [pl.BlockSpec((None, tr, cols), lambda i, s: (s[0], i, 0)), pl.BlockSpec((n_in, tr, cols), lambda i, s: (0, i, 0))],
        [pair, landed], dep)
    grid_spec = pltpu.PrefetchScalarGridSpec(num_scalar_prefetch=1, grid=(h // tr,), in_specs=in_specs,
                                             out_specs=pl.BlockSpec((tr, cols), lambda i, s: (i, 0)))
    return pl.pallas_call(body, out_shape=jax.ShapeDtypeStruct((h, cols), f32), grid_spec=grid_spec, name=name,
                          compiler_params=_params("parallel"))(mine_arr, *operands)


def into_slot(name, w, slot_arr, n_slots, dtype, dep=None):
    tr, cols = _tiles2d(w.shape, 256)

    def body(s_ref, w_ref, *rest):
        rest[-1][...] = w_ref[...].astype(dtype)

    in_specs, operands, _ = _with_dep([pl.BlockSpec((tr, cols), lambda i, s: (i, 0))], [w], dep)
    grid_spec = pltpu.PrefetchScalarGridSpec(num_scalar_prefetch=1, grid=(w.shape[0] // tr,), in_specs=in_specs,
                                             out_specs=pl.BlockSpec((None, tr, cols), lambda i, s: (s[0], i, 0)))
    return pl.pallas_call(body, out_shape=jax.ShapeDtypeStruct((n_slots,) + w.shape, dtype), grid_spec=grid_spec, name=name,
                          compiler_params=_params("parallel"))(slot_arr, *operands)


def sum_slots(name, t):
    S, rows, cols = t.shape
    tr, _ = _tiles2d((rows, cols), 256)

    def body(t_ref, o_ref):
        acc = t_ref[0]
        for s in range(1, S):
            acc = acc + t_ref[s]
        o_ref[...] = acc

    return pl.pallas_call(body, out_shape=jax.ShapeDtypeStruct((rows, cols), f32), grid=(rows // tr,),
                          in_specs=[pl.BlockSpec((S, tr, cols), lambda i: (0, i, 0))], out_specs=pl.BlockSpec((tr, cols), lambda i: (i, 0)),
                          name=name, compiler_params=_params("parallel"))(t)


def _place():
    x, y, c = lax.axis_index("x"), lax.axis_index("y"), lax.axis_index("c")
    return x, y, c


def _other_chips(x, y):
    return [(1 - x, y, 2 * (1 - x) + y), (x, 1 - y, 2 * x + 1 - y), (1 - x, 1 - y, 2 * (1 - x) + 1 - y)]


SEM = pl.BlockSpec(memory_space=pltpu.SEMAPHORE)
VM = pl.BlockSpec(memory_space=pltpu.VMEM)
DATAFLOW = pltpu.SideEffectType.DATAFLOW_SIDE_EFFECTING
TOKEN = jax.ShapeDtypeStruct((SUBLANES, LANES), f32)


def _gather_copy(buf, w, k, chip, c, mine, send, recv):
    px, py, _ = chip
    h = buf.shape[1] // 2
    half = buf.at[mine, pl.ds(c * h, h), :]
    return pltpu.make_async_remote_copy(src_ref=half, dst_ref=half, send_sem=send.at[3 * w + k], recv_sem=recv.at[3 * w + k],
                                        device_id=(px, py, c), device_id_type=MESH)


def _gather_landing(buf, w, k, chip, c, send, recv):
    px, py, s = chip
    h = buf.shape[1] // 2
    landed = buf.at[s, pl.ds(c * h, h), :]
    return pltpu.make_async_remote_copy(src_ref=landed, dst_ref=landed, send_sem=send.at[3 * w + k], recv_sem=recv.at[3 * w + k],
                                        device_id=(px, py, c), device_id_type=MESH)


def gather_start(name, bufs, groups, after, neighbours_only=()):
    nw, ng = len(bufs), len(groups)

    def body(*refs):
        outs = refs[nw + 1:]
        sems, dst = outs[:2 * ng], outs[2 * ng:2 * ng + nw]
        token = outs[2 * ng + nw]
        x, y, c = _place()
        mine = 2 * x + y
        for g, members in enumerate(groups):
            for i, w in enumerate(members):
                for k, chip in enumerate(_other_chips(x, y)[:2 if w in neighbours_only else 3]):
                    _gather_copy(dst[w], i, k, chip, c, mine, sems[2 * g], sems[2 * g + 1]).start()
        token[...] = jnp.zeros_like(token)

    sem_shapes = []
    for members in groups:
        sem_shapes += [pltpu.SemaphoreType.DMA((3 * len(members),))] * 2
    outs = pl.pallas_call(
        body, out_shape=sem_shapes + [jax.ShapeDtypeStruct(b.shape, b.dtype) for b in bufs] + [TOKEN],
        in_specs=[ANY] * (nw + 1), out_specs=[SEM] * (2 * ng) + [ANY] * nw + [VM],
        input_output_aliases={w: 2 * ng + w for w in range(nw)}, name=name,
        compiler_params=pltpu.CompilerParams(has_side_effects=DATAFLOW))(*bufs, after)
    return [(outs[2 * g], outs[2 * g + 1]) for g in range(ng)], list(outs[2 * ng:2 * ng + nw]), outs[2 * ng + nw]


def gather_wait(name, bufs, send, recv, after):
    nw = len(bufs)

    def body(*refs):
        src = refs[:nw]
        send_ref, recv_ref = refs[nw], refs[nw + 1]
        x, y, c = _place()
        mine = 2 * x + y
        for w in range(nw):
            for k, chip in enumerate(_other_chips(x, y)):
                _gather_copy(src[w], w, k, chip, c, mine, send_ref, recv_ref).wait_send()
                _gather_landing(src[w], w, k, chip, c, send_ref, recv_ref).wait_recv()

    return pl.pallas_call(
        body, out_shape=[jax.ShapeDtypeStruct(b.shape, b.dtype) for b in bufs],
        in_specs=[ANY] * nw + [SEM, SEM, ANY], out_specs=[ANY] * nw,
        input_output_aliases={w: w for w in range(nw)}, name=name,
        compiler_params=pltpu.CompilerParams(has_side_effects=DATAFLOW))(*bufs, send, recv, after)


def _relay_copy(buf, w, j, x, y, c, send, recv, landing):
    chips = _other_chips(x, y)
    px, py, _ = chips[j]
    h = buf.shape[1] // 2
    q = h // 2
    s = chips[2][2] if landing else chips[1 - j][2]
    part = buf.at[s, pl.ds(c * h + j * q, q), :]
    return pltpu.make_async_remote_copy(src_ref=part, dst_ref=part, send_sem=send.at[2 * w + j], recv_sem=recv.at[2 * w + j],
                                        device_id=(px, py, c), device_id_type=MESH)


def _early_pass(buf, nw, w, k, x, y, c, send, recv, landing):
    s = _other_chips(x, y)[k][2]
    h = buf.shape[1] // 2
    part = buf.at[s, pl.ds(((1 - c) if landing else c) * h, h), :]
    i = 2 * nw + 2 * w + k
    return pltpu.make_async_remote_copy(src_ref=part, dst_ref=part, send_sem=send.at[i], recv_sem=recv.at[i],
                                        device_id=(x, y, 1 - c), device_id_type=MESH)


def gather_relay(name, bufs, sems, more, after):
    nw, nm = len(bufs), len(more)
    ns = 4 if nm else 2

    def body(*refs):
        ins, outs = refs[:nw + nm + 2 * nw + 1], refs[nw + nm + 2 * nw + 1:]
        src, d_sems = ins[:nw], ins[nw + nm:nw + nm + 2 * nw]
        r_send, r_recv = outs[:2]
        m_send, m_recv = outs[2:ns] if nm else (None, None)
        dst, mdst, token = outs[ns:ns + nw], outs[ns + nw:ns + nw + nm], outs[ns + nw + nm]
        x, y, c = _place()
        mine = 2 * x + y
        chips = _other_chips(x, y)
        for w in range(nw):
            for k in range(2):
                _gather_copy(src[w], 0, k, chips[k], c, mine, d_sems[2 * w], d_sems[2 * w + 1]).wait_send()
                _gather_landing(src[w], 0, k, chips[k], c, d_sems[2 * w], d_sems[2 * w + 1]).wait_recv()
            for j in range(2):
                _relay_copy(dst[w], w, j, x, y, c, r_send, r_recv, False).start()
            for k in range(2):
                _early_pass(dst[w], nw, w, k, x, y, c, r_send, r_recv, False).start()
        for w in range(nm):
            for k, chip in enumerate(chips):
                _gather_copy(mdst[w], w, k, chip, c, mine, m_send, m_recv).start()
        token[...] = jnp.zeros_like(token)

    sem_shapes = [pltpu.SemaphoreType.DMA((4 * nw,))] * 2 + [pltpu.SemaphoreType.DMA((3 * nm,))] * (ns - 2)
    flat_sems = [s for pair in sems for s in pair]
    outs = pl.pallas_call(
        body, out_shape=sem_shapes + [jax.ShapeDtypeStruct(b.shape, b.dtype) for b in list(bufs) + list(more)] + [TOKEN],
        in_specs=[ANY] * (nw + nm) + [SEM] * (2 * nw) + [ANY], out_specs=[SEM] * ns + [ANY] * (nw + nm) + [VM],
        input_output_aliases={i: ns + i for i in range(nw + nm)}, name=name,
        compiler_params=pltpu.CompilerParams(has_side_effects=DATAFLOW))(*bufs, *more, *flat_sems, after)
    m_sems = (outs[2], outs[3]) if nm else None
    return outs[0], outs[1], m_sems, list(outs[ns:ns + nw]), list(outs[ns + nw:ns + nw + nm]), outs[ns + nw + nm]


def gather_wait_relay(name, bufs, r_send, r_recv, after):
    nw = len(bufs)

    def body(*refs):
        src = refs[:nw]
        send_ref, recv_ref = refs[nw], refs[nw + 1]
        x, y, c = _place()
        for w in range(nw):
            for j in range(2):
                _relay_copy(src[w], w, j, x, y, c, send_ref, recv_ref, False).wait_send()
                _relay_copy(src[w], w, j, x, y, c, send_ref, recv_ref, True).wait_recv()
                _early_pass(src[w], nw, w, j, x, y, c, send_ref, recv_ref, False).wait_send()
                _early_pass(src[w], nw, w, j, x, y, c, send_ref, recv_ref, True).wait_recv()

    return pl.pallas_call(
        body, out_shape=[jax.ShapeDtypeStruct(b.shape, b.dtype) for b in bufs],
        in_specs=[ANY] * nw + [SEM, SEM, ANY], out_specs=[ANY] * nw,
        input_output_aliases={w: w for w in range(nw)}, name=name,
        compiler_params=pltpu.CompilerParams(has_side_effects=DATAFLOW))(*bufs, r_send, r_recv, after)


def gather_forward(name, bufs, which=(0, 1, 2)):
    nw = len(bufs)

    def body(*refs):
        dst = refs[nw:2 * nw]
        send, recv = refs[2 * nw:]
        x, y, c = _place()
        sib = (x, y, 1 - c)
        barrier = pltpu.get_barrier_semaphore()
        pl.semaphore_signal(barrier, inc=1, device_id=sib, device_id_type=MESH)
        pl.semaphore_wait(barrier, 1)
        cps = []
        for w in range(nw):
            h = dst[w].shape[1] // 2
            for k in which:
                s = _other_chips(x, y)[k][2]
                landed = dst[w].at[s, pl.ds(c * h, h), :]
                cp = pltpu.make_async_remote_copy(src_ref=landed, dst_ref=landed, send_sem=send.at[w, k], recv_sem=recv.at[w, k],
                                                  device_id=sib, device_id_type=MESH)
                cp.start()
                cps.append(cp)
        for w in range(nw):
            h = dst[w].shape[1] // 2
            for k in which:
                s = _other_chips(x, y)[k][2]
                other = dst[w].at[s, pl.ds((1 - c) * h, h), :]
                pltpu.make_async_remote_copy(src_ref=other, dst_ref=other, send_sem=send.at[w, k], recv_sem=recv.at[w, k],
                                             device_id=sib, device_id_type=MESH).wait_recv()
        for cp in cps:
            cp.wait_send()

    sem = pltpu.SemaphoreType.DMA((nw, 3))
    return pl.pallas_call(
        body, out_shape=[jax.ShapeDtypeStruct(b.shape, b.dtype) for b in bufs],
        in_specs=[ANY] * nw, out_specs=[ANY] * nw, input_output_aliases={w: w for w in range(nw)},
        scratch_shapes=[sem, sem], name=name,
        compiler_params=pltpu.CompilerParams(has_side_effects=True, collective_id=SIBLING_PAIR))(*bufs)


def _scatter_copy(src, dst, w, k, chip, c, send, recv):
    px, py, s = chip
    return pltpu.make_async_remote_copy(src_ref=src.at[s], dst_ref=dst.at[k], send_sem=send.at[3 * w + k], recv_sem=recv.at[3 * w + k],
                                        device_id=(px, py, c), device_id_type=MESH)


def scatter_start(name, parts):
    nw = len(parts)
    lands = [pltpu.with_memory_space_constraint(lax.empty((N_CHIPS - 1,) + p.shape[1:], p.dtype), pltpu.HBM) for p in parts]

    def body(*refs):
        outs = refs[2 * nw:]
        send, recv = outs[0], outs[1]
        src, dst, token = outs[2:2 + nw], outs[2 + nw:2 + 2 * nw], outs[2 + 2 * nw]
        x, y, c = _place()
        for w in range(nw):
            for k, chip in enumerate(_other_chips(x, y)):
                _scatter_copy(src[w], dst[w], w, k, chip, c, send, recv).start()
        token[...] = jnp.zeros_like(token)

    sem = pltpu.SemaphoreType.DMA((3 * nw,))
    outs = pl.pallas_call(
        body, out_shape=[sem, sem] + [jax.ShapeDtypeStruct(p.shape, p.dtype) for p in parts]
        + [jax.ShapeDtypeStruct(l.shape, l.dtype) for l in lands] + [TOKEN],
        in_specs=[ANY] * (2 * nw), out_specs=[SEM, SEM] + [ANY] * (2 * nw) + [VM],
        input_output_aliases={i: 2 + i for i in range(2 * nw)}, name=name,
        compiler_params=pltpu.CompilerParams(has_side_effects=DATAFLOW))(*parts, *lands)
    return outs[0], outs[1], list(outs[2:2 + nw]), list(outs[2 + nw:2 + 2 * nw]), outs[2 + 2 * nw]


def scatter_wait(name, parts, lands, send, recv, after):
    nw = len(parts)

    def body(*refs):
        src, dst = refs[:nw], refs[nw:2 * nw]
        send_ref, recv_ref = refs[2 * nw], refs[2 * nw + 1]
        x, y, c = _place()
        for w in range(nw):
            for k, chip in enumerate(_other_chips(x, y)):
                cp = _scatter_copy(src[w], dst[w], w, k, chip, c, send_ref, recv_ref)
                cp.wait_send()
                cp.wait_recv()

    outs = pl.pallas_call(
        body, out_shape=[jax.ShapeDtypeStruct(a.shape, a.dtype) for a in list(parts) + list(lands)],
        in_specs=[ANY] * (2 * nw) + [SEM, SEM, ANY], out_specs=[ANY] * (2 * nw),
        input_output_aliases={i: i for i in range(2 * nw)}, name=name,
        compiler_params=pltpu.CompilerParams(has_side_effects=DATAFLOW))(*parts, *lands, send, recv, after)
    return list(outs[:nw]), list(outs[nw:])


SIBLING_PAIR = 0


def _sibling_copy(src, dst, w, c, half_rows, send, recv, sib):
    if half_rows:
        h = src.shape[1] // 2
        src = src.at[:, pl.ds((1 - c) * h, h), :]
    return pltpu.make_async_remote_copy(src_ref=src, dst_ref=dst, send_sem=send.at[w], recv_sem=recv.at[w],
                                        device_id=sib, device_id_type=MESH)


def _landing(shape, dtype):
    return pltpu.with_memory_space_constraint(lax.empty(shape, dtype), pltpu.HBM)


def sibling_start(name, srcs, half_rows):
    nw = len(srcs)
    lands = [_landing((s.shape[0], s.shape[1] // 2, s.shape[2]) if half_rows else s.shape, s.dtype) for s in srcs]

    def body(*refs):
        outs = refs[2 * nw:]
        send, recv = outs[0], outs[1]
        src, dst, token = outs[2:2 + nw], outs[2 + nw:2 + 2 * nw], outs[2 + 2 * nw]
        x, y, c = _place()
        barrier = pltpu.get_barrier_semaphore()
        pl.semaphore_signal(barrier, inc=1, device_id=(x, y, 1 - c), device_id_type=MESH)
        pl.semaphore_wait(barrier, 1)
        for w in range(nw):
            _sibling_copy(src[w], dst[w], w, c, half_rows, send, recv, (x, y, 1 - c)).start()
        token[...] = jnp.zeros_like(token)

    sem = pltpu.SemaphoreType.DMA((nw,))
    outs = pl.pallas_call(
        body, out_shape=[sem, sem] + [jax.ShapeDtypeStruct(a.shape, a.dtype) for a in list(srcs) + lands] + [TOKEN],
        in_specs=[ANY] * (2 * nw), out_specs=[SEM, SEM] + [ANY] * (2 * nw) + [VM],
        input_output_aliases={i: 2 + i for i in range(2 * nw)}, name=name,
        compiler_params=pltpu.CompilerParams(has_side_effects=DATAFLOW, collective_id=SIBLING_PAIR))(*srcs, *lands)
    return outs[0], outs[1], list(outs[2:2 + nw]), list(outs[2 + nw:2 + 2 * nw]), outs[2 + 2 * nw]


def sibling_wait(name, srcs, lands, send, recv, half_rows, after):
    nw = len(srcs)

    def body(*refs):
        src, dst = refs[:nw], refs[nw:2 * nw]
        send_ref, recv_ref = refs[2 * nw], refs[2 * nw + 1]
        x, y, c = _place()
        for w in range(nw):
            cp = _sibling_copy(src[w], dst[w], w, c, half_rows, send_ref, recv_ref, (x, y, 1 - c))
            cp.wait_send()
            cp.wait_recv()

    outs = pl.pallas_call(
        body, out_shape=[jax.ShapeDtypeStruct(a.shape, a.dtype) for a in list(srcs) + list(lands)],
        in_specs=[ANY] * (2 * nw) + [SEM, SEM, ANY], out_specs=[ANY] * (2 * nw),
        input_output_aliases={i: i for i in range(2 * nw)}, name=name,
        compiler_params=pltpu.CompilerParams(has_side_effects=DATAFLOW))(*srcs, *lands, send, recv, after)
    return list(outs[:nw]), list(outs[nw:])


def _peer(x, y, c, r):
    return (1 - x if r & 4 else x, 1 - y if r & 2 else y, 1 - c if r & 1 else c)


def _everyone_copy(buf, r, x, y, c, send, recv, landing):
    px, py, pc = _peer(x, y, c, r)
    slot = buf.at[4 * px + 2 * py + pc] if landing else buf.at[4 * x + 2 * y + c]
    return pltpu.make_async_remote_copy(src_ref=slot, dst_ref=slot, send_sem=send.at[r - 1], recv_sem=recv.at[r - 1],
                                        device_id=(px, py, pc), device_id_type=MESH)


def everyone_start(name, buf):
    def body(buf_in, send, recv, buf_ref, token):
        x, y, c = _place()
        for r in range(1, N_DEV):
            _everyone_copy(buf_ref, r, x, y, c, send, recv, False).start()
        token[...] = jnp.zeros_like(token)

    sem = pltpu.SemaphoreType.DMA((N_DEV - 1,))
    return pl.pallas_call(
        body, out_shape=[sem, sem, jax.ShapeDtypeStruct(buf.shape, buf.dtype), TOKEN],
        in_specs=[ANY], out_specs=[SEM, SEM, ANY, VM], input_output_aliases={0: 2}, name=name,
        compiler_params=pltpu.CompilerParams(has_side_effects=DATAFLOW))(buf)


def everyone_wait(name, buf, send, recv, after):
    def body(buf_ref, send_ref, recv_ref, after_ref, out_ref):
        x, y, c = _place()
        for r in range(1, N_DEV):
            _everyone_copy(buf_ref, r, x, y, c, send_ref, recv_ref, False).wait_send()
            _everyone_copy(buf_ref, r, x, y, c, send_ref, recv_ref, True).wait_recv()

    return pl.pallas_call(
        body, out_shape=jax.ShapeDtypeStruct(buf.shape, buf.dtype), in_specs=[ANY, SEM, SEM, ANY], out_specs=ANY,
        input_output_aliases={0: 0}, name=name,
        compiler_params=pltpu.CompilerParams(has_side_effects=DATAFLOW))(buf, send, recv, after)


def local_step(x, pos, tgt, small, d_in, get_w, put_g, first_dep=None, tick=lambda name, after: None):
    L, D = x.shape
    d_kv = N_KV_HEADS * HEAD_DIM
    d_ssm = small["d_skip"].shape[1]
    d_attn = d_in - 2 * d_kv - d_ssm
    big = {}
    G = d_ssm // SSM_GROUP
    N, P = SSM_STATE, SSM_GROUP
    gbf = bf16

    half_dim = HEAD_DIM // 2
    inv_freq = ROPE_THETA ** (-jnp.arange(half_dim, dtype=f32) / half_dim)
    inv_freq = jnp.tile(inv_freq, LANES // half_dim).reshape(1, LANES)
    sink_b = jnp.broadcast_to(small["sinks"].reshape(-1, 1), (small["sinks"].size, LANES))

    spread = jnp.repeat(jnp.eye(G, dtype=f32), P, axis=0)
    logdt_b = jnp.broadcast_to(small["log_dt"].reshape(G, 1), (G, N))
    bt_re = small["b_re"].reshape(G, N, P).transpose(0, 2, 1).reshape(G * P, N)
    bt_im = small["b_im"].reshape(G, N, P).transpose(0, 2, 1).reshape(G * P, N)
    a_re, a_im = small["a_re"].reshape(G, N), small["a_im"].reshape(G, N)
    lam_re, lam_im, bbt_re, bbt_im = ssm_params(a_re, a_im, logdt_b, bt_re, bt_im, spread)
    bd_re = _block_diag(bbt_re.reshape(G, P, N), P, N).astype(bf16)
    bd_im = _block_diag(bbt_im.reshape(G, P, N), P, N).astype(bf16)
    c_re = small["c_re"].reshape(G, P, N).transpose(0, 2, 1)
    c_im = small["c_im"].reshape(G, P, N).transpose(0, 2, 1)
    cd_re = _block_diag(c_re, N, P).astype(bf16)
    cd_im = _block_diag(c_im, N, P).astype(bf16)
    lam_re_l, lam_im_l = _state_layout(lam_re), _state_layout(lam_im)

    def k1(i, nt, xt, g):
        return (rms_fwd(xt, g),)
    xn = rowwise("pre_mix_norm", k1, L, [full(x)], [small["g_pre_mix"]], [(D, bf16)], dep=first_dep)[0]
    big["w_in"] = get_w("w_in", xn)
    proj = mm_nn("proj_in", xn, big["w_in"])
    qr, kk, vv, u_bf = qkv_prep(proj, pos, inv_freq, d_attn, d_kv)
    attn = attn_fwd(qr, kk, vv, sink_b)
    u_cb = (d_attn + 2 * d_kv) // (d_ssm // 2)
    token = tick("attn", attn)
    y, z_bf, s_re, s_im = ssm_fwd(u_bf, proj, u_cb, bd_re, bd_im, cd_re, cd_im, lam_re_l, lam_im_l, small["d_skip"], dep=token)
    token = tick("ssm", z_bf)
    big["w_glu"] = get_w("w_glu", z_bf)
    gl = mm_nn("glu_proj", z_bf, big["w_glu"], dep=token)

    def k6(i, nt, at, yt, glt, bg, ga, gs):
        ssm = gelu(yt) * sigmoid(glt + bg)
        return (jnp.concatenate([rms_fwd(at, ga), rms_fwd(ssm, gs)], axis=1),)
    mixed = rowwise("mix_norms", k6, L, [full(attn), full(y), full(gl)],
                    [small["b_glu"], small["g_attn_out"], small["g_ssm_out"]], [(d_attn + d_ssm, bf16)])[0]
    big["w_o"] = get_w("w_o", mixed)
    mix = mm_nn("proj_out", mixed, big["w_o"])

    def k7(i, nt, xt, mt, gpm, gpf):
        h = xt + rms_fwd(mt, gpm)
        return h, rms_fwd(h, gpf)
    h, hn = rowwise("post_mix", k7, L, [full(x), full(mix)], [small["g_post_mix"], small["g_pre_ffn"]], [(D, f32), (D, bf16)])
    big["w_gate"] = get_w("w_gate", hn)
    big["w_up"] = get_w("w_up", hn)
    gt, up, hid = ffn_hidden(hn, big["w_gate"], big["w_up"])
    d_ff_dim = gt.shape[1]
    big["w_down"] = get_w("w_down", hid)
    ff = mm_nn("ffn_down", hid, big["w_down"], tk=d_ff_dim // 2)

    def k9(i, nt, ht, fft, tt, g):
        out = ht + rms_fwd(fft, g)
        err = out - tt
        per_row = jnp.mean(err * err, axis=-1, keepdims=True)
        loss = 0.5 * jnp.sum(per_row) * jnp.where(_lane((1, LANES)) == 0, 1.0, 0.0)
        d_out = err * (1.0 / D)
        d_ff, dg = rms_bwd(fft, g, d_out)
        return d_out, d_ff, dg, loss
    d_out, d_ff, dg_post_ffn, loss = rowwise("loss_head", k9, L, [full(h), full(ff), full(tgt)], [small["g_post_ffn"]],
                                             [(D, f32), (D, bf16)], reds=[D, LANES])

    d_gt, d_up = ffn_hidden_grad(d_ff, big["w_down"], gt, up)
    token = put_g("w_down", mm_tn("dw_down", hid, d_ff, out_dtype=gbf, tm=d_ff_dim // N_CHIPS))
    d_hn = mm_nt_pair("d_hn", d_gt, big["w_gate"], d_up, big["w_up"], dep=token)
    token = put_g("w_gate", mm_tn("dw_gate", hn, d_gt, shards=N_CHIPS, out_dtype=gbf))
    token = put_g("w_up", mm_tn("dw_up", hn, d_up, shards=N_CHIPS, out_dtype=gbf, dep=token))

    def k11(i, nt, ht, da, do, mt, gpf, gpm):
        dh_n, dg_pf = rms_bwd(ht, gpf, da)
        dh = do + dh_n
        d_mix, dg_pm = rms_bwd(mt, gpm, dh)
        return dh, d_mix, dg_pf, dg_pm
    dh, d_mix, dg_pre_ffn, dg_post_mix = rowwise("post_mix_grad", k11, L, [full(h), full(d_hn), full(d_out), full(mix)],
                                                 [small["g_pre_ffn"], small["g_post_mix"]], [(D, f32), (D, bf16)], reds=[D, D], dep=token)
    d_mixed = mm_nt("d_mixed", d_mix, big["w_o"])
    token = put_g("w_o", mm_tn("dw_o", mixed, d_mix, out_dtype=gbf))

    def k12(i, nt, at, yt, glt, da_n, ds_n, bg, ga, gs):
        z = gelu(yt)
        sg = sigmoid(glt + bg)
        ssm = z * sg
        d_at, dga = rms_bwd(at, ga, da_n)
        d_ssm_t, dgs = rms_bwd(ssm, gs, ds_n)
        d_gl = d_ssm_t * z * sg * (1.0 - sg)
        return d_at, d_ssm_t * sg, d_gl, dga, dgs, colsum(d_gl)
    d_attn_o, dz1, d_gl, dg_attn, dg_ssm, db_glu = rowwise(
        "mix_norms_grad", k12, L, [full(attn), full(y), full(gl), (d_mixed, d_attn, 0, 0), (d_mixed, d_ssm, d_attn // d_ssm, 0)],
        [small["b_glu"], small["g_attn_out"], small["g_ssm_out"]], [(d_attn, f32), (d_ssm, f32), (d_ssm, bf16)],
        reds=[d_attn, d_ssm, d_ssm], dep=token)
    dz2 = mm_nt("d_glu_in", d_gl, big["w_glu"])
    token = put_g("w_glu", mm_tn("dw_glu", z_bf, d_gl, out_dtype=gbf))

    du, dbd_re, dbd_im, dcd_re, dcd_im, dlam_re_l, dlam_im_l, dd_skip = ssm_bwd(
        y, dz1, dz2, u_bf, proj, u_cb, s_re, s_im, bd_re, bd_im, cd_re, cd_im, lam_re_l, lam_im_l, small["d_skip"], dep=token)
    dq, dkk_c, dkk_p, dvv_c, dvv_p, dsink = attn_bwd(qr, kk, vv, sink_b, attn, d_attn_o)
    d_proj = qkv_grad(dq, dkk_c, dkk_p, dvv_c, dvv_p, du, pos, inv_freq)
    d_xn = mm_nt("d_xn", d_proj, big["w_in"])
    token = put_g("w_in", mm_tn("dw_in", xn, d_proj, shards=N_CHIPS, out_dtype=gbf))

    def k17(i, nt, xt, dxn, dht, g):
        dx, dg = rms_bwd(xt, g, dxn)
        return dht + dx, dg
    grad_x, dg_pre_mix = rowwise("pre_mix_grad", k17, L, [full(x), full(d_xn), full(dh)], [small["g_pre_mix"]],
                                 [(D, f32)], reds=[D], dep=token)

    gather = spread.T
    dbbt_re = dbd_re.reshape(G * P, LANES)[:, :N]
    dbbt_im = dbd_im.reshape(G * P, LANES)[:, :N]
    dc_re = dcd_re.reshape(G, N, LANES)[:, :, :P].transpose(0, 2, 1)
    dc_im = dcd_im.reshape(G, N, LANES)[:, :, :P].transpose(0, 2, 1)
    d_a_re, d_a_im, d_logdt, dbt_re, dbt_im = ssm_params_grad(
        a_re, a_im, logdt_b, bt_re, bt_im, spread, gather,
        _state_layout_inv(dlam_re_l, G, N), _state_layout_inv(dlam_im_l, G, N), dbbt_re, dbbt_im)
    q_per_kv = d_attn // HEAD_DIM // N_KV_HEADS
    small_grads = {
        "g_pre_mix": dg_pre_mix, "sinks": dsink[:, :q_per_kv, 0].reshape(1, -1),
        "a_re": d_a_re, "a_im": d_a_im, "log_dt": d_logdt.reshape(1, G),
        "b_re": dbt_re, "b_im": dbt_im,
        "c_re": dc_re, "c_im": dc_im,
        "d_skip": dd_skip, "b_glu": db_glu, "g_attn_out": dg_attn, "g_ssm_out": dg_ssm,
        "g_post_mix": dg_post_mix, "g_pre_ffn": dg_pre_ffn, "g_post_ffn": dg_post_ffn,
    }
    return loss, grad_x, small_grads


WEIGHTS = ['g_pre_mix', 'w_in', 'sinks', 'a_re', 'a_im', 'log_dt', 'b_re', 'b_im', 'c_re', 'c_im', 'd_skip', 'w_glu', 'b_glu',
           'g_attn_out', 'g_ssm_out', 'w_o', 'g_post_mix', 'g_pre_ffn', 'w_gate', 'w_up', 'w_down', 'g_post_ffn']
BIG = ['w_in', 'w_glu', 'w_o', 'w_gate', 'w_up', 'w_down']
COL_SHARDED = ['w_in', 'w_gate', 'w_up']
SMALL = [n for n in WEIGHTS if n not in BIG]
GATHER_GROUPS = [["w_in"], ["w_glu", "w_o"], ["w_gate", "w_up"], ["w_down"]]
REDUCE_GROUPS = [["w_down", "w_gate", "w_up"], ["w_o", "w_glu", "w_in"]]


PACK_ROWS = 256


def _pack(parts):
    flat = jnp.concatenate([p.reshape(-1) for p in parts])
    pad = (-flat.size) % (PACK_ROWS * LANES)
    return jnp.pad(flat, (0, pad)).reshape(-1, LANES)


TRANSPOSED_VIEW = ("b_re", "b_im")


def small_view(name, a):
    if name in TRANSPOSED_VIEW:
        a = a.transpose(0, 1, 3, 2)
    return a.reshape(-1, a.shape[-1])


def small_unview(name, p, shape):
    if name in TRANSPOSED_VIEW:
        return p.reshape(shape[0], shape[1], shape[3], shape[2]).transpose(0, 1, 3, 2)
    return p.reshape(shape)


def _unpack(packed, shapes):
    flat = packed.reshape(-1)
    out, off = [], 0
    for s in shapes:
        n = int(np.prod(s))
        out.append(flat[off:off + n].reshape(s))
        off += n
    return out


def kernel(x, positions, g_pre_mix, w_in, sinks, a_re, a_im, log_dt, b_re, b_im, c_re, c_im, d_skip, w_glu, b_glu, g_attn_out, g_ssm_out, w_o, g_post_mix, g_pre_ffn, w_gate, w_up, w_down, g_post_ffn, loss_target, m_g_pre_mix, m_w_in, m_sinks, m_a_re, m_a_im, m_log_dt, m_b_re, m_b_im, m_c_re, m_c_im, m_d_skip, m_w_glu, m_b_glu, m_g_attn_out, m_g_ssm_out, m_w_o, m_g_post_mix, m_g_pre_ffn, m_w_gate, m_w_up, m_w_down, m_g_post_ffn, v_g_pre_mix, v_w_in, v_sinks, v_a_re, v_a_im, v_log_dt, v_b_re, v_b_im, v_c_re, v_c_im, v_d_skip, v_w_glu, v_b_glu, v_g_attn_out, v_g_ssm_out, v_w_o, v_g_post_mix, v_g_pre_ffn, v_w_gate, v_w_up, v_w_down, v_g_post_ffn):
    args = dict(locals())
    w = {n: args[n] for n in WEIGHTS}
    m = {n: args["m_" + n] for n in WEIGHTS}
    v = {n: args["v_" + n] for n in WEIGHTS}
    L, D = x.shape[1], x.shape[2]

    ax, ay, ac = _place()
    mine_arr = (2 * ax + ay).astype(jnp.int32).reshape(1)
    c_arr = ac.astype(jnp.int32).reshape(1)

    me_arr = (4 * ax + 2 * ay + ac).astype(jnp.int32).reshape(1)

    bufs = {"w_in": into_slot("cast_w_in", w["w_in"][0], mine_arr, N_CHIPS, bf16)}
    (first_sems,), (bufs["w_in"],), token = gather_start("gather_start_in", [bufs["w_in"]], [[0]], mine_arr)
    sems = {"w_in": first_sems}
    for n in BIG[1:]:
        bufs[n] = into_slot("cast_" + n, w[n][0], mine_arr, N_CHIPS, bf16, dep=token)
    first = ["w_glu", "w_o", "w_gate", "w_up"]
    (sems["w_glu"], sems["w_gate"], sems["w_up"]), started, token = gather_start(
        "gather_start_rest", [bufs[n] for n in first], [[0, 1], [2], [3]], token, neighbours_only=(2, 3))
    bufs.update(zip(first, started))
    relays, ready = {}, set()

    def tick(name, after):
        n, more = ("w_gate", []) if name == "attn" else ("w_up", ["w_down"])
        r_send, r_recv, sems["w_down"], (bufs[n],), down, tok = gather_relay(
            "gather_relay_" + n, [bufs[n]], [sems[n]], [bufs[k] for k in more], after)
        bufs.update(zip(more, down))
        relays[n] = (r_send, r_recv)
        return tok

    def get_w(n, after):
        if n not in ready:
            members = [g for g in GATHER_GROUPS if n in g][0]
            if members[0] in relays:
                landed = [gather_wait_relay("gather_wait_" + k, [bufs[k]], *relays[k], after)[0] for k in members]
                which = (2,)
            else:
                landed = gather_wait("gather_wait_" + members[0], [bufs[k] for k in members], *sems[members[0]], after)
                which = (0, 1, 2)
            bufs.update(zip(members, gather_forward("gather_forward_" + members[0], landed, which)))
            ready.update(members)
        g = bufs[n]
        return g if n in COL_SHARDED else g.reshape(g.shape[0] * g.shape[1], g.shape[2])

    swaps, inflight = {}, []

    def put_g(n, g):
        g3 = g if n in COL_SHARDED else g.reshape(N_CHIPS, g.shape[0] // N_CHIPS, g.shape[1])
        swaps[n] = sibling_start("swap_start_" + n, [g3], True)
        for gi, members in enumerate(REDUCE_GROUPS):
            if n == members[-1]:
                last = swaps[n][4]
                pair = []
                for k in members:
                    send, recv, srcs, lands, _ = swaps[k]
                    (src,), (got,) = sibling_wait("swap_wait_" + k, srcs, lands, send, recv, True, last)
                    pair.append(pair_sum("pair_sum_" + k, src, got, c_arr))
                send, recv, parts, lands, tok = scatter_start("scatter_start_%d" % gi, pair)
                inflight.append((members, send, recv, parts, lands))
                return tok
        return swaps[n][4]

    small = {n: w[n].reshape(1, -1) for n in SMALL}
    pos = positions.reshape(L, 1).astype(f32)
    d_in = N_CHIPS * w["w_in"].shape[2]
    loss, grad_x, small_grads = local_step(x[0], pos, loss_target[0], small, d_in, get_w, put_g, first_dep=token, tick=tick)

    shapes = [w[n].shape for n in SMALL]
    blocks = into_slot("small_block", _pack([small_grads[n] for n in SMALL] + [loss]), me_arr, N_DEV, f32)
    small_send, small_recv, blocks, after = everyone_start("small_start", blocks)

    grads, delta, new_m, new_v = {}, {}, {}, {}
    for gi, (members, send, recv, parts, lands) in enumerate(inflight):
        parts, landed = scatter_wait("scatter_wait_%d" % gi, parts, lands, send, recv, after)
        joins, dep = [], None
        for k, p, t in zip(members, parts, landed):
            joins.append(sibling_start("join_start_" + k, [chip_sum("chip_sum_" + k, p, t, mine_arr, dep=dep)], False))
            dep = after = joins[-1][4]
        for n, (send, recv, srcs, lands, _) in zip(members, joins):
            (own,), (sib,) = sibling_wait("join_wait_" + n, srcs, lands, send, recv, False, after)
            g_, d_, m_, v_ = adamw_halves("adamw_" + n, w[n][0], own, sib, m[n][0], v[n][0], c_arr)
            grads[n], delta[n], new_m[n], new_v[n] = g_[None], d_[None], m_[None], v_[None]
            after = v_
    blocks = everyone_wait("small_wait", blocks, small_send, small_recv, after)
    small_sum = sum_slots("small_sum", blocks)
    *small_g, loss = _unpack(small_sum, [small_view(n, w[n]).shape for n in SMALL] + [loss.shape])
    loss = loss[0, 0]
    outs = adamw_many("adamw_small", [small_view(n, w[n]) for n in SMALL], small_g,
                      [small_view(n, m[n]) for n in SMALL], [small_view(n, v[n]) for n in SMALL])
    for t, parts in zip((grads, delta, new_m, new_v), (small_g,) + tuple(outs)):
        t.update({n: small_unview(n, p, w[n].shape) for n, p in zip(SMALL, parts)})

    return (loss, grad_x[None], *[grads[n] for n in WEIGHTS], *[delta[n] for n in WEIGHTS],
            *[new_m[n] for n in WEIGHTS], *[new_v[n] for n in WEIGHTS])
```

```python
import functools
import math

import jax
import jax.numpy as jnp
import numpy as np
from jax import lax
from jax.experimental import pallas as pl
from jax.experimental.pallas import tpu as pltpu

f32 = jnp.float32
bf16 = jnp.bfloat16
HIGHEST = lax.Precision.HIGHEST
MESH = pl.DeviceIdType.MESH

HEAD_DIM = 64
N_KV_HEADS = 4
ATTN_BLOCK = 128
ROPE_THETA = 10000.0
SSM_GROUP = 16
SSM_STATE = 64
RMS_EPS = 1e-6
LANES = 128
SUBLANES = 8
VMEM_LIMIT = 52 * 1024 * 1024
N_CHIPS = 4
N_DEV = 8
NEG = -1e30

ADAM_LR, ADAM_B1, ADAM_B2, ADAM_EPS, ADAM_WD, ADAM_STEP = 0.001, 0.9, 0.999, 1e-08, 0.01, 10

NN = (((1,), (0,)), ((), ()))
NT = (((1,), (1,)), ((), ()))
TN = (((0,), (0,)), ((), ()))


def _params(*sem):
    return pltpu.CompilerParams(dimension_semantics=sem or None, vmem_limit_bytes=VMEM_LIMIT)


def _dot(a, b, dims=NN):
    return lax.dot_general(a, b, dims, preferred_element_type=f32)


def _pick(dim, pref):
    t = min(dim, pref)
    while dim % t:
        t -= LANES
    assert t > 0, (dim, pref)
    return t


ANY = pl.BlockSpec(memory_space=pl.ANY)


def _with_dep(in_specs, operands, dep):
    if dep is None:
        return list(in_specs), list(operands), 0
    return list(in_specs) + [ANY], list(operands) + [dep], 1


def _mm_call(name, grid, in_specs, out_spec, out_shape, acc_shape, dims, operands, dep=None):
    nk = grid[2]
    in_specs, operands, n_dep = _with_dep(in_specs, operands, dep)

    def body_one(a_ref, b_ref, *rest):
        o_ref = rest[n_dep]
        o_ref[...] = _dot(a_ref[...], b_ref[...], dims).astype(o_ref.dtype)

    def body(a_ref, b_ref, *rest):
        o_ref, acc_ref = rest[n_dep], rest[n_dep + 1]
        k = pl.program_id(2)

        @pl.when(k == 0)
        def _():
            acc_ref[...] = _dot(a_ref[...], b_ref[...], dims)

        @pl.when((k > 0) & (k < nk - 1))
        def _():
            acc_ref[...] += _dot(a_ref[...], b_ref[...], dims)

        @pl.when(k == nk - 1)
        def _():
            o_ref[...] = (acc_ref[...] + _dot(a_ref[...], b_ref[...], dims)).astype(o_ref.dtype)

    return pl.pallas_call(
        body_one if nk == 1 else body, out_shape=out_shape, grid=grid, in_specs=in_specs, out_specs=out_spec,
        scratch_shapes=[] if nk == 1 else [pltpu.VMEM(acc_shape, f32)], name=name,
        compiler_params=_params("parallel", "parallel", "arbitrary"))(*operands)


def mm_nt_pair(name, a1, b1, a2, b2, tm=1024, tk=1024, dep=None):
    M = a1.shape[0]
    S, K, n = b1.shape
    tm, tko = _pick(M, tm), _pick(K, tk)
    nk = 2 * S

    def body(a1_ref, b1_ref, a2_ref, b2_ref, *rest):
        o_ref, acc_ref = rest[-2], rest[-1]
        k = pl.program_id(2)

        @pl.when(k == 0)
        def _():
            acc_ref[...] = _dot(a1_ref[...], b1_ref[...], NT)

        @pl.when((k > 0) & (k < S))
        def _():
            acc_ref[...] += _dot(a1_ref[...], b1_ref[...], NT)

        @pl.when((k >= S) & (k < nk - 1))
        def _():
            acc_ref[...] += _dot(a2_ref[...], b2_ref[...], NT)

        @pl.when(k == nk - 1)
        def _():
            o_ref[...] = acc_ref[...] + _dot(a2_ref[...], b2_ref[...], NT)

    first = lambda k: jnp.minimum(k, S - 1)
    second = lambda k: jnp.maximum(k - S, 0)
    in_specs = [pl.BlockSpec((tm, n), lambda i, j, k: (i, first(k))), pl.BlockSpec((None, tko, n), lambda i, j, k: (first(k), j, 0)),
                pl.BlockSpec((tm, n), lambda i, j, k: (i, second(k))), pl.BlockSpec((None, tko, n), lambda i, j, k: (second(k), j, 0))]
    in_specs, operands, _ = _with_dep(in_specs, (a1, b1, a2, b2), dep)
    return pl.pallas_call(
        body, out_shape=jax.ShapeDtypeStruct((M, K), f32), grid=(M // tm, K // tko, nk), in_specs=in_specs,
        out_specs=pl.BlockSpec((tm, tko), lambda i, j, k: (i, j)), scratch_shapes=[pltpu.VMEM((tm, tko), f32)], name=name,
        compiler_params=_params("parallel", "parallel", "arbitrary"))(*operands)


def mm_nn(name, a, b, out_dtype=f32, tm=1024, tn=1024, tk=2048, dep=None):
    M, K = a.shape
    tm, tk = _pick(M, tm), _pick(K, tk)
    if b.ndim == 3:
        S, _, n = b.shape
        tn = _pick(n, 2048)
        per = n // tn
        b_spec = pl.BlockSpec((None, tk, tn), lambda i, j, k: (j // per, k, j % per))
        N = S * n
    else:
        N = b.shape[1]
        tn = _pick(N, tn)
        b_spec = pl.BlockSpec((tk, tn), lambda i, j, k: (k, j))
    grid = (M // tm, N // tn, K // tk)
    return _mm_call(name, grid, [pl.BlockSpec((tm, tk), lambda i, j, k: (i, k)), b_spec],
                    pl.BlockSpec((tm, tn), lambda i, j, k: (i, j)), jax.ShapeDtypeStruct((M, N), out_dtype),
                    (tm, tn), NN, (a, b), dep)


def mm_nt(name, a, b, out_dtype=f32, tm=1024, tn=2048, tk=1024, dep=None):
    M, N = a.shape
    tm = _pick(M, tm)
    if b.ndim == 3:
        S, K, n = b.shape
        tr = _pick(n, 2048)
        per = n // tr
        tko = _pick(K, tk)
        b_spec = pl.BlockSpec((None, tko, tr), lambda i, j, k: (k // per, j, k % per))
    else:
        K = b.shape[0]
        tr = _pick(N, tn)
        tko = _pick(K, tk)
        b_spec = pl.BlockSpec((tko, tr), lambda i, j, k: (j, k))
    grid = (M // tm, K // tko, N // tr)
    return _mm_call(name, grid, [pl.BlockSpec((tm, tr), lambda i, j, k: (i, k)), b_spec],
                    pl.BlockSpec((tm, tko), lambda i, j, k: (i, j)), jax.ShapeDtypeStruct((M, K), out_dtype),
                    (tm, tko), NT, (a, b), dep)


def mm_tn(name, a, b, shards=None, out_dtype=f32, tm=1024, tn=1024, tl=2048, dep=None):
    L, K = a.shape
    N = b.shape[1]
    tl, tko = _pick(L, tl), _pick(K, tm)
    if shards:
        n = N // shards
        tn = _pick(n, 2048)
        per = n // tn
        o_spec = pl.BlockSpec((None, tko, tn), lambda i, j, k: (j // per, i, j % per))
        o_shape = jax.ShapeDtypeStruct((shards, K, n), out_dtype)
    else:
        tn = _pick(N, tn)
        o_spec = pl.BlockSpec((tko, tn), lambda i, j, k: (i, j))
        o_shape = jax.ShapeDtypeStruct((K, N), out_dtype)
    grid = (K // tko, N // tn, L // tl)
    return _mm_call(name, grid, [pl.BlockSpec((tl, tko), lambda i, j, k: (k, i)),
                                 pl.BlockSpec((tl, tn), lambda i, j, k: (k, j))],
                    o_spec, o_shape, (tko, tn), TN, (a, b), dep)


def ffn_hidden(hn, w_gate, w_up, tm=512):
    M, K = hn.shape
    S, _, n = w_gate.shape
    tm = _pick(M, tm)

    def body(a_ref, g_ref, u_ref, gt_ref, up_ref, hid_ref):
        a = a_ref[...]
        g = _dot(a, g_ref[...])
        u = _dot(a, u_ref[...])
        gt_ref[...] = g.astype(bf16)
        up_ref[...] = u.astype(bf16)
        hid_ref[...] = (g * sigmoid(g) * u).astype(bf16)

    w_spec = pl.BlockSpec((None, K, n), lambda s, i: (s, 0, 0))
    o_spec = pl.BlockSpec((tm, n), lambda s, i: (i, s))
    o = jax.ShapeDtypeStruct((M, S * n), bf16)
    return pl.pallas_call(
        body, out_shape=[o, o, o], grid=(S, M // tm), in_specs=[pl.BlockSpec((tm, K), lambda s, i: (i, 0)), w_spec, w_spec],
        out_specs=[o_spec, o_spec, o_spec], name="ffn_hidden", compiler_params=_params("parallel", "parallel"))(hn, w_gate, w_up)


def ffn_hidden_grad(d_ff, w_down, gt, up, tm=512):
    M, D = d_ff.shape
    F = w_down.shape[0]
    n = _pick(F // N_CHIPS, 2048)
    tm = _pick(M, tm)

    def body(a_ref, b_ref, gt_ref, up_ref, dg_ref, du_ref):
        dh = _dot(a_ref[...], b_ref[...], NT)
        g = gt_ref[...].astype(f32)
        sg = sigmoid(g)
        dg_ref[...] = (dh * up_ref[...].astype(f32) * (sg * (1.0 + g * (1.0 - sg)))).astype(bf16)
        du_ref[...] = (dh * (g * sg)).astype(bf16)

    t_spec = pl.BlockSpec((tm, n), lambda j, i: (i, j))
    o = jax.ShapeDtypeStruct((M, F), bf16)
    return pl.pallas_call(
        body, out_shape=[o, o], grid=(F // n, M // tm),
        in_specs=[pl.BlockSpec((tm, D), lambda j, i: (i, 0)), pl.BlockSpec((n, D), lambda j, i: (j, 0)), t_spec, t_spec],
        out_specs=[t_spec, t_spec], name="ffn_hidden_grad", compiler_params=_params("parallel", "parallel"))(d_ff, w_down, gt, up)


def rowwise(name, fn, L, rows, bcast, outs, reds=(), tr=256, dep=None):
    tr = min(tr, L)
    nt = L // tr
    n_rows, n_b, n_o = len(rows), len(bcast), len(outs)
    n_dep = 0 if dep is None else 1

    def body(*refs):
        i = pl.program_id(0)
        ins = [r[...] for r in refs[:n_rows + n_b]]
        res = fn(i, nt, *ins)
        o_refs = refs[n_rows + n_b + n_dep:]
        for k in range(n_o):
            o_refs[k][...] = res[k].astype(o_refs[k].dtype)
        if reds:
            @pl.when(i == 0)
            def _():
                for k in range(len(reds)):
                    o_refs[n_o + k][...] = jnp.zeros_like(o_refs[n_o + k])
            for k in range(len(reds)):
                o_refs[n_o + k][...] += res[n_o + k]

    def row_spec(width, cb, shift):
        if shift:
            return pl.BlockSpec((tr, width), lambda i: (jnp.minimum(i + shift, nt - 1), cb))
        return pl.BlockSpec((tr, width), lambda i: (i, cb))

    in_specs = [row_spec(w, cb, sh) for (_, w, cb, sh) in rows]
    in_specs += [pl.BlockSpec(b.shape, lambda i: (0, 0)) for b in bcast]
    out_specs = [pl.BlockSpec((tr, w), lambda i: (i, 0)) for (w, _) in outs]
    out_specs += [pl.BlockSpec((1, w), lambda i: (0, 0)) for w in reds]
    out_shape = [jax.ShapeDtypeStruct((L, w), dt) for (w, dt) in outs]
    out_shape += [jax.ShapeDtypeStruct((1, w), f32) for w in reds]
    in_specs, operands, _ = _with_dep(in_specs, [r[0] for r in rows] + list(bcast), dep)
    return pl.pallas_call(
        body, out_shape=out_shape, grid=(nt,), in_specs=in_specs, out_specs=out_specs, name=name,
        compiler_params=_params("arbitrary"))(*operands)


def full(a):
    return (a, a.shape[1], 0, 0)


def colsum(v):
    return jnp.sum(v, axis=0, keepdims=True)


def rms_fwd(x, g):
    r = lax.rsqrt(jnp.mean(x * x, axis=-1, keepdims=True) + RMS_EPS)
    return x * r * g


def rms_bwd(x, g, dy):
    r = lax.rsqrt(jnp.mean(x * x, axis=-1, keepdims=True) + RMS_EPS)
    xh = x * r
    dyg = dy * g
    dx = r * (dyg - xh * jnp.mean(dyg * xh, axis=-1, keepdims=True))
    return dx, colsum(dy * xh)


GELU_C = math.sqrt(2.0 / math.pi)


def gelu(y):
    return y * (0.5 * (1.0 + jnp.tanh(GELU_C * (y + 0.044715 * (y * y * y)))))


def gelu_grad(y):
    t = jnp.tanh(GELU_C * (y + 0.044715 * (y * y * y)))
    return 0.5 * (1.0 + t) + 0.5 * y * (1.0 - t * t) * (GELU_C * (1.0 + 3 * 0.044715 * (y * y)))


def sigmoid(v):
    return 1.0 / (1.0 + jnp.exp(-v))


def _lane(shape):
    return lax.broadcasted_iota(jnp.int32, shape, 1)


def _rot_chunk(t, cos, sin_signed):
    first = (_lane(t.shape) % HEAD_DIM) < (HEAD_DIM // 2)
    partner = jnp.where(first, pltpu.roll(t, LANES - HEAD_DIM // 2, 1), pltpu.roll(t, HEAD_DIM // 2, 1))
    return t * cos + partner * sin_signed


def _cos_sin(pos, inv_freq, inverse):
    ang = pos * inv_freq
    cos, sin = jnp.cos(ang), jnp.sin(ang)
    first = (_lane(ang.shape) % HEAD_DIM) < (HEAD_DIM // 2)
    sign = jnp.where(first, -1.0, 1.0) * (-1.0 if inverse else 1.0)
    return cos, sin * sign


def _dup_head(chunk, odd):
    low = _lane(chunk.shape) < HEAD_DIM
    x = jnp.where(low != odd, chunk, 0.0)
    return x + pltpu.roll(x, HEAD_DIM, 1)


def _chunks(v):
    return [v[:, LANES * c:LANES * (c + 1)] for c in range(v.shape[1] // LANES)]


def qkv_prep(proj, pos, inv_freq, d_attn, d_kv):
    L = proj.shape[0]
    d_ssm = proj.shape[1] - d_attn - 2 * d_kv
    half = d_ssm // 2
    scale = 1.0 / math.sqrt(HEAD_DIM)

    def fn(i, nt, q, k, v, u0, u1, p, invf):
        cos, sin = _cos_sin(p, invf, False)
        qr = jnp.concatenate([_rot_chunk(c, cos, sin) for c in _chunks(q)], axis=1) * scale
        kr = [_rot_chunk(c, cos, sin) for c in _chunks(k)]
        kk = jnp.concatenate([_dup_head(c, odd) for c in kr for odd in (False, True)], axis=1)
        vv = jnp.concatenate([_dup_head(c, odd) for c in _chunks(v) for odd in (False, True)], axis=1)
        return qr, kk, vv, jnp.concatenate([u0, u1], axis=1)

    u_cb = (d_attn + 2 * d_kv) // half
    return rowwise("qkv_prep", fn, L,
                   [(proj, d_attn, 0, 0), (proj, d_kv, d_attn // d_kv, 0), (proj, d_kv, d_attn // d_kv + 1, 0),
                    (proj, half, u_cb, 0), (proj, half, u_cb + 1, 0), full(pos)],
                   [inv_freq], [(d_attn, bf16), (2 * d_kv, bf16), (2 * d_kv, bf16), (d_ssm, bf16)])


def qkv_grad(dq, dkk_c, dkk_p, dvv_c, dvv_p, du, pos, inv_freq):
    L, d_attn = dq.shape
    d_kv = dkk_c.shape[1] // 2
    scale = 1.0 / math.sqrt(HEAD_DIM)

    def fold(cur, prev, i, nt):
        t = cur + jnp.where(i < nt - 1, prev, 0.0)
        out = []
        for c in range(t.shape[1] // (2 * LANES)):
            even, odd = t[:, 2 * c * LANES:(2 * c + 1) * LANES], t[:, (2 * c + 1) * LANES:(2 * c + 2) * LANES]
            even, odd = even + pltpu.roll(even, HEAD_DIM, 1), odd + pltpu.roll(odd, HEAD_DIM, 1)
            out.append(jnp.where(_lane(even.shape) < HEAD_DIM, even, odd))
        return out

    def fn(i, nt, dq_t, kc, kp, vc, vp, du_t, p, invf):
        cos, sin = _cos_sin(p, invf, True)
        dq_o = jnp.concatenate([_rot_chunk(c, cos, sin) for c in _chunks(dq_t)], axis=1) * scale
        dk_o = jnp.concatenate([_rot_chunk(c, cos, sin) for c in fold(kc, kp, i, nt)], axis=1)
        dv_o = jnp.concatenate(fold(vc, vp, i, nt), axis=1)
        return (jnp.concatenate([dq_o, dk_o, dv_o, du_t], axis=1),)

    return rowwise("qkv_grad", fn, L,
                   [full(dq), full(dkk_c), (dkk_p, 2 * d_kv, 0, 1), full(dvv_c), (dvv_p, 2 * d_kv, 0, 1), full(du), full(pos)],
                   [inv_freq], [(d_attn + 2 * d_kv + du.shape[1], bf16)], tr=ATTN_BLOCK)[0]


def _attn_specs(L):
    nb = L // ATTN_BLOCK
    B = ATTN_BLOCK
    q_spec = lambda width: pl.BlockSpec((B, width), lambda n: (n, 0))
    prev = lambda width: pl.BlockSpec((B, width), lambda n: (jnp.maximum(n - 1, 0), 0))
    return nb, q_spec, prev


def _attn_mask(n):
    B = ATTN_BLOCK
    row = lax.broadcasted_iota(jnp.int32, (B, 2 * B), 0)
    col = lax.broadcasted_iota(jnp.int32, (B, 2 * B), 1)
    return ((col < B) & (col > row) & (n > 0)) | ((col >= B) & (row >= col - B))


def _attn_probs(qm, kcat, sink, mask):
    s = jnp.where(mask, _dot(qm, kcat, NT), NEG)
    m = jnp.maximum(jnp.max(s, axis=1, keepdims=True), sink)
    p, ps = jnp.exp(s - m), jnp.exp(sink - m)
    inv = 1.0 / (jnp.sum(p, axis=1, keepdims=True) + ps)
    return p, inv, ps


def _attn_heads(q_ref, s_ref, h, q_per_kv):
    low = _lane((ATTN_BLOCK, LANES)) < HEAD_DIM
    heads = []
    for pr in range(h * q_per_kv // 2, (h + 1) * q_per_kv // 2):
        q2 = q_ref[:, LANES * pr:LANES * (pr + 1)]
        for odd in (False, True):
            mine = low != odd
            sink = jnp.max(s_ref[2 * pr + int(odd):2 * pr + int(odd) + 1, :], axis=1, keepdims=True)
            heads.append((pr, mine, jnp.where(mine, q2, jnp.zeros_like(q2)), sink))
    return low, heads


def _kv_block(prev_ref, cur_ref, h):
    return jnp.concatenate([prev_ref[:, LANES * h:LANES * (h + 1)], cur_ref[:, LANES * h:LANES * (h + 1)]], axis=0)


def attn_fwd(qr, kk, vv, sink_b):
    L, d_attn = qr.shape
    nb, q_spec, prev = _attn_specs(L)
    d_kk = kk.shape[1]
    n_kv = d_kk // LANES
    q_per_kv = d_attn // HEAD_DIM // n_kv

    def body(q_ref, kc_ref, kp_ref, vc_ref, vp_ref, s_ref, o_ref):
        mask = _attn_mask(pl.program_id(0))
        for h in range(n_kv):
            kcat, vcat = _kv_block(kp_ref, kc_ref, h), _kv_block(vp_ref, vc_ref, h)
            low, heads = _attn_heads(q_ref, s_ref, h, q_per_kv)
            probs = [_attn_probs(qm, kcat, sink, mask) for (_, _, qm, sink) in heads]
            outs = [_dot(p.astype(bf16), vcat) * inv for (p, inv, _) in probs]
            for i in range(0, len(heads), 2):
                pr = heads[i][0]
                o_ref[:, LANES * pr:LANES * (pr + 1)] = jnp.where(low, outs[i], outs[i + 1])

    return pl.pallas_call(
        body, out_shape=jax.ShapeDtypeStruct((L, d_attn), f32), grid=(nb,),
        in_specs=[q_spec(d_attn), q_spec(d_kk), prev(d_kk), q_spec(d_kk), prev(d_kk), pl.BlockSpec(sink_b.shape, lambda n: (0, 0))],
        out_specs=q_spec(d_attn), name="attn_fwd", compiler_params=_params("arbitrary"))(qr, kk, kk, vv, vv, sink_b)


def attn_bwd(qr, kk, vv, sink_b, attn, d_attn_out):
    L, d_attn = qr.shape
    nb, q_spec, prev = _attn_specs(L)
    d_kk = kk.shape[1]
    n_kv = d_kk // LANES
    q_per_kv = d_attn // HEAD_DIM // n_kv

    def body(q_ref, kc_ref, kp_ref, vc_ref, vp_ref, s_ref, o_ref, do_ref, dq_ref, dkc_ref, dkp_ref, dvc_ref, dvp_ref, ds_ref):
        n = pl.program_id(0)
        B = ATTN_BLOCK
        mask = _attn_mask(n)
        srow = lax.broadcasted_iota(jnp.int32, (SUBLANES, LANES), 0)

        @pl.when(n == 0)
        def _():
            ds_ref[...] = jnp.zeros_like(ds_ref)

        for h in range(n_kv):
            kcat, vcat = _kv_block(kp_ref, kc_ref, h), _kv_block(vp_ref, vc_ref, h)
            low, heads = _attn_heads(q_ref, s_ref, h, q_per_kv)
            probs = [_attn_probs(qm, kcat, sink, mask) for (_, _, qm, sink) in heads]
            dk = jnp.zeros((2 * B, LANES), f32)
            dv = dk
            dsink = jnp.zeros((SUBLANES, LANES), f32)
            dqs = []
            for i, ((pr, mine, qm, _), (p, inv, ps)) in enumerate(zip(heads, probs)):
                do2 = do_ref[:, LANES * pr:LANES * (pr + 1)]
                delta = jnp.sum(jnp.where(mine, do2 * o_ref[:, LANES * pr:LANES * (pr + 1)], 0.0), axis=1, keepdims=True)
                dob = jnp.where(mine, do2, 0.0).astype(bf16)
                p = p * inv
                ds = (p * (_dot(dob, vcat, NT) - delta)).astype(bf16)
                dqs.append(_dot(ds, kcat))
                dk = dk + _dot(ds, qm, TN)
                dv = dv + _dot(p.astype(bf16), dob, TN)
                dsink = dsink + jnp.where(srow == i, -jnp.sum(ps * inv * delta), 0.0)
            for i in range(0, len(heads), 2):
                pr = heads[i][0]
                dq_ref[:, LANES * pr:LANES * (pr + 1)] = jnp.where(low, dqs[i], dqs[i + 1])
            cols = slice(LANES * h, LANES * (h + 1))
            dkp_ref[:, cols] = dk[:B]
            dkc_ref[:, cols] = dk[B:]
            dvp_ref[:, cols] = dv[:B]
            dvc_ref[:, cols] = dv[B:]
            ds_ref[h] += dsink

    kv_shape = jax.ShapeDtypeStruct(kk.shape, f32)
    ds_shape = (n_kv, SUBLANES, LANES)
    return pl.pallas_call(
        body,
        out_shape=[jax.ShapeDtypeStruct((L, d_attn), f32), kv_shape, kv_shape, kv_shape, kv_shape, jax.ShapeDtypeStruct(ds_shape, f32)],
        grid=(nb,),
        in_specs=[q_spec(d_attn), q_spec(d_kk), prev(d_kk), q_spec(d_kk), prev(d_kk), pl.BlockSpec(sink_b.shape, lambda n: (0, 0)),
                  q_spec(d_attn), q_spec(d_attn)],
        out_specs=[q_spec(d_attn)] + [q_spec(d_kk)] * 4 + [pl.BlockSpec(ds_shape, lambda n: (0, 0, 0))],
        name="attn_bwd", compiler_params=_params("arbitrary"))(qr, kk, kk, vv, vv, sink_b, attn, d_attn_out)


SSM_T = 128
NQ = SUBLANES * SSM_STATE // LANES
NJ = SUBLANES


def _strided_put(ref, j, val):
    for q in range(NQ):
        ref.at[q][pl.ds(j, SSM_T, stride=NJ), :] = val[:, LANES * q:LANES * (q + 1)]


def _strided_get(ref, j):
    return jnp.concatenate([ref.at[q][pl.ds(j, SSM_T, stride=NJ), :] for q in range(NQ)], axis=1)


def _ssm_specs(L, rev):
    nt = L // SSM_T
    idx = (lambda i: nt - 1 - i) if rev else (lambda i: i)
    row = lambda w, cb=0: pl.BlockSpec((SSM_T, w), lambda i: (idx(i), cb))
    state = pl.BlockSpec((NQ, SSM_T * NJ, LANES), lambda i: (0, idx(i), 0))
    whole = lambda a: pl.BlockSpec(a.shape, lambda i: (0,) * a.ndim)
    return nt, row, state, whole


def ssm_fwd(u_bf, proj, u_cb, bd_re, bd_im, cd_re, cd_im, lam_re, lam_im, d_skip, dep=None):
    L, d_ssm = u_bf.shape
    nt, row, state, whole = _ssm_specs(L, False)
    half = d_ssm // 2
    gw = d_ssm // NJ
    n_dep = 0 if dep is None else 1

    def body(u_ref, u0_ref, u1_ref, bdr, bdi, cdr, cdi, lr_ref, li_ref, d_ref, *rest):
        y_ref, z_ref, sr_ref, si_ref, carry = rest[n_dep:]
        i = pl.program_id(0)

        @pl.when(i == 0)
        def _():
            carry[...] = jnp.zeros_like(carry)

        for j in range(NJ):
            uj = u_ref[:, gw * j:gw * (j + 1)]
            _strided_put(sr_ref, j, _dot(uj, bdr[j]))
            _strided_put(si_ref, j, _dot(uj, bdi[j]))
        lr = [lr_ref[q] for q in range(NQ)]
        li = [li_ref[q] for q in range(NQ)]

        def step(t, s):
            sr, si = s
            rows = pl.ds(pl.multiple_of(t * NJ, NJ), NJ)
            nr = tuple(lr[q] * sr[q] - li[q] * si[q] + sr_ref[q, rows, :] for q in range(NQ))
            ni = tuple(lr[q] * si[q] + li[q] * sr[q] + si_ref[q, rows, :] for q in range(NQ))
            for q in range(NQ):
                sr_ref[q, rows, :] = nr[q]
                si_ref[q, rows, :] = ni[q]
            return nr, ni

        init = (tuple(carry[0, q] for q in range(NQ)), tuple(carry[1, q] for q in range(NQ)))
        sr, si = lax.fori_loop(0, SSM_T, step, init, unroll=8)
        for q in range(NQ):
            carry[0, q] = sr[q]
            carry[1, q] = si[q]
        uf = jnp.concatenate([u0_ref[...], u1_ref[...]], axis=1)
        for j in range(NJ):
            cols = slice(gw * j, gw * (j + 1))
            yj = _dot(_strided_get(sr_ref, j).astype(bf16), cdr[j]) - _dot(_strided_get(si_ref, j).astype(bf16), cdi[j])
            yj = yj + d_ref[:, cols] * uf[:, cols]
            y_ref[:, cols] = yj
            z_ref[:, cols] = gelu(yj).astype(bf16)

    s_shape = jax.ShapeDtypeStruct((NQ, L * NJ, LANES), f32)
    consts = (bd_re, bd_im, cd_re, cd_im, lam_re, lam_im, d_skip)
    in_specs, operands, _ = _with_dep([row(d_ssm), row(half, u_cb), row(half, u_cb + 1)] + [whole(a) for a in consts],
                                      [u_bf, proj, proj, *consts], dep)
    return pl.pallas_call(
        body, out_shape=[jax.ShapeDtypeStruct((L, d_ssm), f32), jax.ShapeDtypeStruct((L, d_ssm), bf16), s_shape, s_shape], grid=(nt,),
        in_specs=in_specs, out_specs=[row(d_ssm), row(d_ssm), state, state],
        scratch_shapes=[pltpu.VMEM((2, NQ, NJ, LANES), f32)], name="ssm_fwd",
        compiler_params=_params("arbitrary"))(*operands)


def ssm_bwd(y, dz1, dz2, u_bf, proj, u_cb, s_re, s_im, bd_re, bd_im, cd_re, cd_im, lam_re, lam_im, d_skip, dep=None):
    L, d_ssm = y.shape
    nt, row, state, whole = _ssm_specs(L, True)
    half = d_ssm // 2
    gw = d_ssm // NJ
    n_dep = 0 if dep is None else 1

    def body(y_ref, dz1_ref, dz2_ref, u_ref, u0_ref, u1_ref, sr_ref, si_ref, bdr, bdi, cdr, cdi, lr_ref, li_ref, d_ref, *rest):
        du_ref, fbr, fbi, fcr, fci, dlr, dli, dd_ref, gr_ref, gi_ref, carry, dbdr, dbdi, dcdr, dcdi = rest[n_dep:]
        i = pl.program_id(0)

        @pl.when(i == 0)
        def _():
            carry[...] = jnp.zeros_like(carry)
            for r in (dbdr, dbdi, dcdr, dcdi, dlr, dli, dd_ref):
                r[...] = jnp.zeros_like(r)

        dyf = (dz1_ref[...] + dz2_ref[...]) * gelu_grad(y_ref[...])
        dyb = dyf.astype(bf16)
        for j in range(NJ):
            dyj = dyb[:, gw * j:gw * (j + 1)]
            _strided_put(gr_ref, j, _dot(dyj, cdr[j], NT))
            _strided_put(gi_ref, j, -_dot(dyj, cdi[j], NT))
            dcdr[j] += _dot(_strided_get(sr_ref, j).astype(bf16), dyj, TN)
            dcdi[j] -= _dot(_strided_get(si_ref, j).astype(bf16), dyj, TN)
        lr = [lr_ref[q] for q in range(NQ)]
        li = [li_ref[q] for q in range(NQ)]

        def step(k, c):
            gr, gi, ar, ai = c
            rows = pl.ds(pl.multiple_of((SSM_T - 1 - k) * NJ, NJ), NJ)
            s_r = [sr_ref[q, rows, :] for q in range(NQ)]
            s_i = [si_ref[q, rows, :] for q in range(NQ)]
            ar = tuple(ar[q] + gr[q] * s_r[q] + gi[q] * s_i[q] for q in range(NQ))
            ai = tuple(ai[q] + gi[q] * s_r[q] - gr[q] * s_i[q] for q in range(NQ))
            nr = tuple(gr_ref[q, rows, :] + lr[q] * gr[q] + li[q] * gi[q] for q in range(NQ))
            ni = tuple(gi_ref[q, rows, :] + lr[q] * gi[q] - li[q] * gr[q] for q in range(NQ))
            for q in range(NQ):
                gr_ref[q, rows, :] = nr[q]
                gi_ref[q, rows, :] = ni[q]
            return nr, ni, ar, ai

        zero = tuple(jnp.zeros((NJ, LANES), f32) for _ in range(NQ))
        init = (tuple(carry[0, q] for q in range(NQ)), tuple(carry[1, q] for q in range(NQ)), zero, zero)
        gr, gi, ar, ai = lax.fori_loop(0, SSM_T, step, init, unroll=8)
        for q in range(NQ):
            carry[0, q] = gr[q]
            carry[1, q] = gi[q]
            dlr[q] += ar[q]
            dli[q] += ai[q]
        uf = jnp.concatenate([u0_ref[...], u1_ref[...]], axis=1)
        dd_ref[...] += colsum(dyf * uf)
        for j in range(NJ):
            cols = slice(gw * j, gw * (j + 1))
            gjr, gji = _strided_get(gr_ref, j).astype(bf16), _strided_get(gi_ref, j).astype(bf16)
            du_ref[:, cols] = _dot(gjr, bdr[j], NT) + _dot(gji, bdi[j], NT) + d_ref[:, cols] * dyf[:, cols]
            uj = u_ref[:, cols]
            dbdr[j] += _dot(uj, gjr, TN)
            dbdi[j] += _dot(uj, gji, TN)

        @pl.when(i == nt - 1)
        def _():
            nb = NJ * SSM_STATE
            b_diag = (lax.broadcasted_iota(jnp.int32, (gw, nb), 0) // SSM_GROUP) == (lax.broadcasted_iota(jnp.int32, (gw, nb), 1) // SSM_STATE)
            c_diag = (lax.broadcasted_iota(jnp.int32, (nb, gw), 0) // SSM_STATE) == (lax.broadcasted_iota(jnp.int32, (nb, gw), 1) // SSM_GROUP)
            for j in range(NJ):
                for acc_ref, out in ((dbdr, fbr), (dbdi, fbi)):
                    m = jnp.where(b_diag, acc_ref[j], 0.0)
                    f = m[:, :LANES]
                    for q in range(1, nb // LANES):
                        f = f + m[:, LANES * q:LANES * (q + 1)]
                    out[j] = f + pltpu.roll(f, SSM_STATE, 1)
                for acc_ref, out in ((dcdr, fcr), (dcdi, fci)):
                    f = jnp.where(c_diag, acc_ref[j], 0.0)
                    for shift in (64, 32, 16):
                        f = f + pltpu.roll(f, shift, 1)
                    out[j] = f

    consts = (bd_re, bd_im, cd_re, cd_im, lam_re, lam_im, d_skip)
    acc = lambda a: jax.ShapeDtypeStruct(a.shape, f32)
    fb = jax.ShapeDtypeStruct((NJ, gw, LANES), f32)
    fc = jax.ShapeDtypeStruct((NJ, NJ * SSM_STATE, LANES), f32)
    outs = [jax.ShapeDtypeStruct((L, d_ssm), f32), fb, fb, fc, fc, acc(lam_re), acc(lam_im), acc(d_skip)]
    in_specs, operands, _ = _with_dep(
        [row(d_ssm)] * 4 + [row(half, u_cb), row(half, u_cb + 1), state, state] + [whole(a) for a in consts],
        [y, dz1, dz2, u_bf, proj, proj, s_re, s_im, *consts], dep)
    return pl.pallas_call(
        body, out_shape=outs, grid=(nt,),
        in_specs=in_specs, out_specs=[row(d_ssm)] + [whole(a) for a in outs[1:]],
        scratch_shapes=[pltpu.VMEM((NQ, SSM_T * NJ, LANES), f32), pltpu.VMEM((NQ, SSM_T * NJ, LANES), f32),
                        pltpu.VMEM((2, NQ, NJ, LANES), f32)] + [pltpu.VMEM(a.shape, f32) for a in (bd_re, bd_im, cd_re, cd_im)],
        name="ssm_bwd", compiler_params=_params("arbitrary"))(*operands)


def _cmul(ar, ai, br, bi):
    return ar * br - ai * bi, ar * bi + ai * br


def _disc(ar, ai, logdt):
    dt = jnp.exp(logdt)
    mag = jnp.exp(ar * dt)
    lr, li = mag * jnp.cos(ai * dt), mag * jnp.sin(ai * dt)
    den = ar * ar + ai * ai
    nr, ni = lr - 1.0, li
    fr, fi = (nr * ar + ni * ai) / den, (ni * ar - nr * ai) / den
    return dt, lr, li, den, fr, fi


def ssm_params(a_re, a_im, logdt_b, bt_re, bt_im, spread):
    def body(ar_ref, ai_ref, ld_ref, br_ref, bi_ref, sp_ref, lr_ref, li_ref, or_ref, oi_ref):
        _, lr, li, _, fr, fi = _disc(ar_ref[...], ai_ref[...], ld_ref[...])
        lr_ref[...] = lr
        li_ref[...] = li
        fre = jnp.dot(sp_ref[...], fr, precision=HIGHEST, preferred_element_type=f32)
        fie = jnp.dot(sp_ref[...], fi, precision=HIGHEST, preferred_element_type=f32)
        o_r, o_i = _cmul(fre, fie, br_ref[...], bi_ref[...])
        or_ref[...] = o_r
        oi_ref[...] = o_i

    g = jax.ShapeDtypeStruct(a_re.shape, f32)
    b = jax.ShapeDtypeStruct(bt_re.shape, f32)
    return pl.pallas_call(body, out_shape=[g, g, b, b], name="ssm_params",
                          compiler_params=_params())(a_re, a_im, logdt_b, bt_re, bt_im, spread)


def ssm_params_grad(a_re, a_im, logdt_b, bt_re, bt_im, spread, gather, dlam_re, dlam_im, dbt_re, dbt_im):
    def body(ar_ref, ai_ref, ld_ref, br_ref, bi_ref, sp_ref, ga_ref, glr_ref, gli_ref, gbr_ref, gbi_ref,
             dar_ref, dai_ref, dld_ref, dbr_ref, dbi_ref):
        ar, ai = ar_ref[...], ai_ref[...]
        dt, lr, li, den, fr, fi = _disc(ar, ai, ld_ref[...])
        hdot = functools.partial(jnp.dot, precision=HIGHEST, preferred_element_type=f32)
        fre, fie = hdot(sp_ref[...], fr), hdot(sp_ref[...], fi)
        gbr, gbi, br, bi = gbr_ref[...], gbi_ref[...], br_ref[...], bi_ref[...]
        dbr_ref[...], dbi_ref[...] = _cmul(fre, -fie, gbr, gbi)
        t_r, t_i = _cmul(br, -bi, gbr, gbi)
        gfr, gfi = hdot(ga_ref[...], t_r), hdot(ga_ref[...], t_i)
        iwr, iwi = ar / den, -ai / den
        x_r, x_i = _cmul(iwr, -iwi, gfr, gfi)
        glr, gli = glr_ref[...] + x_r, gli_ref[...] + x_i
        q_r, q_i = _cmul(fr, fi, iwr, iwi)
        gwr, gwi = _cmul(-q_r, q_i, gfr, gfi)
        y_r, y_i = _cmul(dt * lr, -dt * li, glr, gli)
        dar_ref[...] = gwr + y_r
        dai_ref[...] = gwi + y_i
        wl_r, wl_i = _cmul(ar, ai, lr, li)
        z_r, _ = _cmul(wl_r, -wl_i, glr, gli)
        dld_ref[...] = jnp.sum(z_r * dt, axis=1, keepdims=True)

    g = jax.ShapeDtypeStruct(a_re.shape, f32)
    b = jax.ShapeDtypeStruct(bt_re.shape, f32)
    return pl.pallas_call(body, out_shape=[g, g, jax.ShapeDtypeStruct((a_re.shape[0], 1), f32), b, b], name="ssm_params_grad",
                          compiler_params=_params())(a_re, a_im, logdt_b, bt_re, bt_im, spread, gather, dlam_re, dlam_im, dbt_re, dbt_im)


def _block_diag(t, rows, cols):
    G = t.shape[0]
    t = t.reshape(G // NJ, NJ, rows, cols)
    eye = jnp.eye(NJ, dtype=t.dtype)
    return jnp.einsum('jgrc,gh->jgrhc', t, eye).reshape(G // NJ, NJ * rows, NJ * cols)


def _state_layout(t):
    return t.reshape(NJ, NQ, LANES).transpose(1, 0, 2)


def _state_layout_inv(t, G, N):
    return t.transpose(1, 0, 2).reshape(G, N)


def _tiles2d(shape, budget_rows=128):
    rows, cols = shape
    tr = rows
    if rows > budget_rows:
        tr = budget_rows
        while rows % tr:
            tr -= SUBLANES
    return tr, cols


ADAM_TILE_BYTES = 3 << 19
ROW_ALIGN = 16


def _tile_rows(rows, row_bytes, target_bytes):
    tr = max(ROW_ALIGN, min(rows, target_bytes // row_bytes) // ROW_ALIGN * ROW_ALIGN)
    while rows % tr:
        tr -= ROW_ALIGN
    return tr


def _adam_update(w, g, m, v):
    c1 = 1.0 - ADAM_B1 ** ADAM_STEP
    c2 = 1.0 - ADAM_B2 ** ADAM_STEP
    nm = ADAM_B1 * m + (1.0 - ADAM_B1) * g
    nv = ADAM_B2 * v + (1.0 - ADAM_B2) * (g * g)
    delta = -ADAM_LR * ((nm / c1) / (jnp.sqrt(nv / c2) + ADAM_EPS) + ADAM_WD * w)
    return delta, nm, nv


def adamw_many(name, ws, gs, ms, vs):
    n = len(ws)

    def body(*refs):
        w, g, m, v = (refs[k * n:(k + 1) * n] for k in range(4))
        d, nm, nv = (refs[(4 + k) * n:(5 + k) * n] for k in range(3))
        for i in range(n):
            d[i][...], nm[i][...], nv[i][...] = _adam_update(w[i][...], g[i][...], m[i][...], v[i][...])

    o = [jax.ShapeDtypeStruct(a.shape, f32) for a in ws]
    outs = pl.pallas_call(body, out_shape=o * 3, name=name, compiler_params=_params())(*ws, *gs, *ms, *vs)
    return outs[:n], outs[n:2 * n], outs[2 * n:]


def adamw_halves(name, w, own, got, m, v, c_arr):
    h, cols = own.shape
    tr = _tile_rows(h, cols * 4, ADAM_TILE_BYTES)
    nh = h // tr

    def body(c_ref, w_ref, own_ref, got_ref, m_ref, v_ref, g_ref, d_ref, nm_ref, nv_ref):
        mine = (pl.program_id(0) // nh) == c_ref[0]
        g = jnp.where(mine, own_ref[...], got_ref[...])
        g_ref[...] = g
        d_ref[...], nm_ref[...], nv_ref[...] = _adam_update(w_ref[...], g, m_ref[...], v_ref[...])

    spec = pl.BlockSpec((tr, cols), lambda i, c: (i, 0))
    own_spec = pl.BlockSpec((tr, cols), lambda i, c: (jnp.where(i // nh == c[0], i % nh, 0), 0))
    got_spec = pl.BlockSpec((tr, cols), lambda i, c: (jnp.where(i // nh == c[0], 0, i % nh), 0))
    o = jax.ShapeDtypeStruct(w.shape, f32)
    grid_spec = pltpu.PrefetchScalarGridSpec(num_scalar_prefetch=1, grid=(2 * nh,),
                                             in_specs=[spec, own_spec, got_spec, spec, spec], out_specs=[spec] * 4)
    return pl.pallas_call(body, out_shape=[o, o, o, o], grid_spec=grid_spec, name=name,
                          compiler_params=_params("arbitrary"))(c_arr, w, own, got, m, v)


def pair_sum(name, g, got, c_arr):
    S, h, cols = got.shape
    tr, _ = _tiles2d((h, cols), 1024)
    nh = h // tr

    def body(c_ref, g_ref, o_ref, out_ref):
        out_ref[...] = (g_ref[...].astype(f32) + o_ref[...].astype(f32)).astype(out_ref.dtype)

    spec = pl.BlockSpec((None, tr, cols), lambda s, i, c: (s, i, 0))
    grid_spec = pltpu.PrefetchScalarGridSpec(
        num_scalar_prefetch=1, grid=(S, nh),
        in_specs=[pl.BlockSpec((None, tr, cols), lambda s, i, c: (s, c[0] * nh + i, 0)), spec], out_specs=spec)
    return pl.pallas_call(body, out_shape=jax.ShapeDtypeStruct(got.shape, g.dtype), grid_spec=grid_spec, name=name,
                          compiler_params=_params("parallel", "parallel"))(c_arr, g, got)


def chip_sum(name, pair, landed, mine_arr, dep=None):
    n_in, h, cols = landed.shape
    tr, _ = _tiles2d((h, cols), 256)

    def body(s_ref, p_ref, l_ref, *rest):
        acc = p_ref[...].astype(f32)
        for k in range(n_in):
            acc = acc + l_ref[k].astype(f32)
        rest[-1][...] = acc

    in_specs, operands, _ = _with_dep(
        [pl.BlockSpec((None, tr, cols), lambda i, s: (s[0], i, 0)), pl.BlockSpec((n_in, tr, cols), lambda i, s: (0, i, 0))],
        [pair, landed], dep)
    grid_spec = pltpu.PrefetchScalarGridSpec(num_scalar_prefetch=1, grid=(h // tr,), in_specs=in_specs,
                                             out_specs=pl.BlockSpec((tr, cols), lambda i, s: (i, 0)))
    return pl.pallas_call(body, out_shape=jax.ShapeDtypeStruct((h, cols), f32), grid_spec=grid_spec, name=name,
                          compiler_params=_params("parallel"))(mine_arr, *operands)


def into_slot(name, w, slot_arr, n_slots, dtype, dep=None):
    tr, cols = _tiles2d(w.shape, 256)

    def body(s_ref, w_ref, *rest):
        rest[-1][...] = w_ref[...].astype(dtype)

    in_specs, operands, _ = _with_dep([pl.BlockSpec((tr, cols), lambda i, s: (i, 0))], [w], dep)
    grid_spec = pltpu.PrefetchScalarGridSpec(num_scalar_prefetch=1, grid=(w.shape[0] // tr,), in_specs=in_specs,
                                             out_specs=pl.BlockSpec((None, tr, cols), lambda i, s: (s[0], i, 0)))
    return pl.pallas_call(body, out_shape=jax.ShapeDtypeStruct((n_slots,) + w.shape, dtype), grid_spec=grid_spec, name=name,
                          compiler_params=_params("parallel"))(slot_arr, *operands)


def sum_slots(name, t):
    S, rows, cols = t.shape
    tr, _ = _tiles2d((rows, cols), 256)

    def body(t_ref, o_ref):
        acc = t_ref[0]
        for s in range(1, S):
            acc = acc + t_ref[s]
        o_ref[...] = acc

    return pl.pallas_call(body, out_shape=jax.ShapeDtypeStruct((rows, cols), f32), grid=(rows // tr,),
                          in_specs=[pl.BlockSpec((S, tr, cols), lambda i: (0, i, 0))], out_specs=pl.BlockSpec((tr, cols), lambda i: (i, 0)),
                          name=name, compiler_params=_params("parallel"))(t)


def _place():
    x, y, c = lax.axis_index("x"), lax.axis_index("y"), lax.axis_index("c")
    return x, y, c


def _other_chips(x, y):
    return [(1 - x, y, 2 * (1 - x) + y), (x, 1 - y, 2 * x + 1 - y), (1 - x, 1 - y, 2 * (1 - x) + 1 - y)]


SEM = pl.BlockSpec(memory_space=pltpu.SEMAPHORE)
VM = pl.BlockSpec(memory_space=pltpu.VMEM)
DATAFLOW = pltpu.SideEffectType.DATAFLOW_SIDE_EFFECTING
TOKEN = jax.ShapeDtypeStruct((SUBLANES, LANES), f32)


def _gather_copy(buf, w, k, chip, c, mine, send, recv):
    px, py, _ = chip
    h = buf.shape[1] // 2
    half = buf.at[mine, pl.ds(c * h, h), :]
    return pltpu.make_async_remote_copy(src_ref=half, dst_ref=half, send_sem=send.at[3 * w + k], recv_sem=recv.at[3 * w + k],
                                        device_id=(px, py, c), device_id_type=MESH)


def _gather_landing(buf, w, k, chip, c, send, recv):
    px, py, s = chip
    h = buf.shape[1] // 2
    landed = buf.at[s, pl.ds(c * h, h), :]
    return pltpu.make_async_remote_copy(src_ref=landed, dst_ref=landed, send_sem=send.at[3 * w + k], recv_sem=recv.at[3 * w + k],
                                        device_id=(px, py, c), device_id_type=MESH)


def gather_start(name, bufs, groups, after, neighbours_only=()):
    nw, ng = len(bufs), len(groups)

    def body(*refs):
        outs = refs[nw + 1:]
        sems, dst = outs[:2 * ng], outs[2 * ng:2 * ng + nw]
        token = outs[2 * ng + nw]
        x, y, c = _place()
        mine = 2 * x + y
        for g, members in enumerate(groups):
            for i, w in enumerate(members):
                for k, chip in enumerate(_other_chips(x, y)[:2 if w in neighbours_only else 3]):
                    _gather_copy(dst[w], i, k, chip, c, mine, sems[2 * g], sems[2 * g + 1]).start()
        token[...] = jnp.zeros_like(token)

    sem_shapes = []
    for members in groups:
        sem_shapes += [pltpu.SemaphoreType.DMA((3 * len(members),))] * 2
    outs = pl.pallas_call(
        body, out_shape=sem_shapes + [jax.ShapeDtypeStruct(b.shape, b.dtype) for b in bufs] + [TOKEN],
        in_specs=[ANY] * (nw + 1), out_specs=[SEM] * (2 * ng) + [ANY] * nw + [VM],
        input_output_aliases={w: 2 * ng + w for w in range(nw)}, name=name,
        compiler_params=pltpu.CompilerParams(has_side_effects=DATAFLOW))(*bufs, after)
    return [(outs[2 * g], outs[2 * g + 1]) for g in range(ng)], list(outs[2 * ng:2 * ng + nw]), outs[2 * ng + nw]


def gather_wait(name, bufs, send, recv, after):
    nw = len(bufs)

    def body(*refs):
        src = refs[:nw]
        send_ref, recv_ref = refs[nw], refs[nw + 1]
        x, y, c = _place()
        mine = 2 * x + y
        for w in range(nw):
            for k, chip in enumerate(_other_chips(x, y)):
                _gather_copy(src[w], w, k, chip, c, mine, send_ref, recv_ref).wait_send()
                _gather_landing(src[w], w, k, chip, c, send_ref, recv_ref).wait_recv()

    return pl.pallas_call(
        body, out_shape=[jax.ShapeDtypeStruct(b.shape, b.dtype) for b in bufs],
        in_specs=[ANY] * nw + [SEM, SEM, ANY], out_specs=[ANY] * nw,
        input_output_aliases={w: w for w in range(nw)}, name=name,
        compiler_params=pltpu.CompilerParams(has_side_effects=DATAFLOW))(*bufs, send, recv, after)


def _relay_copy(buf, w, j, x, y, c, send, recv, landing):
    chips = _other_chips(x, y)
    px, py, _ = chips[j]
    h = buf.shape[1] // 2
    q = h // 2
    s = chips[2][2] if landing else chips[1 - j][2]
    part = buf.at[s, pl.ds(c * h + j * q, q), :]
    return pltpu.make_async_remote_copy(src_ref=part, dst_ref=part, send_sem=send.at[2 * w + j], recv_sem=recv.at[2 * w + j],
                                        device_id=(px, py, c), device_id_type=MESH)


def _early_pass(buf, nw, w, k, x, y, c, send, recv, landing):
    s = _other_chips(x, y)[k][2]
    h = buf.shape[1] // 2
    part = buf.at[s, pl.ds(((1 - c) if landing else c) * h, h), :]
    i = 2 * nw + 2 * w + k
    return pltpu.make_async_remote_copy(src_ref=part, dst_ref=part, send_sem=send.at[i], recv_sem=recv.at[i],
                                        device_id=(x, y, 1 - c), device_id_type=MESH)


def gather_relay(name, bufs, sems, more, after):
    nw, nm = len(bufs), len(more)
    ns = 4 if nm else 2

    def body(*refs):
        ins, outs = refs[:nw + nm + 2 * nw + 1], refs[nw + nm + 2 * nw + 1:]
        src, d_sems = ins[:nw], ins[nw + nm:nw + nm + 2 * nw]
        r_send, r_recv = outs[:2]
        m_send, m_recv = outs[2:ns] if nm else (None, None)
        dst, mdst, token = outs[ns:ns + nw], outs[ns + nw:ns + nw + nm], outs[ns + nw + nm]
        x, y, c = _place()
        mine = 2 * x + y
        chips = _other_chips(x, y)
        for w in range(nw):
            for k in range(2):
                _gather_copy(src[w], 0, k, chips[k], c, mine, d_sems[2 * w], d_sems[2 * w + 1]).wait_send()
                _gather_landing(src[w], 0, k, chips[k], c, d_sems[2 * w], d_sems[2 * w + 1]).wait_recv()
            for j in range(2):
                _relay_copy(dst[w], w, j, x, y, c, r_send, r_recv, False).start()
            for k in range(2):
                _early_pass(dst[w], nw, w, k, x, y, c, r_send, r_recv, False).start()
        for w in range(nm):
            for k, chip in enumerate(chips):
                _gather_copy(mdst[w], w, k, chip, c, mine, m_send, m_recv).start()
        token[...] = jnp.zeros_like(token)

    sem_shapes = [pltpu.SemaphoreType.DMA((4 * nw,))] * 2 + [pltpu.SemaphoreType.DMA((3 * nm,))] * (ns - 2)
    flat_sems = [s for pair in sems for s in pair]
    outs = pl.pallas_call(
        body, out_shape=sem_shapes + [jax.ShapeDtypeStruct(b.shape, b.dtype) for b in list(bufs) + list(more)] + [TOKEN],
        in_specs=[ANY] * (nw + nm) + [SEM] * (2 * nw) + [ANY], out_specs=[SEM] * ns + [ANY] * (nw + nm) + [VM],
        input_output_aliases={i: ns + i for i in range(nw + nm)}, name=name,
        compiler_params=pltpu.CompilerParams(has_side_effects=DATAFLOW))(*bufs, *more, *flat_sems, after)
    m_sems = (outs[2], outs[3]) if nm else None
    return outs[0], outs[1], m_sems, list(outs[ns:ns + nw]), list(outs[ns + nw:ns + nw + nm]), outs[ns + nw + nm]


def gather_wait_relay(name, bufs, r_send, r_recv, after):
    nw = len(bufs)

    def body(*refs):
        src = refs[:nw]
        send_ref, recv_ref = refs[nw], refs[nw + 1]
        x, y, c = _place()
        for w in range(nw):
            for j in range(2):
                _relay_copy(src[w], w, j, x, y, c, send_ref, recv_ref, False).wait_send()
                _relay_copy(src[w], w, j, x, y, c, send_ref, recv_ref, True).wait_recv()
                _early_pass(src[w], nw, w, j, x, y, c, send_ref, recv_ref, False).wait_send()
                _early_pass(src[w], nw, w, j, x, y, c, send_ref, recv_ref, True).wait_recv()

    return pl.pallas_call(
        body, out_shape=[jax.ShapeDtypeStruct(b.shape, b.dtype) for b in bufs],
        in_specs=[ANY] * nw + [SEM, SEM, ANY], out_specs=[ANY] * nw,
        input_output_aliases={w: w for w in range(nw)}, name=name,
        compiler_params=pltpu.CompilerParams(has_side_effects=DATAFLOW))(*bufs, r_send, r_recv, after)


def gather_forward(name, bufs, which=(0, 1, 2)):
    nw = len(bufs)

    def body(*refs):
        dst = refs[nw:2 * nw]
        send, recv = refs[2 * nw:]
        x, y, c = _place()
        sib = (x, y, 1 - c)
        barrier = pltpu.get_barrier_semaphore()
        pl.semaphore_signal(barrier, inc=1, device_id=sib, device_id_type=MESH)
        pl.semaphore_wait(barrier, 1)
        cps = []
        for w in range(nw):
            h = dst[w].shape[1] // 2
            for k in which:
                s = _other_chips(x, y)[k][2]
                landed = dst[w].at[s, pl.ds(c * h, h), :]
                cp = pltpu.make_async_remote_copy(src_ref=landed, dst_ref=landed, send_sem=send.at[w, k], recv_sem=recv.at[w, k],
                                                  device_id=sib, device_id_type=MESH)
                cp.start()
                cps.append(cp)
        for w in range(nw):
            h = dst[w].shape[1] // 2
            for k in which:
                s = _other_chips(x, y)[k][2]
                other = dst[w].at[s, pl.ds((1 - c) * h, h), :]
                pltpu.make_async_remote_copy(src_ref=other, dst_ref=other, send_sem=send.at[w, k], recv_sem=recv.at[w, k],
                                             device_id=sib, device_id_type=MESH).wait_recv()
        for cp in cps:
            cp.wait_send()

    sem = pltpu.SemaphoreType.DMA((nw, 3))
    return pl.pallas_call(
        body, out_shape=[jax.ShapeDtypeStruct(b.shape, b.dtype) for b in bufs],
        in_specs=[ANY] * nw, out_specs=[ANY] * nw, input_output_aliases={w: w for w in range(nw)},
        scratch_shapes=[sem, sem], name=name,
        compiler_params=pltpu.CompilerParams(has_side_effects=True, collective_id=SIBLING_PAIR))(*bufs)


def _scatter_copy(src, dst, w, k, chip, c, send, recv):
    px, py, s = chip
    return pltpu.make_async_remote_copy(src_ref=src.at[s], dst_ref=dst.at[k], send_sem=send.at[3 * w + k], recv_sem=recv.at[3 * w + k],
                                        device_id=(px, py, c), device_id_type=MESH)


def scatter_start(name, parts):
    nw = len(parts)
    lands = [pltpu.with_memory_space_constraint(lax.empty((N_CHIPS - 1,) + p.shape[1:], p.dtype), pltpu.HBM) for p in parts]

    def body(*refs):
        outs = refs[2 * nw:]
        send, recv = outs[0], outs[1]
        src, dst, token = outs[2:2 + nw], outs[2 + nw:2 + 2 * nw], outs[2 + 2 * nw]
        x, y, c = _place()
        for w in range(nw):
            for k, chip in enumerate(_other_chips(x, y)):
                _scatter_copy(src[w], dst[w], w, k, chip, c, send, recv).start()
        token[...] = jnp.zeros_like(token)

    sem = pltpu.SemaphoreType.DMA((3 * nw,))
    outs = pl.pallas_call(
        body, out_shape=[sem, sem] + [jax.ShapeDtypeStruct(p.shape, p.dtype) for p in parts]
        + [jax.ShapeDtypeStruct(l.shape, l.dtype) for l in lands] + [TOKEN],
        in_specs=[ANY] * (2 * nw), out_specs=[SEM, SEM] + [ANY] * (2 * nw) + [VM],
        input_output_aliases={i: 2 + i for i in range(2 * nw)}, name=name,
        compiler_params=pltpu.CompilerParams(has_side_effects=DATAFLOW))(*parts, *lands)
    return outs[0], outs[1], list(outs[2:2 + nw]), list(outs[2 + nw:2 + 2 * nw]), outs[2 + 2 * nw]


def scatter_wait(name, parts, lands, send, recv, after):
    nw = len(parts)

    def body(*refs):
        src, dst = refs[:nw], refs[nw:2 * nw]
        send_ref, recv_ref = refs[2 * nw], refs[2 * nw + 1]
        x, y, c = _place()
        for w in range(nw):
            for k, chip in enumerate(_other_chips(x, y)):
                cp = _scatter_copy(src[w], dst[w], w, k, chip, c, send_ref, recv_ref)
                cp.wait_send()
                cp.wait_recv()

    outs = pl.pallas_call(
        body, out_shape=[jax.ShapeDtypeStruct(a.shape, a.dtype) for a in list(parts) + list(lands)],
        in_specs=[ANY] * (2 * nw) + [SEM, SEM, ANY], out_specs=[ANY] * (2 * nw),
        input_output_aliases={i: i for i in range(2 * nw)}, name=name,
        compiler_params=pltpu.CompilerParams(has_side_effects=DATAFLOW))(*parts, *lands, send, recv, after)
    return list(outs[:nw]), list(outs[nw:])


SIBLING_PAIR = 0


def _sibling_copy(src, dst, w, c, half_rows, send, recv, sib):
    if half_rows:
        h = src.shape[1] // 2
        src = src.at[:, pl.ds((1 - c) * h, h), :]
    return pltpu.make_async_remote_copy(src_ref=src, dst_ref=dst, send_sem=send.at[w], recv_sem=recv.at[w],
                                        device_id=sib, device_id_type=MESH)


def _landing(shape, dtype):
    return pltpu.with_memory_space_constraint(lax.empty(shape, dtype), pltpu.HBM)


def sibling_start(name, srcs, half_rows):
    nw = len(srcs)
    lands = [_landing((s.shape[0], s.shape[1] // 2, s.shape[2]) if half_rows else s.shape, s.dtype) for s in srcs]

    def body(*refs):
        outs = refs[2 * nw:]
        send, recv = outs[0], outs[1]
        src, dst, token = outs[2:2 + nw], outs[2 + nw:2 + 2 * nw], outs[2 + 2 * nw]
        x, y, c = _place()
        barrier = pltpu.get_barrier_semaphore()
        pl.semaphore_signal(barrier, inc=1, device_id=(x, y, 1 - c), device_id_type=MESH)
        pl.semaphore_wait(barrier, 1)
        for w in range(nw):
            _sibling_copy(src[w], dst[w], w, c, half_rows, send, recv, (x, y, 1 - c)).start()
        token[...] = jnp.zeros_like(token)

    sem = pltpu.SemaphoreType.DMA((nw,))
    outs = pl.pallas_call(
        body, out_shape=[sem, sem] + [jax.ShapeDtypeStruct(a.shape, a.dtype) for a in list(srcs) + lands] + [TOKEN],
        in_specs=[ANY] * (2 * nw), out_specs=[SEM, SEM] + [ANY] * (2 * nw) + [VM],
        input_output_aliases={i: 2 + i for i in range(2 * nw)}, name=name,
        compiler_params=pltpu.CompilerParams(has_side_effects=DATAFLOW, collective_id=SIBLING_PAIR))(*srcs, *lands)
    return outs[0], outs[1], list(outs[2:2 + nw]), list(outs[2 + nw:2 + 2 * nw]), outs[2 + 2 * nw]


def sibling_wait(name, srcs, lands, send, recv, half_rows, after):
    nw = len(srcs)

    def body(*refs):
        src, dst = refs[:nw], refs[nw:2 * nw]
        send_ref, recv_ref = refs[2 * nw], refs[2 * nw + 1]
        x, y, c = _place()
        for w in range(nw):
            cp = _sibling_copy(src[w], dst[w], w, c, half_rows, send_ref, recv_ref, (x, y, 1 - c))
            cp.wait_send()
            cp.wait_recv()

    outs = pl.pallas_call(
        body, out_shape=[jax.ShapeDtypeStruct(a.shape, a.dtype) for a in list(srcs) + list(lands)],
        in_specs=[ANY] * (2 * nw) + [SEM, SEM, ANY], out_specs=[ANY] * (2 * nw),
        input_output_aliases={i: i for i in range(2 * nw)}, name=name,
        compiler_params=pltpu.CompilerParams(has_side_effects=DATAFLOW))(*srcs, *lands, send, recv, after)
    return list(outs[:nw]), list(outs[nw:])


def _peer(x, y, c, r):
    return (1 - x if r & 4 else x, 1 - y if r & 2 else y, 1 - c if r & 1 else c)


def _everyone_copy(buf, r, x, y, c, send, recv, landing):
    px, py, pc = _peer(x, y, c, r)
    slot = buf.at[4 * px + 2 * py + pc] if landing else buf.at[4 * x + 2 * y + c]
    return pltpu.make_async_remote_copy(src_ref=slot, dst_ref=slot, send_sem=send.at[r - 1], recv_sem=recv.at[r - 1],
                                        device_id=(px, py, pc), device_id_type=MESH)


def everyone_start(name, buf):
    def body(buf_in, send, recv, buf_ref, token):
        x, y, c = _place()
        for r in range(1, N_DEV):
            _everyone_copy(buf_ref, r, x, y, c, send, recv, False).start()
        token[...] = jnp.zeros_like(token)

    sem = pltpu.SemaphoreType.DMA((N_DEV - 1,))
    return pl.pallas_call(
        body, out_shape=[sem, sem, jax.ShapeDtypeStruct(buf.shape, buf.dtype), TOKEN],
        in_specs=[ANY], out_specs=[SEM, SEM, ANY, VM], input_output_aliases={0: 2}, name=name,
        compiler_params=pltpu.CompilerParams(has_side_effects=DATAFLOW))(buf)


def everyone_wait(name, buf, send, recv, after):
    def body(buf_ref, send_ref, recv_ref, after_ref, out_ref):
        x, y, c = _place()
        for r in range(1, N_DEV):
            _everyone_copy(buf_ref, r, x, y, c, send_ref, recv_ref, False).wait_send()
            _everyone_copy(buf_ref, r, x, y, c, send_ref, recv_ref, True).wait_recv()

    return pl.pallas_call(
        body, out_shape=jax.ShapeDtypeStruct(buf.shape, buf.dtype), in_specs=[ANY, SEM, SEM, ANY], out_specs=ANY,
        input_output_aliases={0: 0}, name=name,
        compiler_params=pltpu.CompilerParams(has_side_effects=DATAFLOW))(buf, send, recv, after)


def local_step(x, pos, tgt, small, d_in, get_w, put_g, first_dep=None, tick=lambda name, after: None):
    L, D = x.shape
    d_kv = N_KV_HEADS * HEAD_DIM
    d_ssm = small["d_skip"].shape[1]
    d_attn = d_in - 2 * d_kv - d_ssm
    big = {}
    G = d_ssm // SSM_GROUP
    N, P = SSM_STATE, SSM_GROUP
    gbf = bf16

    half_dim = HEAD_DIM // 2
    inv_freq = ROPE_THETA ** (-jnp.arange(half_dim, dtype=f32) / half_dim)
    inv_freq = jnp.tile(inv_freq, LANES // half_dim).reshape(1, LANES)
    sink_b = jnp.broadcast_to(small["sinks"].reshape(-1, 1), (small["sinks"].size, LANES))

    spread = jnp.repeat(jnp.eye(G, dtype=f32), P, axis=0)
    logdt_b = jnp.broadcast_to(small["log_dt"].reshape(G, 1), (G, N))
    bt_re = small["b_re"].reshape(G, N, P).transpose(0, 2, 1).reshape(G * P, N)
    bt_im = small["b_im"].reshape(G, N, P).transpose(0, 2, 1).reshape(G * P, N)
    a_re, a_im = small["a_re"].reshape(G, N), small["a_im"].reshape(G, N)
    lam_re, lam_im, bbt_re, bbt_im = ssm_params(a_re, a_im, logdt_b, bt_re, bt_im, spread)
    bd_re = _block_diag(bbt_re.reshape(G, P, N), P, N).astype(bf16)
    bd_im = _block_diag(bbt_im.reshape(G, P, N), P, N).astype(bf16)
    c_re = small["c_re"].reshape(G, P, N).transpose(0, 2, 1)
    c_im = small["c_im"].reshape(G, P, N).transpose(0, 2, 1)
    cd_re = _block_diag(c_re, N, P).astype(bf16)
    cd_im = _block_diag(c_im, N, P).astype(bf16)
    lam_re_l, lam_im_l = _state_layout(lam_re), _state_layout(lam_im)

    def k1(i, nt, xt, g):
        return (rms_fwd(xt, g),)
    xn = rowwise("pre_mix_norm", k1, L, [full(x)], [small["g_pre_mix"]], [(D, bf16)], dep=first_dep)[0]
    big["w_in"] = get_w("w_in", xn)
    proj = mm_nn("proj_in", xn, big["w_in"])
    qr, kk, vv, u_bf = qkv_prep(proj, pos, inv_freq, d_attn, d_kv)
    attn = attn_fwd(qr, kk, vv, sink_b)
    u_cb = (d_attn + 2 * d_kv) // (d_ssm // 2)
    token = tick("attn", attn)
    y, z_bf, s_re, s_im = ssm_fwd(u_bf, proj, u_cb, bd_re, bd_im, cd_re, cd_im, lam_re_l, lam_im_l, small["d_skip"], dep=token)
    token = tick("ssm", z_bf)
    big["w_glu"] = get_w("w_glu", z_bf)
    gl = mm_nn("glu_proj", z_bf, big["w_glu"], dep=token)

    def k6(i, nt, at, yt, glt, bg, ga, gs):
        ssm = gelu(yt) * sigmoid(glt + bg)
        return (jnp.concatenate([rms_fwd(at, ga), rms_fwd(ssm, gs)], axis=1),)
    mixed = rowwise("mix_norms", k6, L, [full(attn), full(y), full(gl)],
                    [small["b_glu"], small["g_attn_out"], small["g_ssm_out"]], [(d_attn + d_ssm, bf16)])[0]
    big["w_o"] = get_w("w_o", mixed)
    mix = mm_nn("proj_out", mixed, big["w_o"])

    def k7(i, nt, xt, mt, gpm, gpf):
        h = xt + rms_fwd(mt, gpm)
        return h, rms_fwd(h, gpf)
    h, hn = rowwise("post_mix", k7, L, [full(x), full(mix)], [small["g_post_mix"], small["g_pre_ffn"]], [(D, f32), (D, bf16)])
    big["w_gate"] = get_w("w_gate", hn)
    big["w_up"] = get_w("w_up", hn)
    gt, up, hid = ffn_hidden(hn, big["w_gate"], big["w_up"])
    d_ff_dim = gt.shape[1]
    big["w_down"] = get_w("w_down", hid)
    ff = mm_nn("ffn_down", hid, big["w_down"], tk=d_ff_dim // 2)

    def k9(i, nt, ht, fft, tt, g):
        out = ht + rms_fwd(fft, g)
        err = out - tt
        per_row = jnp.mean(err * err, axis=-1, keepdims=True)
        loss = 0.5 * jnp.sum(per_row) * jnp.where(_lane((1, LANES)) == 0, 1.0, 0.0)
        d_out = err * (1.0 / D)
        d_ff, dg = rms_bwd(fft, g, d_out)
        return d_out, d_ff, dg, loss
    d_out, d_ff, dg_post_ffn, loss = rowwise("loss_head", k9, L, [full(h), full(ff), full(tgt)], [small["g_post_ffn"]],
                                             [(D, f32), (D, bf16)], reds=[D, LANES])

    d_gt, d_up = ffn_hidden_grad(d_ff, big["w_down"], gt, up)
    token = put_g("w_down", mm_tn("dw_down", hid, d_ff, out_dtype=gbf, tm=d_ff_dim // N_CHIPS))
    d_hn = mm_nt_pair("d_hn", d_gt, big["w_gate"], d_up, big["w_up"], dep=token)
    token = put_g("w_gate", mm_tn("dw_gate", hn, d_gt, shards=N_CHIPS, out_dtype=gbf))
    token = put_g("w_up", mm_tn("dw_up", hn, d_up, shards=N_CHIPS, out_dtype=gbf, dep=token))

    def k11(i, nt, ht, da, do, mt, gpf, gpm):
        dh_n, dg_pf = rms_bwd(ht, gpf, da)
        dh = do + dh_n
        d_mix, dg_pm = rms_bwd(mt, gpm, dh)
        return dh, d_mix, dg_pf, dg_pm
    dh, d_mix, dg_pre_ffn, dg_post_mix = rowwise("post_mix_grad", k11, L, [full(h), full(d_hn), full(d_out), full(mix)],
                                                 [small["g_pre_ffn"], small["g_post_mix"]], [(D, f32), (D, bf16)], reds=[D, D], dep=token)
    d_mixed = mm_nt("d_mixed", d_mix, big["w_o"])
    token = put_g("w_o", mm_tn("dw_o", mixed, d_mix, out_dtype=gbf))

    def k12(i, nt, at, yt, glt, da_n, ds_n, bg, ga, gs):
        z = gelu(yt)
        sg = sigmoid(glt + bg)
        ssm = z * sg
        d_at, dga = rms_bwd(at, ga, da_n)
        d_ssm_t, dgs = rms_bwd(ssm, gs, ds_n)
        d_gl = d_ssm_t * z * sg * (1.0 - sg)
        return d_at, d_ssm_t * sg, d_gl, dga, dgs, colsum(d_gl)
    d_attn_o, dz1, d_gl, dg_attn, dg_ssm, db_glu = rowwise(
        "mix_norms_grad", k12, L, [full(attn), full(y), full(gl), (d_mixed, d_attn, 0, 0), (d_mixed, d_ssm, d_attn // d_ssm, 0)],
        [small["b_glu"], small["g_attn_out"], small["g_ssm_out"]], [(d_attn, f32), (d_ssm, f32), (d_ssm, bf16)],
        reds=[d_attn, d_ssm, d_ssm], dep=token)
    dz2 = mm_nt("d_glu_in", d_gl, big["w_glu"])
    token = put_g("w_glu", mm_tn("dw_glu", z_bf, d_gl, out_dtype=gbf))

    du, dbd_re, dbd_im, dcd_re, dcd_im, dlam_re_l, dlam_im_l, dd_skip = ssm_bwd(
        y, dz1, dz2, u_bf, proj, u_cb, s_re, s_im, bd_re, bd_im, cd_re, cd_im, lam_re_l, lam_im_l, small["d_skip"], dep=token)
    dq, dkk_c, dkk_p, dvv_c, dvv_p, dsink = attn_bwd(qr, kk, vv, sink_b, attn, d_attn_o)
    d_proj = qkv_grad(dq, dkk_c, dkk_p, dvv_c, dvv_p, du, pos, inv_freq)
    d_xn = mm_nt("d_xn", d_proj, big["w_in"])
    token = put_g("w_in", mm_tn("dw_in", xn, d_proj, shards=N_CHIPS, out_dtype=gbf))

    def k17(i, nt, xt, dxn, dht, g):
        dx, dg = rms_bwd(xt, g, dxn)
        return dht + dx, dg
    grad_x, dg_pre_mix = rowwise("pre_mix_grad", k17, L, [full(x), full(d_xn), full(dh)], [small["g_pre_mix"]],
                                 [(D, f32)], reds=[D], dep=token)

    gather = spread.T
    dbbt_re = dbd_re.reshape(G * P, LANES)[:, :N]
    dbbt_im = dbd_im.reshape(G * P, LANES)[:, :N]
    dc_re = dcd_re.reshape(G, N, LANES)[:, :, :P].transpose(0, 2, 1)
    dc_im = dcd_im.reshape(G, N, LANES)[:, :, :P].transpose(0, 2, 1)
    d_a_re, d_a_im, d_logdt, dbt_re, dbt_im = ssm_params_grad(
        a_re, a_im, logdt_b, bt_re, bt_im, spread, gather,
        _state_layout_inv(dlam_re_l, G, N), _state_layout_inv(dlam_im_l, G, N), dbbt_re, dbbt_im)
    q_per_kv = d_attn // HEAD_DIM // N_KV_HEADS
    small_grads = {
        "g_pre_mix": dg_pre_mix, "sinks": dsink[:, :q_per_kv, 0].reshape(1, -1),
        "a_re": d_a_re, "a_im": d_a_im, "log_dt": d_logdt.reshape(1, G),
        "b_re": dbt_re, "b_im": dbt_im,
        "c_re": dc_re, "c_im": dc_im,
        "d_skip": dd_skip, "b_glu": db_glu, "g_attn_out": dg_attn, "g_ssm_out": dg_ssm,
        "g_post_mix": dg_post_mix, "g_pre_ffn": dg_pre_ffn, "g_post_ffn": dg_post_ffn,
    }
    return loss, grad_x, small_grads


WEIGHTS = ['g_pre_mix', 'w_in', 'sinks', 'a_re', 'a_im', 'log_dt', 'b_re', 'b_im', 'c_re', 'c_im', 'd_skip', 'w_glu', 'b_glu',
           'g_attn_out', 'g_ssm_out', 'w_o', 'g_post_mix', 'g_pre_ffn', 'w_gate', 'w_up', 'w_down', 'g_post_ffn']
BIG = ['w_in', 'w_glu', 'w_o', 'w_gate', 'w_up', 'w_down']
COL_SHARDED = ['w_in', 'w_gate', 'w_up']
SMALL = [n for n in WEIGHTS if n not in BIG]
GATHER_GROUPS = [["w_in"], ["w_glu", "w_o"], ["w_gate", "w_up"], ["w_down"]]
REDUCE_GROUPS = [["w_down", "w_gate", "w_up"], ["w_o", "w_glu", "w_in"]]


PACK_ROWS = 256


def _pack(parts):
    flat = jnp.concatenate([p.reshape(-1) for p in parts])
    pad = (-flat.size) % (PACK_ROWS * LANES)
    return jnp.pad(flat, (0, pad)).reshape(-1, LANES)


TRANSPOSED_VIEW = ("b_re", "b_im")


def small_view(name, a):
    if name in TRANSPOSED_VIEW:
        a = a.transpose(0, 1, 3, 2)
    return a.reshape(-1, a.shape[-1])


def small_unview(name, p, shape):
    if name in TRANSPOSED_VIEW:
        return p.reshape(shape[0], shape[1], shape[3], shape[2]).transpose(0, 1, 3, 2)
    return p.reshape(shape)


def _unpack(packed, shapes):
    flat = packed.reshape(-1)
    out, off = [], 0
    for s in shapes:
        n = int(np.prod(s))
        out.append(flat[off:off + n].reshape(s))
        off += n
    return out


def kernel(x, positions, g_pre_mix, w_in, sinks, a_re, a_im, log_dt, b_re, b_im, c_re, c_im, d_skip, w_glu, b_glu, g_attn_out, g_ssm_out, w_o, g_post_mix, g_pre_ffn, w_gate, w_up, w_down, g_post_ffn, loss_target, m_g_pre_mix, m_w_in, m_sinks, m_a_re, m_a_im, m_log_dt, m_b_re, m_b_im, m_c_re, m_c_im, m_d_skip, m_w_glu, m_b_glu, m_g_attn_out, m_g_ssm_out, m_w_o, m_g_post_mix, m_g_pre_ffn, m_w_gate, m_w_up, m_w_down, m_g_post_ffn, v_g_pre_mix, v_w_in, v_sinks, v_a_re, v_a_im, v_log_dt, v_b_re, v_b_im, v_c_re, v_c_im, v_d_skip, v_w_glu, v_b_glu, v_g_attn_out, v_g_ssm_out, v_w_o, v_g_post_mix, v_g_pre_ffn, v_w_gate, v_w_up, v_w_down, v_g_post_ffn):
    args = dict(locals())
    w = {n: args[n] for n in WEIGHTS}
    m = {n: args["m_" + n] for n in WEIGHTS}
    v = {n: args["v_" + n] for n in WEIGHTS}
    L, D = x.shape[1], x.shape[2]

    ax, ay, ac = _place()
    mine_arr = (2 * ax + ay).astype(jnp.int32).reshape(1)
    c_arr = ac.astype(jnp.int32).reshape(1)

    me_arr = (4 * ax + 2 * ay + ac).astype(jnp.int32).reshape(1)

    bufs = {"w_in": into_slot("cast_w_in", w["w_in"][0], mine_arr, N_CHIPS, bf16)}
    (first_sems,), (bufs["w_in"],), token = gather_start("gather_start_in", [bufs["w_in"]], [[0]], mine_arr)
    sems = {"w_in": first_sems}
    for n in BIG[1:]:
        bufs[n] = into_slot("cast_" + n, w[n][0], mine_arr, N_CHIPS, bf16, dep=token)
    first = ["w_gate", "w_glu", "w_o", "w_up"]
    (sems["w_gate"], sems["w_glu"], sems["w_up"]), started, token = gather_start(
        "gather_start_rest", [bufs[n] for n in first], [[0], [1, 2], [3]], token, neighbours_only=(0, 3))
    bufs.update(zip(first, started))
    relays, ready = {}, set()

    def tick(name, after):
        n, more = ("w_gate", []) if name == "attn" else ("w_up", ["w_down"])
        r_send, r_recv, sems["w_down"], (bufs[n],), down, tok = gather_relay(
            "gather_relay_" + n, [bufs[n]], [sems[n]], [bufs[k] for k in more], after)
        bufs.update(zip(more, down))
        relays[n] = (r_send, r_recv)
        return tok

    def get_w(n, after):
        if n not in ready:
            members = [g for g in GATHER_GROUPS if n in g][0]
            if members[0] in relays:
                landed = [gather_wait_relay("gather_wait_" + k, [bufs[k]], *relays[k], after)[0] for k in members]
                which = (2,)
            else:
                landed = gather_wait("gather_wait_" + members[0], [bufs[k] for k in members], *sems[members[0]], after)
                which = (0, 1, 2)
            bufs.update(zip(members, gather_forward("gather_forward_" + members[0], landed, which)))
            ready.update(members)
        g = bufs[n]
        return g if n in COL_SHARDED else g.reshape(g.shape[0] * g.shape[1], g.shape[2])

    swaps, inflight = {}, []

    def put_g(n, g):
        g3 = g if n in COL_SHARDED else g.reshape(N_CHIPS, g.shape[0] // N_CHIPS, g.shape[1])
        swaps[n] = sibling_start("swap_start_" + n, [g3], True)
        for gi, members in enumerate(REDUCE_GROUPS):
            if n == members[-1]:
                last = swaps[n][4]
                pair = []
                for k in members:
                    send, recv, srcs, lands, _ = swaps[k]
                    (src,), (got,) = sibling_wait("swap_wait_" + k, srcs, lands, send, recv, True, last)
                    pair.append(pair_sum("pair_sum_" + k, src, got, c_arr))
                send, recv, parts, lands, tok = scatter_start("scatter_start_%d" % gi, pair)
                inflight.append((members, send, recv, parts, lands))
                return tok
        return swaps[n][4]

    small = {n: w[n].reshape(1, -1) for n in SMALL}
    pos = positions.reshape(L, 1).astype(f32)
    d_in = N_CHIPS * w["w_in"].shape[2]
    loss, grad_x, small_grads = local_step(x[0], pos, loss_target[0], small, d_in, get_w, put_g, first_dep=token, tick=tick)

    shapes = [w[n].shape for n in SMALL]
    blocks = into_slot("small_block", _pack([small_grads[n] for n in SMALL] + [loss]), me_arr, N_DEV, f32)
    small_send, small_recv, blocks, after = everyone_start("small_start", blocks)

    grads, delta, new_m, new_v = {}, {}, {}, {}
    for gi, (members, send, recv, parts, lands) in enumerate(inflight):
        parts, landed = scatter_wait("scatter_wait_%d" % gi, parts, lands, send, recv, after)
        joins, dep = [], None
        for k, p, t in zip(members, parts, landed):
            joins.append(sibling_start("join_start_" + k, [chip_sum("chip_sum_" + k, p, t, mine_arr, dep=dep)], False))
            dep = after = joins[-1][4]
        for n, (send, recv, srcs, lands, _) in zip(members, joins):
            (own,), (sib,) = sibling_wait("join_wait_" + n, srcs, lands, send, recv, False, after)
            g_, d_, m_, v_ = adamw_halves("adamw_" + n, w[n][0], own, sib, m[n][0], v[n][0], c_arr)
            grads[n], delta[n], new_m[n], new_v[n] = g_[None], d_[None], m_[None], v_[None]
            after = v_
    blocks = everyone_wait("small_wait", blocks, small_send, small_recv, after)
    small_sum = sum_slots("small_sum", blocks)
    *small_g, loss = _unpack(small_sum, [small_view(n, w[n]).shape for n in SMALL] + [loss.shape])
    loss = loss[0, 0]
    outs = adamw_many("adamw_small", [small_view(n, w[n]) for n in SMALL], small_g,
                      [small_view(n, m[n]) for n in SMALL], [small_view(n, v[n]) for n in SMALL])
    for t, parts in zip((grads, delta, new_m, new_v), (small_g,) + tuple(outs)):
        t.update({n: small_unview(n, p, w[n].shape) for n, p in zip(SMALL, parts)})

    return (loss, grad_x[None], *[grads[n] for n in WEIGHTS], *[delta[n] for n in WEIGHTS],
            *[new_m[n] for n in WEIGHTS], *[new_v[n] for n in WEIGHTS])
```

```python
import functools
import math

import jax
import jax.numpy as jnp
import numpy as np
from jax import lax
from jax.experimental import pallas as pl
from jax.experimental.pallas import tpu as pltpu

f32 = jnp.float32
bf16 = jnp.bfloat16
HIGHEST = lax.Precision.HIGHEST
MESH = pl.DeviceIdType.MESH

HEAD_DIM = 64
N_KV_HEADS = 4
ATTN_BLOCK = 128
ROPE_THETA = 10000.0
SSM_GROUP = 16
SSM_STATE = 64
RMS_EPS = 1e-6
LANES = 128
SUBLANES = 8
VMEM_LIMIT = 52 * 1024 * 1024
N_CHIPS = 4
N_DEV = 8
NEG = -1e30

ADAM_LR, ADAM_B1, ADAM_B2, ADAM_EPS, ADAM_WD, ADAM_STEP = 0.001, 0.9, 0.999, 1e-08, 0.01, 10

NN = (((1,), (0,)), ((), ()))
NT = (((1,), (1,)), ((), ()))
TN = (((0,), (0,)), ((), ()))


def _params(*sem):
    return pltpu.CompilerParams(dimension_semantics=sem or None, vmem_limit_bytes=VMEM_LIMIT)


def _dot(a, b, dims=NN):
    return lax.dot_general(a, b, dims, preferred_element_type=f32)


def _pick(dim, pref):
    t = min(dim, pref)
    while dim % t:
        t -= LANES
    assert t > 0, (dim, pref)
    return t


ANY = pl.BlockSpec(memory_space=pl.ANY)


def _with_dep(in_specs, operands, dep):
    if dep is None:
        return list(in_specs), list(operands), 0
    return list(in_specs) + [ANY], list(operands) + [dep], 1


def _mm_call(name, grid, in_specs, out_spec, out_shape, acc_shape, dims, operands, dep=None):
    nk = grid[2]
    in_specs, operands, n_dep = _with_dep(in_specs, operands, dep)

    def body_one(a_ref, b_ref, *rest):
        o_ref = rest[n_dep]
        o_ref[...] = _dot(a_ref[...], b_ref[...], dims).astype(o_ref.dtype)

    def body(a_ref, b_ref, *rest):
        o_ref, acc_ref = rest[n_dep], rest[n_dep + 1]
        k = pl.program_id(2)

        @pl.when(k == 0)
        def _():
            acc_ref[...] = _dot(a_ref[...], b_ref[...], dims)

        @pl.when((k > 0) & (k < nk - 1))
        def _():
            acc_ref[...] += _dot(a_ref[...], b_ref[...], dims)

        @pl.when(k == nk - 1)
        def _():
            o_ref[...] = (acc_ref[...] + _dot(a_ref[...], b_ref[...], dims)).astype(o_ref.dtype)

    return pl.pallas_call(
        body_one if nk == 1 else body, out_shape=out_shape, grid=grid, in_specs=in_specs, out_specs=out_spec,
        scratch_shapes=[] if nk == 1 else [pltpu.VMEM(acc_shape, f32)], name=name,
        compiler_params=_params("parallel", "parallel", "arbitrary"))(*operands)


def mm_nt_pair(name, a1, b1, a2, b2, tm=1024, tk=1024, dep=None):
    M = a1.shape[0]
    S, K, n = b1.shape
    tm, tko = _pick(M, tm), _pick(K, tk)
    nk = 2 * S

    def body(a1_ref, b1_ref, a2_ref, b2_ref, *rest):
        o_ref, acc_ref = rest[-2], rest[-1]
        k = pl.program_id(2)

        @pl.when(k == 0)
        def _():
            acc_ref[...] = _dot(a1_ref[...], b1_ref[...], NT)

        @pl.when((k > 0) & (k < S))
        def _():
            acc_ref[...] += _dot(a1_ref[...], b1_ref[...], NT)

        @pl.when((k >= S) & (k < nk - 1))
        def _():
            acc_ref[...] += _dot(a2_ref[...], b2_ref[...], NT)

        @pl.when(k == nk - 1)
        def _():
            o_ref[...] = acc_ref[...] + _dot(a2_ref[...], b2_ref[...], NT)

    first = lambda k: jnp.minimum(k, S - 1)
    second = lambda k: jnp.maximum(k - S, 0)
    in_specs = [pl.BlockSpec((tm, n), lambda i, j, k: (i, first(k))), pl.BlockSpec((None, tko, n), lambda i, j, k: (first(k), j, 0)),
                pl.BlockSpec((tm, n), lambda i, j, k: (i, second(k))), pl.BlockSpec((None, tko, n), lambda i, j, k: (second(k), j, 0))]
    in_specs, operands, _ = _with_dep(in_specs, (a1, b1, a2, b2), dep)
    return pl.pallas_call(
        body, out_shape=jax.ShapeDtypeStruct((M, K), f32), grid=(M // tm, K // tko, nk), in_specs=in_specs,
        out_specs=pl.BlockSpec((tm, tko), lambda i, j, k: (i, j)), scratch_shapes=[pltpu.VMEM((tm, tko), f32)], name=name,
        compiler_params=_params("parallel", "parallel", "arbitrary"))(*operands)


def mm_nn(name, a, b, out_dtype=f32, tm=1024, tn=1024, tk=2048, dep=None):
    M, K = a.shape
    tm, tk = _pick(M, tm), _pick(K, tk)
    if b.ndim == 3:
        S, _, n = b.shape
        tn = _pick(n, 2048)
        per = n // tn
        b_spec = pl.BlockSpec((None, tk, tn), lambda i, j, k: (j // per, k, j % per))
        N = S * n
    else:
        N = b.shape[1]
        tn = _pick(N, tn)
        b_spec = pl.BlockSpec((tk, tn), lambda i, j, k: (k, j))
    grid = (M // tm, N // tn, K // tk)
    return _mm_call(name, grid, [pl.BlockSpec((tm, tk), lambda i, j, k: (i, k)), b_spec],
                    pl.BlockSpec((tm, tn), lambda i, j, k: (i, j)), jax.ShapeDtypeStruct((M, N), out_dtype),
                    (tm, tn), NN, (a, b), dep)


def mm_nt(name, a, b, out_dtype=f32, tm=1024, tn=2048, tk=1024, dep=None):
    M, N = a.shape
    tm = _pick(M, tm)
    if b.ndim == 3:
        S, K, n = b.shape
        tr = _pick(n, 2048)
        per = n // tr
        tko = _pick(K, tk)
        b_spec = pl.BlockSpec((None, tko, tr), lambda i, j, k: (k // per, j, k % per))
    else:
        K = b.shape[0]
        tr = _pick(N, tn)
        tko = _pick(K, tk)
        b_spec = pl.BlockSpec((tko, tr), lambda i, j, k: (j, k))
    grid = (M // tm, K // tko, N // tr)
    return _mm_call(name, grid, [pl.BlockSpec((tm, tr), lambda i, j, k: (i, k)), b_spec],
                    pl.BlockSpec((tm, tko), lambda i, j, k: (i, j)), jax.ShapeDtypeStruct((M, K), out_dtype),
                    (tm, tko), NT, (a, b), dep)


def mm_tn(name, a, b, shards=None, out_dtype=f32, tm=1024, tn=1024, tl=2048, dep=None):
    L, K = a.shape
    N = b.shape[1]
    tl, tko = _pick(L, tl), _pick(K, tm)
    if shards:
        n = N // shards
        tn = _pick(n, 2048)
        per = n // tn
        o_spec = pl.BlockSpec((None, tko, tn), lambda i, j, k: (j // per, i, j % per))
        o_shape = jax.ShapeDtypeStruct((shards, K, n), out_dtype)
    else:
        tn = _pick(N, tn)
        o_spec = pl.BlockSpec((tko, tn), lambda i, j, k: (i, j))
        o_shape = jax.ShapeDtypeStruct((K, N), out_dtype)
    grid = (K // tko, N // tn, L // tl)
    return _mm_call(name, grid, [pl.BlockSpec((tl, tko), lambda i, j, k: (k, i)),
                                 pl.BlockSpec((tl, tn), lambda i, j, k: (k, j))],
                    o_spec, o_shape, (tko, tn), TN, (a, b), dep)


def ffn_hidden(hn, w_gate, w_up, tm=512):
    M, K = hn.shape
    S, _, n = w_gate.shape
    tm = _pick(M, tm)

    def body(a_ref, g_ref, u_ref, dg_ref, du_ref, hid_ref):
        a = a_ref[...]
        g = _dot(a, g_ref[...])
        u = _dot(a, u_ref[...])
        sg = sigmoid(g)
        act = g * sg
        dg_ref[...] = (u * (sg * (1.0 + g * (1.0 - sg)))).astype(bf16)
        du_ref[...] = act.astype(bf16)
        hid_ref[...] = (act * u).astype(bf16)

    w_spec = pl.BlockSpec((None, K, n), lambda s, i: (s, 0, 0))
    o_spec = pl.BlockSpec((tm, n), lambda s, i: (i, s))
    o = jax.ShapeDtypeStruct((M, S * n), bf16)
    return pl.pallas_call(
        body, out_shape=[o, o, o], grid=(S, M // tm), in_specs=[pl.BlockSpec((tm, K), lambda s, i: (i, 0)), w_spec, w_spec],
        out_specs=[o_spec, o_spec, o_spec], name="ffn_hidden", compiler_params=_params("parallel", "parallel"))(hn, w_gate, w_up)


def ffn_hidden_grad(d_ff, w_down, hid_dg, hid_du, tm=512):
    M, D = d_ff.shape
    F = w_down.shape[0]
    n = _pick(F // N_CHIPS, 2048)
    tm = _pick(M, tm)

    def body(a_ref, b_ref, pg_ref, pu_ref, dg_ref, du_ref):
        dh = _dot(a_ref[...], b_ref[...], NT)
        dg_ref[...] = (dh * pg_ref[...].astype(f32)).astype(bf16)
        du_ref[...] = (dh * pu_ref[...].astype(f32)).astype(bf16)

    t_spec = pl.BlockSpec((tm, n), lambda j, i: (i, j))
    o = jax.ShapeDtypeStruct((M, F), bf16)
    return pl.pallas_call(
        body, out_shape=[o, o], grid=(F // n, M // tm),
        in_specs=[pl.BlockSpec((tm, D), lambda j, i: (i, 0)), pl.BlockSpec((n, D), lambda j, i: (j, 0)), t_spec, t_spec],
        out_specs=[t_spec, t_spec], name="ffn_hidden_grad", compiler_params=_params("parallel", "parallel"))(d_ff, w_down, hid_dg, hid_du)


def rowwise(name, fn, L, rows, bcast, outs, reds=(), tr=256, dep=None):
    tr = min(tr, L)
    nt = L // tr
    n_rows, n_b, n_o = len(rows), len(bcast), len(outs)
    n_dep = 0 if dep is None else 1

    def body(*refs):
        i = pl.program_id(0)
        ins = [r[...] for r in refs[:n_rows + n_b]]
        res = fn(i, nt, *ins)
        o_refs = refs[n_rows + n_b + n_dep:]
        for k in range(n_o):
            o_refs[k][...] = res[k].astype(o_refs[k].dtype)
        if reds:
            @pl.when(i == 0)
            def _():
                for k in range(len(reds)):
                    o_refs[n_o + k][...] = jnp.zeros_like(o_refs[n_o + k])
            for k in range(len(reds)):
                o_refs[n_o + k][...] += res[n_o + k]

    def row_spec(width, cb, shift):
        if shift:
            return pl.BlockSpec((tr, width), lambda i: (jnp.minimum(i + shift, nt - 1), cb))
        return pl.BlockSpec((tr, width), lambda i: (i, cb))

    in_specs = [row_spec(w, cb, sh) for (_, w, cb, sh) in rows]
    in_specs += [pl.BlockSpec(b.shape, lambda i: (0, 0)) for b in bcast]
    out_specs = [pl.BlockSpec((tr, w), lambda i: (i, 0)) for (w, _) in outs]
    out_specs += [pl.BlockSpec((1, w), lambda i: (0, 0)) for w in reds]
    out_shape = [jax.ShapeDtypeStruct((L, w), dt) for (w, dt) in outs]
    out_shape += [jax.ShapeDtypeStruct((1, w), f32) for w in reds]
    in_specs, operands, _ = _with_dep(in_specs, [r[0] for r in rows] + list(bcast), dep)
    return pl.pallas_call(
        body, out_shape=out_shape, grid=(nt,), in_specs=in_specs, out_specs=out_specs, name=name,
        compiler_params=_params("arbitrary"))(*operands)


def full(a):
    return (a, a.shape[1], 0, 0)


def colsum(v):
    return jnp.sum(v, axis=0, keepdims=True)


def rms_fwd(x, g):
    r = lax.rsqrt(jnp.mean(x * x, axis=-1, keepdims=True) + RMS_EPS)
    return x * r * g


def rms_bwd(x, g, dy):
    r = lax.rsqrt(jnp.mean(x * x, axis=-1, keepdims=True) + RMS_EPS)
    xh = x * r
    dyg = dy * g
    dx = r * (dyg - xh * jnp.mean(dyg * xh, axis=-1, keepdims=True))
    return dx, colsum(dy * xh)


GELU_C = math.sqrt(2.0 / math.pi)


def gelu(y):
    return y * (0.5 * (1.0 + jnp.tanh(GELU_C * (y + 0.044715 * (y * y * y)))))


def gelu_grad(y):
    t = jnp.tanh(GELU_C * (y + 0.044715 * (y * y * y)))
    return 0.5 * (1.0 + t) + 0.5 * y * (1.0 - t * t) * (GELU_C * (1.0 + 3 * 0.044715 * (y * y)))


def sigmoid(v):
    return 1.0 / (1.0 + jnp.exp(-v))


def _lane(shape):
    return lax.broadcasted_iota(jnp.int32, shape, 1)


def _rot_chunk(t, cos, sin_signed):
    first = (_lane(t.shape) % HEAD_DIM) < (HEAD_DIM // 2)
    partner = jnp.where(first, pltpu.roll(t, LANES - HEAD_DIM // 2, 1), pltpu.roll(t, HEAD_DIM // 2, 1))
    return t * cos + partner * sin_signed


def _cos_sin(pos, inv_freq, inverse):
    ang = pos * inv_freq
    cos, sin = jnp.cos(ang), jnp.sin(ang)
    first = (_lane(ang.shape) % HEAD_DIM) < (HEAD_DIM // 2)
    sign = jnp.where(first, -1.0, 1.0) * (-1.0 if inverse else 1.0)
    return cos, sin * sign


def _dup_head(chunk, odd):
    low = _lane(chunk.shape) < HEAD_DIM
    x = jnp.where(low != odd, chunk, 0.0)
    return x + pltpu.roll(x, HEAD_DIM, 1)


def _chunks(v):
    return [v[:, LANES * c:LANES * (c + 1)] for c in range(v.shape[1] // LANES)]


def qkv_prep(proj, pos, inv_freq, d_attn, d_kv):
    L = proj.shape[0]
    d_ssm = proj.shape[1] - d_attn - 2 * d_kv
    half = d_ssm // 2
    scale = 1.0 / math.sqrt(HEAD_DIM)

    def fn(i, nt, q, k, v, u0, u1, p, invf):
        cos, sin = _cos_sin(p, invf, False)
        qr = jnp.concatenate([_rot_chunk(c, cos, sin) for c in _chunks(q)], axis=1) * scale
        kr = [_rot_chunk(c, cos, sin) for c in _chunks(k)]
        kk = jnp.concatenate([_dup_head(c, odd) for c in kr for odd in (False, True)], axis=1)
        vv = jnp.concatenate([_dup_head(c, odd) for c in _chunks(v) for odd in (False, True)], axis=1)
        return qr, kk, vv, jnp.concatenate([u0, u1], axis=1)

    u_cb = (d_attn + 2 * d_kv) // half
    return rowwise("qkv_prep", fn, L,
                   [(proj, d_attn, 0, 0), (proj, d_kv, d_attn // d_kv, 0), (proj, d_kv, d_attn // d_kv + 1, 0),
                    (proj, half, u_cb, 0), (proj, half, u_cb + 1, 0), full(pos)],
                   [inv_freq], [(d_attn, bf16), (2 * d_kv, bf16), (2 * d_kv, bf16), (d_ssm, bf16)])


def qkv_grad(dq, dkk_c, dkk_p, dvv_c, dvv_p, du, pos, inv_freq):
    L, d_attn = dq.shape
    d_kv = dkk_c.shape[1] // 2
    scale = 1.0 / math.sqrt(HEAD_DIM)

    def fold(cur, prev, i, nt):
        t = cur + jnp.where(i < nt - 1, prev, 0.0)
        out = []
        for c in range(t.shape[1] // (2 * LANES)):
            even, odd = t[:, 2 * c * LANES:(2 * c + 1) * LANES], t[:, (2 * c + 1) * LANES:(2 * c + 2) * LANES]
            even, odd = even + pltpu.roll(even, HEAD_DIM, 1), odd + pltpu.roll(odd, HEAD_DIM, 1)
            out.append(jnp.where(_lane(even.shape) < HEAD_DIM, even, odd))
        return out

    def fn(i, nt, dq_t, kc, kp, vc, vp, du_t, p, invf):
        cos, sin = _cos_sin(p, invf, True)
        dq_o = jnp.concatenate([_rot_chunk(c, cos, sin) for c in _chunks(dq_t)], axis=1) * scale
        dk_o = jnp.concatenate([_rot_chunk(c, cos, sin) for c in fold(kc, kp, i, nt)], axis=1)
        dv_o = jnp.concatenate(fold(vc, vp, i, nt), axis=1)
        return (jnp.concatenate([dq_o, dk_o, dv_o, du_t], axis=1),)

    return rowwise("qkv_grad", fn, L,
                   [full(dq), full(dkk_c), (dkk_p, 2 * d_kv, 0, 1), full(dvv_c), (dvv_p, 2 * d_kv, 0, 1), full(du), full(pos)],
                   [inv_freq], [(d_attn + 2 * d_kv + du.shape[1], bf16)], tr=ATTN_BLOCK)[0]


def _attn_specs(L):
    nb = L // ATTN_BLOCK
    B = ATTN_BLOCK
    q_spec = lambda width: pl.BlockSpec((B, width), lambda n: (n, 0))
    prev = lambda width: pl.BlockSpec((B, width), lambda n: (jnp.maximum(n - 1, 0), 0))
    return nb, q_spec, prev


def _attn_mask(n):
    B = ATTN_BLOCK
    row = lax.broadcasted_iota(jnp.int32, (B, 2 * B), 0)
    col = lax.broadcasted_iota(jnp.int32, (B, 2 * B), 1)
    return ((col < B) & (col > row) & (n > 0)) | ((col >= B) & (row >= col - B))


def _attn_probs(qm, kcat, sink, mask):
    s = jnp.where(mask, _dot(qm, kcat, NT), NEG)
    m = jnp.maximum(jnp.max(s, axis=1, keepdims=True), sink)
    p, ps = jnp.exp(s - m), jnp.exp(sink - m)
    inv = 1.0 / (jnp.sum(p, axis=1, keepdims=True) + ps)
    return p, inv, ps


def _attn_heads(q_ref, s_ref, h, q_per_kv):
    low = _lane((ATTN_BLOCK, LANES)) < HEAD_DIM
    heads = []
    for pr in range(h * q_per_kv // 2, (h + 1) * q_per_kv // 2):
        q2 = q_ref[:, LANES * pr:LANES * (pr + 1)]
        for odd in (False, True):
            mine = low != odd
            sink = jnp.max(s_ref[2 * pr + int(odd):2 * pr + int(odd) + 1, :], axis=1, keepdims=True)
            heads.append((pr, mine, jnp.where(mine, q2, jnp.zeros_like(q2)), sink))
    return low, heads


def _kv_block(prev_ref, cur_ref, h):
    return jnp.concatenate([prev_ref[:, LANES * h:LANES * (h + 1)], cur_ref[:, LANES * h:LANES * (h + 1)]], axis=0)


def attn_fwd(qr, kk, vv, sink_b):
    L, d_attn = qr.shape
    nb, q_spec, prev = _attn_specs(L)
    d_kk = kk.shape[1]
    n_kv = d_kk // LANES
    q_per_kv = d_attn // HEAD_DIM // n_kv

    def body(q_ref, kc_ref, kp_ref, vc_ref, vp_ref, s_ref, o_ref):
        mask = _attn_mask(pl.program_id(0))
        for h in range(n_kv):
            kcat, vcat = _kv_block(kp_ref, kc_ref, h), _kv_block(vp_ref, vc_ref, h)
            low, heads = _attn_heads(q_ref, s_ref, h, q_per_kv)
            probs = [_attn_probs(qm, kcat, sink, mask) for (_, _, qm, sink) in heads]
            outs = [_dot(p.astype(bf16), vcat) * inv for (p, inv, _) in probs]
            for i in range(0, len(heads), 2):
                pr = heads[i][0]
                o_ref[:, LANES * pr:LANES * (pr + 1)] = jnp.where(low, outs[i], outs[i + 1])

    return pl.pallas_call(
        body, out_shape=jax.ShapeDtypeStruct((L, d_attn), f32), grid=(nb,),
        in_specs=[q_spec(d_attn), q_spec(d_kk), prev(d_kk), q_spec(d_kk), prev(d_kk), pl.BlockSpec(sink_b.shape, lambda n: (0, 0))],
        out_specs=q_spec(d_attn), name="attn_fwd", compiler_params=_params("arbitrary"))(qr, kk, kk, vv, vv, sink_b)


def attn_bwd(qr, kk, vv, sink_b, attn, d_attn_out):
    L, d_attn = qr.shape
    nb, q_spec, prev = _attn_specs(L)
    d_kk = kk.shape[1]
    n_kv = d_kk // LANES
    q_per_kv = d_attn // HEAD_DIM // n_kv

    def body(q_ref, kc_ref, kp_ref, vc_ref, vp_ref, s_ref, o_ref, do_ref, dq_ref, dkc_ref, dkp_ref, dvc_ref, dvp_ref, ds_ref):
        n = pl.program_id(0)
        B = ATTN_BLOCK
        mask = _attn_mask(n)
        srow = lax.broadcasted_iota(jnp.int32, (SUBLANES, LANES), 0)

        @pl.when(n == 0)
        def _():
            ds_ref[...] = jnp.zeros_like(ds_ref)

        for h in range(n_kv):
            kcat, vcat = _kv_block(kp_ref, kc_ref, h), _kv_block(vp_ref, vc_ref, h)
            low, heads = _attn_heads(q_ref, s_ref, h, q_per_kv)
            probs = [_attn_probs(qm, kcat, sink, mask) for (_, _, qm, sink) in heads]
            dk = jnp.zeros((2 * B, LANES), f32)
            dv = dk
            dsink = jnp.zeros((SUBLANES, LANES), f32)
            dqs = []
            for i, ((pr, mine, qm, _), (p, inv, ps)) in enumerate(zip(heads, probs)):
                do2 = do_ref[:, LANES * pr:LANES * (pr + 1)]
                delta = jnp.sum(jnp.where(mine, do2 * o_ref[:, LANES * pr:LANES * (pr + 1)], 0.0), axis=1, keepdims=True)
                dob = jnp.where(mine, do2, 0.0).astype(bf16)
                p = p * inv
                ds = (p * (_dot(dob, vcat, NT) - delta)).astype(bf16)
                dqs.append(_dot(ds, kcat))
                dk = dk + _dot(ds, qm, TN)
                dv = dv + _dot(p.astype(bf16), dob, TN)
                dsink = dsink + jnp.where(srow == i, -jnp.sum(ps * inv * delta), 0.0)
            for i in range(0, len(heads), 2):
                pr = heads[i][0]
                dq_ref[:, LANES * pr:LANES * (pr + 1)] = jnp.where(low, dqs[i], dqs[i + 1])
            cols = slice(LANES * h, LANES * (h + 1))
            dkp_ref[:, cols] = dk[:B]
            dkc_ref[:, cols] = dk[B:]
            dvp_ref[:, cols] = dv[:B]
            dvc_ref[:, cols] = dv[B:]
            ds_ref[h] += dsink

    kv_shape = jax.ShapeDtypeStruct(kk.shape, f32)
    ds_shape = (n_kv, SUBLANES, LANES)
    return pl.pallas_call(
        body,
        out_shape=[jax.ShapeDtypeStruct((L, d_attn), f32), kv_shape, kv_shape, kv_shape, kv_shape, jax.ShapeDtypeStruct(ds_shape, f32)],
        grid=(nb,),
        in_specs=[q_spec(d_attn), q_spec(d_kk), prev(d_kk), q_spec(d_kk), prev(d_kk), pl.BlockSpec(sink_b.shape, lambda n: (0, 0)),
                  q_spec(d_attn), q_spec(d_attn)],
        out_specs=[q_spec(d_attn)] + [q_spec(d_kk)] * 4 + [pl.BlockSpec(ds_shape, lambda n: (0, 0, 0))],
        name="attn_bwd", compiler_params=_params("arbitrary"))(qr, kk, kk, vv, vv, sink_b, attn, d_attn_out)


SSM_T = 128
NQ = SUBLANES * SSM_STATE // LANES
NJ = SUBLANES


def _strided_put(ref, j, val):
    for q in range(NQ):
        ref.at[q][pl.ds(j, SSM_T, stride=NJ), :] = val[:, LANES * q:LANES * (q + 1)]


def _strided_get(ref, j):
    return jnp.concatenate([ref.at[q][pl.ds(j, SSM_T, stride=NJ), :] for q in range(NQ)], axis=1)


def _ssm_specs(L, rev):
    nt = L // SSM_T
    idx = (lambda i: nt - 1 - i) if rev else (lambda i: i)
    row = lambda w, cb=0: pl.BlockSpec((SSM_T, w), lambda i: (idx(i), cb))
    state = pl.BlockSpec((NQ, SSM_T * NJ, LANES), lambda i: (0, idx(i), 0))
    whole = lambda a: pl.BlockSpec(a.shape, lambda i: (0,) * a.ndim)
    return nt, row, state, whole


def ssm_fwd(u_bf, proj, u_cb, bd_re, bd_im, cd_re, cd_im, lam_re, lam_im, d_skip, dep=None):
    L, d_ssm = u_bf.shape
    nt, row, state, whole = _ssm_specs(L, False)
    half = d_ssm // 2
    gw = d_ssm // NJ
    n_dep = 0 if dep is None else 1

    def body(u_ref, u0_ref, u1_ref, bdr, bdi, cdr, cdi, lr_ref, li_ref, d_ref, *rest):
        y_ref, z_ref, sr_ref, si_ref, carry = rest[n_dep:]
        i = pl.program_id(0)

        @pl.when(i == 0)
        def _():
            carry[...] = jnp.zeros_like(carry)

        for j in range(NJ):
            uj = u_ref[:, gw * j:gw * (j + 1)]
            _strided_put(sr_ref, j, _dot(uj, bdr[j]))
            _strided_put(si_ref, j, _dot(uj, bdi[j]))
        lr = [lr_ref[q] for q in range(NQ)]
        li = [li_ref[q] for q in range(NQ)]

        def step(t, s):
            sr, si = s
            rows = pl.ds(pl.multiple_of(t * NJ, NJ), NJ)
            nr = tuple(lr[q] * sr[q] - li[q] * si[q] + sr_ref[q, rows, :] for q in range(NQ))
            ni = tuple(lr[q] * si[q] + li[q] * sr[q] + si_ref[q, rows, :] for q in range(NQ))
            for q in range(NQ):
                sr_ref[q, rows, :] = nr[q]
                si_ref[q, rows, :] = ni[q]
            return nr, ni

        init = (tuple(carry[0, q] for q in range(NQ)), tuple(carry[1, q] for q in range(NQ)))
        sr, si = lax.fori_loop(0, SSM_T, step, init, unroll=8)
        for q in range(NQ):
            carry[0, q] = sr[q]
            carry[1, q] = si[q]
        uf = jnp.concatenate([u0_ref[...], u1_ref[...]], axis=1)
        for j in range(NJ):
            cols = slice(gw * j, gw * (j + 1))
            yj = _dot(_strided_get(sr_ref, j).astype(bf16), cdr[j]) - _dot(_strided_get(si_ref, j).astype(bf16), cdi[j])
            yj = yj + d_ref[:, cols] * uf[:, cols]
            y_ref[:, cols] = yj
            z_ref[:, cols] = gelu(yj).astype(bf16)

    s_shape = jax.ShapeDtypeStruct((NQ, L * NJ, LANES), f32)
    consts = (bd_re, bd_im, cd_re, cd_im, lam_re, lam_im, d_skip)
    in_specs, operands, _ = _with_dep([row(d_ssm), row(half, u_cb), row(half, u_cb + 1)] + [whole(a) for a in consts],
                                      [u_bf, proj, proj, *consts], dep)
    return pl.pallas_call(
        body, out_shape=[jax.ShapeDtypeStruct((L, d_ssm), f32), jax.ShapeDtypeStruct((L, d_ssm), bf16), s_shape, s_shape], grid=(nt,),
        in_specs=in_specs, out_specs=[row(d_ssm), row(d_ssm), state, state],
        scratch_shapes=[pltpu.VMEM((2, NQ, NJ, LANES), f32)], name="ssm_fwd",
        compiler_params=_params("arbitrary"))(*operands)


def ssm_bwd(y, dz1, dz2, u_bf, proj, u_cb, s_re, s_im, bd_re, bd_im, cd_re, cd_im, lam_re, lam_im, d_skip, dep=None):
    L, d_ssm = y.shape
    nt, row, state, whole = _ssm_specs(L, True)
    half = d_ssm // 2
    gw = d_ssm // NJ
    n_dep = 0 if dep is None else 1

    def body(y_ref, dz1_ref, dz2_ref, u_ref, u0_ref, u1_ref, sr_ref, si_ref, bdr, bdi, cdr, cdi, lr_ref, li_ref, d_ref, *rest):
        du_ref, fbr, fbi, fcr, fci, dlr, dli, dd_ref, gr_ref, gi_ref, carry, dbdr, dbdi, dcdr, dcdi = rest[n_dep:]
        i = pl.program_id(0)

        @pl.when(i == 0)
        def _():
            carry[...] = jnp.zeros_like(carry)
            for r in (dbdr, dbdi, dcdr, dcdi, dlr, dli, dd_ref):
                r[...] = jnp.zeros_like(r)

        dyf = (dz1_ref[...] + dz2_ref[...]) * gelu_grad(y_ref[...])
        dyb = dyf.astype(bf16)
        for j in range(NJ):
            dyj = dyb[:, gw * j:gw * (j + 1)]
            _strided_put(gr_ref, j, _dot(dyj, cdr[j], NT))
            _strided_put(gi_ref, j, -_dot(dyj, cdi[j], NT))
            dcdr[j] += _dot(_strided_get(sr_ref, j).astype(bf16), dyj, TN)
            dcdi[j] -= _dot(_strided_get(si_ref, j).astype(bf16), dyj, TN)
        lr = [lr_ref[q] for q in range(NQ)]
        li = [li_ref[q] for q in range(NQ)]

        def step(k, c):
            gr, gi, ar, ai = c
            rows = pl.ds(pl.multiple_of((SSM_T - 1 - k) * NJ, NJ), NJ)
            s_r = [sr_ref[q, rows, :] for q in range(NQ)]
            s_i = [si_ref[q, rows, :] for q in range(NQ)]
            ar = tuple(ar[q] + gr[q] * s_r[q] + gi[q] * s_i[q] for q in range(NQ))
            ai = tuple(ai[q] + gi[q] * s_r[q] - gr[q] * s_i[q] for q in range(NQ))
            nr = tuple(gr_ref[q, rows, :] + lr[q] * gr[q] + li[q] * gi[q] for q in range(NQ))
            ni = tuple(gi_ref[q, rows, :] + lr[q] * gi[q] - li[q] * gr[q] for q in range(NQ))
            for q in range(NQ):
                gr_ref[q, rows, :] = nr[q]
                gi_ref[q, rows, :] = ni[q]
            return nr, ni, ar, ai

        zero = tuple(jnp.zeros((NJ, LANES), f32) for _ in range(NQ))
        init = (tuple(carry[0, q] for q in range(NQ)), tuple(carry[1, q] for q in range(NQ)), zero, zero)
        gr, gi, ar, ai = lax.fori_loop(0, SSM_T, step, init, unroll=8)
        for q in range(NQ):
            carry[0, q] = gr[q]
            carry[1, q] = gi[q]
            dlr[q] += ar[q]
            dli[q] += ai[q]
        uf = jnp.concatenate([u0_ref[...], u1_ref[...]], axis=1)
        dd_ref[...] += colsum(dyf * uf)
        for j in range(NJ):
            cols = slice(gw * j, gw * (j + 1))
            gjr, gji = _strided_get(gr_ref, j).astype(bf16), _strided_get(gi_ref, j).astype(bf16)
            du_ref[:, cols] = _dot(gjr, bdr[j], NT) + _dot(gji, bdi[j], NT) + d_ref[:, cols] * dyf[:, cols]
            uj = u_ref[:, cols]
            dbdr[j] += _dot(uj, gjr, TN)
            dbdi[j] += _dot(uj, gji, TN)

        @pl.when(i == nt - 1)
        def _():
            nb = NJ * SSM_STATE
            b_diag = (lax.broadcasted_iota(jnp.int32, (gw, nb), 0) // SSM_GROUP) == (lax.broadcasted_iota(jnp.int32, (gw, nb), 1) // SSM_STATE)
            c_diag = (lax.broadcasted_iota(jnp.int32, (nb, gw), 0) // SSM_STATE) == (lax.broadcasted_iota(jnp.int32, (nb, gw), 1) // SSM_GROUP)
            for j in range(NJ):
                for acc_ref, out in ((dbdr, fbr), (dbdi, fbi)):
                    m = jnp.where(b_diag, acc_ref[j], 0.0)
                    f = m[:, :LANES]
                    for q in range(1, nb // LANES):
                        f = f + m[:, LANES * q:LANES * (q + 1)]
                    out[j] = f + pltpu.roll(f, SSM_STATE, 1)
                for acc_ref, out in ((dcdr, fcr), (dcdi, fci)):
                    f = jnp.where(c_diag, acc_ref[j], 0.0)
                    for shift in (64, 32, 16):
                        f = f + pltpu.roll(f, shift, 1)
                    out[j] = f

    consts = (bd_re, bd_im, cd_re, cd_im, lam_re, lam_im, d_skip)
    acc = lambda a: jax.ShapeDtypeStruct(a.shape, f32)
    fb = jax.ShapeDtypeStruct((NJ, gw, LANES), f32)
    fc = jax.ShapeDtypeStruct((NJ, NJ * SSM_STATE, LANES), f32)
    outs = [jax.ShapeDtypeStruct((L, d_ssm), f32), fb, fb, fc, fc, acc(lam_re), acc(lam_im), acc(d_skip)]
    in_specs, operands, _ = _with_dep(
        [row(d_ssm)] * 4 + [row(half, u_cb), row(half, u_cb + 1), state, state] + [whole(a) for a in consts],
        [y, dz1, dz2, u_bf, proj, proj, s_re, s_im, *consts], dep)
    return pl.pallas_call(
        body, out_shape=outs, grid=(nt,),
        in_specs=in_specs, out_specs=[row(d_ssm)] + [whole(a) for a in outs[1:]],
        scratch_shapes=[pltpu.VMEM((NQ, SSM_T * NJ, LANES), f32), pltpu.VMEM((NQ, SSM_T * NJ, LANES), f32),
                        pltpu.VMEM((2, NQ, NJ, LANES), f32)] + [pltpu.VMEM(a.shape, f32) for a in (bd_re, bd_im, cd_re, cd_im)],
        name="ssm_bwd", compiler_params=_params("arbitrary"))(*operands)


def _cmul(ar, ai, br, bi):
    return ar * br - ai * bi, ar * bi + ai * br


def _disc(ar, ai, logdt):
    dt = jnp.exp(logdt)
    mag = jnp.exp(ar * dt)
    lr, li = mag * jnp.cos(ai * dt), mag * jnp.sin(ai * dt)
    den = ar * ar + ai * ai
    nr, ni = lr - 1.0, li
    fr, fi = (nr * ar + ni * ai) / den, (ni * ar - nr * ai) / den
    return dt, lr, li, den, fr, fi


def ssm_params(a_re, a_im, logdt_b, bt_re, bt_im, spread):
    def body(ar_ref, ai_ref, ld_ref, br_ref, bi_ref, sp_ref, lr_ref, li_ref, or_ref, oi_ref):
        _, lr, li, _, fr, fi = _disc(ar_ref[...], ai_ref[...], ld_ref[...])
        lr_ref[...] = lr
        li_ref[...] = li
        fre = jnp.dot(sp_ref[...], fr, precision=HIGHEST, preferred_element_type=f32)
        fie = jnp.dot(sp_ref[...], fi, precision=HIGHEST, preferred_element_type=f32)
        o_r, o_i = _cmul(fre, fie, br_ref[...], bi_ref[...])
        or_ref[...] = o_r
        oi_ref[...] = o_i

    g = jax.ShapeDtypeStruct(a_re.shape, f32)
    b = jax.ShapeDtypeStruct(bt_re.shape, f32)
    return pl.pallas_call(body, out_shape=[g, g, b, b], name="ssm_params",
                          compiler_params=_params())(a_re, a_im, logdt_b, bt_re, bt_im, spread)


def ssm_params_grad(a_re, a_im, logdt_b, bt_re, bt_im, spread, gather, dlam_re, dlam_im, dbt_re, dbt_im):
    def body(ar_ref, ai_ref, ld_ref, br_ref, bi_ref, sp_ref, ga_ref, glr_ref, gli_ref, gbr_ref, gbi_ref,
             dar_ref, dai_ref, dld_ref, dbr_ref, dbi_ref):
        ar, ai = ar_ref[...], ai_ref[...]
        dt, lr, li, den, fr, fi = _disc(ar, ai, ld_ref[...])
        hdot = functools.partial(jnp.dot, precision=HIGHEST, preferred_element_type=f32)
        fre, fie = hdot(sp_ref[...], fr), hdot(sp_ref[...], fi)
        gbr, gbi, br, bi = gbr_ref[...], gbi_ref[...], br_ref[...], bi_ref[...]
        dbr_ref[...], dbi_ref[...] = _cmul(fre, -fie, gbr, gbi)
        t_r, t_i = _cmul(br, -bi, gbr, gbi)
        gfr, gfi = hdot(ga_ref[...], t_r), hdot(ga_ref[...], t_i)
        iwr, iwi = ar / den, -ai / den
        x_r, x_i = _cmul(iwr, -iwi, gfr, gfi)
        glr, gli = glr_ref[...] + x_r, gli_ref[...] + x_i
        q_r, q_i = _cmul(fr, fi, iwr, iwi)
        gwr, gwi = _cmul(-q_r, q_i, gfr, gfi)
        y_r, y_i = _cmul(dt * lr, -dt * li, glr, gli)
        dar_ref[...] = gwr + y_r
        dai_ref[...] = gwi + y_i
        wl_r, wl_i = _cmul(ar, ai, lr, li)
        z_r, _ = _cmul(wl_r, -wl_i, glr, gli)
        dld_ref[...] = jnp.sum(z_r * dt, axis=1, keepdims=True)

    g = jax.ShapeDtypeStruct(a_re.shape, f32)
    b = jax.ShapeDtypeStruct(bt_re.shape, f32)
    return pl.pallas_call(body, out_shape=[g, g, jax.ShapeDtypeStruct((a_re.shape[0], 1), f32), b, b], name="ssm_params_grad",
                          compiler_params=_params())(a_re, a_im, logdt_b, bt_re, bt_im, spread, gather, dlam_re, dlam_im, dbt_re, dbt_im)


def _block_diag(t, rows, cols):
    G = t.shape[0]
    t = t.reshape(G // NJ, NJ, rows, cols)
    eye = jnp.eye(NJ, dtype=t.dtype)
    return jnp.einsum('jgrc,gh->jgrhc', t, eye).reshape(G // NJ, NJ * rows, NJ * cols)


def _state_layout(t):
    return t.reshape(NJ, NQ, LANES).transpose(1, 0, 2)


def _state_layout_inv(t, G, N):
    return t.transpose(1, 0, 2).reshape(G, N)


def _tiles2d(shape, budget_rows=128):
    rows, cols = shape
    tr = rows
    if rows > budget_rows:
        tr = budget_rows
        while rows % tr:
            tr -= SUBLANES
    return tr, cols


ADAM_TILE_BYTES = 3 << 19
ROW_ALIGN = 16


def _tile_rows(rows, row_bytes, target_bytes):
    tr = max(ROW_ALIGN, min(rows, target_bytes // row_bytes) // ROW_ALIGN * ROW_ALIGN)
    while rows % tr:
        tr -= ROW_ALIGN
    return tr


def _adam_update(w, g, m, v):
    c1 = 1.0 - ADAM_B1 ** ADAM_STEP
    c2 = 1.0 - ADAM_B2 ** ADAM_STEP
    nm = ADAM_B1 * m + (1.0 - ADAM_B1) * g
    nv = ADAM_B2 * v + (1.0 - ADAM_B2) * (g * g)
    delta = -ADAM_LR * ((nm / c1) / (jnp.sqrt(nv / c2) + ADAM_EPS) + ADAM_WD * w)
    return delta, nm, nv


def adamw_many(name, ws, gs, ms, vs):
    n = len(ws)

    def body(*refs):
        w, g, m, v = (refs[k * n:(k + 1) * n] for k in range(4))
        d, nm, nv = (refs[(4 + k) * n:(5 + k) * n] for k in range(3))
        for i in range(n):
            d[i][...], nm[i][...], nv[i][...] = _adam_update(w[i][...], g[i][...], m[i][...], v[i][...])

    o = [jax.ShapeDtypeStruct(a.shape, f32) for a in ws]
    outs = pl.pallas_call(body, out_shape=o * 3, name=name, compiler_params=_params())(*ws, *gs, *ms, *vs)
    return outs[:n], outs[n:2 * n], outs[2 * n:]


def adamw_halves(name, w, own, got, m, v, c_arr):
    h, cols = own.shape
    tr = _tile_rows(h, cols * 4, ADAM_TILE_BYTES)
    nh = h // tr

    def body(c_ref, w_ref, own_ref, got_ref, m_ref, v_ref, g_ref, d_ref, nm_ref, nv_ref):
        mine = (pl.program_id(0) // nh) == c_ref[0]
        g = jnp.where(mine, own_ref[...], got_ref[...])
        g_ref[...] = g
        d_ref[...], nm_ref[...], nv_ref[...] = _adam_update(w_ref[...], g, m_ref[...], v_ref[...])

    spec = pl.BlockSpec((tr, cols), lambda i, c: (i, 0))
    own_spec = pl.BlockSpec((tr, cols), lambda i, c: (jnp.where(i // nh == c[0], i % nh, 0), 0))
    got_spec = pl.BlockSpec((tr, cols), lambda i, c: (jnp.where(i // nh == c[0], 0, i % nh), 0))
    o = jax.ShapeDtypeStruct(w.shape, f32)
    grid_spec = pltpu.PrefetchScalarGridSpec(num_scalar_prefetch=1, grid=(2 * nh,),
                                             in_specs=[spec, own_spec, got_spec, spec, spec], out_specs=[spec] * 4)
    return pl.pallas_call(body, out_shape=[o, o, o, o], grid_spec=grid_spec, name=name,
                          compiler_params=_params("arbitrary"))(c_arr, w, own, got, m, v)


def pair_sum(name, g, got, c_arr):
    S, h, cols = got.shape
    tr, _ = _tiles2d((h, cols), 1024)
    nh = h // tr

    def body(c_ref, g_ref, o_ref, out_ref):
        out_ref[...] = (g_ref[...].astype(f32) + o_ref[...].astype(f32)).astype(out_ref.dtype)

    spec = pl.BlockSpec((None, tr, cols), lambda s, i, c: (s, i, 0))
    grid_spec = pltpu.PrefetchScalarGridSpec(
        num_scalar_prefetch=1, grid=(S, nh),
        in_specs=[pl.BlockSpec((None, tr, cols), lambda s, i, c: (s, c[0] * nh + i, 0)), spec], out_specs=spec)
    return pl.pallas_call(body, out_shape=jax.ShapeDtypeStruct(got.shape, g.dtype), grid_spec=grid_spec, name=name,
                          compiler_params=_params("parallel", "parallel"))(c_arr, g, got)


def chip_sum(name, pair, landed, mine_arr, dep=None):
    n_in, h, cols = landed.shape
    tr, _ = _tiles2d((h, cols), 256)

    def body(s_ref, p_ref, l_ref, *rest):
        acc = p_ref[...].astype(f32)
        for k in range(n_in):
            acc = acc + l_ref[k].astype(f32)
        rest[-1][...] = acc

    in_specs, operands, _ = _with_dep(
        [pl.BlockSpec((None, tr, cols), lambda i, s: (s[0], i, 0)), pl.BlockSpec((n_in, tr, cols), lambda i, s: (0, i, 0))],
        [pair, landed], dep)
    grid_spec = pltpu.PrefetchScalarGridSpec(num_scalar_prefetch=1, grid=(h // tr,), in_specs=in_specs,
                                             out_specs=pl.BlockSpec((tr, cols), lambda i, s: (i, 0)))
    return pl.pallas_call(body, out_shape=jax.ShapeDtypeStruct((h, cols), f32), grid_spec=grid_spec, name=name,
                          compiler_params=_params("parallel"))(mine_arr, *operands)


def into_slot(name, w, slot_arr, n_slots, dtype, dep=None):
    tr, cols = _tiles2d(w.shape, 256)

    def body(s_ref, w_ref, *rest):
        rest[-1][...] = w_ref[...].astype(dtype)

    in_specs, operands, _ = _with_dep([pl.BlockSpec((tr, cols), lambda i, s: (i, 0))], [w], dep)
    grid_spec = pltpu.PrefetchScalarGridSpec(num_scalar_prefetch=1, grid=(w.shape[0] // tr,), in_specs=in_specs,
                                             out_specs=pl.BlockSpec((None, tr, cols), lambda i, s: (s[0], i, 0)))
    return pl.pallas_call(body, out_shape=jax.ShapeDtypeStruct((n_slots,) + w.shape, dtype), grid_spec=grid_spec, name=name,
                          compiler_params=_params("parallel"))(slot_arr, *operands)


def sum_slots(name, t):
    S, rows, cols = t.shape
    tr, _ = _tiles2d((rows, cols), 256)

    def body(t_ref, o_ref):
        acc = t_ref[0]
        for s in range(1, S):
            acc = acc + t_ref[s]
        o_ref[...] = acc

    return pl.pallas_call(body, out_shape=jax.ShapeDtypeStruct((rows, cols), f32), grid=(rows // tr,),
                          in_specs=[pl.BlockSpec((S, tr, cols), lambda i: (0, i, 0))], out_specs=pl.BlockSpec((tr, cols), lambda i: (i, 0)),
                          name=name, compiler_params=_params("parallel"))(t)


def _place():
    x, y, c = lax.axis_index("x"), lax.axis_index("y"), lax.axis_index("c")
    return x, y, c


def _other_chips(x, y):
    return [(1 - x, y, 2 * (1 - x) + y), (x, 1 - y, 2 * x + 1 - y), (1 - x, 1 - y, 2 * (1 - x) + 1 - y)]


SEM = pl.BlockSpec(memory_space=pltpu.SEMAPHORE)
VM = pl.BlockSpec(memory_space=pltpu.VMEM)
DATAFLOW = pltpu.SideEffectType.DATAFLOW_SIDE_EFFECTING
TOKEN = jax.ShapeDtypeStruct((SUBLANES, LANES), f32)


def _gather_copy(buf, w, k, chip, c, mine, send, recv):
    px, py, _ = chip
    h = buf.shape[1] // 2
    half = buf.at[mine, pl.ds(c * h, h), :]
    return pltpu.make_async_remote_copy(src_ref=half, dst_ref=half, send_sem=send.at[3 * w + k], recv_sem=recv.at[3 * w + k],
                                        device_id=(px, py, c), device_id_type=MESH)


def _gather_landing(buf, w, k, chip, c, send, recv):
    px, py, s = chip
    h = buf.shape[1] // 2
    landed = buf.at[s, pl.ds(c * h, h), :]
    return pltpu.make_async_remote_copy(src_ref=landed, dst_ref=landed, send_sem=send.at[3 * w + k], recv_sem=recv.at[3 * w + k],
                                        device_id=(px, py, c), device_id_type=MESH)


def gather_start(name, bufs, groups, after, neighbours_only=()):
    nw, ng = len(bufs), len(groups)

    def body(*refs):
        outs = refs[nw + 1:]
        sems, dst = outs[:2 * ng], outs[2 * ng:2 * ng + nw]
        token = outs[2 * ng + nw]
        x, y, c = _place()
        mine = 2 * x + y
        for g, members in enumerate(groups):
            for i, w in enumerate(members):
                for k, chip in enumerate(_other_chips(x, y)[:2 if w in neighbours_only else 3]):
                    _gather_copy(dst[w], i, k, chip, c, mine, sems[2 * g], sems[2 * g + 1]).start()
        token[...] = jnp.zeros_like(token)

    sem_shapes = []
    for members in groups:
        sem_shapes += [pltpu.SemaphoreType.DMA((3 * len(members),))] * 2
    outs = pl.pallas_call(
        body, out_shape=sem_shapes + [jax.ShapeDtypeStruct(b.shape, b.dtype) for b in bufs] + [TOKEN],
        in_specs=[ANY] * (nw + 1), out_specs=[SEM] * (2 * ng) + [ANY] * nw + [VM],
        input_output_aliases={w: 2 * ng + w for w in range(nw)}, name=name,
        compiler_params=pltpu.CompilerParams(has_side_effects=DATAFLOW))(*bufs, after)
    return [(outs[2 * g], outs[2 * g + 1]) for g in range(ng)], list(outs[2 * ng:2 * ng + nw]), outs[2 * ng + nw]


def gather_wait(name, bufs, send, recv, after):
    nw = len(bufs)

    def body(*refs):
        src = refs[:nw]
        send_ref, recv_ref = refs[nw], refs[nw + 1]
        x, y, c = _place()
        mine = 2 * x + y
        for w in range(nw):
            for k, chip in enumerate(_other_chips(x, y)):
                _gather_copy(src[w], w, k, chip, c, mine, send_ref, recv_ref).wait_send()
                _gather_landing(src[w], w, k, chip, c, send_ref, recv_ref).wait_recv()

    return pl.pallas_call(
        body, out_shape=[jax.ShapeDtypeStruct(b.shape, b.dtype) for b in bufs],
        in_specs=[ANY] * nw + [SEM, SEM, ANY], out_specs=[ANY] * nw,
        input_output_aliases={w: w for w in range(nw)}, name=name,
        compiler_params=pltpu.CompilerParams(has_side_effects=DATAFLOW))(*bufs, send, recv, after)


def _relay_copy(buf, w, j, x, y, c, send, recv, landing):
    chips = _other_chips(x, y)
    px, py, _ = chips[j]
    h = buf.shape[1] // 2
    q = h // 2
    s = chips[2][2] if landing else chips[1 - j][2]
    part = buf.at[s, pl.ds(c * h + j * q, q), :]
    return pltpu.make_async_remote_copy(src_ref=part, dst_ref=part, send_sem=send.at[2 * w + j], recv_sem=recv.at[2 * w + j],
                                        device_id=(px, py, c), device_id_type=MESH)


def _early_pass(buf, nw, w, k, x, y, c, send, recv, landing):
    s = _other_chips(x, y)[k][2]
    h = buf.shape[1] // 2
    part = buf.at[s, pl.ds(((1 - c) if landing else c) * h, h), :]
    i = 2 * nw + 2 * w + k
    return pltpu.make_async_remote_copy(src_ref=part, dst_ref=part, send_sem=send.at[i], recv_sem=recv.at[i],
                                        device_id=(x, y, 1 - c), device_id_type=MESH)


def gather_relay(name, bufs, sems, more, after):
    nw, nm = len(bufs), len(more)
    ns = 4 if nm else 2

    def body(*refs):
        ins, outs = refs[:nw + nm + 2 * nw + 1], refs[nw + nm + 2 * nw + 1:]
        src, d_sems = ins[:nw], ins[nw + nm:nw + nm + 2 * nw]
        r_send, r_recv = outs[:2]
        m_send, m_recv = outs[2:ns] if nm else (None, None)
        dst, mdst, token = outs[ns:ns + nw], outs[ns + nw:ns + nw + nm], outs[ns + nw + nm]
        x, y, c = _place()
        mine = 2 * x + y
        chips = _other_chips(x, y)
        for w in range(nw):
            for k in range(2):
                _gather_copy(src[w], 0, k, chips[k], c, mine, d_sems[2 * w], d_sems[2 * w + 1]).wait_send()
                _gather_landing(src[w], 0, k, chips[k], c, d_sems[2 * w], d_sems[2 * w + 1]).wait_recv()
            for j in range(2):
                _relay_copy(dst[w], w, j, x, y, c, r_send, r_recv, False).start()
            for k in range(2):
                _early_pass(dst[w], nw, w, k, x, y, c, r_send, r_recv, False).start()
        for w in range(nm):
            for k, chip in enumerate(chips):
                _gather_copy(mdst[w], w, k, chip, c, mine, m_send, m_recv).start()
        token[...] = jnp.zeros_like(token)

    sem_shapes = [pltpu.SemaphoreType.DMA((4 * nw,))] * 2 + [pltpu.SemaphoreType.DMA((3 * nm,))] * (ns - 2)
    flat_sems = [s for pair in sems for s in pair]
    outs = pl.pallas_call(
        body, out_shape=sem_shapes + [jax.ShapeDtypeStruct(b.shape, b.dtype) for b in list(bufs) + list(more)] + [TOKEN],
        in_specs=[ANY] * (nw + nm) + [SEM] * (2 * nw) + [ANY], out_specs=[SEM] * ns + [ANY] * (nw + nm) + [VM],
        input_output_aliases={i: ns + i for i in range(nw + nm)}, name=name,
        compiler_params=pltpu.CompilerParams(has_side_effects=DATAFLOW))(*bufs, *more, *flat_sems, after)
    m_sems = (outs[2], outs[3]) if nm else None
    return outs[0], outs[1], m_sems, list(outs[ns:ns + nw]), list(outs[ns + nw:ns + nw + nm]), outs[ns + nw + nm]


def gather_wait_relay(name, bufs, r_send, r_recv, after):
    nw = len(bufs)

    def body(*refs):
        src = refs[:nw]
        send_ref, recv_ref = refs[nw], refs[nw + 1]
        x, y, c = _place()
        for w in range(nw):
            for j in range(2):
                _relay_copy(src[w], w, j, x, y, c, send_ref, recv_ref, False).wait_send()
                _relay_copy(src[w], w, j, x, y, c, send_ref, recv_ref, True).wait_recv()
                _early_pass(src[w], nw, w, j, x, y, c, send_ref, recv_ref, False).wait_send()
                _early_pass(src[w], nw, w, j, x, y, c, send_ref, recv_ref, True).wait_recv()

    return pl.pallas_call(
        body, out_shape=[jax.ShapeDtypeStruct(b.shape, b.dtype) for b in bufs],
        in_specs=[ANY] * nw + [SEM, SEM, ANY], out_specs=[ANY] * nw,
        input_output_aliases={w: w for w in range(nw)}, name=name,
        compiler_params=pltpu.CompilerParams(has_side_effects=DATAFLOW))(*bufs, r_send, r_recv, after)


def gather_forward(name, bufs, which=(0, 1, 2)):
    nw = len(bufs)

    def body(*refs):
        dst = refs[nw:2 * nw]
        send, recv = refs[2 * nw:]
        x, y, c = _place()
        sib = (x, y, 1 - c)
        barrier = pltpu.get_barrier_semaphore()
        pl.semaphore_signal(barrier, inc=1, device_id=sib, device_id_type=MESH)
        pl.semaphore_wait(barrier, 1)
        cps = []
        for w in range(nw):
            h = dst[w].shape[1] // 2
            for k in which:
                s = _other_chips(x, y)[k][2]
                landed = dst[w].at[s, pl.ds(c * h, h), :]
                cp = pltpu.make_async_remote_copy(src_ref=landed, dst_ref=landed, send_sem=send.at[w, k], recv_sem=recv.at[w, k],
                                                  device_id=sib, device_id_type=MESH)
                cp.start()
                cps.append(cp)
        for w in range(nw):
            h = dst[w].shape[1] // 2
            for k in which:
                s = _other_chips(x, y)[k][2]
                other = dst[w].at[s, pl.ds((1 - c) * h, h), :]
                pltpu.make_async_remote_copy(src_ref=other, dst_ref=other, send_sem=send.at[w, k], recv_sem=recv.at[w, k],
                                             device_id=sib, device_id_type=MESH).wait_recv()
        for cp in cps:
            cp.wait_send()

    sem = pltpu.SemaphoreType.DMA((nw, 3))
    return pl.pallas_call(
        body, out_shape=[jax.ShapeDtypeStruct(b.shape, b.dtype) for b in bufs],
        in_specs=[ANY] * nw, out_specs=[ANY] * nw, input_output_aliases={w: w for w in range(nw)},
        scratch_shapes=[sem, sem], name=name,
        compiler_params=pltpu.CompilerParams(has_side_effects=True, collective_id=SIBLING_PAIR))(*bufs)


def _scatter_copy(src, dst, w, k, chip, c, send, recv):
    px, py, s = chip
    return pltpu.make_async_remote_copy(src_ref=src.at[s], dst_ref=dst.at[k], send_sem=send.at[3 * w + k], recv_sem=recv.at[3 * w + k],
                                        device_id=(px, py, c), device_id_type=MESH)


def scatter_start(name, parts):
    nw = len(parts)
    lands = [pltpu.with_memory_space_constraint(lax.empty((N_CHIPS - 1,) + p.shape[1:], p.dtype), pltpu.HBM) for p in parts]

    def body(*refs):
        outs = refs[2 * nw:]
        send, recv = outs[0], outs[1]
        src, dst, token = outs[2:2 + nw], outs[2 + nw:2 + 2 * nw], outs[2 + 2 * nw]
        x, y, c = _place()
        for w in range(nw):
            for k, chip in enumerate(_other_chips(x, y)):
                _scatter_copy(src[w], dst[w], w, k, chip, c, send, recv).start()
        token[...] = jnp.zeros_like(token)

    sem = pltpu.SemaphoreType.DMA((3 * nw,))
    outs = pl.pallas_call(
        body, out_shape=[sem, sem] + [jax.ShapeDtypeStruct(p.shape, p.dtype) for p in parts]
        + [jax.ShapeDtypeStruct(l.shape, l.dtype) for l in lands] + [TOKEN],
        in_specs=[ANY] * (2 * nw), out_specs=[SEM, SEM] + [ANY] * (2 * nw) + [VM],
        input_output_aliases={i: 2 + i for i in range(2 * nw)}, name=name,
        compiler_params=pltpu.CompilerParams(has_side_effects=DATAFLOW))(*parts, *lands)
    return outs[0], outs[1], list(outs[2:2 + nw]), list(outs[2 + nw:2 + 2 * nw]), outs[2 + 2 * nw]


def scatter_wait(name, parts, lands, send, recv, after):
    nw = len(parts)

    def body(*refs):
        src, dst = refs[:nw], refs[nw:2 * nw]
        send_ref, recv_ref = refs[2 * nw], refs[2 * nw + 1]
        x, y, c = _place()
        for w in range(nw):
            for k, chip in enumerate(_other_chips(x, y)):
                cp = _scatter_copy(src[w], dst[w], w, k, chip, c, send_ref, recv_ref)
                cp.wait_send()
                cp.wait_recv()

    outs = pl.pallas_call(
        body, out_shape=[jax.ShapeDtypeStruct(a.shape, a.dtype) for a in list(parts) + list(lands)],
        in_specs=[ANY] * (2 * nw) + [SEM, SEM, ANY], out_specs=[ANY] * (2 * nw),
        input_output_aliases={i: i for i in range(2 * nw)}, name=name,
        compiler_params=pltpu.CompilerParams(has_side_effects=DATAFLOW))(*parts, *lands, send, recv, after)
    return list(outs[:nw]), list(outs[nw:])


SIBLING_PAIR = 0


def _sibling_copy(src, dst, w, c, half_rows, send, recv, sib):
    if half_rows:
        h = src.shape[1] // 2
        src = src.at[:, pl.ds((1 - c) * h, h), :]
    return pltpu.make_async_remote_copy(src_ref=src, dst_ref=dst, send_sem=send.at[w], recv_sem=recv.at[w],
                                        device_id=sib, device_id_type=MESH)


def _landing(shape, dtype):
    return pltpu.with_memory_space_constraint(lax.empty(shape, dtype), pltpu.HBM)


def sibling_start(name, srcs, half_rows):
    nw = len(srcs)
    lands = [_landing((s.shape[0], s.shape[1] // 2, s.shape[2]) if half_rows else s.shape, s.dtype) for s in srcs]

    def body(*refs):
        outs = refs[2 * nw:]
        send, recv = outs[0], outs[1]
        src, dst, token = outs[2:2 + nw], outs[2 + nw:2 + 2 * nw], outs[2 + 2 * nw]
        x, y, c = _place()
        barrier = pltpu.get_barrier_semaphore()
        pl.semaphore_signal(barrier, inc=1, device_id=(x, y, 1 - c), device_id_type=MESH)
        pl.semaphore_wait(barrier, 1)
        for w in range(nw):
            _sibling_copy(src[w], dst[w], w, c, half_rows, send, recv, (x, y, 1 - c)).start()
        token[...] = jnp.zeros_like(token)

    sem = pltpu.SemaphoreType.DMA((nw,))
    outs = pl.pallas_call(
        body, out_shape=[sem, sem] + [jax.ShapeDtypeStruct(a.shape, a.dtype) for a in list(srcs) + lands] + [TOKEN],
        in_specs=[ANY] * (2 * nw), out_specs=[SEM, SEM] + [ANY] * (2 * nw) + [VM],
        input_output_aliases={i: 2 + i for i in range(2 * nw)}, name=name,
        compiler_params=pltpu.CompilerParams(has_side_effects=DATAFLOW, collective_id=SIBLING_PAIR))(*srcs, *lands)
    return outs[0], outs[1], list(outs[2:2 + nw]), list(outs[2 + nw:2 + 2 * nw]), outs[2 + 2 * nw]


def sibling_wait(name, srcs, lands, send, recv, half_rows, after):
    nw = len(srcs)

    def body(*refs):
        src, dst = refs[:nw], refs[nw:2 * nw]
        send_ref, recv_ref = refs[2 * nw], refs[2 * nw + 1]
        x, y, c = _place()
        for w in range(nw):
            cp = _sibling_copy(src[w], dst[w], w, c, half_rows, send_ref, recv_ref, (x, y, 1 - c))
            cp.wait_send()
            cp.wait_recv()

    outs = pl.pallas_call(
        body, out_shape=[jax.ShapeDtypeStruct(a.shape, a.dtype) for a in list(srcs) + list(lands)],
        in_specs=[ANY] * (2 * nw) + [SEM, SEM, ANY], out_specs=[ANY] * (2 * nw),
        input_output_aliases={i: i for i in range(2 * nw)}, name=name,
        compiler_params=pltpu.CompilerParams(has_side_effects=DATAFLOW))(*srcs, *lands, send, recv, after)
    return list(outs[:nw]), list(outs[nw:])


def _peer(x, y, c, r):
    return (1 - x if r & 4 else x, 1 - y if r & 2 else y, 1 - c if r & 1 else c)


def _everyone_copy(buf, r, x, y, c, send, recv, landing):
    px, py, pc = _peer(x, y, c, r)
    slot = buf.at[4 * px + 2 * py + pc] if landing else buf.at[4 * x + 2 * y + c]
    return pltpu.make_async_remote_copy(src_ref=slot, dst_ref=slot, send_sem=send.at[r - 1], recv_sem=recv.at[r - 1],
                                        device_id=(px, py, pc), device_id_type=MESH)


def everyone_start(name, buf):
    def body(buf_in, send, recv, buf_ref, token):
        x, y, c = _place()
        for r in range(1, N_DEV):
            _everyone_copy(buf_ref, r, x, y, c, send, recv, False).start()
        token[...] = jnp.zeros_like(token)

    sem = pltpu.SemaphoreType.DMA((N_DEV - 1,))
    return pl.pallas_call(
        body, out_shape=[sem, sem, jax.ShapeDtypeStruct(buf.shape, buf.dtype), TOKEN],
        in_specs=[ANY], out_specs=[SEM, SEM, ANY, VM], input_output_aliases={0: 2}, name=name,
        compiler_params=pltpu.CompilerParams(has_side_effects=DATAFLOW))(buf)


def everyone_wait(name, buf, send, recv, after):
    def body(buf_ref, send_ref, recv_ref, after_ref, out_ref):
        x, y, c = _place()
        for r in range(1, N_DEV):
            _everyone_copy(buf_ref, r, x, y, c, send_ref, recv_ref, False).wait_send()
            _everyone_copy(buf_ref, r, x, y, c, send_ref, recv_ref, True).wait_recv()

    return pl.pallas_call(
        body, out_shape=jax.ShapeDtypeStruct(buf.shape, buf.dtype), in_specs=[ANY, SEM, SEM, ANY], out_specs=ANY,
        input_output_aliases={0: 0}, name=name,
        compiler_params=pltpu.CompilerParams(has_side_effects=DATAFLOW))(buf, send, recv, after)


def local_step(x, pos, tgt, small, d_in, get_w, put_g, first_dep=None, tick=lambda name, after: None):
    L, D = x.shape
    d_kv = N_KV_HEADS * HEAD_DIM
    d_ssm = small["d_skip"].shape[1]
    d_attn = d_in - 2 * d_kv - d_ssm
    big = {}
    G = d_ssm // SSM_GROUP
    N, P = SSM_STATE, SSM_GROUP
    gbf = bf16

    half_dim = HEAD_DIM // 2
    inv_freq = ROPE_THETA ** (-jnp.arange(half_dim, dtype=f32) / half_dim)
    inv_freq = jnp.tile(inv_freq, LANES // half_dim).reshape(1, LANES)
    sink_b = jnp.broadcast_to(small["sinks"].reshape(-1, 1), (small["sinks"].size, LANES))

    spread = jnp.repeat(jnp.eye(G, dtype=f32), P, axis=0)
    logdt_b = jnp.broadcast_to(small["log_dt"].reshape(G, 1), (G, N))
    bt_re = small["b_re"].reshape(G, N, P).transpose(0, 2, 1).reshape(G * P, N)
    bt_im = small["b_im"].reshape(G, N, P).transpose(0, 2, 1).reshape(G * P, N)
    a_re, a_im = small["a_re"].reshape(G, N), small["a_im"].reshape(G, N)
    lam_re, lam_im, bbt_re, bbt_im = ssm_params(a_re, a_im, logdt_b, bt_re, bt_im, spread)
    bd_re = _block_diag(bbt_re.reshape(G, P, N), P, N).astype(bf16)
    bd_im = _block_diag(bbt_im.reshape(G, P, N), P, N).astype(bf16)
    c_re = small["c_re"].reshape(G, P, N).transpose(0, 2, 1)
    c_im = small["c_im"].reshape(G, P, N).transpose(0, 2, 1)
    cd_re = _block_diag(c_re, N, P).astype(bf16)
    cd_im = _block_diag(c_im, N, P).astype(bf16)
    lam_re_l, lam_im_l = _state_layout(lam_re), _state_layout(lam_im)

    def k1(i, nt, xt, g):
        return (rms_fwd(xt, g),)
    xn = rowwise("pre_mix_norm", k1, L, [full(x)], [small["g_pre_mix"]], [(D, bf16)], dep=first_dep)[0]
    big["w_in"] = get_w("w_in", xn)
    proj = mm_nn("proj_in", xn, big["w_in"])
    qr, kk, vv, u_bf = qkv_prep(proj, pos, inv_freq, d_attn, d_kv)
    attn = attn_fwd(qr, kk, vv, sink_b)
    u_cb = (d_attn + 2 * d_kv) // (d_ssm // 2)
    token = tick("attn", attn)
    y, z_bf, s_re, s_im = ssm_fwd(u_bf, proj, u_cb, bd_re, bd_im, cd_re, cd_im, lam_re_l, lam_im_l, small["d_skip"], dep=token)
    token = tick("ssm", z_bf)
    big["w_glu"] = get_w("w_glu", z_bf)
    gl = mm_nn("glu_proj", z_bf, big["w_glu"], dep=token)

    def k6(i, nt, at, yt, glt, bg, ga, gs):
        ssm = gelu(yt) * sigmoid(glt + bg)
        return (jnp.concatenate([rms_fwd(at, ga), rms_fwd(ssm, gs)], axis=1),)
    mixed = rowwise("mix_norms", k6, L, [full(attn), full(y), full(gl)],
                    [small["b_glu"], small["g_attn_out"], small["g_ssm_out"]], [(d_attn + d_ssm, bf16)])[0]
    big["w_o"] = get_w("w_o", mixed)
    mix = mm_nn("proj_out", mixed, big["w_o"])

    def k7(i, nt, xt, mt, gpm, gpf):
        h = xt + rms_fwd(mt, gpm)
        return h, rms_fwd(h, gpf)
    h, hn = rowwise("post_mix", k7, L, [full(x), full(mix)], [small["g_post_mix"], small["g_pre_ffn"]], [(D, f32), (D, bf16)])
    big["w_gate"] = get_w("w_gate", hn)
    big["w_up"] = get_w("w_up", hn)
    hid_dg, hid_du, hid = ffn_hidden(hn, big["w_gate"], big["w_up"])
    d_ff_dim = hid.shape[1]
    big["w_down"] = get_w("w_down", hid)
    ff = mm_nn("ffn_down", hid, big["w_down"], tk=d_ff_dim // 2)

    def k9(i, nt, ht, fft, tt, g):
        out = ht + rms_fwd(fft, g)
        err = out - tt
        per_row = jnp.mean(err * err, axis=-1, keepdims=True)
        loss = 0.5 * jnp.sum(per_row) * jnp.where(_lane((1, LANES)) == 0, 1.0, 0.0)
        d_out = err * (1.0 / D)
        d_ff, dg = rms_bwd(fft, g, d_out)
        return d_out, d_ff, dg, loss
    d_out, d_ff, dg_post_ffn, loss = rowwise("loss_head", k9, L, [full(h), full(ff), full(tgt)], [small["g_post_ffn"]],
                                             [(D, f32), (D, bf16)], reds=[D, LANES])

    d_gt, d_up = ffn_hidden_grad(d_ff, big["w_down"], hid_dg, hid_du)
    token = put_g("w_down", mm_tn("dw_down", hid, d_ff, out_dtype=gbf, tm=d_ff_dim // N_CHIPS))
    d_hn = mm_nt_pair("d_hn", d_gt, big["w_gate"], d_up, big["w_up"], dep=token)
    token = put_g("w_gate", mm_tn("dw_gate", hn, d_gt, shards=N_CHIPS, out_dtype=gbf))
    token = put_g("w_up", mm_tn("dw_up", hn, d_up, shards=N_CHIPS, out_dtype=gbf, dep=token))

    def k11(i, nt, ht, da, do, mt, gpf, gpm):
        dh_n, dg_pf = rms_bwd(ht, gpf, da)
        dh = do + dh_n
        d_mix, dg_pm = rms_bwd(mt, gpm, dh)
        return dh, d_mix, dg_pf, dg_pm
    dh, d_mix, dg_pre_ffn, dg_post_mix = rowwise("post_mix_grad", k11, L, [full(h), full(d_hn), full(d_out), full(mix)],
                                                 [small["g_pre_ffn"], small["g_post_mix"]], [(D, f32), (D, bf16)], reds=[D, D], dep=token)
    d_mixed = mm_nt("d_mixed", d_mix, big["w_o"])
    token = put_g("w_o", mm_tn("dw_o", mixed, d_mix, out_dtype=gbf))

    def k12(i, nt, at, yt, glt, da_n, ds_n, bg, ga, gs):
        z = gelu(yt)
        sg = sigmoid(glt + bg)
        ssm = z * sg
        d_at, dga = rms_bwd(at, ga, da_n)
        d_ssm_t, dgs = rms_bwd(ssm, gs, ds_n)
        d_gl = d_ssm_t * z * sg * (1.0 - sg)
        return d_at, d_ssm_t * sg, d_gl, dga, dgs, colsum(d_gl)
    d_attn_o, dz1, d_gl, dg_attn, dg_ssm, db_glu = rowwise(
        "mix_norms_grad", k12, L, [full(attn), full(y), full(gl), (d_mixed, d_attn, 0, 0), (d_mixed, d_ssm, d_attn // d_ssm, 0)],
        [small["b_glu"], small["g_attn_out"], small["g_ssm_out"]], [(d_attn, f32), (d_ssm, f32), (d_ssm, bf16)],
        reds=[d_attn, d_ssm, d_ssm], dep=token)
    dz2 = mm_nt("d_glu_in", d_gl, big["w_glu"])
    token = put_g("w_glu", mm_tn("dw_glu", z_bf, d_gl, out_dtype=gbf))

    du, dbd_re, dbd_im, dcd_re, dcd_im, dlam_re_l, dlam_im_l, dd_skip = ssm_bwd(
        y, dz1, dz2, u_bf, proj, u_cb, s_re, s_im, bd_re, bd_im, cd_re, cd_im, lam_re_l, lam_im_l, small["d_skip"], dep=token)
    dq, dkk_c, dkk_p, dvv_c, dvv_p, dsink = attn_bwd(qr, kk, vv, sink_b, attn, d_attn_o)
    d_proj = qkv_grad(dq, dkk_c, dkk_p, dvv_c, dvv_p, du, pos, inv_freq)
    d_xn = mm_nt("d_xn", d_proj, big["w_in"])
    token = put_g("w_in", mm_tn("dw_in", xn, d_proj, shards=N_CHIPS, out_dtype=gbf))

    def k17(i, nt, xt, dxn, dht, g):
        dx, dg = rms_bwd(xt, g, dxn)
        return dht + dx, dg
    grad_x, dg_pre_mix = rowwise("pre_mix_grad", k17, L, [full(x), full(d_xn), full(dh)], [small["g_pre_mix"]],
                                 [(D, f32)], reds=[D], dep=token)

    gather = spread.T
    dbbt_re = dbd_re.reshape(G * P, LANES)[:, :N]
    dbbt_im = dbd_im.reshape(G * P, LANES)[:, :N]
    dc_re = dcd_re.reshape(G, N, LANES)[:, :, :P].transpose(0, 2, 1)
    dc_im = dcd_im.reshape(G, N, LANES)[:, :, :P].transpose(0, 2, 1)
    d_a_re, d_a_im, d_logdt, dbt_re, dbt_im = ssm_params_grad(
        a_re, a_im, logdt_b, bt_re, bt_im, spread, gather,
        _state_layout_inv(dlam_re_l, G, N), _state_layout_inv(dlam_im_l, G, N), dbbt_re, dbbt_im)
    q_per_kv = d_attn // HEAD_DIM // N_KV_HEADS
    small_grads = {
        "g_pre_mix": dg_pre_mix, "sinks": dsink[:, :q_per_kv, 0].reshape(1, -1),
        "a_re": d_a_re, "a_im": d_a_im, "log_dt": d_logdt.reshape(1, G),
        "b_re": dbt_re, "b_im": dbt_im,
        "c_re": dc_re, "c_im": dc_im,
        "d_skip": dd_skip, "b_glu": db_glu, "g_attn_out": dg_attn, "g_ssm_out": dg_ssm,
        "g_post_mix": dg_post_mix, "g_pre_ffn": dg_pre_ffn, "g_post_ffn": dg_post_ffn,
    }
    return loss, grad_x, small_grads


WEIGHTS = ['g_pre_mix', 'w_in', 'sinks', 'a_re', 'a_im', 'log_dt', 'b_re', 'b_im', 'c_re', 'c_im', 'd_skip', 'w_glu', 'b_glu',
           'g_attn_out', 'g_ssm_out', 'w_o', 'g_post_mix', 'g_pre_ffn', 'w_gate', 'w_up', 'w_down', 'g_post_ffn']
BIG = ['w_in', 'w_glu', 'w_o', 'w_gate', 'w_up', 'w_down']
COL_SHARDED = ['w_in', 'w_gate', 'w_up']
SMALL = [n for n in WEIGHTS if n not in BIG]
GATHER_GROUPS = [["w_in"], ["w_glu", "w_o"], ["w_gate", "w_up"], ["w_down"]]
REDUCE_GROUPS = [["w_down", "w_gate", "w_up"], ["w_o", "w_glu", "w_in"]]


PACK_ROWS = 256


def _pack(parts):
    flat = jnp.concatenate([p.reshape(-1) for p in parts])
    pad = (-flat.size) % (PACK_ROWS * LANES)
    return jnp.pad(flat, (0, pad)).reshape(-1, LANES)


TRANSPOSED_VIEW = ("b_re", "b_im")


def small_view(name, a):
    if name in TRANSPOSED_VIEW:
        a = a.transpose(0, 1, 3, 2)
    return a.reshape(-1, a.shape[-1])


def small_unview(name, p, shape):
    if name in TRANSPOSED_VIEW:
        return p.reshape(shape[0], shape[1], shape[3], shape[2]).transpose(0, 1, 3, 2)
    return p.reshape(shape)


def _unpack(packed, shapes):
    flat = packed.reshape(-1)
    out, off = [], 0
    for s in shapes:
        n = int(np.prod(s))
        out.append(flat[off:off + n].reshape(s))
        off += n
    return out


def kernel(x, positions, g_pre_mix, w_in, sinks, a_re, a_im, log_dt, b_re, b_im, c_re, c_im, d_skip, w_glu, b_glu, g_attn_out, g_ssm_out, w_o, g_post_mix, g_pre_ffn, w_gate, w_up, w_down, g_post_ffn, loss_target, m_g_pre_mix, m_w_in, m_sinks, m_a_re, m_a_im, m_log_dt, m_b_re, m_b_im, m_c_re, m_c_im, m_d_skip, m_w_glu, m_b_glu, m_g_attn_out, m_g_ssm_out, m_w_o, m_g_post_mix, m_g_pre_ffn, m_w_gate, m_w_up, m_w_down, m_g_post_ffn, v_g_pre_mix, v_w_in, v_sinks, v_a_re, v_a_im, v_log_dt, v_b_re, v_b_im, v_c_re, v_c_im, v_d_skip, v_w_glu, v_b_glu, v_g_attn_out, v_g_ssm_out, v_w_o, v_g_post_mix, v_g_pre_ffn, v_w_gate, v_w_up, v_w_down, v_g_post_ffn):
    args = dict(locals())
    w = {n: args[n] for n in WEIGHTS}
    m = {n: args["m_" + n] for n in WEIGHTS}
    v = {n: args["v_" + n] for n in WEIGHTS}
    L, D = x.shape[1], x.shape[2]

    ax, ay, ac = _place()
    mine_arr = (2 * ax + ay).astype(jnp.int32).reshape(1)
    c_arr = ac.astype(jnp.int32).reshape(1)

    me_arr = (4 * ax + 2 * ay + ac).astype(jnp.int32).reshape(1)

    bufs = {"w_in": into_slot("cast_w_in", w["w_in"][0], mine_arr, N_CHIPS, bf16)}
    (first_sems,), (bufs["w_in"],), token = gather_start("gather_start_in", [bufs["w_in"]], [[0]], mine_arr)
    sems = {"w_in": first_sems}
    for n in BIG[1:]:
        bufs[n] = into_slot("cast_" + n, w[n][0], mine_arr, N_CHIPS, bf16, dep=token)
    first = ["w_gate", "w_glu", "w_o", "w_up"]
    (sems["w_gate"], sems["w_glu"], sems["w_up"]), started, token = gather_start(
        "gather_start_rest", [bufs[n] for n in first], [[0], [1, 2], [3]], token, neighbours_only=(0, 3))
    bufs.update(zip(first, started))
    relays, ready = {}, set()

    def tick(name, after):
        n, more = ("w_gate", []) if name == "attn" else ("w_up", ["w_down"])
        r_send, r_recv, sems["w_down"], (bufs[n],), down, tok = gather_relay(
            "gather_relay_" + n, [bufs[n]], [sems[n]], [bufs[k] for k in more], after)
        bufs.update(zip(more, down))
        relays[n] = (r_send, r_recv)
        return tok

    def get_w(n, after):
        if n not in ready:
            members = [g for g in GATHER_GROUPS if n in g][0]
            if members[0] in relays:
                landed = [gather_wait_relay("gather_wait_" + k, [bufs[k]], *relays[k], after)[0] for k in members]
                which = (2,)
            else:
                landed = gather_wait("gather_wait_" + members[0], [bufs[k] for k in members], *sems[members[0]], after)
                which = (0, 1, 2)
            bufs.update(zip(members, gather_forward("gather_forward_" + members[0], landed, which)))
            ready.update(members)
        g = bufs[n]
        return g if n in COL_SHARDED else g.reshape(g.shape[0] * g.shape[1], g.shape[2])

    swaps, inflight = {}, []

    def put_g(n, g):
        g3 = g if n in COL_SHARDED else g.reshape(N_CHIPS, g.shape[0] // N_CHIPS, g.shape[1])
        swaps[n] = sibling_start("swap_start_" + n, [g3], True)
        for gi, members in enumerate(REDUCE_GROUPS):
            if n == members[-1]:
                last = swaps[n][4]
                pair = []
                for k in members:
                    send, recv, srcs, lands, _ = swaps[k]
                    (src,), (got,) = sibling_wait("swap_wait_" + k, srcs, lands, send, recv, True, last)
                    pair.append(pair_sum("pair_sum_" + k, src, got, c_arr))
                send, recv, parts, lands, tok = scatter_start("scatter_start_%d" % gi, pair)
                inflight.append((members, send, recv, parts, lands))
                return tok
        return swaps[n][4]

    small = {n: w[n].reshape(1, -1) for n in SMALL}
    pos = positions.reshape(L, 1).astype(f32)
    d_in = N_CHIPS * w["w_in"].shape[2]
    loss, grad_x, small_grads = local_step(x[0], pos, loss_target[0], small, d_in, get_w, put_g, first_dep=token, tick=tick)

    shapes = [w[n].shape for n in SMALL]
    blocks = into_slot("small_block", _pack([small_grads[n] for n in SMALL] + [loss]), me_arr, N_DEV, f32)
    small_send, small_recv, blocks, after = everyone_start("small_start", blocks)

    grads, delta, new_m, new_v = {}, {}, {}, {}
    for gi, (members, send, recv, parts, lands) in enumerate(inflight):
        parts, landed = scatter_wait("scatter_wait_%d" % gi, parts, lands, send, recv, after)
        joins, dep = [], None
        for k, p, t in zip(members, parts, landed):
            joins.append(sibling_start("join_start_" + k, [chip_sum("chip_sum_" + k, p, t, mine_arr, dep=dep)], False))
            dep = after = joins[-1][4]
        for n, (send, recv, srcs, lands, _) in zip(members, joins):
            (own,), (sib,) = sibling_wait("join_wait_" + n, srcs, lands, send, recv, False, after)
            g_, d_, m_, v_ = adamw_halves("adamw_" + n, w[n][0], own, sib, m[n][0], v[n][0], c_arr)
            grads[n], delta[n], new_m[n], new_v[n] = g_[None], d_[None], m_[None], v_[None]
            after = v_
    blocks = everyone_wait("small_wait", blocks, small_send, small_recv, after)
    small_sum = sum_slots("small_sum", blocks)
    *small_g, loss = _unpack(small_sum, [small_view(n, w[n]).shape for n in SMALL] + [loss.shape])
    loss = loss[0, 0]
    outs = adamw_many("adamw_small", [small_view(n, w[n]) for n in SMALL], small_g,
                      [small_view(n, m[n]) for n in SMALL], [small_view(n, v[n]) for n in SMALL])
    for t, parts in zip((grads, delta, new_m, new_v), (small_g,) + tuple(outs)):
        t.update({n: small_unview(n, p, w[n].shape) for n, p in zip(SMALL, parts)})

    return (loss, grad_x[None], *[grads[n] for n in WEIGHTS], *[delta[n] for n in WEIGHTS],
            *[new_m[n] for n in WEIGHTS], *[new_v[n] for n in WEIGHTS])
```

```python
import functools
import math

import jax
import jax.numpy as jnp
import numpy as np
from jax import lax
from jax.experimental import pallas as pl
from jax.experimental.pallas import tpu as pltpu

f32 = jnp.float32
bf16 = jnp.bfloat16
HIGHEST = lax.Precision.HIGHEST
MESH = pl.DeviceIdType.MESH

HEAD_DIM = 64
N_KV_HEADS = 4
ATTN_BLOCK = 128
ROPE_THETA = 10000.0
SSM_GROUP = 16
SSM_STATE = 64
RMS_EPS = 1e-6
LANES = 128
SUBLANES = 8
VMEM_LIMIT = 52 * 1024 * 1024
N_CHIPS = 4
N_DEV = 8
NEG = -1e30

ADAM_LR, ADAM_B1, ADAM_B2, ADAM_EPS, ADAM_WD, ADAM_STEP = 0.001, 0.9, 0.999, 1e-08, 0.01, 10

NN = (((1,), (0,)), ((), ()))
NT = (((1,), (1,)), ((), ()))
TN = (((0,), (0,)), ((), ()))


def _params(*sem):
    return pltpu.CompilerParams(dimension_semantics=sem or None, vmem_limit_bytes=VMEM_LIMIT)


def _dot(a, b, dims=NN):
    return lax.dot_general(a, b, dims, preferred_element_type=f32)


def _pick(dim, pref):
    t = min(dim, pref)
    while dim % t:
        t -= LANES
    assert t > 0, (dim, pref)
    return t


ANY = pl.BlockSpec(memory_space=pl.ANY)


def _with_dep(in_specs, operands, dep):
    if dep is None:
        return list(in_specs), list(operands), 0
    return list(in_specs) + [ANY], list(operands) + [dep], 1


def _mm_call(name, grid, in_specs, out_spec, out_shape, acc_shape, dims, operands, dep=None):
    nk = grid[2]
    in_specs, operands, n_dep = _with_dep(in_specs, operands, dep)

    def body_one(a_ref, b_ref, *rest):
        o_ref = rest[n_dep]
        o_ref[...] = _dot(a_ref[...], b_ref[...], dims).astype(o_ref.dtype)

    def body(a_ref, b_ref, *rest):
        o_ref, acc_ref = rest[n_dep], rest[n_dep + 1]
        k = pl.program_id(2)

        @pl.when(k == 0)
        def _():
            acc_ref[...] = _dot(a_ref[...], b_ref[...], dims)

        @pl.when((k > 0) & (k < nk - 1))
        def _():
            acc_ref[...] += _dot(a_ref[...], b_ref[...], dims)

        @pl.when(k == nk - 1)
        def _():
            o_ref[...] = (acc_ref[...] + _dot(a_ref[...], b_ref[...], dims)).astype(o_ref.dtype)

    return pl.pallas_call(
        body_one if nk == 1 else body, out_shape=out_shape, grid=grid, in_specs=in_specs, out_specs=out_spec,
        scratch_shapes=[] if nk == 1 else [pltpu.VMEM(acc_shape, f32)], name=name,
        compiler_params=_params("parallel", "parallel", "arbitrary"))(*operands)


def mm_nt_pair(name, a1, b1, a2, b2, tm=1024, tk=1024, dep=None):
    M = a1.shape[0]
    S, K, n = b1.shape
    tm, tko = _pick(M, tm), _pick(K, tk)
    nk = 2 * S

    def body(a1_ref, b1_ref, a2_ref, b2_ref, *rest):
        o_ref, acc_ref = rest[-2], rest[-1]
        k = pl.program_id(2)

        @pl.when(k == 0)
        def _():
            acc_ref[...] = _dot(a1_ref[...], b1_ref[...], NT)

        @pl.when((k > 0) & (k < S))
        def _():
            acc_ref[...] += _dot(a1_ref[...], b1_ref[...], NT)

        @pl.when((k >= S) & (k < nk - 1))
        def _():
            acc_ref[...] += _dot(a2_ref[...], b2_ref[...], NT)

        @pl.when(k == nk - 1)
        def _():
            o_ref[...] = acc_ref[...] + _dot(a2_ref[...], b2_ref[...], NT)

    first = lambda k: jnp.minimum(k, S - 1)
    second = lambda k: jnp.maximum(k - S, 0)
    in_specs = [pl.BlockSpec((tm, n), lambda i, j, k: (i, first(k))), pl.BlockSpec((None, tko, n), lambda i, j, k: (first(k), j, 0)),
                pl.BlockSpec((tm, n), lambda i, j, k: (i, second(k))), pl.BlockSpec((None, tko, n), lambda i, j, k: (second(k), j, 0))]
    in_specs, operands, _ = _with_dep(in_specs, (a1, b1, a2, b2), dep)
    return pl.pallas_call(
        body, out_shape=jax.ShapeDtypeStruct((M, K), f32), grid=(M // tm, K // tko, nk), in_specs=in_specs,
        out_specs=pl.BlockSpec((tm, tko), lambda i, j, k: (i, j)), scratch_shapes=[pltpu.VMEM((tm, tko), f32)], name=name,
        compiler_params=_params("parallel", "parallel", "arbitrary"))(*operands)


def mm_nn(name, a, b, out_dtype=f32, tm=1024, tn=1024, tk=2048, dep=None):
    M, K = a.shape
    tm, tk = _pick(M, tm), _pick(K, tk)
    if b.ndim == 3:
        S, _, n = b.shape
        tn = _pick(n, 2048)
        per = n // tn
        b_spec = pl.BlockSpec((None, tk, tn), lambda i, j, k: (j // per, k, j % per))
        N = S * n
    else:
        N = b.shape[1]
        tn = _pick(N, tn)
        b_spec = pl.BlockSpec((tk, tn), lambda i, j, k: (k, j))
    grid = (M // tm, N // tn, K // tk)
    return _mm_call(name, grid, [pl.BlockSpec((tm, tk), lambda i, j, k: (i, k)), b_spec],
                    pl.BlockSpec((tm, tn), lambda i, j, k: (i, j)), jax.ShapeDtypeStruct((M, N), out_dtype),
                    (tm, tn), NN, (a, b), dep)


def mm_nt(name, a, b, out_dtype=f32, tm=1024, tn=2048, tk=1024, dep=None):
    M, N = a.shape
    tm = _pick(M, tm)
    if b.ndim == 3:
        S, K, n = b.shape
        tr = _pick(n, 2048)
        per = n // tr
        tko = _pick(K, tk)
        b_spec = pl.BlockSpec((None, tko, tr), lambda i, j, k: (k // per, j, k % per))
    else:
        K = b.shape[0]
        tr = _pick(N, tn)
        tko = _pick(K, tk)
        b_spec = pl.BlockSpec((tko, tr), lambda i, j, k: (j, k))
    grid = (M // tm, K // tko, N // tr)
    return _mm_call(name, grid, [pl.BlockSpec((tm, tr), lambda i, j, k: (i, k)), b_spec],
                    pl.BlockSpec((tm, tko), lambda i, j, k: (i, j)), jax.ShapeDtypeStruct((M, K), out_dtype),
                    (tm, tko), NT, (a, b), dep)


def mm_tn(name, a, b, shards=None, out_dtype=f32, tm=1024, tn=1024, tl=2048, dep=None):
    L, K = a.shape
    N = b.shape[1]
    tl, tko = _pick(L, tl), _pick(K, tm)
    if shards:
        n = N // shards
        tn = _pick(n, 2048)
        per = n // tn
        o_spec = pl.BlockSpec((None, tko, tn), lambda i, j, k: (j // per, i, j % per))
        o_shape = jax.ShapeDtypeStruct((shards, K, n), out_dtype)
    else:
        tn = _pick(N, tn)
        o_spec = pl.BlockSpec((tko, tn), lambda i, j, k: (i, j))
        o_shape = jax.ShapeDtypeStruct((K, N), out_dtype)
    grid = (K // tko, N // tn, L // tl)
    return _mm_call(name, grid, [pl.BlockSpec((tl, tko), lambda i, j, k: (k, i)),
                                 pl.BlockSpec((tl, tn), lambda i, j, k: (k, j))],
                    o_spec, o_shape, (tko, tn), TN, (a, b), dep)


def ffn_hidden(hn, w_gate, w_up, tm=512):
    M, K = hn.shape
    S, _, n = w_gate.shape
    tm = _pick(M, tm)

    def body(a_ref, g_ref, u_ref, dg_ref, du_ref, hid_ref):
        a = a_ref[...]
        g = _dot(a, g_ref[...])
        u = _dot(a, u_ref[...])
        sg = sigmoid(g)
        act = g * sg
        dg_ref[...] = (u * (sg * (1.0 + g * (1.0 - sg)))).astype(bf16)
        du_ref[...] = act.astype(bf16)
        hid_ref[...] = (act * u).astype(bf16)

    w_spec = pl.BlockSpec((None, K, n), lambda s, i: (s, 0, 0))
    o_spec = pl.BlockSpec((tm, n), lambda s, i: (i, s))
    o = jax.ShapeDtypeStruct((M, S * n), bf16)
    return pl.pallas_call(
        body, out_shape=[o, o, o], grid=(S, M // tm), in_specs=[pl.BlockSpec((tm, K), lambda s, i: (i, 0)), w_spec, w_spec],
        out_specs=[o_spec, o_spec, o_spec], name="ffn_hidden", compiler_params=_params("parallel", "parallel"))(hn, w_gate, w_up)


def ffn_hidden_grad(d_ff, w_down, hid_dg, hid_du, tm=512):
    M, D = d_ff.shape
    F = w_down.shape[0]
    n = _pick(F // N_CHIPS, 2048)
    tm = _pick(M, tm)

    def body(a_ref, b_ref, pg_ref, pu_ref, dg_ref, du_ref):
        dh = _dot(a_ref[...], b_ref[...], NT)
        dg_ref[...] = (dh * pg_ref[...].astype(f32)).astype(bf16)
        du_ref[...] = (dh * pu_ref[...].astype(f32)).astype(bf16)

    t_spec = pl.BlockSpec((tm, n), lambda j, i: (i, j))
    o = jax.ShapeDtypeStruct((M, F), bf16)
    return pl.pallas_call(
        body, out_shape=[o, o], grid=(F // n, M // tm),
        in_specs=[pl.BlockSpec((tm, D), lambda j, i: (i, 0)), pl.BlockSpec((n, D), lambda j, i: (j, 0)), t_spec, t_spec],
        out_specs=[t_spec, t_spec], name="ffn_hidden_grad", compiler_params=_params("parallel", "parallel"))(d_ff, w_down, hid_dg, hid_du)


def rowwise(name, fn, L, rows, bcast, outs, reds=(), tr=256, dep=None):
    tr = min(tr, L)
    nt = L // tr
    n_rows, n_b, n_o = len(rows), len(bcast), len(outs)
    n_dep = 0 if dep is None else 1

    def body(*refs):
        i = pl.program_id(0)
        ins = [r[...] for r in refs[:n_rows + n_b]]
        res = fn(i, nt, *ins)
        o_refs = refs[n_rows + n_b + n_dep:]
        for k in range(n_o):
            o_refs[k][...] = res[k].astype(o_refs[k].dtype)
        if reds:
            @pl.when(i == 0)
            def _():
                for k in range(len(reds)):
                    o_refs[n_o + k][...] = jnp.zeros_like(o_refs[n_o + k])
            for k in range(len(reds)):
                o_refs[n_o + k][...] += res[n_o + k]

    def row_spec(width, cb, shift):
        if shift:
            return pl.BlockSpec((tr, width), lambda i: (jnp.minimum(i + shift, nt - 1), cb))
        return pl.BlockSpec((tr, width), lambda i: (i, cb))

    in_specs = [row_spec(w, cb, sh) for (_, w, cb, sh) in rows]
    in_specs += [pl.BlockSpec(b.shape, lambda i: (0, 0)) for b in bcast]
    out_specs = [pl.BlockSpec((tr, w), lambda i: (i, 0)) for (w, _) in outs]
    out_specs += [pl.BlockSpec((1, w), lambda i: (0, 0)) for w in reds]
    out_shape = [jax.ShapeDtypeStruct((L, w), dt) for (w, dt) in outs]
    out_shape += [jax.ShapeDtypeStruct((1, w), f32) for w in reds]
    in_specs, operands, _ = _with_dep(in_specs, [r[0] for r in rows] + list(bcast), dep)
    return pl.pallas_call(
        body, out_shape=out_shape, grid=(nt,), in_specs=in_specs, out_specs=out_specs, name=name,
        compiler_params=_params("arbitrary"))(*operands)


def full(a):
    return (a, a.shape[1], 0, 0)


def colsum(v):
    return jnp.sum(v, axis=0, keepdims=True)


def rms_fwd(x, g):
    r = lax.rsqrt(jnp.mean(x * x, axis=-1, keepdims=True) + RMS_EPS)
    return x * r * g


def rms_bwd(x, g, dy):
    r = lax.rsqrt(jnp.mean(x * x, axis=-1, keepdims=True) + RMS_EPS)
    xh = x * r
    dyg = dy * g
    dx = r * (dyg - xh * jnp.mean(dyg * xh, axis=-1, keepdims=True))
    return dx, colsum(dy * xh)


GELU_C = math.sqrt(2.0 / math.pi)


def gelu(y):
    return y * (0.5 * (1.0 + jnp.tanh(GELU_C * (y + 0.044715 * (y * y * y)))))


def gelu_grad(y):
    t = jnp.tanh(GELU_C * (y + 0.044715 * (y * y * y)))
    return 0.5 * (1.0 + t) + 0.5 * y * (1.0 - t * t) * (GELU_C * (1.0 + 3 * 0.044715 * (y * y)))


def sigmoid(v):
    return 1.0 / (1.0 + jnp.exp(-v))


def _lane(shape):
    return lax.broadcasted_iota(jnp.int32, shape, 1)


def _rot_chunk(t, cos, sin_signed):
    first = (_lane(t.shape) % HEAD_DIM) < (HEAD_DIM // 2)
    partner = jnp.where(first, pltpu.roll(t, LANES - HEAD_DIM // 2, 1), pltpu.roll(t, HEAD_DIM // 2, 1))
    return t * cos + partner * sin_signed


def _cos_sin(pos, inv_freq, inverse):
    ang = pos * inv_freq
    cos, sin = jnp.cos(ang), jnp.sin(ang)
    first = (_lane(ang.shape) % HEAD_DIM) < (HEAD_DIM // 2)
    sign = jnp.where(first, -1.0, 1.0) * (-1.0 if inverse else 1.0)
    return cos, sin * sign


def _dup_head(chunk, odd):
    low = _lane(chunk.shape) < HEAD_DIM
    x = jnp.where(low != odd, chunk, 0.0)
    return x + pltpu.roll(x, HEAD_DIM, 1)


def _chunks(v):
    return [v[:, LANES * c:LANES * (c + 1)] for c in range(v.shape[1] // LANES)]


def qkv_prep(proj, pos, inv_freq, d_attn, d_kv):
    L = proj.shape[0]
    d_ssm = proj.shape[1] - d_attn - 2 * d_kv
    half = d_ssm // 2
    scale = 1.0 / math.sqrt(HEAD_DIM)

    def fn(i, nt, q, k, v, u0, u1, p, invf):
        cos, sin = _cos_sin(p, invf, False)
        qr = jnp.concatenate([_rot_chunk(c, cos, sin) for c in _chunks(q)], axis=1) * scale
        kr = [_rot_chunk(c, cos, sin) for c in _chunks(k)]
        kk = jnp.concatenate([_dup_head(c, odd) for c in kr for odd in (False, True)], axis=1)
        vv = jnp.concatenate([_dup_head(c, odd) for c in _chunks(v) for odd in (False, True)], axis=1)
        return qr, kk, vv, jnp.concatenate([u0, u1], axis=1)

    u_cb = (d_attn + 2 * d_kv) // half
    return rowwise("qkv_prep", fn, L,
                   [(proj, d_attn, 0, 0), (proj, d_kv, d_attn // d_kv, 0), (proj, d_kv, d_attn // d_kv + 1, 0),
                    (proj, half, u_cb, 0), (proj, half, u_cb + 1, 0), full(pos)],
                   [inv_freq], [(d_attn, bf16), (2 * d_kv, bf16), (2 * d_kv, bf16), (d_ssm, bf16)])


def qkv_grad(dq, dkk_c, dkk_p, dvv_c, dvv_p, du, pos, inv_freq):
    L, d_attn = dq.shape
    d_kv = dkk_c.shape[1] // 2
    scale = 1.0 / math.sqrt(HEAD_DIM)

    def fold(cur, prev, i, nt):
        t = cur + jnp.where(i < nt - 1, prev, 0.0)
        out = []
        for c in range(t.shape[1] // (2 * LANES)):
            even, odd = t[:, 2 * c * LANES:(2 * c + 1) * LANES], t[:, (2 * c + 1) * LANES:(2 * c + 2) * LANES]
            even, odd = even + pltpu.roll(even, HEAD_DIM, 1), odd + pltpu.roll(odd, HEAD_DIM, 1)
            out.append(jnp.where(_lane(even.shape) < HEAD_DIM, even, odd))
        return out

    def fn(i, nt, dq_t, kc, kp, vc, vp, du_t, p, invf):
        cos, sin = _cos_sin(p, invf, True)
        dq_o = jnp.concatenate([_rot_chunk(c, cos, sin) for c in _chunks(dq_t)], axis=1) * scale
        dk_o = jnp.concatenate([_rot_chunk(c, cos, sin) for c in fold(kc, kp, i, nt)], axis=1)
        dv_o = jnp.concatenate(fold(vc, vp, i, nt), axis=1)
        return (jnp.concatenate([dq_o, dk_o, dv_o, du_t], axis=1),)

    return rowwise("qkv_grad", fn, L,
                   [full(dq), full(dkk_c), (dkk_p, 2 * d_kv, 0, 1), full(dvv_c), (dvv_p, 2 * d_kv, 0, 1), full(du), full(pos)],
                   [inv_freq], [(d_attn + 2 * d_kv + du.shape[1], bf16)], tr=ATTN_BLOCK)[0]


def _attn_specs(L):
    nb = L // ATTN_BLOCK
    B = ATTN_BLOCK
    q_spec = lambda width: pl.BlockSpec((B, width), lambda n: (n, 0))
    prev = lambda width: pl.BlockSpec((B, width), lambda n: (jnp.maximum(n - 1, 0), 0))
    return nb, q_spec, prev


def _attn_mask(n):
    B = ATTN_BLOCK
    row = lax.broadcasted_iota(jnp.int32, (B, 2 * B), 0)
    col = lax.broadcasted_iota(jnp.int32, (B, 2 * B), 1)
    return ((col < B) & (col > row) & (n > 0)) | ((col >= B) & (row >= col - B))


def _attn_probs(qm, kcat, sink, mask):
    s = jnp.where(mask, _dot(qm, kcat, NT), NEG)
    m = jnp.maximum(jnp.max(s, axis=1, keepdims=True), sink)
    p, ps = jnp.exp(s - m), jnp.exp(sink - m)
    inv = 1.0 / (jnp.sum(p, axis=1, keepdims=True) + ps)
    return p, inv, ps


def _attn_heads(q_ref, s_ref, h, q_per_kv):
    low = _lane((ATTN_BLOCK, LANES)) < HEAD_DIM
    heads = []
    for pr in range(h * q_per_kv // 2, (h + 1) * q_per_kv // 2):
        q2 = q_ref[:, LANES * pr:LANES * (pr + 1)]
        for odd in (False, True):
            mine = low != odd
            sink = jnp.max(s_ref[2 * pr + int(odd):2 * pr + int(odd) + 1, :], axis=1, keepdims=True)
            heads.append((pr, mine, jnp.where(mine, q2, jnp.zeros_like(q2)), sink))
    return low, heads


def _kv_block(prev_ref, cur_ref, h):
    return jnp.concatenate([prev_ref[:, LANES * h:LANES * (h + 1)], cur_ref[:, LANES * h:LANES * (h + 1)]], axis=0)


def attn_fwd(qr, kk, vv, sink_b):
    L, d_attn = qr.shape
    nb, q_spec, prev = _attn_specs(L)
    d_kk = kk.shape[1]
    n_kv = d_kk // LANES
    q_per_kv = d_attn // HEAD_DIM // n_kv

    def body(q_ref, kc_ref, kp_ref, vc_ref, vp_ref, s_ref, o_ref):
        mask = _attn_mask(pl.program_id(0))
        for h in range(n_kv):
            kcat, vcat = _kv_block(kp_ref, kc_ref, h), _kv_block(vp_ref, vc_ref, h)
            low, heads = _attn_heads(q_ref, s_ref, h, q_per_kv)
            probs = [_attn_probs(qm, kcat, sink, mask) for (_, _, qm, sink) in heads]
            outs = [_dot(p.astype(bf16), vcat) * inv for (p, inv, _) in probs]
            for i in range(0, len(heads), 2):
                pr = heads[i][0]
                o_ref[:, LANES * pr:LANES * (pr + 1)] = jnp.where(low, outs[i], outs[i + 1])

    return pl.pallas_call(
        body, out_shape=jax.ShapeDtypeStruct((L, d_attn), f32), grid=(nb,),
        in_specs=[q_spec(d_attn), q_spec(d_kk), prev(d_kk), q_spec(d_kk), prev(d_kk), pl.BlockSpec(sink_b.shape, lambda n: (0, 0))],
        out_specs=q_spec(d_attn), name="attn_fwd", compiler_params=_params("arbitrary"))(qr, kk, kk, vv, vv, sink_b)


def attn_bwd(qr, kk, vv, sink_b, attn, d_attn_out):
    L, d_attn = qr.shape
    nb, q_spec, prev = _attn_specs(L)
    d_kk = kk.shape[1]
    n_kv = d_kk // LANES
    q_per_kv = d_attn // HEAD_DIM // n_kv

    def body(q_ref, kc_ref, kp_ref, vc_ref, vp_ref, s_ref, o_ref, do_ref, dq_ref, dkc_ref, dkp_ref, dvc_ref, dvp_ref, ds_ref):
        n = pl.program_id(0)
        B = ATTN_BLOCK
        mask = _attn_mask(n)
        srow = lax.broadcasted_iota(jnp.int32, (SUBLANES, LANES), 0)

        @pl.when(n == 0)
        def _():
            ds_ref[...] = jnp.zeros_like(ds_ref)

        for h in range(n_kv):
            kcat, vcat = _kv_block(kp_ref, kc_ref, h), _kv_block(vp_ref, vc_ref, h)
            low, heads = _attn_heads(q_ref, s_ref, h, q_per_kv)
            probs = [_attn_probs(qm, kcat, sink, mask) for (_, _, qm, sink) in heads]
            dk = jnp.zeros((2 * B, LANES), f32)
            dv = dk
            dsink = jnp.zeros((SUBLANES, LANES), f32)
            dqs = []
            for i, ((pr, mine, qm, _), (p, inv, ps)) in enumerate(zip(heads, probs)):
                do2 = do_ref[:, LANES * pr:LANES * (pr + 1)]
                delta = jnp.sum(jnp.where(mine, do2 * o_ref[:, LANES * pr:LANES * (pr + 1)], 0.0), axis=1, keepdims=True)
                dob = jnp.where(mine, do2, 0.0).astype(bf16)
                p = p * inv
                ds = (p * (_dot(dob, vcat, NT) - delta)).astype(bf16)
                dqs.append(_dot(ds, kcat))
                dk = dk + _dot(ds, qm, TN)
                dv = dv + _dot(p.astype(bf16), dob, TN)
                dsink = dsink + jnp.where(srow == i, -jnp.sum(ps * inv * delta), 0.0)
            for i in range(0, len(heads), 2):
                pr = heads[i][0]
                dq_ref[:, LANES * pr:LANES * (pr + 1)] = jnp.where(low, dqs[i], dqs[i + 1])
            cols = slice(LANES * h, LANES * (h + 1))
            dkp_ref[:, cols] = dk[:B]
            dkc_ref[:, cols] = dk[B:]
            dvp_ref[:, cols] = dv[:B]
            dvc_ref[:, cols] = dv[B:]
            ds_ref[h] += dsink

    kv_shape = jax.ShapeDtypeStruct(kk.shape, f32)
    ds_shape = (n_kv, SUBLANES, LANES)
    return pl.pallas_call(
        body,
        out_shape=[jax.ShapeDtypeStruct((L, d_attn), f32), kv_shape, kv_shape, kv_shape, kv_shape, jax.ShapeDtypeStruct(ds_shape, f32)],
        grid=(nb,),
        in_specs=[q_spec(d_attn), q_spec(d_kk), prev(d_kk), q_spec(d_kk), prev(d_kk), pl.BlockSpec(sink_b.shape, lambda n: (0, 0)),
                  q_spec(d_attn), q_spec(d_attn)],
        out_specs=[q_spec(d_attn)] + [q_spec(d_kk)] * 4 + [pl.BlockSpec(ds_shape, lambda n: (0, 0, 0))],
        name="attn_bwd", compiler_params=_params("arbitrary"))(qr, kk, kk, vv, vv, sink_b, attn, d_attn_out)


SSM_T = 128
NQ = SUBLANES * SSM_STATE // LANES
NJ = SUBLANES


def _strided_put(ref, j, val):
    for q in range(NQ):
        ref.at[q][pl.ds(j, SSM_T, stride=NJ), :] = val[:, LANES * q:LANES * (q + 1)]


def _strided_get(ref, j):
    return jnp.concatenate([ref.at[q][pl.ds(j, SSM_T, stride=NJ), :] for q in range(NQ)], axis=1)


def _ssm_specs(L, rev):
    nt = L // SSM_T
    idx = (lambda i: nt - 1 - i) if rev else (lambda i: i)
    row = lambda w, cb=0: pl.BlockSpec((SSM_T, w), lambda i: (idx(i), cb))
    state = pl.BlockSpec((NQ, SSM_T * NJ, LANES), lambda i: (0, idx(i), 0))
    whole = lambda a: pl.BlockSpec(a.shape, lambda i: (0,) * a.ndim)
    return nt, row, state, whole


def ssm_fwd(u_bf, proj, u_cb, bd_re, bd_im, cd_re, cd_im, lam_re, lam_im, d_skip, dep=None):
    L, d_ssm = u_bf.shape
    nt, row, state, whole = _ssm_specs(L, False)
    half = d_ssm // 2
    gw = d_ssm // NJ
    n_dep = 0 if dep is None else 1

    def body(u_ref, u0_ref, u1_ref, bdr, bdi, cdr, cdi, lr_ref, li_ref, d_ref, *rest):
        y_ref, z_ref, sr_ref, si_ref, carry = rest[n_dep:]
        i = pl.program_id(0)

        @pl.when(i == 0)
        def _():
            carry[...] = jnp.zeros_like(carry)

        for j in range(NJ):
            uj = u_ref[:, gw * j:gw * (j + 1)]
            _strided_put(sr_ref, j, _dot(uj, bdr[j]))
            _strided_put(si_ref, j, _dot(uj, bdi[j]))
        lr = [lr_ref[q] for q in range(NQ)]
        li = [li_ref[q] for q in range(NQ)]

        def step(t, s):
            sr, si = s
            rows = pl.ds(pl.multiple_of(t * NJ, NJ), NJ)
            nr = tuple(lr[q] * sr[q] - li[q] * si[q] + sr_ref[q, rows, :] for q in range(NQ))
            ni = tuple(lr[q] * si[q] + li[q] * sr[q] + si_ref[q, rows, :] for q in range(NQ))
            for q in range(NQ):
                sr_ref[q, rows, :] = nr[q]
                si_ref[q, rows, :] = ni[q]
            return nr, ni

        init = (tuple(carry[0, q] for q in range(NQ)), tuple(carry[1, q] for q in range(NQ)))
        sr, si = lax.fori_loop(0, SSM_T, step, init, unroll=8)
        for q in range(NQ):
            carry[0, q] = sr[q]
            carry[1, q] = si[q]
        uf = jnp.concatenate([u0_ref[...], u1_ref[...]], axis=1)
        for j in range(NJ):
            cols = slice(gw * j, gw * (j + 1))
            yj = _dot(_strided_get(sr_ref, j).astype(bf16), cdr[j]) - _dot(_strided_get(si_ref, j).astype(bf16), cdi[j])
            yj = yj + d_ref[:, cols] * uf[:, cols]
            y_ref[:, cols] = yj
            z_ref[:, cols] = gelu(yj).astype(bf16)

    s_shape = jax.ShapeDtypeStruct((NQ, L * NJ, LANES), f32)
    consts = (bd_re, bd_im, cd_re, cd_im, lam_re, lam_im, d_skip)
    in_specs, operands, _ = _with_dep([row(d_ssm), row(half, u_cb), row(half, u_cb + 1)] + [whole(a) for a in consts],
                                      [u_bf, proj, proj, *consts], dep)
    return pl.pallas_call(
        body, out_shape=[jax.ShapeDtypeStruct((L, d_ssm), f32), jax.ShapeDtypeStruct((L, d_ssm), bf16), s_shape, s_shape], grid=(nt,),
        in_specs=in_specs, out_specs=[row(d_ssm), row(d_ssm), state, state],
        scratch_shapes=[pltpu.VMEM((2, NQ, NJ, LANES), f32)], name="ssm_fwd",
        compiler_params=_params("arbitrary"))(*operands)


def ssm_bwd(y, dz1, dz2, u_bf, proj, u_cb, s_re, s_im, bd_re, bd_im, cd_re, cd_im, lam_re, lam_im, d_skip, dep=None):
    L, d_ssm = y.shape
    nt, row, state, whole = _ssm_specs(L, True)
    half = d_ssm // 2
    gw = d_ssm // NJ
    n_dep = 0 if dep is None else 1

    def body(y_ref, dz1_ref, dz2_ref, u_ref, u0_ref, u1_ref, sr_ref, si_ref, bdr, bdi, cdr, cdi, lr_ref, li_ref, d_ref, *rest):
        du_ref, fbr, fbi, fcr, fci, dlr, dli, dd_ref, gr_ref, gi_ref, carry, dbdr, dbdi, dcdr, dcdi = rest[n_dep:]
        i = pl.program_id(0)

        @pl.when(i == 0)
        def _():
            carry[...] = jnp.zeros_like(carry)
            for r in (dbdr, dbdi, dcdr, dcdi, dlr, dli, dd_ref):
                r[...] = jnp.zeros_like(r)

        dyf = (dz1_ref[...] + dz2_ref[...]) * gelu_grad(y_ref[...])
        dyb = dyf.astype(bf16)
        for j in range(NJ):
            dyj = dyb[:, gw * j:gw * (j + 1)]
            _strided_put(gr_ref, j, _dot(dyj, cdr[j], NT))
            _strided_put(gi_ref, j, -_dot(dyj, cdi[j], NT))
            dcdr[j] += _dot(_strided_get(sr_ref, j).astype(bf16), dyj, TN)
            dcdi[j] -= _dot(_strided_get(si_ref, j).astype(bf16), dyj, TN)
        lr = [lr_ref[q] for q in range(NQ)]
        li = [li_ref[q] for q in range(NQ)]

        def step(k, c):
            gr, gi, ar, ai = c
            rows = pl.ds(pl.multiple_of((SSM_T - 1 - k) * NJ, NJ), NJ)
            s_r = [sr_ref[q, rows, :] for q in range(NQ)]
            s_i = [si_ref[q, rows, :] for q in range(NQ)]
            ar = tuple(ar[q] + gr[q] * s_r[q] + gi[q] * s_i[q] for q in range(NQ))
            ai = tuple(ai[q] + gi[q] * s_r[q] - gr[q] * s_i[q] for q in range(NQ))
            nr = tuple(gr_ref[q, rows, :] + lr[q] * gr[q] + li[q] * gi[q] for q in range(NQ))
            ni = tuple(gi_ref[q, rows, :] + lr[q] * gi[q] - li[q] * gr[q] for q in range(NQ))
            for q in range(NQ):
                gr_ref[q, rows, :] = nr[q]
                gi_ref[q, rows, :] = ni[q]
            return nr, ni, ar, ai

        zero = tuple(jnp.zeros((NJ, LANES), f32) for _ in range(NQ))
        init = (tuple(carry[0, q] for q in range(NQ)), tuple(carry[1, q] for q in range(NQ)), zero, zero)
        gr, gi, ar, ai = lax.fori_loop(0, SSM_T, step, init, unroll=8)
        for q in range(NQ):
            carry[0, q] = gr[q]
            carry[1, q] = gi[q]
            dlr[q] += ar[q]
            dli[q] += ai[q]
        uf = jnp.concatenate([u0_ref[...], u1_ref[...]], axis=1)
        dd_ref[...] += colsum(dyf * uf)
        for j in range(NJ):
            cols = slice(gw * j, gw * (j + 1))
            gjr, gji = _strided_get(gr_ref, j).astype(bf16), _strided_get(gi_ref, j).astype(bf16)
            du_ref[:, cols] = _dot(gjr, bdr[j], NT) + _dot(gji, bdi[j], NT) + d_ref[:, cols] * dyf[:, cols]
            uj = u_ref[:, cols]
            dbdr[j] += _dot(uj, gjr, TN)
            dbdi[j] += _dot(uj, gji, TN)

        @pl.when(i == nt - 1)
        def _():
            nb = NJ * SSM_STATE
            b_diag = (lax.broadcasted_iota(jnp.int32, (gw, nb), 0) // SSM_GROUP) == (lax.broadcasted_iota(jnp.int32, (gw, nb), 1) // SSM_STATE)
            c_diag = (lax.broadcasted_iota(jnp.int32, (nb, gw), 0) // SSM_STATE) == (lax.broadcasted_iota(jnp.int32, (nb, gw), 1) // SSM_GROUP)
            for j in range(NJ):
                for acc_ref, out in ((dbdr, fbr), (dbdi, fbi)):
                    m = jnp.where(b_diag, acc_ref[j], 0.0)
                    f = m[:, :LANES]
                    for q in range(1, nb // LANES):
                        f = f + m[:, LANES * q:LANES * (q + 1)]
                    out[j] = f + pltpu.roll(f, SSM_STATE, 1)
                for acc_ref, out in ((dcdr, fcr), (dcdi, fci)):
                    f = jnp.where(c_diag, acc_ref[j], 0.0)
                    for shift in (64, 32, 16):
                        f = f + pltpu.roll(f, shift, 1)
                    out[j] = f

    consts = (bd_re, bd_im, cd_re, cd_im, lam_re, lam_im, d_skip)
    acc = lambda a: jax.ShapeDtypeStruct(a.shape, f32)
    fb = jax.ShapeDtypeStruct((NJ, gw, LANES), f32)
    fc = jax.ShapeDtypeStruct((NJ, NJ * SSM_STATE, LANES), f32)
    outs = [jax.ShapeDtypeStruct((L, d_ssm), f32), fb, fb, fc, fc, acc(lam_re), acc(lam_im), acc(d_skip)]
    in_specs, operands, _ = _with_dep(
        [row(d_ssm)] * 4 + [row(half, u_cb), row(half, u_cb + 1), state, state] + [whole(a) for a in consts],
        [y, dz1, dz2, u_bf, proj, proj, s_re, s_im, *consts], dep)
    return pl.pallas_call(
        body, out_shape=outs, grid=(nt,),
        in_specs=in_specs, out_specs=[row(d_ssm)] + [whole(a) for a in outs[1:]],
        scratch_shapes=[pltpu.VMEM((NQ, SSM_T * NJ, LANES), f32), pltpu.VMEM((NQ, SSM_T * NJ, LANES), f32),
                        pltpu.VMEM((2, NQ, NJ, LANES), f32)] + [pltpu.VMEM(a.shape, f32) for a in (bd_re, bd_im, cd_re, cd_im)],
        name="ssm_bwd", compiler_params=_params("arbitrary"))(*operands)


def _cmul(ar, ai, br, bi):
    return ar * br - ai * bi, ar * bi + ai * br


def _disc(ar, ai, logdt):
    dt = jnp.exp(logdt)
    mag = jnp.exp(ar * dt)
    lr, li = mag * jnp.cos(ai * dt), mag * jnp.sin(ai * dt)
    den = ar * ar + ai * ai
    nr, ni = lr - 1.0, li
    fr, fi = (nr * ar + ni * ai) / den, (ni * ar - nr * ai) / den
    return dt, lr, li, den, fr, fi


def ssm_params(a_re, a_im, logdt_b, bt_re, bt_im, spread):
    def body(ar_ref, ai_ref, ld_ref, br_ref, bi_ref, sp_ref, lr_ref, li_ref, or_ref, oi_ref):
        _, lr, li, _, fr, fi = _disc(ar_ref[...], ai_ref[...], ld_ref[...])
        lr_ref[...] = lr
        li_ref[...] = li
        fre = jnp.dot(sp_ref[...], fr, precision=HIGHEST, preferred_element_type=f32)
        fie = jnp.dot(sp_ref[...], fi, precision=HIGHEST, preferred_element_type=f32)
        o_r, o_i = _cmul(fre, fie, br_ref[...], bi_ref[...])
        or_ref[...] = o_r
        oi_ref[...] = o_i

    g = jax.ShapeDtypeStruct(a_re.shape, f32)
    b = jax.ShapeDtypeStruct(bt_re.shape, f32)
    return pl.pallas_call(body, out_shape=[g, g, b, b], name="ssm_params",
                          compiler_params=_params())(a_re, a_im, logdt_b, bt_re, bt_im, spread)


def ssm_params_grad(a_re, a_im, logdt_b, bt_re, bt_im, spread, gather, dlam_re, dlam_im, dbt_re, dbt_im):
    def body(ar_ref, ai_ref, ld_ref, br_ref, bi_ref, sp_ref, ga_ref, glr_ref, gli_ref, gbr_ref, gbi_ref,
             dar_ref, dai_ref, dld_ref, dbr_ref, dbi_ref):
        ar, ai = ar_ref[...], ai_ref[...]
        dt, lr, li, den, fr, fi = _disc(ar, ai, ld_ref[...])
        hdot = functools.partial(jnp.dot, precision=HIGHEST, preferred_element_type=f32)
        fre, fie = hdot(sp_ref[...], fr), hdot(sp_ref[...], fi)
        gbr, gbi, br, bi = gbr_ref[...], gbi_ref[...], br_ref[...], bi_ref[...]
        dbr_ref[...], dbi_ref[...] = _cmul(fre, -fie, gbr, gbi)
        t_r, t_i = _cmul(br, -bi, gbr, gbi)
        gfr, gfi = hdot(ga_ref[...], t_r), hdot(ga_ref[...], t_i)
        iwr, iwi = ar / den, -ai / den
        x_r, x_i = _cmul(iwr, -iwi, gfr, gfi)
        glr, gli = glr_ref[...] + x_r, gli_ref[...] + x_i
        q_r, q_i = _cmul(fr, fi, iwr, iwi)
        gwr, gwi = _cmul(-q_r, q_i, gfr, gfi)
        y_r, y_i = _cmul(dt * lr, -dt * li, glr, gli)
        dar_ref[...] = gwr + y_r
        dai_ref[...] = gwi + y_i
        wl_r, wl_i = _cmul(ar, ai, lr, li)
        z_r, _ = _cmul(wl_r, -wl_i, glr, gli)
        dld_ref[...] = jnp.sum(z_r * dt, axis=1, keepdims=True)

    g = jax.ShapeDtypeStruct(a_re.shape, f32)
    b = jax.ShapeDtypeStruct(bt_re.shape, f32)
    return pl.pallas_call(body, out_shape=[g, g, jax.ShapeDtypeStruct((a_re.shape[0], 1), f32), b, b], name="ssm_params_grad",
                          compiler_params=_params())(a_re, a_im, logdt_b, bt_re, bt_im, spread, gather, dlam_re, dlam_im, dbt_re, dbt_im)


def _block_diag(t, rows, cols):
    G = t.shape[0]
    t = t.reshape(G // NJ, NJ, rows, cols)
    eye = jnp.eye(NJ, dtype=t.dtype)
    return jnp.einsum('jgrc,gh->jgrhc', t, eye).reshape(G // NJ, NJ * rows, NJ * cols)


def _state_layout(t):
    return t.reshape(NJ, NQ, LANES).transpose(1, 0, 2)


def _state_layout_inv(t, G, N):
    return t.transpose(1, 0, 2).reshape(G, N)


def _tiles2d(shape, budget_rows=128):
    rows, cols = shape
    tr = rows
    if rows > budget_rows:
        tr = budget_rows
        while rows % tr:
            tr -= SUBLANES
    return tr, cols


ADAM_TILE_BYTES = 3 << 19
ROW_ALIGN = 16


def _tile_rows(rows, row_bytes, target_bytes):
    tr = max(ROW_ALIGN, min(rows, target_bytes // row_bytes) // ROW_ALIGN * ROW_ALIGN)
    while rows % tr:
        tr -= ROW_ALIGN
    return tr


def _adam_update(w, g, m, v):
    c1 = 1.0 - ADAM_B1 ** ADAM_STEP
    c2 = 1.0 - ADAM_B2 ** ADAM_STEP
    nm = ADAM_B1 * m + (1.0 - ADAM_B1) * g
    nv = ADAM_B2 * v + (1.0 - ADAM_B2) * (g * g)
    delta = -ADAM_LR * ((nm / c1) / (jnp.sqrt(nv / c2) + ADAM_EPS) + ADAM_WD * w)
    return delta, nm, nv


def adamw_many(name, ws, gs, ms, vs):
    n = len(ws)

    def body(*refs):
        w, g, m, v = (refs[k * n:(k + 1) * n] for k in range(4))
        d, nm, nv = (refs[(4 + k) * n:(5 + k) * n] for k in range(3))
        for i in range(n):
            d[i][...], nm[i][...], nv[i][...] = _adam_update(w[i][...], g[i][...], m[i][...], v[i][...])

    o = [jax.ShapeDtypeStruct(a.shape, f32) for a in ws]
    outs = pl.pallas_call(body, out_shape=o * 3, name=name, compiler_params=_params())(*ws, *gs, *ms, *vs)
    return outs[:n], outs[n:2 * n], outs[2 * n:]


def adamw_halves(name, w, own, got, m, v, c_arr):
    h, cols = own.shape
    tr = _tile_rows(h, cols * 4, ADAM_TILE_BYTES)
    nh = h // tr

    def body(c_ref, w_ref, own_ref, got_ref, m_ref, v_ref, g_ref, d_ref, nm_ref, nv_ref):
        mine = (pl.program_id(0) // nh) == c_ref[0]
        g = jnp.where(mine, own_ref[...], got_ref[...])
        g_ref[...] = g
        d_ref[...], nm_ref[...], nv_ref[...] = _adam_update(w_ref[...], g, m_ref[...], v_ref[...])

    spec = pl.BlockSpec((tr, cols), lambda i, c: (i, 0))
    own_spec = pl.BlockSpec((tr, cols), lambda i, c: (jnp.where(i // nh == c[0], i % nh, 0), 0))
    got_spec = pl.BlockSpec((tr, cols), lambda i, c: (jnp.where(i // nh == c[0], 0, i % nh), 0))
    o = jax.ShapeDtypeStruct(w.shape, f32)
    grid_spec = pltpu.PrefetchScalarGridSpec(num_scalar_prefetch=1, grid=(2 * nh,),
                                             in_specs=[spec, own_spec, got_spec, spec, spec], out_specs=[spec] * 4)
    return pl.pallas_call(body, out_shape=[o, o, o, o], grid_spec=grid_spec, name=name,
                          compiler_params=_params("arbitrary"))(c_arr, w, own, got, m, v)


def pair_sum(name, g, got, c_arr):
    S, h, cols = got.shape
    tr, _ = _tiles2d((h, cols), 1024)
    nh = h // tr

    def body(c_ref, g_ref, o_ref, out_ref):
        out_ref[...] = (g_ref[...].astype(f32) + o_ref[...].astype(f32)).astype(out_ref.dtype)

    spec = pl.BlockSpec((None, tr, cols), lambda s, i, c: (s, i, 0))
    grid_spec = pltpu.PrefetchScalarGridSpec(
        num_scalar_prefetch=1, grid=(S, nh),
        in_specs=[pl.BlockSpec((None, tr, cols), lambda s, i, c: (s, c[0] * nh + i, 0)), spec], out_specs=spec)
    return pl.pallas_call(body, out_shape=jax.ShapeDtypeStruct(got.shape, g.dtype), grid_spec=grid_spec, name=name,
                          compiler_params=_params("parallel", "parallel"))(c_arr, g, got)


def chip_sum(name, pair, landed, mine_arr, dep=None):
    n_in, h, cols = landed.shape
    tr, _ = _tiles2d((h, cols), 256)

    def body(s_ref, p_ref, l_ref, *rest):
        acc = p_ref[...].astype(f32)
        for k in range(n_in):
            acc = acc + l_ref[k].astype(f32)
        rest[-1][...] = acc

    in_specs, operands, _ = _with_dep(
        [pl.BlockSpec((None, tr, cols), lambda i, s: (s[0], i, 0)), pl.BlockSpec((n_in, tr, cols), lambda i, s: (0, i, 0))],
        [pair, landed], dep)
    grid_spec = pltpu.PrefetchScalarGridSpec(num_scalar_prefetch=1, grid=(h // tr,), in_specs=in_specs,
                                             out_specs=pl.BlockSpec((tr, cols), lambda i, s: (i, 0)))
    return pl.pallas_call(body, out_shape=jax.ShapeDtypeStruct((h, cols), f32), grid_spec=grid_spec, name=name,
                          compiler_params=_params("parallel"))(mine_arr, *operands)


def into_slot(name, w, slot_arr, n_slots, dtype, dep=None):
    tr, cols = _tiles2d(w.shape, 256)

    def body(s_ref, w_ref, *rest):
        rest[-1][...] = w_ref[...].astype(dtype)

    in_specs, operands, _ = _with_dep([pl.BlockSpec((tr, cols), lambda i, s: (i, 0))], [w], dep)
    grid_spec = pltpu.PrefetchScalarGridSpec(num_scalar_prefetch=1, grid=(w.shape[0] // tr,), in_specs=in_specs,
                                             out_specs=pl.BlockSpec((None, tr, cols), lambda i, s: (s[0], i, 0)))
    return pl.pallas_call(body, out_shape=jax.ShapeDtypeStruct((n_slots,) + w.shape, dtype), grid_spec=grid_spec, name=name,
                          compiler_params=_params("parallel"))(slot_arr, *operands)


def sum_slots(name, t):
    S, rows, cols = t.shape
    tr, _ = _tiles2d((rows, cols), 256)

    def body(t_ref, o_ref):
        acc = t_ref[0]
        for s in range(1, S):
            acc = acc + t_ref[s]
        o_ref[...] = acc

    return pl.pallas_call(body, out_shape=jax.ShapeDtypeStruct((rows, cols), f32), grid=(rows // tr,),
                          in_specs=[pl.BlockSpec((S, tr, cols), lambda i: (0, i, 0))], out_specs=pl.BlockSpec((tr, cols), lambda i: (i, 0)),
                          name=name, compiler_params=_params("parallel"))(t)


def _place():
    x, y, c = lax.axis_index("x"), lax.axis_index("y"), lax.axis_index("c")
    return x, y, c


def _other_chips(x, y):
    return [(1 - x, y, 2 * (1 - x) + y), (x, 1 - y, 2 * x + 1 - y), (1 - x, 1 - y, 2 * (1 - x) + 1 - y)]


SEM = pl.BlockSpec(memory_space=pltpu.SEMAPHORE)
VM = pl.BlockSpec(memory_space=pltpu.VMEM)
DATAFLOW = pltpu.SideEffectType.DATAFLOW_SIDE_EFFECTING
TOKEN = jax.ShapeDtypeStruct((SUBLANES, LANES), f32)


def _gather_copy(buf, w, k, chip, c, mine, send, recv):
    px, py, _ = chip
    h = buf.shape[1] // 2
    half = buf.at[mine, pl.ds(c * h, h), :]
    return pltpu.make_async_remote_copy(src_ref=half, dst_ref=half, send_sem=send.at[3 * w + k], recv_sem=recv.at[3 * w + k],
                                        device_id=(px, py, c), device_id_type=MESH)


def _gather_landing(buf, w, k, chip, c, send, recv):
    px, py, s = chip
    h = buf.shape[1] // 2
    landed = buf.at[s, pl.ds(c * h, h), :]
    return pltpu.make_async_remote_copy(src_ref=landed, dst_ref=landed, send_sem=send.at[3 * w + k], recv_sem=recv.at[3 * w + k],
                                        device_id=(px, py, c), device_id_type=MESH)


def gather_start(name, bufs, groups, after, neighbours_only=()):
    nw, ng = len(bufs), len(groups)

    def body(*refs):
        outs = refs[nw + 1:]
        sems, dst = outs[:2 * ng], outs[2 * ng:2 * ng + nw]
        token = outs[2 * ng + nw]
        x, y, c = _place()
        mine = 2 * x + y
        for g, members in enumerate(groups):
            for i, w in enumerate(members):
                for k, chip in enumerate(_other_chips(x, y)[:2 if w in neighbours_only else 3]):
                    _gather_copy(dst[w], i, k, chip, c, mine, sems[2 * g], sems[2 * g + 1]).start()
        token[...] = jnp.zeros_like(token)

    sem_shapes = []
    for members in groups:
        sem_shapes += [pltpu.SemaphoreType.DMA((3 * len(members),))] * 2
    outs = pl.pallas_call(
        body, out_shape=sem_shapes + [jax.ShapeDtypeStruct(b.shape, b.dtype) for b in bufs] + [TOKEN],
        in_specs=[ANY] * (nw + 1), out_specs=[SEM] * (2 * ng) + [ANY] * nw + [VM],
        input_output_aliases={w: 2 * ng + w for w in range(nw)}, name=name,
        compiler_params=pltpu.CompilerParams(has_side_effects=DATAFLOW))(*bufs, after)
    return [(outs[2 * g], outs[2 * g + 1]) for g in range(ng)], list(outs[2 * ng:2 * ng + nw]), outs[2 * ng + nw]


def gather_wait(name, bufs, send, recv, after):
    nw = len(bufs)

    def body(*refs):
        src = refs[:nw]
        send_ref, recv_ref = refs[nw], refs[nw + 1]
        x, y, c = _place()
        mine = 2 * x + y
        for w in range(nw):
            for k, chip in enumerate(_other_chips(x, y)):
                _gather_copy(src[w], w, k, chip, c, mine, send_ref, recv_ref).wait_send()
                _gather_landing(src[w], w, k, chip, c, send_ref, recv_ref).wait_recv()

    return pl.pallas_call(
        body, out_shape=[jax.ShapeDtypeStruct(b.shape, b.dtype) for b in bufs],
        in_specs=[ANY] * nw + [SEM, SEM, ANY], out_specs=[ANY] * nw,
        input_output_aliases={w: w for w in range(nw)}, name=name,
        compiler_params=pltpu.CompilerParams(has_side_effects=DATAFLOW))(*bufs, send, recv, after)


def _relay_copy(buf, w, j, x, y, c, send, recv, landing):
    chips = _other_chips(x, y)
    px, py, _ = chips[j]
    h = buf.shape[1] // 2
    q = h // 2
    s = chips[2][2] if landing else chips[1 - j][2]
    part = buf.at[s, pl.ds(c * h + j * q, q), :]
    return pltpu.make_async_remote_copy(src_ref=part, dst_ref=part, send_sem=send.at[2 * w + j], recv_sem=recv.at[2 * w + j],
                                        device_id=(px, py, c), device_id_type=MESH)


def _early_pass(buf, nw, w, k, x, y, c, send, recv, landing):
    s = _other_chips(x, y)[k][2]
    h = buf.shape[1] // 2
    part = buf.at[s, pl.ds(((1 - c) if landing else c) * h, h), :]
    i = 2 * nw + 2 * w + k
    return pltpu.make_async_remote_copy(src_ref=part, dst_ref=part, send_sem=send.at[i], recv_sem=recv.at[i],
                                        device_id=(x, y, 1 - c), device_id_type=MESH)


def gather_relay(name, bufs, sems, more, after):
    nw, nm = len(bufs), len(more)
    ns = 4 if nm else 2

    def body(*refs):
        ins, outs = refs[:nw + nm + 2 * nw + 1], refs[nw + nm + 2 * nw + 1:]
        src, d_sems = ins[:nw], ins[nw + nm:nw + nm + 2 * nw]
        r_send, r_recv = outs[:2]
        m_send, m_recv = outs[2:ns] if nm else (None, None)
        dst, mdst, token = outs[ns:ns + nw], outs[ns + nw:ns + nw + nm], outs[ns + nw + nm]
        x, y, c = _place()
        mine = 2 * x + y
        chips = _other_chips(x, y)
        for w in range(nw):
            for k in range(2):
                _gather_copy(src[w], 0, k, chips[k], c, mine, d_sems[2 * w], d_sems[2 * w + 1]).wait_send()
                _gather_landing(src[w], 0, k, chips[k], c, d_sems[2 * w], d_sems[2 * w + 1]).wait_recv()
            for j in range(2):
                _relay_copy(dst[w], w, j, x, y, c, r_send, r_recv, False).start()
            for k in range(2):
                _early_pass(dst[w], nw, w, k, x, y, c, r_send, r_recv, False).start()
        for w in range(nm):
            for k, chip in enumerate(chips):
                _gather_copy(mdst[w], w, k, chip, c, mine, m_send, m_recv).start()
        token[...] = jnp.zeros_like(token)

    sem_shapes = [pltpu.SemaphoreType.DMA((4 * nw,))] * 2 + [pltpu.SemaphoreType.DMA((3 * nm,))] * (ns - 2)
    flat_sems = [s for pair in sems for s in pair]
    outs = pl.pallas_call(
        body, out_shape=sem_shapes + [jax.ShapeDtypeStruct(b.shape, b.dtype) for b in list(bufs) + list(more)] + [TOKEN],
        in_specs=[ANY] * (nw + nm) + [SEM] * (2 * nw) + [ANY], out_specs=[SEM] * ns + [ANY] * (nw + nm) + [VM],
        input_output_aliases={i: ns + i for i in range(nw + nm)}, name=name,
        compiler_params=pltpu.CompilerParams(has_side_effects=DATAFLOW))(*bufs, *more, *flat_sems, after)
    m_sems = (outs[2], outs[3]) if nm else None
    return outs[0], outs[1], m_sems, list(outs[ns:ns + nw]), list(outs[ns + nw:ns + nw + nm]), outs[ns + nw + nm]


def gather_wait_relay(name, bufs, r_send, r_recv, after):
    nw = len(bufs)

    def body(*refs):
        src = refs[:nw]
        send_ref, recv_ref = refs[nw], refs[nw + 1]
        x, y, c = _place()
        for w in range(nw):
            for j in range(2):
                _relay_copy(src[w], w, j, x, y, c, send_ref, recv_ref, False).wait_send()
                _relay_copy(src[w], w, j, x, y, c, send_ref, recv_ref, True).wait_recv()
                _early_pass(src[w], nw, w, j, x, y, c, send_ref, recv_ref, False).wait_send()
                _early_pass(src[w], nw, w, j, x, y, c, send_ref, recv_ref, True).wait_recv()

    return pl.pallas_call(
        body, out_shape=[jax.ShapeDtypeStruct(b.shape, b.dtype) for b in bufs],
        in_specs=[ANY] * nw + [SEM, SEM, ANY], out_specs=[ANY] * nw,
        input_output_aliases={w: w for w in range(nw)}, name=name,
        compiler_params=pltpu.CompilerParams(has_side_effects=DATAFLOW))(*bufs, r_send, r_recv, after)


def gather_forward(name, bufs, which=(0, 1, 2)):
    nw = len(bufs)

    def body(*refs):
        dst = refs[nw:2 * nw]
        send, recv = refs[2 * nw:]
        x, y, c = _place()
        sib = (x, y, 1 - c)
        barrier = pltpu.get_barrier_semaphore()
        pl.semaphore_signal(barrier, inc=1, device_id=sib, device_id_type=MESH)
        pl.semaphore_wait(barrier, 1)
        cps = []
        for w in range(nw):
            h = dst[w].shape[1] // 2
            for k in which:
                s = _other_chips(x, y)[k][2]
                landed = dst[w].at[s, pl.ds(c * h, h), :]
                cp = pltpu.make_async_remote_copy(src_ref=landed, dst_ref=landed, send_sem=send.at[w, k], recv_sem=recv.at[w, k],
                                                  device_id=sib, device_id_type=MESH)
                cp.start()
                cps.append(cp)
        for w in range(nw):
            h = dst[w].shape[1] // 2
            for k in which:
                s = _other_chips(x, y)[k][2]
                other = dst[w].at[s, pl.ds((1 - c) * h, h), :]
                pltpu.make_async_remote_copy(src_ref=other, dst_ref=other, send_sem=send.at[w, k], recv_sem=recv.at[w, k],
                                             device_id=sib, device_id_type=MESH).wait_recv()
        for cp in cps:
            cp.wait_send()

    sem = pltpu.SemaphoreType.DMA((nw, 3))
    return pl.pallas_call(
        body, out_shape=[jax.ShapeDtypeStruct(b.shape, b.dtype) for b in bufs],
        in_specs=[ANY] * nw, out_specs=[ANY] * nw, input_output_aliases={w: w for w in range(nw)},
        scratch_shapes=[sem, sem], name=name,
        compiler_params=pltpu.CompilerParams(has_side_effects=True, collective_id=SIBLING_PAIR))(*bufs)


def _scatter_copy(src, dst, w, k, chip, c, send, recv):
    px, py, s = chip
    return pltpu.make_async_remote_copy(src_ref=src.at[s], dst_ref=dst.at[k], send_sem=send.at[3 * w + k], recv_sem=recv.at[3 * w + k],
                                        device_id=(px, py, c), device_id_type=MESH)


def scatter_start(name, parts):
    nw = len(parts)
    lands = [pltpu.with_memory_space_constraint(lax.empty((N_CHIPS - 1,) + p.shape[1:], p.dtype), pltpu.HBM) for p in parts]

    def body(*refs):
        outs = refs[2 * nw:]
        send, recv = outs[0], outs[1]
        src, dst, token = outs[2:2 + nw], outs[2 + nw:2 + 2 * nw], outs[2 + 2 * nw]
        x, y, c = _place()
        for w in range(nw):
            for k, chip in enumerate(_other_chips(x, y)):
                _scatter_copy(src[w], dst[w], w, k, chip, c, send, recv).start()
        token[...] = jnp.zeros_like(token)

    sem = pltpu.SemaphoreType.DMA((3 * nw,))
    outs = pl.pallas_call(
        body, out_shape=[sem, sem] + [jax.ShapeDtypeStruct(p.shape, p.dtype) for p in parts]
        + [jax.ShapeDtypeStruct(l.shape, l.dtype) for l in lands] + [TOKEN],
        in_specs=[ANY] * (2 * nw), out_specs=[SEM, SEM] + [ANY] * (2 * nw) + [VM],
        input_output_aliases={i: 2 + i for i in range(2 * nw)}, name=name,
        compiler_params=pltpu.CompilerParams(has_side_effects=DATAFLOW))(*parts, *lands)
    return outs[0], outs[1], list(outs[2:2 + nw]), list(outs[2 + nw:2 + 2 * nw]), outs[2 + 2 * nw]


def scatter_wait(name, parts, lands, send, recv, after):
    nw = len(parts)

    def body(*refs):
        src, dst = refs[:nw], refs[nw:2 * nw]
        send_ref, recv_ref = refs[2 * nw], refs[2 * nw + 1]
        x, y, c = _place()
        for w in range(nw):
            for k, chip in enumerate(_other_chips(x, y)):
                cp = _scatter_copy(src[w], dst[w], w, k, chip, c, send_ref, recv_ref)
                cp.wait_send()
                cp.wait_recv()

    outs = pl.pallas_call(
        body, out_shape=[jax.ShapeDtypeStruct(a.shape, a.dtype) for a in list(parts) + list(lands)],
        in_specs=[ANY] * (2 * nw) + [SEM, SEM, ANY], out_specs=[ANY] * (2 * nw),
        input_output_aliases={i: i for i in range(2 * nw)}, name=name,
        compiler_params=pltpu.CompilerParams(has_side_effects=DATAFLOW))(*parts, *lands, send, recv, after)
    return list(outs[:nw]), list(outs[nw:])


SIBLING_PAIR = 0


def _sibling_copy(src, dst, w, c, half_rows, send, recv, sib):
    if half_rows:
        h = src.shape[1] // 2
        src = src.at[:, pl.ds((1 - c) * h, h), :]
    return pltpu.make_async_remote_copy(src_ref=src, dst_ref=dst, send_sem=send.at[w], recv_sem=recv.at[w],
                                        device_id=sib, device_id_type=MESH)


def _landing(shape, dtype):
    return pltpu.with_memory_space_constraint(lax.empty(shape, dtype), pltpu.HBM)


def sibling_start(name, srcs, half_rows):
    nw = len(srcs)
    lands = [_landing((s.shape[0], s.shape[1] // 2, s.shape[2]) if half_rows else s.shape, s.dtype) for s in srcs]

    def body(*refs):
        outs = refs[2 * nw:]
        send, recv = outs[0], outs[1]
        src, dst, token = outs[2:2 + nw], outs[2 + nw:2 + 2 * nw], outs[2 + 2 * nw]
        x, y, c = _place()
        barrier = pltpu.get_barrier_semaphore()
        pl.semaphore_signal(barrier, inc=1, device_id=(x, y, 1 - c), device_id_type=MESH)
        pl.semaphore_wait(barrier, 1)
        for w in range(nw):
            _sibling_copy(src[w], dst[w], w, c, half_rows, send, recv, (x, y, 1 - c)).start()
        token[...] = jnp.zeros_like(token)

    sem = pltpu.SemaphoreType.DMA((nw,))
    outs = pl.pallas_call(
        body, out_shape=[sem, sem] + [jax.ShapeDtypeStruct(a.shape, a.dtype) for a in list(srcs) + lands] + [TOKEN],
        in_specs=[ANY] * (2 * nw), out_specs=[SEM, SEM] + [ANY] * (2 * nw) + [VM],
        input_output_aliases={i: 2 + i for i in range(2 * nw)}, name=name,
        compiler_params=pltpu.CompilerParams(has_side_effects=DATAFLOW, collective_id=SIBLING_PAIR))(*srcs, *lands)
    return outs[0], outs[1], list(outs[2:2 + nw]), list(outs[2 + nw:2 + 2 * nw]), outs[2 + 2 * nw]


def sibling_wait(name, srcs, lands, send, recv, half_rows, after):
    nw = len(srcs)

    def body(*refs):
        src, dst = refs[:nw], refs[nw:2 * nw]
        send_ref, recv_ref = refs[2 * nw], refs[2 * nw + 1]
        x, y, c = _place()
        for w in range(nw):
            cp = _sibling_copy(src[w], dst[w], w, c, half_rows, send_ref, recv_ref, (x, y, 1 - c))
            cp.wait_send()
            cp.wait_recv()

    outs = pl.pallas_call(
        body, out_shape=[jax.ShapeDtypeStruct(a.shape, a.dtype) for a in list(srcs) + list(lands)],
        in_specs=[ANY] * (2 * nw) + [SEM, SEM, ANY], out_specs=[ANY] * (2 * nw),
        input_output_aliases={i: i for i in range(2 * nw)}, name=name,
        compiler_params=pltpu.CompilerParams(has_side_effects=DATAFLOW))(*srcs, *lands, send, recv, after)
    return list(outs[:nw]), list(outs[nw:])


def _peer(x, y, c, r):
    return (1 - x if r & 4 else x, 1 - y if r & 2 else y, 1 - c if r & 1 else c)


def _everyone_copy(buf, r, x, y, c, send, recv, landing):
    px, py, pc = _peer(x, y, c, r)
    slot = buf.at[4 * px + 2 * py + pc] if landing else buf.at[4 * x + 2 * y + c]
    return pltpu.make_async_remote_copy(src_ref=slot, dst_ref=slot, send_sem=send.at[r - 1], recv_sem=recv.at[r - 1],
                                        device_id=(px, py, pc), device_id_type=MESH)


def everyone_start(name, buf):
    def body(buf_in, send, recv, buf_ref, token):
        x, y, c = _place()
        for r in range(1, N_DEV):
            _everyone_copy(buf_ref, r, x, y, c, send, recv, False).start()
        token[...] = jnp.zeros_like(token)

    sem = pltpu.SemaphoreType.DMA((N_DEV - 1,))
    return pl.pallas_call(
        body, out_shape=[sem, sem, jax.ShapeDtypeStruct(buf.shape, buf.dtype), TOKEN],
        in_specs=[ANY], out_specs=[SEM, SEM, ANY, VM], input_output_aliases={0: 2}, name=name,
        compiler_params=pltpu.CompilerParams(has_side_effects=DATAFLOW))(buf)


def everyone_wait(name, buf, send, recv, after):
    def body(buf_ref, send_ref, recv_ref, after_ref, out_ref):
        x, y, c = _place()
        for r in range(1, N_DEV):
            _everyone_copy(buf_ref, r, x, y, c, send_ref, recv_ref, False).wait_send()
            _everyone_copy(buf_ref, r, x, y, c, send_ref, recv_ref, True).wait_recv()

    return pl.pallas_call(
        body, out_shape=jax.ShapeDtypeStruct(buf.shape, buf.dtype), in_specs=[ANY, SEM, SEM, ANY], out_specs=ANY,
        input_output_aliases={0: 0}, name=name,
        compiler_params=pltpu.CompilerParams(has_side_effects=DATAFLOW))(buf, send, recv, after)


def local_step(x, pos, tgt, small, d_in, get_w, put_g, first_dep=None, tick=lambda name, after: None):
    L, D = x.shape
    d_kv = N_KV_HEADS * HEAD_DIM
    d_ssm = small["d_skip"].shape[1]
    d_attn = d_in - 2 * d_kv - d_ssm
    big = {}
    G = d_ssm // SSM_GROUP
    N, P = SSM_STATE, SSM_GROUP
    gbf = bf16

    half_dim = HEAD_DIM // 2
    inv_freq = ROPE_THETA ** (-jnp.arange(half_dim, dtype=f32) / half_dim)
    inv_freq = jnp.tile(inv_freq, LANES // half_dim).reshape(1, LANES)
    sink_b = jnp.broadcast_to(small["sinks"].reshape(-1, 1), (small["sinks"].size, LANES))

    spread = jnp.repeat(jnp.eye(G, dtype=f32), P, axis=0)
    logdt_b = jnp.broadcast_to(small["log_dt"].reshape(G, 1), (G, N))
    bt_re = small["b_re"].reshape(G, N, P).transpose(0, 2, 1).reshape(G * P, N)
    bt_im = small["b_im"].reshape(G, N, P).transpose(0, 2, 1).reshape(G * P, N)
    a_re, a_im = small["a_re"].reshape(G, N), small["a_im"].reshape(G, N)
    lam_re, lam_im, bbt_re, bbt_im = ssm_params(a_re, a_im, logdt_b, bt_re, bt_im, spread)
    bd_re = _block_diag(bbt_re.reshape(G, P, N), P, N).astype(bf16)
    bd_im = _block_diag(bbt_im.reshape(G, P, N), P, N).astype(bf16)
    c_re = small["c_re"].reshape(G, P, N).transpose(0, 2, 1)
    c_im = small["c_im"].reshape(G, P, N).transpose(0, 2, 1)
    cd_re = _block_diag(c_re, N, P).astype(bf16)
    cd_im = _block_diag(c_im, N, P).astype(bf16)
    lam_re_l, lam_im_l = _state_layout(lam_re), _state_layout(lam_im)

    def k1(i, nt, xt, g):
        return (rms_fwd(xt, g),)
    xn = rowwise("pre_mix_norm", k1, L, [full(x)], [small["g_pre_mix"]], [(D, bf16)], dep=first_dep)[0]
    big["w_in"] = get_w("w_in", xn)
    proj = mm_nn("proj_in", xn, big["w_in"])
    qr, kk, vv, u_bf = qkv_prep(proj, pos, inv_freq, d_attn, d_kv)
    attn = attn_fwd(qr, kk, vv, sink_b)
    u_cb = (d_attn + 2 * d_kv) // (d_ssm // 2)
    token = tick("attn", attn)
    y, z_bf, s_re, s_im = ssm_fwd(u_bf, proj, u_cb, bd_re, bd_im, cd_re, cd_im, lam_re_l, lam_im_l, small["d_skip"], dep=token)
    token = tick("ssm", z_bf)
    big["w_glu"] = get_w("w_glu", z_bf)
    gl = mm_nn("glu_proj", z_bf, big["w_glu"], dep=token)

    def k6(i, nt, at, yt, glt, bg, ga, gs):
        ssm = gelu(yt) * sigmoid(glt + bg)
        return (jnp.concatenate([rms_fwd(at, ga), rms_fwd(ssm, gs)], axis=1),)
    mixed = rowwise("mix_norms", k6, L, [full(attn), full(y), full(gl)],
                    [small["b_glu"], small["g_attn_out"], small["g_ssm_out"]], [(d_attn + d_ssm, bf16)])[0]
    big["w_o"] = get_w("w_o", mixed)
    mix = mm_nn("proj_out", mixed, big["w_o"])

    def k7(i, nt, xt, mt, gpm, gpf):
        h = xt + rms_fwd(mt, gpm)
        return h, rms_fwd(h, gpf)
    h, hn = rowwise("post_mix", k7, L, [full(x), full(mix)], [small["g_post_mix"], small["g_pre_ffn"]], [(D, f32), (D, bf16)])
    big["w_gate"] = get_w("w_gate", hn)
    big["w_up"] = get_w("w_up", hn)
    hid_dg, hid_du, hid = ffn_hidden(hn, big["w_gate"], big["w_up"])
    d_ff_dim = hid.shape[1]
    big["w_down"] = get_w("w_down", hid)
    ff = mm_nn("ffn_down", hid, big["w_down"], tk=d_ff_dim // 2)

    def k9(i, nt, ht, fft, tt, g):
        out = ht + rms_fwd(fft, g)
        err = out - tt
        per_row = jnp.mean(err * err, axis=-1, keepdims=True)
        loss = 0.5 * jnp.sum(per_row) * jnp.where(_lane((1, LANES)) == 0, 1.0, 0.0)
        d_out = err * (1.0 / D)
        d_ff, dg = rms_bwd(fft, g, d_out)
        return d_out, d_ff, dg, loss
    d_out, d_ff, dg_post_ffn, loss = rowwise("loss_head", k9, L, [full(h), full(ff), full(tgt)], [small["g_post_ffn"]],
                                             [(D, f32), (D, bf16)], reds=[D, LANES])

    d_gt, d_up = ffn_hidden_grad(d_ff, big["w_down"], hid_dg, hid_du)
    token = put_g("w_down", mm_tn("dw_down", hid, d_ff, out_dtype=gbf, tm=d_ff_dim // N_CHIPS))
    d_hn = mm_nt_pair("d_hn", d_gt, big["w_gate"], d_up, big["w_up"], dep=token)
    token = put_g("w_gate", mm_tn("dw_gate", hn, d_gt, shards=N_CHIPS, out_dtype=gbf))
    token = put_g("w_up", mm_tn("dw_up", hn, d_up, shards=N_CHIPS, out_dtype=gbf, dep=token))

    def k11(i, nt, ht, da, do, mt, gpf, gpm):
        dh_n, dg_pf = rms_bwd(ht, gpf, da)
        dh = do + dh_n
        d_mix, dg_pm = rms_bwd(mt, gpm, dh)
        return dh, d_mix, dg_pf, dg_pm
    dh, d_mix, dg_pre_ffn, dg_post_mix = rowwise("post_mix_grad", k11, L, [full(h), full(d_hn), full(d_out), full(mix)],
                                                 [small["g_pre_ffn"], small["g_post_mix"]], [(D, f32), (D, bf16)], reds=[D, D], dep=token)
    d_mixed = mm_nt("d_mixed", d_mix, big["w_o"])
    token = put_g("w_o", mm_tn("dw_o", mixed, d_mix, out_dtype=gbf))

    def k12(i, nt, at, yt, glt, da_n, ds_n, bg, ga, gs):
        z = gelu(yt)
        sg = sigmoid(glt + bg)
        ssm = z * sg
        d_at, dga = rms_bwd(at, ga, da_n)
        d_ssm_t, dgs = rms_bwd(ssm, gs, ds_n)
        d_gl = d_ssm_t * z * sg * (1.0 - sg)
        return d_at, d_ssm_t * sg, d_gl, dga, dgs, colsum(d_gl)
    d_attn_o, dz1, d_gl, dg_attn, dg_ssm, db_glu = rowwise(
        "mix_norms_grad", k12, L, [full(attn), full(y), full(gl), (d_mixed, d_attn, 0, 0), (d_mixed, d_ssm, d_attn // d_ssm, 0)],
        [small["b_glu"], small["g_attn_out"], small["g_ssm_out"]], [(d_attn, f32), (d_ssm, f32), (d_ssm, bf16)],
        reds=[d_attn, d_ssm, d_ssm], dep=token)
    dz2 = mm_nt("d_glu_in", d_gl, big["w_glu"])
    token = put_g("w_glu", mm_tn("dw_glu", z_bf, d_gl, out_dtype=gbf))

    du, dbd_re, dbd_im, dcd_re, dcd_im, dlam_re_l, dlam_im_l, dd_skip = ssm_bwd(
        y, dz1, dz2, u_bf, proj, u_cb, s_re, s_im, bd_re, bd_im, cd_re, cd_im, lam_re_l, lam_im_l, small["d_skip"], dep=token)
    dq, dkk_c, dkk_p, dvv_c, dvv_p, dsink = attn_bwd(qr, kk, vv, sink_b, attn, d_attn_o)
    d_proj = qkv_grad(dq, dkk_c, dkk_p, dvv_c, dvv_p, du, pos, inv_freq)
    d_xn = mm_nt("d_xn", d_proj, big["w_in"])
    token = put_g("w_in", mm_tn("dw_in", xn, d_proj, shards=N_CHIPS, out_dtype=gbf))

    def k17(i, nt, xt, dxn, dht, g):
        dx, dg = rms_bwd(xt, g, dxn)
        return dht + dx, dg
    grad_x, dg_pre_mix = rowwise("pre_mix_grad", k17, L, [full(x), full(d_xn), full(dh)], [small["g_pre_mix"]],
                                 [(D, f32)], reds=[D], dep=token)

    gather = spread.T
    dbbt_re = dbd_re.reshape(G * P, LANES)[:, :N]
    dbbt_im = dbd_im.reshape(G * P, LANES)[:, :N]
    dc_re = dcd_re.reshape(G, N, LANES)[:, :, :P].transpose(0, 2, 1)
    dc_im = dcd_im.reshape(G, N, LANES)[:, :, :P].transpose(0, 2, 1)
    d_a_re, d_a_im, d_logdt, dbt_re, dbt_im = ssm_params_grad(
        a_re, a_im, logdt_b, bt_re, bt_im, spread, gather,
        _state_layout_inv(dlam_re_l, G, N), _state_layout_inv(dlam_im_l, G, N), dbbt_re, dbbt_im)
    q_per_kv = d_attn // HEAD_DIM // N_KV_HEADS
    small_grads = {
        "g_pre_mix": dg_pre_mix, "sinks": dsink[:, :q_per_kv, 0].reshape(1, -1),
        "a_re": d_a_re, "a_im": d_a_im, "log_dt": d_logdt.reshape(1, G),
        "b_re": dbt_re, "b_im": dbt_im,
        "c_re": dc_re, "c_im": dc_im,
        "d_skip": dd_skip, "b_glu": db_glu, "g_attn_out": dg_attn, "g_ssm_out": dg_ssm,
        "g_post_mix": dg_post_mix, "g_pre_ffn": dg_pre_ffn, "g_post_ffn": dg_post_ffn,
    }
    return loss, grad_x, small_grads


WEIGHTS = ['g_pre_mix', 'w_in', 'sinks', 'a_re', 'a_im', 'log_dt', 'b_re', 'b_im', 'c_re', 'c_im', 'd_skip', 'w_glu', 'b_glu',
           'g_attn_out', 'g_ssm_out', 'w_o', 'g_post_mix', 'g_pre_ffn', 'w_gate', 'w_up', 'w_down', 'g_post_ffn']
BIG = ['w_in', 'w_glu', 'w_o', 'w_gate', 'w_up', 'w_down']
COL_SHARDED = ['w_in', 'w_gate', 'w_up']
SMALL = [n for n in WEIGHTS if n not in BIG]
GATHER_GROUPS = [["w_in"], ["w_glu", "w_o"], ["w_gate", "w_up"], ["w_down"]]
REDUCE_GROUPS = [["w_down", "w_gate", "w_up"], ["w_o", "w_glu", "w_in"]]


PACK_ROWS = 256


def _pack(parts):
    flat = jnp.concatenate([p.reshape(-1) for p in parts])
    pad = (-flat.size) % (PACK_ROWS * LANES)
    return jnp.pad(flat, (0, pad)).reshape(-1, LANES)


TRANSPOSED_VIEW = ("b_re", "b_im")


def small_view(name, a):
    if name in TRANSPOSED_VIEW:
        a = a.transpose(0, 1, 3, 2)
    return a.reshape(-1, a.shape[-1])


def small_unview(name, p, shape):
    if name in TRANSPOSED_VIEW:
        return p.reshape(shape[0], shape[1], shape[3], shape[2]).transpose(0, 1, 3, 2)
    return p.reshape(shape)


def _unpack(packed, shapes):
    flat = packed.reshape(-1)
    out, off = [], 0
    for s in shapes:
        n = int(np.prod(s))
        out.append(flat[off:off + n].reshape(s))
        off += n
    return out


def kernel(x, positions, g_pre_mix, w_in, sinks, a_re, a_im, log_dt, b_re, b_im, c_re, c_im, d_skip, w_glu, b_glu, g_attn_out, g_ssm_out, w_o, g_post_mix, g_pre_ffn, w_gate, w_up, w_down, g_post_ffn, loss_target, m_g_pre_mix, m_w_in, m_sinks, m_a_re, m_a_im, m_log_dt, m_b_re, m_b_im, m_c_re, m_c_im, m_d_skip, m_w_glu, m_b_glu, m_g_attn_out, m_g_ssm_out, m_w_o, m_g_post_mix, m_g_pre_ffn, m_w_gate, m_w_up, m_w_down, m_g_post_ffn, v_g_pre_mix, v_w_in, v_sinks, v_a_re, v_a_im, v_log_dt, v_b_re, v_b_im, v_c_re, v_c_im, v_d_skip, v_w_glu, v_b_glu, v_g_attn_out, v_g_ssm_out, v_w_o, v_g_post_mix, v_g_pre_ffn, v_w_gate, v_w_up, v_w_down, v_g_post_ffn):
    args = dict(locals())
    w = {n: args[n] for n in WEIGHTS}
    m = {n: args["m_" + n] for n in WEIGHTS}
    v = {n: args["v_" + n] for n in WEIGHTS}
    L, D = x.shape[1], x.shape[2]

    ax, ay, ac = _place()
    mine_arr = (2 * ax + ay).astype(jnp.int32).reshape(1)
    c_arr = ac.astype(jnp.int32).reshape(1)

    me_arr = (4 * ax + 2 * ay + ac).astype(jnp.int32).reshape(1)

    bufs = {"w_in": into_slot("cast_w_in", w["w_in"][0], mine_arr, N_CHIPS, bf16)}
    (first_sems,), (bufs["w_in"],), token = gather_start("gather_start_in", [bufs["w_in"]], [[0]], mine_arr, neighbours_only=(0,))
    sems, relays, ready = {"w_in": first_sems}, {}, set()

    def relay(n, more, after):
        r_send, r_recv, more_sems, (bufs[n],), started, tok = gather_relay(
            "gather_relay_" + n, [bufs[n]], [sems[n]], [bufs[k] for k in more], after)
        sems.update({k: more_sems for k in more})
        bufs.update(zip(more, started))
        relays[n] = (r_send, r_recv)
        return tok

    for n in BIG[1:]:
        bufs[n] = into_slot("cast_" + n, w[n][0], mine_arr, N_CHIPS, bf16, dep=token)
    token = relay("w_in", [], bufs[BIG[-1]])
    first = ["w_gate", "w_glu", "w_o", "w_up"]
    (sems["w_gate"], sems["w_glu"], sems["w_up"]), started, token = gather_start(
        "gather_start_rest", [bufs[n] for n in first], [[0], [1, 2], [3]], token, neighbours_only=(0, 3))
    bufs.update(zip(first, started))

    def tick(name, after):
        return relay("w_gate", [], after) if name == "attn" else relay("w_up", ["w_down"], after)

    def get_w(n, after):
        if n not in ready:
            members = [g for g in GATHER_GROUPS if n in g][0]
            if members[0] in relays:
                landed = [gather_wait_relay("gather_wait_" + k, [bufs[k]], *relays[k], after)[0] for k in members]
                which = (2,)
            else:
                landed = gather_wait("gather_wait_" + members[0], [bufs[k] for k in members], *sems[members[0]], after)
                which = (0, 1, 2)
            bufs.update(zip(members, gather_forward("gather_forward_" + members[0], landed, which)))
            ready.update(members)
        g = bufs[n]
        return g if n in COL_SHARDED else g.reshape(g.shape[0] * g.shape[1], g.shape[2])

    swaps, inflight = {}, []

    def put_g(n, g):
        g3 = g if n in COL_SHARDED else g.reshape(N_CHIPS, g.shape[0] // N_CHIPS, g.shape[1])
        swaps[n] = sibling_start("swap_start_" + n, [g3], True)
        for gi, members in enumerate(REDUCE_GROUPS):
            if n == members[-1]:
                last = swaps[n][4]
                pair = []
                for k in members:
                    send, recv, srcs, lands, _ = swaps[k]
                    (src,), (got,) = sibling_wait("swap_wait_" + k, srcs, lands, send, recv, True, last)
                    pair.append(pair_sum("pair_sum_" + k, src, got, c_arr))
                send, recv, parts, lands, tok = scatter_start("scatter_start_%d" % gi, pair)
                inflight.append((members, send, recv, parts, lands))
                return tok
        return swaps[n][4]

    small = {n: w[n].reshape(1, -1) for n in SMALL}
    pos = positions.reshape(L, 1).astype(f32)
    d_in = N_CHIPS * w["w_in"].shape[2]
    loss, grad_x, small_grads = local_step(x[0], pos, loss_target[0], small, d_in, get_w, put_g, first_dep=token, tick=tick)

    shapes = [w[n].shape for n in SMALL]
    blocks = into_slot("small_block", _pack([small_grads[n] for n in SMALL] + [loss]), me_arr, N_DEV, f32)
    small_send, small_recv, blocks, after = everyone_start("small_start", blocks)

    grads, delta, new_m, new_v = {}, {}, {}, {}
    for gi, (members, send, recv, parts, lands) in enumerate(inflight):
        parts, landed = scatter_wait("scatter_wait_%d" % gi, parts, lands, send, recv, after)
        joins, dep = [], None
        for k, p, t in zip(members, parts, landed):
            joins.append(sibling_start("join_start_" + k, [chip_sum("chip_sum_" + k, p, t, mine_arr, dep=dep)], False))
            dep = after = joins[-1][4]
        for n, (send, recv, srcs, lands, _) in zip(members, joins):
            (own,), (sib,) = sibling_wait("join_wait_" + n, srcs, lands, send, recv, False, after)
            g_, d_, m_, v_ = adamw_halves("adamw_" + n, w[n][0], own, sib, m[n][0], v[n][0], c_arr)
            grads[n], delta[n], new_m[n], new_v[n] = g_[None], d_[None], m_[None], v_[None]
            after = v_
    blocks = everyone_wait("small_wait", blocks, small_send, small_recv, after)
    small_sum = sum_slots("small_sum", blocks)
    *small_g, loss = _unpack(small_sum, [small_view(n, w[n]).shape for n in SMALL] + [loss.shape])
    loss = loss[0, 0]
    outs = adamw_many("adamw_small", [small_view(n, w[n]) for n in SMALL], small_g,
                      [small_view(n, m[n]) for n in SMALL], [small_view(n, v[n]) for n in SMALL])
    for t, parts in zip((grads, delta, new_m, new_v), (small_g,) + tuple(outs)):
        t.update({n: small_unview(n, p, w[n].shape) for n, p in zip(SMALL, parts)})

    return (loss, grad_x[None], *[grads[n] for n in WEIGHTS], *[delta[n] for n in WEIGHTS],
            *[new_m[n] for n in WEIGHTS], *[new_v[n] for n in WEIGHTS])
```

```python
import functools
import math

import jax
import jax.numpy as jnp
import numpy as np
from jax import lax
from jax.experimental import pallas as pl
from jax.experimental.pallas import tpu as pltpu

f32 = jnp.float32
bf16 = jnp.bfloat16
HIGHEST = lax.Precision.HIGHEST
MESH = pl.DeviceIdType.MESH

HEAD_DIM = 64
N_KV_HEADS = 4
ATTN_BLOCK = 128
ROPE_THETA = 10000.0
SSM_GROUP = 16
SSM_STATE = 64
RMS_EPS = 1e-6
LANES = 128
SUBLANES = 8
VMEM_LIMIT = 52 * 1024 * 1024
N_CHIPS = 4
N_DEV = 8
NEG = -1e30

ADAM_LR, ADAM_B1, ADAM_B2, ADAM_EPS, ADAM_WD, ADAM_STEP = 0.001, 0.9, 0.999, 1e-08, 0.01, 10

NN = (((1,), (0,)), ((), ()))
NT = (((1,), (1,)), ((), ()))
TN = (((0,), (0,)), ((), ()))


def _params(*sem):
    return pltpu.CompilerParams(dimension_semantics=sem or None, vmem_limit_bytes=VMEM_LIMIT)


def _dot(a, b, dims=NN):
    return lax.dot_general(a, b, dims, preferred_element_type=f32)


def _pick(dim, pref):
    t = min(dim, pref)
    while dim % t:
        t -= LANES
    assert t > 0, (dim, pref)
    return t


ANY = pl.BlockSpec(memory_space=pl.ANY)


def _with_dep(in_specs, operands, dep):
    if dep is None:
        return list(in_specs), list(operands), 0
    return list(in_specs) + [ANY], list(operands) + [dep], 1


def _mm_call(name, grid, in_specs, out_spec, out_shape, acc_shape, dims, operands, dep=None):
    nk = grid[2]
    in_specs, operands, n_dep = _with_dep(in_specs, operands, dep)

    def body_one(a_ref, b_ref, *rest):
        o_ref = rest[n_dep]
        o_ref[...] = _dot(a_ref[...], b_ref[...], dims).astype(o_ref.dtype)

    def body(a_ref, b_ref, *rest):
        o_ref, acc_ref = rest[n_dep], rest[n_dep + 1]
        k = pl.program_id(2)

        @pl.when(k == 0)
        def _():
            acc_ref[...] = _dot(a_ref[...], b_ref[...], dims)

        @pl.when((k > 0) & (k < nk - 1))
        def _():
            acc_ref[...] += _dot(a_ref[...], b_ref[...], dims)

        @pl.when(k == nk - 1)
        def _():
            o_ref[...] = (acc_ref[...] + _dot(a_ref[...], b_ref[...], dims)).astype(o_ref.dtype)

    return pl.pallas_call(
        body_one if nk == 1 else body, out_shape=out_shape, grid=grid, in_specs=in_specs, out_specs=out_spec,
        scratch_shapes=[] if nk == 1 else [pltpu.VMEM(acc_shape, f32)], name=name,
        compiler_params=_params("parallel", "parallel", "arbitrary"))(*operands)


def mm_nt_pair(name, a1, b1, a2, b2, tm=1024, tk=1024, dep=None):
    M = a1.shape[0]
    S, K, n = b1.shape
    tm, tko = _pick(M, tm), _pick(K, tk)
    nk = 2 * S

    def body(a1_ref, b1_ref, a2_ref, b2_ref, *rest):
        o_ref, acc_ref = rest[-2], rest[-1]
        k = pl.program_id(2)

        @pl.when(k == 0)
        def _():
            acc_ref[...] = _dot(a1_ref[...], b1_ref[...], NT)

        @pl.when((k > 0) & (k < S))
        def _():
            acc_ref[...] += _dot(a1_ref[...], b1_ref[...], NT)

        @pl.when((k >= S) & (k < nk - 1))
        def _():
            acc_ref[...] += _dot(a2_ref[...], b2_ref[...], NT)

        @pl.when(k == nk - 1)
        def _():
            o_ref[...] = acc_ref[...] + _dot(a2_ref[...], b2_ref[...], NT)

    first = lambda k: jnp.minimum(k, S - 1)
    second = lambda k: jnp.maximum(k - S, 0)
    in_specs = [pl.BlockSpec((tm, n), lambda i, j, k: (i, first(k))), pl.BlockSpec((None, tko, n), lambda i, j, k: (first(k), j, 0)),
                pl.BlockSpec((tm, n), lambda i, j, k: (i, second(k))), pl.BlockSpec((None, tko, n), lambda i, j, k: (second(k), j, 0))]
    in_specs, operands, _ = _with_dep(in_specs, (a1, b1, a2, b2), dep)
    return pl.pallas_call(
        body, out_shape=jax.ShapeDtypeStruct((M, K), f32), grid=(M // tm, K // tko, nk), in_specs=in_specs,
        out_specs=pl.BlockSpec((tm, tko), lambda i, j, k: (i, j)), scratch_shapes=[pltpu.VMEM((tm, tko), f32)], name=name,
        compiler_params=_params("parallel", "parallel", "arbitrary"))(*operands)


def mm_nn(name, a, b, out_dtype=f32, tm=1024, tn=1024, tk=2048, dep=None):
    M, K = a.shape
    tm, tk = _pick(M, tm), _pick(K, tk)
    if b.ndim == 3:
        S, _, n = b.shape
        tn = _pick(n, 2048)
        per = n // tn
        b_spec = pl.BlockSpec((None, tk, tn), lambda i, j, k: (j // per, k, j % per))
        N = S * n
    else:
        N = b.shape[1]
        tn = _pick(N, tn)
        b_spec = pl.BlockSpec((tk, tn), lambda i, j, k: (k, j))
    grid = (M // tm, N // tn, K // tk)
    return _mm_call(name, grid, [pl.BlockSpec((tm, tk), lambda i, j, k: (i, k)), b_spec],
                    pl.BlockSpec((tm, tn), lambda i, j, k: (i, j)), jax.ShapeDtypeStruct((M, N), out_dtype),
                    (tm, tn), NN, (a, b), dep)


def mm_nt(name, a, b, out_dtype=f32, tm=1024, tn=2048, tk=1024, dep=None):
    M, N = a.shape
    tm = _pick(M, tm)
    if b.ndim == 3:
        S, K, n = b.shape
        tr = _pick(n, 2048)
        per = n // tr
        tko = _pick(K, tk)
        b_spec = pl.BlockSpec((None, tko, tr), lambda i, j, k: (k // per, j, k % per))
    else:
        K = b.shape[0]
        tr = _pick(N, tn)
        tko = _pick(K, tk)
        b_spec = pl.BlockSpec((tko, tr), lambda i, j, k: (j, k))
    grid = (M // tm, K // tko, N // tr)
    return _mm_call(name, grid, [pl.BlockSpec((tm, tr), lambda i, j, k: (i, k)), b_spec],
                    pl.BlockSpec((tm, tko), lambda i, j, k: (i, j)), jax.ShapeDtypeStruct((M, K), out_dtype),
                    (tm, tko), NT, (a, b), dep)


def mm_tn(name, a, b, shards=None, out_dtype=f32, tm=1024, tn=1024, tl=2048, dep=None):
    L, K = a.shape
    N = b.shape[1]
    tl, tko = _pick(L, tl), _pick(K, tm)
    if shards:
        n = N // shards
        tn = _pick(n, 2048)
        per = n // tn
        o_spec = pl.BlockSpec((None, tko, tn), lambda i, j, k: (j // per, i, j % per))
        o_shape = jax.ShapeDtypeStruct((shards, K, n), out_dtype)
    else:
        tn = _pick(N, tn)
        o_spec = pl.BlockSpec((tko, tn), lambda i, j, k: (i, j))
        o_shape = jax.ShapeDtypeStruct((K, N), out_dtype)
    grid = (K // tko, N // tn, L // tl)
    return _mm_call(name, grid, [pl.BlockSpec((tl, tko), lambda i, j, k: (k, i)),
                                 pl.BlockSpec((tl, tn), lambda i, j, k: (k, j))],
                    o_spec, o_shape, (tko, tn), TN, (a, b), dep)


def mm_tn_half(name, a, b, col_sharded, half_arr, addend=None, dep=None, tn=2048):
    L, K = a.shape
    N = b.shape[1]
    S = N_CHIPS
    h, cols = (K // 2, N // S) if col_sharded else (K // S // 2, N)
    tko, tn = _pick(h, 1024), _pick(cols, tn)
    ni, nj = h // tko, cols // tn
    if col_sharded:
        a_map = lambda s, i, j, hf: (0, hf[0] * ni + i)
        b_map = lambda s, i, j, hf: (0, s * nj + j)
    else:
        a_map = lambda s, i, j, hf: (0, (2 * s + hf[0]) * ni + i)
        b_map = lambda s, i, j, hf: (0, j)
    o_spec = pl.BlockSpec((None, tko, tn), lambda s, i, j, hf: (s, i, j))
    n_add = 0 if addend is None else 1
    n_dep = 0 if dep is None else 1

    def body(hf_ref, a_ref, b_ref, *rest):
        o_ref = rest[n_add + n_dep]
        acc = _dot(a_ref[...], b_ref[...], TN)
        if n_add:
            acc = acc + rest[0][...].astype(f32)
        o_ref[...] = acc.astype(o_ref.dtype)

    in_specs = [pl.BlockSpec((L, tko), a_map), pl.BlockSpec((L, tn), b_map)] + [o_spec] * n_add
    in_specs, operands, _ = _with_dep(in_specs, [a, b] + ([addend] if n_add else []), dep)
    grid_spec = pltpu.PrefetchScalarGridSpec(num_scalar_prefetch=1, grid=(S, ni, nj), in_specs=in_specs, out_specs=o_spec)
    return pl.pallas_call(body, out_shape=jax.ShapeDtypeStruct((S, h, cols), bf16), grid_spec=grid_spec, name=name,
                          compiler_params=_params("parallel", "parallel", "parallel"))(half_arr, *operands)


def weight_grad(name, a, b, dep=None):
    if name in COL_SHARDED:
        return mm_tn("d" + name, a, b, shards=N_CHIPS, out_dtype=bf16, dep=dep)
    return mm_tn("d" + name, a, b, out_dtype=bf16, tm=a.shape[1] // N_CHIPS, dep=dep)


def ffn_hidden(hn, w_gate, w_up, tm=512):
    M, K = hn.shape
    S, _, n = w_gate.shape
    tm = _pick(M, tm)

    def body(a_ref, g_ref, u_ref, dg_ref, du_ref, hid_ref):
        a = a_ref[...]
        g = _dot(a, g_ref[...])
        u = _dot(a, u_ref[...])
        sg = sigmoid(g)
        act = g * sg
        dg_ref[...] = (u * (sg * (1.0 + g * (1.0 - sg)))).astype(bf16)
        du_ref[...] = act.astype(bf16)
        hid_ref[...] = (act * u).astype(bf16)

    w_spec = pl.BlockSpec((None, K, n), lambda s, i: (s, 0, 0))
    o_spec = pl.BlockSpec((tm, n), lambda s, i: (i, s))
    o = jax.ShapeDtypeStruct((M, S * n), bf16)
    return pl.pallas_call(
        body, out_shape=[o, o, o], grid=(S, M // tm), in_specs=[pl.BlockSpec((tm, K), lambda s, i: (i, 0)), w_spec, w_spec],
        out_specs=[o_spec, o_spec, o_spec], name="ffn_hidden", compiler_params=_params("parallel", "parallel"))(hn, w_gate, w_up)


def ffn_hidden_grad(d_ff, w_down, hid_dg, hid_du, tm=512):
    M, D = d_ff.shape
    F = w_down.shape[0]
    n = _pick(F // N_CHIPS, 2048)
    tm = _pick(M, tm)

    def body(a_ref, b_ref, pg_ref, pu_ref, dg_ref, du_ref):
        dh = _dot(a_ref[...], b_ref[...], NT)
        dg_ref[...] = (dh * pg_ref[...].astype(f32)).astype(bf16)
        du_ref[...] = (dh * pu_ref[...].astype(f32)).astype(bf16)

    t_spec = pl.BlockSpec((tm, n), lambda j, i: (i, j))
    o = jax.ShapeDtypeStruct((M, F), bf16)
    return pl.pallas_call(
        body, out_shape=[o, o], grid=(F // n, M // tm),
        in_specs=[pl.BlockSpec((tm, D), lambda j, i: (i, 0)), pl.BlockSpec((n, D), lambda j, i: (j, 0)), t_spec, t_spec],
        out_specs=[t_spec, t_spec], name="ffn_hidden_grad", compiler_params=_params("parallel", "parallel"))(d_ff, w_down, hid_dg, hid_du)


def rowwise(name, fn, L, rows, bcast, outs, reds=(), tr=256, dep=None):
    tr = min(tr, L)
    nt = L // tr
    n_rows, n_b, n_o = len(rows), len(bcast), len(outs)
    n_dep = 0 if dep is None else 1

    def body(*refs):
        i = pl.program_id(0)
        ins = [r[...] for r in refs[:n_rows + n_b]]
        res = fn(i, nt, *ins)
        o_refs = refs[n_rows + n_b + n_dep:]
        for k in range(n_o):
            o_refs[k][...] = res[k].astype(o_refs[k].dtype)
        if reds:
            @pl.when(i == 0)
            def _():
                for k in range(len(reds)):
                    o_refs[n_o + k][...] = jnp.zeros_like(o_refs[n_o + k])
            for k in range(len(reds)):
                o_refs[n_o + k][...] += res[n_o + k]

    def row_spec(width, cb, shift):
        if shift:
            return pl.BlockSpec((tr, width), lambda i: (jnp.minimum(i + shift, nt - 1), cb))
        return pl.BlockSpec((tr, width), lambda i: (i, cb))

    in_specs = [row_spec(w, cb, sh) for (_, w, cb, sh) in rows]
    in_specs += [pl.BlockSpec(b.shape, lambda i: (0, 0)) for b in bcast]
    out_specs = [pl.BlockSpec((tr, w), lambda i: (i, 0)) for (w, _) in outs]
    out_specs += [pl.BlockSpec((1, w), lambda i: (0, 0)) for w in reds]
    out_shape = [jax.ShapeDtypeStruct((L, w), dt) for (w, dt) in outs]
    out_shape += [jax.ShapeDtypeStruct((1, w), f32) for w in reds]
    in_specs, operands, _ = _with_dep(in_specs, [r[0] for r in rows] + list(bcast), dep)
    return pl.pallas_call(
        body, out_shape=out_shape, grid=(nt,), in_specs=in_specs, out_specs=out_specs, name=name,
        compiler_params=_params("arbitrary"))(*operands)


def full(a):
    return (a, a.shape[1], 0, 0)


def colsum(v):
    return jnp.sum(v, axis=0, keepdims=True)


def rms_fwd(x, g):
    r = lax.rsqrt(jnp.mean(x * x, axis=-1, keepdims=True) + RMS_EPS)
    return x * r * g


def rms_bwd(x, g, dy):
    r = lax.rsqrt(jnp.mean(x * x, axis=-1, keepdims=True) + RMS_EPS)
    xh = x * r
    dyg = dy * g
    dx = r * (dyg - xh * jnp.mean(dyg * xh, axis=-1, keepdims=True))
    return dx, colsum(dy * xh)


GELU_C = math.sqrt(2.0 / math.pi)


def gelu(y):
    return y * (0.5 * (1.0 + jnp.tanh(GELU_C * (y + 0.044715 * (y * y * y)))))


def gelu_grad(y):
    t = jnp.tanh(GELU_C * (y + 0.044715 * (y * y * y)))
    return 0.5 * (1.0 + t) + 0.5 * y * (1.0 - t * t) * (GELU_C * (1.0 + 3 * 0.044715 * (y * y)))


def sigmoid(v):
    return 1.0 / (1.0 + jnp.exp(-v))


def _lane(shape):
    return lax.broadcasted_iota(jnp.int32, shape, 1)


def _rot_chunk(t, cos, sin_signed):
    first = (_lane(t.shape) % HEAD_DIM) < (HEAD_DIM // 2)
    partner = jnp.where(first, pltpu.roll(t, LANES - HEAD_DIM // 2, 1), pltpu.roll(t, HEAD_DIM // 2, 1))
    return t * cos + partner * sin_signed


def _cos_sin(pos, inv_freq, inverse):
    ang = pos * inv_freq
    cos, sin = jnp.cos(ang), jnp.sin(ang)
    first = (_lane(ang.shape) % HEAD_DIM) < (HEAD_DIM // 2)
    sign = jnp.where(first, -1.0, 1.0) * (-1.0 if inverse else 1.0)
    return cos, sin * sign


def _dup_head(chunk, odd):
    low = _lane(chunk.shape) < HEAD_DIM
    x = jnp.where(low != odd, chunk, 0.0)
    return x + pltpu.roll(x, HEAD_DIM, 1)


def _chunks(v):
    return [v[:, LANES * c:LANES * (c + 1)] for c in range(v.shape[1] // LANES)]


def qkv_prep(proj, pos, inv_freq, d_attn, d_kv):
    L = proj.shape[0]
    d_ssm = proj.shape[1] - d_attn - 2 * d_kv
    half = d_ssm // 2
    scale = 1.0 / math.sqrt(HEAD_DIM)

    def fn(i, nt, q, k, v, u0, u1, p, invf):
        cos, sin = _cos_sin(p, invf, False)
        qr = jnp.concatenate([_rot_chunk(c, cos, sin) for c in _chunks(q)], axis=1) * scale
        kr = [_rot_chunk(c, cos, sin) for c in _chunks(k)]
        kk = jnp.concatenate([_dup_head(c, odd) for c in kr for odd in (False, True)], axis=1)
        vv = jnp.concatenate([_dup_head(c, odd) for c in _chunks(v) for odd in (False, True)], axis=1)
        return qr, kk, vv, jnp.concatenate([u0, u1], axis=1)

    u_cb = (d_attn + 2 * d_kv) // half
    return rowwise("qkv_prep", fn, L,
                   [(proj, d_attn, 0, 0), (proj, d_kv, d_attn // d_kv, 0), (proj, d_kv, d_attn // d_kv + 1, 0),
                    (proj, half, u_cb, 0), (proj, half, u_cb + 1, 0), full(pos)],
                   [inv_freq], [(d_attn, bf16), (2 * d_kv, bf16), (2 * d_kv, bf16), (d_ssm, bf16)])


def qkv_grad(dq, dkk_c, dkk_p, dvv_c, dvv_p, du, pos, inv_freq):
    L, d_attn = dq.shape
    d_kv = dkk_c.shape[1] // 2
    scale = 1.0 / math.sqrt(HEAD_DIM)

    def fold(cur, prev, i, nt):
        t = cur + jnp.where(i < nt - 1, prev, 0.0)
        out = []
        for c in range(t.shape[1] // (2 * LANES)):
            even, odd = t[:, 2 * c * LANES:(2 * c + 1) * LANES], t[:, (2 * c + 1) * LANES:(2 * c + 2) * LANES]
            even, odd = even + pltpu.roll(even, HEAD_DIM, 1), odd + pltpu.roll(odd, HEAD_DIM, 1)
            out.append(jnp.where(_lane(even.shape) < HEAD_DIM, even, odd))
        return out

    def fn(i, nt, dq_t, kc, kp, vc, vp, du_t, p, invf):
        cos, sin = _cos_sin(p, invf, True)
        dq_o = jnp.concatenate([_rot_chunk(c, cos, sin) for c in _chunks(dq_t)], axis=1) * scale
        dk_o = jnp.concatenate([_rot_chunk(c, cos, sin) for c in fold(kc, kp, i, nt)], axis=1)
        dv_o = jnp.concatenate(fold(vc, vp, i, nt), axis=1)
        return (jnp.concatenate([dq_o, dk_o, dv_o, du_t], axis=1),)

    return rowwise("qkv_grad", fn, L,
                   [full(dq), full(dkk_c), (dkk_p, 2 * d_kv, 0, 1), full(dvv_c), (dvv_p, 2 * d_kv, 0, 1), full(du), full(pos)],
                   [inv_freq], [(d_attn + 2 * d_kv + du.shape[1], bf16)], tr=ATTN_BLOCK)[0]


def _attn_specs(L):
    nb = L // ATTN_BLOCK
    B = ATTN_BLOCK
    q_spec = lambda width: pl.BlockSpec((B, width), lambda n: (n, 0))
    prev = lambda width: pl.BlockSpec((B, width), lambda n: (jnp.maximum(n - 1, 0), 0))
    return nb, q_spec, prev


def _attn_mask(n):
    B = ATTN_BLOCK
    row = lax.broadcasted_iota(jnp.int32, (B, 2 * B), 0)
    col = lax.broadcasted_iota(jnp.int32, (B, 2 * B), 1)
    return ((col < B) & (col > row) & (n > 0)) | ((col >= B) & (row >= col - B))


def _attn_probs(qm, kcat, sink, mask):
    s = jnp.where(mask, _dot(qm, kcat, NT), NEG)
    m = jnp.maximum(jnp.max(s, axis=1, keepdims=True), sink)
    p, ps = jnp.exp(s - m), jnp.exp(sink - m)
    inv = 1.0 / (jnp.sum(p, axis=1, keepdims=True) + ps)
    return p, inv, ps


def _attn_heads(q_ref, s_ref, h, q_per_kv):
    low = _lane((ATTN_BLOCK, LANES)) < HEAD_DIM
    heads = []
    for pr in range(h * q_per_kv // 2, (h + 1) * q_per_kv // 2):
        q2 = q_ref[:, LANES * pr:LANES * (pr + 1)]
        for odd in (False, True):
            mine = low != odd
            sink = jnp.max(s_ref[2 * pr + int(odd):2 * pr + int(odd) + 1, :], axis=1, keepdims=True)
            heads.append((pr, mine, jnp.where(mine, q2, jnp.zeros_like(q2)), sink))
    return low, heads


def _kv_block(prev_ref, cur_ref, h):
    return jnp.concatenate([prev_ref[:, LANES * h:LANES * (h + 1)], cur_ref[:, LANES * h:LANES * (h + 1)]], axis=0)


def attn_fwd(qr, kk, vv, sink_b):
    L, d_attn = qr.shape
    nb, q_spec, prev = _attn_specs(L)
    d_kk = kk.shape[1]
    n_kv = d_kk // LANES
    q_per_kv = d_attn // HEAD_DIM // n_kv

    def body(q_ref, kc_ref, kp_ref, vc_ref, vp_ref, s_ref, o_ref):
        mask = _attn_mask(pl.program_id(0))
        for h in range(n_kv):
            kcat, vcat = _kv_block(kp_ref, kc_ref, h), _kv_block(vp_ref, vc_ref, h)
            low, heads = _attn_heads(q_ref, s_ref, h, q_per_kv)
            probs = [_attn_probs(qm, kcat, sink, mask) for (_, _, qm, sink) in heads]
            outs = [_dot(p.astype(bf16), vcat) * inv for (p, inv, _) in probs]
            for i in range(0, len(heads), 2):
                pr = heads[i][0]
                o_ref[:, LANES * pr:LANES * (pr + 1)] = jnp.where(low, outs[i], outs[i + 1])

    return pl.pallas_call(
        body, out_shape=jax.ShapeDtypeStruct((L, d_attn), f32), grid=(nb,),
        in_specs=[q_spec(d_attn), q_spec(d_kk), prev(d_kk), q_spec(d_kk), prev(d_kk), pl.BlockSpec(sink_b.shape, lambda n: (0, 0))],
        out_specs=q_spec(d_attn), name="attn_fwd", compiler_params=_params("arbitrary"))(qr, kk, kk, vv, vv, sink_b)


def attn_bwd(qr, kk, vv, sink_b, attn, d_attn_out):
    L, d_attn = qr.shape
    nb, q_spec, prev = _attn_specs(L)
    d_kk = kk.shape[1]
    n_kv = d_kk // LANES
    q_per_kv = d_attn // HEAD_DIM // n_kv

    def body(q_ref, kc_ref, kp_ref, vc_ref, vp_ref, s_ref, o_ref, do_ref, dq_ref, dkc_ref, dkp_ref, dvc_ref, dvp_ref, ds_ref):
        n = pl.program_id(0)
        B = ATTN_BLOCK
        mask = _attn_mask(n)
        srow = lax.broadcasted_iota(jnp.int32, (SUBLANES, LANES), 0)

        @pl.when(n == 0)
        def _():
            ds_ref[...] = jnp.zeros_like(ds_ref)

        for h in range(n_kv):
            kcat, vcat = _kv_block(kp_ref, kc_ref, h), _kv_block(vp_ref, vc_ref, h)
            low, heads = _attn_heads(q_ref, s_ref, h, q_per_kv)
            probs = [_attn_probs(qm, kcat, sink, mask) for (_, _, qm, sink) in heads]
            dk = jnp.zeros((2 * B, LANES), f32)
            dv = dk
            dsink = jnp.zeros((SUBLANES, LANES), f32)
            dqs = []
            for i, ((pr, mine, qm, _), (p, inv, ps)) in enumerate(zip(heads, probs)):
                do2 = do_ref[:, LANES * pr:LANES * (pr + 1)]
                delta = jnp.sum(jnp.where(mine, do2 * o_ref[:, LANES * pr:LANES * (pr + 1)], 0.0), axis=1, keepdims=True)
                dob = jnp.where(mine, do2, 0.0).astype(bf16)
                p = p * inv
                ds = (p * (_dot(dob, vcat, NT) - delta)).astype(bf16)
                dqs.append(_dot(ds, kcat))
                dk = dk + _dot(ds, qm, TN)
                dv = dv + _dot(p.astype(bf16), dob, TN)
                dsink = dsink + jnp.where(srow == i, -jnp.sum(ps * inv * delta), 0.0)
            for i in range(0, len(heads), 2):
                pr = heads[i][0]
                dq_ref[:, LANES * pr:LANES * (pr + 1)] = jnp.where(low, dqs[i], dqs[i + 1])
            cols = slice(LANES * h, LANES * (h + 1))
            dkp_ref[:, cols] = dk[:B]
            dkc_ref[:, cols] = dk[B:]
            dvp_ref[:, cols] = dv[:B]
            dvc_ref[:, cols] = dv[B:]
            ds_ref[h] += dsink

    kv_shape = jax.ShapeDtypeStruct(kk.shape, f32)
    ds_shape = (n_kv, SUBLANES, LANES)
    return pl.pallas_call(
        body,
        out_shape=[jax.ShapeDtypeStruct((L, d_attn), f32), kv_shape, kv_shape, kv_shape, kv_shape, jax.ShapeDtypeStruct(ds_shape, f32)],
        grid=(nb,),
        in_specs=[q_spec(d_attn), q_spec(d_kk), prev(d_kk), q_spec(d_kk), prev(d_kk), pl.BlockSpec(sink_b.shape, lambda n: (0, 0)),
                  q_spec(d_attn), q_spec(d_attn)],
        out_specs=[q_spec(d_attn)] + [q_spec(d_kk)] * 4 + [pl.BlockSpec(ds_shape, lambda n: (0, 0, 0))],
        name="attn_bwd", compiler_params=_params("arbitrary"))(qr, kk, kk, vv, vv, sink_b, attn, d_attn_out)


SSM_T = 128
NQ = SUBLANES * SSM_STATE // LANES
NJ = SUBLANES


def _strided_put(ref, j, val):
    for q in range(NQ):
        ref.at[q][pl.ds(j, SSM_T, stride=NJ), :] = val[:, LANES * q:LANES * (q + 1)]


def _strided_get(ref, j):
    return jnp.concatenate([ref.at[q][pl.ds(j, SSM_T, stride=NJ), :] for q in range(NQ)], axis=1)


def _ssm_specs(L, rev):
    nt = L // SSM_T
    idx = (lambda i: nt - 1 - i) if rev else (lambda i: i)
    row = lambda w, cb=0: pl.BlockSpec((SSM_T, w), lambda i: (idx(i), cb))
    state = pl.BlockSpec((NQ, SSM_T * NJ, LANES), lambda i: (0, idx(i), 0))
    whole = lambda a: pl.BlockSpec(a.shape, lambda i: (0,) * a.ndim)
    return nt, row, state, whole


def ssm_fwd(u_bf, proj, u_cb, bd_re, bd_im, cd_re, cd_im, lam_re, lam_im, d_skip, dep=None):
    L, d_ssm = u_bf.shape
    nt, row, state, whole = _ssm_specs(L, False)
    half = d_ssm // 2
    gw = d_ssm // NJ
    n_dep = 0 if dep is None else 1

    def body(u_ref, u0_ref, u1_ref, bdr, bdi, cdr, cdi, lr_ref, li_ref, d_ref, *rest):
        y_ref, z_ref, sr_ref, si_ref, carry = rest[n_dep:]
        i = pl.program_id(0)

        @pl.when(i == 0)
        def _():
            carry[...] = jnp.zeros_like(carry)

        for j in range(NJ):
            uj = u_ref[:, gw * j:gw * (j + 1)]
            _strided_put(sr_ref, j, _dot(uj, bdr[j]))
            _strided_put(si_ref, j, _dot(uj, bdi[j]))
        lr = [lr_ref[q] for q in range(NQ)]
        li = [li_ref[q] for q in range(NQ)]

        def step(t, s):
            sr, si = s
            rows = pl.ds(pl.multiple_of(t * NJ, NJ), NJ)
            nr = tuple(lr[q] * sr[q] - li[q] * si[q] + sr_ref[q, rows, :] for q in range(NQ))
            ni = tuple(lr[q] * si[q] + li[q] * sr[q] + si_ref[q, rows, :] for q in range(NQ))
            for q in range(NQ):
                sr_ref[q, rows, :] = nr[q]
                si_ref[q, rows, :] = ni[q]
            return nr, ni

        init = (tuple(carry[0, q] for q in range(NQ)), tuple(carry[1, q] for q in range(NQ)))
        sr, si = lax.fori_loop(0, SSM_T, step, init, unroll=8)
        for q in range(NQ):
            carry[0, q] = sr[q]
            carry[1, q] = si[q]
        uf = jnp.concatenate([u0_ref[...], u1_ref[...]], axis=1)
        for j in range(NJ):
            cols = slice(gw * j, gw * (j + 1))
            yj = _dot(_strided_get(sr_ref, j).astype(bf16), cdr[j]) - _dot(_strided_get(si_ref, j).astype(bf16), cdi[j])
            yj = yj + d_ref[:, cols] * uf[:, cols]
            y_ref[:, cols] = yj
            z_ref[:, cols] = gelu(yj).astype(bf16)

    s_shape = jax.ShapeDtypeStruct((NQ, L * NJ, LANES), f32)
    consts = (bd_re, bd_im, cd_re, cd_im, lam_re, lam_im, d_skip)
    in_specs, operands, _ = _with_dep([row(d_ssm), row(half, u_cb), row(half, u_cb + 1)] + [whole(a) for a in consts],
                                      [u_bf, proj, proj, *consts], dep)
    return pl.pallas_call(
        body, out_shape=[jax.ShapeDtypeStruct((L, d_ssm), f32), jax.ShapeDtypeStruct((L, d_ssm), bf16), s_shape, s_shape], grid=(nt,),
        in_specs=in_specs, out_specs=[row(d_ssm), row(d_ssm), state, state],
        scratch_shapes=[pltpu.VMEM((2, NQ, NJ, LANES), f32)], name="ssm_fwd",
        compiler_params=_params("arbitrary"))(*operands)


def ssm_bwd(y, dz1, dz2, u_bf, proj, u_cb, s_re, s_im, bd_re, bd_im, cd_re, cd_im, lam_re, lam_im, d_skip, dep=None):
    L, d_ssm = y.shape
    nt, row, state, whole = _ssm_specs(L, True)
    half = d_ssm // 2
    gw = d_ssm // NJ
    n_dep = 0 if dep is None else 1

    def body(y_ref, dz1_ref, dz2_ref, u_ref, u0_ref, u1_ref, sr_ref, si_ref, bdr, bdi, cdr, cdi, lr_ref, li_ref, d_ref, *rest):
        du_ref, fbr, fbi, fcr, fci, dlr, dli, dd_ref, gr_ref, gi_ref, carry, dbdr, dbdi, dcdr, dcdi = rest[n_dep:]
        i = pl.program_id(0)

        @pl.when(i == 0)
        def _():
            carry[...] = jnp.zeros_like(carry)
            for r in (dbdr, dbdi, dcdr, dcdi, dlr, dli, dd_ref):
                r[...] = jnp.zeros_like(r)

        dyf = (dz1_ref[...] + dz2_ref[...]) * gelu_grad(y_ref[...])
        dyb = dyf.astype(bf16)
        for j in range(NJ):
            dyj = dyb[:, gw * j:gw * (j + 1)]
            _strided_put(gr_ref, j, _dot(dyj, cdr[j], NT))
            _strided_put(gi_ref, j, -_dot(dyj, cdi[j], NT))
            dcdr[j] += _dot(_strided_get(sr_ref, j).astype(bf16), dyj, TN)
            dcdi[j] -= _dot(_strided_get(si_ref, j).astype(bf16), dyj, TN)
        lr = [lr_ref[q] for q in range(NQ)]
        li = [li_ref[q] for q in range(NQ)]

        def step(k, c):
            gr, gi, ar, ai = c
            rows = pl.ds(pl.multiple_of((SSM_T - 1 - k) * NJ, NJ), NJ)
            s_r = [sr_ref[q, rows, :] for q in range(NQ)]
            s_i = [si_ref[q, rows, :] for q in range(NQ)]
            ar = tuple(ar[q] + gr[q] * s_r[q] + gi[q] * s_i[q] for q in range(NQ))
            ai = tuple(ai[q] + gi[q] * s_r[q] - gr[q] * s_i[q] for q in range(NQ))
            nr = tuple(gr_ref[q, rows, :] + lr[q] * gr[q] + li[q] * gi[q] for q in range(NQ))
            ni = tuple(gi_ref[q, rows, :] + lr[q] * gi[q] - li[q] * gr[q] for q in range(NQ))
            for q in range(NQ):
                gr_ref[q, rows, :] = nr[q]
                gi_ref[q, rows, :] = ni[q]
            return nr, ni, ar, ai

        zero = tuple(jnp.zeros((NJ, LANES), f32) for _ in range(NQ))
        init = (tuple(carry[0, q] for q in range(NQ)), tuple(carry[1, q] for q in range(NQ)), zero, zero)
        gr, gi, ar, ai = lax.fori_loop(0, SSM_T, step, init, unroll=8)
        for q in range(NQ):
            carry[0, q] = gr[q]
            carry[1, q] = gi[q]
            dlr[q] += ar[q]
            dli[q] += ai[q]
        uf = jnp.concatenate([u0_ref[...], u1_ref[...]], axis=1)
        dd_ref[...] += colsum(dyf * uf)
        for j in range(NJ):
            cols = slice(gw * j, gw * (j + 1))
            gjr, gji = _strided_get(gr_ref, j).astype(bf16), _strided_get(gi_ref, j).astype(bf16)
            du_ref[:, cols] = _dot(gjr, bdr[j], NT) + _dot(gji, bdi[j], NT) + d_ref[:, cols] * dyf[:, cols]
            uj = u_ref[:, cols]
            dbdr[j] += _dot(uj, gjr, TN)
            dbdi[j] += _dot(uj, gji, TN)

        @pl.when(i == nt - 1)
        def _():
            nb = NJ * SSM_STATE
            b_diag = (lax.broadcasted_iota(jnp.int32, (gw, nb), 0) // SSM_GROUP) == (lax.broadcasted_iota(jnp.int32, (gw, nb), 1) // SSM_STATE)
            c_diag = (lax.broadcasted_iota(jnp.int32, (nb, gw), 0) // SSM_STATE) == (lax.broadcasted_iota(jnp.int32, (nb, gw), 1) // SSM_GROUP)
            for j in range(NJ):
                for acc_ref, out in ((dbdr, fbr), (dbdi, fbi)):
                    m = jnp.where(b_diag, acc_ref[j], 0.0)
                    f = m[:, :LANES]
                    for q in range(1, nb // LANES):
                        f = f + m[:, LANES * q:LANES * (q + 1)]
                    out[j] = f + pltpu.roll(f, SSM_STATE, 1)
                for acc_ref, out in ((dcdr, fcr), (dcdi, fci)):
                    f = jnp.where(c_diag, acc_ref[j], 0.0)
                    for shift in (64, 32, 16):
                        f = f + pltpu.roll(f, shift, 1)
                    out[j] = f

    consts = (bd_re, bd_im, cd_re, cd_im, lam_re, lam_im, d_skip)
    acc = lambda a: jax.ShapeDtypeStruct(a.shape, f32)
    fb = jax.ShapeDtypeStruct((NJ, gw, LANES), f32)
    fc = jax.ShapeDtypeStruct((NJ, NJ * SSM_STATE, LANES), f32)
    outs = [jax.ShapeDtypeStruct((L, d_ssm), f32), fb, fb, fc, fc, acc(lam_re), acc(lam_im), acc(d_skip)]
    in_specs, operands, _ = _with_dep(
        [row(d_ssm)] * 4 + [row(half, u_cb), row(half, u_cb + 1), state, state] + [whole(a) for a in consts],
        [y, dz1, dz2, u_bf, proj, proj, s_re, s_im, *consts], dep)
    return pl.pallas_call(
        body, out_shape=outs, grid=(nt,),
        in_specs=in_specs, out_specs=[row(d_ssm)] + [whole(a) for a in outs[1:]],
        scratch_shapes=[pltpu.VMEM((NQ, SSM_T * NJ, LANES), f32), pltpu.VMEM((NQ, SSM_T * NJ, LANES), f32),
                        pltpu.VMEM((2, NQ, NJ, LANES), f32)] + [pltpu.VMEM(a.shape, f32) for a in (bd_re, bd_im, cd_re, cd_im)],
        name="ssm_bwd", compiler_params=_params("arbitrary"))(*operands)


def _cmul(ar, ai, br, bi):
    return ar * br - ai * bi, ar * bi + ai * br


def _disc(ar, ai, logdt):
    dt = jnp.exp(logdt)
    mag = jnp.exp(ar * dt)
    lr, li = mag * jnp.cos(ai * dt), mag * jnp.sin(ai * dt)
    den = ar * ar + ai * ai
    nr, ni = lr - 1.0, li
    fr, fi = (nr * ar + ni * ai) / den, (ni * ar - nr * ai) / den
    return dt, lr, li, den, fr, fi


def ssm_params(a_re, a_im, logdt_b, bt_re, bt_im, spread):
    def body(ar_ref, ai_ref, ld_ref, br_ref, bi_ref, sp_ref, lr_ref, li_ref, or_ref, oi_ref):
        _, lr, li, _, fr, fi = _disc(ar_ref[...], ai_ref[...], ld_ref[...])
        lr_ref[...] = lr
        li_ref[...] = li
        fre = jnp.dot(sp_ref[...], fr, precision=HIGHEST, preferred_element_type=f32)
        fie = jnp.dot(sp_ref[...], fi, precision=HIGHEST, preferred_element_type=f32)
        o_r, o_i = _cmul(fre, fie, br_ref[...], bi_ref[...])
        or_ref[...] = o_r
        oi_ref[...] = o_i

    g = jax.ShapeDtypeStruct(a_re.shape, f32)
    b = jax.ShapeDtypeStruct(bt_re.shape, f32)
    return pl.pallas_call(body, out_shape=[g, g, b, b], name="ssm_params",
                          compiler_params=_params())(a_re, a_im, logdt_b, bt_re, bt_im, spread)


def ssm_params_grad(a_re, a_im, logdt_b, bt_re, bt_im, spread, gather, dlam_re, dlam_im, dbt_re, dbt_im):
    def body(ar_ref, ai_ref, ld_ref, br_ref, bi_ref, sp_ref, ga_ref, glr_ref, gli_ref, gbr_ref, gbi_ref,
             dar_ref, dai_ref, dld_ref, dbr_ref, dbi_ref):
        ar, ai = ar_ref[...], ai_ref[...]
        dt, lr, li, den, fr, fi = _disc(ar, ai, ld_ref[...])
        hdot = functools.partial(jnp.dot, precision=HIGHEST, preferred_element_type=f32)
        fre, fie = hdot(sp_ref[...], fr), hdot(sp_ref[...], fi)
        gbr, gbi, br, bi = gbr_ref[...], gbi_ref[...], br_ref[...], bi_ref[...]
        dbr_ref[...], dbi_ref[...] = _cmul(fre, -fie, gbr, gbi)
        t_r, t_i = _cmul(br, -bi, gbr, gbi)
        gfr, gfi = hdot(ga_ref[...], t_r), hdot(ga_ref[...], t_i)
        iwr, iwi = ar / den, -ai / den
        x_r, x_i = _cmul(iwr, -iwi, gfr, gfi)
        glr, gli = glr_ref[...] + x_r, gli_ref[...] + x_i
        q_r, q_i = _cmul(fr, fi, iwr, iwi)
        gwr, gwi = _cmul(-q_r, q_i, gfr, gfi)
        y_r, y_i = _cmul(dt * lr, -dt * li, glr, gli)
        dar_ref[...] = gwr + y_r
        dai_ref[...] = gwi + y_i
        wl_r, wl_i = _cmul(ar, ai, lr, li)
        z_r, _ = _cmul(wl_r, -wl_i, glr, gli)
        dld_ref[...] = jnp.sum(z_r * dt, axis=1, keepdims=True)

    g = jax.ShapeDtypeStruct(a_re.shape, f32)
    b = jax.ShapeDtypeStruct(bt_re.shape, f32)
    return pl.pallas_call(body, out_shape=[g, g, jax.ShapeDtypeStruct((a_re.shape[0], 1), f32), b, b], name="ssm_params_grad",
                          compiler_params=_params())(a_re, a_im, logdt_b, bt_re, bt_im, spread, gather, dlam_re, dlam_im, dbt_re, dbt_im)


def _block_diag(t, rows, cols):
    G = t.shape[0]
    t = t.reshape(G // NJ, NJ, rows, cols)
    eye = jnp.eye(NJ, dtype=t.dtype)
    return jnp.einsum('jgrc,gh->jgrhc', t, eye).reshape(G // NJ, NJ * rows, NJ * cols)


def _state_layout(t):
    return t.reshape(NJ, NQ, LANES).transpose(1, 0, 2)


def _state_layout_inv(t, G, N):
    return t.transpose(1, 0, 2).reshape(G, N)


def _tiles2d(shape, budget_rows=128):
    rows, cols = shape
    tr = rows
    if rows > budget_rows:
        tr = budget_rows
        while rows % tr:
            tr -= SUBLANES
    return tr, cols


ADAM_TILE_BYTES = 3 << 19
ROW_ALIGN = 16


def _tile_rows(rows, row_bytes, target_bytes):
    tr = max(ROW_ALIGN, min(rows, target_bytes // row_bytes) // ROW_ALIGN * ROW_ALIGN)
    while rows % tr:
        tr -= ROW_ALIGN
    return tr


def _adam_update(w, g, m, v):
    c1 = 1.0 - ADAM_B1 ** ADAM_STEP
    c2 = 1.0 - ADAM_B2 ** ADAM_STEP
    nm = ADAM_B1 * m + (1.0 - ADAM_B1) * g
    nv = ADAM_B2 * v + (1.0 - ADAM_B2) * (g * g)
    delta = -ADAM_LR * ((nm / c1) / (jnp.sqrt(nv / c2) + ADAM_EPS) + ADAM_WD * w)
    return delta, nm, nv


def adamw_many(name, ws, gs, ms, vs):
    n = len(ws)

    def body(*refs):
        w, g, m, v = (refs[k * n:(k + 1) * n] for k in range(4))
        d, nm, nv = (refs[(4 + k) * n:(5 + k) * n] for k in range(3))
        for i in range(n):
            d[i][...], nm[i][...], nv[i][...] = _adam_update(w[i][...], g[i][...], m[i][...], v[i][...])

    o = [jax.ShapeDtypeStruct(a.shape, f32) for a in ws]
    outs = pl.pallas_call(body, out_shape=o * 3, name=name, compiler_params=_params())(*ws, *gs, *ms, *vs)
    return outs[:n], outs[n:2 * n], outs[2 * n:]


def adamw_halves(name, w, own, got, m, v, c_arr):
    h, cols = own.shape
    tr = _tile_rows(h, cols * 4, ADAM_TILE_BYTES)
    nh = h // tr

    def body(c_ref, w_ref, own_ref, got_ref, m_ref, v_ref, g_ref, d_ref, nm_ref, nv_ref):
        mine = (pl.program_id(0) // nh) == c_ref[0]
        g = jnp.where(mine, own_ref[...], got_ref[...])
        g_ref[...] = g
        d_ref[...], nm_ref[...], nv_ref[...] = _adam_update(w_ref[...], g, m_ref[...], v_ref[...])

    spec = pl.BlockSpec((tr, cols), lambda i, c: (i, 0))
    own_spec = pl.BlockSpec((tr, cols), lambda i, c: (jnp.where(i // nh == c[0], i % nh, 0), 0))
    got_spec = pl.BlockSpec((tr, cols), lambda i, c: (jnp.where(i // nh == c[0], 0, i % nh), 0))
    o = jax.ShapeDtypeStruct(w.shape, f32)
    grid_spec = pltpu.PrefetchScalarGridSpec(num_scalar_prefetch=1, grid=(2 * nh,),
                                             in_specs=[spec, own_spec, got_spec, spec, spec], out_specs=[spec] * 4)
    return pl.pallas_call(body, out_shape=[o, o, o, o], grid_spec=grid_spec, name=name,
                          compiler_params=_params("arbitrary"))(c_arr, w, own, got, m, v)


def pair_sum(name, g, got, c_arr):
    S, h, cols = got.shape
    tr, _ = _tiles2d((h, cols), 1024)
    nh = h // tr

    def body(c_ref, g_ref, o_ref, out_ref):
        out_ref[...] = (g_ref[...].astype(f32) + o_ref[...].astype(f32)).astype(out_ref.dtype)

    spec = pl.BlockSpec((None, tr, cols), lambda s, i, c: (s, i, 0))
    grid_spec = pltpu.PrefetchScalarGridSpec(
        num_scalar_prefetch=1, grid=(S, nh),
        in_specs=[pl.BlockSpec((None, tr, cols), lambda s, i, c: (s, c[0] * nh + i, 0)), spec], out_specs=spec)
    return pl.pallas_call(body, out_shape=jax.ShapeDtypeStruct(got.shape, g.dtype), grid_spec=grid_spec, name=name,
                          compiler_params=_params("parallel", "parallel"))(c_arr, g, got)


def chip_sum(name, pair, landed, mine_arr, dep=None):
    n_in, h, cols = landed.shape
    tr, _ = _tiles2d((h, cols), 256)

    def body(s_ref, p_ref, l_ref, *rest):
        acc = p_ref[...].astype(f32)
        for k in range(n_in):
            acc = acc + l_ref[k].astype(f32)
        rest[-1][...] = acc

    in_specs, operands, _ = _with_dep(
        [pl.BlockSpec((None, tr, cols), lambda i, s: (s[0], i, 0)), pl.BlockSpec((n_in, tr, cols), lambda i, s: (0, i, 0))],
        [pair, landed], dep)
    grid_spec = pltpu.PrefetchScalarGridSpec(num_scalar_prefetch=1, grid=(h // tr,), in_specs=in_specs,
                                             out_specs=pl.BlockSpec((tr, cols), lambda i, s: (i, 0)))
    return pl.pallas_call(body, out_shape=jax.ShapeDtypeStruct((h, cols), f32), grid_spec=grid_spec, name=name,
                          compiler_params=_params("parallel"))(mine_arr, *operands)


def into_slot(name, w, slot_arr, n_slots, dtype, dep=None):
    tr, cols = _tiles2d(w.shape, 256)

    def body(s_ref, w_ref, *rest):
        rest[-1][...] = w_ref[...].astype(dtype)

    in_specs, operands, _ = _with_dep([pl.BlockSpec((tr, cols), lambda i, s: (i, 0))], [w], dep)
    grid_spec = pltpu.PrefetchScalarGridSpec(num_scalar_prefetch=1, grid=(w.shape[0] // tr,), in_specs=in_specs,
                                             out_specs=pl.BlockSpec((None, tr, cols), lambda i, s: (s[0], i, 0)))
    return pl.pallas_call(body, out_shape=jax.ShapeDtypeStruct((n_slots,) + w.shape, dtype), grid_spec=grid_spec, name=name,
                          compiler_params=_params("parallel"))(slot_arr, *operands)


def sum_slots(name, t):
    S, rows, cols = t.shape
    tr, _ = _tiles2d((rows, cols), 256)

    def body(t_ref, o_ref):
        acc = t_ref[0]
        for s in range(1, S):
            acc = acc + t_ref[s]
        o_ref[...] = acc

    return pl.pallas_call(body, out_shape=jax.ShapeDtypeStruct((rows, cols), f32), grid=(rows // tr,),
                          in_specs=[pl.BlockSpec((S, tr, cols), lambda i: (0, i, 0))], out_specs=pl.BlockSpec((tr, cols), lambda i: (i, 0)),
                          name=name, compiler_params=_params("parallel"))(t)


def _place():
    x, y, c = lax.axis_index("x"), lax.axis_index("y"), lax.axis_index("c")
    return x, y, c


def _other_chips(x, y):
    return [(1 - x, y, 2 * (1 - x) + y), (x, 1 - y, 2 * x + 1 - y), (1 - x, 1 - y, 2 * (1 - x) + 1 - y)]


SEM = pl.BlockSpec(memory_space=pltpu.SEMAPHORE)
VM = pl.BlockSpec(memory_space=pltpu.VMEM)
DATAFLOW = pltpu.SideEffectType.DATAFLOW_SIDE_EFFECTING
TOKEN = jax.ShapeDtypeStruct((SUBLANES, LANES), f32)


def _gather_copy(buf, w, k, chip, c, mine, send, recv):
    px, py, _ = chip
    h = buf.shape[1] // 2
    half = buf.at[mine, pl.ds(c * h, h), :]
    return pltpu.make_async_remote_copy(src_ref=half, dst_ref=half, send_sem=send.at[3 * w + k], recv_sem=recv.at[3 * w + k],
                                        device_id=(px, py, c), device_id_type=MESH)


def _gather_landing(buf, w, k, chip, c, send, recv):
    px, py, s = chip
    h = buf.shape[1] // 2
    landed = buf.at[s, pl.ds(c * h, h), :]
    return pltpu.make_async_remote_copy(src_ref=landed, dst_ref=landed, send_sem=send.at[3 * w + k], recv_sem=recv.at[3 * w + k],
                                        device_id=(px, py, c), device_id_type=MESH)


def gather_start(name, bufs, groups, after, neighbours_only=()):
    nw, ng = len(bufs), len(groups)

    def body(*refs):
        outs = refs[nw + 1:]
        sems, dst = outs[:2 * ng], outs[2 * ng:2 * ng + nw]
        token = outs[2 * ng + nw]
        x, y, c = _place()
        mine = 2 * x + y
        for g, members in enumerate(groups):
            for i, w in enumerate(members):
                for k, chip in enumerate(_other_chips(x, y)[:2 if w in neighbours_only else 3]):
                    _gather_copy(dst[w], i, k, chip, c, mine, sems[2 * g], sems[2 * g + 1]).start()
        token[...] = jnp.zeros_like(token)

    sem_shapes = []
    for members in groups:
        sem_shapes += [pltpu.SemaphoreType.DMA((3 * len(members),))] * 2
    outs = pl.pallas_call(
        body, out_shape=sem_shapes + [jax.ShapeDtypeStruct(b.shape, b.dtype) for b in bufs] + [TOKEN],
        in_specs=[ANY] * (nw + 1), out_specs=[SEM] * (2 * ng) + [ANY] * nw + [VM],
        input_output_aliases={w: 2 * ng + w for w in range(nw)}, name=name,
        compiler_params=pltpu.CompilerParams(has_side_effects=DATAFLOW))(*bufs, after)
    return [(outs[2 * g], outs[2 * g + 1]) for g in range(ng)], list(outs[2 * ng:2 * ng + nw]), outs[2 * ng + nw]


def gather_wait(name, bufs, send, recv, after):
    nw = len(bufs)

    def body(*refs):
        src = refs[:nw]
        send_ref, recv_ref = refs[nw], refs[nw + 1]
        x, y, c = _place()
        mine = 2 * x + y
        for w in range(nw):
            for k, chip in enumerate(_other_chips(x, y)):
                _gather_copy(src[w], w, k, chip, c, mine, send_ref, recv_ref).wait_send()
                _gather_landing(src[w], w, k, chip, c, send_ref, recv_ref).wait_recv()

    return pl.pallas_call(
        body, out_shape=[jax.ShapeDtypeStruct(b.shape, b.dtype) for b in bufs],
        in_specs=[ANY] * nw + [SEM, SEM, ANY], out_specs=[ANY] * nw,
        input_output_aliases={w: w for w in range(nw)}, name=name,
        compiler_params=pltpu.CompilerParams(has_side_effects=DATAFLOW))(*bufs, send, recv, after)


def _relay_copy(buf, w, j, x, y, c, send, recv, landing):
    chips = _other_chips(x, y)
    px, py, _ = chips[j]
    h = buf.shape[1] // 2
    q = h // 2
    s = chips[2][2] if landing else chips[1 - j][2]
    part = buf.at[s, pl.ds(c * h + j * q, q), :]
    return pltpu.make_async_remote_copy(src_ref=part, dst_ref=part, send_sem=send.at[2 * w + j], recv_sem=recv.at[2 * w + j],
                                        device_id=(px, py, c), device_id_type=MESH)


def _early_pass(buf, nw, w, k, x, y, c, send, recv, landing):
    s = _other_chips(x, y)[k][2]
    h = buf.shape[1] // 2
    part = buf.at[s, pl.ds(((1 - c) if landing else c) * h, h), :]
    i = 2 * nw + 2 * w + k
    return pltpu.make_async_remote_copy(src_ref=part, dst_ref=part, send_sem=send.at[i], recv_sem=recv.at[i],
                                        device_id=(x, y, 1 - c), device_id_type=MESH)


def gather_relay(name, bufs, sems, more, after):
    nw, nm = len(bufs), len(more)
    ns = 4 if nm else 2

    def body(*refs):
        ins, outs = refs[:nw + nm + 2 * nw + 1], refs[nw + nm + 2 * nw + 1:]
        src, d_sems = ins[:nw], ins[nw + nm:nw + nm + 2 * nw]
        r_send, r_recv = outs[:2]
        m_send, m_recv = outs[2:ns] if nm else (None, None)
        dst, mdst, token = outs[ns:ns + nw], outs[ns + nw:ns + nw + nm], outs[ns + nw + nm]
        x, y, c = _place()
        mine = 2 * x + y
        chips = _other_chips(x, y)
        for w in range(nw):
            for k in range(2):
                _gather_copy(src[w], 0, k, chips[k], c, mine, d_sems[2 * w], d_sems[2 * w + 1]).wait_send()
                _gather_landing(src[w], 0, k, chips[k], c, d_sems[2 * w], d_sems[2 * w + 1]).wait_recv()
            for j in range(2):
                _relay_copy(dst[w], w, j, x, y, c, r_send, r_recv, False).start()
            for k in range(2):
                _early_pass(dst[w], nw, w, k, x, y, c, r_send, r_recv, False).start()
        for w in range(nm):
            for k, chip in enumerate(chips):
                _gather_copy(mdst[w], w, k, chip, c, mine, m_send, m_recv).start()
        token[...] = jnp.zeros_like(token)

    sem_shapes = [pltpu.SemaphoreType.DMA((4 * nw,))] * 2 + [pltpu.SemaphoreType.DMA((3 * nm,))] * (ns - 2)
    flat_sems = [s for pair in sems for s in pair]
    outs = pl.pallas_call(
        body, out_shape=sem_shapes + [jax.ShapeDtypeStruct(b.shape, b.dtype) for b in list(bufs) + list(more)] + [TOKEN],
        in_specs=[ANY] * (nw + nm) + [SEM] * (2 * nw) + [ANY], out_specs=[SEM] * ns + [ANY] * (nw + nm) + [VM],
        input_output_aliases={i: ns + i for i in range(nw + nm)}, name=name,
        compiler_params=pltpu.CompilerParams(has_side_effects=DATAFLOW))(*bufs, *more, *flat_sems, after)
    m_sems = (outs[2], outs[3]) if nm else None
    return outs[0], outs[1], m_sems, list(outs[ns:ns + nw]), list(outs[ns + nw:ns + nw + nm]), outs[ns + nw + nm]


def gather_wait_relay(name, bufs, r_send, r_recv, after):
    nw = len(bufs)

    def body(*refs):
        src = refs[:nw]
        send_ref, recv_ref = refs[nw], refs[nw + 1]
        x, y, c = _place()
        for w in range(nw):
            for j in range(2):
                _relay_copy(src[w], w, j, x, y, c, send_ref, recv_ref, False).wait_send()
                _relay_copy(src[w], w, j, x, y, c, send_ref, recv_ref, True).wait_recv()
                _early_pass(src[w], nw, w, j, x, y, c, send_ref, recv_ref, False).wait_send()
                _early_pass(src[w], nw, w, j, x, y, c, send_ref, recv_ref, True).wait_recv()

    return pl.pallas_call(
        body, out_shape=[jax.ShapeDtypeStruct(b.shape, b.dtype) for b in bufs],
        in_specs=[ANY] * nw + [SEM, SEM, ANY], out_specs=[ANY] * nw,
        input_output_aliases={w: w for w in range(nw)}, name=name,
        compiler_params=pltpu.CompilerParams(has_side_effects=DATAFLOW))(*bufs, r_send, r_recv, after)


def gather_forward(name, bufs, which=(0, 1, 2)):
    nw = len(bufs)

    def body(*refs):
        dst = refs[nw:2 * nw]
        send, recv = refs[2 * nw:]
        x, y, c = _place()
        sib = (x, y, 1 - c)
        barrier = pltpu.get_barrier_semaphore()
        pl.semaphore_signal(barrier, inc=1, device_id=sib, device_id_type=MESH)
        pl.semaphore_wait(barrier, 1)
        cps = []
        for w in range(nw):
            h = dst[w].shape[1] // 2
            for k in which:
                s = _other_chips(x, y)[k][2]
                landed = dst[w].at[s, pl.ds(c * h, h), :]
                cp = pltpu.make_async_remote_copy(src_ref=landed, dst_ref=landed, send_sem=send.at[w, k], recv_sem=recv.at[w, k],
                                                  device_id=sib, device_id_type=MESH)
                cp.start()
                cps.append(cp)
        for w in range(nw):
            h = dst[w].shape[1] // 2
            for k in which:
                s = _other_chips(x, y)[k][2]
                other = dst[w].at[s, pl.ds((1 - c) * h, h), :]
                pltpu.make_async_remote_copy(src_ref=other, dst_ref=other, send_sem=send.at[w, k], recv_sem=recv.at[w, k],
                                             device_id=sib, device_id_type=MESH).wait_recv()
        for cp in cps:
            cp.wait_send()

    sem = pltpu.SemaphoreType.DMA((nw, 3))
    return pl.pallas_call(
        body, out_shape=[jax.ShapeDtypeStruct(b.shape, b.dtype) for b in bufs],
        in_specs=[ANY] * nw, out_specs=[ANY] * nw, input_output_aliases={w: w for w in range(nw)},
        scratch_shapes=[sem, sem], name=name,
        compiler_params=pltpu.CompilerParams(has_side_effects=True, collective_id=SIBLING_PAIR))(*bufs)


def _scatter_copy(src, dst, w, k, chip, c, send, recv):
    px, py, s = chip
    return pltpu.make_async_remote_copy(src_ref=src.at[s], dst_ref=dst.at[k], send_sem=send.at[3 * w + k], recv_sem=recv.at[3 * w + k],
                                        device_id=(px, py, c), device_id_type=MESH)


def scatter_start(name, parts):
    nw = len(parts)
    lands = [pltpu.with_memory_space_constraint(lax.empty((N_CHIPS - 1,) + p.shape[1:], p.dtype), pltpu.HBM) for p in parts]

    def body(*refs):
        outs = refs[2 * nw:]
        send, recv = outs[0], outs[1]
        src, dst, token = outs[2:2 + nw], outs[2 + nw:2 + 2 * nw], outs[2 + 2 * nw]
        x, y, c = _place()
        for w in range(nw):
            for k, chip in enumerate(_other_chips(x, y)):
                _scatter_copy(src[w], dst[w], w, k, chip, c, send, recv).start()
        token[...] = jnp.zeros_like(token)

    sem = pltpu.SemaphoreType.DMA((3 * nw,))
    outs = pl.pallas_call(
        body, out_shape=[sem, sem] + [jax.ShapeDtypeStruct(p.shape, p.dtype) for p in parts]
        + [jax.ShapeDtypeStruct(l.shape, l.dtype) for l in lands] + [TOKEN],
        in_specs=[ANY] * (2 * nw), out_specs=[SEM, SEM] + [ANY] * (2 * nw) + [VM],
        input_output_aliases={i: 2 + i for i in range(2 * nw)}, name=name,
        compiler_params=pltpu.CompilerParams(has_side_effects=DATAFLOW))(*parts, *lands)
    return outs[0], outs[1], list(outs[2:2 + nw]), list(outs[2 + nw:2 + 2 * nw]), outs[2 + 2 * nw]


def scatter_wait(name, parts, lands, send, recv, after):
    nw = len(parts)

    def body(*refs):
        src, dst = refs[:nw], refs[nw:2 * nw]
        send_ref, recv_ref = refs[2 * nw], refs[2 * nw + 1]
        x, y, c = _place()
        for w in range(nw):
            for k, chip in enumerate(_other_chips(x, y)):
                cp = _scatter_copy(src[w], dst[w], w, k, chip, c, send_ref, recv_ref)
                cp.wait_send()
                cp.wait_recv()

    outs = pl.pallas_call(
        body, out_shape=[jax.ShapeDtypeStruct(a.shape, a.dtype) for a in list(parts) + list(lands)],
        in_specs=[ANY] * (2 * nw) + [SEM, SEM, ANY], out_specs=[ANY] * (2 * nw),
        input_output_aliases={i: i for i in range(2 * nw)}, name=name,
        compiler_params=pltpu.CompilerParams(has_side_effects=DATAFLOW))(*parts, *lands, send, recv, after)
    return list(outs[:nw]), list(outs[nw:])


SIBLING_PAIR = 0


def _sibling_copy(src, dst, w, c, half_rows, send, recv, sib):
    if half_rows:
        h = src.shape[1] // 2
        src = src.at[:, pl.ds((1 - c) * h, h), :]
    return pltpu.make_async_remote_copy(src_ref=src, dst_ref=dst, send_sem=send.at[w], recv_sem=recv.at[w],
                                        device_id=sib, device_id_type=MESH)


def _landing(shape, dtype):
    return pltpu.with_memory_space_constraint(lax.empty(shape, dtype), pltpu.HBM)


def sibling_start(name, srcs, half_rows):
    nw = len(srcs)
    lands = [_landing((s.shape[0], s.shape[1] // 2, s.shape[2]) if half_rows else s.shape, s.dtype) for s in srcs]

    def body(*refs):
        outs = refs[2 * nw:]
        send, recv = outs[0], outs[1]
        src, dst, token = outs[2:2 + nw], outs[2 + nw:2 + 2 * nw], outs[2 + 2 * nw]
        x, y, c = _place()
        barrier = pltpu.get_barrier_semaphore()
        pl.semaphore_signal(barrier, inc=1, device_id=(x, y, 1 - c), device_id_type=MESH)
        pl.semaphore_wait(barrier, 1)
        for w in range(nw):
            _sibling_copy(src[w], dst[w], w, c, half_rows, send, recv, (x, y, 1 - c)).start()
        token[...] = jnp.zeros_like(token)

    sem = pltpu.SemaphoreType.DMA((nw,))
    outs = pl.pallas_call(
        body, out_shape=[sem, sem] + [jax.ShapeDtypeStruct(a.shape, a.dtype) for a in list(srcs) + lands] + [TOKEN],
        in_specs=[ANY] * (2 * nw), out_specs=[SEM, SEM] + [ANY] * (2 * nw) + [VM],
        input_output_aliases={i: 2 + i for i in range(2 * nw)}, name=name,
        compiler_params=pltpu.CompilerParams(has_side_effects=DATAFLOW, collective_id=SIBLING_PAIR))(*srcs, *lands)
    return outs[0], outs[1], list(outs[2:2 + nw]), list(outs[2 + nw:2 + 2 * nw]), outs[2 + 2 * nw]


def sibling_wait(name, srcs, lands, send, recv, half_rows, after):
    nw = len(srcs)

    def body(*refs):
        src, dst = refs[:nw], refs[nw:2 * nw]
        send_ref, recv_ref = refs[2 * nw], refs[2 * nw + 1]
        x, y, c = _place()
        for w in range(nw):
            cp = _sibling_copy(src[w], dst[w], w, c, half_rows, send_ref, recv_ref, (x, y, 1 - c))
            cp.wait_send()
            cp.wait_recv()

    outs = pl.pallas_call(
        body, out_shape=[jax.ShapeDtypeStruct(a.shape, a.dtype) for a in list(srcs) + list(lands)],
        in_specs=[ANY] * (2 * nw) + [SEM, SEM, ANY], out_specs=[ANY] * (2 * nw),
        input_output_aliases={i: i for i in range(2 * nw)}, name=name,
        compiler_params=pltpu.CompilerParams(has_side_effects=DATAFLOW))(*srcs, *lands, send, recv, after)
    return list(outs[:nw]), list(outs[nw:])


def _peer(x, y, c, r):
    return (1 - x if r & 4 else x, 1 - y if r & 2 else y, 1 - c if r & 1 else c)


def _everyone_copy(buf, r, x, y, c, send, recv, landing):
    px, py, pc = _peer(x, y, c, r)
    slot = buf.at[4 * px + 2 * py + pc] if landing else buf.at[4 * x + 2 * y + c]
    return pltpu.make_async_remote_copy(src_ref=slot, dst_ref=slot, send_sem=send.at[r - 1], recv_sem=recv.at[r - 1],
                                        device_id=(px, py, pc), device_id_type=MESH)


def everyone_start(name, buf):
    def body(buf_in, send, recv, buf_ref, token):
        x, y, c = _place()
        for r in range(1, N_DEV):
            _everyone_copy(buf_ref, r, x, y, c, send, recv, False).start()
        token[...] = jnp.zeros_like(token)

    sem = pltpu.SemaphoreType.DMA((N_DEV - 1,))
    return pl.pallas_call(
        body, out_shape=[sem, sem, jax.ShapeDtypeStruct(buf.shape, buf.dtype), TOKEN],
        in_specs=[ANY], out_specs=[SEM, SEM, ANY, VM], input_output_aliases={0: 2}, name=name,
        compiler_params=pltpu.CompilerParams(has_side_effects=DATAFLOW))(buf)


def everyone_wait(name, buf, send, recv, after):
    def body(buf_ref, send_ref, recv_ref, after_ref, out_ref):
        x, y, c = _place()
        for r in range(1, N_DEV):
            _everyone_copy(buf_ref, r, x, y, c, send_ref, recv_ref, False).wait_send()
            _everyone_copy(buf_ref, r, x, y, c, send_ref, recv_ref, True).wait_recv()

    return pl.pallas_call(
        body, out_shape=jax.ShapeDtypeStruct(buf.shape, buf.dtype), in_specs=[ANY, SEM, SEM, ANY], out_specs=ANY,
        input_output_aliases={0: 0}, name=name,
        compiler_params=pltpu.CompilerParams(has_side_effects=DATAFLOW))(buf, send, recv, after)


def local_step(x, pos, tgt, small, d_in, get_w, put_g, first_dep=None, tick=lambda name, after: None):
    L, D = x.shape
    d_kv = N_KV_HEADS * HEAD_DIM
    d_ssm = small["d_skip"].shape[1]
    d_attn = d_in - 2 * d_kv - d_ssm
    big = {}
    G = d_ssm // SSM_GROUP
    N, P = SSM_STATE, SSM_GROUP
    gbf = bf16

    half_dim = HEAD_DIM // 2
    inv_freq = ROPE_THETA ** (-jnp.arange(half_dim, dtype=f32) / half_dim)
    inv_freq = jnp.tile(inv_freq, LANES // half_dim).reshape(1, LANES)
    sink_b = jnp.broadcast_to(small["sinks"].reshape(-1, 1), (small["sinks"].size, LANES))

    spread = jnp.repeat(jnp.eye(G, dtype=f32), P, axis=0)
    logdt_b = jnp.broadcast_to(small["log_dt"].reshape(G, 1), (G, N))
    bt_re = small["b_re"].reshape(G, N, P).transpose(0, 2, 1).reshape(G * P, N)
    bt_im = small["b_im"].reshape(G, N, P).transpose(0, 2, 1).reshape(G * P, N)
    a_re, a_im = small["a_re"].reshape(G, N), small["a_im"].reshape(G, N)
    lam_re, lam_im, bbt_re, bbt_im = ssm_params(a_re, a_im, logdt_b, bt_re, bt_im, spread)
    bd_re = _block_diag(bbt_re.reshape(G, P, N), P, N).astype(bf16)
    bd_im = _block_diag(bbt_im.reshape(G, P, N), P, N).astype(bf16)
    c_re = small["c_re"].reshape(G, P, N).transpose(0, 2, 1)
    c_im = small["c_im"].reshape(G, P, N).transpose(0, 2, 1)
    cd_re = _block_diag(c_re, N, P).astype(bf16)
    cd_im = _block_diag(c_im, N, P).astype(bf16)
    lam_re_l, lam_im_l = _state_layout(lam_re), _state_layout(lam_im)

    def k1(i, nt, xt, g):
        return (rms_fwd(xt, g),)
    xn = rowwise("pre_mix_norm", k1, L, [full(x)], [small["g_pre_mix"]], [(D, bf16)], dep=first_dep)[0]
    big["w_in"] = get_w("w_in", xn)
    proj = mm_nn("proj_in", xn, big["w_in"])
    qr, kk, vv, u_bf = qkv_prep(proj, pos, inv_freq, d_attn, d_kv)
    attn = attn_fwd(qr, kk, vv, sink_b)
    u_cb = (d_attn + 2 * d_kv) // (d_ssm // 2)
    token = tick("attn", attn)
    y, z_bf, s_re, s_im = ssm_fwd(u_bf, proj, u_cb, bd_re, bd_im, cd_re, cd_im, lam_re_l, lam_im_l, small["d_skip"], dep=token)
    token = tick("ssm", z_bf)
    big["w_glu"] = get_w("w_glu", z_bf)
    gl = mm_nn("glu_proj", z_bf, big["w_glu"], dep=token)

    def k6(i, nt, at, yt, glt, bg, ga, gs):
        ssm = gelu(yt) * sigmoid(glt + bg)
        return (jnp.concatenate([rms_fwd(at, ga), rms_fwd(ssm, gs)], axis=1),)
    mixed = rowwise("mix_norms", k6, L, [full(attn), full(y), full(gl)],
                    [small["b_glu"], small["g_attn_out"], small["g_ssm_out"]], [(d_attn + d_ssm, bf16)])[0]
    big["w_o"] = get_w("w_o", mixed)
    mix = mm_nn("proj_out", mixed, big["w_o"])

    def k7(i, nt, xt, mt, gpm, gpf):
        h = xt + rms_fwd(mt, gpm)
        return h, rms_fwd(h, gpf)
    h, hn = rowwise("post_mix", k7, L, [full(x), full(mix)], [small["g_post_mix"], small["g_pre_ffn"]], [(D, f32), (D, bf16)])
    big["w_gate"] = get_w("w_gate", hn)
    big["w_up"] = get_w("w_up", hn)
    hid_dg, hid_du, hid = ffn_hidden(hn, big["w_gate"], big["w_up"])
    d_ff_dim = hid.shape[1]
    big["w_down"] = get_w("w_down", hid)
    ff = mm_nn("ffn_down", hid, big["w_down"], tk=d_ff_dim // 2)

    def k9(i, nt, ht, fft, tt, g):
        out = ht + rms_fwd(fft, g)
        err = out - tt
        per_row = jnp.mean(err * err, axis=-1, keepdims=True)
        loss = 0.5 * jnp.sum(per_row) * jnp.where(_lane((1, LANES)) == 0, 1.0, 0.0)
        d_out = err * (1.0 / D)
        d_ff, dg = rms_bwd(fft, g, d_out)
        return d_out, d_ff, dg, loss
    d_out, d_ff, dg_post_ffn, loss = rowwise("loss_head", k9, L, [full(h), full(ff), full(tgt)], [small["g_post_ffn"]],
                                             [(D, f32), (D, bf16)], reds=[D, LANES])

    d_gt, d_up = ffn_hidden_grad(d_ff, big["w_down"], hid_dg, hid_du)
    token = put_g("w_down", hid, d_ff, None)
    d_hn = mm_nt_pair("d_hn", d_gt, big["w_gate"], d_up, big["w_up"], dep=token)
    token = put_g("w_gate", hn, d_gt, None)
    token = put_g("w_up", hn, d_up, token)

    def k11(i, nt, ht, da, do, mt, gpf, gpm):
        dh_n, dg_pf = rms_bwd(ht, gpf, da)
        dh = do + dh_n
        d_mix, dg_pm = rms_bwd(mt, gpm, dh)
        return dh, d_mix, dg_pf, dg_pm
    dh, d_mix, dg_pre_ffn, dg_post_mix = rowwise("post_mix_grad", k11, L, [full(h), full(d_hn), full(d_out), full(mix)],
                                                 [small["g_pre_ffn"], small["g_post_mix"]], [(D, f32), (D, bf16)], reds=[D, D], dep=token)
    d_mixed = mm_nt("d_mixed", d_mix, big["w_o"])
    token = put_g("w_o", mixed, d_mix, None)

    def k12(i, nt, at, yt, glt, da_n, ds_n, bg, ga, gs):
        z = gelu(yt)
        sg = sigmoid(glt + bg)
        ssm = z * sg
        d_at, dga = rms_bwd(at, ga, da_n)
        d_ssm_t, dgs = rms_bwd(ssm, gs, ds_n)
        d_gl = d_ssm_t * z * sg * (1.0 - sg)
        return d_at, d_ssm_t * sg, d_gl, dga, dgs, colsum(d_gl)
    d_attn_o, dz1, d_gl, dg_attn, dg_ssm, db_glu = rowwise(
        "mix_norms_grad", k12, L, [full(attn), full(y), full(gl), (d_mixed, d_attn, 0, 0), (d_mixed, d_ssm, d_attn // d_ssm, 0)],
        [small["b_glu"], small["g_attn_out"], small["g_ssm_out"]], [(d_attn, f32), (d_ssm, f32), (d_ssm, bf16)],
        reds=[d_attn, d_ssm, d_ssm], dep=token)
    dz2 = mm_nt("d_glu_in", d_gl, big["w_glu"])
    token = put_g("w_glu", z_bf, d_gl, None)

    du, dbd_re, dbd_im, dcd_re, dcd_im, dlam_re_l, dlam_im_l, dd_skip = ssm_bwd(
        y, dz1, dz2, u_bf, proj, u_cb, s_re, s_im, bd_re, bd_im, cd_re, cd_im, lam_re_l, lam_im_l, small["d_skip"], dep=token)
    dq, dkk_c, dkk_p, dvv_c, dvv_p, dsink = attn_bwd(qr, kk, vv, sink_b, attn, d_attn_o)
    d_proj = qkv_grad(dq, dkk_c, dkk_p, dvv_c, dvv_p, du, pos, inv_freq)
    d_xn = mm_nt("d_xn", d_proj, big["w_in"])
    token = put_g("w_in", xn, d_proj, None)

    def k17(i, nt, xt, dxn, dht, g):
        dx, dg = rms_bwd(xt, g, dxn)
        return dht + dx, dg
    grad_x, dg_pre_mix = rowwise("pre_mix_grad", k17, L, [full(x), full(d_xn), full(dh)], [small["g_pre_mix"]],
                                 [(D, f32)], reds=[D], dep=token)

    gather = spread.T
    dbbt_re = dbd_re.reshape(G * P, LANES)[:, :N]
    dbbt_im = dbd_im.reshape(G * P, LANES)[:, :N]
    dc_re = dcd_re.reshape(G, N, LANES)[:, :, :P].transpose(0, 2, 1)
    dc_im = dcd_im.reshape(G, N, LANES)[:, :, :P].transpose(0, 2, 1)
    d_a_re, d_a_im, d_logdt, dbt_re, dbt_im = ssm_params_grad(
        a_re, a_im, logdt_b, bt_re, bt_im, spread, gather,
        _state_layout_inv(dlam_re_l, G, N), _state_layout_inv(dlam_im_l, G, N), dbbt_re, dbbt_im)
    q_per_kv = d_attn // HEAD_DIM // N_KV_HEADS
    small_grads = {
        "g_pre_mix": dg_pre_mix, "sinks": dsink[:, :q_per_kv, 0].reshape(1, -1),
        "a_re": d_a_re, "a_im": d_a_im, "log_dt": d_logdt.reshape(1, G),
        "b_re": dbt_re, "b_im": dbt_im,
        "c_re": dc_re, "c_im": dc_im,
        "d_skip": dd_skip, "b_glu": db_glu, "g_attn_out": dg_attn, "g_ssm_out": dg_ssm,
        "g_post_mix": dg_post_mix, "g_pre_ffn": dg_pre_ffn, "g_post_ffn": dg_post_ffn,
    }
    return loss, grad_x, small_grads


WEIGHTS = ['g_pre_mix', 'w_in', 'sinks', 'a_re', 'a_im', 'log_dt', 'b_re', 'b_im', 'c_re', 'c_im', 'd_skip', 'w_glu', 'b_glu',
           'g_attn_out', 'g_ssm_out', 'w_o', 'g_post_mix', 'g_pre_ffn', 'w_gate', 'w_up', 'w_down', 'g_post_ffn']
BIG = ['w_in', 'w_glu', 'w_o', 'w_gate', 'w_up', 'w_down']
COL_SHARDED = ['w_in', 'w_gate', 'w_up']
SMALL = [n for n in WEIGHTS if n not in BIG]
GATHER_GROUPS = [["w_in"], ["w_glu", "w_o"], ["w_gate", "w_up"], ["w_down"]]
REDUCE_GROUPS = [["w_down", "w_gate", "w_up"], ["w_o", "w_glu", "w_in"]]


PACK_ROWS = 256


def _pack(parts):
    flat = jnp.concatenate([p.reshape(-1) for p in parts])
    pad = (-flat.size) % (PACK_ROWS * LANES)
    return jnp.pad(flat, (0, pad)).reshape(-1, LANES)


TRANSPOSED_VIEW = ("b_re", "b_im")


def small_view(name, a):
    if name in TRANSPOSED_VIEW:
        a = a.transpose(0, 1, 3, 2)
    return a.reshape(-1, a.shape[-1])


def small_unview(name, p, shape):
    if name in TRANSPOSED_VIEW:
        return p.reshape(shape[0], shape[1], shape[3], shape[2]).transpose(0, 1, 3, 2)
    return p.reshape(shape)


def _unpack(packed, shapes):
    flat = packed.reshape(-1)
    out, off = [], 0
    for s in shapes:
        n = int(np.prod(s))
        out.append(flat[off:off + n].reshape(s))
        off += n
    return out


def kernel(x, positions, g_pre_mix, w_in, sinks, a_re, a_im, log_dt, b_re, b_im, c_re, c_im, d_skip, w_glu, b_glu, g_attn_out, g_ssm_out, w_o, g_post_mix, g_pre_ffn, w_gate, w_up, w_down, g_post_ffn, loss_target, m_g_pre_mix, m_w_in, m_sinks, m_a_re, m_a_im, m_log_dt, m_b_re, m_b_im, m_c_re, m_c_im, m_d_skip, m_w_glu, m_b_glu, m_g_attn_out, m_g_ssm_out, m_w_o, m_g_post_mix, m_g_pre_ffn, m_w_gate, m_w_up, m_w_down, m_g_post_ffn, v_g_pre_mix, v_w_in, v_sinks, v_a_re, v_a_im, v_log_dt, v_b_re, v_b_im, v_c_re, v_c_im, v_d_skip, v_w_glu, v_b_glu, v_g_attn_out, v_g_ssm_out, v_w_o, v_g_post_mix, v_g_pre_ffn, v_w_gate, v_w_up, v_w_down, v_g_post_ffn):
    args = dict(locals())
    w = {n: args[n] for n in WEIGHTS}
    m = {n: args["m_" + n] for n in WEIGHTS}
    v = {n: args["v_" + n] for n in WEIGHTS}
    L, D = x.shape[1], x.shape[2]

    ax, ay, ac = _place()
    mine_arr = (2 * ax + ay).astype(jnp.int32).reshape(1)
    c_arr = ac.astype(jnp.int32).reshape(1)

    me_arr = (4 * ax + 2 * ay + ac).astype(jnp.int32).reshape(1)

    bufs = {"w_in": into_slot("cast_w_in", w["w_in"][0], mine_arr, N_CHIPS, bf16)}
    (first_sems,), (bufs["w_in"],), token = gather_start("gather_start_in", [bufs["w_in"]], [[0]], mine_arr, neighbours_only=(0,))
    sems, relays, ready = {"w_in": first_sems}, {}, set()

    def relay(n, more, after):
        r_send, r_recv, more_sems, (bufs[n],), started, tok = gather_relay(
            "gather_relay_" + n, [bufs[n]], [sems[n]], [bufs[k] for k in more], after)
        sems.update({k: more_sems for k in more})
        bufs.update(zip(more, started))
        relays[n] = (r_send, r_recv)
        return tok

    for n in BIG[1:]:
        bufs[n] = into_slot("cast_" + n, w[n][0], mine_arr, N_CHIPS, bf16, dep=token)
    token = relay("w_in", [], bufs[BIG[-1]])
    first = ["w_gate", "w_glu", "w_o", "w_up"]
    (sems["w_gate"], sems["w_glu"], sems["w_up"]), started, token = gather_start(
        "gather_start_rest", [bufs[n] for n in first], [[0], [1, 2], [3]], token, neighbours_only=(0, 3))
    bufs.update(zip(first, started))

    def tick(name, after):
        return relay("w_gate", [], after) if name == "attn" else relay("w_up", ["w_down"], after)

    def get_w(n, after):
        if n not in ready:
            members = [g for g in GATHER_GROUPS if n in g][0]
            if members[0] in relays:
                landed = [gather_wait_relay("gather_wait_" + k, [bufs[k]], *relays[k], after)[0] for k in members]
                which = (2,)
            else:
                landed = gather_wait("gather_wait_" + members[0], [bufs[k] for k in members], *sems[members[0]], after)
                which = (0, 1, 2)
            bufs.update(zip(members, gather_forward("gather_forward_" + members[0], landed, which)))
            ready.update(members)
        g = bufs[n]
        return g if n in COL_SHARDED else g.reshape(g.shape[0] * g.shape[1], g.shape[2])

    swaps, operands, inflight = {}, {}, []
    other_arr = 1 - c_arr

    def put_g(n, a, b, dep):
        halves = (a.shape[1] // N_CHIPS // 2) % LANES == 0 if n not in COL_SHARDED else True
        if halves:
            operands[n] = (a, b)
            sent = mm_tn_half("d%s_sent" % n, a, b, n in COL_SHARDED, other_arr, dep=dep)
            swaps[n] = sibling_start("swap_start_" + n, [sent], False)
        else:
            g = weight_grad(n, a, b, dep)
            swaps[n] = sibling_start("swap_start_" + n, [g.reshape(N_CHIPS, g.shape[0] // N_CHIPS, g.shape[1])], True)
        for gi, members in enumerate(REDUCE_GROUPS):
            if n == members[-1]:
                last = swaps[n][4]
                pair = []
                for k in members:
                    send, recv, srcs, lands, _ = swaps[k]
                    (src,), (got,) = sibling_wait("swap_wait_" + k, srcs, lands, send, recv, k not in operands, last)
                    if k in operands:
                        pair.append(mm_tn_half("d%s_kept" % k, *operands[k], k in COL_SHARDED, c_arr, addend=got))
                    else:
                        pair.append(pair_sum("pair_sum_" + k, src, got, c_arr))
                send, recv, parts, lands, tok = scatter_start("scatter_start_%d" % gi, pair)
                inflight.append((members, send, recv, parts, lands))
                return tok
        return swaps[n][4]

    small = {n: w[n].reshape(1, -1) for n in SMALL}
    pos = positions.reshape(L, 1).astype(f32)
    d_in = N_CHIPS * w["w_in"].shape[2]
    loss, grad_x, small_grads = local_step(x[0], pos, loss_target[0], small, d_in, get_w, put_g, first_dep=token, tick=tick)

    shapes = [w[n].shape for n in SMALL]
    blocks = into_slot("small_block", _pack([small_grads[n] for n in SMALL] + [loss]), me_arr, N_DEV, f32)
    small_send, small_recv, blocks, after = everyone_start("small_start", blocks)

    grads, delta, new_m, new_v = {}, {}, {}, {}
    for gi, (members, send, recv, parts, lands) in enumerate(inflight):
        parts, landed = scatter_wait("scatter_wait_%d" % gi, parts, lands, send, recv, after)
        joins, dep = [], None
        for k, p, t in zip(members, parts, landed):
            joins.append(sibling_start("join_start_" + k, [chip_sum("chip_sum_" + k, p, t, mine_arr, dep=dep)], False))
            dep = after = joins[-1][4]
        for n, (send, recv, srcs, lands, _) in zip(members, joins):
            (own,), (sib,) = sibling_wait("join_wait_" + n, srcs, lands, send, recv, False, after)
            g_, d_, m_, v_ = adamw_halves("adamw_" + n, w[n][0], own, sib, m[n][0], v[n][0], c_arr)
            grads[n], delta[n], new_m[n], new_v[n] = g_[None], d_[None], m_[None], v_[None]
            after = v_
    blocks = everyone_wait("small_wait", blocks, small_send, small_recv, after)
    small_sum = sum_slots("small_sum", blocks)
    *small_g, loss = _unpack(small_sum, [small_view(n, w[n]).shape for n in SMALL] + [loss.shape])
    loss = loss[0, 0]
    outs = adamw_many("adamw_small", [small_view(n, w[n]) for n in SMALL], small_g,
                      [small_view(n, m[n]) for n in SMALL], [small_view(n, v[n]) for n in SMALL])
    for t, parts in zip((grads, delta, new_m, new_v), (small_g,) + tuple(outs)):
        t.update({n: small_unview(n, p, w[n].shape) for n, p in zip(SMALL, parts)})

    return (loss, grad_x[None], *[grads[n] for n in WEIGHTS], *[delta[n] for n in WEIGHTS],
            *[new_m[n] for n in WEIGHTS], *[new_v[n] for n in WEIGHTS])
```

```python
import functools
import math

import jax
import jax.numpy as jnp
import numpy as np
from jax import lax
from jax.experimental import pallas as pl
from jax.experimental.pallas import tpu as pltpu

f32 = jnp.float32
bf16 = jnp.bfloat16
HIGHEST = lax.Precision.HIGHEST
MESH = pl.DeviceIdType.MESH

HEAD_DIM = 64
N_KV_HEADS = 4
ATTN_BLOCK = 128
ROPE_THETA = 10000.0
SSM_GROUP = 16
SSM_STATE = 64
RMS_EPS = 1e-6
LANES = 128
SUBLANES = 8
VMEM_LIMIT = 52 * 1024 * 1024
N_CHIPS = 4
N_DEV = 8
NEG = -1e30

ADAM_LR, ADAM_B1, ADAM_B2, ADAM_EPS, ADAM_WD, ADAM_STEP = 0.001, 0.9, 0.999, 1e-08, 0.01, 10

NN = (((1,), (0,)), ((), ()))
NT = (((1,), (1,)), ((), ()))
TN = (((0,), (0,)), ((), ()))


def _params(*sem):
    return pltpu.CompilerParams(dimension_semantics=sem or None, vmem_limit_bytes=VMEM_LIMIT)


def _dot(a, b, dims=NN):
    return lax.dot_general(a, b, dims, preferred_element_type=f32)


def _pick(dim, pref):
    t = min(dim, pref)
    while dim % t:
        t -= LANES
    assert t > 0, (dim, pref)
    return t


ANY = pl.BlockSpec(memory_space=pl.ANY)


def _with_dep(in_specs, operands, dep):
    if dep is None:
        return list(in_specs), list(operands), 0
    return list(in_specs) + [ANY], list(operands) + [dep], 1


def _mm_call(name, grid, in_specs, out_spec, out_shape, acc_shape, dims, operands, dep=None):
    nk = grid[2]
    in_specs, operands, n_dep = _with_dep(in_specs, operands, dep)

    def body_one(a_ref, b_ref, *rest):
        o_ref = rest[n_dep]
        o_ref[...] = _dot(a_ref[...], b_ref[...], dims).astype(o_ref.dtype)

    def body(a_ref, b_ref, *rest):
        o_ref, acc_ref = rest[n_dep], rest[n_dep + 1]
        k = pl.program_id(2)

        @pl.when(k == 0)
        def _():
            acc_ref[...] = _dot(a_ref[...], b_ref[...], dims)

        @pl.when((k > 0) & (k < nk - 1))
        def _():
            acc_ref[...] += _dot(a_ref[...], b_ref[...], dims)

        @pl.when(k == nk - 1)
        def _():
            o_ref[...] = (acc_ref[...] + _dot(a_ref[...], b_ref[...], dims)).astype(o_ref.dtype)

    return pl.pallas_call(
        body_one if nk == 1 else body, out_shape=out_shape, grid=grid, in_specs=in_specs, out_specs=out_spec,
        scratch_shapes=[] if nk == 1 else [pltpu.VMEM(acc_shape, f32)], name=name,
        compiler_params=_params("parallel", "parallel", "arbitrary"))(*operands)


def mm_nt_pair(name, a1, b1, a2, b2, tm=1024, tk=1024, dep=None):
    M = a1.shape[0]
    S, K, n = b1.shape
    tm, tko = _pick(M, tm), _pick(K, tk)
    nk = 2 * S

    def body(a1_ref, b1_ref, a2_ref, b2_ref, *rest):
        o_ref, acc_ref = rest[-2], rest[-1]
        k = pl.program_id(2)

        @pl.when(k == 0)
        def _():
            acc_ref[...] = _dot(a1_ref[...], b1_ref[...], NT)

        @pl.when((k > 0) & (k < S))
        def _():
            acc_ref[...] += _dot(a1_ref[...], b1_ref[...], NT)

        @pl.when((k >= S) & (k < nk - 1))
        def _():
            acc_ref[...] += _dot(a2_ref[...], b2_ref[...], NT)

        @pl.when(k == nk - 1)
        def _():
            o_ref[...] = acc_ref[...] + _dot(a2_ref[...], b2_ref[...], NT)

    first = lambda k: jnp.minimum(k, S - 1)
    second = lambda k: jnp.maximum(k - S, 0)
    in_specs = [pl.BlockSpec((tm, n), lambda i, j, k: (i, first(k))), pl.BlockSpec((None, tko, n), lambda i, j, k: (first(k), j, 0)),
                pl.BlockSpec((tm, n), lambda i, j, k: (i, second(k))), pl.BlockSpec((None, tko, n), lambda i, j, k: (second(k), j, 0))]
    in_specs, operands, _ = _with_dep(in_specs, (a1, b1, a2, b2), dep)
    return pl.pallas_call(
        body, out_shape=jax.ShapeDtypeStruct((M, K), f32), grid=(M // tm, K // tko, nk), in_specs=in_specs,
        out_specs=pl.BlockSpec((tm, tko), lambda i, j, k: (i, j)), scratch_shapes=[pltpu.VMEM((tm, tko), f32)], name=name,
        compiler_params=_params("parallel", "parallel", "arbitrary"))(*operands)


def mm_nn(name, a, b, out_dtype=f32, tm=1024, tn=1024, tk=2048, dep=None):
    M, K = a.shape
    tm, tk = _pick(M, tm), _pick(K, tk)
    if b.ndim == 3:
        S, _, n = b.shape
        tn = _pick(n, 2048)
        per = n // tn
        b_spec = pl.BlockSpec((None, tk, tn), lambda i, j, k: (j // per, k, j % per))
        N = S * n
    else:
        N = b.shape[1]
        tn = _pick(N, tn)
        b_spec = pl.BlockSpec((tk, tn), lambda i, j, k: (k, j))
    grid = (M // tm, N // tn, K // tk)
    return _mm_call(name, grid, [pl.BlockSpec((tm, tk), lambda i, j, k: (i, k)), b_spec],
                    pl.BlockSpec((tm, tn), lambda i, j, k: (i, j)), jax.ShapeDtypeStruct((M, N), out_dtype),
                    (tm, tn), NN, (a, b), dep)


def mm_nt(name, a, b, out_dtype=f32, tm=1024, tn=2048, tk=1024, dep=None):
    M, N = a.shape
    tm = _pick(M, tm)
    if b.ndim == 3:
        S, K, n = b.shape
        tr = _pick(n, 2048)
        per = n // tr
        tko = _pick(K, tk)
        b_spec = pl.BlockSpec((None, tko, tr), lambda i, j, k: (k // per, j, k % per))
    else:
        K = b.shape[0]
        tr = _pick(N, tn)
        tko = _pick(K, tk)
        b_spec = pl.BlockSpec((tko, tr), lambda i, j, k: (j, k))
    grid = (M // tm, K // tko, N // tr)
    return _mm_call(name, grid, [pl.BlockSpec((tm, tr), lambda i, j, k: (i, k)), b_spec],
                    pl.BlockSpec((tm, tko), lambda i, j, k: (i, j)), jax.ShapeDtypeStruct((M, K), out_dtype),
                    (tm, tko), NT, (a, b), dep)


def mm_tn(name, a, b, shards=None, out_dtype=f32, tm=1024, tn=1024, tl=2048, dep=None):
    L, K = a.shape
    N = b.shape[1]
    tl, tko = _pick(L, tl), _pick(K, tm)
    if shards:
        n = N // shards
        tn = _pick(n, 2048)
        per = n // tn
        o_spec = pl.BlockSpec((None, tko, tn), lambda i, j, k: (j // per, i, j % per))
        o_shape = jax.ShapeDtypeStruct((shards, K, n), out_dtype)
    else:
        tn = _pick(N, tn)
        o_spec = pl.BlockSpec((tko, tn), lambda i, j, k: (i, j))
        o_shape = jax.ShapeDtypeStruct((K, N), out_dtype)
    grid = (K // tko, N // tn, L // tl)
    return _mm_call(name, grid, [pl.BlockSpec((tl, tko), lambda i, j, k: (k, i)),
                                 pl.BlockSpec((tl, tn), lambda i, j, k: (k, j))],
                    o_spec, o_shape, (tko, tn), TN, (a, b), dep)


def mm_tn_half(name, a, b, col_sharded, half_arr, addend=None, dep=None, tn=2048):
    L, K = a.shape
    N = b.shape[1]
    S = N_CHIPS
    h, cols = (K // 2, N // S) if col_sharded else (K // S // 2, N)
    tko, tn = _pick(h, 1024), _pick(cols, tn)
    ni, nj = h // tko, cols // tn
    if col_sharded:
        a_map = lambda s, i, j, hf: (0, hf[0] * ni + i)
        b_map = lambda s, i, j, hf: (0, s * nj + j)
    else:
        a_map = lambda s, i, j, hf: (0, (2 * s + hf[0]) * ni + i)
        b_map = lambda s, i, j, hf: (0, j)
    o_spec = pl.BlockSpec((None, tko, tn), lambda s, i, j, hf: (s, i, j))
    n_add = 0 if addend is None else 1
    n_dep = 0 if dep is None else 1

    def body(hf_ref, a_ref, b_ref, *rest):
        o_ref = rest[n_add + n_dep]
        acc = _dot(a_ref[...], b_ref[...], TN)
        if n_add:
            acc = acc + rest[0][...].astype(f32)
        o_ref[...] = acc.astype(o_ref.dtype)

    in_specs = [pl.BlockSpec((L, tko), a_map), pl.BlockSpec((L, tn), b_map)] + [o_spec] * n_add
    in_specs, operands, _ = _with_dep(in_specs, [a, b] + ([addend] if n_add else []), dep)
    grid_spec = pltpu.PrefetchScalarGridSpec(num_scalar_prefetch=1, grid=(S, ni, nj), in_specs=in_specs, out_specs=o_spec)
    return pl.pallas_call(body, out_shape=jax.ShapeDtypeStruct((S, h, cols), bf16), grid_spec=grid_spec, name=name,
                          compiler_params=_params("parallel", "parallel", "parallel"))(half_arr, *operands)


def weight_grad(name, a, b, dep=None):
    if name in COL_SHARDED:
        return mm_tn("d" + name, a, b, shards=N_CHIPS, out_dtype=bf16, dep=dep)
    return mm_tn("d" + name, a, b, out_dtype=bf16, tm=a.shape[1] // N_CHIPS, dep=dep)


def ffn_hidden(hn, w_gate, w_up, tm=512):
    M, K = hn.shape
    S, _, n = w_gate.shape
    tm = _pick(M, tm)

    def body(a_ref, g_ref, u_ref, dg_ref, du_ref, hid_ref):
        a = a_ref[...]
        g = _dot(a, g_ref[...])
        u = _dot(a, u_ref[...])
        sg = sigmoid(g)
        act = g * sg
        dg_ref[...] = (u * (sg * (1.0 + g * (1.0 - sg)))).astype(bf16)
        du_ref[...] = act.astype(bf16)
        hid_ref[...] = (act * u).astype(bf16)

    w_spec = pl.BlockSpec((None, K, n), lambda s, i: (s, 0, 0))
    o_spec = pl.BlockSpec((tm, n), lambda s, i: (i, s))
    o = jax.ShapeDtypeStruct((M, S * n), bf16)
    return pl.pallas_call(
        body, out_shape=[o, o, o], grid=(S, M // tm), in_specs=[pl.BlockSpec((tm, K), lambda s, i: (i, 0)), w_spec, w_spec],
        out_specs=[o_spec, o_spec, o_spec], name="ffn_hidden", compiler_params=_params("parallel", "parallel"))(hn, w_gate, w_up)


def ffn_hidden_grad(d_ff, w_down, hid_dg, hid_du, tm=512):
    M, D = d_ff.shape
    F = w_down.shape[0]
    n = _pick(F // N_CHIPS, 2048)
    tm = _pick(M, tm)

    def body(a_ref, b_ref, pg_ref, pu_ref, dg_ref, du_ref):
        dh = _dot(a_ref[...], b_ref[...], NT)
        dg_ref[...] = (dh * pg_ref[...].astype(f32)).astype(bf16)
        du_ref[...] = (dh * pu_ref[...].astype(f32)).astype(bf16)

    t_spec = pl.BlockSpec((tm, n), lambda j, i: (i, j))
    o = jax.ShapeDtypeStruct((M, F), bf16)
    return pl.pallas_call(
        body, out_shape=[o, o], grid=(F // n, M // tm),
        in_specs=[pl.BlockSpec((tm, D), lambda j, i: (i, 0)), pl.BlockSpec((n, D), lambda j, i: (j, 0)), t_spec, t_spec],
        out_specs=[t_spec, t_spec], name="ffn_hidden_grad", compiler_params=_params("parallel", "parallel"))(d_ff, w_down, hid_dg, hid_du)


def rowwise(name, fn, L, rows, bcast, outs, reds=(), tr=256, dep=None):
    tr = min(tr, L)
    nt = L // tr
    n_rows, n_b, n_o = len(rows), len(bcast), len(outs)
    n_dep = 0 if dep is None else 1

    def body(*refs):
        i = pl.program_id(0)
        ins = [r[...] for r in refs[:n_rows + n_b]]
        res = fn(i, nt, *ins)
        o_refs = refs[n_rows + n_b + n_dep:]
        for k in range(n_o):
            o_refs[k][...] = res[k].astype(o_refs[k].dtype)
        if reds:
            @pl.when(i == 0)
            def _():
                for k in range(len(reds)):
                    o_refs[n_o + k][...] = jnp.zeros_like(o_refs[n_o + k])
            for k in range(len(reds)):
                o_refs[n_o + k][...] += res[n_o + k]

    def row_spec(width, cb, shift):
        if shift:
            return pl.BlockSpec((tr, width), lambda i: (jnp.minimum(i + shift, nt - 1), cb))
        return pl.BlockSpec((tr, width), lambda i: (i, cb))

    in_specs = [row_spec(w, cb, sh) for (_, w, cb, sh) in rows]
    in_specs += [pl.BlockSpec(b.shape, lambda i: (0, 0)) for b in bcast]
    out_specs = [pl.BlockSpec((tr, w), lambda i: (i, 0)) for (w, _) in outs]
    out_specs += [pl.BlockSpec((1, w), lambda i: (0, 0)) for w in reds]
    out_shape = [jax.ShapeDtypeStruct((L, w), dt) for (w, dt) in outs]
    out_shape += [jax.ShapeDtypeStruct((1, w), f32) for w in reds]
    in_specs, operands, _ = _with_dep(in_specs, [r[0] for r in rows] + list(bcast), dep)
    return pl.pallas_call(
        body, out_shape=out_shape, grid=(nt,), in_specs=in_specs, out_specs=out_specs, name=name,
        compiler_params=_params("arbitrary"))(*operands)


def full(a):
    return (a, a.shape[1], 0, 0)


def colsum(v):
    return jnp.sum(v, axis=0, keepdims=True)


def rms_fwd(x, g):
    r = lax.rsqrt(jnp.mean(x * x, axis=-1, keepdims=True) + RMS_EPS)
    return x * r * g


def rms_bwd(x, g, dy):
    r = lax.rsqrt(jnp.mean(x * x, axis=-1, keepdims=True) + RMS_EPS)
    xh = x * r
    dyg = dy * g
    dx = r * (dyg - xh * jnp.mean(dyg * xh, axis=-1, keepdims=True))
    return dx, colsum(dy * xh)


GELU_C = math.sqrt(2.0 / math.pi)


def gelu(y):
    return y * (0.5 * (1.0 + jnp.tanh(GELU_C * (y + 0.044715 * (y * y * y)))))


def gelu_grad(y):
    t = jnp.tanh(GELU_C * (y + 0.044715 * (y * y * y)))
    return 0.5 * (1.0 + t) + 0.5 * y * (1.0 - t * t) * (GELU_C * (1.0 + 3 * 0.044715 * (y * y)))


def sigmoid(v):
    return 1.0 / (1.0 + jnp.exp(-v))


def _lane(shape):
    return lax.broadcasted_iota(jnp.int32, shape, 1)


def _rot_chunk(t, cos, sin_signed):
    first = (_lane(t.shape) % HEAD_DIM) < (HEAD_DIM // 2)
    partner = jnp.where(first, pltpu.roll(t, LANES - HEAD_DIM // 2, 1), pltpu.roll(t, HEAD_DIM // 2, 1))
    return t * cos + partner * sin_signed


def _cos_sin(pos, inv_freq, inverse):
    ang = pos * inv_freq
    cos, sin = jnp.cos(ang), jnp.sin(ang)
    first = (_lane(ang.shape) % HEAD_DIM) < (HEAD_DIM // 2)
    sign = jnp.where(first, -1.0, 1.0) * (-1.0 if inverse else 1.0)
    return cos, sin * sign


def _dup_head(chunk, odd):
    low = _lane(chunk.shape) < HEAD_DIM
    x = jnp.where(low != odd, chunk, 0.0)
    return x + pltpu.roll(x, HEAD_DIM, 1)


def _chunks(v):
    return [v[:, LANES * c:LANES * (c + 1)] for c in range(v.shape[1] // LANES)]


def qkv_prep(proj, pos, inv_freq, d_attn, d_kv):
    L = proj.shape[0]
    d_ssm = proj.shape[1] - d_attn - 2 * d_kv
    half = d_ssm // 2
    scale = 1.0 / math.sqrt(HEAD_DIM)

    def fn(i, nt, q, k, v, u0, u1, p, invf):
        cos, sin = _cos_sin(p, invf, False)
        qr = jnp.concatenate([_rot_chunk(c, cos, sin) for c in _chunks(q)], axis=1) * scale
        kr = [_rot_chunk(c, cos, sin) for c in _chunks(k)]
        kk = jnp.concatenate([_dup_head(c, odd) for c in kr for odd in (False, True)], axis=1)
        vv = jnp.concatenate([_dup_head(c, odd) for c in _chunks(v) for odd in (False, True)], axis=1)
        return qr, kk, vv, jnp.concatenate([u0, u1], axis=1)

    u_cb = (d_attn + 2 * d_kv) // half
    return rowwise("qkv_prep", fn, L,
                   [(proj, d_attn, 0, 0), (proj, d_kv, d_attn // d_kv, 0), (proj, d_kv, d_attn // d_kv + 1, 0),
                    (proj, half, u_cb, 0), (proj, half, u_cb + 1, 0), full(pos)],
                   [inv_freq], [(d_attn, bf16), (2 * d_kv, bf16), (2 * d_kv, bf16), (d_ssm, bf16)])


def qkv_grad(dq, dkk_c, dkk_p, dvv_c, dvv_p, du, pos, inv_freq):
    L, d_attn = dq.shape
    d_kv = dkk_c.shape[1] // 2
    scale = 1.0 / math.sqrt(HEAD_DIM)

    def fold(cur, prev, i, nt):
        t = cur + jnp.where(i < nt - 1, prev, 0.0)
        out = []
        for c in range(t.shape[1] // (2 * LANES)):
            even, odd = t[:, 2 * c * LANES:(2 * c + 1) * LANES], t[:, (2 * c + 1) * LANES:(2 * c + 2) * LANES]
            even, odd = even + pltpu.roll(even, HEAD_DIM, 1), odd + pltpu.roll(odd, HEAD_DIM, 1)
            out.append(jnp.where(_lane(even.shape) < HEAD_DIM, even, odd))
        return out

    def fn(i, nt, dq_t, kc, kp, vc, vp, du_t, p, invf):
        cos, sin = _cos_sin(p, invf, True)
        dq_o = jnp.concatenate([_rot_chunk(c, cos, sin) for c in _chunks(dq_t)], axis=1) * scale
        dk_o = jnp.concatenate([_rot_chunk(c, cos, sin) for c in fold(kc, kp, i, nt)], axis=1)
        dv_o = jnp.concatenate(fold(vc, vp, i, nt), axis=1)
        return (jnp.concatenate([dq_o, dk_o, dv_o, du_t], axis=1),)

    return rowwise("qkv_grad", fn, L,
                   [full(dq), full(dkk_c), (dkk_p, 2 * d_kv, 0, 1), full(dvv_c), (dvv_p, 2 * d_kv, 0, 1), full(du), full(pos)],
                   [inv_freq], [(d_attn + 2 * d_kv + du.shape[1], bf16)], tr=ATTN_BLOCK)[0]


def _attn_specs(L):
    nb = L // ATTN_BLOCK
    B = ATTN_BLOCK
    q_spec = lambda width: pl.BlockSpec((B, width), lambda n: (n, 0))
    prev = lambda width: pl.BlockSpec((B, width), lambda n: (jnp.maximum(n - 1, 0), 0))
    return nb, q_spec, prev


def _attn_mask(n):
    B = ATTN_BLOCK
    row = lax.broadcasted_iota(jnp.int32, (B, 2 * B), 0)
    col = lax.broadcasted_iota(jnp.int32, (B, 2 * B), 1)
    return ((col < B) & (col > row) & (n > 0)) | ((col >= B) & (row >= col - B))


def _attn_probs(qm, kcat, sink, mask):
    s = jnp.where(mask, _dot(qm, kcat, NT), NEG)
    m = jnp.maximum(jnp.max(s, axis=1, keepdims=True), sink)
    p, ps = jnp.exp(s - m), jnp.exp(sink - m)
    inv = 1.0 / (jnp.sum(p, axis=1, keepdims=True) + ps)
    return p, inv, ps


def _attn_heads(q_ref, s_ref, h, q_per_kv):
    low = _lane((ATTN_BLOCK, LANES)) < HEAD_DIM
    heads = []
    for pr in range(h * q_per_kv // 2, (h + 1) * q_per_kv // 2):
        q2 = q_ref[:, LANES * pr:LANES * (pr + 1)]
        for odd in (False, True):
            mine = low != odd
            sink = jnp.max(s_ref[2 * pr + int(odd):2 * pr + int(odd) + 1, :], axis=1, keepdims=True)
            heads.append((pr, mine, jnp.where(mine, q2, jnp.zeros_like(q2)), sink))
    return low, heads


def _kv_block(prev_ref, cur_ref, h):
    return jnp.concatenate([prev_ref[:, LANES * h:LANES * (h + 1)], cur_ref[:, LANES * h:LANES * (h + 1)]], axis=0)


def attn_fwd(qr, kk, vv, sink_b):
    L, d_attn = qr.shape
    nb, q_spec, prev = _attn_specs(L)
    d_kk = kk.shape[1]
    n_kv = d_kk // LANES
    q_per_kv = d_attn // HEAD_DIM // n_kv

    def body(q_ref, kc_ref, kp_ref, vc_ref, vp_ref, s_ref, o_ref):
        mask = _attn_mask(pl.program_id(0))
        for h in range(n_kv):
            kcat, vcat = _kv_block(kp_ref, kc_ref, h), _kv_block(vp_ref, vc_ref, h)
            low, heads = _attn_heads(q_ref, s_ref, h, q_per_kv)
            probs = [_attn_probs(qm, kcat, sink, mask) for (_, _, qm, sink) in heads]
            outs = [_dot(p.astype(bf16), vcat) * inv for (p, inv, _) in probs]
            for i in range(0, len(heads), 2):
                pr = heads[i][0]
                o_ref[:, LANES * pr:LANES * (pr + 1)] = jnp.where(low, outs[i], outs[i + 1])

    return pl.pallas_call(
        body, out_shape=jax.ShapeDtypeStruct((L, d_attn), f32), grid=(nb,),
        in_specs=[q_spec(d_attn), q_spec(d_kk), prev(d_kk), q_spec(d_kk), prev(d_kk), pl.BlockSpec(sink_b.shape, lambda n: (0, 0))],
        out_specs=q_spec(d_attn), name="attn_fwd", compiler_params=_params("arbitrary"))(qr, kk, kk, vv, vv, sink_b)


def attn_bwd(qr, kk, vv, sink_b, attn, d_attn_out):
    L, d_attn = qr.shape
    nb, q_spec, prev = _attn_specs(L)
    d_kk = kk.shape[1]
    n_kv = d_kk // LANES
    q_per_kv = d_attn // HEAD_DIM // n_kv

    def body(q_ref, kc_ref, kp_ref, vc_ref, vp_ref, s_ref, o_ref, do_ref, dq_ref, dkc_ref, dkp_ref, dvc_ref, dvp_ref, ds_ref):
        n = pl.program_id(0)
        B = ATTN_BLOCK
        mask = _attn_mask(n)
        srow = lax.broadcasted_iota(jnp.int32, (SUBLANES, LANES), 0)

        @pl.when(n == 0)
        def _():
            ds_ref[...] = jnp.zeros_like(ds_ref)

        for h in range(n_kv):
            kcat, vcat = _kv_block(kp_ref, kc_ref, h), _kv_block(vp_ref, vc_ref, h)
            low, heads = _attn_heads(q_ref, s_ref, h, q_per_kv)
            probs = [_attn_probs(qm, kcat, sink, mask) for (_, _, qm, sink) in heads]
            dk = jnp.zeros((2 * B, LANES), f32)
            dv = dk
            dsink = jnp.zeros((SUBLANES, LANES), f32)
            dqs = []
            for i, ((pr, mine, qm, _), (p, inv, ps)) in enumerate(zip(heads, probs)):
                do2 = do_ref[:, LANES * pr:LANES * (pr + 1)]
                delta = jnp.sum(jnp.where(mine, do2 * o_ref[:, LANES * pr:LANES * (pr + 1)], 0.0), axis=1, keepdims=True)
                dob = jnp.where(mine, do2, 0.0).astype(bf16)
                p = p * inv
                ds = (p * (_dot(dob, vcat, NT) - delta)).astype(bf16)
                dqs.append(_dot(ds, kcat))
                dk = dk + _dot(ds, qm, TN)
                dv = dv + _dot(p.astype(bf16), dob, TN)
                dsink = dsink + jnp.where(srow == i, -jnp.sum(ps * inv * delta), 0.0)
            for i in range(0, len(heads), 2):
                pr = heads[i][0]
                dq_ref[:, LANES * pr:LANES * (pr + 1)] = jnp.where(low, dqs[i], dqs[i + 1])
            cols = slice(LANES * h, LANES * (h + 1))
            dkp_ref[:, cols] = dk[:B]
            dkc_ref[:, cols] = dk[B:]
            dvp_ref[:, cols] = dv[:B]
            dvc_ref[:, cols] = dv[B:]
            ds_ref[h] += dsink

    kv_shape = jax.ShapeDtypeStruct(kk.shape, f32)
    ds_shape = (n_kv, SUBLANES, LANES)
    return pl.pallas_call(
        body,
        out_shape=[jax.ShapeDtypeStruct((L, d_attn), f32), kv_shape, kv_shape, kv_shape, kv_shape, jax.ShapeDtypeStruct(ds_shape, f32)],
        grid=(nb,),
        in_specs=[q_spec(d_attn), q_spec(d_kk), prev(d_kk), q_spec(d_kk), prev(d_kk), pl.BlockSpec(sink_b.shape, lambda n: (0, 0)),
                  q_spec(d_attn), q_spec(d_attn)],
        out_specs=[q_spec(d_attn)] + [q_spec(d_kk)] * 4 + [pl.BlockSpec(ds_shape, lambda n: (0, 0, 0))],
        name="attn_bwd", compiler_params=_params("arbitrary"))(qr, kk, kk, vv, vv, sink_b, attn, d_attn_out)


SSM_T = 128
NQ = SUBLANES * SSM_STATE // LANES
NJ = SUBLANES


SCAN_UNROLL = 8


def _unrolled(step):
    def body(k, carry):
        for u in range(SCAN_UNROLL):
            carry = step(k * SCAN_UNROLL + u, carry)
        return carry
    return body


def _strided_put(ref, j, val):
    for q in range(NQ):
        ref.at[q][pl.ds(j, SSM_T, stride=NJ), :] = val[:, LANES * q:LANES * (q + 1)]


def _strided_get(ref, j):
    return jnp.concatenate([ref.at[q][pl.ds(j, SSM_T, stride=NJ), :] for q in range(NQ)], axis=1)


def _ssm_specs(L, rev):
    nt = L // SSM_T
    idx = (lambda i: nt - 1 - i) if rev else (lambda i: i)
    row = lambda w, cb=0: pl.BlockSpec((SSM_T, w), lambda i: (idx(i), cb))
    state = pl.BlockSpec((NQ, SSM_T * NJ, LANES), lambda i: (0, idx(i), 0))
    whole = lambda a: pl.BlockSpec(a.shape, lambda i: (0,) * a.ndim)
    return nt, row, state, whole


def ssm_fwd(u_bf, proj, u_cb, bd_re, bd_im, cd_re, cd_im, lam_re, lam_im, d_skip, dep=None):
    L, d_ssm = u_bf.shape
    nt, row, state, whole = _ssm_specs(L, False)
    half = d_ssm // 2
    gw = d_ssm // NJ
    n_dep = 0 if dep is None else 1

    def body(u_ref, u0_ref, u1_ref, bdr, bdi, cdr, cdi, lr_ref, li_ref, d_ref, *rest):
        y_ref, z_ref, sr_ref, si_ref, carry = rest[n_dep:]
        i = pl.program_id(0)

        @pl.when(i == 0)
        def _():
            carry[...] = jnp.zeros_like(carry)

        for j in range(NJ):
            uj = u_ref[:, gw * j:gw * (j + 1)]
            _strided_put(sr_ref, j, _dot(uj, bdr[j]))
            _strided_put(si_ref, j, _dot(uj, bdi[j]))
        lr = [lr_ref[q] for q in range(NQ)]
        li = [li_ref[q] for q in range(NQ)]

        def step(t, s):
            sr, si = s
            rows = pl.ds(pl.multiple_of(t * NJ, NJ), NJ)
            nr = tuple(lr[q] * sr[q] - li[q] * si[q] + sr_ref[q, rows, :] for q in range(NQ))
            ni = tuple(lr[q] * si[q] + li[q] * sr[q] + si_ref[q, rows, :] for q in range(NQ))
            for q in range(NQ):
                sr_ref[q, rows, :] = nr[q]
                si_ref[q, rows, :] = ni[q]
            return nr, ni

        init = (tuple(carry[0, q] for q in range(NQ)), tuple(carry[1, q] for q in range(NQ)))
        sr, si = lax.fori_loop(0, SSM_T // SCAN_UNROLL, _unrolled(step), init)
        for q in range(NQ):
            carry[0, q] = sr[q]
            carry[1, q] = si[q]
        uf = jnp.concatenate([u0_ref[...], u1_ref[...]], axis=1)
        for j in range(NJ):
            cols = slice(gw * j, gw * (j + 1))
            yj = _dot(_strided_get(sr_ref, j).astype(bf16), cdr[j]) - _dot(_strided_get(si_ref, j).astype(bf16), cdi[j])
            yj = yj + d_ref[:, cols] * uf[:, cols]
            y_ref[:, cols] = yj
            z_ref[:, cols] = gelu(yj).astype(bf16)

    s_shape = jax.ShapeDtypeStruct((NQ, L * NJ, LANES), f32)
    consts = (bd_re, bd_im, cd_re, cd_im, lam_re, lam_im, d_skip)
    in_specs, operands, _ = _with_dep([row(d_ssm), row(half, u_cb), row(half, u_cb + 1)] + [whole(a) for a in consts],
                                      [u_bf, proj, proj, *consts], dep)
    return pl.pallas_call(
        body, out_shape=[jax.ShapeDtypeStruct((L, d_ssm), f32), jax.ShapeDtypeStruct((L, d_ssm), bf16), s_shape, s_shape], grid=(nt,),
        in_specs=in_specs, out_specs=[row(d_ssm), row(d_ssm), state, state],
        scratch_shapes=[pltpu.VMEM((2, NQ, NJ, LANES), f32)], name="ssm_fwd",
        compiler_params=_params("arbitrary"))(*operands)


def ssm_bwd(y, dz1, dz2, u_bf, proj, u_cb, s_re, s_im, bd_re, bd_im, cd_re, cd_im, lam_re, lam_im, d_skip, dep=None):
    L, d_ssm = y.shape
    nt, row, state, whole = _ssm_specs(L, True)
    half = d_ssm // 2
    gw = d_ssm // NJ
    n_dep = 0 if dep is None else 1

    def body(y_ref, dz1_ref, dz2_ref, u_ref, u0_ref, u1_ref, sr_ref, si_ref, bdr, bdi, cdr, cdi, lr_ref, li_ref, d_ref, *rest):
        du_ref, fbr, fbi, fcr, fci, dlr, dli, dd_ref, gr_ref, gi_ref, carry, dbdr, dbdi, dcdr, dcdi = rest[n_dep:]
        i = pl.program_id(0)

        @pl.when(i == 0)
        def _():
            carry[...] = jnp.zeros_like(carry)
            for r in (dbdr, dbdi, dcdr, dcdi, dlr, dli, dd_ref):
                r[...] = jnp.zeros_like(r)

        dyf = (dz1_ref[...] + dz2_ref[...]) * gelu_grad(y_ref[...])
        dyb = dyf.astype(bf16)
        for j in range(NJ):
            dyj = dyb[:, gw * j:gw * (j + 1)]
            _strided_put(gr_ref, j, _dot(dyj, cdr[j], NT))
            _strided_put(gi_ref, j, -_dot(dyj, cdi[j], NT))
            dcdr[j] += _dot(_strided_get(sr_ref, j).astype(bf16), dyj, TN)
            dcdi[j] -= _dot(_strided_get(si_ref, j).astype(bf16), dyj, TN)
        lr = [lr_ref[q] for q in range(NQ)]
        li = [li_ref[q] for q in range(NQ)]

        def step(k, c):
            gr, gi, ar, ai = c
            rows = pl.ds(pl.multiple_of((SSM_T - 1 - k) * NJ, NJ), NJ)
            s_r = [sr_ref[q, rows, :] for q in range(NQ)]
            s_i = [si_ref[q, rows, :] for q in range(NQ)]
            ar = tuple(ar[q] + gr[q] * s_r[q] + gi[q] * s_i[q] for q in range(NQ))
            ai = tuple(ai[q] + gi[q] * s_r[q] - gr[q] * s_i[q] for q in range(NQ))
            nr = tuple(gr_ref[q, rows, :] + lr[q] * gr[q] + li[q] * gi[q] for q in range(NQ))
            ni = tuple(gi_ref[q, rows, :] + lr[q] * gi[q] - li[q] * gr[q] for q in range(NQ))
            for q in range(NQ):
                gr_ref[q, rows, :] = nr[q]
                gi_ref[q, rows, :] = ni[q]
            return nr, ni, ar, ai

        zero = tuple(jnp.zeros((NJ, LANES), f32) for _ in range(NQ))
        init = (tuple(carry[0, q] for q in range(NQ)), tuple(carry[1, q] for q in range(NQ)), zero, zero)
        gr, gi, ar, ai = lax.fori_loop(0, SSM_T // SCAN_UNROLL, _unrolled(step), init)
        for q in range(NQ):
            carry[0, q] = gr[q]
            carry[1, q] = gi[q]
            dlr[q] += ar[q]
            dli[q] += ai[q]
        uf = jnp.concatenate([u0_ref[...], u1_ref[...]], axis=1)
        dd_ref[...] += colsum(dyf * uf)
        for j in range(NJ):
            cols = slice(gw * j, gw * (j + 1))
            gjr, gji = _strided_get(gr_ref, j).astype(bf16), _strided_get(gi_ref, j).astype(bf16)
            du_ref[:, cols] = _dot(gjr, bdr[j], NT) + _dot(gji, bdi[j], NT) + d_ref[:, cols] * dyf[:, cols]
            uj = u_ref[:, cols]
            dbdr[j] += _dot(uj, gjr, TN)
            dbdi[j] += _dot(uj, gji, TN)

        @pl.when(i == nt - 1)
        def _():
            nb = NJ * SSM_STATE
            b_diag = (lax.broadcasted_iota(jnp.int32, (gw, nb), 0) // SSM_GROUP) == (lax.broadcasted_iota(jnp.int32, (gw, nb), 1) // SSM_STATE)
            c_diag = (lax.broadcasted_iota(jnp.int32, (nb, gw), 0) // SSM_STATE) == (lax.broadcasted_iota(jnp.int32, (nb, gw), 1) // SSM_GROUP)
            for j in range(NJ):
                for acc_ref, out in ((dbdr, fbr), (dbdi, fbi)):
                    m = jnp.where(b_diag, acc_ref[j], 0.0)
                    f = m[:, :LANES]
                    for q in range(1, nb // LANES):
                        f = f + m[:, LANES * q:LANES * (q + 1)]
                    out[j] = f + pltpu.roll(f, SSM_STATE, 1)
                for acc_ref, out in ((dcdr, fcr), (dcdi, fci)):
                    f = jnp.where(c_diag, acc_ref[j], 0.0)
                    for shift in (64, 32, 16):
                        f = f + pltpu.roll(f, shift, 1)
                    out[j] = f

    consts = (bd_re, bd_im, cd_re, cd_im, lam_re, lam_im, d_skip)
    acc = lambda a: jax.ShapeDtypeStruct(a.shape, f32)
    fb = jax.ShapeDtypeStruct((NJ, gw, LANES), f32)
    fc = jax.ShapeDtypeStruct((NJ, NJ * SSM_STATE, LANES), f32)
    outs = [jax.ShapeDtypeStruct((L, d_ssm), f32), fb, fb, fc, fc, acc(lam_re), acc(lam_im), acc(d_skip)]
    in_specs, operands, _ = _with_dep(
        [row(d_ssm)] * 4 + [row(half, u_cb), row(half, u_cb + 1), state, state] + [whole(a) for a in consts],
        [y, dz1, dz2, u_bf, proj, proj, s_re, s_im, *consts], dep)
    return pl.pallas_call(
        body, out_shape=outs, grid=(nt,),
        in_specs=in_specs, out_specs=[row(d_ssm)] + [whole(a) for a in outs[1:]],
        scratch_shapes=[pltpu.VMEM((NQ, SSM_T * NJ, LANES), f32), pltpu.VMEM((NQ, SSM_T * NJ, LANES), f32),
                        pltpu.VMEM((2, NQ, NJ, LANES), f32)] + [pltpu.VMEM(a.shape, f32) for a in (bd_re, bd_im, cd_re, cd_im)],
        name="ssm_bwd", compiler_params=_params("arbitrary"))(*operands)


def _cmul(ar, ai, br, bi):
    return ar * br - ai * bi, ar * bi + ai * br


def _disc(ar, ai, logdt):
    dt = jnp.exp(logdt)
    mag = jnp.exp(ar * dt)
    lr, li = mag * jnp.cos(ai * dt), mag * jnp.sin(ai * dt)
    den = ar * ar + ai * ai
    nr, ni = lr - 1.0, li
    fr, fi = (nr * ar + ni * ai) / den, (ni * ar - nr * ai) / den
    return dt, lr, li, den, fr, fi


def ssm_params(a_re, a_im, logdt_b, bt_re, bt_im, spread):
    def body(ar_ref, ai_ref, ld_ref, br_ref, bi_ref, sp_ref, lr_ref, li_ref, or_ref, oi_ref):
        _, lr, li, _, fr, fi = _disc(ar_ref[...], ai_ref[...], ld_ref[...])
        lr_ref[...] = lr
        li_ref[...] = li
        fre = jnp.dot(sp_ref[...], fr, precision=HIGHEST, preferred_element_type=f32)
        fie = jnp.dot(sp_ref[...], fi, precision=HIGHEST, preferred_element_type=f32)
        o_r, o_i = _cmul(fre, fie, br_ref[...], bi_ref[...])
        or_ref[...] = o_r
        oi_ref[...] = o_i

    g = jax.ShapeDtypeStruct(a_re.shape, f32)
    b = jax.ShapeDtypeStruct(bt_re.shape, f32)
    return pl.pallas_call(body, out_shape=[g, g, b, b], name="ssm_params",
                          compiler_params=_params())(a_re, a_im, logdt_b, bt_re, bt_im, spread)


def ssm_params_grad(a_re, a_im, logdt_b, bt_re, bt_im, spread, gather, dlam_re, dlam_im, dbt_re, dbt_im):
    def body(ar_ref, ai_ref, ld_ref, br_ref, bi_ref, sp_ref, ga_ref, glr_ref, gli_ref, gbr_ref, gbi_ref,
             dar_ref, dai_ref, dld_ref, dbr_ref, dbi_ref):
        ar, ai = ar_ref[...], ai_ref[...]
        dt, lr, li, den, fr, fi = _disc(ar, ai, ld_ref[...])
        hdot = functools.partial(jnp.dot, precision=HIGHEST, preferred_element_type=f32)
        fre, fie = hdot(sp_ref[...], fr), hdot(sp_ref[...], fi)
        gbr, gbi, br, bi = gbr_ref[...], gbi_ref[...], br_ref[...], bi_ref[...]
        dbr_ref[...], dbi_ref[...] = _cmul(fre, -fie, gbr, gbi)
        t_r, t_i = _cmul(br, -bi, gbr, gbi)
        gfr, gfi = hdot(ga_ref[...], t_r), hdot(ga_ref[...], t_i)
        iwr, iwi = ar / den, -ai / den
        x_r, x_i = _cmul(iwr, -iwi, gfr, gfi)
        glr, gli = glr_ref[...] + x_r, gli_ref[...] + x_i
        q_r, q_i = _cmul(fr, fi, iwr, iwi)
        gwr, gwi = _cmul(-q_r, q_i, gfr, gfi)
        y_r, y_i = _cmul(dt * lr, -dt * li, glr, gli)
        dar_ref[...] = gwr + y_r
        dai_ref[...] = gwi + y_i
        wl_r, wl_i = _cmul(ar, ai, lr, li)
        z_r, _ = _cmul(wl_r, -wl_i, glr, gli)
        dld_ref[...] = jnp.sum(z_r * dt, axis=1, keepdims=True)

    g = jax.ShapeDtypeStruct(a_re.shape, f32)
    b = jax.ShapeDtypeStruct(bt_re.shape, f32)
    return pl.pallas_call(body, out_shape=[g, g, jax.ShapeDtypeStruct((a_re.shape[0], 1), f32), b, b], name="ssm_params_grad",
                          compiler_params=_params())(a_re, a_im, logdt_b, bt_re, bt_im, spread, gather, dlam_re, dlam_im, dbt_re, dbt_im)


def _block_diag(t, rows, cols):
    G = t.shape[0]
    t = t.reshape(G // NJ, NJ, rows, cols)
    eye = jnp.eye(NJ, dtype=t.dtype)
    return jnp.einsum('jgrc,gh->jgrhc', t, eye).reshape(G // NJ, NJ * rows, NJ * cols)


def _state_layout(t):
    return t.reshape(NJ, NQ, LANES).transpose(1, 0, 2)


def _state_layout_inv(t, G, N):
    return t.transpose(1, 0, 2).reshape(G, N)


def _tiles2d(shape, budget_rows=128):
    rows, cols = shape
    tr = rows
    if rows > budget_rows:
        tr = budget_rows
        while rows % tr:
            tr -= SUBLANES
    return tr, cols


ADAM_TILE_BYTES = 3 << 19
ROW_ALIGN = 16


def _tile_rows(rows, row_bytes, target_bytes):
    tr = max(ROW_ALIGN, min(rows, target_bytes // row_bytes) // ROW_ALIGN * ROW_ALIGN)
    while rows % tr:
        tr -= ROW_ALIGN
    return tr


def _adam_update(w, g, m, v):
    c1 = 1.0 - ADAM_B1 ** ADAM_STEP
    c2 = 1.0 - ADAM_B2 ** ADAM_STEP
    nm = ADAM_B1 * m + (1.0 - ADAM_B1) * g
    nv = ADAM_B2 * v + (1.0 - ADAM_B2) * (g * g)
    delta = -ADAM_LR * ((nm / c1) / (jnp.sqrt(nv / c2) + ADAM_EPS) + ADAM_WD * w)
    return delta, nm, nv


def adamw_many(name, ws, gs, ms, vs):
    n = len(ws)

    def body(*refs):
        w, g, m, v = (refs[k * n:(k + 1) * n] for k in range(4))
        d, nm, nv = (refs[(4 + k) * n:(5 + k) * n] for k in range(3))
        for i in range(n):
            d[i][...], nm[i][...], nv[i][...] = _adam_update(w[i][...], g[i][...], m[i][...], v[i][...])

    o = [jax.ShapeDtypeStruct(a.shape, f32) for a in ws]
    outs = pl.pallas_call(body, out_shape=o * 3, name=name, compiler_params=_params())(*ws, *gs, *ms, *vs)
    return outs[:n], outs[n:2 * n], outs[2 * n:]


def adamw_halves(name, w, own, got, m, v, c_arr):
    h, cols = own.shape
    tr = _tile_rows(h, cols * 4, ADAM_TILE_BYTES)
    nh = h // tr

    def body(c_ref, w_ref, own_ref, got_ref, m_ref, v_ref, g_ref, d_ref, nm_ref, nv_ref):
        mine = (pl.program_id(0) // nh) == c_ref[0]
        g = jnp.where(mine, own_ref[...], got_ref[...])
        g_ref[...] = g
        d_ref[...], nm_ref[...], nv_ref[...] = _adam_update(w_ref[...], g, m_ref[...], v_ref[...])

    spec = pl.BlockSpec((tr, cols), lambda i, c: (i, 0))
    own_spec = pl.BlockSpec((tr, cols), lambda i, c: (jnp.where(i // nh == c[0], i % nh, 0), 0))
    got_spec = pl.BlockSpec((tr, cols), lambda i, c: (jnp.where(i // nh == c[0], 0, i % nh), 0))
    o = jax.ShapeDtypeStruct(w.shape, f32)
    grid_spec = pltpu.PrefetchScalarGridSpec(num_scalar_prefetch=1, grid=(2 * nh,),
                                             in_specs=[spec, own_spec, got_spec, spec, spec], out_specs=[spec] * 4)
    return pl.pallas_call(body, out_shape=[o, o, o, o], grid_spec=grid_spec, name=name,
                          compiler_params=_params("arbitrary"))(c_arr, w, own, got, m, v)


def pair_sum(name, g, got, c_arr):
    S, h, cols = got.shape
    tr, _ = _tiles2d((h, cols), 1024)
    nh = h // tr

    def body(c_ref, g_ref, o_ref, out_ref):
        out_ref[...] = (g_ref[...].astype(f32) + o_ref[...].astype(f32)).astype(out_ref.dtype)

    spec = pl.BlockSpec((None, tr, cols), lambda s, i, c: (s, i, 0))
    grid_spec = pltpu.PrefetchScalarGridSpec(
        num_scalar_prefetch=1, grid=(S, nh),
        in_specs=[pl.BlockSpec((None, tr, cols), lambda s, i, c: (s, c[0] * nh + i, 0)), spec], out_specs=spec)
    return pl.pallas_call(body, out_shape=jax.ShapeDtypeStruct(got.shape, g.dtype), grid_spec=grid_spec, name=name,
                          compiler_params=_params("parallel", "parallel"))(c_arr, g, got)


def chip_sum(name, pair, landed, mine_arr, dep=None):
    n_in, h, cols = landed.shape
    tr, _ = _tiles2d((h, cols), 256)

    def body(s_ref, p_ref, l_ref, *rest):
        acc = p_ref[...].astype(f32)
        for k in range(n_in):
            acc = acc + l_ref[k].astype(f32)
        rest[-1][...] = acc

    in_specs, operands, _ = _with_dep(
        [pl.BlockSpec((None, tr, cols), lambda i, s: (s[0], i, 0)), pl.BlockSpec((n_in, tr, cols), lambda i, s: (0, i, 0))],
        [pair, landed], dep)
    grid_spec = pltpu.PrefetchScalarGridSpec(num_scalar_prefetch=1, grid=(h // tr,), in_specs=in_specs,
                                             out_specs=pl.BlockSpec((tr, cols), lambda i, s: (i, 0)))
    return pl.pallas_call(body, out_shape=jax.ShapeDtypeStruct((h, cols), f32), grid_spec=grid_spec, name=name,
                          compiler_params=_params("parallel"))(mine_arr, *operands)


def into_slot(name, w, slot_arr, n_slots, dtype, dep=None):
    tr, cols = _tiles2d(w.shape, 256)

    def body(s_ref, w_ref, *rest):
        rest[-1][...] = w_ref[...].astype(dtype)

    in_specs, operands, _ = _with_dep([pl.BlockSpec((tr, cols), lambda i, s: (i, 0))], [w], dep)
    grid_spec = pltpu.PrefetchScalarGridSpec(num_scalar_prefetch=1, grid=(w.shape[0] // tr,), in_specs=in_specs,
                                             out_specs=pl.BlockSpec((None, tr, cols), lambda i, s: (s[0], i, 0)))
    return pl.pallas_call(body, out_shape=jax.ShapeDtypeStruct((n_slots,) + w.shape, dtype), grid_spec=grid_spec, name=name,
                          compiler_params=_params("parallel"))(slot_arr, *operands)


def sum_slots(name, t):
    S, rows, cols = t.shape
    tr, _ = _tiles2d((rows, cols), 256)

    def body(t_ref, o_ref):
        acc = t_ref[0]
        for s in range(1, S):
            acc = acc + t_ref[s]
        o_ref[...] = acc

    return pl.pallas_call(body, out_shape=jax.ShapeDtypeStruct((rows, cols), f32), grid=(rows // tr,),
                          in_specs=[pl.BlockSpec((S, tr, cols), lambda i: (0, i, 0))], out_specs=pl.BlockSpec((tr, cols), lambda i: (i, 0)),
                          name=name, compiler_params=_params("parallel"))(t)


def _place():
    x, y, c = lax.axis_index("x"), lax.axis_index("y"), lax.axis_index("c")
    return x, y, c


def _other_chips(x, y):
    return [(1 - x, y, 2 * (1 - x) + y), (x, 1 - y, 2 * x + 1 - y), (1 - x, 1 - y, 2 * (1 - x) + 1 - y)]


SEM = pl.BlockSpec(memory_space=pltpu.SEMAPHORE)
VM = pl.BlockSpec(memory_space=pltpu.VMEM)
DATAFLOW = pltpu.SideEffectType.DATAFLOW_SIDE_EFFECTING
TOKEN = jax.ShapeDtypeStruct((SUBLANES, LANES), f32)


def _gather_copy(buf, w, k, chip, c, mine, send, recv):
    px, py, _ = chip
    h = buf.shape[1] // 2
    half = buf.at[mine, pl.ds(c * h, h), :]
    return pltpu.make_async_remote_copy(src_ref=half, dst_ref=half, send_sem=send.at[3 * w + k], recv_sem=recv.at[3 * w + k],
                                        device_id=(px, py, c), device_id_type=MESH)


def _gather_landing(buf, w, k, chip, c, send, recv):
    px, py, s = chip
    h = buf.shape[1] // 2
    landed = buf.at[s, pl.ds(c * h, h), :]
    return pltpu.make_async_remote_copy(src_ref=landed, dst_ref=landed, send_sem=send.at[3 * w + k], recv_sem=recv.at[3 * w + k],
                                        device_id=(px, py, c), device_id_type=MESH)


def gather_start(name, bufs, groups, after, neighbours_only=()):
    nw, ng = len(bufs), len(groups)

    def body(*refs):
        outs = refs[nw + 1:]
        sems, dst = outs[:2 * ng], outs[2 * ng:2 * ng + nw]
        token = outs[2 * ng + nw]
        x, y, c = _place()
        mine = 2 * x + y
        for g, members in enumerate(groups):
            for i, w in enumerate(members):
                for k, chip in enumerate(_other_chips(x, y)[:2 if w in neighbours_only else 3]):
                    _gather_copy(dst[w], i, k, chip, c, mine, sems[2 * g], sems[2 * g + 1]).start()
        token[...] = jnp.zeros_like(token)

    sem_shapes = []
    for members in groups:
        sem_shapes += [pltpu.SemaphoreType.DMA((3 * len(members),))] * 2
    outs = pl.pallas_call(
        body, out_shape=sem_shapes + [jax.ShapeDtypeStruct(b.shape, b.dtype) for b in bufs] + [TOKEN],
        in_specs=[ANY] * (nw + 1), out_specs=[SEM] * (2 * ng) + [ANY] * nw + [VM],
        input_output_aliases={w: 2 * ng + w for w in range(nw)}, name=name,
        compiler_params=pltpu.CompilerParams(has_side_effects=DATAFLOW))(*bufs, after)
    return [(outs[2 * g], outs[2 * g + 1]) for g in range(ng)], list(outs[2 * ng:2 * ng + nw]), outs[2 * ng + nw]


def gather_wait(name, bufs, send, recv, after):
    nw = len(bufs)

    def body(*refs):
        src = refs[:nw]
        send_ref, recv_ref = refs[nw], refs[nw + 1]
        x, y, c = _place()
        mine = 2 * x + y
        for w in range(nw):
            for k, chip in enumerate(_other_chips(x, y)):
                _gather_copy(src[w], w, k, chip, c, mine, send_ref, recv_ref).wait_send()
                _gather_landing(src[w], w, k, chip, c, send_ref, recv_ref).wait_recv()

    return pl.pallas_call(
        body, out_shape=[jax.ShapeDtypeStruct(b.shape, b.dtype) for b in bufs],
        in_specs=[ANY] * nw + [SEM, SEM, ANY], out_specs=[ANY] * nw,
        input_output_aliases={w: w for w in range(nw)}, name=name,
        compiler_params=pltpu.CompilerParams(has_side_effects=DATAFLOW))(*bufs, send, recv, after)


def _relay_copy(buf, w, j, x, y, c, send, recv, landing):
    chips = _other_chips(x, y)
    px, py, _ = chips[j]
    h = buf.shape[1] // 2
    q = h // 2
    s = chips[2][2] if landing else chips[1 - j][2]
    part = buf.at[s, pl.ds(c * h + j * q, q), :]
    return pltpu.make_async_remote_copy(src_ref=part, dst_ref=part, send_sem=send.at[2 * w + j], recv_sem=recv.at[2 * w + j],
                                        device_id=(px, py, c), device_id_type=MESH)


def _early_pass(buf, nw, w, k, x, y, c, send, recv, landing):
    s = _other_chips(x, y)[k][2]
    h = buf.shape[1] // 2
    part = buf.at[s, pl.ds(((1 - c) if landing else c) * h, h), :]
    i = 2 * nw + 2 * w + k
    return pltpu.make_async_remote_copy(src_ref=part, dst_ref=part, send_sem=send.at[i], recv_sem=recv.at[i],
                                        device_id=(x, y, 1 - c), device_id_type=MESH)


def gather_relay(name, bufs, sems, more, after):
    nw, nm = len(bufs), len(more)
    ns = 4 if nm else 2

    def body(*refs):
        ins, outs = refs[:nw + nm + 2 * nw + 1], refs[nw + nm + 2 * nw + 1:]
        src, d_sems = ins[:nw], ins[nw + nm:nw + nm + 2 * nw]
        r_send, r_recv = outs[:2]
        m_send, m_recv = outs[2:ns] if nm else (None, None)
        dst, mdst, token = outs[ns:ns + nw], outs[ns + nw:ns + nw + nm], outs[ns + nw + nm]
        x, y, c = _place()
        mine = 2 * x + y
        chips = _other_chips(x, y)
        for w in range(nw):
            for k in range(2):
                _gather_copy(src[w], 0, k, chips[k], c, mine, d_sems[2 * w], d_sems[2 * w + 1]).wait_send()
                _gather_landing(src[w], 0, k, chips[k], c, d_sems[2 * w], d_sems[2 * w + 1]).wait_recv()
            for j in range(2):
                _relay_copy(dst[w], w, j, x, y, c, r_send, r_recv, False).start()
            for k in range(2):
                _early_pass(dst[w], nw, w, k, x, y, c, r_send, r_recv, False).start()
        for w in range(nm):
            for k, chip in enumerate(chips):
                _gather_copy(mdst[w], w, k, chip, c, mine, m_send, m_recv).start()
        token[...] = jnp.zeros_like(token)

    sem_shapes = [pltpu.SemaphoreType.DMA((4 * nw,))] * 2 + [pltpu.SemaphoreType.DMA((3 * nm,))] * (ns - 2)
    flat_sems = [s for pair in sems for s in pair]
    outs = pl.pallas_call(
        body, out_shape=sem_shapes + [jax.ShapeDtypeStruct(b.shape, b.dtype) for b in list(bufs) + list(more)] + [TOKEN],
        in_specs=[ANY] * (nw + nm) + [SEM] * (2 * nw) + [ANY], out_specs=[SEM] * ns + [ANY] * (nw + nm) + [VM],
        input_output_aliases={i: ns + i for i in range(nw + nm)}, name=name,
        compiler_params=pltpu.CompilerParams(has_side_effects=DATAFLOW))(*bufs, *more, *flat_sems, after)
    m_sems = (outs[2], outs[3]) if nm else None
    return outs[0], outs[1], m_sems, list(outs[ns:ns + nw]), list(outs[ns + nw:ns + nw + nm]), outs[ns + nw + nm]


def gather_wait_relay(name, bufs, r_send, r_recv, after):
    nw = len(bufs)

    def body(*refs):
        src = refs[:nw]
        send_ref, recv_ref = refs[nw], refs[nw + 1]
        x, y, c = _place()
        for w in range(nw):
            for j in range(2):
                _relay_copy(src[w], w, j, x, y, c, send_ref, recv_ref, False).wait_send()
                _relay_copy(src[w], w, j, x, y, c, send_ref, recv_ref, True).wait_recv()
                _early_pass(src[w], nw, w, j, x, y, c, send_ref, recv_ref, False).wait_send()
                _early_pass(src[w], nw, w, j, x, y, c, send_ref, recv_ref, True).wait_recv()

    return pl.pallas_call(
        body, out_shape=[jax.ShapeDtypeStruct(b.shape, b.dtype) for b in bufs],
        in_specs=[ANY] * nw + [SEM, SEM, ANY], out_specs=[ANY] * nw,
        input_output_aliases={w: w for w in range(nw)}, name=name,
        compiler_params=pltpu.CompilerParams(has_side_effects=DATAFLOW))(*bufs, r_send, r_recv, after)


def gather_forward(name, bufs, which=(0, 1, 2)):
    nw = len(bufs)

    def body(*refs):
        dst = refs[nw:2 * nw]
        send, recv = refs[2 * nw:]
        x, y, c = _place()
        sib = (x, y, 1 - c)
        barrier = pltpu.get_barrier_semaphore()
        pl.semaphore_signal(barrier, inc=1, device_id=sib, device_id_type=MESH)
        pl.semaphore_wait(barrier, 1)
        cps = []
        for w in range(nw):
            h = dst[w].shape[1] // 2
            for k in which:
                s = _other_chips(x, y)[k][2]
                landed = dst[w].at[s, pl.ds(c * h, h), :]
                cp = pltpu.make_async_remote_copy(src_ref=landed, dst_ref=landed, send_sem=send.at[w, k], recv_sem=recv.at[w, k],
                                                  device_id=sib, device_id_type=MESH)
                cp.start()
                cps.append(cp)
        for w in range(nw):
            h = dst[w].shape[1] // 2
            for k in which:
                s = _other_chips(x, y)[k][2]
                other = dst[w].at[s, pl.ds((1 - c) * h, h), :]
                pltpu.make_async_remote_copy(src_ref=other, dst_ref=other, send_sem=send.at[w, k], recv_sem=recv.at[w, k],
                                             device_id=sib, device_id_type=MESH).wait_recv()
        for cp in cps:
            cp.wait_send()

    sem = pltpu.SemaphoreType.DMA((nw, 3))
    return pl.pallas_call(
        body, out_shape=[jax.ShapeDtypeStruct(b.shape, b.dtype) for b in bufs],
        in_specs=[ANY] * nw, out_specs=[ANY] * nw, input_output_aliases={w: w for w in range(nw)},
        scratch_shapes=[sem, sem], name=name,
        compiler_params=pltpu.CompilerParams(has_side_effects=True, collective_id=SIBLING_PAIR))(*bufs)


def _scatter_copy(src, dst, w, k, chip, c, send, recv):
    px, py, s = chip
    return pltpu.make_async_remote_copy(src_ref=src.at[s], dst_ref=dst.at[k], send_sem=send.at[3 * w + k], recv_sem=recv.at[3 * w + k],
                                        device_id=(px, py, c), device_id_type=MESH)


def scatter_start(name, parts):
    nw = len(parts)
    lands = [pltpu.with_memory_space_constraint(lax.empty((N_CHIPS - 1,) + p.shape[1:], p.dtype), pltpu.HBM) for p in parts]

    def body(*refs):
        outs = refs[2 * nw:]
        send, recv = outs[0], outs[1]
        src, dst, token = outs[2:2 + nw], outs[2 + nw:2 + 2 * nw], outs[2 + 2 * nw]
        x, y, c = _place()
        for w in range(nw):
            for k, chip in enumerate(_other_chips(x, y)):
                _scatter_copy(src[w], dst[w], w, k, chip, c, send, recv).start()
        token[...] = jnp.zeros_like(token)

    sem = pltpu.SemaphoreType.DMA((3 * nw,))
    outs = pl.pallas_call(
        body, out_shape=[sem, sem] + [jax.ShapeDtypeStruct(p.shape, p.dtype) for p in parts]
        + [jax.ShapeDtypeStruct(l.shape, l.dtype) for l in lands] + [TOKEN],
        in_specs=[ANY] * (2 * nw), out_specs=[SEM, SEM] + [ANY] * (2 * nw) + [VM],
        input_output_aliases={i: 2 + i for i in range(2 * nw)}, name=name,
        compiler_params=pltpu.CompilerParams(has_side_effects=DATAFLOW))(*parts, *lands)
    return outs[0], outs[1], list(outs[2:2 + nw]), list(outs[2 + nw:2 + 2 * nw]), outs[2 + 2 * nw]


def scatter_wait(name, parts, lands, send, recv, after):
    nw = len(parts)

    def body(*refs):
        src, dst = refs[:nw], refs[nw:2 * nw]
        send_ref, recv_ref = refs[2 * nw], refs[2 * nw + 1]
        x, y, c = _place()
        for w in range(nw):
            for k, chip in enumerate(_other_chips(x, y)):
                cp = _scatter_copy(src[w], dst[w], w, k, chip, c, send_ref, recv_ref)
                cp.wait_send()
                cp.wait_recv()

    outs = pl.pallas_call(
        body, out_shape=[jax.ShapeDtypeStruct(a.shape, a.dtype) for a in list(parts) + list(lands)],
        in_specs=[ANY] * (2 * nw) + [SEM, SEM, ANY], out_specs=[ANY] * (2 * nw),
        input_output_aliases={i: i for i in range(2 * nw)}, name=name,
        compiler_params=pltpu.CompilerParams(has_side_effects=DATAFLOW))(*parts, *lands, send, recv, after)
    return list(outs[:nw]), list(outs[nw:])


SIBLING_PAIR = 0


def _sibling_copy(src, dst, w, c, half_rows, send, recv, sib):
    if half_rows:
        h = src.shape[1] // 2
        src = src.at[:, pl.ds((1 - c) * h, h), :]
    return pltpu.make_async_remote_copy(src_ref=src, dst_ref=dst, send_sem=send.at[w], recv_sem=recv.at[w],
                                        device_id=sib, device_id_type=MESH)


def _landing(shape, dtype):
    return pltpu.with_memory_space_constraint(lax.empty(shape, dtype), pltpu.HBM)


def sibling_start(name, srcs, half_rows):
    nw = len(srcs)
    lands = [_landing((s.shape[0], s.shape[1] // 2, s.shape[2]) if half_rows else s.shape, s.dtype) for s in srcs]

    def body(*refs):
        outs = refs[2 * nw:]
        send, recv = outs[0], outs[1]
        src, dst, token = outs[2:2 + nw], outs[2 + nw:2 + 2 * nw], outs[2 + 2 * nw]
        x, y, c = _place()
        barrier = pltpu.get_barrier_semaphore()
        pl.semaphore_signal(barrier, inc=1, device_id=(x, y, 1 - c), device_id_type=MESH)
        pl.semaphore_wait(barrier, 1)
        for w in range(nw):
            _sibling_copy(src[w], dst[w], w, c, half_rows, send, recv, (x, y, 1 - c)).start()
        token[...] = jnp.zeros_like(token)

    sem = pltpu.SemaphoreType.DMA((nw,))
    outs = pl.pallas_call(
        body, out_shape=[sem, sem] + [jax.ShapeDtypeStruct(a.shape, a.dtype) for a in list(srcs) + lands] + [TOKEN],
        in_specs=[ANY] * (2 * nw), out_specs=[SEM, SEM] + [ANY] * (2 * nw) + [VM],
        input_output_aliases={i: 2 + i for i in range(2 * nw)}, name=name,
        compiler_params=pltpu.CompilerParams(has_side_effects=DATAFLOW, collective_id=SIBLING_PAIR))(*srcs, *lands)
    return outs[0], outs[1], list(outs[2:2 + nw]), list(outs[2 + nw:2 + 2 * nw]), outs[2 + 2 * nw]


def sibling_wait(name, srcs, lands, send, recv, half_rows, after):
    nw = len(srcs)

    def body(*refs):
        src, dst = refs[:nw], refs[nw:2 * nw]
        send_ref, recv_ref = refs[2 * nw], refs[2 * nw + 1]
        x, y, c = _place()
        for w in range(nw):
            cp = _sibling_copy(src[w], dst[w], w, c, half_rows, send_ref, recv_ref, (x, y, 1 - c))
            cp.wait_send()
            cp.wait_recv()

    outs = pl.pallas_call(
        body, out_shape=[jax.ShapeDtypeStruct(a.shape, a.dtype) for a in list(srcs) + list(lands)],
        in_specs=[ANY] * (2 * nw) + [SEM, SEM, ANY], out_specs=[ANY] * (2 * nw),
        input_output_aliases={i: i for i in range(2 * nw)}, name=name,
        compiler_params=pltpu.CompilerParams(has_side_effects=DATAFLOW))(*srcs, *lands, send, recv, after)
    return list(outs[:nw]), list(outs[nw:])


def _peer(x, y, c, r):
    return (1 - x if r & 4 else x, 1 - y if r & 2 else y, 1 - c if r & 1 else c)


def _everyone_copy(buf, r, x, y, c, send, recv, landing):
    px, py, pc = _peer(x, y, c, r)
    slot = buf.at[4 * px + 2 * py + pc] if landing else buf.at[4 * x + 2 * y + c]
    return pltpu.make_async_remote_copy(src_ref=slot, dst_ref=slot, send_sem=send.at[r - 1], recv_sem=recv.at[r - 1],
                                        device_id=(px, py, pc), device_id_type=MESH)


def everyone_start(name, buf):
    def body(buf_in, send, recv, buf_ref, token):
        x, y, c = _place()
        for r in range(1, N_DEV):
            _everyone_copy(buf_ref, r, x, y, c, send, recv, False).start()
        token[...] = jnp.zeros_like(token)

    sem = pltpu.SemaphoreType.DMA((N_DEV - 1,))
    return pl.pallas_call(
        body, out_shape=[sem, sem, jax.ShapeDtypeStruct(buf.shape, buf.dtype), TOKEN],
        in_specs=[ANY], out_specs=[SEM, SEM, ANY, VM], input_output_aliases={0: 2}, name=name,
        compiler_params=pltpu.CompilerParams(has_side_effects=DATAFLOW))(buf)


def everyone_wait(name, buf, send, recv, after):
    def body(buf_ref, send_ref, recv_ref, after_ref, out_ref):
        x, y, c = _place()
        for r in range(1, N_DEV):
            _everyone_copy(buf_ref, r, x, y, c, send_ref, recv_ref, False).wait_send()
            _everyone_copy(buf_ref, r, x, y, c, send_ref, recv_ref, True).wait_recv()

    return pl.pallas_call(
        body, out_shape=jax.ShapeDtypeStruct(buf.shape, buf.dtype), in_specs=[ANY, SEM, SEM, ANY], out_specs=ANY,
        input_output_aliases={0: 0}, name=name,
        compiler_params=pltpu.CompilerParams(has_side_effects=DATAFLOW))(buf, send, recv, after)


def local_step(x, pos, tgt, small, d_in, get_w, put_g, first_dep=None, tick=lambda name, after: None):
    L, D = x.shape
    d_kv = N_KV_HEADS * HEAD_DIM
    d_ssm = small["d_skip"].shape[1]
    d_attn = d_in - 2 * d_kv - d_ssm
    big = {}
    G = d_ssm // SSM_GROUP
    N, P = SSM_STATE, SSM_GROUP

    half_dim = HEAD_DIM // 2
    inv_freq = ROPE_THETA ** (-jnp.arange(half_dim, dtype=f32) / half_dim)
    inv_freq = jnp.tile(inv_freq, LANES // half_dim).reshape(1, LANES)
    sink_b = jnp.broadcast_to(small["sinks"].reshape(-1, 1), (small["sinks"].size, LANES))

    spread = jnp.repeat(jnp.eye(G, dtype=f32), P, axis=0)
    logdt_b = jnp.broadcast_to(small["log_dt"].reshape(G, 1), (G, N))
    bt_re = small["b_re"].reshape(G, N, P).transpose(0, 2, 1).reshape(G * P, N)
    bt_im = small["b_im"].reshape(G, N, P).transpose(0, 2, 1).reshape(G * P, N)
    a_re, a_im = small["a_re"].reshape(G, N), small["a_im"].reshape(G, N)
    lam_re, lam_im, bbt_re, bbt_im = ssm_params(a_re, a_im, logdt_b, bt_re, bt_im, spread)
    bd_re = _block_diag(bbt_re.reshape(G, P, N), P, N).astype(bf16)
    bd_im = _block_diag(bbt_im.reshape(G, P, N), P, N).astype(bf16)
    c_re = small["c_re"].reshape(G, P, N).transpose(0, 2, 1)
    c_im = small["c_im"].reshape(G, P, N).transpose(0, 2, 1)
    cd_re = _block_diag(c_re, N, P).astype(bf16)
    cd_im = _block_diag(c_im, N, P).astype(bf16)
    lam_re_l, lam_im_l = _state_layout(lam_re), _state_layout(lam_im)

    def k1(i, nt, xt, g):
        return (rms_fwd(xt, g),)
    xn = rowwise("pre_mix_norm", k1, L, [full(x)], [small["g_pre_mix"]], [(D, bf16)], dep=first_dep)[0]
    big["w_in"] = get_w("w_in", xn)
    proj = mm_nn("proj_in", xn, big["w_in"])
    qr, kk, vv, u_bf = qkv_prep(proj, pos, inv_freq, d_attn, d_kv)
    attn = attn_fwd(qr, kk, vv, sink_b)
    u_cb = (d_attn + 2 * d_kv) // (d_ssm // 2)
    token = tick("attn", attn)
    y, z_bf, s_re, s_im = ssm_fwd(u_bf, proj, u_cb, bd_re, bd_im, cd_re, cd_im, lam_re_l, lam_im_l, small["d_skip"], dep=token)
    token = tick("ssm", z_bf)
    big["w_glu"] = get_w("w_glu", z_bf)
    gl = mm_nn("glu_proj", z_bf, big["w_glu"], dep=token)

    def k6(i, nt, at, yt, glt, bg, ga, gs):
        ssm = gelu(yt) * sigmoid(glt + bg)
        return (jnp.concatenate([rms_fwd(at, ga), rms_fwd(ssm, gs)], axis=1),)
    mixed = rowwise("mix_norms", k6, L, [full(attn), full(y), full(gl)],
                    [small["b_glu"], small["g_attn_out"], small["g_ssm_out"]], [(d_attn + d_ssm, bf16)])[0]
    big["w_o"] = get_w("w_o", mixed)
    mix = mm_nn("proj_out", mixed, big["w_o"])

    def k7(i, nt, xt, mt, gpm, gpf):
        h = xt + rms_fwd(mt, gpm)
        return h, rms_fwd(h, gpf)
    h, hn = rowwise("post_mix", k7, L, [full(x), full(mix)], [small["g_post_mix"], small["g_pre_ffn"]], [(D, f32), (D, bf16)])
    big["w_gate"] = get_w("w_gate", hn)
    big["w_up"] = get_w("w_up", hn)
    hid_dg, hid_du, hid = ffn_hidden(hn, big["w_gate"], big["w_up"])
    d_ff_dim = hid.shape[1]
    big["w_down"] = get_w("w_down", hid)
    ff = mm_nn("ffn_down", hid, big["w_down"], tk=d_ff_dim // 2)

    def k9(i, nt, ht, fft, tt, g):
        out = ht + rms_fwd(fft, g)
        err = out - tt
        per_row = jnp.mean(err * err, axis=-1, keepdims=True)
        loss = 0.5 * jnp.sum(per_row) * jnp.where(_lane((1, LANES)) == 0, 1.0, 0.0)
        d_out = err * (1.0 / D)
        d_ff, dg = rms_bwd(fft, g, d_out)
        return d_out, d_ff, dg, loss
    d_out, d_ff, dg_post_ffn, loss = rowwise("loss_head", k9, L, [full(h), full(ff), full(tgt)], [small["g_post_ffn"]],
                                             [(D, f32), (D, bf16)], reds=[D, LANES])

    d_gt, d_up = ffn_hidden_grad(d_ff, big["w_down"], hid_dg, hid_du)
    token = put_g("w_down", hid, d_ff, None)
    d_hn = mm_nt_pair("d_hn", d_gt, big["w_gate"], d_up, big["w_up"], dep=token)
    token = put_g("w_gate", hn, d_gt, None)
    token = put_g("w_up", hn, d_up, token)

    def k11(i, nt, ht, da, do, mt, gpf, gpm):
        dh_n, dg_pf = rms_bwd(ht, gpf, da)
        dh = do + dh_n
        d_mix, dg_pm = rms_bwd(mt, gpm, dh)
        return dh, d_mix, dg_pf, dg_pm
    dh, d_mix, dg_pre_ffn, dg_post_mix = rowwise("post_mix_grad", k11, L, [full(h), full(d_hn), full(d_out), full(mix)],
                                                 [small["g_pre_ffn"], small["g_post_mix"]], [(D, f32), (D, bf16)], reds=[D, D], dep=token)
    d_mixed = mm_nt("d_mixed", d_mix, big["w_o"])
    token = put_g("w_o", mixed, d_mix, None)

    def k12(i, nt, at, yt, glt, da_n, ds_n, bg, ga, gs):
        z = gelu(yt)
        sg = sigmoid(glt + bg)
        ssm = z * sg
        d_at, dga = rms_bwd(at, ga, da_n)
        d_ssm_t, dgs = rms_bwd(ssm, gs, ds_n)
        d_gl = d_ssm_t * z * sg * (1.0 - sg)
        return d_at, d_ssm_t * sg, d_gl, dga, dgs, colsum(d_gl)
    d_attn_o, dz1, d_gl, dg_attn, dg_ssm, db_glu = rowwise(
        "mix_norms_grad", k12, L, [full(attn), full(y), full(gl), (d_mixed, d_attn, 0, 0), (d_mixed, d_ssm, d_attn // d_ssm, 0)],
        [small["b_glu"], small["g_attn_out"], small["g_ssm_out"]], [(d_attn, f32), (d_ssm, f32), (d_ssm, bf16)],
        reds=[d_attn, d_ssm, d_ssm], dep=token)
    dz2 = mm_nt("d_glu_in", d_gl, big["w_glu"])
    token = put_g("w_glu", z_bf, d_gl, None)

    du, dbd_re, dbd_im, dcd_re, dcd_im, dlam_re_l, dlam_im_l, dd_skip = ssm_bwd(
        y, dz1, dz2, u_bf, proj, u_cb, s_re, s_im, bd_re, bd_im, cd_re, cd_im, lam_re_l, lam_im_l, small["d_skip"], dep=token)
    dq, dkk_c, dkk_p, dvv_c, dvv_p, dsink = attn_bwd(qr, kk, vv, sink_b, attn, d_attn_o)
    d_proj = qkv_grad(dq, dkk_c, dkk_p, dvv_c, dvv_p, du, pos, inv_freq)
    d_xn = mm_nt("d_xn", d_proj, big["w_in"])
    token = put_g("w_in", xn, d_proj, None)

    def k17(i, nt, xt, dxn, dht, g):
        dx, dg = rms_bwd(xt, g, dxn)
        return dht + dx, dg
    grad_x, dg_pre_mix = rowwise("pre_mix_grad", k17, L, [full(x), full(d_xn), full(dh)], [small["g_pre_mix"]],
                                 [(D, f32)], reds=[D], dep=token)

    gather = spread.T
    dbbt_re = dbd_re.reshape(G * P, LANES)[:, :N]
    dbbt_im = dbd_im.reshape(G * P, LANES)[:, :N]
    dc_re = dcd_re.reshape(G, N, LANES)[:, :, :P].transpose(0, 2, 1)
    dc_im = dcd_im.reshape(G, N, LANES)[:, :, :P].transpose(0, 2, 1)
    d_a_re, d_a_im, d_logdt, dbt_re, dbt_im = ssm_params_grad(
        a_re, a_im, logdt_b, bt_re, bt_im, spread, gather,
        _state_layout_inv(dlam_re_l, G, N), _state_layout_inv(dlam_im_l, G, N), dbbt_re, dbbt_im)
    q_per_kv = d_attn // HEAD_DIM // N_KV_HEADS
    small_grads = {
        "g_pre_mix": dg_pre_mix, "sinks": dsink[:, :q_per_kv, 0].reshape(1, -1),
        "a_re": d_a_re, "a_im": d_a_im, "log_dt": d_logdt.reshape(1, G),
        "b_re": dbt_re, "b_im": dbt_im,
        "c_re": dc_re, "c_im": dc_im,
        "d_skip": dd_skip, "b_glu": db_glu, "g_attn_out": dg_attn, "g_ssm_out": dg_ssm,
        "g_post_mix": dg_post_mix, "g_pre_ffn": dg_pre_ffn, "g_post_ffn": dg_post_ffn,
    }
    return loss, grad_x, small_grads


WEIGHTS = ['g_pre_mix', 'w_in', 'sinks', 'a_re', 'a_im', 'log_dt', 'b_re', 'b_im', 'c_re', 'c_im', 'd_skip', 'w_glu', 'b_glu',
           'g_attn_out', 'g_ssm_out', 'w_o', 'g_post_mix', 'g_pre_ffn', 'w_gate', 'w_up', 'w_down', 'g_post_ffn']
BIG = ['w_in', 'w_glu', 'w_o', 'w_gate', 'w_up', 'w_down']
COL_SHARDED = ['w_in', 'w_gate', 'w_up']
SMALL = [n for n in WEIGHTS if n not in BIG]
GATHER_GROUPS = [["w_in"], ["w_glu", "w_o"], ["w_gate", "w_up"], ["w_down"]]
REDUCE_GROUPS = [["w_down", "w_gate", "w_up"], ["w_o", "w_glu", "w_in"]]


PACK_ROWS = 256


def _pack(parts):
    flat = jnp.concatenate([p.reshape(-1) for p in parts])
    pad = (-flat.size) % (PACK_ROWS * LANES)
    return jnp.pad(flat, (0, pad)).reshape(-1, LANES)


TRANSPOSED_VIEW = ("b_re", "b_im")


def small_view(name, a):
    if name in TRANSPOSED_VIEW:
        a = a.transpose(0, 1, 3, 2)
    return a.reshape(-1, a.shape[-1])


def small_unview(name, p, shape):
    if name in TRANSPOSED_VIEW:
        return p.reshape(shape[0], shape[1], shape[3], shape[2]).transpose(0, 1, 3, 2)
    return p.reshape(shape)


def _unpack(packed, shapes):
    flat = packed.reshape(-1)
    out, off = [], 0
    for s in shapes:
        n = int(np.prod(s))
        out.append(flat[off:off + n].reshape(s))
        off += n
    return out


def kernel(x, positions, g_pre_mix, w_in, sinks, a_re, a_im, log_dt, b_re, b_im, c_re, c_im, d_skip, w_glu, b_glu, g_attn_out, g_ssm_out, w_o, g_post_mix, g_pre_ffn, w_gate, w_up, w_down, g_post_ffn, loss_target, m_g_pre_mix, m_w_in, m_sinks, m_a_re, m_a_im, m_log_dt, m_b_re, m_b_im, m_c_re, m_c_im, m_d_skip, m_w_glu, m_b_glu, m_g_attn_out, m_g_ssm_out, m_w_o, m_g_post_mix, m_g_pre_ffn, m_w_gate, m_w_up, m_w_down, m_g_post_ffn, v_g_pre_mix, v_w_in, v_sinks, v_a_re, v_a_im, v_log_dt, v_b_re, v_b_im, v_c_re, v_c_im, v_d_skip, v_w_glu, v_b_glu, v_g_attn_out, v_g_ssm_out, v_w_o, v_g_post_mix, v_g_pre_ffn, v_w_gate, v_w_up, v_w_down, v_g_post_ffn):
    args = dict(locals())
    w = {n: args[n] for n in WEIGHTS}
    m = {n: args["m_" + n] for n in WEIGHTS}
    v = {n: args["v_" + n] for n in WEIGHTS}
    L, D = x.shape[1], x.shape[2]

    ax, ay, ac = _place()
    mine_arr = (2 * ax + ay).astype(jnp.int32).reshape(1)
    c_arr = ac.astype(jnp.int32).reshape(1)

    me_arr = (4 * ax + 2 * ay + ac).astype(jnp.int32).reshape(1)

    bufs = {"w_in": into_slot("cast_w_in", w["w_in"][0], mine_arr, N_CHIPS, bf16)}
    (first_sems,), (bufs["w_in"],), token = gather_start("gather_start_in", [bufs["w_in"]], [[0]], mine_arr, neighbours_only=(0,))
    sems, relays, ready = {"w_in": first_sems}, {}, set()

    def relay(n, more, after):
        r_send, r_recv, more_sems, (bufs[n],), started, tok = gather_relay(
            "gather_relay_" + n, [bufs[n]], [sems[n]], [bufs[k] for k in more], after)
        sems.update({k: more_sems for k in more})
        bufs.update(zip(more, started))
        relays[n] = (r_send, r_recv)
        return tok

    for n in BIG[1:]:
        bufs[n] = into_slot("cast_" + n, w[n][0], mine_arr, N_CHIPS, bf16, dep=token)
    token = relay("w_in", [], bufs[BIG[-1]])
    first = ["w_gate", "w_glu", "w_o", "w_up"]
    (sems["w_gate"], sems["w_glu"], sems["w_up"]), started, token = gather_start(
        "gather_start_rest", [bufs[n] for n in first], [[0], [1, 2], [3]], token, neighbours_only=(0, 3))
    bufs.update(zip(first, started))

    def tick(name, after):
        return relay("w_gate", [], after) if name == "attn" else relay("w_up", ["w_down"], after)

    def get_w(n, after):
        if n not in ready:
            members = [g for g in GATHER_GROUPS if n in g][0]
            if members[0] in relays:
                landed = [gather_wait_relay("gather_wait_" + k, [bufs[k]], *relays[k], after)[0] for k in members]
                which = (2,)
            else:
                landed = gather_wait("gather_wait_" + members[0], [bufs[k] for k in members], *sems[members[0]], after)
                which = (0, 1, 2)
            bufs.update(zip(members, gather_forward("gather_forward_" + members[0], landed, which)))
            ready.update(members)
        g = bufs[n]
        return g if n in COL_SHARDED else g.reshape(g.shape[0] * g.shape[1], g.shape[2])

    swaps, operands, inflight = {}, {}, []
    other_arr = 1 - c_arr

    def put_g(n, a, b, dep):
        halves = (a.shape[1] // N_CHIPS // 2) % LANES == 0 if n not in COL_SHARDED else True
        if halves:
            operands[n] = (a, b)
            sent = mm_tn_half("d%s_sent" % n, a, b, n in COL_SHARDED, other_arr, dep=dep)
            swaps[n] = sibling_start("swap_start_" + n, [sent], False)
        else:
            g = weight_grad(n, a, b, dep)
            swaps[n] = sibling_start("swap_start_" + n, [g.reshape(N_CHIPS, g.shape[0] // N_CHIPS, g.shape[1])], True)
        for gi, members in enumerate(REDUCE_GROUPS):
            if n == members[-1]:
                last = swaps[n][4]
                pair = []
                for k in members:
                    send, recv, srcs, lands, _ = swaps[k]
                    (src,), (got,) = sibling_wait("swap_wait_" + k, srcs, lands, send, recv, k not in operands, last)
                    if k in operands:
                        pair.append(mm_tn_half("d%s_kept" % k, *operands[k], k in COL_SHARDED, c_arr, addend=got))
                    else:
                        pair.append(pair_sum("pair_sum_" + k, src, got, c_arr))
                send, recv, parts, lands, tok = scatter_start("scatter_start_%d" % gi, pair)
                inflight.append((members, send, recv, parts, lands))
                return tok
        return swaps[n][4]

    small = {n: w[n].reshape(1, -1) for n in SMALL}
    pos = positions.reshape(L, 1).astype(f32)
    d_in = N_CHIPS * w["w_in"].shape[2]
    loss, grad_x, small_grads = local_step(x[0], pos, loss_target[0], small, d_in, get_w, put_g, first_dep=token, tick=tick)

    shapes = [w[n].shape for n in SMALL]
    blocks = into_slot("small_block", _pack([small_grads[n] for n in SMALL] + [loss]), me_arr, N_DEV, f32)
    small_send, small_recv, blocks, after = everyone_start("small_start", blocks)

    grads, delta, new_m, new_v = {}, {}, {}, {}
    for gi, (members, send, recv, parts, lands) in enumerate(inflight):
        parts, landed = scatter_wait("scatter_wait_%d" % gi, parts, lands, send, recv, after)
        joins, dep = [], None
        for k, p, t in zip(members, parts, landed):
            joins.append(sibling_start("join_start_" + k, [chip_sum("chip_sum_" + k, p, t, mine_arr, dep=dep)], False))
            dep = after = joins[-1][4]
        for n, (send, recv, srcs, lands, _) in zip(members, joins):
            (own,), (sib,) = sibling_wait("join_wait_" + n, srcs, lands, send, recv, False, after)
            g_, d_, m_, v_ = adamw_halves("adamw_" + n, w[n][0], own, sib, m[n][0], v[n][0], c_arr)
            grads[n], delta[n], new_m[n], new_v[n] = g_[None], d_[None], m_[None], v_[None]
            after = v_
    blocks = everyone_wait("small_wait", blocks, small_send, small_recv, after)
    small_sum = sum_slots("small_sum", blocks)
    *small_g, loss = _unpack(small_sum, [small_view(n, w[n]).shape for n in SMALL] + [loss.shape])
    loss = loss[0, 0]
    outs = adamw_many("adamw_small", [small_view(n, w[n]) for n in SMALL], small_g,
                      [small_view(n, m[n]) for n in SMALL], [small_view(n, v[n]) for n in SMALL])
    for t, parts in zip((grads, delta, new_m, new_v), (small_g,) + tuple(outs)):
        t.update({n: small_unview(n, p, w[n].shape) for n, p in zip(SMALL, parts)})

    return (loss, grad_x[None], *[grads[n] for n in WEIGHTS], *[delta[n] for n in WEIGHTS],
            *[new_m[n] for n in WEIGHTS], *[new_v[n] for n in WEIGHTS])
```

```python
import functools
import math

import jax
import jax.numpy as jnp
import numpy as np
from jax import lax
from jax.experimental import pallas as pl
from jax.experimental.pallas import tpu as pltpu

f32 = jnp.float32
bf16 = jnp.bfloat16
HIGHEST = lax.Precision.HIGHEST
MESH = pl.DeviceIdType.MESH

HEAD_DIM = 64
N_KV_HEADS = 4
ATTN_BLOCK = 128
ROPE_THETA = 10000.0
SSM_GROUP = 16
SSM_STATE = 64
RMS_EPS = 1e-6
LANES = 128
SUBLANES = 8
VMEM_LIMIT = 52 * 1024 * 1024
N_CHIPS = 4
N_DEV = 8
NEG = -1e30

ADAM_LR, ADAM_B1, ADAM_B2, ADAM_EPS, ADAM_WD, ADAM_STEP = 0.001, 0.9, 0.999, 1e-08, 0.01, 10

NN = (((1,), (0,)), ((), ()))
NT = (((1,), (1,)), ((), ()))
TN = (((0,), (0,)), ((), ()))


def _params(*sem):
    return pltpu.CompilerParams(dimension_semantics=sem or None, vmem_limit_bytes=VMEM_LIMIT)


def _dot(a, b, dims=NN):
    return lax.dot_general(a, b, dims, preferred_element_type=f32)


def _pick(dim, pref):
    t = min(dim, pref)
    while dim % t:
        t -= LANES
    assert t > 0, (dim, pref)
    return t


ANY = pl.BlockSpec(memory_space=pl.ANY)


def _with_dep(in_specs, operands, dep):
    if dep is None:
        return list(in_specs), list(operands), 0
    return list(in_specs) + [ANY], list(operands) + [dep], 1


def _mm_call(name, grid, in_specs, out_spec, out_shape, acc_shape, dims, operands, dep=None):
    nk = grid[2]
    in_specs, operands, n_dep = _with_dep(in_specs, operands, dep)

    def body_one(a_ref, b_ref, *rest):
        o_ref = rest[n_dep]
        o_ref[...] = _dot(a_ref[...], b_ref[...], dims).astype(o_ref.dtype)

    def body(a_ref, b_ref, *rest):
        o_ref, acc_ref = rest[n_dep], rest[n_dep + 1]
        k = pl.program_id(2)

        @pl.when(k == 0)
        def _():
            acc_ref[...] = _dot(a_ref[...], b_ref[...], dims)

        @pl.when((k > 0) & (k < nk - 1))
        def _():
            acc_ref[...] += _dot(a_ref[...], b_ref[...], dims)

        @pl.when(k == nk - 1)
        def _():
            o_ref[...] = (acc_ref[...] + _dot(a_ref[...], b_ref[...], dims)).astype(o_ref.dtype)

    return pl.pallas_call(
        body_one if nk == 1 else body, out_shape=out_shape, grid=grid, in_specs=in_specs, out_specs=out_spec,
        scratch_shapes=[] if nk == 1 else [pltpu.VMEM(acc_shape, f32)], name=name,
        compiler_params=_params("parallel", "parallel", "arbitrary"))(*operands)


def mm_nt_pair(name, a1, b1, a2, b2, tm=1024, tk=1024, dep=None):
    M = a1.shape[0]
    S, K, n = b1.shape
    tm, tko = _pick(M, tm), _pick(K, tk)
    nk = 2 * S

    def body(a1_ref, b1_ref, a2_ref, b2_ref, *rest):
        o_ref, acc_ref = rest[-2], rest[-1]
        k = pl.program_id(2)

        @pl.when(k == 0)
        def _():
            acc_ref[...] = _dot(a1_ref[...], b1_ref[...], NT)

        @pl.when((k > 0) & (k < S))
        def _():
            acc_ref[...] += _dot(a1_ref[...], b1_ref[...], NT)

        @pl.when((k >= S) & (k < nk - 1))
        def _():
            acc_ref[...] += _dot(a2_ref[...], b2_ref[...], NT)

        @pl.when(k == nk - 1)
        def _():
            o_ref[...] = acc_ref[...] + _dot(a2_ref[...], b2_ref[...], NT)

    first = lambda k: jnp.minimum(k, S - 1)
    second = lambda k: jnp.maximum(k - S, 0)
    in_specs = [pl.BlockSpec((tm, n), lambda i, j, k: (i, first(k))), pl.BlockSpec((None, tko, n), lambda i, j, k: (first(k), j, 0)),
                pl.BlockSpec((tm, n), lambda i, j, k: (i, second(k))), pl.BlockSpec((None, tko, n), lambda i, j, k: (second(k), j, 0))]
    in_specs, operands, _ = _with_dep(in_specs, (a1, b1, a2, b2), dep)
    return pl.pallas_call(
        body, out_shape=jax.ShapeDtypeStruct((M, K), f32), grid=(M // tm, K // tko, nk), in_specs=in_specs,
        out_specs=pl.BlockSpec((tm, tko), lambda i, j, k: (i, j)), scratch_shapes=[pltpu.VMEM((tm, tko), f32)], name=name,
        compiler_params=_params("parallel", "parallel", "arbitrary"))(*operands)


def mm_nn(name, a, b, out_dtype=f32, tm=1024, tn=1024, tk=2048, dep=None):
    M, K = a.shape
    tm, tk = _pick(M, tm), _pick(K, tk)
    if b.ndim == 3:
        S, _, n = b.shape
        tn = _pick(n, 2048)
        per = n // tn
        b_spec = pl.BlockSpec((None, tk, tn), lambda i, j, k: (j // per, k, j % per))
        N = S * n
    else:
        N = b.shape[1]
        tn = _pick(N, tn)
        b_spec = pl.BlockSpec((tk, tn), lambda i, j, k: (k, j))
    grid = (M // tm, N // tn, K // tk)
    return _mm_call(name, grid, [pl.BlockSpec((tm, tk), lambda i, j, k: (i, k)), b_spec],
                    pl.BlockSpec((tm, tn), lambda i, j, k: (i, j)), jax.ShapeDtypeStruct((M, N), out_dtype),
                    (tm, tn), NN, (a, b), dep)


def mm_nt(name, a, b, out_dtype=f32, tm=1024, tn=2048, tk=1024, dep=None):
    M, N = a.shape
    tm = _pick(M, tm)
    if b.ndim == 3:
        S, K, n = b.shape
        tr = _pick(n, 2048)
        per = n // tr
        tko = _pick(K, tk)
        b_spec = pl.BlockSpec((None, tko, tr), lambda i, j, k: (k // per, j, k % per))
    else:
        K = b.shape[0]
        tr = _pick(N, tn)
        tko = _pick(K, tk)
        b_spec = pl.BlockSpec((tko, tr), lambda i, j, k: (j, k))
    grid = (M // tm, K // tko, N // tr)
    return _mm_call(name, grid, [pl.BlockSpec((tm, tr), lambda i, j, k: (i, k)), b_spec],
                    pl.BlockSpec((tm, tko), lambda i, j, k: (i, j)), jax.ShapeDtypeStruct((M, K), out_dtype),
                    (tm, tko), NT, (a, b), dep)


def mm_tn(name, a, b, shards=None, out_dtype=f32, tm=1024, tn=1024, tl=2048, dep=None):
    L, K = a.shape
    N = b.shape[1]
    tl, tko = _pick(L, tl), _pick(K, tm)
    if shards:
        n = N // shards
        tn = _pick(n, 2048)
        per = n // tn
        o_spec = pl.BlockSpec((None, tko, tn), lambda i, j, k: (j // per, i, j % per))
        o_shape = jax.ShapeDtypeStruct((shards, K, n), out_dtype)
    else:
        tn = _pick(N, tn)
        o_spec = pl.BlockSpec((tko, tn), lambda i, j, k: (i, j))
        o_shape = jax.ShapeDtypeStruct((K, N), out_dtype)
    grid = (K // tko, N // tn, L // tl)
    return _mm_call(name, grid, [pl.BlockSpec((tl, tko), lambda i, j, k: (k, i)),
                                 pl.BlockSpec((tl, tn), lambda i, j, k: (k, j))],
                    o_spec, o_shape, (tko, tn), TN, (a, b), dep)


def mm_tn_half(name, a, b, col_sharded, half_arr, addend=None, dep=None, tn=2048):
    L, K = a.shape
    N = b.shape[1]
    S = N_CHIPS
    h, cols = (K // 2, N // S) if col_sharded else (K // S // 2, N)
    tko, tn = _pick(h, 1024), _pick(cols, tn)
    ni, nj = h // tko, cols // tn
    if col_sharded:
        a_map = lambda s, i, j, hf: (0, hf[0] * ni + i)
        b_map = lambda s, i, j, hf: (0, s * nj + j)
    else:
        a_map = lambda s, i, j, hf: (0, (2 * s + hf[0]) * ni + i)
        b_map = lambda s, i, j, hf: (0, j)
    o_spec = pl.BlockSpec((None, tko, tn), lambda s, i, j, hf: (s, i, j))
    n_add = 0 if addend is None else 1
    n_dep = 0 if dep is None else 1

    def body(hf_ref, a_ref, b_ref, *rest):
        o_ref = rest[n_add + n_dep]
        acc = _dot(a_ref[...], b_ref[...], TN)
        if n_add:
            acc = acc + rest[0][...].astype(f32)
        o_ref[...] = acc.astype(o_ref.dtype)

    in_specs = [pl.BlockSpec((L, tko), a_map), pl.BlockSpec((L, tn), b_map)] + [o_spec] * n_add
    in_specs, operands, _ = _with_dep(in_specs, [a, b] + ([addend] if n_add else []), dep)
    grid_spec = pltpu.PrefetchScalarGridSpec(num_scalar_prefetch=1, grid=(S, ni, nj), in_specs=in_specs, out_specs=o_spec)
    return pl.pallas_call(body, out_shape=jax.ShapeDtypeStruct((S, h, cols), bf16), grid_spec=grid_spec, name=name,
                          compiler_params=_params("parallel", "parallel", "parallel"))(half_arr, *operands)


def weight_grad(name, a, b, dep=None):
    if name in COL_SHARDED:
        return mm_tn("d" + name, a, b, shards=N_CHIPS, out_dtype=bf16, dep=dep)
    return mm_tn("d" + name, a, b, out_dtype=bf16, tm=a.shape[1] // N_CHIPS, dep=dep)


def ffn_hidden(hn, w_gate, w_up, tm=512):
    M, K = hn.shape
    S, _, n = w_gate.shape
    tm = _pick(M, tm)

    def body(a_ref, g_ref, u_ref, dg_ref, du_ref, hid_ref):
        a = a_ref[...]
        g = _dot(a, g_ref[...])
        u = _dot(a, u_ref[...])
        sg = sigmoid(g)
        act = g * sg
        dg_ref[...] = (u * (sg * (1.0 + g * (1.0 - sg)))).astype(bf16)
        du_ref[...] = act.astype(bf16)
        hid_ref[...] = (act * u).astype(bf16)

    w_spec = pl.BlockSpec((None, K, n), lambda s, i: (s, 0, 0))
    o_spec = pl.BlockSpec((tm, n), lambda s, i: (i, s))
    o = jax.ShapeDtypeStruct((M, S * n), bf16)
    return pl.pallas_call(
        body, out_shape=[o, o, o], grid=(S, M // tm), in_specs=[pl.BlockSpec((tm, K), lambda s, i: (i, 0)), w_spec, w_spec],
        out_specs=[o_spec, o_spec, o_spec], name="ffn_hidden", compiler_params=_params("parallel", "parallel"))(hn, w_gate, w_up)


def ffn_hidden_grad(d_ff, w_down, hid_dg, hid_du, tm=512):
    M, D = d_ff.shape
    F = w_down.shape[0]
    n = _pick(F // N_CHIPS, 2048)
    tm = _pick(M, tm)

    def body(a_ref, b_ref, pg_ref, pu_ref, dg_ref, du_ref):
        dh = _dot(a_ref[...], b_ref[...], NT)
        dg_ref[...] = (dh * pg_ref[...].astype(f32)).astype(bf16)
        du_ref[...] = (dh * pu_ref[...].astype(f32)).astype(bf16)

    t_spec = pl.BlockSpec((tm, n), lambda j, i: (i, j))
    o = jax.ShapeDtypeStruct((M, F), bf16)
    return pl.pallas_call(
        body, out_shape=[o, o], grid=(F // n, M // tm),
        in_specs=[pl.BlockSpec((tm, D), lambda j, i: (i, 0)), pl.BlockSpec((n, D), lambda j, i: (j, 0)), t_spec, t_spec],
        out_specs=[t_spec, t_spec], name="ffn_hidden_grad", compiler_params=_params("parallel", "parallel"))(d_ff, w_down, hid_dg, hid_du)


def rowwise(name, fn, L, rows, bcast, outs, reds=(), tr=256, dep=None):
    tr = min(tr, L)
    nt = L // tr
    n_rows, n_b, n_o = len(rows), len(bcast), len(outs)
    n_dep = 0 if dep is None else 1

    def body(*refs):
        i = pl.program_id(0)
        ins = [r[...] for r in refs[:n_rows + n_b]]
        res = fn(i, nt, *ins)
        o_refs = refs[n_rows + n_b + n_dep:]
        for k in range(n_o):
            o_refs[k][...] = res[k].astype(o_refs[k].dtype)
        if reds:
            @pl.when(i == 0)
            def _():
                for k in range(len(reds)):
                    o_refs[n_o + k][...] = jnp.zeros_like(o_refs[n_o + k])
            for k in range(len(reds)):
                o_refs[n_o + k][...] += res[n_o + k]

    def row_spec(width, cb, shift):
        if shift:
            return pl.BlockSpec((tr, width), lambda i: (jnp.minimum(i + shift, nt - 1), cb))
        return pl.BlockSpec((tr, width), lambda i: (i, cb))

    in_specs = [row_spec(w, cb, sh) for (_, w, cb, sh) in rows]
    in_specs += [pl.BlockSpec(b.shape, lambda i: (0, 0)) for b in bcast]
    out_specs = [pl.BlockSpec((tr, w), lambda i: (i, 0)) for (w, _) in outs]
    out_specs += [pl.BlockSpec((1, w), lambda i: (0, 0)) for w in reds]
    out_shape = [jax.ShapeDtypeStruct((L, w), dt) for (w, dt) in outs]
    out_shape += [jax.ShapeDtypeStruct((1, w), f32) for w in reds]
    in_specs, operands, _ = _with_dep(in_specs, [r[0] for r in rows] + list(bcast), dep)
    return pl.pallas_call(
        body, out_shape=out_shape, grid=(nt,), in_specs=in_specs, out_specs=out_specs, name=name,
        compiler_params=_params("arbitrary"))(*operands)


def full(a):
    return (a, a.shape[1], 0, 0)


def colsum(v):
    return jnp.sum(v, axis=0, keepdims=True)


def rms_fwd(x, g):
    r = lax.rsqrt(jnp.mean(x * x, axis=-1, keepdims=True) + RMS_EPS)
    return x * r * g


def rms_bwd(x, g, dy):
    r = lax.rsqrt(jnp.mean(x * x, axis=-1, keepdims=True) + RMS_EPS)
    xh = x * r
    dyg = dy * g
    dx = r * (dyg - xh * jnp.mean(dyg * xh, axis=-1, keepdims=True))
    return dx, colsum(dy * xh)


GELU_C = math.sqrt(2.0 / math.pi)


def gelu(y):
    return y * (0.5 * (1.0 + jnp.tanh(GELU_C * (y + 0.044715 * (y * y * y)))))


def gelu_grad(y):
    t = jnp.tanh(GELU_C * (y + 0.044715 * (y * y * y)))
    return 0.5 * (1.0 + t) + 0.5 * y * (1.0 - t * t) * (GELU_C * (1.0 + 3 * 0.044715 * (y * y)))


def sigmoid(v):
    return 1.0 / (1.0 + jnp.exp(-v))


def _lane(shape):
    return lax.broadcasted_iota(jnp.int32, shape, 1)


def _rot_chunk(t, cos, sin_signed):
    first = (_lane(t.shape) % HEAD_DIM) < (HEAD_DIM // 2)
    partner = jnp.where(first, pltpu.roll(t, LANES - HEAD_DIM // 2, 1), pltpu.roll(t, HEAD_DIM // 2, 1))
    return t * cos + partner * sin_signed


def _cos_sin(pos, inv_freq, inverse):
    ang = pos * inv_freq
    cos, sin = jnp.cos(ang), jnp.sin(ang)
    first = (_lane(ang.shape) % HEAD_DIM) < (HEAD_DIM // 2)
    sign = jnp.where(first, -1.0, 1.0) * (-1.0 if inverse else 1.0)
    return cos, sin * sign


def _dup_head(chunk, odd):
    low = _lane(chunk.shape) < HEAD_DIM
    x = jnp.where(low != odd, chunk, 0.0)
    return x + pltpu.roll(x, HEAD_DIM, 1)


def _chunks(v):
    return [v[:, LANES * c:LANES * (c + 1)] for c in range(v.shape[1] // LANES)]


def qkv_prep(proj, pos, inv_freq, d_attn, d_kv):
    L = proj.shape[0]
    d_ssm = proj.shape[1] - d_attn - 2 * d_kv
    half = d_ssm // 2
    scale = 1.0 / math.sqrt(HEAD_DIM)

    def fn(i, nt, q, k, v, u0, u1, p, invf):
        cos, sin = _cos_sin(p, invf, False)
        qr = jnp.concatenate([_rot_chunk(c, cos, sin) for c in _chunks(q)], axis=1) * scale
        kr = [_rot_chunk(c, cos, sin) for c in _chunks(k)]
        kk = jnp.concatenate([_dup_head(c, odd) for c in kr for odd in (False, True)], axis=1)
        vv = jnp.concatenate([_dup_head(c, odd) for c in _chunks(v) for odd in (False, True)], axis=1)
        return qr, kk, vv, jnp.concatenate([u0, u1], axis=1)

    u_cb = (d_attn + 2 * d_kv) // half
    return rowwise("qkv_prep", fn, L,
                   [(proj, d_attn, 0, 0), (proj, d_kv, d_attn // d_kv, 0), (proj, d_kv, d_attn // d_kv + 1, 0),
                    (proj, half, u_cb, 0), (proj, half, u_cb + 1, 0), full(pos)],
                   [inv_freq], [(d_attn, bf16), (2 * d_kv, bf16), (2 * d_kv, bf16), (d_ssm, bf16)])


def qkv_grad(dq, dkk_c, dkk_p, dvv_c, dvv_p, du, pos, inv_freq):
    L, d_attn = dq.shape
    d_kv = dkk_c.shape[1] // 2
    scale = 1.0 / math.sqrt(HEAD_DIM)

    def fold(cur, prev, i, nt):
        t = cur + jnp.where(i < nt - 1, prev, 0.0)
        out = []
        for c in range(t.shape[1] // (2 * LANES)):
            even, odd = t[:, 2 * c * LANES:(2 * c + 1) * LANES], t[:, (2 * c + 1) * LANES:(2 * c + 2) * LANES]
            even, odd = even + pltpu.roll(even, HEAD_DIM, 1), odd + pltpu.roll(odd, HEAD_DIM, 1)
            out.append(jnp.where(_lane(even.shape) < HEAD_DIM, even, odd))
        return out

    def fn(i, nt, dq_t, kc, kp, vc, vp, du_t, p, invf):
        cos, sin = _cos_sin(p, invf, True)
        dq_o = jnp.concatenate([_rot_chunk(c, cos, sin) for c in _chunks(dq_t)], axis=1) * scale
        dk_o = jnp.concatenate([_rot_chunk(c, cos, sin) for c in fold(kc, kp, i, nt)], axis=1)
        dv_o = jnp.concatenate(fold(vc, vp, i, nt), axis=1)
        return (jnp.concatenate([dq_o, dk_o, dv_o, du_t], axis=1),)

    return rowwise("qkv_grad", fn, L,
                   [full(dq), full(dkk_c), (dkk_p, 2 * d_kv, 0, 1), full(dvv_c), (dvv_p, 2 * d_kv, 0, 1), full(du), full(pos)],
                   [inv_freq], [(d_attn + 2 * d_kv + du.shape[1], bf16)], tr=ATTN_BLOCK)[0]


def _attn_specs(L):
    nb = L // ATTN_BLOCK
    B = ATTN_BLOCK
    q_spec = lambda width: pl.BlockSpec((B, width), lambda n: (n, 0))
    prev = lambda width: pl.BlockSpec((B, width), lambda n: (jnp.maximum(n - 1, 0), 0))
    return nb, q_spec, prev


def _attn_mask(n):
    B = ATTN_BLOCK
    row = lax.broadcasted_iota(jnp.int32, (B, 2 * B), 0)
    col = lax.broadcasted_iota(jnp.int32, (B, 2 * B), 1)
    return ((col < B) & (col > row) & (n > 0)) | ((col >= B) & (row >= col - B))


def _attn_probs(qm, kcat, sink, mask):
    s = jnp.where(mask, _dot(qm, kcat, NT), NEG)
    m = jnp.maximum(jnp.max(s, axis=1, keepdims=True), sink)
    p, ps = jnp.exp(s - m), jnp.exp(sink - m)
    inv = 1.0 / (jnp.sum(p, axis=1, keepdims=True) + ps)
    return p, inv, ps


def _attn_heads(q_ref, s_ref, h, q_per_kv):
    low = _lane((ATTN_BLOCK, LANES)) < HEAD_DIM
    heads = []
    for pr in range(h * q_per_kv // 2, (h + 1) * q_per_kv // 2):
        q2 = q_ref[:, LANES * pr:LANES * (pr + 1)]
        for odd in (False, True):
            mine = low != odd
            sink = jnp.max(s_ref[2 * pr + int(odd):2 * pr + int(odd) + 1, :], axis=1, keepdims=True)
            heads.append((pr, mine, jnp.where(mine, q2, jnp.zeros_like(q2)), sink))
    return low, heads


def _kv_block(prev_ref, cur_ref, h):
    return jnp.concatenate([prev_ref[:, LANES * h:LANES * (h + 1)], cur_ref[:, LANES * h:LANES * (h + 1)]], axis=0)


def attn_fwd(qr, kk, vv, sink_b):
    L, d_attn = qr.shape
    nb, q_spec, prev = _attn_specs(L)
    d_kk = kk.shape[1]
    n_kv = d_kk // LANES
    q_per_kv = d_attn // HEAD_DIM // n_kv

    def body(q_ref, kc_ref, kp_ref, vc_ref, vp_ref, s_ref, o_ref):
        mask = _attn_mask(pl.program_id(0))
        for h in range(n_kv):
            kcat, vcat = _kv_block(kp_ref, kc_ref, h), _kv_block(vp_ref, vc_ref, h)
            low, heads = _attn_heads(q_ref, s_ref, h, q_per_kv)
            probs = [_attn_probs(qm, kcat, sink, mask) for (_, _, qm, sink) in heads]
            outs = [_dot(p.astype(bf16), vcat) * inv for (p, inv, _) in probs]
            for i in range(0, len(heads), 2):
                pr = heads[i][0]
                o_ref[:, LANES * pr:LANES * (pr + 1)] = jnp.where(low, outs[i], outs[i + 1])

    return pl.pallas_call(
        body, out_shape=jax.ShapeDtypeStruct((L, d_attn), f32), grid=(nb,),
        in_specs=[q_spec(d_attn), q_spec(d_kk), prev(d_kk), q_spec(d_kk), prev(d_kk), pl.BlockSpec(sink_b.shape, lambda n: (0, 0))],
        out_specs=q_spec(d_attn), name="attn_fwd", compiler_params=_params("arbitrary"))(qr, kk, kk, vv, vv, sink_b)


def attn_bwd(qr, kk, vv, sink_b, attn, d_attn_out):
    L, d_attn = qr.shape
    nb, q_spec, prev = _attn_specs(L)
    d_kk = kk.shape[1]
    n_kv = d_kk // LANES
    q_per_kv = d_attn // HEAD_DIM // n_kv

    def body(q_ref, kc_ref, kp_ref, vc_ref, vp_ref, s_ref, o_ref, do_ref, dq_ref, dkc_ref, dkp_ref, dvc_ref, dvp_ref, ds_ref):
        n = pl.program_id(0)
        B = ATTN_BLOCK
        mask = _attn_mask(n)
        srow = lax.broadcasted_iota(jnp.int32, (SUBLANES, LANES), 0)

        @pl.when(n == 0)
        def _():
            ds_ref[...] = jnp.zeros_like(ds_ref)

        for h in range(n_kv):
            kcat, vcat = _kv_block(kp_ref, kc_ref, h), _kv_block(vp_ref, vc_ref, h)
            low, heads = _attn_heads(q_ref, s_ref, h, q_per_kv)
            probs = [_attn_probs(qm, kcat, sink, mask) for (_, _, qm, sink) in heads]
            dk = jnp.zeros((2 * B, LANES), f32)
            dv = dk
            dsink = jnp.zeros((SUBLANES, LANES), f32)
            dqs = []
            for i, ((pr, mine, qm, _), (p, inv, ps)) in enumerate(zip(heads, probs)):
                do2 = do_ref[:, LANES * pr:LANES * (pr + 1)]
                delta = jnp.sum(jnp.where(mine, do2 * o_ref[:, LANES * pr:LANES * (pr + 1)], 0.0), axis=1, keepdims=True)
                dob = jnp.where(mine, do2, 0.0).astype(bf16)
                p = p * inv
                ds = (p * (_dot(dob, vcat, NT) - delta)).astype(bf16)
                dqs.append(_dot(ds, kcat))
                dk = dk + _dot(ds, qm, TN)
                dv = dv + _dot(p.astype(bf16), dob, TN)
                dsink = dsink + jnp.where(srow == i, -jnp.sum(ps * inv * delta), 0.0)
            for i in range(0, len(heads), 2):
                pr = heads[i][0]
                dq_ref[:, LANES * pr:LANES * (pr + 1)] = jnp.where(low, dqs[i], dqs[i + 1])
            cols = slice(LANES * h, LANES * (h + 1))
            dkp_ref[:, cols] = dk[:B]
            dkc_ref[:, cols] = dk[B:]
            dvp_ref[:, cols] = dv[:B]
            dvc_ref[:, cols] = dv[B:]
            ds_ref[h] += dsink

    kv_shape = jax.ShapeDtypeStruct(kk.shape, f32)
    ds_shape = (n_kv, SUBLANES, LANES)
    return pl.pallas_call(
        body,
        out_shape=[jax.ShapeDtypeStruct((L, d_attn), f32), kv_shape, kv_shape, kv_shape, kv_shape, jax.ShapeDtypeStruct(ds_shape, f32)],
        grid=(nb,),
        in_specs=[q_spec(d_attn), q_spec(d_kk), prev(d_kk), q_spec(d_kk), prev(d_kk), pl.BlockSpec(sink_b.shape, lambda n: (0, 0)),
                  q_spec(d_attn), q_spec(d_attn)],
        out_specs=[q_spec(d_attn)] + [q_spec(d_kk)] * 4 + [pl.BlockSpec(ds_shape, lambda n: (0, 0, 0))],
        name="attn_bwd", compiler_params=_params("arbitrary"))(qr, kk, kk, vv, vv, sink_b, attn, d_attn_out)


SSM_T = 128
NQ = SUBLANES * SSM_STATE // LANES
NJ = SUBLANES


SCAN_UNROLL = 8


def _unrolled(step):
    def body(k, carry):
        for u in range(SCAN_UNROLL):
            carry = step(k * SCAN_UNROLL + u, carry)
        return carry
    return body


def _strided_put(ref, j, val):
    for q in range(NQ):
        ref.at[q][pl.ds(j, SSM_T, stride=NJ), :] = val[:, LANES * q:LANES * (q + 1)]


def _strided_get(ref, j):
    return jnp.concatenate([ref.at[q][pl.ds(j, SSM_T, stride=NJ), :] for q in range(NQ)], axis=1)


def _ssm_specs(L, rev):
    nt = L // SSM_T
    idx = (lambda i: nt - 1 - i) if rev else (lambda i: i)
    row = lambda w, cb=0: pl.BlockSpec((SSM_T, w), lambda i: (idx(i), cb))
    state = pl.BlockSpec((NQ, SSM_T * NJ, LANES), lambda i: (0, idx(i), 0))
    whole = lambda a: pl.BlockSpec(a.shape, lambda i: (0,) * a.ndim)
    return nt, row, state, whole


def ssm_fwd(u_bf, proj, u_cb, bd_re, bd_im, cd_re, cd_im, lam_re, lam_im, d_skip, dep=None):
    L, d_ssm = u_bf.shape
    nt, row, state, whole = _ssm_specs(L, False)
    half = d_ssm // 2
    gw = d_ssm // NJ
    n_dep = 0 if dep is None else 1

    def body(u_ref, u0_ref, u1_ref, bdr, bdi, cdr, cdi, lr_ref, li_ref, d_ref, *rest):
        y_ref, z_ref, sr_ref, si_ref, carry = rest[n_dep:]
        i = pl.program_id(0)

        @pl.when(i == 0)
        def _():
            carry[...] = jnp.zeros_like(carry)

        for j in range(NJ):
            uj = u_ref[:, gw * j:gw * (j + 1)]
            _strided_put(sr_ref, j, _dot(uj, bdr[j]))
            _strided_put(si_ref, j, _dot(uj, bdi[j]))
        lr = [lr_ref[q] for q in range(NQ)]
        li = [li_ref[q] for q in range(NQ)]

        def step(t, s):
            sr, si = s
            rows = pl.ds(pl.multiple_of(t * NJ, NJ), NJ)
            nr = tuple(lr[q] * sr[q] - li[q] * si[q] + sr_ref[q, rows, :] for q in range(NQ))
            ni = tuple(lr[q] * si[q] + li[q] * sr[q] + si_ref[q, rows, :] for q in range(NQ))
            for q in range(NQ):
                sr_ref[q, rows, :] = nr[q]
                si_ref[q, rows, :] = ni[q]
            return nr, ni

        init = (tuple(carry[0, q] for q in range(NQ)), tuple(carry[1, q] for q in range(NQ)))
        sr, si = lax.fori_loop(0, SSM_T // SCAN_UNROLL, _unrolled(step), init)
        for q in range(NQ):
            carry[0, q] = sr[q]
            carry[1, q] = si[q]
        uf = jnp.concatenate([u0_ref[...], u1_ref[...]], axis=1)
        for j in range(NJ):
            cols = slice(gw * j, gw * (j + 1))
            yj = _dot(_strided_get(sr_ref, j).astype(bf16), cdr[j]) - _dot(_strided_get(si_ref, j).astype(bf16), cdi[j])
            yj = yj + d_ref[:, cols] * uf[:, cols]
            y_ref[:, cols] = yj
            z_ref[:, cols] = gelu(yj).astype(bf16)

    s_shape = jax.ShapeDtypeStruct((NQ, L * NJ, LANES), f32)
    consts = (bd_re, bd_im, cd_re, cd_im, lam_re, lam_im, d_skip)
    in_specs, operands, _ = _with_dep([row(d_ssm), row(half, u_cb), row(half, u_cb + 1)] + [whole(a) for a in consts],
                                      [u_bf, proj, proj, *consts], dep)
    return pl.pallas_call(
        body, out_shape=[jax.ShapeDtypeStruct((L, d_ssm), f32), jax.ShapeDtypeStruct((L, d_ssm), bf16), s_shape, s_shape], grid=(nt,),
        in_specs=in_specs, out_specs=[row(d_ssm), row(d_ssm), state, state],
        scratch_shapes=[pltpu.VMEM((2, NQ, NJ, LANES), f32)], name="ssm_fwd",
        compiler_params=_params("arbitrary"))(*operands)


def ssm_bwd(y, dz1, dz2, u_bf, proj, u_cb, s_re, s_im, bd_re, bd_im, cd_re, cd_im, lam_re, lam_im, d_skip, dep=None):
    L, d_ssm = y.shape
    nt, row, state, whole = _ssm_specs(L, True)
    half = d_ssm // 2
    gw = d_ssm // NJ
    n_dep = 0 if dep is None else 1

    def body(y_ref, dz1_ref, dz2_ref, u_ref, u0_ref, u1_ref, sr_ref, si_ref, bdr, bdi, cdr, cdi, lr_ref, li_ref, d_ref, *rest):
        du_ref, fbr, fbi, fcr, fci, dlr, dli, dd_ref, gr_ref, gi_ref, carry, dbdr, dbdi, dcdr, dcdi = rest[n_dep:]
        i = pl.program_id(0)

        @pl.when(i == 0)
        def _():
            carry[...] = jnp.zeros_like(carry)
            for r in (dbdr, dbdi, dcdr, dcdi, dlr, dli, dd_ref):
                r[...] = jnp.zeros_like(r)

        dyf = (dz1_ref[...] + dz2_ref[...]) * gelu_grad(y_ref[...])
        dyb = dyf.astype(bf16)
        for j in range(NJ):
            dyj = dyb[:, gw * j:gw * (j + 1)]
            _strided_put(gr_ref, j, _dot(dyj, cdr[j], NT))
            _strided_put(gi_ref, j, -_dot(dyj, cdi[j], NT))
            dcdr[j] += _dot(dyj, _strided_get(sr_ref, j).astype(bf16), TN)
            dcdi[j] -= _dot(dyj, _strided_get(si_ref, j).astype(bf16), TN)
        lr = [lr_ref[q] for q in range(NQ)]
        li = [li_ref[q] for q in range(NQ)]

        def step(k, c):
            gr, gi, ar, ai = c
            rows = pl.ds(pl.multiple_of((SSM_T - 1 - k) * NJ, NJ), NJ)
            s_r = [sr_ref[q, rows, :] for q in range(NQ)]
            s_i = [si_ref[q, rows, :] for q in range(NQ)]
            ar = tuple(ar[q] + gr[q] * s_r[q] + gi[q] * s_i[q] for q in range(NQ))
            ai = tuple(ai[q] + gi[q] * s_r[q] - gr[q] * s_i[q] for q in range(NQ))
            nr = tuple(gr_ref[q, rows, :] + lr[q] * gr[q] + li[q] * gi[q] for q in range(NQ))
            ni = tuple(gi_ref[q, rows, :] + lr[q] * gi[q] - li[q] * gr[q] for q in range(NQ))
            for q in range(NQ):
                gr_ref[q, rows, :] = nr[q]
                gi_ref[q, rows, :] = ni[q]
            return nr, ni, ar, ai

        zero = tuple(jnp.zeros((NJ, LANES), f32) for _ in range(NQ))
        init = (tuple(carry[0, q] for q in range(NQ)), tuple(carry[1, q] for q in range(NQ)), zero, zero)
        gr, gi, ar, ai = lax.fori_loop(0, SSM_T // SCAN_UNROLL, _unrolled(step), init)
        for q in range(NQ):
            carry[0, q] = gr[q]
            carry[1, q] = gi[q]
            dlr[q] += ar[q]
            dli[q] += ai[q]
        uf = jnp.concatenate([u0_ref[...], u1_ref[...]], axis=1)
        dd_ref[...] += colsum(dyf * uf)
        for j in range(NJ):
            cols = slice(gw * j, gw * (j + 1))
            gjr, gji = _strided_get(gr_ref, j).astype(bf16), _strided_get(gi_ref, j).astype(bf16)
            du_ref[:, cols] = _dot(gjr, bdr[j], NT) + _dot(gji, bdi[j], NT) + d_ref[:, cols] * dyf[:, cols]
            uj = u_ref[:, cols]
            dbdr[j] += _dot(uj, gjr, TN)
            dbdi[j] += _dot(uj, gji, TN)

        @pl.when(i == nt - 1)
        def _():
            nb = NJ * SSM_STATE
            diag = (lax.broadcasted_iota(jnp.int32, (gw, nb), 0) // SSM_GROUP) == (lax.broadcasted_iota(jnp.int32, (gw, nb), 1) // SSM_STATE)
            for j in range(NJ):
                for acc_ref, out in ((dbdr, fbr), (dbdi, fbi), (dcdr, fcr), (dcdi, fci)):
                    m = jnp.where(diag, acc_ref[j], 0.0)
                    f = m[:, :LANES]
                    for q in range(1, nb // LANES):
                        f = f + m[:, LANES * q:LANES * (q + 1)]
                    out[j] = f + pltpu.roll(f, SSM_STATE, 1)

    consts = (bd_re, bd_im, cd_re, cd_im, lam_re, lam_im, d_skip)
    acc = lambda a: jax.ShapeDtypeStruct(a.shape, f32)
    fb = jax.ShapeDtypeStruct((NJ, gw, LANES), f32)
    outs = [jax.ShapeDtypeStruct((L, d_ssm), f32), fb, fb, fb, fb, acc(lam_re), acc(lam_im), acc(d_skip)]
    in_specs, operands, _ = _with_dep(
        [row(d_ssm)] * 4 + [row(half, u_cb), row(half, u_cb + 1), state, state] + [whole(a) for a in consts],
        [y, dz1, dz2, u_bf, proj, proj, s_re, s_im, *consts], dep)
    return pl.pallas_call(
        body, out_shape=outs, grid=(nt,),
        in_specs=in_specs, out_specs=[row(d_ssm)] + [whole(a) for a in outs[1:]],
        scratch_shapes=[pltpu.VMEM((NQ, SSM_T * NJ, LANES), f32), pltpu.VMEM((NQ, SSM_T * NJ, LANES), f32),
                        pltpu.VMEM((2, NQ, NJ, LANES), f32)] + [pltpu.VMEM(bd_re.shape, f32)] * 4,
        name="ssm_bwd", compiler_params=_params("arbitrary"))(*operands)


def _cmul(ar, ai, br, bi):
    return ar * br - ai * bi, ar * bi + ai * br


def _disc(ar, ai, logdt):
    dt = jnp.exp(logdt)
    mag = jnp.exp(ar * dt)
    lr, li = mag * jnp.cos(ai * dt), mag * jnp.sin(ai * dt)
    den = ar * ar + ai * ai
    nr, ni = lr - 1.0, li
    fr, fi = (nr * ar + ni * ai) / den, (ni * ar - nr * ai) / den
    return dt, lr, li, den, fr, fi


def ssm_params(a_re, a_im, logdt_b, bt_re, bt_im, spread):
    def body(ar_ref, ai_ref, ld_ref, br_ref, bi_ref, sp_ref, lr_ref, li_ref, or_ref, oi_ref):
        _, lr, li, _, fr, fi = _disc(ar_ref[...], ai_ref[...], ld_ref[...])
        lr_ref[...] = lr
        li_ref[...] = li
        fre = jnp.dot(sp_ref[...], fr, precision=HIGHEST, preferred_element_type=f32)
        fie = jnp.dot(sp_ref[...], fi, precision=HIGHEST, preferred_element_type=f32)
        o_r, o_i = _cmul(fre, fie, br_ref[...], bi_ref[...])
        or_ref[...] = o_r
        oi_ref[...] = o_i

    g = jax.ShapeDtypeStruct(a_re.shape, f32)
    b = jax.ShapeDtypeStruct(bt_re.shape, f32)
    return pl.pallas_call(body, out_shape=[g, g, b, b], name="ssm_params",
                          compiler_params=_params())(a_re, a_im, logdt_b, bt_re, bt_im, spread)


def ssm_params_grad(a_re, a_im, logdt_b, bt_re, bt_im, spread, gather, dlam_re, dlam_im, dbt_re, dbt_im):
    def body(ar_ref, ai_ref, ld_ref, br_ref, bi_ref, sp_ref, ga_ref, glr_ref, gli_ref, gbr_ref, gbi_ref,
             dar_ref, dai_ref, dld_ref, dbr_ref, dbi_ref):
        ar, ai = ar_ref[...], ai_ref[...]
        dt, lr, li, den, fr, fi = _disc(ar, ai, ld_ref[...])
        hdot = functools.partial(jnp.dot, precision=HIGHEST, preferred_element_type=f32)
        fre, fie = hdot(sp_ref[...], fr), hdot(sp_ref[...], fi)
        gbr, gbi, br, bi = gbr_ref[...], gbi_ref[...], br_ref[...], bi_ref[...]
        dbr_ref[...], dbi_ref[...] = _cmul(fre, -fie, gbr, gbi)
        t_r, t_i = _cmul(br, -bi, gbr, gbi)
        gfr, gfi = hdot(ga_ref[...], t_r), hdot(ga_ref[...], t_i)
        iwr, iwi = ar / den, -ai / den
        x_r, x_i = _cmul(iwr, -iwi, gfr, gfi)
        glr, gli = glr_ref[...] + x_r, gli_ref[...] + x_i
        q_r, q_i = _cmul(fr, fi, iwr, iwi)
        gwr, gwi = _cmul(-q_r, q_i, gfr, gfi)
        y_r, y_i = _cmul(dt * lr, -dt * li, glr, gli)
        dar_ref[...] = gwr + y_r
        dai_ref[...] = gwi + y_i
        wl_r, wl_i = _cmul(ar, ai, lr, li)
        z_r, _ = _cmul(wl_r, -wl_i, glr, gli)
        dld_ref[...] = jnp.sum(z_r * dt, axis=1, keepdims=True)

    g = jax.ShapeDtypeStruct(a_re.shape, f32)
    b = jax.ShapeDtypeStruct(bt_re.shape, f32)
    return pl.pallas_call(body, out_shape=[g, g, jax.ShapeDtypeStruct((a_re.shape[0], 1), f32), b, b], name="ssm_params_grad",
                          compiler_params=_params())(a_re, a_im, logdt_b, bt_re, bt_im, spread, gather, dlam_re, dlam_im, dbt_re, dbt_im)


def _block_diag(t, rows, cols):
    G = t.shape[0]
    t = t.reshape(G // NJ, NJ, rows, cols)
    eye = jnp.eye(NJ, dtype=t.dtype)
    return jnp.einsum('jgrc,gh->jgrhc', t, eye).reshape(G // NJ, NJ * rows, NJ * cols)


def _state_layout(t):
    return t.reshape(NJ, NQ, LANES).transpose(1, 0, 2)


def _state_layout_inv(t, G, N):
    return t.transpose(1, 0, 2).reshape(G, N)


def _tiles2d(shape, budget_rows=128):
    rows, cols = shape
    tr = rows
    if rows > budget_rows:
        tr = budget_rows
        while rows % tr:
            tr -= SUBLANES
    return tr, cols


ADAM_TILE_BYTES = 3 << 19
ROW_ALIGN = 16


def _tile_rows(rows, row_bytes, target_bytes):
    tr = max(ROW_ALIGN, min(rows, target_bytes // row_bytes) // ROW_ALIGN * ROW_ALIGN)
    while rows % tr:
        tr -= ROW_ALIGN
    return tr


def _adam_update(w, g, m, v):
    c1 = 1.0 - ADAM_B1 ** ADAM_STEP
    c2 = 1.0 - ADAM_B2 ** ADAM_STEP
    nm = ADAM_B1 * m + (1.0 - ADAM_B1) * g
    nv = ADAM_B2 * v + (1.0 - ADAM_B2) * (g * g)
    delta = -ADAM_LR * ((nm / c1) / (jnp.sqrt(nv / c2) + ADAM_EPS) + ADAM_WD * w)
    return delta, nm, nv


def adamw_many(name, ws, gs, ms, vs):
    n = len(ws)

    def body(*refs):
        w, g, m, v = (refs[k * n:(k + 1) * n] for k in range(4))
        d, nm, nv = (refs[(4 + k) * n:(5 + k) * n] for k in range(3))
        for i in range(n):
            d[i][...], nm[i][...], nv[i][...] = _adam_update(w[i][...], g[i][...], m[i][...], v[i][...])

    o = [jax.ShapeDtypeStruct(a.shape, f32) for a in ws]
    outs = pl.pallas_call(body, out_shape=o * 3, name=name, compiler_params=_params())(*ws, *gs, *ms, *vs)
    return outs[:n], outs[n:2 * n], outs[2 * n:]


def adamw_halves(name, w, own, got, m, v, c_arr):
    h, cols = own.shape
    tr = _tile_rows(h, cols * 4, ADAM_TILE_BYTES)
    nh = h // tr

    def body(c_ref, w_ref, own_ref, got_ref, m_ref, v_ref, g_ref, d_ref, nm_ref, nv_ref):
        mine = (pl.program_id(0) // nh) == c_ref[0]
        g = jnp.where(mine, own_ref[...], got_ref[...])
        g_ref[...] = g
        d_ref[...], nm_ref[...], nv_ref[...] = _adam_update(w_ref[...], g, m_ref[...], v_ref[...])

    spec = pl.BlockSpec((tr, cols), lambda i, c: (i, 0))
    own_spec = pl.BlockSpec((tr, cols), lambda i, c: (jnp.where(i // nh == c[0], i % nh, 0), 0))
    got_spec = pl.BlockSpec((tr, cols), lambda i, c: (jnp.where(i // nh == c[0], 0, i % nh), 0))
    o = jax.ShapeDtypeStruct(w.shape, f32)
    grid_spec = pltpu.PrefetchScalarGridSpec(num_scalar_prefetch=1, grid=(2 * nh,),
                                             in_specs=[spec, own_spec, got_spec, spec, spec], out_specs=[spec] * 4)
    return pl.pallas_call(body, out_shape=[o, o, o, o], grid_spec=grid_spec, name=name,
                          compiler_params=_params("arbitrary"))(c_arr, w, own, got, m, v)


def pair_sum(name, g, got, c_arr):
    S, h, cols = got.shape
    tr, _ = _tiles2d((h, cols), 1024)
    nh = h // tr

    def body(c_ref, g_ref, o_ref, out_ref):
        out_ref[...] = (g_ref[...].astype(f32) + o_ref[...].astype(f32)).astype(out_ref.dtype)

    spec = pl.BlockSpec((None, tr, cols), lambda s, i, c: (s, i, 0))
    grid_spec = pltpu.PrefetchScalarGridSpec(
        num_scalar_prefetch=1, grid=(S, nh),
        in_specs=[pl.BlockSpec((None, tr, cols), lambda s, i, c: (s, c[0] * nh + i, 0)), spec], out_specs=spec)
    return pl.pallas_call(body, out_shape=jax.ShapeDtypeStruct(got.shape, g.dtype), grid_spec=grid_spec, name=name,
                          compiler_params=_params("parallel", "parallel"))(c_arr, g, got)


def chip_sum(name, pair, landed, mine_arr, dep=None):
    n_in, h, cols = landed.shape
    tr, _ = _tiles2d((h, cols), 256)

    def body(s_ref, p_ref, l_ref, *rest):
        acc = p_ref[...].astype(f32)
        for k in range(n_in):
            acc = acc + l_ref[k].astype(f32)
        rest[-1][...] = acc

    in_specs, operands, _ = _with_dep(
        [pl.BlockSpec((None, tr, cols), lambda i, s: (s[0], i, 0)), pl.BlockSpec((n_in, tr, cols), lambda i, s: (0, i, 0))],
        [pair, landed], dep)
    grid_spec = pltpu.PrefetchScalarGridSpec(num_scalar_prefetch=1, grid=(h // tr,), in_specs=in_specs,
                                             out_specs=pl.BlockSpec((tr, cols), lambda i, s: (i, 0)))
    return pl.pallas_call(body, out_shape=jax.ShapeDtypeStruct((h, cols), f32), grid_spec=grid_spec, name=name,
                          compiler_params=_params("parallel"))(mine_arr, *operands)


def into_slot(name, w, slot_arr, n_slots, dtype, dep=None):
    tr, cols = _tiles2d(w.shape, 256)

    def body(s_ref, w_ref, *rest):
        rest[-1][...] = w_ref[...].astype(dtype)

    in_specs, operands, _ = _with_dep([pl.BlockSpec((tr, cols), lambda i, s: (i, 0))], [w], dep)
    grid_spec = pltpu.PrefetchScalarGridSpec(num_scalar_prefetch=1, grid=(w.shape[0] // tr,), in_specs=in_specs,
                                             out_specs=pl.BlockSpec((None, tr, cols), lambda i, s: (s[0], i, 0)))
    return pl.pallas_call(body, out_shape=jax.ShapeDtypeStruct((n_slots,) + w.shape, dtype), grid_spec=grid_spec, name=name,
                          compiler_params=_params("parallel"))(slot_arr, *operands)


def sum_slots(name, t):
    S, rows, cols = t.shape
    tr, _ = _tiles2d((rows, cols), 256)

    def body(t_ref, o_ref):
        acc = t_ref[0]
        for s in range(1, S):
            acc = acc + t_ref[s]
        o_ref[...] = acc

    return pl.pallas_call(body, out_shape=jax.ShapeDtypeStruct((rows, cols), f32), grid=(rows // tr,),
                          in_specs=[pl.BlockSpec((S, tr, cols), lambda i: (0, i, 0))], out_specs=pl.BlockSpec((tr, cols), lambda i: (i, 0)),
                          name=name, compiler_params=_params("parallel"))(t)


def _place():
    x, y, c = lax.axis_index("x"), lax.axis_index("y"), lax.axis_index("c")
    return x, y, c


def _other_chips(x, y):
    return [(1 - x, y, 2 * (1 - x) + y), (x, 1 - y, 2 * x + 1 - y), (1 - x, 1 - y, 2 * (1 - x) + 1 - y)]


SEM = pl.BlockSpec(memory_space=pltpu.SEMAPHORE)
VM = pl.BlockSpec(memory_space=pltpu.VMEM)
DATAFLOW = pltpu.SideEffectType.DATAFLOW_SIDE_EFFECTING
TOKEN = jax.ShapeDtypeStruct((SUBLANES, LANES), f32)


def _gather_copy(buf, w, k, chip, c, mine, send, recv):
    px, py, _ = chip
    h = buf.shape[1] // 2
    half = buf.at[mine, pl.ds(c * h, h), :]
    return pltpu.make_async_remote_copy(src_ref=half, dst_ref=half, send_sem=send.at[3 * w + k], recv_sem=recv.at[3 * w + k],
                                        device_id=(px, py, c), device_id_type=MESH)


def _gather_landing(buf, w, k, chip, c, send, recv):
    px, py, s = chip
    h = buf.shape[1] // 2
    landed = buf.at[s, pl.ds(c * h, h), :]
    return pltpu.make_async_remote_copy(src_ref=landed, dst_ref=landed, send_sem=send.at[3 * w + k], recv_sem=recv.at[3 * w + k],
                                        device_id=(px, py, c), device_id_type=MESH)


def gather_start(name, bufs, groups, after, neighbours_only=()):
    nw, ng = len(bufs), len(groups)

    def body(*refs):
        outs = refs[nw + 1:]
        sems, dst = outs[:2 * ng], outs[2 * ng:2 * ng + nw]
        token = outs[2 * ng + nw]
        x, y, c = _place()
        mine = 2 * x + y
        for g, members in enumerate(groups):
            for i, w in enumerate(members):
                for k, chip in enumerate(_other_chips(x, y)[:2 if w in neighbours_only else 3]):
                    _gather_copy(dst[w], i, k, chip, c, mine, sems[2 * g], sems[2 * g + 1]).start()
        token[...] = jnp.zeros_like(token)

    sem_shapes = []
    for members in groups:
        sem_shapes += [pltpu.SemaphoreType.DMA((3 * len(members),))] * 2
    outs = pl.pallas_call(
        body, out_shape=sem_shapes + [jax.ShapeDtypeStruct(b.shape, b.dtype) for b in bufs] + [TOKEN],
        in_specs=[ANY] * (nw + 1), out_specs=[SEM] * (2 * ng) + [ANY] * nw + [VM],
        input_output_aliases={w: 2 * ng + w for w in range(nw)}, name=name,
        compiler_params=pltpu.CompilerParams(has_side_effects=DATAFLOW))(*bufs, after)
    return [(outs[2 * g], outs[2 * g + 1]) for g in range(ng)], list(outs[2 * ng:2 * ng + nw]), outs[2 * ng + nw]


def gather_wait(name, bufs, send, recv, after):
    nw = len(bufs)

    def body(*refs):
        src = refs[:nw]
        send_ref, recv_ref = refs[nw], refs[nw + 1]
        x, y, c = _place()
        mine = 2 * x + y
        for w in range(nw):
            for k, chip in enumerate(_other_chips(x, y)):
                _gather_copy(src[w], w, k, chip, c, mine, send_ref, recv_ref).wait_send()
                _gather_landing(src[w], w, k, chip, c, send_ref, recv_ref).wait_recv()

    return pl.pallas_call(
        body, out_shape=[jax.ShapeDtypeStruct(b.shape, b.dtype) for b in bufs],
        in_specs=[ANY] * nw + [SEM, SEM, ANY], out_specs=[ANY] * nw,
        input_output_aliases={w: w for w in range(nw)}, name=name,
        compiler_params=pltpu.CompilerParams(has_side_effects=DATAFLOW))(*bufs, send, recv, after)


def _relay_copy(buf, w, j, x, y, c, send, recv, landing):
    chips = _other_chips(x, y)
    px, py, _ = chips[j]
    h = buf.shape[1] // 2
    q = h // 2
    s = chips[2][2] if landing else chips[1 - j][2]
    part = buf.at[s, pl.ds(c * h + j * q, q), :]
    return pltpu.make_async_remote_copy(src_ref=part, dst_ref=part, send_sem=send.at[2 * w + j], recv_sem=recv.at[2 * w + j],
                                        device_id=(px, py, c), device_id_type=MESH)


def _early_pass(buf, nw, w, k, x, y, c, send, recv, landing):
    s = _other_chips(x, y)[k][2]
    h = buf.shape[1] // 2
    part = buf.at[s, pl.ds(((1 - c) if landing else c) * h, h), :]
    i = 2 * nw + 2 * w + k
    return pltpu.make_async_remote_copy(src_ref=part, dst_ref=part, send_sem=send.at[i], recv_sem=recv.at[i],
                                        device_id=(x, y, 1 - c), device_id_type=MESH)


def gather_relay(name, bufs, sems, more, after):
    nw, nm = len(bufs), len(more)
    ns = 4 if nm else 2

    def body(*refs):
        ins, outs = refs[:nw + nm + 2 * nw + 1], refs[nw + nm + 2 * nw + 1:]
        src, d_sems = ins[:nw], ins[nw + nm:nw + nm + 2 * nw]
        r_send, r_recv = outs[:2]
        m_send, m_recv = outs[2:ns] if nm else (None, None)
        dst, mdst, token = outs[ns:ns + nw], outs[ns + nw:ns + nw + nm], outs[ns + nw + nm]
        x, y, c = _place()
        mine = 2 * x + y
        chips = _other_chips(x, y)
        for w in range(nw):
            for k in range(2):
                _gather_copy(src[w], 0, k, chips[k], c, mine, d_sems[2 * w], d_sems[2 * w + 1]).wait_send()
                _gather_landing(src[w], 0, k, chips[k], c, d_sems[2 * w], d_sems[2 * w + 1]).wait_recv()
            for j in range(2):
                _relay_copy(dst[w], w, j, x, y, c, r_send, r_recv, False).start()
            for k in range(2):
                _early_pass(dst[w], nw, w, k, x, y, c, r_send, r_recv, False).start()
        for w in range(nm):
            for k, chip in enumerate(chips):
                _gather_copy(mdst[w], w, k, chip, c, mine, m_send, m_recv).start()
        token[...] = jnp.zeros_like(token)

    sem_shapes = [pltpu.SemaphoreType.DMA((4 * nw,))] * 2 + [pltpu.SemaphoreType.DMA((3 * nm,))] * (ns - 2)
    flat_sems = [s for pair in sems for s in pair]
    outs = pl.pallas_call(
        body, out_shape=sem_shapes + [jax.ShapeDtypeStruct(b.shape, b.dtype) for b in list(bufs) + list(more)] + [TOKEN],
        in_specs=[ANY] * (nw + nm) + [SEM] * (2 * nw) + [ANY], out_specs=[SEM] * ns + [ANY] * (nw + nm) + [VM],
        input_output_aliases={i: ns + i for i in range(nw + nm)}, name=name,
        compiler_params=pltpu.CompilerParams(has_side_effects=DATAFLOW))(*bufs, *more, *flat_sems, after)
    m_sems = (outs[2], outs[3]) if nm else None
    return outs[0], outs[1], m_sems, list(outs[ns:ns + nw]), list(outs[ns + nw:ns + nw + nm]), outs[ns + nw + nm]


def gather_wait_relay(name, bufs, r_send, r_recv, after):
    nw = len(bufs)

    def body(*refs):
        src = refs[:nw]
        send_ref, recv_ref = refs[nw], refs[nw + 1]
        x, y, c = _place()
        for w in range(nw):
            for j in range(2):
                _relay_copy(src[w], w, j, x, y, c, send_ref, recv_ref, False).wait_send()
                _relay_copy(src[w], w, j, x, y, c, send_ref, recv_ref, True).wait_recv()
                _early_pass(src[w], nw, w, j, x, y, c, send_ref, recv_ref, False).wait_send()
                _early_pass(src[w], nw, w, j, x, y, c, send_ref, recv_ref, True).wait_recv()

    return pl.pallas_call(
        body, out_shape=[jax.ShapeDtypeStruct(b.shape, b.dtype) for b in bufs],
        in_specs=[ANY] * nw + [SEM, SEM, ANY], out_specs=[ANY] * nw,
        input_output_aliases={w: w for w in range(nw)}, name=name,
        compiler_params=pltpu.CompilerParams(has_side_effects=DATAFLOW))(*bufs, r_send, r_recv, after)


def gather_forward(name, bufs, which=(0, 1, 2)):
    nw = len(bufs)

    def body(*refs):
        dst = refs[nw:2 * nw]
        send, recv = refs[2 * nw:]
        x, y, c = _place()
        sib = (x, y, 1 - c)
        barrier = pltpu.get_barrier_semaphore()
        pl.semaphore_signal(barrier, inc=1, device_id=sib, device_id_type=MESH)
        pl.semaphore_wait(barrier, 1)
        cps = []
        for w in range(nw):
            h = dst[w].shape[1] // 2
            for k in which:
                s = _other_chips(x, y)[k][2]
                landed = dst[w].at[s, pl.ds(c * h, h), :]
                cp = pltpu.make_async_remote_copy(src_ref=landed, dst_ref=landed, send_sem=send.at[w, k], recv_sem=recv.at[w, k],
                                                  device_id=sib, device_id_type=MESH)
                cp.start()
                cps.append(cp)
        for w in range(nw):
            h = dst[w].shape[1] // 2
            for k in which:
                s = _other_chips(x, y)[k][2]
                other = dst[w].at[s, pl.ds((1 - c) * h, h), :]
                pltpu.make_async_remote_copy(src_ref=other, dst_ref=other, send_sem=send.at[w, k], recv_sem=recv.at[w, k],
                                             device_id=sib, device_id_type=MESH).wait_recv()
        for cp in cps:
            cp.wait_send()

    sem = pltpu.SemaphoreType.DMA((nw, 3))
    return pl.pallas_call(
        body, out_shape=[jax.ShapeDtypeStruct(b.shape, b.dtype) for b in bufs],
        in_specs=[ANY] * nw, out_specs=[ANY] * nw, input_output_aliases={w: w for w in range(nw)},
        scratch_shapes=[sem, sem], name=name,
        compiler_params=pltpu.CompilerParams(has_side_effects=True, collective_id=SIBLING_PAIR))(*bufs)


def _scatter_copy(src, dst, w, k, chip, c, send, recv):
    px, py, s = chip
    return pltpu.make_async_remote_copy(src_ref=src.at[s], dst_ref=dst.at[k], send_sem=send.at[3 * w + k], recv_sem=recv.at[3 * w + k],
                                        device_id=(px, py, c), device_id_type=MESH)


def scatter_start(name, parts):
    nw = len(parts)
    lands = [pltpu.with_memory_space_constraint(lax.empty((N_CHIPS - 1,) + p.shape[1:], p.dtype), pltpu.HBM) for p in parts]

    def body(*refs):
        outs = refs[2 * nw:]
        send, recv = outs[0], outs[1]
        src, dst, token = outs[2:2 + nw], outs[2 + nw:2 + 2 * nw], outs[2 + 2 * nw]
        x, y, c = _place()
        for w in range(nw):
            for k, chip in enumerate(_other_chips(x, y)):
                _scatter_copy(src[w], dst[w], w, k, chip, c, send, recv).start()
        token[...] = jnp.zeros_like(token)

    sem = pltpu.SemaphoreType.DMA((3 * nw,))
    outs = pl.pallas_call(
        body, out_shape=[sem, sem] + [jax.ShapeDtypeStruct(p.shape, p.dtype) for p in parts]
        + [jax.ShapeDtypeStruct(l.shape, l.dtype) for l in lands] + [TOKEN],
        in_specs=[ANY] * (2 * nw), out_specs=[SEM, SEM] + [ANY] * (2 * nw) + [VM],
        input_output_aliases={i: 2 + i for i in range(2 * nw)}, name=name,
        compiler_params=pltpu.CompilerParams(has_side_effects=DATAFLOW))(*parts, *lands)
    return outs[0], outs[1], list(outs[2:2 + nw]), list(outs[2 + nw:2 + 2 * nw]), outs[2 + 2 * nw]


def scatter_wait(name, parts, lands, send, recv, after):
    nw = len(parts)

    def body(*refs):
        src, dst = refs[:nw], refs[nw:2 * nw]
        send_ref, recv_ref = refs[2 * nw], refs[2 * nw + 1]
        x, y, c = _place()
        for w in range(nw):
            for k, chip in enumerate(_other_chips(x, y)):
                cp = _scatter_copy(src[w], dst[w], w, k, chip, c, send_ref, recv_ref)
                cp.wait_send()
                cp.wait_recv()

    outs = pl.pallas_call(
        body, out_shape=[jax.ShapeDtypeStruct(a.shape, a.dtype) for a in list(parts) + list(lands)],
        in_specs=[ANY] * (2 * nw) + [SEM, SEM, ANY], out_specs=[ANY] * (2 * nw),
        input_output_aliases={i: i for i in range(2 * nw)}, name=name,
        compiler_params=pltpu.CompilerParams(has_side_effects=DATAFLOW))(*parts, *lands, send, recv, after)
    return list(outs[:nw]), list(outs[nw:])


SIBLING_PAIR = 0


def _sibling_copy(src, dst, w, c, half_rows, send, recv, sib):
    if half_rows:
        h = src.shape[1] // 2
        src = src.at[:, pl.ds((1 - c) * h, h), :]
    return pltpu.make_async_remote_copy(src_ref=src, dst_ref=dst, send_sem=send.at[w], recv_sem=recv.at[w],
                                        device_id=sib, device_id_type=MESH)


def _landing(shape, dtype):
    return pltpu.with_memory_space_constraint(lax.empty(shape, dtype), pltpu.HBM)


def sibling_start(name, srcs, half_rows):
    nw = len(srcs)
    lands = [_landing((s.shape[0], s.shape[1] // 2, s.shape[2]) if half_rows else s.shape, s.dtype) for s in srcs]

    def body(*refs):
        outs = refs[2 * nw:]
        send, recv = outs[0], outs[1]
        src, dst, token = outs[2:2 + nw], outs[2 + nw:2 + 2 * nw], outs[2 + 2 * nw]
        x, y, c = _place()
        barrier = pltpu.get_barrier_semaphore()
        pl.semaphore_signal(barrier, inc=1, device_id=(x, y, 1 - c), device_id_type=MESH)
        pl.semaphore_wait(barrier, 1)
        for w in range(nw):
            _sibling_copy(src[w], dst[w], w, c, half_rows, send, recv, (x, y, 1 - c)).start()
        token[...] = jnp.zeros_like(token)

    sem = pltpu.SemaphoreType.DMA((nw,))
    outs = pl.pallas_call(
        body, out_shape=[sem, sem] + [jax.ShapeDtypeStruct(a.shape, a.dtype) for a in list(srcs) + lands] + [TOKEN],
        in_specs=[ANY] * (2 * nw), out_specs=[SEM, SEM] + [ANY] * (2 * nw) + [VM],
        input_output_aliases={i: 2 + i for i in range(2 * nw)}, name=name,
        compiler_params=pltpu.CompilerParams(has_side_effects=DATAFLOW, collective_id=SIBLING_PAIR))(*srcs, *lands)
    return outs[0], outs[1], list(outs[2:2 + nw]), list(outs[2 + nw:2 + 2 * nw]), outs[2 + 2 * nw]


def sibling_wait(name, srcs, lands, send, recv, half_rows, after):
    nw = len(srcs)

    def body(*refs):
        src, dst = refs[:nw], refs[nw:2 * nw]
        send_ref, recv_ref = refs[2 * nw], refs[2 * nw + 1]
        x, y, c = _place()
        for w in range(nw):
            cp = _sibling_copy(src[w], dst[w], w, c, half_rows, send_ref, recv_ref, (x, y, 1 - c))
            cp.wait_send()
            cp.wait_recv()

    outs = pl.pallas_call(
        body, out_shape=[jax.ShapeDtypeStruct(a.shape, a.dtype) for a in list(srcs) + list(lands)],
        in_specs=[ANY] * (2 * nw) + [SEM, SEM, ANY], out_specs=[ANY] * (2 * nw),
        input_output_aliases={i: i for i in range(2 * nw)}, name=name,
        compiler_params=pltpu.CompilerParams(has_side_effects=DATAFLOW))(*srcs, *lands, send, recv, after)
    return list(outs[:nw]), list(outs[nw:])


def _peer(x, y, c, r):
    return (1 - x if r & 4 else x, 1 - y if r & 2 else y, 1 - c if r & 1 else c)


def _everyone_copy(buf, r, x, y, c, send, recv, landing):
    px, py, pc = _peer(x, y, c, r)
    slot = buf.at[4 * px + 2 * py + pc] if landing else buf.at[4 * x + 2 * y + c]
    return pltpu.make_async_remote_copy(src_ref=slot, dst_ref=slot, send_sem=send.at[r - 1], recv_sem=recv.at[r - 1],
                                        device_id=(px, py, pc), device_id_type=MESH)


def everyone_start(name, buf):
    def body(buf_in, send, recv, buf_ref, token):
        x, y, c = _place()
        for r in range(1, N_DEV):
            _everyone_copy(buf_ref, r, x, y, c, send, recv, False).start()
        token[...] = jnp.zeros_like(token)

    sem = pltpu.SemaphoreType.DMA((N_DEV - 1,))
    return pl.pallas_call(
        body, out_shape=[sem, sem, jax.ShapeDtypeStruct(buf.shape, buf.dtype), TOKEN],
        in_specs=[ANY], out_specs=[SEM, SEM, ANY, VM], input_output_aliases={0: 2}, name=name,
        compiler_params=pltpu.CompilerParams(has_side_effects=DATAFLOW))(buf)


def everyone_wait(name, buf, send, recv, after):
    def body(buf_ref, send_ref, recv_ref, after_ref, out_ref):
        x, y, c = _place()
        for r in range(1, N_DEV):
            _everyone_copy(buf_ref, r, x, y, c, send_ref, recv_ref, False).wait_send()
            _everyone_copy(buf_ref, r, x, y, c, send_ref, recv_ref, True).wait_recv()

    return pl.pallas_call(
        body, out_shape=jax.ShapeDtypeStruct(buf.shape, buf.dtype), in_specs=[ANY, SEM, SEM, ANY], out_specs=ANY,
        input_output_aliases={0: 0}, name=name,
        compiler_params=pltpu.CompilerParams(has_side_effects=DATAFLOW))(buf, send, recv, after)


def local_step(x, pos, tgt, small, d_in, get_w, put_g, first_dep=None, tick=lambda name, after: None):
    L, D = x.shape
    d_kv = N_KV_HEADS * HEAD_DIM
    d_ssm = small["d_skip"].shape[1]
    d_attn = d_in - 2 * d_kv - d_ssm
    big = {}
    G = d_ssm // SSM_GROUP
    N, P = SSM_STATE, SSM_GROUP

    half_dim = HEAD_DIM // 2
    inv_freq = ROPE_THETA ** (-jnp.arange(half_dim, dtype=f32) / half_dim)
    inv_freq = jnp.tile(inv_freq, LANES // half_dim).reshape(1, LANES)
    sink_b = jnp.broadcast_to(small["sinks"].reshape(-1, 1), (small["sinks"].size, LANES))

    spread = jnp.repeat(jnp.eye(G, dtype=f32), P, axis=0)
    logdt_b = jnp.broadcast_to(small["log_dt"].reshape(G, 1), (G, N))
    bt_re = small["b_re"].reshape(G, N, P).transpose(0, 2, 1).reshape(G * P, N)
    bt_im = small["b_im"].reshape(G, N, P).transpose(0, 2, 1).reshape(G * P, N)
    a_re, a_im = small["a_re"].reshape(G, N), small["a_im"].reshape(G, N)
    lam_re, lam_im, bbt_re, bbt_im = ssm_params(a_re, a_im, logdt_b, bt_re, bt_im, spread)
    bd_re = _block_diag(bbt_re.reshape(G, P, N), P, N).astype(bf16)
    bd_im = _block_diag(bbt_im.reshape(G, P, N), P, N).astype(bf16)
    c_re = small["c_re"].reshape(G, P, N).transpose(0, 2, 1)
    c_im = small["c_im"].reshape(G, P, N).transpose(0, 2, 1)
    cd_re = _block_diag(c_re, N, P).astype(bf16)
    cd_im = _block_diag(c_im, N, P).astype(bf16)
    lam_re_l, lam_im_l = _state_layout(lam_re), _state_layout(lam_im)

    def k1(i, nt, xt, g):
        return (rms_fwd(xt, g),)
    xn = rowwise("pre_mix_norm", k1, L, [full(x)], [small["g_pre_mix"]], [(D, bf16)], dep=first_dep)[0]
    big["w_in"] = get_w("w_in", xn)
    proj = mm_nn("proj_in", xn, big["w_in"])
    qr, kk, vv, u_bf = qkv_prep(proj, pos, inv_freq, d_attn, d_kv)
    attn = attn_fwd(qr, kk, vv, sink_b)
    u_cb = (d_attn + 2 * d_kv) // (d_ssm // 2)
    token = tick("attn", attn)
    y, z_bf, s_re, s_im = ssm_fwd(u_bf, proj, u_cb, bd_re, bd_im, cd_re, cd_im, lam_re_l, lam_im_l, small["d_skip"], dep=token)
    token = tick("ssm", z_bf)
    big["w_glu"] = get_w("w_glu", z_bf)
    gl = mm_nn("glu_proj", z_bf, big["w_glu"], dep=token)

    def k6(i, nt, at, yt, glt, bg, ga, gs):
        ssm = gelu(yt) * sigmoid(glt + bg)
        return (jnp.concatenate([rms_fwd(at, ga), rms_fwd(ssm, gs)], axis=1),)
    mixed = rowwise("mix_norms", k6, L, [full(attn), full(y), full(gl)],
                    [small["b_glu"], small["g_attn_out"], small["g_ssm_out"]], [(d_attn + d_ssm, bf16)])[0]
    big["w_o"] = get_w("w_o", mixed)
    mix = mm_nn("proj_out", mixed, big["w_o"])

    def k7(i, nt, xt, mt, gpm, gpf):
        h = xt + rms_fwd(mt, gpm)
        return h, rms_fwd(h, gpf)
    h, hn = rowwise("post_mix", k7, L, [full(x), full(mix)], [small["g_post_mix"], small["g_pre_ffn"]], [(D, f32), (D, bf16)])
    big["w_gate"] = get_w("w_gate", hn)
    big["w_up"] = get_w("w_up", hn)
    hid_dg, hid_du, hid = ffn_hidden(hn, big["w_gate"], big["w_up"])
    d_ff_dim = hid.shape[1]
    big["w_down"] = get_w("w_down", hid)
    ff = mm_nn("ffn_down", hid, big["w_down"], tk=d_ff_dim // 2)

    def k9(i, nt, ht, fft, tt, g):
        out = ht + rms_fwd(fft, g)
        err = out - tt
        per_row = jnp.mean(err * err, axis=-1, keepdims=True)
        loss = 0.5 * jnp.sum(per_row) * jnp.where(_lane((1, LANES)) == 0, 1.0, 0.0)
        d_out = err * (1.0 / D)
        d_ff, dg = rms_bwd(fft, g, d_out)
        return d_out, d_ff, dg, loss
    d_out, d_ff, dg_post_ffn, loss = rowwise("loss_head", k9, L, [full(h), full(ff), full(tgt)], [small["g_post_ffn"]],
                                             [(D, f32), (D, bf16)], reds=[D, LANES])

    d_gt, d_up = ffn_hidden_grad(d_ff, big["w_down"], hid_dg, hid_du)
    token = put_g("w_down", hid, d_ff, None)
    d_hn = mm_nt_pair("d_hn", d_gt, big["w_gate"], d_up, big["w_up"], dep=token)
    token = put_g("w_gate", hn, d_gt, None)
    token = put_g("w_up", hn, d_up, token)

    def k11(i, nt, ht, da, do, mt, gpf, gpm):
        dh_n, dg_pf = rms_bwd(ht, gpf, da)
        dh = do + dh_n
        d_mix, dg_pm = rms_bwd(mt, gpm, dh)
        return dh, d_mix, dg_pf, dg_pm
    dh, d_mix, dg_pre_ffn, dg_post_mix = rowwise("post_mix_grad", k11, L, [full(h), full(d_hn), full(d_out), full(mix)],
                                                 [small["g_pre_ffn"], small["g_post_mix"]], [(D, f32), (D, bf16)], reds=[D, D], dep=token)
    d_mixed = mm_nt("d_mixed", d_mix, big["w_o"])
    token = put_g("w_o", mixed, d_mix, None)

    def k12(i, nt, at, yt, glt, da_n, ds_n, bg, ga, gs):
        z = gelu(yt)
        sg = sigmoid(glt + bg)
        ssm = z * sg
        d_at, dga = rms_bwd(at, ga, da_n)
        d_ssm_t, dgs = rms_bwd(ssm, gs, ds_n)
        d_gl = d_ssm_t * z * sg * (1.0 - sg)
        return d_at, d_ssm_t * sg, d_gl, dga, dgs, colsum(d_gl)
    d_attn_o, dz1, d_gl, dg_attn, dg_ssm, db_glu = rowwise(
        "mix_norms_grad", k12, L, [full(attn), full(y), full(gl), (d_mixed, d_attn, 0, 0), (d_mixed, d_ssm, d_attn // d_ssm, 0)],
        [small["b_glu"], small["g_attn_out"], small["g_ssm_out"]], [(d_attn, f32), (d_ssm, f32), (d_ssm, bf16)],
        reds=[d_attn, d_ssm, d_ssm], dep=token)
    dz2 = mm_nt("d_glu_in", d_gl, big["w_glu"])
    token = put_g("w_glu", z_bf, d_gl, None)

    du, dbd_re, dbd_im, dcd_re, dcd_im, dlam_re_l, dlam_im_l, dd_skip = ssm_bwd(
        y, dz1, dz2, u_bf, proj, u_cb, s_re, s_im, bd_re, bd_im, cd_re, cd_im, lam_re_l, lam_im_l, small["d_skip"], dep=token)
    dq, dkk_c, dkk_p, dvv_c, dvv_p, dsink = attn_bwd(qr, kk, vv, sink_b, attn, d_attn_o)
    d_proj = qkv_grad(dq, dkk_c, dkk_p, dvv_c, dvv_p, du, pos, inv_freq)
    d_xn = mm_nt("d_xn", d_proj, big["w_in"])
    token = put_g("w_in", xn, d_proj, None)

    def k17(i, nt, xt, dxn, dht, g):
        dx, dg = rms_bwd(xt, g, dxn)
        return dht + dx, dg
    grad_x, dg_pre_mix = rowwise("pre_mix_grad", k17, L, [full(x), full(d_xn), full(dh)], [small["g_pre_mix"]],
                                 [(D, f32)], reds=[D], dep=token)

    gather = spread.T
    dbbt_re = dbd_re.reshape(G * P, LANES)[:, :N]
    dbbt_im = dbd_im.reshape(G * P, LANES)[:, :N]
    dc_re = dcd_re.reshape(G * P, LANES)[:, :N].reshape(G, P, N)
    dc_im = dcd_im.reshape(G * P, LANES)[:, :N].reshape(G, P, N)
    d_a_re, d_a_im, d_logdt, dbt_re, dbt_im = ssm_params_grad(
        a_re, a_im, logdt_b, bt_re, bt_im, spread, gather,
        _state_layout_inv(dlam_re_l, G, N), _state_layout_inv(dlam_im_l, G, N), dbbt_re, dbbt_im)
    q_per_kv = d_attn // HEAD_DIM // N_KV_HEADS
    small_grads = {
        "g_pre_mix": dg_pre_mix, "sinks": dsink[:, :q_per_kv, 0].reshape(1, -1),
        "a_re": d_a_re, "a_im": d_a_im, "log_dt": d_logdt.reshape(1, G),
        "b_re": dbt_re, "b_im": dbt_im,
        "c_re": dc_re, "c_im": dc_im,
        "d_skip": dd_skip, "b_glu": db_glu, "g_attn_out": dg_attn, "g_ssm_out": dg_ssm,
        "g_post_mix": dg_post_mix, "g_pre_ffn": dg_pre_ffn, "g_post_ffn": dg_post_ffn,
    }
    return loss, grad_x, small_grads


WEIGHTS = ['g_pre_mix', 'w_in', 'sinks', 'a_re', 'a_im', 'log_dt', 'b_re', 'b_im', 'c_re', 'c_im', 'd_skip', 'w_glu', 'b_glu',
           'g_attn_out', 'g_ssm_out', 'w_o', 'g_post_mix', 'g_pre_ffn', 'w_gate', 'w_up', 'w_down', 'g_post_ffn']
BIG = ['w_in', 'w_glu', 'w_o', 'w_gate', 'w_up', 'w_down']
COL_SHARDED = ['w_in', 'w_gate', 'w_up']
SMALL = [n for n in WEIGHTS if n not in BIG]
GATHER_GROUPS = [["w_in"], ["w_glu", "w_o"], ["w_gate", "w_up"], ["w_down"]]
REDUCE_GROUPS = [["w_down", "w_gate", "w_up"], ["w_o", "w_glu", "w_in"]]


PACK_ROWS = 256


def _pack(parts):
    flat = jnp.concatenate([p.reshape(-1) for p in parts])
    pad = (-flat.size) % (PACK_ROWS * LANES)
    return jnp.pad(flat, (0, pad)).reshape(-1, LANES)


TRANSPOSED_VIEW = ("b_re", "b_im")


def small_view(name, a):
    if name in TRANSPOSED_VIEW:
        a = a.transpose(0, 1, 3, 2)
    return a.reshape(-1, a.shape[-1])


def small_unview(name, p, shape):
    if name in TRANSPOSED_VIEW:
        return p.reshape(shape[0], shape[1], shape[3], shape[2]).transpose(0, 1, 3, 2)
    return p.reshape(shape)


def _unpack(packed, shapes):
    flat = packed.reshape(-1)
    out, off = [], 0
    for s in shapes:
        n = int(np.prod(s))
        out.append(flat[off:off + n].reshape(s))
        off += n
    return out


def kernel(x, positions, g_pre_mix, w_in, sinks, a_re, a_im, log_dt, b_re, b_im, c_re, c_im, d_skip, w_glu, b_glu, g_attn_out, g_ssm_out, w_o, g_post_mix, g_pre_ffn, w_gate, w_up, w_down, g_post_ffn, loss_target, m_g_pre_mix, m_w_in, m_sinks, m_a_re, m_a_im, m_log_dt, m_b_re, m_b_im, m_c_re, m_c_im, m_d_skip, m_w_glu, m_b_glu, m_g_attn_out, m_g_ssm_out, m_w_o, m_g_post_mix, m_g_pre_ffn, m_w_gate, m_w_up, m_w_down, m_g_post_ffn, v_g_pre_mix, v_w_in, v_sinks, v_a_re, v_a_im, v_log_dt, v_b_re, v_b_im, v_c_re, v_c_im, v_d_skip, v_w_glu, v_b_glu, v_g_attn_out, v_g_ssm_out, v_w_o, v_g_post_mix, v_g_pre_ffn, v_w_gate, v_w_up, v_w_down, v_g_post_ffn):
    args = dict(locals())
    w = {n: args[n] for n in WEIGHTS}
    m = {n: args["m_" + n] for n in WEIGHTS}
    v = {n: args["v_" + n] for n in WEIGHTS}
    L, D = x.shape[1], x.shape[2]

    ax, ay, ac = _place()
    mine_arr = (2 * ax + ay).astype(jnp.int32).reshape(1)
    c_arr = ac.astype(jnp.int32).reshape(1)

    me_arr = (4 * ax + 2 * ay + ac).astype(jnp.int32).reshape(1)

    bufs = {"w_in": into_slot("cast_w_in", w["w_in"][0], mine_arr, N_CHIPS, bf16)}
    (first_sems,), (bufs["w_in"],), token = gather_start("gather_start_in", [bufs["w_in"]], [[0]], mine_arr, neighbours_only=(0,))
    sems, relays, ready = {"w_in": first_sems}, {}, set()

    def relay(n, more, after):
        r_send, r_recv, more_sems, (bufs[n],), started, tok = gather_relay(
            "gather_relay_" + n, [bufs[n]], [sems[n]], [bufs[k] for k in more], after)
        sems.update({k: more_sems for k in more})
        bufs.update(zip(more, started))
        relays[n] = (r_send, r_recv)
        return tok

    for n in BIG[1:]:
        bufs[n] = into_slot("cast_" + n, w[n][0], mine_arr, N_CHIPS, bf16, dep=token)
    token = relay("w_in", [], bufs[BIG[-1]])
    first = ["w_gate", "w_glu", "w_o", "w_up"]
    (sems["w_gate"], sems["w_glu"], sems["w_up"]), started, token = gather_start(
        "gather_start_rest", [bufs[n] for n in first], [[0], [1, 2], [3]], token, neighbours_only=(0, 3))
    bufs.update(zip(first, started))

    def tick(name, after):
        return relay("w_gate", [], after) if name == "attn" else relay("w_up", ["w_down"], after)

    def get_w(n, after):
        if n not in ready:
            members = [g for g in GATHER_GROUPS if n in g][0]
            if members[0] in relays:
                landed = [gather_wait_relay("gather_wait_" + k, [bufs[k]], *relays[k], after)[0] for k in members]
                which = (2,)
            else:
                landed = gather_wait("gather_wait_" + members[0], [bufs[k] for k in members], *sems[members[0]], after)
                which = (0, 1, 2)
            bufs.update(zip(members, gather_forward("gather_forward_" + members[0], landed, which)))
            ready.update(members)
        g = bufs[n]
        return g if n in COL_SHARDED else g.reshape(g.shape[0] * g.shape[1], g.shape[2])

    swaps, operands, inflight = {}, {}, []
    other_arr = 1 - c_arr

    def put_g(n, a, b, dep):
        halves = (a.shape[1] // N_CHIPS // 2) % LANES == 0 if n not in COL_SHARDED else True
        if halves:
            operands[n] = (a, b)
            sent = mm_tn_half("d%s_sent" % n, a, b, n in COL_SHARDED, other_arr, dep=dep)
            swaps[n] = sibling_start("swap_start_" + n, [sent], False)
        else:
            g = weight_grad(n, a, b, dep)
            swaps[n] = sibling_start("swap_start_" + n, [g.reshape(N_CHIPS, g.shape[0] // N_CHIPS, g.shape[1])], True)
        for gi, members in enumerate(REDUCE_GROUPS):
            if n == members[-1]:
                last = swaps[n][4]
                pair = []
                for k in members:
                    send, recv, srcs, lands, _ = swaps[k]
                    (src,), (got,) = sibling_wait("swap_wait_" + k, srcs, lands, send, recv, k not in operands, last)
                    if k in operands:
                        pair.append(mm_tn_half("d%s_kept" % k, *operands[k], k in COL_SHARDED, c_arr, addend=got))
                    else:
                        pair.append(pair_sum("pair_sum_" + k, src, got, c_arr))
                send, recv, parts, lands, tok = scatter_start("scatter_start_%d" % gi, pair)
                inflight.append((members, send, recv, parts, lands))
                return tok
        return swaps[n][4]

    small = {n: w[n].reshape(1, -1) for n in SMALL}
    pos = positions.reshape(L, 1).astype(f32)
    d_in = N_CHIPS * w["w_in"].shape[2]
    loss, grad_x, small_grads = local_step(x[0], pos, loss_target[0], small, d_in, get_w, put_g, first_dep=token, tick=tick)

    shapes = [w[n].shape for n in SMALL]
    blocks = into_slot("small_block", _pack([small_grads[n] for n in SMALL] + [loss]), me_arr, N_DEV, f32)
    small_send, small_recv, blocks, after = everyone_start("small_start", blocks)

    grads, delta, new_m, new_v = {}, {}, {}, {}
    for gi, (members, send, recv, parts, lands) in enumerate(inflight):
        parts, landed = scatter_wait("scatter_wait_%d" % gi, parts, lands, send, recv, after)
        joins, dep = [], None
        for k, p, t in zip(members, parts, landed):
            joins.append(sibling_start("join_start_" + k, [chip_sum("chip_sum_" + k, p, t, mine_arr, dep=dep)], False))
            dep = after = joins[-1][4]
        for n, (send, recv, srcs, lands, _) in zip(members, joins):
            (own,), (sib,) = sibling_wait("join_wait_" + n, srcs, lands, send, recv, False, after)
            g_, d_, m_, v_ = adamw_halves("adamw_" + n, w[n][0], own, sib, m[n][0], v[n][0], c_arr)
            grads[n], delta[n], new_m[n], new_v[n] = g_[None], d_[None], m_[None], v_[None]
            after = v_
    blocks = everyone_wait("small_wait", blocks, small_send, small_recv, after)
    small_sum = sum_slots("small_sum", blocks)
    *small_g, loss = _unpack(small_sum, [small_view(n, w[n]).shape for n in SMALL] + [loss.shape])
    loss = loss[0, 0]
    outs = adamw_many("adamw_small", [small_view(n, w[n]) for n in SMALL], small_g,
                      [small_view(n, m[n]) for n in SMALL], [small_view(n, v[n]) for n in SMALL])
    for t, parts in zip((grads, delta, new_m, new_v), (small_g,) + tuple(outs)):
        t.update({n: small_unview(n, p, w[n].shape) for n, p in zip(SMALL, parts)})

    return (loss, grad_x[None], *[grads[n] for n in WEIGHTS], *[delta[n] for n in WEIGHTS],
            *[new_m[n] for n in WEIGHTS], *[new_v[n] for n in WEIGHTS])
```

```python
import functools
import math

import jax
import jax.numpy as jnp
import numpy as np
from jax import lax
from jax.experimental import pallas as pl
from jax.experimental.pallas import tpu as pltpu

f32 = jnp.float32
bf16 = jnp.bfloat16
HIGHEST = lax.Precision.HIGHEST
MESH = pl.DeviceIdType.MESH

HEAD_DIM = 64
N_KV_HEADS = 4
ATTN_BLOCK = 128
ROPE_THETA = 10000.0
SSM_GROUP = 16
SSM_STATE = 64
RMS_EPS = 1e-6
LANES = 128
SUBLANES = 8
VMEM_LIMIT = 52 * 1024 * 1024
N_CHIPS = 4
N_DEV = 8
NEG = -1e30

ADAM_LR, ADAM_B1, ADAM_B2, ADAM_EPS, ADAM_WD, ADAM_STEP = 0.001, 0.9, 0.999, 1e-08, 0.01, 10

NN = (((1,), (0,)), ((), ()))
NT = (((1,), (1,)), ((), ()))
TN = (((0,), (0,)), ((), ()))


def _params(*sem):
    return pltpu.CompilerParams(dimension_semantics=sem or None, vmem_limit_bytes=VMEM_LIMIT)


def _dot(a, b, dims=NN):
    return lax.dot_general(a, b, dims, preferred_element_type=f32)


def _pick(dim, pref):
    t = min(dim, pref)
    while dim % t:
        t -= LANES
    assert t > 0, (dim, pref)
    return t


ANY = pl.BlockSpec(memory_space=pl.ANY)


def _with_dep(in_specs, operands, dep):
    if dep is None:
        return list(in_specs), list(operands), 0
    return list(in_specs) + [ANY], list(operands) + [dep], 1


def _mm_call(name, grid, in_specs, out_spec, out_shape, acc_shape, dims, operands, dep=None):
    nk = grid[2]
    in_specs, operands, n_dep = _with_dep(in_specs, operands, dep)

    def body_one(a_ref, b_ref, *rest):
        o_ref = rest[n_dep]
        o_ref[...] = _dot(a_ref[...], b_ref[...], dims).astype(o_ref.dtype)

    def body(a_ref, b_ref, *rest):
        o_ref, acc_ref = rest[n_dep], rest[n_dep + 1]
        k = pl.program_id(2)

        @pl.when(k == 0)
        def _():
            acc_ref[...] = _dot(a_ref[...], b_ref[...], dims)

        @pl.when((k > 0) & (k < nk - 1))
        def _():
            acc_ref[...] += _dot(a_ref[...], b_ref[...], dims)

        @pl.when(k == nk - 1)
        def _():
            o_ref[...] = (acc_ref[...] + _dot(a_ref[...], b_ref[...], dims)).astype(o_ref.dtype)

    return pl.pallas_call(
        body_one if nk == 1 else body, out_shape=out_shape, grid=grid, in_specs=in_specs, out_specs=out_spec,
        scratch_shapes=[] if nk == 1 else [pltpu.VMEM(acc_shape, f32)], name=name,
        compiler_params=_params("parallel", "parallel", "arbitrary"))(*operands)


def mm_nt_pair(name, a1, b1, a2, b2, tm=1024, tk=1024, dep=None):
    M = a1.shape[0]
    S, K, n = b1.shape
    tm, tko = _pick(M, tm), _pick(K, tk)
    nk = 2 * S

    def body(a1_ref, b1_ref, a2_ref, b2_ref, *rest):
        o_ref, acc_ref = rest[-2], rest[-1]
        k = pl.program_id(2)

        @pl.when(k == 0)
        def _():
            acc_ref[...] = _dot(a1_ref[...], b1_ref[...], NT)

        @pl.when((k > 0) & (k < S))
        def _():
            acc_ref[...] += _dot(a1_ref[...], b1_ref[...], NT)

        @pl.when((k >= S) & (k < nk - 1))
        def _():
            acc_ref[...] += _dot(a2_ref[...], b2_ref[...], NT)

        @pl.when(k == nk - 1)
        def _():
            o_ref[...] = acc_ref[...] + _dot(a2_ref[...], b2_ref[...], NT)

    first = lambda k: jnp.minimum(k, S - 1)
    second = lambda k: jnp.maximum(k - S, 0)
    in_specs = [pl.BlockSpec((tm, n), lambda i, j, k: (i, first(k))), pl.BlockSpec((None, tko, n), lambda i, j, k: (first(k), j, 0)),
                pl.BlockSpec((tm, n), lambda i, j, k: (i, second(k))), pl.BlockSpec((None, tko, n), lambda i, j, k: (second(k), j, 0))]
    in_specs, operands, _ = _with_dep(in_specs, (a1, b1, a2, b2), dep)
    return pl.pallas_call(
        body, out_shape=jax.ShapeDtypeStruct((M, K), f32), grid=(M // tm, K // tko, nk), in_specs=in_specs,
        out_specs=pl.BlockSpec((tm, tko), lambda i, j, k: (i, j)), scratch_shapes=[pltpu.VMEM((tm, tko), f32)], name=name,
        compiler_params=_params("parallel", "parallel", "arbitrary"))(*operands)


def mm_nn(name, a, b, out_dtype=f32, tm=1024, tn=1024, tk=2048, dep=None):
    M, K = a.shape
    tm, tk = _pick(M, tm), _pick(K, tk)
    if b.ndim == 3:
        S, _, n = b.shape
        tn = _pick(n, 2048)
        per = n // tn
        b_spec = pl.BlockSpec((None, tk, tn), lambda i, j, k: (j // per, k, j % per))
        N = S * n
    else:
        N = b.shape[1]
        tn = _pick(N, tn)
        b_spec = pl.BlockSpec((tk, tn), lambda i, j, k: (k, j))
    grid = (M // tm, N // tn, K // tk)
    return _mm_call(name, grid, [pl.BlockSpec((tm, tk), lambda i, j, k: (i, k)), b_spec],
                    pl.BlockSpec((tm, tn), lambda i, j, k: (i, j)), jax.ShapeDtypeStruct((M, N), out_dtype),
                    (tm, tn), NN, (a, b), dep)


def mm_nt(name, a, b, out_dtype=f32, tm=1024, tn=2048, tk=1024, dep=None):
    M, N = a.shape
    tm = _pick(M, tm)
    if b.ndim == 3:
        S, K, n = b.shape
        tr = _pick(n, 2048)
        per = n // tr
        tko = _pick(K, tk)
        b_spec = pl.BlockSpec((None, tko, tr), lambda i, j, k: (k // per, j, k % per))
    else:
        K = b.shape[0]
        tr = _pick(N, tn)
        tko = _pick(K, tk)
        b_spec = pl.BlockSpec((tko, tr), lambda i, j, k: (j, k))
    grid = (M // tm, K // tko, N // tr)
    return _mm_call(name, grid, [pl.BlockSpec((tm, tr), lambda i, j, k: (i, k)), b_spec],
                    pl.BlockSpec((tm, tko), lambda i, j, k: (i, j)), jax.ShapeDtypeStruct((M, K), out_dtype),
                    (tm, tko), NT, (a, b), dep)


def mm_tn(name, a, b, shards=None, out_dtype=f32, tm=1024, tn=1024, tl=2048, dep=None):
    L, K = a.shape
    N = b.shape[1]
    tl, tko = _pick(L, tl), _pick(K, tm)
    if shards:
        n = N // shards
        tn = _pick(n, 2048)
        per = n // tn
        o_spec = pl.BlockSpec((None, tko, tn), lambda i, j, k: (j // per, i, j % per))
        o_shape = jax.ShapeDtypeStruct((shards, K, n), out_dtype)
    else:
        tn = _pick(N, tn)
        o_spec = pl.BlockSpec((tko, tn), lambda i, j, k: (i, j))
        o_shape = jax.ShapeDtypeStruct((K, N), out_dtype)
    grid = (K // tko, N // tn, L // tl)
    return _mm_call(name, grid, [pl.BlockSpec((tl, tko), lambda i, j, k: (k, i)),
                                 pl.BlockSpec((tl, tn), lambda i, j, k: (k, j))],
                    o_spec, o_shape, (tko, tn), TN, (a, b), dep)


def halves_by_columns(k_rows, col_sharded):
    return not col_sharded and (k_rows // N_CHIPS // 2) % LANES != 0


def mm_tn_half(name, a, b, col_sharded, half_arr, addend=None, dep=None, tn=2048):
    L, K = a.shape
    N = b.shape[1]
    S = N_CHIPS
    by_columns = halves_by_columns(K, col_sharded)
    if col_sharded:
        h, cols = K // 2, N // S
    else:
        h, cols = (K // S, N // 2) if by_columns else (K // S // 2, N)
    tko, tn = _pick(h, 2048 if by_columns else 1024), _pick(cols, 1024 if by_columns else tn)
    ni, nj = h // tko, cols // tn
    if col_sharded:
        a_map = lambda s, i, j, hf: (0, hf[0] * ni + i)
        b_map = lambda s, i, j, hf: (0, s * nj + j)
    elif by_columns:
        a_map = lambda s, i, j, hf: (0, s * ni + i)
        b_map = lambda s, i, j, hf: (0, hf[0] * nj + j)
    else:
        a_map = lambda s, i, j, hf: (0, (2 * s + hf[0]) * ni + i)
        b_map = lambda s, i, j, hf: (0, j)
    o_spec = pl.BlockSpec((None, tko, tn), lambda s, i, j, hf: (s, i, j))
    n_add = 0 if addend is None else 1
    n_dep = 0 if dep is None else 1

    def body(hf_ref, a_ref, b_ref, *rest):
        o_ref = rest[n_add + n_dep]
        acc = _dot(a_ref[...], b_ref[...], TN)
        if n_add:
            acc = acc + rest[0][...].astype(f32)
        o_ref[...] = acc.astype(o_ref.dtype)

    in_specs = [pl.BlockSpec((L, tko), a_map), pl.BlockSpec((L, tn), b_map)] + [o_spec] * n_add
    in_specs, operands, _ = _with_dep(in_specs, [a, b] + ([addend] if n_add else []), dep)
    grid_spec = pltpu.PrefetchScalarGridSpec(num_scalar_prefetch=1, grid=(S, ni, nj), in_specs=in_specs, out_specs=o_spec)
    return pl.pallas_call(body, out_shape=jax.ShapeDtypeStruct((S, h, cols), bf16), grid_spec=grid_spec, name=name,
                          compiler_params=_params("parallel", "parallel", "parallel"))(half_arr, *operands)


def weight_grad(name, a, b, dep=None):
    if name in COL_SHARDED:
        return mm_tn("d" + name, a, b, shards=N_CHIPS, out_dtype=bf16, dep=dep)
    return mm_tn("d" + name, a, b, out_dtype=bf16, tm=a.shape[1] // N_CHIPS, dep=dep)


def ffn_hidden(hn, w_gate, w_up, tm=512):
    M, K = hn.shape
    S, _, n = w_gate.shape
    tm = _pick(M, tm)

    def body(a_ref, g_ref, u_ref, dg_ref, du_ref, hid_ref):
        a = a_ref[...]
        g = _dot(a, g_ref[...])
        u = _dot(a, u_ref[...])
        sg = sigmoid(g)
        act = g * sg
        dg_ref[...] = (u * (sg * (1.0 + g * (1.0 - sg)))).astype(bf16)
        du_ref[...] = act.astype(bf16)
        hid_ref[...] = (act * u).astype(bf16)

    w_spec = pl.BlockSpec((None, K, n), lambda s, i: (s, 0, 0))
    o_spec = pl.BlockSpec((tm, n), lambda s, i: (i, s))
    o = jax.ShapeDtypeStruct((M, S * n), bf16)
    return pl.pallas_call(
        body, out_shape=[o, o, o], grid=(S, M // tm), in_specs=[pl.BlockSpec((tm, K), lambda s, i: (i, 0)), w_spec, w_spec],
        out_specs=[o_spec, o_spec, o_spec], name="ffn_hidden", compiler_params=_params("parallel", "parallel"))(hn, w_gate, w_up)


def ffn_hidden_grad(d_ff, w_down, hid_dg, hid_du, tm=512):
    M, D = d_ff.shape
    F = w_down.shape[0]
    n = _pick(F // N_CHIPS, 2048)
    tm = _pick(M, tm)

    def body(a_ref, b_ref, pg_ref, pu_ref, dg_ref, du_ref):
        dh = _dot(a_ref[...], b_ref[...], NT)
        dg_ref[...] = (dh * pg_ref[...].astype(f32)).astype(bf16)
        du_ref[...] = (dh * pu_ref[...].astype(f32)).astype(bf16)

    t_spec = pl.BlockSpec((tm, n), lambda j, i: (i, j))
    o = jax.ShapeDtypeStruct((M, F), bf16)
    return pl.pallas_call(
        body, out_shape=[o, o], grid=(F // n, M // tm),
        in_specs=[pl.BlockSpec((tm, D), lambda j, i: (i, 0)), pl.BlockSpec((n, D), lambda j, i: (j, 0)), t_spec, t_spec],
        out_specs=[t_spec, t_spec], name="ffn_hidden_grad", compiler_params=_params("parallel", "parallel"))(d_ff, w_down, hid_dg, hid_du)


def rowwise(name, fn, L, rows, bcast, outs, reds=(), tr=256, dep=None):
    tr = min(tr, L)
    nt = L // tr
    n_rows, n_b, n_o = len(rows), len(bcast), len(outs)
    n_dep = 0 if dep is None else 1

    def body(*refs):
        i = pl.program_id(0)
        ins = [r[...] for r in refs[:n_rows + n_b]]
        res = fn(i, nt, *ins)
        o_refs = refs[n_rows + n_b + n_dep:]
        for k in range(n_o):
            o_refs[k][...] = res[k].astype(o_refs[k].dtype)
        if reds:
            @pl.when(i == 0)
            def _():
                for k in range(len(reds)):
                    o_refs[n_o + k][...] = jnp.zeros_like(o_refs[n_o + k])
            for k in range(len(reds)):
                o_refs[n_o + k][...] += res[n_o + k]

    def row_spec(width, cb, shift):
        if shift:
            return pl.BlockSpec((tr, width), lambda i: (jnp.minimum(i + shift, nt - 1), cb))
        return pl.BlockSpec((tr, width), lambda i: (i, cb))

    in_specs = [row_spec(w, cb, sh) for (_, w, cb, sh) in rows]
    in_specs += [pl.BlockSpec(b.shape, lambda i: (0, 0)) for b in bcast]
    out_specs = [pl.BlockSpec((tr, w), lambda i: (i, 0)) for (w, _) in outs]
    out_specs += [pl.BlockSpec((1, w), lambda i: (0, 0)) for w in reds]
    out_shape = [jax.ShapeDtypeStruct((L, w), dt) for (w, dt) in outs]
    out_shape += [jax.ShapeDtypeStruct((1, w), f32) for w in reds]
    in_specs, operands, _ = _with_dep(in_specs, [r[0] for r in rows] + list(bcast), dep)
    return pl.pallas_call(
        body, out_shape=out_shape, grid=(nt,), in_specs=in_specs, out_specs=out_specs, name=name,
        compiler_params=_params("arbitrary"))(*operands)


def full(a):
    return (a, a.shape[1], 0, 0)


def colsum(v):
    return jnp.sum(v, axis=0, keepdims=True)


def rms_fwd(x, g):
    r = lax.rsqrt(jnp.mean(x * x, axis=-1, keepdims=True) + RMS_EPS)
    return x * r * g


def rms_bwd(x, g, dy):
    r = lax.rsqrt(jnp.mean(x * x, axis=-1, keepdims=True) + RMS_EPS)
    xh = x * r
    dyg = dy * g
    dx = r * (dyg - xh * jnp.mean(dyg * xh, axis=-1, keepdims=True))
    return dx, colsum(dy * xh)


GELU_C = math.sqrt(2.0 / math.pi)


def gelu(y):
    return y * (0.5 * (1.0 + jnp.tanh(GELU_C * (y + 0.044715 * (y * y * y)))))


def gelu_grad(y):
    t = jnp.tanh(GELU_C * (y + 0.044715 * (y * y * y)))
    return 0.5 * (1.0 + t) + 0.5 * y * (1.0 - t * t) * (GELU_C * (1.0 + 3 * 0.044715 * (y * y)))


def sigmoid(v):
    return 1.0 / (1.0 + jnp.exp(-v))


def _lane(shape):
    return lax.broadcasted_iota(jnp.int32, shape, 1)


def _rot_chunk(t, cos, sin_signed):
    first = (_lane(t.shape) % HEAD_DIM) < (HEAD_DIM // 2)
    partner = jnp.where(first, pltpu.roll(t, LANES - HEAD_DIM // 2, 1), pltpu.roll(t, HEAD_DIM // 2, 1))
    return t * cos + partner * sin_signed


def _cos_sin(pos, inv_freq, inverse):
    ang = pos * inv_freq
    cos, sin = jnp.cos(ang), jnp.sin(ang)
    first = (_lane(ang.shape) % HEAD_DIM) < (HEAD_DIM // 2)
    sign = jnp.where(first, -1.0, 1.0) * (-1.0 if inverse else 1.0)
    return cos, sin * sign


def _dup_head(chunk, odd):
    low = _lane(chunk.shape) < HEAD_DIM
    x = jnp.where(low != odd, chunk, 0.0)
    return x + pltpu.roll(x, HEAD_DIM, 1)


def _chunks(v):
    return [v[:, LANES * c:LANES * (c + 1)] for c in range(v.shape[1] // LANES)]


def qkv_prep(proj, pos, inv_freq, d_attn, d_kv):
    L = proj.shape[0]
    d_ssm = proj.shape[1] - d_attn - 2 * d_kv
    half = d_ssm // 2
    scale = 1.0 / math.sqrt(HEAD_DIM)

    def fn(i, nt, q, k, v, u0, u1, p, invf):
        cos, sin = _cos_sin(p, invf, False)
        qr = jnp.concatenate([_rot_chunk(c, cos, sin) for c in _chunks(q)], axis=1) * scale
        kr = [_rot_chunk(c, cos, sin) for c in _chunks(k)]
        kk = jnp.concatenate([_dup_head(c, odd) for c in kr for odd in (False, True)], axis=1)
        vv = jnp.concatenate([_dup_head(c, odd) for c in _chunks(v) for odd in (False, True)], axis=1)
        return qr, kk, vv, jnp.concatenate([u0, u1], axis=1)

    u_cb = (d_attn + 2 * d_kv) // half
    return rowwise("qkv_prep", fn, L,
                   [(proj, d_attn, 0, 0), (proj, d_kv, d_attn // d_kv, 0), (proj, d_kv, d_attn // d_kv + 1, 0),
                    (proj, half, u_cb, 0), (proj, half, u_cb + 1, 0), full(pos)],
                   [inv_freq], [(d_attn, bf16), (2 * d_kv, bf16), (2 * d_kv, bf16), (d_ssm, bf16)])


def qkv_grad(dq, dkk_c, dkk_p, dvv_c, dvv_p, du, pos, inv_freq):
    L, d_attn = dq.shape
    d_kv = dkk_c.shape[1] // 2
    scale = 1.0 / math.sqrt(HEAD_DIM)

    def fold(cur, prev, i, nt):
        t = cur + jnp.where(i < nt - 1, prev, 0.0)
        out = []
        for c in range(t.shape[1] // (2 * LANES)):
            even, odd = t[:, 2 * c * LANES:(2 * c + 1) * LANES], t[:, (2 * c + 1) * LANES:(2 * c + 2) * LANES]
            even, odd = even + pltpu.roll(even, HEAD_DIM, 1), odd + pltpu.roll(odd, HEAD_DIM, 1)
            out.append(jnp.where(_lane(even.shape) < HEAD_DIM, even, odd))
        return out

    def fn(i, nt, dq_t, kc, kp, vc, vp, du_t, p, invf):
        cos, sin = _cos_sin(p, invf, True)
        dq_o = jnp.concatenate([_rot_chunk(c, cos, sin) for c in _chunks(dq_t)], axis=1) * scale
        dk_o = jnp.concatenate([_rot_chunk(c, cos, sin) for c in fold(kc, kp, i, nt)], axis=1)
        dv_o = jnp.concatenate(fold(vc, vp, i, nt), axis=1)
        return (jnp.concatenate([dq_o, dk_o, dv_o, du_t], axis=1),)

    return rowwise("qkv_grad", fn, L,
                   [full(dq), full(dkk_c), (dkk_p, 2 * d_kv, 0, 1), full(dvv_c), (dvv_p, 2 * d_kv, 0, 1), full(du), full(pos)],
                   [inv_freq], [(d_attn + 2 * d_kv + du.shape[1], bf16)], tr=ATTN_BLOCK)[0]


def _attn_specs(L):
    nb = L // ATTN_BLOCK
    B = ATTN_BLOCK
    q_spec = lambda width: pl.BlockSpec((B, width), lambda n: (n, 0))
    prev = lambda width: pl.BlockSpec((B, width), lambda n: (jnp.maximum(n - 1, 0), 0))
    return nb, q_spec, prev


def _attn_mask(n):
    B = ATTN_BLOCK
    row = lax.broadcasted_iota(jnp.int32, (B, 2 * B), 0)
    col = lax.broadcasted_iota(jnp.int32, (B, 2 * B), 1)
    return ((col < B) & (col > row) & (n > 0)) | ((col >= B) & (row >= col - B))


def _attn_probs(qm, kcat, sink, mask):
    s = jnp.where(mask, _dot(qm, kcat, NT), NEG)
    m = jnp.maximum(jnp.max(s, axis=1, keepdims=True), sink)
    p, ps = jnp.exp(s - m), jnp.exp(sink - m)
    inv = 1.0 / (jnp.sum(p, axis=1, keepdims=True) + ps)
    return p, inv, ps


def _attn_heads(q_ref, s_ref, h, q_per_kv):
    low = _lane((ATTN_BLOCK, LANES)) < HEAD_DIM
    heads = []
    for pr in range(h * q_per_kv // 2, (h + 1) * q_per_kv // 2):
        q2 = q_ref[:, LANES * pr:LANES * (pr + 1)]
        for odd in (False, True):
            mine = low != odd
            sink = jnp.max(s_ref[2 * pr + int(odd):2 * pr + int(odd) + 1, :], axis=1, keepdims=True)
            heads.append((pr, mine, jnp.where(mine, q2, jnp.zeros_like(q2)), sink))
    return low, heads


def _kv_block(prev_ref, cur_ref, h):
    return jnp.concatenate([prev_ref[:, LANES * h:LANES * (h + 1)], cur_ref[:, LANES * h:LANES * (h + 1)]], axis=0)


def attn_fwd(qr, kk, vv, sink_b):
    L, d_attn = qr.shape
    nb, q_spec, prev = _attn_specs(L)
    d_kk = kk.shape[1]
    n_kv = d_kk // LANES
    q_per_kv = d_attn // HEAD_DIM // n_kv

    def body(q_ref, kc_ref, kp_ref, vc_ref, vp_ref, s_ref, o_ref):
        mask = _attn_mask(pl.program_id(0))
        for h in range(n_kv):
            kcat, vcat = _kv_block(kp_ref, kc_ref, h), _kv_block(vp_ref, vc_ref, h)
            low, heads = _attn_heads(q_ref, s_ref, h, q_per_kv)
            probs = [_attn_probs(qm, kcat, sink, mask) for (_, _, qm, sink) in heads]
            outs = [_dot(p.astype(bf16), vcat) * inv for (p, inv, _) in probs]
            for i in range(0, len(heads), 2):
                pr = heads[i][0]
                o_ref[:, LANES * pr:LANES * (pr + 1)] = jnp.where(low, outs[i], outs[i + 1])

    return pl.pallas_call(
        body, out_shape=jax.ShapeDtypeStruct((L, d_attn), f32), grid=(nb,),
        in_specs=[q_spec(d_attn), q_spec(d_kk), prev(d_kk), q_spec(d_kk), prev(d_kk), pl.BlockSpec(sink_b.shape, lambda n: (0, 0))],
        out_specs=q_spec(d_attn), name="attn_fwd", compiler_params=_params("arbitrary"))(qr, kk, kk, vv, vv, sink_b)


def attn_bwd(qr, kk, vv, sink_b, attn, d_attn_out):
    L, d_attn = qr.shape
    nb, q_spec, prev = _attn_specs(L)
    d_kk = kk.shape[1]
    n_kv = d_kk // LANES
    q_per_kv = d_attn // HEAD_DIM // n_kv

    def body(q_ref, kc_ref, kp_ref, vc_ref, vp_ref, s_ref, o_ref, do_ref, dq_ref, dkc_ref, dkp_ref, dvc_ref, dvp_ref, ds_ref):
        n = pl.program_id(0)
        B = ATTN_BLOCK
        mask = _attn_mask(n)
        srow = lax.broadcasted_iota(jnp.int32, (SUBLANES, LANES), 0)

        @pl.when(n == 0)
        def _():
            ds_ref[...] = jnp.zeros_like(ds_ref)

        for h in range(n_kv):
            kcat, vcat = _kv_block(kp_ref, kc_ref, h), _kv_block(vp_ref, vc_ref, h)
            low, heads = _attn_heads(q_ref, s_ref, h, q_per_kv)
            probs = [_attn_probs(qm, kcat, sink, mask) for (_, _, qm, sink) in heads]
            dk = jnp.zeros((2 * B, LANES), f32)
            dv = dk
            dsink = jnp.zeros((SUBLANES, LANES), f32)
            dqs = []
            for i, ((pr, mine, qm, _), (p, inv, ps)) in enumerate(zip(heads, probs)):
                do2 = do_ref[:, LANES * pr:LANES * (pr + 1)]
                delta = jnp.sum(jnp.where(mine, do2 * o_ref[:, LANES * pr:LANES * (pr + 1)], 0.0), axis=1, keepdims=True)
                dob = jnp.where(mine, do2, 0.0).astype(bf16)
                p = p * inv
                ds = (p * (_dot(dob, vcat, NT) - delta)).astype(bf16)
                dqs.append(_dot(ds, kcat))
                dk = dk + _dot(ds, qm, TN)
                dv = dv + _dot(p.astype(bf16), dob, TN)
                dsink = dsink + jnp.where(srow == i, -jnp.sum(ps * inv * delta), 0.0)
            for i in range(0, len(heads), 2):
                pr = heads[i][0]
                dq_ref[:, LANES * pr:LANES * (pr + 1)] = jnp.where(low, dqs[i], dqs[i + 1])
            cols = slice(LANES * h, LANES * (h + 1))
            dkp_ref[:, cols] = dk[:B]
            dkc_ref[:, cols] = dk[B:]
            dvp_ref[:, cols] = dv[:B]
            dvc_ref[:, cols] = dv[B:]
            ds_ref[h] += dsink

    kv_shape = jax.ShapeDtypeStruct(kk.shape, f32)
    ds_shape = (n_kv, SUBLANES, LANES)
    return pl.pallas_call(
        body,
        out_shape=[jax.ShapeDtypeStruct((L, d_attn), f32), kv_shape, kv_shape, kv_shape, kv_shape, jax.ShapeDtypeStruct(ds_shape, f32)],
        grid=(nb,),
        in_specs=[q_spec(d_attn), q_spec(d_kk), prev(d_kk), q_spec(d_kk), prev(d_kk), pl.BlockSpec(sink_b.shape, lambda n: (0, 0)),
                  q_spec(d_attn), q_spec(d_attn)],
        out_specs=[q_spec(d_attn)] + [q_spec(d_kk)] * 4 + [pl.BlockSpec(ds_shape, lambda n: (0, 0, 0))],
        name="attn_bwd", compiler_params=_params("arbitrary"))(qr, kk, kk, vv, vv, sink_b, attn, d_attn_out)


SSM_T = 128
NQ = SUBLANES * SSM_STATE // LANES
NJ = SUBLANES


SCAN_UNROLL = 8


def _unrolled(step):
    def body(k, carry):
        for u in range(SCAN_UNROLL):
            carry = step(k * SCAN_UNROLL + u, carry)
        return carry
    return body


def _strided_put(ref, j, val):
    for q in range(NQ):
        ref.at[q][pl.ds(j, SSM_T, stride=NJ), :] = val[:, LANES * q:LANES * (q + 1)]


def _strided_get(ref, j):
    return jnp.concatenate([ref.at[q][pl.ds(j, SSM_T, stride=NJ), :] for q in range(NQ)], axis=1)


def _ssm_specs(L, rev):
    nt = L // SSM_T
    idx = (lambda i: nt - 1 - i) if rev else (lambda i: i)
    row = lambda w, cb=0: pl.BlockSpec((SSM_T, w), lambda i: (idx(i), cb))
    state = pl.BlockSpec((NQ, SSM_T * NJ, LANES), lambda i: (0, idx(i), 0))
    whole = lambda a: pl.BlockSpec(a.shape, lambda i: (0,) * a.ndim)
    return nt, row, state, whole


def ssm_fwd(u_bf, proj, u_cb, bd_re, bd_im, cd_re, cd_im, lam_re, lam_im, d_skip, dep=None):
    L, d_ssm = u_bf.shape
    nt, row, state, whole = _ssm_specs(L, False)
    half = d_ssm // 2
    gw = d_ssm // NJ
    n_dep = 0 if dep is None else 1

    def body(u_ref, u0_ref, u1_ref, bdr, bdi, cdr, cdi, lr_ref, li_ref, d_ref, *rest):
        y_ref, z_ref, sr_ref, si_ref, carry = rest[n_dep:]
        i = pl.program_id(0)

        @pl.when(i == 0)
        def _():
            carry[...] = jnp.zeros_like(carry)

        for j in range(NJ):
            uj = u_ref[:, gw * j:gw * (j + 1)]
            _strided_put(sr_ref, j, _dot(uj, bdr[j]))
            _strided_put(si_ref, j, _dot(uj, bdi[j]))
        lr = [lr_ref[q] for q in range(NQ)]
        li = [li_ref[q] for q in range(NQ)]

        def step(t, s):
            sr, si = s
            rows = pl.ds(pl.multiple_of(t * NJ, NJ), NJ)
            nr = tuple(lr[q] * sr[q] - li[q] * si[q] + sr_ref[q, rows, :] for q in range(NQ))
            ni = tuple(lr[q] * si[q] + li[q] * sr[q] + si_ref[q, rows, :] for q in range(NQ))
            for q in range(NQ):
                sr_ref[q, rows, :] = nr[q]
                si_ref[q, rows, :] = ni[q]
            return nr, ni

        init = (tuple(carry[0, q] for q in range(NQ)), tuple(carry[1, q] for q in range(NQ)))
        sr, si = lax.fori_loop(0, SSM_T // SCAN_UNROLL, _unrolled(step), init)
        for q in range(NQ):
            carry[0, q] = sr[q]
            carry[1, q] = si[q]
        uf = jnp.concatenate([u0_ref[...], u1_ref[...]], axis=1)
        for j in range(NJ):
            cols = slice(gw * j, gw * (j + 1))
            yj = _dot(_strided_get(sr_ref, j).astype(bf16), cdr[j]) - _dot(_strided_get(si_ref, j).astype(bf16), cdi[j])
            yj = yj + d_ref[:, cols] * uf[:, cols]
            y_ref[:, cols] = yj
            z_ref[:, cols] = gelu(yj).astype(bf16)

    s_shape = jax.ShapeDtypeStruct((NQ, L * NJ, LANES), f32)
    consts = (bd_re, bd_im, cd_re, cd_im, lam_re, lam_im, d_skip)
    in_specs, operands, _ = _with_dep([row(d_ssm), row(half, u_cb), row(half, u_cb + 1)] + [whole(a) for a in consts],
                                      [u_bf, proj, proj, *consts], dep)
    return pl.pallas_call(
        body, out_shape=[jax.ShapeDtypeStruct((L, d_ssm), f32), jax.ShapeDtypeStruct((L, d_ssm), bf16), s_shape, s_shape], grid=(nt,),
        in_specs=in_specs, out_specs=[row(d_ssm), row(d_ssm), state, state],
        scratch_shapes=[pltpu.VMEM((2, NQ, NJ, LANES), f32)], name="ssm_fwd",
        compiler_params=_params("arbitrary"))(*operands)


def ssm_bwd(y, dz1, dz2, u_bf, proj, u_cb, s_re, s_im, bd_re, bd_im, cd_re, cd_im, lam_re, lam_im, d_skip, dep=None):
    L, d_ssm = y.shape
    nt, row, state, whole = _ssm_specs(L, True)
    half = d_ssm // 2
    gw = d_ssm // NJ
    n_dep = 0 if dep is None else 1

    def body(y_ref, dz1_ref, dz2_ref, u_ref, u0_ref, u1_ref, sr_ref, si_ref, bdr, bdi, cdr, cdi, lr_ref, li_ref, d_ref, *rest):
        du_ref, fbr, fbi, fcr, fci, dlr, dli, dd_ref, gr_ref, gi_ref, carry, dbdr, dbdi, dcdr, dcdi = rest[n_dep:]
        i = pl.program_id(0)

        @pl.when(i == 0)
        def _():
            carry[...] = jnp.zeros_like(carry)
            for r in (dbdr, dbdi, dcdr, dcdi, dlr, dli, dd_ref):
                r[...] = jnp.zeros_like(r)

        dyf = (dz1_ref[...] + dz2_ref[...]) * gelu_grad(y_ref[...])
        dyb = dyf.astype(bf16)
        for j in range(NJ):
            dyj = dyb[:, gw * j:gw * (j + 1)]
            _strided_put(gr_ref, j, _dot(dyj, cdr[j], NT))
            _strided_put(gi_ref, j, -_dot(dyj, cdi[j], NT))
            dcdr[j] += _dot(dyj, _strided_get(sr_ref, j).astype(bf16), TN)
            dcdi[j] -= _dot(dyj, _strided_get(si_ref, j).astype(bf16), TN)
        lr = [lr_ref[q] for q in range(NQ)]
        li = [li_ref[q] for q in range(NQ)]

        def step(k, c):
            gr, gi, ar, ai = c
            rows = pl.ds(pl.multiple_of((SSM_T - 1 - k) * NJ, NJ), NJ)
            s_r = [sr_ref[q, rows, :] for q in range(NQ)]
            s_i = [si_ref[q, rows, :] for q in range(NQ)]
            ar = tuple(ar[q] + gr[q] * s_r[q] + gi[q] * s_i[q] for q in range(NQ))
            ai = tuple(ai[q] + gi[q] * s_r[q] - gr[q] * s_i[q] for q in range(NQ))
            nr = tuple(gr_ref[q, rows, :] + lr[q] * gr[q] + li[q] * gi[q] for q in range(NQ))
            ni = tuple(gi_ref[q, rows, :] + lr[q] * gi[q] - li[q] * gr[q] for q in range(NQ))
            for q in range(NQ):
                gr_ref[q, rows, :] = nr[q]
                gi_ref[q, rows, :] = ni[q]
            return nr, ni, ar, ai

        zero = tuple(jnp.zeros((NJ, LANES), f32) for _ in range(NQ))
        init = (tuple(carry[0, q] for q in range(NQ)), tuple(carry[1, q] for q in range(NQ)), zero, zero)
        gr, gi, ar, ai = lax.fori_loop(0, SSM_T // SCAN_UNROLL, _unrolled(step), init)
        for q in range(NQ):
            carry[0, q] = gr[q]
            carry[1, q] = gi[q]
            dlr[q] += ar[q]
            dli[q] += ai[q]
        uf = jnp.concatenate([u0_ref[...], u1_ref[...]], axis=1)
        dd_ref[...] += colsum(dyf * uf)
        for j in range(NJ):
            cols = slice(gw * j, gw * (j + 1))
            gjr, gji = _strided_get(gr_ref, j).astype(bf16), _strided_get(gi_ref, j).astype(bf16)
            du_ref[:, cols] = _dot(gjr, bdr[j], NT) + _dot(gji, bdi[j], NT) + d_ref[:, cols] * dyf[:, cols]
            uj = u_ref[:, cols]
            dbdr[j] += _dot(uj, gjr, TN)
            dbdi[j] += _dot(uj, gji, TN)

        @pl.when(i == nt - 1)
        def _():
            nb = NJ * SSM_STATE
            diag = (lax.broadcasted_iota(jnp.int32, (gw, nb), 0) // SSM_GROUP) == (lax.broadcasted_iota(jnp.int32, (gw, nb), 1) // SSM_STATE)
            for j in range(NJ):
                for acc_ref, out in ((dbdr, fbr), (dbdi, fbi), (dcdr, fcr), (dcdi, fci)):
                    m = jnp.where(diag, acc_ref[j], 0.0)
                    f = m[:, :LANES]
                    for q in range(1, nb // LANES):
                        f = f + m[:, LANES * q:LANES * (q + 1)]
                    out[j] = f + pltpu.roll(f, SSM_STATE, 1)

    consts = (bd_re, bd_im, cd_re, cd_im, lam_re, lam_im, d_skip)
    acc = lambda a: jax.ShapeDtypeStruct(a.shape, f32)
    fb = jax.ShapeDtypeStruct((NJ, gw, LANES), f32)
    outs = [jax.ShapeDtypeStruct((L, d_ssm), f32), fb, fb, fb, fb, acc(lam_re), acc(lam_im), acc(d_skip)]
    in_specs, operands, _ = _with_dep(
        [row(d_ssm)] * 4 + [row(half, u_cb), row(half, u_cb + 1), state, state] + [whole(a) for a in consts],
        [y, dz1, dz2, u_bf, proj, proj, s_re, s_im, *consts], dep)
    return pl.pallas_call(
        body, out_shape=outs, grid=(nt,),
        in_specs=in_specs, out_specs=[row(d_ssm)] + [whole(a) for a in outs[1:]],
        scratch_shapes=[pltpu.VMEM((NQ, SSM_T * NJ, LANES), f32), pltpu.VMEM((NQ, SSM_T * NJ, LANES), f32),
                        pltpu.VMEM((2, NQ, NJ, LANES), f32)] + [pltpu.VMEM(bd_re.shape, f32)] * 4,
        name="ssm_bwd", compiler_params=_params("arbitrary"))(*operands)


def _cmul(ar, ai, br, bi):
    return ar * br - ai * bi, ar * bi + ai * br


def _disc(ar, ai, logdt):
    dt = jnp.exp(logdt)
    mag = jnp.exp(ar * dt)
    lr, li = mag * jnp.cos(ai * dt), mag * jnp.sin(ai * dt)
    den = ar * ar + ai * ai
    nr, ni = lr - 1.0, li
    fr, fi = (nr * ar + ni * ai) / den, (ni * ar - nr * ai) / den
    return dt, lr, li, den, fr, fi


def ssm_params(a_re, a_im, logdt_b, bt_re, bt_im, spread):
    def body(ar_ref, ai_ref, ld_ref, br_ref, bi_ref, sp_ref, lr_ref, li_ref, or_ref, oi_ref):
        _, lr, li, _, fr, fi = _disc(ar_ref[...], ai_ref[...], ld_ref[...])
        lr_ref[...] = lr
        li_ref[...] = li
        fre = jnp.dot(sp_ref[...], fr, precision=HIGHEST, preferred_element_type=f32)
        fie = jnp.dot(sp_ref[...], fi, precision=HIGHEST, preferred_element_type=f32)
        o_r, o_i = _cmul(fre, fie, br_ref[...], bi_ref[...])
        or_ref[...] = o_r
        oi_ref[...] = o_i

    g = jax.ShapeDtypeStruct(a_re.shape, f32)
    b = jax.ShapeDtypeStruct(bt_re.shape, f32)
    return pl.pallas_call(body, out_shape=[g, g, b, b], name="ssm_params",
                          compiler_params=_params())(a_re, a_im, logdt_b, bt_re, bt_im, spread)


def ssm_params_grad(a_re, a_im, logdt_b, bt_re, bt_im, spread, gather, dlam_re, dlam_im, dbt_re, dbt_im):
    def body(ar_ref, ai_ref, ld_ref, br_ref, bi_ref, sp_ref, ga_ref, glr_ref, gli_ref, gbr_ref, gbi_ref,
             dar_ref, dai_ref, dld_ref, dbr_ref, dbi_ref):
        ar, ai = ar_ref[...], ai_ref[...]
        dt, lr, li, den, fr, fi = _disc(ar, ai, ld_ref[...])
        hdot = functools.partial(jnp.dot, precision=HIGHEST, preferred_element_type=f32)
        fre, fie = hdot(sp_ref[...], fr), hdot(sp_ref[...], fi)
        gbr, gbi, br, bi = gbr_ref[...], gbi_ref[...], br_ref[...], bi_ref[...]
        dbr_ref[...], dbi_ref[...] = _cmul(fre, -fie, gbr, gbi)
        t_r, t_i = _cmul(br, -bi, gbr, gbi)
        gfr, gfi = hdot(ga_ref[...], t_r), hdot(ga_ref[...], t_i)
        iwr, iwi = ar / den, -ai / den
        x_r, x_i = _cmul(iwr, -iwi, gfr, gfi)
        glr, gli = glr_ref[...] + x_r, gli_ref[...] + x_i
        q_r, q_i = _cmul(fr, fi, iwr, iwi)
        gwr, gwi = _cmul(-q_r, q_i, gfr, gfi)
        y_r, y_i = _cmul(dt * lr, -dt * li, glr, gli)
        dar_ref[...] = gwr + y_r
        dai_ref[...] = gwi + y_i
        wl_r, wl_i = _cmul(ar, ai, lr, li)
        z_r, _ = _cmul(wl_r, -wl_i, glr, gli)
        dld_ref[...] = jnp.sum(z_r * dt, axis=1, keepdims=True)

    g = jax.ShapeDtypeStruct(a_re.shape, f32)
    b = jax.ShapeDtypeStruct(bt_re.shape, f32)
    return pl.pallas_call(body, out_shape=[g, g, jax.ShapeDtypeStruct((a_re.shape[0], 1), f32), b, b], name="ssm_params_grad",
                          compiler_params=_params())(a_re, a_im, logdt_b, bt_re, bt_im, spread, gather, dlam_re, dlam_im, dbt_re, dbt_im)


def _block_diag(t, rows, cols):
    G = t.shape[0]
    t = t.reshape(G // NJ, NJ, rows, cols)
    eye = jnp.eye(NJ, dtype=t.dtype)
    return jnp.einsum('jgrc,gh->jgrhc', t, eye).reshape(G // NJ, NJ * rows, NJ * cols)


def _state_layout(t):
    return t.reshape(NJ, NQ, LANES).transpose(1, 0, 2)


def _state_layout_inv(t, G, N):
    return t.transpose(1, 0, 2).reshape(G, N)


def _tiles2d(shape, budget_rows=128):
    rows, cols = shape
    tr = rows
    if rows > budget_rows:
        tr = budget_rows
        while rows % tr:
            tr -= SUBLANES
    return tr, cols


ADAM_TILE_BYTES = 3 << 19
ROW_ALIGN = 16


def _tile_rows(rows, row_bytes, target_bytes):
    tr = max(ROW_ALIGN, min(rows, target_bytes // row_bytes) // ROW_ALIGN * ROW_ALIGN)
    while rows % tr:
        tr -= ROW_ALIGN
    return tr


def _adam_update(w, g, m, v):
    c1 = 1.0 - ADAM_B1 ** ADAM_STEP
    c2 = 1.0 - ADAM_B2 ** ADAM_STEP
    nm = ADAM_B1 * m + (1.0 - ADAM_B1) * g
    nv = ADAM_B2 * v + (1.0 - ADAM_B2) * (g * g)
    delta = -ADAM_LR * ((nm / c1) / (jnp.sqrt(nv / c2) + ADAM_EPS) + ADAM_WD * w)
    return delta, nm, nv


def adamw_many(name, ws, gs, ms, vs):
    n = len(ws)

    def body(*refs):
        w, g, m, v = (refs[k * n:(k + 1) * n] for k in range(4))
        d, nm, nv = (refs[(4 + k) * n:(5 + k) * n] for k in range(3))
        for i in range(n):
            d[i][...], nm[i][...], nv[i][...] = _adam_update(w[i][...], g[i][...], m[i][...], v[i][...])

    o = [jax.ShapeDtypeStruct(a.shape, f32) for a in ws]
    outs = pl.pallas_call(body, out_shape=o * 3, name=name, compiler_params=_params())(*ws, *gs, *ms, *vs)
    return outs[:n], outs[n:2 * n], outs[2 * n:]


def adamw_halves(name, w, own, got, m, v, c_arr):
    h, cols = own.shape
    by_columns = h == w.shape[0]
    tr = _tile_rows(h, cols * 4, ADAM_TILE_BYTES)
    nh = h // tr

    def body(c_ref, w_ref, own_ref, got_ref, m_ref, v_ref, g_ref, d_ref, nm_ref, nv_ref):
        mine = (pl.program_id(0) // nh) == c_ref[0]
        g = jnp.where(mine, own_ref[...], got_ref[...])
        g_ref[...] = g
        d_ref[...], nm_ref[...], nv_ref[...] = _adam_update(w_ref[...], g, m_ref[...], v_ref[...])

    spec = pl.BlockSpec((tr, cols), (lambda i, c: (i % nh, i // nh)) if by_columns else (lambda i, c: (i, 0)))
    own_spec = pl.BlockSpec((tr, cols), lambda i, c: (jnp.where(i // nh == c[0], i % nh, 0), 0))
    got_spec = pl.BlockSpec((tr, cols), lambda i, c: (jnp.where(i // nh == c[0], 0, i % nh), 0))
    o = jax.ShapeDtypeStruct(w.shape, f32)
    grid_spec = pltpu.PrefetchScalarGridSpec(num_scalar_prefetch=1, grid=(2 * nh,),
                                             in_specs=[spec, own_spec, got_spec, spec, spec], out_specs=[spec] * 4)
    return pl.pallas_call(body, out_shape=[o, o, o, o], grid_spec=grid_spec, name=name,
                          compiler_params=_params("arbitrary"))(c_arr, w, own, got, m, v)


def chip_sum(name, pair, landed, mine_arr, dep=None):
    n_in, h, cols = landed.shape
    tr, _ = _tiles2d((h, cols), 256)

    def body(s_ref, p_ref, l_ref, *rest):
        acc = p_ref[...].astype(f32)
        for k in range(n_in):
            acc = acc + l_ref[k].astype(f32)
        rest[-1][...] = acc

    in_specs, operands, _ = _with_dep(
        [pl.BlockSpec((None, tr, cols), lambda i, s: (s[0], i, 0)), pl.BlockSpec((n_in, tr, cols), lambda i, s: (0, i, 0))],
        [pair, landed], dep)
    grid_spec = pltpu.PrefetchScalarGridSpec(num_scalar_prefetch=1, grid=(h // tr,), in_specs=in_specs,
                                             out_specs=pl.BlockSpec((tr, cols), lambda i, s: (i, 0)))
    return pl.pallas_call(body, out_shape=jax.ShapeDtypeStruct((h, cols), f32), grid_spec=grid_spec, name=name,
                          compiler_params=_params("parallel"))(mine_arr, *operands)


def into_slot(name, w, slot_arr, n_slots, dtype, dep=None):
    tr, cols = _tiles2d(w.shape, 256)

    def body(s_ref, w_ref, *rest):
        rest[-1][...] = w_ref[...].astype(dtype)

    in_specs, operands, _ = _with_dep([pl.BlockSpec((tr, cols), lambda i, s: (i, 0))], [w], dep)
    grid_spec = pltpu.PrefetchScalarGridSpec(num_scalar_prefetch=1, grid=(w.shape[0] // tr,), in_specs=in_specs,
                                             out_specs=pl.BlockSpec((None, tr, cols), lambda i, s: (s[0], i, 0)))
    return pl.pallas_call(body, out_shape=jax.ShapeDtypeStruct((n_slots,) + w.shape, dtype), grid_spec=grid_spec, name=name,
                          compiler_params=_params("parallel"))(slot_arr, *operands)


def sum_slots(name, t):
    S, rows, cols = t.shape
    tr, _ = _tiles2d((rows, cols), 256)

    def body(t_ref, o_ref):
        acc = t_ref[0]
        for s in range(1, S):
            acc = acc + t_ref[s]
        o_ref[...] = acc

    return pl.pallas_call(body, out_shape=jax.ShapeDtypeStruct((rows, cols), f32), grid=(rows // tr,),
                          in_specs=[pl.BlockSpec((S, tr, cols), lambda i: (0, i, 0))], out_specs=pl.BlockSpec((tr, cols), lambda i: (i, 0)),
                          name=name, compiler_params=_params("parallel"))(t)


def _place():
    x, y, c = lax.axis_index("x"), lax.axis_index("y"), lax.axis_index("c")
    return x, y, c


def _other_chips(x, y):
    return [(1 - x, y, 2 * (1 - x) + y), (x, 1 - y, 2 * x + 1 - y), (1 - x, 1 - y, 2 * (1 - x) + 1 - y)]


SEM = pl.BlockSpec(memory_space=pltpu.SEMAPHORE)
VM = pl.BlockSpec(memory_space=pltpu.VMEM)
DATAFLOW = pltpu.SideEffectType.DATAFLOW_SIDE_EFFECTING
TOKEN = jax.ShapeDtypeStruct((SUBLANES, LANES), f32)


def _gather_copy(buf, w, k, chip, c, mine, send, recv):
    px, py, _ = chip
    h = buf.shape[1] // 2
    half = buf.at[mine, pl.ds(c * h, h), :]
    return pltpu.make_async_remote_copy(src_ref=half, dst_ref=half, send_sem=send.at[3 * w + k], recv_sem=recv.at[3 * w + k],
                                        device_id=(px, py, c), device_id_type=MESH)


def _gather_landing(buf, w, k, chip, c, send, recv):
    px, py, s = chip
    h = buf.shape[1] // 2
    landed = buf.at[s, pl.ds(c * h, h), :]
    return pltpu.make_async_remote_copy(src_ref=landed, dst_ref=landed, send_sem=send.at[3 * w + k], recv_sem=recv.at[3 * w + k],
                                        device_id=(px, py, c), device_id_type=MESH)


def gather_start(name, bufs, groups, after, neighbours_only=()):
    nw, ng = len(bufs), len(groups)

    def body(*refs):
        outs = refs[nw + 1:]
        sems, dst = outs[:2 * ng], outs[2 * ng:2 * ng + nw]
        token = outs[2 * ng + nw]
        x, y, c = _place()
        mine = 2 * x + y
        for g, members in enumerate(groups):
            for i, w in enumerate(members):
                for k, chip in enumerate(_other_chips(x, y)[:2 if w in neighbours_only else 3]):
                    _gather_copy(dst[w], i, k, chip, c, mine, sems[2 * g], sems[2 * g + 1]).start()
        token[...] = jnp.zeros_like(token)

    sem_shapes = []
    for members in groups:
        sem_shapes += [pltpu.SemaphoreType.DMA((3 * len(members),))] * 2
    outs = pl.pallas_call(
        body, out_shape=sem_shapes + [jax.ShapeDtypeStruct(b.shape, b.dtype) for b in bufs] + [TOKEN],
        in_specs=[ANY] * (nw + 1), out_specs=[SEM] * (2 * ng) + [ANY] * nw + [VM],
        input_output_aliases={w: 2 * ng + w for w in range(nw)}, name=name,
        compiler_params=pltpu.CompilerParams(has_side_effects=DATAFLOW))(*bufs, after)
    return [(outs[2 * g], outs[2 * g + 1]) for g in range(ng)], list(outs[2 * ng:2 * ng + nw]), outs[2 * ng + nw]


def gather_wait(name, bufs, send, recv, after):
    nw = len(bufs)

    def body(*refs):
        src = refs[:nw]
        send_ref, recv_ref = refs[nw], refs[nw + 1]
        x, y, c = _place()
        mine = 2 * x + y
        for w in range(nw):
            for k, chip in enumerate(_other_chips(x, y)):
                _gather_copy(src[w], w, k, chip, c, mine, send_ref, recv_ref).wait_send()
                _gather_landing(src[w], w, k, chip, c, send_ref, recv_ref).wait_recv()

    return pl.pallas_call(
        body, out_shape=[jax.ShapeDtypeStruct(b.shape, b.dtype) for b in bufs],
        in_specs=[ANY] * nw + [SEM, SEM, ANY], out_specs=[ANY] * nw,
        input_output_aliases={w: w for w in range(nw)}, name=name,
        compiler_params=pltpu.CompilerParams(has_side_effects=DATAFLOW))(*bufs, send, recv, after)


def _relay_copy(buf, w, j, x, y, c, send, recv, landing):
    chips = _other_chips(x, y)
    px, py, _ = chips[j]
    h = buf.shape[1] // 2
    q = h // 2
    s = chips[2][2] if landing else chips[1 - j][2]
    part = buf.at[s, pl.ds(c * h + j * q, q), :]
    return pltpu.make_async_remote_copy(src_ref=part, dst_ref=part, send_sem=send.at[2 * w + j], recv_sem=recv.at[2 * w + j],
                                        device_id=(px, py, c), device_id_type=MESH)


def _early_pass(buf, nw, w, k, x, y, c, send, recv, landing):
    s = _other_chips(x, y)[k][2]
    h = buf.shape[1] // 2
    part = buf.at[s, pl.ds(((1 - c) if landing else c) * h, h), :]
    i = 2 * nw + 2 * w + k
    return pltpu.make_async_remote_copy(src_ref=part, dst_ref=part, send_sem=send.at[i], recv_sem=recv.at[i],
                                        device_id=(x, y, 1 - c), device_id_type=MESH)


def gather_relay(name, bufs, sems, more, after):
    nw, nm = len(bufs), len(more)
    ns = 4 if nm else 2

    def body(*refs):
        ins, outs = refs[:nw + nm + 2 * nw + 1], refs[nw + nm + 2 * nw + 1:]
        src, d_sems = ins[:nw], ins[nw + nm:nw + nm + 2 * nw]
        r_send, r_recv = outs[:2]
        m_send, m_recv = outs[2:ns] if nm else (None, None)
        dst, mdst, token = outs[ns:ns + nw], outs[ns + nw:ns + nw + nm], outs[ns + nw + nm]
        x, y, c = _place()
        mine = 2 * x + y
        chips = _other_chips(x, y)
        for w in range(nw):
            for k in range(2):
                _gather_copy(src[w], 0, k, chips[k], c, mine, d_sems[2 * w], d_sems[2 * w + 1]).wait_send()
                _gather_landing(src[w], 0, k, chips[k], c, d_sems[2 * w], d_sems[2 * w + 1]).wait_recv()
            for j in range(2):
                _relay_copy(dst[w], w, j, x, y, c, r_send, r_recv, False).start()
            for k in range(2):
                _early_pass(dst[w], nw, w, k, x, y, c, r_send, r_recv, False).start()
        for w in range(nm):
            for k, chip in enumerate(chips):
                _gather_copy(mdst[w], w, k, chip, c, mine, m_send, m_recv).start()
        token[...] = jnp.zeros_like(token)

    sem_shapes = [pltpu.SemaphoreType.DMA((4 * nw,))] * 2 + [pltpu.SemaphoreType.DMA((3 * nm,))] * (ns - 2)
    flat_sems = [s for pair in sems for s in pair]
    outs = pl.pallas_call(
        body, out_shape=sem_shapes + [jax.ShapeDtypeStruct(b.shape, b.dtype) for b in list(bufs) + list(more)] + [TOKEN],
        in_specs=[ANY] * (nw + nm) + [SEM] * (2 * nw) + [ANY], out_specs=[SEM] * ns + [ANY] * (nw + nm) + [VM],
        input_output_aliases={i: ns + i for i in range(nw + nm)}, name=name,
        compiler_params=pltpu.CompilerParams(has_side_effects=DATAFLOW))(*bufs, *more, *flat_sems, after)
    m_sems = (outs[2], outs[3]) if nm else None
    return outs[0], outs[1], m_sems, list(outs[ns:ns + nw]), list(outs[ns + nw:ns + nw + nm]), outs[ns + nw + nm]


def gather_wait_relay(name, bufs, r_send, r_recv, after):
    nw = len(bufs)

    def body(*refs):
        src = refs[:nw]
        send_ref, recv_ref = refs[nw], refs[nw + 1]
        x, y, c = _place()
        for w in range(nw):
            for j in range(2):
                _relay_copy(src[w], w, j, x, y, c, send_ref, recv_ref, False).wait_send()
                _relay_copy(src[w], w, j, x, y, c, send_ref, recv_ref, True).wait_recv()
                _early_pass(src[w], nw, w, j, x, y, c, send_ref, recv_ref, False).wait_send()
                _early_pass(src[w], nw, w, j, x, y, c, send_ref, recv_ref, True).wait_recv()

    return pl.pallas_call(
        body, out_shape=[jax.ShapeDtypeStruct(b.shape, b.dtype) for b in bufs],
        in_specs=[ANY] * nw + [SEM, SEM, ANY], out_specs=[ANY] * nw,
        input_output_aliases={w: w for w in range(nw)}, name=name,
        compiler_params=pltpu.CompilerParams(has_side_effects=DATAFLOW))(*bufs, r_send, r_recv, after)


def gather_forward(name, bufs, which=(0, 1, 2)):
    nw = len(bufs)

    def body(*refs):
        dst = refs[nw:2 * nw]
        send, recv = refs[2 * nw:]
        x, y, c = _place()
        sib = (x, y, 1 - c)
        barrier = pltpu.get_barrier_semaphore()
        pl.semaphore_signal(barrier, inc=1, device_id=sib, device_id_type=MESH)
        pl.semaphore_wait(barrier, 1)
        cps = []
        for w in range(nw):
            h = dst[w].shape[1] // 2
            for k in which:
                s = _other_chips(x, y)[k][2]
                landed = dst[w].at[s, pl.ds(c * h, h), :]
                cp = pltpu.make_async_remote_copy(src_ref=landed, dst_ref=landed, send_sem=send.at[w, k], recv_sem=recv.at[w, k],
                                                  device_id=sib, device_id_type=MESH)
                cp.start()
                cps.append(cp)
        for w in range(nw):
            h = dst[w].shape[1] // 2
            for k in which:
                s = _other_chips(x, y)[k][2]
                other = dst[w].at[s, pl.ds((1 - c) * h, h), :]
                pltpu.make_async_remote_copy(src_ref=other, dst_ref=other, send_sem=send.at[w, k], recv_sem=recv.at[w, k],
                                             device_id=sib, device_id_type=MESH).wait_recv()
        for cp in cps:
            cp.wait_send()

    sem = pltpu.SemaphoreType.DMA((nw, 3))
    return pl.pallas_call(
        body, out_shape=[jax.ShapeDtypeStruct(b.shape, b.dtype) for b in bufs],
        in_specs=[ANY] * nw, out_specs=[ANY] * nw, input_output_aliases={w: w for w in range(nw)},
        scratch_shapes=[sem, sem], name=name,
        compiler_params=pltpu.CompilerParams(has_side_effects=True, collective_id=SIBLING_PAIR))(*bufs)


def _scatter_copy(src, dst, w, k, chip, c, send, recv):
    px, py, s = chip
    return pltpu.make_async_remote_copy(src_ref=src.at[s], dst_ref=dst.at[k], send_sem=send.at[3 * w + k], recv_sem=recv.at[3 * w + k],
                                        device_id=(px, py, c), device_id_type=MESH)


def scatter_start(name, parts):
    nw = len(parts)
    lands = [pltpu.with_memory_space_constraint(lax.empty((N_CHIPS - 1,) + p.shape[1:], p.dtype), pltpu.HBM) for p in parts]

    def body(*refs):
        outs = refs[2 * nw:]
        send, recv = outs[0], outs[1]
        src, dst, token = outs[2:2 + nw], outs[2 + nw:2 + 2 * nw], outs[2 + 2 * nw]
        x, y, c = _place()
        for w in range(nw):
            for k, chip in enumerate(_other_chips(x, y)):
                _scatter_copy(src[w], dst[w], w, k, chip, c, send, recv).start()
        token[...] = jnp.zeros_like(token)

    sem = pltpu.SemaphoreType.DMA((3 * nw,))
    outs = pl.pallas_call(
        body, out_shape=[sem, sem] + [jax.ShapeDtypeStruct(p.shape, p.dtype) for p in parts]
        + [jax.ShapeDtypeStruct(l.shape, l.dtype) for l in lands] + [TOKEN],
        in_specs=[ANY] * (2 * nw), out_specs=[SEM, SEM] + [ANY] * (2 * nw) + [VM],
        input_output_aliases={i: 2 + i for i in range(2 * nw)}, name=name,
        compiler_params=pltpu.CompilerParams(has_side_effects=DATAFLOW))(*parts, *lands)
    return outs[0], outs[1], list(outs[2:2 + nw]), list(outs[2 + nw:2 + 2 * nw]), outs[2 + 2 * nw]


def scatter_wait(name, parts, lands, send, recv, after):
    nw = len(parts)

    def body(*refs):
        src, dst = refs[:nw], refs[nw:2 * nw]
        send_ref, recv_ref = refs[2 * nw], refs[2 * nw + 1]
        x, y, c = _place()
        for w in range(nw):
            for k, chip in enumerate(_other_chips(x, y)):
                cp = _scatter_copy(src[w], dst[w], w, k, chip, c, send_ref, recv_ref)
                cp.wait_send()
                cp.wait_recv()

    outs = pl.pallas_call(
        body, out_shape=[jax.ShapeDtypeStruct(a.shape, a.dtype) for a in list(parts) + list(lands)],
        in_specs=[ANY] * (2 * nw) + [SEM, SEM, ANY], out_specs=[ANY] * (2 * nw),
        input_output_aliases={i: i for i in range(2 * nw)}, name=name,
        compiler_params=pltpu.CompilerParams(has_side_effects=DATAFLOW))(*parts, *lands, send, recv, after)
    return list(outs[:nw]), list(outs[nw:])


SIBLING_PAIR = 0


def _sibling_copy(src, dst, w, c, half_rows, send, recv, sib):
    if half_rows:
        h = src.shape[1] // 2
        src = src.at[:, pl.ds((1 - c) * h, h), :]
    return pltpu.make_async_remote_copy(src_ref=src, dst_ref=dst, send_sem=send.at[w], recv_sem=recv.at[w],
                                        device_id=sib, device_id_type=MESH)


def _landing(shape, dtype):
    return pltpu.with_memory_space_constraint(lax.empty(shape, dtype), pltpu.HBM)


def sibling_start(name, srcs, half_rows):
    nw = len(srcs)
    lands = [_landing((s.shape[0], s.shape[1] // 2, s.shape[2]) if half_rows else s.shape, s.dtype) for s in srcs]

    def body(*refs):
        outs = refs[2 * nw:]
        send, recv = outs[0], outs[1]
        src, dst, token = outs[2:2 + nw], outs[2 + nw:2 + 2 * nw], outs[2 + 2 * nw]
        x, y, c = _place()
        barrier = pltpu.get_barrier_semaphore()
        pl.semaphore_signal(barrier, inc=1, device_id=(x, y, 1 - c), device_id_type=MESH)
        pl.semaphore_wait(barrier, 1)
        for w in range(nw):
            _sibling_copy(src[w], dst[w], w, c, half_rows, send, recv, (x, y, 1 - c)).start()
        token[...] = jnp.zeros_like(token)

    sem = pltpu.SemaphoreType.DMA((nw,))
    outs = pl.pallas_call(
        body, out_shape=[sem, sem] + [jax.ShapeDtypeStruct(a.shape, a.dtype) for a in list(srcs) + lands] + [TOKEN],
        in_specs=[ANY] * (2 * nw), out_specs=[SEM, SEM] + [ANY] * (2 * nw) + [VM],
        input_output_aliases={i: 2 + i for i in range(2 * nw)}, name=name,
        compiler_params=pltpu.CompilerParams(has_side_effects=DATAFLOW, collective_id=SIBLING_PAIR))(*srcs, *lands)
    return outs[0], outs[1], list(outs[2:2 + nw]), list(outs[2 + nw:2 + 2 * nw]), outs[2 + 2 * nw]


def sibling_wait(name, srcs, lands, send, recv, half_rows, after):
    nw = len(srcs)

    def body(*refs):
        src, dst = refs[:nw], refs[nw:2 * nw]
        send_ref, recv_ref = refs[2 * nw], refs[2 * nw + 1]
        x, y, c = _place()
        for w in range(nw):
            cp = _sibling_copy(src[w], dst[w], w, c, half_rows, send_ref, recv_ref, (x, y, 1 - c))
            cp.wait_send()
            cp.wait_recv()

    outs = pl.pallas_call(
        body, out_shape=[jax.ShapeDtypeStruct(a.shape, a.dtype) for a in list(srcs) + list(lands)],
        in_specs=[ANY] * (2 * nw) + [SEM, SEM, ANY], out_specs=[ANY] * (2 * nw),
        input_output_aliases={i: i for i in range(2 * nw)}, name=name,
        compiler_params=pltpu.CompilerParams(has_side_effects=DATAFLOW))(*srcs, *lands, send, recv, after)
    return list(outs[:nw]), list(outs[nw:])


def _peer(x, y, c, r):
    return (1 - x if r & 4 else x, 1 - y if r & 2 else y, 1 - c if r & 1 else c)


def _everyone_copy(buf, r, x, y, c, send, recv, landing):
    px, py, pc = _peer(x, y, c, r)
    slot = buf.at[4 * px + 2 * py + pc] if landing else buf.at[4 * x + 2 * y + c]
    return pltpu.make_async_remote_copy(src_ref=slot, dst_ref=slot, send_sem=send.at[r - 1], recv_sem=recv.at[r - 1],
                                        device_id=(px, py, pc), device_id_type=MESH)


def everyone_start(name, buf):
    def body(buf_in, send, recv, buf_ref, token):
        x, y, c = _place()
        for r in range(1, N_DEV):
            _everyone_copy(buf_ref, r, x, y, c, send, recv, False).start()
        token[...] = jnp.zeros_like(token)

    sem = pltpu.SemaphoreType.DMA((N_DEV - 1,))
    return pl.pallas_call(
        body, out_shape=[sem, sem, jax.ShapeDtypeStruct(buf.shape, buf.dtype), TOKEN],
        in_specs=[ANY], out_specs=[SEM, SEM, ANY, VM], input_output_aliases={0: 2}, name=name,
        compiler_params=pltpu.CompilerParams(has_side_effects=DATAFLOW))(buf)


def everyone_wait(name, buf, send, recv, after):
    def body(buf_ref, send_ref, recv_ref, after_ref, out_ref):
        x, y, c = _place()
        for r in range(1, N_DEV):
            _everyone_copy(buf_ref, r, x, y, c, send_ref, recv_ref, False).wait_send()
            _everyone_copy(buf_ref, r, x, y, c, send_ref, recv_ref, True).wait_recv()

    return pl.pallas_call(
        body, out_shape=jax.ShapeDtypeStruct(buf.shape, buf.dtype), in_specs=[ANY, SEM, SEM, ANY], out_specs=ANY,
        input_output_aliases={0: 0}, name=name,
        compiler_params=pltpu.CompilerParams(has_side_effects=DATAFLOW))(buf, send, recv, after)


def local_step(x, pos, tgt, small, d_in, get_w, put_g, first_dep=None, tick=lambda name, after: None):
    L, D = x.shape
    d_kv = N_KV_HEADS * HEAD_DIM
    d_ssm = small["d_skip"].shape[1]
    d_attn = d_in - 2 * d_kv - d_ssm
    big = {}
    G = d_ssm // SSM_GROUP
    N, P = SSM_STATE, SSM_GROUP

    half_dim = HEAD_DIM // 2
    inv_freq = ROPE_THETA ** (-jnp.arange(half_dim, dtype=f32) / half_dim)
    inv_freq = jnp.tile(inv_freq, LANES // half_dim).reshape(1, LANES)
    sink_b = jnp.broadcast_to(small["sinks"].reshape(-1, 1), (small["sinks"].size, LANES))

    spread = jnp.repeat(jnp.eye(G, dtype=f32), P, axis=0)
    logdt_b = jnp.broadcast_to(small["log_dt"].reshape(G, 1), (G, N))
    bt_re = small["b_re"].reshape(G, N, P).transpose(0, 2, 1).reshape(G * P, N)
    bt_im = small["b_im"].reshape(G, N, P).transpose(0, 2, 1).reshape(G * P, N)
    a_re, a_im = small["a_re"].reshape(G, N), small["a_im"].reshape(G, N)
    lam_re, lam_im, bbt_re, bbt_im = ssm_params(a_re, a_im, logdt_b, bt_re, bt_im, spread)
    bd_re = _block_diag(bbt_re.reshape(G, P, N), P, N).astype(bf16)
    bd_im = _block_diag(bbt_im.reshape(G, P, N), P, N).astype(bf16)
    c_re = small["c_re"].reshape(G, P, N).transpose(0, 2, 1)
    c_im = small["c_im"].reshape(G, P, N).transpose(0, 2, 1)
    cd_re = _block_diag(c_re, N, P).astype(bf16)
    cd_im = _block_diag(c_im, N, P).astype(bf16)
    lam_re_l, lam_im_l = _state_layout(lam_re), _state_layout(lam_im)

    def k1(i, nt, xt, g):
        return (rms_fwd(xt, g),)
    xn = rowwise("pre_mix_norm", k1, L, [full(x)], [small["g_pre_mix"]], [(D, bf16)], dep=first_dep)[0]
    big["w_in"] = get_w("w_in", xn)
    proj = mm_nn("proj_in", xn, big["w_in"])
    qr, kk, vv, u_bf = qkv_prep(proj, pos, inv_freq, d_attn, d_kv)
    attn = attn_fwd(qr, kk, vv, sink_b)
    u_cb = (d_attn + 2 * d_kv) // (d_ssm // 2)
    token = tick("attn", attn)
    y, z_bf, s_re, s_im = ssm_fwd(u_bf, proj, u_cb, bd_re, bd_im, cd_re, cd_im, lam_re_l, lam_im_l, small["d_skip"], dep=token)
    token = tick("ssm", z_bf)
    big["w_glu"] = get_w("w_glu", z_bf)
    gl = mm_nn("glu_proj", z_bf, big["w_glu"], dep=token)

    def k6(i, nt, at, yt, glt, bg, ga, gs):
        ssm = gelu(yt) * sigmoid(glt + bg)
        return (jnp.concatenate([rms_fwd(at, ga), rms_fwd(ssm, gs)], axis=1),)
    mixed = rowwise("mix_norms", k6, L, [full(attn), full(y), full(gl)],
                    [small["b_glu"], small["g_attn_out"], small["g_ssm_out"]], [(d_attn + d_ssm, bf16)])[0]
    big["w_o"] = get_w("w_o", mixed)
    mix = mm_nn("proj_out", mixed, big["w_o"])

    def k7(i, nt, xt, mt, gpm, gpf):
        h = xt + rms_fwd(mt, gpm)
        return h, rms_fwd(h, gpf)
    h, hn = rowwise("post_mix", k7, L, [full(x), full(mix)], [small["g_post_mix"], small["g_pre_ffn"]], [(D, f32), (D, bf16)])
    big["w_gate"] = get_w("w_gate", hn)
    big["w_up"] = get_w("w_up", hn)
    hid_dg, hid_du, hid = ffn_hidden(hn, big["w_gate"], big["w_up"])
    d_ff_dim = hid.shape[1]
    big["w_down"] = get_w("w_down", hid)
    ff = mm_nn("ffn_down", hid, big["w_down"], tk=d_ff_dim // 2)

    def k9(i, nt, ht, fft, tt, g):
        out = ht + rms_fwd(fft, g)
        err = out - tt
        per_row = jnp.mean(err * err, axis=-1, keepdims=True)
        loss = 0.5 * jnp.sum(per_row) * jnp.where(_lane((1, LANES)) == 0, 1.0, 0.0)
        d_out = err * (1.0 / D)
        d_ff, dg = rms_bwd(fft, g, d_out)
        return d_out, d_ff, dg, loss
    d_out, d_ff, dg_post_ffn, loss = rowwise("loss_head", k9, L, [full(h), full(ff), full(tgt)], [small["g_post_ffn"]],
                                             [(D, f32), (D, bf16)], reds=[D, LANES])

    d_gt, d_up = ffn_hidden_grad(d_ff, big["w_down"], hid_dg, hid_du)
    token = put_g("w_down", hid, d_ff, None)
    d_hn = mm_nt_pair("d_hn", d_gt, big["w_gate"], d_up, big["w_up"], dep=token)
    token = put_g("w_gate", hn, d_gt, None)
    token = put_g("w_up", hn, d_up, token)

    def k11(i, nt, ht, da, do, mt, gpf, gpm):
        dh_n, dg_pf = rms_bwd(ht, gpf, da)
        dh = do + dh_n
        d_mix, dg_pm = rms_bwd(mt, gpm, dh)
        return dh, d_mix, dg_pf, dg_pm
    dh, d_mix, dg_pre_ffn, dg_post_mix = rowwise("post_mix_grad", k11, L, [full(h), full(d_hn), full(d_out), full(mix)],
                                                 [small["g_pre_ffn"], small["g_post_mix"]], [(D, f32), (D, bf16)], reds=[D, D], dep=token)
    d_mixed = mm_nt("d_mixed", d_mix, big["w_o"])
    token = put_g("w_o", mixed, d_mix, None)

    def k12(i, nt, at, yt, glt, da_n, ds_n, bg, ga, gs):
        z = gelu(yt)
        sg = sigmoid(glt + bg)
        ssm = z * sg
        d_at, dga = rms_bwd(at, ga, da_n)
        d_ssm_t, dgs = rms_bwd(ssm, gs, ds_n)
        d_gl = d_ssm_t * z * sg * (1.0 - sg)
        return d_at, d_ssm_t * sg, d_gl, dga, dgs, colsum(d_gl)
    d_attn_o, dz1, d_gl, dg_attn, dg_ssm, db_glu = rowwise(
        "mix_norms_grad", k12, L, [full(attn), full(y), full(gl), (d_mixed, d_attn, 0, 0), (d_mixed, d_ssm, d_attn // d_ssm, 0)],
        [small["b_glu"], small["g_attn_out"], small["g_ssm_out"]], [(d_attn, f32), (d_ssm, f32), (d_ssm, bf16)],
        reds=[d_attn, d_ssm, d_ssm], dep=token)
    dz2 = mm_nt("d_glu_in", d_gl, big["w_glu"])
    token = put_g("w_glu", z_bf, d_gl, None)

    du, dbd_re, dbd_im, dcd_re, dcd_im, dlam_re_l, dlam_im_l, dd_skip = ssm_bwd(
        y, dz1, dz2, u_bf, proj, u_cb, s_re, s_im, bd_re, bd_im, cd_re, cd_im, lam_re_l, lam_im_l, small["d_skip"], dep=token)
    dq, dkk_c, dkk_p, dvv_c, dvv_p, dsink = attn_bwd(qr, kk, vv, sink_b, attn, d_attn_o)
    d_proj = qkv_grad(dq, dkk_c, dkk_p, dvv_c, dvv_p, du, pos, inv_freq)
    d_xn = mm_nt("d_xn", d_proj, big["w_in"])
    token = put_g("w_in", xn, d_proj, None)

    def k17(i, nt, xt, dxn, dht, g):
        dx, dg = rms_bwd(xt, g, dxn)
        return dht + dx, dg
    grad_x, dg_pre_mix = rowwise("pre_mix_grad", k17, L, [full(x), full(d_xn), full(dh)], [small["g_pre_mix"]],
                                 [(D, f32)], reds=[D], dep=token)

    gather = spread.T
    dbbt_re = dbd_re.reshape(G * P, LANES)[:, :N]
    dbbt_im = dbd_im.reshape(G * P, LANES)[:, :N]
    dc_re = dcd_re.reshape(G * P, LANES)[:, :N].reshape(G, P, N)
    dc_im = dcd_im.reshape(G * P, LANES)[:, :N].reshape(G, P, N)
    d_a_re, d_a_im, d_logdt, dbt_re, dbt_im = ssm_params_grad(
        a_re, a_im, logdt_b, bt_re, bt_im, spread, gather,
        _state_layout_inv(dlam_re_l, G, N), _state_layout_inv(dlam_im_l, G, N), dbbt_re, dbbt_im)
    q_per_kv = d_attn // HEAD_DIM // N_KV_HEADS
    small_grads = {
        "g_pre_mix": dg_pre_mix, "sinks": dsink[:, :q_per_kv, 0].reshape(1, -1),
        "a_re": d_a_re, "a_im": d_a_im, "log_dt": d_logdt.reshape(1, G),
        "b_re": dbt_re, "b_im": dbt_im,
        "c_re": dc_re, "c_im": dc_im,
        "d_skip": dd_skip, "b_glu": db_glu, "g_attn_out": dg_attn, "g_ssm_out": dg_ssm,
        "g_post_mix": dg_post_mix, "g_pre_ffn": dg_pre_ffn, "g_post_ffn": dg_post_ffn,
    }
    return loss, grad_x, small_grads


WEIGHTS = ['g_pre_mix', 'w_in', 'sinks', 'a_re', 'a_im', 'log_dt', 'b_re', 'b_im', 'c_re', 'c_im', 'd_skip', 'w_glu', 'b_glu',
           'g_attn_out', 'g_ssm_out', 'w_o', 'g_post_mix', 'g_pre_ffn', 'w_gate', 'w_up', 'w_down', 'g_post_ffn']
BIG = ['w_in', 'w_glu', 'w_o', 'w_gate', 'w_up', 'w_down']
COL_SHARDED = ['w_in', 'w_gate', 'w_up']
SMALL = [n for n in WEIGHTS if n not in BIG]
GATHER_GROUPS = [["w_in"], ["w_glu", "w_o"], ["w_gate", "w_up"], ["w_down"]]
REDUCE_GROUPS = [["w_down", "w_gate", "w_up"], ["w_o", "w_glu", "w_in"]]


PACK_ROWS = 256


def _pack(parts):
    flat = jnp.concatenate([p.reshape(-1) for p in parts])
    pad = (-flat.size) % (PACK_ROWS * LANES)
    return jnp.pad(flat, (0, pad)).reshape(-1, LANES)


TRANSPOSED_VIEW = ("b_re", "b_im")


def small_view(name, a):
    if name in TRANSPOSED_VIEW:
        a = a.transpose(0, 1, 3, 2)
    return a.reshape(-1, a.shape[-1])


def small_unview(name, p, shape):
    if name in TRANSPOSED_VIEW:
        return p.reshape(shape[0], shape[1], shape[3], shape[2]).transpose(0, 1, 3, 2)
    return p.reshape(shape)


def _unpack(packed, shapes):
    flat = packed.reshape(-1)
    out, off = [], 0
    for s in shapes:
        n = int(np.prod(s))
        out.append(flat[off:off + n].reshape(s))
        off += n
    return out


def kernel(x, positions, g_pre_mix, w_in, sinks, a_re, a_im, log_dt, b_re, b_im, c_re, c_im, d_skip, w_glu, b_glu, g_attn_out, g_ssm_out, w_o, g_post_mix, g_pre_ffn, w_gate, w_up, w_down, g_post_ffn, loss_target, m_g_pre_mix, m_w_in, m_sinks, m_a_re, m_a_im, m_log_dt, m_b_re, m_b_im, m_c_re, m_c_im, m_d_skip, m_w_glu, m_b_glu, m_g_attn_out, m_g_ssm_out, m_w_o, m_g_post_mix, m_g_pre_ffn, m_w_gate, m_w_up, m_w_down, m_g_post_ffn, v_g_pre_mix, v_w_in, v_sinks, v_a_re, v_a_im, v_log_dt, v_b_re, v_b_im, v_c_re, v_c_im, v_d_skip, v_w_glu, v_b_glu, v_g_attn_out, v_g_ssm_out, v_w_o, v_g_post_mix, v_g_pre_ffn, v_w_gate, v_w_up, v_w_down, v_g_post_ffn):
    args = dict(locals())
    w = {n: args[n] for n in WEIGHTS}
    m = {n: args["m_" + n] for n in WEIGHTS}
    v = {n: args["v_" + n] for n in WEIGHTS}
    L, D = x.shape[1], x.shape[2]

    ax, ay, ac = _place()
    mine_arr = (2 * ax + ay).astype(jnp.int32).reshape(1)
    c_arr = ac.astype(jnp.int32).reshape(1)

    me_arr = (4 * ax + 2 * ay + ac).astype(jnp.int32).reshape(1)

    bufs = {"w_in": into_slot("cast_w_in", w["w_in"][0], mine_arr, N_CHIPS, bf16)}
    (first_sems,), (bufs["w_in"],), token = gather_start("gather_start_in", [bufs["w_in"]], [[0]], mine_arr, neighbours_only=(0,))
    sems, relays, ready = {"w_in": first_sems}, {}, set()

    def relay(n, more, after):
        r_send, r_recv, more_sems, (bufs[n],), started, tok = gather_relay(
            "gather_relay_" + n, [bufs[n]], [sems[n]], [bufs[k] for k in more], after)
        sems.update({k: more_sems for k in more})
        bufs.update(zip(more, started))
        relays[n] = (r_send, r_recv)
        return tok

    for n in BIG[1:]:
        bufs[n] = into_slot("cast_" + n, w[n][0], mine_arr, N_CHIPS, bf16, dep=token)
    token = relay("w_in", [], bufs[BIG[-1]])
    first = ["w_gate", "w_glu", "w_o", "w_up"]
    (sems["w_gate"], sems["w_glu"], sems["w_up"]), started, token = gather_start(
        "gather_start_rest", [bufs[n] for n in first], [[0], [1, 2], [3]], token, neighbours_only=(0, 3))
    bufs.update(zip(first, started))

    def tick(name, after):
        return relay("w_gate", [], after) if name == "attn" else relay("w_up", ["w_down"], after)

    def get_w(n, after):
        if n not in ready:
            members = [g for g in GATHER_GROUPS if n in g][0]
            if members[0] in relays:
                landed = [gather_wait_relay("gather_wait_" + k, [bufs[k]], *relays[k], after)[0] for k in members]
                which = (2,)
            else:
                landed = gather_wait("gather_wait_" + members[0], [bufs[k] for k in members], *sems[members[0]], after)
                which = (0, 1, 2)
            bufs.update(zip(members, gather_forward("gather_forward_" + members[0], landed, which)))
            ready.update(members)
        g = bufs[n]
        return g if n in COL_SHARDED else g.reshape(g.shape[0] * g.shape[1], g.shape[2])

    swaps, operands, inflight = {}, {}, []
    other_arr = 1 - c_arr

    def put_g(n, a, b, dep):
        operands[n] = (a, b)
        sent = mm_tn_half("d%s_sent" % n, a, b, n in COL_SHARDED, other_arr, dep=dep)
        swaps[n] = sibling_start("swap_start_" + n, [sent], False)
        for gi, members in enumerate(REDUCE_GROUPS):
            if n == members[-1]:
                last = swaps[n][4]
                pair = []
                for k in members:
                    send, recv, srcs, lands, _ = swaps[k]
                    _, (got,) = sibling_wait("swap_wait_" + k, srcs, lands, send, recv, False, last)
                    pair.append(mm_tn_half("d%s_kept" % k, *operands[k], k in COL_SHARDED, c_arr, addend=got))
                send, recv, parts, lands, tok = scatter_start("scatter_start_%d" % gi, pair)
                inflight.append((members, send, recv, parts, lands))
                return tok
        return swaps[n][4]

    small = {n: w[n].reshape(1, -1) for n in SMALL}
    pos = positions.reshape(L, 1).astype(f32)
    d_in = N_CHIPS * w["w_in"].shape[2]
    loss, grad_x, small_grads = local_step(x[0], pos, loss_target[0], small, d_in, get_w, put_g, first_dep=token, tick=tick)

    shapes = [w[n].shape for n in SMALL]
    blocks = into_slot("small_block", _pack([small_grads[n] for n in SMALL] + [loss]), me_arr, N_DEV, f32)
    small_send, small_recv, blocks, after = everyone_start("small_start", blocks)

    grads, delta, new_m, new_v = {}, {}, {}, {}
    for gi, (members, send, recv, parts, lands) in enumerate(inflight):
        parts, landed = scatter_wait("scatter_wait_%d" % gi, parts, lands, send, recv, after)
        joins, dep = [], None
        for k, p, t in zip(members, parts, landed):
            joins.append(sibling_start("join_start_" + k, [chip_sum("chip_sum_" + k, p, t, mine_arr, dep=dep)], False))
            dep = after = joins[-1][4]
        for n, (send, recv, srcs, lands, _) in zip(members, joins):
            (own,), (sib,) = sibling_wait("join_wait_" + n, srcs, lands, send, recv, False, after)
            g_, d_, m_, v_ = adamw_halves("adamw_" + n, w[n][0], own, sib, m[n][0], v[n][0], c_arr)
            grads[n], delta[n], new_m[n], new_v[n] = g_[None], d_[None], m_[None], v_[None]
            after = v_
    blocks = everyone_wait("small_wait", blocks, small_send, small_recv, after)
    small_sum = sum_slots("small_sum", blocks)
    *small_g, loss = _unpack(small_sum, [small_view(n, w[n]).shape for n in SMALL] + [loss.shape])
    loss = loss[0, 0]
    outs = adamw_many("adamw_small", [small_view(n, w[n]) for n in SMALL], small_g,
                      [small_view(n, m[n]) for n in SMALL], [small_view(n, v[n]) for n in SMALL])
    for t, parts in zip((grads, delta, new_m, new_v), (small_g,) + tuple(outs)):
        t.update({n: small_unview(n, p, w[n].shape) for n, p in zip(SMALL, parts)})

    return (loss, grad_x[None], *[grads[n] for n in WEIGHTS], *[delta[n] for n in WEIGHTS],
            *[new_m[n] for n in WEIGHTS], *[new_v[n] for n in WEIGHTS])
```

```python
import functools
import math

import jax
import jax.numpy as jnp
import numpy as np
from jax import lax
from jax.experimental import pallas as pl
from jax.experimental.pallas import tpu as pltpu

f32 = jnp.float32
bf16 = jnp.bfloat16
HIGHEST = lax.Precision.HIGHEST
MESH = pl.DeviceIdType.MESH

HEAD_DIM = 64
N_KV_HEADS = 4
ATTN_BLOCK = 128
ROPE_THETA = 10000.0
SSM_GROUP = 16
SSM_STATE = 64
RMS_EPS = 1e-6
LANES = 128
SUBLANES = 8
VMEM_LIMIT = 52 * 1024 * 1024
N_CHIPS = 4
N_DEV = 8
NEG = -1e30

ADAM_LR, ADAM_B1, ADAM_B2, ADAM_EPS, ADAM_WD, ADAM_STEP = 0.001, 0.9, 0.999, 1e-08, 0.01, 10

NN = (((1,), (0,)), ((), ()))
NT = (((1,), (1,)), ((), ()))
TN = (((0,), (0,)), ((), ()))


def _params(*sem):
    return pltpu.CompilerParams(dimension_semantics=sem or None, vmem_limit_bytes=VMEM_LIMIT)


def _dot(a, b, dims=NN):
    return lax.dot_general(a, b, dims, preferred_element_type=f32)


def _pick(dim, pref):
    t = min(dim, pref)
    while dim % t:
        t -= LANES
    assert t > 0, (dim, pref)
    return t


ANY = pl.BlockSpec(memory_space=pl.ANY)


def _with_dep(in_specs, operands, dep):
    if dep is None:
        return list(in_specs), list(operands), 0
    return list(in_specs) + [ANY], list(operands) + [dep], 1


def _mm_call(name, grid, in_specs, out_spec, out_shape, acc_shape, dims, operands, dep=None):
    nk = grid[2]
    in_specs, operands, n_dep = _with_dep(in_specs, operands, dep)

    def body_one(a_ref, b_ref, *rest):
        o_ref = rest[n_dep]
        o_ref[...] = _dot(a_ref[...], b_ref[...], dims).astype(o_ref.dtype)

    def body(a_ref, b_ref, *rest):
        o_ref, acc_ref = rest[n_dep], rest[n_dep + 1]
        k = pl.program_id(2)

        @pl.when(k == 0)
        def _():
            acc_ref[...] = _dot(a_ref[...], b_ref[...], dims)

        @pl.when((k > 0) & (k < nk - 1))
        def _():
            acc_ref[...] += _dot(a_ref[...], b_ref[...], dims)

        @pl.when(k == nk - 1)
        def _():
            o_ref[...] = (acc_ref[...] + _dot(a_ref[...], b_ref[...], dims)).astype(o_ref.dtype)

    return pl.pallas_call(
        body_one if nk == 1 else body, out_shape=out_shape, grid=grid, in_specs=in_specs, out_specs=out_spec,
        scratch_shapes=[] if nk == 1 else [pltpu.VMEM(acc_shape, f32)], name=name,
        compiler_params=_params("parallel", "parallel", "arbitrary"))(*operands)


def mm_nt_pair(name, a1, b1, a2, b2, tm=1024, tk=1024, dep=None):
    M = a1.shape[0]
    S, K, n = b1.shape
    tm, tko = _pick(M, tm), _pick(K, tk)
    nk = 2 * S

    def body(a1_ref, b1_ref, a2_ref, b2_ref, *rest):
        o_ref, acc_ref = rest[-2], rest[-1]
        k = pl.program_id(2)

        @pl.when(k == 0)
        def _():
            acc_ref[...] = _dot(a1_ref[...], b1_ref[...], NT)

        @pl.when((k > 0) & (k < S))
        def _():
            acc_ref[...] += _dot(a1_ref[...], b1_ref[...], NT)

        @pl.when((k >= S) & (k < nk - 1))
        def _():
            acc_ref[...] += _dot(a2_ref[...], b2_ref[...], NT)

        @pl.when(k == nk - 1)
        def _():
            o_ref[...] = acc_ref[...] + _dot(a2_ref[...], b2_ref[...], NT)

    first = lambda k: jnp.minimum(k, S - 1)
    second = lambda k: jnp.maximum(k - S, 0)
    in_specs = [pl.BlockSpec((tm, n), lambda i, j, k: (i, first(k))), pl.BlockSpec((None, tko, n), lambda i, j, k: (first(k), j, 0)),
                pl.BlockSpec((tm, n), lambda i, j, k: (i, second(k))), pl.BlockSpec((None, tko, n), lambda i, j, k: (second(k), j, 0))]
    in_specs, operands, _ = _with_dep(in_specs, (a1, b1, a2, b2), dep)
    return pl.pallas_call(
        body, out_shape=jax.ShapeDtypeStruct((M, K), f32), grid=(M // tm, K // tko, nk), in_specs=in_specs,
        out_specs=pl.BlockSpec((tm, tko), lambda i, j, k: (i, j)), scratch_shapes=[pltpu.VMEM((tm, tko), f32)], name=name,
        compiler_params=_params("parallel", "parallel", "arbitrary"))(*operands)


def mm_nn(name, a, b, out_dtype=f32, tm=1024, tn=1024, tk=2048, dep=None):
    M, K = a.shape
    tm, tk = _pick(M, tm), _pick(K, tk)
    if b.ndim == 3:
        S, _, n = b.shape
        tn = _pick(n, 2048)
        per = n // tn
        b_spec = pl.BlockSpec((None, tk, tn), lambda i, j, k: (j // per, k, j % per))
        N = S * n
    else:
        N = b.shape[1]
        tn = _pick(N, tn)
        b_spec = pl.BlockSpec((tk, tn), lambda i, j, k: (k, j))
    grid = (M // tm, N // tn, K // tk)
    return _mm_call(name, grid, [pl.BlockSpec((tm, tk), lambda i, j, k: (i, k)), b_spec],
                    pl.BlockSpec((tm, tn), lambda i, j, k: (i, j)), jax.ShapeDtypeStruct((M, N), out_dtype),
                    (tm, tn), NN, (a, b), dep)


def mm_nt(name, a, b, out_dtype=f32, tm=1024, tn=2048, tk=1024, dep=None):
    M, N = a.shape
    tm = _pick(M, tm)
    if b.ndim == 3:
        S, K, n = b.shape
        tr = _pick(n, 2048)
        per = n // tr
        tko = _pick(K, tk)
        b_spec = pl.BlockSpec((None, tko, tr), lambda i, j, k: (k // per, j, k % per))
    else:
        K = b.shape[0]
        tr = _pick(N, tn)
        tko = _pick(K, tk)
        b_spec = pl.BlockSpec((tko, tr), lambda i, j, k: (j, k))
    grid = (M // tm, K // tko, N // tr)
    return _mm_call(name, grid, [pl.BlockSpec((tm, tr), lambda i, j, k: (i, k)), b_spec],
                    pl.BlockSpec((tm, tko), lambda i, j, k: (i, j)), jax.ShapeDtypeStruct((M, K), out_dtype),
                    (tm, tko), NT, (a, b), dep)


def mm_tn(name, a, b, shards=None, out_dtype=f32, tm=1024, tn=1024, tl=2048, dep=None):
    L, K = a.shape
    N = b.shape[1]
    tl, tko = _pick(L, tl), _pick(K, tm)
    if shards:
        n = N // shards
        tn = _pick(n, 2048)
        per = n // tn
        o_spec = pl.BlockSpec((None, tko, tn), lambda i, j, k: (j // per, i, j % per))
        o_shape = jax.ShapeDtypeStruct((shards, K, n), out_dtype)
    else:
        tn = _pick(N, tn)
        o_spec = pl.BlockSpec((tko, tn), lambda i, j, k: (i, j))
        o_shape = jax.ShapeDtypeStruct((K, N), out_dtype)
    grid = (K // tko, N // tn, L // tl)
    return _mm_call(name, grid, [pl.BlockSpec((tl, tko), lambda i, j, k: (k, i)),
                                 pl.BlockSpec((tl, tn), lambda i, j, k: (k, j))],
                    o_spec, o_shape, (tko, tn), TN, (a, b), dep)


def mm_tn_half(name, a, b, col_sharded, half_arr, addend=None, dep=None, tn=2048):
    L, K = a.shape
    N = b.shape[1]
    S = N_CHIPS
    h, cols = (K // 2, N // S) if col_sharded else (K // S // 2, N)
    tko, tn = _pick(h, 1024), _pick(cols, tn)
    ni, nj = h // tko, cols // tn
    if col_sharded:
        a_map = lambda s, i, j, hf: (0, hf[0] * ni + i)
        b_map = lambda s, i, j, hf: (0, s * nj + j)
    else:
        a_map = lambda s, i, j, hf: (0, (2 * s + hf[0]) * ni + i)
        b_map = lambda s, i, j, hf: (0, j)
    o_spec = pl.BlockSpec((None, tko, tn), lambda s, i, j, hf: (s, i, j))
    n_add = 0 if addend is None else 1
    n_dep = 0 if dep is None else 1

    def body(hf_ref, a_ref, b_ref, *rest):
        o_ref = rest[n_add + n_dep]
        acc = _dot(a_ref[...], b_ref[...], TN)
        if n_add:
            acc = acc + rest[0][...].astype(f32)
        o_ref[...] = acc.astype(o_ref.dtype)

    in_specs = [pl.BlockSpec((L, tko), a_map), pl.BlockSpec((L, tn), b_map)] + [o_spec] * n_add
    in_specs, operands, _ = _with_dep(in_specs, [a, b] + ([addend] if n_add else []), dep)
    grid_spec = pltpu.PrefetchScalarGridSpec(num_scalar_prefetch=1, grid=(S, ni, nj), in_specs=in_specs, out_specs=o_spec)
    return pl.pallas_call(body, out_shape=jax.ShapeDtypeStruct((S, h, cols), bf16), grid_spec=grid_spec, name=name,
                          compiler_params=_params("parallel", "parallel", "parallel"))(half_arr, *operands)


def weight_grad(name, a, b, dep=None):
    if name in COL_SHARDED:
        return mm_tn("d" + name, a, b, shards=N_CHIPS, out_dtype=bf16, dep=dep)
    return mm_tn("d" + name, a, b, out_dtype=bf16, tm=a.shape[1] // N_CHIPS, dep=dep)


def ffn_hidden(hn, w_gate, w_up, tm=512):
    M, K = hn.shape
    S, _, n = w_gate.shape
    tm = _pick(M, tm)

    def body(a_ref, g_ref, u_ref, dg_ref, du_ref, hid_ref):
        a = a_ref[...]
        g = _dot(a, g_ref[...])
        u = _dot(a, u_ref[...])
        sg = sigmoid(g)
        act = g * sg
        dg_ref[...] = (u * (sg * (1.0 + g * (1.0 - sg)))).astype(bf16)
        du_ref[...] = act.astype(bf16)
        hid_ref[...] = (act * u).astype(bf16)

    w_spec = pl.BlockSpec((None, K, n), lambda s, i: (s, 0, 0))
    o_spec = pl.BlockSpec((tm, n), lambda s, i: (i, s))
    o = jax.ShapeDtypeStruct((M, S * n), bf16)
    return pl.pallas_call(
        body, out_shape=[o, o, o], grid=(S, M // tm), in_specs=[pl.BlockSpec((tm, K), lambda s, i: (i, 0)), w_spec, w_spec],
        out_specs=[o_spec, o_spec, o_spec], name="ffn_hidden", compiler_params=_params("parallel", "parallel"))(hn, w_gate, w_up)


def ffn_hidden_grad(d_ff, w_down, hid_dg, hid_du, tm=1024):
    M, D = d_ff.shape
    F = w_down.shape[0]
    n = _pick(F // N_CHIPS, 2048)
    tm = _pick(M, tm)

    def body(a_ref, b_ref, pg_ref, pu_ref, dg_ref, du_ref):
        dh = _dot(a_ref[...], b_ref[...], NT)
        dg_ref[...] = (dh * pg_ref[...].astype(f32)).astype(bf16)
        du_ref[...] = (dh * pu_ref[...].astype(f32)).astype(bf16)

    t_spec = pl.BlockSpec((tm, n), lambda j, i: (i, j))
    o = jax.ShapeDtypeStruct((M, F), bf16)
    return pl.pallas_call(
        body, out_shape=[o, o], grid=(F // n, M // tm),
        in_specs=[pl.BlockSpec((tm, D), lambda j, i: (i, 0)), pl.BlockSpec((n, D), lambda j, i: (j, 0)), t_spec, t_spec],
        out_specs=[t_spec, t_spec], name="ffn_hidden_grad", compiler_params=_params("parallel", "parallel"))(d_ff, w_down, hid_dg, hid_du)


def rowwise(name, fn, L, rows, bcast, outs, reds=(), tr=256, dep=None):
    tr = min(tr, L)
    nt = L // tr
    n_rows, n_b, n_o = len(rows), len(bcast), len(outs)
    n_dep = 0 if dep is None else 1

    def body(*refs):
        i = pl.program_id(0)
        ins = [r[...] for r in refs[:n_rows + n_b]]
        res = fn(i, nt, *ins)
        o_refs = refs[n_rows + n_b + n_dep:]
        for k in range(n_o):
            o_refs[k][...] = res[k].astype(o_refs[k].dtype)
        if reds:
            @pl.when(i == 0)
            def _():
                for k in range(len(reds)):
                    o_refs[n_o + k][...] = jnp.zeros_like(o_refs[n_o + k])
            for k in range(len(reds)):
                o_refs[n_o + k][...] += res[n_o + k]

    def row_spec(width, cb, shift):
        if shift:
            return pl.BlockSpec((tr, width), lambda i: (jnp.minimum(i + shift, nt - 1), cb))
        return pl.BlockSpec((tr, width), lambda i: (i, cb))

    in_specs = [row_spec(w, cb, sh) for (_, w, cb, sh) in rows]
    in_specs += [pl.BlockSpec(b.shape, lambda i: (0, 0)) for b in bcast]
    out_specs = [pl.BlockSpec((tr, w), lambda i: (i, 0)) for (w, _) in outs]
    out_specs += [pl.BlockSpec((1, w), lambda i: (0, 0)) for w in reds]
    out_shape = [jax.ShapeDtypeStruct((L, w), dt) for (w, dt) in outs]
    out_shape += [jax.ShapeDtypeStruct((1, w), f32) for w in reds]
    in_specs, operands, _ = _with_dep(in_specs, [r[0] for r in rows] + list(bcast), dep)
    return pl.pallas_call(
        body, out_shape=out_shape, grid=(nt,), in_specs=in_specs, out_specs=out_specs, name=name,
        compiler_params=_params("arbitrary"))(*operands)


def full(a):
    return (a, a.shape[1], 0, 0)


def colsum(v):
    return jnp.sum(v, axis=0, keepdims=True)


def rms_fwd(x, g):
    r = lax.rsqrt(jnp.mean(x * x, axis=-1, keepdims=True) + RMS_EPS)
    return x * r * g


def rms_bwd(x, g, dy):
    r = lax.rsqrt(jnp.mean(x * x, axis=-1, keepdims=True) + RMS_EPS)
    xh = x * r
    dyg = dy * g
    dx = r * (dyg - xh * jnp.mean(dyg * xh, axis=-1, keepdims=True))
    return dx, colsum(dy * xh)


GELU_C = math.sqrt(2.0 / math.pi)


def gelu(y):
    return y * (0.5 * (1.0 + jnp.tanh(GELU_C * (y + 0.044715 * (y * y * y)))))


def gelu_grad(y):
    t = jnp.tanh(GELU_C * (y + 0.044715 * (y * y * y)))
    return 0.5 * (1.0 + t) + 0.5 * y * (1.0 - t * t) * (GELU_C * (1.0 + 3 * 0.044715 * (y * y)))


def sigmoid(v):
    return 1.0 / (1.0 + jnp.exp(-v))


def _lane(shape):
    return lax.broadcasted_iota(jnp.int32, shape, 1)


def _rot_chunk(t, cos, sin_signed):
    first = (_lane(t.shape) % HEAD_DIM) < (HEAD_DIM // 2)
    partner = jnp.where(first, pltpu.roll(t, LANES - HEAD_DIM // 2, 1), pltpu.roll(t, HEAD_DIM // 2, 1))
    return t * cos + partner * sin_signed


def _cos_sin(pos, inv_freq, inverse):
    ang = pos * inv_freq
    cos, sin = jnp.cos(ang), jnp.sin(ang)
    first = (_lane(ang.shape) % HEAD_DIM) < (HEAD_DIM // 2)
    sign = jnp.where(first, -1.0, 1.0) * (-1.0 if inverse else 1.0)
    return cos, sin * sign


def _dup_head(chunk, odd):
    low = _lane(chunk.shape) < HEAD_DIM
    x = jnp.where(low != odd, chunk, 0.0)
    return x + pltpu.roll(x, HEAD_DIM, 1)


def _chunks(v):
    return [v[:, LANES * c:LANES * (c + 1)] for c in range(v.shape[1] // LANES)]


def qkv_prep(proj, pos, inv_freq, d_attn, d_kv):
    L = proj.shape[0]
    d_ssm = proj.shape[1] - d_attn - 2 * d_kv
    half = d_ssm // 2
    scale = 1.0 / math.sqrt(HEAD_DIM)

    def fn(i, nt, q, k, v, u0, u1, p, invf):
        cos, sin = _cos_sin(p, invf, False)
        qr = jnp.concatenate([_rot_chunk(c, cos, sin) for c in _chunks(q)], axis=1) * scale
        kr = [_rot_chunk(c, cos, sin) for c in _chunks(k)]
        kk = jnp.concatenate([_dup_head(c, odd) for c in kr for odd in (False, True)], axis=1)
        vv = jnp.concatenate([_dup_head(c, odd) for c in _chunks(v) for odd in (False, True)], axis=1)
        return qr, kk, vv, jnp.concatenate([u0, u1], axis=1)

    u_cb = (d_attn + 2 * d_kv) // half
    return rowwise("qkv_prep", fn, L,
                   [(proj, d_attn, 0, 0), (proj, d_kv, d_attn // d_kv, 0), (proj, d_kv, d_attn // d_kv + 1, 0),
                    (proj, half, u_cb, 0), (proj, half, u_cb + 1, 0), full(pos)],
                   [inv_freq], [(d_attn, bf16), (2 * d_kv, bf16), (2 * d_kv, bf16), (d_ssm, bf16)])


def qkv_grad(dq, dkk_c, dkk_p, dvv_c, dvv_p, du, pos, inv_freq):
    L, d_attn = dq.shape
    d_kv = dkk_c.shape[1] // 2
    scale = 1.0 / math.sqrt(HEAD_DIM)

    def fold(cur, prev, i, nt):
        t = cur + jnp.where(i < nt - 1, prev, 0.0)
        out = []
        for c in range(t.shape[1] // (2 * LANES)):
            even, odd = t[:, 2 * c * LANES:(2 * c + 1) * LANES], t[:, (2 * c + 1) * LANES:(2 * c + 2) * LANES]
            even, odd = even + pltpu.roll(even, HEAD_DIM, 1), odd + pltpu.roll(odd, HEAD_DIM, 1)
            out.append(jnp.where(_lane(even.shape) < HEAD_DIM, even, odd))
        return out

    def fn(i, nt, dq_t, kc, kp, vc, vp, du_t, p, invf):
        cos, sin = _cos_sin(p, invf, True)
        dq_o = jnp.concatenate([_rot_chunk(c, cos, sin) for c in _chunks(dq_t)], axis=1) * scale
        dk_o = jnp.concatenate([_rot_chunk(c, cos, sin) for c in fold(kc, kp, i, nt)], axis=1)
        dv_o = jnp.concatenate(fold(vc, vp, i, nt), axis=1)
        return (jnp.concatenate([dq_o, dk_o, dv_o, du_t], axis=1),)

    return rowwise("qkv_grad", fn, L,
                   [full(dq), full(dkk_c), (dkk_p, 2 * d_kv, 0, 1), full(dvv_c), (dvv_p, 2 * d_kv, 0, 1), full(du), full(pos)],
                   [inv_freq], [(d_attn + 2 * d_kv + du.shape[1], bf16)], tr=ATTN_BLOCK)[0]


def _attn_specs(L):
    nb = L // ATTN_BLOCK
    B = ATTN_BLOCK
    q_spec = lambda width: pl.BlockSpec((B, width), lambda n: (n, 0))
    prev = lambda width: pl.BlockSpec((B, width), lambda n: (jnp.maximum(n - 1, 0), 0))
    return nb, q_spec, prev


def _attn_mask(n):
    B = ATTN_BLOCK
    row = lax.broadcasted_iota(jnp.int32, (B, 2 * B), 0)
    col = lax.broadcasted_iota(jnp.int32, (B, 2 * B), 1)
    return ((col < B) & (col > row) & (n > 0)) | ((col >= B) & (row >= col - B))


def _attn_probs(qm, kcat, sink, mask):
    s = jnp.where(mask, _dot(qm, kcat, NT), NEG)
    m = jnp.maximum(jnp.max(s, axis=1, keepdims=True), sink)
    p, ps = jnp.exp(s - m), jnp.exp(sink - m)
    inv = 1.0 / (jnp.sum(p, axis=1, keepdims=True) + ps)
    return p, inv, ps


def _attn_heads(q_ref, s_ref, h, q_per_kv):
    low = _lane((ATTN_BLOCK, LANES)) < HEAD_DIM
    heads = []
    for pr in range(h * q_per_kv // 2, (h + 1) * q_per_kv // 2):
        q2 = q_ref[:, LANES * pr:LANES * (pr + 1)]
        for odd in (False, True):
            mine = low != odd
            sink = jnp.max(s_ref[2 * pr + int(odd):2 * pr + int(odd) + 1, :], axis=1, keepdims=True)
            heads.append((pr, mine, jnp.where(mine, q2, jnp.zeros_like(q2)), sink))
    return low, heads


def _kv_block(prev_ref, cur_ref, h):
    return jnp.concatenate([prev_ref[:, LANES * h:LANES * (h + 1)], cur_ref[:, LANES * h:LANES * (h + 1)]], axis=0)


def attn_fwd(qr, kk, vv, sink_b):
    L, d_attn = qr.shape
    nb, q_spec, prev = _attn_specs(L)
    d_kk = kk.shape[1]
    n_kv = d_kk // LANES
    q_per_kv = d_attn // HEAD_DIM // n_kv

    def body(q_ref, kc_ref, kp_ref, vc_ref, vp_ref, s_ref, o_ref):
        mask = _attn_mask(pl.program_id(0))
        for h in range(n_kv):
            kcat, vcat = _kv_block(kp_ref, kc_ref, h), _kv_block(vp_ref, vc_ref, h)
            low, heads = _attn_heads(q_ref, s_ref, h, q_per_kv)
            probs = [_attn_probs(qm, kcat, sink, mask) for (_, _, qm, sink) in heads]
            outs = [_dot(p.astype(bf16), vcat) * inv for (p, inv, _) in probs]
            for i in range(0, len(heads), 2):
                pr = heads[i][0]
                o_ref[:, LANES * pr:LANES * (pr + 1)] = jnp.where(low, outs[i], outs[i + 1])

    return pl.pallas_call(
        body, out_shape=jax.ShapeDtypeStruct((L, d_attn), f32), grid=(nb,),
        in_specs=[q_spec(d_attn), q_spec(d_kk), prev(d_kk), q_spec(d_kk), prev(d_kk), pl.BlockSpec(sink_b.shape, lambda n: (0, 0))],
        out_specs=q_spec(d_attn), name="attn_fwd", compiler_params=_params("arbitrary"))(qr, kk, kk, vv, vv, sink_b)


def attn_bwd(qr, kk, vv, sink_b, attn, d_attn_out):
    L, d_attn = qr.shape
    nb, q_spec, prev = _attn_specs(L)
    d_kk = kk.shape[1]
    n_kv = d_kk // LANES
    q_per_kv = d_attn // HEAD_DIM // n_kv

    def body(q_ref, kc_ref, kp_ref, vc_ref, vp_ref, s_ref, o_ref, do_ref, dq_ref, dkc_ref, dkp_ref, dvc_ref, dvp_ref, ds_ref):
        n = pl.program_id(0)
        B = ATTN_BLOCK
        mask = _attn_mask(n)
        srow = lax.broadcasted_iota(jnp.int32, (SUBLANES, LANES), 0)

        @pl.when(n == 0)
        def _():
            ds_ref[...] = jnp.zeros_like(ds_ref)

        for h in range(n_kv):
            kcat, vcat = _kv_block(kp_ref, kc_ref, h), _kv_block(vp_ref, vc_ref, h)
            low, heads = _attn_heads(q_ref, s_ref, h, q_per_kv)
            probs = [_attn_probs(qm, kcat, sink, mask) for (_, _, qm, sink) in heads]
            dk = jnp.zeros((2 * B, LANES), f32)
            dv = dk
            dsink = jnp.zeros((SUBLANES, LANES), f32)
            dqs = []
            for i, ((pr, mine, qm, _), (p, inv, ps)) in enumerate(zip(heads, probs)):
                do2 = do_ref[:, LANES * pr:LANES * (pr + 1)]
                delta = jnp.sum(jnp.where(mine, do2 * o_ref[:, LANES * pr:LANES * (pr + 1)], 0.0), axis=1, keepdims=True)
                dob = jnp.where(mine, do2, 0.0).astype(bf16)
                p = p * inv
                ds = (p * (_dot(dob, vcat, NT) - delta)).astype(bf16)
                dqs.append(_dot(ds, kcat))
                dk = dk + _dot(ds, qm, TN)
                dv = dv + _dot(p.astype(bf16), dob, TN)
                dsink = dsink + jnp.where(srow == i, -jnp.sum(ps * inv * delta), 0.0)
            for i in range(0, len(heads), 2):
                pr = heads[i][0]
                dq_ref[:, LANES * pr:LANES * (pr + 1)] = jnp.where(low, dqs[i], dqs[i + 1])
            cols = slice(LANES * h, LANES * (h + 1))
            dkp_ref[:, cols] = dk[:B]
            dkc_ref[:, cols] = dk[B:]
            dvp_ref[:, cols] = dv[:B]
            dvc_ref[:, cols] = dv[B:]
            ds_ref[h] += dsink

    kv_shape = jax.ShapeDtypeStruct(kk.shape, f32)
    ds_shape = (n_kv, SUBLANES, LANES)
    return pl.pallas_call(
        body,
        out_shape=[jax.ShapeDtypeStruct((L, d_attn), f32), kv_shape, kv_shape, kv_shape, kv_shape, jax.ShapeDtypeStruct(ds_shape, f32)],
        grid=(nb,),
        in_specs=[q_spec(d_attn), q_spec(d_kk), prev(d_kk), q_spec(d_kk), prev(d_kk), pl.BlockSpec(sink_b.shape, lambda n: (0, 0)),
                  q_spec(d_attn), q_spec(d_attn)],
        out_specs=[q_spec(d_attn)] + [q_spec(d_kk)] * 4 + [pl.BlockSpec(ds_shape, lambda n: (0, 0, 0))],
        name="attn_bwd", compiler_params=_params("arbitrary"))(qr, kk, kk, vv, vv, sink_b, attn, d_attn_out)


SSM_T = 128
NQ = SUBLANES * SSM_STATE // LANES
NJ = SUBLANES


SCAN_UNROLL = 8


def _unrolled(step):
    def body(k, carry):
        for u in range(SCAN_UNROLL):
            carry = step(k * SCAN_UNROLL + u, carry)
        return carry
    return body


def _strided_put(ref, j, val):
    for q in range(NQ):
        ref.at[q][pl.ds(j, SSM_T, stride=NJ), :] = val[:, LANES * q:LANES * (q + 1)]


def _strided_get(ref, j):
    return jnp.concatenate([ref.at[q][pl.ds(j, SSM_T, stride=NJ), :] for q in range(NQ)], axis=1)


def _ssm_specs(L, rev):
    nt = L // SSM_T
    idx = (lambda i: nt - 1 - i) if rev else (lambda i: i)
    row = lambda w, cb=0: pl.BlockSpec((SSM_T, w), lambda i: (idx(i), cb))
    state = pl.BlockSpec((NQ, SSM_T * NJ, LANES), lambda i: (0, idx(i), 0))
    whole = lambda a: pl.BlockSpec(a.shape, lambda i: (0,) * a.ndim)
    return nt, row, state, whole


def ssm_fwd(u_bf, proj, u_cb, bd_re, bd_im, cd_re, cd_im, lam_re, lam_im, d_skip, dep=None):
    L, d_ssm = u_bf.shape
    nt, row, state, whole = _ssm_specs(L, False)
    half = d_ssm // 2
    gw = d_ssm // NJ
    n_dep = 0 if dep is None else 1

    def body(u_ref, u0_ref, u1_ref, bdr, bdi, cdr, cdi, lr_ref, li_ref, d_ref, *rest):
        y_ref, z_ref, sr_ref, si_ref, carry = rest[n_dep:]
        i = pl.program_id(0)

        @pl.when(i == 0)
        def _():
            carry[...] = jnp.zeros_like(carry)

        for j in range(NJ):
            uj = u_ref[:, gw * j:gw * (j + 1)]
            _strided_put(sr_ref, j, _dot(uj, bdr[j]))
            _strided_put(si_ref, j, _dot(uj, bdi[j]))
        lr = [lr_ref[q] for q in range(NQ)]
        li = [li_ref[q] for q in range(NQ)]

        def step(t, s):
            sr, si = s
            rows = pl.ds(pl.multiple_of(t * NJ, NJ), NJ)
            nr = tuple(lr[q] * sr[q] - li[q] * si[q] + sr_ref[q, rows, :] for q in range(NQ))
            ni = tuple(lr[q] * si[q] + li[q] * sr[q] + si_ref[q, rows, :] for q in range(NQ))
            for q in range(NQ):
                sr_ref[q, rows, :] = nr[q]
                si_ref[q, rows, :] = ni[q]
            return nr, ni

        init = (tuple(carry[0, q] for q in range(NQ)), tuple(carry[1, q] for q in range(NQ)))
        sr, si = lax.fori_loop(0, SSM_T // SCAN_UNROLL, _unrolled(step), init)
        for q in range(NQ):
            carry[0, q] = sr[q]
            carry[1, q] = si[q]
        uf = jnp.concatenate([u0_ref[...], u1_ref[...]], axis=1)
        for j in range(NJ):
            cols = slice(gw * j, gw * (j + 1))
            yj = _dot(_strided_get(sr_ref, j).astype(bf16), cdr[j]) - _dot(_strided_get(si_ref, j).astype(bf16), cdi[j])
            yj = yj + d_ref[:, cols] * uf[:, cols]
            y_ref[:, cols] = yj
            z_ref[:, cols] = gelu(yj).astype(bf16)

    s_shape = jax.ShapeDtypeStruct((NQ, L * NJ, LANES), f32)
    consts = (bd_re, bd_im, cd_re, cd_im, lam_re, lam_im, d_skip)
    in_specs, operands, _ = _with_dep([row(d_ssm), row(half, u_cb), row(half, u_cb + 1)] + [whole(a) for a in consts],
                                      [u_bf, proj, proj, *consts], dep)
    return pl.pallas_call(
        body, out_shape=[jax.ShapeDtypeStruct((L, d_ssm), f32), jax.ShapeDtypeStruct((L, d_ssm), bf16), s_shape, s_shape], grid=(nt,),
        in_specs=in_specs, out_specs=[row(d_ssm), row(d_ssm), state, state],
        scratch_shapes=[pltpu.VMEM((2, NQ, NJ, LANES), f32)], name="ssm_fwd",
        compiler_params=_params("arbitrary"))(*operands)


def ssm_bwd(y, dz1, dz2, u_bf, proj, u_cb, s_re, s_im, bd_re, bd_im, cd_re, cd_im, lam_re, lam_im, d_skip, dep=None):
    L, d_ssm = y.shape
    nt, row, state, whole = _ssm_specs(L, True)
    half = d_ssm // 2
    gw = d_ssm // NJ
    n_dep = 0 if dep is None else 1

    def body(y_ref, dz1_ref, dz2_ref, u_ref, u0_ref, u1_ref, sr_ref, si_ref, bdr, bdi, cdr, cdi, lr_ref, li_ref, d_ref, *rest):
        du_ref, fbr, fbi, fcr, fci, dlr, dli, dd_ref, gr_ref, gi_ref, carry, dbdr, dbdi, dcdr, dcdi = rest[n_dep:]
        i = pl.program_id(0)

        @pl.when(i == 0)
        def _():
            carry[...] = jnp.zeros_like(carry)
            for r in (dbdr, dbdi, dcdr, dcdi, dlr, dli, dd_ref):
                r[...] = jnp.zeros_like(r)

        dyf = (dz1_ref[...] + dz2_ref[...]) * gelu_grad(y_ref[...])
        dyb = dyf.astype(bf16)
        for j in range(NJ):
            dyj = dyb[:, gw * j:gw * (j + 1)]
            _strided_put(gr_ref, j, _dot(dyj, cdr[j], NT))
            _strided_put(gi_ref, j, -_dot(dyj, cdi[j], NT))
            dcdr[j] += _dot(dyj, _strided_get(sr_ref, j).astype(bf16), TN)
            dcdi[j] -= _dot(dyj, _strided_get(si_ref, j).astype(bf16), TN)
        lr = [lr_ref[q] for q in range(NQ)]
        li = [li_ref[q] for q in range(NQ)]

        def step(k, c):
            gr, gi, ar, ai = c
            rows = pl.ds(pl.multiple_of((SSM_T - 1 - k) * NJ, NJ), NJ)
            s_r = [sr_ref[q, rows, :] for q in range(NQ)]
            s_i = [si_ref[q, rows, :] for q in range(NQ)]
            ar = tuple(ar[q] + gr[q] * s_r[q] + gi[q] * s_i[q] for q in range(NQ))
            ai = tuple(ai[q] + gi[q] * s_r[q] - gr[q] * s_i[q] for q in range(NQ))
            nr = tuple(gr_ref[q, rows, :] + lr[q] * gr[q] + li[q] * gi[q] for q in range(NQ))
            ni = tuple(gi_ref[q, rows, :] + lr[q] * gi[q] - li[q] * gr[q] for q in range(NQ))
            for q in range(NQ):
                gr_ref[q, rows, :] = nr[q]
                gi_ref[q, rows, :] = ni[q]
            return nr, ni, ar, ai

        zero = tuple(jnp.zeros((NJ, LANES), f32) for _ in range(NQ))
        init = (tuple(carry[0, q] for q in range(NQ)), tuple(carry[1, q] for q in range(NQ)), zero, zero)
        gr, gi, ar, ai = lax.fori_loop(0, SSM_T // SCAN_UNROLL, _unrolled(step), init)
        for q in range(NQ):
            carry[0, q] = gr[q]
            carry[1, q] = gi[q]
            dlr[q] += ar[q]
            dli[q] += ai[q]
        uf = jnp.concatenate([u0_ref[...], u1_ref[...]], axis=1)
        dd_ref[...] += colsum(dyf * uf)
        for j in range(NJ):
            cols = slice(gw * j, gw * (j + 1))
            gjr, gji = _strided_get(gr_ref, j).astype(bf16), _strided_get(gi_ref, j).astype(bf16)
            du_ref[:, cols] = _dot(gjr, bdr[j], NT) + _dot(gji, bdi[j], NT) + d_ref[:, cols] * dyf[:, cols]
            uj = u_ref[:, cols]
            dbdr[j] += _dot(uj, gjr, TN)
            dbdi[j] += _dot(uj, gji, TN)

        @pl.when(i == nt - 1)
        def _():
            nb = NJ * SSM_STATE
            diag = (lax.broadcasted_iota(jnp.int32, (gw, nb), 0) // SSM_GROUP) == (lax.broadcasted_iota(jnp.int32, (gw, nb), 1) // SSM_STATE)
            for j in range(NJ):
                for acc_ref, out in ((dbdr, fbr), (dbdi, fbi), (dcdr, fcr), (dcdi, fci)):
                    m = jnp.where(diag, acc_ref[j], 0.0)
                    f = m[:, :LANES]
                    for q in range(1, nb // LANES):
                        f = f + m[:, LANES * q:LANES * (q + 1)]
                    out[j] = f + pltpu.roll(f, SSM_STATE, 1)

    consts = (bd_re, bd_im, cd_re, cd_im, lam_re, lam_im, d_skip)
    acc = lambda a: jax.ShapeDtypeStruct(a.shape, f32)
    fb = jax.ShapeDtypeStruct((NJ, gw, LANES), f32)
    outs = [jax.ShapeDtypeStruct((L, d_ssm), f32), fb, fb, fb, fb, acc(lam_re), acc(lam_im), acc(d_skip)]
    in_specs, operands, _ = _with_dep(
        [row(d_ssm)] * 4 + [row(half, u_cb), row(half, u_cb + 1), state, state] + [whole(a) for a in consts],
        [y, dz1, dz2, u_bf, proj, proj, s_re, s_im, *consts], dep)
    return pl.pallas_call(
        body, out_shape=outs, grid=(nt,),
        in_specs=in_specs, out_specs=[row(d_ssm)] + [whole(a) for a in outs[1:]],
        scratch_shapes=[pltpu.VMEM((NQ, SSM_T * NJ, LANES), f32), pltpu.VMEM((NQ, SSM_T * NJ, LANES), f32),
                        pltpu.VMEM((2, NQ, NJ, LANES), f32)] + [pltpu.VMEM(bd_re.shape, f32)] * 4,
        name="ssm_bwd", compiler_params=_params("arbitrary"))(*operands)


def _cmul(ar, ai, br, bi):
    return ar * br - ai * bi, ar * bi + ai * br


def _disc(ar, ai, logdt):
    dt = jnp.exp(logdt)
    mag = jnp.exp(ar * dt)
    lr, li = mag * jnp.cos(ai * dt), mag * jnp.sin(ai * dt)
    den = ar * ar + ai * ai
    nr, ni = lr - 1.0, li
    fr, fi = (nr * ar + ni * ai) / den, (ni * ar - nr * ai) / den
    return dt, lr, li, den, fr, fi


def ssm_params(a_re, a_im, logdt_b, bt_re, bt_im, spread):
    def body(ar_ref, ai_ref, ld_ref, br_ref, bi_ref, sp_ref, lr_ref, li_ref, or_ref, oi_ref):
        _, lr, li, _, fr, fi = _disc(ar_ref[...], ai_ref[...], ld_ref[...])
        lr_ref[...] = lr
        li_ref[...] = li
        fre = jnp.dot(sp_ref[...], fr, precision=HIGHEST, preferred_element_type=f32)
        fie = jnp.dot(sp_ref[...], fi, precision=HIGHEST, preferred_element_type=f32)
        o_r, o_i = _cmul(fre, fie, br_ref[...], bi_ref[...])
        or_ref[...] = o_r
        oi_ref[...] = o_i

    g = jax.ShapeDtypeStruct(a_re.shape, f32)
    b = jax.ShapeDtypeStruct(bt_re.shape, f32)
    return pl.pallas_call(body, out_shape=[g, g, b, b], name="ssm_params",
                          compiler_params=_params())(a_re, a_im, logdt_b, bt_re, bt_im, spread)


def ssm_params_grad(a_re, a_im, logdt_b, bt_re, bt_im, spread, gather, dlam_re, dlam_im, dbt_re, dbt_im):
    def body(ar_ref, ai_ref, ld_ref, br_ref, bi_ref, sp_ref, ga_ref, glr_ref, gli_ref, gbr_ref, gbi_ref,
             dar_ref, dai_ref, dld_ref, dbr_ref, dbi_ref):
        ar, ai = ar_ref[...], ai_ref[...]
        dt, lr, li, den, fr, fi = _disc(ar, ai, ld_ref[...])
        hdot = functools.partial(jnp.dot, precision=HIGHEST, preferred_element_type=f32)
        fre, fie = hdot(sp_ref[...], fr), hdot(sp_ref[...], fi)
        gbr, gbi, br, bi = gbr_ref[...], gbi_ref[...], br_ref[...], bi_ref[...]
        dbr_ref[...], dbi_ref[...] = _cmul(fre, -fie, gbr, gbi)
        t_r, t_i = _cmul(br, -bi, gbr, gbi)
        gfr, gfi = hdot(ga_ref[...], t_r), hdot(ga_ref[...], t_i)
        iwr, iwi = ar / den, -ai / den
        x_r, x_i = _cmul(iwr, -iwi, gfr, gfi)
        glr, gli = glr_ref[...] + x_r, gli_ref[...] + x_i
        q_r, q_i = _cmul(fr, fi, iwr, iwi)
        gwr, gwi = _cmul(-q_r, q_i, gfr, gfi)
        y_r, y_i = _cmul(dt * lr, -dt * li, glr, gli)
        dar_ref[...] = gwr + y_r
        dai_ref[...] = gwi + y_i
        wl_r, wl_i = _cmul(ar, ai, lr, li)
        z_r, _ = _cmul(wl_r, -wl_i, glr, gli)
        dld_ref[...] = jnp.sum(z_r * dt, axis=1, keepdims=True)

    g = jax.ShapeDtypeStruct(a_re.shape, f32)
    b = jax.ShapeDtypeStruct(bt_re.shape, f32)
    return pl.pallas_call(body, out_shape=[g, g, jax.ShapeDtypeStruct((a_re.shape[0], 1), f32), b, b], name="ssm_params_grad",
                          compiler_params=_params())(a_re, a_im, logdt_b, bt_re, bt_im, spread, gather, dlam_re, dlam_im, dbt_re, dbt_im)


def _block_diag(t, rows, cols):
    G = t.shape[0]
    t = t.reshape(G // NJ, NJ, rows, cols)
    eye = jnp.eye(NJ, dtype=t.dtype)
    return jnp.einsum('jgrc,gh->jgrhc', t, eye).reshape(G // NJ, NJ * rows, NJ * cols)


def _state_layout(t):
    return t.reshape(NJ, NQ, LANES).transpose(1, 0, 2)


def _state_layout_inv(t, G, N):
    return t.transpose(1, 0, 2).reshape(G, N)


def _tiles2d(shape, budget_rows=128):
    rows, cols = shape
    tr = rows
    if rows > budget_rows:
        tr = budget_rows
        while rows % tr:
            tr -= SUBLANES
    return tr, cols


ADAM_TILE_BYTES = 3 << 19
ROW_ALIGN = 16


def _tile_rows(rows, row_bytes, target_bytes):
    tr = max(ROW_ALIGN, min(rows, target_bytes // row_bytes) // ROW_ALIGN * ROW_ALIGN)
    while rows % tr:
        tr -= ROW_ALIGN
    return tr


def _adam_update(w, g, m, v):
    c1 = 1.0 - ADAM_B1 ** ADAM_STEP
    c2 = 1.0 - ADAM_B2 ** ADAM_STEP
    nm = ADAM_B1 * m + (1.0 - ADAM_B1) * g
    nv = ADAM_B2 * v + (1.0 - ADAM_B2) * (g * g)
    delta = -ADAM_LR * ((nm / c1) / (jnp.sqrt(nv / c2) + ADAM_EPS) + ADAM_WD * w)
    return delta, nm, nv


def adamw_many(name, ws, gs, ms, vs):
    n = len(ws)

    def body(*refs):
        w, g, m, v = (refs[k * n:(k + 1) * n] for k in range(4))
        d, nm, nv = (refs[(4 + k) * n:(5 + k) * n] for k in range(3))
        for i in range(n):
            d[i][...], nm[i][...], nv[i][...] = _adam_update(w[i][...], g[i][...], m[i][...], v[i][...])

    o = [jax.ShapeDtypeStruct(a.shape, f32) for a in ws]
    outs = pl.pallas_call(body, out_shape=o * 3, name=name, compiler_params=_params())(*ws, *gs, *ms, *vs)
    return outs[:n], outs[n:2 * n], outs[2 * n:]


def adamw_halves(name, w, own, got, m, v, c_arr):
    h, cols = own.shape
    tr = _tile_rows(h, cols * 4, ADAM_TILE_BYTES)
    nh = h // tr

    def body(c_ref, w_ref, own_ref, got_ref, m_ref, v_ref, g_ref, d_ref, nm_ref, nv_ref):
        mine = (pl.program_id(0) // nh) == c_ref[0]
        g = jnp.where(mine, own_ref[...], got_ref[...])
        g_ref[...] = g
        d_ref[...], nm_ref[...], nv_ref[...] = _adam_update(w_ref[...], g, m_ref[...], v_ref[...])

    spec = pl.BlockSpec((tr, cols), lambda i, c: (i, 0))
    own_spec = pl.BlockSpec((tr, cols), lambda i, c: (jnp.where(i // nh == c[0], i % nh, 0), 0))
    got_spec = pl.BlockSpec((tr, cols), lambda i, c: (jnp.where(i // nh == c[0], 0, i % nh), 0))
    o = jax.ShapeDtypeStruct(w.shape, f32)
    grid_spec = pltpu.PrefetchScalarGridSpec(num_scalar_prefetch=1, grid=(2 * nh,),
                                             in_specs=[spec, own_spec, got_spec, spec, spec], out_specs=[spec] * 4)
    return pl.pallas_call(body, out_shape=[o, o, o, o], grid_spec=grid_spec, name=name,
                          compiler_params=_params("arbitrary"))(c_arr, w, own, got, m, v)


def pair_sum(name, g, got, c_arr):
    S, h, cols = got.shape
    tr, _ = _tiles2d((h, cols), 1024)
    nh = h // tr

    def body(c_ref, g_ref, o_ref, out_ref):
        out_ref[...] = (g_ref[...].astype(f32) + o_ref[...].astype(f32)).astype(out_ref.dtype)

    spec = pl.BlockSpec((None, tr, cols), lambda s, i, c: (s, i, 0))
    grid_spec = pltpu.PrefetchScalarGridSpec(
        num_scalar_prefetch=1, grid=(S, nh),
        in_specs=[pl.BlockSpec((None, tr, cols), lambda s, i, c: (s, c[0] * nh + i, 0)), spec], out_specs=spec)
    return pl.pallas_call(body, out_shape=jax.ShapeDtypeStruct(got.shape, g.dtype), grid_spec=grid_spec, name=name,
                          compiler_params=_params("parallel", "parallel"))(c_arr, g, got)


def chip_sum(name, pair, landed, mine_arr, dep=None):
    n_in, h, cols = landed.shape
    tr, _ = _tiles2d((h, cols), 256)

    def body(s_ref, p_ref, l_ref, *rest):
        acc = p_ref[...].astype(f32)
        for k in range(n_in):
            acc = acc + l_ref[k].astype(f32)
        rest[-1][...] = acc

    in_specs, operands, _ = _with_dep(
        [pl.BlockSpec((None, tr, cols), lambda i, s: (s[0], i, 0)), pl.BlockSpec((n_in, tr, cols), lambda i, s: (0, i, 0))],
        [pair, landed], dep)
    grid_spec = pltpu.PrefetchScalarGridSpec(num_scalar_prefetch=1, grid=(h // tr,), in_specs=in_specs,
                                             out_specs=pl.BlockSpec((tr, cols), lambda i, s: (i, 0)))
    return pl.pallas_call(body, out_shape=jax.ShapeDtypeStruct((h, cols), f32), grid_spec=grid_spec, name=name,
                          compiler_params=_params("parallel"))(mine_arr, *operands)


def into_slot(name, w, slot_arr, n_slots, dtype, dep=None):
    tr, cols = _tiles2d(w.shape, 256)

    def body(s_ref, w_ref, *rest):
        rest[-1][...] = w_ref[...].astype(dtype)

    in_specs, operands, _ = _with_dep([pl.BlockSpec((tr, cols), lambda i, s: (i, 0))], [w], dep)
    grid_spec = pltpu.PrefetchScalarGridSpec(num_scalar_prefetch=1, grid=(w.shape[0] // tr,), in_specs=in_specs,
                                             out_specs=pl.BlockSpec((None, tr, cols), lambda i, s: (s[0], i, 0)))
    return pl.pallas_call(body, out_shape=jax.ShapeDtypeStruct((n_slots,) + w.shape, dtype), grid_spec=grid_spec, name=name,
                          compiler_params=_params("parallel"))(slot_arr, *operands)


def sum_slots(name, t):
    S, rows, cols = t.shape
    tr, _ = _tiles2d((rows, cols), 256)

    def body(t_ref, o_ref):
        acc = t_ref[0]
        for s in range(1, S):
            acc = acc + t_ref[s]
        o_ref[...] = acc

    return pl.pallas_call(body, out_shape=jax.ShapeDtypeStruct((rows, cols), f32), grid=(rows // tr,),
                          in_specs=[pl.BlockSpec((S, tr, cols), lambda i: (0, i, 0))], out_specs=pl.BlockSpec((tr, cols), lambda i: (i, 0)),
                          name=name, compiler_params=_params("parallel"))(t)


def _place():
    x, y, c = lax.axis_index("x"), lax.axis_index("y"), lax.axis_index("c")
    return x, y, c


def _other_chips(x, y):
    return [(1 - x, y, 2 * (1 - x) + y), (x, 1 - y, 2 * x + 1 - y), (1 - x, 1 - y, 2 * (1 - x) + 1 - y)]


SEM = pl.BlockSpec(memory_space=pltpu.SEMAPHORE)
VM = pl.BlockSpec(memory_space=pltpu.VMEM)
DATAFLOW = pltpu.SideEffectType.DATAFLOW_SIDE_EFFECTING
TOKEN = jax.ShapeDtypeStruct((SUBLANES, LANES), f32)


def _gather_copy(buf, w, k, chip, c, mine, send, recv):
    px, py, _ = chip
    h = buf.shape[1] // 2
    half = buf.at[mine, pl.ds(c * h, h), :]
    return pltpu.make_async_remote_copy(src_ref=half, dst_ref=half, send_sem=send.at[3 * w + k], recv_sem=recv.at[3 * w + k],
                                        device_id=(px, py, c), device_id_type=MESH)


def _gather_landing(buf, w, k, chip, c, send, recv):
    px, py, s = chip
    h = buf.shape[1] // 2
    landed = buf.at[s, pl.ds(c * h, h), :]
    return pltpu.make_async_remote_copy(src_ref=landed, dst_ref=landed, send_sem=send.at[3 * w + k], recv_sem=recv.at[3 * w + k],
                                        device_id=(px, py, c), device_id_type=MESH)


def gather_start(name, bufs, groups, after, neighbours_only=()):
    nw, ng = len(bufs), len(groups)

    def body(*refs):
        outs = refs[nw + 1:]
        sems, dst = outs[:2 * ng], outs[2 * ng:2 * ng + nw]
        token = outs[2 * ng + nw]
        x, y, c = _place()
        mine = 2 * x + y
        for g, members in enumerate(groups):
            for i, w in enumerate(members):
                for k, chip in enumerate(_other_chips(x, y)[:2 if w in neighbours_only else 3]):
                    _gather_copy(dst[w], i, k, chip, c, mine, sems[2 * g], sems[2 * g + 1]).start()
        token[...] = jnp.zeros_like(token)

    sem_shapes = []
    for members in groups:
        sem_shapes += [pltpu.SemaphoreType.DMA((3 * len(members),))] * 2
    outs = pl.pallas_call(
        body, out_shape=sem_shapes + [jax.ShapeDtypeStruct(b.shape, b.dtype) for b in bufs] + [TOKEN],
        in_specs=[ANY] * (nw + 1), out_specs=[SEM] * (2 * ng) + [ANY] * nw + [VM],
        input_output_aliases={w: 2 * ng + w for w in range(nw)}, name=name,
        compiler_params=pltpu.CompilerParams(has_side_effects=DATAFLOW))(*bufs, after)
    return [(outs[2 * g], outs[2 * g + 1]) for g in range(ng)], list(outs[2 * ng:2 * ng + nw]), outs[2 * ng + nw]


def gather_wait(name, bufs, send, recv, after):
    nw = len(bufs)

    def body(*refs):
        src = refs[:nw]
        send_ref, recv_ref = refs[nw], refs[nw + 1]
        x, y, c = _place()
        mine = 2 * x + y
        for w in range(nw):
            for k, chip in enumerate(_other_chips(x, y)):
                _gather_copy(src[w], w, k, chip, c, mine, send_ref, recv_ref).wait_send()
                _gather_landing(src[w], w, k, chip, c, send_ref, recv_ref).wait_recv()

    return pl.pallas_call(
        body, out_shape=[jax.ShapeDtypeStruct(b.shape, b.dtype) for b in bufs],
        in_specs=[ANY] * nw + [SEM, SEM, ANY], out_specs=[ANY] * nw,
        input_output_aliases={w: w for w in range(nw)}, name=name,
        compiler_params=pltpu.CompilerParams(has_side_effects=DATAFLOW))(*bufs, send, recv, after)


def _relay_copy(buf, w, j, x, y, c, send, recv, landing):
    chips = _other_chips(x, y)
    px, py, _ = chips[j]
    h = buf.shape[1] // 2
    q = h // 2
    s = chips[2][2] if landing else chips[1 - j][2]
    part = buf.at[s, pl.ds(c * h + j * q, q), :]
    return pltpu.make_async_remote_copy(src_ref=part, dst_ref=part, send_sem=send.at[2 * w + j], recv_sem=recv.at[2 * w + j],
                                        device_id=(px, py, c), device_id_type=MESH)


def _early_pass(buf, nw, w, k, x, y, c, send, recv, landing):
    s = _other_chips(x, y)[k][2]
    h = buf.shape[1] // 2
    part = buf.at[s, pl.ds(((1 - c) if landing else c) * h, h), :]
    i = 2 * nw + 2 * w + k
    return pltpu.make_async_remote_copy(src_ref=part, dst_ref=part, send_sem=send.at[i], recv_sem=recv.at[i],
                                        device_id=(x, y, 1 - c), device_id_type=MESH)


def gather_relay(name, bufs, sems, more, after):
    nw, nm = len(bufs), len(more)
    ns = 4 if nm else 2

    def body(*refs):
        ins, outs = refs[:nw + nm + 2 * nw + 1], refs[nw + nm + 2 * nw + 1:]
        src, d_sems = ins[:nw], ins[nw + nm:nw + nm + 2 * nw]
        r_send, r_recv = outs[:2]
        m_send, m_recv = outs[2:ns] if nm else (None, None)
        dst, mdst, token = outs[ns:ns + nw], outs[ns + nw:ns + nw + nm], outs[ns + nw + nm]
        x, y, c = _place()
        mine = 2 * x + y
        chips = _other_chips(x, y)
        for w in range(nw):
            for k in range(2):
                _gather_copy(src[w], 0, k, chips[k], c, mine, d_sems[2 * w], d_sems[2 * w + 1]).wait_send()
                _gather_landing(src[w], 0, k, chips[k], c, d_sems[2 * w], d_sems[2 * w + 1]).wait_recv()
            for j in range(2):
                _relay_copy(dst[w], w, j, x, y, c, r_send, r_recv, False).start()
            for k in range(2):
                _early_pass(dst[w], nw, w, k, x, y, c, r_send, r_recv, False).start()
        for w in range(nm):
            for k, chip in enumerate(chips):
                _gather_copy(mdst[w], w, k, chip, c, mine, m_send, m_recv).start()
        token[...] = jnp.zeros_like(token)

    sem_shapes = [pltpu.SemaphoreType.DMA((4 * nw,))] * 2 + [pltpu.SemaphoreType.DMA((3 * nm,))] * (ns - 2)
    flat_sems = [s for pair in sems for s in pair]
    outs = pl.pallas_call(
        body, out_shape=sem_shapes + [jax.ShapeDtypeStruct(b.shape, b.dtype) for b in list(bufs) + list(more)] + [TOKEN],
        in_specs=[ANY] * (nw + nm) + [SEM] * (2 * nw) + [ANY], out_specs=[SEM] * ns + [ANY] * (nw + nm) + [VM],
        input_output_aliases={i: ns + i for i in range(nw + nm)}, name=name,
        compiler_params=pltpu.CompilerParams(has_side_effects=DATAFLOW))(*bufs, *more, *flat_sems, after)
    m_sems = (outs[2], outs[3]) if nm else None
    return outs[0], outs[1], m_sems, list(outs[ns:ns + nw]), list(outs[ns + nw:ns + nw + nm]), outs[ns + nw + nm]


def gather_wait_relay(name, bufs, r_send, r_recv, after):
    nw = len(bufs)

    def body(*refs):
        src = refs[:nw]
        send_ref, recv_ref = refs[nw], refs[nw + 1]
        x, y, c = _place()
        for w in range(nw):
            for j in range(2):
                _relay_copy(src[w], w, j, x, y, c, send_ref, recv_ref, False).wait_send()
                _relay_copy(src[w], w, j, x, y, c, send_ref, recv_ref, True).wait_recv()
                _early_pass(src[w], nw, w, j, x, y, c, send_ref, recv_ref, False).wait_send()
                _early_pass(src[w], nw, w, j, x, y, c, send_ref, recv_ref, True).wait_recv()

    return pl.pallas_call(
        body, out_shape=[jax.ShapeDtypeStruct(b.shape, b.dtype) for b in bufs],
        in_specs=[ANY] * nw + [SEM, SEM, ANY], out_specs=[ANY] * nw,
        input_output_aliases={w: w for w in range(nw)}, name=name,
        compiler_params=pltpu.CompilerParams(has_side_effects=DATAFLOW))(*bufs, r_send, r_recv, after)


def gather_forward(name, bufs, which=(0, 1, 2)):
    nw = len(bufs)

    def body(*refs):
        dst = refs[nw:2 * nw]
        send, recv = refs[2 * nw:]
        x, y, c = _place()
        sib = (x, y, 1 - c)
        barrier = pltpu.get_barrier_semaphore()
        pl.semaphore_signal(barrier, inc=1, device_id=sib, device_id_type=MESH)
        pl.semaphore_wait(barrier, 1)
        cps = []
        for w in range(nw):
            h = dst[w].shape[1] // 2
            for k in which:
                s = _other_chips(x, y)[k][2]
                landed = dst[w].at[s, pl.ds(c * h, h), :]
                cp = pltpu.make_async_remote_copy(src_ref=landed, dst_ref=landed, send_sem=send.at[w, k], recv_sem=recv.at[w, k],
                                                  device_id=sib, device_id_type=MESH)
                cp.start()
                cps.append(cp)
        for w in range(nw):
            h = dst[w].shape[1] // 2
            for k in which:
                s = _other_chips(x, y)[k][2]
                other = dst[w].at[s, pl.ds((1 - c) * h, h), :]
                pltpu.make_async_remote_copy(src_ref=other, dst_ref=other, send_sem=send.at[w, k], recv_sem=recv.at[w, k],
                                             device_id=sib, device_id_type=MESH).wait_recv()
        for cp in cps:
            cp.wait_send()

    sem = pltpu.SemaphoreType.DMA((nw, 3))
    return pl.pallas_call(
        body, out_shape=[jax.ShapeDtypeStruct(b.shape, b.dtype) for b in bufs],
        in_specs=[ANY] * nw, out_specs=[ANY] * nw, input_output_aliases={w: w for w in range(nw)},
        scratch_shapes=[sem, sem], name=name,
        compiler_params=pltpu.CompilerParams(has_side_effects=True, collective_id=SIBLING_PAIR))(*bufs)


def _scatter_copy(src, dst, w, k, chip, c, send, recv):
    px, py, s = chip
    return pltpu.make_async_remote_copy(src_ref=src.at[s], dst_ref=dst.at[k], send_sem=send.at[3 * w + k], recv_sem=recv.at[3 * w + k],
                                        device_id=(px, py, c), device_id_type=MESH)


def scatter_start(name, parts):
    nw = len(parts)
    lands = [pltpu.with_memory_space_constraint(lax.empty((N_CHIPS - 1,) + p.shape[1:], p.dtype), pltpu.HBM) for p in parts]

    def body(*refs):
        outs = refs[2 * nw:]
        send, recv = outs[0], outs[1]
        src, dst, token = outs[2:2 + nw], outs[2 + nw:2 + 2 * nw], outs[2 + 2 * nw]
        x, y, c = _place()
        for w in range(nw):
            for k, chip in enumerate(_other_chips(x, y)):
                _scatter_copy(src[w], dst[w], w, k, chip, c, send, recv).start()
        token[...] = jnp.zeros_like(token)

    sem = pltpu.SemaphoreType.DMA((3 * nw,))
    outs = pl.pallas_call(
        body, out_shape=[sem, sem] + [jax.ShapeDtypeStruct(p.shape, p.dtype) for p in parts]
        + [jax.ShapeDtypeStruct(l.shape, l.dtype) for l in lands] + [TOKEN],
        in_specs=[ANY] * (2 * nw), out_specs=[SEM, SEM] + [ANY] * (2 * nw) + [VM],
        input_output_aliases={i: 2 + i for i in range(2 * nw)}, name=name,
        compiler_params=pltpu.CompilerParams(has_side_effects=DATAFLOW))(*parts, *lands)
    return outs[0], outs[1], list(outs[2:2 + nw]), list(outs[2 + nw:2 + 2 * nw]), outs[2 + 2 * nw]


def scatter_wait(name, parts, lands, send, recv, after):
    nw = len(parts)

    def body(*refs):
        src, dst = refs[:nw], refs[nw:2 * nw]
        send_ref, recv_ref = refs[2 * nw], refs[2 * nw + 1]
        x, y, c = _place()
        for w in range(nw):
            for k, chip in enumerate(_other_chips(x, y)):
                cp = _scatter_copy(src[w], dst[w], w, k, chip, c, send_ref, recv_ref)
                cp.wait_send()
                cp.wait_recv()

    outs = pl.pallas_call(
        body, out_shape=[jax.ShapeDtypeStruct(a.shape, a.dtype) for a in list(parts) + list(lands)],
        in_specs=[ANY] * (2 * nw) + [SEM, SEM, ANY], out_specs=[ANY] * (2 * nw),
        input_output_aliases={i: i for i in range(2 * nw)}, name=name,
        compiler_params=pltpu.CompilerParams(has_side_effects=DATAFLOW))(*parts, *lands, send, recv, after)
    return list(outs[:nw]), list(outs[nw:])


SIBLING_PAIR = 0


def _sibling_copy(src, dst, w, c, half_rows, send, recv, sib):
    if half_rows:
        h = src.shape[1] // 2
        src = src.at[:, pl.ds((1 - c) * h, h), :]
    return pltpu.make_async_remote_copy(src_ref=src, dst_ref=dst, send_sem=send.at[w], recv_sem=recv.at[w],
                                        device_id=sib, device_id_type=MESH)


def _landing(shape, dtype):
    return pltpu.with_memory_space_constraint(lax.empty(shape, dtype), pltpu.HBM)


def sibling_start(name, srcs, half_rows):
    nw = len(srcs)
    lands = [_landing((s.shape[0], s.shape[1] // 2, s.shape[2]) if half_rows else s.shape, s.dtype) for s in srcs]

    def body(*refs):
        outs = refs[2 * nw:]
        send, recv = outs[0], outs[1]
        src, dst, token = outs[2:2 + nw], outs[2 + nw:2 + 2 * nw], outs[2 + 2 * nw]
        x, y, c = _place()
        barrier = pltpu.get_barrier_semaphore()
        pl.semaphore_signal(barrier, inc=1, device_id=(x, y, 1 - c), device_id_type=MESH)
        pl.semaphore_wait(barrier, 1)
        for w in range(nw):
            _sibling_copy(src[w], dst[w], w, c, half_rows, send, recv, (x, y, 1 - c)).start()
        token[...] = jnp.zeros_like(token)

    sem = pltpu.SemaphoreType.DMA((nw,))
    outs = pl.pallas_call(
        body, out_shape=[sem, sem] + [jax.ShapeDtypeStruct(a.shape, a.dtype) for a in list(srcs) + lands] + [TOKEN],
        in_specs=[ANY] * (2 * nw), out_specs=[SEM, SEM] + [ANY] * (2 * nw) + [VM],
        input_output_aliases={i: 2 + i for i in range(2 * nw)}, name=name,
        compiler_params=pltpu.CompilerParams(has_side_effects=DATAFLOW, collective_id=SIBLING_PAIR))(*srcs, *lands)
    return outs[0], outs[1], list(outs[2:2 + nw]), list(outs[2 + nw:2 + 2 * nw]), outs[2 + 2 * nw]


def sibling_wait(name, srcs, lands, send, recv, half_rows, after):
    nw = len(srcs)

    def body(*refs):
        src, dst = refs[:nw], refs[nw:2 * nw]
        send_ref, recv_ref = refs[2 * nw], refs[2 * nw + 1]
        x, y, c = _place()
        for w in range(nw):
            cp = _sibling_copy(src[w], dst[w], w, c, half_rows, send_ref, recv_ref, (x, y, 1 - c))
            cp.wait_send()
            cp.wait_recv()

    outs = pl.pallas_call(
        body, out_shape=[jax.ShapeDtypeStruct(a.shape, a.dtype) for a in list(srcs) + list(lands)],
        in_specs=[ANY] * (2 * nw) + [SEM, SEM, ANY], out_specs=[ANY] * (2 * nw),
        input_output_aliases={i: i for i in range(2 * nw)}, name=name,
        compiler_params=pltpu.CompilerParams(has_side_effects=DATAFLOW))(*srcs, *lands, send, recv, after)
    return list(outs[:nw]), list(outs[nw:])


def _peer(x, y, c, r):
    return (1 - x if r & 4 else x, 1 - y if r & 2 else y, 1 - c if r & 1 else c)


def _everyone_copy(buf, r, x, y, c, send, recv, landing):
    px, py, pc = _peer(x, y, c, r)
    slot = buf.at[4 * px + 2 * py + pc] if landing else buf.at[4 * x + 2 * y + c]
    return pltpu.make_async_remote_copy(src_ref=slot, dst_ref=slot, send_sem=send.at[r - 1], recv_sem=recv.at[r - 1],
                                        device_id=(px, py, pc), device_id_type=MESH)


def everyone_start(name, buf):
    def body(buf_in, send, recv, buf_ref, token):
        x, y, c = _place()
        for r in range(1, N_DEV):
            _everyone_copy(buf_ref, r, x, y, c, send, recv, False).start()
        token[...] = jnp.zeros_like(token)

    sem = pltpu.SemaphoreType.DMA((N_DEV - 1,))
    return pl.pallas_call(
        body, out_shape=[sem, sem, jax.ShapeDtypeStruct(buf.shape, buf.dtype), TOKEN],
        in_specs=[ANY], out_specs=[SEM, SEM, ANY, VM], input_output_aliases={0: 2}, name=name,
        compiler_params=pltpu.CompilerParams(has_side_effects=DATAFLOW))(buf)


def everyone_wait(name, buf, send, recv, after):
    def body(buf_ref, send_ref, recv_ref, after_ref, out_ref):
        x, y, c = _place()
        for r in range(1, N_DEV):
            _everyone_copy(buf_ref, r, x, y, c, send_ref, recv_ref, False).wait_send()
            _everyone_copy(buf_ref, r, x, y, c, send_ref, recv_ref, True).wait_recv()

    return pl.pallas_call(
        body, out_shape=jax.ShapeDtypeStruct(buf.shape, buf.dtype), in_specs=[ANY, SEM, SEM, ANY], out_specs=ANY,
        input_output_aliases={0: 0}, name=name,
        compiler_params=pltpu.CompilerParams(has_side_effects=DATAFLOW))(buf, send, recv, after)


def local_step(x, pos, tgt, small, d_in, get_w, put_g, first_dep=None, tick=lambda name, after: None):
    L, D = x.shape
    d_kv = N_KV_HEADS * HEAD_DIM
    d_ssm = small["d_skip"].shape[1]
    d_attn = d_in - 2 * d_kv - d_ssm
    big = {}
    G = d_ssm // SSM_GROUP
    N, P = SSM_STATE, SSM_GROUP

    half_dim = HEAD_DIM // 2
    inv_freq = ROPE_THETA ** (-jnp.arange(half_dim, dtype=f32) / half_dim)
    inv_freq = jnp.tile(inv_freq, LANES // half_dim).reshape(1, LANES)
    sink_b = jnp.broadcast_to(small["sinks"].reshape(-1, 1), (small["sinks"].size, LANES))

    spread = jnp.repeat(jnp.eye(G, dtype=f32), P, axis=0)
    logdt_b = jnp.broadcast_to(small["log_dt"].reshape(G, 1), (G, N))
    bt_re = small["b_re"].reshape(G, N, P).transpose(0, 2, 1).reshape(G * P, N)
    bt_im = small["b_im"].reshape(G, N, P).transpose(0, 2, 1).reshape(G * P, N)
    a_re, a_im = small["a_re"].reshape(G, N), small["a_im"].reshape(G, N)
    lam_re, lam_im, bbt_re, bbt_im = ssm_params(a_re, a_im, logdt_b, bt_re, bt_im, spread)
    bd_re = _block_diag(bbt_re.reshape(G, P, N), P, N).astype(bf16)
    bd_im = _block_diag(bbt_im.reshape(G, P, N), P, N).astype(bf16)
    c_re = small["c_re"].reshape(G, P, N).transpose(0, 2, 1)
    c_im = small["c_im"].reshape(G, P, N).transpose(0, 2, 1)
    cd_re = _block_diag(c_re, N, P).astype(bf16)
    cd_im = _block_diag(c_im, N, P).astype(bf16)
    lam_re_l, lam_im_l = _state_layout(lam_re), _state_layout(lam_im)

    def k1(i, nt, xt, g):
        return (rms_fwd(xt, g),)
    xn = rowwise("pre_mix_norm", k1, L, [full(x)], [small["g_pre_mix"]], [(D, bf16)], tr=512, dep=first_dep)[0]
    big["w_in"] = get_w("w_in", xn)
    proj = mm_nn("proj_in", xn, big["w_in"])
    qr, kk, vv, u_bf = qkv_prep(proj, pos, inv_freq, d_attn, d_kv)
    attn = attn_fwd(qr, kk, vv, sink_b)
    u_cb = (d_attn + 2 * d_kv) // (d_ssm // 2)
    token = tick("attn", attn)
    y, z_bf, s_re, s_im = ssm_fwd(u_bf, proj, u_cb, bd_re, bd_im, cd_re, cd_im, lam_re_l, lam_im_l, small["d_skip"], dep=token)
    token = tick("ssm", z_bf)
    big["w_glu"] = get_w("w_glu", z_bf)
    gl = mm_nn("glu_proj", z_bf, big["w_glu"], dep=token)

    def k6(i, nt, at, yt, glt, bg, ga, gs):
        ssm = gelu(yt) * sigmoid(glt + bg)
        return (jnp.concatenate([rms_fwd(at, ga), rms_fwd(ssm, gs)], axis=1),)
    mixed = rowwise("mix_norms", k6, L, [full(attn), full(y), full(gl)],
                    [small["b_glu"], small["g_attn_out"], small["g_ssm_out"]], [(d_attn + d_ssm, bf16)])[0]
    big["w_o"] = get_w("w_o", mixed)
    mix = mm_nn("proj_out", mixed, big["w_o"])

    def k7(i, nt, xt, mt, gpm, gpf):
        h = xt + rms_fwd(mt, gpm)
        return h, rms_fwd(h, gpf)
    h, hn = rowwise("post_mix", k7, L, [full(x), full(mix)], [small["g_post_mix"], small["g_pre_ffn"]], [(D, f32), (D, bf16)], tr=512)
    big["w_gate"] = get_w("w_gate", hn)
    big["w_up"] = get_w("w_up", hn)
    hid_dg, hid_du, hid = ffn_hidden(hn, big["w_gate"], big["w_up"])
    d_ff_dim = hid.shape[1]
    big["w_down"] = get_w("w_down", hid)
    ff = mm_nn("ffn_down", hid, big["w_down"], tk=d_ff_dim // 2)

    def k9(i, nt, ht, fft, tt, g):
        out = ht + rms_fwd(fft, g)
        err = out - tt
        per_row = jnp.mean(err * err, axis=-1, keepdims=True)
        loss = 0.5 * jnp.sum(per_row) * jnp.where(_lane((1, LANES)) == 0, 1.0, 0.0)
        d_out = err * (1.0 / D)
        d_ff, dg = rms_bwd(fft, g, d_out)
        return d_out, d_ff, dg, loss
    d_out, d_ff, dg_post_ffn, loss = rowwise("loss_head", k9, L, [full(h), full(ff), full(tgt)], [small["g_post_ffn"]],
                                             [(D, f32), (D, bf16)], reds=[D, LANES])

    d_gt, d_up = ffn_hidden_grad(d_ff, big["w_down"], hid_dg, hid_du)
    token = put_g("w_down", hid, d_ff, None)
    d_hn = mm_nt_pair("d_hn", d_gt, big["w_gate"], d_up, big["w_up"], dep=token)
    token = put_g("w_gate", hn, d_gt, None)
    token = put_g("w_up", hn, d_up, token)

    def k11(i, nt, ht, da, do, mt, gpf, gpm):
        dh_n, dg_pf = rms_bwd(ht, gpf, da)
        dh = do + dh_n
        d_mix, dg_pm = rms_bwd(mt, gpm, dh)
        return dh, d_mix, dg_pf, dg_pm
    dh, d_mix, dg_pre_ffn, dg_post_mix = rowwise("post_mix_grad", k11, L, [full(h), full(d_hn), full(d_out), full(mix)],
                                                 [small["g_pre_ffn"], small["g_post_mix"]], [(D, f32), (D, bf16)], reds=[D, D], dep=token)
    d_mixed = mm_nt("d_mixed", d_mix, big["w_o"])
    token = put_g("w_o", mixed, d_mix, None)

    def k12(i, nt, at, yt, glt, da_n, ds_n, bg, ga, gs):
        z = gelu(yt)
        sg = sigmoid(glt + bg)
        ssm = z * sg
        d_at, dga = rms_bwd(at, ga, da_n)
        d_ssm_t, dgs = rms_bwd(ssm, gs, ds_n)
        d_gl = d_ssm_t * z * sg * (1.0 - sg)
        return d_at, d_ssm_t * sg, d_gl, dga, dgs, colsum(d_gl)
    d_attn_o, dz1, d_gl, dg_attn, dg_ssm, db_glu = rowwise(
        "mix_norms_grad", k12, L, [full(attn), full(y), full(gl), (d_mixed, d_attn, 0, 0), (d_mixed, d_ssm, d_attn // d_ssm, 0)],
        [small["b_glu"], small["g_attn_out"], small["g_ssm_out"]], [(d_attn, f32), (d_ssm, f32), (d_ssm, bf16)],
        reds=[d_attn, d_ssm, d_ssm], dep=token)
    dz2 = mm_nt("d_glu_in", d_gl, big["w_glu"])
    token = put_g("w_glu", z_bf, d_gl, None)

    du, dbd_re, dbd_im, dcd_re, dcd_im, dlam_re_l, dlam_im_l, dd_skip = ssm_bwd(
        y, dz1, dz2, u_bf, proj, u_cb, s_re, s_im, bd_re, bd_im, cd_re, cd_im, lam_re_l, lam_im_l, small["d_skip"], dep=token)
    dq, dkk_c, dkk_p, dvv_c, dvv_p, dsink = attn_bwd(qr, kk, vv, sink_b, attn, d_attn_o)
    d_proj = qkv_grad(dq, dkk_c, dkk_p, dvv_c, dvv_p, du, pos, inv_freq)
    d_xn = mm_nt("d_xn", d_proj, big["w_in"])
    token = put_g("w_in", xn, d_proj, None)

    def k17(i, nt, xt, dxn, dht, g):
        dx, dg = rms_bwd(xt, g, dxn)
        return dht + dx, dg
    grad_x, dg_pre_mix = rowwise("pre_mix_grad", k17, L, [full(x), full(d_xn), full(dh)], [small["g_pre_mix"]],
                                 [(D, f32)], reds=[D], dep=token)

    gather = spread.T
    dbbt_re = dbd_re.reshape(G * P, LANES)[:, :N]
    dbbt_im = dbd_im.reshape(G * P, LANES)[:, :N]
    dc_re = dcd_re.reshape(G * P, LANES)[:, :N].reshape(G, P, N)
    dc_im = dcd_im.reshape(G * P, LANES)[:, :N].reshape(G, P, N)
    d_a_re, d_a_im, d_logdt, dbt_re, dbt_im = ssm_params_grad(
        a_re, a_im, logdt_b, bt_re, bt_im, spread, gather,
        _state_layout_inv(dlam_re_l, G, N), _state_layout_inv(dlam_im_l, G, N), dbbt_re, dbbt_im)
    q_per_kv = d_attn // HEAD_DIM // N_KV_HEADS
    small_grads = {
        "g_pre_mix": dg_pre_mix, "sinks": dsink[:, :q_per_kv, 0].reshape(1, -1),
        "a_re": d_a_re, "a_im": d_a_im, "log_dt": d_logdt.reshape(1, G),
        "b_re": dbt_re, "b_im": dbt_im,
        "c_re": dc_re, "c_im": dc_im,
        "d_skip": dd_skip, "b_glu": db_glu, "g_attn_out": dg_attn, "g_ssm_out": dg_ssm,
        "g_post_mix": dg_post_mix, "g_pre_ffn": dg_pre_ffn, "g_post_ffn": dg_post_ffn,
    }
    return loss, grad_x, small_grads


WEIGHTS = ['g_pre_mix', 'w_in', 'sinks', 'a_re', 'a_im', 'log_dt', 'b_re', 'b_im', 'c_re', 'c_im', 'd_skip', 'w_glu', 'b_glu',
           'g_attn_out', 'g_ssm_out', 'w_o', 'g_post_mix', 'g_pre_ffn', 'w_gate', 'w_up', 'w_down', 'g_post_ffn']
BIG = ['w_in', 'w_glu', 'w_o', 'w_gate', 'w_up', 'w_down']
COL_SHARDED = ['w_in', 'w_gate', 'w_up']
SMALL = [n for n in WEIGHTS if n not in BIG]
GATHER_GROUPS = [["w_in"], ["w_glu", "w_o"], ["w_gate", "w_up"], ["w_down"]]
REDUCE_GROUPS = [["w_down", "w_gate", "w_up"], ["w_o", "w_glu", "w_in"]]


PACK_ROWS = 256


def _pack(parts):
    flat = jnp.concatenate([p.reshape(-1) for p in parts])
    pad = (-flat.size) % (PACK_ROWS * LANES)
    return jnp.pad(flat, (0, pad)).reshape(-1, LANES)


TRANSPOSED_VIEW = ("b_re", "b_im")


def small_view(name, a):
    if name in TRANSPOSED_VIEW:
        a = a.transpose(0, 1, 3, 2)
    return a.reshape(-1, a.shape[-1])


def small_unview(name, p, shape):
    if name in TRANSPOSED_VIEW:
        return p.reshape(shape[0], shape[1], shape[3], shape[2]).transpose(0, 1, 3, 2)
    return p.reshape(shape)


def _unpack(packed, shapes):
    flat = packed.reshape(-1)
    out, off = [], 0
    for s in shapes:
        n = int(np.prod(s))
        out.append(flat[off:off + n].reshape(s))
        off += n
    return out


def kernel(x, positions, g_pre_mix, w_in, sinks, a_re, a_im, log_dt, b_re, b_im, c_re, c_im, d_skip, w_glu, b_glu, g_attn_out, g_ssm_out, w_o, g_post_mix, g_pre_ffn, w_gate, w_up, w_down, g_post_ffn, loss_target, m_g_pre_mix, m_w_in, m_sinks, m_a_re, m_a_im, m_log_dt, m_b_re, m_b_im, m_c_re, m_c_im, m_d_skip, m_w_glu, m_b_glu, m_g_attn_out, m_g_ssm_out, m_w_o, m_g_post_mix, m_g_pre_ffn, m_w_gate, m_w_up, m_w_down, m_g_post_ffn, v_g_pre_mix, v_w_in, v_sinks, v_a_re, v_a_im, v_log_dt, v_b_re, v_b_im, v_c_re, v_c_im, v_d_skip, v_w_glu, v_b_glu, v_g_attn_out, v_g_ssm_out, v_w_o, v_g_post_mix, v_g_pre_ffn, v_w_gate, v_w_up, v_w_down, v_g_post_ffn):
    args = dict(locals())
    w = {n: args[n] for n in WEIGHTS}
    m = {n: args["m_" + n] for n in WEIGHTS}
    v = {n: args["v_" + n] for n in WEIGHTS}
    L, D = x.shape[1], x.shape[2]

    ax, ay, ac = _place()
    mine_arr = (2 * ax + ay).astype(jnp.int32).reshape(1)
    c_arr = ac.astype(jnp.int32).reshape(1)

    me_arr = (4 * ax + 2 * ay + ac).astype(jnp.int32).reshape(1)

    bufs = {"w_in": into_slot("cast_w_in", w["w_in"][0], mine_arr, N_CHIPS, bf16)}
    (first_sems,), (bufs["w_in"],), token = gather_start("gather_start_in", [bufs["w_in"]], [[0]], mine_arr, neighbours_only=(0,))
    sems, relays, ready = {"w_in": first_sems}, {}, set()

    def relay(n, more, after):
        r_send, r_recv, more_sems, (bufs[n],), started, tok = gather_relay(
            "gather_relay_" + n, [bufs[n]], [sems[n]], [bufs[k] for k in more], after)
        sems.update({k: more_sems for k in more})
        bufs.update(zip(more, started))
        relays[n] = (r_send, r_recv)
        return tok

    for n in BIG[1:]:
        bufs[n] = into_slot("cast_" + n, w[n][0], mine_arr, N_CHIPS, bf16, dep=token)
    token = relay("w_in", [], bufs[BIG[-1]])
    first = ["w_gate", "w_glu", "w_o", "w_up"]
    (sems["w_gate"], sems["w_glu"], sems["w_up"]), started, token = gather_start(
        "gather_start_rest", [bufs[n] for n in first], [[0], [1, 2], [3]], token, neighbours_only=(0, 3))
    bufs.update(zip(first, started))

    def tick(name, after):
        return relay("w_gate", [], after) if name == "attn" else relay("w_up", ["w_down"], after)

    def get_w(n, after):
        if n not in ready:
            members = [g for g in GATHER_GROUPS if n in g][0]
            if members[0] in relays:
                landed = [gather_wait_relay("gather_wait_" + k, [bufs[k]], *relays[k], after)[0] for k in members]
                which = (2,)
            else:
                landed = gather_wait("gather_wait_" + members[0], [bufs[k] for k in members], *sems[members[0]], after)
                which = (0, 1, 2)
            bufs.update(zip(members, gather_forward("gather_forward_" + members[0], landed, which)))
            ready.update(members)
        g = bufs[n]
        return g if n in COL_SHARDED else g.reshape(g.shape[0] * g.shape[1], g.shape[2])

    swaps, operands, inflight = {}, {}, []
    other_arr = 1 - c_arr

    def put_g(n, a, b, dep):
        halves = (a.shape[1] // N_CHIPS // 2) % LANES == 0 if n not in COL_SHARDED else True
        if halves:
            operands[n] = (a, b)
            sent = mm_tn_half("d%s_sent" % n, a, b, n in COL_SHARDED, other_arr, dep=dep)
            swaps[n] = sibling_start("swap_start_" + n, [sent], False)
        else:
            g = weight_grad(n, a, b, dep)
            swaps[n] = sibling_start("swap_start_" + n, [g.reshape(N_CHIPS, g.shape[0] // N_CHIPS, g.shape[1])], True)
        for gi, members in enumerate(REDUCE_GROUPS):
            if n == members[-1]:
                last = swaps[n][4]
                pair = []
                for k in members:
                    send, recv, srcs, lands, _ = swaps[k]
                    (src,), (got,) = sibling_wait("swap_wait_" + k, srcs, lands, send, recv, k not in operands, last)
                    if k in operands:
                        pair.append(mm_tn_half("d%s_kept" % k, *operands[k], k in COL_SHARDED, c_arr, addend=got))
                    else:
                        pair.append(pair_sum("pair_sum_" + k, src, got, c_arr))
                send, recv, parts, lands, tok = scatter_start("scatter_start_%d" % gi, pair)
                inflight.append((members, send, recv, parts, lands))
                return tok
        return swaps[n][4]

    small = {n: w[n].reshape(1, -1) for n in SMALL}
    pos = positions.reshape(L, 1).astype(f32)
    d_in = N_CHIPS * w["w_in"].shape[2]
    loss, grad_x, small_grads = local_step(x[0], pos, loss_target[0], small, d_in, get_w, put_g, first_dep=token, tick=tick)

    shapes = [w[n].shape for n in SMALL]
    blocks = into_slot("small_block", _pack([small_grads[n] for n in SMALL] + [loss]), me_arr, N_DEV, f32)
    small_send, small_recv, blocks, after = everyone_start("small_start", blocks)

    grads, delta, new_m, new_v = {}, {}, {}, {}
    for gi, (members, send, recv, parts, lands) in enumerate(inflight):
        parts, landed = scatter_wait("scatter_wait_%d" % gi, parts, lands, send, recv, after)
        joins, dep = [], None
        for k, p, t in zip(members, parts, landed):
            joins.append(sibling_start("join_start_" + k, [chip_sum("chip_sum_" + k, p, t, mine_arr, dep=dep)], False))
            dep = after = joins[-1][4]
        for n, (send, recv, srcs, lands, _) in zip(members, joins):
            (own,), (sib,) = sibling_wait("join_wait_" + n, srcs, lands, send, recv, False, after)
            g_, d_, m_, v_ = adamw_halves("adamw_" + n, w[n][0], own, sib, m[n][0], v[n][0], c_arr)
            grads[n], delta[n], new_m[n], new_v[n] = g_[None], d_[None], m_[None], v_[None]
            after = v_
    blocks = everyone_wait("small_wait", blocks, small_send, small_recv, after)
    small_sum = sum_slots("small_sum", blocks)
    *small_g, loss = _unpack(small_sum, [small_view(n, w[n]).shape for n in SMALL] + [loss.shape])
    loss = loss[0, 0]
    outs = adamw_many("adamw_small", [small_view(n, w[n]) for n in SMALL], small_g,
                      [small_view(n, m[n]) for n in SMALL], [small_view(n, v[n]) for n in SMALL])
    for t, parts in zip((grads, delta, new_m, new_v), (small_g,) + tuple(outs)):
        t.update({n: small_unview(n, p, w[n].shape) for n, p in zip(SMALL, parts)})

    return (loss, grad_x[None], *[grads[n] for n in WEIGHTS], *[delta[n] for n in WEIGHTS],
            *[new_m[n] for n in WEIGHTS], *[new_v[n] for n in WEIGHTS])
```

```python
import functools
import math

import jax
import jax.numpy as jnp
import numpy as np
from jax import lax
from jax.experimental import pallas as pl
from jax.experimental.pallas import tpu as pltpu

f32 = jnp.float32
bf16 = jnp.bfloat16
HIGHEST = lax.Precision.HIGHEST
MESH = pl.DeviceIdType.MESH

HEAD_DIM = 64
N_KV_HEADS = 4
ATTN_BLOCK = 128
ROPE_THETA = 10000.0
SSM_GROUP = 16
SSM_STATE = 64
RMS_EPS = 1e-6
LANES = 128
SUBLANES = 8
VMEM_LIMIT = 52 * 1024 * 1024
N_CHIPS = 4
N_DEV = 8
NEG = -1e30

ADAM_LR, ADAM_B1, ADAM_B2, ADAM_EPS, ADAM_WD, ADAM_STEP = 0.001, 0.9, 0.999, 1e-08, 0.01, 10

NN = (((1,), (0,)), ((), ()))
NT = (((1,), (1,)), ((), ()))
TN = (((0,), (0,)), ((), ()))


def _params(*sem):
    return pltpu.CompilerParams(dimension_semantics=sem or None, vmem_limit_bytes=VMEM_LIMIT)


def _dot(a, b, dims=NN):
    return lax.dot_general(a, b, dims, preferred_element_type=f32)


def _pick(dim, pref):
    t = min(dim, pref)
    while dim % t:
        t -= LANES
    assert t > 0, (dim, pref)
    return t


ANY = pl.BlockSpec(memory_space=pl.ANY)


def _with_dep(in_specs, operands, dep):
    if dep is None:
        return list(in_specs), list(operands), 0
    return list(in_specs) + [ANY], list(operands) + [dep], 1


def _mm_call(name, grid, in_specs, out_spec, out_shape, acc_shape, dims, operands, dep=None):
    nk = grid[2]
    in_specs, operands, n_dep = _with_dep(in_specs, operands, dep)

    def body_one(a_ref, b_ref, *rest):
        o_ref = rest[n_dep]
        o_ref[...] = _dot(a_ref[...], b_ref[...], dims).astype(o_ref.dtype)

    def body(a_ref, b_ref, *rest):
        o_ref, acc_ref = rest[n_dep], rest[n_dep + 1]
        k = pl.program_id(2)

        @pl.when(k == 0)
        def _():
            acc_ref[...] = _dot(a_ref[...], b_ref[...], dims)

        @pl.when((k > 0) & (k < nk - 1))
        def _():
            acc_ref[...] += _dot(a_ref[...], b_ref[...], dims)

        @pl.when(k == nk - 1)
        def _():
            o_ref[...] = (acc_ref[...] + _dot(a_ref[...], b_ref[...], dims)).astype(o_ref.dtype)

    return pl.pallas_call(
        body_one if nk == 1 else body, out_shape=out_shape, grid=grid, in_specs=in_specs, out_specs=out_spec,
        scratch_shapes=[] if nk == 1 else [pltpu.VMEM(acc_shape, f32)], name=name,
        compiler_params=_params("parallel", "parallel", "arbitrary"))(*operands)


def mm_nt_pair(name, a1, b1, a2, b2, tm=1024, tk=1024, dep=None):
    M = a1.shape[0]
    S, K, n = b1.shape
    tm, tko = _pick(M, tm), _pick(K, tk)
    nk = 2 * S

    def body(a1_ref, b1_ref, a2_ref, b2_ref, *rest):
        o_ref, acc_ref = rest[-2], rest[-1]
        k = pl.program_id(2)

        @pl.when(k == 0)
        def _():
            acc_ref[...] = _dot(a1_ref[...], b1_ref[...], NT)

        @pl.when((k > 0) & (k < S))
        def _():
            acc_ref[...] += _dot(a1_ref[...], b1_ref[...], NT)

        @pl.when((k >= S) & (k < nk - 1))
        def _():
            acc_ref[...] += _dot(a2_ref[...], b2_ref[...], NT)

        @pl.when(k == nk - 1)
        def _():
            o_ref[...] = acc_ref[...] + _dot(a2_ref[...], b2_ref[...], NT)

    first = lambda k: jnp.minimum(k, S - 1)
    second = lambda k: jnp.maximum(k - S, 0)
    in_specs = [pl.BlockSpec((tm, n), lambda i, j, k: (i, first(k))), pl.BlockSpec((None, tko, n), lambda i, j, k: (first(k), j, 0)),
                pl.BlockSpec((tm, n), lambda i, j, k: (i, second(k))), pl.BlockSpec((None, tko, n), lambda i, j, k: (second(k), j, 0))]
    in_specs, operands, _ = _with_dep(in_specs, (a1, b1, a2, b2), dep)
    return pl.pallas_call(
        body, out_shape=jax.ShapeDtypeStruct((M, K), f32), grid=(M // tm, K // tko, nk), in_specs=in_specs,
        out_specs=pl.BlockSpec((tm, tko), lambda i, j, k: (i, j)), scratch_shapes=[pltpu.VMEM((tm, tko), f32)], name=name,
        compiler_params=_params("parallel", "parallel", "arbitrary"))(*operands)


def mm_nn(name, a, b, out_dtype=f32, tm=1024, tn=1024, tk=2048, dep=None):
    M, K = a.shape
    tm, tk = _pick(M, tm), _pick(K, tk)
    if b.ndim == 3:
        S, _, n = b.shape
        tn = _pick(n, 2048)
        per = n // tn
        b_spec = pl.BlockSpec((None, tk, tn), lambda i, j, k: (j // per, k, j % per))
        N = S * n
    else:
        N = b.shape[1]
        tn = _pick(N, tn)
        b_spec = pl.BlockSpec((tk, tn), lambda i, j, k: (k, j))
    grid = (M // tm, N // tn, K // tk)
    return _mm_call(name, grid, [pl.BlockSpec((tm, tk), lambda i, j, k: (i, k)), b_spec],
                    pl.BlockSpec((tm, tn), lambda i, j, k: (i, j)), jax.ShapeDtypeStruct((M, N), out_dtype),
                    (tm, tn), NN, (a, b), dep)


def mm_nt(name, a, b, out_dtype=f32, tm=1024, tn=2048, tk=1024, dep=None):
    M, N = a.shape
    tm = _pick(M, tm)
    if b.ndim == 3:
        S, K, n = b.shape
        tr = _pick(n, 2048)
        per = n // tr
        tko = _pick(K, tk)
        b_spec = pl.BlockSpec((None, tko, tr), lambda i, j, k: (k // per, j, k % per))
    else:
        K = b.shape[0]
        tr = _pick(N, tn)
        tko = _pick(K, tk)
        b_spec = pl.BlockSpec((tko, tr), lambda i, j, k: (j, k))
    grid = (M // tm, K // tko, N // tr)
    return _mm_call(name, grid, [pl.BlockSpec((tm, tr), lambda i, j, k: (i, k)), b_spec],
                    pl.BlockSpec((tm, tko), lambda i, j, k: (i, j)), jax.ShapeDtypeStruct((M, K), out_dtype),
                    (tm, tko), NT, (a, b), dep)


def mm_tn(name, a, b, shards=None, out_dtype=f32, tm=1024, tn=1024, tl=2048, dep=None):
    L, K = a.shape
    N = b.shape[1]
    tl, tko = _pick(L, tl), _pick(K, tm)
    if shards:
        n = N // shards
        tn = _pick(n, 2048)
        per = n // tn
        o_spec = pl.BlockSpec((None, tko, tn), lambda i, j, k: (j // per, i, j % per))
        o_shape = jax.ShapeDtypeStruct((shards, K, n), out_dtype)
    else:
        tn = _pick(N, tn)
        o_spec = pl.BlockSpec((tko, tn), lambda i, j, k: (i, j))
        o_shape = jax.ShapeDtypeStruct((K, N), out_dtype)
    grid = (K // tko, N // tn, L // tl)
    return _mm_call(name, grid, [pl.BlockSpec((tl, tko), lambda i, j, k: (k, i)),
                                 pl.BlockSpec((tl, tn), lambda i, j, k: (k, j))],
                    o_spec, o_shape, (tko, tn), TN, (a, b), dep)


def mm_tn_half(name, a, b, col_sharded, half_arr, addend=None, dep=None, tn=2048):
    L, K = a.shape
    N = b.shape[1]
    S = N_CHIPS
    h, cols = (K // 2, N // S) if col_sharded else (K // S // 2, N)
    tko, tn = _pick(h, 1024), _pick(cols, tn)
    ni, nj = h // tko, cols // tn
    if col_sharded:
        a_map = lambda s, i, j, hf: (0, hf[0] * ni + i)
        b_map = lambda s, i, j, hf: (0, s * nj + j)
    else:
        a_map = lambda s, i, j, hf: (0, (2 * s + hf[0]) * ni + i)
        b_map = lambda s, i, j, hf: (0, j)
    o_spec = pl.BlockSpec((None, tko, tn), lambda s, i, j, hf: (s, i, j))
    n_add = 0 if addend is None else 1
    n_dep = 0 if dep is None else 1

    def body(hf_ref, a_ref, b_ref, *rest):
        o_ref = rest[n_add + n_dep]
        acc = _dot(a_ref[...], b_ref[...], TN)
        if n_add:
            acc = acc + rest[0][...].astype(f32)
        o_ref[...] = acc.astype(o_ref.dtype)

    in_specs = [pl.BlockSpec((L, tko), a_map), pl.BlockSpec((L, tn), b_map)] + [o_spec] * n_add
    in_specs, operands, _ = _with_dep(in_specs, [a, b] + ([addend] if n_add else []), dep)
    grid_spec = pltpu.PrefetchScalarGridSpec(num_scalar_prefetch=1, grid=(S, ni, nj), in_specs=in_specs, out_specs=o_spec)
    return pl.pallas_call(body, out_shape=jax.ShapeDtypeStruct((S, h, cols), bf16), grid_spec=grid_spec, name=name,
                          compiler_params=_params("parallel", "parallel", "parallel"))(half_arr, *operands)


def weight_grad(name, a, b, dep=None):
    if name in COL_SHARDED:
        return mm_tn("d" + name, a, b, shards=N_CHIPS, out_dtype=bf16, dep=dep)
    return mm_tn("d" + name, a, b, out_dtype=bf16, tm=a.shape[1] // N_CHIPS, dep=dep)


def ffn_hidden(hn, w_gate, w_up, tm=512):
    M, K = hn.shape
    S, _, n = w_gate.shape
    tm = _pick(M, tm)

    def body(a_ref, g_ref, u_ref, dg_ref, du_ref, hid_ref):
        a = a_ref[...]
        g = _dot(a, g_ref[...])
        u = _dot(a, u_ref[...])
        sg = sigmoid(g)
        act = g * sg
        dg_ref[...] = (u * (sg * (1.0 + g * (1.0 - sg)))).astype(bf16)
        du_ref[...] = act.astype(bf16)
        hid_ref[...] = (act * u).astype(bf16)

    w_spec = pl.BlockSpec((None, K, n), lambda s, i: (s, 0, 0))
    o_spec = pl.BlockSpec((tm, n), lambda s, i: (i, s))
    o = jax.ShapeDtypeStruct((M, S * n), bf16)
    return pl.pallas_call(
        body, out_shape=[o, o, o], grid=(S, M // tm), in_specs=[pl.BlockSpec((tm, K), lambda s, i: (i, 0)), w_spec, w_spec],
        out_specs=[o_spec, o_spec, o_spec], name="ffn_hidden", compiler_params=_params("parallel", "parallel"))(hn, w_gate, w_up)


def ffn_hidden_grad(d_ff, w_down, hid_dg, hid_du, tm=1024):
    M, D = d_ff.shape
    F = w_down.shape[0]
    n = _pick(F // N_CHIPS, 2048)
    tm = _pick(M, tm)

    def body(a_ref, b_ref, pg_ref, pu_ref, dg_ref, du_ref):
        dh = _dot(a_ref[...], b_ref[...], NT)
        dg_ref[...] = (dh * pg_ref[...].astype(f32)).astype(bf16)
        du_ref[...] = (dh * pu_ref[...].astype(f32)).astype(bf16)

    t_spec = pl.BlockSpec((tm, n), lambda j, i: (i, j))
    o = jax.ShapeDtypeStruct((M, F), bf16)
    return pl.pallas_call(
        body, out_shape=[o, o], grid=(F // n, M // tm),
        in_specs=[pl.BlockSpec((tm, D), lambda j, i: (i, 0)), pl.BlockSpec((n, D), lambda j, i: (j, 0)), t_spec, t_spec],
        out_specs=[t_spec, t_spec], name="ffn_hidden_grad", compiler_params=_params("parallel", "parallel"))(d_ff, w_down, hid_dg, hid_du)


def rowwise(name, fn, L, rows, bcast, outs, reds=(), tr=256, dep=None):
    tr = min(tr, L)
    nt = L // tr
    n_rows, n_b, n_o = len(rows), len(bcast), len(outs)
    n_dep = 0 if dep is None else 1

    def body(*refs):
        i = pl.program_id(0)
        ins = [r[...] for r in refs[:n_rows + n_b]]
        res = fn(i, nt, *ins)
        o_refs = refs[n_rows + n_b + n_dep:]
        for k in range(n_o):
            o_refs[k][...] = res[k].astype(o_refs[k].dtype)
        if reds:
            @pl.when(i == 0)
            def _():
                for k in range(len(reds)):
                    o_refs[n_o + k][...] = jnp.zeros_like(o_refs[n_o + k])
            for k in range(len(reds)):
                o_refs[n_o + k][...] += res[n_o + k]

    def row_spec(width, cb, shift):
        if shift:
            return pl.BlockSpec((tr, width), lambda i: (jnp.minimum(i + shift, nt - 1), cb))
        return pl.BlockSpec((tr, width), lambda i: (i, cb))

    in_specs = [row_spec(w, cb, sh) for (_, w, cb, sh) in rows]
    in_specs += [pl.BlockSpec(b.shape, lambda i: (0, 0)) for b in bcast]
    out_specs = [pl.BlockSpec((tr, w), lambda i: (i, 0)) for (w, _) in outs]
    out_specs += [pl.BlockSpec((1, w), lambda i: (0, 0)) for w in reds]
    out_shape = [jax.ShapeDtypeStruct((L, w), dt) for (w, dt) in outs]
    out_shape += [jax.ShapeDtypeStruct((1, w), f32) for w in reds]
    in_specs, operands, _ = _with_dep(in_specs, [r[0] for r in rows] + list(bcast), dep)
    return pl.pallas_call(
        body, out_shape=out_shape, grid=(nt,), in_specs=in_specs, out_specs=out_specs, name=name,
        compiler_params=_params("arbitrary"))(*operands)


def full(a):
    return (a, a.shape[1], 0, 0)


def colsum(v):
    return jnp.sum(v, axis=0, keepdims=True)


def rms_fwd(x, g):
    r = lax.rsqrt(jnp.mean(x * x, axis=-1, keepdims=True) + RMS_EPS)
    return x * r * g


def rms_bwd(x, g, dy):
    r = lax.rsqrt(jnp.mean(x * x, axis=-1, keepdims=True) + RMS_EPS)
    xh = x * r
    dyg = dy * g
    dx = r * (dyg - xh * jnp.mean(dyg * xh, axis=-1, keepdims=True))
    return dx, colsum(dy * xh)


GELU_C = math.sqrt(2.0 / math.pi)


def gelu(y):
    return y * (0.5 * (1.0 + jnp.tanh(GELU_C * (y + 0.044715 * (y * y * y)))))


def gelu_grad(y):
    t = jnp.tanh(GELU_C * (y + 0.044715 * (y * y * y)))
    return 0.5 * (1.0 + t) + 0.5 * y * (1.0 - t * t) * (GELU_C * (1.0 + 3 * 0.044715 * (y * y)))


def sigmoid(v):
    return 1.0 / (1.0 + jnp.exp(-v))


def _lane(shape):
    return lax.broadcasted_iota(jnp.int32, shape, 1)


def _rot_chunk(t, cos, sin_signed):
    first = (_lane(t.shape) % HEAD_DIM) < (HEAD_DIM // 2)
    partner = jnp.where(first, pltpu.roll(t, LANES - HEAD_DIM // 2, 1), pltpu.roll(t, HEAD_DIM // 2, 1))
    return t * cos + partner * sin_signed


def _cos_sin(pos, inv_freq, inverse):
    ang = pos * inv_freq
    cos, sin = jnp.cos(ang), jnp.sin(ang)
    first = (_lane(ang.shape) % HEAD_DIM) < (HEAD_DIM // 2)
    sign = jnp.where(first, -1.0, 1.0) * (-1.0 if inverse else 1.0)
    return cos, sin * sign


def _dup_head(chunk, odd):
    low = _lane(chunk.shape) < HEAD_DIM
    x = jnp.where(low != odd, chunk, 0.0)
    return x + pltpu.roll(x, HEAD_DIM, 1)


def _chunks(v):
    return [v[:, LANES * c:LANES * (c + 1)] for c in range(v.shape[1] // LANES)]


def qkv_prep(proj, pos, inv_freq, d_attn, d_kv):
    L = proj.shape[0]
    d_ssm = proj.shape[1] - d_attn - 2 * d_kv
    half = d_ssm // 2
    scale = 1.0 / math.sqrt(HEAD_DIM)

    def fn(i, nt, q, k, v, u0, u1, p, invf):
        cos, sin = _cos_sin(p, invf, False)
        qr = jnp.concatenate([_rot_chunk(c, cos, sin) for c in _chunks(q)], axis=1) * scale
        kr = [_rot_chunk(c, cos, sin) for c in _chunks(k)]
        kk = jnp.concatenate([_dup_head(c, odd) for c in kr for odd in (False, True)], axis=1)
        vv = jnp.concatenate([_dup_head(c, odd) for c in _chunks(v) for odd in (False, True)], axis=1)
        return qr, kk, vv, jnp.concatenate([u0, u1], axis=1)

    u_cb = (d_attn + 2 * d_kv) // half
    return rowwise("qkv_prep", fn, L,
                   [(proj, d_attn, 0, 0), (proj, d_kv, d_attn // d_kv, 0), (proj, d_kv, d_attn // d_kv + 1, 0),
                    (proj, half, u_cb, 0), (proj, half, u_cb + 1, 0), full(pos)],
                   [inv_freq], [(d_attn, bf16), (2 * d_kv, bf16), (2 * d_kv, bf16), (d_ssm, bf16)], tr=512)


def qkv_grad(dq, dkk_c, dkk_p, dvv_c, dvv_p, du, pos, inv_freq):
    L, d_attn = dq.shape
    d_kv = dkk_c.shape[1] // 2
    scale = 1.0 / math.sqrt(HEAD_DIM)

    def fold(cur, prev, i, nt):
        t = cur + jnp.where(i < nt - 1, prev, 0.0)
        out = []
        for c in range(t.shape[1] // (2 * LANES)):
            even, odd = t[:, 2 * c * LANES:(2 * c + 1) * LANES], t[:, (2 * c + 1) * LANES:(2 * c + 2) * LANES]
            even, odd = even + pltpu.roll(even, HEAD_DIM, 1), odd + pltpu.roll(odd, HEAD_DIM, 1)
            out.append(jnp.where(_lane(even.shape) < HEAD_DIM, even, odd))
        return out

    def fn(i, nt, dq_t, kc, kp, vc, vp, du_t, p, invf):
        cos, sin = _cos_sin(p, invf, True)
        dq_o = jnp.concatenate([_rot_chunk(c, cos, sin) for c in _chunks(dq_t)], axis=1) * scale
        dk_o = jnp.concatenate([_rot_chunk(c, cos, sin) for c in fold(kc, kp, i, nt)], axis=1)
        dv_o = jnp.concatenate(fold(vc, vp, i, nt), axis=1)
        return (jnp.concatenate([dq_o, dk_o, dv_o, du_t], axis=1),)

    return rowwise("qkv_grad", fn, L,
                   [full(dq), full(dkk_c), (dkk_p, 2 * d_kv, 0, 1), full(dvv_c), (dvv_p, 2 * d_kv, 0, 1), full(du), full(pos)],
                   [inv_freq], [(d_attn + 2 * d_kv + du.shape[1], bf16)], tr=ATTN_BLOCK)[0]


def _attn_specs(L):
    nb = L // ATTN_BLOCK
    B = ATTN_BLOCK
    q_spec = lambda width: pl.BlockSpec((B, width), lambda n: (n, 0))
    prev = lambda width: pl.BlockSpec((B, width), lambda n: (jnp.maximum(n - 1, 0), 0))
    return nb, q_spec, prev


def _attn_mask(n):
    B = ATTN_BLOCK
    row = lax.broadcasted_iota(jnp.int32, (B, 2 * B), 0)
    col = lax.broadcasted_iota(jnp.int32, (B, 2 * B), 1)
    return ((col < B) & (col > row) & (n > 0)) | ((col >= B) & (row >= col - B))


def _attn_probs(qm, kcat, sink, mask):
    s = jnp.where(mask, _dot(qm, kcat, NT), NEG)
    m = jnp.maximum(jnp.max(s, axis=1, keepdims=True), sink)
    p, ps = jnp.exp(s - m), jnp.exp(sink - m)
    inv = 1.0 / (jnp.sum(p, axis=1, keepdims=True) + ps)
    return p, inv, ps


def _attn_heads(q_ref, s_ref, h, q_per_kv):
    low = _lane((ATTN_BLOCK, LANES)) < HEAD_DIM
    heads = []
    for pr in range(h * q_per_kv // 2, (h + 1) * q_per_kv // 2):
        q2 = q_ref[:, LANES * pr:LANES * (pr + 1)]
        for odd in (False, True):
            mine = low != odd
            sink = jnp.max(s_ref[2 * pr + int(odd):2 * pr + int(odd) + 1, :], axis=1, keepdims=True)
            heads.append((pr, mine, jnp.where(mine, q2, jnp.zeros_like(q2)), sink))
    return low, heads


def _kv_block(prev_ref, cur_ref, h):
    return jnp.concatenate([prev_ref[:, LANES * h:LANES * (h + 1)], cur_ref[:, LANES * h:LANES * (h + 1)]], axis=0)


def attn_fwd(qr, kk, vv, sink_b):
    L, d_attn = qr.shape
    nb, q_spec, prev = _attn_specs(L)
    d_kk = kk.shape[1]
    n_kv = d_kk // LANES
    q_per_kv = d_attn // HEAD_DIM // n_kv

    def body(q_ref, kc_ref, kp_ref, vc_ref, vp_ref, s_ref, o_ref):
        mask = _attn_mask(pl.program_id(0))
        for h in range(n_kv):
            kcat, vcat = _kv_block(kp_ref, kc_ref, h), _kv_block(vp_ref, vc_ref, h)
            low, heads = _attn_heads(q_ref, s_ref, h, q_per_kv)
            probs = [_attn_probs(qm, kcat, sink, mask) for (_, _, qm, sink) in heads]
            outs = [_dot(p.astype(bf16), vcat) * inv for (p, inv, _) in probs]
            for i in range(0, len(heads), 2):
                pr = heads[i][0]
                o_ref[:, LANES * pr:LANES * (pr + 1)] = jnp.where(low, outs[i], outs[i + 1])

    return pl.pallas_call(
        body, out_shape=jax.ShapeDtypeStruct((L, d_attn), f32), grid=(nb,),
        in_specs=[q_spec(d_attn), q_spec(d_kk), prev(d_kk), q_spec(d_kk), prev(d_kk), pl.BlockSpec(sink_b.shape, lambda n: (0, 0))],
        out_specs=q_spec(d_attn), name="attn_fwd", compiler_params=_params("arbitrary"))(qr, kk, kk, vv, vv, sink_b)


def attn_bwd(qr, kk, vv, sink_b, attn, d_attn_out):
    L, d_attn = qr.shape
    nb, q_spec, prev = _attn_specs(L)
    d_kk = kk.shape[1]
    n_kv = d_kk // LANES
    q_per_kv = d_attn // HEAD_DIM // n_kv

    def body(q_ref, kc_ref, kp_ref, vc_ref, vp_ref, s_ref, o_ref, do_ref, dq_ref, dkc_ref, dkp_ref, dvc_ref, dvp_ref, ds_ref):
        n = pl.program_id(0)
        B = ATTN_BLOCK
        mask = _attn_mask(n)
        srow = lax.broadcasted_iota(jnp.int32, (SUBLANES, LANES), 0)

        @pl.when(n == 0)
        def _():
            ds_ref[...] = jnp.zeros_like(ds_ref)

        for h in range(n_kv):
            kcat, vcat = _kv_block(kp_ref, kc_ref, h), _kv_block(vp_ref, vc_ref, h)
            low, heads = _attn_heads(q_ref, s_ref, h, q_per_kv)
            probs = [_attn_probs(qm, kcat, sink, mask) for (_, _, qm, sink) in heads]
            dk = jnp.zeros((2 * B, LANES), f32)
            dv = dk
            dsink = jnp.zeros((SUBLANES, LANES), f32)
            dqs = []
            for i, ((pr, mine, qm, _), (p, inv, ps)) in enumerate(zip(heads, probs)):
                do2 = do_ref[:, LANES * pr:LANES * (pr + 1)]
                delta = jnp.sum(jnp.where(mine, do2 * o_ref[:, LANES * pr:LANES * (pr + 1)], 0.0), axis=1, keepdims=True)
                dob = jnp.where(mine, do2, 0.0).astype(bf16)
                p = p * inv
                ds = (p * (_dot(dob, vcat, NT) - delta)).astype(bf16)
                dqs.append(_dot(ds, kcat))
                dk = dk + _dot(ds, qm, TN)
                dv = dv + _dot(p.astype(bf16), dob, TN)
                dsink = dsink + jnp.where(srow == i, -jnp.sum(ps * inv * delta), 0.0)
            for i in range(0, len(heads), 2):
                pr = heads[i][0]
                dq_ref[:, LANES * pr:LANES * (pr + 1)] = jnp.where(low, dqs[i], dqs[i + 1])
            cols = slice(LANES * h, LANES * (h + 1))
            dkp_ref[:, cols] = dk[:B]
            dkc_ref[:, cols] = dk[B:]
            dvp_ref[:, cols] = dv[:B]
            dvc_ref[:, cols] = dv[B:]
            ds_ref[h] += dsink

    kv_shape = jax.ShapeDtypeStruct(kk.shape, f32)
    ds_shape = (n_kv, SUBLANES, LANES)
    return pl.pallas_call(
        body,
        out_shape=[jax.ShapeDtypeStruct((L, d_attn), f32), kv_shape, kv_shape, kv_shape, kv_shape, jax.ShapeDtypeStruct(ds_shape, f32)],
        grid=(nb,),
        in_specs=[q_spec(d_attn), q_spec(d_kk), prev(d_kk), q_spec(d_kk), prev(d_kk), pl.BlockSpec(sink_b.shape, lambda n: (0, 0)),
                  q_spec(d_attn), q_spec(d_attn)],
        out_specs=[q_spec(d_attn)] + [q_spec(d_kk)] * 4 + [pl.BlockSpec(ds_shape, lambda n: (0, 0, 0))],
        name="attn_bwd", compiler_params=_params("arbitrary"))(qr, kk, kk, vv, vv, sink_b, attn, d_attn_out)


SSM_T = 128
NQ = SUBLANES * SSM_STATE // LANES
NJ = SUBLANES


SCAN_UNROLL = 8


def _unrolled(step):
    def body(k, carry):
        for u in range(SCAN_UNROLL):
            carry = step(k * SCAN_UNROLL + u, carry)
        return carry
    return body


def _strided_put(ref, j, val):
    for q in range(NQ):
        ref.at[q][pl.ds(j, SSM_T, stride=NJ), :] = val[:, LANES * q:LANES * (q + 1)]


def _strided_get(ref, j):
    return jnp.concatenate([ref.at[q][pl.ds(j, SSM_T, stride=NJ), :] for q in range(NQ)], axis=1)


def _ssm_specs(L, rev):
    nt = L // SSM_T
    idx = (lambda i: nt - 1 - i) if rev else (lambda i: i)
    row = lambda w, cb=0: pl.BlockSpec((SSM_T, w), lambda i: (idx(i), cb))
    state = pl.BlockSpec((NQ, SSM_T * NJ, LANES), lambda i: (0, idx(i), 0))
    whole = lambda a: pl.BlockSpec(a.shape, lambda i: (0,) * a.ndim)
    return nt, row, state, whole


def ssm_fwd(u_bf, proj, u_cb, bd_re, bd_im, cd_re, cd_im, lam_re, lam_im, d_skip, dep=None):
    L, d_ssm = u_bf.shape
    nt, row, state, whole = _ssm_specs(L, False)
    half = d_ssm // 2
    gw = d_ssm // NJ
    n_dep = 0 if dep is None else 1

    def body(u_ref, u0_ref, u1_ref, bdr, bdi, cdr, cdi, lr_ref, li_ref, d_ref, *rest):
        y_ref, z_ref, sr_ref, si_ref, carry = rest[n_dep:]
        i = pl.program_id(0)

        @pl.when(i == 0)
        def _():
            carry[...] = jnp.zeros_like(carry)

        for j in range(NJ):
            uj = u_ref[:, gw * j:gw * (j + 1)]
            _strided_put(sr_ref, j, _dot(uj, bdr[j]))
            _strided_put(si_ref, j, _dot(uj, bdi[j]))
        lr = [lr_ref[q] for q in range(NQ)]
        li = [li_ref[q] for q in range(NQ)]

        def step(t, s):
            sr, si = s
            rows = pl.ds(pl.multiple_of(t * NJ, NJ), NJ)
            nr = tuple(lr[q] * sr[q] - li[q] * si[q] + sr_ref[q, rows, :] for q in range(NQ))
            ni = tuple(lr[q] * si[q] + li[q] * sr[q] + si_ref[q, rows, :] for q in range(NQ))
            for q in range(NQ):
                sr_ref[q, rows, :] = nr[q]
                si_ref[q, rows, :] = ni[q]
            return nr, ni

        init = (tuple(carry[0, q] for q in range(NQ)), tuple(carry[1, q] for q in range(NQ)))
        sr, si = lax.fori_loop(0, SSM_T // SCAN_UNROLL, _unrolled(step), init)
        for q in range(NQ):
            carry[0, q] = sr[q]
            carry[1, q] = si[q]
        uf = jnp.concatenate([u0_ref[...], u1_ref[...]], axis=1)
        for j in range(NJ):
            cols = slice(gw * j, gw * (j + 1))
            yj = _dot(_strided_get(sr_ref, j).astype(bf16), cdr[j]) - _dot(_strided_get(si_ref, j).astype(bf16), cdi[j])
            yj = yj + d_ref[:, cols] * uf[:, cols]
            y_ref[:, cols] = yj
            z_ref[:, cols] = gelu(yj).astype(bf16)

    s_shape = jax.ShapeDtypeStruct((NQ, L * NJ, LANES), f32)
    consts = (bd_re, bd_im, cd_re, cd_im, lam_re, lam_im, d_skip)
    in_specs, operands, _ = _with_dep([row(d_ssm), row(half, u_cb), row(half, u_cb + 1)] + [whole(a) for a in consts],
                                      [u_bf, proj, proj, *consts], dep)
    return pl.pallas_call(
        body, out_shape=[jax.ShapeDtypeStruct((L, d_ssm), f32), jax.ShapeDtypeStruct((L, d_ssm), bf16), s_shape, s_shape], grid=(nt,),
        in_specs=in_specs, out_specs=[row(d_ssm), row(d_ssm), state, state],
        scratch_shapes=[pltpu.VMEM((2, NQ, NJ, LANES), f32)], name="ssm_fwd",
        compiler_params=_params("arbitrary"))(*operands)


def ssm_bwd(y, dz1, dz2, u_bf, proj, u_cb, s_re, s_im, bd_re, bd_im, cd_re, cd_im, lam_re, lam_im, d_skip, dep=None):
    L, d_ssm = y.shape
    nt, row, state, whole = _ssm_specs(L, True)
    half = d_ssm // 2
    gw = d_ssm // NJ
    n_dep = 0 if dep is None else 1

    def body(y_ref, dz1_ref, dz2_ref, u_ref, u0_ref, u1_ref, sr_ref, si_ref, bdr, bdi, cdr, cdi, lr_ref, li_ref, d_ref, *rest):
        du_ref, fbr, fbi, fcr, fci, dlr, dli, dd_ref, gr_ref, gi_ref, carry, dbdr, dbdi, dcdr, dcdi = rest[n_dep:]
        i = pl.program_id(0)

        @pl.when(i == 0)
        def _():
            carry[...] = jnp.zeros_like(carry)
            for r in (dbdr, dbdi, dcdr, dcdi, dlr, dli, dd_ref):
                r[...] = jnp.zeros_like(r)

        dyf = (dz1_ref[...] + dz2_ref[...]) * gelu_grad(y_ref[...])
        dyb = dyf.astype(bf16)
        for j in range(NJ):
            dyj = dyb[:, gw * j:gw * (j + 1)]
            _strided_put(gr_ref, j, _dot(dyj, cdr[j], NT))
            _strided_put(gi_ref, j, -_dot(dyj, cdi[j], NT))
            dcdr[j] += _dot(dyj, _strided_get(sr_ref, j).astype(bf16), TN)
            dcdi[j] -= _dot(dyj, _strided_get(si_ref, j).astype(bf16), TN)
        lr = [lr_ref[q] for q in range(NQ)]
        li = [li_ref[q] for q in range(NQ)]

        def step(k, c):
            gr, gi, ar, ai = c
            rows = pl.ds(pl.multiple_of((SSM_T - 1 - k) * NJ, NJ), NJ)
            s_r = [sr_ref[q, rows, :] for q in range(NQ)]
            s_i = [si_ref[q, rows, :] for q in range(NQ)]
            ar = tuple(ar[q] + gr[q] * s_r[q] + gi[q] * s_i[q] for q in range(NQ))
            ai = tuple(ai[q] + gi[q] * s_r[q] - gr[q] * s_i[q] for q in range(NQ))
            nr = tuple(gr_ref[q, rows, :] + lr[q] * gr[q] + li[q] * gi[q] for q in range(NQ))
            ni = tuple(gi_ref[q, rows, :] + lr[q] * gi[q] - li[q] * gr[q] for q in range(NQ))
            for q in range(NQ):
                gr_ref[q, rows, :] = nr[q]
                gi_ref[q, rows, :] = ni[q]
            return nr, ni, ar, ai

        zero = tuple(jnp.zeros((NJ, LANES), f32) for _ in range(NQ))
        init = (tuple(carry[0, q] for q in range(NQ)), tuple(carry[1, q] for q in range(NQ)), zero, zero)
        gr, gi, ar, ai = lax.fori_loop(0, SSM_T // SCAN_UNROLL, _unrolled(step), init)
        for q in range(NQ):
            carry[0, q] = gr[q]
            carry[1, q] = gi[q]
            dlr[q] += ar[q]
            dli[q] += ai[q]
        uf = jnp.concatenate([u0_ref[...], u1_ref[...]], axis=1)
        dd_ref[...] += colsum(dyf * uf)
        for j in range(NJ):
            cols = slice(gw * j, gw * (j + 1))
            gjr, gji = _strided_get(gr_ref, j).astype(bf16), _strided_get(gi_ref, j).astype(bf16)
            du_ref[:, cols] = _dot(gjr, bdr[j], NT) + _dot(gji, bdi[j], NT) + d_ref[:, cols] * dyf[:, cols]
            uj = u_ref[:, cols]
            dbdr[j] += _dot(uj, gjr, TN)
            dbdi[j] += _dot(uj, gji, TN)

        @pl.when(i == nt - 1)
        def _():
            nb = NJ * SSM_STATE
            diag = (lax.broadcasted_iota(jnp.int32, (gw, nb), 0) // SSM_GROUP) == (lax.broadcasted_iota(jnp.int32, (gw, nb), 1) // SSM_STATE)
            for j in range(NJ):
                for acc_ref, out in ((dbdr, fbr), (dbdi, fbi), (dcdr, fcr), (dcdi, fci)):
                    m = jnp.where(diag, acc_ref[j], 0.0)
                    f = m[:, :LANES]
                    for q in range(1, nb // LANES):
                        f = f + m[:, LANES * q:LANES * (q + 1)]
                    out[j] = f + pltpu.roll(f, SSM_STATE, 1)

    consts = (bd_re, bd_im, cd_re, cd_im, lam_re, lam_im, d_skip)
    acc = lambda a: jax.ShapeDtypeStruct(a.shape, f32)
    fb = jax.ShapeDtypeStruct((NJ, gw, LANES), f32)
    outs = [jax.ShapeDtypeStruct((L, d_ssm), f32), fb, fb, fb, fb, acc(lam_re), acc(lam_im), acc(d_skip)]
    in_specs, operands, _ = _with_dep(
        [row(d_ssm)] * 4 + [row(half, u_cb), row(half, u_cb + 1), state, state] + [whole(a) for a in consts],
        [y, dz1, dz2, u_bf, proj, proj, s_re, s_im, *consts], dep)
    return pl.pallas_call(
        body, out_shape=outs, grid=(nt,),
        in_specs=in_specs, out_specs=[row(d_ssm)] + [whole(a) for a in outs[1:]],
        scratch_shapes=[pltpu.VMEM((NQ, SSM_T * NJ, LANES), f32), pltpu.VMEM((NQ, SSM_T * NJ, LANES), f32),
                        pltpu.VMEM((2, NQ, NJ, LANES), f32)] + [pltpu.VMEM(bd_re.shape, f32)] * 4,
        name="ssm_bwd", compiler_params=_params("arbitrary"))(*operands)


def _cmul(ar, ai, br, bi):
    return ar * br - ai * bi, ar * bi + ai * br


def _disc(ar, ai, logdt):
    dt = jnp.exp(logdt)
    mag = jnp.exp(ar * dt)
    lr, li = mag * jnp.cos(ai * dt), mag * jnp.sin(ai * dt)
    den = ar * ar + ai * ai
    nr, ni = lr - 1.0, li
    fr, fi = (nr * ar + ni * ai) / den, (ni * ar - nr * ai) / den
    return dt, lr, li, den, fr, fi


def ssm_params(a_re, a_im, logdt_b, bt_re, bt_im, spread):
    def body(ar_ref, ai_ref, ld_ref, br_ref, bi_ref, sp_ref, lr_ref, li_ref, or_ref, oi_ref):
        _, lr, li, _, fr, fi = _disc(ar_ref[...], ai_ref[...], ld_ref[...])
        lr_ref[...] = lr
        li_ref[...] = li
        fre = jnp.dot(sp_ref[...], fr, precision=HIGHEST, preferred_element_type=f32)
        fie = jnp.dot(sp_ref[...], fi, precision=HIGHEST, preferred_element_type=f32)
        o_r, o_i = _cmul(fre, fie, br_ref[...], bi_ref[...])
        or_ref[...] = o_r
        oi_ref[...] = o_i

    g = jax.ShapeDtypeStruct(a_re.shape, f32)
    b = jax.ShapeDtypeStruct(bt_re.shape, f32)
    return pl.pallas_call(body, out_shape=[g, g, b, b], name="ssm_params",
                          compiler_params=_params())(a_re, a_im, logdt_b, bt_re, bt_im, spread)


def ssm_params_grad(a_re, a_im, logdt_b, bt_re, bt_im, spread, gather, dlam_re, dlam_im, dbt_re, dbt_im):
    def body(ar_ref, ai_ref, ld_ref, br_ref, bi_ref, sp_ref, ga_ref, glr_ref, gli_ref, gbr_ref, gbi_ref,
             dar_ref, dai_ref, dld_ref, dbr_ref, dbi_ref):
        ar, ai = ar_ref[...], ai_ref[...]
        dt, lr, li, den, fr, fi = _disc(ar, ai, ld_ref[...])
        hdot = functools.partial(jnp.dot, precision=HIGHEST, preferred_element_type=f32)
        fre, fie = hdot(sp_ref[...], fr), hdot(sp_ref[...], fi)
        gbr, gbi, br, bi = gbr_ref[...], gbi_ref[...], br_ref[...], bi_ref[...]
        dbr_ref[...], dbi_ref[...] = _cmul(fre, -fie, gbr, gbi)
        t_r, t_i = _cmul(br, -bi, gbr, gbi)
        gfr, gfi = hdot(ga_ref[...], t_r), hdot(ga_ref[...], t_i)
        iwr, iwi = ar / den, -ai / den
        x_r, x_i = _cmul(iwr, -iwi, gfr, gfi)
        glr, gli = glr_ref[...] + x_r, gli_ref[...] + x_i
        q_r, q_i = _cmul(fr, fi, iwr, iwi)
        gwr, gwi = _cmul(-q_r, q_i, gfr, gfi)
        y_r, y_i = _cmul(dt * lr, -dt * li, glr, gli)
        dar_ref[...] = gwr + y_r
        dai_ref[...] = gwi + y_i
        wl_r, wl_i = _cmul(ar, ai, lr, li)
        z_r, _ = _cmul(wl_r, -wl_i, glr, gli)
        dld_ref[...] = jnp.sum(z_r * dt, axis=1, keepdims=True)

    g = jax.ShapeDtypeStruct(a_re.shape, f32)
    b = jax.ShapeDtypeStruct(bt_re.shape, f32)
    return pl.pallas_call(body, out_shape=[g, g, jax.ShapeDtypeStruct((a_re.shape[0], 1), f32), b, b], name="ssm_params_grad",
                          compiler_params=_params())(a_re, a_im, logdt_b, bt_re, bt_im, spread, gather, dlam_re, dlam_im, dbt_re, dbt_im)


def _block_diag(t, rows, cols):
    G = t.shape[0]
    t = t.reshape(G // NJ, NJ, rows, cols)
    eye = jnp.eye(NJ, dtype=t.dtype)
    return jnp.einsum('jgrc,gh->jgrhc', t, eye).reshape(G // NJ, NJ * rows, NJ * cols)


def _state_layout(t):
    return t.reshape(NJ, NQ, LANES).transpose(1, 0, 2)


def _state_layout_inv(t, G, N):
    return t.transpose(1, 0, 2).reshape(G, N)


def _tiles2d(shape, budget_rows=128):
    rows, cols = shape
    tr = rows
    if rows > budget_rows:
        tr = budget_rows
        while rows % tr:
            tr -= SUBLANES
    return tr, cols


ADAM_TILE_BYTES = 3 << 19
ROW_ALIGN = 16


def _tile_rows(rows, row_bytes, target_bytes):
    tr = max(ROW_ALIGN, min(rows, target_bytes // row_bytes) // ROW_ALIGN * ROW_ALIGN)
    while rows % tr:
        tr -= ROW_ALIGN
    return tr


def _adam_update(w, g, m, v):
    c1 = 1.0 - ADAM_B1 ** ADAM_STEP
    c2 = 1.0 - ADAM_B2 ** ADAM_STEP
    nm = ADAM_B1 * m + (1.0 - ADAM_B1) * g
    nv = ADAM_B2 * v + (1.0 - ADAM_B2) * (g * g)
    delta = -ADAM_LR * ((nm / c1) / (jnp.sqrt(nv / c2) + ADAM_EPS) + ADAM_WD * w)
    return delta, nm, nv


def adamw_many(name, ws, gs, ms, vs):
    n = len(ws)

    def body(*refs):
        w, g, m, v = (refs[k * n:(k + 1) * n] for k in range(4))
        d, nm, nv = (refs[(4 + k) * n:(5 + k) * n] for k in range(3))
        for i in range(n):
            d[i][...], nm[i][...], nv[i][...] = _adam_update(w[i][...], g[i][...], m[i][...], v[i][...])

    o = [jax.ShapeDtypeStruct(a.shape, f32) for a in ws]
    outs = pl.pallas_call(body, out_shape=o * 3, name=name, compiler_params=_params())(*ws, *gs, *ms, *vs)
    return outs[:n], outs[n:2 * n], outs[2 * n:]


def adamw_halves(name, w, own, got, m, v, c_arr):
    h, cols = own.shape
    tr = _tile_rows(h, cols * 4, ADAM_TILE_BYTES)
    nh = h // tr

    def body(c_ref, w_ref, own_ref, got_ref, m_ref, v_ref, g_ref, d_ref, nm_ref, nv_ref):
        mine = (pl.program_id(0) // nh) == c_ref[0]
        g = jnp.where(mine, own_ref[...], got_ref[...])
        g_ref[...] = g
        d_ref[...], nm_ref[...], nv_ref[...] = _adam_update(w_ref[...], g, m_ref[...], v_ref[...])

    spec = pl.BlockSpec((tr, cols), lambda i, c: (i, 0))
    own_spec = pl.BlockSpec((tr, cols), lambda i, c: (jnp.where(i // nh == c[0], i % nh, 0), 0))
    got_spec = pl.BlockSpec((tr, cols), lambda i, c: (jnp.where(i // nh == c[0], 0, i % nh), 0))
    o = jax.ShapeDtypeStruct(w.shape, f32)
    grid_spec = pltpu.PrefetchScalarGridSpec(num_scalar_prefetch=1, grid=(2 * nh,),
                                             in_specs=[spec, own_spec, got_spec, spec, spec], out_specs=[spec] * 4)
    return pl.pallas_call(body, out_shape=[o, o, o, o], grid_spec=grid_spec, name=name,
                          compiler_params=_params("arbitrary"))(c_arr, w, own, got, m, v)


def pair_sum(name, g, got, c_arr):
    S, h, cols = got.shape
    tr, _ = _tiles2d((h, cols), 1024)
    nh = h // tr

    def body(c_ref, g_ref, o_ref, out_ref):
        out_ref[...] = (g_ref[...].astype(f32) + o_ref[...].astype(f32)).astype(out_ref.dtype)

    spec = pl.BlockSpec((None, tr, cols), lambda s, i, c: (s, i, 0))
    grid_spec = pltpu.PrefetchScalarGridSpec(
        num_scalar_prefetch=1, grid=(S, nh),
        in_specs=[pl.BlockSpec((None, tr, cols), lambda s, i, c: (s, c[0] * nh + i, 0)), spec], out_specs=spec)
    return pl.pallas_call(body, out_shape=jax.ShapeDtypeStruct(got.shape, g.dtype), grid_spec=grid_spec, name=name,
                          compiler_params=_params("parallel", "parallel"))(c_arr, g, got)


def chip_sum(name, pair, landed, mine_arr, dep=None):
    n_in, h, cols = landed.shape
    tr, _ = _tiles2d((h, cols), 256)

    def body(s_ref, p_ref, l_ref, *rest):
        acc = p_ref[...].astype(f32)
        for k in range(n_in):
            acc = acc + l_ref[k].astype(f32)
        rest[-1][...] = acc

    in_specs, operands, _ = _with_dep(
        [pl.BlockSpec((None, tr, cols), lambda i, s: (s[0], i, 0)), pl.BlockSpec((n_in, tr, cols), lambda i, s: (0, i, 0))],
        [pair, landed], dep)
    grid_spec = pltpu.PrefetchScalarGridSpec(num_scalar_prefetch=1, grid=(h // tr,), in_specs=in_specs,
                                             out_specs=pl.BlockSpec((tr, cols), lambda i, s: (i, 0)))
    return pl.pallas_call(body, out_shape=jax.ShapeDtypeStruct((h, cols), f32), grid_spec=grid_spec, name=name,
                          compiler_params=_params("parallel"))(mine_arr, *operands)


def into_slot(name, w, slot_arr, n_slots, dtype, dep=None):
    tr, cols = _tiles2d(w.shape, 256)

    def body(s_ref, w_ref, *rest):
        rest[-1][...] = w_ref[...].astype(dtype)

    in_specs, operands, _ = _with_dep([pl.BlockSpec((tr, cols), lambda i, s: (i, 0))], [w], dep)
    grid_spec = pltpu.PrefetchScalarGridSpec(num_scalar_prefetch=1, grid=(w.shape[0] // tr,), in_specs=in_specs,
                                             out_specs=pl.BlockSpec((None, tr, cols), lambda i, s: (s[0], i, 0)))
    return pl.pallas_call(body, out_shape=jax.ShapeDtypeStruct((n_slots,) + w.shape, dtype), grid_spec=grid_spec, name=name,
                          compiler_params=_params("parallel"))(slot_arr, *operands)


def sum_slots(name, t):
    S, rows, cols = t.shape
    tr, _ = _tiles2d((rows, cols), 256)

    def body(t_ref, o_ref):
        acc = t_ref[0]
        for s in range(1, S):
            acc = acc + t_ref[s]
        o_ref[...] = acc

    return pl.pallas_call(body, out_shape=jax.ShapeDtypeStruct((rows, cols), f32), grid=(rows // tr,),
                          in_specs=[pl.BlockSpec((S, tr, cols), lambda i: (0, i, 0))], out_specs=pl.BlockSpec((tr, cols), lambda i: (i, 0)),
                          name=name, compiler_params=_params("parallel"))(t)


def _place():
    x, y, c = lax.axis_index("x"), lax.axis_index("y"), lax.axis_index("c")
    return x, y, c


def _other_chips(x, y):
    return [(1 - x, y, 2 * (1 - x) + y), (x, 1 - y, 2 * x + 1 - y), (1 - x, 1 - y, 2 * (1 - x) + 1 - y)]


SEM = pl.BlockSpec(memory_space=pltpu.SEMAPHORE)
VM = pl.BlockSpec(memory_space=pltpu.VMEM)
DATAFLOW = pltpu.SideEffectType.DATAFLOW_SIDE_EFFECTING
TOKEN = jax.ShapeDtypeStruct((SUBLANES, LANES), f32)


def _gather_copy(buf, w, k, chip, c, mine, send, recv):
    px, py, _ = chip
    h = buf.shape[1] // 2
    half = buf.at[mine, pl.ds(c * h, h), :]
    return pltpu.make_async_remote_copy(src_ref=half, dst_ref=half, send_sem=send.at[3 * w + k], recv_sem=recv.at[3 * w + k],
                                        device_id=(px, py, c), device_id_type=MESH)


def _gather_landing(buf, w, k, chip, c, send, recv):
    px, py, s = chip
    h = buf.shape[1] // 2
    landed = buf.at[s, pl.ds(c * h, h), :]
    return pltpu.make_async_remote_copy(src_ref=landed, dst_ref=landed, send_sem=send.at[3 * w + k], recv_sem=recv.at[3 * w + k],
                                        device_id=(px, py, c), device_id_type=MESH)


def gather_start(name, bufs, groups, after, neighbours_only=()):
    nw, ng = len(bufs), len(groups)

    def body(*refs):
        outs = refs[nw + 1:]
        sems, dst = outs[:2 * ng], outs[2 * ng:2 * ng + nw]
        token = outs[2 * ng + nw]
        x, y, c = _place()
        mine = 2 * x + y
        for g, members in enumerate(groups):
            for i, w in enumerate(members):
                for k, chip in enumerate(_other_chips(x, y)[:2 if w in neighbours_only else 3]):
                    _gather_copy(dst[w], i, k, chip, c, mine, sems[2 * g], sems[2 * g + 1]).start()
        token[...] = jnp.zeros_like(token)

    sem_shapes = []
    for members in groups:
        sem_shapes += [pltpu.SemaphoreType.DMA((3 * len(members),))] * 2
    outs = pl.pallas_call(
        body, out_shape=sem_shapes + [jax.ShapeDtypeStruct(b.shape, b.dtype) for b in bufs] + [TOKEN],
        in_specs=[ANY] * (nw + 1), out_specs=[SEM] * (2 * ng) + [ANY] * nw + [VM],
        input_output_aliases={w: 2 * ng + w for w in range(nw)}, name=name,
        compiler_params=pltpu.CompilerParams(has_side_effects=DATAFLOW))(*bufs, after)
    return [(outs[2 * g], outs[2 * g + 1]) for g in range(ng)], list(outs[2 * ng:2 * ng + nw]), outs[2 * ng + nw]


def gather_wait(name, bufs, send, recv, after):
    nw = len(bufs)

    def body(*refs):
        src = refs[:nw]
        send_ref, recv_ref = refs[nw], refs[nw + 1]
        x, y, c = _place()
        mine = 2 * x + y
        for w in range(nw):
            for k, chip in enumerate(_other_chips(x, y)):
                _gather_copy(src[w], w, k, chip, c, mine, send_ref, recv_ref).wait_send()
                _gather_landing(src[w], w, k, chip, c, send_ref, recv_ref).wait_recv()

    return pl.pallas_call(
        body, out_shape=[jax.ShapeDtypeStruct(b.shape, b.dtype) for b in bufs],
        in_specs=[ANY] * nw + [SEM, SEM, ANY], out_specs=[ANY] * nw,
        input_output_aliases={w: w for w in range(nw)}, name=name,
        compiler_params=pltpu.CompilerParams(has_side_effects=DATAFLOW))(*bufs, send, recv, after)


def _relay_copy(buf, w, j, x, y, c, send, recv, landing):
    chips = _other_chips(x, y)
    px, py, _ = chips[j]
    h = buf.shape[1] // 2
    q = h // 2
    s = chips[2][2] if landing else chips[1 - j][2]
    part = buf.at[s, pl.ds(c * h + j * q, q), :]
    return pltpu.make_async_remote_copy(src_ref=part, dst_ref=part, send_sem=send.at[2 * w + j], recv_sem=recv.at[2 * w + j],
                                        device_id=(px, py, c), device_id_type=MESH)


def _early_pass(buf, nw, w, k, x, y, c, send, recv, landing):
    s = _other_chips(x, y)[k][2]
    h = buf.shape[1] // 2
    part = buf.at[s, pl.ds(((1 - c) if landing else c) * h, h), :]
    i = 2 * nw + 2 * w + k
    return pltpu.make_async_remote_copy(src_ref=part, dst_ref=part, send_sem=send.at[i], recv_sem=recv.at[i],
                                        device_id=(x, y, 1 - c), device_id_type=MESH)


def gather_relay(name, bufs, sems, more, after):
    nw, nm = len(bufs), len(more)
    ns = 4 if nm else 2

    def body(*refs):
        ins, outs = refs[:nw + nm + 2 * nw + 1], refs[nw + nm + 2 * nw + 1:]
        src, d_sems = ins[:nw], ins[nw + nm:nw + nm + 2 * nw]
        r_send, r_recv = outs[:2]
        m_send, m_recv = outs[2:ns] if nm else (None, None)
        dst, mdst, token = outs[ns:ns + nw], outs[ns + nw:ns + nw + nm], outs[ns + nw + nm]
        x, y, c = _place()
        mine = 2 * x + y
        chips = _other_chips(x, y)
        for w in range(nw):
            for k in range(2):
                _gather_copy(src[w], 0, k, chips[k], c, mine, d_sems[2 * w], d_sems[2 * w + 1]).wait_send()
                _gather_landing(src[w], 0, k, chips[k], c, d_sems[2 * w], d_sems[2 * w + 1]).wait_recv()
            for j in range(2):
                _relay_copy(dst[w], w, j, x, y, c, r_send, r_recv, False).start()
            for k in range(2):
                _early_pass(dst[w], nw, w, k, x, y, c, r_send, r_recv, False).start()
        for w in range(nm):
            for k, chip in enumerate(chips):
                _gather_copy(mdst[w], w, k, chip, c, mine, m_send, m_recv).start()
        token[...] = jnp.zeros_like(token)

    sem_shapes = [pltpu.SemaphoreType.DMA((4 * nw,))] * 2 + [pltpu.SemaphoreType.DMA((3 * nm,))] * (ns - 2)
    flat_sems = [s for pair in sems for s in pair]
    outs = pl.pallas_call(
        body, out_shape=sem_shapes + [jax.ShapeDtypeStruct(b.shape, b.dtype) for b in list(bufs) + list(more)] + [TOKEN],
        in_specs=[ANY] * (nw + nm) + [SEM] * (2 * nw) + [ANY], out_specs=[SEM] * ns + [ANY] * (nw + nm) + [VM],
        input_output_aliases={i: ns + i for i in range(nw + nm)}, name=name,
        compiler_params=pltpu.CompilerParams(has_side_effects=DATAFLOW))(*bufs, *more, *flat_sems, after)
    m_sems = (outs[2], outs[3]) if nm else None
    return outs[0], outs[1], m_sems, list(outs[ns:ns + nw]), list(outs[ns + nw:ns + nw + nm]), outs[ns + nw + nm]


def gather_wait_relay(name, bufs, r_send, r_recv, after):
    nw = len(bufs)

    def body(*refs):
        src = refs[:nw]
        send_ref, recv_ref = refs[nw], refs[nw + 1]
        x, y, c = _place()
        for w in range(nw):
            for j in range(2):
                _relay_copy(src[w], w, j, x, y, c, send_ref, recv_ref, False).wait_send()
                _relay_copy(src[w], w, j, x, y, c, send_ref, recv_ref, True).wait_recv()
                _early_pass(src[w], nw, w, j, x, y, c, send_ref, recv_ref, False).wait_send()
                _early_pass(src[w], nw, w, j, x, y, c, send_ref, recv_ref, True).wait_recv()

    return pl.pallas_call(
        body, out_shape=[jax.ShapeDtypeStruct(b.shape, b.dtype) for b in bufs],
        in_specs=[ANY] * nw + [SEM, SEM, ANY], out_specs=[ANY] * nw,
        input_output_aliases={w: w for w in range(nw)}, name=name,
        compiler_params=pltpu.CompilerParams(has_side_effects=DATAFLOW))(*bufs, r_send, r_recv, after)


def gather_forward(name, bufs, which=(0, 1, 2)):
    nw = len(bufs)

    def body(*refs):
        dst = refs[nw:2 * nw]
        send, recv = refs[2 * nw:]
        x, y, c = _place()
        sib = (x, y, 1 - c)
        barrier = pltpu.get_barrier_semaphore()
        pl.semaphore_signal(barrier, inc=1, device_id=sib, device_id_type=MESH)
        pl.semaphore_wait(barrier, 1)
        cps = []
        for w in range(nw):
            h = dst[w].shape[1] // 2
            for k in which:
                s = _other_chips(x, y)[k][2]
                landed = dst[w].at[s, pl.ds(c * h, h), :]
                cp = pltpu.make_async_remote_copy(src_ref=landed, dst_ref=landed, send_sem=send.at[w, k], recv_sem=recv.at[w, k],
                                                  device_id=sib, device_id_type=MESH)
                cp.start()
                cps.append(cp)
        for w in range(nw):
            h = dst[w].shape[1] // 2
            for k in which:
                s = _other_chips(x, y)[k][2]
                other = dst[w].at[s, pl.ds((1 - c) * h, h), :]
                pltpu.make_async_remote_copy(src_ref=other, dst_ref=other, send_sem=send.at[w, k], recv_sem=recv.at[w, k],
                                             device_id=sib, device_id_type=MESH).wait_recv()
        for cp in cps:
            cp.wait_send()

    sem = pltpu.SemaphoreType.DMA((nw, 3))
    return pl.pallas_call(
        body, out_shape=[jax.ShapeDtypeStruct(b.shape, b.dtype) for b in bufs],
        in_specs=[ANY] * nw, out_specs=[ANY] * nw, input_output_aliases={w: w for w in range(nw)},
        scratch_shapes=[sem, sem], name=name,
        compiler_params=pltpu.CompilerParams(has_side_effects=True, collective_id=SIBLING_PAIR))(*bufs)


def _scatter_copy(src, dst, w, k, chip, c, send, recv):
    px, py, s = chip
    return pltpu.make_async_remote_copy(src_ref=src.at[s], dst_ref=dst.at[k], send_sem=send.at[3 * w + k], recv_sem=recv.at[3 * w + k],
                                        device_id=(px, py, c), device_id_type=MESH)


def scatter_start(name, parts):
    nw = len(parts)
    lands = [pltpu.with_memory_space_constraint(lax.empty((N_CHIPS - 1,) + p.shape[1:], p.dtype), pltpu.HBM) for p in parts]

    def body(*refs):
        outs = refs[2 * nw:]
        send, recv = outs[0], outs[1]
        src, dst, token = outs[2:2 + nw], outs[2 + nw:2 + 2 * nw], outs[2 + 2 * nw]
        x, y, c = _place()
        for w in range(nw):
            for k, chip in enumerate(_other_chips(x, y)):
                _scatter_copy(src[w], dst[w], w, k, chip, c, send, recv).start()
        token[...] = jnp.zeros_like(token)

    sem = pltpu.SemaphoreType.DMA((3 * nw,))
    outs = pl.pallas_call(
        body, out_shape=[sem, sem] + [jax.ShapeDtypeStruct(p.shape, p.dtype) for p in parts]
        + [jax.ShapeDtypeStruct(l.shape, l.dtype) for l in lands] + [TOKEN],
        in_specs=[ANY] * (2 * nw), out_specs=[SEM, SEM] + [ANY] * (2 * nw) + [VM],
        input_output_aliases={i: 2 + i for i in range(2 * nw)}, name=name,
        compiler_params=pltpu.CompilerParams(has_side_effects=DATAFLOW))(*parts, *lands)
    return outs[0], outs[1], list(outs[2:2 + nw]), list(outs[2 + nw:2 + 2 * nw]), outs[2 + 2 * nw]


def scatter_wait(name, parts, lands, send, recv, after):
    nw = len(parts)

    def body(*refs):
        src, dst = refs[:nw], refs[nw:2 * nw]
        send_ref, recv_ref = refs[2 * nw], refs[2 * nw + 1]
        x, y, c = _place()
        for w in range(nw):
            for k, chip in enumerate(_other_chips(x, y)):
                cp = _scatter_copy(src[w], dst[w], w, k, chip, c, send_ref, recv_ref)
                cp.wait_send()
                cp.wait_recv()

    outs = pl.pallas_call(
        body, out_shape=[jax.ShapeDtypeStruct(a.shape, a.dtype) for a in list(parts) + list(lands)],
        in_specs=[ANY] * (2 * nw) + [SEM, SEM, ANY], out_specs=[ANY] * (2 * nw),
        input_output_aliases={i: i for i in range(2 * nw)}, name=name,
        compiler_params=pltpu.CompilerParams(has_side_effects=DATAFLOW))(*parts, *lands, send, recv, after)
    return list(outs[:nw]), list(outs[nw:])


SIBLING_PAIR = 0


def _sibling_copy(src, dst, w, c, half_rows, send, recv, sib):
    if half_rows:
        h = src.shape[1] // 2
        src = src.at[:, pl.ds((1 - c) * h, h), :]
    return pltpu.make_async_remote_copy(src_ref=src, dst_ref=dst, send_sem=send.at[w], recv_sem=recv.at[w],
                                        device_id=sib, device_id_type=MESH)


def _landing(shape, dtype):
    return pltpu.with_memory_space_constraint(lax.empty(shape, dtype), pltpu.HBM)


def sibling_start(name, srcs, half_rows):
    nw = len(srcs)
    lands = [_landing((s.shape[0], s.shape[1] // 2, s.shape[2]) if half_rows else s.shape, s.dtype) for s in srcs]

    def body(*refs):
        outs = refs[2 * nw:]
        send, recv = outs[0], outs[1]
        src, dst, token = outs[2:2 + nw], outs[2 + nw:2 + 2 * nw], outs[2 + 2 * nw]
        x, y, c = _place()
        barrier = pltpu.get_barrier_semaphore()
        pl.semaphore_signal(barrier, inc=1, device_id=(x, y, 1 - c), device_id_type=MESH)
        pl.semaphore_wait(barrier, 1)
        for w in range(nw):
            _sibling_copy(src[w], dst[w], w, c, half_rows, send, recv, (x, y, 1 - c)).start()
        token[...] = jnp.zeros_like(token)

    sem = pltpu.SemaphoreType.DMA((nw,))
    outs = pl.pallas_call(
        body, out_shape=[sem, sem] + [jax.ShapeDtypeStruct(a.shape, a.dtype) for a in list(srcs) + lands] + [TOKEN],
        in_specs=[ANY] * (2 * nw), out_specs=[SEM, SEM] + [ANY] * (2 * nw) + [VM],
        input_output_aliases={i: 2 + i for i in range(2 * nw)}, name=name,
        compiler_params=pltpu.CompilerParams(has_side_effects=DATAFLOW, collective_id=SIBLING_PAIR))(*srcs, *lands)
    return outs[0], outs[1], list(outs[2:2 + nw]), list(outs[2 + nw:2 + 2 * nw]), outs[2 + 2 * nw]


def sibling_wait(name, srcs, lands, send, recv, half_rows, after):
    nw = len(srcs)

    def body(*refs):
        src, dst = refs[:nw], refs[nw:2 * nw]
        send_ref, recv_ref = refs[2 * nw], refs[2 * nw + 1]
        x, y, c = _place()
        for w in range(nw):
            cp = _sibling_copy(src[w], dst[w], w, c, half_rows, send_ref, recv_ref, (x, y, 1 - c))
            cp.wait_send()
            cp.wait_recv()

    outs = pl.pallas_call(
        body, out_shape=[jax.ShapeDtypeStruct(a.shape, a.dtype) for a in list(srcs) + list(lands)],
        in_specs=[ANY] * (2 * nw) + [SEM, SEM, ANY], out_specs=[ANY] * (2 * nw),
        input_output_aliases={i: i for i in range(2 * nw)}, name=name,
        compiler_params=pltpu.CompilerParams(has_side_effects=DATAFLOW))(*srcs, *lands, send, recv, after)
    return list(outs[:nw]), list(outs[nw:])


def _peer(x, y, c, r):
    return (1 - x if r & 4 else x, 1 - y if r & 2 else y, 1 - c if r & 1 else c)


def _everyone_copy(buf, r, x, y, c, send, recv, landing):
    px, py, pc = _peer(x, y, c, r)
    slot = buf.at[4 * px + 2 * py + pc] if landing else buf.at[4 * x + 2 * y + c]
    return pltpu.make_async_remote_copy(src_ref=slot, dst_ref=slot, send_sem=send.at[r - 1], recv_sem=recv.at[r - 1],
                                        device_id=(px, py, pc), device_id_type=MESH)


def everyone_start(name, buf):
    def body(buf_in, send, recv, buf_ref, token):
        x, y, c = _place()
        for r in range(1, N_DEV):
            _everyone_copy(buf_ref, r, x, y, c, send, recv, False).start()
        token[...] = jnp.zeros_like(token)

    sem = pltpu.SemaphoreType.DMA((N_DEV - 1,))
    return pl.pallas_call(
        body, out_shape=[sem, sem, jax.ShapeDtypeStruct(buf.shape, buf.dtype), TOKEN],
        in_specs=[ANY], out_specs=[SEM, SEM, ANY, VM], input_output_aliases={0: 2}, name=name,
        compiler_params=pltpu.CompilerParams(has_side_effects=DATAFLOW))(buf)


def everyone_wait(name, buf, send, recv, after):
    def body(buf_ref, send_ref, recv_ref, after_ref, out_ref):
        x, y, c = _place()
        for r in range(1, N_DEV):
            _everyone_copy(buf_ref, r, x, y, c, send_ref, recv_ref, False).wait_send()
            _everyone_copy(buf_ref, r, x, y, c, send_ref, recv_ref, True).wait_recv()

    return pl.pallas_call(
        body, out_shape=jax.ShapeDtypeStruct(buf.shape, buf.dtype), in_specs=[ANY, SEM, SEM, ANY], out_specs=ANY,
        input_output_aliases={0: 0}, name=name,
        compiler_params=pltpu.CompilerParams(has_side_effects=DATAFLOW))(buf, send, recv, after)


def local_step(x, pos, tgt, small, d_in, get_w, put_g, first_dep=None, tick=lambda name, after: None):
    L, D = x.shape
    d_kv = N_KV_HEADS * HEAD_DIM
    d_ssm = small["d_skip"].shape[1]
    d_attn = d_in - 2 * d_kv - d_ssm
    big = {}
    G = d_ssm // SSM_GROUP
    N, P = SSM_STATE, SSM_GROUP

    half_dim = HEAD_DIM // 2
    inv_freq = ROPE_THETA ** (-jnp.arange(half_dim, dtype=f32) / half_dim)
    inv_freq = jnp.tile(inv_freq, LANES // half_dim).reshape(1, LANES)
    sink_b = jnp.broadcast_to(small["sinks"].reshape(-1, 1), (small["sinks"].size, LANES))

    spread = jnp.repeat(jnp.eye(G, dtype=f32), P, axis=0)
    logdt_b = jnp.broadcast_to(small["log_dt"].reshape(G, 1), (G, N))
    bt_re = small["b_re"].reshape(G, N, P).transpose(0, 2, 1).reshape(G * P, N)
    bt_im = small["b_im"].reshape(G, N, P).transpose(0, 2, 1).reshape(G * P, N)
    a_re, a_im = small["a_re"].reshape(G, N), small["a_im"].reshape(G, N)
    lam_re, lam_im, bbt_re, bbt_im = ssm_params(a_re, a_im, logdt_b, bt_re, bt_im, spread)
    bd_re = _block_diag(bbt_re.reshape(G, P, N), P, N).astype(bf16)
    bd_im = _block_diag(bbt_im.reshape(G, P, N), P, N).astype(bf16)
    c_re = small["c_re"].reshape(G, P, N).transpose(0, 2, 1)
    c_im = small["c_im"].reshape(G, P, N).transpose(0, 2, 1)
    cd_re = _block_diag(c_re, N, P).astype(bf16)
    cd_im = _block_diag(c_im, N, P).astype(bf16)
    lam_re_l, lam_im_l = _state_layout(lam_re), _state_layout(lam_im)

    def k1(i, nt, xt, g):
        return (rms_fwd(xt, g),)
    xn = rowwise("pre_mix_norm", k1, L, [full(x)], [small["g_pre_mix"]], [(D, bf16)], tr=512, dep=first_dep)[0]
    big["w_in"] = get_w("w_in", xn)
    proj = mm_nn("proj_in", xn, big["w_in"])
    qr, kk, vv, u_bf = qkv_prep(proj, pos, inv_freq, d_attn, d_kv)
    attn = attn_fwd(qr, kk, vv, sink_b)
    u_cb = (d_attn + 2 * d_kv) // (d_ssm // 2)
    token = tick("attn", attn)
    y, z_bf, s_re, s_im = ssm_fwd(u_bf, proj, u_cb, bd_re, bd_im, cd_re, cd_im, lam_re_l, lam_im_l, small["d_skip"], dep=token)
    token = tick("ssm", z_bf)
    big["w_glu"] = get_w("w_glu", z_bf)
    gl = mm_nn("glu_proj", z_bf, big["w_glu"], dep=token)

    def k6(i, nt, at, yt, glt, bg, ga, gs):
        ssm = gelu(yt) * sigmoid(glt + bg)
        return (jnp.concatenate([rms_fwd(at, ga), rms_fwd(ssm, gs)], axis=1),)
    mixed = rowwise("mix_norms", k6, L, [full(attn), full(y), full(gl)],
                    [small["b_glu"], small["g_attn_out"], small["g_ssm_out"]], [(d_attn + d_ssm, bf16)], tr=512)[0]
    big["w_o"] = get_w("w_o", mixed)
    mix = mm_nn("proj_out", mixed, big["w_o"])

    def k7(i, nt, xt, mt, gpm, gpf):
        h = xt + rms_fwd(mt, gpm)
        return h, rms_fwd(h, gpf)
    h, hn = rowwise("post_mix", k7, L, [full(x), full(mix)], [small["g_post_mix"], small["g_pre_ffn"]], [(D, f32), (D, bf16)], tr=512)
    big["w_gate"] = get_w("w_gate", hn)
    big["w_up"] = get_w("w_up", hn)
    hid_dg, hid_du, hid = ffn_hidden(hn, big["w_gate"], big["w_up"])
    d_ff_dim = hid.shape[1]
    big["w_down"] = get_w("w_down", hid)
    ff = mm_nn("ffn_down", hid, big["w_down"], tk=d_ff_dim // 2)

    def k9(i, nt, ht, fft, tt, g):
        out = ht + rms_fwd(fft, g)
        err = out - tt
        per_row = jnp.mean(err * err, axis=-1, keepdims=True)
        loss = 0.5 * jnp.sum(per_row) * jnp.where(_lane((1, LANES)) == 0, 1.0, 0.0)
        d_out = err * (1.0 / D)
        d_ff, dg = rms_bwd(fft, g, d_out)
        return d_out, d_ff, dg, loss
    d_out, d_ff, dg_post_ffn, loss = rowwise("loss_head", k9, L, [full(h), full(ff), full(tgt)], [small["g_post_ffn"]],
                                             [(D, f32), (D, bf16)], reds=[D, LANES])

    d_gt, d_up = ffn_hidden_grad(d_ff, big["w_down"], hid_dg, hid_du)
    token = put_g("w_down", hid, d_ff, None)
    d_hn = mm_nt_pair("d_hn", d_gt, big["w_gate"], d_up, big["w_up"], dep=token)
    token = put_g("w_gate", hn, d_gt, None)
    token = put_g("w_up", hn, d_up, token)

    def k11(i, nt, ht, da, do, mt, gpf, gpm):
        dh_n, dg_pf = rms_bwd(ht, gpf, da)
        dh = do + dh_n
        d_mix, dg_pm = rms_bwd(mt, gpm, dh)
        return dh, d_mix, dg_pf, dg_pm
    dh, d_mix, dg_pre_ffn, dg_post_mix = rowwise("post_mix_grad", k11, L, [full(h), full(d_hn), full(d_out), full(mix)],
                                                 [small["g_pre_ffn"], small["g_post_mix"]], [(D, f32), (D, bf16)], reds=[D, D], dep=token)
    d_mixed = mm_nt("d_mixed", d_mix, big["w_o"])
    token = put_g("w_o", mixed, d_mix, None)

    def k12(i, nt, at, yt, glt, da_n, ds_n, bg, ga, gs):
        z = gelu(yt)
        sg = sigmoid(glt + bg)
        ssm = z * sg
        d_at, dga = rms_bwd(at, ga, da_n)
        d_ssm_t, dgs = rms_bwd(ssm, gs, ds_n)
        d_gl = d_ssm_t * z * sg * (1.0 - sg)
        return d_at, d_ssm_t * sg, d_gl, dga, dgs, colsum(d_gl)
    d_attn_o, dz1, d_gl, dg_attn, dg_ssm, db_glu = rowwise(
        "mix_norms_grad", k12, L, [full(attn), full(y), full(gl), (d_mixed, d_attn, 0, 0), (d_mixed, d_ssm, d_attn // d_ssm, 0)],
        [small["b_glu"], small["g_attn_out"], small["g_ssm_out"]], [(d_attn, f32), (d_ssm, f32), (d_ssm, bf16)],
        reds=[d_attn, d_ssm, d_ssm], tr=512, dep=token)
    dz2 = mm_nt("d_glu_in", d_gl, big["w_glu"])
    token = put_g("w_glu", z_bf, d_gl, None)

    du, dbd_re, dbd_im, dcd_re, dcd_im, dlam_re_l, dlam_im_l, dd_skip = ssm_bwd(
        y, dz1, dz2, u_bf, proj, u_cb, s_re, s_im, bd_re, bd_im, cd_re, cd_im, lam_re_l, lam_im_l, small["d_skip"], dep=token)
    dq, dkk_c, dkk_p, dvv_c, dvv_p, dsink = attn_bwd(qr, kk, vv, sink_b, attn, d_attn_o)
    d_proj = qkv_grad(dq, dkk_c, dkk_p, dvv_c, dvv_p, du, pos, inv_freq)
    d_xn = mm_nt("d_xn", d_proj, big["w_in"])
    token = put_g("w_in", xn, d_proj, None)

    def k17(i, nt, xt, dxn, dht, g):
        dx, dg = rms_bwd(xt, g, dxn)
        return dht + dx, dg
    grad_x, dg_pre_mix = rowwise("pre_mix_grad", k17, L, [full(x), full(d_xn), full(dh)], [small["g_pre_mix"]],
                                 [(D, f32)], reds=[D], dep=token)

    gather = spread.T
    dbbt_re = dbd_re.reshape(G * P, LANES)[:, :N]
    dbbt_im = dbd_im.reshape(G * P, LANES)[:, :N]
    dc_re = dcd_re.reshape(G * P, LANES)[:, :N].reshape(G, P, N)
    dc_im = dcd_im.reshape(G * P, LANES)[:, :N].reshape(G, P, N)
    d_a_re, d_a_im, d_logdt, dbt_re, dbt_im = ssm_params_grad(
        a_re, a_im, logdt_b, bt_re, bt_im, spread, gather,
        _state_layout_inv(dlam_re_l, G, N), _state_layout_inv(dlam_im_l, G, N), dbbt_re, dbbt_im)
    q_per_kv = d_attn // HEAD_DIM // N_KV_HEADS
    small_grads = {
        "g_pre_mix": dg_pre_mix, "sinks": dsink[:, :q_per_kv, 0].reshape(1, -1),
        "a_re": d_a_re, "a_im": d_a_im, "log_dt": d_logdt.reshape(1, G),
        "b_re": dbt_re, "b_im": dbt_im,
        "c_re": dc_re, "c_im": dc_im,
        "d_skip": dd_skip, "b_glu": db_glu, "g_attn_out": dg_attn, "g_ssm_out": dg_ssm,
        "g_post_mix": dg_post_mix, "g_pre_ffn": dg_pre_ffn, "g_post_ffn": dg_post_ffn,
    }
    return loss, grad_x, small_grads


WEIGHTS = ['g_pre_mix', 'w_in', 'sinks', 'a_re', 'a_im', 'log_dt', 'b_re', 'b_im', 'c_re', 'c_im', 'd_skip', 'w_glu', 'b_glu',
           'g_attn_out', 'g_ssm_out', 'w_o', 'g_post_mix', 'g_pre_ffn', 'w_gate', 'w_up', 'w_down', 'g_post_ffn']
BIG = ['w_in', 'w_glu', 'w_o', 'w_gate', 'w_up', 'w_down']
COL_SHARDED = ['w_in', 'w_gate', 'w_up']
SMALL = [n for n in WEIGHTS if n not in BIG]
GATHER_GROUPS = [["w_in"], ["w_glu", "w_o"], ["w_gate", "w_up"], ["w_down"]]
REDUCE_GROUPS = [["w_down", "w_gate", "w_up"], ["w_o", "w_glu", "w_in"]]


PACK_ROWS = 256


def _pack(parts):
    flat = jnp.concatenate([p.reshape(-1) for p in parts])
    pad = (-flat.size) % (PACK_ROWS * LANES)
    return jnp.pad(flat, (0, pad)).reshape(-1, LANES)


TRANSPOSED_VIEW = ("b_re", "b_im")


def small_view(name, a):
    if name in TRANSPOSED_VIEW:
        a = a.transpose(0, 1, 3, 2)
    return a.reshape(-1, a.shape[-1])


def small_unview(name, p, shape):
    if name in TRANSPOSED_VIEW:
        return p.reshape(shape[0], shape[1], shape[3], shape[2]).transpose(0, 1, 3, 2)
    return p.reshape(shape)


def _unpack(packed, shapes):
    flat = packed.reshape(-1)
    out, off = [], 0
    for s in shapes:
        n = int(np.prod(s))
        out.append(flat[off:off + n].reshape(s))
        off += n
    return out


def kernel(x, positions, g_pre_mix, w_in, sinks, a_re, a_im, log_dt, b_re, b_im, c_re, c_im, d_skip, w_glu, b_glu, g_attn_out, g_ssm_out, w_o, g_post_mix, g_pre_ffn, w_gate, w_up, w_down, g_post_ffn, loss_target, m_g_pre_mix, m_w_in, m_sinks, m_a_re, m_a_im, m_log_dt, m_b_re, m_b_im, m_c_re, m_c_im, m_d_skip, m_w_glu, m_b_glu, m_g_attn_out, m_g_ssm_out, m_w_o, m_g_post_mix, m_g_pre_ffn, m_w_gate, m_w_up, m_w_down, m_g_post_ffn, v_g_pre_mix, v_w_in, v_sinks, v_a_re, v_a_im, v_log_dt, v_b_re, v_b_im, v_c_re, v_c_im, v_d_skip, v_w_glu, v_b_glu, v_g_attn_out, v_g_ssm_out, v_w_o, v_g_post_mix, v_g_pre_ffn, v_w_gate, v_w_up, v_w_down, v_g_post_ffn):
    args = dict(locals())
    w = {n: args[n] for n in WEIGHTS}
    m = {n: args["m_" + n] for n in WEIGHTS}
    v = {n: args["v_" + n] for n in WEIGHTS}
    L, D = x.shape[1], x.shape[2]

    ax, ay, ac = _place()
    mine_arr = (2 * ax + ay).astype(jnp.int32).reshape(1)
    c_arr = ac.astype(jnp.int32).reshape(1)

    me_arr = (4 * ax + 2 * ay + ac).astype(jnp.int32).reshape(1)

    bufs = {"w_in": into_slot("cast_w_in", w["w_in"][0], mine_arr, N_CHIPS, bf16)}
    (first_sems,), (bufs["w_in"],), token = gather_start("gather_start_in", [bufs["w_in"]], [[0]], mine_arr, neighbours_only=(0,))
    sems, relays, ready = {"w_in": first_sems}, {}, set()

    def relay(n, more, after):
        r_send, r_recv, more_sems, (bufs[n],), started, tok = gather_relay(
            "gather_relay_" + n, [bufs[n]], [sems[n]], [bufs[k] for k in more], after)
        sems.update({k: more_sems for k in more})
        bufs.update(zip(more, started))
        relays[n] = (r_send, r_recv)
        return tok

    for n in BIG[1:]:
        bufs[n] = into_slot("cast_" + n, w[n][0], mine_arr, N_CHIPS, bf16, dep=token)
    token = relay("w_in", [], bufs[BIG[-1]])
    first = ["w_gate", "w_glu", "w_o", "w_up"]
    (sems["w_gate"], sems["w_glu"], sems["w_up"]), started, token = gather_start(
        "gather_start_rest", [bufs[n] for n in first], [[0], [1, 2], [3]], token, neighbours_only=(0, 3))
    bufs.update(zip(first, started))

    def tick(name, after):
        return relay("w_gate", [], after) if name == "attn" else relay("w_up", ["w_down"], after)

    def get_w(n, after):
        if n not in ready:
            members = [g for g in GATHER_GROUPS if n in g][0]
            if members[0] in relays:
                landed = [gather_wait_relay("gather_wait_" + k, [bufs[k]], *relays[k], after)[0] for k in members]
                which = (2,)
            else:
                landed = gather_wait("gather_wait_" + members[0], [bufs[k] for k in members], *sems[members[0]], after)
                which = (0, 1, 2)
            bufs.update(zip(members, gather_forward("gather_forward_" + members[0], landed, which)))
            ready.update(members)
        g = bufs[n]
        return g if n in COL_SHARDED else g.reshape(g.shape[0] * g.shape[1], g.shape[2])

    swaps, operands, inflight = {}, {}, []
    other_arr = 1 - c_arr

    def put_g(n, a, b, dep):
        halves = (a.shape[1] // N_CHIPS // 2) % LANES == 0 if n not in COL_SHARDED else True
        if halves:
            operands[n] = (a, b)
            sent = mm_tn_half("d%s_sent" % n, a, b, n in COL_SHARDED, other_arr, dep=dep)
            swaps[n] = sibling_start("swap_start_" + n, [sent], False)
        else:
            g = weight_grad(n, a, b, dep)
            swaps[n] = sibling_start("swap_start_" + n, [g.reshape(N_CHIPS, g.shape[0] // N_CHIPS, g.shape[1])], True)
        for gi, members in enumerate(REDUCE_GROUPS):
            if n == members[-1]:
                last = swaps[n][4]
                pair = []
                for k in members:
                    send, recv, srcs, lands, _ = swaps[k]
                    (src,), (got,) = sibling_wait("swap_wait_" + k, srcs, lands, send, recv, k not in operands, last)
                    if k in operands:
                        pair.append(mm_tn_half("d%s_kept" % k, *operands[k], k in COL_SHARDED, c_arr, addend=got))
                    else:
                        pair.append(pair_sum("pair_sum_" + k, src, got, c_arr))
                send, recv, parts, lands, tok = scatter_start("scatter_start_%d" % gi, pair)
                inflight.append((members, send, recv, parts, lands))
                return tok
        return swaps[n][4]

    small = {n: w[n].reshape(1, -1) for n in SMALL}
    pos = positions.reshape(L, 1).astype(f32)
    d_in = N_CHIPS * w["w_in"].shape[2]
    loss, grad_x, small_grads = local_step(x[0], pos, loss_target[0], small, d_in, get_w, put_g, first_dep=token, tick=tick)

    shapes = [w[n].shape for n in SMALL]
    blocks = into_slot("small_block", _pack([small_grads[n] for n in SMALL] + [loss]), me_arr, N_DEV, f32)
    small_send, small_recv, blocks, after = everyone_start("small_start", blocks)

    grads, delta, new_m, new_v = {}, {}, {}, {}
    for gi, (members, send, recv, parts, lands) in enumerate(inflight):
        parts, landed = scatter_wait("scatter_wait_%d" % gi, parts, lands, send, recv, after)
        joins, dep = [], None
        for k, p, t in zip(members, parts, landed):
            joins.append(sibling_start("join_start_" + k, [chip_sum("chip_sum_" + k, p, t, mine_arr, dep=dep)], False))
            dep = after = joins[-1][4]
        for n, (send, recv, srcs, lands, _) in zip(members, joins):
            (own,), (sib,) = sibling_wait("join_wait_" + n, srcs, lands, send, recv, False, after)
            g_, d_, m_, v_ = adamw_halves("adamw_" + n, w[n][0], own, sib, m[n][0], v[n][0], c_arr)
            grads[n], delta[n], new_m[n], new_v[n] = g_[None], d_[None], m_[None], v_[None]
            after = v_
    blocks = everyone_wait("small_wait", blocks, small_send, small_recv, after)
    small_sum = sum_slots("small_sum", blocks)
    *small_g, loss = _unpack(small_sum, [small_view(n, w[n]).shape for n in SMALL] + [loss.shape])
    loss = loss[0, 0]
    outs = adamw_many("adamw_small", [small_view(n, w[n]) for n in SMALL], small_g,
                      [small_view(n, m[n]) for n in SMALL], [small_view(n, v[n]) for n in SMALL])
    for t, parts in zip((grads, delta, new_m, new_v), (small_g,) + tuple(outs)):
        t.update({n: small_unview(n, p, w[n].shape) for n, p in zip(SMALL, parts)})

    return (loss, grad_x[None], *[grads[n] for n in WEIGHTS], *[delta[n] for n in WEIGHTS],
            *[new_m[n] for n in WEIGHTS], *[new_v[n] for n in WEIGHTS])
```
